```python
import jax, jax.numpy as jnp
from jax import lax
import numpy as np

D_MODEL = 1024
BATCH = 32
SEQ = 2048
DEPTH = 1

MLA_HEADS = 8
MLA_NOPE = 64
MLA_ROPE = 32
MLA_V = 64
Q_LORA = 384
KV_LORA = 256
MLA_WIDTH = MLA_HEADS * MLA_V

SWA_HEADS = 8
SWA_KV_HEADS = 2
SWA_HEAD_DIM = 64
SWA_GROUP = SWA_HEADS // SWA_KV_HEADS
SWA_WIDTH = SWA_HEADS * SWA_HEAD_DIM
SWA_KV_WIDTH = SWA_KV_HEADS * SWA_HEAD_DIM
WINDOW = 128

D_MIX = MLA_WIDTH + SWA_WIDTH
Q_BLOCK = 128
ROPE_THETA = 10000.0
EPS = 1e-6
ALIBI_MAX_EXP = 8.0

IN_SPLITS = (Q_LORA, KV_LORA, MLA_ROPE, MLA_WIDTH,
             SWA_WIDTH, SWA_KV_WIDTH, SWA_KV_WIDTH, SWA_WIDTH)
D_IN = int(sum(IN_SPLITS))
SPLIT_IDX = [int(v) for v in np.cumsum(IN_SPLITS)[:-1]]

kernel_name = 'hymba_mla_swa_adaln_block'


def rmsnorm(t, gain):
    tf = t.astype(jnp.float32)
    y = tf * lax.rsqrt(jnp.mean(tf * tf, axis=-1, keepdims=True) + EPS)
    return (y * gain.astype(jnp.float32)).astype(t.dtype)


def rope_cos_sin(positions):
    inv = ROPE_THETA ** (-jnp.arange(0, MLA_ROPE, 2, dtype=jnp.float32) / MLA_ROPE)
    ang = positions.astype(jnp.float32)[..., None] * inv
    return jnp.cos(ang), jnp.sin(ang)


def apply_rope(t, cos, sin):
    tf = t.astype(jnp.float32)
    t1, t2 = jnp.split(tf, 2, axis=-1)
    return jnp.concatenate([t1 * cos - t2 * sin, t2 * cos + t1 * sin], axis=-1).astype(t.dtype)


def alibi_slopes(n_heads):
    h = jnp.arange(1, n_heads + 1, dtype=jnp.float32)
    return 2.0 ** (-ALIBI_MAX_EXP * h / n_heads)


def mla_attention(q_nope, q_pe, k_nope, k_pe, v):
    B, S = q_nope.shape[0], q_nope.shape[1]
    nb = S // Q_BLOCK
    scale = (MLA_NOPE + MLA_ROPE) ** -0.5
    key_idx = jnp.arange(S)

    def to_blocks(t):
        return jnp.moveaxis(t.reshape(B, nb, Q_BLOCK, *t.shape[2:]), 1, 0)

    def one_block(args):
        qn, qp, blk = args
        s = (jnp.einsum('bqhd,bshd->bhqs', qn, k_nope, preferred_element_type=jnp.float32)
             + jnp.einsum('bqhr,bsr->bhqs', qp, k_pe, preferred_element_type=jnp.float32)) * scale
        q_idx = blk * Q_BLOCK + jnp.arange(Q_BLOCK)
        s = jnp.where(key_idx[None, :] <= q_idx[:, None], s, -jnp.inf)
        p = jax.nn.softmax(s, axis=-1).astype(v.dtype)
        return jnp.einsum('bhqs,bshd->bqhd', p, v)

    o = lax.map(one_block, (to_blocks(q_nope), to_blocks(q_pe), jnp.arange(nb)))
    return jnp.moveaxis(o, 0, 1).reshape(B, S, MLA_WIDTH)


def swa_attention(q, k, v, positions, slopes, sinks):
    B, S = q.shape[0], q.shape[1]
    nb = S // WINDOW
    scale = SWA_HEAD_DIM ** -0.5

    def band(t):
        tb = t.reshape(B, nb, WINDOW, *t.shape[2:])
        prev = jnp.concatenate([jnp.zeros_like(tb[:, :1]), tb[:, :-1]], axis=1)
        return jnp.concatenate([prev, tb], axis=2)

    qb = q.reshape(B, nb, WINDOW, SWA_KV_HEADS, SWA_GROUP, SWA_HEAD_DIM)
    kb = band(k.reshape(B, S, SWA_KV_HEADS, SWA_HEAD_DIM))
    vb = band(v.reshape(B, S, SWA_KV_HEADS, SWA_HEAD_DIM))
    s = jnp.einsum('bnqkgd,bnskd->bnkgqs', qb, kb, preferred_element_type=jnp.float32) * scale
    pq = positions.reshape(B, nb, WINDOW)
    pk = band(positions)
    dist = (pq[..., :, None] - pk[..., None, :]).astype(jnp.float32)
    s = s - slopes.reshape(SWA_KV_HEADS, SWA_GROUP)[None, None, :, :, None, None] * dist[:, :, None, None]
    i = jnp.arange(WINDOW)[:, None]
    j = jnp.arange(2 * WINDOW)[None, :]
    rel = WINDOW + i - j
    blk = jnp.arange(nb)[:, None, None]
    valid = (rel >= 0) & (rel < WINDOW) & ((blk > 0) | (j >= WINDOW))
    s = jnp.where(valid[None, :, None, None], s, -jnp.inf)
    sink = jnp.broadcast_to(sinks.astype(jnp.float32).reshape(SWA_KV_HEADS, SWA_GROUP)[None, None, :, :, None, None],
                            s.shape[:-1] + (1,))
    p = jax.nn.softmax(jnp.concatenate([s, sink], axis=-1), axis=-1)[..., :-1].astype(v.dtype)
    o = jnp.einsum('bnkgqs,bnskd->bnqkgd', p, vb)
    return o.reshape(B, S, SWA_WIDTH)


def _fwd_setup_inputs(seed: int = 0) -> dict:
    key = jax.random.key(seed)
    ks = jax.random.split(key, 16)
    f32 = jnp.float32
    nrm = lambda k, shape, s: jax.random.normal(k, shape, f32) * s
    x = jax.random.normal(ks[0], (BATCH, SEQ, D_MODEL), f32)
    c = jax.random.normal(ks[1], (BATCH, D_MODEL), f32)
    offs = jax.random.randint(ks[2], (BATCH, 1), 0, 1024, dtype=jnp.int32)
    positions = offs + jnp.arange(SEQ, dtype=jnp.int32)[None, :]
    return {
        'x': x,
        'c': c,
        'positions': positions,
        'w_ada': nrm(ks[3], (DEPTH, D_MODEL, 3 * D_MODEL), D_MODEL ** -0.5),
        'b_ada': nrm(ks[4], (DEPTH, 3 * D_MODEL), 0.02),
        'norm_gain': 1.0 + nrm(ks[5], (DEPTH, D_MODEL), 0.02),
        'w_in': nrm(ks[6], (DEPTH, D_MODEL, D_IN), D_MODEL ** -0.5),
        'q_norm_gain': 1.0 + nrm(ks[7], (DEPTH, Q_LORA), 0.02),
        'kv_norm_gain': 1.0 + nrm(ks[8], (DEPTH, KV_LORA), 0.02),
        'w_uq': nrm(ks[9], (DEPTH, Q_LORA, MLA_HEADS * (MLA_NOPE + MLA_ROPE)), Q_LORA ** -0.5),
        'w_ukv': nrm(ks[10], (DEPTH, KV_LORA, MLA_HEADS * (MLA_NOPE + MLA_V)), KV_LORA ** -0.5),
        'swa_sinks': nrm(ks[11], (DEPTH, SWA_HEADS), 1.0),
        'w_out': nrm(ks[12], (DEPTH, D_MIX, D_MODEL), D_MIX ** -0.5),
        'final_gain': 1.0 + nrm(ks[13], (D_MODEL,), 0.02),
    }


def _fwd_reference(x, c, positions, w_ada, b_ada, norm_gain, w_in, q_norm_gain, kv_norm_gain,
              w_uq, w_ukv, swa_sinks, w_out, final_gain):
    B, S, _ = x.shape
    cos, sin = rope_cos_sin(positions)
    slopes = alibi_slopes(SWA_HEADS)
    c_act = jax.nn.silu(c)
    for l in range(DEPTH):
        mod = c_act @ w_ada[l] + b_ada[l]
        shift, scale, gate = jnp.split(mod, 3, axis=-1)
        h = rmsnorm(x, norm_gain[l]) * (1.0 + scale[:, None, :]) + shift[:, None, :]
        z = h @ w_in[l]
        zq, zkv, kr, g_mla, q_s, k_s, v_s, g_swa = jnp.split(z, SPLIT_IDX, axis=-1)
        q = (rmsnorm(zq, q_norm_gain[l]) @ w_uq[l]).reshape(B, S, MLA_HEADS, MLA_NOPE + MLA_ROPE)
        q_nope, q_pe = q[..., :MLA_NOPE], apply_rope(q[..., MLA_NOPE:], cos[:, :, None, :], sin[:, :, None, :])
        kv = (rmsnorm(zkv, kv_norm_gain[l]) @ w_ukv[l]).reshape(B, S, MLA_HEADS, MLA_NOPE + MLA_V)
        k_nope, v_mla = kv[..., :MLA_NOPE], kv[..., MLA_NOPE:]
        k_pe = apply_rope(kr, cos, sin)
        o_mla = mla_attention(q_nope, q_pe, k_nope, k_pe, v_mla).astype(x.dtype)
        o_swa = swa_attention(q_s, k_s, v_s, positions, slopes, swa_sinks[l]).astype(x.dtype)
        y = jnp.concatenate([o_mla * jax.nn.silu(g_mla), o_swa * jax.nn.silu(g_swa)], axis=-1) @ w_out[l]
        x = x + gate[:, None, :] * y
    return rmsnorm(x, final_gain)


import jax as _jax
import jax.numpy as _jnp

TWIN_FORMAT = 'train_step'
FWD_PARAMS = ['x', 'c', 'positions', 'w_ada', 'b_ada', 'norm_gain', 'w_in', 'q_norm_gain', 'kv_norm_gain', 'w_uq', 'w_ukv', 'swa_sinks', 'w_out', 'final_gain']
TWIN_WEIGHTS = ['w_ada', 'b_ada', 'norm_gain', 'w_in', 'q_norm_gain', 'kv_norm_gain', 'w_uq', 'w_ukv', 'swa_sinks', 'w_out', 'final_gain']
TWIN_DIFF_INPUT = 'x'
TWIN_INPUTS = ['x', 'c', 'positions', 'w_ada', 'b_ada', 'norm_gain', 'w_in', 'q_norm_gain', 'kv_norm_gain', 'w_uq', 'w_ukv', 'swa_sinks', 'w_out', 'final_gain', 'loss_target', 'm_w_ada', 'm_b_ada', 'm_norm_gain', 'm_w_in', 'm_q_norm_gain', 'm_kv_norm_gain', 'm_w_uq', 'm_w_ukv', 'm_swa_sinks', 'm_w_out', 'm_final_gain', 'v_w_ada', 'v_b_ada', 'v_norm_gain', 'v_w_in', 'v_q_norm_gain', 'v_kv_norm_gain', 'v_w_uq', 'v_w_ukv', 'v_swa_sinks', 'v_w_out', 'v_final_gain']
TWIN_OUTPUTS = ['loss', 'grad_x', 'grad_w_ada', 'grad_b_ada', 'grad_norm_gain', 'grad_w_in', 'grad_q_norm_gain', 'grad_kv_norm_gain', 'grad_w_uq', 'grad_w_ukv', 'grad_swa_sinks', 'grad_w_out', 'grad_final_gain', 'delta_w_ada', 'delta_b_ada', 'delta_norm_gain', 'delta_w_in', 'delta_q_norm_gain', 'delta_kv_norm_gain', 'delta_w_uq', 'delta_w_ukv', 'delta_swa_sinks', 'delta_w_out', 'delta_final_gain', 'new_m_w_ada', 'new_m_b_ada', 'new_m_norm_gain', 'new_m_w_in', 'new_m_q_norm_gain', 'new_m_kv_norm_gain', 'new_m_w_uq', 'new_m_w_ukv', 'new_m_swa_sinks', 'new_m_w_out', 'new_m_final_gain', 'new_v_w_ada', 'new_v_b_ada', 'new_v_norm_gain', 'new_v_w_in', 'new_v_q_norm_gain', 'new_v_kv_norm_gain', 'new_v_w_uq', 'new_v_w_ukv', 'new_v_swa_sinks', 'new_v_w_out', 'new_v_final_gain']
TWIN_LEAF_KINDS = {'loss': 'loss', 'grad_x': 'grad_x', 'grad_w_ada': 'grad_w', 'grad_b_ada': 'grad_w', 'grad_norm_gain': 'grad_w', 'grad_w_in': 'grad_w', 'grad_q_norm_gain': 'grad_w', 'grad_kv_norm_gain': 'grad_w', 'grad_w_uq': 'grad_w', 'grad_w_ukv': 'grad_w', 'grad_swa_sinks': 'grad_w', 'grad_w_out': 'grad_w', 'grad_final_gain': 'grad_w', 'delta_w_ada': 'delta_w', 'delta_b_ada': 'delta_w', 'delta_norm_gain': 'delta_w', 'delta_w_in': 'delta_w', 'delta_q_norm_gain': 'delta_w', 'delta_kv_norm_gain': 'delta_w', 'delta_w_uq': 'delta_w', 'delta_w_ukv': 'delta_w', 'delta_swa_sinks': 'delta_w', 'delta_w_out': 'delta_w', 'delta_final_gain': 'delta_w', 'new_m_w_ada': 'new_m', 'new_m_b_ada': 'new_m', 'new_m_norm_gain': 'new_m', 'new_m_w_in': 'new_m', 'new_m_q_norm_gain': 'new_m', 'new_m_kv_norm_gain': 'new_m', 'new_m_w_uq': 'new_m', 'new_m_w_ukv': 'new_m', 'new_m_swa_sinks': 'new_m', 'new_m_w_out': 'new_m', 'new_m_final_gain': 'new_m', 'new_v_w_ada': 'new_v', 'new_v_b_ada': 'new_v', 'new_v_norm_gain': 'new_v', 'new_v_w_in': 'new_v', 'new_v_q_norm_gain': 'new_v', 'new_v_kv_norm_gain': 'new_v', 'new_v_w_uq': 'new_v', 'new_v_w_ukv': 'new_v', 'new_v_swa_sinks': 'new_v', 'new_v_w_out': 'new_v', 'new_v_final_gain': 'new_v'}


def _forward(args):
    return _fwd_reference(*[args[k] for k in FWD_PARAMS])


def _output_shape():
    out = _jax.eval_shape(lambda: _forward(_fwd_setup_inputs(0)))
    return out.shape, out.dtype

N_MICROBATCH = 1
ADAM_LR = 0.001
ADAM_B1 = 0.9
ADAM_B2 = 0.999
ADAM_EPS = 1e-08
ADAM_WD = 0.01
ADAM_STEP = 10
PER_EXAMPLE_BATCH_AXIS = {'x': 0, 'c': 0, 'positions': 0, 'loss_target': 0}
SHARED_INPUTS = []
_WEIGHT_DTYPES = {'w_ada': _jnp.float32, 'b_ada': _jnp.float32, 'norm_gain': _jnp.float32, 'w_in': _jnp.float32, 'q_norm_gain': _jnp.float32, 'kv_norm_gain': _jnp.float32, 'w_uq': _jnp.float32, 'w_ukv': _jnp.float32, 'swa_sinks': _jnp.float32, 'w_out': _jnp.float32, 'final_gain': _jnp.float32}
MOMENT_SCALE = {'w_ada': 8.201775e-02, 'b_ada': 1.345800e-01, 'norm_gain': 1.040184e-01, 'w_in': 8.288441e-02, 'q_norm_gain': 2.350463e-02, 'kv_norm_gain': 9.633602e-02, 'w_uq': 1.666570e-02, 'w_ukv': 4.561222e-02, 'swa_sinks': 8.989145e-02, 'w_out': 7.790294e-02, 'final_gain': 6.408235e+01}


def _to_microbatches(a, axis):
    t = _jnp.moveaxis(a, axis, 0)
    t = t.reshape((N_MICROBATCH, t.shape[0] // N_MICROBATCH) + t.shape[1:])
    return _jnp.moveaxis(t, 1, axis + 1)


def setup_inputs(seed: int = 0) -> dict:
    inp = _fwd_setup_inputs(seed)
    key = _jax.random.fold_in(_jax.random.key(seed), 7919)
    shape, _ = _output_shape()
    out = dict(inp)
    out["loss_target"] = _jax.random.normal(_jax.random.fold_in(key, 0), shape, _jnp.float32)
    for i, name in enumerate(TWIN_WEIGHTS):
        w = inp[name].astype(_jnp.float32)
        if MOMENT_SCALE is None:
            s = _jnp.sqrt(_jnp.mean(_jnp.square(w)) + 1e-30)
        else:
            s = MOMENT_SCALE[name]
        km, kv = _jax.random.split(_jax.random.fold_in(key, i + 1))
        out[name] = w
        out["m_" + name] = s * _jax.random.normal(km, w.shape, _jnp.float32)
        out["v_" + name] = (s * s) * _jax.random.uniform(kv, w.shape, _jnp.float32, 0.5, 1.5)
    if N_MICROBATCH > 1:
        for name, axis in PER_EXAMPLE_BATCH_AXIS.items():
            out[name] = _to_microbatches(out[name], axis)
    return {'x': out['x'], 'c': out['c'], 'positions': out['positions'], 'w_ada': out['w_ada'], 'b_ada': out['b_ada'], 'norm_gain': out['norm_gain'], 'w_in': out['w_in'], 'q_norm_gain': out['q_norm_gain'], 'kv_norm_gain': out['kv_norm_gain'], 'w_uq': out['w_uq'], 'w_ukv': out['w_ukv'], 'swa_sinks': out['swa_sinks'], 'w_out': out['w_out'], 'final_gain': out['final_gain'], 'loss_target': out['loss_target'], 'm_w_ada': out['m_w_ada'], 'm_b_ada': out['m_b_ada'], 'm_norm_gain': out['m_norm_gain'], 'm_w_in': out['m_w_in'], 'm_q_norm_gain': out['m_q_norm_gain'], 'm_kv_norm_gain': out['m_kv_norm_gain'], 'm_w_uq': out['m_w_uq'], 'm_w_ukv': out['m_w_ukv'], 'm_swa_sinks': out['m_swa_sinks'], 'm_w_out': out['m_w_out'], 'm_final_gain': out['m_final_gain'], 'v_w_ada': out['v_w_ada'], 'v_b_ada': out['v_b_ada'], 'v_norm_gain': out['v_norm_gain'], 'v_w_in': out['v_w_in'], 'v_q_norm_gain': out['v_q_norm_gain'], 'v_kv_norm_gain': out['v_kv_norm_gain'], 'v_w_uq': out['v_w_uq'], 'v_w_ukv': out['v_w_ukv'], 'v_swa_sinks': out['v_swa_sinks'], 'v_w_out': out['v_w_out'], 'v_final_gain': out['v_final_gain']}


def _loss(weights, diff, rest, loss_target):
    with _jax.named_scope("forward"):
        args = {**rest, TWIN_DIFF_INPUT: diff, **{k: w.astype(_WEIGHT_DTYPES[k]) for k, w in weights.items()}}
        y = _forward(args)
    with _jax.named_scope("loss_head"):
        err = _jnp.square(y.astype(_jnp.float32) - loss_target)
        return 0.5 * _jnp.sum(_jnp.mean(err, axis=-1)) if err.ndim else 0.5 * err


def _adamw(w, g, m, v):
    m = ADAM_B1 * m + (1.0 - ADAM_B1) * g
    v = ADAM_B2 * v + (1.0 - ADAM_B2) * _jnp.square(g)
    m_hat = m / (1.0 - ADAM_B1 ** ADAM_STEP)
    v_hat = v / (1.0 - ADAM_B2 ** ADAM_STEP)
    delta = -ADAM_LR * (m_hat / (_jnp.sqrt(v_hat) + ADAM_EPS) + ADAM_WD * w)
    return delta, m, v


def reference(x, c, positions, w_ada, b_ada, norm_gain, w_in, q_norm_gain, kv_norm_gain, w_uq, w_ukv, swa_sinks, w_out, final_gain, loss_target, m_w_ada, m_b_ada, m_norm_gain, m_w_in, m_q_norm_gain, m_kv_norm_gain, m_w_uq, m_w_ukv, m_swa_sinks, m_w_out, m_final_gain, v_w_ada, v_b_ada, v_norm_gain, v_w_in, v_q_norm_gain, v_kv_norm_gain, v_w_uq, v_w_ukv, v_swa_sinks, v_w_out, v_final_gain):
    given = dict(x=x, c=c, positions=positions, w_ada=w_ada, b_ada=b_ada, norm_gain=norm_gain, w_in=w_in, q_norm_gain=q_norm_gain, kv_norm_gain=kv_norm_gain, w_uq=w_uq, w_ukv=w_ukv, swa_sinks=swa_sinks, w_out=w_out, final_gain=final_gain, loss_target=loss_target, m_w_ada=m_w_ada, m_b_ada=m_b_ada, m_norm_gain=m_norm_gain, m_w_in=m_w_in, m_q_norm_gain=m_q_norm_gain, m_kv_norm_gain=m_kv_norm_gain, m_w_uq=m_w_uq, m_w_ukv=m_w_ukv, m_swa_sinks=m_swa_sinks, m_w_out=m_w_out, m_final_gain=m_final_gain, v_w_ada=v_w_ada, v_b_ada=v_b_ada, v_norm_gain=v_norm_gain, v_w_in=v_w_in, v_q_norm_gain=v_q_norm_gain, v_kv_norm_gain=v_kv_norm_gain, v_w_uq=v_w_uq, v_w_ukv=v_w_ukv, v_swa_sinks=v_swa_sinks, v_w_out=v_w_out, v_final_gain=v_final_gain)
    weights = {n: given[n] for n in TWIN_WEIGHTS}
    shared = {n: given[n] for n in SHARED_INPUTS}
    per_example = {n: given[n] for n in ['x', 'c', 'positions']}
    grad_fn = _jax.value_and_grad(_loss, argnums=(0, 1))

    def one_microbatch(ex, loss_target):
        ex = dict(ex)
        diff = ex.pop(TWIN_DIFF_INPUT)
        return grad_fn(weights, diff, {**shared, **ex}, loss_target)

    if N_MICROBATCH == 1:
        loss, (grad_w, grad_x) = one_microbatch(per_example, given["loss_target"])
    else:
        def body(carry, xs):
            loss_sum, grad_sum = carry
            l_k, (gw_k, gx_k) = one_microbatch(xs[0], xs[1])
            with _jax.named_scope("update"):
                return (loss_sum + l_k, _jax.tree.map(_jnp.add, grad_sum, gw_k)), gx_k

        init = (_jnp.zeros((), _jnp.float32), _jax.tree.map(_jnp.zeros_like, weights))
        (loss, grad_w), grad_x = _jax.lax.scan(body, init, (per_example, given["loss_target"]))
    with _jax.named_scope("update"):
        delta_w, new_m, new_v = {}, {}, {}
        for n in TWIN_WEIGHTS:
            delta_w[n], new_m[n], new_v[n] = _adamw(weights[n], grad_w[n], given["m_" + n], given["v_" + n])
    return (loss, grad_x, *[grad_w[n] for n in TWIN_WEIGHTS], *[delta_w[n] for n in TWIN_WEIGHTS],
            *[new_m[n] for n in TWIN_WEIGHTS], *[new_v[n] for n in TWIN_WEIGHTS])
```

```python
import functools

import jax
import jax.numpy as jnp
from jax import lax
from jax.experimental import pallas as pl
from jax.experimental.pallas import tpu as pltpu

F32 = jnp.float32
BF16 = jnp.bfloat16

D_MODEL = 1024
Q_LORA = 384
KV_LORA = 256
N_HEADS = 8
MLA_NOPE = 64
MLA_ROPE = 32
HEAD_LANES = 128
HALF = 64
SWA_WINDOW = 128
EPS = 1e-6
ROPE_THETA = 10000.0
MLA_SCALE = (MLA_NOPE + MLA_ROPE) ** -0.5
SWA_SCALE = 64 ** -0.5
NEG = -1e30

ADAM_LR = 0.001
ADAM_B1 = 0.9
ADAM_B2 = 0.999
ADAM_EPS = 1e-08
ADAM_WD = 0.01
ADAM_STEP = 10

A_ZQ, A_ZKV, A_GM, A_QS, A_KD, A_VD, A_GS, A_END = 0, 384, 640, 1152, 1664, 1920, 2176, 2688
IN_SPLITS = (384, 256, 32, 512, 512, 128, 128, 512)
D_IN = sum(IN_SPLITS)

TOKEN_TILE = 256
ATT_TILE = 256
VMEM_LIMIT = 56 * 1024 * 1024

PACK_ROWS = 1008
PACK_HALF = 504
SHARD_SIZES = (1024 * 616, 384 * 192, 256 * 256, 256 * 1024)


def _dot(a, b):
    return jnp.dot(a, b, preferred_element_type=F32)


def _dot_nt(a, b):
    return lax.dot_general(a, b, (((1,), (1,)), ((), ())), preferred_element_type=F32)


def _dot_tn(a, b):
    return lax.dot_general(a, b, (((0,), (0,)), ((), ())), preferred_element_type=F32)


def _params(n_grid):
    return pltpu.CompilerParams(dimension_semantics=("arbitrary",) * n_grid, vmem_limit_bytes=VMEM_LIMIT)


def _full(shape):
    nd = len(shape)
    return pl.BlockSpec(shape, lambda *_: (0,) * nd)


def _sigmoid(g):
    return 1.0 / (1.0 + jnp.exp(-g))


def _exchange(name, arrays, masks, pick=None):
    n, m = len(arrays), len(masks)
    shapes = [a.shape if pick is None else a.shape[1:] for a in arrays]

    def body(*refs):
        ins, outs = refs[:n], refs[n:2 * n]
        send_sems, recv_sems = refs[2 * n], refs[2 * n + 1]
        x, y, c = lax.axis_index("x"), lax.axis_index("y"), lax.axis_index("c")
        copies = []
        for j, k in enumerate(masks):
            tx = 1 - x if k & 4 else x
            ty = 1 - y if k & 2 else y
            tc = 1 - c if k & 1 else c
            for i in range(n):
                src = ins[i] if pick is None else ins[i].at[pick(tx, ty, tc)]
                cp = pltpu.make_async_remote_copy(
                    src_ref=src, dst_ref=outs[i].at[j],
                    send_sem=send_sems.at[i * m + j], recv_sem=recv_sems.at[i * m + j],
                    device_id=(tx, ty, tc), device_id_type=pl.DeviceIdType.MESH)
                cp.start()
                copies.append(cp)
        for cp in copies:
            cp.wait()

    any_spec = pl.BlockSpec(memory_space=pl.ANY)
    return pl.pallas_call(
        body, name=name,
        out_shape=[jax.ShapeDtypeStruct((m,) + tuple(s), a.dtype) for s, a in zip(shapes, arrays)],
        in_specs=[any_spec] * n, out_specs=[any_spec] * n,
        scratch_shapes=[pltpu.SemaphoreType.DMA((n * m,)), pltpu.SemaphoreType.DMA((n * m,))],
    )(*arrays)


ALL_OTHERS = (1, 2, 3, 4, 5, 6, 7)
OTHER_CHIPS = (2, 4, 6)
SIBLING = (1,)


def _by_device(own, got, n_bits):
    x, y, c = lax.axis_index("x"), lax.axis_index("y"), lax.axis_index("c")
    if n_bits == 3:
        me, masks = 4 * x + 2 * y + c, ALL_OTHERS
    else:
        me, masks = 2 * x + y, (1, 2, 3)
    slots = jnp.concatenate([own[None], got], axis=0)
    order = jnp.bitwise_xor(jnp.arange(len(masks) + 1), me)
    return jnp.take(slots, order, axis=0)


def _ada_fwd_call(c_all, w_ada):
    def body(c_ref, w_ref, act_ref, o_ref):
        cv = c_ref[...]
        act = cv * _sigmoid(cv)
        act_ref[...] = act
        o_ref[...] = _dot(act.astype(BF16), w_ref[...].astype(BF16))

    nb = c_all.shape[0]
    return pl.pallas_call(
        body, name="ada_fwd", grid=(1,),
        out_shape=[jax.ShapeDtypeStruct((nb, D_MODEL), F32), jax.ShapeDtypeStruct((nb, w_ada.shape[1]), F32)],
        in_specs=[_full(c_all.shape), _full(w_ada.shape)],
        out_specs=[_full((nb, D_MODEL)), _full((nb, w_ada.shape[1]))],
        compiler_params=_params(1),
    )(c_all, w_ada)


def _rope_tables(pos_col, inv_row):
    ang = pos_col * inv_row
    return jnp.cos(ang), jnp.sin(ang)


def _pre_call(x, pos_col, mod, b_ada, ng, qg, kvg, inv128, wa, wkr2, wq2, wkv, seq):
    n_tok = x.shape[0]
    tm = min(TOKEN_TILE, seq)
    per_seq = seq // tm

    def body(x_ref, pos_ref, mod_ref, bada_ref, ng_ref, qg_ref, kvg_ref, inv_ref, wa_ref, wkr_ref, wq_ref, wkv_ref,
             zqkv_ref, gates_ref, qf_ref, kf_ref, v_ref, qs_ref, kd_ref, vd_ref):
        xv = x_ref[...]
        modv = mod_ref[0] + bada_ref[...]
        shift, scale = modv[:, :D_MODEL], modv[:, D_MODEL:2 * D_MODEL]
        r1 = lax.rsqrt(jnp.mean(xv * xv, axis=-1, keepdims=True) + EPS)
        h = ((xv * r1) * ng_ref[...]) * (1.0 + scale) + shift
        hb = h.astype(BF16)
        za = _dot(hb, wa_ref[...])
        zkr = _dot(hb, wkr_ref[...])
        cos, sin = _rope_tables(pos_ref[...], inv_ref[...])
        zqkv_ref[...] = za[:, :A_GM]
        gates_ref[:, :512] = za[:, A_GM:A_QS]
        gates_ref[:, 512:] = za[:, A_GS:A_END]
        qs_ref[...] = (za[:, A_QS:A_KD] * SWA_SCALE).astype(BF16)
        kd_ref[...] = za[:, A_KD:A_VD].astype(BF16)
        vd_ref[...] = za[:, A_VD:A_GS].astype(BF16)
        zq, zkv = za[:, A_ZQ:A_ZKV], za[:, A_ZKV:A_GM]
        rq = lax.rsqrt(jnp.mean(zq * zq, axis=-1, keepdims=True) + EPS)
        qn = ((zq * rq) * qg_ref[...]).astype(BF16)
        qr = _dot(qn, wq_ref[...])
        cf, sf = jnp.tile(cos, (1, N_HEADS)), jnp.tile(sin, (1, N_HEADS))
        qf_ref[...] = ((qr[:, :1024] * cf + qr[:, 1024:] * sf) * MLA_SCALE).astype(BF16)
        rkv = lax.rsqrt(jnp.mean(zkv * zkv, axis=-1, keepdims=True) + EPS)
        kvn = ((zkv * rkv) * kvg_ref[...]).astype(BF16)
        kv = _dot(kvn, wkv_ref[...])
        kpe = zkr[:, :128] * cos + zkr[:, 128:] * sin
        kf_ref[...] = (kv[:, :1024] + jnp.tile(kpe, (1, N_HEADS))).astype(BF16)
        v_ref[...] = kv[:, 1024:].astype(BF16)

    tok = lambda w: pl.BlockSpec((tm, w), lambda i: (i, 0))
    outs = [(640, F32), (1024, F32), (1024, BF16), (1024, BF16), (512, BF16), (512, BF16), (256, BF16), (256, BF16)]
    return pl.pallas_call(
        body, name="pre", grid=(n_tok // tm,),
        out_shape=[jax.ShapeDtypeStruct((n_tok, w), dt) for w, dt in outs],
        in_specs=[tok(D_MODEL), tok(1), pl.BlockSpec((1, 1, 3 * D_MODEL), lambda i: (i // per_seq, 0, 0)),
                  _full(b_ada.shape), _full(ng.shape), _full(qg.shape), _full(kvg.shape), _full(inv128.shape),
                  _full(wa.shape), _full(wkr2.shape), _full(wq2.shape), _full(wkv.shape)],
        out_specs=[tok(w) for w, _ in outs],
        compiler_params=_params(1),
    )(x, pos_col, mod, b_ada, ng, qg, kvg, inv128, wa, wkr2, wq2, wkv)


def _lane_lo(width=HEAD_LANES):
    return lax.broadcasted_iota(jnp.int32, (1, width), 1) < HALF


def _mla_fwd_call(qf, kf, v, n_seq, seq):
    tq = min(ATT_TILE, seq)
    nq = seq // tq

    def body(q_ref, k_ref, v_ref, o_ref, lse_ref):
        i = pl.program_id(2)
        lo = _lane_lo()
        q = q_ref[...]
        rows = i * tq + lax.broadcasted_iota(jnp.int32, (tq, 1), 0)

        def step(kt, carry):
            m0, l0, m1, l1, acc = carry
            start = pl.multiple_of(kt * tq, tq)
            k = k_ref[pl.ds(start, tq), :]
            vv = v_ref[pl.ds(start, tq), :]
            cols = kt * tq + lax.broadcasted_iota(jnp.int32, (1, tq), 1)
            keep = cols <= rows
            new = []
            pv = []
            for hh, (m_old, l_old) in enumerate(((m0, l0), (m1, l1))):
                sl = slice(hh * HEAD_LANES, (hh + 1) * HEAD_LANES)
                s = jnp.where(keep, _dot_nt(q[:, sl], k[:, sl]), NEG)
                m_new = jnp.maximum(m_old, jnp.max(s, axis=-1, keepdims=True))
                p = jnp.exp(s - m_new)
                alpha = jnp.exp(m_old - m_new)
                l_new = alpha * l_old + jnp.sum(p, axis=-1, keepdims=True)
                vh = jnp.where(lo if hh == 0 else jnp.logical_not(lo), vv, jnp.zeros_like(vv))
                pv.append(_dot(p.astype(BF16), vh))
                new.append((m_new, l_new, alpha))
            alpha_pair = jnp.where(lo, new[0][2], new[1][2])
            acc = acc * alpha_pair + pv[0] + pv[1]
            return new[0][0], new[0][1], new[1][0], new[1][1], acc

        col = lambda val: jnp.full((tq, 1), val, F32)
        init = (col(NEG), col(0.0), col(NEG), col(0.0), jnp.zeros((tq, HEAD_LANES), F32))
        m0, l0, m1, l1, acc = lax.fori_loop(0, i + 1, step, init)
        o_ref[...] = acc * jnp.where(lo, 1.0 / l0, 1.0 / l1)
        lse_ref[:, :HEAD_LANES] = jnp.broadcast_to(m0 + jnp.log(l0), (tq, HEAD_LANES))
        lse_ref[:, HEAD_LANES:] = jnp.broadcast_to(m1 + jnp.log(l1), (tq, HEAD_LANES))

    n_tok = qf.shape[0]
    return pl.pallas_call(
        body, name="mla_fwd", grid=(n_seq, N_HEADS // 2, nq),
        out_shape=[jax.ShapeDtypeStruct((n_tok, 512), F32), jax.ShapeDtypeStruct((n_tok, 1024), F32)],
        in_specs=[pl.BlockSpec((tq, 2 * HEAD_LANES), lambda b, hp, i: (b * nq + i, hp)),
                  pl.BlockSpec((seq, 2 * HEAD_LANES), lambda b, hp, i: (b, hp)),
                  pl.BlockSpec((seq, HEAD_LANES), lambda b, hp, i: (b, hp))],
        out_specs=[pl.BlockSpec((tq, HEAD_LANES), lambda b, hp, i: (b * nq + i, hp)),
                   pl.BlockSpec((tq, 2 * HEAD_LANES), lambda b, hp, i: (b * nq + i, hp))],
        compiler_params=_params(3),
    )(qf, kf, v)


def _mla_bwd_call(qf, kf, v, do, o, lse, n_seq, seq):
    tq = min(ATT_TILE, seq)
    nq = seq // tq

    def body(q_ref, k_ref, v_ref, do_ref, o_ref, lse_ref, dq_ref, dk_ref, dv_ref, delta_ref):
        lo = _lane_lo()
        hi = jnp.logical_not(lo)

        def delta_step(t, _):
            r = pl.ds(pl.multiple_of(t * tq, tq), tq)
            prod = do_ref[r, :].astype(F32) * o_ref[r, :]
            delta_ref[0, r, :] = jnp.sum(jnp.where(lo, prod, 0.0), axis=-1, keepdims=True)
            delta_ref[1, r, :] = jnp.sum(jnp.where(hi, prod, 0.0), axis=-1, keepdims=True)
            return 0

        lax.fori_loop(0, nq, delta_step, 0)
        dq_ref[...] = jnp.zeros_like(dq_ref)

        def k_step(kt, _):
            kr = pl.ds(pl.multiple_of(kt * tq, tq), tq)
            k = k_ref[kr, :]
            vv = v_ref[kr, :]
            cols = kt * tq + lax.broadcasted_iota(jnp.int32, (1, tq), 1)

            def q_step(qt, carry):
                dk0, dk1, dvv = carry
                qr = pl.ds(pl.multiple_of(qt * tq, tq), tq)
                q = q_ref[qr, :]
                dov = do_ref[qr, :]
                lse_v = lse_ref[qr, :]
                rows = qt * tq + lax.broadcasted_iota(jnp.int32, (tq, 1), 0)
                keep = cols <= rows
                dks = []
                for hh in range(2):
                    sl = slice(hh * HEAD_LANES, (hh + 1) * HEAD_LANES)
                    s = _dot_nt(q[:, sl], k[:, sl])
                    p = jnp.where(keep, jnp.exp(s - lse_v[:, hh * HEAD_LANES:hh * HEAD_LANES + 1]), 0.0)
                    doh = jnp.where(lo if hh == 0 else hi, dov, jnp.zeros_like(dov))
                    dvv = dvv + _dot_tn(p.astype(BF16), doh)
                    dp = _dot_nt(doh, vv)
                    ds = (p * (dp - delta_ref[hh, qr, :])).astype(BF16)
                    dq_ref[qr, sl] += _dot(ds, k[:, sl])
                    dks.append(_dot_tn(ds, q[:, sl]))
                return dk0 + dks[0], dk1 + dks[1], dvv

            z = jnp.zeros((tq, HEAD_LANES), F32)
            dk0, dk1, dvv = lax.fori_loop(kt, nq, q_step, (z, z, z))
            dk_ref[kr, :HEAD_LANES] = dk0
            dk_ref[kr, HEAD_LANES:] = dk1
            dv_ref[kr, :] = dvv
            return 0

        lax.fori_loop(0, nq, k_step, 0)

    n_tok = qf.shape[0]
    blk = lambda w: pl.BlockSpec((seq, w), lambda b, hp: (b, hp))
    return pl.pallas_call(
        body, name="mla_bwd", grid=(n_seq, N_HEADS // 2),
        out_shape=[jax.ShapeDtypeStruct((n_tok, 1024), F32), jax.ShapeDtypeStruct((n_tok, 1024), F32),
                   jax.ShapeDtypeStruct((n_tok, 512), F32)],
        in_specs=[blk(256), blk(256), blk(128), blk(128), blk(128), blk(256)],
        out_specs=[blk(256), blk(256), blk(128)],
        scratch_shapes=[pltpu.VMEM((2, seq, 1), F32)],
        compiler_params=_params(2),
    )(qf, kf, v, do, o, lse)


def _swa_block(n, pos_col_ref, pos_row_ref):
    w = SWA_WINDOW
    start = pl.multiple_of(jnp.maximum(n - 1, 0) * w, w)
    posq = pos_col_ref[...]
    posk = pos_row_ref[0, :, pl.ds(start, 2 * w)]
    dist = posq - posk
    rel = (n * w + lax.broadcasted_iota(jnp.int32, (w, 1), 0)) - (start + lax.broadcasted_iota(jnp.int32, (1, 2 * w), 1))
    valid = jnp.logical_and(rel >= 0, rel < w)
    return start, dist, valid


def _swa_fwd_call(qs, kd, vd, pos_col, pos_row, sinks, n_seq, seq):
    w = SWA_WINDOW
    nb = seq // w

    def body(q_ref, k_ref, v_ref, pc_ref, pr_ref, sink_ref, o_ref, lse_ref):
        n = pl.program_id(1)
        lo = _lane_lo()
        hi = jnp.logical_not(lo)
        start, dist, valid = _swa_block(n, pc_ref, pr_ref)
        for j in range(N_HEADS // 2):
            kvl = slice((j // 2) * HEAD_LANES, (j // 2 + 1) * HEAD_LANES)
            qp = q_ref[:, j * HEAD_LANES:(j + 1) * HEAD_LANES]
            kk = k_ref[pl.ds(start, 2 * w), kvl]
            vv = v_ref[pl.ds(start, 2 * w), kvl]
            o_pair = jnp.zeros((w, HEAD_LANES), F32)
            for hh in range(2):
                h = 2 * j + hh
                half = lo if hh == 0 else hi
                qh = jnp.where(half, qp, jnp.zeros_like(qp))
                s = _dot_nt(qh, kk) - (2.0 ** -(h + 1)) * dist
                s = jnp.where(valid, s, NEG)
                sink = sink_ref[0, h]
                m = jnp.maximum(jnp.max(s, axis=-1, keepdims=True), sink)
                p = jnp.exp(s - m)
                l = jnp.sum(p, axis=-1, keepdims=True) + jnp.exp(sink - m)
                pn = (p * (1.0 / l)).astype(BF16)
                o_pair = o_pair + _dot(pn, jnp.where(half, vv, jnp.zeros_like(vv)))
                lse_ref[:, h * HEAD_LANES:(h + 1) * HEAD_LANES] = jnp.broadcast_to(m + jnp.log(l), (w, HEAD_LANES))
            o_ref[:, j * HEAD_LANES:(j + 1) * HEAD_LANES] = o_pair

    n_tok = qs.shape[0]
    tok = lambda width: pl.BlockSpec((w, width), lambda b, n: (b * nb + n, 0))
    whole = lambda width: pl.BlockSpec((seq, width), lambda b, n: (b, 0))
    return pl.pallas_call(
        body, name="swa_fwd", grid=(n_seq, nb),
        out_shape=[jax.ShapeDtypeStruct((n_tok, 512), F32), jax.ShapeDtypeStruct((n_tok, 1024), F32)],
        in_specs=[tok(512), whole(256), whole(256), tok(1), pl.BlockSpec((1, 1, seq), lambda b, n: (b, 0, 0)),
                  pl.BlockSpec(memory_space=pltpu.SMEM)],
        out_specs=[tok(512), tok(1024)],
        compiler_params=_params(2),
    )(qs, kd, vd, pos_col, pos_row, sinks)


def _swa_bwd_call(qs, kd, vd, do, o, lse, pos_col, pos_row, sinks, n_seq, seq):
    w = SWA_WINDOW
    nb = seq // w

    def body(q_ref, k_ref, v_ref, do_ref, o_ref, lse_ref, pc_ref, pr_ref, sink_ref, dq_ref, dk_ref, dv_ref, dsink_ref):
        b, n = pl.program_id(0), pl.program_id(1)
        lo = _lane_lo()
        hi = jnp.logical_not(lo)

        @pl.when(n == 0)
        def _():
            dk_ref[...] = jnp.zeros_like(dk_ref)
            dv_ref[...] = jnp.zeros_like(dv_ref)

        @pl.when(jnp.logical_and(n == 0, b == 0))
        def _():
            dsink_ref[...] = jnp.zeros_like(dsink_ref)

        start, dist, valid = _swa_block(n, pc_ref, pr_ref)
        win = pl.ds(start, 2 * w)
        for j in range(N_HEADS // 2):
            pair = slice(j * HEAD_LANES, (j + 1) * HEAD_LANES)
            kvl = slice((j // 2) * HEAD_LANES, (j // 2 + 1) * HEAD_LANES)
            qp = q_ref[:, pair]
            dop = do_ref[:, pair]
            prod = dop.astype(F32) * o_ref[:, pair]
            kk = k_ref[win, kvl]
            vv = v_ref[win, kvl]
            dq_pair = jnp.zeros((w, HEAD_LANES), F32)
            dk_acc = jnp.zeros((2 * w, HEAD_LANES), F32)
            dv_acc = jnp.zeros((2 * w, HEAD_LANES), F32)
            for hh in range(2):
                h = 2 * j + hh
                half = lo if hh == 0 else hi
                qh = jnp.where(half, qp, jnp.zeros_like(qp))
                doh = jnp.where(half, dop, jnp.zeros_like(dop))
                delta = jnp.sum(jnp.where(half, prod, 0.0), axis=-1, keepdims=True)
                lse_h = lse_ref[:, h * HEAD_LANES:h * HEAD_LANES + 1]
                s = _dot_nt(qh, kk) - (2.0 ** -(h + 1)) * dist
                p = jnp.where(valid, jnp.exp(s - lse_h), 0.0)
                dv_acc = dv_acc + _dot_tn(p.astype(BF16), doh)
                dp = _dot_nt(doh, vv)
                ds = (p * (dp - delta)).astype(BF16)
                dq_pair = dq_pair + _dot(ds, jnp.where(half, kk, jnp.zeros_like(kk)))
                dk_acc = dk_acc + _dot_tn(ds, qh)
                p_sink = jnp.exp(sink_ref[0, h] - lse_h)
                dsink_ref[h:h + 1, :] += jnp.broadcast_to(-jnp.sum(p_sink * delta, axis=0, keepdims=True), (1, HEAD_LANES))
            dq_ref[:, pair] = dq_pair * SWA_SCALE
            dk_ref[win, kvl] += dk_acc
            dv_ref[win, kvl] += dv_acc

    n_tok = qs.shape[0]
    tok = lambda width: pl.BlockSpec((w, width), lambda b, n: (b * nb + n, 0))
    whole = lambda width: pl.BlockSpec((seq, width), lambda b, n: (b, 0))
    return pl.pallas_call(
        body, name="swa_bwd", grid=(n_seq, nb),
        out_shape=[jax.ShapeDtypeStruct((n_tok, 512), F32), jax.ShapeDtypeStruct((n_tok, 256), F32),
                   jax.ShapeDtypeStruct((n_tok, 256), F32), jax.ShapeDtypeStruct((N_HEADS, HEAD_LANES), F32)],
        in_specs=[tok(512), whole(256), whole(256), tok(512), tok(512), tok(1024), tok(1),
                  pl.BlockSpec((1, 1, seq), lambda b, n: (b, 0, 0)), pl.BlockSpec(memory_space=pltpu.SMEM)],
        out_specs=[tok(512), whole(256), whole(256), _full((N_HEADS, HEAD_LANES))],
        compiler_params=_params(2),
    )(qs, kd, vd, do, o, lse, pos_col, pos_row, sinks)


def _post_call(x, target, o_mla, o_swa, gates, mod, b_ada, fg, w_out, w_out_t, seq):
    n_tok = x.shape[0]
    tm = min(TOKEN_TILE, seq)
    per_seq = seq // tm
    n_seq = n_tok // seq

    def body(x_ref, t_ref, om_ref, os_ref, g_ref, mod_ref, bada_ref, fg_ref, w_ref, wt_ref,
             dx2_ref, do_ref, dg_ref, gw_ref, gfg_ref, dgate_ref, loss_ref):
        i = pl.program_id(0)

        @pl.when(i == 0)
        def _():
            gw_ref[...] = jnp.zeros_like(gw_ref)
            gfg_ref[...] = jnp.zeros_like(gfg_ref)
            loss_ref[...] = jnp.zeros_like(loss_ref)

        @pl.when(i % per_seq == 0)
        def _():
            dgate_ref[...] = jnp.zeros_like(dgate_ref)

        gate = mod_ref[0][:, 2 * D_MODEL:] + bada_ref[:, 2 * D_MODEL:]
        g = g_ref[...]
        o = jnp.concatenate([om_ref[...], os_ref[...]], axis=-1)
        sg = _sigmoid(g)
        sil = g * sg
        ypre = (o * sil).astype(BF16)
        y = _dot(ypre, w_ref[...])
        x2 = x_ref[...] + gate * y
        r2 = lax.rsqrt(jnp.mean(x2 * x2, axis=-1, keepdims=True) + EPS)
        xn2 = x2 * r2
        fgv = fg_ref[...]
        err = xn2 * fgv - t_ref[...]
        e2 = jnp.sum(err * err, axis=-1, keepdims=True)
        loss_ref[...] += jnp.broadcast_to(jnp.sum(e2, axis=0, keepdims=True) * (0.5 / D_MODEL), loss_ref.shape)
        dout = err * (1.0 / D_MODEL)
        gfg_ref[...] += jnp.sum(dout * xn2, axis=0, keepdims=True)
        dxn2 = dout * fgv
        dx2 = r2 * (dxn2 - xn2 * jnp.mean(dxn2 * xn2, axis=-1, keepdims=True))
        dx2_ref[...] = dx2
        dgate_ref[0] += jnp.sum(dx2 * y, axis=0, keepdims=True)
        dy = (dx2 * gate).astype(BF16)
        gw_ref[...] += _dot_tn(ypre, dy)
        dypre = _dot(dy, wt_ref[...])
        do_ref[...] = (dypre * sil).astype(BF16)
        dg_ref[...] = (dypre * o * (sg * (1.0 + g * (1.0 - sg)))).astype(BF16)

    tok = lambda w: pl.BlockSpec((tm, w), lambda i: (i, 0))
    per_b = pl.BlockSpec((1, 1, 3 * D_MODEL), lambda i: (i // per_seq, 0, 0))
    return pl.pallas_call(
        body, name="post", grid=(n_tok // tm,),
        out_shape=[jax.ShapeDtypeStruct((n_tok, D_MODEL), F32), jax.ShapeDtypeStruct((n_tok, D_MODEL), BF16),
                   jax.ShapeDtypeStruct((n_tok, D_MODEL), BF16), jax.ShapeDtypeStruct((D_MODEL, D_MODEL), F32),
                   jax.ShapeDtypeStruct((1, D_MODEL), F32), jax.ShapeDtypeStruct((n_seq, 1, D_MODEL), F32),
                   jax.ShapeDtypeStruct((1, HEAD_LANES), F32)],
        in_specs=[tok(D_MODEL), tok(D_MODEL), tok(512), tok(512), tok(D_MODEL), per_b, _full(b_ada.shape),
                  _full(fg.shape), _full(w_out.shape), _full(w_out_t.shape)],
        out_specs=[tok(D_MODEL), tok(D_MODEL), tok(D_MODEL), _full((D_MODEL, D_MODEL)), _full((1, D_MODEL)),
                   pl.BlockSpec((1, 1, D_MODEL), lambda i: (i // per_seq, 0, 0)), _full((1, HEAD_LANES))],
        compiler_params=_params(1),
    )(x, target, o_mla, o_swa, gates, mod, b_ada, fg, w_out, w_out_t)


def _mid_bwd_call(dqf, dkf, dv, zqkv, pos_col, qg, kvg, inv128, wq2, wkv, seq):
    n_tok = dqf.shape[0]
    tm = min(TOKEN_TILE, seq)

    def body(dq_ref, dk_ref, dv_ref, z_ref, pos_ref, qg_ref, kvg_ref, inv_ref, wq_ref, wkv_ref,
             dz_ref, dkr_ref, gwq_ref, gwkv_ref, gqg_ref, gkvg_ref):
        i = pl.program_id(0)

        @pl.when(i == 0)
        def _():
            gwq_ref[...] = jnp.zeros_like(gwq_ref)
            gwkv_ref[...] = jnp.zeros_like(gwkv_ref)
            gqg_ref[...] = jnp.zeros_like(gqg_ref)
            gkvg_ref[...] = jnp.zeros_like(gkvg_ref)

        cos, sin = _rope_tables(pos_ref[...], inv_ref[...])
        cf, sf = jnp.tile(cos, (1, N_HEADS)), jnp.tile(sin, (1, N_HEADS))
        dq = dq_ref[...] * MLA_SCALE
        dqr = jnp.concatenate([dq * cf, dq * sf], axis=-1).astype(BF16)
        zq, zkv = z_ref[:, :Q_LORA], z_ref[:, Q_LORA:]
        qgv, kvgv = qg_ref[...], kvg_ref[...]

        rq = lax.rsqrt(jnp.mean(zq * zq, axis=-1, keepdims=True) + EPS)
        xq = zq * rq
        gwq_ref[...] += _dot_tn((xq * qgv).astype(BF16), dqr)
        dqn = _dot_nt(dqr, wq_ref[...])
        gqg_ref[...] += jnp.sum(dqn * xq, axis=0, keepdims=True)
        dxq = dqn * qgv
        dz_ref[:, :Q_LORA] = (rq * (dxq - xq * jnp.mean(dxq * xq, axis=-1, keepdims=True))).astype(BF16)

        dk = dk_ref[...]
        dkv = jnp.concatenate([dk, dv_ref[...]], axis=-1).astype(BF16)
        rkv = lax.rsqrt(jnp.mean(zkv * zkv, axis=-1, keepdims=True) + EPS)
        xkv = zkv * rkv
        gwkv_ref[...] += _dot_tn((xkv * kvgv).astype(BF16), dkv)
        dkvn = _dot_nt(dkv, wkv_ref[...])
        gkvg_ref[...] += jnp.sum(dkvn * xkv, axis=0, keepdims=True)
        dxkv = dkvn * kvgv
        dz_ref[:, Q_LORA:] = (rkv * (dxkv - xkv * jnp.mean(dxkv * xkv, axis=-1, keepdims=True))).astype(BF16)

        dkpe = dk[:, :HEAD_LANES]
        for h in range(1, N_HEADS):
            dkpe = dkpe + dk[:, h * HEAD_LANES:(h + 1) * HEAD_LANES]
        dkr_ref[:, :HEAD_LANES] = (dkpe * cos).astype(BF16)
        dkr_ref[:, HEAD_LANES:] = (dkpe * sin).astype(BF16)

    tok = lambda w: pl.BlockSpec((tm, w), lambda i: (i, 0))
    return pl.pallas_call(
        body, name="mid_bwd", grid=(n_tok // tm,),
        out_shape=[jax.ShapeDtypeStruct((n_tok, 640), BF16), jax.ShapeDtypeStruct((n_tok, 256), BF16),
                   jax.ShapeDtypeStruct(wq2.shape, F32), jax.ShapeDtypeStruct(wkv.shape, F32),
                   jax.ShapeDtypeStruct((1, Q_LORA), F32), jax.ShapeDtypeStruct((1, KV_LORA), F32)],
        in_specs=[tok(1024), tok(1024), tok(512), tok(640), tok(1), _full(qg.shape), _full(kvg.shape),
                  _full(inv128.shape), _full(wq2.shape), _full(wkv.shape)],
        out_specs=[tok(640), tok(256), _full(wq2.shape), _full(wkv.shape), _full((1, Q_LORA)), _full((1, KV_LORA))],
        compiler_params=_params(1),
    )(dqf, dkf, dv, zqkv, pos_col, qg, kvg, inv128, wq2, wkv)


def _in_bwd_call(x, dx2, dz, dkr, dg, dqs, dkd, dvd, mod, b_ada, ng, wa_t, wkr2_t, seq):
    n_tok = x.shape[0]
    tm = min(TOKEN_TILE, seq)
    per_seq = seq // tm
    n_seq = n_tok // seq

    def body(x_ref, dx2_ref, dz_ref, dkr_ref, dg_ref, dqs_ref, dkd_ref, dvd_ref, mod_ref, bada_ref, ng_ref,
             wat_ref, wkrt_ref, gx_ref, gwa_ref, gwkr_ref, gng_ref, dshift_ref, dscale_ref):
        i = pl.program_id(0)

        @pl.when(i == 0)
        def _():
            gwa_ref[...] = jnp.zeros_like(gwa_ref)
            gwkr_ref[...] = jnp.zeros_like(gwkr_ref)
            gng_ref[...] = jnp.zeros_like(gng_ref)

        @pl.when(i % per_seq == 0)
        def _():
            dshift_ref[...] = jnp.zeros_like(dshift_ref)
            dscale_ref[...] = jnp.zeros_like(dscale_ref)

        xv = x_ref[...]
        modv = mod_ref[0] + bada_ref[...]
        shift, scale = modv[:, :D_MODEL], modv[:, D_MODEL:2 * D_MODEL]
        ngv = ng_ref[...]
        r1 = lax.rsqrt(jnp.mean(xv * xv, axis=-1, keepdims=True) + EPS)
        xn = xv * r1
        hb = ((xn * ngv) * (1.0 + scale) + shift).astype(BF16)

        dgv = dg_ref[...]
        pieces = [(A_ZQ, dz_ref[...]), (A_GM, dgv[:, :512]), (A_QS, dqs_ref[...].astype(BF16)),
                  (A_KD, dkd_ref[...].astype(BF16)), (A_VD, dvd_ref[...].astype(BF16)), (A_GS, dgv[:, 512:])]
        dkr = dkr_ref[...]
        gwkr_ref[...] += _dot_tn(hb, dkr)
        dh = _dot(dkr, wkrt_ref[...])
        for off, piece in pieces:
            wd = piece.shape[1]
            gwa_ref[:, off:off + wd] += _dot_tn(hb, piece)
            dh = dh + _dot(piece, wat_ref[off:off + wd, :])

        dshift_ref[0] += jnp.sum(dh, axis=0, keepdims=True)
        dscale_ref[0] += jnp.sum(dh * (xn * ngv), axis=0, keepdims=True)
        gng_ref[...] += jnp.sum(dh * xn * (1.0 + scale), axis=0, keepdims=True)
        dxn = dh * ngv * (1.0 + scale)
        gx_ref[...] = dx2_ref[...] + r1 * (dxn - xn * jnp.mean(dxn * xn, axis=-1, keepdims=True))

    tok = lambda w: pl.BlockSpec((tm, w), lambda i: (i, 0))
    per_b = lambda w: pl.BlockSpec((1, 1, w), lambda i: (i // per_seq, 0, 0))
    return pl.pallas_call(
        body, name="in_bwd", grid=(n_tok // tm,),
        out_shape=[jax.ShapeDtypeStruct((n_tok, D_MODEL), F32), jax.ShapeDtypeStruct((D_MODEL, A_END), F32),
                   jax.ShapeDtypeStruct((D_MODEL, 256), F32), jax.ShapeDtypeStruct((1, D_MODEL), F32),
                   jax.ShapeDtypeStruct((n_seq, 1, D_MODEL), F32), jax.ShapeDtypeStruct((n_seq, 1, D_MODEL), F32)],
        in_specs=[tok(D_MODEL), tok(D_MODEL), tok(640), tok(256), tok(D_MODEL), tok(512), tok(256), tok(256),
                  per_b(3 * D_MODEL), _full(b_ada.shape), _full(ng.shape), _full(wa_t.shape), _full(wkr2_t.shape)],
        out_specs=[tok(D_MODEL), _full((D_MODEL, A_END)), _full((D_MODEL, 256)), _full((1, D_MODEL)),
                   per_b(D_MODEL), per_b(D_MODEL)],
        compiler_params=_params(1),
    )(x, dx2, dz, dkr, dg, dqs, dkd, dvd, mod, b_ada, ng, wa_t, wkr2_t)


def _sum_call(name, stacked):
    n, rows, lanes = stacked.shape
    tr = PACK_HALF if rows % PACK_HALF == 0 else rows

    def body(s_ref, o_ref):
        acc = s_ref[0]
        for j in range(1, n):
            acc = acc + s_ref[j]
        o_ref[...] = acc

    return pl.pallas_call(
        body, name=name, grid=(rows // tr,),
        out_shape=jax.ShapeDtypeStruct((rows, lanes), F32),
        in_specs=[pl.BlockSpec((n, tr, lanes), lambda i: (0, i, 0))],
        out_specs=pl.BlockSpec((tr, lanes), lambda i: (i, 0)),
        compiler_params=_params(1),
    )(stacked)


def _adam_math(w, g, m, v):
    m_new = ADAM_B1 * m + (1.0 - ADAM_B1) * g
    v_new = ADAM_B2 * v + (1.0 - ADAM_B2) * (g * g)
    m_hat = m_new / (1.0 - ADAM_B1 ** ADAM_STEP)
    v_hat = v_new / (1.0 - ADAM_B2 ** ADAM_STEP)
    delta = -ADAM_LR * (m_hat / (jnp.sqrt(v_hat) + ADAM_EPS) + ADAM_WD * w)
    return delta, m_new, v_new


def _adam_call(name, w, g, m, v):
    rows, cols = w.shape
    tr = 256 if rows % 256 == 0 else rows

    def body(w_ref, g_ref, m_ref, v_ref, d_ref, mo_ref, vo_ref):
        d, mn, vn = _adam_math(w_ref[...], g_ref[...], m_ref[...], v_ref[...])
        d_ref[...] = d
        mo_ref[...] = mn
        vo_ref[...] = vn

    spec = pl.BlockSpec((tr, cols), lambda i: (i, 0))
    return pl.pallas_call(
        body, name=name, grid=(rows // tr,),
        out_shape=[jax.ShapeDtypeStruct(w.shape, F32)] * 3,
        in_specs=[spec] * 4, out_specs=[spec] * 3,
        compiler_params=_params(1),
    )(w, g, m, v)


def _ada_bwd_call(act_all, dmod_cols, w, m, v):
    rows, cols = w.shape
    tr = 256

    def body(a_ref, dm_ref, w_ref, m_ref, v_ref, g_ref, d_ref, mo_ref, vo_ref):
        g = _dot_tn(a_ref[...].astype(BF16), dm_ref[...].astype(BF16))
        d, mn, vn = _adam_math(w_ref[...], g, m_ref[...], v_ref[...])
        g_ref[...] = g
        d_ref[...] = d
        mo_ref[...] = mn
        vo_ref[...] = vn

    spec = pl.BlockSpec((tr, cols), lambda i: (i, 0))
    nb = act_all.shape[0]
    return pl.pallas_call(
        body, name="ada_bwd", grid=(rows // tr,),
        out_shape=[jax.ShapeDtypeStruct(w.shape, F32)] * 4,
        in_specs=[pl.BlockSpec((nb, tr), lambda i: (0, i)), _full(dmod_cols.shape), spec, spec, spec],
        out_specs=[spec] * 4,
        compiler_params=_params(1),
    )(act_all, dmod_cols, w, m, v)


def _small_call(parts, dmod_all, smalls, small_row_index):
    def body(p_ref, dm_ref, w_ref, m_ref, v_ref, g_ref, d_ref, mo_ref, vo_ref):
        r0 = small_row_index
        row = p_ref[0, r0:r0 + 1, :]
        for dv in range(1, 8):
            row = row + p_ref[dv, r0:r0 + 1, :]
        gb = dm_ref[0:1, :]
        for r in range(1, dm_ref.shape[0]):
            gb = gb + dm_ref[r:r + 1, :]
        g = jnp.concatenate([gb, row, jnp.zeros((6, 3 * D_MODEL), F32)], axis=0)
        d, mn, vn = _adam_math(w_ref[...], g, m_ref[...], v_ref[...])
        g_ref[...] = g
        d_ref[...] = d
        mo_ref[...] = mn
        vo_ref[...] = vn

    w, m, v = smalls
    return pl.pallas_call(
        body, name="small_update", grid=(1,),
        out_shape=[jax.ShapeDtypeStruct(w.shape, F32)] * 4,
        in_specs=[_full(parts.shape), _full(dmod_all.shape), _full(w.shape), _full(w.shape), _full(w.shape)],
        out_specs=[_full(w.shape)] * 4,
        compiler_params=_params(1),
    )(parts, dmod_all, w, m, v)


def _rot(t):
    half = t.shape[-1] // 2
    return jnp.concatenate([-t[..., half:], t[..., :half]], axis=-1)


def _rot_t(g):
    half = g.shape[-1] // 2
    return jnp.concatenate([g[..., half:], -g[..., :half]], axis=-1)


def _prepare_weights(w_in, w_uq, w_ukv):
    o = [0]
    for s in IN_SPLITS:
        o.append(o[-1] + s)
    ks, vs = w_in[:, o[5]:o[6]], w_in[:, o[6]:o[7]]
    dup = lambda t: jnp.concatenate([t[:, :64], t[:, :64], t[:, 64:], t[:, 64:]], axis=1)
    wa = jnp.concatenate([w_in[:, :o[2]], w_in[:, o[3]:o[5]], dup(ks), dup(vs), w_in[:, o[7]:]], axis=1)
    kr = w_in[:, o[2]:o[3]]
    zc = lambda n: jnp.zeros((w_in.shape[0], n), w_in.dtype)
    wkr2 = jnp.concatenate([zc(64), kr, zc(32), zc(64), _rot(kr), zc(32)], axis=1)
    uq = w_uq.reshape(Q_LORA, N_HEADS, MLA_NOPE + MLA_ROPE)
    zq = jnp.zeros((Q_LORA, N_HEADS, 32), w_uq.dtype)
    uq_full = jnp.concatenate([uq, zq], axis=-1).reshape(Q_LORA, 1024)
    uq_rot = jnp.concatenate([jnp.zeros((Q_LORA, N_HEADS, 64), w_uq.dtype), _rot(uq[..., MLA_NOPE:]), zq],
                             axis=-1).reshape(Q_LORA, 1024)
    wq2 = jnp.concatenate([uq_full, uq_rot], axis=1)
    ukv = w_ukv.reshape(KV_LORA, N_HEADS, 128)
    k_full = jnp.concatenate([ukv[..., :64], jnp.zeros((KV_LORA, N_HEADS, 64), w_ukv.dtype)], axis=-1).reshape(KV_LORA, 1024)
    wkv = jnp.concatenate([k_full, ukv[..., 64:].reshape(KV_LORA, 512)], axis=1)
    return wa, wkr2, wq2, wkv


def _restore_grads(gwa, gwkr2, gwq2, gwkv):
    fold = lambda g: jnp.concatenate([g[:, 0:64] + g[:, 64:128], g[:, 128:192] + g[:, 192:256]], axis=1)
    gkr = gwkr2[:, 64:96] + _rot_t(gwkr2[:, 192:224])
    g_in = jnp.concatenate([gwa[:, :A_GM], gkr, gwa[:, A_GM:A_KD], fold(gwa[:, A_KD:A_VD]), fold(gwa[:, A_VD:A_GS]),
                            gwa[:, A_GS:]], axis=1)
    gf = gwq2[:, :1024].reshape(Q_LORA, N_HEADS, 128)
    gr = gwq2[:, 1024:].reshape(Q_LORA, N_HEADS, 128)
    g_uq = jnp.concatenate([gf[..., :64], gf[..., 64:96] + _rot_t(gr[..., 64:96])], axis=-1).reshape(Q_LORA, 768)
    gk = gwkv[:, :1024].reshape(KV_LORA, N_HEADS, 128)[..., :64]
    gv = gwkv[:, 1024:].reshape(KV_LORA, N_HEADS, 64)
    g_ukv = jnp.concatenate([gk, gv], axis=-1).reshape(KV_LORA, 1024)
    return g_in, g_uq, g_ukv


def _local_step(x, positions, target, mod_rows, b_ada, ng, qg, kvg, sinks, fg, w_in_b, w_uq_b, w_ukv_b, w_out_b):
    n_seq, seq, _ = x.shape
    n_tok = n_seq * seq
    x2d = x.reshape(n_tok, D_MODEL)
    t2d = target.reshape(n_tok, D_MODEL)
    pos_f = positions.astype(F32)
    pos_col = pos_f.reshape(n_tok, 1)
    pos_row = pos_f.reshape(n_seq, 1, seq)
    mod3 = mod_rows.reshape(n_seq, 1, 3 * D_MODEL)
    inv = ROPE_THETA ** (-jnp.arange(0, MLA_ROPE, 2, dtype=F32) / MLA_ROPE)
    inv128 = jnp.concatenate([jnp.zeros((64,), F32), inv, inv, jnp.zeros((32,), F32)]).reshape(1, 128)
    fg2 = fg.reshape(1, D_MODEL)

    wa, wkr2, wq2, wkv = _prepare_weights(w_in_b, w_uq_b, w_ukv_b)

    zqkv, gates, qf, kf, v, qs, kd, vd = _pre_call(x2d, pos_col, mod3, b_ada, ng, qg, kvg, inv128, wa, wkr2, wq2, wkv, seq)
    o_mla, lse_mla = _mla_fwd_call(qf, kf, v, n_seq, seq)
    o_swa, lse_swa = _swa_fwd_call(qs, kd, vd, pos_col, pos_row, sinks, n_seq, seq)
    dx2, do, dg, g_out, g_fg, dgate, loss = _post_call(x2d, t2d, o_mla, o_swa, gates, mod3, b_ada, fg2, w_out_b, w_out_b.T, seq)
    do_mla, do_swa = do[:, :512], do[:, 512:]
    dqf, dkf, dv = _mla_bwd_call(qf, kf, v, do_mla, o_mla, lse_mla, n_seq, seq)
    dqs, dkd, dvd, dsink = _swa_bwd_call(qs, kd, vd, do_swa, o_swa, lse_swa, pos_col, pos_row, sinks, n_seq, seq)
    dz, dkr, g_wq2, g_wkv, g_qg, g_kvg = _mid_bwd_call(dqf, dkf, dv, zqkv, pos_col, qg, kvg, inv128, wq2, wkv, seq)
    gx, g_wa, g_wkr2, g_ng, dshift, dscale = _in_bwd_call(x2d, dx2, dz, dkr, dg, dqs, dkd, dvd, mod3, b_ada, ng,
                                                         wa.T, wkr2.T, seq)
    g_in, g_uq, g_ukv = _restore_grads(g_wa, g_wkr2, g_wq2, g_wkv)
    dmod = jnp.concatenate([dshift, dscale, dgate], axis=-1).reshape(n_seq, 3 * D_MODEL)
    small_row = jnp.concatenate([g_ng, g_fg, g_qg, g_kvg, jnp.pad(dsink[:, 0].reshape(1, N_HEADS), ((0, 0), (0, 120))),
                                 loss, jnp.zeros((1, 128), F32)], axis=1)
    return gx.reshape(x.shape), (g_in, g_uq, g_ukv, g_out), small_row, dmod


def kernel(x, c, positions, w_ada, b_ada, norm_gain, w_in, q_norm_gain, kv_norm_gain, w_uq, w_ukv, swa_sinks, w_out, final_gain, loss_target, m_w_ada, m_b_ada, m_norm_gain, m_w_in, m_q_norm_gain, m_kv_norm_gain, m_w_uq, m_w_ukv, m_swa_sinks, m_w_out, m_final_gain, v_w_ada, v_b_ada, v_norm_gain, v_w_in, v_q_norm_gain, v_kv_norm_gain, v_w_uq, v_w_ukv, v_swa_sinks, v_w_out, v_final_gain):
    n_seq = x.shape[0]
    xi, yi, ci = lax.axis_index("x"), lax.axis_index("y"), lax.axis_index("c")
    dev = 4 * xi + 2 * yi + ci
    chip = 2 * xi + yi

    (c_got,) = _exchange("gather_c", [c], ALL_OTHERS)
    c_all = _by_device(c, c_got, 3).reshape(8 * n_seq, D_MODEL)
    act_all, mod_piece = _ada_fwd_call(c_all, w_ada[0])
    (piece_got,) = _exchange("gather_mod", [mod_piece], OTHER_CHIPS)
    pieces = _by_device(mod_piece, piece_got, 2)
    mod_all = jnp.transpose(pieces, (1, 0, 2)).reshape(8 * n_seq, 3 * D_MODEL)
    mod_rows = lax.dynamic_slice_in_dim(mod_all, dev * n_seq, n_seq, axis=0)

    shards = [w_in[0].astype(BF16), w_uq[0].astype(BF16), w_ukv[0].astype(BF16), w_out[0].astype(BF16)]
    got = _exchange("gather_w", shards, OTHER_CHIPS)
    full = [_by_device(s, g, 2) for s, g in zip(shards, got)]
    cols = lambda t: jnp.transpose(t, (1, 0, 2)).reshape(t.shape[1], 4 * t.shape[2])
    w_in_b, w_uq_b, w_ukv_b = cols(full[0]), cols(full[1]), cols(full[2])
    w_out_b = full[3].reshape(D_MODEL, D_MODEL)

    gx, (g_in, g_uq, g_ukv, g_out), small_row, dmod = _local_step(
        x, positions, loss_target, mod_rows, b_ada, norm_gain, q_norm_gain, kv_norm_gain, swa_sinks, final_gain,
        w_in_b, w_uq_b, w_ukv_b, w_out_b)

    part = jnp.concatenate([dmod, small_row, jnp.zeros((8 - n_seq - 1, 3 * D_MODEL), F32)], axis=0)
    (part_got,) = _exchange("gather_small", [part], ALL_OTHERS)
    parts = _by_device(part, part_got, 3)
    dmod_all = parts[:, :n_seq, :].reshape(8 * n_seq, 3 * D_MODEL)

    rows_of = lambda g, n: jnp.transpose(g.reshape(g.shape[0], 4, n), (1, 0, 2)).reshape(4, -1)
    packed = jnp.concatenate([rows_of(g_in, 616), rows_of(g_uq, 192), rows_of(g_ukv, 256), g_out.reshape(4, -1)], axis=1)
    packed = jnp.transpose(packed.reshape(4, 2, PACK_HALF, 1024), (1, 0, 2, 3))
    (sib,) = _exchange("reduce_sibling", [packed], SIBLING, pick=lambda tx, ty, tc: tc)
    mine = lax.dynamic_index_in_dim(packed, ci, axis=0, keepdims=False)
    pair = jnp.stack([mine, sib[0]])
    chip_sum = _sum_call("sum_cores", pair.reshape(2, 4 * PACK_HALF, 1024)).reshape(4, PACK_HALF, 1024)
    (from_chips,) = _exchange("reduce_chips", [chip_sum], OTHER_CHIPS, pick=lambda tx, ty, tc: 2 * tx + ty)
    own = lax.dynamic_index_in_dim(chip_sum, chip, axis=0, keepdims=False)
    half_sum = _sum_call("sum_chips", _by_device(own, from_chips, 2))
    (other_half,) = _exchange("share_sibling", [half_sum], SIBLING)
    flat = jnp.where(ci == 0, jnp.concatenate([half_sum, other_half[0]]), jnp.concatenate([other_half[0], half_sum]))
    flat = flat.reshape(-1)
    offs = [0]
    for s in SHARD_SIZES:
        offs.append(offs[-1] + s)
    g_in_s = flat[offs[0]:offs[1]].reshape(1024, 616)
    g_uq_s = flat[offs[1]:offs[2]].reshape(384, 192)
    g_ukv_s = flat[offs[2]:offs[3]].reshape(256, 256)
    g_out_s = flat[offs[3]:offs[4]].reshape(256, 1024)

    d_in, nm_in, nv_in = _adam_call("adam_w_in", w_in[0], g_in_s, m_w_in[0], v_w_in[0])
    d_uq, nm_uq, nv_uq = _adam_call("adam_w_uq", w_uq[0], g_uq_s, m_w_uq[0], v_w_uq[0])
    d_ukv, nm_ukv, nv_ukv = _adam_call("adam_w_ukv", w_ukv[0], g_ukv_s, m_w_ukv[0], v_w_ukv[0])
    d_out, nm_out, nv_out = _adam_call("adam_w_out", w_out[0], g_out_s, m_w_out[0], v_w_out[0])
    dmod_cols = lax.dynamic_slice_in_dim(dmod_all, chip * 768, 768, axis=1)
    g_ada, d_ada, nm_ada, nv_ada = _ada_bwd_call(act_all, dmod_cols, w_ada[0], m_w_ada[0], v_w_ada[0])

    def small_pack(b, ngv, fgv, qgv, kvgv, sk):
        row1 = jnp.concatenate([ngv.reshape(1, -1), fgv.reshape(1, -1), qgv.reshape(1, -1), kvgv.reshape(1, -1),
                                jnp.pad(sk.reshape(1, -1), ((0, 0), (0, 120))), jnp.zeros((1, 256), F32)], axis=1)
        return jnp.concatenate([b.reshape(1, -1), row1, jnp.zeros((6, 3 * D_MODEL), F32)], axis=0)

    smalls = (small_pack(b_ada, norm_gain, final_gain, q_norm_gain, kv_norm_gain, swa_sinks),
              small_pack(m_b_ada, m_norm_gain, m_final_gain, m_q_norm_gain, m_kv_norm_gain, m_swa_sinks),
              small_pack(v_b_ada, v_norm_gain, v_final_gain, v_q_norm_gain, v_kv_norm_gain, v_swa_sinks))
    sg, sd, sm, sv = _small_call(parts, dmod_all, smalls, n_seq)

    def small_unpack(t):
        r = t[1]
        return (t[0].reshape(1, 3 * D_MODEL), r[0:1024].reshape(1, 1024), r[2048:2432].reshape(1, 384),
                r[2432:2688].reshape(1, 256), r[2688:2696].reshape(1, 8), r[1024:2048].reshape(1024))

    loss = sg[1, 2816]
    gb, gng, gqg, gkvg, gsk, gfg = small_unpack(sg)
    db, dng, dqg, dkvg, dsk, dfg = small_unpack(sd)
    mb, mng, mqg, mkvg, msk, mfg = small_unpack(sm)
    vb, vng, vqg, vkvg, vsk, vfg = small_unpack(sv)
    e = lambda t: t[None]
    return (loss, gx,
            e(g_ada), gb, gng, e(g_in_s), gqg, gkvg, e(g_uq_s), e(g_ukv_s), gsk, e(g_out_s), gfg,
            e(d_ada), db, dng, e(d_in), dqg, dkvg, e(d_uq), e(d_ukv), dsk, e(d_out), dfg,
            e(nm_ada), mb, mng, e(nm_in), mqg, mkvg, e(nm_uq), e(nm_ukv), msk, e(nm_out), mfg,
            e(nv_ada), vb, vng, e(nv_in), vqg, vkvg, e(nv_uq), e(nv_ukv), vsk, e(nv_out), vfg)
```

```python
import functools

import jax
import jax.numpy as jnp
from jax import lax
from jax.experimental import pallas as pl
from jax.experimental.pallas import tpu as pltpu

F32 = jnp.float32
BF16 = jnp.bfloat16

D_MODEL = 1024
Q_LORA = 384
KV_LORA = 256
N_HEADS = 8
MLA_NOPE = 64
MLA_ROPE = 32
HEAD_LANES = 128
HALF = 64
SWA_WINDOW = 128
EPS = 1e-6
ROPE_THETA = 10000.0
MLA_SCALE = (MLA_NOPE + MLA_ROPE) ** -0.5
SWA_SCALE = 64 ** -0.5
NEG = -1e30

ADAM_LR = 0.001
ADAM_B1 = 0.9
ADAM_B2 = 0.999
ADAM_EPS = 1e-08
ADAM_WD = 0.01
ADAM_STEP = 10

A_ZQ, A_ZKV, A_GM, A_QS, A_KD, A_VD, A_GS, A_END = 0, 384, 640, 1152, 1664, 1920, 2176, 2688
IN_SPLITS = (384, 256, 32, 512, 512, 128, 128, 512)
D_IN = sum(IN_SPLITS)

TOKEN_TILE = 256
ATT_TILE = 256
VMEM_LIMIT = 56 * 1024 * 1024

PACK_ROWS = 1008
PACK_HALF = 504
SHARD_SIZES = (1024 * 616, 384 * 192, 256 * 256, 256 * 1024)


def _dot(a, b):
    return jnp.dot(a, b, preferred_element_type=F32)


def _dot_nt(a, b):
    return lax.dot_general(a, b, (((1,), (1,)), ((), ())), preferred_element_type=F32)


def _dot_tn(a, b):
    return lax.dot_general(a, b, (((0,), (0,)), ((), ())), preferred_element_type=F32)


def _params(n_grid):
    return pltpu.CompilerParams(dimension_semantics=("arbitrary",) * n_grid, vmem_limit_bytes=VMEM_LIMIT)


def _full(shape):
    nd = len(shape)
    return pl.BlockSpec(shape, lambda *_: (0,) * nd)


def _sigmoid(g):
    return 1.0 / (1.0 + jnp.exp(-g))


def _exchange(name, arrays, masks, pick=None):
    n, m = len(arrays), len(masks)
    shapes = [a.shape if pick is None else a.shape[1:] for a in arrays]

    def body(*refs):
        ins, outs = refs[:n], refs[n:2 * n]
        send_sems, recv_sems = refs[2 * n], refs[2 * n + 1]
        x, y, c = lax.axis_index("x"), lax.axis_index("y"), lax.axis_index("c")
        copies = []
        for j, k in enumerate(masks):
            tx = 1 - x if k & 4 else x
            ty = 1 - y if k & 2 else y
            tc = 1 - c if k & 1 else c
            for i in range(n):
                src = ins[i] if pick is None else ins[i].at[pick(tx, ty, tc)]
                cp = pltpu.make_async_remote_copy(
                    src_ref=src, dst_ref=outs[i].at[j],
                    send_sem=send_sems.at[i * m + j], recv_sem=recv_sems.at[i * m + j],
                    device_id=(tx, ty, tc), device_id_type=pl.DeviceIdType.MESH)
                cp.start()
                copies.append(cp)
        for cp in copies:
            cp.wait()

    any_spec = pl.BlockSpec(memory_space=pl.ANY)
    return pl.pallas_call(
        body, name=name,
        out_shape=[jax.ShapeDtypeStruct((m,) + tuple(s), a.dtype) for s, a in zip(shapes, arrays)],
        in_specs=[any_spec] * n, out_specs=[any_spec] * n,
        scratch_shapes=[pltpu.SemaphoreType.DMA((n * m,)), pltpu.SemaphoreType.DMA((n * m,))],
    )(*arrays)


ALL_OTHERS = (1, 2, 3, 4, 5, 6, 7)
OTHER_CHIPS = (2, 4, 6)
SIBLING = (1,)


def _by_device(own, got, n_bits):
    x, y, c = lax.axis_index("x"), lax.axis_index("y"), lax.axis_index("c")
    if n_bits == 3:
        me, masks = 4 * x + 2 * y + c, ALL_OTHERS
    else:
        me, masks = 2 * x + y, (1, 2, 3)
    slots = jnp.concatenate([own[None], got], axis=0)
    order = jnp.bitwise_xor(jnp.arange(len(masks) + 1), me)
    return jnp.take(slots, order, axis=0)


def _ada_fwd_call(c_all, w_ada):
    def body(c_ref, w_ref, act_ref, o_ref):
        cv = c_ref[...]
        act = cv * _sigmoid(cv)
        act_ref[...] = act
        o_ref[...] = _dot(act.astype(BF16), w_ref[...].astype(BF16))

    nb = c_all.shape[0]
    return pl.pallas_call(
        body, name="ada_fwd", grid=(1,),
        out_shape=[jax.ShapeDtypeStruct((nb, D_MODEL), F32), jax.ShapeDtypeStruct((nb, w_ada.shape[1]), F32)],
        in_specs=[_full(c_all.shape), _full(w_ada.shape)],
        out_specs=[_full((nb, D_MODEL)), _full((nb, w_ada.shape[1]))],
        compiler_params=_params(1),
    )(c_all, w_ada)


def _rope_tables(pos_col, inv_row):
    ang = pos_col * inv_row
    return jnp.cos(ang), jnp.sin(ang)


def _pre_call(x, pos_col, mod, b_ada, ng, qg, kvg, inv128, wa, wkr2, wq2, wkv, seq):
    n_tok = x.shape[0]
    tm = min(TOKEN_TILE, seq)
    per_seq = seq // tm

    def body(x_ref, pos_ref, mod_ref, bada_ref, ng_ref, qg_ref, kvg_ref, inv_ref, wa_ref, wkr_ref, wq_ref, wkv_ref,
             zqkv_ref, gates_ref, qf_ref, kf_ref, v_ref, qs_ref, kd_ref, vd_ref):
        xv = x_ref[...]
        modv = mod_ref[0] + bada_ref[...]
        shift, scale = modv[:, :D_MODEL], modv[:, D_MODEL:2 * D_MODEL]
        r1 = lax.rsqrt(jnp.mean(xv * xv, axis=-1, keepdims=True) + EPS)
        h = ((xv * r1) * ng_ref[...]) * (1.0 + scale) + shift
        hb = h.astype(BF16)
        za = _dot(hb, wa_ref[...])
        zkr = _dot(hb, wkr_ref[...])
        cos, sin = _rope_tables(pos_ref[...], inv_ref[...])
        zqkv_ref[...] = za[:, :A_GM]
        gates_ref[:, :512] = za[:, A_GM:A_QS]
        gates_ref[:, 512:] = za[:, A_GS:A_END]
        qs_ref[...] = (za[:, A_QS:A_KD] * SWA_SCALE).astype(BF16)
        kd_ref[...] = za[:, A_KD:A_VD].astype(BF16)
        vd_ref[...] = za[:, A_VD:A_GS].astype(BF16)
        zq, zkv = za[:, A_ZQ:A_ZKV], za[:, A_ZKV:A_GM]
        rq = lax.rsqrt(jnp.mean(zq * zq, axis=-1, keepdims=True) + EPS)
        qn = ((zq * rq) * qg_ref[...]).astype(BF16)
        qr = _dot(qn, wq_ref[...])
        cf, sf = jnp.tile(cos, (1, N_HEADS)), jnp.tile(sin, (1, N_HEADS))
        qf_ref[...] = ((qr[:, :1024] * cf + qr[:, 1024:] * sf) * MLA_SCALE).astype(BF16)
        rkv = lax.rsqrt(jnp.mean(zkv * zkv, axis=-1, keepdims=True) + EPS)
        kvn = ((zkv * rkv) * kvg_ref[...]).astype(BF16)
        kv = _dot(kvn, wkv_ref[...])
        kpe = zkr[:, :128] * cos + zkr[:, 128:] * sin
        kf_ref[...] = (kv[:, :1024] + jnp.tile(kpe, (1, N_HEADS))).astype(BF16)
        v_ref[...] = kv[:, 1024:].astype(BF16)

    tok = lambda w: pl.BlockSpec((tm, w), lambda i: (i, 0))
    outs = [(640, F32), (1024, F32), (1024, BF16), (1024, BF16), (512, BF16), (512, BF16), (256, BF16), (256, BF16)]
    return pl.pallas_call(
        body, name="pre", grid=(n_tok // tm,),
        out_shape=[jax.ShapeDtypeStruct((n_tok, w), dt) for w, dt in outs],
        in_specs=[tok(D_MODEL), tok(1), pl.BlockSpec((1, 1, 3 * D_MODEL), lambda i: (i // per_seq, 0, 0)),
                  _full(b_ada.shape), _full(ng.shape), _full(qg.shape), _full(kvg.shape), _full(inv128.shape),
                  _full(wa.shape), _full(wkr2.shape), _full(wq2.shape), _full(wkv.shape)],
        out_specs=[tok(w) for w, _ in outs],
        compiler_params=_params(1),
    )(x, pos_col, mod, b_ada, ng, qg, kvg, inv128, wa, wkr2, wq2, wkv)


def _lane_lo(width=HEAD_LANES):
    return lax.broadcasted_iota(jnp.int32, (1, width), 1) < HALF


def _eye(n=HEAD_LANES):
    r = lax.broadcasted_iota(jnp.int32, (n, n), 0)
    c = lax.broadcasted_iota(jnp.int32, (n, n), 1)
    return jnp.where(r == c, 1.0, 0.0).astype(BF16)


def _mla_fwd_call(qf, kf, v, n_seq, seq):
    tq = min(ATT_TILE, seq)
    nq = seq // tq

    def body(q_ref, k_ref, v_ref, o_ref, lse_ref, vt_ref):
        i = pl.program_id(1)
        eye = _eye()

        @pl.when(i == 0)
        def _():
            for t in range(nq):
                for p in range(N_HEADS // 2):
                    pair = slice(p * HEAD_LANES, (p + 1) * HEAD_LANES)
                    vt_ref[pair, t * tq:(t + 1) * tq] = _dot_nt(eye, v_ref[t * tq:(t + 1) * tq, pair]).astype(BF16)

        q = q_ref[...]
        qcol = i * tq + lax.broadcasted_iota(jnp.int32, (1, tq), 1)
        heads = range(N_HEADS)
        lanes = [slice(h * HEAD_LANES, (h + 1) * HEAD_LANES) for h in heads]

        def make_step(masked):
            def step(kt, carry):
                start = pl.multiple_of(kt * tq, tq)
                k = k_ref[pl.ds(start, tq), :]
                vt = vt_ref[:, pl.ds(start, tq)]
                sts = [_dot_nt(k[:, lanes[h]], q[:, lanes[h]]) for h in heads]
                if masked:
                    keep = (kt * tq + lax.broadcasted_iota(jnp.int32, (tq, 1), 0)) <= qcol
                    sts = [jnp.where(keep, st, NEG) for st in sts]
                stats, pts = [], []
                for h in heads:
                    m_old, l_old = carry[3 * h], carry[3 * h + 1]
                    m_new = jnp.maximum(m_old, jnp.max(sts[h], axis=0, keepdims=True))
                    pt = jnp.exp(sts[h] - m_new)
                    alpha = jnp.exp(m_old - m_new)
                    stats.append((m_new, alpha * l_old + jnp.sum(pt, axis=0, keepdims=True), alpha))
                    pts.append(pt.astype(BF16))
                pvs = [_dot(vt[h * HALF:(h + 1) * HALF, :], pts[h]) for h in heads]
                out = []
                for h in heads:
                    out += [stats[h][0], stats[h][1], carry[3 * h + 2] * stats[h][2] + pvs[h]]
                return tuple(out)
            return step

        row = lambda val: jnp.full((1, tq), val, F32)
        init = (row(NEG), row(0.0), jnp.zeros((HALF, tq), F32)) * N_HEADS
        carry = lax.fori_loop(0, i, make_step(False), init)
        carry = make_step(True)(i, carry)
        acc_t = jnp.concatenate([carry[3 * h + 2] * (1.0 / carry[3 * h + 1]) for h in heads], axis=0)
        o_ref[...] = acc_t.T
        for h in heads:
            lse_ref[0, h // 4, h % 4:h % 4 + 1, :] = carry[3 * h] + jnp.log(carry[3 * h + 1])

    n_tok = qf.shape[0]
    return pl.pallas_call(
        body, name="mla_fwd", grid=(n_seq, nq),
        out_shape=[jax.ShapeDtypeStruct((n_tok, 512), F32), jax.ShapeDtypeStruct((n_seq, 2, 4, seq), F32)],
        in_specs=[pl.BlockSpec((tq, 1024), lambda b, i: (b * nq + i, 0)),
                  pl.BlockSpec((seq, 1024), lambda b, i: (b, 0)),
                  pl.BlockSpec((seq, 512), lambda b, i: (b, 0))],
        out_specs=[pl.BlockSpec((tq, 512), lambda b, i: (b * nq + i, 0)),
                   pl.BlockSpec((1, 2, 4, tq), lambda b, i: (b, 0, 0, i))],
        scratch_shapes=[pltpu.VMEM((512, seq), BF16)],
        compiler_params=_params(2),
    )(qf, kf, v)


def _mla_bwd_call(qf, kf, v, do, o, lse, n_seq, seq):
    tq = min(ATT_TILE, seq)
    nq = seq // tq

    nh = 4
    heads = range(nh)
    lanes = [slice(h * HEAD_LANES, (h + 1) * HEAD_LANES) for h in heads]

    def body(q_ref, k_ref, v_ref, do_ref, o_ref, lse_ref, dq_ref, dk_ref, dv_ref,
             kt_ref, dot_ref, delta_ref, dqt_ref):
        eye = _eye()
        lo = _lane_lo()
        sub_lo = lax.broadcasted_iota(jnp.int32, (HEAD_LANES, 1), 0) < HALF
        ones_lo = jnp.where(jnp.broadcast_to(lo, (8, HEAD_LANES)), 1.0, 0.0).astype(BF16)
        ones_hi = jnp.where(jnp.broadcast_to(lo, (8, HEAD_LANES)), 0.0, 1.0).astype(BF16)

        for t in range(nq):
            r = slice(t * tq, (t + 1) * tq)
            kv = k_ref[r, :]
            for h in heads:
                kt_ref[lanes[h], r] = _dot_nt(eye, kv[:, lanes[h]]).astype(BF16)
            for p in range(nh // 2):
                dov = do_ref[r, lanes[p]]
                dt = _dot_nt(eye, dov)
                dot_ref[2 * p, :, r] = jnp.where(sub_lo, dt, 0.0).astype(BF16)
                dot_ref[2 * p + 1, :, r] = jnp.where(sub_lo, 0.0, dt).astype(BF16)
                prod = dov.astype(F32) * o_ref[r, lanes[p]]
                p_hi = prod.astype(BF16)
                p_lo = (prod - p_hi.astype(F32)).astype(BF16)
                delta_ref[2 * p, :, r] = _dot_nt(ones_lo, p_hi) + _dot_nt(ones_lo, p_lo)
                delta_ref[2 * p + 1, :, r] = _dot_nt(ones_hi, p_hi) + _dot_nt(ones_hi, p_lo)
        dqt_ref[...] = jnp.zeros_like(dqt_ref)

        def k_step(kt, _):
            kr = pl.ds(pl.multiple_of(kt * tq, tq), tq)
            k = k_ref[kr, :]
            vv = v_ref[kr, :]
            k_t = kt_ref[:, kr]
            krow = kt * tq + lax.broadcasted_iota(jnp.int32, (tq, 1), 0)

            def make_step(masked):
                def q_step(qt, carry):
                    qr = pl.ds(pl.multiple_of(qt * tq, tq), tq)
                    q = q_ref[qr, :]
                    do_ts = [dot_ref[h, :, qr] for h in heads]
                    sts = [_dot_nt(k[:, lanes[h]], q[:, lanes[h]]) for h in heads]
                    dpts = [_dot(vv[:, lanes[h // 2]], do_ts[h]) for h in heads]
                    if masked:
                        keep = krow <= (qt * tq + lax.broadcasted_iota(jnp.int32, (1, tq), 1))
                    pts, dsts = [], []
                    for h in heads:
                        pt = jnp.exp(sts[h] - lse_ref[0, 0, h:h + 1, qr])
                        if masked:
                            pt = jnp.where(keep, pt, 0.0)
                        dsts.append((pt * (dpts[h] - delta_ref[h, 0:1, qr])).astype(BF16))
                        pts.append(pt.astype(BF16))
                    out = []
                    for h in heads:
                        hh = h % 2
                        dvt = _dot_nt(do_ts[h][hh * HALF:(hh + 1) * HALF, :], pts[h])
                        dk = _dot(dsts[h], q[:, lanes[h]])
                        dqt_ref[lanes[h], qr] += _dot(k_t[lanes[h], :], dsts[h])
                        out += [carry[2 * h] + dk, carry[2 * h + 1] + dvt]
                    return tuple(out)
                return q_step

            init = (jnp.zeros((tq, HEAD_LANES), F32), jnp.zeros((HALF, tq), F32)) * nh
            carry = make_step(True)(kt, init)
            carry = lax.fori_loop(kt + 1, nq, make_step(False), carry)
            for h in heads:
                dk_ref[kr, lanes[h]] = carry[2 * h]
            for p in range(nh // 2):
                dv_ref[kr, lanes[p]] = jnp.concatenate([carry[4 * p + 1], carry[4 * p + 3]], axis=0).T
            return 0

        lax.fori_loop(0, nq, k_step, 0)
        for t in range(nq):
            r = slice(t * tq, (t + 1) * tq)
            for h in heads:
                dq_ref[r, lanes[h]] = dqt_ref[lanes[h], r].T

    n_tok = qf.shape[0]
    groups = N_HEADS // nh
    blk = lambda w: pl.BlockSpec((seq, w), lambda b, g: (b, g))
    return pl.pallas_call(
        body, name="mla_bwd", grid=(n_seq, groups),
        out_shape=[jax.ShapeDtypeStruct((n_tok, 1024), F32), jax.ShapeDtypeStruct((n_tok, 1024), F32),
                   jax.ShapeDtypeStruct((n_tok, 512), F32)],
        in_specs=[blk(512), blk(512), blk(256), blk(256), blk(256),
                  pl.BlockSpec((1, 1, nh, seq), lambda b, g: (b, g, 0, 0))],
        out_specs=[blk(512), blk(512), blk(256)],
        scratch_shapes=[pltpu.VMEM((nh * HEAD_LANES, seq), BF16), pltpu.VMEM((nh, HEAD_LANES, seq), BF16),
                        pltpu.VMEM((nh, 8, seq), F32), pltpu.VMEM((nh * HEAD_LANES, seq), F32)],
        compiler_params=_params(2),
    )(qf, kf, v, do, o, lse)


def _swa_block(n, pos_col_ref, pos_row_ref):
    w = SWA_WINDOW
    start = pl.multiple_of(jnp.maximum(n - 1, 0) * w, w)
    posq = pos_col_ref[...]
    posk = pos_row_ref[0, :, pl.ds(start, 2 * w)]
    dist = posq - posk
    rel = (n * w + lax.broadcasted_iota(jnp.int32, (w, 1), 0)) - (start + lax.broadcasted_iota(jnp.int32, (1, 2 * w), 1))
    valid = jnp.logical_and(rel >= 0, rel < w)
    return start, dist, valid


def _swa_fwd_call(qs, kd, vd, pos_col, pos_row, sinks, n_seq, seq):
    w = SWA_WINDOW
    nb = seq // w

    def body(q_ref, k_ref, v_ref, pc_ref, pr_ref, sink_ref, o_ref, lse_ref):
        n = pl.program_id(1)
        lo = _lane_lo()
        hi = jnp.logical_not(lo)
        start, dist, valid = _swa_block(n, pc_ref, pr_ref)
        for j in range(N_HEADS // 2):
            kvl = slice((j // 2) * HEAD_LANES, (j // 2 + 1) * HEAD_LANES)
            qp = q_ref[:, j * HEAD_LANES:(j + 1) * HEAD_LANES]
            kk = k_ref[pl.ds(start, 2 * w), kvl]
            vv = v_ref[pl.ds(start, 2 * w), kvl]
            o_pair = jnp.zeros((w, HEAD_LANES), F32)
            for hh in range(2):
                h = 2 * j + hh
                half = lo if hh == 0 else hi
                qh = jnp.where(half, qp, jnp.zeros_like(qp))
                s = _dot_nt(qh, kk) - (2.0 ** -(h + 1)) * dist
                s = jnp.where(valid, s, NEG)
                sink = sink_ref[0, h]
                m = jnp.maximum(jnp.max(s, axis=-1, keepdims=True), sink)
                p = jnp.exp(s - m)
                l = jnp.sum(p, axis=-1, keepdims=True) + jnp.exp(sink - m)
                pn = (p * (1.0 / l)).astype(BF16)
                o_pair = o_pair + _dot(pn, jnp.where(half, vv, jnp.zeros_like(vv)))
                lse_ref[:, h * HEAD_LANES:(h + 1) * HEAD_LANES] = jnp.broadcast_to(m + jnp.log(l), (w, HEAD_LANES))
            o_ref[:, j * HEAD_LANES:(j + 1) * HEAD_LANES] = o_pair

    n_tok = qs.shape[0]
    tok = lambda width: pl.BlockSpec((w, width), lambda b, n: (b * nb + n, 0))
    whole = lambda width: pl.BlockSpec((seq, width), lambda b, n: (b, 0))
    return pl.pallas_call(
        body, name="swa_fwd", grid=(n_seq, nb),
        out_shape=[jax.ShapeDtypeStruct((n_tok, 512), F32), jax.ShapeDtypeStruct((n_tok, 1024), F32)],
        in_specs=[tok(512), whole(256), whole(256), tok(1), pl.BlockSpec((1, 1, seq), lambda b, n: (b, 0, 0)),
                  pl.BlockSpec(memory_space=pltpu.SMEM)],
        out_specs=[tok(512), tok(1024)],
        compiler_params=_params(2),
    )(qs, kd, vd, pos_col, pos_row, sinks)


def _swa_bwd_call(qs, kd, vd, do, o, lse, pos_col, pos_row, sinks, n_seq, seq):
    w = SWA_WINDOW
    nb = seq // w

    def body(q_ref, k_ref, v_ref, do_ref, o_ref, lse_ref, pc_ref, pr_ref, sink_ref, dq_ref, dk_ref, dv_ref, dsink_ref):
        b, n = pl.program_id(0), pl.program_id(1)
        lo = _lane_lo()
        hi = jnp.logical_not(lo)

        @pl.when(n == 0)
        def _():
            dk_ref[...] = jnp.zeros_like(dk_ref)
            dv_ref[...] = jnp.zeros_like(dv_ref)

        @pl.when(jnp.logical_and(n == 0, b == 0))
        def _():
            dsink_ref[...] = jnp.zeros_like(dsink_ref)

        start, dist, valid = _swa_block(n, pc_ref, pr_ref)
        win = pl.ds(start, 2 * w)
        for j in range(N_HEADS // 2):
            pair = slice(j * HEAD_LANES, (j + 1) * HEAD_LANES)
            kvl = slice((j // 2) * HEAD_LANES, (j // 2 + 1) * HEAD_LANES)
            qp = q_ref[:, pair]
            dop = do_ref[:, pair]
            prod = dop.astype(F32) * o_ref[:, pair]
            kk = k_ref[win, kvl]
            vv = v_ref[win, kvl]
            dq_pair = jnp.zeros((w, HEAD_LANES), F32)
            dk_acc = jnp.zeros((2 * w, HEAD_LANES), F32)
            dv_acc = jnp.zeros((2 * w, HEAD_LANES), F32)
            for hh in range(2):
                h = 2 * j + hh
                half = lo if hh == 0 else hi
                qh = jnp.where(half, qp, jnp.zeros_like(qp))
                doh = jnp.where(half, dop, jnp.zeros_like(dop))
                delta = jnp.sum(jnp.where(half, prod, 0.0), axis=-1, keepdims=True)
                lse_h = lse_ref[:, h * HEAD_LANES:h * HEAD_LANES + 1]
                s = _dot_nt(qh, kk) - (2.0 ** -(h + 1)) * dist
                p = jnp.where(valid, jnp.exp(s - lse_h), 0.0)
                dv_acc = dv_acc + _dot_tn(p.astype(BF16), doh)
                dp = _dot_nt(doh, vv)
                ds = (p * (dp - delta)).astype(BF16)
                dq_pair = dq_pair + _dot(ds, jnp.where(half, kk, jnp.zeros_like(kk)))
                dk_acc = dk_acc + _dot_tn(ds, qh)
                p_sink = jnp.exp(sink_ref[0, h] - lse_h)
                dsink_ref[h:h + 1, :] += jnp.broadcast_to(-jnp.sum(p_sink * delta, axis=0, keepdims=True), (1, HEAD_LANES))
            dq_ref[:, pair] = dq_pair * SWA_SCALE
            dk_ref[win, kvl] += dk_acc
            dv_ref[win, kvl] += dv_acc

    n_tok = qs.shape[0]
    tok = lambda width: pl.BlockSpec((w, width), lambda b, n: (b * nb + n, 0))
    whole = lambda width: pl.BlockSpec((seq, width), lambda b, n: (b, 0))
    return pl.pallas_call(
        body, name="swa_bwd", grid=(n_seq, nb),
        out_shape=[jax.ShapeDtypeStruct((n_tok, 512), F32), jax.ShapeDtypeStruct((n_tok, 256), F32),
                   jax.ShapeDtypeStruct((n_tok, 256), F32), jax.ShapeDtypeStruct((N_HEADS, HEAD_LANES), F32)],
        in_specs=[tok(512), whole(256), whole(256), tok(512), tok(512), tok(1024), tok(1),
                  pl.BlockSpec((1, 1, seq), lambda b, n: (b, 0, 0)), pl.BlockSpec(memory_space=pltpu.SMEM)],
        out_specs=[tok(512), whole(256), whole(256), _full((N_HEADS, HEAD_LANES))],
        compiler_params=_params(2),
    )(qs, kd, vd, do, o, lse, pos_col, pos_row, sinks)


def _post_call(x, target, o_mla, o_swa, gates, mod, b_ada, fg, w_out, w_out_t, seq):
    n_tok = x.shape[0]
    tm = min(TOKEN_TILE, seq)
    per_seq = seq // tm
    n_seq = n_tok // seq

    def body(x_ref, t_ref, om_ref, os_ref, g_ref, mod_ref, bada_ref, fg_ref, w_ref, wt_ref,
             dx2_ref, do_ref, dg_ref, gw_ref, gfg_ref, dgate_ref, loss_ref):
        i = pl.program_id(0)

        @pl.when(i == 0)
        def _():
            gw_ref[...] = jnp.zeros_like(gw_ref)
            gfg_ref[...] = jnp.zeros_like(gfg_ref)
            loss_ref[...] = jnp.zeros_like(loss_ref)

        @pl.when(i % per_seq == 0)
        def _():
            dgate_ref[...] = jnp.zeros_like(dgate_ref)

        gate = mod_ref[0][:, 2 * D_MODEL:] + bada_ref[:, 2 * D_MODEL:]
        g = g_ref[...]
        o = jnp.concatenate([om_ref[...], os_ref[...]], axis=-1)
        sg = _sigmoid(g)
        sil = g * sg
        ypre = (o * sil).astype(BF16)
        y = _dot(ypre, w_ref[...])
        x2 = x_ref[...] + gate * y
        r2 = lax.rsqrt(jnp.mean(x2 * x2, axis=-1, keepdims=True) + EPS)
        xn2 = x2 * r2
        fgv = fg_ref[...]
        err = xn2 * fgv - t_ref[...]
        e2 = jnp.sum(err * err, axis=-1, keepdims=True)
        loss_ref[...] += jnp.broadcast_to(jnp.sum(e2, axis=0, keepdims=True) * (0.5 / D_MODEL), loss_ref.shape)
        dout = err * (1.0 / D_MODEL)
        gfg_ref[...] += jnp.sum(dout * xn2, axis=0, keepdims=True)
        dxn2 = dout * fgv
        dx2 = r2 * (dxn2 - xn2 * jnp.mean(dxn2 * xn2, axis=-1, keepdims=True))
        dx2_ref[...] = dx2
        dgate_ref[0] += jnp.sum(dx2 * y, axis=0, keepdims=True)
        dy = (dx2 * gate).astype(BF16)
        gw_ref[...] += _dot_tn(ypre, dy)
        dypre = _dot(dy, wt_ref[...])
        do_ref[...] = (dypre * sil).astype(BF16)
        dg_ref[...] = (dypre * o * (sg * (1.0 + g * (1.0 - sg)))).astype(BF16)

    tok = lambda w: pl.BlockSpec((tm, w), lambda i: (i, 0))
    per_b = pl.BlockSpec((1, 1, 3 * D_MODEL), lambda i: (i // per_seq, 0, 0))
    return pl.pallas_call(
        body, name="post", grid=(n_tok // tm,),
        out_shape=[jax.ShapeDtypeStruct((n_tok, D_MODEL), F32), jax.ShapeDtypeStruct((n_tok, D_MODEL), BF16),
                   jax.ShapeDtypeStruct((n_tok, D_MODEL), BF16), jax.ShapeDtypeStruct((D_MODEL, D_MODEL), F32),
                   jax.ShapeDtypeStruct((1, D_MODEL), F32), jax.ShapeDtypeStruct((n_seq, 1, D_MODEL), F32),
                   jax.ShapeDtypeStruct((1, HEAD_LANES), F32)],
        in_specs=[tok(D_MODEL), tok(D_MODEL), tok(512), tok(512), tok(D_MODEL), per_b, _full(b_ada.shape),
                  _full(fg.shape), _full(w_out.shape), _full(w_out_t.shape)],
        out_specs=[tok(D_MODEL), tok(D_MODEL), tok(D_MODEL), _full((D_MODEL, D_MODEL)), _full((1, D_MODEL)),
                   pl.BlockSpec((1, 1, D_MODEL), lambda i: (i // per_seq, 0, 0)), _full((1, HEAD_LANES))],
        compiler_params=_params(1),
    )(x, target, o_mla, o_swa, gates, mod, b_ada, fg, w_out, w_out_t)


def _mid_bwd_call(dqf, dkf, dv, zqkv, pos_col, qg, kvg, inv128, wq2, wkv, seq):
    n_tok = dqf.shape[0]
    tm = min(TOKEN_TILE, seq)

    def body(dq_ref, dk_ref, dv_ref, z_ref, pos_ref, qg_ref, kvg_ref, inv_ref, wq_ref, wkv_ref,
             dz_ref, dkr_ref, gwq_ref, gwkv_ref, gqg_ref, gkvg_ref):
        i = pl.program_id(0)

        @pl.when(i == 0)
        def _():
            gwq_ref[...] = jnp.zeros_like(gwq_ref)
            gwkv_ref[...] = jnp.zeros_like(gwkv_ref)
            gqg_ref[...] = jnp.zeros_like(gqg_ref)
            gkvg_ref[...] = jnp.zeros_like(gkvg_ref)

        cos, sin = _rope_tables(pos_ref[...], inv_ref[...])
        cf, sf = jnp.tile(cos, (1, N_HEADS)), jnp.tile(sin, (1, N_HEADS))
        dq = dq_ref[...] * MLA_SCALE
        dqr = jnp.concatenate([dq * cf, dq * sf], axis=-1).astype(BF16)
        zq, zkv = z_ref[:, :Q_LORA], z_ref[:, Q_LORA:]
        qgv, kvgv = qg_ref[...], kvg_ref[...]

        rq = lax.rsqrt(jnp.mean(zq * zq, axis=-1, keepdims=True) + EPS)
        xq = zq * rq
        gwq_ref[...] += _dot_tn((xq * qgv).astype(BF16), dqr)
        dqn = _dot_nt(dqr, wq_ref[...])
        gqg_ref[...] += jnp.sum(dqn * xq, axis=0, keepdims=True)
        dxq = dqn * qgv
        dz_ref[:, :Q_LORA] = (rq * (dxq - xq * jnp.mean(dxq * xq, axis=-1, keepdims=True))).astype(BF16)

        dk = dk_ref[...]
        dkv = jnp.concatenate([dk, dv_ref[...]], axis=-1).astype(BF16)
        rkv = lax.rsqrt(jnp.mean(zkv * zkv, axis=-1, keepdims=True) + EPS)
        xkv = zkv * rkv
        gwkv_ref[...] += _dot_tn((xkv * kvgv).astype(BF16), dkv)
        dkvn = _dot_nt(dkv, wkv_ref[...])
        gkvg_ref[...] += jnp.sum(dkvn * xkv, axis=0, keepdims=True)
        dxkv = dkvn * kvgv
        dz_ref[:, Q_LORA:] = (rkv * (dxkv - xkv * jnp.mean(dxkv * xkv, axis=-1, keepdims=True))).astype(BF16)

        dkpe = dk[:, :HEAD_LANES]
        for h in range(1, N_HEADS):
            dkpe = dkpe + dk[:, h * HEAD_LANES:(h + 1) * HEAD_LANES]
        dkr_ref[:, :HEAD_LANES] = (dkpe * cos).astype(BF16)
        dkr_ref[:, HEAD_LANES:] = (dkpe * sin).astype(BF16)

    tok = lambda w: pl.BlockSpec((tm, w), lambda i: (i, 0))
    return pl.pallas_call(
        body, name="mid_bwd", grid=(n_tok // tm,),
        out_shape=[jax.ShapeDtypeStruct((n_tok, 640), BF16), jax.ShapeDtypeStruct((n_tok, 256), BF16),
                   jax.ShapeDtypeStruct(wq2.shape, F32), jax.ShapeDtypeStruct(wkv.shape, F32),
                   jax.ShapeDtypeStruct((1, Q_LORA), F32), jax.ShapeDtypeStruct((1, KV_LORA), F32)],
        in_specs=[tok(1024), tok(1024), tok(512), tok(640), tok(1), _full(qg.shape), _full(kvg.shape),
                  _full(inv128.shape), _full(wq2.shape), _full(wkv.shape)],
        out_specs=[tok(640), tok(256), _full(wq2.shape), _full(wkv.shape), _full((1, Q_LORA)), _full((1, KV_LORA))],
        compiler_params=_params(1),
    )(dqf, dkf, dv, zqkv, pos_col, qg, kvg, inv128, wq2, wkv)


def _in_bwd_call(x, dx2, dz, dkr, dg, dqs, dkd, dvd, mod, b_ada, ng, wa_t, wkr2_t, seq):
    n_tok = x.shape[0]
    tm = min(TOKEN_TILE, seq)
    per_seq = seq // tm
    n_seq = n_tok // seq

    def body(x_ref, dx2_ref, dz_ref, dkr_ref, dg_ref, dqs_ref, dkd_ref, dvd_ref, mod_ref, bada_ref, ng_ref,
             wat_ref, wkrt_ref, gx_ref, gwa_ref, gwkr_ref, gng_ref, dshift_ref, dscale_ref):
        i = pl.program_id(0)

        @pl.when(i == 0)
        def _():
            gwa_ref[...] = jnp.zeros_like(gwa_ref)
            gwkr_ref[...] = jnp.zeros_like(gwkr_ref)
            gng_ref[...] = jnp.zeros_like(gng_ref)

        @pl.when(i % per_seq == 0)
        def _():
            dshift_ref[...] = jnp.zeros_like(dshift_ref)
            dscale_ref[...] = jnp.zeros_like(dscale_ref)

        xv = x_ref[...]
        modv = mod_ref[0] + bada_ref[...]
        shift, scale = modv[:, :D_MODEL], modv[:, D_MODEL:2 * D_MODEL]
        ngv = ng_ref[...]
        r1 = lax.rsqrt(jnp.mean(xv * xv, axis=-1, keepdims=True) + EPS)
        xn = xv * r1
        hb = ((xn * ngv) * (1.0 + scale) + shift).astype(BF16)

        dgv = dg_ref[...]
        pieces = [(A_ZQ, dz_ref[...]), (A_GM, dgv[:, :512]), (A_QS, dqs_ref[...].astype(BF16)),
                  (A_KD, dkd_ref[...].astype(BF16)), (A_VD, dvd_ref[...].astype(BF16)), (A_GS, dgv[:, 512:])]
        dkr = dkr_ref[...]
        gwkr_ref[...] += _dot_tn(hb, dkr)
        dh = _dot(dkr, wkrt_ref[...])
        for off, piece in pieces:
            wd = piece.shape[1]
            gwa_ref[:, off:off + wd] += _dot_tn(hb, piece)
            dh = dh + _dot(piece, wat_ref[off:off + wd, :])

        dshift_ref[0] += jnp.sum(dh, axis=0, keepdims=True)
        dscale_ref[0] += jnp.sum(dh * (xn * ngv), axis=0, keepdims=True)
        gng_ref[...] += jnp.sum(dh * xn * (1.0 + scale), axis=0, keepdims=True)
        dxn = dh * ngv * (1.0 + scale)
        gx_ref[...] = dx2_ref[...] + r1 * (dxn - xn * jnp.mean(dxn * xn, axis=-1, keepdims=True))

    tok = lambda w: pl.BlockSpec((tm, w), lambda i: (i, 0))
    per_b = lambda w: pl.BlockSpec((1, 1, w), lambda i: (i // per_seq, 0, 0))
    return pl.pallas_call(
        body, name="in_bwd", grid=(n_tok // tm,),
        out_shape=[jax.ShapeDtypeStruct((n_tok, D_MODEL), F32), jax.ShapeDtypeStruct((D_MODEL, A_END), F32),
                   jax.ShapeDtypeStruct((D_MODEL, 256), F32), jax.ShapeDtypeStruct((1, D_MODEL), F32),
                   jax.ShapeDtypeStruct((n_seq, 1, D_MODEL), F32), jax.ShapeDtypeStruct((n_seq, 1, D_MODEL), F32)],
        in_specs=[tok(D_MODEL), tok(D_MODEL), tok(640), tok(256), tok(D_MODEL), tok(512), tok(256), tok(256),
                  per_b(3 * D_MODEL), _full(b_ada.shape), _full(ng.shape), _full(wa_t.shape), _full(wkr2_t.shape)],
        out_specs=[tok(D_MODEL), _full((D_MODEL, A_END)), _full((D_MODEL, 256)), _full((1, D_MODEL)),
                   per_b(D_MODEL), per_b(D_MODEL)],
        compiler_params=_params(1),
    )(x, dx2, dz, dkr, dg, dqs, dkd, dvd, mod, b_ada, ng, wa_t, wkr2_t)


def _sum_call(name, stacked):
    n, rows, lanes = stacked.shape
    tr = PACK_HALF if rows % PACK_HALF == 0 else rows

    def body(s_ref, o_ref):
        acc = s_ref[0]
        for j in range(1, n):
            acc = acc + s_ref[j]
        o_ref[...] = acc

    return pl.pallas_call(
        body, name=name, grid=(rows // tr,),
        out_shape=jax.ShapeDtypeStruct((rows, lanes), F32),
        in_specs=[pl.BlockSpec((n, tr, lanes), lambda i: (0, i, 0))],
        out_specs=pl.BlockSpec((tr, lanes), lambda i: (i, 0)),
        compiler_params=_params(1),
    )(stacked)


def _adam_math(w, g, m, v):
    m_new = ADAM_B1 * m + (1.0 - ADAM_B1) * g
    v_new = ADAM_B2 * v + (1.0 - ADAM_B2) * (g * g)
    m_hat = m_new / (1.0 - ADAM_B1 ** ADAM_STEP)
    v_hat = v_new / (1.0 - ADAM_B2 ** ADAM_STEP)
    delta = -ADAM_LR * (m_hat / (jnp.sqrt(v_hat) + ADAM_EPS) + ADAM_WD * w)
    return delta, m_new, v_new


def _adam_call(name, w, g, m, v):
    rows, cols = w.shape
    tr = 256 if rows % 256 == 0 else rows

    def body(w_ref, g_ref, m_ref, v_ref, d_ref, mo_ref, vo_ref):
        d, mn, vn = _adam_math(w_ref[...], g_ref[...], m_ref[...], v_ref[...])
        d_ref[...] = d
        mo_ref[...] = mn
        vo_ref[...] = vn

    spec = pl.BlockSpec((tr, cols), lambda i: (i, 0))
    return pl.pallas_call(
        body, name=name, grid=(rows // tr,),
        out_shape=[jax.ShapeDtypeStruct(w.shape, F32)] * 3,
        in_specs=[spec] * 4, out_specs=[spec] * 3,
        compiler_params=_params(1),
    )(w, g, m, v)


def _ada_bwd_call(act_all, dmod_cols, w, m, v):
    rows, cols = w.shape
    tr = 256

    def body(a_ref, dm_ref, w_ref, m_ref, v_ref, g_ref, d_ref, mo_ref, vo_ref):
        g = _dot_tn(a_ref[...].astype(BF16), dm_ref[...].astype(BF16))
        d, mn, vn = _adam_math(w_ref[...], g, m_ref[...], v_ref[...])
        g_ref[...] = g
        d_ref[...] = d
        mo_ref[...] = mn
        vo_ref[...] = vn

    spec = pl.BlockSpec((tr, cols), lambda i: (i, 0))
    nb = act_all.shape[0]
    return pl.pallas_call(
        body, name="ada_bwd", grid=(rows // tr,),
        out_shape=[jax.ShapeDtypeStruct(w.shape, F32)] * 4,
        in_specs=[pl.BlockSpec((nb, tr), lambda i: (0, i)), _full(dmod_cols.shape), spec, spec, spec],
        out_specs=[spec] * 4,
        compiler_params=_params(1),
    )(act_all, dmod_cols, w, m, v)


def _small_call(parts, dmod_all, smalls, small_row_index):
    def body(p_ref, dm_ref, w_ref, m_ref, v_ref, g_ref, d_ref, mo_ref, vo_ref):
        r0 = small_row_index
        row = p_ref[0, r0:r0 + 1, :]
        for dv in range(1, 8):
            row = row + p_ref[dv, r0:r0 + 1, :]
        gb = dm_ref[0:1, :]
        for r in range(1, dm_ref.shape[0]):
            gb = gb + dm_ref[r:r + 1, :]
        g = jnp.concatenate([gb, row, jnp.zeros((6, 3 * D_MODEL), F32)], axis=0)
        d, mn, vn = _adam_math(w_ref[...], g, m_ref[...], v_ref[...])
        g_ref[...] = g
        d_ref[...] = d
        mo_ref[...] = mn
        vo_ref[...] = vn

    w, m, v = smalls
    return pl.pallas_call(
        body, name="small_update", grid=(1,),
        out_shape=[jax.ShapeDtypeStruct(w.shape, F32)] * 4,
        in_specs=[_full(parts.shape), _full(dmod_all.shape), _full(w.shape), _full(w.shape), _full(w.shape)],
        out_specs=[_full(w.shape)] * 4,
        compiler_params=_params(1),
    )(parts, dmod_all, w, m, v)


def _rot(t):
    half = t.shape[-1] // 2
    return jnp.concatenate([-t[..., half:], t[..., :half]], axis=-1)


def _rot_t(g):
    half = g.shape[-1] // 2
    return jnp.concatenate([g[..., half:], -g[..., :half]], axis=-1)


def _prepare_weights(w_in, w_uq, w_ukv):
    o = [0]
    for s in IN_SPLITS:
        o.append(o[-1] + s)
    ks, vs = w_in[:, o[5]:o[6]], w_in[:, o[6]:o[7]]
    dup = lambda t: jnp.concatenate([t[:, :64], t[:, :64], t[:, 64:], t[:, 64:]], axis=1)
    wa = jnp.concatenate([w_in[:, :o[2]], w_in[:, o[3]:o[5]], dup(ks), dup(vs), w_in[:, o[7]:]], axis=1)
    kr = w_in[:, o[2]:o[3]]
    zc = lambda n: jnp.zeros((w_in.shape[0], n), w_in.dtype)
    wkr2 = jnp.concatenate([zc(64), kr, zc(32), zc(64), _rot(kr), zc(32)], axis=1)
    uq = w_uq.reshape(Q_LORA, N_HEADS, MLA_NOPE + MLA_ROPE)
    zq = jnp.zeros((Q_LORA, N_HEADS, 32), w_uq.dtype)
    uq_full = jnp.concatenate([uq, zq], axis=-1).reshape(Q_LORA, 1024)
    uq_rot = jnp.concatenate([jnp.zeros((Q_LORA, N_HEADS, 64), w_uq.dtype), _rot(uq[..., MLA_NOPE:]), zq],
                             axis=-1).reshape(Q_LORA, 1024)
    wq2 = jnp.concatenate([uq_full, uq_rot], axis=1)
    ukv = w_ukv.reshape(KV_LORA, N_HEADS, 128)
    k_full = jnp.concatenate([ukv[..., :64], jnp.zeros((KV_LORA, N_HEADS, 64), w_ukv.dtype)], axis=-1).reshape(KV_LORA, 1024)
    wkv = jnp.concatenate([k_full, ukv[..., 64:].reshape(KV_LORA, 512)], axis=1)
    return wa, wkr2, wq2, wkv


def _restore_grads(gwa, gwkr2, gwq2, gwkv):
    fold = lambda g: jnp.concatenate([g[:, 0:64] + g[:, 64:128], g[:, 128:192] + g[:, 192:256]], axis=1)
    gkr = gwkr2[:, 64:96] + _rot_t(gwkr2[:, 192:224])
    g_in = jnp.concatenate([gwa[:, :A_GM], gkr, gwa[:, A_GM:A_KD], fold(gwa[:, A_KD:A_VD]), fold(gwa[:, A_VD:A_GS]),
                            gwa[:, A_GS:]], axis=1)
    gf = gwq2[:, :1024].reshape(Q_LORA, N_HEADS, 128)
    gr = gwq2[:, 1024:].reshape(Q_LORA, N_HEADS, 128)
    g_uq = jnp.concatenate([gf[..., :64], gf[..., 64:96] + _rot_t(gr[..., 64:96])], axis=-1).reshape(Q_LORA, 768)
    gk = gwkv[:, :1024].reshape(KV_LORA, N_HEADS, 128)[..., :64]
    gv = gwkv[:, 1024:].reshape(KV_LORA, N_HEADS, 64)
    g_ukv = jnp.concatenate([gk, gv], axis=-1).reshape(KV_LORA, 1024)
    return g_in, g_uq, g_ukv


def _local_step(x, positions, target, mod_rows, b_ada, ng, qg, kvg, sinks, fg, w_in_b, w_uq_b, w_ukv_b, w_out_b):
    n_seq, seq, _ = x.shape
    n_tok = n_seq * seq
    x2d = x.reshape(n_tok, D_MODEL)
    t2d = target.reshape(n_tok, D_MODEL)
    pos_f = positions.astype(F32)
    pos_col = pos_f.reshape(n_tok, 1)
    pos_row = pos_f.reshape(n_seq, 1, seq)
    mod3 = mod_rows.reshape(n_seq, 1, 3 * D_MODEL)
    inv = ROPE_THETA ** (-jnp.arange(0, MLA_ROPE, 2, dtype=F32) / MLA_ROPE)
    inv128 = jnp.concatenate([jnp.zeros((64,), F32), inv, inv, jnp.zeros((32,), F32)]).reshape(1, 128)
    fg2 = fg.reshape(1, D_MODEL)

    wa, wkr2, wq2, wkv = _prepare_weights(w_in_b, w_uq_b, w_ukv_b)

    zqkv, gates, qf, kf, v, qs, kd, vd = _pre_call(x2d, pos_col, mod3, b_ada, ng, qg, kvg, inv128, wa, wkr2, wq2, wkv, seq)
    o_mla, lse_mla = _mla_fwd_call(qf, kf, v, n_seq, seq)
    o_swa, lse_swa = _swa_fwd_call(qs, kd, vd, pos_col, pos_row, sinks, n_seq, seq)
    dx2, do, dg, g_out, g_fg, dgate, loss = _post_call(x2d, t2d, o_mla, o_swa, gates, mod3, b_ada, fg2, w_out_b, w_out_b.T, seq)
    do_mla, do_swa = do[:, :512], do[:, 512:]
    dqf, dkf, dv = _mla_bwd_call(qf, kf, v, do_mla, o_mla, lse_mla, n_seq, seq)
    dqs, dkd, dvd, dsink = _swa_bwd_call(qs, kd, vd, do_swa, o_swa, lse_swa, pos_col, pos_row, sinks, n_seq, seq)
    dz, dkr, g_wq2, g_wkv, g_qg, g_kvg = _mid_bwd_call(dqf, dkf, dv, zqkv, pos_col, qg, kvg, inv128, wq2, wkv, seq)
    gx, g_wa, g_wkr2, g_ng, dshift, dscale = _in_bwd_call(x2d, dx2, dz, dkr, dg, dqs, dkd, dvd, mod3, b_ada, ng,
                                                         wa.T, wkr2.T, seq)
    g_in, g_uq, g_ukv = _restore_grads(g_wa, g_wkr2, g_wq2, g_wkv)
    dmod = jnp.concatenate([dshift, dscale, dgate], axis=-1).reshape(n_seq, 3 * D_MODEL)
    small_row = jnp.concatenate([g_ng, g_fg, g_qg, g_kvg, jnp.pad(dsink[:, 0].reshape(1, N_HEADS), ((0, 0), (0, 120))),
                                 loss, jnp.zeros((1, 128), F32)], axis=1)
    return gx.reshape(x.shape), (g_in, g_uq, g_ukv, g_out), small_row, dmod


def kernel(x, c, positions, w_ada, b_ada, norm_gain, w_in, q_norm_gain, kv_norm_gain, w_uq, w_ukv, swa_sinks, w_out, final_gain, loss_target, m_w_ada, m_b_ada, m_norm_gain, m_w_in, m_q_norm_gain, m_kv_norm_gain, m_w_uq, m_w_ukv, m_swa_sinks, m_w_out, m_final_gain, v_w_ada, v_b_ada, v_norm_gain, v_w_in, v_q_norm_gain, v_kv_norm_gain, v_w_uq, v_w_ukv, v_swa_sinks, v_w_out, v_final_gain):
    n_seq = x.shape[0]
    xi, yi, ci = lax.axis_index("x"), lax.axis_index("y"), lax.axis_index("c")
    dev = 4 * xi + 2 * yi + ci
    chip = 2 * xi + yi

    (c_got,) = _exchange("gather_c", [c], ALL_OTHERS)
    c_all = _by_device(c, c_got, 3).reshape(8 * n_seq, D_MODEL)
    act_all, mod_piece = _ada_fwd_call(c_all, w_ada[0])
    (piece_got,) = _exchange("gather_mod", [mod_piece], OTHER_CHIPS)
    pieces = _by_device(mod_piece, piece_got, 2)
    mod_all = jnp.transpose(pieces, (1, 0, 2)).reshape(8 * n_seq, 3 * D_MODEL)
    mod_rows = lax.dynamic_slice_in_dim(mod_all, dev * n_seq, n_seq, axis=0)

    shards = [w_in[0].astype(BF16), w_uq[0].astype(BF16), w_ukv[0].astype(BF16), w_out[0].astype(BF16)]
    got = _exchange("gather_w", shards, OTHER_CHIPS)
    full = [_by_device(s, g, 2) for s, g in zip(shards, got)]
    cols = lambda t: jnp.transpose(t, (1, 0, 2)).reshape(t.shape[1], 4 * t.shape[2])
    w_in_b, w_uq_b, w_ukv_b = cols(full[0]), cols(full[1]), cols(full[2])
    w_out_b = full[3].reshape(D_MODEL, D_MODEL)

    gx, (g_in, g_uq, g_ukv, g_out), small_row, dmod = _local_step(
        x, positions, loss_target, mod_rows, b_ada, norm_gain, q_norm_gain, kv_norm_gain, swa_sinks, final_gain,
        w_in_b, w_uq_b, w_ukv_b, w_out_b)

    part = jnp.concatenate([dmod, small_row, jnp.zeros((8 - n_seq - 1, 3 * D_MODEL), F32)], axis=0)
    (part_got,) = _exchange("gather_small", [part], ALL_OTHERS)
    parts = _by_device(part, part_got, 3)
    dmod_all = parts[:, :n_seq, :].reshape(8 * n_seq, 3 * D_MODEL)

    rows_of = lambda g, n: jnp.transpose(g.reshape(g.shape[0], 4, n), (1, 0, 2)).reshape(4, -1)
    packed = jnp.concatenate([rows_of(g_in, 616), rows_of(g_uq, 192), rows_of(g_ukv, 256), g_out.reshape(4, -1)], axis=1)
    packed = jnp.transpose(packed.reshape(4, 2, PACK_HALF, 1024), (1, 0, 2, 3))
    (sib,) = _exchange("reduce_sibling", [packed], SIBLING, pick=lambda tx, ty, tc: tc)
    mine = lax.dynamic_index_in_dim(packed, ci, axis=0, keepdims=False)
    pair = jnp.stack([mine, sib[0]])
    chip_sum = _sum_call("sum_cores", pair.reshape(2, 4 * PACK_HALF, 1024)).reshape(4, PACK_HALF, 1024)
    (from_chips,) = _exchange("reduce_chips", [chip_sum], OTHER_CHIPS, pick=lambda tx, ty, tc: 2 * tx + ty)
    own = lax.dynamic_index_in_dim(chip_sum, chip, axis=0, keepdims=False)
    half_sum = _sum_call("sum_chips", _by_device(own, from_chips, 2))
    (other_half,) = _exchange("share_sibling", [half_sum], SIBLING)
    flat = jnp.where(ci == 0, jnp.concatenate([half_sum, other_half[0]]), jnp.concatenate([other_half[0], half_sum]))
    flat = flat.reshape(-1)
    offs = [0]
    for s in SHARD_SIZES:
        offs.append(offs[-1] + s)
    g_in_s = flat[offs[0]:offs[1]].reshape(1024, 616)
    g_uq_s = flat[offs[1]:offs[2]].reshape(384, 192)
    g_ukv_s = flat[offs[2]:offs[3]].reshape(256, 256)
    g_out_s = flat[offs[3]:offs[4]].reshape(256, 1024)

    d_in, nm_in, nv_in = _adam_call("adam_w_in", w_in[0], g_in_s, m_w_in[0], v_w_in[0])
    d_uq, nm_uq, nv_uq = _adam_call("adam_w_uq", w_uq[0], g_uq_s, m_w_uq[0], v_w_uq[0])
    d_ukv, nm_ukv, nv_ukv = _adam_call("adam_w_ukv", w_ukv[0], g_ukv_s, m_w_ukv[0], v_w_ukv[0])
    d_out, nm_out, nv_out = _adam_call("adam_w_out", w_out[0], g_out_s, m_w_out[0], v_w_out[0])
    dmod_cols = lax.dynamic_slice_in_dim(dmod_all, chip * 768, 768, axis=1)
    g_ada, d_ada, nm_ada, nv_ada = _ada_bwd_call(act_all, dmod_cols, w_ada[0], m_w_ada[0], v_w_ada[0])

    def small_pack(b, ngv, fgv, qgv, kvgv, sk):
        row1 = jnp.concatenate([ngv.reshape(1, -1), fgv.reshape(1, -1), qgv.reshape(1, -1), kvgv.reshape(1, -1),
                                jnp.pad(sk.reshape(1, -1), ((0, 0), (0, 120))), jnp.zeros((1, 256), F32)], axis=1)
        return jnp.concatenate([b.reshape(1, -1), row1, jnp.zeros((6, 3 * D_MODEL), F32)], axis=0)

    smalls = (small_pack(b_ada, norm_gain, final_gain, q_norm_gain, kv_norm_gain, swa_sinks),
              small_pack(m_b_ada, m_norm_gain, m_final_gain, m_q_norm_gain, m_kv_norm_gain, m_swa_sinks),
              small_pack(v_b_ada, v_norm_gain, v_final_gain, v_q_norm_gain, v_kv_norm_gain, v_swa_sinks))
    sg, sd, sm, sv = _small_call(parts, dmod_all, smalls, n_seq)

    def small_unpack(t):
        r = t[1]
        return (t[0].reshape(1, 3 * D_MODEL), r[0:1024].reshape(1, 1024), r[2048:2432].reshape(1, 384),
                r[2432:2688].reshape(1, 256), r[2688:2696].reshape(1, 8), r[1024:2048].reshape(1024))

    loss = sg[1, 2816]
    gb, gng, gqg, gkvg, gsk, gfg = small_unpack(sg)
    db, dng, dqg, dkvg, dsk, dfg = small_unpack(sd)
    mb, mng, mqg, mkvg, msk, mfg = small_unpack(sm)
    vb, vng, vqg, vkvg, vsk, vfg = small_unpack(sv)
    e = lambda t: t[None]
    return (loss, gx,
            e(g_ada), gb, gng, e(g_in_s), gqg, gkvg, e(g_uq_s), e(g_ukv_s), gsk, e(g_out_s), gfg,
            e(d_ada), db, dng, e(d_in), dqg, dkvg, e(d_uq), e(d_ukv), dsk, e(d_out), dfg,
            e(nm_ada), mb, mng, e(nm_in), mqg, mkvg, e(nm_uq), e(nm_ukv), msk, e(nm_out), mfg,
            e(nv_ada), vb, vng, e(nv_in), vqg, vkvg, e(nv_uq), e(nv_ukv), vsk, e(nv_out), vfg)
```

```python
import functools

import jax
import jax.numpy as jnp
from jax import lax
from jax.experimental import pallas as pl
from jax.experimental.pallas import tpu as pltpu

F32 = jnp.float32
BF16 = jnp.bfloat16

D_MODEL = 1024
Q_LORA = 384
KV_LORA = 256
N_HEADS = 8
MLA_NOPE = 64
MLA_ROPE = 32
HEAD_LANES = 128
HALF = 64
SWA_WINDOW = 128
EPS = 1e-6
ROPE_THETA = 10000.0
MLA_SCALE = (MLA_NOPE + MLA_ROPE) ** -0.5
SWA_SCALE = 64 ** -0.5
NEG = -1e30

ADAM_LR = 0.001
ADAM_B1 = 0.9
ADAM_B2 = 0.999
ADAM_EPS = 1e-08
ADAM_WD = 0.01
ADAM_STEP = 10

A_ZQ, A_ZKV, A_GM, A_QS, A_KD, A_VD, A_GS, A_END = 0, 384, 640, 1152, 1664, 1920, 2176, 2688
IN_SPLITS = (384, 256, 32, 512, 512, 128, 128, 512)
D_IN = sum(IN_SPLITS)

TOKEN_TILE = 256
ATT_TILE = 256
VMEM_LIMIT = 56 * 1024 * 1024


def _dot(a, b):
    return jnp.dot(a, b, preferred_element_type=F32)


def _dot_nt(a, b):
    return lax.dot_general(a, b, (((1,), (1,)), ((), ())), preferred_element_type=F32)


def _dot_tn(a, b):
    return lax.dot_general(a, b, (((0,), (0,)), ((), ())), preferred_element_type=F32)


def _params(n_grid):
    return pltpu.CompilerParams(dimension_semantics=("arbitrary",) * n_grid, vmem_limit_bytes=VMEM_LIMIT)


def _full(shape):
    nd = len(shape)
    return pl.BlockSpec(shape, lambda *_: (0,) * nd)


def _sigmoid(g):
    return 1.0 / (1.0 + jnp.exp(-g))


MESH = pl.DeviceIdType.MESH
ROWS_PER_DEVICE = 8
VMEM_SPEC = pl.BlockSpec(memory_space=pltpu.VMEM)
ANY_SPEC = pl.BlockSpec(memory_space=pl.ANY)


def _position():
    x, y, c = lax.axis_index("x"), lax.axis_index("y"), lax.axis_index("c")
    sibling = (x, y, 1 - c)
    others = [(1 - x, y, c), (x, 1 - y, c), (1 - x, 1 - y, c)]
    return (x, y, c), 4 * x + 2 * y + c, 2 * x + y, sibling, others


def _rows_of(dev):
    return pl.ds(pl.multiple_of(dev * ROWS_PER_DEVICE, ROWS_PER_DEVICE), ROWS_PER_DEVICE)


def _all_to_all_rows(block_ref, table_ref, dev, me, send_sems, recv_sems):
    x, y, c = me
    waits = []
    for k in range(1, 8):
        peer = (1 - x if k & 4 else x, 1 - y if k & 2 else y, 1 - c if k & 1 else c)
        pltpu.make_async_remote_copy(src_ref=block_ref, dst_ref=table_ref.at[_rows_of(dev)], send_sem=send_sems.at[k - 1],
                                     recv_sem=recv_sems.at[k - 1], device_id=peer, device_id_type=MESH).start()
        waits.append(pltpu.make_async_remote_copy(
            src_ref=block_ref, dst_ref=table_ref.at[_rows_of(jnp.bitwise_xor(dev, k))], send_sem=send_sems.at[k - 1],
            recv_sem=recv_sems.at[k - 1], device_id=peer, device_id_type=MESH))
    return waits


def _comm_fwd_call(c_blk, w_ada, shards):
    n = len(shards)

    def body(c_ref, wada_ref, *refs):
        w_refs, act_ref, pieces_ref, full_refs = refs[:n], refs[n], refs[n + 1], refs[n + 2:2 * n + 2]
        c_all_ref = refs[2 * n + 2]
        c_send, c_recv, p_send, p_recv, w_send, w_recv, f_send, f_recv, loc_sem = refs[2 * n + 3:]
        me, dev, chip, sibling, others = _position()
        core = me[2]
        chip_of = [2 * p[0] + p[1] for p in others]

        local = [pltpu.make_async_copy(w_refs[i], full_refs[i].at[chip], loc_sem.at[i]) for i in range(n)]
        for cp in local:
            cp.start()

        def over_ici(i, j, src_chip):
            return pltpu.make_async_remote_copy(
                src_ref=w_refs[i].at[core], dst_ref=full_refs[i].at[src_chip, core], send_sem=w_send.at[3 * i + j],
                recv_sem=w_recv.at[3 * i + j], device_id=others[j], device_id_type=MESH)

        def to_sibling(i, j, half):
            return pltpu.make_async_remote_copy(
                src_ref=full_refs[i].at[chip_of[j], half], dst_ref=full_refs[i].at[chip_of[j], half],
                send_sem=f_send.at[3 * i + j], recv_sem=f_recv.at[3 * i + j], device_id=sibling, device_id_type=MESH)

        sent = [over_ici(i, j, chip) for i in range(n) for j in range(3)]
        for cp in sent:
            cp.start()

        c_all_ref[_rows_of(dev), :] = c_ref[...]
        c_waits = _all_to_all_rows(c_ref, c_all_ref, dev, me, c_send, c_recv)
        for cp in c_waits:
            cp.wait()
        cv = c_all_ref[...]
        act = cv * _sigmoid(cv)
        act_ref[...] = act
        pieces_ref[chip] = _dot(act.astype(BF16), wada_ref[...].astype(BF16))
        piece = lambda j, src_chip: pltpu.make_async_remote_copy(
            src_ref=pieces_ref.at[chip], dst_ref=pieces_ref.at[src_chip], send_sem=p_send.at[j], recv_sem=p_recv.at[j],
            device_id=others[j], device_id_type=MESH)
        for j in range(3):
            piece(j, chip).start()
        for j in range(3):
            piece(j, chip).wait_send()
            piece(j, chip_of[j]).wait_recv()

        for i in range(n):
            for j in range(3):
                over_ici(i, j, chip_of[j]).wait_recv()
                to_sibling(i, j, core).start()
        for i in range(n):
            for j in range(3):
                to_sibling(i, j, 1 - core).wait_recv()
                to_sibling(i, j, core).wait_send()
        for cp in sent:
            cp.wait_send()
        for cp in local:
            cp.wait()

    rows = 8 * ROWS_PER_DEVICE
    dma = pltpu.SemaphoreType.DMA
    return pl.pallas_call(
        body, name="comm_fwd",
        out_shape=[jax.ShapeDtypeStruct((rows, D_MODEL), F32), jax.ShapeDtypeStruct((4, rows, w_ada.shape[1]), F32)]
        + [jax.ShapeDtypeStruct((4,) + s.shape, s.dtype) for s in shards],
        in_specs=[VMEM_SPEC, VMEM_SPEC] + [ANY_SPEC] * n,
        out_specs=[VMEM_SPEC, VMEM_SPEC] + [ANY_SPEC] * n,
        scratch_shapes=[pltpu.VMEM((rows, D_MODEL), F32), dma((7,)), dma((7,)), dma((3,)), dma((3,)),
                        dma((3 * n,)), dma((3 * n,)), dma((3 * n,)), dma((3 * n,)), dma((n,))],
        compiler_params=pltpu.CompilerParams(vmem_limit_bytes=VMEM_LIMIT),
    )(c_blk, w_ada, *shards)


def _comm_bwd_call(grads, part):
    n = len(grads)

    def body(part_ref, *refs):
        g_refs, f_refs, parts_ref = refs[:n], refs[n:2 * n], refs[2 * n]
        scratch = refs[2 * n + 1:]
        a_refs, b_refs, p_refs, r_refs = (scratch[k * n:(k + 1) * n] for k in range(4))
        s_send, s_recv, d_send, d_recv, e_send, e_recv, h_send, h_recv, loc_sem = scratch[4 * n:]
        me, dev, chip, sibling, others = _position()
        core = me[2]
        chip_of = [2 * p[0] + p[1] for p in others]

        parts_ref[_rows_of(dev), :] = part_ref[...]
        s_waits = _all_to_all_rows(part_ref, parts_ref, dev, me, s_send, s_recv)

        mine = [pltpu.make_async_copy(g_refs[i].at[core], a_refs[i], loc_sem.at[i]) for i in range(n)]
        swap = [pltpu.make_async_remote_copy(src_ref=g_refs[i].at[1 - core], dst_ref=b_refs[i], send_sem=d_send.at[i],
                                             recv_sem=d_recv.at[i], device_id=sibling, device_id_type=MESH) for i in range(n)]
        for cp in mine + swap:
            cp.start()
        for i in range(n):
            mine[i].wait()
            swap[i].wait()
            for k in range(4):
                s = a_refs[i][k] + b_refs[i][k]
                a_refs[i][k] = s
                p_refs[i][k] = s.astype(BF16)

        cross = [pltpu.make_async_remote_copy(src_ref=p_refs[i].at[chip_of[j]], dst_ref=r_refs[i].at[j],
                                              send_sem=e_send.at[3 * i + j], recv_sem=e_recv.at[3 * i + j],
                                              device_id=others[j], device_id_type=MESH) for i in range(n) for j in range(3)]
        for cp in cross:
            cp.start()
        share = []
        for i in range(n):
            for j in range(3):
                cross[3 * i + j].wait()
            f_refs[i][core] = (a_refs[i][chip] + r_refs[i][0].astype(F32) + r_refs[i][1].astype(F32)
                               + r_refs[i][2].astype(F32))
            cp = pltpu.make_async_remote_copy(src_ref=f_refs[i].at[core], dst_ref=f_refs[i].at[core], send_sem=h_send.at[i],
                                              recv_sem=h_recv.at[i], device_id=sibling, device_id_type=MESH)
            cp.start()
            share.append(cp)
        for i in range(n):
            share[i].wait_send()
            pltpu.make_async_remote_copy(src_ref=f_refs[i].at[core], dst_ref=f_refs[i].at[1 - core], send_sem=h_send.at[i],
                                         recv_sem=h_recv.at[i], device_id=sibling, device_id_type=MESH).wait_recv()
        for cp in s_waits:
            cp.wait()

    rows = 8 * ROWS_PER_DEVICE
    dma = pltpu.SemaphoreType.DMA
    quarter = [g.shape[1:] for g in grads]
    return pl.pallas_call(
        body, name="comm_bwd",
        out_shape=[jax.ShapeDtypeStruct((2,) + g.shape[2:], F32) for g in grads]
        + [jax.ShapeDtypeStruct((rows, part.shape[1]), F32)],
        in_specs=[VMEM_SPEC] + [ANY_SPEC] * n,
        out_specs=[VMEM_SPEC] * (n + 1),
        scratch_shapes=[pltpu.VMEM(q, F32) for q in quarter] + [pltpu.VMEM(q, F32) for q in quarter]
        + [pltpu.VMEM(q, BF16) for q in quarter] + [pltpu.VMEM((3,) + q[1:], BF16) for q in quarter]
        + [dma((7,)), dma((7,)), dma((n,)), dma((n,)), dma((3 * n,)), dma((3 * n,)), dma((n,)), dma((n,)), dma((n,))],
        compiler_params=pltpu.CompilerParams(vmem_limit_bytes=VMEM_LIMIT),
    )(part, *grads)


def _rope_tables(pos_col, inv_row):
    ang = pos_col * inv_row
    return jnp.cos(ang), jnp.sin(ang)


def _pre_call(x, pos_col, mod, b_ada, ng, qg, kvg, inv128, wa, wkr2, wq2, wkv, seq):
    n_tok = x.shape[0]
    tm = min(TOKEN_TILE, seq)
    per_seq = seq // tm

    def body(x_ref, pos_ref, mod_ref, bada_ref, ng_ref, qg_ref, kvg_ref, inv_ref, wa_ref, wkr_ref, wq_ref, wkv_ref,
             zqkv_ref, gates_ref, qf_ref, kf_ref, v_ref, qs_ref, kd_ref, vd_ref):
        xv = x_ref[...]
        modv = mod_ref[0] + bada_ref[...]
        shift, scale = modv[:, :D_MODEL], modv[:, D_MODEL:2 * D_MODEL]
        r1 = lax.rsqrt(jnp.mean(xv * xv, axis=-1, keepdims=True) + EPS)
        h = ((xv * r1) * ng_ref[...]) * (1.0 + scale) + shift
        hb = h.astype(BF16)
        za = _dot(hb, wa_ref[...])
        zkr = _dot(hb, wkr_ref[...])
        cos, sin = _rope_tables(pos_ref[...], inv_ref[...])
        zqkv_ref[...] = za[:, :A_GM]
        gates_ref[:, :512] = za[:, A_GM:A_QS]
        gates_ref[:, 512:] = za[:, A_GS:A_END]
        qs_ref[...] = (za[:, A_QS:A_KD] * SWA_SCALE).astype(BF16)
        kd_ref[...] = za[:, A_KD:A_VD].astype(BF16)
        vd_ref[...] = za[:, A_VD:A_GS].astype(BF16)
        zq, zkv = za[:, A_ZQ:A_ZKV], za[:, A_ZKV:A_GM]
        rq = lax.rsqrt(jnp.mean(zq * zq, axis=-1, keepdims=True) + EPS)
        qn = ((zq * rq) * qg_ref[...]).astype(BF16)
        qr = _dot(qn, wq_ref[...])
        cf, sf = jnp.tile(cos, (1, N_HEADS)), jnp.tile(sin, (1, N_HEADS))
        qf_ref[...] = ((qr[:, :1024] * cf + qr[:, 1024:] * sf) * MLA_SCALE).astype(BF16)
        rkv = lax.rsqrt(jnp.mean(zkv * zkv, axis=-1, keepdims=True) + EPS)
        kvn = ((zkv * rkv) * kvg_ref[...]).astype(BF16)
        kv = _dot(kvn, wkv_ref[...])
        kpe = zkr[:, :128] * cos + zkr[:, 128:] * sin
        kf_ref[...] = (kv[:, :1024] + jnp.tile(kpe, (1, N_HEADS))).astype(BF16)
        v_ref[...] = kv[:, 1024:].astype(BF16)

    tok = lambda w: pl.BlockSpec((tm, w), lambda i: (i, 0))
    outs = [(640, F32), (1024, F32), (1024, BF16), (1024, BF16), (512, BF16), (512, BF16), (256, BF16), (256, BF16)]
    return pl.pallas_call(
        body, name="pre", grid=(n_tok // tm,),
        out_shape=[jax.ShapeDtypeStruct((n_tok, w), dt) for w, dt in outs],
        in_specs=[tok(D_MODEL), tok(1), pl.BlockSpec((1, 1, 3 * D_MODEL), lambda i: (i // per_seq, 0, 0)),
                  _full(b_ada.shape), _full(ng.shape), _full(qg.shape), _full(kvg.shape), _full(inv128.shape),
                  _full(wa.shape), _full(wkr2.shape), _full(wq2.shape), _full(wkv.shape)],
        out_specs=[tok(w) for w, _ in outs],
        compiler_params=_params(1),
    )(x, pos_col, mod, b_ada, ng, qg, kvg, inv128, wa, wkr2, wq2, wkv)


def _lane_lo(width=HEAD_LANES):
    return lax.broadcasted_iota(jnp.int32, (1, width), 1) < HALF


def _eye(n=HEAD_LANES):
    r = lax.broadcasted_iota(jnp.int32, (n, n), 0)
    c = lax.broadcasted_iota(jnp.int32, (n, n), 1)
    return jnp.where(r == c, 1.0, 0.0).astype(BF16)


def _mla_fwd_call(qf, kf, v, n_seq, seq):
    tq = min(ATT_TILE, seq)
    nq = seq // tq

    def body(q_ref, k_ref, v_ref, o_ref, lse_ref, vt_ref):
        i = pl.program_id(1)
        eye = _eye()

        @pl.when(i == 0)
        def _():
            for t in range(nq):
                for p in range(N_HEADS // 2):
                    pair = slice(p * HEAD_LANES, (p + 1) * HEAD_LANES)
                    vt_ref[pair, t * tq:(t + 1) * tq] = _dot_nt(eye, v_ref[t * tq:(t + 1) * tq, pair]).astype(BF16)

        q = q_ref[...]
        qcol = i * tq + lax.broadcasted_iota(jnp.int32, (1, tq), 1)
        heads = range(N_HEADS)
        lanes = [slice(h * HEAD_LANES, (h + 1) * HEAD_LANES) for h in heads]

        def make_step(masked):
            def step(kt, carry):
                start = pl.multiple_of(kt * tq, tq)
                k = k_ref[pl.ds(start, tq), :]
                vt = vt_ref[:, pl.ds(start, tq)]
                sts = [_dot_nt(k[:, lanes[h]], q[:, lanes[h]]) for h in heads]
                if masked:
                    keep = (kt * tq + lax.broadcasted_iota(jnp.int32, (tq, 1), 0)) <= qcol
                    sts = [jnp.where(keep, st, NEG) for st in sts]
                stats, pts = [], []
                for h in heads:
                    m_old, l_old = carry[3 * h], carry[3 * h + 1]
                    m_new = jnp.maximum(m_old, jnp.max(sts[h], axis=0, keepdims=True))
                    pt = jnp.exp(sts[h] - m_new)
                    alpha = jnp.exp(m_old - m_new)
                    stats.append((m_new, alpha * l_old + jnp.sum(pt, axis=0, keepdims=True), alpha))
                    pts.append(pt.astype(BF16))
                pvs = [_dot(vt[h * HALF:(h + 1) * HALF, :], pts[h]) for h in heads]
                out = []
                for h in heads:
                    out += [stats[h][0], stats[h][1], carry[3 * h + 2] * stats[h][2] + pvs[h]]
                return tuple(out)
            return step

        row = lambda val: jnp.full((1, tq), val, F32)
        init = (row(NEG), row(0.0), jnp.zeros((HALF, tq), F32)) * N_HEADS
        carry = lax.fori_loop(0, i, make_step(False), init)
        carry = make_step(True)(i, carry)
        acc_t = jnp.concatenate([carry[3 * h + 2] * (1.0 / carry[3 * h + 1]) for h in heads], axis=0)
        o_ref[...] = acc_t.T
        for h in heads:
            lse_ref[0, h // 4, h % 4:h % 4 + 1, :] = carry[3 * h] + jnp.log(carry[3 * h + 1])

    n_tok = qf.shape[0]
    return pl.pallas_call(
        body, name="mla_fwd", grid=(n_seq, nq),
        out_shape=[jax.ShapeDtypeStruct((n_tok, 512), F32), jax.ShapeDtypeStruct((n_seq, 2, 4, seq), F32)],
        in_specs=[pl.BlockSpec((tq, 1024), lambda b, i: (b * nq + i, 0)),
                  pl.BlockSpec((seq, 1024), lambda b, i: (b, 0)),
                  pl.BlockSpec((seq, 512), lambda b, i: (b, 0))],
        out_specs=[pl.BlockSpec((tq, 512), lambda b, i: (b * nq + i, 0)),
                   pl.BlockSpec((1, 2, 4, tq), lambda b, i: (b, 0, 0, i))],
        scratch_shapes=[pltpu.VMEM((512, seq), BF16)],
        compiler_params=_params(2),
    )(qf, kf, v)


def _mla_bwd_call(qf, kf, v, do, o, lse, n_seq, seq):
    tq = min(ATT_TILE, seq)
    nq = seq // tq

    nh = 4
    heads = range(nh)
    lanes = [slice(h * HEAD_LANES, (h + 1) * HEAD_LANES) for h in heads]

    def body(q_ref, k_ref, v_ref, do_ref, o_ref, lse_ref, dq_ref, dk_ref, dv_ref,
             kt_ref, dot_ref, delta_ref, dqt_ref):
        eye = _eye()
        lo = _lane_lo()
        sub_lo = lax.broadcasted_iota(jnp.int32, (HEAD_LANES, 1), 0) < HALF
        ones_lo = jnp.where(jnp.broadcast_to(lo, (8, HEAD_LANES)), 1.0, 0.0).astype(BF16)
        ones_hi = jnp.where(jnp.broadcast_to(lo, (8, HEAD_LANES)), 0.0, 1.0).astype(BF16)

        for t in range(nq):
            r = slice(t * tq, (t + 1) * tq)
            kv = k_ref[r, :]
            for h in heads:
                kt_ref[lanes[h], r] = _dot_nt(eye, kv[:, lanes[h]]).astype(BF16)
            for p in range(nh // 2):
                dov = do_ref[r, lanes[p]]
                dt = _dot_nt(eye, dov)
                dot_ref[2 * p, :, r] = jnp.where(sub_lo, dt, 0.0).astype(BF16)
                dot_ref[2 * p + 1, :, r] = jnp.where(sub_lo, 0.0, dt).astype(BF16)
                prod = dov.astype(F32) * o_ref[r, lanes[p]]
                p_hi = prod.astype(BF16)
                p_lo = (prod - p_hi.astype(F32)).astype(BF16)
                delta_ref[2 * p, :, r] = _dot_nt(ones_lo, p_hi) + _dot_nt(ones_lo, p_lo)
                delta_ref[2 * p + 1, :, r] = _dot_nt(ones_hi, p_hi) + _dot_nt(ones_hi, p_lo)
        dqt_ref[...] = jnp.zeros_like(dqt_ref)

        def k_step(kt, _):
            kr = pl.ds(pl.multiple_of(kt * tq, tq), tq)
            k = k_ref[kr, :]
            vv = v_ref[kr, :]
            k_t = kt_ref[:, kr]
            krow = kt * tq + lax.broadcasted_iota(jnp.int32, (tq, 1), 0)

            def make_step(masked):
                def q_step(qt, carry):
                    qr = pl.ds(pl.multiple_of(qt * tq, tq), tq)
                    q = q_ref[qr, :]
                    do_ts = [dot_ref[h, :, qr] for h in heads]
                    sts = [_dot_nt(k[:, lanes[h]], q[:, lanes[h]]) for h in heads]
                    dpts = [_dot(vv[:, lanes[h // 2]], do_ts[h]) for h in heads]
                    if masked:
                        keep = krow <= (qt * tq + lax.broadcasted_iota(jnp.int32, (1, tq), 1))
                    pts, dsts = [], []
                    for h in heads:
                        pt = jnp.exp(sts[h] - lse_ref[0, 0, h:h + 1, qr])
                        if masked:
                            pt = jnp.where(keep, pt, 0.0)
                        dsts.append((pt * (dpts[h] - delta_ref[h, 0:1, qr])).astype(BF16))
                        pts.append(pt.astype(BF16))
                    out = []
                    for h in heads:
                        hh = h % 2
                        dvt = _dot_nt(do_ts[h][hh * HALF:(hh + 1) * HALF, :], pts[h])
                        dk = _dot(dsts[h], q[:, lanes[h]])
                        dqt_ref[lanes[h], qr] += _dot(k_t[lanes[h], :], dsts[h])
                        out += [carry[2 * h] + dk, carry[2 * h + 1] + dvt]
                    return tuple(out)
                return q_step

            init = (jnp.zeros((tq, HEAD_LANES), F32), jnp.zeros((HALF, tq), F32)) * nh
            carry = make_step(True)(kt, init)
            carry = lax.fori_loop(kt + 1, nq, make_step(False), carry)
            for h in heads:
                dk_ref[kr, lanes[h]] = carry[2 * h]
            for p in range(nh // 2):
                dv_ref[kr, lanes[p]] = jnp.concatenate([carry[4 * p + 1], carry[4 * p + 3]], axis=0).T
            return 0

        lax.fori_loop(0, nq, k_step, 0)
        for t in range(nq):
            r = slice(t * tq, (t + 1) * tq)
            for h in heads:
                dq_ref[r, lanes[h]] = dqt_ref[lanes[h], r].T

    n_tok = qf.shape[0]
    groups = N_HEADS // nh
    blk = lambda w: pl.BlockSpec((seq, w), lambda b, g: (b, g))
    return pl.pallas_call(
        body, name="mla_bwd", grid=(n_seq, groups),
        out_shape=[jax.ShapeDtypeStruct((n_tok, 1024), F32), jax.ShapeDtypeStruct((n_tok, 1024), F32),
                   jax.ShapeDtypeStruct((n_tok, 512), F32)],
        in_specs=[blk(512), blk(512), blk(256), blk(256), blk(256),
                  pl.BlockSpec((1, 1, nh, seq), lambda b, g: (b, g, 0, 0))],
        out_specs=[blk(512), blk(512), blk(256)],
        scratch_shapes=[pltpu.VMEM((nh * HEAD_LANES, seq), BF16), pltpu.VMEM((nh, HEAD_LANES, seq), BF16),
                        pltpu.VMEM((nh, 8, seq), F32), pltpu.VMEM((nh * HEAD_LANES, seq), F32)],
        compiler_params=_params(2),
    )(qf, kf, v, do, o, lse)


def _swa_block(n, pos_col_ref, pos_row_ref):
    w = SWA_WINDOW
    start = pl.multiple_of(jnp.maximum(n - 1, 0) * w, w)
    posq = pos_col_ref[...]
    posk = pos_row_ref[0, :, pl.ds(start, 2 * w)]
    dist = posq - posk
    rel = (n * w + lax.broadcasted_iota(jnp.int32, (w, 1), 0)) - (start + lax.broadcasted_iota(jnp.int32, (1, 2 * w), 1))
    valid = jnp.logical_and(rel >= 0, rel < w)
    return start, dist, valid


def _swa_fwd_call(qs, kd, vd, pos_col, pos_row, sinks, n_seq, seq):
    w = SWA_WINDOW
    nb = seq // w

    def body(q_ref, k_ref, v_ref, pc_ref, pr_ref, sink_ref, o_ref, lse_ref):
        n = pl.program_id(1)
        lo = _lane_lo()
        hi = jnp.logical_not(lo)
        start, dist, valid = _swa_block(n, pc_ref, pr_ref)
        for j in range(N_HEADS // 2):
            kvl = slice((j // 2) * HEAD_LANES, (j // 2 + 1) * HEAD_LANES)
            qp = q_ref[:, j * HEAD_LANES:(j + 1) * HEAD_LANES]
            kk = k_ref[pl.ds(start, 2 * w), kvl]
            vv = v_ref[pl.ds(start, 2 * w), kvl]
            o_pair = jnp.zeros((w, HEAD_LANES), F32)
            for hh in range(2):
                h = 2 * j + hh
                half = lo if hh == 0 else hi
                qh = jnp.where(half, qp, jnp.zeros_like(qp))
                s = _dot_nt(qh, kk) - (2.0 ** -(h + 1)) * dist
                s = jnp.where(valid, s, NEG)
                sink = sink_ref[0, h]
                m = jnp.maximum(jnp.max(s, axis=-1, keepdims=True), sink)
                p = jnp.exp(s - m)
                l = jnp.sum(p, axis=-1, keepdims=True) + jnp.exp(sink - m)
                pn = (p * (1.0 / l)).astype(BF16)
                o_pair = o_pair + _dot(pn, jnp.where(half, vv, jnp.zeros_like(vv)))
                lse_ref[:, h * HEAD_LANES:(h + 1) * HEAD_LANES] = jnp.broadcast_to(m + jnp.log(l), (w, HEAD_LANES))
            o_ref[:, j * HEAD_LANES:(j + 1) * HEAD_LANES] = o_pair

    n_tok = qs.shape[0]
    tok = lambda width: pl.BlockSpec((w, width), lambda b, n: (b * nb + n, 0))
    whole = lambda width: pl.BlockSpec((seq, width), lambda b, n: (b, 0))
    return pl.pallas_call(
        body, name="swa_fwd", grid=(n_seq, nb),
        out_shape=[jax.ShapeDtypeStruct((n_tok, 512), F32), jax.ShapeDtypeStruct((n_tok, 1024), F32)],
        in_specs=[tok(512), whole(256), whole(256), tok(1), pl.BlockSpec((1, 1, seq), lambda b, n: (b, 0, 0)),
                  pl.BlockSpec(memory_space=pltpu.SMEM)],
        out_specs=[tok(512), tok(1024)],
        compiler_params=_params(2),
    )(qs, kd, vd, pos_col, pos_row, sinks)


def _swa_bwd_call(qs, kd, vd, do, o, lse, pos_col, pos_row, sinks, n_seq, seq):
    w = SWA_WINDOW
    nb = seq // w

    def body(q_ref, k_ref, v_ref, do_ref, o_ref, lse_ref, pc_ref, pr_ref, sink_ref, dq_ref, dk_ref, dv_ref, dsink_ref):
        b, n = pl.program_id(0), pl.program_id(1)
        lo = _lane_lo()
        hi = jnp.logical_not(lo)

        @pl.when(n == 0)
        def _():
            dk_ref[...] = jnp.zeros_like(dk_ref)
            dv_ref[...] = jnp.zeros_like(dv_ref)

        @pl.when(jnp.logical_and(n == 0, b == 0))
        def _():
            dsink_ref[...] = jnp.zeros_like(dsink_ref)

        start, dist, valid = _swa_block(n, pc_ref, pr_ref)
        win = pl.ds(start, 2 * w)
        for j in range(N_HEADS // 2):
            pair = slice(j * HEAD_LANES, (j + 1) * HEAD_LANES)
            kvl = slice((j // 2) * HEAD_LANES, (j // 2 + 1) * HEAD_LANES)
            qp = q_ref[:, pair]
            dop = do_ref[:, pair]
            prod = dop.astype(F32) * o_ref[:, pair]
            kk = k_ref[win, kvl]
            vv = v_ref[win, kvl]
            dq_pair = jnp.zeros((w, HEAD_LANES), F32)
            dk_acc = jnp.zeros((2 * w, HEAD_LANES), F32)
            dv_acc = jnp.zeros((2 * w, HEAD_LANES), F32)
            for hh in range(2):
                h = 2 * j + hh
                half = lo if hh == 0 else hi
                qh = jnp.where(half, qp, jnp.zeros_like(qp))
                doh = jnp.where(half, dop, jnp.zeros_like(dop))
                delta = jnp.sum(jnp.where(half, prod, 0.0), axis=-1, keepdims=True)
                lse_h = lse_ref[:, h * HEAD_LANES:h * HEAD_LANES + 1]
                s = _dot_nt(qh, kk) - (2.0 ** -(h + 1)) * dist
                p = jnp.where(valid, jnp.exp(s - lse_h), 0.0)
                dv_acc = dv_acc + _dot_tn(p.astype(BF16), doh)
                dp = _dot_nt(doh, vv)
                ds = (p * (dp - delta)).astype(BF16)
                dq_pair = dq_pair + _dot(ds, jnp.where(half, kk, jnp.zeros_like(kk)))
                dk_acc = dk_acc + _dot_tn(ds, qh)
                p_sink = jnp.exp(sink_ref[0, h] - lse_h)
                dsink_ref[h:h + 1, :] += jnp.broadcast_to(-jnp.sum(p_sink * delta, axis=0, keepdims=True), (1, HEAD_LANES))
            dq_ref[:, pair] = dq_pair * SWA_SCALE
            dk_ref[win, kvl] += dk_acc
            dv_ref[win, kvl] += dv_acc

    n_tok = qs.shape[0]
    tok = lambda width: pl.BlockSpec((w, width), lambda b, n: (b * nb + n, 0))
    whole = lambda width: pl.BlockSpec((seq, width), lambda b, n: (b, 0))
    return pl.pallas_call(
        body, name="swa_bwd", grid=(n_seq, nb),
        out_shape=[jax.ShapeDtypeStruct((n_tok, 512), F32), jax.ShapeDtypeStruct((n_tok, 256), F32),
                   jax.ShapeDtypeStruct((n_tok, 256), F32), jax.ShapeDtypeStruct((N_HEADS, HEAD_LANES), F32)],
        in_specs=[tok(512), whole(256), whole(256), tok(512), tok(512), tok(1024), tok(1),
                  pl.BlockSpec((1, 1, seq), lambda b, n: (b, 0, 0)), pl.BlockSpec(memory_space=pltpu.SMEM)],
        out_specs=[tok(512), whole(256), whole(256), _full((N_HEADS, HEAD_LANES))],
        compiler_params=_params(2),
    )(qs, kd, vd, do, o, lse, pos_col, pos_row, sinks)


def _post_call(x, target, o_mla, o_swa, gates, mod, b_ada, fg, w_out, w_out_t, seq):
    n_tok = x.shape[0]
    tm = min(TOKEN_TILE, seq)
    per_seq = seq // tm
    n_seq = n_tok // seq

    def body(x_ref, t_ref, om_ref, os_ref, g_ref, mod_ref, bada_ref, fg_ref, w_ref, wt_ref,
             dx2_ref, do_ref, dg_ref, gw_ref, gfg_ref, dgate_ref, loss_ref):
        i = pl.program_id(0)

        @pl.when(i == 0)
        def _():
            gw_ref[...] = jnp.zeros_like(gw_ref)
            gfg_ref[...] = jnp.zeros_like(gfg_ref)
            loss_ref[...] = jnp.zeros_like(loss_ref)

        @pl.when(i % per_seq == 0)
        def _():
            dgate_ref[...] = jnp.zeros_like(dgate_ref)

        gate = mod_ref[0][:, 2 * D_MODEL:] + bada_ref[:, 2 * D_MODEL:]
        g = g_ref[...]
        o = jnp.concatenate([om_ref[...], os_ref[...]], axis=-1)
        sg = _sigmoid(g)
        sil = g * sg
        ypre = (o * sil).astype(BF16)
        y = _dot(ypre, w_ref[...])
        x2 = x_ref[...] + gate * y
        r2 = lax.rsqrt(jnp.mean(x2 * x2, axis=-1, keepdims=True) + EPS)
        xn2 = x2 * r2
        fgv = fg_ref[...]
        err = xn2 * fgv - t_ref[...]
        e2 = jnp.sum(err * err, axis=-1, keepdims=True)
        loss_ref[...] += jnp.broadcast_to(jnp.sum(e2, axis=0, keepdims=True) * (0.5 / D_MODEL), loss_ref.shape)
        dout = err * (1.0 / D_MODEL)
        gfg_ref[...] += jnp.sum(dout * xn2, axis=0, keepdims=True)
        dxn2 = dout * fgv
        dx2 = r2 * (dxn2 - xn2 * jnp.mean(dxn2 * xn2, axis=-1, keepdims=True))
        dx2_ref[...] = dx2
        dgate_ref[0] += jnp.sum(dx2 * y, axis=0, keepdims=True)
        dy = (dx2 * gate).astype(BF16)
        gw_ref[...] += _dot_tn(ypre, dy)
        dypre = _dot(dy, wt_ref[...])
        do_ref[...] = (dypre * sil).astype(BF16)
        dg_ref[...] = (dypre * o * (sg * (1.0 + g * (1.0 - sg)))).astype(BF16)

    tok = lambda w: pl.BlockSpec((tm, w), lambda i: (i, 0))
    per_b = pl.BlockSpec((1, 1, 3 * D_MODEL), lambda i: (i // per_seq, 0, 0))
    return pl.pallas_call(
        body, name="post", grid=(n_tok // tm,),
        out_shape=[jax.ShapeDtypeStruct((n_tok, D_MODEL), F32), jax.ShapeDtypeStruct((n_tok, D_MODEL), BF16),
                   jax.ShapeDtypeStruct((n_tok, D_MODEL), BF16), jax.ShapeDtypeStruct((D_MODEL, D_MODEL), F32),
                   jax.ShapeDtypeStruct((1, D_MODEL), F32), jax.ShapeDtypeStruct((n_seq, 1, D_MODEL), F32),
                   jax.ShapeDtypeStruct((1, HEAD_LANES), F32)],
        in_specs=[tok(D_MODEL), tok(D_MODEL), tok(512), tok(512), tok(D_MODEL), per_b, _full(b_ada.shape),
                  _full(fg.shape), _full(w_out.shape), _full(w_out_t.shape)],
        out_specs=[tok(D_MODEL), tok(D_MODEL), tok(D_MODEL), _full((D_MODEL, D_MODEL)), _full((1, D_MODEL)),
                   pl.BlockSpec((1, 1, D_MODEL), lambda i: (i // per_seq, 0, 0)), _full((1, HEAD_LANES))],
        compiler_params=_params(1),
    )(x, target, o_mla, o_swa, gates, mod, b_ada, fg, w_out, w_out_t)


def _mid_bwd_call(dqf, dkf, dv, zqkv, pos_col, qg, kvg, inv128, wq2, wkv, seq):
    n_tok = dqf.shape[0]
    tm = min(TOKEN_TILE, seq)

    def body(dq_ref, dk_ref, dv_ref, z_ref, pos_ref, qg_ref, kvg_ref, inv_ref, wq_ref, wkv_ref,
             dz_ref, dkr_ref, gwq_ref, gwkv_ref, gqg_ref, gkvg_ref):
        i = pl.program_id(0)

        @pl.when(i == 0)
        def _():
            gwq_ref[...] = jnp.zeros_like(gwq_ref)
            gwkv_ref[...] = jnp.zeros_like(gwkv_ref)
            gqg_ref[...] = jnp.zeros_like(gqg_ref)
            gkvg_ref[...] = jnp.zeros_like(gkvg_ref)

        cos, sin = _rope_tables(pos_ref[...], inv_ref[...])
        cf, sf = jnp.tile(cos, (1, N_HEADS)), jnp.tile(sin, (1, N_HEADS))
        dq = dq_ref[...] * MLA_SCALE
        dqr = jnp.concatenate([dq * cf, dq * sf], axis=-1).astype(BF16)
        zq, zkv = z_ref[:, :Q_LORA], z_ref[:, Q_LORA:]
        qgv, kvgv = qg_ref[...], kvg_ref[...]

        rq = lax.rsqrt(jnp.mean(zq * zq, axis=-1, keepdims=True) + EPS)
        xq = zq * rq
        gwq_ref[...] += _dot_tn((xq * qgv).astype(BF16), dqr)
        dqn = _dot_nt(dqr, wq_ref[...])
        gqg_ref[...] += jnp.sum(dqn * xq, axis=0, keepdims=True)
        dxq = dqn * qgv
        dz_ref[:, :Q_LORA] = (rq * (dxq - xq * jnp.mean(dxq * xq, axis=-1, keepdims=True))).astype(BF16)

        dk = dk_ref[...]
        dkv = jnp.concatenate([dk, dv_ref[...]], axis=-1).astype(BF16)
        rkv = lax.rsqrt(jnp.mean(zkv * zkv, axis=-1, keepdims=True) + EPS)
        xkv = zkv * rkv
        gwkv_ref[...] += _dot_tn((xkv * kvgv).astype(BF16), dkv)
        dkvn = _dot_nt(dkv, wkv_ref[...])
        gkvg_ref[...] += jnp.sum(dkvn * xkv, axis=0, keepdims=True)
        dxkv = dkvn * kvgv
        dz_ref[:, Q_LORA:] = (rkv * (dxkv - xkv * jnp.mean(dxkv * xkv, axis=-1, keepdims=True))).astype(BF16)

        dkpe = dk[:, :HEAD_LANES]
        for h in range(1, N_HEADS):
            dkpe = dkpe + dk[:, h * HEAD_LANES:(h + 1) * HEAD_LANES]
        dkr_ref[:, :HEAD_LANES] = (dkpe * cos).astype(BF16)
        dkr_ref[:, HEAD_LANES:] = (dkpe * sin).astype(BF16)

    tok = lambda w: pl.BlockSpec((tm, w), lambda i: (i, 0))
    return pl.pallas_call(
        body, name="mid_bwd", grid=(n_tok // tm,),
        out_shape=[jax.ShapeDtypeStruct((n_tok, 640), BF16), jax.ShapeDtypeStruct((n_tok, 256), BF16),
                   jax.ShapeDtypeStruct(wq2.shape, F32), jax.ShapeDtypeStruct(wkv.shape, F32),
                   jax.ShapeDtypeStruct((1, Q_LORA), F32), jax.ShapeDtypeStruct((1, KV_LORA), F32)],
        in_specs=[tok(1024), tok(1024), tok(512), tok(640), tok(1), _full(qg.shape), _full(kvg.shape),
                  _full(inv128.shape), _full(wq2.shape), _full(wkv.shape)],
        out_specs=[tok(640), tok(256), _full(wq2.shape), _full(wkv.shape), _full((1, Q_LORA)), _full((1, KV_LORA))],
        compiler_params=_params(1),
    )(dqf, dkf, dv, zqkv, pos_col, qg, kvg, inv128, wq2, wkv)


def _in_bwd_call(x, dx2, dz, dkr, dg, dqs, dkd, dvd, mod, b_ada, ng, wa_t, wkr2_t, seq):
    n_tok = x.shape[0]
    tm = min(TOKEN_TILE, seq)
    per_seq = seq // tm
    n_seq = n_tok // seq

    def body(x_ref, dx2_ref, dz_ref, dkr_ref, dg_ref, dqs_ref, dkd_ref, dvd_ref, mod_ref, bada_ref, ng_ref,
             wat_ref, wkrt_ref, gx_ref, gwa_ref, gwkr_ref, gng_ref, dshift_ref, dscale_ref):
        i = pl.program_id(0)

        @pl.when(i == 0)
        def _():
            gwa_ref[...] = jnp.zeros_like(gwa_ref)
            gwkr_ref[...] = jnp.zeros_like(gwkr_ref)
            gng_ref[...] = jnp.zeros_like(gng_ref)

        @pl.when(i % per_seq == 0)
        def _():
            dshift_ref[...] = jnp.zeros_like(dshift_ref)
            dscale_ref[...] = jnp.zeros_like(dscale_ref)

        xv = x_ref[...]
        modv = mod_ref[0] + bada_ref[...]
        shift, scale = modv[:, :D_MODEL], modv[:, D_MODEL:2 * D_MODEL]
        ngv = ng_ref[...]
        r1 = lax.rsqrt(jnp.mean(xv * xv, axis=-1, keepdims=True) + EPS)
        xn = xv * r1
        hb = ((xn * ngv) * (1.0 + scale) + shift).astype(BF16)

        dgv = dg_ref[...]
        pieces = [(A_ZQ, dz_ref[...]), (A_GM, dgv[:, :512]), (A_QS, dqs_ref[...].astype(BF16)),
                  (A_KD, dkd_ref[...].astype(BF16)), (A_VD, dvd_ref[...].astype(BF16)), (A_GS, dgv[:, 512:])]
        dkr = dkr_ref[...]
        gwkr_ref[...] += _dot_tn(hb, dkr)
        dh = _dot(dkr, wkrt_ref[...])
        for off, piece in pieces:
            wd = piece.shape[1]
            gwa_ref[:, off:off + wd] += _dot_tn(hb, piece)
            dh = dh + _dot(piece, wat_ref[off:off + wd, :])

        dshift_ref[0] += jnp.sum(dh, axis=0, keepdims=True)
        dscale_ref[0] += jnp.sum(dh * (xn * ngv), axis=0, keepdims=True)
        gng_ref[...] += jnp.sum(dh * xn * (1.0 + scale), axis=0, keepdims=True)
        dxn = dh * ngv * (1.0 + scale)
        gx_ref[...] = dx2_ref[...] + r1 * (dxn - xn * jnp.mean(dxn * xn, axis=-1, keepdims=True))

    tok = lambda w: pl.BlockSpec((tm, w), lambda i: (i, 0))
    per_b = lambda w: pl.BlockSpec((1, 1, w), lambda i: (i // per_seq, 0, 0))
    return pl.pallas_call(
        body, name="in_bwd", grid=(n_tok // tm,),
        out_shape=[jax.ShapeDtypeStruct((n_tok, D_MODEL), F32), jax.ShapeDtypeStruct((D_MODEL, A_END), F32),
                   jax.ShapeDtypeStruct((D_MODEL, 256), F32), jax.ShapeDtypeStruct((1, D_MODEL), F32),
                   jax.ShapeDtypeStruct((n_seq, 1, D_MODEL), F32), jax.ShapeDtypeStruct((n_seq, 1, D_MODEL), F32)],
        in_specs=[tok(D_MODEL), tok(D_MODEL), tok(640), tok(256), tok(D_MODEL), tok(512), tok(256), tok(256),
                  per_b(3 * D_MODEL), _full(b_ada.shape), _full(ng.shape), _full(wa_t.shape), _full(wkr2_t.shape)],
        out_specs=[tok(D_MODEL), _full((D_MODEL, A_END)), _full((D_MODEL, 256)), _full((1, D_MODEL)),
                   per_b(D_MODEL), per_b(D_MODEL)],
        compiler_params=_params(1),
    )(x, dx2, dz, dkr, dg, dqs, dkd, dvd, mod, b_ada, ng, wa_t, wkr2_t)


def _adam_math(w, g, m, v):
    m_new = ADAM_B1 * m + (1.0 - ADAM_B1) * g
    v_new = ADAM_B2 * v + (1.0 - ADAM_B2) * (g * g)
    m_hat = m_new / (1.0 - ADAM_B1 ** ADAM_STEP)
    v_hat = v_new / (1.0 - ADAM_B2 ** ADAM_STEP)
    delta = -ADAM_LR * (m_hat / (jnp.sqrt(v_hat) + ADAM_EPS) + ADAM_WD * w)
    return delta, m_new, v_new


def _adam_call(name, w, g, m, v):
    rows, cols = w.shape
    tr = 256 if rows % 256 == 0 else rows

    def body(w_ref, g_ref, m_ref, v_ref, d_ref, mo_ref, vo_ref):
        d, mn, vn = _adam_math(w_ref[...], g_ref[...], m_ref[...], v_ref[...])
        d_ref[...] = d
        mo_ref[...] = mn
        vo_ref[...] = vn

    spec = pl.BlockSpec((tr, cols), lambda i: (i, 0))
    return pl.pallas_call(
        body, name=name, grid=(rows // tr,),
        out_shape=[jax.ShapeDtypeStruct(w.shape, F32)] * 3,
        in_specs=[spec] * 4, out_specs=[spec] * 3,
        compiler_params=_params(1),
    )(w, g, m, v)


def _ada_bwd_call(act_all, dmod_cols, w, m, v):
    rows, cols = w.shape
    tr = 256

    def body(a_ref, dm_ref, w_ref, m_ref, v_ref, g_ref, d_ref, mo_ref, vo_ref):
        g = _dot_tn(a_ref[...].astype(BF16), dm_ref[...].astype(BF16))
        d, mn, vn = _adam_math(w_ref[...], g, m_ref[...], v_ref[...])
        g_ref[...] = g
        d_ref[...] = d
        mo_ref[...] = mn
        vo_ref[...] = vn

    spec = pl.BlockSpec((tr, cols), lambda i: (i, 0))
    nb = act_all.shape[0]
    return pl.pallas_call(
        body, name="ada_bwd", grid=(rows // tr,),
        out_shape=[jax.ShapeDtypeStruct(w.shape, F32)] * 4,
        in_specs=[pl.BlockSpec((nb, tr), lambda i: (0, i)), _full(dmod_cols.shape), spec, spec, spec],
        out_specs=[spec] * 4,
        compiler_params=_params(1),
    )(act_all, dmod_cols, w, m, v)


SMALL_ROW = {"norm_gain": (0, 1024), "final_gain": (1024, 2048), "q_norm_gain": (2048, 2432),
             "kv_norm_gain": (2432, 2688), "swa_sinks": (2688, 2696), "loss": (2816, 2944)}
SMALL_ORDER = ("b_ada", "norm_gain", "q_norm_gain", "kv_norm_gain", "swa_sinks", "final_gain")


def _small_call(parts_all, n_seq, params):
    k = len(params)

    def body(p_ref, *refs):
        ins, outs, loss_ref = refs[:3 * k], refs[3 * k:7 * k], refs[7 * k]
        row = p_ref[n_seq:n_seq + 1, :]
        for dv in range(1, 8):
            r0 = dv * ROWS_PER_DEVICE + n_seq
            row = row + p_ref[r0:r0 + 1, :]
        gb = None
        for dv in range(8):
            for r in range(n_seq):
                r0 = dv * ROWS_PER_DEVICE + r
                gb = p_ref[r0:r0 + 1, :] if gb is None else gb + p_ref[r0:r0 + 1, :]
        for j, name in enumerate(SMALL_ORDER):
            g = gb if name == "b_ada" else row[:, SMALL_ROW[name][0]:SMALL_ROW[name][1]]
            d, mn, vn = _adam_math(ins[3 * j][...], g, ins[3 * j + 1][...], ins[3 * j + 2][...])
            outs[4 * j][...] = g
            outs[4 * j + 1][...] = d
            outs[4 * j + 2][...] = mn
            outs[4 * j + 3][...] = vn
        loss_ref[...] = row[:, SMALL_ROW["loss"][0]:SMALL_ROW["loss"][1]]

    flat = [t for p in params for t in p]
    res = pl.pallas_call(
        body, name="small_update", grid=(1,),
        out_shape=[jax.ShapeDtypeStruct(p[0].shape, F32) for p in params for _ in range(4)]
        + [jax.ShapeDtypeStruct((1, HEAD_LANES), F32)],
        in_specs=[_full(parts_all.shape)] + [_full(t.shape) for t in flat],
        out_specs=[_full(p[0].shape) for p in params for _ in range(4)] + [_full((1, HEAD_LANES))],
        compiler_params=_params(1),
    )(parts_all, *flat)
    return [res[4 * j:4 * j + 4] for j in range(k)], res[4 * k]


def _rot(t):
    half = t.shape[-1] // 2
    return jnp.concatenate([-t[..., half:], t[..., :half]], axis=-1)


def _rot_t(g):
    half = g.shape[-1] // 2
    return jnp.concatenate([g[..., half:], -g[..., :half]], axis=-1)


def _prepare_weights(w_in, w_uq, w_ukv):
    o = [0]
    for s in IN_SPLITS:
        o.append(o[-1] + s)
    ks, vs = w_in[:, o[5]:o[6]], w_in[:, o[6]:o[7]]
    dup = lambda t: jnp.concatenate([t[:, :64], t[:, :64], t[:, 64:], t[:, 64:]], axis=1)
    wa = jnp.concatenate([w_in[:, :o[2]], w_in[:, o[3]:o[5]], dup(ks), dup(vs), w_in[:, o[7]:]], axis=1)
    kr = w_in[:, o[2]:o[3]]
    zc = lambda n: jnp.zeros((w_in.shape[0], n), w_in.dtype)
    wkr2 = jnp.concatenate([zc(64), kr, zc(32), zc(64), _rot(kr), zc(32)], axis=1)
    uq = w_uq.reshape(Q_LORA, N_HEADS, MLA_NOPE + MLA_ROPE)
    zq = jnp.zeros((Q_LORA, N_HEADS, 32), w_uq.dtype)
    uq_full = jnp.concatenate([uq, zq], axis=-1).reshape(Q_LORA, 1024)
    uq_rot = jnp.concatenate([jnp.zeros((Q_LORA, N_HEADS, 64), w_uq.dtype), _rot(uq[..., MLA_NOPE:]), zq],
                             axis=-1).reshape(Q_LORA, 1024)
    wq2 = jnp.concatenate([uq_full, uq_rot], axis=1)
    ukv = w_ukv.reshape(KV_LORA, N_HEADS, 128)
    k_full = jnp.concatenate([ukv[..., :64], jnp.zeros((KV_LORA, N_HEADS, 64), w_ukv.dtype)], axis=-1).reshape(KV_LORA, 1024)
    wkv = jnp.concatenate([k_full, ukv[..., 64:].reshape(KV_LORA, 512)], axis=1)
    return wa, wkr2, wq2, wkv


def _restore_grads(gwa, gwkr2, gwq2, gwkv):
    fold = lambda g: jnp.concatenate([g[:, 0:64] + g[:, 64:128], g[:, 128:192] + g[:, 192:256]], axis=1)
    gkr = gwkr2[:, 64:96] + _rot_t(gwkr2[:, 192:224])
    g_in = jnp.concatenate([gwa[:, :A_GM], gkr, gwa[:, A_GM:A_KD], fold(gwa[:, A_KD:A_VD]), fold(gwa[:, A_VD:A_GS]),
                            gwa[:, A_GS:]], axis=1)
    gf = gwq2[:, :1024].reshape(Q_LORA, N_HEADS, 128)
    gr = gwq2[:, 1024:].reshape(Q_LORA, N_HEADS, 128)
    g_uq = jnp.concatenate([gf[..., :64], gf[..., 64:96] + _rot_t(gr[..., 64:96])], axis=-1).reshape(Q_LORA, 768)
    gk = gwkv[:, :1024].reshape(KV_LORA, N_HEADS, 128)[..., :64]
    gv = gwkv[:, 1024:].reshape(KV_LORA, N_HEADS, 64)
    g_ukv = jnp.concatenate([gk, gv], axis=-1).reshape(KV_LORA, 1024)
    return g_in, g_uq, g_ukv


def _local_step(x, positions, target, mod_rows, b_ada, ng, qg, kvg, sinks, fg, w_in_b, w_uq_b, w_ukv_b, w_out_b):
    n_seq, seq, _ = x.shape
    n_tok = n_seq * seq
    x2d = x.reshape(n_tok, D_MODEL)
    t2d = target.reshape(n_tok, D_MODEL)
    pos_f = positions.astype(F32)
    pos_col = pos_f.reshape(n_tok, 1)
    pos_row = pos_f.reshape(n_seq, 1, seq)
    mod3 = mod_rows.reshape(n_seq, 1, 3 * D_MODEL)
    inv = ROPE_THETA ** (-jnp.arange(0, MLA_ROPE, 2, dtype=F32) / MLA_ROPE)
    inv128 = jnp.concatenate([jnp.zeros((64,), F32), inv, inv, jnp.zeros((32,), F32)]).reshape(1, 128)
    fg2 = fg.reshape(1, D_MODEL)

    wa, wkr2, wq2, wkv = _prepare_weights(w_in_b, w_uq_b, w_ukv_b)

    zqkv, gates, qf, kf, v, qs, kd, vd = _pre_call(x2d, pos_col, mod3, b_ada, ng, qg, kvg, inv128, wa, wkr2, wq2, wkv, seq)
    o_mla, lse_mla = _mla_fwd_call(qf, kf, v, n_seq, seq)
    o_swa, lse_swa = _swa_fwd_call(qs, kd, vd, pos_col, pos_row, sinks, n_seq, seq)
    dx2, do, dg, g_out, g_fg, dgate, loss = _post_call(x2d, t2d, o_mla, o_swa, gates, mod3, b_ada, fg2, w_out_b, w_out_b.T, seq)
    do_mla, do_swa = do[:, :512], do[:, 512:]
    dqf, dkf, dv = _mla_bwd_call(qf, kf, v, do_mla, o_mla, lse_mla, n_seq, seq)
    dqs, dkd, dvd, dsink = _swa_bwd_call(qs, kd, vd, do_swa, o_swa, lse_swa, pos_col, pos_row, sinks, n_seq, seq)
    dz, dkr, g_wq2, g_wkv, g_qg, g_kvg = _mid_bwd_call(dqf, dkf, dv, zqkv, pos_col, qg, kvg, inv128, wq2, wkv, seq)
    gx, g_wa, g_wkr2, g_ng, dshift, dscale = _in_bwd_call(x2d, dx2, dz, dkr, dg, dqs, dkd, dvd, mod3, b_ada, ng,
                                                         wa.T, wkr2.T, seq)
    g_in, g_uq, g_ukv = _restore_grads(g_wa, g_wkr2, g_wq2, g_wkv)
    dmod = jnp.concatenate([dshift, dscale, dgate], axis=-1).reshape(n_seq, 3 * D_MODEL)
    small_row = jnp.concatenate([g_ng, g_fg, g_qg, g_kvg, jnp.pad(dsink[:, 0].reshape(1, N_HEADS), ((0, 0), (0, 120))),
                                 loss, jnp.zeros((1, 128), F32)], axis=1)
    return gx.reshape(x.shape), (g_in, g_uq, g_ukv, g_out), small_row, dmod


def kernel(x, c, positions, w_ada, b_ada, norm_gain, w_in, q_norm_gain, kv_norm_gain, w_uq, w_ukv, swa_sinks, w_out, final_gain, loss_target, m_w_ada, m_b_ada, m_norm_gain, m_w_in, m_q_norm_gain, m_kv_norm_gain, m_w_uq, m_w_ukv, m_swa_sinks, m_w_out, m_final_gain, v_w_ada, v_b_ada, v_norm_gain, v_w_in, v_q_norm_gain, v_kv_norm_gain, v_w_uq, v_w_ukv, v_swa_sinks, v_w_out, v_final_gain):
    n_seq = x.shape[0]
    xi, yi, ci = lax.axis_index("x"), lax.axis_index("y"), lax.axis_index("c")
    dev = 4 * xi + 2 * yi + ci
    chip = 2 * xi + yi

    halves = lambda w: w.astype(BF16).reshape(2, w.shape[0] // 2, w.shape[1])
    c_blk = jnp.pad(c, ((0, ROWS_PER_DEVICE - n_seq), (0, 0)))
    act_all, pieces, f_in, f_uq, f_ukv, f_out = _comm_fwd_call(
        c_blk, w_ada[0], [halves(w_in[0]), halves(w_uq[0]), halves(w_ukv[0]), halves(w_out[0])])
    mine = lax.dynamic_slice_in_dim(pieces, dev * ROWS_PER_DEVICE, n_seq, axis=1)
    mod_rows = jnp.transpose(mine, (1, 0, 2)).reshape(n_seq, 3 * D_MODEL)
    cols = lambda t, r: jnp.transpose(t.reshape(4, r, -1), (1, 0, 2)).reshape(r, -1)
    w_in_b, w_uq_b, w_ukv_b = cols(f_in, D_MODEL), cols(f_uq, Q_LORA), cols(f_ukv, KV_LORA)
    w_out_b = f_out.reshape(D_MODEL, D_MODEL)

    gx, (g_in, g_uq, g_ukv, g_out), small_row, dmod = _local_step(
        x, positions, loss_target, mod_rows, b_ada, norm_gain, q_norm_gain, kv_norm_gain, swa_sinks, final_gain,
        w_in_b, w_uq_b, w_ukv_b, w_out_b)

    by_owner = lambda g, n: jnp.transpose(g.reshape(2, g.shape[0] // 2, 4, n), (0, 2, 1, 3))
    grads = [by_owner(g_in, 616), by_owner(g_uq, 192), by_owner(g_ukv, 256),
             jnp.transpose(g_out.reshape(4, 2, 128, D_MODEL), (1, 0, 2, 3))]
    part = jnp.concatenate([dmod, small_row, jnp.zeros((ROWS_PER_DEVICE - n_seq - 1, 3 * D_MODEL), F32)], axis=0)
    r_in, r_uq, r_ukv, r_out, parts_all = _comm_bwd_call(grads, part)
    g_in_s, g_uq_s = r_in.reshape(w_in.shape[1:]), r_uq.reshape(w_uq.shape[1:])
    g_ukv_s, g_out_s = r_ukv.reshape(w_ukv.shape[1:]), r_out.reshape(w_out.shape[1:])

    d_in, nm_in, nv_in = _adam_call("adam_w_in", w_in[0], g_in_s, m_w_in[0], v_w_in[0])
    d_uq, nm_uq, nv_uq = _adam_call("adam_w_uq", w_uq[0], g_uq_s, m_w_uq[0], v_w_uq[0])
    d_ukv, nm_ukv, nv_ukv = _adam_call("adam_w_ukv", w_ukv[0], g_ukv_s, m_w_ukv[0], v_w_ukv[0])
    d_out, nm_out, nv_out = _adam_call("adam_w_out", w_out[0], g_out_s, m_w_out[0], v_w_out[0])
    dmod_cols = lax.dynamic_slice_in_dim(parts_all, chip * 768, 768, axis=1)
    g_ada, d_ada, nm_ada, nv_ada = _ada_bwd_call(act_all, dmod_cols, w_ada[0], m_w_ada[0], v_w_ada[0])

    row = lambda t: t.reshape(1, -1)
    small = {"b_ada": (b_ada, m_b_ada, v_b_ada), "norm_gain": (norm_gain, m_norm_gain, v_norm_gain),
             "q_norm_gain": (q_norm_gain, m_q_norm_gain, v_q_norm_gain),
             "kv_norm_gain": (kv_norm_gain, m_kv_norm_gain, v_kv_norm_gain),
             "swa_sinks": (swa_sinks, m_swa_sinks, v_swa_sinks),
             "final_gain": (row(final_gain), row(m_final_gain), row(v_final_gain))}
    res, loss_row = _small_call(parts_all, n_seq, [small[name] for name in SMALL_ORDER])
    res = dict(zip(SMALL_ORDER, res))
    res["final_gain"] = [t.reshape(-1) for t in res["final_gain"]]
    e = lambda t: t[None]
    big = {"w_ada": (e(g_ada), e(d_ada), e(nm_ada), e(nv_ada)), "w_in": (e(g_in_s), e(d_in), e(nm_in), e(nv_in)),
           "w_uq": (e(g_uq_s), e(d_uq), e(nm_uq), e(nv_uq)), "w_ukv": (e(g_ukv_s), e(d_ukv), e(nm_ukv), e(nv_ukv)),
           "w_out": (e(g_out_s), e(d_out), e(nm_out), e(nv_out))}
    order = ("w_ada", "b_ada", "norm_gain", "w_in", "q_norm_gain", "kv_norm_gain", "w_uq", "w_ukv", "swa_sinks", "w_out",
             "final_gain")
    pick = lambda kind: [(big[n] if n in big else res[n])[kind] for n in order]
    return (loss_row[0, 0], gx, *pick(0), *pick(1), *pick(2), *pick(3))
```

```python
import functools

import jax
import jax.numpy as jnp
from jax import lax
from jax.experimental import pallas as pl
from jax.experimental.pallas import tpu as pltpu

F32 = jnp.float32
BF16 = jnp.bfloat16

D_MODEL = 1024
Q_LORA = 384
KV_LORA = 256
N_HEADS = 8
MLA_NOPE = 64
MLA_ROPE = 32
HEAD_LANES = 128
HALF = 64
SWA_WINDOW = 128
EPS = 1e-6
ROPE_THETA = 10000.0
MLA_SCALE = (MLA_NOPE + MLA_ROPE) ** -0.5
SWA_SCALE = 64 ** -0.5
NEG = -1e30

ADAM_LR = 0.001
ADAM_B1 = 0.9
ADAM_B2 = 0.999
ADAM_EPS = 1e-08
ADAM_WD = 0.01
ADAM_STEP = 10

A_ZQ, A_ZKV, A_GM, A_QS, A_KD, A_VD, A_GS, A_END = 0, 384, 640, 1152, 1664, 1920, 2176, 2688
IN_SPLITS = (384, 256, 32, 512, 512, 128, 128, 512)
D_IN = sum(IN_SPLITS)

TOKEN_TILE = 256
ATT_TILE = 256
VMEM_LIMIT = 56 * 1024 * 1024


def _dot(a, b):
    return jnp.dot(a, b, preferred_element_type=F32)


def _dot_nt(a, b):
    return lax.dot_general(a, b, (((1,), (1,)), ((), ())), preferred_element_type=F32)


def _dot_tn(a, b):
    return lax.dot_general(a, b, (((0,), (0,)), ((), ())), preferred_element_type=F32)


def _params(n_grid):
    return pltpu.CompilerParams(dimension_semantics=("arbitrary",) * n_grid, vmem_limit_bytes=VMEM_LIMIT)


def _full(shape):
    nd = len(shape)
    return pl.BlockSpec(shape, lambda *_: (0,) * nd)


def _sigmoid(g):
    return 1.0 / (1.0 + jnp.exp(-g))


MESH = pl.DeviceIdType.MESH
ROWS_PER_DEVICE = 8
VMEM_SPEC = pl.BlockSpec(memory_space=pltpu.VMEM)
ANY_SPEC = pl.BlockSpec(memory_space=pl.ANY)


def _position():
    x, y, c = lax.axis_index("x"), lax.axis_index("y"), lax.axis_index("c")
    sibling = (x, y, 1 - c)
    others = [(1 - x, y, c), (x, 1 - y, c), (1 - x, 1 - y, c)]
    return (x, y, c), 4 * x + 2 * y + c, 2 * x + y, sibling, others


def _rows_of(dev):
    return pl.ds(pl.multiple_of(dev * ROWS_PER_DEVICE, ROWS_PER_DEVICE), ROWS_PER_DEVICE)


def _all_to_all_rows(block_ref, table_ref, dev, me, send_sems, recv_sems):
    x, y, c = me
    waits = []
    for k in range(1, 8):
        peer = (1 - x if k & 4 else x, 1 - y if k & 2 else y, 1 - c if k & 1 else c)
        pltpu.make_async_remote_copy(src_ref=block_ref, dst_ref=table_ref.at[_rows_of(dev)], send_sem=send_sems.at[k - 1],
                                     recv_sem=recv_sems.at[k - 1], device_id=peer, device_id_type=MESH).start()
        waits.append(pltpu.make_async_remote_copy(
            src_ref=block_ref, dst_ref=table_ref.at[_rows_of(jnp.bitwise_xor(dev, k))], send_sem=send_sems.at[k - 1],
            recv_sem=recv_sems.at[k - 1], device_id=peer, device_id_type=MESH))
    return waits


def _comm_fwd_call(c_blk, w_ada, shards):
    n = len(shards)

    def body(c_ref, wada_ref, *refs):
        w_refs, act_ref, pieces_ref, full_refs = refs[:n], refs[n], refs[n + 1], refs[n + 2:2 * n + 2]
        c_all_ref = refs[2 * n + 2]
        c_send, c_recv, p_send, p_recv, w_send, w_recv, f_send, f_recv, loc_sem = refs[2 * n + 3:]
        me, dev, chip, sibling, others = _position()
        core = me[2]
        chip_of = [2 * p[0] + p[1] for p in others]

        local = [pltpu.make_async_copy(w_refs[i], full_refs[i].at[chip], loc_sem.at[i]) for i in range(n)]
        for cp in local:
            cp.start()

        def over_ici(i, j, src_chip):
            return pltpu.make_async_remote_copy(
                src_ref=w_refs[i].at[core], dst_ref=full_refs[i].at[src_chip, core], send_sem=w_send.at[3 * i + j],
                recv_sem=w_recv.at[3 * i + j], device_id=others[j], device_id_type=MESH)

        def to_sibling(i, j, half):
            return pltpu.make_async_remote_copy(
                src_ref=full_refs[i].at[chip_of[j], half], dst_ref=full_refs[i].at[chip_of[j], half],
                send_sem=f_send.at[3 * i + j], recv_sem=f_recv.at[3 * i + j], device_id=sibling, device_id_type=MESH)

        sent = [over_ici(i, j, chip) for i in range(n) for j in range(3)]
        for cp in sent:
            cp.start()

        c_all_ref[_rows_of(dev), :] = c_ref[...]
        c_waits = _all_to_all_rows(c_ref, c_all_ref, dev, me, c_send, c_recv)
        for cp in c_waits:
            cp.wait()
        cv = c_all_ref[...]
        act = cv * _sigmoid(cv)
        act_ref[...] = act
        pieces_ref[chip] = _dot(act.astype(BF16), wada_ref[...].astype(BF16))
        piece = lambda j, src_chip: pltpu.make_async_remote_copy(
            src_ref=pieces_ref.at[chip], dst_ref=pieces_ref.at[src_chip], send_sem=p_send.at[j], recv_sem=p_recv.at[j],
            device_id=others[j], device_id_type=MESH)
        for j in range(3):
            piece(j, chip).start()
        for j in range(3):
            piece(j, chip).wait_send()
            piece(j, chip_of[j]).wait_recv()

        for i in range(n):
            for j in range(3):
                over_ici(i, j, chip_of[j]).wait_recv()
                to_sibling(i, j, core).start()
        for i in range(n):
            for j in range(3):
                to_sibling(i, j, 1 - core).wait_recv()
                to_sibling(i, j, core).wait_send()
        for cp in sent:
            cp.wait_send()
        for cp in local:
            cp.wait()

    rows = 8 * ROWS_PER_DEVICE
    dma = pltpu.SemaphoreType.DMA
    return pl.pallas_call(
        body, name="comm_fwd",
        out_shape=[jax.ShapeDtypeStruct((rows, D_MODEL), F32), jax.ShapeDtypeStruct((4, rows, w_ada.shape[1]), F32)]
        + [jax.ShapeDtypeStruct((4,) + s.shape, s.dtype) for s in shards],
        in_specs=[VMEM_SPEC, VMEM_SPEC] + [ANY_SPEC] * n,
        out_specs=[VMEM_SPEC, VMEM_SPEC] + [ANY_SPEC] * n,
        scratch_shapes=[pltpu.VMEM((rows, D_MODEL), F32), dma((7,)), dma((7,)), dma((3,)), dma((3,)),
                        dma((3 * n,)), dma((3 * n,)), dma((3 * n,)), dma((3 * n,)), dma((n,))],
        compiler_params=pltpu.CompilerParams(vmem_limit_bytes=VMEM_LIMIT),
    )(c_blk, w_ada, *shards)


def _comm_bwd_call(grads, part):
    n = len(grads)

    def body(part_ref, *refs):
        g_refs, f_refs, parts_ref = refs[:n], refs[n:2 * n], refs[2 * n]
        scratch = refs[2 * n + 1:]
        a_refs, b_refs, p_refs, r_refs = (scratch[k * n:(k + 1) * n] for k in range(4))
        s_send, s_recv, d_send, d_recv, e_send, e_recv, h_send, h_recv, loc_sem = scratch[4 * n:]
        me, dev, chip, sibling, others = _position()
        core = me[2]
        chip_of = [2 * p[0] + p[1] for p in others]

        parts_ref[_rows_of(dev), :] = part_ref[...]
        s_waits = _all_to_all_rows(part_ref, parts_ref, dev, me, s_send, s_recv)

        mine = [pltpu.make_async_copy(g_refs[i].at[core], a_refs[i], loc_sem.at[i]) for i in range(n)]
        swap = [pltpu.make_async_remote_copy(src_ref=g_refs[i].at[1 - core], dst_ref=b_refs[i], send_sem=d_send.at[i],
                                             recv_sem=d_recv.at[i], device_id=sibling, device_id_type=MESH) for i in range(n)]
        for cp in mine + swap:
            cp.start()
        for i in range(n):
            mine[i].wait()
            swap[i].wait()
            for k in range(4):
                s = a_refs[i][k] + b_refs[i][k]
                a_refs[i][k] = s
                p_refs[i][k] = s.astype(BF16)

        cross = [pltpu.make_async_remote_copy(src_ref=p_refs[i].at[chip_of[j]], dst_ref=r_refs[i].at[j],
                                              send_sem=e_send.at[3 * i + j], recv_sem=e_recv.at[3 * i + j],
                                              device_id=others[j], device_id_type=MESH) for i in range(n) for j in range(3)]
        for cp in cross:
            cp.start()
        share = []
        for i in range(n):
            for j in range(3):
                cross[3 * i + j].wait()
            f_refs[i][core] = (a_refs[i][chip] + r_refs[i][0].astype(F32) + r_refs[i][1].astype(F32)
                               + r_refs[i][2].astype(F32))
            cp = pltpu.make_async_remote_copy(src_ref=f_refs[i].at[core], dst_ref=f_refs[i].at[core], send_sem=h_send.at[i],
                                              recv_sem=h_recv.at[i], device_id=sibling, device_id_type=MESH)
            cp.start()
            share.append(cp)
        for i in range(n):
            share[i].wait_send()
            pltpu.make_async_remote_copy(src_ref=f_refs[i].at[core], dst_ref=f_refs[i].at[1 - core], send_sem=h_send.at[i],
                                         recv_sem=h_recv.at[i], device_id=sibling, device_id_type=MESH).wait_recv()
        for cp in s_waits:
            cp.wait()

    rows = 8 * ROWS_PER_DEVICE
    dma = pltpu.SemaphoreType.DMA
    quarter = [g.shape[1:] for g in grads]
    return pl.pallas_call(
        body, name="comm_bwd",
        out_shape=[jax.ShapeDtypeStruct((2,) + g.shape[2:], F32) for g in grads]
        + [jax.ShapeDtypeStruct((rows, part.shape[1]), F32)],
        in_specs=[VMEM_SPEC] + [ANY_SPEC] * n,
        out_specs=[VMEM_SPEC] * (n + 1),
        scratch_shapes=[pltpu.VMEM(q, F32) for q in quarter] + [pltpu.VMEM(q, F32) for q in quarter]
        + [pltpu.VMEM(q, BF16) for q in quarter] + [pltpu.VMEM((3,) + q[1:], BF16) for q in quarter]
        + [dma((7,)), dma((7,)), dma((n,)), dma((n,)), dma((3 * n,)), dma((3 * n,)), dma((n,)), dma((n,)), dma((n,))],
        compiler_params=pltpu.CompilerParams(vmem_limit_bytes=VMEM_LIMIT),
    )(part, *grads)


def _rope_tables(pos_col, inv_row):
    ang = pos_col * inv_row
    return jnp.cos(ang), jnp.sin(ang)


def _pre_call(x, pos_col, mod, b_ada, ng, qg, kvg, inv128, wa, wkr2, wq2, wkv, seq):
    n_tok = x.shape[0]
    tm = min(TOKEN_TILE, seq)
    per_seq = seq // tm

    def body(x_ref, pos_ref, mod_ref, bada_ref, ng_ref, qg_ref, kvg_ref, inv_ref, wa_ref, wkr_ref, wq_ref, wkv_ref,
             zqkv_ref, gates_ref, qf_ref, kf_ref, v_ref, qs_ref, kd_ref, vd_ref):
        xv = x_ref[...]
        modv = mod_ref[0] + bada_ref[...]
        shift, scale = modv[:, :D_MODEL], modv[:, D_MODEL:2 * D_MODEL]
        r1 = lax.rsqrt(jnp.mean(xv * xv, axis=-1, keepdims=True) + EPS)
        h = ((xv * r1) * ng_ref[...]) * (1.0 + scale) + shift
        hb = h.astype(BF16)
        za = _dot(hb, wa_ref[...])
        zkr = _dot(hb, wkr_ref[...])
        cos, sin = _rope_tables(pos_ref[...], inv_ref[...])
        zqkv_ref[...] = za[:, :A_GM]
        gates_ref[:, :512] = za[:, A_GM:A_QS]
        gates_ref[:, 512:] = za[:, A_GS:A_END]
        qs_ref[...] = (za[:, A_QS:A_KD] * SWA_SCALE).astype(BF16)
        kd_ref[...] = za[:, A_KD:A_VD].astype(BF16)
        vd_ref[...] = za[:, A_VD:A_GS].astype(BF16)
        zq, zkv = za[:, A_ZQ:A_ZKV], za[:, A_ZKV:A_GM]
        rq = lax.rsqrt(jnp.mean(zq * zq, axis=-1, keepdims=True) + EPS)
        qn = ((zq * rq) * qg_ref[...]).astype(BF16)
        qr = _dot(qn, wq_ref[...])
        cf, sf = jnp.tile(cos, (1, N_HEADS)), jnp.tile(sin, (1, N_HEADS))
        qf_ref[...] = ((qr[:, :1024] * cf + qr[:, 1024:] * sf) * MLA_SCALE).astype(BF16)
        rkv = lax.rsqrt(jnp.mean(zkv * zkv, axis=-1, keepdims=True) + EPS)
        kvn = ((zkv * rkv) * kvg_ref[...]).astype(BF16)
        kv = _dot(kvn, wkv_ref[...])
        kpe = zkr[:, :128] * cos + zkr[:, 128:] * sin
        kf_ref[...] = (kv[:, :1024] + jnp.tile(kpe, (1, N_HEADS))).astype(BF16)
        v_ref[...] = kv[:, 1024:].astype(BF16)

    tok = lambda w: pl.BlockSpec((tm, w), lambda i: (i, 0))
    outs = [(640, F32), (1024, F32), (1024, BF16), (1024, BF16), (512, BF16), (512, BF16), (256, BF16), (256, BF16)]
    return pl.pallas_call(
        body, name="pre", grid=(n_tok // tm,),
        out_shape=[jax.ShapeDtypeStruct((n_tok, w), dt) for w, dt in outs],
        in_specs=[tok(D_MODEL), tok(1), pl.BlockSpec((1, 1, 3 * D_MODEL), lambda i: (i // per_seq, 0, 0)),
                  _full(b_ada.shape), _full(ng.shape), _full(qg.shape), _full(kvg.shape), _full(inv128.shape),
                  _full(wa.shape), _full(wkr2.shape), _full(wq2.shape), _full(wkv.shape)],
        out_specs=[tok(w) for w, _ in outs],
        compiler_params=_params(1),
    )(x, pos_col, mod, b_ada, ng, qg, kvg, inv128, wa, wkr2, wq2, wkv)


def _lane_lo(width=HEAD_LANES):
    return lax.broadcasted_iota(jnp.int32, (1, width), 1) < HALF


def _eye(n=HEAD_LANES):
    r = lax.broadcasted_iota(jnp.int32, (n, n), 0)
    c = lax.broadcasted_iota(jnp.int32, (n, n), 1)
    return jnp.where(r == c, 1.0, 0.0).astype(BF16)


def _mla_fwd_call(qf, kf, v, n_seq, seq):
    tq = min(ATT_TILE, seq)
    nq = seq // tq

    def body(q_ref, k_ref, v_ref, o_ref, lse_ref, vt_ref):
        i = pl.program_id(1)
        eye = _eye()

        @pl.when(i == 0)
        def _():
            for t in range(nq):
                for p in range(N_HEADS // 2):
                    pair = slice(p * HEAD_LANES, (p + 1) * HEAD_LANES)
                    vt_ref[pair, t * tq:(t + 1) * tq] = _dot_nt(eye, v_ref[t * tq:(t + 1) * tq, pair]).astype(BF16)

        q = q_ref[...]
        qcol = i * tq + lax.broadcasted_iota(jnp.int32, (1, tq), 1)
        heads = range(N_HEADS)
        lanes = [slice(h * HEAD_LANES, (h + 1) * HEAD_LANES) for h in heads]

        def make_step(masked):
            def step(kt, carry):
                start = pl.multiple_of(kt * tq, tq)
                k = k_ref[pl.ds(start, tq), :]
                vt = vt_ref[:, pl.ds(start, tq)]
                sts = [_dot_nt(k[:, lanes[h]], q[:, lanes[h]]) for h in heads]
                if masked:
                    keep = (kt * tq + lax.broadcasted_iota(jnp.int32, (tq, 1), 0)) <= qcol
                    sts = [jnp.where(keep, st, NEG) for st in sts]
                stats, pts = [], []
                for h in heads:
                    m_old, l_old = carry[3 * h], carry[3 * h + 1]
                    m_new = jnp.maximum(m_old, jnp.max(sts[h], axis=0, keepdims=True))
                    pt = jnp.exp(sts[h] - m_new)
                    alpha = jnp.exp(m_old - m_new)
                    stats.append((m_new, alpha * l_old + jnp.sum(pt, axis=0, keepdims=True), alpha))
                    pts.append(pt.astype(BF16))
                pvs = [_dot(vt[h * HALF:(h + 1) * HALF, :], pts[h]) for h in heads]
                out = []
                for h in heads:
                    out += [stats[h][0], stats[h][1], carry[3 * h + 2] * stats[h][2] + pvs[h]]
                return tuple(out)
            return step

        row = lambda val: jnp.full((1, tq), val, F32)
        init = (row(NEG), row(0.0), jnp.zeros((HALF, tq), F32)) * N_HEADS
        carry = lax.fori_loop(0, i, make_step(False), init)
        carry = make_step(True)(i, carry)
        acc_t = jnp.concatenate([carry[3 * h + 2] * (1.0 / carry[3 * h + 1]) for h in heads], axis=0)
        o_ref[...] = acc_t.T
        for h in heads:
            lse_ref[0, h // 4, h % 4:h % 4 + 1, :] = carry[3 * h] + jnp.log(carry[3 * h + 1])

    n_tok = qf.shape[0]
    return pl.pallas_call(
        body, name="mla_fwd", grid=(n_seq, nq),
        out_shape=[jax.ShapeDtypeStruct((n_tok, 512), F32), jax.ShapeDtypeStruct((n_seq, 2, 4, seq), F32)],
        in_specs=[pl.BlockSpec((tq, 1024), lambda b, i: (b * nq + i, 0)),
                  pl.BlockSpec((seq, 1024), lambda b, i: (b, 0)),
                  pl.BlockSpec((seq, 512), lambda b, i: (b, 0))],
        out_specs=[pl.BlockSpec((tq, 512), lambda b, i: (b * nq + i, 0)),
                   pl.BlockSpec((1, 2, 4, tq), lambda b, i: (b, 0, 0, i))],
        scratch_shapes=[pltpu.VMEM((512, seq), BF16)],
        compiler_params=_params(2),
    )(qf, kf, v)


def _mla_bwd_call(qf, kf, v, do, o, lse, n_seq, seq):
    tq = min(ATT_TILE, seq)
    nq = seq // tq

    nh = 4
    heads = range(nh)
    lanes = [slice(h * HEAD_LANES, (h + 1) * HEAD_LANES) for h in heads]

    def body(q_ref, k_ref, v_ref, do_ref, o_ref, lse_ref, dq_ref, dk_ref, dv_ref,
             kt_ref, dot_ref, delta_ref, dqt_ref):
        eye = _eye()
        lo = _lane_lo()
        sub_lo = lax.broadcasted_iota(jnp.int32, (HEAD_LANES, 1), 0) < HALF
        ones_lo = jnp.where(jnp.broadcast_to(lo, (8, HEAD_LANES)), 1.0, 0.0).astype(BF16)
        ones_hi = jnp.where(jnp.broadcast_to(lo, (8, HEAD_LANES)), 0.0, 1.0).astype(BF16)

        for t in range(nq):
            r = slice(t * tq, (t + 1) * tq)
            kv = k_ref[r, :]
            for h in heads:
                kt_ref[lanes[h], r] = _dot_nt(eye, kv[:, lanes[h]]).astype(BF16)
            for p in range(nh // 2):
                dov = do_ref[r, lanes[p]]
                dt = _dot_nt(eye, dov)
                dot_ref[2 * p, :, r] = jnp.where(sub_lo, dt, 0.0).astype(BF16)
                dot_ref[2 * p + 1, :, r] = jnp.where(sub_lo, 0.0, dt).astype(BF16)
                prod = dov.astype(F32) * o_ref[r, lanes[p]]
                p_hi = prod.astype(BF16)
                p_lo = (prod - p_hi.astype(F32)).astype(BF16)
                delta_ref[2 * p, :, r] = _dot_nt(ones_lo, p_hi) + _dot_nt(ones_lo, p_lo)
                delta_ref[2 * p + 1, :, r] = _dot_nt(ones_hi, p_hi) + _dot_nt(ones_hi, p_lo)
        dqt_ref[...] = jnp.zeros_like(dqt_ref)

        def k_step(kt, _):
            kr = pl.ds(pl.multiple_of(kt * tq, tq), tq)
            k = k_ref[kr, :]
            vv = v_ref[kr, :]
            k_t = kt_ref[:, kr]
            krow = kt * tq + lax.broadcasted_iota(jnp.int32, (tq, 1), 0)

            def make_step(masked):
                def q_step(qt, carry):
                    qr = pl.ds(pl.multiple_of(qt * tq, tq), tq)
                    q = q_ref[qr, :]
                    do_ts = [dot_ref[h, :, qr] for h in heads]
                    sts = [_dot_nt(k[:, lanes[h]], q[:, lanes[h]]) for h in heads]
                    dpts = [_dot(vv[:, lanes[h // 2]], do_ts[h]) for h in heads]
                    if masked:
                        keep = krow <= (qt * tq + lax.broadcasted_iota(jnp.int32, (1, tq), 1))
                    pts, dsts = [], []
                    for h in heads:
                        pt = jnp.exp(sts[h] - lse_ref[0, 0, h:h + 1, qr])
                        if masked:
                            pt = jnp.where(keep, pt, 0.0)
                        dsts.append((pt * (dpts[h] - delta_ref[h, 0:1, qr])).astype(BF16))
                        pts.append(pt.astype(BF16))
                    out = []
                    for h in heads:
                        hh = h % 2
                        dvt = _dot_nt(do_ts[h][hh * HALF:(hh + 1) * HALF, :], pts[h])
                        dk = _dot(dsts[h], q[:, lanes[h]])
                        dqt_ref[lanes[h], qr] += _dot(k_t[lanes[h], :], dsts[h])
                        out += [carry[2 * h] + dk, carry[2 * h + 1] + dvt]
                    return tuple(out)
                return q_step

            init = (jnp.zeros((tq, HEAD_LANES), F32), jnp.zeros((HALF, tq), F32)) * nh
            carry = make_step(True)(kt, init)
            carry = lax.fori_loop(kt + 1, nq, make_step(False), carry)
            for h in heads:
                dk_ref[kr, lanes[h]] = carry[2 * h]
            for p in range(nh // 2):
                dv_ref[kr, lanes[p]] = jnp.concatenate([carry[4 * p + 1], carry[4 * p + 3]], axis=0).T
            return 0

        lax.fori_loop(0, nq, k_step, 0)
        for t in range(nq):
            r = slice(t * tq, (t + 1) * tq)
            for h in heads:
                dq_ref[r, lanes[h]] = dqt_ref[lanes[h], r].T

    n_tok = qf.shape[0]
    groups = N_HEADS // nh
    blk = lambda w: pl.BlockSpec((seq, w), lambda b, g: (b, g))
    return pl.pallas_call(
        body, name="mla_bwd", grid=(n_seq, groups),
        out_shape=[jax.ShapeDtypeStruct((n_tok, 1024), F32), jax.ShapeDtypeStruct((n_tok, 1024), F32),
                   jax.ShapeDtypeStruct((n_tok, 512), F32)],
        in_specs=[blk(512), blk(512), blk(256), blk(256), blk(256),
                  pl.BlockSpec((1, 1, nh, seq), lambda b, g: (b, g, 0, 0))],
        out_specs=[blk(512), blk(512), blk(256)],
        scratch_shapes=[pltpu.VMEM((nh * HEAD_LANES, seq), BF16), pltpu.VMEM((nh, HEAD_LANES, seq), BF16),
                        pltpu.VMEM((nh, 8, seq), F32), pltpu.VMEM((nh * HEAD_LANES, seq), F32)],
        compiler_params=_params(2),
    )(qf, kf, v, do, o, lse)


def _swa_block(n, pos_col_ref, pos_row_ref):
    w = SWA_WINDOW
    start = pl.multiple_of(jnp.maximum(n - 1, 0) * w, w)
    posq = pos_row_ref[0]
    posk = pos_col_ref[pl.ds(start, 2 * w), :]
    dist = posq - posk
    rel = (n * w + lax.broadcasted_iota(jnp.int32, (1, w), 1)) - (start + lax.broadcasted_iota(jnp.int32, (2 * w, 1), 0))
    valid = jnp.logical_and(rel >= 0, rel < w)
    return start, dist, valid


def _transpose_rows(eye, src_ref, dst_ref, seq, width):
    step = 2 * SWA_WINDOW
    for t in range(seq // step):
        for p in range(width // HEAD_LANES):
            lanes = slice(p * HEAD_LANES, (p + 1) * HEAD_LANES)
            dst_ref[lanes, t * step:(t + 1) * step] = _dot_nt(eye, src_ref[t * step:(t + 1) * step, lanes]).astype(BF16)


def _swa_fwd_call(qs, kd, vd, pos_col, pos_row, sinks, n_seq, seq):
    w = SWA_WINDOW
    nb = seq // w

    def body(q_ref, k_ref, v_ref, pc_ref, pr_ref, sink_ref, o_ref, lse_ref, vt_ref):
        n = pl.program_id(1)
        lo = _lane_lo()
        hi = jnp.logical_not(lo)
        eye = _eye()

        @pl.when(n == 0)
        def _():
            _transpose_rows(eye, v_ref, vt_ref, seq, 2 * HEAD_LANES)

        start, dist, valid = _swa_block(n, pc_ref, pr_ref)
        win = pl.ds(start, 2 * w)
        heads = range(N_HEADS)
        q = q_ref[...]
        kwin = k_ref[win, :]
        vt = vt_ref[:, win]
        sts = []
        for h in heads:
            qp = q[:, (h // 2) * HEAD_LANES:(h // 2 + 1) * HEAD_LANES]
            qh = jnp.where(lo if h % 2 == 0 else hi, qp, jnp.zeros_like(qp))
            sts.append(_dot_nt(kwin[:, (h // 4) * HEAD_LANES:(h // 4 + 1) * HEAD_LANES], qh))
        pns = []
        for h in heads:
            s = jnp.where(valid, sts[h] - (2.0 ** -(h + 1)) * dist, NEG)
            sink = sink_ref[0, h]
            m = jnp.maximum(jnp.max(s, axis=0, keepdims=True), sink)
            p = jnp.exp(s - m)
            l = jnp.sum(p, axis=0, keepdims=True) + jnp.exp(sink - m)
            pns.append((p * (1.0 / l)).astype(BF16))
            lse_ref[0, h:h + 1, :] = m + jnp.log(l)
        ots = [_dot(vt[(h // 4) * HEAD_LANES:(h // 4) * HEAD_LANES + HALF, :], pns[h]) for h in heads]
        o_ref[...] = jnp.concatenate(ots, axis=0).T

    n_tok = qs.shape[0]
    tok = lambda width: pl.BlockSpec((w, width), lambda b, n: (b * nb + n, 0))
    whole = lambda width: pl.BlockSpec((seq, width), lambda b, n: (b, 0))
    return pl.pallas_call(
        body, name="swa_fwd", grid=(n_seq, nb),
        out_shape=[jax.ShapeDtypeStruct((n_tok, 512), F32), jax.ShapeDtypeStruct((n_seq, N_HEADS, seq), F32)],
        in_specs=[tok(512), whole(256), whole(256), whole(1), pl.BlockSpec((1, 1, w), lambda b, n: (b * nb + n, 0, 0)),
                  pl.BlockSpec(memory_space=pltpu.SMEM)],
        out_specs=[tok(512), pl.BlockSpec((1, N_HEADS, w), lambda b, n: (b, 0, n))],
        scratch_shapes=[pltpu.VMEM((2 * HEAD_LANES, seq), BF16)],
        compiler_params=_params(2),
    )(qs, kd, vd, pos_col, pos_row, sinks)


def _swa_bwd_call(qs, kd, vd, do, o, lse, pos_col, pos_row, sinks, n_seq, seq):
    w = SWA_WINDOW
    nb = seq // w

    def body(q_ref, k_ref, v_ref, do_ref, o_ref, lse_ref, pc_ref, pr_ref, sink_ref, dq_ref, dk_ref, dv_ref, dsink_ref,
             kt_ref):
        b, n = pl.program_id(0), pl.program_id(1)
        lo = _lane_lo()
        hi = jnp.logical_not(lo)
        sub_lo = lax.broadcasted_iota(jnp.int32, (HEAD_LANES, 1), 0) < HALF
        eye = _eye()
        ones_lo = jnp.where(jnp.broadcast_to(lo, (8, HEAD_LANES)), 1.0, 0.0).astype(BF16)
        ones_hi = jnp.where(jnp.broadcast_to(lo, (8, HEAD_LANES)), 0.0, 1.0).astype(BF16)

        @pl.when(n == 0)
        def _():
            dk_ref[...] = jnp.zeros_like(dk_ref)
            dv_ref[...] = jnp.zeros_like(dv_ref)
            _transpose_rows(eye, k_ref, kt_ref, seq, 2 * HEAD_LANES)

        @pl.when(jnp.logical_and(n == 0, b == 0))
        def _():
            dsink_ref[...] = jnp.zeros_like(dsink_ref)

        start, dist, valid = _swa_block(n, pc_ref, pr_ref)
        win = pl.ds(start, 2 * w)
        heads = range(N_HEADS)
        kv_lanes = lambda h: slice((h // 4) * HEAD_LANES, (h // 4 + 1) * HEAD_LANES)
        q, do, o = q_ref[...], do_ref[...], o_ref[...]
        kwin, vwin, ktw = k_ref[win, :], v_ref[win, :], kt_ref[:, win]

        do_ts, deltas, qms, doms = [], [], [], []
        for j in range(N_HEADS // 2):
            pair = slice(j * HEAD_LANES, (j + 1) * HEAD_LANES)
            dop = do[:, pair]
            dt = _dot_nt(eye, dop)
            prod = dop.astype(F32) * o[:, pair]
            p_hi = prod.astype(BF16)
            p_lo = (prod - p_hi.astype(F32)).astype(BF16)
            for hh in range(2):
                half, ones = (lo, ones_lo) if hh == 0 else (hi, ones_hi)
                do_ts.append(jnp.where(sub_lo, dt, 0.0).astype(BF16) if hh == 0 else jnp.where(sub_lo, 0.0, dt).astype(BF16))
                deltas.append((_dot_nt(ones, p_hi) + _dot_nt(ones, p_lo))[0:1, :])
                qms.append(jnp.where(half, q[:, pair], jnp.zeros_like(dop)))
                doms.append(jnp.where(half, dop, jnp.zeros_like(dop)))
        sts = [_dot_nt(kwin[:, kv_lanes(h)], qms[h]) for h in heads]
        dpts = [_dot(vwin[:, kv_lanes(h)], do_ts[h]) for h in heads]
        pts, dsts = [], []
        for h in heads:
            lse_h = lse_ref[0, h:h + 1, :]
            pt = jnp.where(valid, jnp.exp(sts[h] - (2.0 ** -(h + 1)) * dist - lse_h), 0.0)
            dsts.append((pt * (dpts[h] - deltas[h])).astype(BF16))
            pts.append(pt.astype(BF16))
            dsink_ref[h:h + 1, :] += -jnp.exp(sink_ref[0, h] - lse_h) * deltas[h]
        for kv in range(2):
            dk_acc, dv_acc = None, None
            for h in range(4 * kv, 4 * kv + 4):
                dk_h, dv_h = _dot(dsts[h], qms[h]), _dot(pts[h], doms[h])
                dk_acc = dk_h if dk_acc is None else dk_acc + dk_h
                dv_acc = dv_h if dv_acc is None else dv_acc + dv_h
            dk_ref[win, kv_lanes(4 * kv)] += dk_acc
            dv_ref[win, kv_lanes(4 * kv)] += dv_acc
        for j in range(N_HEADS // 2):
            k_t = ktw[kv_lanes(2 * j), :]
            dq_t = jnp.where(sub_lo, _dot(k_t, dsts[2 * j]), _dot(k_t, dsts[2 * j + 1]))
            dq_ref[:, j * HEAD_LANES:(j + 1) * HEAD_LANES] = dq_t.T * SWA_SCALE

    n_tok = qs.shape[0]
    tok = lambda width: pl.BlockSpec((w, width), lambda b, n: (b * nb + n, 0))
    whole = lambda width: pl.BlockSpec((seq, width), lambda b, n: (b, 0))
    return pl.pallas_call(
        body, name="swa_bwd", grid=(n_seq, nb),
        out_shape=[jax.ShapeDtypeStruct((n_tok, 512), F32), jax.ShapeDtypeStruct((n_tok, 256), F32),
                   jax.ShapeDtypeStruct((n_tok, 256), F32), jax.ShapeDtypeStruct((N_HEADS, HEAD_LANES), F32)],
        in_specs=[tok(512), whole(256), whole(256), tok(512), tok(512), pl.BlockSpec((1, N_HEADS, w), lambda b, n: (b, 0, n)),
                  whole(1), pl.BlockSpec((1, 1, w), lambda b, n: (b * nb + n, 0, 0)), pl.BlockSpec(memory_space=pltpu.SMEM)],
        out_specs=[tok(512), whole(256), whole(256), _full((N_HEADS, HEAD_LANES))],
        scratch_shapes=[pltpu.VMEM((2 * HEAD_LANES, seq), BF16)],
        compiler_params=_params(2),
    )(qs, kd, vd, do, o, lse, pos_col, pos_row, sinks)


def _post_call(x, target, o_mla, o_swa, gates, mod, b_ada, fg, w_out, w_out_t, seq):
    n_tok = x.shape[0]
    tm = min(TOKEN_TILE, seq)
    per_seq = seq // tm
    n_seq = n_tok // seq

    def body(x_ref, t_ref, om_ref, os_ref, g_ref, mod_ref, bada_ref, fg_ref, w_ref, wt_ref,
             dx2_ref, do_ref, dg_ref, gw_ref, gfg_ref, dgate_ref, loss_ref):
        i = pl.program_id(0)

        @pl.when(i == 0)
        def _():
            gw_ref[...] = jnp.zeros_like(gw_ref)
            gfg_ref[...] = jnp.zeros_like(gfg_ref)
            loss_ref[...] = jnp.zeros_like(loss_ref)

        @pl.when(i % per_seq == 0)
        def _():
            dgate_ref[...] = jnp.zeros_like(dgate_ref)

        gate = mod_ref[0][:, 2 * D_MODEL:] + bada_ref[:, 2 * D_MODEL:]
        g = g_ref[...]
        o = jnp.concatenate([om_ref[...], os_ref[...]], axis=-1)
        sg = _sigmoid(g)
        sil = g * sg
        ypre = (o * sil).astype(BF16)
        y = _dot(ypre, w_ref[...])
        x2 = x_ref[...] + gate * y
        r2 = lax.rsqrt(jnp.mean(x2 * x2, axis=-1, keepdims=True) + EPS)
        xn2 = x2 * r2
        fgv = fg_ref[...]
        err = xn2 * fgv - t_ref[...]
        e2 = jnp.sum(err * err, axis=-1, keepdims=True)
        loss_ref[...] += jnp.broadcast_to(jnp.sum(e2, axis=0, keepdims=True) * (0.5 / D_MODEL), loss_ref.shape)
        dout = err * (1.0 / D_MODEL)
        gfg_ref[...] += jnp.sum(dout * xn2, axis=0, keepdims=True)
        dxn2 = dout * fgv
        dx2 = r2 * (dxn2 - xn2 * jnp.mean(dxn2 * xn2, axis=-1, keepdims=True))
        dx2_ref[...] = dx2
        dgate_ref[0] += jnp.sum(dx2 * y, axis=0, keepdims=True)
        dy = (dx2 * gate).astype(BF16)
        gw_ref[...] += _dot_tn(ypre, dy)
        dypre = _dot(dy, wt_ref[...])
        do_ref[...] = (dypre * sil).astype(BF16)
        dg_ref[...] = (dypre * o * (sg * (1.0 + g * (1.0 - sg)))).astype(BF16)

    tok = lambda w: pl.BlockSpec((tm, w), lambda i: (i, 0))
    per_b = pl.BlockSpec((1, 1, 3 * D_MODEL), lambda i: (i // per_seq, 0, 0))
    return pl.pallas_call(
        body, name="post", grid=(n_tok // tm,),
        out_shape=[jax.ShapeDtypeStruct((n_tok, D_MODEL), F32), jax.ShapeDtypeStruct((n_tok, D_MODEL), BF16),
                   jax.ShapeDtypeStruct((n_tok, D_MODEL), BF16), jax.ShapeDtypeStruct((D_MODEL, D_MODEL), F32),
                   jax.ShapeDtypeStruct((1, D_MODEL), F32), jax.ShapeDtypeStruct((n_seq, 1, D_MODEL), F32),
                   jax.ShapeDtypeStruct((1, HEAD_LANES), F32)],
        in_specs=[tok(D_MODEL), tok(D_MODEL), tok(512), tok(512), tok(D_MODEL), per_b, _full(b_ada.shape),
                  _full(fg.shape), _full(w_out.shape), _full(w_out_t.shape)],
        out_specs=[tok(D_MODEL), tok(D_MODEL), tok(D_MODEL), _full((D_MODEL, D_MODEL)), _full((1, D_MODEL)),
                   pl.BlockSpec((1, 1, D_MODEL), lambda i: (i // per_seq, 0, 0)), _full((1, HEAD_LANES))],
        compiler_params=_params(1),
    )(x, target, o_mla, o_swa, gates, mod, b_ada, fg, w_out, w_out_t)


def _mid_bwd_call(dqf, dkf, dv, zqkv, pos_col, qg, kvg, inv128, wq2, wkv, seq):
    n_tok = dqf.shape[0]
    tm = min(TOKEN_TILE, seq)

    def body(dq_ref, dk_ref, dv_ref, z_ref, pos_ref, qg_ref, kvg_ref, inv_ref, wq_ref, wkv_ref,
             dz_ref, dkr_ref, gwq_ref, gwkv_ref, gqg_ref, gkvg_ref):
        i = pl.program_id(0)

        @pl.when(i == 0)
        def _():
            gwq_ref[...] = jnp.zeros_like(gwq_ref)
            gwkv_ref[...] = jnp.zeros_like(gwkv_ref)
            gqg_ref[...] = jnp.zeros_like(gqg_ref)
            gkvg_ref[...] = jnp.zeros_like(gkvg_ref)

        cos, sin = _rope_tables(pos_ref[...], inv_ref[...])
        cf, sf = jnp.tile(cos, (1, N_HEADS)), jnp.tile(sin, (1, N_HEADS))
        dq = dq_ref[...] * MLA_SCALE
        dqr = jnp.concatenate([dq * cf, dq * sf], axis=-1).astype(BF16)
        zq, zkv = z_ref[:, :Q_LORA], z_ref[:, Q_LORA:]
        qgv, kvgv = qg_ref[...], kvg_ref[...]

        rq = lax.rsqrt(jnp.mean(zq * zq, axis=-1, keepdims=True) + EPS)
        xq = zq * rq
        gwq_ref[...] += _dot_tn((xq * qgv).astype(BF16), dqr)
        dqn = _dot_nt(dqr, wq_ref[...])
        gqg_ref[...] += jnp.sum(dqn * xq, axis=0, keepdims=True)
        dxq = dqn * qgv
        dz_ref[:, :Q_LORA] = (rq * (dxq - xq * jnp.mean(dxq * xq, axis=-1, keepdims=True))).astype(BF16)

        dk = dk_ref[...]
        dkv = jnp.concatenate([dk, dv_ref[...]], axis=-1).astype(BF16)
        rkv = lax.rsqrt(jnp.mean(zkv * zkv, axis=-1, keepdims=True) + EPS)
        xkv = zkv * rkv
        gwkv_ref[...] += _dot_tn((xkv * kvgv).astype(BF16), dkv)
        dkvn = _dot_nt(dkv, wkv_ref[...])
        gkvg_ref[...] += jnp.sum(dkvn * xkv, axis=0, keepdims=True)
        dxkv = dkvn * kvgv
        dz_ref[:, Q_LORA:] = (rkv * (dxkv - xkv * jnp.mean(dxkv * xkv, axis=-1, keepdims=True))).astype(BF16)

        dkpe = dk[:, :HEAD_LANES]
        for h in range(1, N_HEADS):
            dkpe = dkpe + dk[:, h * HEAD_LANES:(h + 1) * HEAD_LANES]
        dkr_ref[:, :HEAD_LANES] = (dkpe * cos).astype(BF16)
        dkr_ref[:, HEAD_LANES:] = (dkpe * sin).astype(BF16)

    tok = lambda w: pl.BlockSpec((tm, w), lambda i: (i, 0))
    return pl.pallas_call(
        body, name="mid_bwd", grid=(n_tok // tm,),
        out_shape=[jax.ShapeDtypeStruct((n_tok, 640), BF16), jax.ShapeDtypeStruct((n_tok, 256), BF16),
                   jax.ShapeDtypeStruct(wq2.shape, F32), jax.ShapeDtypeStruct(wkv.shape, F32),
                   jax.ShapeDtypeStruct((1, Q_LORA), F32), jax.ShapeDtypeStruct((1, KV_LORA), F32)],
        in_specs=[tok(1024), tok(1024), tok(512), tok(640), tok(1), _full(qg.shape), _full(kvg.shape),
                  _full(inv128.shape), _full(wq2.shape), _full(wkv.shape)],
        out_specs=[tok(640), tok(256), _full(wq2.shape), _full(wkv.shape), _full((1, Q_LORA)), _full((1, KV_LORA))],
        compiler_params=_params(1),
    )(dqf, dkf, dv, zqkv, pos_col, qg, kvg, inv128, wq2, wkv)


def _in_bwd_call(x, dx2, dz, dkr, dg, dqs, dkd, dvd, mod, b_ada, ng, wa_t, wkr2_t, seq):
    n_tok = x.shape[0]
    tm = min(TOKEN_TILE, seq)
    per_seq = seq // tm
    n_seq = n_tok // seq

    def body(x_ref, dx2_ref, dz_ref, dkr_ref, dg_ref, dqs_ref, dkd_ref, dvd_ref, mod_ref, bada_ref, ng_ref,
             wat_ref, wkrt_ref, gx_ref, gwa_ref, gwkr_ref, gng_ref, dshift_ref, dscale_ref):
        i = pl.program_id(0)

        @pl.when(i == 0)
        def _():
            gwa_ref[...] = jnp.zeros_like(gwa_ref)
            gwkr_ref[...] = jnp.zeros_like(gwkr_ref)
            gng_ref[...] = jnp.zeros_like(gng_ref)

        @pl.when(i % per_seq == 0)
        def _():
            dshift_ref[...] = jnp.zeros_like(dshift_ref)
            dscale_ref[...] = jnp.zeros_like(dscale_ref)

        xv = x_ref[...]
        modv = mod_ref[0] + bada_ref[...]
        shift, scale = modv[:, :D_MODEL], modv[:, D_MODEL:2 * D_MODEL]
        ngv = ng_ref[...]
        r1 = lax.rsqrt(jnp.mean(xv * xv, axis=-1, keepdims=True) + EPS)
        xn = xv * r1
        hb = ((xn * ngv) * (1.0 + scale) + shift).astype(BF16)

        dgv = dg_ref[...]
        pieces = [(A_ZQ, dz_ref[...]), (A_GM, dgv[:, :512]), (A_QS, dqs_ref[...].astype(BF16)),
                  (A_KD, dkd_ref[...].astype(BF16)), (A_VD, dvd_ref[...].astype(BF16)), (A_GS, dgv[:, 512:])]
        dkr = dkr_ref[...]
        gwkr_ref[...] += _dot_tn(hb, dkr)
        dh = _dot(dkr, wkrt_ref[...])
        for off, piece in pieces:
            wd = piece.shape[1]
            gwa_ref[:, off:off + wd] += _dot_tn(hb, piece)
            dh = dh + _dot(piece, wat_ref[off:off + wd, :])

        dshift_ref[0] += jnp.sum(dh, axis=0, keepdims=True)
        dscale_ref[0] += jnp.sum(dh * (xn * ngv), axis=0, keepdims=True)
        gng_ref[...] += jnp.sum(dh * xn * (1.0 + scale), axis=0, keepdims=True)
        dxn = dh * ngv * (1.0 + scale)
        gx_ref[...] = dx2_ref[...] + r1 * (dxn - xn * jnp.mean(dxn * xn, axis=-1, keepdims=True))

    tok = lambda w: pl.BlockSpec((tm, w), lambda i: (i, 0))
    per_b = lambda w: pl.BlockSpec((1, 1, w), lambda i: (i // per_seq, 0, 0))
    return pl.pallas_call(
        body, name="in_bwd", grid=(n_tok // tm,),
        out_shape=[jax.ShapeDtypeStruct((n_tok, D_MODEL), F32), jax.ShapeDtypeStruct((D_MODEL, A_END), F32),
                   jax.ShapeDtypeStruct((D_MODEL, 256), F32), jax.ShapeDtypeStruct((1, D_MODEL), F32),
                   jax.ShapeDtypeStruct((n_seq, 1, D_MODEL), F32), jax.ShapeDtypeStruct((n_seq, 1, D_MODEL), F32)],
        in_specs=[tok(D_MODEL), tok(D_MODEL), tok(640), tok(256), tok(D_MODEL), tok(512), tok(256), tok(256),
                  per_b(3 * D_MODEL), _full(b_ada.shape), _full(ng.shape), _full(wa_t.shape), _full(wkr2_t.shape)],
        out_specs=[tok(D_MODEL), _full((D_MODEL, A_END)), _full((D_MODEL, 256)), _full((1, D_MODEL)),
                   per_b(D_MODEL), per_b(D_MODEL)],
        compiler_params=_params(1),
    )(x, dx2, dz, dkr, dg, dqs, dkd, dvd, mod, b_ada, ng, wa_t, wkr2_t)


def _adam_math(w, g, m, v):
    m_new = ADAM_B1 * m + (1.0 - ADAM_B1) * g
    v_new = ADAM_B2 * v + (1.0 - ADAM_B2) * (g * g)
    m_hat = m_new / (1.0 - ADAM_B1 ** ADAM_STEP)
    v_hat = v_new / (1.0 - ADAM_B2 ** ADAM_STEP)
    delta = -ADAM_LR * (m_hat / (jnp.sqrt(v_hat) + ADAM_EPS) + ADAM_WD * w)
    return delta, m_new, v_new


def _adam_call(name, w, g, m, v):
    rows, cols = w.shape
    tr = 256 if rows % 256 == 0 else rows

    def body(w_ref, g_ref, m_ref, v_ref, d_ref, mo_ref, vo_ref):
        d, mn, vn = _adam_math(w_ref[...], g_ref[...], m_ref[...], v_ref[...])
        d_ref[...] = d
        mo_ref[...] = mn
        vo_ref[...] = vn

    spec = pl.BlockSpec((tr, cols), lambda i: (i, 0))
    return pl.pallas_call(
        body, name=name, grid=(rows // tr,),
        out_shape=[jax.ShapeDtypeStruct(w.shape, F32)] * 3,
        in_specs=[spec] * 4, out_specs=[spec] * 3,
        compiler_params=_params(1),
    )(w, g, m, v)


def _ada_bwd_call(act_all, dmod_cols, w, m, v):
    rows, cols = w.shape
    tr = 256

    def body(a_ref, dm_ref, w_ref, m_ref, v_ref, g_ref, d_ref, mo_ref, vo_ref):
        g = _dot_tn(a_ref[...].astype(BF16), dm_ref[...].astype(BF16))
        d, mn, vn = _adam_math(w_ref[...], g, m_ref[...], v_ref[...])
        g_ref[...] = g
        d_ref[...] = d
        mo_ref[...] = mn
        vo_ref[...] = vn

    spec = pl.BlockSpec((tr, cols), lambda i: (i, 0))
    nb = act_all.shape[0]
    return pl.pallas_call(
        body, name="ada_bwd", grid=(rows // tr,),
        out_shape=[jax.ShapeDtypeStruct(w.shape, F32)] * 4,
        in_specs=[pl.BlockSpec((nb, tr), lambda i: (0, i)), _full(dmod_cols.shape), spec, spec, spec],
        out_specs=[spec] * 4,
        compiler_params=_params(1),
    )(act_all, dmod_cols, w, m, v)


SMALL_ROW = {"norm_gain": (0, 1024), "final_gain": (1024, 2048), "q_norm_gain": (2048, 2432),
             "kv_norm_gain": (2432, 2688), "swa_sinks": (2688, 2696), "loss": (2816, 2944)}
SMALL_ORDER = ("b_ada", "norm_gain", "q_norm_gain", "kv_norm_gain", "swa_sinks", "final_gain")


def _small_call(parts_all, n_seq, params):
    k = len(params)

    def body(p_ref, *refs):
        ins, outs, loss_ref = refs[:3 * k], refs[3 * k:7 * k], refs[7 * k]
        row = p_ref[n_seq:n_seq + 1, :]
        for dv in range(1, 8):
            r0 = dv * ROWS_PER_DEVICE + n_seq
            row = row + p_ref[r0:r0 + 1, :]
        gb = None
        for dv in range(8):
            for r in range(n_seq):
                r0 = dv * ROWS_PER_DEVICE + r
                gb = p_ref[r0:r0 + 1, :] if gb is None else gb + p_ref[r0:r0 + 1, :]
        for j, name in enumerate(SMALL_ORDER):
            g = gb if name == "b_ada" else row[:, SMALL_ROW[name][0]:SMALL_ROW[name][1]]
            d, mn, vn = _adam_math(ins[3 * j][...], g, ins[3 * j + 1][...], ins[3 * j + 2][...])
            outs[4 * j][...] = g
            outs[4 * j + 1][...] = d
            outs[4 * j + 2][...] = mn
            outs[4 * j + 3][...] = vn
        loss_ref[...] = row[:, SMALL_ROW["loss"][0]:SMALL_ROW["loss"][1]]

    flat = [t for p in params for t in p]
    res = pl.pallas_call(
        body, name="small_update", grid=(1,),
        out_shape=[jax.ShapeDtypeStruct(p[0].shape, F32) for p in params for _ in range(4)]
        + [jax.ShapeDtypeStruct((1, HEAD_LANES), F32)],
        in_specs=[_full(parts_all.shape)] + [_full(t.shape) for t in flat],
        out_specs=[_full(p[0].shape) for p in params for _ in range(4)] + [_full((1, HEAD_LANES))],
        compiler_params=_params(1),
    )(parts_all, *flat)
    return [res[4 * j:4 * j + 4] for j in range(k)], res[4 * k]


def _rot(t):
    half = t.shape[-1] // 2
    return jnp.concatenate([-t[..., half:], t[..., :half]], axis=-1)


def _rot_t(g):
    half = g.shape[-1] // 2
    return jnp.concatenate([g[..., half:], -g[..., :half]], axis=-1)


def _prepare_weights(w_in, w_uq, w_ukv):
    o = [0]
    for s in IN_SPLITS:
        o.append(o[-1] + s)
    ks, vs = w_in[:, o[5]:o[6]], w_in[:, o[6]:o[7]]
    dup = lambda t: jnp.concatenate([t[:, :64], t[:, :64], t[:, 64:], t[:, 64:]], axis=1)
    wa = jnp.concatenate([w_in[:, :o[2]], w_in[:, o[3]:o[5]], dup(ks), dup(vs), w_in[:, o[7]:]], axis=1)
    kr = w_in[:, o[2]:o[3]]
    zc = lambda n: jnp.zeros((w_in.shape[0], n), w_in.dtype)
    wkr2 = jnp.concatenate([zc(64), kr, zc(32), zc(64), _rot(kr), zc(32)], axis=1)
    uq = w_uq.reshape(Q_LORA, N_HEADS, MLA_NOPE + MLA_ROPE)
    zq = jnp.zeros((Q_LORA, N_HEADS, 32), w_uq.dtype)
    uq_full = jnp.concatenate([uq, zq], axis=-1).reshape(Q_LORA, 1024)
    uq_rot = jnp.concatenate([jnp.zeros((Q_LORA, N_HEADS, 64), w_uq.dtype), _rot(uq[..., MLA_NOPE:]), zq],
                             axis=-1).reshape(Q_LORA, 1024)
    wq2 = jnp.concatenate([uq_full, uq_rot], axis=1)
    ukv = w_ukv.reshape(KV_LORA, N_HEADS, 128)
    k_full = jnp.concatenate([ukv[..., :64], jnp.zeros((KV_LORA, N_HEADS, 64), w_ukv.dtype)], axis=-1).reshape(KV_LORA, 1024)
    wkv = jnp.concatenate([k_full, ukv[..., 64:].reshape(KV_LORA, 512)], axis=1)
    return wa, wkr2, wq2, wkv


def _restore_grads(gwa, gwkr2, gwq2, gwkv):
    fold = lambda g: jnp.concatenate([g[:, 0:64] + g[:, 64:128], g[:, 128:192] + g[:, 192:256]], axis=1)
    gkr = gwkr2[:, 64:96] + _rot_t(gwkr2[:, 192:224])
    g_in = jnp.concatenate([gwa[:, :A_GM], gkr, gwa[:, A_GM:A_KD], fold(gwa[:, A_KD:A_VD]), fold(gwa[:, A_VD:A_GS]),
                            gwa[:, A_GS:]], axis=1)
    gf = gwq2[:, :1024].reshape(Q_LORA, N_HEADS, 128)
    gr = gwq2[:, 1024:].reshape(Q_LORA, N_HEADS, 128)
    g_uq = jnp.concatenate([gf[..., :64], gf[..., 64:96] + _rot_t(gr[..., 64:96])], axis=-1).reshape(Q_LORA, 768)
    gk = gwkv[:, :1024].reshape(KV_LORA, N_HEADS, 128)[..., :64]
    gv = gwkv[:, 1024:].reshape(KV_LORA, N_HEADS, 64)
    g_ukv = jnp.concatenate([gk, gv], axis=-1).reshape(KV_LORA, 1024)
    return g_in, g_uq, g_ukv


def _local_step(x, positions, target, mod_rows, b_ada, ng, qg, kvg, sinks, fg, w_in_b, w_uq_b, w_ukv_b, w_out_b):
    n_seq, seq, _ = x.shape
    n_tok = n_seq * seq
    x2d = x.reshape(n_tok, D_MODEL)
    t2d = target.reshape(n_tok, D_MODEL)
    pos_f = positions.astype(F32)
    pos_col = pos_f.reshape(n_tok, 1)
    pos_row = pos_f.reshape(n_tok // SWA_WINDOW, 1, SWA_WINDOW)
    mod3 = mod_rows.reshape(n_seq, 1, 3 * D_MODEL)
    inv = ROPE_THETA ** (-jnp.arange(0, MLA_ROPE, 2, dtype=F32) / MLA_ROPE)
    inv128 = jnp.concatenate([jnp.zeros((64,), F32), inv, inv, jnp.zeros((32,), F32)]).reshape(1, 128)
    fg2 = fg.reshape(1, D_MODEL)

    wa, wkr2, wq2, wkv = _prepare_weights(w_in_b, w_uq_b, w_ukv_b)

    zqkv, gates, qf, kf, v, qs, kd, vd = _pre_call(x2d, pos_col, mod3, b_ada, ng, qg, kvg, inv128, wa, wkr2, wq2, wkv, seq)
    o_mla, lse_mla = _mla_fwd_call(qf, kf, v, n_seq, seq)
    o_swa, lse_swa = _swa_fwd_call(qs, kd, vd, pos_col, pos_row, sinks, n_seq, seq)
    dx2, do, dg, g_out, g_fg, dgate, loss = _post_call(x2d, t2d, o_mla, o_swa, gates, mod3, b_ada, fg2, w_out_b, w_out_b.T, seq)
    do_mla, do_swa = do[:, :512], do[:, 512:]
    dqf, dkf, dv = _mla_bwd_call(qf, kf, v, do_mla, o_mla, lse_mla, n_seq, seq)
    dqs, dkd, dvd, dsink = _swa_bwd_call(qs, kd, vd, do_swa, o_swa, lse_swa, pos_col, pos_row, sinks, n_seq, seq)
    dz, dkr, g_wq2, g_wkv, g_qg, g_kvg = _mid_bwd_call(dqf, dkf, dv, zqkv, pos_col, qg, kvg, inv128, wq2, wkv, seq)
    gx, g_wa, g_wkr2, g_ng, dshift, dscale = _in_bwd_call(x2d, dx2, dz, dkr, dg, dqs, dkd, dvd, mod3, b_ada, ng,
                                                         wa.T, wkr2.T, seq)
    g_in, g_uq, g_ukv = _restore_grads(g_wa, g_wkr2, g_wq2, g_wkv)
    dmod = jnp.concatenate([dshift, dscale, dgate], axis=-1).reshape(n_seq, 3 * D_MODEL)
    small_row = jnp.concatenate([g_ng, g_fg, g_qg, g_kvg, jnp.pad(jnp.sum(dsink, axis=1).reshape(1, N_HEADS), ((0, 0), (0, 120))),
                                 loss, jnp.zeros((1, 128), F32)], axis=1)
    return gx.reshape(x.shape), (g_in, g_uq, g_ukv, g_out), small_row, dmod


def kernel(x, c, positions, w_ada, b_ada, norm_gain, w_in, q_norm_gain, kv_norm_gain, w_uq, w_ukv, swa_sinks, w_out, final_gain, loss_target, m_w_ada, m_b_ada, m_norm_gain, m_w_in, m_q_norm_gain, m_kv_norm_gain, m_w_uq, m_w_ukv, m_swa_sinks, m_w_out, m_final_gain, v_w_ada, v_b_ada, v_norm_gain, v_w_in, v_q_norm_gain, v_kv_norm_gain, v_w_uq, v_w_ukv, v_swa_sinks, v_w_out, v_final_gain):
    n_seq = x.shape[0]
    xi, yi, ci = lax.axis_index("x"), lax.axis_index("y"), lax.axis_index("c")
    dev = 4 * xi + 2 * yi + ci
    chip = 2 * xi + yi

    halves = lambda w: w.astype(BF16).reshape(2, w.shape[0] // 2, w.shape[1])
    c_blk = jnp.pad(c, ((0, ROWS_PER_DEVICE - n_seq), (0, 0)))
    act_all, pieces, f_in, f_uq, f_ukv, f_out = _comm_fwd_call(
        c_blk, w_ada[0], [halves(w_in[0]), halves(w_uq[0]), halves(w_ukv[0]), halves(w_out[0])])
    mine = lax.dynamic_slice_in_dim(pieces, dev * ROWS_PER_DEVICE, n_seq, axis=1)
    mod_rows = jnp.transpose(mine, (1, 0, 2)).reshape(n_seq, 3 * D_MODEL)
    cols = lambda t, r: jnp.transpose(t.reshape(4, r, -1), (1, 0, 2)).reshape(r, -1)
    w_in_b, w_uq_b, w_ukv_b = cols(f_in, D_MODEL), cols(f_uq, Q_LORA), cols(f_ukv, KV_LORA)
    w_out_b = f_out.reshape(D_MODEL, D_MODEL)

    gx, (g_in, g_uq, g_ukv, g_out), small_row, dmod = _local_step(
        x, positions, loss_target, mod_rows, b_ada, norm_gain, q_norm_gain, kv_norm_gain, swa_sinks, final_gain,
        w_in_b, w_uq_b, w_ukv_b, w_out_b)

    by_owner = lambda g, n: jnp.transpose(g.reshape(2, g.shape[0] // 2, 4, n), (0, 2, 1, 3))
    grads = [by_owner(g_in, 616), by_owner(g_uq, 192), by_owner(g_ukv, 256),
             jnp.transpose(g_out.reshape(4, 2, 128, D_MODEL), (1, 0, 2, 3))]
    part = jnp.concatenate([dmod, small_row, jnp.zeros((ROWS_PER_DEVICE - n_seq - 1, 3 * D_MODEL), F32)], axis=0)
    r_in, r_uq, r_ukv, r_out, parts_all = _comm_bwd_call(grads, part)
    g_in_s, g_uq_s = r_in.reshape(w_in.shape[1:]), r_uq.reshape(w_uq.shape[1:])
    g_ukv_s, g_out_s = r_ukv.reshape(w_ukv.shape[1:]), r_out.reshape(w_out.shape[1:])

    d_in, nm_in, nv_in = _adam_call("adam_w_in", w_in[0], g_in_s, m_w_in[0], v_w_in[0])
    d_uq, nm_uq, nv_uq = _adam_call("adam_w_uq", w_uq[0], g_uq_s, m_w_uq[0], v_w_uq[0])
    d_ukv, nm_ukv, nv_ukv = _adam_call("adam_w_ukv", w_ukv[0], g_ukv_s, m_w_ukv[0], v_w_ukv[0])
    d_out, nm_out, nv_out = _adam_call("adam_w_out", w_out[0], g_out_s, m_w_out[0], v_w_out[0])
    dmod_cols = lax.dynamic_slice_in_dim(parts_all, chip * 768, 768, axis=1)
    g_ada, d_ada, nm_ada, nv_ada = _ada_bwd_call(act_all, dmod_cols, w_ada[0], m_w_ada[0], v_w_ada[0])

    row = lambda t: t.reshape(1, -1)
    small = {"b_ada": (b_ada, m_b_ada, v_b_ada), "norm_gain": (norm_gain, m_norm_gain, v_norm_gain),
             "q_norm_gain": (q_norm_gain, m_q_norm_gain, v_q_norm_gain),
             "kv_norm_gain": (kv_norm_gain, m_kv_norm_gain, v_kv_norm_gain),
             "swa_sinks": (swa_sinks, m_swa_sinks, v_swa_sinks),
             "final_gain": (row(final_gain), row(m_final_gain), row(v_final_gain))}
    res, loss_row = _small_call(parts_all, n_seq, [small[name] for name in SMALL_ORDER])
    res = dict(zip(SMALL_ORDER, res))
    res["final_gain"] = [t.reshape(-1) for t in res["final_gain"]]
    e = lambda t: t[None]
    big = {"w_ada": (e(g_ada), e(d_ada), e(nm_ada), e(nv_ada)), "w_in": (e(g_in_s), e(d_in), e(nm_in), e(nv_in)),
           "w_uq": (e(g_uq_s), e(d_uq), e(nm_uq), e(nv_uq)), "w_ukv": (e(g_ukv_s), e(d_ukv), e(nm_ukv), e(nv_ukv)),
           "w_out": (e(g_out_s), e(d_out), e(nm_out), e(nv_out))}
    order = ("w_ada", "b_ada", "norm_gain", "w_in", "q_norm_gain", "kv_norm_gain", "w_uq", "w_ukv", "swa_sinks", "w_out",
             "final_gain")
    pick = lambda kind: [(big[n] if n in big else res[n])[kind] for n in order]
    return (loss_row[0, 0], gx, *pick(0), *pick(1), *pick(2), *pick(3))
```

```python
import functools

import jax
import jax.numpy as jnp
from jax import lax
from jax.experimental import pallas as pl
from jax.experimental.pallas import tpu as pltpu

F32 = jnp.float32
BF16 = jnp.bfloat16

D_MODEL = 1024
Q_LORA = 384
KV_LORA = 256
N_HEADS = 8
MLA_NOPE = 64
MLA_ROPE = 32
HEAD_LANES = 128
HALF = 64
SWA_WINDOW = 128
EPS = 1e-6
ROPE_THETA = 10000.0
MLA_SCALE = (MLA_NOPE + MLA_ROPE) ** -0.5
LOG2E = 1.4426950408889634
LN2 = 0.6931471805599453
SWA_SCALE = 64 ** -0.5
NEG = -1e30

ADAM_LR = 0.001
ADAM_B1 = 0.9
ADAM_B2 = 0.999
ADAM_EPS = 1e-08
ADAM_WD = 0.01
ADAM_STEP = 10

A_ZQ, A_ZKV, A_GM, A_QS, A_KD, A_VD, A_GS, A_END = 0, 384, 640, 1152, 1664, 1920, 2176, 2688
IN_SPLITS = (384, 256, 32, 512, 512, 128, 128, 512)
D_IN = sum(IN_SPLITS)

TOKEN_TILE = 512
ATT_TILE = 256
VMEM_LIMIT = 56 * 1024 * 1024


def _dot(a, b):
    return jnp.dot(a, b, preferred_element_type=F32)


def _dot_nt(a, b):
    return lax.dot_general(a, b, (((1,), (1,)), ((), ())), preferred_element_type=F32)


def _dot_tn(a, b):
    return lax.dot_general(a, b, (((0,), (0,)), ((), ())), preferred_element_type=F32)


def _params(n_grid):
    return pltpu.CompilerParams(dimension_semantics=("arbitrary",) * n_grid, vmem_limit_bytes=VMEM_LIMIT)


def _full(shape):
    nd = len(shape)
    return pl.BlockSpec(shape, lambda *_: (0,) * nd, pipeline_mode=pl.Buffered(1))


def _sigmoid(g):
    return 1.0 / (1.0 + jnp.exp(-g))


MESH = pl.DeviceIdType.MESH
ROWS_PER_DEVICE = 8
VMEM_SPEC = pl.BlockSpec(memory_space=pltpu.VMEM)
ANY_SPEC = pl.BlockSpec(memory_space=pl.ANY)


def _position():
    x, y, c = lax.axis_index("x"), lax.axis_index("y"), lax.axis_index("c")
    sibling = (x, y, 1 - c)
    others = [(1 - x, y, c), (x, 1 - y, c), (1 - x, 1 - y, c)]
    return (x, y, c), 4 * x + 2 * y + c, 2 * x + y, sibling, others


def _rows_of(dev):
    return pl.ds(pl.multiple_of(dev * ROWS_PER_DEVICE, ROWS_PER_DEVICE), ROWS_PER_DEVICE)


def _all_to_all_rows(block_ref, table_ref, dev, me, send_sems, recv_sems):
    x, y, c = me
    waits = []
    for k in range(1, 8):
        peer = (1 - x if k & 4 else x, 1 - y if k & 2 else y, 1 - c if k & 1 else c)
        pltpu.make_async_remote_copy(src_ref=block_ref, dst_ref=table_ref.at[_rows_of(dev)], send_sem=send_sems.at[k - 1],
                                     recv_sem=recv_sems.at[k - 1], device_id=peer, device_id_type=MESH).start()
        waits.append(pltpu.make_async_remote_copy(
            src_ref=block_ref, dst_ref=table_ref.at[_rows_of(jnp.bitwise_xor(dev, k))], send_sem=send_sems.at[k - 1],
            recv_sem=recv_sems.at[k - 1], device_id=peer, device_id_type=MESH))
    return waits


def _comm_fwd_call(c_blk, w_ada, shards):
    n = len(shards)

    def body(c_ref, wada_ref, *refs):
        w_refs, act_ref, pieces_ref, full_refs = refs[:n], refs[n], refs[n + 1], refs[n + 2:2 * n + 2]
        c_all_ref = refs[2 * n + 2]
        c_send, c_recv, p_send, p_recv, w_send, w_recv, f_send, f_recv, loc_sem = refs[2 * n + 3:]
        me, dev, chip, sibling, others = _position()
        core = me[2]
        chip_of = [2 * p[0] + p[1] for p in others]

        local = [pltpu.make_async_copy(w_refs[i], full_refs[i].at[chip], loc_sem.at[i]) for i in range(n)]
        for cp in local:
            cp.start()

        def over_ici(i, j, src_chip):
            return pltpu.make_async_remote_copy(
                src_ref=w_refs[i].at[core], dst_ref=full_refs[i].at[src_chip, core], send_sem=w_send.at[3 * i + j],
                recv_sem=w_recv.at[3 * i + j], device_id=others[j], device_id_type=MESH)

        def to_sibling(i, j, half):
            return pltpu.make_async_remote_copy(
                src_ref=full_refs[i].at[chip_of[j], half], dst_ref=full_refs[i].at[chip_of[j], half],
                send_sem=f_send.at[3 * i + j], recv_sem=f_recv.at[3 * i + j], device_id=sibling, device_id_type=MESH)

        sent = [over_ici(i, j, chip) for i in range(n) for j in range(3)]
        for cp in sent:
            cp.start()

        c_all_ref[_rows_of(dev), :] = c_ref[...]
        c_waits = _all_to_all_rows(c_ref, c_all_ref, dev, me, c_send, c_recv)
        for cp in c_waits:
            cp.wait()
        cv = c_all_ref[...]
        act = cv * _sigmoid(cv)
        act_ref[...] = act
        pieces_ref[chip] = _dot(act.astype(BF16), wada_ref[...].astype(BF16))
        piece = lambda j, src_chip: pltpu.make_async_remote_copy(
            src_ref=pieces_ref.at[chip], dst_ref=pieces_ref.at[src_chip], send_sem=p_send.at[j], recv_sem=p_recv.at[j],
            device_id=others[j], device_id_type=MESH)
        for j in range(3):
            piece(j, chip).start()
        for j in range(3):
            piece(j, chip).wait_send()
            piece(j, chip_of[j]).wait_recv()

        for i in range(n):
            for j in range(3):
                over_ici(i, j, chip_of[j]).wait_recv()
                to_sibling(i, j, core).start()
        for i in range(n):
            for j in range(3):
                to_sibling(i, j, 1 - core).wait_recv()
                to_sibling(i, j, core).wait_send()
        for cp in sent:
            cp.wait_send()
        for cp in local:
            cp.wait()

    rows = 8 * ROWS_PER_DEVICE
    dma = pltpu.SemaphoreType.DMA
    return pl.pallas_call(
        body, name="comm_fwd",
        out_shape=[jax.ShapeDtypeStruct((rows, D_MODEL), F32), jax.ShapeDtypeStruct((4, rows, w_ada.shape[1]), F32)]
        + [jax.ShapeDtypeStruct((4,) + s.shape, s.dtype) for s in shards],
        in_specs=[VMEM_SPEC, VMEM_SPEC] + [ANY_SPEC] * n,
        out_specs=[VMEM_SPEC, VMEM_SPEC] + [ANY_SPEC] * n,
        scratch_shapes=[pltpu.VMEM((rows, D_MODEL), F32), dma((7,)), dma((7,)), dma((3,)), dma((3,)),
                        dma((3 * n,)), dma((3 * n,)), dma((3 * n,)), dma((3 * n,)), dma((n,))],
        compiler_params=pltpu.CompilerParams(vmem_limit_bytes=VMEM_LIMIT),
    )(c_blk, w_ada, *shards)


def _comm_bwd_call(grads, part):
    n = len(grads)

    def body(part_ref, *refs):
        g_refs, f_refs, parts_ref = refs[:n], refs[n:2 * n], refs[2 * n]
        scratch = refs[2 * n + 1:]
        a_refs, b_refs, p_refs, r_refs = (scratch[k * n:(k + 1) * n] for k in range(4))
        s_send, s_recv, d_send, d_recv, e_send, e_recv, h_send, h_recv, loc_sem = scratch[4 * n:]
        me, dev, chip, sibling, others = _position()
        core = me[2]
        chip_of = [2 * p[0] + p[1] for p in others]

        parts_ref[_rows_of(dev), :] = part_ref[...]
        s_waits = _all_to_all_rows(part_ref, parts_ref, dev, me, s_send, s_recv)

        mine = [pltpu.make_async_copy(g_refs[i].at[core], a_refs[i], loc_sem.at[i]) for i in range(n)]
        swap = [pltpu.make_async_remote_copy(src_ref=g_refs[i].at[1 - core], dst_ref=b_refs[i], send_sem=d_send.at[i],
                                             recv_sem=d_recv.at[i], device_id=sibling, device_id_type=MESH) for i in range(n)]
        for cp in mine + swap:
            cp.start()
        for i in range(n):
            mine[i].wait()
            swap[i].wait()
            for k in range(4):
                s = a_refs[i][k] + b_refs[i][k]
                a_refs[i][k] = s
                p_refs[i][k] = s.astype(BF16)

        cross = [pltpu.make_async_remote_copy(src_ref=p_refs[i].at[chip_of[j]], dst_ref=r_refs[i].at[j],
                                              send_sem=e_send.at[3 * i + j], recv_sem=e_recv.at[3 * i + j],
                                              device_id=others[j], device_id_type=MESH) for i in range(n) for j in range(3)]
        for cp in cross:
            cp.start()
        share = []
        for i in range(n):
            for j in range(3):
                cross[3 * i + j].wait()
            f_refs[i][core] = (a_refs[i][chip] + r_refs[i][0].astype(F32) + r_refs[i][1].astype(F32)
                               + r_refs[i][2].astype(F32))
            cp = pltpu.make_async_remote_copy(src_ref=f_refs[i].at[core], dst_ref=f_refs[i].at[core], send_sem=h_send.at[i],
                                              recv_sem=h_recv.at[i], device_id=sibling, device_id_type=MESH)
            cp.start()
            share.append(cp)
        for i in range(n):
            share[i].wait_send()
            pltpu.make_async_remote_copy(src_ref=f_refs[i].at[core], dst_ref=f_refs[i].at[1 - core], send_sem=h_send.at[i],
                                         recv_sem=h_recv.at[i], device_id=sibling, device_id_type=MESH).wait_recv()
        for cp in s_waits:
            cp.wait()

    rows = 8 * ROWS_PER_DEVICE
    dma = pltpu.SemaphoreType.DMA
    quarter = [g.shape[1:] for g in grads]
    return pl.pallas_call(
        body, name="comm_bwd",
        out_shape=[jax.ShapeDtypeStruct((2,) + g.shape[2:], F32) for g in grads]
        + [jax.ShapeDtypeStruct((rows, part.shape[1]), F32)],
        in_specs=[VMEM_SPEC] + [ANY_SPEC] * n,
        out_specs=[VMEM_SPEC] * (n + 1),
        scratch_shapes=[pltpu.VMEM(q, F32) for q in quarter] + [pltpu.VMEM(q, F32) for q in quarter]
        + [pltpu.VMEM(q, BF16) for q in quarter] + [pltpu.VMEM((3,) + q[1:], BF16) for q in quarter]
        + [dma((7,)), dma((7,)), dma((n,)), dma((n,)), dma((3 * n,)), dma((3 * n,)), dma((n,)), dma((n,)), dma((n,))],
        compiler_params=pltpu.CompilerParams(vmem_limit_bytes=VMEM_LIMIT),
    )(part, *grads)


def _rope_tables(pos_col, inv_row):
    ang = pos_col * inv_row
    return jnp.cos(ang), jnp.sin(ang)


def _pre_call(x, pos_col, mod, b_ada, ng, qg, kvg, inv128, wa, wkr2, wq2, wkv, seq):
    n_tok = x.shape[0]
    tm = min(TOKEN_TILE, seq)
    per_seq = seq // tm

    def body(x_ref, pos_ref, mod_ref, bada_ref, ng_ref, qg_ref, kvg_ref, inv_ref, wa_ref, wkr_ref, wq_ref, wkv_ref,
             zqkv_ref, gates_ref, qf_ref, kf_ref, v_ref, qs_ref, kd_ref, vd_ref):
        xv = x_ref[...]
        modv = mod_ref[0] + bada_ref[...]
        shift, scale = modv[:, :D_MODEL], modv[:, D_MODEL:2 * D_MODEL]
        r1 = lax.rsqrt(jnp.mean(xv * xv, axis=-1, keepdims=True) + EPS)
        h = ((xv * r1) * ng_ref[...]) * (1.0 + scale) + shift
        hb = h.astype(BF16)
        za = _dot(hb, wa_ref[...])
        zkr = _dot(hb, wkr_ref[...])
        cos, sin = _rope_tables(pos_ref[...], inv_ref[...])
        zqkv_ref[...] = za[:, :A_GM]
        gates_ref[:, :512] = za[:, A_GM:A_QS]
        gates_ref[:, 512:] = za[:, A_GS:A_END]
        qs_ref[...] = (za[:, A_QS:A_KD] * SWA_SCALE).astype(BF16)
        kd_ref[...] = za[:, A_KD:A_VD].astype(BF16)
        vd_ref[...] = za[:, A_VD:A_GS].astype(BF16)
        zq, zkv = za[:, A_ZQ:A_ZKV], za[:, A_ZKV:A_GM]
        rq = lax.rsqrt(jnp.mean(zq * zq, axis=-1, keepdims=True) + EPS)
        qn = ((zq * rq) * qg_ref[...]).astype(BF16)
        qr = _dot(qn, wq_ref[...])
        cf, sf = jnp.tile(cos, (1, N_HEADS)), jnp.tile(sin, (1, N_HEADS))
        qf_ref[...] = ((qr[:, :1024] * cf + qr[:, 1024:] * sf) * (MLA_SCALE * LOG2E)).astype(BF16)
        rkv = lax.rsqrt(jnp.mean(zkv * zkv, axis=-1, keepdims=True) + EPS)
        kvn = ((zkv * rkv) * kvg_ref[...]).astype(BF16)
        kv = _dot(kvn, wkv_ref[...])
        kpe = zkr[:, :128] * cos + zkr[:, 128:] * sin
        kf_ref[...] = (kv[:, :1024] + jnp.tile(kpe, (1, N_HEADS))).astype(BF16)
        v_ref[...] = kv[:, 1024:].astype(BF16)

    tok = lambda w: pl.BlockSpec((tm, w), lambda i: (i, 0))
    outs = [(640, F32), (1024, F32), (1024, BF16), (1024, BF16), (512, BF16), (512, BF16), (256, BF16), (256, BF16)]
    return pl.pallas_call(
        body, name="pre", grid=(n_tok // tm,),
        out_shape=[jax.ShapeDtypeStruct((n_tok, w), dt) for w, dt in outs],
        in_specs=[tok(D_MODEL), tok(1), pl.BlockSpec((1, 1, 3 * D_MODEL), lambda i: (i // per_seq, 0, 0)),
                  _full(b_ada.shape), _full(ng.shape), _full(qg.shape), _full(kvg.shape), _full(inv128.shape),
                  _full(wa.shape), _full(wkr2.shape), _full(wq2.shape), _full(wkv.shape)],
        out_specs=[tok(w) for w, _ in outs],
        compiler_params=_params(1),
    )(x, pos_col, mod, b_ada, ng, qg, kvg, inv128, wa, wkr2, wq2, wkv)


def _lane_lo(width=HEAD_LANES):
    return lax.broadcasted_iota(jnp.int32, (1, width), 1) < HALF


def _eye(n=HEAD_LANES):
    r = lax.broadcasted_iota(jnp.int32, (n, n), 0)
    c = lax.broadcasted_iota(jnp.int32, (n, n), 1)
    return jnp.where(r == c, 1.0, 0.0).astype(BF16)


def _mla_fwd_call(qf, kf, v, n_seq, seq):
    tq = min(ATT_TILE, seq)
    nq = seq // tq

    ext = HALF + 16

    def body(q_ref, k_ref, v_ref, o_ref, lse_ref, vt_ref):
        i = pl.program_id(1)
        eye = _eye()

        @pl.when(i == 0)
        def _():
            for h in range(N_HEADS):
                vt_ref[h * ext + HALF:(h + 1) * ext, :] = jnp.ones((16, seq), BF16)
            for t in range(nq):
                for p in range(N_HEADS // 2):
                    pair = slice(p * HEAD_LANES, (p + 1) * HEAD_LANES)
                    v_t = _dot_nt(eye, v_ref[t * tq:(t + 1) * tq, pair]).astype(BF16)
                    for hh in range(2):
                        r0 = (2 * p + hh) * ext
                        vt_ref[r0:r0 + HALF, t * tq:(t + 1) * tq] = v_t[hh * HALF:(hh + 1) * HALF, :]

        q = q_ref[...]
        qcol = i * tq + lax.broadcasted_iota(jnp.int32, (1, tq), 1)
        heads = range(N_HEADS)
        lanes = [slice(h * HEAD_LANES, (h + 1) * HEAD_LANES) for h in heads]

        def make_step(masked):
            def step(kt, carry):
                start = pl.multiple_of(kt * tq, tq)
                k = k_ref[pl.ds(start, tq), :]
                vt = vt_ref[:, pl.ds(start, tq)]
                sts = [_dot_nt(k[:, lanes[h]], q[:, lanes[h]]) for h in heads]
                if masked:
                    keep = (kt * tq + lax.broadcasted_iota(jnp.int32, (tq, 1), 0)) <= qcol
                    sts = [jnp.where(keep, st, NEG) for st in sts]
                stats, pts = [], []
                for h in heads:
                    m_old = carry[2 * h]
                    m_new = jnp.maximum(m_old, jnp.max(sts[h], axis=0, keepdims=True))
                    pts.append(jnp.exp2(sts[h] - m_new).astype(BF16))
                    stats.append((m_new, jnp.exp2(m_old - m_new)))
                pvs = [_dot(vt[h * ext:(h + 1) * ext, :], pts[h]) for h in heads]
                out = []
                for h in heads:
                    out += [stats[h][0], carry[2 * h + 1] * stats[h][1] + pvs[h]]
                return tuple(out)
            return step

        init = (jnp.full((1, tq), NEG, F32), jnp.zeros((ext, tq), F32)) * N_HEADS
        carry = lax.fori_loop(0, i, make_step(False), init)
        carry = make_step(True)(i, carry)
        dens = [carry[2 * h + 1][HALF:HALF + 1, :] for h in heads]
        acc_t = jnp.concatenate([carry[2 * h + 1][:HALF, :] * (1.0 / dens[h]) for h in heads], axis=0)
        o_ref[...] = acc_t.T
        for h in heads:
            lse_ref[0, h // 4, h % 4:h % 4 + 1, :] = carry[2 * h] + jnp.log2(dens[h])

    n_tok = qf.shape[0]
    return pl.pallas_call(
        body, name="mla_fwd", grid=(n_seq, nq),
        out_shape=[jax.ShapeDtypeStruct((n_tok, 512), F32), jax.ShapeDtypeStruct((n_seq, 2, 4, seq), F32)],
        in_specs=[pl.BlockSpec((tq, 1024), lambda b, i: (b * nq + i, 0)),
                  pl.BlockSpec((seq, 1024), lambda b, i: (b, 0)),
                  pl.BlockSpec((seq, 512), lambda b, i: (b, 0))],
        out_specs=[pl.BlockSpec((tq, 512), lambda b, i: (b * nq + i, 0)),
                   pl.BlockSpec((1, 2, 4, tq), lambda b, i: (b, 0, 0, i))],
        scratch_shapes=[pltpu.VMEM((N_HEADS * ext, seq), BF16)],
        compiler_params=_params(2),
    )(qf, kf, v)


def _mla_bwd_call(qf, kf, v, do, o, lse, n_seq, seq):
    tq = min(ATT_TILE, seq)
    nq = seq // tq

    nh = 4
    heads = range(nh)
    lanes = [slice(h * HEAD_LANES, (h + 1) * HEAD_LANES) for h in heads]

    def body(q_ref, k_ref, v_ref, do_ref, o_ref, lse_ref, dq_ref, dk_ref, dv_ref,
             kt_ref, dot_ref, delta_ref, dqt_ref):
        eye = _eye()
        lo = _lane_lo()
        sub_lo = lax.broadcasted_iota(jnp.int32, (HEAD_LANES, 1), 0) < HALF
        ones_lo = jnp.where(jnp.broadcast_to(lo, (8, HEAD_LANES)), 1.0, 0.0).astype(BF16)
        ones_hi = jnp.where(jnp.broadcast_to(lo, (8, HEAD_LANES)), 0.0, 1.0).astype(BF16)

        for t in range(nq):
            r = slice(t * tq, (t + 1) * tq)
            kv = k_ref[r, :]
            for h in heads:
                kt_ref[lanes[h], r] = _dot_nt(eye, kv[:, lanes[h]]).astype(BF16)
            for p in range(nh // 2):
                dov = do_ref[r, lanes[p]]
                dt = _dot_nt(eye, dov)
                dot_ref[2 * p, :, r] = jnp.where(sub_lo, dt, 0.0).astype(BF16)
                dot_ref[2 * p + 1, :, r] = jnp.where(sub_lo, 0.0, dt).astype(BF16)
                prod = dov.astype(F32) * o_ref[r, lanes[p]]
                p_hi = prod.astype(BF16)
                p_lo = (prod - p_hi.astype(F32)).astype(BF16)
                delta_ref[2 * p, :, r] = _dot_nt(ones_lo, p_hi) + _dot_nt(ones_lo, p_lo)
                delta_ref[2 * p + 1, :, r] = _dot_nt(ones_hi, p_hi) + _dot_nt(ones_hi, p_lo)
        dqt_ref[...] = jnp.zeros_like(dqt_ref)

        def k_step(kt, _):
            kr = pl.ds(pl.multiple_of(kt * tq, tq), tq)
            k = k_ref[kr, :]
            vv = v_ref[kr, :]
            k_t = kt_ref[:, kr]
            krow = kt * tq + lax.broadcasted_iota(jnp.int32, (tq, 1), 0)

            def make_step(masked):
                def q_step(qt, carry):
                    qr = pl.ds(pl.multiple_of(qt * tq, tq), tq)
                    q = q_ref[qr, :]
                    do_ts = [dot_ref[h, :, qr] for h in heads]
                    sts = [_dot_nt(k[:, lanes[h]], q[:, lanes[h]]) for h in heads]
                    dpts = [_dot(vv[:, lanes[h // 2]], do_ts[h]) for h in heads]
                    if masked:
                        keep = krow <= (qt * tq + lax.broadcasted_iota(jnp.int32, (1, tq), 1))
                    pts, dsts = [], []
                    for h in heads:
                        pt = jnp.exp2(sts[h] - lse_ref[0, 0, h:h + 1, qr])
                        if masked:
                            pt = jnp.where(keep, pt, 0.0)
                        dsts.append((pt * (dpts[h] - delta_ref[h, 0:1, qr])).astype(BF16))
                        pts.append(pt.astype(BF16))
                    out = []
                    for h in heads:
                        hh = h % 2
                        dvt = _dot_nt(do_ts[h][hh * HALF:(hh + 1) * HALF, :], pts[h])
                        dk = _dot(dsts[h], q[:, lanes[h]])
                        dqt_ref[lanes[h], qr] += _dot(k_t[lanes[h], :], dsts[h])
                        out += [carry[2 * h] + dk, carry[2 * h + 1] + dvt]
                    return tuple(out)
                return q_step

            init = (jnp.zeros((tq, HEAD_LANES), F32), jnp.zeros((HALF, tq), F32)) * nh
            carry = make_step(True)(kt, init)
            carry = lax.fori_loop(kt + 1, nq, make_step(False), carry)
            for h in heads:
                dk_ref[kr, lanes[h]] = carry[2 * h]
            for p in range(nh // 2):
                dv_ref[kr, lanes[p]] = jnp.concatenate([carry[4 * p + 1], carry[4 * p + 3]], axis=0).T
            return 0

        lax.fori_loop(0, nq, k_step, 0)
        for t in range(nq):
            r = slice(t * tq, (t + 1) * tq)
            for h in heads:
                dq_ref[r, lanes[h]] = dqt_ref[lanes[h], r].T

    n_tok = qf.shape[0]
    groups = N_HEADS // nh
    blk = lambda w: pl.BlockSpec((seq, w), lambda b, g: (b, g))
    return pl.pallas_call(
        body, name="mla_bwd", grid=(n_seq, groups),
        out_shape=[jax.ShapeDtypeStruct((n_tok, 1024), F32), jax.ShapeDtypeStruct((n_tok, 1024), F32),
                   jax.ShapeDtypeStruct((n_tok, 512), F32)],
        in_specs=[blk(512), blk(512), blk(256), blk(256), blk(256),
                  pl.BlockSpec((1, 1, nh, seq), lambda b, g: (b, g, 0, 0))],
        out_specs=[blk(512), blk(512), blk(256)],
        scratch_shapes=[pltpu.VMEM((nh * HEAD_LANES, seq), BF16), pltpu.VMEM((nh, HEAD_LANES, seq), BF16),
                        pltpu.VMEM((nh, 8, seq), F32), pltpu.VMEM((nh * HEAD_LANES, seq), F32)],
        compiler_params=_params(2),
    )(qf, kf, v, do, o, lse)


def _swa_block(n, pos_col_ref, pos_row_ref):
    w = SWA_WINDOW
    start = pl.multiple_of(jnp.maximum(n - 1, 0) * w, w)
    posq = pos_row_ref[0]
    posk = pos_col_ref[pl.ds(start, 2 * w), :]
    dist = posq - posk
    rel = (n * w + lax.broadcasted_iota(jnp.int32, (1, w), 1)) - (start + lax.broadcasted_iota(jnp.int32, (2 * w, 1), 0))
    valid = jnp.logical_and(rel >= 0, rel < w)
    return start, dist, valid


def _transpose_rows(eye, src_ref, dst_ref, seq, width):
    step = 2 * SWA_WINDOW
    for t in range(seq // step):
        for p in range(width // HEAD_LANES):
            lanes = slice(p * HEAD_LANES, (p + 1) * HEAD_LANES)
            dst_ref[lanes, t * step:(t + 1) * step] = _dot_nt(eye, src_ref[t * step:(t + 1) * step, lanes]).astype(BF16)


def _swa_fwd_call(qs, kd, vd, pos_col, pos_row, sinks, n_seq, seq):
    w = SWA_WINDOW
    nb = seq // w

    def body(q_ref, k_ref, v_ref, pc_ref, pr_ref, sink_ref, o_ref, lse_ref, vt_ref):
        n = pl.program_id(1)
        lo = _lane_lo()
        hi = jnp.logical_not(lo)
        eye = _eye()

        @pl.when(n == 0)
        def _():
            _transpose_rows(eye, v_ref, vt_ref, seq, 2 * HEAD_LANES)

        start, dist, valid = _swa_block(n, pc_ref, pr_ref)
        win = pl.ds(start, 2 * w)
        heads = range(N_HEADS)
        q = q_ref[...]
        kwin = k_ref[win, :]
        vt = vt_ref[:, win]
        sts = []
        for h in heads:
            qp = q[:, (h // 2) * HEAD_LANES:(h // 2 + 1) * HEAD_LANES]
            qh = jnp.where(lo if h % 2 == 0 else hi, qp, jnp.zeros_like(qp))
            sts.append(_dot_nt(kwin[:, (h // 4) * HEAD_LANES:(h // 4 + 1) * HEAD_LANES], qh))
        pns = []
        for h in heads:
            s = jnp.where(valid, sts[h] - (2.0 ** -(h + 1)) * dist, NEG)
            sink = sink_ref[0, h]
            m = jnp.maximum(jnp.max(s, axis=0, keepdims=True), sink)
            p = jnp.exp(s - m)
            l = jnp.sum(p, axis=0, keepdims=True) + jnp.exp(sink - m)
            pns.append((p * (1.0 / l)).astype(BF16))
            lse_ref[0, h:h + 1, :] = m + jnp.log(l)
        ots = [_dot(vt[(h // 4) * HEAD_LANES:(h // 4) * HEAD_LANES + HALF, :], pns[h]) for h in heads]
        o_ref[...] = jnp.concatenate(ots, axis=0).T

    n_tok = qs.shape[0]
    tok = lambda width: pl.BlockSpec((w, width), lambda b, n: (b * nb + n, 0))
    whole = lambda width: pl.BlockSpec((seq, width), lambda b, n: (b, 0))
    return pl.pallas_call(
        body, name="swa_fwd", grid=(n_seq, nb),
        out_shape=[jax.ShapeDtypeStruct((n_tok, 512), F32), jax.ShapeDtypeStruct((n_seq, N_HEADS, seq), F32)],
        in_specs=[tok(512), whole(256), whole(256), whole(1), pl.BlockSpec((1, 1, w), lambda b, n: (b * nb + n, 0, 0)),
                  pl.BlockSpec(memory_space=pltpu.SMEM)],
        out_specs=[tok(512), pl.BlockSpec((1, N_HEADS, w), lambda b, n: (b, 0, n))],
        scratch_shapes=[pltpu.VMEM((2 * HEAD_LANES, seq), BF16)],
        compiler_params=_params(2),
    )(qs, kd, vd, pos_col, pos_row, sinks)


def _swa_bwd_call(qs, kd, vd, do, o, lse, pos_col, pos_row, sinks, n_seq, seq):
    w = SWA_WINDOW
    nb = seq // w

    def body(q_ref, k_ref, v_ref, do_ref, o_ref, lse_ref, pc_ref, pr_ref, sink_ref, dq_ref, dk_ref, dv_ref, dsink_ref,
             kt_ref):
        b, n = pl.program_id(0), pl.program_id(1)
        lo = _lane_lo()
        hi = jnp.logical_not(lo)
        sub_lo = lax.broadcasted_iota(jnp.int32, (HEAD_LANES, 1), 0) < HALF
        eye = _eye()
        ones_lo = jnp.where(jnp.broadcast_to(lo, (8, HEAD_LANES)), 1.0, 0.0).astype(BF16)
        ones_hi = jnp.where(jnp.broadcast_to(lo, (8, HEAD_LANES)), 0.0, 1.0).astype(BF16)

        @pl.when(n == 0)
        def _():
            dk_ref[...] = jnp.zeros_like(dk_ref)
            dv_ref[...] = jnp.zeros_like(dv_ref)
            _transpose_rows(eye, k_ref, kt_ref, seq, 2 * HEAD_LANES)

        @pl.when(jnp.logical_and(n == 0, b == 0))
        def _():
            dsink_ref[...] = jnp.zeros_like(dsink_ref)

        start, dist, valid = _swa_block(n, pc_ref, pr_ref)
        win = pl.ds(start, 2 * w)
        heads = range(N_HEADS)
        kv_lanes = lambda h: slice((h // 4) * HEAD_LANES, (h // 4 + 1) * HEAD_LANES)
        q, do, o = q_ref[...], do_ref[...], o_ref[...]
        kwin, vwin, ktw = k_ref[win, :], v_ref[win, :], kt_ref[:, win]

        do_ts, deltas, qms, doms = [], [], [], []
        for j in range(N_HEADS // 2):
            pair = slice(j * HEAD_LANES, (j + 1) * HEAD_LANES)
            dop = do[:, pair]
            dt = _dot_nt(eye, dop)
            prod = dop.astype(F32) * o[:, pair]
            p_hi = prod.astype(BF16)
            p_lo = (prod - p_hi.astype(F32)).astype(BF16)
            for hh in range(2):
                half, ones = (lo, ones_lo) if hh == 0 else (hi, ones_hi)
                do_ts.append(jnp.where(sub_lo, dt, 0.0).astype(BF16) if hh == 0 else jnp.where(sub_lo, 0.0, dt).astype(BF16))
                deltas.append((_dot_nt(ones, p_hi) + _dot_nt(ones, p_lo))[0:1, :])
                qms.append(jnp.where(half, q[:, pair], jnp.zeros_like(dop)))
                doms.append(jnp.where(half, dop, jnp.zeros_like(dop)))
        sts = [_dot_nt(kwin[:, kv_lanes(h)], qms[h]) for h in heads]
        dpts = [_dot(vwin[:, kv_lanes(h)], do_ts[h]) for h in heads]
        pts, dsts = [], []
        for h in heads:
            lse_h = lse_ref[0, h:h + 1, :]
            pt = jnp.where(valid, jnp.exp(sts[h] - (2.0 ** -(h + 1)) * dist - lse_h), 0.0)
            dsts.append((pt * (dpts[h] - deltas[h])).astype(BF16))
            pts.append(pt.astype(BF16))
            dsink_ref[h:h + 1, :] += -jnp.exp(sink_ref[0, h] - lse_h) * deltas[h]
        for kv in range(2):
            dk_acc, dv_acc = None, None
            for h in range(4 * kv, 4 * kv + 4):
                dk_h, dv_h = _dot(dsts[h], qms[h]), _dot(pts[h], doms[h])
                dk_acc = dk_h if dk_acc is None else dk_acc + dk_h
                dv_acc = dv_h if dv_acc is None else dv_acc + dv_h
            dk_ref[win, kv_lanes(4 * kv)] += dk_acc
            dv_ref[win, kv_lanes(4 * kv)] += dv_acc
        for j in range(N_HEADS // 2):
            k_t = ktw[kv_lanes(2 * j), :]
            dq_t = jnp.where(sub_lo, _dot(k_t, dsts[2 * j]), _dot(k_t, dsts[2 * j + 1]))
            dq_ref[:, j * HEAD_LANES:(j + 1) * HEAD_LANES] = dq_t.T * SWA_SCALE

    n_tok = qs.shape[0]
    tok = lambda width: pl.BlockSpec((w, width), lambda b, n: (b * nb + n, 0))
    whole = lambda width: pl.BlockSpec((seq, width), lambda b, n: (b, 0))
    return pl.pallas_call(
        body, name="swa_bwd", grid=(n_seq, nb),
        out_shape=[jax.ShapeDtypeStruct((n_tok, 512), F32), jax.ShapeDtypeStruct((n_tok, 256), F32),
                   jax.ShapeDtypeStruct((n_tok, 256), F32), jax.ShapeDtypeStruct((N_HEADS, HEAD_LANES), F32)],
        in_specs=[tok(512), whole(256), whole(256), pl.BlockSpec((w, 512), lambda b, n: (b * nb + n, 1)), tok(512),
                  pl.BlockSpec((1, N_HEADS, w), lambda b, n: (b, 0, n)),
                  whole(1), pl.BlockSpec((1, 1, w), lambda b, n: (b * nb + n, 0, 0)), pl.BlockSpec(memory_space=pltpu.SMEM)],
        out_specs=[tok(512), whole(256), whole(256), _full((N_HEADS, HEAD_LANES))],
        scratch_shapes=[pltpu.VMEM((2 * HEAD_LANES, seq), BF16)],
        compiler_params=_params(2),
    )(qs, kd, vd, do, o, lse, pos_col, pos_row, sinks)


def _post_call(x, target, o_mla, o_swa, gates, mod, b_ada, fg, w_out, w_out_t, seq):
    n_tok = x.shape[0]
    tm = min(TOKEN_TILE, seq)
    per_seq = seq // tm
    n_seq = n_tok // seq

    def body(x_ref, t_ref, om_ref, os_ref, g_ref, mod_ref, bada_ref, fg_ref, w_ref, wt_ref,
             dx2_ref, do_ref, dg_ref, gw_ref, gfg_ref, dgate_ref, loss_ref):
        i = pl.program_id(0)

        @pl.when(i == 0)
        def _():
            gw_ref[...] = jnp.zeros_like(gw_ref)
            gfg_ref[...] = jnp.zeros_like(gfg_ref)
            loss_ref[...] = jnp.zeros_like(loss_ref)

        @pl.when(i % per_seq == 0)
        def _():
            dgate_ref[...] = jnp.zeros_like(dgate_ref)

        gate = mod_ref[0][:, 2 * D_MODEL:] + bada_ref[:, 2 * D_MODEL:]
        g = g_ref[...]
        o = jnp.concatenate([om_ref[...], os_ref[...]], axis=-1)
        sg = _sigmoid(g)
        sil = g * sg
        ypre = (o * sil).astype(BF16)
        y = _dot(ypre, w_ref[...])
        x2 = x_ref[...] + gate * y
        r2 = lax.rsqrt(jnp.mean(x2 * x2, axis=-1, keepdims=True) + EPS)
        xn2 = x2 * r2
        fgv = fg_ref[...]
        err = xn2 * fgv - t_ref[...]
        e2 = jnp.sum(err * err, axis=-1, keepdims=True)
        loss_ref[...] += jnp.broadcast_to(jnp.sum(e2, axis=0, keepdims=True) * (0.5 / D_MODEL), loss_ref.shape)
        dout = err * (1.0 / D_MODEL)
        gfg_ref[...] += jnp.sum(dout * xn2, axis=0, keepdims=True)
        dxn2 = dout * fgv
        dx2 = r2 * (dxn2 - xn2 * jnp.mean(dxn2 * xn2, axis=-1, keepdims=True))
        dx2_ref[...] = dx2
        dgate_ref[0] += jnp.sum(dx2 * y, axis=0, keepdims=True)
        dy = (dx2 * gate).astype(BF16)
        gw_ref[...] += _dot_tn(ypre, dy)
        dypre = _dot(dy, wt_ref[...])
        do_ref[...] = (dypre * sil).astype(BF16)
        dg_ref[...] = (dypre * o * (sg * (1.0 + g * (1.0 - sg)))).astype(BF16)

    tok = lambda w: pl.BlockSpec((tm, w), lambda i: (i, 0))
    per_b = pl.BlockSpec((1, 1, 3 * D_MODEL), lambda i: (i // per_seq, 0, 0))
    return pl.pallas_call(
        body, name="post", grid=(n_tok // tm,),
        out_shape=[jax.ShapeDtypeStruct((n_tok, D_MODEL), F32), jax.ShapeDtypeStruct((n_tok, D_MODEL), BF16),
                   jax.ShapeDtypeStruct((n_tok, D_MODEL), BF16), jax.ShapeDtypeStruct((D_MODEL, D_MODEL), F32),
                   jax.ShapeDtypeStruct((1, D_MODEL), F32), jax.ShapeDtypeStruct((n_seq, 1, D_MODEL), F32),
                   jax.ShapeDtypeStruct((1, HEAD_LANES), F32)],
        in_specs=[tok(D_MODEL), tok(D_MODEL), tok(512), tok(512), tok(D_MODEL), per_b, _full(b_ada.shape),
                  _full(fg.shape), _full(w_out.shape), _full(w_out_t.shape)],
        out_specs=[tok(D_MODEL), tok(D_MODEL), tok(D_MODEL), _full((D_MODEL, D_MODEL)), _full((1, D_MODEL)),
                   pl.BlockSpec((1, 1, D_MODEL), lambda i: (i // per_seq, 0, 0)), _full((1, HEAD_LANES))],
        compiler_params=_params(1),
    )(x, target, o_mla, o_swa, gates, mod, b_ada, fg, w_out, w_out_t)


def _mid_bwd_call(dqf, dkf, dv, zqkv, pos_col, qg, kvg, inv128, wq2, wkv, seq):
    n_tok = dqf.shape[0]
    tm = min(TOKEN_TILE, seq)

    def body(dq_ref, dk_ref, dv_ref, z_ref, pos_ref, qg_ref, kvg_ref, inv_ref, wq_ref, wkv_ref,
             dz_ref, dkr_ref, gwq_ref, gwkv_ref, gqg_ref, gkvg_ref):
        i = pl.program_id(0)

        @pl.when(i == 0)
        def _():
            gwq_ref[...] = jnp.zeros_like(gwq_ref)
            gwkv_ref[...] = jnp.zeros_like(gwkv_ref)
            gqg_ref[...] = jnp.zeros_like(gqg_ref)
            gkvg_ref[...] = jnp.zeros_like(gkvg_ref)

        cos, sin = _rope_tables(pos_ref[...], inv_ref[...])
        cf, sf = jnp.tile(cos, (1, N_HEADS)), jnp.tile(sin, (1, N_HEADS))
        dq = dq_ref[...] * MLA_SCALE
        dqr = jnp.concatenate([dq * cf, dq * sf], axis=-1).astype(BF16)
        zq, zkv = z_ref[:, :Q_LORA], z_ref[:, Q_LORA:]
        qgv, kvgv = qg_ref[...], kvg_ref[...]

        rq = lax.rsqrt(jnp.mean(zq * zq, axis=-1, keepdims=True) + EPS)
        xq = zq * rq
        gwq_ref[...] += _dot_tn((xq * qgv).astype(BF16), dqr)
        dqn = _dot_nt(dqr, wq_ref[...])
        gqg_ref[...] += jnp.sum(dqn * xq, axis=0, keepdims=True)
        dxq = dqn * qgv
        dz_ref[:, :Q_LORA] = (rq * (dxq - xq * jnp.mean(dxq * xq, axis=-1, keepdims=True))).astype(BF16)

        dk = dk_ref[...] * LN2
        dkv = jnp.concatenate([dk, dv_ref[...]], axis=-1).astype(BF16)
        rkv = lax.rsqrt(jnp.mean(zkv * zkv, axis=-1, keepdims=True) + EPS)
        xkv = zkv * rkv
        gwkv_ref[...] += _dot_tn((xkv * kvgv).astype(BF16), dkv)
        dkvn = _dot_nt(dkv, wkv_ref[...])
        gkvg_ref[...] += jnp.sum(dkvn * xkv, axis=0, keepdims=True)
        dxkv = dkvn * kvgv
        dz_ref[:, Q_LORA:] = (rkv * (dxkv - xkv * jnp.mean(dxkv * xkv, axis=-1, keepdims=True))).astype(BF16)

        dkpe = dk[:, :HEAD_LANES]
        for h in range(1, N_HEADS):
            dkpe = dkpe + dk[:, h * HEAD_LANES:(h + 1) * HEAD_LANES]
        dkr_ref[:, :HEAD_LANES] = (dkpe * cos).astype(BF16)
        dkr_ref[:, HEAD_LANES:] = (dkpe * sin).astype(BF16)

    tok = lambda w: pl.BlockSpec((tm, w), lambda i: (i, 0))
    return pl.pallas_call(
        body, name="mid_bwd", grid=(n_tok // tm,),
        out_shape=[jax.ShapeDtypeStruct((n_tok, 640), BF16), jax.ShapeDtypeStruct((n_tok, 256), BF16),
                   jax.ShapeDtypeStruct(wq2.shape, F32), jax.ShapeDtypeStruct(wkv.shape, F32),
                   jax.ShapeDtypeStruct((1, Q_LORA), F32), jax.ShapeDtypeStruct((1, KV_LORA), F32)],
        in_specs=[tok(1024), tok(1024), tok(512), tok(640), tok(1), _full(qg.shape), _full(kvg.shape),
                  _full(inv128.shape), _full(wq2.shape), _full(wkv.shape)],
        out_specs=[tok(640), tok(256), _full(wq2.shape), _full(wkv.shape), _full((1, Q_LORA)), _full((1, KV_LORA))],
        compiler_params=_params(1),
    )(dqf, dkf, dv, zqkv, pos_col, qg, kvg, inv128, wq2, wkv)


def _in_bwd_call(x, dx2, dz, dkr, dg, dqs, dkd, dvd, mod, b_ada, ng, wa_t, wkr2_t, seq):
    n_tok = x.shape[0]
    tm = min(TOKEN_TILE, seq)
    per_seq = seq // tm
    n_seq = n_tok // seq

    def body(x_ref, dx2_ref, dz_ref, dkr_ref, dg_ref, dqs_ref, dkd_ref, dvd_ref, mod_ref, bada_ref, ng_ref,
             wat_ref, wkrt_ref, gx_ref, gwa_ref, gwkr_ref, gng_ref, dshift_ref, dscale_ref):
        i = pl.program_id(0)

        @pl.when(i == 0)
        def _():
            gwa_ref[...] = jnp.zeros_like(gwa_ref)
            gwkr_ref[...] = jnp.zeros_like(gwkr_ref)
            gng_ref[...] = jnp.zeros_like(gng_ref)

        @pl.when(i % per_seq == 0)
        def _():
            dshift_ref[...] = jnp.zeros_like(dshift_ref)
            dscale_ref[...] = jnp.zeros_like(dscale_ref)

        xv = x_ref[...]
        modv = mod_ref[0] + bada_ref[...]
        shift, scale = modv[:, :D_MODEL], modv[:, D_MODEL:2 * D_MODEL]
        ngv = ng_ref[...]
        r1 = lax.rsqrt(jnp.mean(xv * xv, axis=-1, keepdims=True) + EPS)
        xn = xv * r1
        hb = ((xn * ngv) * (1.0 + scale) + shift).astype(BF16)

        dgv = dg_ref[...]
        pieces = [(A_ZQ, dz_ref[...]), (A_GM, dgv[:, :512]), (A_QS, dqs_ref[...].astype(BF16)),
                  (A_KD, dkd_ref[...].astype(BF16)), (A_VD, dvd_ref[...].astype(BF16)), (A_GS, dgv[:, 512:])]
        dkr = dkr_ref[...]
        gwkr_ref[...] += _dot_tn(hb, dkr)
        dh = _dot(dkr, wkrt_ref[...])
        for off, piece in pieces:
            wd = piece.shape[1]
            gwa_ref[:, off:off + wd] += _dot_tn(hb, piece)
            dh = dh + _dot(piece, wat_ref[off:off + wd, :])

        dshift_ref[0] += jnp.sum(dh, axis=0, keepdims=True)
        dscale_ref[0] += jnp.sum(dh * (xn * ngv), axis=0, keepdims=True)
        gng_ref[...] += jnp.sum(dh * xn * (1.0 + scale), axis=0, keepdims=True)
        dxn = dh * ngv * (1.0 + scale)
        gx_ref[...] = dx2_ref[...] + r1 * (dxn - xn * jnp.mean(dxn * xn, axis=-1, keepdims=True))

    tok = lambda w: pl.BlockSpec((tm, w), lambda i: (i, 0))
    per_b = lambda w: pl.BlockSpec((1, 1, w), lambda i: (i // per_seq, 0, 0))
    return pl.pallas_call(
        body, name="in_bwd", grid=(n_tok // tm,),
        out_shape=[jax.ShapeDtypeStruct((n_tok, D_MODEL), F32), jax.ShapeDtypeStruct((D_MODEL, A_END), F32),
                   jax.ShapeDtypeStruct((D_MODEL, 256), F32), jax.ShapeDtypeStruct((1, D_MODEL), F32),
                   jax.ShapeDtypeStruct((n_seq, 1, D_MODEL), F32), jax.ShapeDtypeStruct((n_seq, 1, D_MODEL), F32)],
        in_specs=[tok(D_MODEL), tok(D_MODEL), tok(640), tok(256), tok(D_MODEL), tok(512), tok(256), tok(256),
                  per_b(3 * D_MODEL), _full(b_ada.shape), _full(ng.shape), _full(wa_t.shape), _full(wkr2_t.shape)],
        out_specs=[tok(D_MODEL), _full((D_MODEL, A_END)), _full((D_MODEL, 256)), _full((1, D_MODEL)),
                   per_b(D_MODEL), per_b(D_MODEL)],
        compiler_params=_params(1),
    )(x, dx2, dz, dkr, dg, dqs, dkd, dvd, mod, b_ada, ng, wa_t, wkr2_t)


def _adam_math(w, g, m, v):
    m_new = ADAM_B1 * m + (1.0 - ADAM_B1) * g
    v_new = ADAM_B2 * v + (1.0 - ADAM_B2) * (g * g)
    m_hat = m_new / (1.0 - ADAM_B1 ** ADAM_STEP)
    v_hat = v_new / (1.0 - ADAM_B2 ** ADAM_STEP)
    delta = -ADAM_LR * (m_hat / (jnp.sqrt(v_hat) + ADAM_EPS) + ADAM_WD * w)
    return delta, m_new, v_new


def _adam_call(name, w, g, m, v):
    rows, cols = w.shape
    tr = 256 if rows % 256 == 0 else rows

    def body(w_ref, g_ref, m_ref, v_ref, d_ref, mo_ref, vo_ref):
        d, mn, vn = _adam_math(w_ref[...], g_ref[...], m_ref[...], v_ref[...])
        d_ref[...] = d
        mo_ref[...] = mn
        vo_ref[...] = vn

    spec = pl.BlockSpec((tr, cols), lambda i: (i, 0))
    return pl.pallas_call(
        body, name=name, grid=(rows // tr,),
        out_shape=[jax.ShapeDtypeStruct(w.shape, F32)] * 3,
        in_specs=[spec] * 4, out_specs=[spec] * 3,
        compiler_params=_params(1),
    )(w, g, m, v)


def _ada_bwd_call(act_all, dmod_cols, w, m, v):
    rows, cols = w.shape
    tr = 256

    def body(a_ref, dm_ref, w_ref, m_ref, v_ref, g_ref, d_ref, mo_ref, vo_ref):
        g = _dot_tn(a_ref[...].astype(BF16), dm_ref[...].astype(BF16))
        d, mn, vn = _adam_math(w_ref[...], g, m_ref[...], v_ref[...])
        g_ref[...] = g
        d_ref[...] = d
        mo_ref[...] = mn
        vo_ref[...] = vn

    spec = pl.BlockSpec((tr, cols), lambda i: (i, 0))
    nb = act_all.shape[0]
    return pl.pallas_call(
        body, name="ada_bwd", grid=(rows // tr,),
        out_shape=[jax.ShapeDtypeStruct(w.shape, F32)] * 4,
        in_specs=[pl.BlockSpec((nb, tr), lambda i: (0, i)), _full(dmod_cols.shape), spec, spec, spec],
        out_specs=[spec] * 4,
        compiler_params=_params(1),
    )(act_all, dmod_cols, w, m, v)


SMALL_ROW = {"norm_gain": (0, 1024), "final_gain": (1024, 2048), "q_norm_gain": (2048, 2432),
             "kv_norm_gain": (2432, 2688), "swa_sinks": (2688, 2696), "loss": (2816, 2944)}
SMALL_ORDER = ("b_ada", "norm_gain", "q_norm_gain", "kv_norm_gain", "swa_sinks", "final_gain")


def _small_call(parts_all, n_seq, params):
    k = len(params)

    def body(p_ref, *refs):
        ins, outs, loss_ref = refs[:3 * k], refs[3 * k:7 * k], refs[7 * k]
        row = p_ref[n_seq:n_seq + 1, :]
        for dv in range(1, 8):
            r0 = dv * ROWS_PER_DEVICE + n_seq
            row = row + p_ref[r0:r0 + 1, :]
        gb = None
        for dv in range(8):
            for r in range(n_seq):
                r0 = dv * ROWS_PER_DEVICE + r
                gb = p_ref[r0:r0 + 1, :] if gb is None else gb + p_ref[r0:r0 + 1, :]
        for j, name in enumerate(SMALL_ORDER):
            g = gb if name == "b_ada" else row[:, SMALL_ROW[name][0]:SMALL_ROW[name][1]]
            d, mn, vn = _adam_math(ins[3 * j][...], g, ins[3 * j + 1][...], ins[3 * j + 2][...])
            outs[4 * j][...] = g
            outs[4 * j + 1][...] = d
            outs[4 * j + 2][...] = mn
            outs[4 * j + 3][...] = vn
        loss_ref[...] = row[:, SMALL_ROW["loss"][0]:SMALL_ROW["loss"][1]]

    flat = [t for p in params for t in p]
    res = pl.pallas_call(
        body, name="small_update", grid=(1,),
        out_shape=[jax.ShapeDtypeStruct(p[0].shape, F32) for p in params for _ in range(4)]
        + [jax.ShapeDtypeStruct((1, HEAD_LANES), F32)],
        in_specs=[_full(parts_all.shape)] + [_full(t.shape) for t in flat],
        out_specs=[_full(p[0].shape) for p in params for _ in range(4)] + [_full((1, HEAD_LANES))],
        compiler_params=_params(1),
    )(parts_all, *flat)
    return [res[4 * j:4 * j + 4] for j in range(k)], res[4 * k]


def _rot(t):
    half = t.shape[-1] // 2
    return jnp.concatenate([-t[..., half:], t[..., :half]], axis=-1)


def _rot_t(g):
    half = g.shape[-1] // 2
    return jnp.concatenate([g[..., half:], -g[..., :half]], axis=-1)


def _prepare_weights(w_in, w_uq, w_ukv):
    o = [0]
    for s in IN_SPLITS:
        o.append(o[-1] + s)
    ks, vs = w_in[:, o[5]:o[6]], w_in[:, o[6]:o[7]]
    dup = lambda t: jnp.concatenate([t[:, :64], t[:, :64], t[:, 64:], t[:, 64:]], axis=1)
    wa = jnp.concatenate([w_in[:, :o[2]], w_in[:, o[3]:o[5]], dup(ks), dup(vs), w_in[:, o[7]:]], axis=1)
    kr = w_in[:, o[2]:o[3]]
    zc = lambda n: jnp.zeros((w_in.shape[0], n), w_in.dtype)
    wkr2 = jnp.concatenate([zc(64), kr, zc(32), zc(64), _rot(kr), zc(32)], axis=1)
    uq = w_uq.reshape(Q_LORA, N_HEADS, MLA_NOPE + MLA_ROPE)
    zq = jnp.zeros((Q_LORA, N_HEADS, 32), w_uq.dtype)
    uq_full = jnp.concatenate([uq, zq], axis=-1).reshape(Q_LORA, 1024)
    uq_rot = jnp.concatenate([jnp.zeros((Q_LORA, N_HEADS, 64), w_uq.dtype), _rot(uq[..., MLA_NOPE:]), zq],
                             axis=-1).reshape(Q_LORA, 1024)
    wq2 = jnp.concatenate([uq_full, uq_rot], axis=1)
    ukv = w_ukv.reshape(KV_LORA, N_HEADS, 128)
    k_full = jnp.concatenate([ukv[..., :64], jnp.zeros((KV_LORA, N_HEADS, 64), w_ukv.dtype)], axis=-1).reshape(KV_LORA, 1024)
    wkv = jnp.concatenate([k_full, ukv[..., 64:].reshape(KV_LORA, 512)], axis=1)
    return wa, wkr2, wq2, wkv


def _restore_grads(gwa, gwkr2, gwq2, gwkv):
    fold = lambda g: jnp.concatenate([g[:, 0:64] + g[:, 64:128], g[:, 128:192] + g[:, 192:256]], axis=1)
    gkr = gwkr2[:, 64:96] + _rot_t(gwkr2[:, 192:224])
    g_in = jnp.concatenate([gwa[:, :A_GM], gkr, gwa[:, A_GM:A_KD], fold(gwa[:, A_KD:A_VD]), fold(gwa[:, A_VD:A_GS]),
                            gwa[:, A_GS:]], axis=1)
    gf = gwq2[:, :1024].reshape(Q_LORA, N_HEADS, 128)
    gr = gwq2[:, 1024:].reshape(Q_LORA, N_HEADS, 128)
    g_uq = jnp.concatenate([gf[..., :64], gf[..., 64:96] + _rot_t(gr[..., 64:96])], axis=-1).reshape(Q_LORA, 768)
    gk = gwkv[:, :1024].reshape(KV_LORA, N_HEADS, 128)[..., :64]
    gv = gwkv[:, 1024:].reshape(KV_LORA, N_HEADS, 64)
    g_ukv = jnp.concatenate([gk, gv], axis=-1).reshape(KV_LORA, 1024)
    return g_in, g_uq, g_ukv


def _local_step(x, positions, target, mod_rows, b_ada, ng, qg, kvg, sinks, fg, w_in_b, w_uq_b, w_ukv_b, w_out_b):
    n_seq, seq, _ = x.shape
    n_tok = n_seq * seq
    x2d = x.reshape(n_tok, D_MODEL)
    t2d = target.reshape(n_tok, D_MODEL)
    pos_f = positions.astype(F32)
    pos_col = pos_f.reshape(n_tok, 1)
    pos_row = pos_f.reshape(n_tok // SWA_WINDOW, 1, SWA_WINDOW)
    mod3 = mod_rows.reshape(n_seq, 1, 3 * D_MODEL)
    inv = ROPE_THETA ** (-jnp.arange(0, MLA_ROPE, 2, dtype=F32) / MLA_ROPE)
    inv128 = jnp.concatenate([jnp.zeros((64,), F32), inv, inv, jnp.zeros((32,), F32)]).reshape(1, 128)
    fg2 = fg.reshape(1, D_MODEL)

    wa, wkr2, wq2, wkv = _prepare_weights(w_in_b, w_uq_b, w_ukv_b)

    zqkv, gates, qf, kf, v, qs, kd, vd = _pre_call(x2d, pos_col, mod3, b_ada, ng, qg, kvg, inv128, wa, wkr2, wq2, wkv, seq)
    o_mla, lse_mla = _mla_fwd_call(qf, kf, v, n_seq, seq)
    o_swa, lse_swa = _swa_fwd_call(qs, kd, vd, pos_col, pos_row, sinks, n_seq, seq)
    dx2, do, dg, g_out, g_fg, dgate, loss = _post_call(x2d, t2d, o_mla, o_swa, gates, mod3, b_ada, fg2, w_out_b, w_out_b.T, seq)
    dqf, dkf, dv = _mla_bwd_call(qf, kf, v, do, o_mla, lse_mla, n_seq, seq)
    dqs, dkd, dvd, dsink = _swa_bwd_call(qs, kd, vd, do, o_swa, lse_swa, pos_col, pos_row, sinks, n_seq, seq)
    dz, dkr, g_wq2, g_wkv, g_qg, g_kvg = _mid_bwd_call(dqf, dkf, dv, zqkv, pos_col, qg, kvg, inv128, wq2, wkv, seq)
    gx, g_wa, g_wkr2, g_ng, dshift, dscale = _in_bwd_call(x2d, dx2, dz, dkr, dg, dqs, dkd, dvd, mod3, b_ada, ng,
                                                         wa.T, wkr2.T, seq)
    g_in, g_uq, g_ukv = _restore_grads(g_wa, g_wkr2, g_wq2, g_wkv)
    dmod = jnp.concatenate([dshift, dscale, dgate], axis=-1).reshape(n_seq, 3 * D_MODEL)
    small_row = jnp.concatenate([g_ng, g_fg, g_qg, g_kvg, jnp.pad(jnp.sum(dsink, axis=1).reshape(1, N_HEADS), ((0, 0), (0, 120))),
                                 loss, jnp.zeros((1, 128), F32)], axis=1)
    return gx.reshape(x.shape), (g_in, g_uq, g_ukv, g_out), small_row, dmod


def kernel(x, c, positions, w_ada, b_ada, norm_gain, w_in, q_norm_gain, kv_norm_gain, w_uq, w_ukv, swa_sinks, w_out, final_gain, loss_target, m_w_ada, m_b_ada, m_norm_gain, m_w_in, m_q_norm_gain, m_kv_norm_gain, m_w_uq, m_w_ukv, m_swa_sinks, m_w_out, m_final_gain, v_w_ada, v_b_ada, v_norm_gain, v_w_in, v_q_norm_gain, v_kv_norm_gain, v_w_uq, v_w_ukv, v_swa_sinks, v_w_out, v_final_gain):
    n_seq = x.shape[0]
    xi, yi, ci = lax.axis_index("x"), lax.axis_index("y"), lax.axis_index("c")
    dev = 4 * xi + 2 * yi + ci
    chip = 2 * xi + yi

    halves = lambda w: w.astype(BF16).reshape(2, w.shape[0] // 2, w.shape[1])
    c_blk = jnp.pad(c, ((0, ROWS_PER_DEVICE - n_seq), (0, 0)))
    act_all, pieces, f_in, f_uq, f_ukv, f_out = _comm_fwd_call(
        c_blk, w_ada[0], [halves(w_in[0]), halves(w_uq[0]), halves(w_ukv[0]), halves(w_out[0])])
    mine = lax.dynamic_slice_in_dim(pieces, dev * ROWS_PER_DEVICE, n_seq, axis=1)
    mod_rows = jnp.transpose(mine, (1, 0, 2)).reshape(n_seq, 3 * D_MODEL)
    cols = lambda t, r: jnp.transpose(t.reshape(4, r, -1), (1, 0, 2)).reshape(r, -1)
    w_in_b, w_uq_b, w_ukv_b = cols(f_in, D_MODEL), cols(f_uq, Q_LORA), cols(f_ukv, KV_LORA)
    w_out_b = f_out.reshape(D_MODEL, D_MODEL)

    gx, (g_in, g_uq, g_ukv, g_out), small_row, dmod = _local_step(
        x, positions, loss_target, mod_rows, b_ada, norm_gain, q_norm_gain, kv_norm_gain, swa_sinks, final_gain,
        w_in_b, w_uq_b, w_ukv_b, w_out_b)

    by_owner = lambda g, n: jnp.transpose(g.reshape(2, g.shape[0] // 2, 4, n), (0, 2, 1, 3))
    grads = [by_owner(g_in, 616), by_owner(g_uq, 192), by_owner(g_ukv, 256),
             jnp.transpose(g_out.reshape(4, 2, 128, D_MODEL), (1, 0, 2, 3))]
    part = jnp.concatenate([dmod, small_row, jnp.zeros((ROWS_PER_DEVICE - n_seq - 1, 3 * D_MODEL), F32)], axis=0)
    r_in, r_uq, r_ukv, r_out, parts_all = _comm_bwd_call(grads, part)
    g_in_s, g_uq_s = r_in.reshape(w_in.shape[1:]), r_uq.reshape(w_uq.shape[1:])
    g_ukv_s, g_out_s = r_ukv.reshape(w_ukv.shape[1:]), r_out.reshape(w_out.shape[1:])

    d_in, nm_in, nv_in = _adam_call("adam_w_in", w_in[0], g_in_s, m_w_in[0], v_w_in[0])
    d_uq, nm_uq, nv_uq = _adam_call("adam_w_uq", w_uq[0], g_uq_s, m_w_uq[0], v_w_uq[0])
    d_ukv, nm_ukv, nv_ukv = _adam_call("adam_w_ukv", w_ukv[0], g_ukv_s, m_w_ukv[0], v_w_ukv[0])
    d_out, nm_out, nv_out = _adam_call("adam_w_out", w_out[0], g_out_s, m_w_out[0], v_w_out[0])
    dmod_cols = lax.dynamic_slice_in_dim(parts_all, chip * 768, 768, axis=1)
    g_ada, d_ada, nm_ada, nv_ada = _ada_bwd_call(act_all, dmod_cols, w_ada[0], m_w_ada[0], v_w_ada[0])

    row = lambda t: t.reshape(1, -1)
    small = {"b_ada": (b_ada, m_b_ada, v_b_ada), "norm_gain": (norm_gain, m_norm_gain, v_norm_gain),
             "q_norm_gain": (q_norm_gain, m_q_norm_gain, v_q_norm_gain),
             "kv_norm_gain": (kv_norm_gain, m_kv_norm_gain, v_kv_norm_gain),
             "swa_sinks": (swa_sinks, m_swa_sinks, v_swa_sinks),
             "final_gain": (row(final_gain), row(m_final_gain), row(v_final_gain))}
    res, loss_row = _small_call(parts_all, n_seq, [small[name] for name in SMALL_ORDER])
    res = dict(zip(SMALL_ORDER, res))
    res["final_gain"] = [t.reshape(-1) for t in res["final_gain"]]
    e = lambda t: t[None]
    big = {"w_ada": (e(g_ada), e(d_ada), e(nm_ada), e(nv_ada)), "w_in": (e(g_in_s), e(d_in), e(nm_in), e(nv_in)),
           "w_uq": (e(g_uq_s), e(d_uq), e(nm_uq), e(nv_uq)), "w_ukv": (e(g_ukv_s), e(d_ukv), e(nm_ukv), e(nv_ukv)),
           "w_out": (e(g_out_s), e(d_out), e(nm_out), e(nv_out))}
    order = ("w_ada", "b_ada", "norm_gain", "w_in", "q_norm_gain", "kv_norm_gain", "w_uq", "w_ukv", "swa_sinks", "w_out",
             "final_gain")
    pick = lambda kind: [(big[n] if n in big else res[n])[kind] for n in order]
    return (loss_row[0, 0], gx, *pick(0), *pick(1), *pick(2), *pick(3))
```

```python
import functools

import jax
import jax.numpy as jnp
from jax import lax
from jax.experimental import pallas as pl
from jax.experimental.pallas import tpu as pltpu

F32 = jnp.float32
BF16 = jnp.bfloat16

D_MODEL = 1024
Q_LORA = 384
KV_LORA = 256
N_HEADS = 8
MLA_NOPE = 64
MLA_ROPE = 32
HEAD_LANES = 128
HALF = 64
SWA_WINDOW = 128
EPS = 1e-6
ROPE_THETA = 10000.0
MLA_SCALE = (MLA_NOPE + MLA_ROPE) ** -0.5
LOG2E = 1.4426950408889634
LN2 = 0.6931471805599453
SWA_SCALE = 64 ** -0.5
NEG = -1e30

ADAM_LR = 0.001
ADAM_B1 = 0.9
ADAM_B2 = 0.999
ADAM_EPS = 1e-08
ADAM_WD = 0.01
ADAM_STEP = 10

A_ZQ, A_ZKV, A_GM, A_QS, A_KD, A_VD, A_GS, A_END = 0, 384, 640, 1152, 1664, 1920, 2176, 2688
IN_SPLITS = (384, 256, 32, 512, 512, 128, 128, 512)
D_IN = sum(IN_SPLITS)

TOKEN_TILE = 512
ATT_TILE = 256
VMEM_LIMIT = 56 * 1024 * 1024


def _dot(a, b):
    return jnp.dot(a, b, preferred_element_type=F32)


def _dot_nt(a, b):
    return lax.dot_general(a, b, (((1,), (1,)), ((), ())), preferred_element_type=F32)


def _dot_tn(a, b):
    return lax.dot_general(a, b, (((0,), (0,)), ((), ())), preferred_element_type=F32)


def _params(n_grid):
    return pltpu.CompilerParams(dimension_semantics=("arbitrary",) * n_grid, vmem_limit_bytes=VMEM_LIMIT)


def _full(shape):
    nd = len(shape)
    return pl.BlockSpec(shape, lambda *_: (0,) * nd, pipeline_mode=pl.Buffered(1))


def _sigmoid(g):
    return 1.0 / (1.0 + jnp.exp(-g))


SUB_TILE = 256


def _sub_tiles(tm):
    sub = min(SUB_TILE, tm)
    return [slice(s * sub, (s + 1) * sub) for s in range(tm // sub)]


MESH = pl.DeviceIdType.MESH
ROWS_PER_DEVICE = 8
VMEM_SPEC = pl.BlockSpec(memory_space=pltpu.VMEM)
ANY_SPEC = pl.BlockSpec(memory_space=pl.ANY)


def _position():
    x, y, c = lax.axis_index("x"), lax.axis_index("y"), lax.axis_index("c")
    sibling = (x, y, 1 - c)
    others = [(1 - x, y, c), (x, 1 - y, c), (1 - x, 1 - y, c)]
    return (x, y, c), 4 * x + 2 * y + c, 2 * x + y, sibling, others


def _rows_of(dev):
    return pl.ds(pl.multiple_of(dev * ROWS_PER_DEVICE, ROWS_PER_DEVICE), ROWS_PER_DEVICE)


def _all_to_all_rows(block_ref, table_ref, dev, me, send_sems, recv_sems):
    x, y, c = me
    waits = []
    for k in range(1, 8):
        peer = (1 - x if k & 4 else x, 1 - y if k & 2 else y, 1 - c if k & 1 else c)
        pltpu.make_async_remote_copy(src_ref=block_ref, dst_ref=table_ref.at[_rows_of(dev)], send_sem=send_sems.at[k - 1],
                                     recv_sem=recv_sems.at[k - 1], device_id=peer, device_id_type=MESH).start()
        waits.append(pltpu.make_async_remote_copy(
            src_ref=block_ref, dst_ref=table_ref.at[_rows_of(jnp.bitwise_xor(dev, k))], send_sem=send_sems.at[k - 1],
            recv_sem=recv_sems.at[k - 1], device_id=peer, device_id_type=MESH))
    return waits


def _comm_fwd_call(c_blk, w_ada, shards):
    n = len(shards)

    def body(c_ref, wada_ref, *refs):
        w_refs, act_ref, pieces_ref, full_refs = refs[:n], refs[n], refs[n + 1], refs[n + 2:2 * n + 2]
        c_all_ref = refs[2 * n + 2]
        c_send, c_recv, p_send, p_recv, w_send, w_recv, f_send, f_recv, loc_sem = refs[2 * n + 3:]
        me, dev, chip, sibling, others = _position()
        core = me[2]
        chip_of = [2 * p[0] + p[1] for p in others]

        local = [pltpu.make_async_copy(w_refs[i], full_refs[i].at[chip], loc_sem.at[i]) for i in range(n)]
        for cp in local:
            cp.start()

        def over_ici(i, j, src_chip):
            return pltpu.make_async_remote_copy(
                src_ref=w_refs[i].at[core], dst_ref=full_refs[i].at[src_chip, core], send_sem=w_send.at[3 * i + j],
                recv_sem=w_recv.at[3 * i + j], device_id=others[j], device_id_type=MESH)

        def to_sibling(i, j, half):
            return pltpu.make_async_remote_copy(
                src_ref=full_refs[i].at[chip_of[j], half], dst_ref=full_refs[i].at[chip_of[j], half],
                send_sem=f_send.at[3 * i + j], recv_sem=f_recv.at[3 * i + j], device_id=sibling, device_id_type=MESH)

        sent = [over_ici(i, j, chip) for i in range(n) for j in range(3)]
        for cp in sent:
            cp.start()

        c_all_ref[_rows_of(dev), :] = c_ref[...]
        c_waits = _all_to_all_rows(c_ref, c_all_ref, dev, me, c_send, c_recv)
        for cp in c_waits:
            cp.wait()
        cv = c_all_ref[...]
        act = cv * _sigmoid(cv)
        act_ref[...] = act
        pieces_ref[chip] = _dot(act.astype(BF16), wada_ref[...].astype(BF16))
        piece = lambda j, src_chip: pltpu.make_async_remote_copy(
            src_ref=pieces_ref.at[chip], dst_ref=pieces_ref.at[src_chip], send_sem=p_send.at[j], recv_sem=p_recv.at[j],
            device_id=others[j], device_id_type=MESH)
        for j in range(3):
            piece(j, chip).start()
        for j in range(3):
            piece(j, chip).wait_send()
            piece(j, chip_of[j]).wait_recv()

        for i in range(n):
            for j in range(3):
                over_ici(i, j, chip_of[j]).wait_recv()
                to_sibling(i, j, core).start()
        for i in range(n):
            for j in range(3):
                to_sibling(i, j, 1 - core).wait_recv()
                to_sibling(i, j, core).wait_send()
        for cp in sent:
            cp.wait_send()
        for cp in local:
            cp.wait()

    rows = 8 * ROWS_PER_DEVICE
    dma = pltpu.SemaphoreType.DMA
    return pl.pallas_call(
        body, name="comm_fwd",
        out_shape=[jax.ShapeDtypeStruct((rows, D_MODEL), F32), jax.ShapeDtypeStruct((4, rows, w_ada.shape[1]), F32)]
        + [jax.ShapeDtypeStruct((4,) + s.shape, s.dtype) for s in shards],
        in_specs=[VMEM_SPEC, VMEM_SPEC] + [ANY_SPEC] * n,
        out_specs=[VMEM_SPEC, VMEM_SPEC] + [ANY_SPEC] * n,
        scratch_shapes=[pltpu.VMEM((rows, D_MODEL), F32), dma((7,)), dma((7,)), dma((3,)), dma((3,)),
                        dma((3 * n,)), dma((3 * n,)), dma((3 * n,)), dma((3 * n,)), dma((n,))],
        compiler_params=pltpu.CompilerParams(vmem_limit_bytes=VMEM_LIMIT),
    )(c_blk, w_ada, *shards)


def _comm_bwd_call(grads, part):
    n = len(grads)

    def body(part_ref, *refs):
        g_refs, f_refs, parts_ref = refs[:n], refs[n:2 * n], refs[2 * n]
        scratch = refs[2 * n + 1:]
        a_refs, b_refs, p_refs, r_refs = (scratch[k * n:(k + 1) * n] for k in range(4))
        s_send, s_recv, d_send, d_recv, e_send, e_recv, h_send, h_recv, loc_sem = scratch[4 * n:]
        me, dev, chip, sibling, others = _position()
        core = me[2]
        chip_of = [2 * p[0] + p[1] for p in others]

        parts_ref[_rows_of(dev), :] = part_ref[...]
        s_waits = _all_to_all_rows(part_ref, parts_ref, dev, me, s_send, s_recv)

        mine = [pltpu.make_async_copy(g_refs[i].at[core], a_refs[i], loc_sem.at[i]) for i in range(n)]
        swap = [pltpu.make_async_remote_copy(src_ref=g_refs[i].at[1 - core], dst_ref=b_refs[i], send_sem=d_send.at[i],
                                             recv_sem=d_recv.at[i], device_id=sibling, device_id_type=MESH) for i in range(n)]
        order = sorted(range(n), key=lambda i: g_refs[i].shape[2] * g_refs[i].shape[3])
        for i in order:
            mine[i].start()
            swap[i].start()
        cross = [pltpu.make_async_remote_copy(src_ref=p_refs[i].at[chip_of[j]], dst_ref=r_refs[i].at[j],
                                              send_sem=e_send.at[3 * i + j], recv_sem=e_recv.at[3 * i + j],
                                              device_id=others[j], device_id_type=MESH) for i in range(n) for j in range(3)]
        for i in order:
            mine[i].wait()
            swap[i].wait()
            for k in range(4):
                s = a_refs[i][k] + b_refs[i][k]
                a_refs[i][k] = s
                p_refs[i][k] = s.astype(BF16)
            for j in range(3):
                cross[3 * i + j].start()
        share = {}
        for i in order:
            for j in range(3):
                cross[3 * i + j].wait()
            f_refs[i][core] = (a_refs[i][chip] + r_refs[i][0].astype(F32) + r_refs[i][1].astype(F32)
                               + r_refs[i][2].astype(F32))
            share[i] = pltpu.make_async_remote_copy(src_ref=f_refs[i].at[core], dst_ref=f_refs[i].at[core],
                                                    send_sem=h_send.at[i], recv_sem=h_recv.at[i], device_id=sibling,
                                                    device_id_type=MESH)
            share[i].start()
        for i in range(n):
            share[i].wait_send()
            pltpu.make_async_remote_copy(src_ref=f_refs[i].at[core], dst_ref=f_refs[i].at[1 - core], send_sem=h_send.at[i],
                                         recv_sem=h_recv.at[i], device_id=sibling, device_id_type=MESH).wait_recv()
        for cp in s_waits:
            cp.wait()

    rows = 8 * ROWS_PER_DEVICE
    dma = pltpu.SemaphoreType.DMA
    quarter = [g.shape[1:] for g in grads]
    return pl.pallas_call(
        body, name="comm_bwd",
        out_shape=[jax.ShapeDtypeStruct((2,) + g.shape[2:], F32) for g in grads]
        + [jax.ShapeDtypeStruct((rows, part.shape[1]), F32)],
        in_specs=[VMEM_SPEC] + [ANY_SPEC] * n,
        out_specs=[VMEM_SPEC] * (n + 1),
        scratch_shapes=[pltpu.VMEM(q, F32) for q in quarter] + [pltpu.VMEM(q, F32) for q in quarter]
        + [pltpu.VMEM(q, BF16) for q in quarter] + [pltpu.VMEM((3,) + q[1:], BF16) for q in quarter]
        + [dma((7,)), dma((7,)), dma((n,)), dma((n,)), dma((3 * n,)), dma((3 * n,)), dma((n,)), dma((n,)), dma((n,))],
        compiler_params=pltpu.CompilerParams(vmem_limit_bytes=VMEM_LIMIT),
    )(part, *grads)


def _rope_tables(pos_col, inv_row):
    ang = pos_col * inv_row
    return jnp.cos(ang), jnp.sin(ang)


def _pre_call(x, pos_col, mod, b_ada, ng, qg, kvg, inv128, wa, wkr2, wq2, wkv, seq):
    n_tok = x.shape[0]
    tm = min(TOKEN_TILE, seq)
    per_seq = seq // tm

    def body(x_ref, pos_ref, mod_ref, bada_ref, ng_ref, qg_ref, kvg_ref, inv_ref, wa_ref, wkr_ref, wq_ref, wkv_ref,
             zqkv_ref, gates_ref, qf_ref, kf_ref, v_ref, qs_ref, kd_ref, vd_ref):
        xv = x_ref[...]
        modv = mod_ref[0] + bada_ref[...]
        shift, scale = modv[:, :D_MODEL], modv[:, D_MODEL:2 * D_MODEL]
        r1 = lax.rsqrt(jnp.mean(xv * xv, axis=-1, keepdims=True) + EPS)
        h = ((xv * r1) * ng_ref[...]) * (1.0 + scale) + shift
        hb = h.astype(BF16)
        za = _dot(hb, wa_ref[...])
        zkr = _dot(hb, wkr_ref[...])
        cos, sin = _rope_tables(pos_ref[...], inv_ref[...])
        zqkv_ref[...] = za[:, :A_GM]
        gates_ref[:, :512] = za[:, A_GM:A_QS]
        gates_ref[:, 512:] = za[:, A_GS:A_END]
        qs_ref[...] = (za[:, A_QS:A_KD] * SWA_SCALE).astype(BF16)
        kd_ref[...] = za[:, A_KD:A_VD].astype(BF16)
        vd_ref[...] = za[:, A_VD:A_GS].astype(BF16)
        zq, zkv = za[:, A_ZQ:A_ZKV], za[:, A_ZKV:A_GM]
        rq = lax.rsqrt(jnp.mean(zq * zq, axis=-1, keepdims=True) + EPS)
        qn = ((zq * rq) * qg_ref[...]).astype(BF16)
        qr = _dot(qn, wq_ref[...])
        cf, sf = jnp.tile(cos, (1, N_HEADS)), jnp.tile(sin, (1, N_HEADS))
        qf_ref[...] = ((qr[:, :1024] * cf + qr[:, 1024:] * sf) * (MLA_SCALE * LOG2E)).astype(BF16)
        rkv = lax.rsqrt(jnp.mean(zkv * zkv, axis=-1, keepdims=True) + EPS)
        kvn = ((zkv * rkv) * kvg_ref[...]).astype(BF16)
        kv = _dot(kvn, wkv_ref[...])
        kpe = zkr[:, :128] * cos + zkr[:, 128:] * sin
        kf_ref[...] = (kv[:, :1024] + jnp.tile(kpe, (1, N_HEADS))).astype(BF16)
        v_ref[...] = kv[:, 1024:].astype(BF16)

    tok = lambda w: pl.BlockSpec((tm, w), lambda i: (i, 0))
    outs = [(640, F32), (1024, F32), (1024, BF16), (1024, BF16), (512, BF16), (512, BF16), (256, BF16), (256, BF16)]
    return pl.pallas_call(
        body, name="pre", grid=(n_tok // tm,),
        out_shape=[jax.ShapeDtypeStruct((n_tok, w), dt) for w, dt in outs],
        in_specs=[tok(D_MODEL), tok(1), pl.BlockSpec((1, 1, 3 * D_MODEL), lambda i: (i // per_seq, 0, 0)),
                  _full(b_ada.shape), _full(ng.shape), _full(qg.shape), _full(kvg.shape), _full(inv128.shape),
                  _full(wa.shape), _full(wkr2.shape), _full(wq2.shape), _full(wkv.shape)],
        out_specs=[tok(w) for w, _ in outs],
        compiler_params=_params(1),
    )(x, pos_col, mod, b_ada, ng, qg, kvg, inv128, wa, wkr2, wq2, wkv)


def _lane_lo(width=HEAD_LANES):
    return lax.broadcasted_iota(jnp.int32, (1, width), 1) < HALF


def _eye(n=HEAD_LANES):
    r = lax.broadcasted_iota(jnp.int32, (n, n), 0)
    c = lax.broadcasted_iota(jnp.int32, (n, n), 1)
    return jnp.where(r == c, 1.0, 0.0).astype(BF16)


def _mla_fwd_call(qf, kf, v, n_seq, seq):
    tq = min(ATT_TILE, seq)
    nq = seq // tq

    ext = HALF + 16

    def body(q_ref, k_ref, v_ref, o_ref, lse_ref, vt_ref):
        i = pl.program_id(1)
        eye = _eye()

        @pl.when(i == 0)
        def _():
            for h in range(N_HEADS):
                vt_ref[h * ext + HALF:(h + 1) * ext, :] = jnp.ones((16, seq), BF16)
            for t in range(nq):
                for p in range(N_HEADS // 2):
                    pair = slice(p * HEAD_LANES, (p + 1) * HEAD_LANES)
                    v_t = _dot_nt(eye, v_ref[t * tq:(t + 1) * tq, pair]).astype(BF16)
                    for hh in range(2):
                        r0 = (2 * p + hh) * ext
                        vt_ref[r0:r0 + HALF, t * tq:(t + 1) * tq] = v_t[hh * HALF:(hh + 1) * HALF, :]

        q = q_ref[...]
        qcol = i * tq + lax.broadcasted_iota(jnp.int32, (1, tq), 1)
        heads = range(N_HEADS)
        lanes = [slice(h * HEAD_LANES, (h + 1) * HEAD_LANES) for h in heads]

        def make_step(masked, n_tiles):
            def step(kt0, carry):
                tiles = range(n_tiles)
                start = pl.multiple_of(kt0 * tq, tq)
                ks = [k_ref[pl.ds(pl.multiple_of((kt0 + t) * tq, tq), tq), :] for t in tiles]
                vt = vt_ref[:, pl.ds(start, n_tiles * tq)]
                sts = [[_dot_nt(ks[t][:, lanes[h]], q[:, lanes[h]]) for h in heads] for t in tiles]
                if masked:
                    last = n_tiles - 1
                    keep = ((kt0 + last) * tq + lax.broadcasted_iota(jnp.int32, (tq, 1), 0)) <= qcol
                    sts[last] = [jnp.where(keep, st, NEG) for st in sts[last]]
                stats, pts = [], []
                for h in heads:
                    m_old = carry[2 * h]
                    m_new = m_old
                    for t in tiles:
                        m_new = jnp.maximum(m_new, jnp.max(sts[t][h], axis=0, keepdims=True))
                    pts.append(jnp.concatenate([jnp.exp2(sts[t][h] - m_new).astype(BF16) for t in tiles], axis=0))
                    stats.append((m_new, jnp.exp2(m_old - m_new)))
                pvs = [_dot(vt[h * ext:(h + 1) * ext, :], pts[h]) for h in heads]
                out = []
                for h in heads:
                    out += [stats[h][0], carry[2 * h + 1] * stats[h][1] + pvs[h]]
                return tuple(out)
            return step

        init = (jnp.full((1, tq), NEG, F32), jnp.zeros((ext, tq), F32)) * N_HEADS
        count = i + 1
        carry = lax.fori_loop(0, (count + 1) // 2 - 1, lambda j, c: make_step(False, 2)(2 * j, c), init)
        carry = lax.cond(count % 2 == 0, lambda c: make_step(True, 2)(i - 1, c), lambda c: make_step(True, 1)(i, c), carry)
        dens = [carry[2 * h + 1][HALF:HALF + 1, :] for h in heads]
        acc_t = jnp.concatenate([carry[2 * h + 1][:HALF, :] * (1.0 / dens[h]) for h in heads], axis=0)
        o_ref[...] = acc_t.T
        for h in heads:
            lse_ref[0, h // 4, h % 4:h % 4 + 1, :] = carry[2 * h] + jnp.log2(dens[h])

    n_tok = qf.shape[0]
    return pl.pallas_call(
        body, name="mla_fwd", grid=(n_seq, nq),
        out_shape=[jax.ShapeDtypeStruct((n_tok, 512), F32), jax.ShapeDtypeStruct((n_seq, 2, 4, seq), F32)],
        in_specs=[pl.BlockSpec((tq, 1024), lambda b, i: (b * nq + i, 0)),
                  pl.BlockSpec((seq, 1024), lambda b, i: (b, 0)),
                  pl.BlockSpec((seq, 512), lambda b, i: (b, 0))],
        out_specs=[pl.BlockSpec((tq, 512), lambda b, i: (b * nq + i, 0)),
                   pl.BlockSpec((1, 2, 4, tq), lambda b, i: (b, 0, 0, i))],
        scratch_shapes=[pltpu.VMEM((N_HEADS * ext, seq), BF16)],
        compiler_params=_params(2),
    )(qf, kf, v)


def _mla_bwd_call(qf, kf, v, do, o, lse, n_seq, seq):
    tq = min(ATT_TILE, seq)
    nq = seq // tq

    nh = 4
    heads = range(nh)
    lanes = [slice(h * HEAD_LANES, (h + 1) * HEAD_LANES) for h in heads]

    def body(q_ref, k_ref, v_ref, do_ref, o_ref, lse_ref, dq_ref, dk_ref, dv_ref,
             kt_ref, dot_ref, delta_ref, dqt_ref):
        eye = _eye()
        lo = _lane_lo()
        sub_lo = lax.broadcasted_iota(jnp.int32, (HEAD_LANES, 1), 0) < HALF
        ones_lo = jnp.where(jnp.broadcast_to(lo, (8, HEAD_LANES)), 1.0, 0.0).astype(BF16)
        ones_hi = jnp.where(jnp.broadcast_to(lo, (8, HEAD_LANES)), 0.0, 1.0).astype(BF16)

        for t in range(nq):
            r = slice(t * tq, (t + 1) * tq)
            kv = k_ref[r, :]
            for h in heads:
                kt_ref[lanes[h], r] = _dot_nt(eye, kv[:, lanes[h]]).astype(BF16)
            for p in range(nh // 2):
                dov = do_ref[r, lanes[p]]
                dt = _dot_nt(eye, dov)
                dot_ref[2 * p, :, r] = jnp.where(sub_lo, dt, 0.0).astype(BF16)
                dot_ref[2 * p + 1, :, r] = jnp.where(sub_lo, 0.0, dt).astype(BF16)
                prod = dov.astype(F32) * o_ref[r, lanes[p]]
                p_hi = prod.astype(BF16)
                p_lo = (prod - p_hi.astype(F32)).astype(BF16)
                delta_ref[2 * p, :, r] = _dot_nt(ones_lo, p_hi) + _dot_nt(ones_lo, p_lo)
                delta_ref[2 * p + 1, :, r] = _dot_nt(ones_hi, p_hi) + _dot_nt(ones_hi, p_lo)
        dqt_ref[...] = jnp.zeros_like(dqt_ref)

        def k_step(kt, _):
            kr = pl.ds(pl.multiple_of(kt * tq, tq), tq)
            k = k_ref[kr, :]
            vv = v_ref[kr, :]
            k_t = kt_ref[:, kr]
            krow = kt * tq + lax.broadcasted_iota(jnp.int32, (tq, 1), 0)

            def make_step(masked, n_tiles):
                def q_step(qt0, carry):
                    tiles = range(n_tiles)
                    qrs = [pl.ds(pl.multiple_of((qt0 + t) * tq, tq), tq) for t in tiles]
                    qs = [q_ref[qr, :] for qr in qrs]
                    do_ts = [[dot_ref[h, :, qr] for h in heads] for qr in qrs]
                    sts = [[_dot_nt(k[:, lanes[h]], qs[t][:, lanes[h]]) for h in heads] for t in tiles]
                    dpts = [[_dot(vv[:, lanes[h // 2]], do_ts[t][h]) for h in heads] for t in tiles]
                    if masked:
                        keep = krow <= (qt0 * tq + lax.broadcasted_iota(jnp.int32, (1, tq), 1))
                    pts, dsts = [], []
                    for t in tiles:
                        pts.append([])
                        dsts.append([])
                        for h in heads:
                            pt = jnp.exp2(sts[t][h] - lse_ref[0, 0, h:h + 1, qrs[t]])
                            if masked and t == 0:
                                pt = jnp.where(keep, pt, 0.0)
                            dsts[t].append((pt * (dpts[t][h] - delta_ref[h, 0:1, qrs[t]])).astype(BF16))
                            pts[t].append(pt.astype(BF16))
                    out = []
                    for h in heads:
                        hh = h % 2
                        half = slice(hh * HALF, (hh + 1) * HALF)
                        dst_all = jnp.concatenate([dsts[t][h] for t in tiles], axis=1)
                        pt_all = jnp.concatenate([pts[t][h] for t in tiles], axis=1)
                        do_all = jnp.concatenate([do_ts[t][h][half, :] for t in tiles], axis=1)
                        q_all = jnp.concatenate([qs[t][:, lanes[h]] for t in tiles], axis=0)
                        dvt = _dot_nt(do_all, pt_all)
                        dk = _dot(dst_all, q_all)
                        for t in tiles:
                            dqt_ref[lanes[h], qrs[t]] += _dot(k_t[lanes[h], :], dsts[t][h])
                        out += [carry[2 * h] + dk, carry[2 * h + 1] + dvt]
                    return tuple(out)
                return q_step

            init = (jnp.zeros((tq, HEAD_LANES), F32), jnp.zeros((HALF, tq), F32)) * nh
            count = nq - kt
            carry = lax.cond(count >= 2, lambda c: make_step(True, 2)(kt, c), lambda c: make_step(True, 1)(kt, c), init)
            carry = lax.fori_loop(1, count // 2, lambda j, c: make_step(False, 2)(kt + 2 * j, c), carry)
            carry = lax.cond(jnp.logical_and(count % 2 == 1, count >= 3),
                             lambda c: make_step(False, 1)(nq - 1, c), lambda c: c, carry)
            for h in heads:
                dk_ref[kr, lanes[h]] = carry[2 * h]
            for p in range(nh // 2):
                dv_ref[kr, lanes[p]] = jnp.concatenate([carry[4 * p + 1], carry[4 * p + 3]], axis=0).T
            return 0

        lax.fori_loop(0, nq, k_step, 0)
        for t in range(nq):
            r = slice(t * tq, (t + 1) * tq)
            for h in heads:
                dq_ref[r, lanes[h]] = dqt_ref[lanes[h], r].T

    n_tok = qf.shape[0]
    groups = N_HEADS // nh
    blk = lambda w: pl.BlockSpec((seq, w), lambda b, g: (b, g))
    return pl.pallas_call(
        body, name="mla_bwd", grid=(n_seq, groups),
        out_shape=[jax.ShapeDtypeStruct((n_tok, 1024), F32), jax.ShapeDtypeStruct((n_tok, 1024), F32),
                   jax.ShapeDtypeStruct((n_tok, 512), F32)],
        in_specs=[blk(512), blk(512), blk(256), blk(256), blk(256),
                  pl.BlockSpec((1, 1, nh, seq), lambda b, g: (b, g, 0, 0))],
        out_specs=[blk(512), blk(512), blk(256)],
        scratch_shapes=[pltpu.VMEM((nh * HEAD_LANES, seq), BF16), pltpu.VMEM((nh, HEAD_LANES, seq), BF16),
                        pltpu.VMEM((nh, 8, seq), F32), pltpu.VMEM((nh * HEAD_LANES, seq), F32)],
        compiler_params=_params(2),
    )(qf, kf, v, do, o, lse)


def _swa_block(n, pos_col_ref, pos_row_ref):
    w = SWA_WINDOW
    start = pl.multiple_of(jnp.maximum(n - 1, 0) * w, w)
    posq = pos_row_ref[0]
    posk = pos_col_ref[pl.ds(start, 2 * w), :]
    dist = posq - posk
    rel = (n * w + lax.broadcasted_iota(jnp.int32, (1, w), 1)) - (start + lax.broadcasted_iota(jnp.int32, (2 * w, 1), 0))
    valid = jnp.logical_and(rel >= 0, rel < w)
    return start, dist, valid


def _transpose_rows(eye, src_ref, dst_ref, seq, width):
    step = 2 * SWA_WINDOW
    for t in range(seq // step):
        for p in range(width // HEAD_LANES):
            lanes = slice(p * HEAD_LANES, (p + 1) * HEAD_LANES)
            dst_ref[lanes, t * step:(t + 1) * step] = _dot_nt(eye, src_ref[t * step:(t + 1) * step, lanes]).astype(BF16)


def _swa_fwd_call(qs, kd, vd, pos_col, pos_row, sinks, n_seq, seq):
    w = SWA_WINDOW
    nb = seq // w

    def body(q_ref, k_ref, v_ref, pc_ref, pr_ref, sink_ref, o_ref, lse_ref, vt_ref):
        n = pl.program_id(1)
        lo = _lane_lo()
        hi = jnp.logical_not(lo)
        eye = _eye()

        @pl.when(n == 0)
        def _():
            _transpose_rows(eye, v_ref, vt_ref, seq, 2 * HEAD_LANES)

        start, dist, valid = _swa_block(n, pc_ref, pr_ref)
        win = pl.ds(start, 2 * w)
        heads = range(N_HEADS)
        q = q_ref[...]
        kwin = k_ref[win, :]
        vt = vt_ref[:, win]
        sts = []
        for h in heads:
            qp = q[:, (h // 2) * HEAD_LANES:(h // 2 + 1) * HEAD_LANES]
            qh = jnp.where(lo if h % 2 == 0 else hi, qp, jnp.zeros_like(qp))
            sts.append(_dot_nt(kwin[:, (h // 4) * HEAD_LANES:(h // 4 + 1) * HEAD_LANES], qh))
        pns = []
        for h in heads:
            s = jnp.where(valid, sts[h] - (2.0 ** -(h + 1)) * dist, NEG)
            sink = sink_ref[0, h]
            m = jnp.maximum(jnp.max(s, axis=0, keepdims=True), sink)
            p = jnp.exp(s - m)
            l = jnp.sum(p, axis=0, keepdims=True) + jnp.exp(sink - m)
            pns.append((p * (1.0 / l)).astype(BF16))
            lse_ref[0, h:h + 1, :] = m + jnp.log(l)
        ots = [_dot(vt[(h // 4) * HEAD_LANES:(h // 4) * HEAD_LANES + HALF, :], pns[h]) for h in heads]
        o_ref[...] = jnp.concatenate(ots, axis=0).T

    n_tok = qs.shape[0]
    tok = lambda width: pl.BlockSpec((w, width), lambda b, n: (b * nb + n, 0))
    whole = lambda width: pl.BlockSpec((seq, width), lambda b, n: (b, 0))
    return pl.pallas_call(
        body, name="swa_fwd", grid=(n_seq, nb),
        out_shape=[jax.ShapeDtypeStruct((n_tok, 512), F32), jax.ShapeDtypeStruct((n_seq, N_HEADS, seq), F32)],
        in_specs=[tok(512), whole(256), whole(256), whole(1), pl.BlockSpec((1, 1, w), lambda b, n: (b * nb + n, 0, 0)),
                  pl.BlockSpec(memory_space=pltpu.SMEM)],
        out_specs=[tok(512), pl.BlockSpec((1, N_HEADS, w), lambda b, n: (b, 0, n))],
        scratch_shapes=[pltpu.VMEM((2 * HEAD_LANES, seq), BF16)],
        compiler_params=_params(2),
    )(qs, kd, vd, pos_col, pos_row, sinks)


def _swa_bwd_call(qs, kd, vd, do, o, lse, pos_col, pos_row, sinks, n_seq, seq):
    w = SWA_WINDOW
    nb = seq // w

    def body(q_ref, k_ref, v_ref, do_ref, o_ref, lse_ref, pc_ref, pr_ref, sink_ref, dq_ref, dk_ref, dv_ref, dsink_ref,
             kt_ref):
        b, n = pl.program_id(0), pl.program_id(1)
        lo = _lane_lo()
        hi = jnp.logical_not(lo)
        sub_lo = lax.broadcasted_iota(jnp.int32, (HEAD_LANES, 1), 0) < HALF
        eye = _eye()
        ones_lo = jnp.where(jnp.broadcast_to(lo, (8, HEAD_LANES)), 1.0, 0.0).astype(BF16)
        ones_hi = jnp.where(jnp.broadcast_to(lo, (8, HEAD_LANES)), 0.0, 1.0).astype(BF16)

        @pl.when(n == 0)
        def _():
            dk_ref[...] = jnp.zeros_like(dk_ref)
            dv_ref[...] = jnp.zeros_like(dv_ref)
            _transpose_rows(eye, k_ref, kt_ref, seq, 2 * HEAD_LANES)

        @pl.when(jnp.logical_and(n == 0, b == 0))
        def _():
            dsink_ref[...] = jnp.zeros_like(dsink_ref)

        start, dist, valid = _swa_block(n, pc_ref, pr_ref)
        win = pl.ds(start, 2 * w)
        heads = range(N_HEADS)
        kv_lanes = lambda h: slice((h // 4) * HEAD_LANES, (h // 4 + 1) * HEAD_LANES)
        q, do, o = q_ref[...], do_ref[...], o_ref[...]
        kwin, vwin, ktw = k_ref[win, :], v_ref[win, :], kt_ref[:, win]

        do_ts, deltas, qms, doms = [], [], [], []
        for j in range(N_HEADS // 2):
            pair = slice(j * HEAD_LANES, (j + 1) * HEAD_LANES)
            dop = do[:, pair]
            dt = _dot_nt(eye, dop)
            prod = dop.astype(F32) * o[:, pair]
            p_hi = prod.astype(BF16)
            p_lo = (prod - p_hi.astype(F32)).astype(BF16)
            for hh in range(2):
                half, ones = (lo, ones_lo) if hh == 0 else (hi, ones_hi)
                do_ts.append(jnp.where(sub_lo, dt, 0.0).astype(BF16) if hh == 0 else jnp.where(sub_lo, 0.0, dt).astype(BF16))
                deltas.append((_dot_nt(ones, p_hi) + _dot_nt(ones, p_lo))[0:1, :])
                qms.append(jnp.where(half, q[:, pair], jnp.zeros_like(dop)))
                doms.append(jnp.where(half, dop, jnp.zeros_like(dop)))
        sts = [_dot_nt(kwin[:, kv_lanes(h)], qms[h]) for h in heads]
        dpts = [_dot(vwin[:, kv_lanes(h)], do_ts[h]) for h in heads]
        pts, dsts = [], []
        for h in heads:
            lse_h = lse_ref[0, h:h + 1, :]
            pt = jnp.where(valid, jnp.exp(sts[h] - (2.0 ** -(h + 1)) * dist - lse_h), 0.0)
            dsts.append((pt * (dpts[h] - deltas[h])).astype(BF16))
            pts.append(pt.astype(BF16))
            dsink_ref[h:h + 1, :] += -jnp.exp(sink_ref[0, h] - lse_h) * deltas[h]
        for kv in range(2):
            dk_acc, dv_acc = None, None
            for h in range(4 * kv, 4 * kv + 4):
                dk_h, dv_h = _dot(dsts[h], qms[h]), _dot(pts[h], doms[h])
                dk_acc = dk_h if dk_acc is None else dk_acc + dk_h
                dv_acc = dv_h if dv_acc is None else dv_acc + dv_h
            dk_ref[win, kv_lanes(4 * kv)] += dk_acc
            dv_ref[win, kv_lanes(4 * kv)] += dv_acc
        for j in range(N_HEADS // 2):
            k_t = ktw[kv_lanes(2 * j), :]
            dq_t = jnp.where(sub_lo, _dot(k_t, dsts[2 * j]), _dot(k_t, dsts[2 * j + 1]))
            dq_ref[:, j * HEAD_LANES:(j + 1) * HEAD_LANES] = dq_t.T * SWA_SCALE

    n_tok = qs.shape[0]
    tok = lambda width: pl.BlockSpec((w, width), lambda b, n: (b * nb + n, 0))
    whole = lambda width: pl.BlockSpec((seq, width), lambda b, n: (b, 0))
    return pl.pallas_call(
        body, name="swa_bwd", grid=(n_seq, nb),
        out_shape=[jax.ShapeDtypeStruct((n_tok, 512), F32), jax.ShapeDtypeStruct((n_tok, 256), F32),
                   jax.ShapeDtypeStruct((n_tok, 256), F32), jax.ShapeDtypeStruct((N_HEADS, HEAD_LANES), F32)],
        in_specs=[tok(512), whole(256), whole(256), pl.BlockSpec((w, 512), lambda b, n: (b * nb + n, 1)), tok(512),
                  pl.BlockSpec((1, N_HEADS, w), lambda b, n: (b, 0, n)),
                  whole(1), pl.BlockSpec((1, 1, w), lambda b, n: (b * nb + n, 0, 0)), pl.BlockSpec(memory_space=pltpu.SMEM)],
        out_specs=[tok(512), whole(256), whole(256), _full((N_HEADS, HEAD_LANES))],
        scratch_shapes=[pltpu.VMEM((2 * HEAD_LANES, seq), BF16)],
        compiler_params=_params(2),
    )(qs, kd, vd, do, o, lse, pos_col, pos_row, sinks)


def _post_call(x, target, o_mla, o_swa, gates, mod, b_ada, fg, w_out, w_out_t, seq):
    n_tok = x.shape[0]
    tm = min(TOKEN_TILE, seq)
    per_seq = seq // tm
    n_seq = n_tok // seq

    def body(x_ref, t_ref, om_ref, os_ref, g_ref, mod_ref, bada_ref, fg_ref, w_ref, wt_ref,
             dx2_ref, do_ref, dg_ref, gw_ref, gfg_ref, dgate_ref, loss_ref):
        i = pl.program_id(0)

        @pl.when(i == 0)
        def _():
            gw_ref[...] = jnp.zeros_like(gw_ref)
            gfg_ref[...] = jnp.zeros_like(gfg_ref)
            loss_ref[...] = jnp.zeros_like(loss_ref)

        @pl.when(i % per_seq == 0)
        def _():
            dgate_ref[...] = jnp.zeros_like(dgate_ref)

        gate = mod_ref[0][:, 2 * D_MODEL:] + bada_ref[:, 2 * D_MODEL:]
        fgv = fg_ref[...]
        subs = _sub_tiles(tm)
        gs = [g_ref[r, :] for r in subs]
        os_ = [jnp.concatenate([om_ref[r, :], os_ref[r, :]], axis=-1) for r in subs]
        sgs = [_sigmoid(g) for g in gs]
        sils = [g * sg for g, sg in zip(gs, sgs)]
        ypres = [(o * sil).astype(BF16) for o, sil in zip(os_, sils)]
        ys = [_dot(ypre, w_ref[...]) for ypre in ypres]
        dys, loss, gfg, dgate = [], 0.0, 0.0, 0.0
        for r, y in zip(subs, ys):
            x2 = x_ref[r, :] + gate * y
            r2 = lax.rsqrt(jnp.mean(x2 * x2, axis=-1, keepdims=True) + EPS)
            xn2 = x2 * r2
            err = xn2 * fgv - t_ref[r, :]
            loss = loss + jnp.sum(jnp.sum(err * err, axis=-1, keepdims=True), axis=0, keepdims=True)
            dout = err * (1.0 / D_MODEL)
            gfg = gfg + jnp.sum(dout * xn2, axis=0, keepdims=True)
            dxn2 = dout * fgv
            dx2 = r2 * (dxn2 - xn2 * jnp.mean(dxn2 * xn2, axis=-1, keepdims=True))
            dx2_ref[r, :] = dx2
            dgate = dgate + jnp.sum(dx2 * y, axis=0, keepdims=True)
            dys.append((dx2 * gate).astype(BF16))
        loss_ref[...] += jnp.broadcast_to(loss * (0.5 / D_MODEL), loss_ref.shape)
        gfg_ref[...] += gfg
        dgate_ref[0] += dgate
        gw_ref[...] += _dot_tn(jnp.concatenate(ypres, axis=0), jnp.concatenate(dys, axis=0))
        dypres = [_dot(dy, wt_ref[...]) for dy in dys]
        for r, dypre, o, g, sg, sil in zip(subs, dypres, os_, gs, sgs, sils):
            do_ref[r, :] = (dypre * sil).astype(BF16)
            dg_ref[r, :] = (dypre * o * (sg * (1.0 + g * (1.0 - sg)))).astype(BF16)

    tok = lambda w: pl.BlockSpec((tm, w), lambda i: (i, 0))
    per_b = pl.BlockSpec((1, 1, 3 * D_MODEL), lambda i: (i // per_seq, 0, 0))
    return pl.pallas_call(
        body, name="post", grid=(n_tok // tm,),
        out_shape=[jax.ShapeDtypeStruct((n_tok, D_MODEL), F32), jax.ShapeDtypeStruct((n_tok, D_MODEL), BF16),
                   jax.ShapeDtypeStruct((n_tok, D_MODEL), BF16), jax.ShapeDtypeStruct((D_MODEL, D_MODEL), F32),
                   jax.ShapeDtypeStruct((1, D_MODEL), F32), jax.ShapeDtypeStruct((n_seq, 1, D_MODEL), F32),
                   jax.ShapeDtypeStruct((1, HEAD_LANES), F32)],
        in_specs=[tok(D_MODEL), tok(D_MODEL), tok(512), tok(512), tok(D_MODEL), per_b, _full(b_ada.shape),
                  _full(fg.shape), _full(w_out.shape), _full(w_out_t.shape)],
        out_specs=[tok(D_MODEL), tok(D_MODEL), tok(D_MODEL), _full((D_MODEL, D_MODEL)), _full((1, D_MODEL)),
                   pl.BlockSpec((1, 1, D_MODEL), lambda i: (i // per_seq, 0, 0)), _full((1, HEAD_LANES))],
        compiler_params=_params(1),
    )(x, target, o_mla, o_swa, gates, mod, b_ada, fg, w_out, w_out_t)


def _mid_bwd_call(dqf, dkf, dv, zqkv, pos_col, qg, kvg, inv128, wq2, wkv, seq):
    n_tok = dqf.shape[0]
    tm = min(TOKEN_TILE, seq)

    def body(dq_ref, dk_ref, dv_ref, z_ref, pos_ref, qg_ref, kvg_ref, inv_ref, wq_ref, wkv_ref,
             dz_ref, dkr_ref, gwq_ref, gwkv_ref, gqg_ref, gkvg_ref):
        i = pl.program_id(0)

        @pl.when(i == 0)
        def _():
            gwq_ref[...] = jnp.zeros_like(gwq_ref)
            gwkv_ref[...] = jnp.zeros_like(gwkv_ref)
            gqg_ref[...] = jnp.zeros_like(gqg_ref)
            gkvg_ref[...] = jnp.zeros_like(gkvg_ref)

        cos, sin = _rope_tables(pos_ref[...], inv_ref[...])
        cf, sf = jnp.tile(cos, (1, N_HEADS)), jnp.tile(sin, (1, N_HEADS))
        dq = dq_ref[...] * MLA_SCALE
        dqr = jnp.concatenate([dq * cf, dq * sf], axis=-1).astype(BF16)
        zq, zkv = z_ref[:, :Q_LORA], z_ref[:, Q_LORA:]
        qgv, kvgv = qg_ref[...], kvg_ref[...]

        rq = lax.rsqrt(jnp.mean(zq * zq, axis=-1, keepdims=True) + EPS)
        xq = zq * rq
        gwq_ref[...] += _dot_tn((xq * qgv).astype(BF16), dqr)
        dqn = _dot_nt(dqr, wq_ref[...])
        gqg_ref[...] += jnp.sum(dqn * xq, axis=0, keepdims=True)
        dxq = dqn * qgv
        dz_ref[:, :Q_LORA] = (rq * (dxq - xq * jnp.mean(dxq * xq, axis=-1, keepdims=True))).astype(BF16)

        dk = dk_ref[...] * LN2
        dkv = jnp.concatenate([dk, dv_ref[...]], axis=-1).astype(BF16)
        rkv = lax.rsqrt(jnp.mean(zkv * zkv, axis=-1, keepdims=True) + EPS)
        xkv = zkv * rkv
        gwkv_ref[...] += _dot_tn((xkv * kvgv).astype(BF16), dkv)
        dkvn = _dot_nt(dkv, wkv_ref[...])
        gkvg_ref[...] += jnp.sum(dkvn * xkv, axis=0, keepdims=True)
        dxkv = dkvn * kvgv
        dz_ref[:, Q_LORA:] = (rkv * (dxkv - xkv * jnp.mean(dxkv * xkv, axis=-1, keepdims=True))).astype(BF16)

        dkpe = dk[:, :HEAD_LANES]
        for h in range(1, N_HEADS):
            dkpe = dkpe + dk[:, h * HEAD_LANES:(h + 1) * HEAD_LANES]
        dkr_ref[:, :HEAD_LANES] = (dkpe * cos).astype(BF16)
        dkr_ref[:, HEAD_LANES:] = (dkpe * sin).astype(BF16)

    tok = lambda w: pl.BlockSpec((tm, w), lambda i: (i, 0))
    return pl.pallas_call(
        body, name="mid_bwd", grid=(n_tok // tm,),
        out_shape=[jax.ShapeDtypeStruct((n_tok, 640), BF16), jax.ShapeDtypeStruct((n_tok, 256), BF16),
                   jax.ShapeDtypeStruct(wq2.shape, F32), jax.ShapeDtypeStruct(wkv.shape, F32),
                   jax.ShapeDtypeStruct((1, Q_LORA), F32), jax.ShapeDtypeStruct((1, KV_LORA), F32)],
        in_specs=[tok(1024), tok(1024), tok(512), tok(640), tok(1), _full(qg.shape), _full(kvg.shape),
                  _full(inv128.shape), _full(wq2.shape), _full(wkv.shape)],
        out_specs=[tok(640), tok(256), _full(wq2.shape), _full(wkv.shape), _full((1, Q_LORA)), _full((1, KV_LORA))],
        compiler_params=_params(1),
    )(dqf, dkf, dv, zqkv, pos_col, qg, kvg, inv128, wq2, wkv)


def _in_bwd_call(x, dx2, dz, dkr, dg, dqs, dkd, dvd, mod, b_ada, ng, wa_t, wkr2_t, seq):
    n_tok = x.shape[0]
    tm = min(TOKEN_TILE, seq)
    per_seq = seq // tm
    n_seq = n_tok // seq

    def body(x_ref, dx2_ref, dz_ref, dkr_ref, dg_ref, dqs_ref, dkd_ref, dvd_ref, mod_ref, bada_ref, ng_ref,
             wat_ref, wkrt_ref, gx_ref, gwa_ref, gwkr_ref, gng_ref, dshift_ref, dscale_ref):
        i = pl.program_id(0)

        @pl.when(i == 0)
        def _():
            gwa_ref[...] = jnp.zeros_like(gwa_ref)
            gwkr_ref[...] = jnp.zeros_like(gwkr_ref)
            gng_ref[...] = jnp.zeros_like(gng_ref)

        @pl.when(i % per_seq == 0)
        def _():
            dshift_ref[...] = jnp.zeros_like(dshift_ref)
            dscale_ref[...] = jnp.zeros_like(dscale_ref)

        xv = x_ref[...]
        modv = mod_ref[0] + bada_ref[...]
        shift, scale = modv[:, :D_MODEL], modv[:, D_MODEL:2 * D_MODEL]
        ngv = ng_ref[...]
        r1 = lax.rsqrt(jnp.mean(xv * xv, axis=-1, keepdims=True) + EPS)
        xn = xv * r1
        hb = ((xn * ngv) * (1.0 + scale) + shift).astype(BF16)

        dgv = dg_ref[...]
        pieces = [(A_ZQ, dz_ref[...]), (A_GM, dgv[:, :512]), (A_QS, dqs_ref[...].astype(BF16)),
                  (A_KD, dkd_ref[...].astype(BF16)), (A_VD, dvd_ref[...].astype(BF16)), (A_GS, dgv[:, 512:])]
        dkr = dkr_ref[...]
        gwkr_ref[...] += _dot_tn(hb, dkr)
        dh = _dot(dkr, wkrt_ref[...])
        for off, piece in pieces:
            wd = piece.shape[1]
            gwa_ref[:, off:off + wd] += _dot_tn(hb, piece)
            dh = dh + _dot(piece, wat_ref[off:off + wd, :])

        dshift_ref[0] += jnp.sum(dh, axis=0, keepdims=True)
        dscale_ref[0] += jnp.sum(dh * (xn * ngv), axis=0, keepdims=True)
        gng_ref[...] += jnp.sum(dh * xn * (1.0 + scale), axis=0, keepdims=True)
        dxn = dh * ngv * (1.0 + scale)
        gx_ref[...] = dx2_ref[...] + r1 * (dxn - xn * jnp.mean(dxn * xn, axis=-1, keepdims=True))

    tok = lambda w: pl.BlockSpec((tm, w), lambda i: (i, 0))
    per_b = lambda w: pl.BlockSpec((1, 1, w), lambda i: (i // per_seq, 0, 0))
    return pl.pallas_call(
        body, name="in_bwd", grid=(n_tok // tm,),
        out_shape=[jax.ShapeDtypeStruct((n_tok, D_MODEL), F32), jax.ShapeDtypeStruct((D_MODEL, A_END), F32),
                   jax.ShapeDtypeStruct((D_MODEL, 256), F32), jax.ShapeDtypeStruct((1, D_MODEL), F32),
                   jax.ShapeDtypeStruct((n_seq, 1, D_MODEL), F32), jax.ShapeDtypeStruct((n_seq, 1, D_MODEL), F32)],
        in_specs=[tok(D_MODEL), tok(D_MODEL), tok(640), tok(256), tok(D_MODEL), tok(512), tok(256), tok(256),
                  per_b(3 * D_MODEL), _full(b_ada.shape), _full(ng.shape), _full(wa_t.shape), _full(wkr2_t.shape)],
        out_specs=[tok(D_MODEL), _full((D_MODEL, A_END)), _full((D_MODEL, 256)), _full((1, D_MODEL)),
                   per_b(D_MODEL), per_b(D_MODEL)],
        compiler_params=_params(1),
    )(x, dx2, dz, dkr, dg, dqs, dkd, dvd, mod, b_ada, ng, wa_t, wkr2_t)


def _adam_math(w, g, m, v):
    m_new = ADAM_B1 * m + (1.0 - ADAM_B1) * g
    v_new = ADAM_B2 * v + (1.0 - ADAM_B2) * (g * g)
    m_hat = m_new / (1.0 - ADAM_B1 ** ADAM_STEP)
    v_hat = v_new / (1.0 - ADAM_B2 ** ADAM_STEP)
    delta = -ADAM_LR * (m_hat / (jnp.sqrt(v_hat) + ADAM_EPS) + ADAM_WD * w)
    return delta, m_new, v_new


def _adam_call(name, w, g, m, v):
    rows, cols = w.shape
    tr = 256 if rows % 256 == 0 else rows

    def body(w_ref, g_ref, m_ref, v_ref, d_ref, mo_ref, vo_ref):
        d, mn, vn = _adam_math(w_ref[...], g_ref[...], m_ref[...], v_ref[...])
        d_ref[...] = d
        mo_ref[...] = mn
        vo_ref[...] = vn

    spec = pl.BlockSpec((tr, cols), lambda i: (i, 0))
    return pl.pallas_call(
        body, name=name, grid=(rows // tr,),
        out_shape=[jax.ShapeDtypeStruct(w.shape, F32)] * 3,
        in_specs=[spec] * 4, out_specs=[spec] * 3,
        compiler_params=_params(1),
    )(w, g, m, v)


def _ada_bwd_call(act_all, dmod_cols, w, m, v):
    rows, cols = w.shape
    tr = 256

    def body(a_ref, dm_ref, w_ref, m_ref, v_ref, g_ref, d_ref, mo_ref, vo_ref):
        g = _dot_tn(a_ref[...].astype(BF16), dm_ref[...].astype(BF16))
        d, mn, vn = _adam_math(w_ref[...], g, m_ref[...], v_ref[...])
        g_ref[...] = g
        d_ref[...] = d
        mo_ref[...] = mn
        vo_ref[...] = vn

    spec = pl.BlockSpec((tr, cols), lambda i: (i, 0))
    nb = act_all.shape[0]
    return pl.pallas_call(
        body, name="ada_bwd", grid=(rows // tr,),
        out_shape=[jax.ShapeDtypeStruct(w.shape, F32)] * 4,
        in_specs=[pl.BlockSpec((nb, tr), lambda i: (0, i)), _full(dmod_cols.shape), spec, spec, spec],
        out_specs=[spec] * 4,
        compiler_params=_params(1),
    )(act_all, dmod_cols, w, m, v)


SMALL_ROW = {"norm_gain": (0, 1024), "final_gain": (1024, 2048), "q_norm_gain": (2048, 2432),
             "kv_norm_gain": (2432, 2688), "swa_sinks": (2688, 2696), "loss": (2816, 2944)}
SMALL_ORDER = ("b_ada", "norm_gain", "q_norm_gain", "kv_norm_gain", "swa_sinks", "final_gain")


def _small_call(parts_all, n_seq, params):
    k = len(params)

    def body(p_ref, *refs):
        ins, outs, loss_ref = refs[:3 * k], refs[3 * k:7 * k], refs[7 * k]
        row = p_ref[n_seq:n_seq + 1, :]
        for dv in range(1, 8):
            r0 = dv * ROWS_PER_DEVICE + n_seq
            row = row + p_ref[r0:r0 + 1, :]
        gb = None
        for dv in range(8):
            for r in range(n_seq):
                r0 = dv * ROWS_PER_DEVICE + r
                gb = p_ref[r0:r0 + 1, :] if gb is None else gb + p_ref[r0:r0 + 1, :]
        for j, name in enumerate(SMALL_ORDER):
            g = gb if name == "b_ada" else row[:, SMALL_ROW[name][0]:SMALL_ROW[name][1]]
            d, mn, vn = _adam_math(ins[3 * j][...], g, ins[3 * j + 1][...], ins[3 * j + 2][...])
            outs[4 * j][...] = g
            outs[4 * j + 1][...] = d
            outs[4 * j + 2][...] = mn
            outs[4 * j + 3][...] = vn
        loss_ref[...] = row[:, SMALL_ROW["loss"][0]:SMALL_ROW["loss"][1]]

    flat = [t for p in params for t in p]
    res = pl.pallas_call(
        body, name="small_update", grid=(1,),
        out_shape=[jax.ShapeDtypeStruct(p[0].shape, F32) for p in params for _ in range(4)]
        + [jax.ShapeDtypeStruct((1, HEAD_LANES), F32)],
        in_specs=[_full(parts_all.shape)] + [_full(t.shape) for t in flat],
        out_specs=[_full(p[0].shape) for p in params for _ in range(4)] + [_full((1, HEAD_LANES))],
        compiler_params=_params(1),
    )(parts_all, *flat)
    return [res[4 * j:4 * j + 4] for j in range(k)], res[4 * k]


def _rot(t):
    half = t.shape[-1] // 2
    return jnp.concatenate([-t[..., half:], t[..., :half]], axis=-1)


def _rot_t(g):
    half = g.shape[-1] // 2
    return jnp.concatenate([g[..., half:], -g[..., :half]], axis=-1)


def _prepare_weights(w_in, w_uq, w_ukv):
    o = [0]
    for s in IN_SPLITS:
        o.append(o[-1] + s)
    ks, vs = w_in[:, o[5]:o[6]], w_in[:, o[6]:o[7]]
    dup = lambda t: jnp.concatenate([t[:, :64], t[:, :64], t[:, 64:], t[:, 64:]], axis=1)
    wa = jnp.concatenate([w_in[:, :o[2]], w_in[:, o[3]:o[5]], dup(ks), dup(vs), w_in[:, o[7]:]], axis=1)
    kr = w_in[:, o[2]:o[3]]
    zc = lambda n: jnp.zeros((w_in.shape[0], n), w_in.dtype)
    wkr2 = jnp.concatenate([zc(64), kr, zc(32), zc(64), _rot(kr), zc(32)], axis=1)
    uq = w_uq.reshape(Q_LORA, N_HEADS, MLA_NOPE + MLA_ROPE)
    zq = jnp.zeros((Q_LORA, N_HEADS, 32), w_uq.dtype)
    uq_full = jnp.concatenate([uq, zq], axis=-1).reshape(Q_LORA, 1024)
    uq_rot = jnp.concatenate([jnp.zeros((Q_LORA, N_HEADS, 64), w_uq.dtype), _rot(uq[..., MLA_NOPE:]), zq],
                             axis=-1).reshape(Q_LORA, 1024)
    wq2 = jnp.concatenate([uq_full, uq_rot], axis=1)
    ukv = w_ukv.reshape(KV_LORA, N_HEADS, 128)
    k_full = jnp.concatenate([ukv[..., :64], jnp.zeros((KV_LORA, N_HEADS, 64), w_ukv.dtype)], axis=-1).reshape(KV_LORA, 1024)
    wkv = jnp.concatenate([k_full, ukv[..., 64:].reshape(KV_LORA, 512)], axis=1)
    return wa, wkr2, wq2, wkv


def _restore_grads(gwa, gwkr2, gwq2, gwkv):
    fold = lambda g: jnp.concatenate([g[:, 0:64] + g[:, 64:128], g[:, 128:192] + g[:, 192:256]], axis=1)
    gkr = gwkr2[:, 64:96] + _rot_t(gwkr2[:, 192:224])
    g_in = jnp.concatenate([gwa[:, :A_GM], gkr, gwa[:, A_GM:A_KD], fold(gwa[:, A_KD:A_VD]), fold(gwa[:, A_VD:A_GS]),
                            gwa[:, A_GS:]], axis=1)
    gf = gwq2[:, :1024].reshape(Q_LORA, N_HEADS, 128)
    gr = gwq2[:, 1024:].reshape(Q_LORA, N_HEADS, 128)
    g_uq = jnp.concatenate([gf[..., :64], gf[..., 64:96] + _rot_t(gr[..., 64:96])], axis=-1).reshape(Q_LORA, 768)
    gk = gwkv[:, :1024].reshape(KV_LORA, N_HEADS, 128)[..., :64]
    gv = gwkv[:, 1024:].reshape(KV_LORA, N_HEADS, 64)
    g_ukv = jnp.concatenate([gk, gv], axis=-1).reshape(KV_LORA, 1024)
    return g_in, g_uq, g_ukv


def _local_step(x, positions, target, mod_rows, b_ada, ng, qg, kvg, sinks, fg, w_in_b, w_uq_b, w_ukv_b, w_out_b):
    n_seq, seq, _ = x.shape
    n_tok = n_seq * seq
    x2d = x.reshape(n_tok, D_MODEL)
    t2d = target.reshape(n_tok, D_MODEL)
    pos_f = positions.astype(F32)
    pos_col = pos_f.reshape(n_tok, 1)
    pos_row = pos_f.reshape(n_tok // SWA_WINDOW, 1, SWA_WINDOW)
    mod3 = mod_rows.reshape(n_seq, 1, 3 * D_MODEL)
    inv = ROPE_THETA ** (-jnp.arange(0, MLA_ROPE, 2, dtype=F32) / MLA_ROPE)
    inv128 = jnp.concatenate([jnp.zeros((64,), F32), inv, inv, jnp.zeros((32,), F32)]).reshape(1, 128)
    fg2 = fg.reshape(1, D_MODEL)

    wa, wkr2, wq2, wkv = _prepare_weights(w_in_b, w_uq_b, w_ukv_b)

    zqkv, gates, qf, kf, v, qs, kd, vd = _pre_call(x2d, pos_col, mod3, b_ada, ng, qg, kvg, inv128, wa, wkr2, wq2, wkv, seq)
    o_mla, lse_mla = _mla_fwd_call(qf, kf, v, n_seq, seq)
    o_swa, lse_swa = _swa_fwd_call(qs, kd, vd, pos_col, pos_row, sinks, n_seq, seq)
    dx2, do, dg, g_out, g_fg, dgate, loss = _post_call(x2d, t2d, o_mla, o_swa, gates, mod3, b_ada, fg2, w_out_b, w_out_b.T, seq)
    dqf, dkf, dv = _mla_bwd_call(qf, kf, v, do, o_mla, lse_mla, n_seq, seq)
    dqs, dkd, dvd, dsink = _swa_bwd_call(qs, kd, vd, do, o_swa, lse_swa, pos_col, pos_row, sinks, n_seq, seq)
    dz, dkr, g_wq2, g_wkv, g_qg, g_kvg = _mid_bwd_call(dqf, dkf, dv, zqkv, pos_col, qg, kvg, inv128, wq2, wkv, seq)
    gx, g_wa, g_wkr2, g_ng, dshift, dscale = _in_bwd_call(x2d, dx2, dz, dkr, dg, dqs, dkd, dvd, mod3, b_ada, ng,
                                                         wa.T, wkr2.T, seq)
    g_in, g_uq, g_ukv = _restore_grads(g_wa, g_wkr2, g_wq2, g_wkv)
    dmod = jnp.concatenate([dshift, dscale, dgate], axis=-1).reshape(n_seq, 3 * D_MODEL)
    small_row = jnp.concatenate([g_ng, g_fg, g_qg, g_kvg, jnp.pad(jnp.sum(dsink, axis=1).reshape(1, N_HEADS), ((0, 0), (0, 120))),
                                 loss, jnp.zeros((1, 128), F32)], axis=1)
    return gx.reshape(x.shape), (g_in, g_uq, g_ukv, g_out), small_row, dmod


def kernel(x, c, positions, w_ada, b_ada, norm_gain, w_in, q_norm_gain, kv_norm_gain, w_uq, w_ukv, swa_sinks, w_out, final_gain, loss_target, m_w_ada, m_b_ada, m_norm_gain, m_w_in, m_q_norm_gain, m_kv_norm_gain, m_w_uq, m_w_ukv, m_swa_sinks, m_w_out, m_final_gain, v_w_ada, v_b_ada, v_norm_gain, v_w_in, v_q_norm_gain, v_kv_norm_gain, v_w_uq, v_w_ukv, v_swa_sinks, v_w_out, v_final_gain):
    n_seq = x.shape[0]
    xi, yi, ci = lax.axis_index("x"), lax.axis_index("y"), lax.axis_index("c")
    dev = 4 * xi + 2 * yi + ci
    chip = 2 * xi + yi

    halves = lambda w: w.astype(BF16).reshape(2, w.shape[0] // 2, w.shape[1])
    c_blk = jnp.pad(c, ((0, ROWS_PER_DEVICE - n_seq), (0, 0)))
    act_all, pieces, f_in, f_uq, f_ukv, f_out = _comm_fwd_call(
        c_blk, w_ada[0], [halves(w_in[0]), halves(w_uq[0]), halves(w_ukv[0]), halves(w_out[0])])
    mine = lax.dynamic_slice_in_dim(pieces, dev * ROWS_PER_DEVICE, n_seq, axis=1)
    mod_rows = jnp.transpose(mine, (1, 0, 2)).reshape(n_seq, 3 * D_MODEL)
    cols = lambda t, r: jnp.transpose(t.reshape(4, r, -1), (1, 0, 2)).reshape(r, -1)
    w_in_b, w_uq_b, w_ukv_b = cols(f_in, D_MODEL), cols(f_uq, Q_LORA), cols(f_ukv, KV_LORA)
    w_out_b = f_out.reshape(D_MODEL, D_MODEL)

    gx, (g_in, g_uq, g_ukv, g_out), small_row, dmod = _local_step(
        x, positions, loss_target, mod_rows, b_ada, norm_gain, q_norm_gain, kv_norm_gain, swa_sinks, final_gain,
        w_in_b, w_uq_b, w_ukv_b, w_out_b)

    by_owner = lambda g, n: jnp.transpose(g.reshape(2, g.shape[0] // 2, 4, n), (0, 2, 1, 3))
    grads = [by_owner(g_in, 616), by_owner(g_uq, 192), by_owner(g_ukv, 256),
             jnp.transpose(g_out.reshape(4, 2, 128, D_MODEL), (1, 0, 2, 3))]
    part = jnp.concatenate([dmod, small_row, jnp.zeros((ROWS_PER_DEVICE - n_seq - 1, 3 * D_MODEL), F32)], axis=0)
    r_in, r_uq, r_ukv, r_out, parts_all = _comm_bwd_call(grads, part)
    g_in_s, g_uq_s = r_in.reshape(w_in.shape[1:]), r_uq.reshape(w_uq.shape[1:])
    g_ukv_s, g_out_s = r_ukv.reshape(w_ukv.shape[1:]), r_out.reshape(w_out.shape[1:])

    d_in, nm_in, nv_in = _adam_call("adam_w_in", w_in[0], g_in_s, m_w_in[0], v_w_in[0])
    d_uq, nm_uq, nv_uq = _adam_call("adam_w_uq", w_uq[0], g_uq_s, m_w_uq[0], v_w_uq[0])
    d_ukv, nm_ukv, nv_ukv = _adam_call("adam_w_ukv", w_ukv[0], g_ukv_s, m_w_ukv[0], v_w_ukv[0])
    d_out, nm_out, nv_out = _adam_call("adam_w_out", w_out[0], g_out_s, m_w_out[0], v_w_out[0])
    dmod_cols = lax.dynamic_slice_in_dim(parts_all, chip * 768, 768, axis=1)
    g_ada, d_ada, nm_ada, nv_ada = _ada_bwd_call(act_all, dmod_cols, w_ada[0], m_w_ada[0], v_w_ada[0])

    row = lambda t: t.reshape(1, -1)
    small = {"b_ada": (b_ada, m_b_ada, v_b_ada), "norm_gain": (norm_gain, m_norm_gain, v_norm_gain),
             "q_norm_gain": (q_norm_gain, m_q_norm_gain, v_q_norm_gain),
             "kv_norm_gain": (kv_norm_gain, m_kv_norm_gain, v_kv_norm_gain),
             "swa_sinks": (swa_sinks, m_swa_sinks, v_swa_sinks),
             "final_gain": (row(final_gain), row(m_final_gain), row(v_final_gain))}
    res, loss_row = _small_call(parts_all, n_seq, [small[name] for name in SMALL_ORDER])
    res = dict(zip(SMALL_ORDER, res))
    res["final_gain"] = [t.reshape(-1) for t in res["final_gain"]]
    e = lambda t: t[None]
    big = {"w_ada": (e(g_ada), e(d_ada), e(nm_ada), e(nv_ada)), "w_in": (e(g_in_s), e(d_in), e(nm_in), e(nv_in)),
           "w_uq": (e(g_uq_s), e(d_uq), e(nm_uq), e(nv_uq)), "w_ukv": (e(g_ukv_s), e(d_ukv), e(nm_ukv), e(nv_ukv)),
           "w_out": (e(g_out_s), e(d_out), e(nm_out), e(nv_out))}
    order = ("w_ada", "b_ada", "norm_gain", "w_in", "q_norm_gain", "kv_norm_gain", "w_uq", "w_ukv", "swa_sinks", "w_out",
             "final_gain")
    pick = lambda kind: [(big[n] if n in big else res[n])[kind] for n in order]
    return (loss_row[0, 0], gx, *pick(0), *pick(1), *pick(2), *pick(3))
```

```python
import functools

import jax
import jax.numpy as jnp
from jax import lax
from jax.experimental import pallas as pl
from jax.experimental.pallas import tpu as pltpu

F32 = jnp.float32
BF16 = jnp.bfloat16

D_MODEL = 1024
Q_LORA = 384
KV_LORA = 256
N_HEADS = 8
MLA_NOPE = 64
MLA_ROPE = 32
HEAD_LANES = 128
HALF = 64
SWA_WINDOW = 128
EPS = 1e-6
ROPE_THETA = 10000.0
MLA_SCALE = (MLA_NOPE + MLA_ROPE) ** -0.5
LOG2E = 1.4426950408889634
LN2 = 0.6931471805599453
SWA_SCALE = 64 ** -0.5
NEG = -1e30

ADAM_LR = 0.001
ADAM_B1 = 0.9
ADAM_B2 = 0.999
ADAM_EPS = 1e-08
ADAM_WD = 0.01
ADAM_STEP = 10

A_ZQ, A_ZKV, A_GM, A_QS, A_KD, A_VD, A_GS, A_END = 0, 384, 640, 1152, 1664, 1920, 2176, 2688
IN_SPLITS = (384, 256, 32, 512, 512, 128, 128, 512)
D_IN = sum(IN_SPLITS)

TOKEN_TILE = 512
ATT_TILE = 256
VMEM_LIMIT = 56 * 1024 * 1024


def _dot(a, b):
    return jnp.dot(a, b, preferred_element_type=F32)


def _dot_nt(a, b):
    return lax.dot_general(a, b, (((1,), (1,)), ((), ())), preferred_element_type=F32)


def _dot_tn(a, b):
    return lax.dot_general(a, b, (((0,), (0,)), ((), ())), preferred_element_type=F32)


def _params(n_grid):
    return pltpu.CompilerParams(dimension_semantics=("arbitrary",) * n_grid, vmem_limit_bytes=VMEM_LIMIT)


def _full(shape):
    nd = len(shape)
    return pl.BlockSpec(shape, lambda *_: (0,) * nd, pipeline_mode=pl.Buffered(1))


def _sigmoid(g):
    return 1.0 / (1.0 + jnp.exp(-g))


SUB_TILE = 256


def _sub_tiles(tm):
    sub = min(SUB_TILE, tm)
    return [slice(s * sub, (s + 1) * sub) for s in range(tm // sub)]


MESH = pl.DeviceIdType.MESH
ROWS_PER_DEVICE = 8
VMEM_SPEC = pl.BlockSpec(memory_space=pltpu.VMEM)
ANY_SPEC = pl.BlockSpec(memory_space=pl.ANY)


def _position():
    x, y, c = lax.axis_index("x"), lax.axis_index("y"), lax.axis_index("c")
    sibling = (x, y, 1 - c)
    others = [(1 - x, y, c), (x, 1 - y, c), (1 - x, 1 - y, c)]
    return (x, y, c), 4 * x + 2 * y + c, 2 * x + y, sibling, others


def _rows_of(dev):
    return pl.ds(pl.multiple_of(dev * ROWS_PER_DEVICE, ROWS_PER_DEVICE), ROWS_PER_DEVICE)


def _all_to_all_rows(block_ref, table_ref, dev, me, send_sems, recv_sems):
    x, y, c = me
    waits = []
    for k in range(1, 8):
        peer = (1 - x if k & 4 else x, 1 - y if k & 2 else y, 1 - c if k & 1 else c)
        pltpu.make_async_remote_copy(src_ref=block_ref, dst_ref=table_ref.at[_rows_of(dev)], send_sem=send_sems.at[k - 1],
                                     recv_sem=recv_sems.at[k - 1], device_id=peer, device_id_type=MESH).start()
        waits.append(pltpu.make_async_remote_copy(
            src_ref=block_ref, dst_ref=table_ref.at[_rows_of(jnp.bitwise_xor(dev, k))], send_sem=send_sems.at[k - 1],
            recv_sem=recv_sems.at[k - 1], device_id=peer, device_id_type=MESH))
    return waits


def _comm_fwd_call(c_blk, w_ada, shards):
    n = len(shards)

    def body(c_ref, wada_ref, *refs):
        w_refs, act_ref, pieces_ref, full_refs = refs[:n], refs[n], refs[n + 1], refs[n + 2:2 * n + 2]
        c_all_ref = refs[2 * n + 2]
        c_send, c_recv, p_send, p_recv, w_send, w_recv, f_send, f_recv, loc_sem = refs[2 * n + 3:]
        me, dev, chip, sibling, others = _position()
        core = me[2]
        chip_of = [2 * p[0] + p[1] for p in others]

        local = [pltpu.make_async_copy(w_refs[i], full_refs[i].at[chip], loc_sem.at[i]) for i in range(n)]
        for cp in local:
            cp.start()

        def over_ici(i, j, src_chip):
            return pltpu.make_async_remote_copy(
                src_ref=w_refs[i].at[core], dst_ref=full_refs[i].at[src_chip, core], send_sem=w_send.at[3 * i + j],
                recv_sem=w_recv.at[3 * i + j], device_id=others[j], device_id_type=MESH)

        def to_sibling(i, j, half):
            return pltpu.make_async_remote_copy(
                src_ref=full_refs[i].at[chip_of[j], half], dst_ref=full_refs[i].at[chip_of[j], half],
                send_sem=f_send.at[3 * i + j], recv_sem=f_recv.at[3 * i + j], device_id=sibling, device_id_type=MESH)

        sent = [over_ici(i, j, chip) for i in range(n) for j in range(3)]
        for cp in sent:
            cp.start()

        c_all_ref[_rows_of(dev), :] = c_ref[...]
        c_waits = _all_to_all_rows(c_ref, c_all_ref, dev, me, c_send, c_recv)
        for cp in c_waits:
            cp.wait()
        cv = c_all_ref[...]
        act = cv * _sigmoid(cv)
        act_ref[...] = act
        pieces_ref[chip] = _dot(act.astype(BF16), wada_ref[...].astype(BF16))
        piece = lambda j, src_chip: pltpu.make_async_remote_copy(
            src_ref=pieces_ref.at[chip], dst_ref=pieces_ref.at[src_chip], send_sem=p_send.at[j], recv_sem=p_recv.at[j],
            device_id=others[j], device_id_type=MESH)
        for j in range(3):
            piece(j, chip).start()
        for j in range(3):
            piece(j, chip).wait_send()
            piece(j, chip_of[j]).wait_recv()

        for i in range(n):
            for j in range(3):
                over_ici(i, j, chip_of[j]).wait_recv()
                to_sibling(i, j, core).start()
        for i in range(n):
            for j in range(3):
                to_sibling(i, j, 1 - core).wait_recv()
                to_sibling(i, j, core).wait_send()
        for cp in sent:
            cp.wait_send()
        for cp in local:
            cp.wait()

    rows = 8 * ROWS_PER_DEVICE
    dma = pltpu.SemaphoreType.DMA
    return pl.pallas_call(
        body, name="comm_fwd",
        out_shape=[jax.ShapeDtypeStruct((rows, D_MODEL), F32), jax.ShapeDtypeStruct((4, rows, w_ada.shape[1]), F32)]
        + [jax.ShapeDtypeStruct((4,) + s.shape, s.dtype) for s in shards],
        in_specs=[VMEM_SPEC, VMEM_SPEC] + [ANY_SPEC] * n,
        out_specs=[VMEM_SPEC, VMEM_SPEC] + [ANY_SPEC] * n,
        scratch_shapes=[pltpu.VMEM((rows, D_MODEL), F32), dma((7,)), dma((7,)), dma((3,)), dma((3,)),
                        dma((3 * n,)), dma((3 * n,)), dma((3 * n,)), dma((3 * n,)), dma((n,))],
        compiler_params=pltpu.CompilerParams(vmem_limit_bytes=VMEM_LIMIT),
    )(c_blk, w_ada, *shards)


def _comm_bwd_call(grads, part):
    n = len(grads)

    def body(part_ref, *refs):
        g_refs, f_refs, parts_ref = refs[:n], refs[n:2 * n], refs[2 * n]
        scratch = refs[2 * n + 1:]
        a_refs, b_refs, p_refs, r_refs = (scratch[k * n:(k + 1) * n] for k in range(4))
        s_send, s_recv, d_send, d_recv, e_send, e_recv, h_send, h_recv, loc_sem = scratch[4 * n:]
        me, dev, chip, sibling, others = _position()
        core = me[2]
        chip_of = [2 * p[0] + p[1] for p in others]

        parts_ref[_rows_of(dev), :] = part_ref[...]
        s_waits = _all_to_all_rows(part_ref, parts_ref, dev, me, s_send, s_recv)

        mine = [pltpu.make_async_copy(g_refs[i].at[core], a_refs[i], loc_sem.at[i]) for i in range(n)]
        swap = [pltpu.make_async_remote_copy(src_ref=g_refs[i].at[1 - core], dst_ref=b_refs[i], send_sem=d_send.at[i],
                                             recv_sem=d_recv.at[i], device_id=sibling, device_id_type=MESH) for i in range(n)]
        order = sorted(range(n), key=lambda i: g_refs[i].shape[2] * g_refs[i].shape[3])
        for i in order:
            mine[i].start()
            swap[i].start()
        cross = [pltpu.make_async_remote_copy(src_ref=p_refs[i].at[chip_of[j]], dst_ref=r_refs[i].at[j],
                                              send_sem=e_send.at[3 * i + j], recv_sem=e_recv.at[3 * i + j],
                                              device_id=others[j], device_id_type=MESH) for i in range(n) for j in range(3)]
        for i in order:
            mine[i].wait()
            swap[i].wait()
            for k in range(4):
                s = a_refs[i][k] + b_refs[i][k]
                a_refs[i][k] = s
                p_refs[i][k] = s.astype(BF16)
            for j in range(3):
                cross[3 * i + j].start()
        share = {}
        for i in order:
            for j in range(3):
                cross[3 * i + j].wait()
            f_refs[i][core] = (a_refs[i][chip] + r_refs[i][0].astype(F32) + r_refs[i][1].astype(F32)
                               + r_refs[i][2].astype(F32))
            share[i] = pltpu.make_async_remote_copy(src_ref=f_refs[i].at[core], dst_ref=f_refs[i].at[core],
                                                    send_sem=h_send.at[i], recv_sem=h_recv.at[i], device_id=sibling,
                                                    device_id_type=MESH)
            share[i].start()
        for i in range(n):
            share[i].wait_send()
            pltpu.make_async_remote_copy(src_ref=f_refs[i].at[core], dst_ref=f_refs[i].at[1 - core], send_sem=h_send.at[i],
                                         recv_sem=h_recv.at[i], device_id=sibling, device_id_type=MESH).wait_recv()
        for cp in s_waits:
            cp.wait()

    rows = 8 * ROWS_PER_DEVICE
    dma = pltpu.SemaphoreType.DMA
    quarter = [g.shape[1:] for g in grads]
    return pl.pallas_call(
        body, name="comm_bwd",
        out_shape=[jax.ShapeDtypeStruct((2,) + g.shape[2:], F32) for g in grads]
        + [jax.ShapeDtypeStruct((rows, part.shape[1]), F32)],
        in_specs=[VMEM_SPEC] + [ANY_SPEC] * n,
        out_specs=[VMEM_SPEC] * (n + 1),
        scratch_shapes=[pltpu.VMEM(q, F32) for q in quarter] + [pltpu.VMEM(q, F32) for q in quarter]
        + [pltpu.VMEM(q, BF16) for q in quarter] + [pltpu.VMEM((3,) + q[1:], BF16) for q in quarter]
        + [dma((7,)), dma((7,)), dma((n,)), dma((n,)), dma((3 * n,)), dma((3 * n,)), dma((n,)), dma((n,)), dma((n,))],
        compiler_params=pltpu.CompilerParams(vmem_limit_bytes=VMEM_LIMIT),
    )(part, *grads)


def _rope_tables(pos_col, inv_row):
    ang = pos_col * inv_row
    return jnp.cos(ang), jnp.sin(ang)


def _pre_call(x, pos_col, mod, b_ada, ng, qg, kvg, inv128, wa, wkr2, wq2, wkv, seq):
    n_tok = x.shape[0]
    tm = min(TOKEN_TILE, seq)
    per_seq = seq // tm

    def body(x_ref, pos_ref, mod_ref, bada_ref, ng_ref, qg_ref, kvg_ref, inv_ref, wa_ref, wkr_ref, wq_ref, wkv_ref,
             zqkv_ref, gates_ref, qf_ref, kf_ref, v_ref, qs_ref, kd_ref, vd_ref):
        xv = x_ref[...]
        modv = mod_ref[0] + bada_ref[...]
        shift, scale = modv[:, :D_MODEL], modv[:, D_MODEL:2 * D_MODEL]
        r1 = lax.rsqrt(jnp.mean(xv * xv, axis=-1, keepdims=True) + EPS)
        h = ((xv * r1) * ng_ref[...]) * (1.0 + scale) + shift
        hb = h.astype(BF16)
        za = _dot(hb, wa_ref[...])
        zkr = _dot(hb, wkr_ref[...])
        cos, sin = _rope_tables(pos_ref[...], inv_ref[...])
        zqkv_ref[...] = za[:, :A_GM]
        gates_ref[:, :512] = za[:, A_GM:A_QS]
        gates_ref[:, 512:] = za[:, A_GS:A_END]
        qs_ref[...] = (za[:, A_QS:A_KD] * (SWA_SCALE * LOG2E)).astype(BF16)
        kd_ref[...] = za[:, A_KD:A_VD].astype(BF16)
        vd_ref[...] = za[:, A_VD:A_GS].astype(BF16)
        zq, zkv = za[:, A_ZQ:A_ZKV], za[:, A_ZKV:A_GM]
        rq = lax.rsqrt(jnp.mean(zq * zq, axis=-1, keepdims=True) + EPS)
        qn = ((zq * rq) * qg_ref[...]).astype(BF16)
        qr = _dot(qn, wq_ref[...])
        cf, sf = jnp.tile(cos, (1, N_HEADS)), jnp.tile(sin, (1, N_HEADS))
        qf_ref[...] = ((qr[:, :1024] * cf + qr[:, 1024:] * sf) * (MLA_SCALE * LOG2E)).astype(BF16)
        rkv = lax.rsqrt(jnp.mean(zkv * zkv, axis=-1, keepdims=True) + EPS)
        kvn = ((zkv * rkv) * kvg_ref[...]).astype(BF16)
        kv = _dot(kvn, wkv_ref[...])
        kpe = zkr[:, :128] * cos + zkr[:, 128:] * sin
        kf_ref[...] = (kv[:, :1024] + jnp.tile(kpe, (1, N_HEADS))).astype(BF16)
        v_ref[...] = kv[:, 1024:].astype(BF16)

    tok = lambda w: pl.BlockSpec((tm, w), lambda i: (i, 0))
    outs = [(640, F32), (1024, F32), (1024, BF16), (1024, BF16), (512, BF16), (512, BF16), (256, BF16), (256, BF16)]
    return pl.pallas_call(
        body, name="pre", grid=(n_tok // tm,),
        out_shape=[jax.ShapeDtypeStruct((n_tok, w), dt) for w, dt in outs],
        in_specs=[tok(D_MODEL), tok(1), pl.BlockSpec((1, 1, 3 * D_MODEL), lambda i: (i // per_seq, 0, 0)),
                  _full(b_ada.shape), _full(ng.shape), _full(qg.shape), _full(kvg.shape), _full(inv128.shape),
                  _full(wa.shape), _full(wkr2.shape), _full(wq2.shape), _full(wkv.shape)],
        out_specs=[tok(w) for w, _ in outs],
        compiler_params=_params(1),
    )(x, pos_col, mod, b_ada, ng, qg, kvg, inv128, wa, wkr2, wq2, wkv)


def _lane_lo(width=HEAD_LANES):
    return lax.broadcasted_iota(jnp.int32, (1, width), 1) < HALF


def _eye(n=HEAD_LANES):
    r = lax.broadcasted_iota(jnp.int32, (n, n), 0)
    c = lax.broadcasted_iota(jnp.int32, (n, n), 1)
    return jnp.where(r == c, 1.0, 0.0).astype(BF16)


def _mla_fwd_call(qf, kf, v, n_seq, seq):
    tq = min(ATT_TILE, seq)
    nq = seq // tq

    ext = HALF + 16

    def body(q_ref, k_ref, v_ref, o_ref, lse_ref, vt_ref):
        i = pl.program_id(1)
        eye = _eye()

        @pl.when(i == 0)
        def _():
            for h in range(N_HEADS):
                vt_ref[h * ext + HALF:(h + 1) * ext, :] = jnp.ones((16, seq), BF16)
            for t in range(nq):
                for p in range(N_HEADS // 2):
                    pair = slice(p * HEAD_LANES, (p + 1) * HEAD_LANES)
                    v_t = _dot_nt(eye, v_ref[t * tq:(t + 1) * tq, pair]).astype(BF16)
                    for hh in range(2):
                        r0 = (2 * p + hh) * ext
                        vt_ref[r0:r0 + HALF, t * tq:(t + 1) * tq] = v_t[hh * HALF:(hh + 1) * HALF, :]

        q = q_ref[...]
        qcol = i * tq + lax.broadcasted_iota(jnp.int32, (1, tq), 1)
        heads = range(N_HEADS)
        lanes = [slice(h * HEAD_LANES, (h + 1) * HEAD_LANES) for h in heads]

        def make_step(masked, n_tiles):
            def step(kt0, carry):
                tiles = range(n_tiles)
                start = pl.multiple_of(kt0 * tq, tq)
                ks = [k_ref[pl.ds(pl.multiple_of((kt0 + t) * tq, tq), tq), :] for t in tiles]
                vt = vt_ref[:, pl.ds(start, n_tiles * tq)]
                sts = [[_dot_nt(ks[t][:, lanes[h]], q[:, lanes[h]]) for h in heads] for t in tiles]
                if masked:
                    last = n_tiles - 1
                    keep = ((kt0 + last) * tq + lax.broadcasted_iota(jnp.int32, (tq, 1), 0)) <= qcol
                    sts[last] = [jnp.where(keep, st, NEG) for st in sts[last]]
                stats, pts = [], []
                for h in heads:
                    m_old = carry[2 * h]
                    m_new = m_old
                    for t in tiles:
                        m_new = jnp.maximum(m_new, jnp.max(sts[t][h], axis=0, keepdims=True))
                    pts.append(jnp.concatenate([jnp.exp2(sts[t][h] - m_new).astype(BF16) for t in tiles], axis=0))
                    stats.append((m_new, jnp.exp2(m_old - m_new)))
                pvs = [_dot(vt[h * ext:(h + 1) * ext, :], pts[h]) for h in heads]
                out = []
                for h in heads:
                    out += [stats[h][0], carry[2 * h + 1] * stats[h][1] + pvs[h]]
                return tuple(out)
            return step

        init = (jnp.full((1, tq), NEG, F32), jnp.zeros((ext, tq), F32)) * N_HEADS
        count = i + 1
        carry = lax.fori_loop(0, (count + 1) // 2 - 1, lambda j, c: make_step(False, 2)(2 * j, c), init)
        carry = lax.cond(count % 2 == 0, lambda c: make_step(True, 2)(i - 1, c), lambda c: make_step(True, 1)(i, c), carry)
        dens = [carry[2 * h + 1][HALF:HALF + 1, :] for h in heads]
        acc_t = jnp.concatenate([carry[2 * h + 1][:HALF, :] * (1.0 / dens[h]) for h in heads], axis=0)
        o_ref[...] = acc_t.T
        for h in heads:
            lse_ref[0, h // 4, h % 4:h % 4 + 1, :] = carry[2 * h] + jnp.log2(dens[h])

    n_tok = qf.shape[0]
    return pl.pallas_call(
        body, name="mla_fwd", grid=(n_seq, nq),
        out_shape=[jax.ShapeDtypeStruct((n_tok, 512), F32), jax.ShapeDtypeStruct((n_seq, 2, 4, seq), F32)],
        in_specs=[pl.BlockSpec((tq, 1024), lambda b, i: (b * nq + i, 0)),
                  pl.BlockSpec((seq, 1024), lambda b, i: (b, 0)),
                  pl.BlockSpec((seq, 512), lambda b, i: (b, 0))],
        out_specs=[pl.BlockSpec((tq, 512), lambda b, i: (b * nq + i, 0)),
                   pl.BlockSpec((1, 2, 4, tq), lambda b, i: (b, 0, 0, i))],
        scratch_shapes=[pltpu.VMEM((N_HEADS * ext, seq), BF16)],
        compiler_params=_params(2),
    )(qf, kf, v)


def _mla_bwd_call(qf, kf, v, do, o, lse, n_seq, seq):
    tq = min(ATT_TILE, seq)
    nq = seq // tq

    nh = 4
    heads = range(nh)
    lanes = [slice(h * HEAD_LANES, (h + 1) * HEAD_LANES) for h in heads]

    def body(q_ref, k_ref, v_ref, do_ref, o_ref, lse_ref, dq_ref, dk_ref, dv_ref,
             kt_ref, dot_ref, delta_ref, dqt_ref):
        eye = _eye()
        lo = _lane_lo()
        sub_lo = lax.broadcasted_iota(jnp.int32, (HEAD_LANES, 1), 0) < HALF
        ones_lo = jnp.where(jnp.broadcast_to(lo, (8, HEAD_LANES)), 1.0, 0.0).astype(BF16)
        ones_hi = jnp.where(jnp.broadcast_to(lo, (8, HEAD_LANES)), 0.0, 1.0).astype(BF16)

        for t in range(nq):
            r = slice(t * tq, (t + 1) * tq)
            kv = k_ref[r, :]
            for h in heads:
                kt_ref[lanes[h], r] = _dot_nt(eye, kv[:, lanes[h]]).astype(BF16)
            for p in range(nh // 2):
                dov = do_ref[r, lanes[p]]
                dt = _dot_nt(eye, dov)
                dot_ref[2 * p, :, r] = jnp.where(sub_lo, dt, 0.0).astype(BF16)
                dot_ref[2 * p + 1, :, r] = jnp.where(sub_lo, 0.0, dt).astype(BF16)
                prod = dov.astype(F32) * o_ref[r, lanes[p]]
                p_hi = prod.astype(BF16)
                p_lo = (prod - p_hi.astype(F32)).astype(BF16)
                delta_ref[2 * p, :, r] = _dot_nt(ones_lo, p_hi) + _dot_nt(ones_lo, p_lo)
                delta_ref[2 * p + 1, :, r] = _dot_nt(ones_hi, p_hi) + _dot_nt(ones_hi, p_lo)
        dqt_ref[...] = jnp.zeros_like(dqt_ref)

        def k_step(kt, _):
            kr = pl.ds(pl.multiple_of(kt * tq, tq), tq)
            k = k_ref[kr, :]
            vv = v_ref[kr, :]
            k_t = kt_ref[:, kr]
            krow = kt * tq + lax.broadcasted_iota(jnp.int32, (tq, 1), 0)

            def make_step(masked, n_tiles):
                def q_step(qt0, carry):
                    tiles = range(n_tiles)
                    qrs = [pl.ds(pl.multiple_of((qt0 + t) * tq, tq), tq) for t in tiles]
                    qs = [q_ref[qr, :] for qr in qrs]
                    do_ts = [[dot_ref[h, :, qr] for h in heads] for qr in qrs]
                    sts = [[_dot_nt(k[:, lanes[h]], qs[t][:, lanes[h]]) for h in heads] for t in tiles]
                    dpts = [[_dot(vv[:, lanes[h // 2]], do_ts[t][h]) for h in heads] for t in tiles]
                    if masked:
                        keep = krow <= (qt0 * tq + lax.broadcasted_iota(jnp.int32, (1, tq), 1))
                    pts, dsts = [], []
                    for t in tiles:
                        pts.append([])
                        dsts.append([])
                        for h in heads:
                            pt = jnp.exp2(sts[t][h] - lse_ref[0, 0, h:h + 1, qrs[t]])
                            if masked and t == 0:
                                pt = jnp.where(keep, pt, 0.0)
                            dsts[t].append((pt * (dpts[t][h] - delta_ref[h, 0:1, qrs[t]])).astype(BF16))
                            pts[t].append(pt.astype(BF16))
                    out = []
                    for h in heads:
                        hh = h % 2
                        half = slice(hh * HALF, (hh + 1) * HALF)
                        dst_all = jnp.concatenate([dsts[t][h] for t in tiles], axis=1)
                        pt_all = jnp.concatenate([pts[t][h] for t in tiles], axis=1)
                        do_all = jnp.concatenate([do_ts[t][h][half, :] for t in tiles], axis=1)
                        q_all = jnp.concatenate([qs[t][:, lanes[h]] for t in tiles], axis=0)
                        dvt = _dot_nt(do_all, pt_all)
                        dk = _dot(dst_all, q_all)
                        for t in tiles:
                            dqt_ref[lanes[h], qrs[t]] += _dot(k_t[lanes[h], :], dsts[t][h])
                        out += [carry[2 * h] + dk, carry[2 * h + 1] + dvt]
                    return tuple(out)
                return q_step

            init = (jnp.zeros((tq, HEAD_LANES), F32), jnp.zeros((HALF, tq), F32)) * nh
            count = nq - kt
            carry = lax.cond(count >= 2, lambda c: make_step(True, 2)(kt, c), lambda c: make_step(True, 1)(kt, c), init)
            carry = lax.fori_loop(1, count // 2, lambda j, c: make_step(False, 2)(kt + 2 * j, c), carry)
            carry = lax.cond(jnp.logical_and(count % 2 == 1, count >= 3),
                             lambda c: make_step(False, 1)(nq - 1, c), lambda c: c, carry)
            for h in heads:
                dk_ref[kr, lanes[h]] = carry[2 * h]
            for p in range(nh // 2):
                dv_ref[kr, lanes[p]] = jnp.concatenate([carry[4 * p + 1], carry[4 * p + 3]], axis=0).T
            return 0

        lax.fori_loop(0, nq, k_step, 0)
        for t in range(nq):
            r = slice(t * tq, (t + 1) * tq)
            for h in heads:
                dq_ref[r, lanes[h]] = dqt_ref[lanes[h], r].T

    n_tok = qf.shape[0]
    groups = N_HEADS // nh
    blk = lambda w: pl.BlockSpec((seq, w), lambda b, g: (b, g))
    return pl.pallas_call(
        body, name="mla_bwd", grid=(n_seq, groups),
        out_shape=[jax.ShapeDtypeStruct((n_tok, 1024), F32), jax.ShapeDtypeStruct((n_tok, 1024), F32),
                   jax.ShapeDtypeStruct((n_tok, 512), F32)],
        in_specs=[blk(512), blk(512), blk(256), blk(256), blk(256),
                  pl.BlockSpec((1, 1, nh, seq), lambda b, g: (b, g, 0, 0))],
        out_specs=[blk(512), blk(512), blk(256)],
        scratch_shapes=[pltpu.VMEM((nh * HEAD_LANES, seq), BF16), pltpu.VMEM((nh, HEAD_LANES, seq), BF16),
                        pltpu.VMEM((nh, 8, seq), F32), pltpu.VMEM((nh * HEAD_LANES, seq), F32)],
        compiler_params=_params(2),
    )(qf, kf, v, do, o, lse)


SWA_BLOCKS = 2


def _swa_block(n, pos_col_ref, posq):
    w = SWA_WINDOW
    start = pl.multiple_of(jnp.maximum(n - 1, 0) * w, w)
    posk = pos_col_ref[pl.ds(start, 2 * w), :]
    rel = (n * w + lax.broadcasted_iota(jnp.int32, (1, w), 1)) - (start + lax.broadcasted_iota(jnp.int32, (2 * w, 1), 0))
    valid = jnp.logical_and(rel >= 0, rel < w)
    return start, jnp.where(valid, posq - posk, 1e30)


def _alibi(h):
    return LOG2E * 2.0 ** -(h + 1)


def _transpose_rows(eye, src_ref, dst_ref, seq, width):
    step = 2 * SWA_WINDOW
    for t in range(seq // step):
        for p in range(width // HEAD_LANES):
            lanes = slice(p * HEAD_LANES, (p + 1) * HEAD_LANES)
            dst_ref[lanes, t * step:(t + 1) * step] = _dot_nt(eye, src_ref[t * step:(t + 1) * step, lanes]).astype(BF16)


def _swa_fwd_call(qs, kd, vd, pos_col, pos_row, sinks, n_seq, seq):
    w = SWA_WINDOW
    qb = SWA_BLOCKS
    steps = seq // (qb * w)
    ext = HALF + 16

    def body(q_ref, k_ref, v_ref, pc_ref, pr_ref, sink_ref, o_ref, lse_ref, vt_ref):
        n = pl.program_id(1)
        lo = _lane_lo()
        hi = jnp.logical_not(lo)
        eye = _eye()

        @pl.when(n == 0)
        def _():
            step = 2 * w
            for kv in range(2):
                vt_ref[kv * ext + HALF:(kv + 1) * ext, :] = jnp.ones((16, seq), BF16)
                for t in range(seq // step):
                    v_t = _dot_nt(eye, v_ref[t * step:(t + 1) * step, kv * HEAD_LANES:(kv + 1) * HEAD_LANES])
                    vt_ref[kv * ext:kv * ext + HALF, t * step:(t + 1) * step] = v_t[:HALF, :].astype(BF16)

        heads = range(N_HEADS)
        blocks = range(qb)
        geo = [_swa_block(n * qb + bi, pc_ref, pr_ref[bi]) for bi in blocks]
        wins = [pl.ds(g[0], 2 * w) for g in geo]
        kwins = [k_ref[win, :] for win in wins]
        vts = [vt_ref[:, win] for win in wins]
        sts = []
        for bi in blocks:
            q = q_ref[bi * w:(bi + 1) * w, :]
            sts.append([])
            for h in heads:
                qp = q[:, (h // 2) * HEAD_LANES:(h // 2 + 1) * HEAD_LANES]
                qh = jnp.where(lo if h % 2 == 0 else hi, qp, jnp.zeros_like(qp))
                sts[bi].append(_dot_nt(kwins[bi][:, (h // 4) * HEAD_LANES:(h // 4 + 1) * HEAD_LANES], qh))
        ps, ms = [], []
        for bi in blocks:
            ps.append([])
            ms.append([])
            for h in heads:
                s = sts[bi][h] - _alibi(h) * geo[bi][1]
                m = jnp.maximum(jnp.max(s, axis=0, keepdims=True), sink_ref[0, h] * LOG2E)
                ps[bi].append(jnp.exp2(s - m).astype(BF16))
                ms[bi].append(m)
        for bi in blocks:
            ots = []
            for h in heads:
                pv = _dot(vts[bi][(h // 4) * ext:(h // 4 + 1) * ext, :], ps[bi][h])
                l = pv[HALF:HALF + 1, :] + jnp.exp2(sink_ref[0, h] * LOG2E - ms[bi][h])
                ots.append(pv[:HALF, :] * (1.0 / l))
                lse_ref[0, h:h + 1, bi * w:(bi + 1) * w] = ms[bi][h] + jnp.log2(l)
            o_ref[bi * w:(bi + 1) * w, :] = jnp.concatenate(ots, axis=0).T

    n_tok = qs.shape[0]
    tok = lambda width: pl.BlockSpec((qb * w, width), lambda b, n: (b * steps + n, 0))
    whole = lambda width: pl.BlockSpec((seq, width), lambda b, n: (b, 0))
    return pl.pallas_call(
        body, name="swa_fwd", grid=(n_seq, steps),
        out_shape=[jax.ShapeDtypeStruct((n_tok, 512), F32), jax.ShapeDtypeStruct((n_seq, N_HEADS, seq), F32)],
        in_specs=[tok(512), whole(256), whole(256), whole(1), pl.BlockSpec((qb, 1, w), lambda b, n: (b * steps + n, 0, 0)),
                  pl.BlockSpec(memory_space=pltpu.SMEM)],
        out_specs=[tok(512), pl.BlockSpec((1, N_HEADS, qb * w), lambda b, n: (b, 0, n))],
        scratch_shapes=[pltpu.VMEM((2 * ext, seq), BF16)],
        compiler_params=_params(2),
    )(qs, kd, vd, pos_col, pos_row, sinks)


def _swa_bwd_call(qs, kd, vd, do, o, lse, pos_col, pos_row, sinks, n_seq, seq):
    w = SWA_WINDOW
    qb = SWA_BLOCKS
    steps = seq // (qb * w)

    def body(q_ref, k_ref, v_ref, do_ref, o_ref, lse_ref, pc_ref, pr_ref, sink_ref, dq_ref, dk_ref, dv_ref, dsink_ref,
             kt_ref):
        b, n = pl.program_id(0), pl.program_id(1)
        lo = _lane_lo()
        hi = jnp.logical_not(lo)
        sub_lo = lax.broadcasted_iota(jnp.int32, (HEAD_LANES, 1), 0) < HALF
        eye = _eye()
        ones_lo = jnp.where(jnp.broadcast_to(lo, (8, HEAD_LANES)), 1.0, 0.0).astype(BF16)
        ones_hi = jnp.where(jnp.broadcast_to(lo, (8, HEAD_LANES)), 0.0, 1.0).astype(BF16)

        @pl.when(n == 0)
        def _():
            dk_ref[...] = jnp.zeros_like(dk_ref)
            dv_ref[...] = jnp.zeros_like(dv_ref)
            _transpose_rows(eye, k_ref, kt_ref, seq, 2 * HEAD_LANES)

        @pl.when(jnp.logical_and(n == 0, b == 0))
        def _():
            dsink_ref[...] = jnp.zeros_like(dsink_ref)

        heads = range(N_HEADS)
        blocks = range(qb)
        kv_lanes = lambda h: slice((h // 4) * HEAD_LANES, (h // 4 + 1) * HEAD_LANES)
        geo = [_swa_block(n * qb + bi, pc_ref, pr_ref[bi]) for bi in blocks]
        wins = [pl.ds(g[0], 2 * w) for g in geo]
        kwins = [k_ref[win, :] for win in wins]
        vwins = [v_ref[win, :] for win in wins]

        do_ts, deltas, qms, doms = [], [], [], []
        for bi in blocks:
            rows = slice(bi * w, (bi + 1) * w)
            for lst in (do_ts, deltas, qms, doms):
                lst.append([])
            for j in range(N_HEADS // 2):
                pair = slice(j * HEAD_LANES, (j + 1) * HEAD_LANES)
                dop = do_ref[rows, pair]
                qp = q_ref[rows, pair]
                dt = _dot_nt(eye, dop)
                prod = dop.astype(F32) * o_ref[rows, pair]
                p_hi = prod.astype(BF16)
                p_lo = (prod - p_hi.astype(F32)).astype(BF16)
                for hh in range(2):
                    half, ones = (lo, ones_lo) if hh == 0 else (hi, ones_hi)
                    do_ts[bi].append(jnp.where(sub_lo, dt, 0.0).astype(BF16) if hh == 0
                                     else jnp.where(sub_lo, 0.0, dt).astype(BF16))
                    deltas[bi].append((_dot_nt(ones, p_hi) + _dot_nt(ones, p_lo))[0:1, :])
                    qms[bi].append(jnp.where(half, qp, jnp.zeros_like(qp)))
                    doms[bi].append(jnp.where(half, dop, jnp.zeros_like(dop)))
        sts = [[_dot_nt(kwins[bi][:, kv_lanes(h)], qms[bi][h]) for h in heads] for bi in blocks]
        dpts = [[_dot(vwins[bi][:, kv_lanes(h)], do_ts[bi][h]) for h in heads] for bi in blocks]
        pts, dsts = [], []
        for bi in blocks:
            pts.append([])
            dsts.append([])
            for h in heads:
                lse_h = lse_ref[0, h:h + 1, bi * w:(bi + 1) * w]
                pt = jnp.exp2(sts[bi][h] - _alibi(h) * geo[bi][1] - lse_h)
                dsts[bi].append((pt * (dpts[bi][h] - deltas[bi][h])).astype(BF16))
                pts[bi].append(pt.astype(BF16))
                dsink_ref[h:h + 1, :] += -jnp.exp2(sink_ref[0, h] * LOG2E - lse_h) * deltas[bi][h]
        for bi in blocks:
            for kv in range(2):
                group = range(4 * kv, 4 * kv + 4)
                dst_all = jnp.concatenate([dsts[bi][h] for h in group], axis=1)
                pt_all = jnp.concatenate([pts[bi][h] for h in group], axis=1)
                q_all = jnp.concatenate([qms[bi][h] for h in group], axis=0)
                do_all = jnp.concatenate([doms[bi][h] for h in group], axis=0)
                dk_ref[wins[bi], kv_lanes(4 * kv)] += _dot(dst_all, q_all)
                dv_ref[wins[bi], kv_lanes(4 * kv)] += _dot(pt_all, do_all)
        for bi in blocks:
            ktw = kt_ref[:, wins[bi]]
            for j in range(N_HEADS // 2):
                k_t = ktw[kv_lanes(2 * j), :]
                dq_t = jnp.where(sub_lo, _dot(k_t, dsts[bi][2 * j]), _dot(k_t, dsts[bi][2 * j + 1]))
                dq_ref[bi * w:(bi + 1) * w, j * HEAD_LANES:(j + 1) * HEAD_LANES] = dq_t.T * SWA_SCALE

    n_tok = qs.shape[0]
    tok = lambda width: pl.BlockSpec((qb * w, width), lambda b, n: (b * steps + n, 0))
    whole = lambda width: pl.BlockSpec((seq, width), lambda b, n: (b, 0))
    return pl.pallas_call(
        body, name="swa_bwd", grid=(n_seq, steps),
        out_shape=[jax.ShapeDtypeStruct((n_tok, 512), F32), jax.ShapeDtypeStruct((n_tok, 256), F32),
                   jax.ShapeDtypeStruct((n_tok, 256), F32), jax.ShapeDtypeStruct((N_HEADS, HEAD_LANES), F32)],
        in_specs=[tok(512), whole(256), whole(256), pl.BlockSpec((qb * w, 512), lambda b, n: (b * steps + n, 1)), tok(512),
                  pl.BlockSpec((1, N_HEADS, qb * w), lambda b, n: (b, 0, n)),
                  whole(1), pl.BlockSpec((qb, 1, w), lambda b, n: (b * steps + n, 0, 0)),
                  pl.BlockSpec(memory_space=pltpu.SMEM)],
        out_specs=[tok(512), whole(256), whole(256), _full((N_HEADS, HEAD_LANES))],
        scratch_shapes=[pltpu.VMEM((2 * HEAD_LANES, seq), BF16)],
        compiler_params=_params(2),
    )(qs, kd, vd, do, o, lse, pos_col, pos_row, sinks)


def _post_call(x, target, o_mla, o_swa, gates, mod, b_ada, fg, w_out, w_out_t, seq):
    n_tok = x.shape[0]
    tm = min(TOKEN_TILE, seq)
    per_seq = seq // tm
    n_seq = n_tok // seq

    def body(x_ref, t_ref, om_ref, os_ref, g_ref, mod_ref, bada_ref, fg_ref, w_ref, wt_ref,
             dx2_ref, do_ref, dg_ref, gw_ref, gfg_ref, dgate_ref, loss_ref):
        i = pl.program_id(0)

        @pl.when(i == 0)
        def _():
            gw_ref[...] = jnp.zeros_like(gw_ref)
            gfg_ref[...] = jnp.zeros_like(gfg_ref)
            loss_ref[...] = jnp.zeros_like(loss_ref)

        @pl.when(i % per_seq == 0)
        def _():
            dgate_ref[...] = jnp.zeros_like(dgate_ref)

        gate = mod_ref[0][:, 2 * D_MODEL:] + bada_ref[:, 2 * D_MODEL:]
        fgv = fg_ref[...]
        subs = _sub_tiles(tm)
        gs = [g_ref[r, :] for r in subs]
        os_ = [jnp.concatenate([om_ref[r, :], os_ref[r, :]], axis=-1) for r in subs]
        sgs = [_sigmoid(g) for g in gs]
        sils = [g * sg for g, sg in zip(gs, sgs)]
        ypres = [(o * sil).astype(BF16) for o, sil in zip(os_, sils)]
        ys = [_dot(ypre, w_ref[...]) for ypre in ypres]
        dys, loss, gfg, dgate = [], 0.0, 0.0, 0.0
        for r, y in zip(subs, ys):
            x2 = x_ref[r, :] + gate * y
            r2 = lax.rsqrt(jnp.mean(x2 * x2, axis=-1, keepdims=True) + EPS)
            xn2 = x2 * r2
            err = xn2 * fgv - t_ref[r, :]
            loss = loss + jnp.sum(jnp.sum(err * err, axis=-1, keepdims=True), axis=0, keepdims=True)
            dout = err * (1.0 / D_MODEL)
            gfg = gfg + jnp.sum(dout * xn2, axis=0, keepdims=True)
            dxn2 = dout * fgv
            dx2 = r2 * (dxn2 - xn2 * jnp.mean(dxn2 * xn2, axis=-1, keepdims=True))
            dx2_ref[r, :] = dx2
            dgate = dgate + jnp.sum(dx2 * y, axis=0, keepdims=True)
            dys.append((dx2 * gate).astype(BF16))
        loss_ref[...] += jnp.broadcast_to(loss * (0.5 / D_MODEL), loss_ref.shape)
        gfg_ref[...] += gfg
        dgate_ref[0] += dgate
        gw_ref[...] += _dot_tn(jnp.concatenate(ypres, axis=0), jnp.concatenate(dys, axis=0))
        dypres = [_dot(dy, wt_ref[...]) for dy in dys]
        for r, dypre, o, g, sg, sil in zip(subs, dypres, os_, gs, sgs, sils):
            do_ref[r, :] = (dypre * sil).astype(BF16)
            dg_ref[r, :] = (dypre * o * (sg * (1.0 + g * (1.0 - sg)))).astype(BF16)

    tok = lambda w: pl.BlockSpec((tm, w), lambda i: (i, 0))
    per_b = pl.BlockSpec((1, 1, 3 * D_MODEL), lambda i: (i // per_seq, 0, 0))
    return pl.pallas_call(
        body, name="post", grid=(n_tok // tm,),
        out_shape=[jax.ShapeDtypeStruct((n_tok, D_MODEL), F32), jax.ShapeDtypeStruct((n_tok, D_MODEL), BF16),
                   jax.ShapeDtypeStruct((n_tok, D_MODEL), BF16), jax.ShapeDtypeStruct((D_MODEL, D_MODEL), F32),
                   jax.ShapeDtypeStruct((1, D_MODEL), F32), jax.ShapeDtypeStruct((n_seq, 1, D_MODEL), F32),
                   jax.ShapeDtypeStruct((1, HEAD_LANES), F32)],
        in_specs=[tok(D_MODEL), tok(D_MODEL), tok(512), tok(512), tok(D_MODEL), per_b, _full(b_ada.shape),
                  _full(fg.shape), _full(w_out.shape), _full(w_out_t.shape)],
        out_specs=[tok(D_MODEL), tok(D_MODEL), tok(D_MODEL), _full((D_MODEL, D_MODEL)), _full((1, D_MODEL)),
                   pl.BlockSpec((1, 1, D_MODEL), lambda i: (i // per_seq, 0, 0)), _full((1, HEAD_LANES))],
        compiler_params=_params(1),
    )(x, target, o_mla, o_swa, gates, mod, b_ada, fg, w_out, w_out_t)


def _mid_bwd_call(dqf, dkf, dv, zqkv, pos_col, qg, kvg, inv128, wq2, wkv, seq):
    n_tok = dqf.shape[0]
    tm = min(TOKEN_TILE, seq)

    def body(dq_ref, dk_ref, dv_ref, z_ref, pos_ref, qg_ref, kvg_ref, inv_ref, wq_ref, wkv_ref,
             dz_ref, dkr_ref, gwq_ref, gwkv_ref, gqg_ref, gkvg_ref):
        i = pl.program_id(0)

        @pl.when(i == 0)
        def _():
            gwq_ref[...] = jnp.zeros_like(gwq_ref)
            gwkv_ref[...] = jnp.zeros_like(gwkv_ref)
            gqg_ref[...] = jnp.zeros_like(gqg_ref)
            gkvg_ref[...] = jnp.zeros_like(gkvg_ref)

        cos, sin = _rope_tables(pos_ref[...], inv_ref[...])
        cf, sf = jnp.tile(cos, (1, N_HEADS)), jnp.tile(sin, (1, N_HEADS))
        dq = dq_ref[...] * MLA_SCALE
        dqr = jnp.concatenate([dq * cf, dq * sf], axis=-1).astype(BF16)
        zq, zkv = z_ref[:, :Q_LORA], z_ref[:, Q_LORA:]
        qgv, kvgv = qg_ref[...], kvg_ref[...]

        rq = lax.rsqrt(jnp.mean(zq * zq, axis=-1, keepdims=True) + EPS)
        xq = zq * rq
        gwq_ref[...] += _dot_tn((xq * qgv).astype(BF16), dqr)
        dqn = _dot_nt(dqr, wq_ref[...])
        gqg_ref[...] += jnp.sum(dqn * xq, axis=0, keepdims=True)
        dxq = dqn * qgv
        dz_ref[:, :Q_LORA] = (rq * (dxq - xq * jnp.mean(dxq * xq, axis=-1, keepdims=True))).astype(BF16)

        dk = dk_ref[...] * LN2
        dkv = jnp.concatenate([dk, dv_ref[...]], axis=-1).astype(BF16)
        rkv = lax.rsqrt(jnp.mean(zkv * zkv, axis=-1, keepdims=True) + EPS)
        xkv = zkv * rkv
        gwkv_ref[...] += _dot_tn((xkv * kvgv).astype(BF16), dkv)
        dkvn = _dot_nt(dkv, wkv_ref[...])
        gkvg_ref[...] += jnp.sum(dkvn * xkv, axis=0, keepdims=True)
        dxkv = dkvn * kvgv
        dz_ref[:, Q_LORA:] = (rkv * (dxkv - xkv * jnp.mean(dxkv * xkv, axis=-1, keepdims=True))).astype(BF16)

        dkpe = dk[:, :HEAD_LANES]
        for h in range(1, N_HEADS):
            dkpe = dkpe + dk[:, h * HEAD_LANES:(h + 1) * HEAD_LANES]
        dkr_ref[:, :HEAD_LANES] = (dkpe * cos).astype(BF16)
        dkr_ref[:, HEAD_LANES:] = (dkpe * sin).astype(BF16)

    tok = lambda w: pl.BlockSpec((tm, w), lambda i: (i, 0))
    return pl.pallas_call(
        body, name="mid_bwd", grid=(n_tok // tm,),
        out_shape=[jax.ShapeDtypeStruct((n_tok, 640), BF16), jax.ShapeDtypeStruct((n_tok, 256), BF16),
                   jax.ShapeDtypeStruct(wq2.shape, F32), jax.ShapeDtypeStruct(wkv.shape, F32),
                   jax.ShapeDtypeStruct((1, Q_LORA), F32), jax.ShapeDtypeStruct((1, KV_LORA), F32)],
        in_specs=[tok(1024), tok(1024), tok(512), tok(640), tok(1), _full(qg.shape), _full(kvg.shape),
                  _full(inv128.shape), _full(wq2.shape), _full(wkv.shape)],
        out_specs=[tok(640), tok(256), _full(wq2.shape), _full(wkv.shape), _full((1, Q_LORA)), _full((1, KV_LORA))],
        compiler_params=_params(1),
    )(dqf, dkf, dv, zqkv, pos_col, qg, kvg, inv128, wq2, wkv)


def _in_bwd_call(x, dx2, dz, dkr, dg, dqs, dkd, dvd, mod, b_ada, ng, wa_t, wkr2_t, seq):
    n_tok = x.shape[0]
    tm = min(TOKEN_TILE, seq)
    per_seq = seq // tm
    n_seq = n_tok // seq

    def body(x_ref, dx2_ref, dz_ref, dkr_ref, dg_ref, dqs_ref, dkd_ref, dvd_ref, mod_ref, bada_ref, ng_ref,
             wat_ref, wkrt_ref, gx_ref, gwa_ref, gwkr_ref, gng_ref, dshift_ref, dscale_ref):
        i = pl.program_id(0)

        @pl.when(i == 0)
        def _():
            gwa_ref[...] = jnp.zeros_like(gwa_ref)
            gwkr_ref[...] = jnp.zeros_like(gwkr_ref)
            gng_ref[...] = jnp.zeros_like(gng_ref)

        @pl.when(i % per_seq == 0)
        def _():
            dshift_ref[...] = jnp.zeros_like(dshift_ref)
            dscale_ref[...] = jnp.zeros_like(dscale_ref)

        xv = x_ref[...]
        modv = mod_ref[0] + bada_ref[...]
        shift, scale = modv[:, :D_MODEL], modv[:, D_MODEL:2 * D_MODEL]
        ngv = ng_ref[...]
        r1 = lax.rsqrt(jnp.mean(xv * xv, axis=-1, keepdims=True) + EPS)
        xn = xv * r1
        hb = ((xn * ngv) * (1.0 + scale) + shift).astype(BF16)

        dgv = dg_ref[...]
        pieces = [(A_ZQ, dz_ref[...]), (A_GM, dgv[:, :512]), (A_QS, dqs_ref[...].astype(BF16)),
                  (A_KD, (dkd_ref[...] * LN2).astype(BF16)),
                  (A_VD, dvd_ref[...].astype(BF16)), (A_GS, dgv[:, 512:])]
        dkr = dkr_ref[...]
        gwkr_ref[...] += _dot_tn(hb, dkr)
        dh = _dot(dkr, wkrt_ref[...])
        for off, piece in pieces:
            wd = piece.shape[1]
            gwa_ref[:, off:off + wd] += _dot_tn(hb, piece)
            dh = dh + _dot(piece, wat_ref[off:off + wd, :])

        dshift_ref[0] += jnp.sum(dh, axis=0, keepdims=True)
        dscale_ref[0] += jnp.sum(dh * (xn * ngv), axis=0, keepdims=True)
        gng_ref[...] += jnp.sum(dh * xn * (1.0 + scale), axis=0, keepdims=True)
        dxn = dh * ngv * (1.0 + scale)
        gx_ref[...] = dx2_ref[...] + r1 * (dxn - xn * jnp.mean(dxn * xn, axis=-1, keepdims=True))

    tok = lambda w: pl.BlockSpec((tm, w), lambda i: (i, 0))
    per_b = lambda w: pl.BlockSpec((1, 1, w), lambda i: (i // per_seq, 0, 0))
    return pl.pallas_call(
        body, name="in_bwd", grid=(n_tok // tm,),
        out_shape=[jax.ShapeDtypeStruct((n_tok, D_MODEL), F32), jax.ShapeDtypeStruct((D_MODEL, A_END), F32),
                   jax.ShapeDtypeStruct((D_MODEL, 256), F32), jax.ShapeDtypeStruct((1, D_MODEL), F32),
                   jax.ShapeDtypeStruct((n_seq, 1, D_MODEL), F32), jax.ShapeDtypeStruct((n_seq, 1, D_MODEL), F32)],
        in_specs=[tok(D_MODEL), tok(D_MODEL), tok(640), tok(256), tok(D_MODEL), tok(512), tok(256), tok(256),
                  per_b(3 * D_MODEL), _full(b_ada.shape), _full(ng.shape), _full(wa_t.shape), _full(wkr2_t.shape)],
        out_specs=[tok(D_MODEL), _full((D_MODEL, A_END)), _full((D_MODEL, 256)), _full((1, D_MODEL)),
                   per_b(D_MODEL), per_b(D_MODEL)],
        compiler_params=_params(1),
    )(x, dx2, dz, dkr, dg, dqs, dkd, dvd, mod, b_ada, ng, wa_t, wkr2_t)


def _adam_math(w, g, m, v):
    m_new = ADAM_B1 * m + (1.0 - ADAM_B1) * g
    v_new = ADAM_B2 * v + (1.0 - ADAM_B2) * (g * g)
    m_hat = m_new / (1.0 - ADAM_B1 ** ADAM_STEP)
    v_hat = v_new / (1.0 - ADAM_B2 ** ADAM_STEP)
    delta = -ADAM_LR * (m_hat / (jnp.sqrt(v_hat) + ADAM_EPS) + ADAM_WD * w)
    return delta, m_new, v_new


def _adam_call(name, w, g, m, v):
    rows, cols = w.shape
    tr = 256 if rows % 256 == 0 else rows

    def body(w_ref, g_ref, m_ref, v_ref, d_ref, mo_ref, vo_ref):
        d, mn, vn = _adam_math(w_ref[...], g_ref[...], m_ref[...], v_ref[...])
        d_ref[...] = d
        mo_ref[...] = mn
        vo_ref[...] = vn

    spec = pl.BlockSpec((tr, cols), lambda i: (i, 0))
    return pl.pallas_call(
        body, name=name, grid=(rows // tr,),
        out_shape=[jax.ShapeDtypeStruct(w.shape, F32)] * 3,
        in_specs=[spec] * 4, out_specs=[spec] * 3,
        compiler_params=_params(1),
    )(w, g, m, v)


def _ada_bwd_call(act_all, dmod_cols, w, m, v):
    rows, cols = w.shape
    tr = 256

    def body(a_ref, dm_ref, w_ref, m_ref, v_ref, g_ref, d_ref, mo_ref, vo_ref):
        g = _dot_tn(a_ref[...].astype(BF16), dm_ref[...].astype(BF16))
        d, mn, vn = _adam_math(w_ref[...], g, m_ref[...], v_ref[...])
        g_ref[...] = g
        d_ref[...] = d
        mo_ref[...] = mn
        vo_ref[...] = vn

    spec = pl.BlockSpec((tr, cols), lambda i: (i, 0))
    nb = act_all.shape[0]
    return pl.pallas_call(
        body, name="ada_bwd", grid=(rows // tr,),
        out_shape=[jax.ShapeDtypeStruct(w.shape, F32)] * 4,
        in_specs=[pl.BlockSpec((nb, tr), lambda i: (0, i)), _full(dmod_cols.shape), spec, spec, spec],
        out_specs=[spec] * 4,
        compiler_params=_params(1),
    )(act_all, dmod_cols, w, m, v)


SMALL_ROW = {"norm_gain": (0, 1024), "final_gain": (1024, 2048), "q_norm_gain": (2048, 2432),
             "kv_norm_gain": (2432, 2688), "swa_sinks": (2688, 2696), "loss": (2816, 2944)}
SMALL_ORDER = ("b_ada", "norm_gain", "q_norm_gain", "kv_norm_gain", "swa_sinks", "final_gain")


def _small_call(parts_all, n_seq, params):
    k = len(params)

    def body(p_ref, *refs):
        ins, outs, loss_ref = refs[:3 * k], refs[3 * k:7 * k], refs[7 * k]
        row = p_ref[n_seq:n_seq + 1, :]
        for dv in range(1, 8):
            r0 = dv * ROWS_PER_DEVICE + n_seq
            row = row + p_ref[r0:r0 + 1, :]
        gb = None
        for dv in range(8):
            for r in range(n_seq):
                r0 = dv * ROWS_PER_DEVICE + r
                gb = p_ref[r0:r0 + 1, :] if gb is None else gb + p_ref[r0:r0 + 1, :]
        for j, name in enumerate(SMALL_ORDER):
            g = gb if name == "b_ada" else row[:, SMALL_ROW[name][0]:SMALL_ROW[name][1]]
            d, mn, vn = _adam_math(ins[3 * j][...], g, ins[3 * j + 1][...], ins[3 * j + 2][...])
            outs[4 * j][...] = g
            outs[4 * j + 1][...] = d
            outs[4 * j + 2][...] = mn
            outs[4 * j + 3][...] = vn
        loss_ref[...] = row[:, SMALL_ROW["loss"][0]:SMALL_ROW["loss"][1]]

    flat = [t for p in params for t in p]
    res = pl.pallas_call(
        body, name="small_update", grid=(1,),
        out_shape=[jax.ShapeDtypeStruct(p[0].shape, F32) for p in params for _ in range(4)]
        + [jax.ShapeDtypeStruct((1, HEAD_LANES), F32)],
        in_specs=[_full(parts_all.shape)] + [_full(t.shape) for t in flat],
        out_specs=[_full(p[0].shape) for p in params for _ in range(4)] + [_full((1, HEAD_LANES))],
        compiler_params=_params(1),
    )(parts_all, *flat)
    return [res[4 * j:4 * j + 4] for j in range(k)], res[4 * k]


def _rot(t):
    half = t.shape[-1] // 2
    return jnp.concatenate([-t[..., half:], t[..., :half]], axis=-1)


def _rot_t(g):
    half = g.shape[-1] // 2
    return jnp.concatenate([g[..., half:], -g[..., :half]], axis=-1)


def _prepare_weights(w_in, w_uq, w_ukv):
    o = [0]
    for s in IN_SPLITS:
        o.append(o[-1] + s)
    ks, vs = w_in[:, o[5]:o[6]], w_in[:, o[6]:o[7]]
    dup = lambda t: jnp.concatenate([t[:, :64], t[:, :64], t[:, 64:], t[:, 64:]], axis=1)
    wa = jnp.concatenate([w_in[:, :o[2]], w_in[:, o[3]:o[5]], dup(ks), dup(vs), w_in[:, o[7]:]], axis=1)
    kr = w_in[:, o[2]:o[3]]
    zc = lambda n: jnp.zeros((w_in.shape[0], n), w_in.dtype)
    wkr2 = jnp.concatenate([zc(64), kr, zc(32), zc(64), _rot(kr), zc(32)], axis=1)
    uq = w_uq.reshape(Q_LORA, N_HEADS, MLA_NOPE + MLA_ROPE)
    zq = jnp.zeros((Q_LORA, N_HEADS, 32), w_uq.dtype)
    uq_full = jnp.concatenate([uq, zq], axis=-1).reshape(Q_LORA, 1024)
    uq_rot = jnp.concatenate([jnp.zeros((Q_LORA, N_HEADS, 64), w_uq.dtype), _rot(uq[..., MLA_NOPE:]), zq],
                             axis=-1).reshape(Q_LORA, 1024)
    wq2 = jnp.concatenate([uq_full, uq_rot], axis=1)
    ukv = w_ukv.reshape(KV_LORA, N_HEADS, 128)
    k_full = jnp.concatenate([ukv[..., :64], jnp.zeros((KV_LORA, N_HEADS, 64), w_ukv.dtype)], axis=-1).reshape(KV_LORA, 1024)
    wkv = jnp.concatenate([k_full, ukv[..., 64:].reshape(KV_LORA, 512)], axis=1)
    return wa, wkr2, wq2, wkv


def _restore_grads(gwa, gwkr2, gwq2, gwkv):
    fold = lambda g: jnp.concatenate([g[:, 0:64] + g[:, 64:128], g[:, 128:192] + g[:, 192:256]], axis=1)
    gkr = gwkr2[:, 64:96] + _rot_t(gwkr2[:, 192:224])
    g_in = jnp.concatenate([gwa[:, :A_GM], gkr, gwa[:, A_GM:A_KD], fold(gwa[:, A_KD:A_VD]), fold(gwa[:, A_VD:A_GS]),
                            gwa[:, A_GS:]], axis=1)
    gf = gwq2[:, :1024].reshape(Q_LORA, N_HEADS, 128)
    gr = gwq2[:, 1024:].reshape(Q_LORA, N_HEADS, 128)
    g_uq = jnp.concatenate([gf[..., :64], gf[..., 64:96] + _rot_t(gr[..., 64:96])], axis=-1).reshape(Q_LORA, 768)
    gk = gwkv[:, :1024].reshape(KV_LORA, N_HEADS, 128)[..., :64]
    gv = gwkv[:, 1024:].reshape(KV_LORA, N_HEADS, 64)
    g_ukv = jnp.concatenate([gk, gv], axis=-1).reshape(KV_LORA, 1024)
    return g_in, g_uq, g_ukv


def _local_step(x, positions, target, mod_rows, b_ada, ng, qg, kvg, sinks, fg, w_in_b, w_uq_b, w_ukv_b, w_out_b):
    n_seq, seq, _ = x.shape
    n_tok = n_seq * seq
    x2d = x.reshape(n_tok, D_MODEL)
    t2d = target.reshape(n_tok, D_MODEL)
    pos_f = positions.astype(F32)
    pos_col = pos_f.reshape(n_tok, 1)
    pos_row = pos_f.reshape(n_tok // SWA_WINDOW, 1, SWA_WINDOW)
    mod3 = mod_rows.reshape(n_seq, 1, 3 * D_MODEL)
    inv = ROPE_THETA ** (-jnp.arange(0, MLA_ROPE, 2, dtype=F32) / MLA_ROPE)
    inv128 = jnp.concatenate([jnp.zeros((64,), F32), inv, inv, jnp.zeros((32,), F32)]).reshape(1, 128)
    fg2 = fg.reshape(1, D_MODEL)

    wa, wkr2, wq2, wkv = _prepare_weights(w_in_b, w_uq_b, w_ukv_b)

    zqkv, gates, qf, kf, v, qs, kd, vd = _pre_call(x2d, pos_col, mod3, b_ada, ng, qg, kvg, inv128, wa, wkr2, wq2, wkv, seq)
    o_mla, lse_mla = _mla_fwd_call(qf, kf, v, n_seq, seq)
    o_swa, lse_swa = _swa_fwd_call(qs, kd, vd, pos_col, pos_row, sinks, n_seq, seq)
    dx2, do, dg, g_out, g_fg, dgate, loss = _post_call(x2d, t2d, o_mla, o_swa, gates, mod3, b_ada, fg2, w_out_b, w_out_b.T, seq)
    dqf, dkf, dv = _mla_bwd_call(qf, kf, v, do, o_mla, lse_mla, n_seq, seq)
    dqs, dkd, dvd, dsink = _swa_bwd_call(qs, kd, vd, do, o_swa, lse_swa, pos_col, pos_row, sinks, n_seq, seq)
    dz, dkr, g_wq2, g_wkv, g_qg, g_kvg = _mid_bwd_call(dqf, dkf, dv, zqkv, pos_col, qg, kvg, inv128, wq2, wkv, seq)
    gx, g_wa, g_wkr2, g_ng, dshift, dscale = _in_bwd_call(x2d, dx2, dz, dkr, dg, dqs, dkd, dvd, mod3, b_ada, ng,
                                                         wa.T, wkr2.T, seq)
    g_in, g_uq, g_ukv = _restore_grads(g_wa, g_wkr2, g_wq2, g_wkv)
    dmod = jnp.concatenate([dshift, dscale, dgate], axis=-1).reshape(n_seq, 3 * D_MODEL)
    small_row = jnp.concatenate([g_ng, g_fg, g_qg, g_kvg, jnp.pad(jnp.sum(dsink, axis=1).reshape(1, N_HEADS), ((0, 0), (0, 120))),
                                 loss, jnp.zeros((1, 128), F32)], axis=1)
    return gx.reshape(x.shape), (g_in, g_uq, g_ukv, g_out), small_row, dmod


def kernel(x, c, positions, w_ada, b_ada, norm_gain, w_in, q_norm_gain, kv_norm_gain, w_uq, w_ukv, swa_sinks, w_out, final_gain, loss_target, m_w_ada, m_b_ada, m_norm_gain, m_w_in, m_q_norm_gain, m_kv_norm_gain, m_w_uq, m_w_ukv, m_swa_sinks, m_w_out, m_final_gain, v_w_ada, v_b_ada, v_norm_gain, v_w_in, v_q_norm_gain, v_kv_norm_gain, v_w_uq, v_w_ukv, v_swa_sinks, v_w_out, v_final_gain):
    n_seq = x.shape[0]
    xi, yi, ci = lax.axis_index("x"), lax.axis_index("y"), lax.axis_index("c")
    dev = 4 * xi + 2 * yi + ci
    chip = 2 * xi + yi

    halves = lambda w: w.astype(BF16).reshape(2, w.shape[0] // 2, w.shape[1])
    c_blk = jnp.pad(c, ((0, ROWS_PER_DEVICE - n_seq), (0, 0)))
    act_all, pieces, f_in, f_uq, f_ukv, f_out = _comm_fwd_call(
        c_blk, w_ada[0], [halves(w_in[0]), halves(w_uq[0]), halves(w_ukv[0]), halves(w_out[0])])
    mine = lax.dynamic_slice_in_dim(pieces, dev * ROWS_PER_DEVICE, n_seq, axis=1)
    mod_rows = jnp.transpose(mine, (1, 0, 2)).reshape(n_seq, 3 * D_MODEL)
    cols = lambda t, r: jnp.transpose(t.reshape(4, r, -1), (1, 0, 2)).reshape(r, -1)
    w_in_b, w_uq_b, w_ukv_b = cols(f_in, D_MODEL), cols(f_uq, Q_LORA), cols(f_ukv, KV_LORA)
    w_out_b = f_out.reshape(D_MODEL, D_MODEL)

    gx, (g_in, g_uq, g_ukv, g_out), small_row, dmod = _local_step(
        x, positions, loss_target, mod_rows, b_ada, norm_gain, q_norm_gain, kv_norm_gain, swa_sinks, final_gain,
        w_in_b, w_uq_b, w_ukv_b, w_out_b)

    by_owner = lambda g, n: jnp.transpose(g.reshape(2, g.shape[0] // 2, 4, n), (0, 2, 1, 3))
    grads = [by_owner(g_in, 616), by_owner(g_uq, 192), by_owner(g_ukv, 256),
             jnp.transpose(g_out.reshape(4, 2, 128, D_MODEL), (1, 0, 2, 3))]
    part = jnp.concatenate([dmod, small_row, jnp.zeros((ROWS_PER_DEVICE - n_seq - 1, 3 * D_MODEL), F32)], axis=0)
    r_in, r_uq, r_ukv, r_out, parts_all = _comm_bwd_call(grads, part)
    g_in_s, g_uq_s = r_in.reshape(w_in.shape[1:]), r_uq.reshape(w_uq.shape[1:])
    g_ukv_s, g_out_s = r_ukv.reshape(w_ukv.shape[1:]), r_out.reshape(w_out.shape[1:])

    d_in, nm_in, nv_in = _adam_call("adam_w_in", w_in[0], g_in_s, m_w_in[0], v_w_in[0])
    d_uq, nm_uq, nv_uq = _adam_call("adam_w_uq", w_uq[0], g_uq_s, m_w_uq[0], v_w_uq[0])
    d_ukv, nm_ukv, nv_ukv = _adam_call("adam_w_ukv", w_ukv[0], g_ukv_s, m_w_ukv[0], v_w_ukv[0])
    d_out, nm_out, nv_out = _adam_call("adam_w_out", w_out[0], g_out_s, m_w_out[0], v_w_out[0])
    dmod_cols = lax.dynamic_slice_in_dim(parts_all, chip * 768, 768, axis=1)
    g_ada, d_ada, nm_ada, nv_ada = _ada_bwd_call(act_all, dmod_cols, w_ada[0], m_w_ada[0], v_w_ada[0])

    row = lambda t: t.reshape(1, -1)
    small = {"b_ada": (b_ada, m_b_ada, v_b_ada), "norm_gain": (norm_gain, m_norm_gain, v_norm_gain),
             "q_norm_gain": (q_norm_gain, m_q_norm_gain, v_q_norm_gain),
             "kv_norm_gain": (kv_norm_gain, m_kv_norm_gain, v_kv_norm_gain),
             "swa_sinks": (swa_sinks, m_swa_sinks, v_swa_sinks),
             "final_gain": (row(final_gain), row(m_final_gain), row(v_final_gain))}
    res, loss_row = _small_call(parts_all, n_seq, [small[name] for name in SMALL_ORDER])
    res = dict(zip(SMALL_ORDER, res))
    res["final_gain"] = [t.reshape(-1) for t in res["final_gain"]]
    e = lambda t: t[None]
    big = {"w_ada": (e(g_ada), e(d_ada), e(nm_ada), e(nv_ada)), "w_in": (e(g_in_s), e(d_in), e(nm_in), e(nv_in)),
           "w_uq": (e(g_uq_s), e(d_uq), e(nm_uq), e(nv_uq)), "w_ukv": (e(g_ukv_s), e(d_ukv), e(nm_ukv), e(nv_ukv)),
           "w_out": (e(g_out_s), e(d_out), e(nm_out), e(nv_out))}
    order = ("w_ada", "b_ada", "norm_gain", "w_in", "q_norm_gain", "kv_norm_gain", "w_uq", "w_ukv", "swa_sinks", "w_out",
             "final_gain")
    pick = lambda kind: [(big[n] if n in big else res[n])[kind] for n in order]
    return (loss_row[0, 0], gx, *pick(0), *pick(1), *pick(2), *pick(3))
```

```python
import functools

import jax
import jax.numpy as jnp
from jax import lax
from jax.experimental import pallas as pl
from jax.experimental.pallas import tpu as pltpu

F32 = jnp.float32
BF16 = jnp.bfloat16

D_MODEL = 1024
Q_LORA = 384
KV_LORA = 256
N_HEADS = 8
MLA_NOPE = 64
MLA_ROPE = 32
HEAD_LANES = 128
HALF = 64
SWA_WINDOW = 128
EPS = 1e-6
ROPE_THETA = 10000.0
MLA_SCALE = (MLA_NOPE + MLA_ROPE) ** -0.5
LOG2E = 1.4426950408889634
LN2 = 0.6931471805599453
SWA_SCALE = 64 ** -0.5
NEG = -1e30

ADAM_LR = 0.001
ADAM_B1 = 0.9
ADAM_B2 = 0.999
ADAM_EPS = 1e-08
ADAM_WD = 0.01
ADAM_STEP = 10

A_ZQ, A_ZKV, A_GM, A_QS, A_KD, A_VD, A_GS, A_END = 0, 384, 640, 1152, 1664, 1920, 2176, 2688
IN_SPLITS = (384, 256, 32, 512, 512, 128, 128, 512)
D_IN = sum(IN_SPLITS)

TOKEN_TILE = 512
ATT_TILE = 256
VMEM_LIMIT = 56 * 1024 * 1024


def _dot(a, b):
    return jnp.dot(a, b, preferred_element_type=F32)


def _dot_nt(a, b):
    return lax.dot_general(a, b, (((1,), (1,)), ((), ())), preferred_element_type=F32)


def _dot_tn(a, b):
    return lax.dot_general(a, b, (((0,), (0,)), ((), ())), preferred_element_type=F32)


def _params(n_grid):
    return pltpu.CompilerParams(dimension_semantics=("arbitrary",) * n_grid, vmem_limit_bytes=VMEM_LIMIT)


def _full(shape):
    nd = len(shape)
    return pl.BlockSpec(shape, lambda *_: (0,) * nd, pipeline_mode=pl.Buffered(1))


def _sigmoid(g):
    return 1.0 / (1.0 + jnp.exp(-g))


SUB_TILE = 256


def _sub_tiles(tm):
    sub = min(SUB_TILE, tm)
    return [slice(s * sub, (s + 1) * sub) for s in range(tm // sub)]


MESH = pl.DeviceIdType.MESH
ROWS_PER_DEVICE = 8
VMEM_SPEC = pl.BlockSpec(memory_space=pltpu.VMEM)
ANY_SPEC = pl.BlockSpec(memory_space=pl.ANY)


def _position():
    x, y, c = lax.axis_index("x"), lax.axis_index("y"), lax.axis_index("c")
    sibling = (x, y, 1 - c)
    others = [(1 - x, y, c), (x, 1 - y, c), (1 - x, 1 - y, c)]
    return (x, y, c), 4 * x + 2 * y + c, 2 * x + y, sibling, others


def _rows_of(dev):
    return pl.ds(pl.multiple_of(dev * ROWS_PER_DEVICE, ROWS_PER_DEVICE), ROWS_PER_DEVICE)


def _all_to_all_rows(block_ref, table_ref, dev, me, send_sems, recv_sems):
    x, y, c = me
    waits = []
    for k in range(1, 8):
        peer = (1 - x if k & 4 else x, 1 - y if k & 2 else y, 1 - c if k & 1 else c)
        pltpu.make_async_remote_copy(src_ref=block_ref, dst_ref=table_ref.at[_rows_of(dev)], send_sem=send_sems.at[k - 1],
                                     recv_sem=recv_sems.at[k - 1], device_id=peer, device_id_type=MESH).start()
        waits.append(pltpu.make_async_remote_copy(
            src_ref=block_ref, dst_ref=table_ref.at[_rows_of(jnp.bitwise_xor(dev, k))], send_sem=send_sems.at[k - 1],
            recv_sem=recv_sems.at[k - 1], device_id=peer, device_id_type=MESH))
    return waits


def _comm_fwd_call(c_blk, w_ada, shards):
    n = len(shards)

    def body(c_ref, wada_ref, *refs):
        w_refs, act_ref, pieces_ref, full_refs = refs[:n], refs[n], refs[n + 1], refs[n + 2:2 * n + 2]
        c_all_ref = refs[2 * n + 2]
        c_send, c_recv, p_send, p_recv, w_send, w_recv, f_send, f_recv, loc_sem = refs[2 * n + 3:]
        me, dev, chip, sibling, others = _position()
        core = me[2]
        chip_of = [2 * p[0] + p[1] for p in others]

        local = [pltpu.make_async_copy(w_refs[i], full_refs[i].at[chip], loc_sem.at[i]) for i in range(n)]
        for cp in local:
            cp.start()

        def over_ici(i, j, src_chip):
            return pltpu.make_async_remote_copy(
                src_ref=w_refs[i].at[core], dst_ref=full_refs[i].at[src_chip, core], send_sem=w_send.at[3 * i + j],
                recv_sem=w_recv.at[3 * i + j], device_id=others[j], device_id_type=MESH)

        def to_sibling(i, j, half):
            return pltpu.make_async_remote_copy(
                src_ref=full_refs[i].at[chip_of[j], half], dst_ref=full_refs[i].at[chip_of[j], half],
                send_sem=f_send.at[3 * i + j], recv_sem=f_recv.at[3 * i + j], device_id=sibling, device_id_type=MESH)

        sent = [over_ici(i, j, chip) for i in range(n) for j in range(3)]
        for cp in sent:
            cp.start()

        c_all_ref[_rows_of(dev), :] = c_ref[...]
        c_waits = _all_to_all_rows(c_ref, c_all_ref, dev, me, c_send, c_recv)
        for cp in c_waits:
            cp.wait()
        cv = c_all_ref[...]
        act = cv * _sigmoid(cv)
        act_ref[...] = act
        pieces_ref[chip] = _dot(act.astype(BF16), wada_ref[...].astype(BF16))
        piece = lambda j, src_chip: pltpu.make_async_remote_copy(
            src_ref=pieces_ref.at[chip], dst_ref=pieces_ref.at[src_chip], send_sem=p_send.at[j], recv_sem=p_recv.at[j],
            device_id=others[j], device_id_type=MESH)
        for j in range(3):
            piece(j, chip).start()
        for j in range(3):
            piece(j, chip).wait_send()
            piece(j, chip_of[j]).wait_recv()

        for i in range(n):
            for j in range(3):
                over_ici(i, j, chip_of[j]).wait_recv()
                to_sibling(i, j, core).start()
        for i in range(n):
            for j in range(3):
                to_sibling(i, j, 1 - core).wait_recv()
                to_sibling(i, j, core).wait_send()
        for cp in sent:
            cp.wait_send()
        for cp in local:
            cp.wait()

    rows = 8 * ROWS_PER_DEVICE
    dma = pltpu.SemaphoreType.DMA
    return pl.pallas_call(
        body, name="comm_fwd",
        out_shape=[jax.ShapeDtypeStruct((rows, D_MODEL), F32), jax.ShapeDtypeStruct((4, rows, w_ada.shape[1]), F32)]
        + [jax.ShapeDtypeStruct((4,) + s.shape, s.dtype) for s in shards],
        in_specs=[VMEM_SPEC, VMEM_SPEC] + [ANY_SPEC] * n,
        out_specs=[VMEM_SPEC, VMEM_SPEC] + [ANY_SPEC] * n,
        scratch_shapes=[pltpu.VMEM((rows, D_MODEL), F32), dma((7,)), dma((7,)), dma((3,)), dma((3,)),
                        dma((3 * n,)), dma((3 * n,)), dma((3 * n,)), dma((3 * n,)), dma((n,))],
        compiler_params=pltpu.CompilerParams(vmem_limit_bytes=VMEM_LIMIT),
    )(c_blk, w_ada, *shards)


def _comm_bwd_call(grads, part):
    n = len(grads)

    def body(part_ref, *refs):
        g_refs, f_refs, parts_ref = refs[:n], refs[n:2 * n], refs[2 * n]
        scratch = refs[2 * n + 1:]
        a_refs, b_refs, p_refs, r_refs = (scratch[k * n:(k + 1) * n] for k in range(4))
        s_send, s_recv, d_send, d_recv, e_send, e_recv, h_send, h_recv, loc_sem = scratch[4 * n:]
        me, dev, chip, sibling, others = _position()
        core = me[2]
        chip_of = [2 * p[0] + p[1] for p in others]

        parts_ref[_rows_of(dev), :] = part_ref[...]
        s_waits = _all_to_all_rows(part_ref, parts_ref, dev, me, s_send, s_recv)

        mine = [pltpu.make_async_copy(g_refs[i].at[:, core], a_refs[i], loc_sem.at[i]) for i in range(n)]
        swap = [pltpu.make_async_remote_copy(src_ref=g_refs[i].at[:, 1 - core], dst_ref=b_refs[i], send_sem=d_send.at[i],
                                             recv_sem=d_recv.at[i], device_id=sibling, device_id_type=MESH) for i in range(n)]
        order = sorted(range(n), key=lambda i: g_refs[i].shape[2] * g_refs[i].shape[3])
        for i in order:
            mine[i].start()
            swap[i].start()
        cross = [pltpu.make_async_remote_copy(src_ref=p_refs[i].at[chip_of[j]], dst_ref=r_refs[i].at[j],
                                              send_sem=e_send.at[3 * i + j], recv_sem=e_recv.at[3 * i + j],
                                              device_id=others[j], device_id_type=MESH) for i in range(n) for j in range(3)]
        for i in order:
            mine[i].wait()
            swap[i].wait()
            for k in range(4):
                s = a_refs[i][k] + b_refs[i][k]
                a_refs[i][k] = s
                p_refs[i][k] = s.astype(BF16)
            for j in range(3):
                cross[3 * i + j].start()
        share = {}
        for i in order:
            for j in range(3):
                cross[3 * i + j].wait()
            f_refs[i][core] = (a_refs[i][chip] + r_refs[i][0].astype(F32) + r_refs[i][1].astype(F32)
                               + r_refs[i][2].astype(F32))
            share[i] = pltpu.make_async_remote_copy(src_ref=f_refs[i].at[core], dst_ref=f_refs[i].at[core],
                                                    send_sem=h_send.at[i], recv_sem=h_recv.at[i], device_id=sibling,
                                                    device_id_type=MESH)
            share[i].start()
        for i in range(n):
            share[i].wait_send()
            pltpu.make_async_remote_copy(src_ref=f_refs[i].at[core], dst_ref=f_refs[i].at[1 - core], send_sem=h_send.at[i],
                                         recv_sem=h_recv.at[i], device_id=sibling, device_id_type=MESH).wait_recv()
        for cp in s_waits:
            cp.wait()

    rows = 8 * ROWS_PER_DEVICE
    dma = pltpu.SemaphoreType.DMA
    quarter = [(4,) + g.shape[2:] for g in grads]
    return pl.pallas_call(
        body, name="comm_bwd",
        out_shape=[jax.ShapeDtypeStruct((2,) + g.shape[2:], F32) for g in grads]
        + [jax.ShapeDtypeStruct((rows, part.shape[1]), F32)],
        in_specs=[VMEM_SPEC] + [ANY_SPEC] * n,
        out_specs=[VMEM_SPEC] * (n + 1),
        scratch_shapes=[pltpu.VMEM(q, F32) for q in quarter] + [pltpu.VMEM(q, F32) for q in quarter]
        + [pltpu.VMEM(q, BF16) for q in quarter] + [pltpu.VMEM((3,) + q[1:], BF16) for q in quarter]
        + [dma((7,)), dma((7,)), dma((n,)), dma((n,)), dma((3 * n,)), dma((3 * n,)), dma((n,)), dma((n,)), dma((n,))],
        compiler_params=pltpu.CompilerParams(vmem_limit_bytes=VMEM_LIMIT),
    )(part, *grads)


def _rope_tables(pos_col, inv_row):
    ang = pos_col * inv_row
    return jnp.cos(ang), jnp.sin(ang)


def _pre_call(x, pos_col, mod, b_ada, ng, qg, kvg, inv128, wa, wkr2, wq2, wkv, seq):
    n_tok = x.shape[0]
    tm = min(TOKEN_TILE, seq)
    per_seq = seq // tm

    def body(x_ref, pos_ref, mod_ref, bada_ref, ng_ref, qg_ref, kvg_ref, inv_ref, wa_ref, wkr_ref, wq_ref, wkv_ref,
             zqkv_ref, gates_ref, qf_ref, kf_ref, v_ref, qs_ref, kd_ref, vd_ref):
        xv = x_ref[...]
        modv = mod_ref[0] + bada_ref[...]
        shift, scale = modv[:, :D_MODEL], modv[:, D_MODEL:2 * D_MODEL]
        r1 = lax.rsqrt(jnp.mean(xv * xv, axis=-1, keepdims=True) + EPS)
        h = ((xv * r1) * ng_ref[...]) * (1.0 + scale) + shift
        hb = h.astype(BF16)
        za = _dot(hb, wa_ref[...])
        zkr = _dot(hb, wkr_ref[...])
        cos, sin = _rope_tables(pos_ref[...], inv_ref[...])
        zqkv_ref[...] = za[:, :A_GM]
        gates_ref[:, :512] = za[:, A_GM:A_QS]
        gates_ref[:, 512:] = za[:, A_GS:A_END]
        qs_ref[...] = (za[:, A_QS:A_KD] * (SWA_SCALE * LOG2E)).astype(BF16)
        kd_ref[...] = za[:, A_KD:A_VD].astype(BF16)
        vd_ref[...] = za[:, A_VD:A_GS].astype(BF16)
        zq, zkv = za[:, A_ZQ:A_ZKV], za[:, A_ZKV:A_GM]
        rq = lax.rsqrt(jnp.mean(zq * zq, axis=-1, keepdims=True) + EPS)
        qn = ((zq * rq) * qg_ref[...]).astype(BF16)
        qr = _dot(qn, wq_ref[...])
        cf, sf = jnp.tile(cos, (1, N_HEADS)), jnp.tile(sin, (1, N_HEADS))
        qf_ref[...] = ((qr[:, :1024] * cf + qr[:, 1024:] * sf) * (MLA_SCALE * LOG2E)).astype(BF16)
        rkv = lax.rsqrt(jnp.mean(zkv * zkv, axis=-1, keepdims=True) + EPS)
        kvn = ((zkv * rkv) * kvg_ref[...]).astype(BF16)
        kv = _dot(kvn, wkv_ref[...])
        kpe = zkr[:, :128] * cos + zkr[:, 128:] * sin
        kf_ref[...] = (kv[:, :1024] + jnp.tile(kpe, (1, N_HEADS))).astype(BF16)
        v_ref[...] = kv[:, 1024:].astype(BF16)

    tok = lambda w: pl.BlockSpec((tm, w), lambda i: (i, 0))
    outs = [(640, F32), (1024, F32), (1024, BF16), (1024, BF16), (512, BF16), (512, BF16), (256, BF16), (256, BF16)]
    return pl.pallas_call(
        body, name="pre", grid=(n_tok // tm,),
        out_shape=[jax.ShapeDtypeStruct((n_tok, w), dt) for w, dt in outs],
        in_specs=[tok(D_MODEL), tok(1), pl.BlockSpec((1, 1, 3 * D_MODEL), lambda i: (i // per_seq, 0, 0)),
                  _full(b_ada.shape), _full(ng.shape), _full(qg.shape), _full(kvg.shape), _full(inv128.shape),
                  _full(wa.shape), _full(wkr2.shape), _full(wq2.shape), _full(wkv.shape)],
        out_specs=[tok(w) for w, _ in outs],
        compiler_params=_params(1),
    )(x, pos_col, mod, b_ada, ng, qg, kvg, inv128, wa, wkr2, wq2, wkv)


def _lane_lo(width=HEAD_LANES):
    return lax.broadcasted_iota(jnp.int32, (1, width), 1) < HALF


def _eye(n=HEAD_LANES):
    r = lax.broadcasted_iota(jnp.int32, (n, n), 0)
    c = lax.broadcasted_iota(jnp.int32, (n, n), 1)
    return jnp.where(r == c, 1.0, 0.0).astype(BF16)


def _mla_fwd_call(qf, kf, v, n_seq, seq):
    tq = min(ATT_TILE, seq)
    nq = seq // tq

    ext = HALF + 16

    def body(q_ref, k_ref, v_ref, o_ref, lse_ref, vt_ref):
        i = pl.program_id(1)
        eye = _eye()

        @pl.when(i == 0)
        def _():
            for h in range(N_HEADS):
                vt_ref[h * ext + HALF:(h + 1) * ext, :] = jnp.ones((16, seq), BF16)
            for t in range(nq):
                for p in range(N_HEADS // 2):
                    pair = slice(p * HEAD_LANES, (p + 1) * HEAD_LANES)
                    v_t = _dot_nt(eye, v_ref[t * tq:(t + 1) * tq, pair]).astype(BF16)
                    for hh in range(2):
                        r0 = (2 * p + hh) * ext
                        vt_ref[r0:r0 + HALF, t * tq:(t + 1) * tq] = v_t[hh * HALF:(hh + 1) * HALF, :]

        q = q_ref[...]
        qcol = i * tq + lax.broadcasted_iota(jnp.int32, (1, tq), 1)
        heads = range(N_HEADS)
        lanes = [slice(h * HEAD_LANES, (h + 1) * HEAD_LANES) for h in heads]

        def make_step(masked, n_tiles):
            def step(kt0, carry):
                tiles = range(n_tiles)
                start = pl.multiple_of(kt0 * tq, tq)
                ks = [k_ref[pl.ds(pl.multiple_of((kt0 + t) * tq, tq), tq), :] for t in tiles]
                vt = vt_ref[:, pl.ds(start, n_tiles * tq)]
                sts = [[_dot_nt(ks[t][:, lanes[h]], q[:, lanes[h]]) for h in heads] for t in tiles]
                if masked:
                    last = n_tiles - 1
                    keep = ((kt0 + last) * tq + lax.broadcasted_iota(jnp.int32, (tq, 1), 0)) <= qcol
                    sts[last] = [jnp.where(keep, st, NEG) for st in sts[last]]
                stats, pts = [], []
                for h in heads:
                    m_old = carry[2 * h]
                    m_new = m_old
                    for t in tiles:
                        m_new = jnp.maximum(m_new, jnp.max(sts[t][h], axis=0, keepdims=True))
                    pts.append(jnp.concatenate([jnp.exp2(sts[t][h] - m_new).astype(BF16) for t in tiles], axis=0))
                    stats.append((m_new, jnp.exp2(m_old - m_new)))
                pvs = [_dot(vt[h * ext:(h + 1) * ext, :], pts[h]) for h in heads]
                out = []
                for h in heads:
                    out += [stats[h][0], carry[2 * h + 1] * stats[h][1] + pvs[h]]
                return tuple(out)
            return step

        init = (jnp.full((1, tq), NEG, F32), jnp.zeros((ext, tq), F32)) * N_HEADS
        count = i + 1
        carry = lax.fori_loop(0, (count + 1) // 2 - 1, lambda j, c: make_step(False, 2)(2 * j, c), init)
        carry = lax.cond(count % 2 == 0, lambda c: make_step(True, 2)(i - 1, c), lambda c: make_step(True, 1)(i, c), carry)
        dens = [carry[2 * h + 1][HALF:HALF + 1, :] for h in heads]
        acc_t = jnp.concatenate([carry[2 * h + 1][:HALF, :] * (1.0 / dens[h]) for h in heads], axis=0)
        o_ref[...] = acc_t.T
        for h in heads:
            lse_ref[0, h // 4, h % 4:h % 4 + 1, :] = carry[2 * h] + jnp.log2(dens[h])

    n_tok = qf.shape[0]
    return pl.pallas_call(
        body, name="mla_fwd", grid=(n_seq, nq),
        out_shape=[jax.ShapeDtypeStruct((n_tok, 512), F32), jax.ShapeDtypeStruct((n_seq, 2, 4, seq), F32)],
        in_specs=[pl.BlockSpec((tq, 1024), lambda b, i: (b * nq + i, 0)),
                  pl.BlockSpec((seq, 1024), lambda b, i: (b, 0)),
                  pl.BlockSpec((seq, 512), lambda b, i: (b, 0))],
        out_specs=[pl.BlockSpec((tq, 512), lambda b, i: (b * nq + i, 0)),
                   pl.BlockSpec((1, 2, 4, tq), lambda b, i: (b, 0, 0, i))],
        scratch_shapes=[pltpu.VMEM((N_HEADS * ext, seq), BF16)],
        compiler_params=_params(2),
    )(qf, kf, v)


def _mla_bwd_call(qf, kf, v, do, o, lse, n_seq, seq):
    tq = min(ATT_TILE, seq)
    nq = seq // tq

    nh = 4
    heads = range(nh)
    lanes = [slice(h * HEAD_LANES, (h + 1) * HEAD_LANES) for h in heads]

    def body(q_ref, k_ref, v_ref, do_ref, o_ref, lse_ref, dq_ref, dk_ref, dv_ref,
             kt_ref, dot_ref, delta_ref, dqt_ref):
        eye = _eye()
        lo = _lane_lo()
        sub_lo = lax.broadcasted_iota(jnp.int32, (HEAD_LANES, 1), 0) < HALF
        ones_lo = jnp.where(jnp.broadcast_to(lo, (8, HEAD_LANES)), 1.0, 0.0).astype(BF16)
        ones_hi = jnp.where(jnp.broadcast_to(lo, (8, HEAD_LANES)), 0.0, 1.0).astype(BF16)

        for t in range(nq):
            r = slice(t * tq, (t + 1) * tq)
            kv = k_ref[r, :]
            for h in heads:
                kt_ref[lanes[h], r] = _dot_nt(eye, kv[:, lanes[h]]).astype(BF16)
            for p in range(nh // 2):
                dov = do_ref[r, lanes[p]]
                dt = _dot_nt(eye, dov)
                dot_ref[2 * p, :, r] = jnp.where(sub_lo, dt, 0.0).astype(BF16)
                dot_ref[2 * p + 1, :, r] = jnp.where(sub_lo, 0.0, dt).astype(BF16)
                prod = dov.astype(F32) * o_ref[r, lanes[p]]
                p_hi = prod.astype(BF16)
                p_lo = (prod - p_hi.astype(F32)).astype(BF16)
                delta_ref[2 * p, :, r] = _dot_nt(ones_lo, p_hi) + _dot_nt(ones_lo, p_lo)
                delta_ref[2 * p + 1, :, r] = _dot_nt(ones_hi, p_hi) + _dot_nt(ones_hi, p_lo)
        dqt_ref[...] = jnp.zeros_like(dqt_ref)

        def k_step(kt, _):
            kr = pl.ds(pl.multiple_of(kt * tq, tq), tq)
            k = k_ref[kr, :]
            vv = v_ref[kr, :]
            k_t = kt_ref[:, kr]
            krow = kt * tq + lax.broadcasted_iota(jnp.int32, (tq, 1), 0)

            def make_step(masked, n_tiles):
                def q_step(qt0, carry):
                    tiles = range(n_tiles)
                    qrs = [pl.ds(pl.multiple_of((qt0 + t) * tq, tq), tq) for t in tiles]
                    qs = [q_ref[qr, :] for qr in qrs]
                    do_ts = [[dot_ref[h, :, qr] for h in heads] for qr in qrs]
                    sts = [[_dot_nt(k[:, lanes[h]], qs[t][:, lanes[h]]) for h in heads] for t in tiles]
                    dpts = [[_dot(vv[:, lanes[h // 2]], do_ts[t][h]) for h in heads] for t in tiles]
                    if masked:
                        keep = krow <= (qt0 * tq + lax.broadcasted_iota(jnp.int32, (1, tq), 1))
                    pts, dsts = [], []
                    for t in tiles:
                        pts.append([])
                        dsts.append([])
                        for h in heads:
                            pt = jnp.exp2(sts[t][h] - lse_ref[0, 0, h:h + 1, qrs[t]])
                            if masked and t == 0:
                                pt = jnp.where(keep, pt, 0.0)
                            dsts[t].append((pt * (dpts[t][h] - delta_ref[h, 0:1, qrs[t]])).astype(BF16))
                            pts[t].append(pt.astype(BF16))
                    out = []
                    for h in heads:
                        hh = h % 2
                        half = slice(hh * HALF, (hh + 1) * HALF)
                        dst_all = jnp.concatenate([dsts[t][h] for t in tiles], axis=1)
                        pt_all = jnp.concatenate([pts[t][h] for t in tiles], axis=1)
                        do_all = jnp.concatenate([do_ts[t][h][half, :] for t in tiles], axis=1)
                        q_all = jnp.concatenate([qs[t][:, lanes[h]] for t in tiles], axis=0)
                        dvt = _dot_nt(do_all, pt_all)
                        dk = _dot(dst_all, q_all)
                        for t in tiles:
                            dqt_ref[lanes[h], qrs[t]] += _dot(k_t[lanes[h], :], dsts[t][h])
                        out += [carry[2 * h] + dk, carry[2 * h + 1] + dvt]
                    return tuple(out)
                return q_step

            init = (jnp.zeros((tq, HEAD_LANES), F32), jnp.zeros((HALF, tq), F32)) * nh
            count = nq - kt
            carry = lax.cond(count >= 2, lambda c: make_step(True, 2)(kt, c), lambda c: make_step(True, 1)(kt, c), init)
            carry = lax.fori_loop(1, count // 2, lambda j, c: make_step(False, 2)(kt + 2 * j, c), carry)
            carry = lax.cond(jnp.logical_and(count % 2 == 1, count >= 3),
                             lambda c: make_step(False, 1)(nq - 1, c), lambda c: c, carry)
            for h in heads:
                dk_ref[kr, lanes[h]] = carry[2 * h]
            for p in range(nh // 2):
                dv_ref[kr, lanes[p]] = jnp.concatenate([carry[4 * p + 1], carry[4 * p + 3]], axis=0).T
            return 0

        lax.fori_loop(0, nq, k_step, 0)
        for t in range(nq):
            r = slice(t * tq, (t + 1) * tq)
            for h in heads:
                dq_ref[r, lanes[h]] = dqt_ref[lanes[h], r].T

    n_tok = qf.shape[0]
    groups = N_HEADS // nh
    blk = lambda w: pl.BlockSpec((seq, w), lambda b, g: (b, g))
    return pl.pallas_call(
        body, name="mla_bwd", grid=(n_seq, groups),
        out_shape=[jax.ShapeDtypeStruct((n_tok, 1024), F32), jax.ShapeDtypeStruct((n_tok, 1024), F32),
                   jax.ShapeDtypeStruct((n_tok, 512), F32)],
        in_specs=[blk(512), blk(512), blk(256), blk(256), blk(256),
                  pl.BlockSpec((1, 1, nh, seq), lambda b, g: (b, g, 0, 0))],
        out_specs=[blk(512), blk(512), blk(256)],
        scratch_shapes=[pltpu.VMEM((nh * HEAD_LANES, seq), BF16), pltpu.VMEM((nh, HEAD_LANES, seq), BF16),
                        pltpu.VMEM((nh, 8, seq), F32), pltpu.VMEM((nh * HEAD_LANES, seq), F32)],
        compiler_params=_params(2),
    )(qf, kf, v, do, o, lse)


SWA_BLOCKS = 2


def _swa_block(n, pos_col_ref, posq):
    w = SWA_WINDOW
    start = pl.multiple_of(jnp.maximum(n - 1, 0) * w, w)
    posk = pos_col_ref[pl.ds(start, 2 * w), :]
    rel = (n * w + lax.broadcasted_iota(jnp.int32, (1, w), 1)) - (start + lax.broadcasted_iota(jnp.int32, (2 * w, 1), 0))
    valid = jnp.logical_and(rel >= 0, rel < w)
    return start, jnp.where(valid, posq - posk, 1e30)


def _alibi(h):
    return LOG2E * 2.0 ** -(h + 1)


def _transpose_rows(eye, src_ref, dst_ref, seq, width):
    step = 2 * SWA_WINDOW
    for t in range(seq // step):
        for p in range(width // HEAD_LANES):
            lanes = slice(p * HEAD_LANES, (p + 1) * HEAD_LANES)
            dst_ref[lanes, t * step:(t + 1) * step] = _dot_nt(eye, src_ref[t * step:(t + 1) * step, lanes]).astype(BF16)


def _swa_fwd_call(qs, kd, vd, pos_col, pos_row, sinks, n_seq, seq):
    w = SWA_WINDOW
    qb = SWA_BLOCKS
    steps = seq // (qb * w)
    ext = HALF + 16

    def body(q_ref, k_ref, v_ref, pc_ref, pr_ref, sink_ref, o_ref, lse_ref, vt_ref):
        n = pl.program_id(1)
        lo = _lane_lo()
        hi = jnp.logical_not(lo)
        eye = _eye()

        @pl.when(n == 0)
        def _():
            step = 2 * w
            for kv in range(2):
                vt_ref[kv * ext + HALF:(kv + 1) * ext, :] = jnp.ones((16, seq), BF16)
                for t in range(seq // step):
                    v_t = _dot_nt(eye, v_ref[t * step:(t + 1) * step, kv * HEAD_LANES:(kv + 1) * HEAD_LANES])
                    vt_ref[kv * ext:kv * ext + HALF, t * step:(t + 1) * step] = v_t[:HALF, :].astype(BF16)

        heads = range(N_HEADS)
        blocks = range(qb)
        geo = [_swa_block(n * qb + bi, pc_ref, pr_ref[bi]) for bi in blocks]
        wins = [pl.ds(g[0], 2 * w) for g in geo]
        kwins = [k_ref[win, :] for win in wins]
        vts = [vt_ref[:, win] for win in wins]
        sts = []
        for bi in blocks:
            q = q_ref[bi * w:(bi + 1) * w, :]
            sts.append([])
            for h in heads:
                qp = q[:, (h // 2) * HEAD_LANES:(h // 2 + 1) * HEAD_LANES]
                qh = jnp.where(lo if h % 2 == 0 else hi, qp, jnp.zeros_like(qp))
                sts[bi].append(_dot_nt(kwins[bi][:, (h // 4) * HEAD_LANES:(h // 4 + 1) * HEAD_LANES], qh))
        ps, ms = [], []
        for bi in blocks:
            ps.append([])
            ms.append([])
            for h in heads:
                s = sts[bi][h] - _alibi(h) * geo[bi][1]
                m = jnp.maximum(jnp.max(s, axis=0, keepdims=True), sink_ref[0, h] * LOG2E)
                ps[bi].append(jnp.exp2(s - m).astype(BF16))
                ms[bi].append(m)
        for bi in blocks:
            ots = []
            for h in heads:
                pv = _dot(vts[bi][(h // 4) * ext:(h // 4 + 1) * ext, :], ps[bi][h])
                l = pv[HALF:HALF + 1, :] + jnp.exp2(sink_ref[0, h] * LOG2E - ms[bi][h])
                ots.append(pv[:HALF, :] * (1.0 / l))
                lse_ref[0, h:h + 1, bi * w:(bi + 1) * w] = ms[bi][h] + jnp.log2(l)
            o_ref[bi * w:(bi + 1) * w, :] = jnp.concatenate(ots, axis=0).T

    n_tok = qs.shape[0]
    tok = lambda width: pl.BlockSpec((qb * w, width), lambda b, n: (b * steps + n, 0))
    whole = lambda width: pl.BlockSpec((seq, width), lambda b, n: (b, 0))
    return pl.pallas_call(
        body, name="swa_fwd", grid=(n_seq, steps),
        out_shape=[jax.ShapeDtypeStruct((n_tok, 512), F32), jax.ShapeDtypeStruct((n_seq, N_HEADS, seq), F32)],
        in_specs=[tok(512), whole(256), whole(256), whole(1), pl.BlockSpec((qb, 1, w), lambda b, n: (b * steps + n, 0, 0)),
                  pl.BlockSpec(memory_space=pltpu.SMEM)],
        out_specs=[tok(512), pl.BlockSpec((1, N_HEADS, qb * w), lambda b, n: (b, 0, n))],
        scratch_shapes=[pltpu.VMEM((2 * ext, seq), BF16)],
        compiler_params=_params(2),
    )(qs, kd, vd, pos_col, pos_row, sinks)


def _swa_bwd_call(qs, kd, vd, do, o, lse, pos_col, pos_row, sinks, n_seq, seq):
    w = SWA_WINDOW
    qb = SWA_BLOCKS
    steps = seq // (qb * w)

    def body(q_ref, k_ref, v_ref, do_ref, o_ref, lse_ref, pc_ref, pr_ref, sink_ref, dq_ref, dk_ref, dv_ref, dsink_ref,
             kt_ref):
        b, n = pl.program_id(0), pl.program_id(1)
        lo = _lane_lo()
        hi = jnp.logical_not(lo)
        sub_lo = lax.broadcasted_iota(jnp.int32, (HEAD_LANES, 1), 0) < HALF
        eye = _eye()
        ones_lo = jnp.where(jnp.broadcast_to(lo, (8, HEAD_LANES)), 1.0, 0.0).astype(BF16)
        ones_hi = jnp.where(jnp.broadcast_to(lo, (8, HEAD_LANES)), 0.0, 1.0).astype(BF16)

        @pl.when(n == 0)
        def _():
            dk_ref[...] = jnp.zeros_like(dk_ref)
            dv_ref[...] = jnp.zeros_like(dv_ref)
            _transpose_rows(eye, k_ref, kt_ref, seq, 2 * HEAD_LANES)

        @pl.when(jnp.logical_and(n == 0, b == 0))
        def _():
            dsink_ref[...] = jnp.zeros_like(dsink_ref)

        heads = range(N_HEADS)
        blocks = range(qb)
        kv_lanes = lambda h: slice((h // 4) * HEAD_LANES, (h // 4 + 1) * HEAD_LANES)
        geo = [_swa_block(n * qb + bi, pc_ref, pr_ref[bi]) for bi in blocks]
        wins = [pl.ds(g[0], 2 * w) for g in geo]
        kwins = [k_ref[win, :] for win in wins]
        vwins = [v_ref[win, :] for win in wins]

        do_ts, deltas, qms, doms = [], [], [], []
        for bi in blocks:
            rows = slice(bi * w, (bi + 1) * w)
            for lst in (do_ts, deltas, qms, doms):
                lst.append([])
            for j in range(N_HEADS // 2):
                pair = slice(j * HEAD_LANES, (j + 1) * HEAD_LANES)
                dop = do_ref[rows, pair]
                qp = q_ref[rows, pair]
                dt = _dot_nt(eye, dop)
                prod = dop.astype(F32) * o_ref[rows, pair]
                p_hi = prod.astype(BF16)
                p_lo = (prod - p_hi.astype(F32)).astype(BF16)
                for hh in range(2):
                    half, ones = (lo, ones_lo) if hh == 0 else (hi, ones_hi)
                    do_ts[bi].append(jnp.where(sub_lo, dt, 0.0).astype(BF16) if hh == 0
                                     else jnp.where(sub_lo, 0.0, dt).astype(BF16))
                    deltas[bi].append((_dot_nt(ones, p_hi) + _dot_nt(ones, p_lo))[0:1, :])
                    qms[bi].append(jnp.where(half, qp, jnp.zeros_like(qp)))
                    doms[bi].append(jnp.where(half, dop, jnp.zeros_like(dop)))
        sts = [[_dot_nt(kwins[bi][:, kv_lanes(h)], qms[bi][h]) for h in heads] for bi in blocks]
        dpts = [[_dot(vwins[bi][:, kv_lanes(h)], do_ts[bi][h]) for h in heads] for bi in blocks]
        pts, dsts = [], []
        for bi in blocks:
            pts.append([])
            dsts.append([])
            for h in heads:
                lse_h = lse_ref[0, h:h + 1, bi * w:(bi + 1) * w]
                pt = jnp.exp2(sts[bi][h] - _alibi(h) * geo[bi][1] - lse_h)
                dsts[bi].append((pt * (dpts[bi][h] - deltas[bi][h])).astype(BF16))
                pts[bi].append(pt.astype(BF16))
                dsink_ref[h:h + 1, :] += -jnp.exp2(sink_ref[0, h] * LOG2E - lse_h) * deltas[bi][h]
        for bi in blocks:
            for kv in range(2):
                group = range(4 * kv, 4 * kv + 4)
                dst_all = jnp.concatenate([dsts[bi][h] for h in group], axis=1)
                pt_all = jnp.concatenate([pts[bi][h] for h in group], axis=1)
                q_all = jnp.concatenate([qms[bi][h] for h in group], axis=0)
                do_all = jnp.concatenate([doms[bi][h] for h in group], axis=0)
                dk_ref[wins[bi], kv_lanes(4 * kv)] += _dot(dst_all, q_all)
                dv_ref[wins[bi], kv_lanes(4 * kv)] += _dot(pt_all, do_all)
        for bi in blocks:
            ktw = kt_ref[:, wins[bi]]
            for j in range(N_HEADS // 2):
                k_t = ktw[kv_lanes(2 * j), :]
                dq_t = jnp.where(sub_lo, _dot(k_t, dsts[bi][2 * j]), _dot(k_t, dsts[bi][2 * j + 1]))
                dq_ref[bi * w:(bi + 1) * w, j * HEAD_LANES:(j + 1) * HEAD_LANES] = dq_t.T * SWA_SCALE

    n_tok = qs.shape[0]
    tok = lambda width: pl.BlockSpec((qb * w, width), lambda b, n: (b * steps + n, 0))
    whole = lambda width: pl.BlockSpec((seq, width), lambda b, n: (b, 0))
    return pl.pallas_call(
        body, name="swa_bwd", grid=(n_seq, steps),
        out_shape=[jax.ShapeDtypeStruct((n_tok, 512), F32), jax.ShapeDtypeStruct((n_tok, 256), F32),
                   jax.ShapeDtypeStruct((n_tok, 256), F32), jax.ShapeDtypeStruct((N_HEADS, HEAD_LANES), F32)],
        in_specs=[tok(512), whole(256), whole(256), pl.BlockSpec((qb * w, 512), lambda b, n: (b * steps + n, 1)), tok(512),
                  pl.BlockSpec((1, N_HEADS, qb * w), lambda b, n: (b, 0, n)),
                  whole(1), pl.BlockSpec((qb, 1, w), lambda b, n: (b * steps + n, 0, 0)),
                  pl.BlockSpec(memory_space=pltpu.SMEM)],
        out_specs=[tok(512), whole(256), whole(256), _full((N_HEADS, HEAD_LANES))],
        scratch_shapes=[pltpu.VMEM((2 * HEAD_LANES, seq), BF16)],
        compiler_params=_params(2),
    )(qs, kd, vd, do, o, lse, pos_col, pos_row, sinks)


def _post_call(x, target, o_mla, o_swa, gates, mod, b_ada, fg, w_out, w_out_t, seq):
    n_tok = x.shape[0]
    tm = min(TOKEN_TILE, seq)
    per_seq = seq // tm
    n_seq = n_tok // seq

    def body(x_ref, t_ref, om_ref, os_ref, g_ref, mod_ref, bada_ref, fg_ref, w_ref, wt_ref,
             dx2_ref, do_ref, dg_ref, gw_ref, gfg_ref, dgate_ref, loss_ref):
        i = pl.program_id(0)

        @pl.when(i == 0)
        def _():
            gw_ref[...] = jnp.zeros_like(gw_ref)
            gfg_ref[...] = jnp.zeros_like(gfg_ref)
            loss_ref[...] = jnp.zeros_like(loss_ref)

        @pl.when(i % per_seq == 0)
        def _():
            dgate_ref[...] = jnp.zeros_like(dgate_ref)

        gate = mod_ref[0][:, 2 * D_MODEL:] + bada_ref[:, 2 * D_MODEL:]
        fgv = fg_ref[...]
        subs = _sub_tiles(tm)
        gs = [g_ref[r, :] for r in subs]
        os_ = [jnp.concatenate([om_ref[r, :], os_ref[r, :]], axis=-1) for r in subs]
        sgs = [_sigmoid(g) for g in gs]
        sils = [g * sg for g, sg in zip(gs, sgs)]
        ypres = [(o * sil).astype(BF16) for o, sil in zip(os_, sils)]
        ys = [_dot(ypre, w_ref[...]) for ypre in ypres]
        dys, loss, gfg, dgate = [], 0.0, 0.0, 0.0
        for r, y in zip(subs, ys):
            x2 = x_ref[r, :] + gate * y
            r2 = lax.rsqrt(jnp.mean(x2 * x2, axis=-1, keepdims=True) + EPS)
            xn2 = x2 * r2
            err = xn2 * fgv - t_ref[r, :]
            loss = loss + jnp.sum(jnp.sum(err * err, axis=-1, keepdims=True), axis=0, keepdims=True)
            dout = err * (1.0 / D_MODEL)
            gfg = gfg + jnp.sum(dout * xn2, axis=0, keepdims=True)
            dxn2 = dout * fgv
            dx2 = r2 * (dxn2 - xn2 * jnp.mean(dxn2 * xn2, axis=-1, keepdims=True))
            dx2_ref[r, :] = dx2
            dgate = dgate + jnp.sum(dx2 * y, axis=0, keepdims=True)
            dys.append((dx2 * gate).astype(BF16))
        loss_ref[...] += jnp.broadcast_to(loss * (0.5 / D_MODEL), loss_ref.shape)
        gfg_ref[...] += gfg
        dgate_ref[0] += dgate
        gw_ref[...] += _dot_tn(jnp.concatenate(ypres, axis=0), jnp.concatenate(dys, axis=0))
        dypres = [_dot(dy, wt_ref[...]) for dy in dys]
        for r, dypre, o, g, sg, sil in zip(subs, dypres, os_, gs, sgs, sils):
            do_ref[r, :] = (dypre * sil).astype(BF16)
            dg_ref[r, :] = (dypre * o * (sg * (1.0 + g * (1.0 - sg)))).astype(BF16)

    tok = lambda w: pl.BlockSpec((tm, w), lambda i: (i, 0))
    per_b = pl.BlockSpec((1, 1, 3 * D_MODEL), lambda i: (i // per_seq, 0, 0))
    return pl.pallas_call(
        body, name="post", grid=(n_tok // tm,),
        out_shape=[jax.ShapeDtypeStruct((n_tok, D_MODEL), F32), jax.ShapeDtypeStruct((n_tok, D_MODEL), BF16),
                   jax.ShapeDtypeStruct((n_tok, D_MODEL), BF16), jax.ShapeDtypeStruct((D_MODEL, D_MODEL), F32),
                   jax.ShapeDtypeStruct((1, D_MODEL), F32), jax.ShapeDtypeStruct((n_seq, 1, D_MODEL), F32),
                   jax.ShapeDtypeStruct((1, HEAD_LANES), F32)],
        in_specs=[tok(D_MODEL), tok(D_MODEL), tok(512), tok(512), tok(D_MODEL), per_b, _full(b_ada.shape),
                  _full(fg.shape), _full(w_out.shape), _full(w_out_t.shape)],
        out_specs=[tok(D_MODEL), tok(D_MODEL), tok(D_MODEL), _full((D_MODEL, D_MODEL)), _full((1, D_MODEL)),
                   pl.BlockSpec((1, 1, D_MODEL), lambda i: (i // per_seq, 0, 0)), _full((1, HEAD_LANES))],
        compiler_params=_params(1),
    )(x, target, o_mla, o_swa, gates, mod, b_ada, fg, w_out, w_out_t)


def _mid_bwd_call(dqf, dkf, dv, zqkv, pos_col, qg, kvg, inv128, wq2, wkv, seq):
    n_tok = dqf.shape[0]
    tm = min(TOKEN_TILE, seq)

    def body(dq_ref, dk_ref, dv_ref, z_ref, pos_ref, qg_ref, kvg_ref, inv_ref, wq_ref, wkv_ref,
             dz_ref, dkr_ref, gwq_ref, gwkv_ref, gqg_ref, gkvg_ref):
        i = pl.program_id(0)

        @pl.when(i == 0)
        def _():
            gwq_ref[...] = jnp.zeros_like(gwq_ref)
            gwkv_ref[...] = jnp.zeros_like(gwkv_ref)
            gqg_ref[...] = jnp.zeros_like(gqg_ref)
            gkvg_ref[...] = jnp.zeros_like(gkvg_ref)

        cos, sin = _rope_tables(pos_ref[...], inv_ref[...])
        cf, sf = jnp.tile(cos, (1, N_HEADS)), jnp.tile(sin, (1, N_HEADS))
        dq = dq_ref[...] * MLA_SCALE
        dqr = jnp.concatenate([dq * cf, dq * sf], axis=-1).astype(BF16)
        zq, zkv = z_ref[:, :Q_LORA], z_ref[:, Q_LORA:]
        qgv, kvgv = qg_ref[...], kvg_ref[...]

        rq = lax.rsqrt(jnp.mean(zq * zq, axis=-1, keepdims=True) + EPS)
        xq = zq * rq
        gwq_ref[...] += _dot_tn((xq * qgv).astype(BF16), dqr)
        dqn = _dot_nt(dqr, wq_ref[...])
        gqg_ref[...] += jnp.sum(dqn * xq, axis=0, keepdims=True)
        dxq = dqn * qgv
        dz_ref[:, :Q_LORA] = (rq * (dxq - xq * jnp.mean(dxq * xq, axis=-1, keepdims=True))).astype(BF16)

        dk = dk_ref[...] * LN2
        dkv = jnp.concatenate([dk, dv_ref[...]], axis=-1).astype(BF16)
        rkv = lax.rsqrt(jnp.mean(zkv * zkv, axis=-1, keepdims=True) + EPS)
        xkv = zkv * rkv
        gwkv_ref[...] += _dot_tn((xkv * kvgv).astype(BF16), dkv)
        dkvn = _dot_nt(dkv, wkv_ref[...])
        gkvg_ref[...] += jnp.sum(dkvn * xkv, axis=0, keepdims=True)
        dxkv = dkvn * kvgv
        dz_ref[:, Q_LORA:] = (rkv * (dxkv - xkv * jnp.mean(dxkv * xkv, axis=-1, keepdims=True))).astype(BF16)

        dkpe = dk[:, :HEAD_LANES]
        for h in range(1, N_HEADS):
            dkpe = dkpe + dk[:, h * HEAD_LANES:(h + 1) * HEAD_LANES]
        dkr_ref[:, :HEAD_LANES] = (dkpe * cos).astype(BF16)
        dkr_ref[:, HEAD_LANES:] = (dkpe * sin).astype(BF16)

    tok = lambda w: pl.BlockSpec((tm, w), lambda i: (i, 0))
    return pl.pallas_call(
        body, name="mid_bwd", grid=(n_tok // tm,),
        out_shape=[jax.ShapeDtypeStruct((n_tok, 640), BF16), jax.ShapeDtypeStruct((n_tok, 256), BF16),
                   jax.ShapeDtypeStruct(wq2.shape, F32), jax.ShapeDtypeStruct(wkv.shape, F32),
                   jax.ShapeDtypeStruct((1, Q_LORA), F32), jax.ShapeDtypeStruct((1, KV_LORA), F32)],
        in_specs=[tok(1024), tok(1024), tok(512), tok(640), tok(1), _full(qg.shape), _full(kvg.shape),
                  _full(inv128.shape), _full(wq2.shape), _full(wkv.shape)],
        out_specs=[tok(640), tok(256), _full(wq2.shape), _full(wkv.shape), _full((1, Q_LORA)), _full((1, KV_LORA))],
        compiler_params=_params(1),
    )(dqf, dkf, dv, zqkv, pos_col, qg, kvg, inv128, wq2, wkv)


def _in_bwd_call(x, dx2, dz, dkr, dg, dqs, dkd, dvd, mod, b_ada, ng, wa_t, wkr2_t, seq):
    n_tok = x.shape[0]
    tm = min(TOKEN_TILE, seq)
    per_seq = seq // tm
    n_seq = n_tok // seq

    def body(x_ref, dx2_ref, dz_ref, dkr_ref, dg_ref, dqs_ref, dkd_ref, dvd_ref, mod_ref, bada_ref, ng_ref,
             wat_ref, wkrt_ref, gx_ref, gwa_ref, gwkr_ref, gng_ref, dshift_ref, dscale_ref):
        i = pl.program_id(0)

        @pl.when(i == 0)
        def _():
            gwa_ref[...] = jnp.zeros_like(gwa_ref)
            gwkr_ref[...] = jnp.zeros_like(gwkr_ref)
            gng_ref[...] = jnp.zeros_like(gng_ref)

        @pl.when(i % per_seq == 0)
        def _():
            dshift_ref[...] = jnp.zeros_like(dshift_ref)
            dscale_ref[...] = jnp.zeros_like(dscale_ref)

        xv = x_ref[...]
        modv = mod_ref[0] + bada_ref[...]
        shift, scale = modv[:, :D_MODEL], modv[:, D_MODEL:2 * D_MODEL]
        ngv = ng_ref[...]
        r1 = lax.rsqrt(jnp.mean(xv * xv, axis=-1, keepdims=True) + EPS)
        xn = xv * r1
        hb = ((xn * ngv) * (1.0 + scale) + shift).astype(BF16)

        dgv = dg_ref[...]
        pieces = [(A_ZQ, dz_ref[...]), (A_GM, dgv[:, :512]), (A_QS, dqs_ref[...].astype(BF16)),
                  (A_KD, (dkd_ref[...] * LN2).astype(BF16)),
                  (A_VD, dvd_ref[...].astype(BF16)), (A_GS, dgv[:, 512:])]
        dkr = dkr_ref[...]
        gwkr_ref[...] += _dot_tn(hb, dkr)
        dh = _dot(dkr, wkrt_ref[...])
        for off, piece in pieces:
            wd = piece.shape[1]
            gwa_ref[:, off:off + wd] += _dot_tn(hb, piece)
            dh = dh + _dot(piece, wat_ref[off:off + wd, :])

        dshift_ref[0] += jnp.sum(dh, axis=0, keepdims=True)
        dscale_ref[0] += jnp.sum(dh * (xn * ngv), axis=0, keepdims=True)
        gng_ref[...] += jnp.sum(dh * xn * (1.0 + scale), axis=0, keepdims=True)
        dxn = dh * ngv * (1.0 + scale)
        gx_ref[...] = dx2_ref[...] + r1 * (dxn - xn * jnp.mean(dxn * xn, axis=-1, keepdims=True))

    tok = lambda w: pl.BlockSpec((tm, w), lambda i: (i, 0))
    per_b = lambda w: pl.BlockSpec((1, 1, w), lambda i: (i // per_seq, 0, 0))
    return pl.pallas_call(
        body, name="in_bwd", grid=(n_tok // tm,),
        out_shape=[jax.ShapeDtypeStruct((n_tok, D_MODEL), F32), jax.ShapeDtypeStruct((D_MODEL, A_END), F32),
                   jax.ShapeDtypeStruct((D_MODEL, 256), F32), jax.ShapeDtypeStruct((1, D_MODEL), F32),
                   jax.ShapeDtypeStruct((n_seq, 1, D_MODEL), F32), jax.ShapeDtypeStruct((n_seq, 1, D_MODEL), F32)],
        in_specs=[tok(D_MODEL), tok(D_MODEL), tok(640), tok(256), tok(D_MODEL), tok(512), tok(256), tok(256),
                  per_b(3 * D_MODEL), _full(b_ada.shape), _full(ng.shape), _full(wa_t.shape), _full(wkr2_t.shape)],
        out_specs=[tok(D_MODEL), _full((D_MODEL, A_END)), _full((D_MODEL, 256)), _full((1, D_MODEL)),
                   per_b(D_MODEL), per_b(D_MODEL)],
        compiler_params=_params(1),
    )(x, dx2, dz, dkr, dg, dqs, dkd, dvd, mod, b_ada, ng, wa_t, wkr2_t)


def _adam_math(w, g, m, v):
    m_new = ADAM_B1 * m + (1.0 - ADAM_B1) * g
    v_new = ADAM_B2 * v + (1.0 - ADAM_B2) * (g * g)
    m_hat = m_new / (1.0 - ADAM_B1 ** ADAM_STEP)
    v_hat = v_new / (1.0 - ADAM_B2 ** ADAM_STEP)
    delta = -ADAM_LR * (m_hat / (jnp.sqrt(v_hat) + ADAM_EPS) + ADAM_WD * w)
    return delta, m_new, v_new


def _adam_call(name, w, g, m, v):
    rows, cols = w.shape
    tr = 256 if rows % 256 == 0 else rows

    def body(w_ref, g_ref, m_ref, v_ref, d_ref, mo_ref, vo_ref):
        d, mn, vn = _adam_math(w_ref[...], g_ref[...], m_ref[...], v_ref[...])
        d_ref[...] = d
        mo_ref[...] = mn
        vo_ref[...] = vn

    spec = pl.BlockSpec((tr, cols), lambda i: (i, 0))
    return pl.pallas_call(
        body, name=name, grid=(rows // tr,),
        out_shape=[jax.ShapeDtypeStruct(w.shape, F32)] * 3,
        in_specs=[spec] * 4, out_specs=[spec] * 3,
        compiler_params=_params(1),
    )(w, g, m, v)


def _ada_bwd_call(act_all, dmod_cols, w, m, v):
    rows, cols = w.shape
    tr = 256

    def body(a_ref, dm_ref, w_ref, m_ref, v_ref, g_ref, d_ref, mo_ref, vo_ref):
        g = _dot_tn(a_ref[...].astype(BF16), dm_ref[...].astype(BF16))
        d, mn, vn = _adam_math(w_ref[...], g, m_ref[...], v_ref[...])
        g_ref[...] = g
        d_ref[...] = d
        mo_ref[...] = mn
        vo_ref[...] = vn

    spec = pl.BlockSpec((tr, cols), lambda i: (i, 0))
    nb = act_all.shape[0]
    return pl.pallas_call(
        body, name="ada_bwd", grid=(rows // tr,),
        out_shape=[jax.ShapeDtypeStruct(w.shape, F32)] * 4,
        in_specs=[pl.BlockSpec((nb, tr), lambda i: (0, i)), _full(dmod_cols.shape), spec, spec, spec],
        out_specs=[spec] * 4,
        compiler_params=_params(1),
    )(act_all, dmod_cols, w, m, v)


SMALL_ROW = {"norm_gain": (0, 1024), "final_gain": (1024, 2048), "q_norm_gain": (2048, 2432),
             "kv_norm_gain": (2432, 2688), "swa_sinks": (2688, 2696), "loss": (2816, 2944)}
SMALL_ORDER = ("b_ada", "norm_gain", "q_norm_gain", "kv_norm_gain", "swa_sinks", "final_gain")


def _small_call(parts_all, n_seq, params):
    k = len(params)

    def body(p_ref, *refs):
        ins, outs, loss_ref = refs[:3 * k], refs[3 * k:7 * k], refs[7 * k]
        row = p_ref[n_seq:n_seq + 1, :]
        for dv in range(1, 8):
            r0 = dv * ROWS_PER_DEVICE + n_seq
            row = row + p_ref[r0:r0 + 1, :]
        gb = None
        for dv in range(8):
            for r in range(n_seq):
                r0 = dv * ROWS_PER_DEVICE + r
                gb = p_ref[r0:r0 + 1, :] if gb is None else gb + p_ref[r0:r0 + 1, :]
        for j, name in enumerate(SMALL_ORDER):
            g = gb if name == "b_ada" else row[:, SMALL_ROW[name][0]:SMALL_ROW[name][1]]
            d, mn, vn = _adam_math(ins[3 * j][...], g, ins[3 * j + 1][...], ins[3 * j + 2][...])
            outs[4 * j][...] = g
            outs[4 * j + 1][...] = d
            outs[4 * j + 2][...] = mn
            outs[4 * j + 3][...] = vn
        loss_ref[...] = row[:, SMALL_ROW["loss"][0]:SMALL_ROW["loss"][1]]

    flat = [t for p in params for t in p]
    res = pl.pallas_call(
        body, name="small_update", grid=(1,),
        out_shape=[jax.ShapeDtypeStruct(p[0].shape, F32) for p in params for _ in range(4)]
        + [jax.ShapeDtypeStruct((1, HEAD_LANES), F32)],
        in_specs=[_full(parts_all.shape)] + [_full(t.shape) for t in flat],
        out_specs=[_full(p[0].shape) for p in params for _ in range(4)] + [_full((1, HEAD_LANES))],
        compiler_params=_params(1),
    )(parts_all, *flat)
    return [res[4 * j:4 * j + 4] for j in range(k)], res[4 * k]


def _rot(t):
    half = t.shape[-1] // 2
    return jnp.concatenate([-t[..., half:], t[..., :half]], axis=-1)


def _rot_t(g):
    half = g.shape[-1] // 2
    return jnp.concatenate([g[..., half:], -g[..., :half]], axis=-1)


def _columns(segments, lo, hi):
    out, at = [], 0
    for seg in segments:
        n = seg.shape[1]
        a, b = max(lo, at), min(hi, at + n)
        if a < b:
            out.append(seg[:, a - at:b - at])
        at += n
    return out


def _prepare_weights(w_in_blocks, w_uq, w_ukv):
    o = [0]
    for s in IN_SPLITS:
        o.append(o[-1] + s)
    part = lambda a, b: _columns(w_in_blocks, a, b)
    dup = lambda a: part(a, a + 64) * 2 + part(a + 64, a + 128) * 2
    wa = jnp.concatenate(part(0, o[2]) + part(o[3], o[5]) + dup(o[5]) + dup(o[6]) + part(o[7], o[8]), axis=1)
    kr = jnp.concatenate(part(o[2], o[3]), axis=1)
    zc = lambda n: jnp.zeros((kr.shape[0], n), kr.dtype)
    wkr2 = jnp.concatenate([zc(64), kr, zc(32), zc(64), _rot(kr), zc(32)], axis=1)
    uq = w_uq.reshape(Q_LORA, N_HEADS, MLA_NOPE + MLA_ROPE)
    zq = jnp.zeros((Q_LORA, N_HEADS, 32), w_uq.dtype)
    uq_full = jnp.concatenate([uq, zq], axis=-1).reshape(Q_LORA, 1024)
    uq_rot = jnp.concatenate([jnp.zeros((Q_LORA, N_HEADS, 64), w_uq.dtype), _rot(uq[..., MLA_NOPE:]), zq],
                             axis=-1).reshape(Q_LORA, 1024)
    wq2 = jnp.concatenate([uq_full, uq_rot], axis=1)
    ukv = w_ukv.reshape(KV_LORA, N_HEADS, 128)
    k_full = jnp.concatenate([ukv[..., :64], jnp.zeros((KV_LORA, N_HEADS, 64), w_ukv.dtype)], axis=-1).reshape(KV_LORA, 1024)
    wkv = jnp.concatenate([k_full, ukv[..., 64:].reshape(KV_LORA, 512)], axis=1)
    return wa, wkr2, wq2, wkv


def _restore_grads(gwa, gwkr2, gwq2, gwkv):
    fold = lambda g: jnp.concatenate([g[:, 0:64] + g[:, 64:128], g[:, 128:192] + g[:, 192:256]], axis=1)
    gkr = gwkr2[:, 64:96] + _rot_t(gwkr2[:, 192:224])
    in_order = [gwa[:, :A_GM], gkr, gwa[:, A_GM:A_KD], fold(gwa[:, A_KD:A_VD]), fold(gwa[:, A_VD:A_GS]), gwa[:, A_GS:]]
    n = D_IN // 4
    g_in = [jnp.concatenate(_columns(in_order, k * n, (k + 1) * n), axis=1) for k in range(4)]
    gf = gwq2[:, :1024].reshape(Q_LORA, N_HEADS, 128)
    gr = gwq2[:, 1024:].reshape(Q_LORA, N_HEADS, 128)
    g_uq = jnp.concatenate([gf[..., :64], gf[..., 64:96] + _rot_t(gr[..., 64:96])], axis=-1).reshape(Q_LORA, 768)
    gk = gwkv[:, :1024].reshape(KV_LORA, N_HEADS, 128)[..., :64]
    gv = gwkv[:, 1024:].reshape(KV_LORA, N_HEADS, 64)
    g_ukv = jnp.concatenate([gk, gv], axis=-1).reshape(KV_LORA, 1024)
    return g_in, g_uq, g_ukv


def _local_step(x, positions, target, mod_rows, b_ada, ng, qg, kvg, sinks, fg, w_in_b, w_uq_b, w_ukv_b, w_out_b):
    n_seq, seq, _ = x.shape
    n_tok = n_seq * seq
    x2d = x.reshape(n_tok, D_MODEL)
    t2d = target.reshape(n_tok, D_MODEL)
    pos_f = positions.astype(F32)
    pos_col = pos_f.reshape(n_tok, 1)
    pos_row = pos_f.reshape(n_tok // SWA_WINDOW, 1, SWA_WINDOW)
    mod3 = mod_rows.reshape(n_seq, 1, 3 * D_MODEL)
    inv = ROPE_THETA ** (-jnp.arange(0, MLA_ROPE, 2, dtype=F32) / MLA_ROPE)
    inv128 = jnp.concatenate([jnp.zeros((64,), F32), inv, inv, jnp.zeros((32,), F32)]).reshape(1, 128)
    fg2 = fg.reshape(1, D_MODEL)

    wa, wkr2, wq2, wkv = _prepare_weights(w_in_b, w_uq_b, w_ukv_b)

    zqkv, gates, qf, kf, v, qs, kd, vd = _pre_call(x2d, pos_col, mod3, b_ada, ng, qg, kvg, inv128, wa, wkr2, wq2, wkv, seq)
    o_mla, lse_mla = _mla_fwd_call(qf, kf, v, n_seq, seq)
    o_swa, lse_swa = _swa_fwd_call(qs, kd, vd, pos_col, pos_row, sinks, n_seq, seq)
    dx2, do, dg, g_out, g_fg, dgate, loss = _post_call(x2d, t2d, o_mla, o_swa, gates, mod3, b_ada, fg2, w_out_b, w_out_b.T, seq)
    dqf, dkf, dv = _mla_bwd_call(qf, kf, v, do, o_mla, lse_mla, n_seq, seq)
    dqs, dkd, dvd, dsink = _swa_bwd_call(qs, kd, vd, do, o_swa, lse_swa, pos_col, pos_row, sinks, n_seq, seq)
    dz, dkr, g_wq2, g_wkv, g_qg, g_kvg = _mid_bwd_call(dqf, dkf, dv, zqkv, pos_col, qg, kvg, inv128, wq2, wkv, seq)
    gx, g_wa, g_wkr2, g_ng, dshift, dscale = _in_bwd_call(x2d, dx2, dz, dkr, dg, dqs, dkd, dvd, mod3, b_ada, ng,
                                                         wa.T, wkr2.T, seq)
    g_in, g_uq, g_ukv = _restore_grads(g_wa, g_wkr2, g_wq2, g_wkv)
    dmod = jnp.concatenate([dshift, dscale, dgate], axis=-1).reshape(n_seq, 3 * D_MODEL)
    small_row = jnp.concatenate([g_ng, g_fg, g_qg, g_kvg, jnp.pad(jnp.sum(dsink, axis=1).reshape(1, N_HEADS), ((0, 0), (0, 120))),
                                 loss, jnp.zeros((1, 128), F32)], axis=1)
    return gx.reshape(x.shape), (g_in, g_uq, g_ukv, g_out), small_row, dmod


def kernel(x, c, positions, w_ada, b_ada, norm_gain, w_in, q_norm_gain, kv_norm_gain, w_uq, w_ukv, swa_sinks, w_out, final_gain, loss_target, m_w_ada, m_b_ada, m_norm_gain, m_w_in, m_q_norm_gain, m_kv_norm_gain, m_w_uq, m_w_ukv, m_swa_sinks, m_w_out, m_final_gain, v_w_ada, v_b_ada, v_norm_gain, v_w_in, v_q_norm_gain, v_kv_norm_gain, v_w_uq, v_w_ukv, v_swa_sinks, v_w_out, v_final_gain):
    n_seq = x.shape[0]
    xi, yi, ci = lax.axis_index("x"), lax.axis_index("y"), lax.axis_index("c")
    dev = 4 * xi + 2 * yi + ci
    chip = 2 * xi + yi

    halves = lambda w: w.astype(BF16).reshape(2, w.shape[0] // 2, w.shape[1])
    c_blk = jnp.pad(c, ((0, ROWS_PER_DEVICE - n_seq), (0, 0)))
    act_all, pieces, f_in, f_uq, f_ukv, f_out = _comm_fwd_call(
        c_blk, w_ada[0], [halves(w_in[0]), halves(w_uq[0]), halves(w_ukv[0]), halves(w_out[0])])
    mine = lax.dynamic_slice_in_dim(pieces, dev * ROWS_PER_DEVICE, n_seq, axis=1)
    mod_rows = jnp.transpose(mine, (1, 0, 2)).reshape(n_seq, 3 * D_MODEL)
    cols = lambda t, r: jnp.transpose(t.reshape(4, r, -1), (1, 0, 2)).reshape(r, -1)
    w_in_blocks = [f_in[k].reshape(D_MODEL, -1) for k in range(4)]
    w_uq_b, w_ukv_b = cols(f_uq, Q_LORA), cols(f_ukv, KV_LORA)
    w_out_b = f_out.reshape(D_MODEL, D_MODEL)

    gx, (g_in_blocks, g_uq, g_ukv, g_out), small_row, dmod = _local_step(
        x, positions, loss_target, mod_rows, b_ada, norm_gain, q_norm_gain, kv_norm_gain, swa_sinks, final_gain,
        w_in_blocks, w_uq_b, w_ukv_b, w_out_b)

    by_owner = lambda g, n: jnp.transpose(g.reshape(g.shape[0], 4, n), (1, 0, 2)).reshape(4, 2, g.shape[0] // 2, n)
    grads = [jnp.stack(g_in_blocks).reshape(4, 2, D_MODEL // 2, -1), by_owner(g_uq, 192), by_owner(g_ukv, 256),
             g_out.reshape(4, 2, 128, D_MODEL)]
    part = jnp.concatenate([dmod, small_row, jnp.zeros((ROWS_PER_DEVICE - n_seq - 1, 3 * D_MODEL), F32)], axis=0)
    r_in, r_uq, r_ukv, r_out, parts_all = _comm_bwd_call(grads, part)
    g_in_s, g_uq_s = r_in.reshape(w_in.shape[1:]), r_uq.reshape(w_uq.shape[1:])
    g_ukv_s, g_out_s = r_ukv.reshape(w_ukv.shape[1:]), r_out.reshape(w_out.shape[1:])

    d_in, nm_in, nv_in = _adam_call("adam_w_in", w_in[0], g_in_s, m_w_in[0], v_w_in[0])
    d_uq, nm_uq, nv_uq = _adam_call("adam_w_uq", w_uq[0], g_uq_s, m_w_uq[0], v_w_uq[0])
    d_ukv, nm_ukv, nv_ukv = _adam_call("adam_w_ukv", w_ukv[0], g_ukv_s, m_w_ukv[0], v_w_ukv[0])
    d_out, nm_out, nv_out = _adam_call("adam_w_out", w_out[0], g_out_s, m_w_out[0], v_w_out[0])
    dmod_cols = lax.dynamic_slice_in_dim(parts_all, chip * 768, 768, axis=1)
    g_ada, d_ada, nm_ada, nv_ada = _ada_bwd_call(act_all, dmod_cols, w_ada[0], m_w_ada[0], v_w_ada[0])

    row = lambda t: t.reshape(1, -1)
    small = {"b_ada": (b_ada, m_b_ada, v_b_ada), "norm_gain": (norm_gain, m_norm_gain, v_norm_gain),
             "q_norm_gain": (q_norm_gain, m_q_norm_gain, v_q_norm_gain),
             "kv_norm_gain": (kv_norm_gain, m_kv_norm_gain, v_kv_norm_gain),
             "swa_sinks": (swa_sinks, m_swa_sinks, v_swa_sinks),
             "final_gain": (row(final_gain), row(m_final_gain), row(v_final_gain))}
    res, loss_row = _small_call(parts_all, n_seq, [small[name] for name in SMALL_ORDER])
    res = dict(zip(SMALL_ORDER, res))
    res["final_gain"] = [t.reshape(-1) for t in res["final_gain"]]
    e = lambda t: t[None]
    big = {"w_ada": (e(g_ada), e(d_ada), e(nm_ada), e(nv_ada)), "w_in": (e(g_in_s), e(d_in), e(nm_in), e(nv_in)),
           "w_uq": (e(g_uq_s), e(d_uq), e(nm_uq), e(nv_uq)), "w_ukv": (e(g_ukv_s), e(d_ukv), e(nm_ukv), e(nv_ukv)),
           "w_out": (e(g_out_s), e(d_out), e(nm_out), e(nv_out))}
    order = ("w_ada", "b_ada", "norm_gain", "w_in", "q_norm_gain", "kv_norm_gain", "w_uq", "w_ukv", "swa_sinks", "w_out",
             "final_gain")
    pick = lambda kind: [(big[n] if n in big else res[n])[kind] for n in order]
    return (loss_row[0, 0], gx, *pick(0), *pick(1), *pick(2), *pick(3))
```

```python
import functools

import jax
import jax.numpy as jnp
from jax import lax
from jax.experimental import pallas as pl
from jax.experimental.pallas import tpu as pltpu

F32 = jnp.float32
BF16 = jnp.bfloat16

D_MODEL = 1024
Q_LORA = 384
KV_LORA = 256
N_HEADS = 8
MLA_NOPE = 64
MLA_ROPE = 32
HEAD_LANES = 128
HALF = 64
SWA_WINDOW = 128
EPS = 1e-6
ROPE_THETA = 10000.0
MLA_SCALE = (MLA_NOPE + MLA_ROPE) ** -0.5
LOG2E = 1.4426950408889634
LN2 = 0.6931471805599453
SWA_SCALE = 64 ** -0.5
NEG = -1e30

ADAM_LR = 0.001
ADAM_B1 = 0.9
ADAM_B2 = 0.999
ADAM_EPS = 1e-08
ADAM_WD = 0.01
ADAM_STEP = 10

A_ZQ, A_ZKV, A_GM, A_QS, A_KD, A_VD, A_GS, A_END = 0, 384, 640, 1152, 1664, 1920, 2176, 2688
IN_SPLITS = (384, 256, 32, 512, 512, 128, 128, 512)
D_IN = sum(IN_SPLITS)

TOKEN_TILE = 512
ATT_TILE = 256
VMEM_LIMIT = 56 * 1024 * 1024


def _dot(a, b):
    return jnp.dot(a, b, preferred_element_type=F32)


def _dot_nt(a, b):
    return lax.dot_general(a, b, (((1,), (1,)), ((), ())), preferred_element_type=F32)


def _dot_tn(a, b):
    return lax.dot_general(a, b, (((0,), (0,)), ((), ())), preferred_element_type=F32)


def _params(n_grid):
    return pltpu.CompilerParams(dimension_semantics=("arbitrary",) * n_grid, vmem_limit_bytes=VMEM_LIMIT)


def _full(shape):
    nd = len(shape)
    return pl.BlockSpec(shape, lambda *_: (0,) * nd, pipeline_mode=pl.Buffered(1))


def _sigmoid(g):
    return 1.0 / (1.0 + jnp.exp(-g))


SUB_TILE = 256


def _sub_tiles(tm):
    sub = min(SUB_TILE, tm)
    return [slice(s * sub, (s + 1) * sub) for s in range(tm // sub)]


MESH = pl.DeviceIdType.MESH
ROWS_PER_DEVICE = 8
VMEM_SPEC = pl.BlockSpec(memory_space=pltpu.VMEM)
ANY_SPEC = pl.BlockSpec(memory_space=pl.ANY)


def _position():
    x, y, c = lax.axis_index("x"), lax.axis_index("y"), lax.axis_index("c")
    sibling = (x, y, 1 - c)
    others = [(1 - x, y, c), (x, 1 - y, c), (1 - x, 1 - y, c)]
    return (x, y, c), 4 * x + 2 * y + c, 2 * x + y, sibling, others


def _rows_of(dev):
    return pl.ds(pl.multiple_of(dev * ROWS_PER_DEVICE, ROWS_PER_DEVICE), ROWS_PER_DEVICE)


def _all_to_all_rows(block_ref, table_ref, dev, me, send_sems, recv_sems):
    x, y, c = me
    waits = []
    for k in range(1, 8):
        peer = (1 - x if k & 4 else x, 1 - y if k & 2 else y, 1 - c if k & 1 else c)
        pltpu.make_async_remote_copy(src_ref=block_ref, dst_ref=table_ref.at[_rows_of(dev)], send_sem=send_sems.at[k - 1],
                                     recv_sem=recv_sems.at[k - 1], device_id=peer, device_id_type=MESH).start()
        waits.append(pltpu.make_async_remote_copy(
            src_ref=block_ref, dst_ref=table_ref.at[_rows_of(jnp.bitwise_xor(dev, k))], send_sem=send_sems.at[k - 1],
            recv_sem=recv_sems.at[k - 1], device_id=peer, device_id_type=MESH))
    return waits


def _comm_fwd_call(c_blk, w_ada, shards):
    n = len(shards)

    def body(c_ref, wada_ref, *refs):
        w_refs, act_ref, pieces_ref, full_refs = refs[:n], refs[n], refs[n + 1], refs[n + 2:2 * n + 2]
        c_all_ref = refs[2 * n + 2]
        c_send, c_recv, p_send, p_recv, w_send, w_recv, f_send, f_recv, loc_sem = refs[2 * n + 3:]
        me, dev, chip, sibling, others = _position()
        core = me[2]
        chip_of = [2 * p[0] + p[1] for p in others]

        local = [pltpu.make_async_copy(w_refs[i], full_refs[i].at[chip], loc_sem.at[i]) for i in range(n)]
        for cp in local:
            cp.start()

        def over_ici(i, j, src_chip):
            return pltpu.make_async_remote_copy(
                src_ref=w_refs[i].at[core], dst_ref=full_refs[i].at[src_chip, core], send_sem=w_send.at[3 * i + j],
                recv_sem=w_recv.at[3 * i + j], device_id=others[j], device_id_type=MESH)

        def to_sibling(i, j, half):
            return pltpu.make_async_remote_copy(
                src_ref=full_refs[i].at[chip_of[j], half], dst_ref=full_refs[i].at[chip_of[j], half],
                send_sem=f_send.at[3 * i + j], recv_sem=f_recv.at[3 * i + j], device_id=sibling, device_id_type=MESH)

        sent = [over_ici(i, j, chip) for i in range(n) for j in range(3)]
        for cp in sent:
            cp.start()

        c_all_ref[_rows_of(dev), :] = c_ref[...]
        c_waits = _all_to_all_rows(c_ref, c_all_ref, dev, me, c_send, c_recv)
        for cp in c_waits:
            cp.wait()
        cv = c_all_ref[...]
        act = cv * _sigmoid(cv)
        act_ref[...] = act
        pieces_ref[chip] = _dot(act.astype(BF16), wada_ref[...].astype(BF16))
        piece = lambda j, src_chip: pltpu.make_async_remote_copy(
            src_ref=pieces_ref.at[chip], dst_ref=pieces_ref.at[src_chip], send_sem=p_send.at[j], recv_sem=p_recv.at[j],
            device_id=others[j], device_id_type=MESH)
        for j in range(3):
            piece(j, chip).start()
        for j in range(3):
            piece(j, chip).wait_send()
            piece(j, chip_of[j]).wait_recv()

        for i in range(n):
            for j in range(3):
                over_ici(i, j, chip_of[j]).wait_recv()
                to_sibling(i, j, core).start()
        for i in range(n):
            for j in range(3):
                to_sibling(i, j, 1 - core).wait_recv()
                to_sibling(i, j, core).wait_send()
        for cp in sent:
            cp.wait_send()
        for cp in local:
            cp.wait()

    rows = 8 * ROWS_PER_DEVICE
    dma = pltpu.SemaphoreType.DMA
    return pl.pallas_call(
        body, name="comm_fwd",
        out_shape=[jax.ShapeDtypeStruct((rows, D_MODEL), F32), jax.ShapeDtypeStruct((4, rows, w_ada.shape[1]), F32)]
        + [jax.ShapeDtypeStruct((4,) + s.shape, s.dtype) for s in shards],
        in_specs=[VMEM_SPEC, VMEM_SPEC] + [ANY_SPEC] * n,
        out_specs=[VMEM_SPEC, VMEM_SPEC] + [ANY_SPEC] * n,
        scratch_shapes=[pltpu.VMEM((rows, D_MODEL), F32), dma((7,)), dma((7,)), dma((3,)), dma((3,)),
                        dma((3 * n,)), dma((3 * n,)), dma((3 * n,)), dma((3 * n,)), dma((n,))],
        compiler_params=pltpu.CompilerParams(vmem_limit_bytes=VMEM_LIMIT),
    )(c_blk, w_ada, *shards)


def _comm_bwd_call(grads, part):
    n = len(grads)

    def body(part_ref, *refs):
        g_refs, f_refs, parts_ref = refs[:n], refs[n:2 * n], refs[2 * n]
        scratch = refs[2 * n + 1:]
        a_refs, b_refs, p_refs, r_refs = (scratch[k * n:(k + 1) * n] for k in range(4))
        s_send, s_recv, d_send, d_recv, e_send, e_recv, h_send, h_recv, loc_sem = scratch[4 * n:]
        me, dev, chip, sibling, others = _position()
        core = me[2]
        chip_of = [2 * p[0] + p[1] for p in others]

        parts_ref[_rows_of(dev), :] = part_ref[...]
        s_waits = _all_to_all_rows(part_ref, parts_ref, dev, me, s_send, s_recv)

        mine = [pltpu.make_async_copy(g_refs[i].at[:, core], a_refs[i], loc_sem.at[i]) for i in range(n)]
        swap = [pltpu.make_async_remote_copy(src_ref=g_refs[i].at[:, 1 - core], dst_ref=b_refs[i], send_sem=d_send.at[i],
                                             recv_sem=d_recv.at[i], device_id=sibling, device_id_type=MESH) for i in range(n)]
        order = sorted(range(n), key=lambda i: g_refs[i].shape[2] * g_refs[i].shape[3])
        for i in order:
            mine[i].start()
            swap[i].start()
        cross = [pltpu.make_async_remote_copy(src_ref=p_refs[i].at[chip_of[j]], dst_ref=r_refs[i].at[j],
                                              send_sem=e_send.at[3 * i + j], recv_sem=e_recv.at[3 * i + j],
                                              device_id=others[j], device_id_type=MESH) for i in range(n) for j in range(3)]
        for i in order:
            mine[i].wait()
            swap[i].wait()
            for k in range(4):
                s = a_refs[i][k] + b_refs[i][k]
                a_refs[i][k] = s
                p_refs[i][k] = s.astype(BF16)
            for j in range(3):
                cross[3 * i + j].start()
        share = {}
        for i in order:
            for j in range(3):
                cross[3 * i + j].wait()
            f_refs[i][core] = (a_refs[i][chip] + r_refs[i][0].astype(F32) + r_refs[i][1].astype(F32)
                               + r_refs[i][2].astype(F32))
            share[i] = pltpu.make_async_remote_copy(src_ref=f_refs[i].at[core], dst_ref=f_refs[i].at[core],
                                                    send_sem=h_send.at[i], recv_sem=h_recv.at[i], device_id=sibling,
                                                    device_id_type=MESH)
            share[i].start()
        for i in range(n):
            share[i].wait_send()
            pltpu.make_async_remote_copy(src_ref=f_refs[i].at[core], dst_ref=f_refs[i].at[1 - core], send_sem=h_send.at[i],
                                         recv_sem=h_recv.at[i], device_id=sibling, device_id_type=MESH).wait_recv()
        for cp in s_waits:
            cp.wait()

    rows = 8 * ROWS_PER_DEVICE
    dma = pltpu.SemaphoreType.DMA
    quarter = [(4,) + g.shape[2:] for g in grads]
    return pl.pallas_call(
        body, name="comm_bwd",
        out_shape=[jax.ShapeDtypeStruct((2,) + g.shape[2:], F32) for g in grads]
        + [jax.ShapeDtypeStruct((rows, part.shape[1]), F32)],
        in_specs=[VMEM_SPEC] + [ANY_SPEC] * n,
        out_specs=[VMEM_SPEC] * (n + 1),
        scratch_shapes=[pltpu.VMEM(q, F32) for q in quarter] + [pltpu.VMEM(q, F32) for q in quarter]
        + [pltpu.VMEM(q, BF16) for q in quarter] + [pltpu.VMEM((3,) + q[1:], BF16) for q in quarter]
        + [dma((7,)), dma((7,)), dma((n,)), dma((n,)), dma((3 * n,)), dma((3 * n,)), dma((n,)), dma((n,)), dma((n,))],
        compiler_params=pltpu.CompilerParams(vmem_limit_bytes=VMEM_LIMIT),
    )(part, *grads)


def _rope_tables(pos_col, inv_row):
    ang = pos_col * inv_row
    return jnp.cos(ang), jnp.sin(ang)


def _pre_call(x, pos_col, mod, b_ada, ng, qg, kvg, inv128, wa, wkr2, wq2, wkv, seq):
    n_tok = x.shape[0]
    tm = min(TOKEN_TILE, seq)
    per_seq = seq // tm

    def body(x_ref, pos_ref, mod_ref, bada_ref, ng_ref, qg_ref, kvg_ref, inv_ref, wa_ref, wkr_ref, wq_ref, wkv_ref,
             zqkv_ref, gates_ref, qf_ref, kf_ref, v_ref, qs_ref, kd_ref, vd_ref, rope_ref):
        xv = x_ref[...]
        modv = mod_ref[0] + bada_ref[...]
        shift, scale = modv[:, :D_MODEL], modv[:, D_MODEL:2 * D_MODEL]
        r1 = lax.rsqrt(jnp.mean(xv * xv, axis=-1, keepdims=True) + EPS)
        h = ((xv * r1) * ng_ref[...]) * (1.0 + scale) + shift
        hb = h.astype(BF16)
        za = _dot(hb, wa_ref[...])
        zkr = _dot(hb, wkr_ref[...])
        cos, sin = _rope_tables(pos_ref[...], inv_ref[...])
        rope_ref[:, :HEAD_LANES] = cos
        rope_ref[:, HEAD_LANES:] = sin
        zqkv_ref[...] = za[:, :A_GM]
        gates_ref[:, :512] = za[:, A_GM:A_QS]
        gates_ref[:, 512:] = za[:, A_GS:A_END]
        qs_ref[...] = (za[:, A_QS:A_KD] * (SWA_SCALE * LOG2E)).astype(BF16)
        kd_ref[...] = za[:, A_KD:A_VD].astype(BF16)
        vd_ref[...] = za[:, A_VD:A_GS].astype(BF16)
        zq, zkv = za[:, A_ZQ:A_ZKV], za[:, A_ZKV:A_GM]
        rq = lax.rsqrt(jnp.mean(zq * zq, axis=-1, keepdims=True) + EPS)
        qn = ((zq * rq) * qg_ref[...]).astype(BF16)
        qr = _dot(qn, wq_ref[...])
        cf, sf = jnp.tile(cos, (1, N_HEADS)), jnp.tile(sin, (1, N_HEADS))
        qf_ref[...] = ((qr[:, :1024] * cf + qr[:, 1024:] * sf) * (MLA_SCALE * LOG2E)).astype(BF16)
        rkv = lax.rsqrt(jnp.mean(zkv * zkv, axis=-1, keepdims=True) + EPS)
        kvn = ((zkv * rkv) * kvg_ref[...]).astype(BF16)
        kv = _dot(kvn, wkv_ref[...])
        kpe = zkr[:, :128] * cos + zkr[:, 128:] * sin
        kf_ref[...] = (kv[:, :1024] + jnp.tile(kpe, (1, N_HEADS))).astype(BF16)
        v_ref[...] = kv[:, 1024:].astype(BF16)

    tok = lambda w: pl.BlockSpec((tm, w), lambda i: (i, 0))
    outs = [(640, F32), (1024, F32), (1024, BF16), (1024, BF16), (512, BF16), (512, BF16), (256, BF16), (256, BF16),
            (2 * HEAD_LANES, F32)]
    return pl.pallas_call(
        body, name="pre", grid=(n_tok // tm,),
        out_shape=[jax.ShapeDtypeStruct((n_tok, w), dt) for w, dt in outs],
        in_specs=[tok(D_MODEL), tok(1), pl.BlockSpec((1, 1, 3 * D_MODEL), lambda i: (i // per_seq, 0, 0)),
                  _full(b_ada.shape), _full(ng.shape), _full(qg.shape), _full(kvg.shape), _full(inv128.shape),
                  _full(wa.shape), _full(wkr2.shape), _full(wq2.shape), _full(wkv.shape)],
        out_specs=[tok(w) for w, _ in outs],
        compiler_params=_params(1),
    )(x, pos_col, mod, b_ada, ng, qg, kvg, inv128, wa, wkr2, wq2, wkv)


def _lane_lo(width=HEAD_LANES):
    return lax.broadcasted_iota(jnp.int32, (1, width), 1) < HALF


def _eye(n=HEAD_LANES):
    r = lax.broadcasted_iota(jnp.int32, (n, n), 0)
    c = lax.broadcasted_iota(jnp.int32, (n, n), 1)
    return jnp.where(r == c, 1.0, 0.0).astype(BF16)


def _mla_fwd_call(qf, kf, v, n_seq, seq):
    tq = min(ATT_TILE, seq)
    nq = seq // tq

    ext = HALF + 16

    def body(q_ref, k_ref, v_ref, o_ref, lse_ref, vt_ref):
        i = pl.program_id(1)
        eye = _eye()

        @pl.when(i == 0)
        def _():
            for h in range(N_HEADS):
                vt_ref[h * ext + HALF:(h + 1) * ext, :] = jnp.ones((16, seq), BF16)
            for t in range(nq):
                for p in range(N_HEADS // 2):
                    pair = slice(p * HEAD_LANES, (p + 1) * HEAD_LANES)
                    v_t = _dot_nt(eye, v_ref[t * tq:(t + 1) * tq, pair]).astype(BF16)
                    for hh in range(2):
                        r0 = (2 * p + hh) * ext
                        vt_ref[r0:r0 + HALF, t * tq:(t + 1) * tq] = v_t[hh * HALF:(hh + 1) * HALF, :]

        q = q_ref[...]
        qcol = i * tq + lax.broadcasted_iota(jnp.int32, (1, tq), 1)
        heads = range(N_HEADS)
        lanes = [slice(h * HEAD_LANES, (h + 1) * HEAD_LANES) for h in heads]

        def make_step(masked, n_tiles):
            def step(kt0, carry):
                tiles = range(n_tiles)
                start = pl.multiple_of(kt0 * tq, tq)
                ks = [k_ref[pl.ds(pl.multiple_of((kt0 + t) * tq, tq), tq), :] for t in tiles]
                vt = vt_ref[:, pl.ds(start, n_tiles * tq)]
                sts = [[_dot_nt(ks[t][:, lanes[h]], q[:, lanes[h]]) for h in heads] for t in tiles]
                if masked:
                    last = n_tiles - 1
                    keep = ((kt0 + last) * tq + lax.broadcasted_iota(jnp.int32, (tq, 1), 0)) <= qcol
                    sts[last] = [jnp.where(keep, st, NEG) for st in sts[last]]
                stats, pts = [], []
                for h in heads:
                    m_old = carry[2 * h]
                    m_new = m_old
                    for t in tiles:
                        m_new = jnp.maximum(m_new, jnp.max(sts[t][h], axis=0, keepdims=True))
                    pts.append(jnp.concatenate([jnp.exp2(sts[t][h] - m_new).astype(BF16) for t in tiles], axis=0))
                    stats.append((m_new, jnp.exp2(m_old - m_new)))
                pvs = [_dot(vt[h * ext:(h + 1) * ext, :], pts[h]) for h in heads]
                out = []
                for h in heads:
                    out += [stats[h][0], carry[2 * h + 1] * stats[h][1] + pvs[h]]
                return tuple(out)
            return step

        init = (jnp.full((1, tq), NEG, F32), jnp.zeros((ext, tq), F32)) * N_HEADS
        count = i + 1
        carry = lax.fori_loop(0, (count + 1) // 2 - 1, lambda j, c: make_step(False, 2)(2 * j, c), init)
        carry = lax.cond(count % 2 == 0, lambda c: make_step(True, 2)(i - 1, c), lambda c: make_step(True, 1)(i, c), carry)
        dens = [carry[2 * h + 1][HALF:HALF + 1, :] for h in heads]
        acc_t = jnp.concatenate([carry[2 * h + 1][:HALF, :] * (1.0 / dens[h]) for h in heads], axis=0)
        o_ref[...] = acc_t.T
        for h in heads:
            lse_ref[0, h // 4, h % 4:h % 4 + 1, :] = carry[2 * h] + jnp.log2(dens[h])

    n_tok = qf.shape[0]
    return pl.pallas_call(
        body, name="mla_fwd", grid=(n_seq, nq),
        out_shape=[jax.ShapeDtypeStruct((n_tok, 512), F32), jax.ShapeDtypeStruct((n_seq, 2, 4, seq), F32)],
        in_specs=[pl.BlockSpec((tq, 1024), lambda b, i: (b * nq + i, 0)),
                  pl.BlockSpec((seq, 1024), lambda b, i: (b, 0)),
                  pl.BlockSpec((seq, 512), lambda b, i: (b, 0))],
        out_specs=[pl.BlockSpec((tq, 512), lambda b, i: (b * nq + i, 0)),
                   pl.BlockSpec((1, 2, 4, tq), lambda b, i: (b, 0, 0, i))],
        scratch_shapes=[pltpu.VMEM((N_HEADS * ext, seq), BF16)],
        compiler_params=_params(2),
    )(qf, kf, v)


def _mla_bwd_call(qf, kf, v, do, o, lse, n_seq, seq):
    tq = min(ATT_TILE, seq)
    nq = seq // tq

    nh = 4
    heads = range(nh)
    lanes = [slice(h * HEAD_LANES, (h + 1) * HEAD_LANES) for h in heads]

    def body(q_ref, k_ref, v_ref, do_ref, o_ref, lse_ref, dq_ref, dk_ref, dv_ref,
             kt_ref, dot_ref, delta_ref, dqt_ref):
        eye = _eye()
        lo = _lane_lo()
        sub_lo = lax.broadcasted_iota(jnp.int32, (HEAD_LANES, 1), 0) < HALF
        ones_lo = jnp.where(jnp.broadcast_to(lo, (8, HEAD_LANES)), 1.0, 0.0).astype(BF16)
        ones_hi = jnp.where(jnp.broadcast_to(lo, (8, HEAD_LANES)), 0.0, 1.0).astype(BF16)

        for t in range(nq):
            r = slice(t * tq, (t + 1) * tq)
            kv = k_ref[r, :]
            for h in heads:
                kt_ref[lanes[h], r] = _dot_nt(eye, kv[:, lanes[h]]).astype(BF16)
            for p in range(nh // 2):
                dov = do_ref[r, lanes[p]]
                dt = _dot_nt(eye, dov)
                dot_ref[2 * p, :, r] = jnp.where(sub_lo, dt, 0.0).astype(BF16)
                dot_ref[2 * p + 1, :, r] = jnp.where(sub_lo, 0.0, dt).astype(BF16)
                prod = dov.astype(F32) * o_ref[r, lanes[p]]
                p_hi = prod.astype(BF16)
                p_lo = (prod - p_hi.astype(F32)).astype(BF16)
                delta_ref[2 * p, :, r] = _dot_nt(ones_lo, p_hi) + _dot_nt(ones_lo, p_lo)
                delta_ref[2 * p + 1, :, r] = _dot_nt(ones_hi, p_hi) + _dot_nt(ones_hi, p_lo)
        dqt_ref[...] = jnp.zeros_like(dqt_ref)

        def k_step(kt, _):
            kr = pl.ds(pl.multiple_of(kt * tq, tq), tq)
            k = k_ref[kr, :]
            vv = v_ref[kr, :]
            k_t = kt_ref[:, kr]
            krow = kt * tq + lax.broadcasted_iota(jnp.int32, (tq, 1), 0)

            def make_step(masked, n_tiles):
                def q_step(qt0, carry):
                    tiles = range(n_tiles)
                    qrs = [pl.ds(pl.multiple_of((qt0 + t) * tq, tq), tq) for t in tiles]
                    qs = [q_ref[qr, :] for qr in qrs]
                    do_ts = [[dot_ref[h, :, qr] for h in heads] for qr in qrs]
                    sts = [[_dot_nt(k[:, lanes[h]], qs[t][:, lanes[h]]) for h in heads] for t in tiles]
                    dpts = [[_dot(vv[:, lanes[h // 2]], do_ts[t][h]) for h in heads] for t in tiles]
                    if masked:
                        keep = krow <= (qt0 * tq + lax.broadcasted_iota(jnp.int32, (1, tq), 1))
                    pts, dsts = [], []
                    for t in tiles:
                        pts.append([])
                        dsts.append([])
                        for h in heads:
                            pt = jnp.exp2(sts[t][h] - lse_ref[0, 0, h:h + 1, qrs[t]])
                            if masked and t == 0:
                                pt = jnp.where(keep, pt, 0.0)
                            dsts[t].append((pt * (dpts[t][h] - delta_ref[h, 0:1, qrs[t]])).astype(BF16))
                            pts[t].append(pt.astype(BF16))
                    out = []
                    for h in heads:
                        hh = h % 2
                        half = slice(hh * HALF, (hh + 1) * HALF)
                        dst_all = jnp.concatenate([dsts[t][h] for t in tiles], axis=1)
                        pt_all = jnp.concatenate([pts[t][h] for t in tiles], axis=1)
                        do_all = jnp.concatenate([do_ts[t][h][half, :] for t in tiles], axis=1)
                        q_all = jnp.concatenate([qs[t][:, lanes[h]] for t in tiles], axis=0)
                        dvt = _dot_nt(do_all, pt_all)
                        dk = _dot(dst_all, q_all)
                        for t in tiles:
                            dqt_ref[lanes[h], qrs[t]] += _dot(k_t[lanes[h], :], dsts[t][h])
                        out += [carry[2 * h] + dk, carry[2 * h + 1] + dvt]
                    return tuple(out)
                return q_step

            init = (jnp.zeros((tq, HEAD_LANES), F32), jnp.zeros((HALF, tq), F32)) * nh
            count = nq - kt
            carry = lax.cond(count >= 2, lambda c: make_step(True, 2)(kt, c), lambda c: make_step(True, 1)(kt, c), init)
            carry = lax.fori_loop(1, count // 2, lambda j, c: make_step(False, 2)(kt + 2 * j, c), carry)
            carry = lax.cond(jnp.logical_and(count % 2 == 1, count >= 3),
                             lambda c: make_step(False, 1)(nq - 1, c), lambda c: c, carry)
            for h in heads:
                dk_ref[kr, lanes[h]] = carry[2 * h]
            for p in range(nh // 2):
                dv_ref[kr, lanes[p]] = jnp.concatenate([carry[4 * p + 1], carry[4 * p + 3]], axis=0).T
            return 0

        lax.fori_loop(0, nq, k_step, 0)
        for t in range(nq):
            r = slice(t * tq, (t + 1) * tq)
            for h in heads:
                dq_ref[r, lanes[h]] = dqt_ref[lanes[h], r].T

    n_tok = qf.shape[0]
    groups = N_HEADS // nh
    blk = lambda w: pl.BlockSpec((seq, w), lambda b, g: (b, g))
    return pl.pallas_call(
        body, name="mla_bwd", grid=(n_seq, groups),
        out_shape=[jax.ShapeDtypeStruct((n_tok, 1024), F32), jax.ShapeDtypeStruct((n_tok, 1024), F32),
                   jax.ShapeDtypeStruct((n_tok, 512), F32)],
        in_specs=[blk(512), blk(512), blk(256), blk(256), blk(256),
                  pl.BlockSpec((1, 1, nh, seq), lambda b, g: (b, g, 0, 0))],
        out_specs=[blk(512), blk(512), blk(256)],
        scratch_shapes=[pltpu.VMEM((nh * HEAD_LANES, seq), BF16), pltpu.VMEM((nh, HEAD_LANES, seq), BF16),
                        pltpu.VMEM((nh, 8, seq), F32), pltpu.VMEM((nh * HEAD_LANES, seq), F32)],
        compiler_params=_params(2),
    )(qf, kf, v, do, o, lse)


SWA_BLOCKS = 4


def _swa_block(n, pos_col_ref, posq):
    w = SWA_WINDOW
    start = pl.multiple_of(jnp.maximum(n - 1, 0) * w, w)
    posk = pos_col_ref[pl.ds(start, 2 * w), :]
    rel = (n * w + lax.broadcasted_iota(jnp.int32, (1, w), 1)) - (start + lax.broadcasted_iota(jnp.int32, (2 * w, 1), 0))
    valid = jnp.logical_and(rel >= 0, rel < w)
    return start, jnp.where(valid, posq - posk, 1e30)


def _alibi(h):
    return LOG2E * 2.0 ** -(h + 1)


def _transpose_rows(eye, src_ref, dst_ref, seq, width):
    step = 2 * SWA_WINDOW
    for t in range(seq // step):
        for p in range(width // HEAD_LANES):
            lanes = slice(p * HEAD_LANES, (p + 1) * HEAD_LANES)
            dst_ref[lanes, t * step:(t + 1) * step] = _dot_nt(eye, src_ref[t * step:(t + 1) * step, lanes]).astype(BF16)


def _swa_fwd_call(qs, kd, vd, pos_col, pos_row, sinks, n_seq, seq):
    w = SWA_WINDOW
    qb = SWA_BLOCKS
    steps = seq // (qb * w)
    ext = HALF + 16

    def body(q_ref, k_ref, v_ref, pc_ref, pr_ref, sink_ref, o_ref, lse_ref, vt_ref):
        n = pl.program_id(1)
        lo = _lane_lo()
        hi = jnp.logical_not(lo)
        eye = _eye()

        @pl.when(n == 0)
        def _():
            step = 2 * w
            for kv in range(2):
                vt_ref[kv * ext + HALF:(kv + 1) * ext, :] = jnp.ones((16, seq), BF16)
                for t in range(seq // step):
                    v_t = _dot_nt(eye, v_ref[t * step:(t + 1) * step, kv * HEAD_LANES:(kv + 1) * HEAD_LANES])
                    vt_ref[kv * ext:kv * ext + HALF, t * step:(t + 1) * step] = v_t[:HALF, :].astype(BF16)

        heads = range(N_HEADS)
        blocks = range(qb)
        geo = [_swa_block(n * qb + bi, pc_ref, pr_ref[bi]) for bi in blocks]
        wins = [pl.ds(g[0], 2 * w) for g in geo]
        kwins = [k_ref[win, :] for win in wins]
        vts = [vt_ref[:, win] for win in wins]
        sts = []
        for bi in blocks:
            q = q_ref[bi * w:(bi + 1) * w, :]
            sts.append([])
            for h in heads:
                qp = q[:, (h // 2) * HEAD_LANES:(h // 2 + 1) * HEAD_LANES]
                qh = jnp.where(lo if h % 2 == 0 else hi, qp, jnp.zeros_like(qp))
                sts[bi].append(_dot_nt(kwins[bi][:, (h // 4) * HEAD_LANES:(h // 4 + 1) * HEAD_LANES], qh))
        ps, ms = [], []
        for bi in blocks:
            ps.append([])
            ms.append([])
            for h in heads:
                s = sts[bi][h] - _alibi(h) * geo[bi][1]
                m = jnp.maximum(jnp.max(s, axis=0, keepdims=True), sink_ref[0, h] * LOG2E)
                ps[bi].append(jnp.exp2(s - m).astype(BF16))
                ms[bi].append(m)
        for bi in blocks:
            ots = []
            for h in heads:
                pv = _dot(vts[bi][(h // 4) * ext:(h // 4 + 1) * ext, :], ps[bi][h])
                l = pv[HALF:HALF + 1, :] + jnp.exp2(sink_ref[0, h] * LOG2E - ms[bi][h])
                ots.append(pv[:HALF, :] * (1.0 / l))
                lse_ref[0, h:h + 1, bi * w:(bi + 1) * w] = ms[bi][h] + jnp.log2(l)
            o_ref[bi * w:(bi + 1) * w, :] = jnp.concatenate(ots, axis=0).T

    n_tok = qs.shape[0]
    tok = lambda width: pl.BlockSpec((qb * w, width), lambda b, n: (b * steps + n, 0))
    whole = lambda width: pl.BlockSpec((seq, width), lambda b, n: (b, 0))
    return pl.pallas_call(
        body, name="swa_fwd", grid=(n_seq, steps),
        out_shape=[jax.ShapeDtypeStruct((n_tok, 512), F32), jax.ShapeDtypeStruct((n_seq, N_HEADS, seq), F32)],
        in_specs=[tok(512), whole(256), whole(256), whole(1), pl.BlockSpec((qb, 1, w), lambda b, n: (b * steps + n, 0, 0)),
                  pl.BlockSpec(memory_space=pltpu.SMEM)],
        out_specs=[tok(512), pl.BlockSpec((1, N_HEADS, qb * w), lambda b, n: (b, 0, n))],
        scratch_shapes=[pltpu.VMEM((2 * ext, seq), BF16)],
        compiler_params=_params(2),
    )(qs, kd, vd, pos_col, pos_row, sinks)


def _swa_bwd_call(qs, kd, vd, do, o, lse, pos_col, pos_row, sinks, n_seq, seq):
    w = SWA_WINDOW
    qb = SWA_BLOCKS
    steps = seq // (qb * w)

    def body(q_ref, k_ref, v_ref, do_ref, o_ref, lse_ref, pc_ref, pr_ref, sink_ref, dq_ref, dk_ref, dv_ref, dsink_ref,
             kt_ref):
        b, n = pl.program_id(0), pl.program_id(1)
        lo = _lane_lo()
        hi = jnp.logical_not(lo)
        sub_lo = lax.broadcasted_iota(jnp.int32, (HEAD_LANES, 1), 0) < HALF
        eye = _eye()
        ones_lo = jnp.where(jnp.broadcast_to(lo, (8, HEAD_LANES)), 1.0, 0.0).astype(BF16)
        ones_hi = jnp.where(jnp.broadcast_to(lo, (8, HEAD_LANES)), 0.0, 1.0).astype(BF16)

        @pl.when(n == 0)
        def _():
            dk_ref[...] = jnp.zeros_like(dk_ref)
            dv_ref[...] = jnp.zeros_like(dv_ref)
            _transpose_rows(eye, k_ref, kt_ref, seq, 2 * HEAD_LANES)

        @pl.when(jnp.logical_and(n == 0, b == 0))
        def _():
            dsink_ref[...] = jnp.zeros_like(dsink_ref)

        heads = range(N_HEADS)
        blocks = range(qb)
        kv_lanes = lambda h: slice((h // 4) * HEAD_LANES, (h // 4 + 1) * HEAD_LANES)
        geo = [_swa_block(n * qb + bi, pc_ref, pr_ref[bi]) for bi in blocks]
        wins = [pl.ds(g[0], 2 * w) for g in geo]
        kwins = [k_ref[win, :] for win in wins]
        vwins = [v_ref[win, :] for win in wins]

        do_ts, deltas, qms, doms = [], [], [], []
        for bi in blocks:
            rows = slice(bi * w, (bi + 1) * w)
            for lst in (do_ts, deltas, qms, doms):
                lst.append([])
            for j in range(N_HEADS // 2):
                pair = slice(j * HEAD_LANES, (j + 1) * HEAD_LANES)
                dop = do_ref[rows, pair]
                qp = q_ref[rows, pair]
                dt = _dot_nt(eye, dop)
                prod = dop.astype(F32) * o_ref[rows, pair]
                p_hi = prod.astype(BF16)
                p_lo = (prod - p_hi.astype(F32)).astype(BF16)
                for hh in range(2):
                    half, ones = (lo, ones_lo) if hh == 0 else (hi, ones_hi)
                    do_ts[bi].append(jnp.where(sub_lo, dt, 0.0).astype(BF16) if hh == 0
                                     else jnp.where(sub_lo, 0.0, dt).astype(BF16))
                    deltas[bi].append((_dot_nt(ones, p_hi) + _dot_nt(ones, p_lo))[0:1, :])
                    qms[bi].append(jnp.where(half, qp, jnp.zeros_like(qp)))
                    doms[bi].append(jnp.where(half, dop, jnp.zeros_like(dop)))
        sts = [[_dot_nt(kwins[bi][:, kv_lanes(h)], qms[bi][h]) for h in heads] for bi in blocks]
        dpts = [[_dot(vwins[bi][:, kv_lanes(h)], do_ts[bi][h]) for h in heads] for bi in blocks]
        pts, dsts = [], []
        for bi in blocks:
            pts.append([])
            dsts.append([])
            for h in heads:
                lse_h = lse_ref[0, h:h + 1, bi * w:(bi + 1) * w]
                pt = jnp.exp2(sts[bi][h] - _alibi(h) * geo[bi][1] - lse_h)
                dsts[bi].append((pt * (dpts[bi][h] - deltas[bi][h])).astype(BF16))
                pts[bi].append(pt.astype(BF16))
                dsink_ref[h:h + 1, :] += -jnp.exp2(sink_ref[0, h] * LOG2E - lse_h) * deltas[bi][h]
        for bi in blocks:
            for kv in range(2):
                group = range(4 * kv, 4 * kv + 4)
                dst_all = jnp.concatenate([dsts[bi][h] for h in group], axis=1)
                pt_all = jnp.concatenate([pts[bi][h] for h in group], axis=1)
                q_all = jnp.concatenate([qms[bi][h] for h in group], axis=0)
                do_all = jnp.concatenate([doms[bi][h] for h in group], axis=0)
                dk_ref[wins[bi], kv_lanes(4 * kv)] += _dot(dst_all, q_all)
                dv_ref[wins[bi], kv_lanes(4 * kv)] += _dot(pt_all, do_all)
        for bi in blocks:
            ktw = kt_ref[:, wins[bi]]
            for j in range(N_HEADS // 2):
                k_t = ktw[kv_lanes(2 * j), :]
                dq_t = jnp.where(sub_lo, _dot(k_t, dsts[bi][2 * j]), _dot(k_t, dsts[bi][2 * j + 1]))
                dq_ref[bi * w:(bi + 1) * w, j * HEAD_LANES:(j + 1) * HEAD_LANES] = dq_t.T * SWA_SCALE

    n_tok = qs.shape[0]
    tok = lambda width: pl.BlockSpec((qb * w, width), lambda b, n: (b * steps + n, 0))
    whole = lambda width: pl.BlockSpec((seq, width), lambda b, n: (b, 0))
    return pl.pallas_call(
        body, name="swa_bwd", grid=(n_seq, steps),
        out_shape=[jax.ShapeDtypeStruct((n_tok, 512), F32), jax.ShapeDtypeStruct((n_tok, 256), F32),
                   jax.ShapeDtypeStruct((n_tok, 256), F32), jax.ShapeDtypeStruct((N_HEADS, HEAD_LANES), F32)],
        in_specs=[tok(512), whole(256), whole(256), pl.BlockSpec((qb * w, 512), lambda b, n: (b * steps + n, 1)), tok(512),
                  pl.BlockSpec((1, N_HEADS, qb * w), lambda b, n: (b, 0, n)),
                  whole(1), pl.BlockSpec((qb, 1, w), lambda b, n: (b * steps + n, 0, 0)),
                  pl.BlockSpec(memory_space=pltpu.SMEM)],
        out_specs=[tok(512), whole(256), whole(256), _full((N_HEADS, HEAD_LANES))],
        scratch_shapes=[pltpu.VMEM((2 * HEAD_LANES, seq), BF16)],
        compiler_params=_params(2),
    )(qs, kd, vd, do, o, lse, pos_col, pos_row, sinks)


def _post_call(x, target, o_mla, o_swa, gates, mod, b_ada, fg, w_out, w_out_t, seq):
    n_tok = x.shape[0]
    tm = min(TOKEN_TILE, seq)
    per_seq = seq // tm
    n_seq = n_tok // seq

    def body(x_ref, t_ref, om_ref, os_ref, g_ref, mod_ref, bada_ref, fg_ref, w_ref, wt_ref,
             dx2_ref, do_ref, dg_ref, gw_ref, gfg_ref, dgate_ref, loss_ref):
        i = pl.program_id(0)

        @pl.when(i == 0)
        def _():
            gw_ref[...] = jnp.zeros_like(gw_ref)
            gfg_ref[...] = jnp.zeros_like(gfg_ref)
            loss_ref[...] = jnp.zeros_like(loss_ref)

        @pl.when(i % per_seq == 0)
        def _():
            dgate_ref[...] = jnp.zeros_like(dgate_ref)

        gate = mod_ref[0][:, 2 * D_MODEL:] + bada_ref[:, 2 * D_MODEL:]
        fgv = fg_ref[...]
        subs = _sub_tiles(tm)
        gs = [g_ref[r, :] for r in subs]
        os_ = [jnp.concatenate([om_ref[r, :], os_ref[r, :]], axis=-1) for r in subs]
        sgs = [_sigmoid(g) for g in gs]
        sils = [g * sg for g, sg in zip(gs, sgs)]
        ypres = [(o * sil).astype(BF16) for o, sil in zip(os_, sils)]
        ys = [_dot(ypre, w_ref[...]) for ypre in ypres]
        dys, loss, gfg, dgate = [], 0.0, 0.0, 0.0
        for r, y in zip(subs, ys):
            x2 = x_ref[r, :] + gate * y
            r2 = lax.rsqrt(jnp.mean(x2 * x2, axis=-1, keepdims=True) + EPS)
            xn2 = x2 * r2
            err = xn2 * fgv - t_ref[r, :]
            loss = loss + jnp.sum(jnp.sum(err * err, axis=-1, keepdims=True), axis=0, keepdims=True)
            dout = err * (1.0 / D_MODEL)
            gfg = gfg + jnp.sum(dout * xn2, axis=0, keepdims=True)
            dxn2 = dout * fgv
            dx2 = r2 * (dxn2 - xn2 * jnp.mean(dxn2 * xn2, axis=-1, keepdims=True))
            dx2_ref[r, :] = dx2
            dgate = dgate + jnp.sum(dx2 * y, axis=0, keepdims=True)
            dys.append((dx2 * gate).astype(BF16))
        loss_ref[...] += jnp.broadcast_to(loss * (0.5 / D_MODEL), loss_ref.shape)
        gfg_ref[...] += gfg
        dgate_ref[0] += dgate
        gw_ref[...] += _dot_tn(jnp.concatenate(ypres, axis=0), jnp.concatenate(dys, axis=0))
        dypres = [_dot(dy, wt_ref[...]) for dy in dys]
        for r, dypre, o, g, sg, sil in zip(subs, dypres, os_, gs, sgs, sils):
            do_ref[r, :] = (dypre * sil).astype(BF16)
            dg_ref[r, :] = (dypre * o * (sg * (1.0 + g * (1.0 - sg)))).astype(BF16)

    tok = lambda w: pl.BlockSpec((tm, w), lambda i: (i, 0))
    per_b = pl.BlockSpec((1, 1, 3 * D_MODEL), lambda i: (i // per_seq, 0, 0))
    return pl.pallas_call(
        body, name="post", grid=(n_tok // tm,),
        out_shape=[jax.ShapeDtypeStruct((n_tok, D_MODEL), F32), jax.ShapeDtypeStruct((n_tok, D_MODEL), BF16),
                   jax.ShapeDtypeStruct((n_tok, D_MODEL), BF16), jax.ShapeDtypeStruct((D_MODEL, D_MODEL), F32),
                   jax.ShapeDtypeStruct((1, D_MODEL), F32), jax.ShapeDtypeStruct((n_seq, 1, D_MODEL), F32),
                   jax.ShapeDtypeStruct((1, HEAD_LANES), F32)],
        in_specs=[tok(D_MODEL), tok(D_MODEL), tok(512), tok(512), tok(D_MODEL), per_b, _full(b_ada.shape),
                  _full(fg.shape), _full(w_out.shape), _full(w_out_t.shape)],
        out_specs=[tok(D_MODEL), tok(D_MODEL), tok(D_MODEL), _full((D_MODEL, D_MODEL)), _full((1, D_MODEL)),
                   pl.BlockSpec((1, 1, D_MODEL), lambda i: (i // per_seq, 0, 0)), _full((1, HEAD_LANES))],
        compiler_params=_params(1),
    )(x, target, o_mla, o_swa, gates, mod, b_ada, fg, w_out, w_out_t)


def _mid_bwd_call(dqf, dkf, dv, zqkv, rope, qg, kvg, wq2, wkv, seq):
    n_tok = dqf.shape[0]
    tm = min(TOKEN_TILE, seq)

    def body(dq_ref, dk_ref, dv_ref, z_ref, rope_ref, qg_ref, kvg_ref, wq_ref, wkv_ref,
             dz_ref, dkr_ref, gwq_ref, gwkv_ref, gqg_ref, gkvg_ref):
        i = pl.program_id(0)

        @pl.when(i == 0)
        def _():
            gwq_ref[...] = jnp.zeros_like(gwq_ref)
            gwkv_ref[...] = jnp.zeros_like(gwkv_ref)
            gqg_ref[...] = jnp.zeros_like(gqg_ref)
            gkvg_ref[...] = jnp.zeros_like(gkvg_ref)

        cos, sin = rope_ref[:, :HEAD_LANES], rope_ref[:, HEAD_LANES:]
        cf, sf = jnp.tile(cos, (1, N_HEADS)), jnp.tile(sin, (1, N_HEADS))
        dq = dq_ref[...] * MLA_SCALE
        dqr = jnp.concatenate([dq * cf, dq * sf], axis=-1).astype(BF16)
        zq, zkv = z_ref[:, :Q_LORA], z_ref[:, Q_LORA:]
        qgv, kvgv = qg_ref[...], kvg_ref[...]

        rq = lax.rsqrt(jnp.mean(zq * zq, axis=-1, keepdims=True) + EPS)
        xq = zq * rq
        gwq_ref[...] += _dot_tn((xq * qgv).astype(BF16), dqr)
        dqn = _dot_nt(dqr, wq_ref[...])
        gqg_ref[...] += jnp.sum(dqn * xq, axis=0, keepdims=True)
        dxq = dqn * qgv
        dz_ref[:, :Q_LORA] = (rq * (dxq - xq * jnp.mean(dxq * xq, axis=-1, keepdims=True))).astype(BF16)

        dk = dk_ref[...] * LN2
        dkv = jnp.concatenate([dk, dv_ref[...]], axis=-1).astype(BF16)
        rkv = lax.rsqrt(jnp.mean(zkv * zkv, axis=-1, keepdims=True) + EPS)
        xkv = zkv * rkv
        gwkv_ref[...] += _dot_tn((xkv * kvgv).astype(BF16), dkv)
        dkvn = _dot_nt(dkv, wkv_ref[...])
        gkvg_ref[...] += jnp.sum(dkvn * xkv, axis=0, keepdims=True)
        dxkv = dkvn * kvgv
        dz_ref[:, Q_LORA:] = (rkv * (dxkv - xkv * jnp.mean(dxkv * xkv, axis=-1, keepdims=True))).astype(BF16)

        dkpe = dk[:, :HEAD_LANES]
        for h in range(1, N_HEADS):
            dkpe = dkpe + dk[:, h * HEAD_LANES:(h + 1) * HEAD_LANES]
        dkr_ref[:, :HEAD_LANES] = (dkpe * cos).astype(BF16)
        dkr_ref[:, HEAD_LANES:] = (dkpe * sin).astype(BF16)

    tok = lambda w: pl.BlockSpec((tm, w), lambda i: (i, 0))
    return pl.pallas_call(
        body, name="mid_bwd", grid=(n_tok // tm,),
        out_shape=[jax.ShapeDtypeStruct((n_tok, 640), BF16), jax.ShapeDtypeStruct((n_tok, 256), BF16),
                   jax.ShapeDtypeStruct(wq2.shape, F32), jax.ShapeDtypeStruct(wkv.shape, F32),
                   jax.ShapeDtypeStruct((1, Q_LORA), F32), jax.ShapeDtypeStruct((1, KV_LORA), F32)],
        in_specs=[tok(1024), tok(1024), tok(512), tok(640), tok(2 * HEAD_LANES), _full(qg.shape), _full(kvg.shape),
                  _full(wq2.shape), _full(wkv.shape)],
        out_specs=[tok(640), tok(256), _full(wq2.shape), _full(wkv.shape), _full((1, Q_LORA)), _full((1, KV_LORA))],
        compiler_params=_params(1),
    )(dqf, dkf, dv, zqkv, rope, qg, kvg, wq2, wkv)


def _in_bwd_call(x, dx2, dz, dkr, dg, dqs, dkd, dvd, mod, b_ada, ng, wa_t, wkr2_t, seq):
    n_tok = x.shape[0]
    tm = min(TOKEN_TILE, seq)
    per_seq = seq // tm
    n_seq = n_tok // seq

    def body(x_ref, dx2_ref, dz_ref, dkr_ref, dg_ref, dqs_ref, dkd_ref, dvd_ref, mod_ref, bada_ref, ng_ref,
             wat_ref, wkrt_ref, gx_ref, gwa_ref, gwkr_ref, gng_ref, dshift_ref, dscale_ref):
        i = pl.program_id(0)

        @pl.when(i == 0)
        def _():
            gwa_ref[...] = jnp.zeros_like(gwa_ref)
            gwkr_ref[...] = jnp.zeros_like(gwkr_ref)
            gng_ref[...] = jnp.zeros_like(gng_ref)

        @pl.when(i % per_seq == 0)
        def _():
            dshift_ref[...] = jnp.zeros_like(dshift_ref)
            dscale_ref[...] = jnp.zeros_like(dscale_ref)

        xv = x_ref[...]
        modv = mod_ref[0] + bada_ref[...]
        shift, scale = modv[:, :D_MODEL], modv[:, D_MODEL:2 * D_MODEL]
        ngv = ng_ref[...]
        r1 = lax.rsqrt(jnp.mean(xv * xv, axis=-1, keepdims=True) + EPS)
        xn = xv * r1
        hb = ((xn * ngv) * (1.0 + scale) + shift).astype(BF16)

        dgv = dg_ref[...]
        pieces = [(A_ZQ, dz_ref[...]), (A_GM, dgv[:, :512]), (A_QS, dqs_ref[...].astype(BF16)),
                  (A_KD, (dkd_ref[...] * LN2).astype(BF16)),
                  (A_VD, dvd_ref[...].astype(BF16)), (A_GS, dgv[:, 512:])]
        dkr = dkr_ref[...]
        gwkr_ref[...] += _dot_tn(hb, dkr)
        dh = _dot(dkr, wkrt_ref[...])
        for off, piece in pieces:
            wd = piece.shape[1]
            gwa_ref[:, off:off + wd] += _dot_tn(hb, piece)
            dh = dh + _dot(piece, wat_ref[off:off + wd, :])

        dshift_ref[0] += jnp.sum(dh, axis=0, keepdims=True)
        dscale_ref[0] += jnp.sum(dh * (xn * ngv), axis=0, keepdims=True)
        gng_ref[...] += jnp.sum(dh * xn * (1.0 + scale), axis=0, keepdims=True)
        dxn = dh * ngv * (1.0 + scale)
        gx_ref[...] = dx2_ref[...] + r1 * (dxn - xn * jnp.mean(dxn * xn, axis=-1, keepdims=True))

    tok = lambda w: pl.BlockSpec((tm, w), lambda i: (i, 0))
    per_b = lambda w: pl.BlockSpec((1, 1, w), lambda i: (i // per_seq, 0, 0))
    return pl.pallas_call(
        body, name="in_bwd", grid=(n_tok // tm,),
        out_shape=[jax.ShapeDtypeStruct((n_tok, D_MODEL), F32), jax.ShapeDtypeStruct((D_MODEL, A_END), F32),
                   jax.ShapeDtypeStruct((D_MODEL, 256), F32), jax.ShapeDtypeStruct((1, D_MODEL), F32),
                   jax.ShapeDtypeStruct((n_seq, 1, D_MODEL), F32), jax.ShapeDtypeStruct((n_seq, 1, D_MODEL), F32)],
        in_specs=[tok(D_MODEL), tok(D_MODEL), tok(640), tok(256), tok(D_MODEL), tok(512), tok(256), tok(256),
                  per_b(3 * D_MODEL), _full(b_ada.shape), _full(ng.shape), _full(wa_t.shape), _full(wkr2_t.shape)],
        out_specs=[tok(D_MODEL), _full((D_MODEL, A_END)), _full((D_MODEL, 256)), _full((1, D_MODEL)),
                   per_b(D_MODEL), per_b(D_MODEL)],
        compiler_params=_params(1),
    )(x, dx2, dz, dkr, dg, dqs, dkd, dvd, mod, b_ada, ng, wa_t, wkr2_t)


def _adam_math(w, g, m, v):
    m_new = ADAM_B1 * m + (1.0 - ADAM_B1) * g
    v_new = ADAM_B2 * v + (1.0 - ADAM_B2) * (g * g)
    m_hat = m_new / (1.0 - ADAM_B1 ** ADAM_STEP)
    v_hat = v_new / (1.0 - ADAM_B2 ** ADAM_STEP)
    delta = -ADAM_LR * (m_hat / (jnp.sqrt(v_hat) + ADAM_EPS) + ADAM_WD * w)
    return delta, m_new, v_new


def _adam_call(name, w, g, m, v):
    rows, cols = w.shape
    tr = 256 if rows % 256 == 0 else rows

    def body(w_ref, g_ref, m_ref, v_ref, d_ref, mo_ref, vo_ref):
        d, mn, vn = _adam_math(w_ref[...], g_ref[...], m_ref[...], v_ref[...])
        d_ref[...] = d
        mo_ref[...] = mn
        vo_ref[...] = vn

    spec = pl.BlockSpec((tr, cols), lambda i: (i, 0))
    return pl.pallas_call(
        body, name=name, grid=(rows // tr,),
        out_shape=[jax.ShapeDtypeStruct(w.shape, F32)] * 3,
        in_specs=[spec] * 4, out_specs=[spec] * 3,
        compiler_params=_params(1),
    )(w, g, m, v)


def _ada_bwd_call(act_all, dmod_cols, w, m, v):
    rows, cols = w.shape
    tr = 256

    def body(a_ref, dm_ref, w_ref, m_ref, v_ref, g_ref, d_ref, mo_ref, vo_ref):
        g = _dot_tn(a_ref[...].astype(BF16), dm_ref[...].astype(BF16))
        d, mn, vn = _adam_math(w_ref[...], g, m_ref[...], v_ref[...])
        g_ref[...] = g
        d_ref[...] = d
        mo_ref[...] = mn
        vo_ref[...] = vn

    spec = pl.BlockSpec((tr, cols), lambda i: (i, 0))
    nb = act_all.shape[0]
    return pl.pallas_call(
        body, name="ada_bwd", grid=(rows // tr,),
        out_shape=[jax.ShapeDtypeStruct(w.shape, F32)] * 4,
        in_specs=[pl.BlockSpec((nb, tr), lambda i: (0, i)), _full(dmod_cols.shape), spec, spec, spec],
        out_specs=[spec] * 4,
        compiler_params=_params(1),
    )(act_all, dmod_cols, w, m, v)


SMALL_ROW = {"norm_gain": (0, 1024), "final_gain": (1024, 2048), "q_norm_gain": (2048, 2432),
             "kv_norm_gain": (2432, 2688), "swa_sinks": (2688, 2696), "loss": (2816, 2944)}
SMALL_ORDER = ("b_ada", "norm_gain", "q_norm_gain", "kv_norm_gain", "swa_sinks", "final_gain")


def _small_call(parts_all, n_seq, params):
    k = len(params)

    def body(p_ref, *refs):
        ins, outs, loss_ref = refs[:3 * k], refs[3 * k:7 * k], refs[7 * k]
        row = p_ref[n_seq:n_seq + 1, :]
        for dv in range(1, 8):
            r0 = dv * ROWS_PER_DEVICE + n_seq
            row = row + p_ref[r0:r0 + 1, :]
        gb = None
        for dv in range(8):
            for r in range(n_seq):
                r0 = dv * ROWS_PER_DEVICE + r
                gb = p_ref[r0:r0 + 1, :] if gb is None else gb + p_ref[r0:r0 + 1, :]
        for j, name in enumerate(SMALL_ORDER):
            g = gb if name == "b_ada" else row[:, SMALL_ROW[name][0]:SMALL_ROW[name][1]]
            d, mn, vn = _adam_math(ins[3 * j][...], g, ins[3 * j + 1][...], ins[3 * j + 2][...])
            outs[4 * j][...] = g
            outs[4 * j + 1][...] = d
            outs[4 * j + 2][...] = mn
            outs[4 * j + 3][...] = vn
        loss_ref[...] = row[:, SMALL_ROW["loss"][0]:SMALL_ROW["loss"][1]]

    flat = [t for p in params for t in p]
    res = pl.pallas_call(
        body, name="small_update", grid=(1,),
        out_shape=[jax.ShapeDtypeStruct(p[0].shape, F32) for p in params for _ in range(4)]
        + [jax.ShapeDtypeStruct((1, HEAD_LANES), F32)],
        in_specs=[_full(parts_all.shape)] + [_full(t.shape) for t in flat],
        out_specs=[_full(p[0].shape) for p in params for _ in range(4)] + [_full((1, HEAD_LANES))],
        compiler_params=_params(1),
    )(parts_all, *flat)
    return [res[4 * j:4 * j + 4] for j in range(k)], res[4 * k]


def _rot(t):
    half = t.shape[-1] // 2
    return jnp.concatenate([-t[..., half:], t[..., :half]], axis=-1)


def _rot_t(g):
    half = g.shape[-1] // 2
    return jnp.concatenate([g[..., half:], -g[..., :half]], axis=-1)


def _columns(segments, lo, hi):
    out, at = [], 0
    for seg in segments:
        n = seg.shape[1]
        a, b = max(lo, at), min(hi, at + n)
        if a < b:
            out.append(seg[:, a - at:b - at])
        at += n
    return out


def _prepare_weights(w_in_blocks, w_uq, w_ukv):
    o = [0]
    for s in IN_SPLITS:
        o.append(o[-1] + s)
    part = lambda a, b: _columns(w_in_blocks, a, b)
    dup = lambda a: part(a, a + 64) * 2 + part(a + 64, a + 128) * 2
    wa = jnp.concatenate(part(0, o[2]) + part(o[3], o[5]) + dup(o[5]) + dup(o[6]) + part(o[7], o[8]), axis=1)
    kr = jnp.concatenate(part(o[2], o[3]), axis=1)
    zc = lambda n: jnp.zeros((kr.shape[0], n), kr.dtype)
    wkr2 = jnp.concatenate([zc(64), kr, zc(32), zc(64), _rot(kr), zc(32)], axis=1)
    uq = w_uq.reshape(Q_LORA, N_HEADS, MLA_NOPE + MLA_ROPE)
    zq = jnp.zeros((Q_LORA, N_HEADS, 32), w_uq.dtype)
    uq_full = jnp.concatenate([uq, zq], axis=-1).reshape(Q_LORA, 1024)
    uq_rot = jnp.concatenate([jnp.zeros((Q_LORA, N_HEADS, 64), w_uq.dtype), _rot(uq[..., MLA_NOPE:]), zq],
                             axis=-1).reshape(Q_LORA, 1024)
    wq2 = jnp.concatenate([uq_full, uq_rot], axis=1)
    ukv = w_ukv.reshape(KV_LORA, N_HEADS, 128)
    k_full = jnp.concatenate([ukv[..., :64], jnp.zeros((KV_LORA, N_HEADS, 64), w_ukv.dtype)], axis=-1).reshape(KV_LORA, 1024)
    wkv = jnp.concatenate([k_full, ukv[..., 64:].reshape(KV_LORA, 512)], axis=1)
    return wa, wkr2, wq2, wkv


def _restore_grads(gwa, gwkr2, gwq2, gwkv):
    fold = lambda g: jnp.concatenate([g[:, 0:64] + g[:, 64:128], g[:, 128:192] + g[:, 192:256]], axis=1)
    gkr = gwkr2[:, 64:96] + _rot_t(gwkr2[:, 192:224])
    in_order = [gwa[:, :A_GM], gkr, gwa[:, A_GM:A_KD], fold(gwa[:, A_KD:A_VD]), fold(gwa[:, A_VD:A_GS]), gwa[:, A_GS:]]
    n = D_IN // 4
    g_in = [jnp.concatenate(_columns(in_order, k * n, (k + 1) * n), axis=1) for k in range(4)]
    gf = gwq2[:, :1024].reshape(Q_LORA, N_HEADS, 128)
    gr = gwq2[:, 1024:].reshape(Q_LORA, N_HEADS, 128)
    g_uq = jnp.concatenate([gf[..., :64], gf[..., 64:96] + _rot_t(gr[..., 64:96])], axis=-1).reshape(Q_LORA, 768)
    gk = gwkv[:, :1024].reshape(KV_LORA, N_HEADS, 128)[..., :64]
    gv = gwkv[:, 1024:].reshape(KV_LORA, N_HEADS, 64)
    g_ukv = jnp.concatenate([gk, gv], axis=-1).reshape(KV_LORA, 1024)
    return g_in, g_uq, g_ukv


def _local_step(x, positions, target, mod_rows, b_ada, ng, qg, kvg, sinks, fg, w_in_b, w_uq_b, w_ukv_b, w_out_b):
    n_seq, seq, _ = x.shape
    n_tok = n_seq * seq
    x2d = x.reshape(n_tok, D_MODEL)
    t2d = target.reshape(n_tok, D_MODEL)
    pos_f = positions.astype(F32)
    pos_col = pos_f.reshape(n_tok, 1)
    pos_row = pos_f.reshape(n_tok // SWA_WINDOW, 1, SWA_WINDOW)
    mod3 = mod_rows.reshape(n_seq, 1, 3 * D_MODEL)
    inv = ROPE_THETA ** (-jnp.arange(0, MLA_ROPE, 2, dtype=F32) / MLA_ROPE)
    inv128 = jnp.concatenate([jnp.zeros((64,), F32), inv, inv, jnp.zeros((32,), F32)]).reshape(1, 128)
    fg2 = fg.reshape(1, D_MODEL)

    wa, wkr2, wq2, wkv = _prepare_weights(w_in_b, w_uq_b, w_ukv_b)

    zqkv, gates, qf, kf, v, qs, kd, vd, rope = _pre_call(x2d, pos_col, mod3, b_ada, ng, qg, kvg, inv128, wa, wkr2, wq2, wkv, seq)
    o_mla, lse_mla = _mla_fwd_call(qf, kf, v, n_seq, seq)
    o_swa, lse_swa = _swa_fwd_call(qs, kd, vd, pos_col, pos_row, sinks, n_seq, seq)
    dx2, do, dg, g_out, g_fg, dgate, loss = _post_call(x2d, t2d, o_mla, o_swa, gates, mod3, b_ada, fg2, w_out_b, w_out_b.T, seq)
    dqf, dkf, dv = _mla_bwd_call(qf, kf, v, do, o_mla, lse_mla, n_seq, seq)
    dqs, dkd, dvd, dsink = _swa_bwd_call(qs, kd, vd, do, o_swa, lse_swa, pos_col, pos_row, sinks, n_seq, seq)
    dz, dkr, g_wq2, g_wkv, g_qg, g_kvg = _mid_bwd_call(dqf, dkf, dv, zqkv, rope, qg, kvg, wq2, wkv, seq)
    gx, g_wa, g_wkr2, g_ng, dshift, dscale = _in_bwd_call(x2d, dx2, dz, dkr, dg, dqs, dkd, dvd, mod3, b_ada, ng,
                                                         wa.T, wkr2.T, seq)
    g_in, g_uq, g_ukv = _restore_grads(g_wa, g_wkr2, g_wq2, g_wkv)
    dmod = jnp.concatenate([dshift, dscale, dgate], axis=-1).reshape(n_seq, 3 * D_MODEL)
    small_row = jnp.concatenate([g_ng, g_fg, g_qg, g_kvg, jnp.pad(jnp.sum(dsink, axis=1).reshape(1, N_HEADS), ((0, 0), (0, 120))),
                                 loss, jnp.zeros((1, 128), F32)], axis=1)
    return gx.reshape(x.shape), (g_in, g_uq, g_ukv, g_out), small_row, dmod


def kernel(x, c, positions, w_ada, b_ada, norm_gain, w_in, q_norm_gain, kv_norm_gain, w_uq, w_ukv, swa_sinks, w_out, final_gain, loss_target, m_w_ada, m_b_ada, m_norm_gain, m_w_in, m_q_norm_gain, m_kv_norm_gain, m_w_uq, m_w_ukv, m_swa_sinks, m_w_out, m_final_gain, v_w_ada, v_b_ada, v_norm_gain, v_w_in, v_q_norm_gain, v_kv_norm_gain, v_w_uq, v_w_ukv, v_swa_sinks, v_w_out, v_final_gain):
    n_seq = x.shape[0]
    xi, yi, ci = lax.axis_index("x"), lax.axis_index("y"), lax.axis_index("c")
    dev = 4 * xi + 2 * yi + ci
    chip = 2 * xi + yi

    halves = lambda w: w.astype(BF16).reshape(2, w.shape[0] // 2, w.shape[1])
    c_blk = jnp.pad(c, ((0, ROWS_PER_DEVICE - n_seq), (0, 0)))
    act_all, pieces, f_in, f_uq, f_ukv, f_out = _comm_fwd_call(
        c_blk, w_ada[0], [halves(w_in[0]), halves(w_uq[0]), halves(w_ukv[0]), halves(w_out[0])])
    mine = lax.dynamic_slice_in_dim(pieces, dev * ROWS_PER_DEVICE, n_seq, axis=1)
    mod_rows = jnp.transpose(mine, (1, 0, 2)).reshape(n_seq, 3 * D_MODEL)
    cols = lambda t, r: jnp.transpose(t.reshape(4, r, -1), (1, 0, 2)).reshape(r, -1)
    w_in_blocks = [f_in[k].reshape(D_MODEL, -1) for k in range(4)]
    w_uq_b, w_ukv_b = cols(f_uq, Q_LORA), cols(f_ukv, KV_LORA)
    w_out_b = f_out.reshape(D_MODEL, D_MODEL)

    gx, (g_in_blocks, g_uq, g_ukv, g_out), small_row, dmod = _local_step(
        x, positions, loss_target, mod_rows, b_ada, norm_gain, q_norm_gain, kv_norm_gain, swa_sinks, final_gain,
        w_in_blocks, w_uq_b, w_ukv_b, w_out_b)

    by_owner = lambda g, n: jnp.transpose(g.reshape(g.shape[0], 4, n), (1, 0, 2)).reshape(4, 2, g.shape[0] // 2, n)
    grads = [jnp.stack(g_in_blocks).reshape(4, 2, D_MODEL // 2, -1), by_owner(g_uq, 192), by_owner(g_ukv, 256),
             g_out.reshape(4, 2, 128, D_MODEL)]
    part = jnp.concatenate([dmod, small_row, jnp.zeros((ROWS_PER_DEVICE - n_seq - 1, 3 * D_MODEL), F32)], axis=0)
    r_in, r_uq, r_ukv, r_out, parts_all = _comm_bwd_call(grads, part)
    g_in_s, g_uq_s = r_in.reshape(w_in.shape[1:]), r_uq.reshape(w_uq.shape[1:])
    g_ukv_s, g_out_s = r_ukv.reshape(w_ukv.shape[1:]), r_out.reshape(w_out.shape[1:])

    d_in, nm_in, nv_in = _adam_call("adam_w_in", w_in[0], g_in_s, m_w_in[0], v_w_in[0])
    d_uq, nm_uq, nv_uq = _adam_call("adam_w_uq", w_uq[0], g_uq_s, m_w_uq[0], v_w_uq[0])
    d_ukv, nm_ukv, nv_ukv = _adam_call("adam_w_ukv", w_ukv[0], g_ukv_s, m_w_ukv[0], v_w_ukv[0])
    d_out, nm_out, nv_out = _adam_call("adam_w_out", w_out[0], g_out_s, m_w_out[0], v_w_out[0])
    dmod_cols = lax.dynamic_slice_in_dim(parts_all, chip * 768, 768, axis=1)
    g_ada, d_ada, nm_ada, nv_ada = _ada_bwd_call(act_all, dmod_cols, w_ada[0], m_w_ada[0], v_w_ada[0])

    row = lambda t: t.reshape(1, -1)
    small = {"b_ada": (b_ada, m_b_ada, v_b_ada), "norm_gain": (norm_gain, m_norm_gain, v_norm_gain),
             "q_norm_gain": (q_norm_gain, m_q_norm_gain, v_q_norm_gain),
             "kv_norm_gain": (kv_norm_gain, m_kv_norm_gain, v_kv_norm_gain),
             "swa_sinks": (swa_sinks, m_swa_sinks, v_swa_sinks),
             "final_gain": (row(final_gain), row(m_final_gain), row(v_final_gain))}
    res, loss_row = _small_call(parts_all, n_seq, [small[name] for name in SMALL_ORDER])
    res = dict(zip(SMALL_ORDER, res))
    res["final_gain"] = [t.reshape(-1) for t in res["final_gain"]]
    e = lambda t: t[None]
    big = {"w_ada": (e(g_ada), e(d_ada), e(nm_ada), e(nv_ada)), "w_in": (e(g_in_s), e(d_in), e(nm_in), e(nv_in)),
           "w_uq": (e(g_uq_s), e(d_uq), e(nm_uq), e(nv_uq)), "w_ukv": (e(g_ukv_s), e(d_ukv), e(nm_ukv), e(nv_ukv)),
           "w_out": (e(g_out_s), e(d_out), e(nm_out), e(nv_out))}
    order = ("w_ada", "b_ada", "norm_gain", "w_in", "q_norm_gain", "kv_norm_gain", "w_uq", "w_ukv", "swa_sinks", "w_out",
             "final_gain")
    pick = lambda kind: [(big[n] if n in big else res[n])[kind] for n in order]
    return (loss_row[0, 0], gx, *pick(0), *pick(1), *pick(2), *pick(3))
```

```python
import functools

import jax
import jax.numpy as jnp
from jax import lax
from jax.experimental import pallas as pl
from jax.experimental.pallas import tpu as pltpu

F32 = jnp.float32
BF16 = jnp.bfloat16

D_MODEL = 1024
Q_LORA = 384
KV_LORA = 256
N_HEADS = 8
MLA_NOPE = 64
MLA_ROPE = 32
HEAD_LANES = 128
HALF = 64
SWA_WINDOW = 128
EPS = 1e-6
ROPE_THETA = 10000.0
MLA_SCALE = (MLA_NOPE + MLA_ROPE) ** -0.5
LOG2E = 1.4426950408889634
LN2 = 0.6931471805599453
SWA_SCALE = 64 ** -0.5
NEG = -1e30

ADAM_LR = 0.001
ADAM_B1 = 0.9
ADAM_B2 = 0.999
ADAM_EPS = 1e-08
ADAM_WD = 0.01
ADAM_STEP = 10

A_ZQ, A_ZKV, A_GM, A_QS, A_KD, A_VD, A_GS, A_END = 0, 384, 640, 1152, 1664, 1920, 2176, 2688
IN_SPLITS = (384, 256, 32, 512, 512, 128, 128, 512)
D_IN = sum(IN_SPLITS)

TOKEN_TILE = 512
ATT_TILE = 256
VMEM_LIMIT = 56 * 1024 * 1024


def _dot(a, b):
    return jnp.dot(a, b, preferred_element_type=F32)


def _dot_nt(a, b):
    return lax.dot_general(a, b, (((1,), (1,)), ((), ())), preferred_element_type=F32)


def _dot_tn(a, b):
    return lax.dot_general(a, b, (((0,), (0,)), ((), ())), preferred_element_type=F32)


def _params(n_grid):
    return pltpu.CompilerParams(dimension_semantics=("arbitrary",) * n_grid, vmem_limit_bytes=VMEM_LIMIT)


def _full(shape):
    nd = len(shape)
    return pl.BlockSpec(shape, lambda *_: (0,) * nd, pipeline_mode=pl.Buffered(1))


def _sigmoid(g):
    return 1.0 / (1.0 + jnp.exp(-g))


SUB_TILE = 256


def _sub_tiles(tm):
    sub = min(SUB_TILE, tm)
    return [slice(s * sub, (s + 1) * sub) for s in range(tm // sub)]


MESH = pl.DeviceIdType.MESH
ROWS_PER_DEVICE = 8
VMEM_SPEC = pl.BlockSpec(memory_space=pltpu.VMEM)
ANY_SPEC = pl.BlockSpec(memory_space=pl.ANY)


def _position():
    x, y, c = lax.axis_index("x"), lax.axis_index("y"), lax.axis_index("c")
    sibling = (x, y, 1 - c)
    others = [(1 - x, y, c), (x, 1 - y, c), (1 - x, 1 - y, c)]
    return (x, y, c), 4 * x + 2 * y + c, 2 * x + y, sibling, others


def _rows_of(dev):
    return pl.ds(pl.multiple_of(dev * ROWS_PER_DEVICE, ROWS_PER_DEVICE), ROWS_PER_DEVICE)


def _all_to_all_rows(block_ref, table_ref, dev, me, send_sems, recv_sems):
    x, y, c = me
    waits = []
    for k in range(1, 8):
        peer = (1 - x if k & 4 else x, 1 - y if k & 2 else y, 1 - c if k & 1 else c)
        pltpu.make_async_remote_copy(src_ref=block_ref, dst_ref=table_ref.at[_rows_of(dev)], send_sem=send_sems.at[k - 1],
                                     recv_sem=recv_sems.at[k - 1], device_id=peer, device_id_type=MESH).start()
        waits.append(pltpu.make_async_remote_copy(
            src_ref=block_ref, dst_ref=table_ref.at[_rows_of(jnp.bitwise_xor(dev, k))], send_sem=send_sems.at[k - 1],
            recv_sem=recv_sems.at[k - 1], device_id=peer, device_id_type=MESH))
    return waits


def _comm_fwd_call(c_blk, w_ada, shards):
    n = len(shards)

    def body(c_ref, wada_ref, *refs):
        w_refs, act_ref, pieces_ref, full_refs = refs[:n], refs[n], refs[n + 1], refs[n + 2:2 * n + 2]
        c_all_ref = refs[2 * n + 2]
        c_send, c_recv, p_send, p_recv, w_send, w_recv, f_send, f_recv, loc_sem = refs[2 * n + 3:]
        me, dev, chip, sibling, others = _position()
        core = me[2]
        chip_of = [2 * p[0] + p[1] for p in others]

        local = [pltpu.make_async_copy(w_refs[i], full_refs[i].at[chip], loc_sem.at[i]) for i in range(n)]
        for cp in local:
            cp.start()

        def over_ici(i, j, src_chip):
            return pltpu.make_async_remote_copy(
                src_ref=w_refs[i].at[core], dst_ref=full_refs[i].at[src_chip, core], send_sem=w_send.at[3 * i + j],
                recv_sem=w_recv.at[3 * i + j], device_id=others[j], device_id_type=MESH)

        def to_sibling(i, j, half):
            return pltpu.make_async_remote_copy(
                src_ref=full_refs[i].at[chip_of[j], half], dst_ref=full_refs[i].at[chip_of[j], half],
                send_sem=f_send.at[3 * i + j], recv_sem=f_recv.at[3 * i + j], device_id=sibling, device_id_type=MESH)

        sent = [over_ici(i, j, chip) for i in range(n) for j in range(3)]
        for cp in sent:
            cp.start()

        c_all_ref[_rows_of(dev), :] = c_ref[...]
        c_waits = _all_to_all_rows(c_ref, c_all_ref, dev, me, c_send, c_recv)
        for cp in c_waits:
            cp.wait()
        cv = c_all_ref[...]
        act = cv * _sigmoid(cv)
        act_ref[...] = act
        pieces_ref[chip] = _dot(act.astype(BF16), wada_ref[...].astype(BF16))
        piece = lambda j, src_chip: pltpu.make_async_remote_copy(
            src_ref=pieces_ref.at[chip], dst_ref=pieces_ref.at[src_chip], send_sem=p_send.at[j], recv_sem=p_recv.at[j],
            device_id=others[j], device_id_type=MESH)
        for j in range(3):
            piece(j, chip).start()
        for j in range(3):
            piece(j, chip).wait_send()
            piece(j, chip_of[j]).wait_recv()

        for i in range(n):
            for j in range(3):
                over_ici(i, j, chip_of[j]).wait_recv()
                to_sibling(i, j, core).start()
        for i in range(n):
            for j in range(3):
                to_sibling(i, j, 1 - core).wait_recv()
                to_sibling(i, j, core).wait_send()
        for cp in sent:
            cp.wait_send()
        for cp in local:
            cp.wait()

    rows = 8 * ROWS_PER_DEVICE
    dma = pltpu.SemaphoreType.DMA
    return pl.pallas_call(
        body, name="comm_fwd",
        out_shape=[jax.ShapeDtypeStruct((rows, D_MODEL), F32), jax.ShapeDtypeStruct((4, rows, w_ada.shape[1]), F32)]
        + [jax.ShapeDtypeStruct((4,) + s.shape, s.dtype) for s in shards],
        in_specs=[VMEM_SPEC, VMEM_SPEC] + [ANY_SPEC] * n,
        out_specs=[VMEM_SPEC, VMEM_SPEC] + [ANY_SPEC] * n,
        scratch_shapes=[pltpu.VMEM((rows, D_MODEL), F32), dma((7,)), dma((7,)), dma((3,)), dma((3,)),
                        dma((3 * n,)), dma((3 * n,)), dma((3 * n,)), dma((3 * n,)), dma((n,))],
        compiler_params=pltpu.CompilerParams(vmem_limit_bytes=VMEM_LIMIT),
    )(c_blk, w_ada, *shards)


def _comm_bwd_call(grads, part):
    n = len(grads)

    def body(part_ref, *refs):
        g_refs, f_refs, parts_ref = refs[:n], refs[n:2 * n], refs[2 * n]
        scratch = refs[2 * n + 1:]
        a_refs, b_refs, p_refs, r_refs = (scratch[k * n:(k + 1) * n] for k in range(4))
        s_send, s_recv, d_send, d_recv, e_send, e_recv, h_send, h_recv, loc_sem = scratch[4 * n:]
        me, dev, chip, sibling, others = _position()
        core = me[2]
        chip_of = [2 * p[0] + p[1] for p in others]

        parts_ref[_rows_of(dev), :] = part_ref[...]
        s_waits = _all_to_all_rows(part_ref, parts_ref, dev, me, s_send, s_recv)

        mine = [pltpu.make_async_copy(g_refs[i].at[:, core], a_refs[i], loc_sem.at[i]) for i in range(n)]
        swap = [pltpu.make_async_remote_copy(src_ref=g_refs[i].at[:, 1 - core], dst_ref=b_refs[i], send_sem=d_send.at[i],
                                             recv_sem=d_recv.at[i], device_id=sibling, device_id_type=MESH) for i in range(n)]
        order = sorted(range(n), key=lambda i: g_refs[i].shape[2] * g_refs[i].shape[3])
        for i in order:
            mine[i].start()
            swap[i].start()
        cross = [pltpu.make_async_remote_copy(src_ref=p_refs[i].at[chip_of[j]], dst_ref=r_refs[i].at[j],
                                              send_sem=e_send.at[3 * i + j], recv_sem=e_recv.at[3 * i + j],
                                              device_id=others[j], device_id_type=MESH) for i in range(n) for j in range(3)]
        for i in order:
            mine[i].wait()
            swap[i].wait()
            for k in range(4):
                s = a_refs[i][k] + b_refs[i][k]
                a_refs[i][k] = s
                p_refs[i][k] = s.astype(BF16)
            for j in range(3):
                cross[3 * i + j].start()
        share = {}
        for i in order:
            for j in range(3):
                cross[3 * i + j].wait()
            f_refs[i][core] = (a_refs[i][chip] + r_refs[i][0].astype(F32) + r_refs[i][1].astype(F32)
                               + r_refs[i][2].astype(F32))
            share[i] = pltpu.make_async_remote_copy(src_ref=f_refs[i].at[core], dst_ref=f_refs[i].at[core],
                                                    send_sem=h_send.at[i], recv_sem=h_recv.at[i], device_id=sibling,
                                                    device_id_type=MESH)
            share[i].start()
        for i in range(n):
            share[i].wait_send()
            pltpu.make_async_remote_copy(src_ref=f_refs[i].at[core], dst_ref=f_refs[i].at[1 - core], send_sem=h_send.at[i],
                                         recv_sem=h_recv.at[i], device_id=sibling, device_id_type=MESH).wait_recv()
        for cp in s_waits:
            cp.wait()

    rows = 8 * ROWS_PER_DEVICE
    dma = pltpu.SemaphoreType.DMA
    quarter = [(4,) + g.shape[2:] for g in grads]
    return pl.pallas_call(
        body, name="comm_bwd",
        out_shape=[jax.ShapeDtypeStruct((2,) + g.shape[2:], F32) for g in grads]
        + [jax.ShapeDtypeStruct((rows, part.shape[1]), F32)],
        in_specs=[VMEM_SPEC] + [ANY_SPEC] * n,
        out_specs=[VMEM_SPEC] * (n + 1),
        scratch_shapes=[pltpu.VMEM(q, F32) for q in quarter] + [pltpu.VMEM(q, F32) for q in quarter]
        + [pltpu.VMEM(q, BF16) for q in quarter] + [pltpu.VMEM((3,) + q[1:], BF16) for q in quarter]
        + [dma((7,)), dma((7,)), dma((n,)), dma((n,)), dma((3 * n,)), dma((3 * n,)), dma((n,)), dma((n,)), dma((n,))],
        compiler_params=pltpu.CompilerParams(vmem_limit_bytes=VMEM_LIMIT),
    )(part, *grads)


def _rope_tables(pos_col, inv_row):
    ang = pos_col * inv_row
    return jnp.cos(ang), jnp.sin(ang)


def _pre_call(x, pos_col, mod, b_ada, ng, qg, kvg, inv128, wa, wkr2, wq2, wkv, seq):
    n_tok = x.shape[0]
    tm = min(TOKEN_TILE, seq)
    per_seq = seq // tm

    def body(x_ref, pos_ref, mod_ref, bada_ref, ng_ref, qg_ref, kvg_ref, inv_ref, wa_ref, wkr_ref, wq_ref, wkv_ref,
             zqkv_ref, gates_ref, qf_ref, kf_ref, v_ref, qs_ref, kd_ref, vd_ref, rope_ref):
        xv = x_ref[...]
        modv = mod_ref[0] + bada_ref[...]
        shift, scale = modv[:, :D_MODEL], modv[:, D_MODEL:2 * D_MODEL]
        r1 = lax.rsqrt(jnp.mean(xv * xv, axis=-1, keepdims=True) + EPS)
        h = ((xv * r1) * ng_ref[...]) * (1.0 + scale) + shift
        hb = h.astype(BF16)
        za = _dot(hb, wa_ref[...])
        zkr = _dot(hb, wkr_ref[...])
        cos, sin = _rope_tables(pos_ref[...], inv_ref[...])
        rope_ref[:, :HEAD_LANES] = cos
        rope_ref[:, HEAD_LANES:] = sin
        zqkv_ref[...] = za[:, :A_GM]
        gates_ref[:, :512] = za[:, A_GM:A_QS]
        gates_ref[:, 512:] = za[:, A_GS:A_END]
        qs_ref[...] = (za[:, A_QS:A_KD] * (SWA_SCALE * LOG2E)).astype(BF16)
        kd_ref[...] = za[:, A_KD:A_VD].astype(BF16)
        vd_ref[...] = za[:, A_VD:A_GS].astype(BF16)
        zq, zkv = za[:, A_ZQ:A_ZKV], za[:, A_ZKV:A_GM]
        rq = lax.rsqrt(jnp.mean(zq * zq, axis=-1, keepdims=True) + EPS)
        qn = ((zq * rq) * qg_ref[...]).astype(BF16)
        qr = _dot(qn, wq_ref[...])
        cf, sf = jnp.tile(cos, (1, N_HEADS)), jnp.tile(sin, (1, N_HEADS))
        qf_ref[...] = ((qr[:, :1024] * cf + qr[:, 1024:] * sf) * (MLA_SCALE * LOG2E)).astype(BF16)
        rkv = lax.rsqrt(jnp.mean(zkv * zkv, axis=-1, keepdims=True) + EPS)
        kvn = ((zkv * rkv) * kvg_ref[...]).astype(BF16)
        kv = _dot(kvn, wkv_ref[...])
        kpe = zkr[:, :128] * cos + zkr[:, 128:] * sin
        kf_ref[...] = (kv[:, :1024] + jnp.tile(kpe, (1, N_HEADS))).astype(BF16)
        v_ref[...] = kv[:, 1024:].astype(BF16)

    tok = lambda w: pl.BlockSpec((tm, w), lambda i: (i, 0))
    outs = [(640, F32), (1024, F32), (1024, BF16), (1024, BF16), (512, BF16), (512, BF16), (256, BF16), (256, BF16),
            (2 * HEAD_LANES, F32)]
    return pl.pallas_call(
        body, name="pre", grid=(n_tok // tm,),
        out_shape=[jax.ShapeDtypeStruct((n_tok, w), dt) for w, dt in outs],
        in_specs=[tok(D_MODEL), tok(1), pl.BlockSpec((1, 1, 3 * D_MODEL), lambda i: (i // per_seq, 0, 0)),
                  _full(b_ada.shape), _full(ng.shape), _full(qg.shape), _full(kvg.shape), _full(inv128.shape),
                  _full(wa.shape), _full(wkr2.shape), _full(wq2.shape), _full(wkv.shape)],
        out_specs=[tok(w) for w, _ in outs],
        compiler_params=_params(1),
    )(x, pos_col, mod, b_ada, ng, qg, kvg, inv128, wa, wkr2, wq2, wkv)


def _lane_lo(width=HEAD_LANES):
    return lax.broadcasted_iota(jnp.int32, (1, width), 1) < HALF


def _eye(n=HEAD_LANES):
    r = lax.broadcasted_iota(jnp.int32, (n, n), 0)
    c = lax.broadcasted_iota(jnp.int32, (n, n), 1)
    return jnp.where(r == c, 1.0, 0.0).astype(BF16)


def _mla_fwd_call(qf, kf, v, n_seq, seq):
    tq = min(ATT_TILE, seq)
    nq = seq // tq

    ext = HALF + 16

    def body(q_ref, k_ref, v_ref, o_ref, lse_ref, vt_ref):
        i = pl.program_id(1)
        eye = _eye()

        @pl.when(i == 0)
        def _():
            for h in range(N_HEADS):
                vt_ref[h * ext + HALF:(h + 1) * ext, :] = jnp.ones((16, seq), BF16)
            for t in range(nq):
                for p in range(N_HEADS // 2):
                    pair = slice(p * HEAD_LANES, (p + 1) * HEAD_LANES)
                    v_t = _dot_nt(eye, v_ref[t * tq:(t + 1) * tq, pair]).astype(BF16)
                    for hh in range(2):
                        r0 = (2 * p + hh) * ext
                        vt_ref[r0:r0 + HALF, t * tq:(t + 1) * tq] = v_t[hh * HALF:(hh + 1) * HALF, :]

        q = q_ref[...]
        qcol = i * tq + lax.broadcasted_iota(jnp.int32, (1, tq), 1)
        heads = range(N_HEADS)
        lanes = [slice(h * HEAD_LANES, (h + 1) * HEAD_LANES) for h in heads]

        def make_step(masked, n_tiles):
            def step(kt0, carry):
                tiles = range(n_tiles)
                start = pl.multiple_of(kt0 * tq, tq)
                ks = [k_ref[pl.ds(pl.multiple_of((kt0 + t) * tq, tq), tq), :] for t in tiles]
                vt = vt_ref[:, pl.ds(start, n_tiles * tq)]
                sts = [[_dot_nt(ks[t][:, lanes[h]], q[:, lanes[h]]) for h in heads] for t in tiles]
                if masked:
                    last = n_tiles - 1
                    keep = ((kt0 + last) * tq + lax.broadcasted_iota(jnp.int32, (tq, 1), 0)) <= qcol
                    sts[last] = [jnp.where(keep, st, NEG) for st in sts[last]]
                stats, pts = [], []
                for h in heads:
                    m_old = carry[2 * h]
                    m_new = m_old
                    for t in tiles:
                        m_new = jnp.maximum(m_new, jnp.max(sts[t][h], axis=0, keepdims=True))
                    pts.append(jnp.concatenate([jnp.exp2(sts[t][h] - m_new).astype(BF16) for t in tiles], axis=0))
                    stats.append((m_new, jnp.exp2(m_old - m_new)))
                pvs = [_dot(vt[h * ext:(h + 1) * ext, :], pts[h]) for h in heads]
                out = []
                for h in heads:
                    out += [stats[h][0], carry[2 * h + 1] * stats[h][1] + pvs[h]]
                return tuple(out)
            return step

        init = (jnp.full((1, tq), NEG, F32), jnp.zeros((ext, tq), F32)) * N_HEADS
        count = i + 1
        carry = lax.fori_loop(0, (count + 1) // 2 - 1, lambda j, c: make_step(False, 2)(2 * j, c), init)
        carry = lax.cond(count % 2 == 0, lambda c: make_step(True, 2)(i - 1, c), lambda c: make_step(True, 1)(i, c), carry)
        dens = [carry[2 * h + 1][HALF:HALF + 1, :] for h in heads]
        acc_t = jnp.concatenate([carry[2 * h + 1][:HALF, :] * (1.0 / dens[h]) for h in heads], axis=0)
        o_ref[...] = acc_t.T
        for h in heads:
            lse_ref[0, h // 4, h % 4:h % 4 + 1, :] = carry[2 * h] + jnp.log2(dens[h])

    n_tok = qf.shape[0]
    return pl.pallas_call(
        body, name="mla_fwd", grid=(n_seq, nq),
        out_shape=[jax.ShapeDtypeStruct((n_tok, 512), F32), jax.ShapeDtypeStruct((n_seq, 2, 4, seq), F32)],
        in_specs=[pl.BlockSpec((tq, 1024), lambda b, i: (b * nq + i, 0)),
                  pl.BlockSpec((seq, 1024), lambda b, i: (b, 0)),
                  pl.BlockSpec((seq, 512), lambda b, i: (b, 0))],
        out_specs=[pl.BlockSpec((tq, 512), lambda b, i: (b * nq + i, 0)),
                   pl.BlockSpec((1, 2, 4, tq), lambda b, i: (b, 0, 0, i))],
        scratch_shapes=[pltpu.VMEM((N_HEADS * ext, seq), BF16)],
        compiler_params=_params(2),
    )(qf, kf, v)


def _mla_bwd_call(qf, kf, v, do, o, lse, n_seq, seq):
    tq = min(ATT_TILE, seq)
    nq = seq // tq

    nh = 4
    heads = range(nh)
    lanes = [slice(h * HEAD_LANES, (h + 1) * HEAD_LANES) for h in heads]

    def body(q_ref, k_ref, v_ref, do_ref, o_ref, lse_ref, dq_ref, dk_ref, dv_ref,
             kt_ref, dot_ref, delta_ref, dqt_ref):
        eye = _eye()
        lo = _lane_lo()
        sub_lo = lax.broadcasted_iota(jnp.int32, (HEAD_LANES, 1), 0) < HALF
        ones_lo = jnp.where(jnp.broadcast_to(lo, (8, HEAD_LANES)), 1.0, 0.0).astype(BF16)
        ones_hi = jnp.where(jnp.broadcast_to(lo, (8, HEAD_LANES)), 0.0, 1.0).astype(BF16)

        for t in range(nq):
            r = slice(t * tq, (t + 1) * tq)
            kv = k_ref[r, :]
            for h in heads:
                kt_ref[lanes[h], r] = _dot_nt(eye, kv[:, lanes[h]]).astype(BF16)
            for p in range(nh // 2):
                dov = do_ref[r, lanes[p]]
                dt = _dot_nt(eye, dov)
                dot_ref[2 * p, :, r] = jnp.where(sub_lo, dt, 0.0).astype(BF16)
                dot_ref[2 * p + 1, :, r] = jnp.where(sub_lo, 0.0, dt).astype(BF16)
                prod = dov.astype(F32) * o_ref[r, lanes[p]]
                p_hi = prod.astype(BF16)
                p_lo = (prod - p_hi.astype(F32)).astype(BF16)
                delta_ref[2 * p, :, r] = _dot_nt(ones_lo, p_hi) + _dot_nt(ones_lo, p_lo)
                delta_ref[2 * p + 1, :, r] = _dot_nt(ones_hi, p_hi) + _dot_nt(ones_hi, p_lo)
        dqt_ref[...] = jnp.zeros_like(dqt_ref)

        def k_step(kt, _):
            kr = pl.ds(pl.multiple_of(kt * tq, tq), tq)
            k = k_ref[kr, :]
            vv = v_ref[kr, :]
            k_t = kt_ref[:, kr]
            krow = kt * tq + lax.broadcasted_iota(jnp.int32, (tq, 1), 0)

            def make_step(masked, n_tiles):
                def q_step(qt0, carry):
                    tiles = range(n_tiles)
                    qrs = [pl.ds(pl.multiple_of((qt0 + t) * tq, tq), tq) for t in tiles]
                    qs = [q_ref[qr, :] for qr in qrs]
                    do_ts = [[dot_ref[h, :, qr] for h in heads] for qr in qrs]
                    sts = [[_dot_nt(k[:, lanes[h]], qs[t][:, lanes[h]]) for h in heads] for t in tiles]
                    dpts = [[_dot(vv[:, lanes[h // 2]], do_ts[t][h]) for h in heads] for t in tiles]
                    if masked:
                        keep = krow <= (qt0 * tq + lax.broadcasted_iota(jnp.int32, (1, tq), 1))
                    pts, dsts = [], []
                    for t in tiles:
                        pts.append([])
                        dsts.append([])
                        for h in heads:
                            pt = jnp.exp2(sts[t][h] - lse_ref[0, 0, h:h + 1, qrs[t]])
                            if masked and t == 0:
                                pt = jnp.where(keep, pt, 0.0)
                            dsts[t].append((pt * (dpts[t][h] - delta_ref[h, 0:1, qrs[t]])).astype(BF16))
                            pts[t].append(pt.astype(BF16))
                    out = []
                    for h in heads:
                        hh = h % 2
                        half = slice(hh * HALF, (hh + 1) * HALF)
                        dst_all = jnp.concatenate([dsts[t][h] for t in tiles], axis=1)
                        pt_all = jnp.concatenate([pts[t][h] for t in tiles], axis=1)
                        do_all = jnp.concatenate([do_ts[t][h][half, :] for t in tiles], axis=1)
                        q_all = jnp.concatenate([qs[t][:, lanes[h]] for t in tiles], axis=0)
                        dvt = _dot_nt(do_all, pt_all)
                        dk = _dot(dst_all, q_all)
                        for t in tiles:
                            dqt_ref[lanes[h], qrs[t]] += _dot(k_t[lanes[h], :], dsts[t][h])
                        out += [carry[2 * h] + dk, carry[2 * h + 1] + dvt]
                    return tuple(out)
                return q_step

            init = (jnp.zeros((tq, HEAD_LANES), F32), jnp.zeros((HALF, tq), F32)) * nh
            count = nq - kt
            carry = lax.cond(count >= 2, lambda c: make_step(True, 2)(kt, c), lambda c: make_step(True, 1)(kt, c), init)
            carry = lax.fori_loop(1, count // 2, lambda j, c: make_step(False, 2)(kt + 2 * j, c), carry)
            carry = lax.cond(jnp.logical_and(count % 2 == 1, count >= 3),
                             lambda c: make_step(False, 1)(nq - 1, c), lambda c: c, carry)
            for h in heads:
                dk_ref[kr, lanes[h]] = carry[2 * h]
            for p in range(nh // 2):
                dv_ref[kr, lanes[p]] = jnp.concatenate([carry[4 * p + 1], carry[4 * p + 3]], axis=0).T
            return 0

        lax.fori_loop(0, nq, k_step, 0)
        for t in range(nq):
            r = slice(t * tq, (t + 1) * tq)
            for h in heads:
                dq_ref[r, lanes[h]] = dqt_ref[lanes[h], r].T

    n_tok = qf.shape[0]
    groups = N_HEADS // nh
    blk = lambda w: pl.BlockSpec((seq, w), lambda b, g: (b, g))
    return pl.pallas_call(
        body, name="mla_bwd", grid=(n_seq, groups),
        out_shape=[jax.ShapeDtypeStruct((n_tok, 1024), F32), jax.ShapeDtypeStruct((n_tok, 1024), F32),
                   jax.ShapeDtypeStruct((n_tok, 512), F32)],
        in_specs=[blk(512), blk(512), blk(256), blk(256), blk(256),
                  pl.BlockSpec((1, 1, nh, seq), lambda b, g: (b, g, 0, 0))],
        out_specs=[blk(512), blk(512), blk(256)],
        scratch_shapes=[pltpu.VMEM((nh * HEAD_LANES, seq), BF16), pltpu.VMEM((nh, HEAD_LANES, seq), BF16),
                        pltpu.VMEM((nh, 8, seq), F32), pltpu.VMEM((nh * HEAD_LANES, seq), F32)],
        compiler_params=_params(2),
    )(qf, kf, v, do, o, lse)


SWA_BLOCKS = 4


def _swa_block(n, pos_col_ref, posq):
    w = SWA_WINDOW
    start = pl.multiple_of(jnp.maximum(n - 1, 0) * w, w)
    posk = pos_col_ref[pl.ds(start, 2 * w), :]
    rel = (n * w + lax.broadcasted_iota(jnp.int32, (1, w), 1)) - (start + lax.broadcasted_iota(jnp.int32, (2 * w, 1), 0))
    valid = jnp.logical_and(rel >= 0, rel < w)
    return start, jnp.where(valid, posq - posk, 1e30)


def _alibi(h):
    return LOG2E * 2.0 ** -(h + 1)


def _transpose_rows(eye, src_ref, dst_ref, seq, width):
    step = 2 * SWA_WINDOW
    for t in range(seq // step):
        for p in range(width // HEAD_LANES):
            lanes = slice(p * HEAD_LANES, (p + 1) * HEAD_LANES)
            dst_ref[lanes, t * step:(t + 1) * step] = _dot_nt(eye, src_ref[t * step:(t + 1) * step, lanes]).astype(BF16)


def _swa_fwd_call(qs, kd, vd, pos_col, pos_row, sinks, n_seq, seq):
    w = SWA_WINDOW
    qb = SWA_BLOCKS
    steps = seq // (qb * w)
    ext = HALF + 16

    def body(q_ref, k_ref, v_ref, pc_ref, pr_ref, sink_ref, o_ref, lse_ref, vt_ref):
        n = pl.program_id(1)
        lo = _lane_lo()
        hi = jnp.logical_not(lo)
        eye = _eye()

        @pl.when(n == 0)
        def _():
            step = 2 * w
            for kv in range(2):
                vt_ref[kv * ext + HALF:(kv + 1) * ext, :] = jnp.ones((16, seq), BF16)
                for t in range(seq // step):
                    v_t = _dot_nt(eye, v_ref[t * step:(t + 1) * step, kv * HEAD_LANES:(kv + 1) * HEAD_LANES])
                    vt_ref[kv * ext:kv * ext + HALF, t * step:(t + 1) * step] = v_t[:HALF, :].astype(BF16)

        heads = range(N_HEADS)
        blocks = range(qb)
        geo = [_swa_block(n * qb + bi, pc_ref, pr_ref[bi]) for bi in blocks]
        wins = [pl.ds(g[0], 2 * w) for g in geo]
        kwins = [k_ref[win, :] for win in wins]
        vts = [vt_ref[:, win] for win in wins]
        sts = []
        for bi in blocks:
            q = q_ref[bi * w:(bi + 1) * w, :]
            sts.append([])
            for h in heads:
                qp = q[:, (h // 2) * HEAD_LANES:(h // 2 + 1) * HEAD_LANES]
                qh = jnp.where(lo if h % 2 == 0 else hi, qp, jnp.zeros_like(qp))
                sts[bi].append(_dot_nt(kwins[bi][:, (h // 4) * HEAD_LANES:(h // 4 + 1) * HEAD_LANES], qh))
        ps, ms = [], []
        for bi in blocks:
            ps.append([])
            ms.append([])
            for h in heads:
                s = sts[bi][h] - _alibi(h) * geo[bi][1]
                m = jnp.maximum(jnp.max(s, axis=0, keepdims=True), sink_ref[0, h] * LOG2E)
                ps[bi].append(jnp.exp2(s - m).astype(BF16))
                ms[bi].append(m)
        for bi in blocks:
            ots = []
            for h in heads:
                pv = _dot(vts[bi][(h // 4) * ext:(h // 4 + 1) * ext, :], ps[bi][h])
                l = pv[HALF:HALF + 1, :] + jnp.exp2(sink_ref[0, h] * LOG2E - ms[bi][h])
                ots.append(pv[:HALF, :] * (1.0 / l))
                lse_ref[0, h:h + 1, bi * w:(bi + 1) * w] = ms[bi][h] + jnp.log2(l)
            o_ref[bi * w:(bi + 1) * w, :] = jnp.concatenate(ots, axis=0).T

    n_tok = qs.shape[0]
    tok = lambda width: pl.BlockSpec((qb * w, width), lambda b, n: (b * steps + n, 0))
    whole = lambda width: pl.BlockSpec((seq, width), lambda b, n: (b, 0))
    return pl.pallas_call(
        body, name="swa_fwd", grid=(n_seq, steps),
        out_shape=[jax.ShapeDtypeStruct((n_tok, 512), F32), jax.ShapeDtypeStruct((n_seq, N_HEADS, seq), F32)],
        in_specs=[tok(512), whole(256), whole(256), whole(1), pl.BlockSpec((qb, 1, w), lambda b, n: (b * steps + n, 0, 0)),
                  pl.BlockSpec(memory_space=pltpu.SMEM)],
        out_specs=[tok(512), pl.BlockSpec((1, N_HEADS, qb * w), lambda b, n: (b, 0, n))],
        scratch_shapes=[pltpu.VMEM((2 * ext, seq), BF16)],
        compiler_params=_params(2),
    )(qs, kd, vd, pos_col, pos_row, sinks)


def _swa_bwd_call(qs, kd, vd, do, o, lse, pos_col, pos_row, sinks, n_seq, seq):
    w = SWA_WINDOW
    qb = SWA_BLOCKS
    steps = seq // (qb * w)

    def body(q_ref, k_ref, v_ref, do_ref, o_ref, lse_ref, pc_ref, pr_ref, sink_ref, dq_ref, dk_ref, dv_ref, dsink_ref,
             kt_ref):
        b, n = pl.program_id(0), pl.program_id(1)
        lo = _lane_lo()
        hi = jnp.logical_not(lo)
        sub_lo = lax.broadcasted_iota(jnp.int32, (HEAD_LANES, 1), 0) < HALF
        eye = _eye()
        ones_lo = jnp.where(jnp.broadcast_to(lo, (8, HEAD_LANES)), 1.0, 0.0).astype(BF16)
        ones_hi = jnp.where(jnp.broadcast_to(lo, (8, HEAD_LANES)), 0.0, 1.0).astype(BF16)

        @pl.when(n == 0)
        def _():
            dk_ref[...] = jnp.zeros_like(dk_ref)
            dv_ref[...] = jnp.zeros_like(dv_ref)
            _transpose_rows(eye, k_ref, kt_ref, seq, 2 * HEAD_LANES)

        @pl.when(jnp.logical_and(n == 0, b == 0))
        def _():
            dsink_ref[...] = jnp.zeros_like(dsink_ref)

        heads = range(N_HEADS)
        blocks = range(qb)
        kv_lanes = lambda h: slice((h // 4) * HEAD_LANES, (h // 4 + 1) * HEAD_LANES)
        geo = [_swa_block(n * qb + bi, pc_ref, pr_ref[bi]) for bi in blocks]
        wins = [pl.ds(g[0], 2 * w) for g in geo]
        kwins = [k_ref[win, :] for win in wins]
        vwins = [v_ref[win, :] for win in wins]

        do_ts, deltas, qms, doms = [], [], [], []
        for bi in blocks:
            rows = slice(bi * w, (bi + 1) * w)
            for lst in (do_ts, deltas, qms, doms):
                lst.append([])
            for j in range(N_HEADS // 2):
                pair = slice(j * HEAD_LANES, (j + 1) * HEAD_LANES)
                dop = do_ref[rows, pair]
                qp = q_ref[rows, pair]
                dt = _dot_nt(eye, dop)
                prod = dop.astype(F32) * o_ref[rows, pair]
                p_hi = prod.astype(BF16)
                p_lo = (prod - p_hi.astype(F32)).astype(BF16)
                for hh in range(2):
                    half, ones = (lo, ones_lo) if hh == 0 else (hi, ones_hi)
                    do_ts[bi].append(jnp.where(sub_lo, dt, 0.0).astype(BF16) if hh == 0
                                     else jnp.where(sub_lo, 0.0, dt).astype(BF16))
                    deltas[bi].append((_dot_nt(ones, p_hi) + _dot_nt(ones, p_lo))[0:1, :])
                    qms[bi].append(jnp.where(half, qp, jnp.zeros_like(qp)))
                    doms[bi].append(jnp.where(half, dop, jnp.zeros_like(dop)))
        sts = [[_dot_nt(kwins[bi][:, kv_lanes(h)], qms[bi][h]) for h in heads] for bi in blocks]
        dpts = [[_dot(vwins[bi][:, kv_lanes(h)], do_ts[bi][h]) for h in heads] for bi in blocks]
        pts, dsts = [], []
        for bi in blocks:
            pts.append([])
            dsts.append([])
            for h in heads:
                lse_h = lse_ref[0, h:h + 1, bi * w:(bi + 1) * w]
                pt = jnp.exp2(sts[bi][h] - _alibi(h) * geo[bi][1] - lse_h)
                dsts[bi].append((pt * (dpts[bi][h] - deltas[bi][h])).astype(BF16))
                pts[bi].append(pt.astype(BF16))
                dsink_ref[h:h + 1, :] += -jnp.exp2(sink_ref[0, h] * LOG2E - lse_h) * deltas[bi][h]
        for bi in blocks:
            for kv in range(2):
                group = range(4 * kv, 4 * kv + 4)
                dst_all = jnp.concatenate([dsts[bi][h] for h in group], axis=1)
                pt_all = jnp.concatenate([pts[bi][h] for h in group], axis=1)
                q_all = jnp.concatenate([qms[bi][h] for h in group], axis=0)
                do_all = jnp.concatenate([doms[bi][h] for h in group], axis=0)
                dk_ref[wins[bi], kv_lanes(4 * kv)] += _dot(dst_all, q_all)
                dv_ref[wins[bi], kv_lanes(4 * kv)] += _dot(pt_all, do_all)
        for bi in blocks:
            ktw = kt_ref[:, wins[bi]]
            for j in range(N_HEADS // 2):
                k_t = ktw[kv_lanes(2 * j), :]
                dq_t = jnp.where(sub_lo, _dot(k_t, dsts[bi][2 * j]), _dot(k_t, dsts[bi][2 * j + 1]))
                dq_ref[bi * w:(bi + 1) * w, j * HEAD_LANES:(j + 1) * HEAD_LANES] = dq_t.T * SWA_SCALE

    n_tok = qs.shape[0]
    tok = lambda width: pl.BlockSpec((qb * w, width), lambda b, n: (b * steps + n, 0))
    whole = lambda width: pl.BlockSpec((seq, width), lambda b, n: (b, 0))
    return pl.pallas_call(
        body, name="swa_bwd", grid=(n_seq, steps),
        out_shape=[jax.ShapeDtypeStruct((n_tok, 512), F32), jax.ShapeDtypeStruct((n_tok, 256), F32),
                   jax.ShapeDtypeStruct((n_tok, 256), F32), jax.ShapeDtypeStruct((N_HEADS, HEAD_LANES), F32)],
        in_specs=[tok(512), whole(256), whole(256), pl.BlockSpec((qb * w, 512), lambda b, n: (b * steps + n, 1)), tok(512),
                  pl.BlockSpec((1, N_HEADS, qb * w), lambda b, n: (b, 0, n)),
                  whole(1), pl.BlockSpec((qb, 1, w), lambda b, n: (b * steps + n, 0, 0)),
                  pl.BlockSpec(memory_space=pltpu.SMEM)],
        out_specs=[tok(512), whole(256), whole(256), _full((N_HEADS, HEAD_LANES))],
        scratch_shapes=[pltpu.VMEM((2 * HEAD_LANES, seq), BF16)],
        compiler_params=_params(2),
    )(qs, kd, vd, do, o, lse, pos_col, pos_row, sinks)


def _post_call(x, target, o_mla, o_swa, gates, mod, b_ada, fg, w_out, seq):
    n_tok = x.shape[0]
    tm = min(TOKEN_TILE, seq)
    per_seq = seq // tm
    n_seq = n_tok // seq

    def body(x_ref, t_ref, om_ref, os_ref, g_ref, mod_ref, bada_ref, fg_ref, w_ref,
             dx2_ref, do_ref, dg_ref, gw_ref, gfg_ref, dgate_ref, loss_ref):
        i = pl.program_id(0)

        @pl.when(i == 0)
        def _():
            gw_ref[...] = jnp.zeros_like(gw_ref)
            gfg_ref[...] = jnp.zeros_like(gfg_ref)
            loss_ref[...] = jnp.zeros_like(loss_ref)

        @pl.when(i % per_seq == 0)
        def _():
            dgate_ref[...] = jnp.zeros_like(dgate_ref)

        gate = mod_ref[0][:, 2 * D_MODEL:] + bada_ref[:, 2 * D_MODEL:]
        fgv = fg_ref[...]
        subs = _sub_tiles(tm)
        gs = [g_ref[r, :] for r in subs]
        os_ = [jnp.concatenate([om_ref[r, :], os_ref[r, :]], axis=-1) for r in subs]
        sgs = [_sigmoid(g) for g in gs]
        sils = [g * sg for g, sg in zip(gs, sgs)]
        ypres = [(o * sil).astype(BF16) for o, sil in zip(os_, sils)]
        ys = [_dot(ypre, w_ref[...]) for ypre in ypres]
        dys, loss, gfg, dgate = [], 0.0, 0.0, 0.0
        for r, y in zip(subs, ys):
            x2 = x_ref[r, :] + gate * y
            r2 = lax.rsqrt(jnp.mean(x2 * x2, axis=-1, keepdims=True) + EPS)
            xn2 = x2 * r2
            err = xn2 * fgv - t_ref[r, :]
            loss = loss + jnp.sum(jnp.sum(err * err, axis=-1, keepdims=True), axis=0, keepdims=True)
            dout = err * (1.0 / D_MODEL)
            gfg = gfg + jnp.sum(dout * xn2, axis=0, keepdims=True)
            dxn2 = dout * fgv
            dx2 = r2 * (dxn2 - xn2 * jnp.mean(dxn2 * xn2, axis=-1, keepdims=True))
            dx2_ref[r, :] = dx2
            dgate = dgate + jnp.sum(dx2 * y, axis=0, keepdims=True)
            dys.append((dx2 * gate).astype(BF16))
        loss_ref[...] += jnp.broadcast_to(loss * (0.5 / D_MODEL), loss_ref.shape)
        gfg_ref[...] += gfg
        dgate_ref[0] += dgate
        gw_ref[...] += _dot_tn(jnp.concatenate(ypres, axis=0), jnp.concatenate(dys, axis=0))
        dypres = [_dot_nt(dy, w_ref[...]) for dy in dys]
        for r, dypre, o, g, sg, sil in zip(subs, dypres, os_, gs, sgs, sils):
            do_ref[r, :] = (dypre * sil).astype(BF16)
            dg_ref[r, :] = (dypre * o * (sg * (1.0 + g * (1.0 - sg)))).astype(BF16)

    tok = lambda w: pl.BlockSpec((tm, w), lambda i: (i, 0))
    per_b = pl.BlockSpec((1, 1, 3 * D_MODEL), lambda i: (i // per_seq, 0, 0))
    return pl.pallas_call(
        body, name="post", grid=(n_tok // tm,),
        out_shape=[jax.ShapeDtypeStruct((n_tok, D_MODEL), F32), jax.ShapeDtypeStruct((n_tok, D_MODEL), BF16),
                   jax.ShapeDtypeStruct((n_tok, D_MODEL), BF16), jax.ShapeDtypeStruct((D_MODEL, D_MODEL), F32),
                   jax.ShapeDtypeStruct((1, D_MODEL), F32), jax.ShapeDtypeStruct((n_seq, 1, D_MODEL), F32),
                   jax.ShapeDtypeStruct((1, HEAD_LANES), F32)],
        in_specs=[tok(D_MODEL), tok(D_MODEL), tok(512), tok(512), tok(D_MODEL), per_b, _full(b_ada.shape),
                  _full(fg.shape), _full(w_out.shape)],
        out_specs=[tok(D_MODEL), tok(D_MODEL), tok(D_MODEL), _full((D_MODEL, D_MODEL)), _full((1, D_MODEL)),
                   pl.BlockSpec((1, 1, D_MODEL), lambda i: (i // per_seq, 0, 0)), _full((1, HEAD_LANES))],
        compiler_params=_params(1),
    )(x, target, o_mla, o_swa, gates, mod, b_ada, fg, w_out)


def _mid_bwd_call(dqf, dkf, dv, zqkv, rope, qg, kvg, wq2, wkv, seq):
    n_tok = dqf.shape[0]
    tm = min(TOKEN_TILE, seq)

    def body(dq_ref, dk_ref, dv_ref, z_ref, rope_ref, qg_ref, kvg_ref, wq_ref, wkv_ref,
             dz_ref, dkr_ref, gwq_ref, gwkv_ref, gqg_ref, gkvg_ref):
        i = pl.program_id(0)

        @pl.when(i == 0)
        def _():
            gwq_ref[...] = jnp.zeros_like(gwq_ref)
            gwkv_ref[...] = jnp.zeros_like(gwkv_ref)
            gqg_ref[...] = jnp.zeros_like(gqg_ref)
            gkvg_ref[...] = jnp.zeros_like(gkvg_ref)

        cos, sin = rope_ref[:, :HEAD_LANES], rope_ref[:, HEAD_LANES:]
        cf, sf = jnp.tile(cos, (1, N_HEADS)), jnp.tile(sin, (1, N_HEADS))
        dq = dq_ref[...] * MLA_SCALE
        dqr = jnp.concatenate([dq * cf, dq * sf], axis=-1).astype(BF16)
        zq, zkv = z_ref[:, :Q_LORA], z_ref[:, Q_LORA:]
        qgv, kvgv = qg_ref[...], kvg_ref[...]

        rq = lax.rsqrt(jnp.mean(zq * zq, axis=-1, keepdims=True) + EPS)
        xq = zq * rq
        gwq_ref[...] += _dot_tn((xq * qgv).astype(BF16), dqr)
        dqn = _dot_nt(dqr, wq_ref[...])
        gqg_ref[...] += jnp.sum(dqn * xq, axis=0, keepdims=True)
        dxq = dqn * qgv
        dz_ref[:, :Q_LORA] = (rq * (dxq - xq * jnp.mean(dxq * xq, axis=-1, keepdims=True))).astype(BF16)

        dk = dk_ref[...] * LN2
        dkv = jnp.concatenate([dk, dv_ref[...]], axis=-1).astype(BF16)
        rkv = lax.rsqrt(jnp.mean(zkv * zkv, axis=-1, keepdims=True) + EPS)
        xkv = zkv * rkv
        gwkv_ref[...] += _dot_tn((xkv * kvgv).astype(BF16), dkv)
        dkvn = _dot_nt(dkv, wkv_ref[...])
        gkvg_ref[...] += jnp.sum(dkvn * xkv, axis=0, keepdims=True)
        dxkv = dkvn * kvgv
        dz_ref[:, Q_LORA:] = (rkv * (dxkv - xkv * jnp.mean(dxkv * xkv, axis=-1, keepdims=True))).astype(BF16)

        dkpe = dk[:, :HEAD_LANES]
        for h in range(1, N_HEADS):
            dkpe = dkpe + dk[:, h * HEAD_LANES:(h + 1) * HEAD_LANES]
        dkr_ref[:, :HEAD_LANES] = (dkpe * cos).astype(BF16)
        dkr_ref[:, HEAD_LANES:] = (dkpe * sin).astype(BF16)

    tok = lambda w: pl.BlockSpec((tm, w), lambda i: (i, 0))
    return pl.pallas_call(
        body, name="mid_bwd", grid=(n_tok // tm,),
        out_shape=[jax.ShapeDtypeStruct((n_tok, 640), BF16), jax.ShapeDtypeStruct((n_tok, 256), BF16),
                   jax.ShapeDtypeStruct(wq2.shape, F32), jax.ShapeDtypeStruct(wkv.shape, F32),
                   jax.ShapeDtypeStruct((1, Q_LORA), F32), jax.ShapeDtypeStruct((1, KV_LORA), F32)],
        in_specs=[tok(1024), tok(1024), tok(512), tok(640), tok(2 * HEAD_LANES), _full(qg.shape), _full(kvg.shape),
                  _full(wq2.shape), _full(wkv.shape)],
        out_specs=[tok(640), tok(256), _full(wq2.shape), _full(wkv.shape), _full((1, Q_LORA)), _full((1, KV_LORA))],
        compiler_params=_params(1),
    )(dqf, dkf, dv, zqkv, rope, qg, kvg, wq2, wkv)


def _in_bwd_call(x, dx2, dz, dkr, dg, dqs, dkd, dvd, mod, b_ada, ng, wa, wkr2, seq):
    n_tok = x.shape[0]
    tm = min(TOKEN_TILE, seq)
    per_seq = seq // tm
    n_seq = n_tok // seq

    def body(x_ref, dx2_ref, dz_ref, dkr_ref, dg_ref, dqs_ref, dkd_ref, dvd_ref, mod_ref, bada_ref, ng_ref,
             wa_ref, wkr_ref, gx_ref, gwa_ref, gwkr_ref, gng_ref, dshift_ref, dscale_ref):
        i = pl.program_id(0)

        @pl.when(i == 0)
        def _():
            gwa_ref[...] = jnp.zeros_like(gwa_ref)
            gwkr_ref[...] = jnp.zeros_like(gwkr_ref)
            gng_ref[...] = jnp.zeros_like(gng_ref)

        @pl.when(i % per_seq == 0)
        def _():
            dshift_ref[...] = jnp.zeros_like(dshift_ref)
            dscale_ref[...] = jnp.zeros_like(dscale_ref)

        xv = x_ref[...]
        modv = mod_ref[0] + bada_ref[...]
        shift, scale = modv[:, :D_MODEL], modv[:, D_MODEL:2 * D_MODEL]
        ngv = ng_ref[...]
        r1 = lax.rsqrt(jnp.mean(xv * xv, axis=-1, keepdims=True) + EPS)
        xn = xv * r1
        hb = ((xn * ngv) * (1.0 + scale) + shift).astype(BF16)

        dgv = dg_ref[...]
        pieces = [(A_ZQ, dz_ref[...]), (A_GM, dgv[:, :512]), (A_QS, dqs_ref[...].astype(BF16)),
                  (A_KD, (dkd_ref[...] * LN2).astype(BF16)),
                  (A_VD, dvd_ref[...].astype(BF16)), (A_GS, dgv[:, 512:])]
        dkr = dkr_ref[...]
        gwkr_ref[...] += _dot_tn(hb, dkr)
        dh = _dot_nt(dkr, wkr_ref[...])
        for off, piece in pieces:
            wd = piece.shape[1]
            gwa_ref[:, off:off + wd] += _dot_tn(hb, piece)
            dh = dh + _dot_nt(piece, wa_ref[:, off:off + wd])

        dshift_ref[0] += jnp.sum(dh, axis=0, keepdims=True)
        dscale_ref[0] += jnp.sum(dh * (xn * ngv), axis=0, keepdims=True)
        gng_ref[...] += jnp.sum(dh * xn * (1.0 + scale), axis=0, keepdims=True)
        dxn = dh * ngv * (1.0 + scale)
        gx_ref[...] = dx2_ref[...] + r1 * (dxn - xn * jnp.mean(dxn * xn, axis=-1, keepdims=True))

    tok = lambda w: pl.BlockSpec((tm, w), lambda i: (i, 0))
    per_b = lambda w: pl.BlockSpec((1, 1, w), lambda i: (i // per_seq, 0, 0))
    return pl.pallas_call(
        body, name="in_bwd", grid=(n_tok // tm,),
        out_shape=[jax.ShapeDtypeStruct((n_tok, D_MODEL), F32), jax.ShapeDtypeStruct((D_MODEL, A_END), F32),
                   jax.ShapeDtypeStruct((D_MODEL, 256), F32), jax.ShapeDtypeStruct((1, D_MODEL), F32),
                   jax.ShapeDtypeStruct((n_seq, 1, D_MODEL), F32), jax.ShapeDtypeStruct((n_seq, 1, D_MODEL), F32)],
        in_specs=[tok(D_MODEL), tok(D_MODEL), tok(640), tok(256), tok(D_MODEL), tok(512), tok(256), tok(256),
                  per_b(3 * D_MODEL), _full(b_ada.shape), _full(ng.shape), _full(wa.shape), _full(wkr2.shape)],
        out_specs=[tok(D_MODEL), _full((D_MODEL, A_END)), _full((D_MODEL, 256)), _full((1, D_MODEL)),
                   per_b(D_MODEL), per_b(D_MODEL)],
        compiler_params=_params(1),
    )(x, dx2, dz, dkr, dg, dqs, dkd, dvd, mod, b_ada, ng, wa, wkr2)


def _adam_math(w, g, m, v):
    m_new = ADAM_B1 * m + (1.0 - ADAM_B1) * g
    v_new = ADAM_B2 * v + (1.0 - ADAM_B2) * (g * g)
    m_hat = m_new / (1.0 - ADAM_B1 ** ADAM_STEP)
    v_hat = v_new / (1.0 - ADAM_B2 ** ADAM_STEP)
    delta = -ADAM_LR * (m_hat / (jnp.sqrt(v_hat) + ADAM_EPS) + ADAM_WD * w)
    return delta, m_new, v_new


def _adam_call(name, w, g, m, v):
    rows, cols = w.shape
    tr = next((t for t in (256, 128, 88) if rows % t == 0), rows)

    def body(w_ref, g_ref, m_ref, v_ref, d_ref, mo_ref, vo_ref):
        d, mn, vn = _adam_math(w_ref[...], g_ref[...], m_ref[...], v_ref[...])
        d_ref[...] = d
        mo_ref[...] = mn
        vo_ref[...] = vn

    spec = pl.BlockSpec((tr, cols), lambda i: (i, 0))
    return pl.pallas_call(
        body, name=name, grid=(rows // tr,),
        out_shape=[jax.ShapeDtypeStruct(w.shape, F32)] * 3,
        in_specs=[spec] * 4, out_specs=[spec] * 3,
        compiler_params=_params(1),
    )(w, g, m, v)


def _ada_bwd_call(act_all, dmod_cols, w, m, v):
    rows, cols = w.shape
    tr = 256

    def body(a_ref, dm_ref, w_ref, m_ref, v_ref, g_ref, d_ref, mo_ref, vo_ref):
        g = _dot_tn(a_ref[...].astype(BF16), dm_ref[...].astype(BF16))
        d, mn, vn = _adam_math(w_ref[...], g, m_ref[...], v_ref[...])
        g_ref[...] = g
        d_ref[...] = d
        mo_ref[...] = mn
        vo_ref[...] = vn

    spec = pl.BlockSpec((tr, cols), lambda i: (i, 0))
    nb = act_all.shape[0]
    return pl.pallas_call(
        body, name="ada_bwd", grid=(rows // tr,),
        out_shape=[jax.ShapeDtypeStruct(w.shape, F32)] * 4,
        in_specs=[pl.BlockSpec((nb, tr), lambda i: (0, i)), _full(dmod_cols.shape), spec, spec, spec],
        out_specs=[spec] * 4,
        compiler_params=_params(1),
    )(act_all, dmod_cols, w, m, v)


SMALL_ROW = {"norm_gain": (0, 1024), "final_gain": (1024, 2048), "q_norm_gain": (2048, 2432),
             "kv_norm_gain": (2432, 2688), "swa_sinks": (2688, 2696), "loss": (2816, 2944)}
SMALL_ORDER = ("b_ada", "norm_gain", "q_norm_gain", "kv_norm_gain", "swa_sinks", "final_gain")


def _small_call(parts_all, n_seq, params):
    k = len(params)

    def body(p_ref, *refs):
        ins, outs, loss_ref = refs[:3 * k], refs[3 * k:7 * k], refs[7 * k]
        row = p_ref[n_seq:n_seq + 1, :]
        for dv in range(1, 8):
            r0 = dv * ROWS_PER_DEVICE + n_seq
            row = row + p_ref[r0:r0 + 1, :]
        gb = None
        for dv in range(8):
            for r in range(n_seq):
                r0 = dv * ROWS_PER_DEVICE + r
                gb = p_ref[r0:r0 + 1, :] if gb is None else gb + p_ref[r0:r0 + 1, :]
        for j, name in enumerate(SMALL_ORDER):
            g = gb if name == "b_ada" else row[:, SMALL_ROW[name][0]:SMALL_ROW[name][1]]
            d, mn, vn = _adam_math(ins[3 * j][...], g, ins[3 * j + 1][...], ins[3 * j + 2][...])
            outs[4 * j][...] = g
            outs[4 * j + 1][...] = d
            outs[4 * j + 2][...] = mn
            outs[4 * j + 3][...] = vn
        loss_ref[...] = row[:, SMALL_ROW["loss"][0]:SMALL_ROW["loss"][1]]

    flat = [t for p in params for t in p]
    res = pl.pallas_call(
        body, name="small_update", grid=(1,),
        out_shape=[jax.ShapeDtypeStruct(p[0].shape, F32) for p in params for _ in range(4)]
        + [jax.ShapeDtypeStruct((1, HEAD_LANES), F32)],
        in_specs=[_full(parts_all.shape)] + [_full(t.shape) for t in flat],
        out_specs=[_full(p[0].shape) for p in params for _ in range(4)] + [_full((1, HEAD_LANES))],
        compiler_params=_params(1),
    )(parts_all, *flat)
    return [res[4 * j:4 * j + 4] for j in range(k)], res[4 * k]


def _rot(t):
    half = t.shape[-1] // 2
    return jnp.concatenate([-t[..., half:], t[..., :half]], axis=-1)


def _rot_t(g):
    half = g.shape[-1] // 2
    return jnp.concatenate([g[..., half:], -g[..., :half]], axis=-1)


def _columns(segments, lo, hi):
    out, at = [], 0
    for seg in segments:
        n = seg.shape[1]
        a, b = max(lo, at), min(hi, at + n)
        if a < b:
            out.append(seg[:, a - at:b - at])
        at += n
    return out


def _prepare_weights(w_in_blocks, w_uq, w_ukv):
    o = [0]
    for s in IN_SPLITS:
        o.append(o[-1] + s)
    part = lambda a, b: _columns(w_in_blocks, a, b)
    dup = lambda a: part(a, a + 64) * 2 + part(a + 64, a + 128) * 2
    wa = jnp.concatenate(part(0, o[2]) + part(o[3], o[5]) + dup(o[5]) + dup(o[6]) + part(o[7], o[8]), axis=1)
    kr = jnp.concatenate(part(o[2], o[3]), axis=1)
    zc = lambda n: jnp.zeros((kr.shape[0], n), kr.dtype)
    wkr2 = jnp.concatenate([zc(64), kr, zc(32), zc(64), _rot(kr), zc(32)], axis=1)
    uq = w_uq.reshape(Q_LORA, N_HEADS, MLA_NOPE + MLA_ROPE)
    zq = jnp.zeros((Q_LORA, N_HEADS, 32), w_uq.dtype)
    uq_full = jnp.concatenate([uq, zq], axis=-1).reshape(Q_LORA, 1024)
    uq_rot = jnp.concatenate([jnp.zeros((Q_LORA, N_HEADS, 64), w_uq.dtype), _rot(uq[..., MLA_NOPE:]), zq],
                             axis=-1).reshape(Q_LORA, 1024)
    wq2 = jnp.concatenate([uq_full, uq_rot], axis=1)
    ukv = w_ukv.reshape(KV_LORA, N_HEADS, 128)
    k_full = jnp.concatenate([ukv[..., :64], jnp.zeros((KV_LORA, N_HEADS, 64), w_ukv.dtype)], axis=-1).reshape(KV_LORA, 1024)
    wkv = jnp.concatenate([k_full, ukv[..., 64:].reshape(KV_LORA, 512)], axis=1)
    return wa, wkr2, wq2, wkv


def _restore_grads(gwa, gwkr2, gwq2, gwkv):
    fold = lambda g: jnp.concatenate([g[:, 0:64] + g[:, 64:128], g[:, 128:192] + g[:, 192:256]], axis=1)
    gkr = gwkr2[:, 64:96] + _rot_t(gwkr2[:, 192:224])
    in_order = [gwa[:, :A_GM], gkr, gwa[:, A_GM:A_KD], fold(gwa[:, A_KD:A_VD]), fold(gwa[:, A_VD:A_GS]), gwa[:, A_GS:]]
    n = D_IN // 4
    g_in = [jnp.concatenate(_columns(in_order, k * n, (k + 1) * n), axis=1) for k in range(4)]
    gf = gwq2[:, :1024].reshape(Q_LORA, N_HEADS, 128)
    gr = gwq2[:, 1024:].reshape(Q_LORA, N_HEADS, 128)
    g_uq = jnp.concatenate([gf[..., :64], gf[..., 64:96] + _rot_t(gr[..., 64:96])], axis=-1).reshape(Q_LORA, 768)
    gk = gwkv[:, :1024].reshape(KV_LORA, N_HEADS, 128)[..., :64]
    gv = gwkv[:, 1024:].reshape(KV_LORA, N_HEADS, 64)
    g_ukv = jnp.concatenate([gk, gv], axis=-1).reshape(KV_LORA, 1024)
    return g_in, g_uq, g_ukv


def _local_step(x, positions, target, mod_rows, b_ada, ng, qg, kvg, sinks, fg, w_in_b, w_uq_b, w_ukv_b, w_out_b):
    n_seq, seq, _ = x.shape
    n_tok = n_seq * seq
    x2d = x.reshape(n_tok, D_MODEL)
    t2d = target.reshape(n_tok, D_MODEL)
    pos_f = positions.astype(F32)
    pos_col = pos_f.reshape(n_tok, 1)
    pos_row = pos_f.reshape(n_tok // SWA_WINDOW, 1, SWA_WINDOW)
    mod3 = mod_rows.reshape(n_seq, 1, 3 * D_MODEL)
    inv = ROPE_THETA ** (-jnp.arange(0, MLA_ROPE, 2, dtype=F32) / MLA_ROPE)
    inv128 = jnp.concatenate([jnp.zeros((64,), F32), inv, inv, jnp.zeros((32,), F32)]).reshape(1, 128)
    fg2 = fg.reshape(1, D_MODEL)

    wa, wkr2, wq2, wkv = _prepare_weights(w_in_b, w_uq_b, w_ukv_b)

    zqkv, gates, qf, kf, v, qs, kd, vd, rope = _pre_call(x2d, pos_col, mod3, b_ada, ng, qg, kvg, inv128, wa, wkr2, wq2, wkv, seq)
    o_mla, lse_mla = _mla_fwd_call(qf, kf, v, n_seq, seq)
    o_swa, lse_swa = _swa_fwd_call(qs, kd, vd, pos_col, pos_row, sinks, n_seq, seq)
    dx2, do, dg, g_out, g_fg, dgate, loss = _post_call(x2d, t2d, o_mla, o_swa, gates, mod3, b_ada, fg2, w_out_b, seq)
    dqf, dkf, dv = _mla_bwd_call(qf, kf, v, do, o_mla, lse_mla, n_seq, seq)
    dqs, dkd, dvd, dsink = _swa_bwd_call(qs, kd, vd, do, o_swa, lse_swa, pos_col, pos_row, sinks, n_seq, seq)
    dz, dkr, g_wq2, g_wkv, g_qg, g_kvg = _mid_bwd_call(dqf, dkf, dv, zqkv, rope, qg, kvg, wq2, wkv, seq)
    gx, g_wa, g_wkr2, g_ng, dshift, dscale = _in_bwd_call(x2d, dx2, dz, dkr, dg, dqs, dkd, dvd, mod3, b_ada, ng,
                                                         wa, wkr2, seq)
    g_in, g_uq, g_ukv = _restore_grads(g_wa, g_wkr2, g_wq2, g_wkv)
    dmod = jnp.concatenate([dshift, dscale, dgate], axis=-1).reshape(n_seq, 3 * D_MODEL)
    small_row = jnp.concatenate([g_ng, g_fg, g_qg, g_kvg, jnp.pad(jnp.sum(dsink, axis=1).reshape(1, N_HEADS), ((0, 0), (0, 120))),
                                 loss, jnp.zeros((1, 128), F32)], axis=1)
    return gx.reshape(x.shape), (g_in, g_uq, g_ukv, g_out), small_row, dmod


def kernel(x, c, positions, w_ada, b_ada, norm_gain, w_in, q_norm_gain, kv_norm_gain, w_uq, w_ukv, swa_sinks, w_out, final_gain, loss_target, m_w_ada, m_b_ada, m_norm_gain, m_w_in, m_q_norm_gain, m_kv_norm_gain, m_w_uq, m_w_ukv, m_swa_sinks, m_w_out, m_final_gain, v_w_ada, v_b_ada, v_norm_gain, v_w_in, v_q_norm_gain, v_kv_norm_gain, v_w_uq, v_w_ukv, v_swa_sinks, v_w_out, v_final_gain):
    n_seq = x.shape[0]
    xi, yi, ci = lax.axis_index("x"), lax.axis_index("y"), lax.axis_index("c")
    dev = 4 * xi + 2 * yi + ci
    chip = 2 * xi + yi

    halves = lambda w: w.astype(BF16).reshape(2, w.shape[0] // 2, w.shape[1])
    c_blk = jnp.pad(c, ((0, ROWS_PER_DEVICE - n_seq), (0, 0)))
    act_all, pieces, f_in, f_uq, f_ukv, f_out = _comm_fwd_call(
        c_blk, w_ada[0], [halves(w_in[0]), halves(w_uq[0]), halves(w_ukv[0]), halves(w_out[0])])
    mine = lax.dynamic_slice_in_dim(pieces, dev * ROWS_PER_DEVICE, n_seq, axis=1)
    mod_rows = jnp.transpose(mine, (1, 0, 2)).reshape(n_seq, 3 * D_MODEL)
    cols = lambda t, r: jnp.transpose(t.reshape(4, r, -1), (1, 0, 2)).reshape(r, -1)
    w_in_blocks = [f_in[k].reshape(D_MODEL, -1) for k in range(4)]
    w_uq_b, w_ukv_b = cols(f_uq, Q_LORA), cols(f_ukv, KV_LORA)
    w_out_b = f_out.reshape(D_MODEL, D_MODEL)

    gx, (g_in_blocks, g_uq, g_ukv, g_out), small_row, dmod = _local_step(
        x, positions, loss_target, mod_rows, b_ada, norm_gain, q_norm_gain, kv_norm_gain, swa_sinks, final_gain,
        w_in_blocks, w_uq_b, w_ukv_b, w_out_b)

    by_owner = lambda g, n: jnp.transpose(g.reshape(g.shape[0], 4, n), (1, 0, 2)).reshape(4, 2, g.shape[0] // 2, n)
    grads = [jnp.stack(g_in_blocks).reshape(4, 2, D_MODEL // 2, -1), by_owner(g_uq, 192), by_owner(g_ukv, 256),
             g_out.reshape(4, 2, 128, D_MODEL)]
    part = jnp.concatenate([dmod, small_row, jnp.zeros((ROWS_PER_DEVICE - n_seq - 1, 3 * D_MODEL), F32)], axis=0)
    r_in, r_uq, r_ukv, r_out, parts_all = _comm_bwd_call(grads, part)
    g_in_s, g_uq_s = r_in.reshape(w_in.shape[1:]), r_uq.reshape(w_uq.shape[1:])
    g_ukv_s, g_out_s = r_ukv.reshape(w_ukv.shape[1:]), r_out.reshape(w_out.shape[1:])

    tr = lambda a: jnp.swapaxes(a[0], 0, 1)
    back = lambda ts: [jnp.swapaxes(t, 0, 1) for t in ts]
    d_in, nm_in, nv_in = back(_adam_call("adam_w_in", tr(w_in), g_in_s.T, tr(m_w_in), tr(v_w_in)))
    d_uq, nm_uq, nv_uq = back(_adam_call("adam_w_uq", tr(w_uq), g_uq_s.T, tr(m_w_uq), tr(v_w_uq)))
    d_ukv, nm_ukv, nv_ukv = _adam_call("adam_w_ukv", w_ukv[0], g_ukv_s, m_w_ukv[0], v_w_ukv[0])
    d_out, nm_out, nv_out = _adam_call("adam_w_out", w_out[0], g_out_s, m_w_out[0], v_w_out[0])
    dmod_cols = lax.dynamic_slice_in_dim(parts_all, chip * 768, 768, axis=1)
    g_ada, d_ada, nm_ada, nv_ada = _ada_bwd_call(act_all, dmod_cols, w_ada[0], m_w_ada[0], v_w_ada[0])

    row = lambda t: t.reshape(1, -1)
    small = {"b_ada": (b_ada, m_b_ada, v_b_ada), "norm_gain": (norm_gain, m_norm_gain, v_norm_gain),
             "q_norm_gain": (q_norm_gain, m_q_norm_gain, v_q_norm_gain),
             "kv_norm_gain": (kv_norm_gain, m_kv_norm_gain, v_kv_norm_gain),
             "swa_sinks": (swa_sinks, m_swa_sinks, v_swa_sinks),
             "final_gain": (row(final_gain), row(m_final_gain), row(v_final_gain))}
    res, loss_row = _small_call(parts_all, n_seq, [small[name] for name in SMALL_ORDER])
    res = dict(zip(SMALL_ORDER, res))
    res["final_gain"] = [t.reshape(-1) for t in res["final_gain"]]
    e = lambda t: t[None]
    big = {"w_ada": (e(g_ada), e(d_ada), e(nm_ada), e(nv_ada)), "w_in": (e(g_in_s), e(d_in), e(nm_in), e(nv_in)),
           "w_uq": (e(g_uq_s), e(d_uq), e(nm_uq), e(nv_uq)), "w_ukv": (e(g_ukv_s), e(d_ukv), e(nm_ukv), e(nv_ukv)),
           "w_out": (e(g_out_s), e(d_out), e(nm_out), e(nv_out))}
    order = ("w_ada", "b_ada", "norm_gain", "w_in", "q_norm_gain", "kv_norm_gain", "w_uq", "w_ukv", "swa_sinks", "w_out",
             "final_gain")
    pick = lambda kind: [(big[n] if n in big else res[n])[kind] for n in order]
    return (loss_row[0, 0], gx, *pick(0), *pick(1), *pick(2), *pick(3))
```

```python
import functools

import jax
import jax.numpy as jnp
from jax import lax
from jax.experimental import pallas as pl
from jax.experimental.pallas import tpu as pltpu

F32 = jnp.float32
BF16 = jnp.bfloat16

D_MODEL = 1024
Q_LORA = 384
KV_LORA = 256
N_HEADS = 8
MLA_NOPE = 64
MLA_ROPE = 32
HEAD_LANES = 128
HALF = 64
SWA_WINDOW = 128
EPS = 1e-6
ROPE_THETA = 10000.0
MLA_SCALE = (MLA_NOPE + MLA_ROPE) ** -0.5
LOG2E = 1.4426950408889634
LN2 = 0.6931471805599453
SWA_SCALE = 64 ** -0.5
NEG = -1e30

ADAM_LR = 0.001
ADAM_B1 = 0.9
ADAM_B2 = 0.999
ADAM_EPS = 1e-08
ADAM_WD = 0.01
ADAM_STEP = 10

A_ZQ, A_ZKV, A_GM, A_QS, A_KD, A_VD, A_GS, A_END = 0, 384, 640, 1152, 1664, 1920, 2176, 2688
IN_SPLITS = (384, 256, 32, 512, 512, 128, 128, 512)
D_IN = sum(IN_SPLITS)

TOKEN_TILE = 512
ATT_TILE = 256
VMEM_LIMIT = 56 * 1024 * 1024


def _dot(a, b):
    return jnp.dot(a, b, preferred_element_type=F32)


def _dot_nt(a, b):
    return lax.dot_general(a, b, (((1,), (1,)), ((), ())), preferred_element_type=F32)


def _dot_tn(a, b):
    return lax.dot_general(a, b, (((0,), (0,)), ((), ())), preferred_element_type=F32)


def _params(n_grid):
    return pltpu.CompilerParams(dimension_semantics=("arbitrary",) * n_grid, vmem_limit_bytes=VMEM_LIMIT)


def _full(shape):
    nd = len(shape)
    return pl.BlockSpec(shape, lambda *_: (0,) * nd, pipeline_mode=pl.Buffered(1))


def _sigmoid(g):
    return 1.0 / (1.0 + jnp.exp(-g))


SUB_TILE = 256


def _sub_tiles(tm):
    sub = min(SUB_TILE, tm)
    return [slice(s * sub, (s + 1) * sub) for s in range(tm // sub)]


MESH = pl.DeviceIdType.MESH
ROWS_PER_DEVICE = 8
VMEM_SPEC = pl.BlockSpec(memory_space=pltpu.VMEM)
ANY_SPEC = pl.BlockSpec(memory_space=pl.ANY)


def _position():
    x, y, c = lax.axis_index("x"), lax.axis_index("y"), lax.axis_index("c")
    sibling = (x, y, 1 - c)
    others = [(1 - x, y, c), (x, 1 - y, c), (1 - x, 1 - y, c)]
    return (x, y, c), 4 * x + 2 * y + c, 2 * x + y, sibling, others


def _rows_of(dev):
    return pl.ds(pl.multiple_of(dev * ROWS_PER_DEVICE, ROWS_PER_DEVICE), ROWS_PER_DEVICE)


def _all_to_all_rows(block_ref, table_ref, dev, me, send_sems, recv_sems):
    x, y, c = me
    waits = []
    for k in range(1, 8):
        peer = (1 - x if k & 4 else x, 1 - y if k & 2 else y, 1 - c if k & 1 else c)
        pltpu.make_async_remote_copy(src_ref=block_ref, dst_ref=table_ref.at[_rows_of(dev)], send_sem=send_sems.at[k - 1],
                                     recv_sem=recv_sems.at[k - 1], device_id=peer, device_id_type=MESH).start()
        waits.append(pltpu.make_async_remote_copy(
            src_ref=block_ref, dst_ref=table_ref.at[_rows_of(jnp.bitwise_xor(dev, k))], send_sem=send_sems.at[k - 1],
            recv_sem=recv_sems.at[k - 1], device_id=peer, device_id_type=MESH))
    return waits


def _comm_fwd_call(c_blk, w_ada, shards):
    n = len(shards)

    def body(c_ref, wada_ref, *refs):
        w_refs, act_ref, pieces_ref, full_refs = refs[:n], refs[n], refs[n + 1], refs[n + 2:2 * n + 2]
        c_all_ref = refs[2 * n + 2]
        c_send, c_recv, p_send, p_recv, w_send, w_recv, f_send, f_recv, loc_sem = refs[2 * n + 3:]
        me, dev, chip, sibling, others = _position()
        core = me[2]
        chip_of = [2 * p[0] + p[1] for p in others]

        local = [pltpu.make_async_copy(w_refs[i], full_refs[i].at[chip], loc_sem.at[i]) for i in range(n)]
        for cp in local:
            cp.start()

        def over_ici(i, j, src_chip):
            return pltpu.make_async_remote_copy(
                src_ref=w_refs[i].at[core], dst_ref=full_refs[i].at[src_chip, core], send_sem=w_send.at[3 * i + j],
                recv_sem=w_recv.at[3 * i + j], device_id=others[j], device_id_type=MESH)

        def to_sibling(i, j, half):
            return pltpu.make_async_remote_copy(
                src_ref=full_refs[i].at[chip_of[j], half], dst_ref=full_refs[i].at[chip_of[j], half],
                send_sem=f_send.at[3 * i + j], recv_sem=f_recv.at[3 * i + j], device_id=sibling, device_id_type=MESH)

        sent = [over_ici(i, j, chip) for i in range(n) for j in range(3)]
        for cp in sent:
            cp.start()

        c_all_ref[_rows_of(dev), :] = c_ref[...]
        c_waits = _all_to_all_rows(c_ref, c_all_ref, dev, me, c_send, c_recv)
        for cp in c_waits:
            cp.wait()
        cv = c_all_ref[...]
        act = cv * _sigmoid(cv)
        act_ref[...] = act
        pieces_ref[chip] = _dot(act.astype(BF16), wada_ref[...].astype(BF16))
        piece = lambda j, src_chip: pltpu.make_async_remote_copy(
            src_ref=pieces_ref.at[chip], dst_ref=pieces_ref.at[src_chip], send_sem=p_send.at[j], recv_sem=p_recv.at[j],
            device_id=others[j], device_id_type=MESH)
        for j in range(3):
            piece(j, chip).start()
        for j in range(3):
            piece(j, chip).wait_send()
            piece(j, chip_of[j]).wait_recv()

        for i in range(n):
            for j in range(3):
                over_ici(i, j, chip_of[j]).wait_recv()
                to_sibling(i, j, core).start()
        for i in range(n):
            for j in range(3):
                to_sibling(i, j, 1 - core).wait_recv()
                to_sibling(i, j, core).wait_send()
        for cp in sent:
            cp.wait_send()
        for cp in local:
            cp.wait()

    rows = 8 * ROWS_PER_DEVICE
    dma = pltpu.SemaphoreType.DMA
    return pl.pallas_call(
        body, name="comm_fwd",
        out_shape=[jax.ShapeDtypeStruct((rows, D_MODEL), F32), jax.ShapeDtypeStruct((4, rows, w_ada.shape[1]), F32)]
        + [jax.ShapeDtypeStruct((4,) + s.shape, s.dtype) for s in shards],
        in_specs=[VMEM_SPEC, VMEM_SPEC] + [ANY_SPEC] * n,
        out_specs=[VMEM_SPEC, VMEM_SPEC] + [ANY_SPEC] * n,
        scratch_shapes=[pltpu.VMEM((rows, D_MODEL), F32), dma((7,)), dma((7,)), dma((3,)), dma((3,)),
                        dma((3 * n,)), dma((3 * n,)), dma((3 * n,)), dma((3 * n,)), dma((n,))],
        compiler_params=pltpu.CompilerParams(vmem_limit_bytes=VMEM_LIMIT),
    )(c_blk, w_ada, *shards)


def _comm_bwd_call(grads, part):
    n = len(grads)

    def body(part_ref, *refs):
        g_refs, f_refs, parts_ref = refs[:n], refs[n:2 * n], refs[2 * n]
        scratch = refs[2 * n + 1:]
        a_refs, b_refs, p_refs, r_refs = (scratch[k * n:(k + 1) * n] for k in range(4))
        s_send, s_recv, d_send, d_recv, e_send, e_recv, h_send, h_recv, loc_sem = scratch[4 * n:]
        me, dev, chip, sibling, others = _position()
        core = me[2]
        chip_of = [2 * p[0] + p[1] for p in others]

        parts_ref[_rows_of(dev), :] = part_ref[...]
        s_waits = _all_to_all_rows(part_ref, parts_ref, dev, me, s_send, s_recv)

        mine = [pltpu.make_async_copy(g_refs[i].at[:, core], a_refs[i], loc_sem.at[i]) for i in range(n)]
        swap = [pltpu.make_async_remote_copy(src_ref=g_refs[i].at[:, 1 - core], dst_ref=b_refs[i], send_sem=d_send.at[i],
                                             recv_sem=d_recv.at[i], device_id=sibling, device_id_type=MESH) for i in range(n)]
        order = sorted(range(n), key=lambda i: g_refs[i].shape[2] * g_refs[i].shape[3])
        for i in order:
            mine[i].start()
            swap[i].start()
        cross = [pltpu.make_async_remote_copy(src_ref=p_refs[i].at[chip_of[j]], dst_ref=r_refs[i].at[j],
                                              send_sem=e_send.at[3 * i + j], recv_sem=e_recv.at[3 * i + j],
                                              device_id=others[j], device_id_type=MESH) for i in range(n) for j in range(3)]
        for i in order:
            mine[i].wait()
            swap[i].wait()
            for k in range(4):
                s = a_refs[i][k] + b_refs[i][k]
                a_refs[i][k] = s
                p_refs[i][k] = s.astype(BF16)
            for j in range(3):
                cross[3 * i + j].start()
        share = {}
        for i in order:
            for j in range(3):
                cross[3 * i + j].wait()
            f_refs[i][core] = (a_refs[i][chip] + r_refs[i][0].astype(F32) + r_refs[i][1].astype(F32)
                               + r_refs[i][2].astype(F32))
            share[i] = pltpu.make_async_remote_copy(src_ref=f_refs[i].at[core], dst_ref=f_refs[i].at[core],
                                                    send_sem=h_send.at[i], recv_sem=h_recv.at[i], device_id=sibling,
                                                    device_id_type=MESH)
            share[i].start()
        for i in range(n):
            share[i].wait_send()
            pltpu.make_async_remote_copy(src_ref=f_refs[i].at[core], dst_ref=f_refs[i].at[1 - core], send_sem=h_send.at[i],
                                         recv_sem=h_recv.at[i], device_id=sibling, device_id_type=MESH).wait_recv()
        for cp in s_waits:
            cp.wait()

    rows = 8 * ROWS_PER_DEVICE
    dma = pltpu.SemaphoreType.DMA
    quarter = [(4,) + g.shape[2:] for g in grads]
    return pl.pallas_call(
        body, name="comm_bwd",
        out_shape=[jax.ShapeDtypeStruct((2,) + g.shape[2:], F32) for g in grads]
        + [jax.ShapeDtypeStruct((rows, part.shape[1]), F32)],
        in_specs=[VMEM_SPEC] + [ANY_SPEC] * n,
        out_specs=[VMEM_SPEC] * (n + 1),
        scratch_shapes=[pltpu.VMEM(q, F32) for q in quarter] + [pltpu.VMEM(q, F32) for q in quarter]
        + [pltpu.VMEM(q, BF16) for q in quarter] + [pltpu.VMEM((3,) + q[1:], BF16) for q in quarter]
        + [dma((7,)), dma((7,)), dma((n,)), dma((n,)), dma((3 * n,)), dma((3 * n,)), dma((n,)), dma((n,)), dma((n,))],
        compiler_params=pltpu.CompilerParams(vmem_limit_bytes=VMEM_LIMIT),
    )(part, *grads)


def _rope_tables(pos_col, inv_row):
    ang = pos_col * inv_row
    return jnp.cos(ang), jnp.sin(ang)


def _pre_call(x, pos_col, mod, b_ada, ng, qg, kvg, inv128, wa, wkr2, wq2, wkv, seq):
    n_tok = x.shape[0]
    tm = min(TOKEN_TILE, seq)
    per_seq = seq // tm

    def body(x_ref, pos_ref, mod_ref, bada_ref, ng_ref, qg_ref, kvg_ref, inv_ref, wa_ref, wkr_ref, wq_ref, wkv_ref,
             zqkv_ref, gates_ref, qf_ref, kf_ref, v_ref, qs_ref, kd_ref, vd_ref, rope_ref):
        xv = x_ref[...]
        modv = mod_ref[0] + bada_ref[...]
        shift, scale = modv[:, :D_MODEL], modv[:, D_MODEL:2 * D_MODEL]
        r1 = lax.rsqrt(jnp.mean(xv * xv, axis=-1, keepdims=True) + EPS)
        h = ((xv * r1) * ng_ref[...]) * (1.0 + scale) + shift
        hb = h.astype(BF16)
        za = _dot(hb, wa_ref[...])
        zkr = _dot(hb, wkr_ref[...])
        cos, sin = _rope_tables(pos_ref[...], inv_ref[...])
        rope_ref[:, :HEAD_LANES] = cos
        rope_ref[:, HEAD_LANES:] = sin
        zqkv_ref[...] = za[:, :A_GM]
        gates_ref[:, :512] = za[:, A_GM:A_QS]
        gates_ref[:, 512:] = za[:, A_GS:A_END]
        qs_ref[...] = (za[:, A_QS:A_KD] * (SWA_SCALE * LOG2E)).astype(BF16)
        kd_ref[...] = za[:, A_KD:A_VD].astype(BF16)
        vd_ref[...] = za[:, A_VD:A_GS].astype(BF16)
        zq, zkv = za[:, A_ZQ:A_ZKV], za[:, A_ZKV:A_GM]
        rq = lax.rsqrt(jnp.mean(zq * zq, axis=-1, keepdims=True) + EPS)
        qn = ((zq * rq) * qg_ref[...]).astype(BF16)
        qr = _dot(qn, wq_ref[...])
        cf, sf = jnp.tile(cos, (1, N_HEADS)), jnp.tile(sin, (1, N_HEADS))
        qf_ref[...] = ((qr[:, :1024] * cf + qr[:, 1024:] * sf) * (MLA_SCALE * LOG2E)).astype(BF16)
        rkv = lax.rsqrt(jnp.mean(zkv * zkv, axis=-1, keepdims=True) + EPS)
        kvn = ((zkv * rkv) * kvg_ref[...]).astype(BF16)
        kv = _dot(kvn, wkv_ref[...])
        kpe = zkr[:, :128] * cos + zkr[:, 128:] * sin
        kf_ref[...] = (kv[:, :1024] + jnp.tile(kpe, (1, N_HEADS))).astype(BF16)
        v_ref[...] = kv[:, 1024:].astype(BF16)

    tok = lambda w: pl.BlockSpec((tm, w), lambda i: (i, 0))
    outs = [(640, F32), (1024, F32), (1024, BF16), (1024, BF16), (512, BF16), (512, BF16), (256, BF16), (256, BF16),
            (2 * HEAD_LANES, F32)]
    return pl.pallas_call(
        body, name="pre", grid=(n_tok // tm,),
        out_shape=[jax.ShapeDtypeStruct((n_tok, w), dt) for w, dt in outs],
        in_specs=[tok(D_MODEL), tok(1), pl.BlockSpec((1, 1, 3 * D_MODEL), lambda i: (i // per_seq, 0, 0)),
                  _full(b_ada.shape), _full(ng.shape), _full(qg.shape), _full(kvg.shape), _full(inv128.shape),
                  _full(wa.shape), _full(wkr2.shape), _full(wq2.shape), _full(wkv.shape)],
        out_specs=[tok(w) for w, _ in outs],
        compiler_params=_params(1),
    )(x, pos_col, mod, b_ada, ng, qg, kvg, inv128, wa, wkr2, wq2, wkv)


def _lane_lo(width=HEAD_LANES):
    return lax.broadcasted_iota(jnp.int32, (1, width), 1) < HALF


def _eye(n=HEAD_LANES):
    r = lax.broadcasted_iota(jnp.int32, (n, n), 0)
    c = lax.broadcasted_iota(jnp.int32, (n, n), 1)
    return jnp.where(r == c, 1.0, 0.0).astype(BF16)


def _mla_fwd_call(qf, kf, v, n_seq, seq):
    tq = min(ATT_TILE, seq)
    nq = seq // tq

    ext = HALF + 16

    def body(q_ref, k_ref, v_ref, o_ref, lse_ref, vt_ref):
        i = pl.program_id(1)
        eye = _eye()

        @pl.when(i == 0)
        def _():
            for h in range(N_HEADS):
                vt_ref[h * ext + HALF:(h + 1) * ext, :] = jnp.ones((16, seq), BF16)
            for t in range(nq):
                for p in range(N_HEADS // 2):
                    pair = slice(p * HEAD_LANES, (p + 1) * HEAD_LANES)
                    v_t = _dot_nt(eye, v_ref[t * tq:(t + 1) * tq, pair]).astype(BF16)
                    for hh in range(2):
                        r0 = (2 * p + hh) * ext
                        vt_ref[r0:r0 + HALF, t * tq:(t + 1) * tq] = v_t[hh * HALF:(hh + 1) * HALF, :]

        q = q_ref[...]
        qcol = i * tq + lax.broadcasted_iota(jnp.int32, (1, tq), 1)
        heads = range(N_HEADS)
        lanes = [slice(h * HEAD_LANES, (h + 1) * HEAD_LANES) for h in heads]

        def make_step(masked, n_tiles):
            def step(kt0, carry):
                tiles = range(n_tiles)
                start = pl.multiple_of(kt0 * tq, tq)
                ks = [k_ref[pl.ds(pl.multiple_of((kt0 + t) * tq, tq), tq), :] for t in tiles]
                vt = vt_ref[:, pl.ds(start, n_tiles * tq)]
                last = n_tiles - 1
                if masked:
                    keep = ((kt0 + last) * tq + lax.broadcasted_iota(jnp.int32, (tq, 1), 0)) <= qcol

                def scores(h):
                    sts = [_dot_nt(ks[t][:, lanes[h]], q[:, lanes[h]]) for t in tiles]
                    if masked:
                        sts[last] = jnp.where(keep, sts[last], NEG)
                    return sts

                def softmax(h, sts):
                    m_old = carry[2 * h]
                    m_new = m_old
                    for st in sts:
                        m_new = jnp.maximum(m_new, jnp.max(st, axis=0, keepdims=True))
                    pt = jnp.concatenate([jnp.exp2(st - m_new).astype(BF16) for st in sts], axis=0)
                    return m_new, jnp.exp2(m_old - m_new), pt

                def values(h, alpha, pt):
                    return carry[2 * h + 1] * alpha + _dot(vt[h * ext:(h + 1) * ext, :], pt)

                sts, soft, out = {0: scores(0), 1: scores(1)}, {}, {}
                for h in range(N_HEADS + 1):
                    if h + 2 < N_HEADS:
                        sts[h + 2] = scores(h + 2)
                    if h < N_HEADS:
                        soft[h] = softmax(h, sts.pop(h))
                    if h >= 1:
                        m_new, alpha, pt = soft.pop(h - 1)
                        out[h - 1] = (m_new, values(h - 1, alpha, pt))
                return tuple(v for h in heads for v in out[h])
            return step

        init = (jnp.full((1, tq), NEG, F32), jnp.zeros((ext, tq), F32)) * N_HEADS
        count = i + 1
        carry = lax.fori_loop(0, (count + 1) // 2 - 1, lambda j, c: make_step(False, 2)(2 * j, c), init)
        carry = lax.cond(count % 2 == 0, lambda c: make_step(True, 2)(i - 1, c), lambda c: make_step(True, 1)(i, c), carry)
        dens = [carry[2 * h + 1][HALF:HALF + 1, :] for h in heads]
        acc_t = jnp.concatenate([carry[2 * h + 1][:HALF, :] * (1.0 / dens[h]) for h in heads], axis=0)
        o_ref[...] = acc_t.T
        for h in heads:
            lse_ref[0, h // 4, h % 4:h % 4 + 1, :] = carry[2 * h] + jnp.log2(dens[h])

    n_tok = qf.shape[0]
    return pl.pallas_call(
        body, name="mla_fwd", grid=(n_seq, nq),
        out_shape=[jax.ShapeDtypeStruct((n_tok, 512), F32), jax.ShapeDtypeStruct((n_seq, 2, 4, seq), F32)],
        in_specs=[pl.BlockSpec((tq, 1024), lambda b, i: (b * nq + i, 0)),
                  pl.BlockSpec((seq, 1024), lambda b, i: (b, 0)),
                  pl.BlockSpec((seq, 512), lambda b, i: (b, 0))],
        out_specs=[pl.BlockSpec((tq, 512), lambda b, i: (b * nq + i, 0)),
                   pl.BlockSpec((1, 2, 4, tq), lambda b, i: (b, 0, 0, i))],
        scratch_shapes=[pltpu.VMEM((N_HEADS * ext, seq), BF16)],
        compiler_params=_params(2),
    )(qf, kf, v)


def _mla_bwd_call(qf, kf, v, do, o, lse, n_seq, seq):
    tq = min(ATT_TILE, seq)
    nq = seq // tq

    nh = 4
    heads = range(nh)
    lanes = [slice(h * HEAD_LANES, (h + 1) * HEAD_LANES) for h in heads]

    def body(q_ref, k_ref, v_ref, do_ref, o_ref, lse_ref, dq_ref, dk_ref, dv_ref,
             kt_ref, dot_ref, delta_ref, dqt_ref):
        eye = _eye()
        lo = _lane_lo()
        sub_lo = lax.broadcasted_iota(jnp.int32, (HEAD_LANES, 1), 0) < HALF
        ones_lo = jnp.where(jnp.broadcast_to(lo, (8, HEAD_LANES)), 1.0, 0.0).astype(BF16)
        ones_hi = jnp.where(jnp.broadcast_to(lo, (8, HEAD_LANES)), 0.0, 1.0).astype(BF16)

        for t in range(nq):
            r = slice(t * tq, (t + 1) * tq)
            kv = k_ref[r, :]
            for h in heads:
                kt_ref[lanes[h], r] = _dot_nt(eye, kv[:, lanes[h]]).astype(BF16)
            for p in range(nh // 2):
                dov = do_ref[r, lanes[p]]
                dt = _dot_nt(eye, dov)
                dot_ref[2 * p, :, r] = jnp.where(sub_lo, dt, 0.0).astype(BF16)
                dot_ref[2 * p + 1, :, r] = jnp.where(sub_lo, 0.0, dt).astype(BF16)
                prod = dov.astype(F32) * o_ref[r, lanes[p]]
                p_hi = prod.astype(BF16)
                p_lo = (prod - p_hi.astype(F32)).astype(BF16)
                delta_ref[2 * p, :, r] = _dot_nt(ones_lo, p_hi) + _dot_nt(ones_lo, p_lo)
                delta_ref[2 * p + 1, :, r] = _dot_nt(ones_hi, p_hi) + _dot_nt(ones_hi, p_lo)
        dqt_ref[...] = jnp.zeros_like(dqt_ref)

        def k_step(kt, _):
            kr = pl.ds(pl.multiple_of(kt * tq, tq), tq)
            k = k_ref[kr, :]
            vv = v_ref[kr, :]
            k_t = kt_ref[:, kr]
            krow = kt * tq + lax.broadcasted_iota(jnp.int32, (tq, 1), 0)

            def make_step(masked, n_tiles):
                def q_step(qt0, carry):
                    tiles = range(n_tiles)
                    qrs = [pl.ds(pl.multiple_of((qt0 + t) * tq, tq), tq) for t in tiles]
                    qs = [q_ref[qr, :] for qr in qrs]
                    if masked:
                        keep = krow <= (qt0 * tq + lax.broadcasted_iota(jnp.int32, (1, tq), 1))

                    def scores(h):
                        do_ts = [dot_ref[h, :, qr] for qr in qrs]
                        sts = [_dot_nt(k[:, lanes[h]], qs[t][:, lanes[h]]) for t in tiles]
                        dpts = [_dot(vv[:, lanes[h // 2]], do_ts[t]) for t in tiles]
                        return do_ts, sts, dpts

                    def softmax(h, sts, dpts):
                        pts, dsts = [], []
                        for t in tiles:
                            pt = jnp.exp2(sts[t] - lse_ref[0, 0, h:h + 1, qrs[t]])
                            if masked and t == 0:
                                pt = jnp.where(keep, pt, 0.0)
                            dsts.append((pt * (dpts[t] - delta_ref[h, 0:1, qrs[t]])).astype(BF16))
                            pts.append(pt.astype(BF16))
                        return pts, dsts

                    def grads(h, do_ts, pts, dsts):
                        half = slice((h % 2) * HALF, (h % 2 + 1) * HALF)
                        dst_all = jnp.concatenate(dsts, axis=1)
                        pt_all = jnp.concatenate(pts, axis=1)
                        do_all = jnp.concatenate([do_ts[t][half, :] for t in tiles], axis=1)
                        q_all = jnp.concatenate([qs[t][:, lanes[h]] for t in tiles], axis=0)
                        dvt = _dot_nt(do_all, pt_all)
                        dk = _dot(dst_all, q_all)
                        for t in tiles:
                            dqt_ref[lanes[h], qrs[t]] += _dot(k_t[lanes[h], :], dsts[t])
                        return carry[2 * h] + dk, carry[2 * h + 1] + dvt

                    first, second, out = {0: scores(0)}, {}, {}
                    for h in range(nh + 1):
                        if h + 1 < nh:
                            first[h + 1] = scores(h + 1)
                        if h < nh:
                            do_ts, sts, dpts = first.pop(h)
                            second[h] = (do_ts,) + softmax(h, sts, dpts)
                        if h >= 1:
                            out[h - 1] = grads(h - 1, *second.pop(h - 1))
                    return tuple(v for h in heads for v in out[h])
                return q_step

            init = (jnp.zeros((tq, HEAD_LANES), F32), jnp.zeros((HALF, tq), F32)) * nh
            count = nq - kt
            carry = lax.cond(count >= 2, lambda c: make_step(True, 2)(kt, c), lambda c: make_step(True, 1)(kt, c), init)
            carry = lax.fori_loop(1, count // 2, lambda j, c: make_step(False, 2)(kt + 2 * j, c), carry)
            carry = lax.cond(jnp.logical_and(count % 2 == 1, count >= 3),
                             lambda c: make_step(False, 1)(nq - 1, c), lambda c: c, carry)
            for h in heads:
                dk_ref[kr, lanes[h]] = carry[2 * h]
            for p in range(nh // 2):
                dv_ref[kr, lanes[p]] = jnp.concatenate([carry[4 * p + 1], carry[4 * p + 3]], axis=0).T
            return 0

        lax.fori_loop(0, nq, k_step, 0)
        for t in range(nq):
            r = slice(t * tq, (t + 1) * tq)
            for h in heads:
                dq_ref[r, lanes[h]] = dqt_ref[lanes[h], r].T

    n_tok = qf.shape[0]
    groups = N_HEADS // nh
    blk = lambda w: pl.BlockSpec((seq, w), lambda b, g: (b, g))
    return pl.pallas_call(
        body, name="mla_bwd", grid=(n_seq, groups),
        out_shape=[jax.ShapeDtypeStruct((n_tok, 1024), F32), jax.ShapeDtypeStruct((n_tok, 1024), F32),
                   jax.ShapeDtypeStruct((n_tok, 512), F32)],
        in_specs=[blk(512), blk(512), blk(256), blk(256), blk(256),
                  pl.BlockSpec((1, 1, nh, seq), lambda b, g: (b, g, 0, 0))],
        out_specs=[blk(512), blk(512), blk(256)],
        scratch_shapes=[pltpu.VMEM((nh * HEAD_LANES, seq), BF16), pltpu.VMEM((nh, HEAD_LANES, seq), BF16),
                        pltpu.VMEM((nh, 8, seq), F32), pltpu.VMEM((nh * HEAD_LANES, seq), F32)],
        compiler_params=_params(2),
    )(qf, kf, v, do, o, lse)


SWA_BLOCKS = 4


def _swa_block(n, pos_col_ref, posq):
    w = SWA_WINDOW
    start = pl.multiple_of(jnp.maximum(n - 1, 0) * w, w)
    posk = pos_col_ref[pl.ds(start, 2 * w), :]
    rel = (n * w + lax.broadcasted_iota(jnp.int32, (1, w), 1)) - (start + lax.broadcasted_iota(jnp.int32, (2 * w, 1), 0))
    valid = jnp.logical_and(rel >= 0, rel < w)
    return start, jnp.where(valid, posq - posk, 1e30)


def _alibi(h):
    return LOG2E * 2.0 ** -(h + 1)


def _transpose_rows(eye, src_ref, dst_ref, seq, width):
    step = 2 * SWA_WINDOW
    for t in range(seq // step):
        for p in range(width // HEAD_LANES):
            lanes = slice(p * HEAD_LANES, (p + 1) * HEAD_LANES)
            dst_ref[lanes, t * step:(t + 1) * step] = _dot_nt(eye, src_ref[t * step:(t + 1) * step, lanes]).astype(BF16)


def _swa_fwd_call(qs, kd, vd, pos_col, pos_row, sinks, n_seq, seq):
    w = SWA_WINDOW
    qb = SWA_BLOCKS
    steps = seq // (qb * w)
    ext = HALF + 16

    def body(q_ref, k_ref, v_ref, pc_ref, pr_ref, sink_ref, o_ref, lse_ref, vt_ref):
        n = pl.program_id(1)
        lo = _lane_lo()
        hi = jnp.logical_not(lo)
        eye = _eye()

        @pl.when(n == 0)
        def _():
            step = 2 * w
            for kv in range(2):
                vt_ref[kv * ext + HALF:(kv + 1) * ext, :] = jnp.ones((16, seq), BF16)
                for t in range(seq // step):
                    v_t = _dot_nt(eye, v_ref[t * step:(t + 1) * step, kv * HEAD_LANES:(kv + 1) * HEAD_LANES])
                    vt_ref[kv * ext:kv * ext + HALF, t * step:(t + 1) * step] = v_t[:HALF, :].astype(BF16)

        heads = range(N_HEADS)
        blocks = range(qb)
        geo = [_swa_block(n * qb + bi, pc_ref, pr_ref[bi]) for bi in blocks]
        wins = [pl.ds(g[0], 2 * w) for g in geo]
        kwins = [k_ref[win, :] for win in wins]
        vts = [vt_ref[:, win] for win in wins]
        sts = []
        for bi in blocks:
            q = q_ref[bi * w:(bi + 1) * w, :]
            sts.append([])
            for h in heads:
                qp = q[:, (h // 2) * HEAD_LANES:(h // 2 + 1) * HEAD_LANES]
                qh = jnp.where(lo if h % 2 == 0 else hi, qp, jnp.zeros_like(qp))
                sts[bi].append(_dot_nt(kwins[bi][:, (h // 4) * HEAD_LANES:(h // 4 + 1) * HEAD_LANES], qh))
        ps, ms = [], []
        for bi in blocks:
            ps.append([])
            ms.append([])
            for h in heads:
                s = sts[bi][h] - _alibi(h) * geo[bi][1]
                m = jnp.maximum(jnp.max(s, axis=0, keepdims=True), sink_ref[0, h] * LOG2E)
                ps[bi].append(jnp.exp2(s - m).astype(BF16))
                ms[bi].append(m)
        for bi in blocks:
            ots = []
            for h in heads:
                pv = _dot(vts[bi][(h // 4) * ext:(h // 4 + 1) * ext, :], ps[bi][h])
                l = pv[HALF:HALF + 1, :] + jnp.exp2(sink_ref[0, h] * LOG2E - ms[bi][h])
                ots.append(pv[:HALF, :] * (1.0 / l))
                lse_ref[0, h:h + 1, bi * w:(bi + 1) * w] = ms[bi][h] + jnp.log2(l)
            o_ref[bi * w:(bi + 1) * w, :] = jnp.concatenate(ots, axis=0).T

    n_tok = qs.shape[0]
    tok = lambda width: pl.BlockSpec((qb * w, width), lambda b, n: (b * steps + n, 0))
    whole = lambda width: pl.BlockSpec((seq, width), lambda b, n: (b, 0))
    return pl.pallas_call(
        body, name="swa_fwd", grid=(n_seq, steps),
        out_shape=[jax.ShapeDtypeStruct((n_tok, 512), F32), jax.ShapeDtypeStruct((n_seq, N_HEADS, seq), F32)],
        in_specs=[tok(512), whole(256), whole(256), whole(1), pl.BlockSpec((qb, 1, w), lambda b, n: (b * steps + n, 0, 0)),
                  pl.BlockSpec(memory_space=pltpu.SMEM)],
        out_specs=[tok(512), pl.BlockSpec((1, N_HEADS, qb * w), lambda b, n: (b, 0, n))],
        scratch_shapes=[pltpu.VMEM((2 * ext, seq), BF16)],
        compiler_params=_params(2),
    )(qs, kd, vd, pos_col, pos_row, sinks)


def _swa_bwd_call(qs, kd, vd, do, o, lse, pos_col, pos_row, sinks, n_seq, seq):
    w = SWA_WINDOW
    qb = SWA_BLOCKS
    steps = seq // (qb * w)

    def body(q_ref, k_ref, v_ref, do_ref, o_ref, lse_ref, pc_ref, pr_ref, sink_ref, dq_ref, dk_ref, dv_ref, dsink_ref,
             kt_ref):
        b, n = pl.program_id(0), pl.program_id(1)
        lo = _lane_lo()
        hi = jnp.logical_not(lo)
        sub_lo = lax.broadcasted_iota(jnp.int32, (HEAD_LANES, 1), 0) < HALF
        eye = _eye()
        ones_lo = jnp.where(jnp.broadcast_to(lo, (8, HEAD_LANES)), 1.0, 0.0).astype(BF16)
        ones_hi = jnp.where(jnp.broadcast_to(lo, (8, HEAD_LANES)), 0.0, 1.0).astype(BF16)

        @pl.when(n == 0)
        def _():
            dk_ref[...] = jnp.zeros_like(dk_ref)
            dv_ref[...] = jnp.zeros_like(dv_ref)
            _transpose_rows(eye, k_ref, kt_ref, seq, 2 * HEAD_LANES)

        @pl.when(jnp.logical_and(n == 0, b == 0))
        def _():
            dsink_ref[...] = jnp.zeros_like(dsink_ref)

        heads = range(N_HEADS)
        blocks = range(qb)
        kv_lanes = lambda h: slice((h // 4) * HEAD_LANES, (h // 4 + 1) * HEAD_LANES)
        geo = [_swa_block(n * qb + bi, pc_ref, pr_ref[bi]) for bi in blocks]
        wins = [pl.ds(g[0], 2 * w) for g in geo]
        kwins = [k_ref[win, :] for win in wins]
        vwins = [v_ref[win, :] for win in wins]

        do_ts, deltas, qms, doms = [], [], [], []
        for bi in blocks:
            rows = slice(bi * w, (bi + 1) * w)
            for lst in (do_ts, deltas, qms, doms):
                lst.append([])
            for j in range(N_HEADS // 2):
                pair = slice(j * HEAD_LANES, (j + 1) * HEAD_LANES)
                dop = do_ref[rows, pair]
                qp = q_ref[rows, pair]
                dt = _dot_nt(eye, dop)
                prod = dop.astype(F32) * o_ref[rows, pair]
                p_hi = prod.astype(BF16)
                p_lo = (prod - p_hi.astype(F32)).astype(BF16)
                for hh in range(2):
                    half, ones = (lo, ones_lo) if hh == 0 else (hi, ones_hi)
                    do_ts[bi].append(jnp.where(sub_lo, dt, 0.0).astype(BF16) if hh == 0
                                     else jnp.where(sub_lo, 0.0, dt).astype(BF16))
                    deltas[bi].append((_dot_nt(ones, p_hi) + _dot_nt(ones, p_lo))[0:1, :])
                    qms[bi].append(jnp.where(half, qp, jnp.zeros_like(qp)))
                    doms[bi].append(jnp.where(half, dop, jnp.zeros_like(dop)))
        sts = [[_dot_nt(kwins[bi][:, kv_lanes(h)], qms[bi][h]) for h in heads] for bi in blocks]
        dpts = [[_dot(vwins[bi][:, kv_lanes(h)], do_ts[bi][h]) for h in heads] for bi in blocks]
        pts, dsts = [], []
        for bi in blocks:
            pts.append([])
            dsts.append([])
            for h in heads:
                lse_h = lse_ref[0, h:h + 1, bi * w:(bi + 1) * w]
                pt = jnp.exp2(sts[bi][h] - _alibi(h) * geo[bi][1] - lse_h)
                dsts[bi].append((pt * (dpts[bi][h] - deltas[bi][h])).astype(BF16))
                pts[bi].append(pt.astype(BF16))
                dsink_ref[h:h + 1, :] += -jnp.exp2(sink_ref[0, h] * LOG2E - lse_h) * deltas[bi][h]
        for bi in blocks:
            for kv in range(2):
                group = range(4 * kv, 4 * kv + 4)
                dst_all = jnp.concatenate([dsts[bi][h] for h in group], axis=1)
                pt_all = jnp.concatenate([pts[bi][h] for h in group], axis=1)
                q_all = jnp.concatenate([qms[bi][h] for h in group], axis=0)
                do_all = jnp.concatenate([doms[bi][h] for h in group], axis=0)
                dk_ref[wins[bi], kv_lanes(4 * kv)] += _dot(dst_all, q_all)
                dv_ref[wins[bi], kv_lanes(4 * kv)] += _dot(pt_all, do_all)
        for bi in blocks:
            ktw = kt_ref[:, wins[bi]]
            for j in range(N_HEADS // 2):
                k_t = ktw[kv_lanes(2 * j), :]
                dq_t = jnp.where(sub_lo, _dot(k_t, dsts[bi][2 * j]), _dot(k_t, dsts[bi][2 * j + 1]))
                dq_ref[bi * w:(bi + 1) * w, j * HEAD_LANES:(j + 1) * HEAD_LANES] = dq_t.T * SWA_SCALE

    n_tok = qs.shape[0]
    tok = lambda width: pl.BlockSpec((qb * w, width), lambda b, n: (b * steps + n, 0))
    whole = lambda width: pl.BlockSpec((seq, width), lambda b, n: (b, 0))
    return pl.pallas_call(
        body, name="swa_bwd", grid=(n_seq, steps),
        out_shape=[jax.ShapeDtypeStruct((n_tok, 512), F32), jax.ShapeDtypeStruct((n_tok, 256), F32),
                   jax.ShapeDtypeStruct((n_tok, 256), F32), jax.ShapeDtypeStruct((N_HEADS, HEAD_LANES), F32)],
        in_specs=[tok(512), whole(256), whole(256), pl.BlockSpec((qb * w, 512), lambda b, n: (b * steps + n, 1)), tok(512),
                  pl.BlockSpec((1, N_HEADS, qb * w), lambda b, n: (b, 0, n)),
                  whole(1), pl.BlockSpec((qb, 1, w), lambda b, n: (b * steps + n, 0, 0)),
                  pl.BlockSpec(memory_space=pltpu.SMEM)],
        out_specs=[tok(512), whole(256), whole(256), _full((N_HEADS, HEAD_LANES))],
        scratch_shapes=[pltpu.VMEM((2 * HEAD_LANES, seq), BF16)],
        compiler_params=_params(2),
    )(qs, kd, vd, do, o, lse, pos_col, pos_row, sinks)


def _post_call(x, target, o_mla, o_swa, gates, mod, b_ada, fg, w_out, seq):
    n_tok = x.shape[0]
    tm = min(TOKEN_TILE, seq)
    per_seq = seq // tm
    n_seq = n_tok // seq

    def body(x_ref, t_ref, om_ref, os_ref, g_ref, mod_ref, bada_ref, fg_ref, w_ref,
             dx2_ref, do_ref, dg_ref, gw_ref, gfg_ref, dgate_ref, loss_ref):
        i = pl.program_id(0)

        @pl.when(i == 0)
        def _():
            gw_ref[...] = jnp.zeros_like(gw_ref)
            gfg_ref[...] = jnp.zeros_like(gfg_ref)
            loss_ref[...] = jnp.zeros_like(loss_ref)

        @pl.when(i % per_seq == 0)
        def _():
            dgate_ref[...] = jnp.zeros_like(dgate_ref)

        gate = mod_ref[0][:, 2 * D_MODEL:] + bada_ref[:, 2 * D_MODEL:]
        fgv = fg_ref[...]
        subs = _sub_tiles(tm)
        gs = [g_ref[r, :] for r in subs]
        os_ = [jnp.concatenate([om_ref[r, :], os_ref[r, :]], axis=-1) for r in subs]
        sgs = [_sigmoid(g) for g in gs]
        sils = [g * sg for g, sg in zip(gs, sgs)]
        ypres = [(o * sil).astype(BF16) for o, sil in zip(os_, sils)]
        ys = [_dot(ypre, w_ref[...]) for ypre in ypres]
        dys, loss, gfg, dgate = [], 0.0, 0.0, 0.0
        for r, y in zip(subs, ys):
            x2 = x_ref[r, :] + gate * y
            r2 = lax.rsqrt(jnp.mean(x2 * x2, axis=-1, keepdims=True) + EPS)
            xn2 = x2 * r2
            err = xn2 * fgv - t_ref[r, :]
            loss = loss + jnp.sum(jnp.sum(err * err, axis=-1, keepdims=True), axis=0, keepdims=True)
            dout = err * (1.0 / D_MODEL)
            gfg = gfg + jnp.sum(dout * xn2, axis=0, keepdims=True)
            dxn2 = dout * fgv
            dx2 = r2 * (dxn2 - xn2 * jnp.mean(dxn2 * xn2, axis=-1, keepdims=True))
            dx2_ref[r, :] = dx2
            dgate = dgate + jnp.sum(dx2 * y, axis=0, keepdims=True)
            dys.append((dx2 * gate).astype(BF16))
        loss_ref[...] += jnp.broadcast_to(loss * (0.5 / D_MODEL), loss_ref.shape)
        gfg_ref[...] += gfg
        dgate_ref[0] += dgate
        gw_ref[...] += _dot_tn(jnp.concatenate(ypres, axis=0), jnp.concatenate(dys, axis=0))
        dypres = [_dot_nt(dy, w_ref[...]) for dy in dys]
        for r, dypre, o, g, sg, sil in zip(subs, dypres, os_, gs, sgs, sils):
            do_ref[r, :] = (dypre * sil).astype(BF16)
            dg_ref[r, :] = (dypre * o * (sg * (1.0 + g * (1.0 - sg)))).astype(BF16)

    tok = lambda w: pl.BlockSpec((tm, w), lambda i: (i, 0))
    per_b = pl.BlockSpec((1, 1, 3 * D_MODEL), lambda i: (i // per_seq, 0, 0))
    return pl.pallas_call(
        body, name="post", grid=(n_tok // tm,),
        out_shape=[jax.ShapeDtypeStruct((n_tok, D_MODEL), F32), jax.ShapeDtypeStruct((n_tok, D_MODEL), BF16),
                   jax.ShapeDtypeStruct((n_tok, D_MODEL), BF16), jax.ShapeDtypeStruct((D_MODEL, D_MODEL), F32),
                   jax.ShapeDtypeStruct((1, D_MODEL), F32), jax.ShapeDtypeStruct((n_seq, 1, D_MODEL), F32),
                   jax.ShapeDtypeStruct((1, HEAD_LANES), F32)],
        in_specs=[tok(D_MODEL), tok(D_MODEL), tok(512), tok(512), tok(D_MODEL), per_b, _full(b_ada.shape),
                  _full(fg.shape), _full(w_out.shape)],
        out_specs=[tok(D_MODEL), tok(D_MODEL), tok(D_MODEL), _full((D_MODEL, D_MODEL)), _full((1, D_MODEL)),
                   pl.BlockSpec((1, 1, D_MODEL), lambda i: (i // per_seq, 0, 0)), _full((1, HEAD_LANES))],
        compiler_params=_params(1),
    )(x, target, o_mla, o_swa, gates, mod, b_ada, fg, w_out)


def _mid_bwd_call(dqf, dkf, dv, zqkv, rope, qg, kvg, wq2, wkv, seq):
    n_tok = dqf.shape[0]
    tm = min(TOKEN_TILE, seq)

    def body(dq_ref, dk_ref, dv_ref, z_ref, rope_ref, qg_ref, kvg_ref, wq_ref, wkv_ref,
             dz_ref, dkr_ref, gwq_ref, gwkv_ref, gqg_ref, gkvg_ref):
        i = pl.program_id(0)

        @pl.when(i == 0)
        def _():
            gwq_ref[...] = jnp.zeros_like(gwq_ref)
            gwkv_ref[...] = jnp.zeros_like(gwkv_ref)
            gqg_ref[...] = jnp.zeros_like(gqg_ref)
            gkvg_ref[...] = jnp.zeros_like(gkvg_ref)

        cos, sin = rope_ref[:, :HEAD_LANES], rope_ref[:, HEAD_LANES:]
        cf, sf = jnp.tile(cos, (1, N_HEADS)), jnp.tile(sin, (1, N_HEADS))
        dq = dq_ref[...] * MLA_SCALE
        dqr = jnp.concatenate([dq * cf, dq * sf], axis=-1).astype(BF16)
        zq, zkv = z_ref[:, :Q_LORA], z_ref[:, Q_LORA:]
        qgv, kvgv = qg_ref[...], kvg_ref[...]

        rq = lax.rsqrt(jnp.mean(zq * zq, axis=-1, keepdims=True) + EPS)
        xq = zq * rq
        gwq_ref[...] += _dot_tn((xq * qgv).astype(BF16), dqr)
        dqn = _dot_nt(dqr, wq_ref[...])
        gqg_ref[...] += jnp.sum(dqn * xq, axis=0, keepdims=True)
        dxq = dqn * qgv
        dz_ref[:, :Q_LORA] = (rq * (dxq - xq * jnp.mean(dxq * xq, axis=-1, keepdims=True))).astype(BF16)

        dk = dk_ref[...] * LN2
        dkv = jnp.concatenate([dk, dv_ref[...]], axis=-1).astype(BF16)
        rkv = lax.rsqrt(jnp.mean(zkv * zkv, axis=-1, keepdims=True) + EPS)
        xkv = zkv * rkv
        gwkv_ref[...] += _dot_tn((xkv * kvgv).astype(BF16), dkv)
        dkvn = _dot_nt(dkv, wkv_ref[...])
        gkvg_ref[...] += jnp.sum(dkvn * xkv, axis=0, keepdims=True)
        dxkv = dkvn * kvgv
        dz_ref[:, Q_LORA:] = (rkv * (dxkv - xkv * jnp.mean(dxkv * xkv, axis=-1, keepdims=True))).astype(BF16)

        dkpe = dk[:, :HEAD_LANES]
        for h in range(1, N_HEADS):
            dkpe = dkpe + dk[:, h * HEAD_LANES:(h + 1) * HEAD_LANES]
        dkr_ref[:, :HEAD_LANES] = (dkpe * cos).astype(BF16)
        dkr_ref[:, HEAD_LANES:] = (dkpe * sin).astype(BF16)

    tok = lambda w: pl.BlockSpec((tm, w), lambda i: (i, 0))
    return pl.pallas_call(
        body, name="mid_bwd", grid=(n_tok // tm,),
        out_shape=[jax.ShapeDtypeStruct((n_tok, 640), BF16), jax.ShapeDtypeStruct((n_tok, 256), BF16),
                   jax.ShapeDtypeStruct(wq2.shape, F32), jax.ShapeDtypeStruct(wkv.shape, F32),
                   jax.ShapeDtypeStruct((1, Q_LORA), F32), jax.ShapeDtypeStruct((1, KV_LORA), F32)],
        in_specs=[tok(1024), tok(1024), tok(512), tok(640), tok(2 * HEAD_LANES), _full(qg.shape), _full(kvg.shape),
                  _full(wq2.shape), _full(wkv.shape)],
        out_specs=[tok(640), tok(256), _full(wq2.shape), _full(wkv.shape), _full((1, Q_LORA)), _full((1, KV_LORA))],
        compiler_params=_params(1),
    )(dqf, dkf, dv, zqkv, rope, qg, kvg, wq2, wkv)


def _in_bwd_call(x, dx2, dz, dkr, dg, dqs, dkd, dvd, mod, b_ada, ng, wa, wkr2, seq):
    n_tok = x.shape[0]
    tm = min(TOKEN_TILE, seq)
    per_seq = seq // tm
    n_seq = n_tok // seq

    def body(x_ref, dx2_ref, dz_ref, dkr_ref, dg_ref, dqs_ref, dkd_ref, dvd_ref, mod_ref, bada_ref, ng_ref,
             wa_ref, wkr_ref, gx_ref, gwa_ref, gwkr_ref, gng_ref, dshift_ref, dscale_ref):
        i = pl.program_id(0)

        @pl.when(i == 0)
        def _():
            gwa_ref[...] = jnp.zeros_like(gwa_ref)
            gwkr_ref[...] = jnp.zeros_like(gwkr_ref)
            gng_ref[...] = jnp.zeros_like(gng_ref)

        @pl.when(i % per_seq == 0)
        def _():
            dshift_ref[...] = jnp.zeros_like(dshift_ref)
            dscale_ref[...] = jnp.zeros_like(dscale_ref)

        xv = x_ref[...]
        modv = mod_ref[0] + bada_ref[...]
        shift, scale = modv[:, :D_MODEL], modv[:, D_MODEL:2 * D_MODEL]
        ngv = ng_ref[...]
        r1 = lax.rsqrt(jnp.mean(xv * xv, axis=-1, keepdims=True) + EPS)
        xn = xv * r1
        hb = ((xn * ngv) * (1.0 + scale) + shift).astype(BF16)

        dgv = dg_ref[...]
        pieces = [(A_ZQ, dz_ref[...]), (A_GM, dgv[:, :512]), (A_QS, dqs_ref[...].astype(BF16)),
                  (A_KD, (dkd_ref[...] * LN2).astype(BF16)),
                  (A_VD, dvd_ref[...].astype(BF16)), (A_GS, dgv[:, 512:])]
        dkr = dkr_ref[...]
        gwkr_ref[...] += _dot_tn(hb, dkr)
        dh = _dot_nt(dkr, wkr_ref[...])
        for off, piece in pieces:
            wd = piece.shape[1]
            gwa_ref[:, off:off + wd] += _dot_tn(hb, piece)
            dh = dh + _dot_nt(piece, wa_ref[:, off:off + wd])

        dshift_ref[0] += jnp.sum(dh, axis=0, keepdims=True)
        dscale_ref[0] += jnp.sum(dh * (xn * ngv), axis=0, keepdims=True)
        gng_ref[...] += jnp.sum(dh * xn * (1.0 + scale), axis=0, keepdims=True)
        dxn = dh * ngv * (1.0 + scale)
        gx_ref[...] = dx2_ref[...] + r1 * (dxn - xn * jnp.mean(dxn * xn, axis=-1, keepdims=True))

    tok = lambda w: pl.BlockSpec((tm, w), lambda i: (i, 0))
    per_b = lambda w: pl.BlockSpec((1, 1, w), lambda i: (i // per_seq, 0, 0))
    return pl.pallas_call(
        body, name="in_bwd", grid=(n_tok // tm,),
        out_shape=[jax.ShapeDtypeStruct((n_tok, D_MODEL), F32), jax.ShapeDtypeStruct((D_MODEL, A_END), F32),
                   jax.ShapeDtypeStruct((D_MODEL, 256), F32), jax.ShapeDtypeStruct((1, D_MODEL), F32),
                   jax.ShapeDtypeStruct((n_seq, 1, D_MODEL), F32), jax.ShapeDtypeStruct((n_seq, 1, D_MODEL), F32)],
        in_specs=[tok(D_MODEL), tok(D_MODEL), tok(640), tok(256), tok(D_MODEL), tok(512), tok(256), tok(256),
                  per_b(3 * D_MODEL), _full(b_ada.shape), _full(ng.shape), _full(wa.shape), _full(wkr2.shape)],
        out_specs=[tok(D_MODEL), _full((D_MODEL, A_END)), _full((D_MODEL, 256)), _full((1, D_MODEL)),
                   per_b(D_MODEL), per_b(D_MODEL)],
        compiler_params=_params(1),
    )(x, dx2, dz, dkr, dg, dqs, dkd, dvd, mod, b_ada, ng, wa, wkr2)


def _adam_math(w, g, m, v):
    m_new = ADAM_B1 * m + (1.0 - ADAM_B1) * g
    v_new = ADAM_B2 * v + (1.0 - ADAM_B2) * (g * g)
    m_hat = m_new / (1.0 - ADAM_B1 ** ADAM_STEP)
    v_hat = v_new / (1.0 - ADAM_B2 ** ADAM_STEP)
    delta = -ADAM_LR * (m_hat / (jnp.sqrt(v_hat) + ADAM_EPS) + ADAM_WD * w)
    return delta, m_new, v_new


def _adam_call(name, w, g, m, v):
    rows, cols = w.shape
    tr = next((t for t in (256, 128, 88) if rows % t == 0), rows)

    def body(w_ref, g_ref, m_ref, v_ref, d_ref, mo_ref, vo_ref):
        d, mn, vn = _adam_math(w_ref[...], g_ref[...], m_ref[...], v_ref[...])
        d_ref[...] = d
        mo_ref[...] = mn
        vo_ref[...] = vn

    spec = pl.BlockSpec((tr, cols), lambda i: (i, 0))
    return pl.pallas_call(
        body, name=name, grid=(rows // tr,),
        out_shape=[jax.ShapeDtypeStruct(w.shape, F32)] * 3,
        in_specs=[spec] * 4, out_specs=[spec] * 3,
        compiler_params=_params(1),
    )(w, g, m, v)


def _ada_bwd_call(act_all, dmod_cols, w, m, v):
    rows, cols = w.shape
    tr = 256

    def body(a_ref, dm_ref, w_ref, m_ref, v_ref, g_ref, d_ref, mo_ref, vo_ref):
        g = _dot_tn(a_ref[...].astype(BF16), dm_ref[...].astype(BF16))
        d, mn, vn = _adam_math(w_ref[...], g, m_ref[...], v_ref[...])
        g_ref[...] = g
        d_ref[...] = d
        mo_ref[...] = mn
        vo_ref[...] = vn

    spec = pl.BlockSpec((tr, cols), lambda i: (i, 0))
    nb = act_all.shape[0]
    return pl.pallas_call(
        body, name="ada_bwd", grid=(rows // tr,),
        out_shape=[jax.ShapeDtypeStruct(w.shape, F32)] * 4,
        in_specs=[pl.BlockSpec((nb, tr), lambda i: (0, i)), _full(dmod_cols.shape), spec, spec, spec],
        out_specs=[spec] * 4,
        compiler_params=_params(1),
    )(act_all, dmod_cols, w, m, v)


SMALL_ROW = {"norm_gain": (0, 1024), "final_gain": (1024, 2048), "q_norm_gain": (2048, 2432),
             "kv_norm_gain": (2432, 2688), "swa_sinks": (2688, 2696), "loss": (2816, 2944)}
SMALL_ORDER = ("b_ada", "norm_gain", "q_norm_gain", "kv_norm_gain", "swa_sinks", "final_gain")


def _small_call(parts_all, n_seq, params):
    k = len(params)

    def body(p_ref, *refs):
        ins, outs, loss_ref = refs[:3 * k], refs[3 * k:7 * k], refs[7 * k]
        row = p_ref[n_seq:n_seq + 1, :]
        for dv in range(1, 8):
            r0 = dv * ROWS_PER_DEVICE + n_seq
            row = row + p_ref[r0:r0 + 1, :]
        gb = None
        for dv in range(8):
            for r in range(n_seq):
                r0 = dv * ROWS_PER_DEVICE + r
                gb = p_ref[r0:r0 + 1, :] if gb is None else gb + p_ref[r0:r0 + 1, :]
        for j, name in enumerate(SMALL_ORDER):
            g = gb if name == "b_ada" else row[:, SMALL_ROW[name][0]:SMALL_ROW[name][1]]
            d, mn, vn = _adam_math(ins[3 * j][...], g, ins[3 * j + 1][...], ins[3 * j + 2][...])
            outs[4 * j][...] = g
            outs[4 * j + 1][...] = d
            outs[4 * j + 2][...] = mn
            outs[4 * j + 3][...] = vn
        loss_ref[...] = row[:, SMALL_ROW["loss"][0]:SMALL_ROW["loss"][1]]

    flat = [t for p in params for t in p]
    res = pl.pallas_call(
        body, name="small_update", grid=(1,),
        out_shape=[jax.ShapeDtypeStruct(p[0].shape, F32) for p in params for _ in range(4)]
        + [jax.ShapeDtypeStruct((1, HEAD_LANES), F32)],
        in_specs=[_full(parts_all.shape)] + [_full(t.shape) for t in flat],
        out_specs=[_full(p[0].shape) for p in params for _ in range(4)] + [_full((1, HEAD_LANES))],
        compiler_params=_params(1),
    )(parts_all, *flat)
    return [res[4 * j:4 * j + 4] for j in range(k)], res[4 * k]


def _rot(t):
    half = t.shape[-1] // 2
    return jnp.concatenate([-t[..., half:], t[..., :half]], axis=-1)


def _rot_t(g):
    half = g.shape[-1] // 2
    return jnp.concatenate([g[..., half:], -g[..., :half]], axis=-1)


def _columns(segments, lo, hi):
    out, at = [], 0
    for seg in segments:
        n = seg.shape[1]
        a, b = max(lo, at), min(hi, at + n)
        if a < b:
            out.append(seg[:, a - at:b - at])
        at += n
    return out


def _prepare_weights(w_in_blocks, w_uq, w_ukv):
    o = [0]
    for s in IN_SPLITS:
        o.append(o[-1] + s)
    part = lambda a, b: _columns(w_in_blocks, a, b)
    dup = lambda a: part(a, a + 64) * 2 + part(a + 64, a + 128) * 2
    wa = jnp.concatenate(part(0, o[2]) + part(o[3], o[5]) + dup(o[5]) + dup(o[6]) + part(o[7], o[8]), axis=1)
    kr = jnp.concatenate(part(o[2], o[3]), axis=1)
    zc = lambda n: jnp.zeros((kr.shape[0], n), kr.dtype)
    wkr2 = jnp.concatenate([zc(64), kr, zc(32), zc(64), _rot(kr), zc(32)], axis=1)
    uq = w_uq.reshape(Q_LORA, N_HEADS, MLA_NOPE + MLA_ROPE)
    zq = jnp.zeros((Q_LORA, N_HEADS, 32), w_uq.dtype)
    uq_full = jnp.concatenate([uq, zq], axis=-1).reshape(Q_LORA, 1024)
    uq_rot = jnp.concatenate([jnp.zeros((Q_LORA, N_HEADS, 64), w_uq.dtype), _rot(uq[..., MLA_NOPE:]), zq],
                             axis=-1).reshape(Q_LORA, 1024)
    wq2 = jnp.concatenate([uq_full, uq_rot], axis=1)
    ukv = w_ukv.reshape(KV_LORA, N_HEADS, 128)
    k_full = jnp.concatenate([ukv[..., :64], jnp.zeros((KV_LORA, N_HEADS, 64), w_ukv.dtype)], axis=-1).reshape(KV_LORA, 1024)
    wkv = jnp.concatenate([k_full, ukv[..., 64:].reshape(KV_LORA, 512)], axis=1)
    return wa, wkr2, wq2, wkv


def _restore_grads(gwa, gwkr2, gwq2, gwkv):
    fold = lambda g: jnp.concatenate([g[:, 0:64] + g[:, 64:128], g[:, 128:192] + g[:, 192:256]], axis=1)
    gkr = gwkr2[:, 64:96] + _rot_t(gwkr2[:, 192:224])
    in_order = [gwa[:, :A_GM], gkr, gwa[:, A_GM:A_KD], fold(gwa[:, A_KD:A_VD]), fold(gwa[:, A_VD:A_GS]), gwa[:, A_GS:]]
    n = D_IN // 4
    g_in = [jnp.concatenate(_columns(in_order, k * n, (k + 1) * n), axis=1) for k in range(4)]
    gf = gwq2[:, :1024].reshape(Q_LORA, N_HEADS, 128)
    gr = gwq2[:, 1024:].reshape(Q_LORA, N_HEADS, 128)
    g_uq = jnp.concatenate([gf[..., :64], gf[..., 64:96] + _rot_t(gr[..., 64:96])], axis=-1).reshape(Q_LORA, 768)
    gk = gwkv[:, :1024].reshape(KV_LORA, N_HEADS, 128)[..., :64]
    gv = gwkv[:, 1024:].reshape(KV_LORA, N_HEADS, 64)
    g_ukv = jnp.concatenate([gk, gv], axis=-1).reshape(KV_LORA, 1024)
    return g_in, g_uq, g_ukv


def _local_step(x, positions, target, mod_rows, b_ada, ng, qg, kvg, sinks, fg, w_in_b, w_uq_b, w_ukv_b, w_out_b):
    n_seq, seq, _ = x.shape
    n_tok = n_seq * seq
    x2d = x.reshape(n_tok, D_MODEL)
    t2d = target.reshape(n_tok, D_MODEL)
    pos_f = positions.astype(F32)
    pos_col = pos_f.reshape(n_tok, 1)
    pos_row = pos_f.reshape(n_tok // SWA_WINDOW, 1, SWA_WINDOW)
    mod3 = mod_rows.reshape(n_seq, 1, 3 * D_MODEL)
    inv = ROPE_THETA ** (-jnp.arange(0, MLA_ROPE, 2, dtype=F32) / MLA_ROPE)
    inv128 = jnp.concatenate([jnp.zeros((64,), F32), inv, inv, jnp.zeros((32,), F32)]).reshape(1, 128)
    fg2 = fg.reshape(1, D_MODEL)

    wa, wkr2, wq2, wkv = _prepare_weights(w_in_b, w_uq_b, w_ukv_b)

    zqkv, gates, qf, kf, v, qs, kd, vd, rope = _pre_call(x2d, pos_col, mod3, b_ada, ng, qg, kvg, inv128, wa, wkr2, wq2, wkv, seq)
    o_mla, lse_mla = _mla_fwd_call(qf, kf, v, n_seq, seq)
    o_swa, lse_swa = _swa_fwd_call(qs, kd, vd, pos_col, pos_row, sinks, n_seq, seq)
    dx2, do, dg, g_out, g_fg, dgate, loss = _post_call(x2d, t2d, o_mla, o_swa, gates, mod3, b_ada, fg2, w_out_b, seq)
    dqf, dkf, dv = _mla_bwd_call(qf, kf, v, do, o_mla, lse_mla, n_seq, seq)
    dqs, dkd, dvd, dsink = _swa_bwd_call(qs, kd, vd, do, o_swa, lse_swa, pos_col, pos_row, sinks, n_seq, seq)
    dz, dkr, g_wq2, g_wkv, g_qg, g_kvg = _mid_bwd_call(dqf, dkf, dv, zqkv, rope, qg, kvg, wq2, wkv, seq)
    gx, g_wa, g_wkr2, g_ng, dshift, dscale = _in_bwd_call(x2d, dx2, dz, dkr, dg, dqs, dkd, dvd, mod3, b_ada, ng,
                                                         wa, wkr2, seq)
    g_in, g_uq, g_ukv = _restore_grads(g_wa, g_wkr2, g_wq2, g_wkv)
    dmod = jnp.concatenate([dshift, dscale, dgate], axis=-1).reshape(n_seq, 3 * D_MODEL)
    small_row = jnp.concatenate([g_ng, g_fg, g_qg, g_kvg, jnp.pad(jnp.sum(dsink, axis=1).reshape(1, N_HEADS), ((0, 0), (0, 120))),
                                 loss, jnp.zeros((1, 128), F32)], axis=1)
    return gx.reshape(x.shape), (g_in, g_uq, g_ukv, g_out), small_row, dmod


def kernel(x, c, positions, w_ada, b_ada, norm_gain, w_in, q_norm_gain, kv_norm_gain, w_uq, w_ukv, swa_sinks, w_out, final_gain, loss_target, m_w_ada, m_b_ada, m_norm_gain, m_w_in, m_q_norm_gain, m_kv_norm_gain, m_w_uq, m_w_ukv, m_swa_sinks, m_w_out, m_final_gain, v_w_ada, v_b_ada, v_norm_gain, v_w_in, v_q_norm_gain, v_kv_norm_gain, v_w_uq, v_w_ukv, v_swa_sinks, v_w_out, v_final_gain):
    n_seq = x.shape[0]
    xi, yi, ci = lax.axis_index("x"), lax.axis_index("y"), lax.axis_index("c")
    dev = 4 * xi + 2 * yi + ci
    chip = 2 * xi + yi

    halves = lambda w: w.astype(BF16).reshape(2, w.shape[0] // 2, w.shape[1])
    c_blk = jnp.pad(c, ((0, ROWS_PER_DEVICE - n_seq), (0, 0)))
    act_all, pieces, f_in, f_uq, f_ukv, f_out = _comm_fwd_call(
        c_blk, w_ada[0], [halves(w_in[0]), halves(w_uq[0]), halves(w_ukv[0]), halves(w_out[0])])
    mine = lax.dynamic_slice_in_dim(pieces, dev * ROWS_PER_DEVICE, n_seq, axis=1)
    mod_rows = jnp.transpose(mine, (1, 0, 2)).reshape(n_seq, 3 * D_MODEL)
    cols = lambda t, r: jnp.transpose(t.reshape(4, r, -1), (1, 0, 2)).reshape(r, -1)
    w_in_blocks = [f_in[k].reshape(D_MODEL, -1) for k in range(4)]
    w_uq_b, w_ukv_b = cols(f_uq, Q_LORA), cols(f_ukv, KV_LORA)
    w_out_b = f_out.reshape(D_MODEL, D_MODEL)

    gx, (g_in_blocks, g_uq, g_ukv, g_out), small_row, dmod = _local_step(
        x, positions, loss_target, mod_rows, b_ada, norm_gain, q_norm_gain, kv_norm_gain, swa_sinks, final_gain,
        w_in_blocks, w_uq_b, w_ukv_b, w_out_b)

    by_owner = lambda g, n: jnp.transpose(g.reshape(g.shape[0], 4, n), (1, 0, 2)).reshape(4, 2, g.shape[0] // 2, n)
    grads = [jnp.stack(g_in_blocks).reshape(4, 2, D_MODEL // 2, -1), by_owner(g_uq, 192), by_owner(g_ukv, 256),
             g_out.reshape(4, 2, 128, D_MODEL)]
    part = jnp.concatenate([dmod, small_row, jnp.zeros((ROWS_PER_DEVICE - n_seq - 1, 3 * D_MODEL), F32)], axis=0)
    r_in, r_uq, r_ukv, r_out, parts_all = _comm_bwd_call(grads, part)
    g_in_s, g_uq_s = r_in.reshape(w_in.shape[1:]), r_uq.reshape(w_uq.shape[1:])
    g_ukv_s, g_out_s = r_ukv.reshape(w_ukv.shape[1:]), r_out.reshape(w_out.shape[1:])

    tr = lambda a: jnp.swapaxes(a[0], 0, 1)
    back = lambda ts: [jnp.swapaxes(t, 0, 1) for t in ts]
    d_in, nm_in, nv_in = back(_adam_call("adam_w_in", tr(w_in), g_in_s.T, tr(m_w_in), tr(v_w_in)))
    d_uq, nm_uq, nv_uq = back(_adam_call("adam_w_uq", tr(w_uq), g_uq_s.T, tr(m_w_uq), tr(v_w_uq)))
    d_ukv, nm_ukv, nv_ukv = _adam_call("adam_w_ukv", w_ukv[0], g_ukv_s, m_w_ukv[0], v_w_ukv[0])
    d_out, nm_out, nv_out = _adam_call("adam_w_out", w_out[0], g_out_s, m_w_out[0], v_w_out[0])
    dmod_cols = lax.dynamic_slice_in_dim(parts_all, chip * 768, 768, axis=1)
    g_ada, d_ada, nm_ada, nv_ada = _ada_bwd_call(act_all, dmod_cols, w_ada[0], m_w_ada[0], v_w_ada[0])

    row = lambda t: t.reshape(1, -1)
    small = {"b_ada": (b_ada, m_b_ada, v_b_ada), "norm_gain": (norm_gain, m_norm_gain, v_norm_gain),
             "q_norm_gain": (q_norm_gain, m_q_norm_gain, v_q_norm_gain),
             "kv_norm_gain": (kv_norm_gain, m_kv_norm_gain, v_kv_norm_gain),
             "swa_sinks": (swa_sinks, m_swa_sinks, v_swa_sinks),
             "final_gain": (row(final_gain), row(m_final_gain), row(v_final_gain))}
    res, loss_row = _small_call(parts_all, n_seq, [small[name] for name in SMALL_ORDER])
    res = dict(zip(SMALL_ORDER, res))
    res["final_gain"] = [t.reshape(-1) for t in res["final_gain"]]
    e = lambda t: t[None]
    big = {"w_ada": (e(g_ada), e(d_ada), e(nm_ada), e(nv_ada)), "w_in": (e(g_in_s), e(d_in), e(nm_in), e(nv_in)),
           "w_uq": (e(g_uq_s), e(d_uq), e(nm_uq), e(nv_uq)), "w_ukv": (e(g_ukv_s), e(d_ukv), e(nm_ukv), e(nv_ukv)),
           "w_out": (e(g_out_s), e(d_out), e(nm_out), e(nv_out))}
    order = ("w_ada", "b_ada", "norm_gain", "w_in", "q_norm_gain", "kv_norm_gain", "w_uq", "w_ukv", "swa_sinks", "w_out",
             "final_gain")
    pick = lambda kind: [(big[n] if n in big else res[n])[kind] for n in order]
    return (loss_row[0, 0], gx, *pick(0), *pick(1), *pick(2), *pick(3))
```

```python
import functools

import jax
import jax.numpy as jnp
from jax import lax
from jax.experimental import pallas as pl
from jax.experimental.pallas import tpu as pltpu

F32 = jnp.float32
BF16 = jnp.bfloat16

D_MODEL = 1024
Q_LORA = 384
KV_LORA = 256
N_HEADS = 8
MLA_NOPE = 64
MLA_ROPE = 32
HEAD_LANES = 128
HALF = 64
SWA_WINDOW = 128
EPS = 1e-6
ROPE_THETA = 10000.0
MLA_SCALE = (MLA_NOPE + MLA_ROPE) ** -0.5
LOG2E = 1.4426950408889634
LN2 = 0.6931471805599453
SWA_SCALE = 64 ** -0.5
NEG = -1e30

ADAM_LR = 0.001
ADAM_B1 = 0.9
ADAM_B2 = 0.999
ADAM_EPS = 1e-08
ADAM_WD = 0.01
ADAM_STEP = 10

A_ZQ, A_ZKV, A_GM, A_QS, A_KS, A_VS, A_GS, A_END = 0, 384, 640, 1152, 1664, 1792, 1920, 2432
IN_SPLITS = (384, 256, 32, 512, 512, 128, 128, 512)
D_IN = sum(IN_SPLITS)

TOKEN_TILE = 512
ATT_TILE = 256
VMEM_LIMIT = 56 * 1024 * 1024


def _dot(a, b):
    return jnp.dot(a, b, preferred_element_type=F32)


def _dot_nt(a, b):
    return lax.dot_general(a, b, (((1,), (1,)), ((), ())), preferred_element_type=F32)


def _dot_tn(a, b):
    return lax.dot_general(a, b, (((0,), (0,)), ((), ())), preferred_element_type=F32)


def _params(n_grid):
    return pltpu.CompilerParams(dimension_semantics=("arbitrary",) * n_grid, vmem_limit_bytes=VMEM_LIMIT)


def _full(shape):
    nd = len(shape)
    return pl.BlockSpec(shape, lambda *_: (0,) * nd, pipeline_mode=pl.Buffered(1))


def _sigmoid(g):
    return 1.0 / (1.0 + jnp.exp(-g))


SUB_TILE = 256


def _sub_tiles(tm):
    sub = min(SUB_TILE, tm)
    return [slice(s * sub, (s + 1) * sub) for s in range(tm // sub)]


MESH = pl.DeviceIdType.MESH
ROWS_PER_DEVICE = 8
VMEM_SPEC = pl.BlockSpec(memory_space=pltpu.VMEM)
ANY_SPEC = pl.BlockSpec(memory_space=pl.ANY)


def _position():
    x, y, c = lax.axis_index("x"), lax.axis_index("y"), lax.axis_index("c")
    sibling = (x, y, 1 - c)
    others = [(1 - x, y, c), (x, 1 - y, c), (1 - x, 1 - y, c)]
    return (x, y, c), 4 * x + 2 * y + c, 2 * x + y, sibling, others


def _rows_of(dev):
    return pl.ds(pl.multiple_of(dev * ROWS_PER_DEVICE, ROWS_PER_DEVICE), ROWS_PER_DEVICE)


def _all_to_all_rows(block_ref, table_ref, dev, me, send_sems, recv_sems):
    x, y, c = me
    waits = []
    for k in range(1, 8):
        peer = (1 - x if k & 4 else x, 1 - y if k & 2 else y, 1 - c if k & 1 else c)
        pltpu.make_async_remote_copy(src_ref=block_ref, dst_ref=table_ref.at[_rows_of(dev)], send_sem=send_sems.at[k - 1],
                                     recv_sem=recv_sems.at[k - 1], device_id=peer, device_id_type=MESH).start()
        waits.append(pltpu.make_async_remote_copy(
            src_ref=block_ref, dst_ref=table_ref.at[_rows_of(jnp.bitwise_xor(dev, k))], send_sem=send_sems.at[k - 1],
            recv_sem=recv_sems.at[k - 1], device_id=peer, device_id_type=MESH))
    return waits


def _comm_fwd_call(c_blk, w_ada, shards):
    n = len(shards)

    def body(c_ref, wada_ref, *refs):
        w_refs, act_ref, pieces_ref, full_refs = refs[:n], refs[n], refs[n + 1], refs[n + 2:2 * n + 2]
        c_all_ref = refs[2 * n + 2]
        c_send, c_recv, p_send, p_recv, w_send, w_recv, f_send, f_recv, loc_sem = refs[2 * n + 3:]
        me, dev, chip, sibling, others = _position()
        core = me[2]
        chip_of = [2 * p[0] + p[1] for p in others]

        local = [pltpu.make_async_copy(w_refs[i], full_refs[i].at[chip], loc_sem.at[i]) for i in range(n)]
        for cp in local:
            cp.start()

        def over_ici(i, j, src_chip):
            return pltpu.make_async_remote_copy(
                src_ref=w_refs[i].at[core], dst_ref=full_refs[i].at[src_chip, core], send_sem=w_send.at[3 * i + j],
                recv_sem=w_recv.at[3 * i + j], device_id=others[j], device_id_type=MESH)

        def to_sibling(i, j, half):
            return pltpu.make_async_remote_copy(
                src_ref=full_refs[i].at[chip_of[j], half], dst_ref=full_refs[i].at[chip_of[j], half],
                send_sem=f_send.at[3 * i + j], recv_sem=f_recv.at[3 * i + j], device_id=sibling, device_id_type=MESH)

        sent = [over_ici(i, j, chip) for i in range(n) for j in range(3)]
        for cp in sent:
            cp.start()

        c_all_ref[_rows_of(dev), :] = c_ref[...]
        c_waits = _all_to_all_rows(c_ref, c_all_ref, dev, me, c_send, c_recv)
        for cp in c_waits:
            cp.wait()
        cv = c_all_ref[...]
        act = cv * _sigmoid(cv)
        act_ref[...] = act
        pieces_ref[chip] = _dot(act.astype(BF16), wada_ref[...].astype(BF16))
        piece = lambda j, src_chip: pltpu.make_async_remote_copy(
            src_ref=pieces_ref.at[chip], dst_ref=pieces_ref.at[src_chip], send_sem=p_send.at[j], recv_sem=p_recv.at[j],
            device_id=others[j], device_id_type=MESH)
        for j in range(3):
            piece(j, chip).start()
        for j in range(3):
            piece(j, chip).wait_send()
            piece(j, chip_of[j]).wait_recv()

        for i in range(n):
            for j in range(3):
                over_ici(i, j, chip_of[j]).wait_recv()
                to_sibling(i, j, core).start()
        for i in range(n):
            for j in range(3):
                to_sibling(i, j, 1 - core).wait_recv()
                to_sibling(i, j, core).wait_send()
        for cp in sent:
            cp.wait_send()
        for cp in local:
            cp.wait()

    rows = 8 * ROWS_PER_DEVICE
    dma = pltpu.SemaphoreType.DMA
    return pl.pallas_call(
        body, name="comm_fwd",
        out_shape=[jax.ShapeDtypeStruct((rows, D_MODEL), F32), jax.ShapeDtypeStruct((4, rows, w_ada.shape[1]), F32)]
        + [jax.ShapeDtypeStruct((4,) + s.shape, s.dtype) for s in shards],
        in_specs=[VMEM_SPEC, VMEM_SPEC] + [ANY_SPEC] * n,
        out_specs=[VMEM_SPEC, VMEM_SPEC] + [ANY_SPEC] * n,
        scratch_shapes=[pltpu.VMEM((rows, D_MODEL), F32), dma((7,)), dma((7,)), dma((3,)), dma((3,)),
                        dma((3 * n,)), dma((3 * n,)), dma((3 * n,)), dma((3 * n,)), dma((n,))],
        compiler_params=pltpu.CompilerParams(vmem_limit_bytes=VMEM_LIMIT),
    )(c_blk, w_ada, *shards)


def _comm_bwd_call(grads, part):
    n = len(grads)

    def body(part_ref, *refs):
        g_refs, f_refs, parts_ref = refs[:n], refs[n:2 * n], refs[2 * n]
        scratch = refs[2 * n + 1:]
        a_refs, b_refs, p_refs, r_refs = (scratch[k * n:(k + 1) * n] for k in range(4))
        s_send, s_recv, d_send, d_recv, e_send, e_recv, h_send, h_recv, loc_sem = scratch[4 * n:]
        me, dev, chip, sibling, others = _position()
        core = me[2]
        chip_of = [2 * p[0] + p[1] for p in others]

        parts_ref[_rows_of(dev), :] = part_ref[...]
        s_waits = _all_to_all_rows(part_ref, parts_ref, dev, me, s_send, s_recv)

        mine = [pltpu.make_async_copy(g_refs[i].at[:, core], a_refs[i], loc_sem.at[i]) for i in range(n)]
        swap = [pltpu.make_async_remote_copy(src_ref=g_refs[i].at[:, 1 - core], dst_ref=b_refs[i], send_sem=d_send.at[i],
                                             recv_sem=d_recv.at[i], device_id=sibling, device_id_type=MESH) for i in range(n)]
        order = sorted(range(n), key=lambda i: g_refs[i].shape[2] * g_refs[i].shape[3])
        for i in order:
            mine[i].start()
            swap[i].start()
        cross = [pltpu.make_async_remote_copy(src_ref=p_refs[i].at[chip_of[j]], dst_ref=r_refs[i].at[j],
                                              send_sem=e_send.at[3 * i + j], recv_sem=e_recv.at[3 * i + j],
                                              device_id=others[j], device_id_type=MESH) for i in range(n) for j in range(3)]
        for i in order:
            mine[i].wait()
            swap[i].wait()
            for k in range(4):
                s = a_refs[i][k] + b_refs[i][k]
                a_refs[i][k] = s
                p_refs[i][k] = s.astype(BF16)
            for j in range(3):
                cross[3 * i + j].start()
        share = {}
        for i in order:
            for j in range(3):
                cross[3 * i + j].wait()
            f_refs[i][core] = (a_refs[i][chip] + r_refs[i][0].astype(F32) + r_refs[i][1].astype(F32)
                               + r_refs[i][2].astype(F32))
            share[i] = pltpu.make_async_remote_copy(src_ref=f_refs[i].at[core], dst_ref=f_refs[i].at[core],
                                                    send_sem=h_send.at[i], recv_sem=h_recv.at[i], device_id=sibling,
                                                    device_id_type=MESH)
            share[i].start()
        for i in range(n):
            share[i].wait_send()
            pltpu.make_async_remote_copy(src_ref=f_refs[i].at[core], dst_ref=f_refs[i].at[1 - core], send_sem=h_send.at[i],
                                         recv_sem=h_recv.at[i], device_id=sibling, device_id_type=MESH).wait_recv()
        for cp in s_waits:
            cp.wait()

    rows = 8 * ROWS_PER_DEVICE
    dma = pltpu.SemaphoreType.DMA
    quarter = [(4,) + g.shape[2:] for g in grads]
    return pl.pallas_call(
        body, name="comm_bwd",
        out_shape=[jax.ShapeDtypeStruct((2,) + g.shape[2:], F32) for g in grads]
        + [jax.ShapeDtypeStruct((rows, part.shape[1]), F32)],
        in_specs=[VMEM_SPEC] + [ANY_SPEC] * n,
        out_specs=[VMEM_SPEC] * (n + 1),
        scratch_shapes=[pltpu.VMEM(q, F32) for q in quarter] + [pltpu.VMEM(q, F32) for q in quarter]
        + [pltpu.VMEM(q, BF16) for q in quarter] + [pltpu.VMEM((3,) + q[1:], BF16) for q in quarter]
        + [dma((7,)), dma((7,)), dma((n,)), dma((n,)), dma((3 * n,)), dma((3 * n,)), dma((n,)), dma((n,)), dma((n,))],
        compiler_params=pltpu.CompilerParams(vmem_limit_bytes=VMEM_LIMIT),
    )(part, *grads)


def _twice(t):
    lo = _lane_lo()
    other = pltpu.roll(t, HALF, 1)
    return jnp.concatenate([jnp.where(lo, t, other), jnp.where(lo, other, t)], axis=1)


def _once(g):
    first, second = g[:, :HEAD_LANES], g[:, HEAD_LANES:]
    return jnp.where(_lane_lo(), first + pltpu.roll(first, HALF, 1), second + pltpu.roll(second, HALF, 1))


def _rope_tables(pos_col, inv_row):
    ang = pos_col * inv_row
    return jnp.cos(ang), jnp.sin(ang)


def _pre_call(x, pos_col, mod, b_ada, ng, qg, kvg, inv128, wa, wkr2, wq2, wkv, seq):
    n_tok = x.shape[0]
    tm = min(TOKEN_TILE, seq)
    per_seq = seq // tm

    def body(x_ref, pos_ref, mod_ref, bada_ref, ng_ref, qg_ref, kvg_ref, inv_ref, wa_ref, wkr_ref, wq_ref, wkv_ref,
             zqkv_ref, gates_ref, qf_ref, kf_ref, v_ref, qs_ref, kd_ref, vd_ref, rope_ref):
        xv = x_ref[...]
        modv = mod_ref[0] + bada_ref[...]
        shift, scale = modv[:, :D_MODEL], modv[:, D_MODEL:2 * D_MODEL]
        r1 = lax.rsqrt(jnp.mean(xv * xv, axis=-1, keepdims=True) + EPS)
        h = ((xv * r1) * ng_ref[...]) * (1.0 + scale) + shift
        hb = h.astype(BF16)
        za = _dot(hb, wa_ref[...])
        zkr = _dot(hb, wkr_ref[...])
        cos, sin = _rope_tables(pos_ref[...], inv_ref[...])
        rope_ref[:, :HEAD_LANES] = cos
        rope_ref[:, HEAD_LANES:] = sin
        zqkv_ref[...] = za[:, :A_GM]
        gates_ref[:, :512] = za[:, A_GM:A_QS]
        gates_ref[:, 512:] = za[:, A_GS:A_END]
        qs_ref[...] = (za[:, A_QS:A_KS] * (SWA_SCALE * LOG2E)).astype(BF16)
        kd_ref[...] = _twice(za[:, A_KS:A_VS]).astype(BF16)
        vd_ref[...] = _twice(za[:, A_VS:A_GS]).astype(BF16)
        zq, zkv = za[:, A_ZQ:A_ZKV], za[:, A_ZKV:A_GM]
        rq = lax.rsqrt(jnp.mean(zq * zq, axis=-1, keepdims=True) + EPS)
        qn = ((zq * rq) * qg_ref[...]).astype(BF16)
        qr = _dot(qn, wq_ref[...])
        cf, sf = jnp.tile(cos, (1, N_HEADS)), jnp.tile(sin, (1, N_HEADS))
        qf_ref[...] = ((qr[:, :1024] * cf + qr[:, 1024:] * sf) * (MLA_SCALE * LOG2E)).astype(BF16)
        rkv = lax.rsqrt(jnp.mean(zkv * zkv, axis=-1, keepdims=True) + EPS)
        kvn = ((zkv * rkv) * kvg_ref[...]).astype(BF16)
        kv = _dot(kvn, wkv_ref[...])
        kpe = zkr[:, :128] * cos + zkr[:, 128:] * sin
        kf_ref[...] = (kv[:, :1024] + jnp.tile(kpe, (1, N_HEADS))).astype(BF16)
        v_ref[...] = kv[:, 1024:].astype(BF16)

    tok = lambda w: pl.BlockSpec((tm, w), lambda i: (i, 0))
    outs = [(640, F32), (1024, F32), (1024, BF16), (1024, BF16), (512, BF16), (512, BF16), (256, BF16), (256, BF16),
            (2 * HEAD_LANES, F32)]
    return pl.pallas_call(
        body, name="pre", grid=(n_tok // tm,),
        out_shape=[jax.ShapeDtypeStruct((n_tok, w), dt) for w, dt in outs],
        in_specs=[tok(D_MODEL), tok(1), pl.BlockSpec((1, 1, 3 * D_MODEL), lambda i: (i // per_seq, 0, 0)),
                  _full(b_ada.shape), _full(ng.shape), _full(qg.shape), _full(kvg.shape), _full(inv128.shape),
                  _full(wa.shape), _full(wkr2.shape), _full(wq2.shape), _full(wkv.shape)],
        out_specs=[tok(w) for w, _ in outs],
        compiler_params=_params(1),
    )(x, pos_col, mod, b_ada, ng, qg, kvg, inv128, wa, wkr2, wq2, wkv)


def _lane_lo(width=HEAD_LANES):
    return lax.broadcasted_iota(jnp.int32, (1, width), 1) < HALF


def _eye(n=HEAD_LANES):
    r = lax.broadcasted_iota(jnp.int32, (n, n), 0)
    c = lax.broadcasted_iota(jnp.int32, (n, n), 1)
    return jnp.where(r == c, 1.0, 0.0).astype(BF16)


def _mla_fwd_call(qf, kf, v, n_seq, seq):
    tq = min(ATT_TILE, seq)
    nq = seq // tq

    ext = HALF + 16

    def body(q_ref, k_ref, v_ref, o_ref, lse_ref, vt_ref):
        i = pl.program_id(1)
        eye = _eye()

        @pl.when(i == 0)
        def _():
            for h in range(N_HEADS):
                vt_ref[h * ext + HALF:(h + 1) * ext, :] = jnp.ones((16, seq), BF16)
            for t in range(nq):
                for p in range(N_HEADS // 2):
                    pair = slice(p * HEAD_LANES, (p + 1) * HEAD_LANES)
                    v_t = _dot_nt(eye, v_ref[t * tq:(t + 1) * tq, pair]).astype(BF16)
                    for hh in range(2):
                        r0 = (2 * p + hh) * ext
                        vt_ref[r0:r0 + HALF, t * tq:(t + 1) * tq] = v_t[hh * HALF:(hh + 1) * HALF, :]

        q = q_ref[...]
        qcol = i * tq + lax.broadcasted_iota(jnp.int32, (1, tq), 1)
        heads = range(N_HEADS)
        lanes = [slice(h * HEAD_LANES, (h + 1) * HEAD_LANES) for h in heads]

        def make_step(masked, n_tiles):
            def step(kt0, carry):
                tiles = range(n_tiles)
                start = pl.multiple_of(kt0 * tq, tq)
                ks = [k_ref[pl.ds(pl.multiple_of((kt0 + t) * tq, tq), tq), :] for t in tiles]
                vt = vt_ref[:, pl.ds(start, n_tiles * tq)]
                last = n_tiles - 1
                if masked:
                    keep = ((kt0 + last) * tq + lax.broadcasted_iota(jnp.int32, (tq, 1), 0)) <= qcol

                def scores(h):
                    sts = [_dot_nt(ks[t][:, lanes[h]], q[:, lanes[h]]) for t in tiles]
                    if masked:
                        sts[last] = jnp.where(keep, sts[last], NEG)
                    return sts

                def softmax(h, sts):
                    m_old = carry[2 * h]
                    m_new = m_old
                    for st in sts:
                        m_new = jnp.maximum(m_new, jnp.max(st, axis=0, keepdims=True))
                    pt = jnp.concatenate([jnp.exp2(st - m_new).astype(BF16) for st in sts], axis=0)
                    return m_new, jnp.exp2(m_old - m_new), pt

                def values(h, alpha, pt):
                    return carry[2 * h + 1] * alpha + _dot(vt[h * ext:(h + 1) * ext, :], pt)

                sts, soft, out = {0: scores(0), 1: scores(1)}, {}, {}
                for h in range(N_HEADS + 1):
                    if h + 2 < N_HEADS:
                        sts[h + 2] = scores(h + 2)
                    if h < N_HEADS:
                        soft[h] = softmax(h, sts.pop(h))
                    if h >= 1:
                        m_new, alpha, pt = soft.pop(h - 1)
                        out[h - 1] = (m_new, values(h - 1, alpha, pt))
                return tuple(v for h in heads for v in out[h])
            return step

        init = (jnp.full((1, tq), NEG, F32), jnp.zeros((ext, tq), F32)) * N_HEADS
        count = i + 1
        carry = lax.fori_loop(0, (count + 1) // 2 - 1, lambda j, c: make_step(False, 2)(2 * j, c), init)
        carry = lax.cond(count % 2 == 0, lambda c: make_step(True, 2)(i - 1, c), lambda c: make_step(True, 1)(i, c), carry)
        dens = [carry[2 * h + 1][HALF:HALF + 1, :] for h in heads]
        acc_t = jnp.concatenate([carry[2 * h + 1][:HALF, :] * (1.0 / dens[h]) for h in heads], axis=0)
        o_ref[...] = acc_t.T
        for h in heads:
            lse_ref[0, h // 4, h % 4:h % 4 + 1, :] = carry[2 * h] + jnp.log2(dens[h])

    n_tok = qf.shape[0]
    return pl.pallas_call(
        body, name="mla_fwd", grid=(n_seq, nq),
        out_shape=[jax.ShapeDtypeStruct((n_tok, 512), F32), jax.ShapeDtypeStruct((n_seq, 2, 4, seq), F32)],
        in_specs=[pl.BlockSpec((tq, 1024), lambda b, i: (b * nq + i, 0)),
                  pl.BlockSpec((seq, 1024), lambda b, i: (b, 0)),
                  pl.BlockSpec((seq, 512), lambda b, i: (b, 0))],
        out_specs=[pl.BlockSpec((tq, 512), lambda b, i: (b * nq + i, 0)),
                   pl.BlockSpec((1, 2, 4, tq), lambda b, i: (b, 0, 0, i))],
        scratch_shapes=[pltpu.VMEM((N_HEADS * ext, seq), BF16)],
        compiler_params=_params(2),
    )(qf, kf, v)


def _mla_bwd_call(qf, kf, v, do, o, lse, n_seq, seq):
    tq = min(ATT_TILE, seq)
    nq = seq // tq

    nh = 4
    heads = range(nh)
    lanes = [slice(h * HEAD_LANES, (h + 1) * HEAD_LANES) for h in heads]

    def body(q_ref, k_ref, v_ref, do_ref, o_ref, lse_ref, dq_ref, dk_ref, dv_ref,
             kt_ref, dot_ref, delta_ref, dqt_ref):
        eye = _eye()
        lo = _lane_lo()
        sub_lo = lax.broadcasted_iota(jnp.int32, (HEAD_LANES, 1), 0) < HALF
        ones_lo = jnp.where(jnp.broadcast_to(lo, (8, HEAD_LANES)), 1.0, 0.0).astype(BF16)
        ones_hi = jnp.where(jnp.broadcast_to(lo, (8, HEAD_LANES)), 0.0, 1.0).astype(BF16)

        for t in range(nq):
            r = slice(t * tq, (t + 1) * tq)
            kv = k_ref[r, :]
            for h in heads:
                kt_ref[lanes[h], r] = _dot_nt(eye, kv[:, lanes[h]]).astype(BF16)
            for p in range(nh // 2):
                dov = do_ref[r, lanes[p]]
                dt = _dot_nt(eye, dov)
                dot_ref[2 * p, :, r] = jnp.where(sub_lo, dt, 0.0).astype(BF16)
                dot_ref[2 * p + 1, :, r] = jnp.where(sub_lo, 0.0, dt).astype(BF16)
                prod = dov.astype(F32) * o_ref[r, lanes[p]]
                p_hi = prod.astype(BF16)
                p_lo = (prod - p_hi.astype(F32)).astype(BF16)
                delta_ref[2 * p, :, r] = _dot_nt(ones_lo, p_hi) + _dot_nt(ones_lo, p_lo)
                delta_ref[2 * p + 1, :, r] = _dot_nt(ones_hi, p_hi) + _dot_nt(ones_hi, p_lo)
        dqt_ref[...] = jnp.zeros_like(dqt_ref)

        def k_step(kt, _):
            kr = pl.ds(pl.multiple_of(kt * tq, tq), tq)
            k = k_ref[kr, :]
            vv = v_ref[kr, :]
            k_t = kt_ref[:, kr]
            krow = kt * tq + lax.broadcasted_iota(jnp.int32, (tq, 1), 0)

            def make_step(masked, n_tiles):
                def q_step(qt0, carry):
                    tiles = range(n_tiles)
                    qrs = [pl.ds(pl.multiple_of((qt0 + t) * tq, tq), tq) for t in tiles]
                    qs = [q_ref[qr, :] for qr in qrs]
                    if masked:
                        keep = krow <= (qt0 * tq + lax.broadcasted_iota(jnp.int32, (1, tq), 1))

                    def scores(h):
                        do_ts = [dot_ref[h, :, qr] for qr in qrs]
                        sts = [_dot_nt(k[:, lanes[h]], qs[t][:, lanes[h]]) for t in tiles]
                        dpts = [_dot(vv[:, lanes[h // 2]], do_ts[t]) for t in tiles]
                        return do_ts, sts, dpts

                    def softmax(h, sts, dpts):
                        pts, dsts = [], []
                        for t in tiles:
                            pt = jnp.exp2(sts[t] - lse_ref[0, 0, h:h + 1, qrs[t]])
                            if masked and t == 0:
                                pt = jnp.where(keep, pt, 0.0)
                            dsts.append((pt * (dpts[t] - delta_ref[h, 0:1, qrs[t]])).astype(BF16))
                            pts.append(pt.astype(BF16))
                        return pts, dsts

                    def grads(h, do_ts, pts, dsts):
                        half = slice((h % 2) * HALF, (h % 2 + 1) * HALF)
                        dst_all = jnp.concatenate(dsts, axis=1)
                        pt_all = jnp.concatenate(pts, axis=1)
                        do_all = jnp.concatenate([do_ts[t][half, :] for t in tiles], axis=1)
                        q_all = jnp.concatenate([qs[t][:, lanes[h]] for t in tiles], axis=0)
                        dvt = _dot_nt(do_all, pt_all)
                        dk = _dot(dst_all, q_all)
                        for t in tiles:
                            dqt_ref[lanes[h], qrs[t]] += _dot(k_t[lanes[h], :], dsts[t])
                        return carry[2 * h] + dk, carry[2 * h + 1] + dvt

                    first, second, out = {0: scores(0)}, {}, {}
                    for h in range(nh + 1):
                        if h + 1 < nh:
                            first[h + 1] = scores(h + 1)
                        if h < nh:
                            do_ts, sts, dpts = first.pop(h)
                            second[h] = (do_ts,) + softmax(h, sts, dpts)
                        if h >= 1:
                            out[h - 1] = grads(h - 1, *second.pop(h - 1))
                    return tuple(v for h in heads for v in out[h])
                return q_step

            init = (jnp.zeros((tq, HEAD_LANES), F32), jnp.zeros((HALF, tq), F32)) * nh
            count = nq - kt
            carry = lax.cond(count >= 2, lambda c: make_step(True, 2)(kt, c), lambda c: make_step(True, 1)(kt, c), init)
            carry = lax.fori_loop(1, count // 2, lambda j, c: make_step(False, 2)(kt + 2 * j, c), carry)
            carry = lax.cond(jnp.logical_and(count % 2 == 1, count >= 3),
                             lambda c: make_step(False, 1)(nq - 1, c), lambda c: c, carry)
            for h in heads:
                dk_ref[kr, lanes[h]] = carry[2 * h]
            for p in range(nh // 2):
                dv_ref[kr, lanes[p]] = jnp.concatenate([carry[4 * p + 1], carry[4 * p + 3]], axis=0).T
            return 0

        lax.fori_loop(0, nq, k_step, 0)
        for t in range(nq):
            r = slice(t * tq, (t + 1) * tq)
            for h in heads:
                dq_ref[r, lanes[h]] = dqt_ref[lanes[h], r].T

    n_tok = qf.shape[0]
    groups = N_HEADS // nh
    blk = lambda w: pl.BlockSpec((seq, w), lambda b, g: (b, g))
    return pl.pallas_call(
        body, name="mla_bwd", grid=(n_seq, groups),
        out_shape=[jax.ShapeDtypeStruct((n_tok, 1024), F32), jax.ShapeDtypeStruct((n_tok, 1024), F32),
                   jax.ShapeDtypeStruct((n_tok, 512), F32)],
        in_specs=[blk(512), blk(512), blk(256), blk(256), blk(256),
                  pl.BlockSpec((1, 1, nh, seq), lambda b, g: (b, g, 0, 0))],
        out_specs=[blk(512), blk(512), blk(256)],
        scratch_shapes=[pltpu.VMEM((nh * HEAD_LANES, seq), BF16), pltpu.VMEM((nh, HEAD_LANES, seq), BF16),
                        pltpu.VMEM((nh, 8, seq), F32), pltpu.VMEM((nh * HEAD_LANES, seq), F32)],
        compiler_params=_params(2),
    )(qf, kf, v, do, o, lse)


SWA_BLOCKS = 4


def _swa_block(n, pos_col_ref, posq):
    w = SWA_WINDOW
    start = pl.multiple_of(jnp.maximum(n - 1, 0) * w, w)
    posk = pos_col_ref[pl.ds(start, 2 * w), :]
    rel = (n * w + lax.broadcasted_iota(jnp.int32, (1, w), 1)) - (start + lax.broadcasted_iota(jnp.int32, (2 * w, 1), 0))
    valid = jnp.logical_and(rel >= 0, rel < w)
    return start, jnp.where(valid, posq - posk, 1e30)


def _alibi(h):
    return LOG2E * 2.0 ** -(h + 1)


def _transpose_rows(eye, src_ref, dst_ref, seq, width):
    step = 2 * SWA_WINDOW
    for t in range(seq // step):
        for p in range(width // HEAD_LANES):
            lanes = slice(p * HEAD_LANES, (p + 1) * HEAD_LANES)
            dst_ref[lanes, t * step:(t + 1) * step] = _dot_nt(eye, src_ref[t * step:(t + 1) * step, lanes]).astype(BF16)


def _swa_fwd_call(qs, kd, vd, pos_col, pos_row, sinks, n_seq, seq):
    w = SWA_WINDOW
    qb = SWA_BLOCKS
    steps = seq // (qb * w)
    ext = HALF + 16

    def body(q_ref, k_ref, v_ref, pc_ref, pr_ref, sink_ref, o_ref, lse_ref, vt_ref):
        n = pl.program_id(1)
        lo = _lane_lo()
        hi = jnp.logical_not(lo)
        eye = _eye()

        @pl.when(n == 0)
        def _():
            step = 2 * w
            for kv in range(2):
                vt_ref[kv * ext + HALF:(kv + 1) * ext, :] = jnp.ones((16, seq), BF16)
                for t in range(seq // step):
                    v_t = _dot_nt(eye, v_ref[t * step:(t + 1) * step, kv * HEAD_LANES:(kv + 1) * HEAD_LANES])
                    vt_ref[kv * ext:kv * ext + HALF, t * step:(t + 1) * step] = v_t[:HALF, :].astype(BF16)

        heads = range(N_HEADS)
        blocks = range(qb)
        geo = [_swa_block(n * qb + bi, pc_ref, pr_ref[bi]) for bi in blocks]
        wins = [pl.ds(g[0], 2 * w) for g in geo]
        kwins = [k_ref[win, :] for win in wins]
        vts = [vt_ref[:, win] for win in wins]
        sts = []
        for bi in blocks:
            q = q_ref[bi * w:(bi + 1) * w, :]
            sts.append([])
            for h in heads:
                qp = q[:, (h // 2) * HEAD_LANES:(h // 2 + 1) * HEAD_LANES]
                qh = jnp.where(lo if h % 2 == 0 else hi, qp, jnp.zeros_like(qp))
                sts[bi].append(_dot_nt(kwins[bi][:, (h // 4) * HEAD_LANES:(h // 4 + 1) * HEAD_LANES], qh))
        ps, ms = [], []
        for bi in blocks:
            ps.append([])
            ms.append([])
            for h in heads:
                s = sts[bi][h] - _alibi(h) * geo[bi][1]
                m = jnp.maximum(jnp.max(s, axis=0, keepdims=True), sink_ref[0, h] * LOG2E)
                ps[bi].append(jnp.exp2(s - m).astype(BF16))
                ms[bi].append(m)
        for bi in blocks:
            ots = []
            for h in heads:
                pv = _dot(vts[bi][(h // 4) * ext:(h // 4 + 1) * ext, :], ps[bi][h])
                l = pv[HALF:HALF + 1, :] + jnp.exp2(sink_ref[0, h] * LOG2E - ms[bi][h])
                ots.append(pv[:HALF, :] * (1.0 / l))
                lse_ref[0, h:h + 1, bi * w:(bi + 1) * w] = ms[bi][h] + jnp.log2(l)
            o_ref[bi * w:(bi + 1) * w, :] = jnp.concatenate(ots, axis=0).T

    n_tok = qs.shape[0]
    tok = lambda width: pl.BlockSpec((qb * w, width), lambda b, n: (b * steps + n, 0))
    whole = lambda width: pl.BlockSpec((seq, width), lambda b, n: (b, 0))
    return pl.pallas_call(
        body, name="swa_fwd", grid=(n_seq, steps),
        out_shape=[jax.ShapeDtypeStruct((n_tok, 512), F32), jax.ShapeDtypeStruct((n_seq, N_HEADS, seq), F32)],
        in_specs=[tok(512), whole(256), whole(256), whole(1), pl.BlockSpec((qb, 1, w), lambda b, n: (b * steps + n, 0, 0)),
                  pl.BlockSpec(memory_space=pltpu.SMEM)],
        out_specs=[tok(512), pl.BlockSpec((1, N_HEADS, qb * w), lambda b, n: (b, 0, n))],
        scratch_shapes=[pltpu.VMEM((2 * ext, seq), BF16)],
        compiler_params=_params(2),
    )(qs, kd, vd, pos_col, pos_row, sinks)


def _swa_bwd_call(qs, kd, vd, do, o, lse, pos_col, pos_row, sinks, n_seq, seq):
    w = SWA_WINDOW
    qb = SWA_BLOCKS
    steps = seq // (qb * w)

    def body(q_ref, k_ref, v_ref, do_ref, o_ref, lse_ref, pc_ref, pr_ref, sink_ref, dq_ref, dk_ref, dv_ref, dsink_ref,
             kt_ref):
        b, n = pl.program_id(0), pl.program_id(1)
        lo = _lane_lo()
        hi = jnp.logical_not(lo)
        sub_lo = lax.broadcasted_iota(jnp.int32, (HEAD_LANES, 1), 0) < HALF
        eye = _eye()
        ones_lo = jnp.where(jnp.broadcast_to(lo, (8, HEAD_LANES)), 1.0, 0.0).astype(BF16)
        ones_hi = jnp.where(jnp.broadcast_to(lo, (8, HEAD_LANES)), 0.0, 1.0).astype(BF16)

        @pl.when(n == 0)
        def _():
            dk_ref[...] = jnp.zeros_like(dk_ref)
            dv_ref[...] = jnp.zeros_like(dv_ref)
            _transpose_rows(eye, k_ref, kt_ref, seq, 2 * HEAD_LANES)

        @pl.when(jnp.logical_and(n == 0, b == 0))
        def _():
            dsink_ref[...] = jnp.zeros_like(dsink_ref)

        heads = range(N_HEADS)
        blocks = range(qb)
        kv_lanes = lambda h: slice((h // 4) * HEAD_LANES, (h // 4 + 1) * HEAD_LANES)
        geo = [_swa_block(n * qb + bi, pc_ref, pr_ref[bi]) for bi in blocks]
        wins = [pl.ds(g[0], 2 * w) for g in geo]
        kwins = [k_ref[win, :] for win in wins]
        vwins = [v_ref[win, :] for win in wins]

        do_ts, deltas, qms, doms = [], [], [], []
        for bi in blocks:
            rows = slice(bi * w, (bi + 1) * w)
            for lst in (do_ts, deltas, qms, doms):
                lst.append([])
            for j in range(N_HEADS // 2):
                pair = slice(j * HEAD_LANES, (j + 1) * HEAD_LANES)
                dop = do_ref[rows, pair]
                qp = q_ref[rows, pair]
                dt = _dot_nt(eye, dop)
                prod = dop.astype(F32) * o_ref[rows, pair]
                p_hi = prod.astype(BF16)
                p_lo = (prod - p_hi.astype(F32)).astype(BF16)
                for hh in range(2):
                    half, ones = (lo, ones_lo) if hh == 0 else (hi, ones_hi)
                    do_ts[bi].append(jnp.where(sub_lo, dt, 0.0).astype(BF16) if hh == 0
                                     else jnp.where(sub_lo, 0.0, dt).astype(BF16))
                    deltas[bi].append((_dot_nt(ones, p_hi) + _dot_nt(ones, p_lo))[0:1, :])
                    qms[bi].append(jnp.where(half, qp, jnp.zeros_like(qp)))
                    doms[bi].append(jnp.where(half, dop, jnp.zeros_like(dop)))
        sts = [[_dot_nt(kwins[bi][:, kv_lanes(h)], qms[bi][h]) for h in heads] for bi in blocks]
        dpts = [[_dot(vwins[bi][:, kv_lanes(h)], do_ts[bi][h]) for h in heads] for bi in blocks]
        pts, dsts = [], []
        for bi in blocks:
            pts.append([])
            dsts.append([])
            for h in heads:
                lse_h = lse_ref[0, h:h + 1, bi * w:(bi + 1) * w]
                pt = jnp.exp2(sts[bi][h] - _alibi(h) * geo[bi][1] - lse_h)
                dsts[bi].append((pt * (dpts[bi][h] - deltas[bi][h])).astype(BF16))
                pts[bi].append(pt.astype(BF16))
                dsink_ref[h:h + 1, :] += -jnp.exp2(sink_ref[0, h] * LOG2E - lse_h) * deltas[bi][h]
        for bi in blocks:
            for kv in range(2):
                group = range(4 * kv, 4 * kv + 4)
                dst_all = jnp.concatenate([dsts[bi][h] for h in group], axis=1)
                pt_all = jnp.concatenate([pts[bi][h] for h in group], axis=1)
                q_all = jnp.concatenate([qms[bi][h] for h in group], axis=0)
                do_all = jnp.concatenate([doms[bi][h] for h in group], axis=0)
                dk_ref[wins[bi], kv_lanes(4 * kv)] += _dot(dst_all, q_all)
                dv_ref[wins[bi], kv_lanes(4 * kv)] += _dot(pt_all, do_all)
        for bi in blocks:
            ktw = kt_ref[:, wins[bi]]
            for j in range(N_HEADS // 2):
                k_t = ktw[kv_lanes(2 * j), :]
                dq_t = jnp.where(sub_lo, _dot(k_t, dsts[bi][2 * j]), _dot(k_t, dsts[bi][2 * j + 1]))
                dq_ref[bi * w:(bi + 1) * w, j * HEAD_LANES:(j + 1) * HEAD_LANES] = dq_t.T * SWA_SCALE

    n_tok = qs.shape[0]
    tok = lambda width: pl.BlockSpec((qb * w, width), lambda b, n: (b * steps + n, 0))
    whole = lambda width: pl.BlockSpec((seq, width), lambda b, n: (b, 0))
    return pl.pallas_call(
        body, name="swa_bwd", grid=(n_seq, steps),
        out_shape=[jax.ShapeDtypeStruct((n_tok, 512), F32), jax.ShapeDtypeStruct((n_tok, 256), F32),
                   jax.ShapeDtypeStruct((n_tok, 256), F32), jax.ShapeDtypeStruct((N_HEADS, HEAD_LANES), F32)],
        in_specs=[tok(512), whole(256), whole(256), pl.BlockSpec((qb * w, 512), lambda b, n: (b * steps + n, 1)), tok(512),
                  pl.BlockSpec((1, N_HEADS, qb * w), lambda b, n: (b, 0, n)),
                  whole(1), pl.BlockSpec((qb, 1, w), lambda b, n: (b * steps + n, 0, 0)),
                  pl.BlockSpec(memory_space=pltpu.SMEM)],
        out_specs=[tok(512), whole(256), whole(256), _full((N_HEADS, HEAD_LANES))],
        scratch_shapes=[pltpu.VMEM((2 * HEAD_LANES, seq), BF16)],
        compiler_params=_params(2),
    )(qs, kd, vd, do, o, lse, pos_col, pos_row, sinks)


def _post_call(x, target, o_mla, o_swa, gates, mod, b_ada, fg, w_out, seq):
    n_tok = x.shape[0]
    tm = min(TOKEN_TILE, seq)
    per_seq = seq // tm
    n_seq = n_tok // seq

    def body(x_ref, t_ref, om_ref, os_ref, g_ref, mod_ref, bada_ref, fg_ref, w_ref,
             dx2_ref, do_ref, dg_ref, gw_ref, gfg_ref, dgate_ref, loss_ref):
        i = pl.program_id(0)

        @pl.when(i == 0)
        def _():
            gw_ref[...] = jnp.zeros_like(gw_ref)
            gfg_ref[...] = jnp.zeros_like(gfg_ref)
            loss_ref[...] = jnp.zeros_like(loss_ref)

        @pl.when(i % per_seq == 0)
        def _():
            dgate_ref[...] = jnp.zeros_like(dgate_ref)

        gate = mod_ref[0][:, 2 * D_MODEL:] + bada_ref[:, 2 * D_MODEL:]
        fgv = fg_ref[...]
        subs = _sub_tiles(tm)
        gs = [g_ref[r, :] for r in subs]
        os_ = [jnp.concatenate([om_ref[r, :], os_ref[r, :]], axis=-1) for r in subs]
        sgs = [_sigmoid(g) for g in gs]
        sils = [g * sg for g, sg in zip(gs, sgs)]
        ypres = [(o * sil).astype(BF16) for o, sil in zip(os_, sils)]
        ys = [_dot(ypre, w_ref[...]) for ypre in ypres]
        dys, loss, gfg, dgate = [], 0.0, 0.0, 0.0
        for r, y in zip(subs, ys):
            x2 = x_ref[r, :] + gate * y
            r2 = lax.rsqrt(jnp.mean(x2 * x2, axis=-1, keepdims=True) + EPS)
            xn2 = x2 * r2
            err = xn2 * fgv - t_ref[r, :]
            loss = loss + jnp.sum(jnp.sum(err * err, axis=-1, keepdims=True), axis=0, keepdims=True)
            dout = err * (1.0 / D_MODEL)
            gfg = gfg + jnp.sum(dout * xn2, axis=0, keepdims=True)
            dxn2 = dout * fgv
            dx2 = r2 * (dxn2 - xn2 * jnp.mean(dxn2 * xn2, axis=-1, keepdims=True))
            dx2_ref[r, :] = dx2
            dgate = dgate + jnp.sum(dx2 * y, axis=0, keepdims=True)
            dys.append((dx2 * gate).astype(BF16))
        loss_ref[...] += jnp.broadcast_to(loss * (0.5 / D_MODEL), loss_ref.shape)
        gfg_ref[...] += gfg
        dgate_ref[0] += dgate
        gw_ref[...] += _dot_tn(jnp.concatenate(ypres, axis=0), jnp.concatenate(dys, axis=0))
        dypres = [_dot_nt(dy, w_ref[...]) for dy in dys]
        for r, dypre, o, g, sg, sil in zip(subs, dypres, os_, gs, sgs, sils):
            do_ref[r, :] = (dypre * sil).astype(BF16)
            dg_ref[r, :] = (dypre * o * (sg * (1.0 + g * (1.0 - sg)))).astype(BF16)

    tok = lambda w: pl.BlockSpec((tm, w), lambda i: (i, 0))
    per_b = pl.BlockSpec((1, 1, 3 * D_MODEL), lambda i: (i // per_seq, 0, 0))
    return pl.pallas_call(
        body, name="post", grid=(n_tok // tm,),
        out_shape=[jax.ShapeDtypeStruct((n_tok, D_MODEL), F32), jax.ShapeDtypeStruct((n_tok, D_MODEL), BF16),
                   jax.ShapeDtypeStruct((n_tok, D_MODEL), BF16), jax.ShapeDtypeStruct((D_MODEL, D_MODEL), F32),
                   jax.ShapeDtypeStruct((1, D_MODEL), F32), jax.ShapeDtypeStruct((n_seq, 1, D_MODEL), F32),
                   jax.ShapeDtypeStruct((1, HEAD_LANES), F32)],
        in_specs=[tok(D_MODEL), tok(D_MODEL), tok(512), tok(512), tok(D_MODEL), per_b, _full(b_ada.shape),
                  _full(fg.shape), _full(w_out.shape)],
        out_specs=[tok(D_MODEL), tok(D_MODEL), tok(D_MODEL), _full((D_MODEL, D_MODEL)), _full((1, D_MODEL)),
                   pl.BlockSpec((1, 1, D_MODEL), lambda i: (i // per_seq, 0, 0)), _full((1, HEAD_LANES))],
        compiler_params=_params(1),
    )(x, target, o_mla, o_swa, gates, mod, b_ada, fg, w_out)


def _mid_bwd_call(dqf, dkf, dv, zqkv, rope, qg, kvg, wq2, wkv, seq):
    n_tok = dqf.shape[0]
    tm = min(TOKEN_TILE, seq)

    def body(dq_ref, dk_ref, dv_ref, z_ref, rope_ref, qg_ref, kvg_ref, wq_ref, wkv_ref,
             dz_ref, dkr_ref, gwq_ref, gwkv_ref, gqg_ref, gkvg_ref):
        i = pl.program_id(0)

        @pl.when(i == 0)
        def _():
            gwq_ref[...] = jnp.zeros_like(gwq_ref)
            gwkv_ref[...] = jnp.zeros_like(gwkv_ref)
            gqg_ref[...] = jnp.zeros_like(gqg_ref)
            gkvg_ref[...] = jnp.zeros_like(gkvg_ref)

        cos, sin = rope_ref[:, :HEAD_LANES], rope_ref[:, HEAD_LANES:]
        cf, sf = jnp.tile(cos, (1, N_HEADS)), jnp.tile(sin, (1, N_HEADS))
        dq = dq_ref[...] * MLA_SCALE
        dqr = jnp.concatenate([dq * cf, dq * sf], axis=-1).astype(BF16)
        zq, zkv = z_ref[:, :Q_LORA], z_ref[:, Q_LORA:]
        qgv, kvgv = qg_ref[...], kvg_ref[...]

        rq = lax.rsqrt(jnp.mean(zq * zq, axis=-1, keepdims=True) + EPS)
        xq = zq * rq
        gwq_ref[...] += _dot_tn((xq * qgv).astype(BF16), dqr)
        dqn = _dot_nt(dqr, wq_ref[...])
        gqg_ref[...] += jnp.sum(dqn * xq, axis=0, keepdims=True)
        dxq = dqn * qgv
        dz_ref[:, :Q_LORA] = (rq * (dxq - xq * jnp.mean(dxq * xq, axis=-1, keepdims=True))).astype(BF16)

        dk = dk_ref[...] * LN2
        dkv = jnp.concatenate([dk, dv_ref[...]], axis=-1).astype(BF16)
        rkv = lax.rsqrt(jnp.mean(zkv * zkv, axis=-1, keepdims=True) + EPS)
        xkv = zkv * rkv
        gwkv_ref[...] += _dot_tn((xkv * kvgv).astype(BF16), dkv)
        dkvn = _dot_nt(dkv, wkv_ref[...])
        gkvg_ref[...] += jnp.sum(dkvn * xkv, axis=0, keepdims=True)
        dxkv = dkvn * kvgv
        dz_ref[:, Q_LORA:] = (rkv * (dxkv - xkv * jnp.mean(dxkv * xkv, axis=-1, keepdims=True))).astype(BF16)

        dkpe = dk[:, :HEAD_LANES]
        for h in range(1, N_HEADS):
            dkpe = dkpe + dk[:, h * HEAD_LANES:(h + 1) * HEAD_LANES]
        dkr_ref[:, :HEAD_LANES] = (dkpe * cos).astype(BF16)
        dkr_ref[:, HEAD_LANES:] = (dkpe * sin).astype(BF16)

    tok = lambda w: pl.BlockSpec((tm, w), lambda i: (i, 0))
    return pl.pallas_call(
        body, name="mid_bwd", grid=(n_tok // tm,),
        out_shape=[jax.ShapeDtypeStruct((n_tok, 640), BF16), jax.ShapeDtypeStruct((n_tok, 256), BF16),
                   jax.ShapeDtypeStruct(wq2.shape, F32), jax.ShapeDtypeStruct(wkv.shape, F32),
                   jax.ShapeDtypeStruct((1, Q_LORA), F32), jax.ShapeDtypeStruct((1, KV_LORA), F32)],
        in_specs=[tok(1024), tok(1024), tok(512), tok(640), tok(2 * HEAD_LANES), _full(qg.shape), _full(kvg.shape),
                  _full(wq2.shape), _full(wkv.shape)],
        out_specs=[tok(640), tok(256), _full(wq2.shape), _full(wkv.shape), _full((1, Q_LORA)), _full((1, KV_LORA))],
        compiler_params=_params(1),
    )(dqf, dkf, dv, zqkv, rope, qg, kvg, wq2, wkv)


def _in_bwd_call(x, dx2, dz, dkr, dg, dqs, dkd, dvd, mod, b_ada, ng, wa, wkr2, seq):
    n_tok = x.shape[0]
    tm = min(TOKEN_TILE, seq)
    per_seq = seq // tm
    n_seq = n_tok // seq

    def body(x_ref, dx2_ref, dz_ref, dkr_ref, dg_ref, dqs_ref, dkd_ref, dvd_ref, mod_ref, bada_ref, ng_ref,
             wa_ref, wkr_ref, gx_ref, gwa_ref, gwkr_ref, gng_ref, dshift_ref, dscale_ref):
        i = pl.program_id(0)

        @pl.when(i == 0)
        def _():
            gwa_ref[...] = jnp.zeros_like(gwa_ref)
            gwkr_ref[...] = jnp.zeros_like(gwkr_ref)
            gng_ref[...] = jnp.zeros_like(gng_ref)

        @pl.when(i % per_seq == 0)
        def _():
            dshift_ref[...] = jnp.zeros_like(dshift_ref)
            dscale_ref[...] = jnp.zeros_like(dscale_ref)

        xv = x_ref[...]
        modv = mod_ref[0] + bada_ref[...]
        shift, scale = modv[:, :D_MODEL], modv[:, D_MODEL:2 * D_MODEL]
        ngv = ng_ref[...]
        r1 = lax.rsqrt(jnp.mean(xv * xv, axis=-1, keepdims=True) + EPS)
        xn = xv * r1
        hb = ((xn * ngv) * (1.0 + scale) + shift).astype(BF16)

        dgv = dg_ref[...]
        pieces = [(A_ZQ, dz_ref[...]), (A_GM, dgv[:, :512]), (A_QS, dqs_ref[...].astype(BF16)),
                  (A_KS, jnp.concatenate([_once(dkd_ref[...]) * LN2, _once(dvd_ref[...])], axis=1).astype(BF16)),
                  (A_GS, dgv[:, 512:])]
        dkr = dkr_ref[...]
        gwkr_ref[...] += _dot_tn(hb, dkr)
        dh = _dot_nt(dkr, wkr_ref[...])
        for off, piece in pieces:
            wd = piece.shape[1]
            gwa_ref[:, off:off + wd] += _dot_tn(hb, piece)
            dh = dh + _dot_nt(piece, wa_ref[:, off:off + wd])

        dshift_ref[0] += jnp.sum(dh, axis=0, keepdims=True)
        dscale_ref[0] += jnp.sum(dh * (xn * ngv), axis=0, keepdims=True)
        gng_ref[...] += jnp.sum(dh * xn * (1.0 + scale), axis=0, keepdims=True)
        dxn = dh * ngv * (1.0 + scale)
        gx_ref[...] = dx2_ref[...] + r1 * (dxn - xn * jnp.mean(dxn * xn, axis=-1, keepdims=True))

    tok = lambda w: pl.BlockSpec((tm, w), lambda i: (i, 0))
    per_b = lambda w: pl.BlockSpec((1, 1, w), lambda i: (i // per_seq, 0, 0))
    return pl.pallas_call(
        body, name="in_bwd", grid=(n_tok // tm,),
        out_shape=[jax.ShapeDtypeStruct((n_tok, D_MODEL), F32), jax.ShapeDtypeStruct((D_MODEL, A_END), F32),
                   jax.ShapeDtypeStruct((D_MODEL, 256), F32), jax.ShapeDtypeStruct((1, D_MODEL), F32),
                   jax.ShapeDtypeStruct((n_seq, 1, D_MODEL), F32), jax.ShapeDtypeStruct((n_seq, 1, D_MODEL), F32)],
        in_specs=[tok(D_MODEL), tok(D_MODEL), tok(640), tok(256), tok(D_MODEL), tok(512), tok(256), tok(256),
                  per_b(3 * D_MODEL), _full(b_ada.shape), _full(ng.shape), _full(wa.shape), _full(wkr2.shape)],
        out_specs=[tok(D_MODEL), _full((D_MODEL, A_END)), _full((D_MODEL, 256)), _full((1, D_MODEL)),
                   per_b(D_MODEL), per_b(D_MODEL)],
        compiler_params=_params(1),
    )(x, dx2, dz, dkr, dg, dqs, dkd, dvd, mod, b_ada, ng, wa, wkr2)


def _adam_math(w, g, m, v):
    m_new = ADAM_B1 * m + (1.0 - ADAM_B1) * g
    v_new = ADAM_B2 * v + (1.0 - ADAM_B2) * (g * g)
    m_hat = m_new / (1.0 - ADAM_B1 ** ADAM_STEP)
    v_hat = v_new / (1.0 - ADAM_B2 ** ADAM_STEP)
    delta = -ADAM_LR * (m_hat / (jnp.sqrt(v_hat) + ADAM_EPS) + ADAM_WD * w)
    return delta, m_new, v_new


def _adam_call(name, w, g, m, v):
    rows, cols = w.shape
    tr = next((t for t in (256, 128, 88) if rows % t == 0), rows)

    def body(w_ref, g_ref, m_ref, v_ref, d_ref, mo_ref, vo_ref):
        d, mn, vn = _adam_math(w_ref[...], g_ref[...], m_ref[...], v_ref[...])
        d_ref[...] = d
        mo_ref[...] = mn
        vo_ref[...] = vn

    spec = pl.BlockSpec((tr, cols), lambda i: (i, 0))
    return pl.pallas_call(
        body, name=name, grid=(rows // tr,),
        out_shape=[jax.ShapeDtypeStruct(w.shape, F32)] * 3,
        in_specs=[spec] * 4, out_specs=[spec] * 3,
        compiler_params=_params(1),
    )(w, g, m, v)


def _ada_bwd_call(act_all, dmod_cols, w, m, v):
    rows, cols = w.shape
    tr = 256

    def body(a_ref, dm_ref, w_ref, m_ref, v_ref, g_ref, d_ref, mo_ref, vo_ref):
        g = _dot_tn(a_ref[...].astype(BF16), dm_ref[...].astype(BF16))
        d, mn, vn = _adam_math(w_ref[...], g, m_ref[...], v_ref[...])
        g_ref[...] = g
        d_ref[...] = d
        mo_ref[...] = mn
        vo_ref[...] = vn

    spec = pl.BlockSpec((tr, cols), lambda i: (i, 0))
    nb = act_all.shape[0]
    return pl.pallas_call(
        body, name="ada_bwd", grid=(rows // tr,),
        out_shape=[jax.ShapeDtypeStruct(w.shape, F32)] * 4,
        in_specs=[pl.BlockSpec((nb, tr), lambda i: (0, i)), _full(dmod_cols.shape), spec, spec, spec],
        out_specs=[spec] * 4,
        compiler_params=_params(1),
    )(act_all, dmod_cols, w, m, v)


SMALL_ROW = {"norm_gain": (0, 1024), "final_gain": (1024, 2048), "q_norm_gain": (2048, 2432),
             "kv_norm_gain": (2432, 2688), "swa_sinks": (2688, 2696), "loss": (2816, 2944)}
SMALL_ORDER = ("b_ada", "norm_gain", "q_norm_gain", "kv_norm_gain", "swa_sinks", "final_gain")


def _small_call(parts_all, n_seq, params):
    k = len(params)

    def body(p_ref, *refs):
        ins, outs, loss_ref = refs[:3 * k], refs[3 * k:7 * k], refs[7 * k]
        row = p_ref[n_seq:n_seq + 1, :]
        for dv in range(1, 8):
            r0 = dv * ROWS_PER_DEVICE + n_seq
            row = row + p_ref[r0:r0 + 1, :]
        gb = None
        for dv in range(8):
            for r in range(n_seq):
                r0 = dv * ROWS_PER_DEVICE + r
                gb = p_ref[r0:r0 + 1, :] if gb is None else gb + p_ref[r0:r0 + 1, :]
        for j, name in enumerate(SMALL_ORDER):
            g = gb if name == "b_ada" else row[:, SMALL_ROW[name][0]:SMALL_ROW[name][1]]
            d, mn, vn = _adam_math(ins[3 * j][...], g, ins[3 * j + 1][...], ins[3 * j + 2][...])
            outs[4 * j][...] = g
            outs[4 * j + 1][...] = d
            outs[4 * j + 2][...] = mn
            outs[4 * j + 3][...] = vn
        loss_ref[...] = row[:, SMALL_ROW["loss"][0]:SMALL_ROW["loss"][1]]

    flat = [t for p in params for t in p]
    res = pl.pallas_call(
        body, name="small_update", grid=(1,),
        out_shape=[jax.ShapeDtypeStruct(p[0].shape, F32) for p in params for _ in range(4)]
        + [jax.ShapeDtypeStruct((1, HEAD_LANES), F32)],
        in_specs=[_full(parts_all.shape)] + [_full(t.shape) for t in flat],
        out_specs=[_full(p[0].shape) for p in params for _ in range(4)] + [_full((1, HEAD_LANES))],
        compiler_params=_params(1),
    )(parts_all, *flat)
    return [res[4 * j:4 * j + 4] for j in range(k)], res[4 * k]


def _rot(t):
    half = t.shape[-1] // 2
    return jnp.concatenate([-t[..., half:], t[..., :half]], axis=-1)


def _rot_t(g):
    half = g.shape[-1] // 2
    return jnp.concatenate([g[..., half:], -g[..., :half]], axis=-1)


def _columns(segments, lo, hi):
    out, at = [], 0
    for seg in segments:
        n = seg.shape[1]
        a, b = max(lo, at), min(hi, at + n)
        if a < b:
            out.append(seg[:, a - at:b - at])
        at += n
    return out


def _prepare_weights(w_in_blocks, w_uq, w_ukv):
    o = [0]
    for s in IN_SPLITS:
        o.append(o[-1] + s)
    part = lambda a, b: _columns(w_in_blocks, a, b)
    wa = jnp.concatenate(part(0, o[2]) + part(o[3], o[8]), axis=1)
    kr = jnp.concatenate(part(o[2], o[3]), axis=1)
    zc = lambda n: jnp.zeros((kr.shape[0], n), kr.dtype)
    wkr2 = jnp.concatenate([zc(64), kr, zc(32), zc(64), _rot(kr), zc(32)], axis=1)
    uq = w_uq.reshape(Q_LORA, N_HEADS, MLA_NOPE + MLA_ROPE)
    zq = jnp.zeros((Q_LORA, N_HEADS, 32), w_uq.dtype)
    uq_full = jnp.concatenate([uq, zq], axis=-1).reshape(Q_LORA, 1024)
    uq_rot = jnp.concatenate([jnp.zeros((Q_LORA, N_HEADS, 64), w_uq.dtype), _rot(uq[..., MLA_NOPE:]), zq],
                             axis=-1).reshape(Q_LORA, 1024)
    wq2 = jnp.concatenate([uq_full, uq_rot], axis=1)
    ukv = w_ukv.reshape(KV_LORA, N_HEADS, 128)
    k_full = jnp.concatenate([ukv[..., :64], jnp.zeros((KV_LORA, N_HEADS, 64), w_ukv.dtype)], axis=-1).reshape(KV_LORA, 1024)
    wkv = jnp.concatenate([k_full, ukv[..., 64:].reshape(KV_LORA, 512)], axis=1)
    return wa, wkr2, wq2, wkv


def _restore_grads(gwa, gwkr2, gwq2, gwkv):
    gkr = gwkr2[:, 64:96] + _rot_t(gwkr2[:, 192:224])
    in_order = [gwa[:, :A_GM], gkr, gwa[:, A_GM:]]
    n = D_IN // 4
    g_in = [jnp.concatenate(_columns(in_order, k * n, (k + 1) * n), axis=1) for k in range(4)]
    gf = gwq2[:, :1024].reshape(Q_LORA, N_HEADS, 128)
    gr = gwq2[:, 1024:].reshape(Q_LORA, N_HEADS, 128)
    g_uq = jnp.concatenate([gf[..., :64], gf[..., 64:96] + _rot_t(gr[..., 64:96])], axis=-1).reshape(Q_LORA, 768)
    gk = gwkv[:, :1024].reshape(KV_LORA, N_HEADS, 128)[..., :64]
    gv = gwkv[:, 1024:].reshape(KV_LORA, N_HEADS, 64)
    g_ukv = jnp.concatenate([gk, gv], axis=-1).reshape(KV_LORA, 1024)
    return g_in, g_uq, g_ukv


def _local_step(x, positions, target, mod_rows, b_ada, ng, qg, kvg, sinks, fg, w_in_b, w_uq_b, w_ukv_b, w_out_b):
    n_seq, seq, _ = x.shape
    n_tok = n_seq * seq
    x2d = x.reshape(n_tok, D_MODEL)
    t2d = target.reshape(n_tok, D_MODEL)
    pos_f = positions.astype(F32)
    pos_col = pos_f.reshape(n_tok, 1)
    pos_row = pos_f.reshape(n_tok // SWA_WINDOW, 1, SWA_WINDOW)
    mod3 = mod_rows.reshape(n_seq, 1, 3 * D_MODEL)
    inv = ROPE_THETA ** (-jnp.arange(0, MLA_ROPE, 2, dtype=F32) / MLA_ROPE)
    inv128 = jnp.concatenate([jnp.zeros((64,), F32), inv, inv, jnp.zeros((32,), F32)]).reshape(1, 128)
    fg2 = fg.reshape(1, D_MODEL)

    wa, wkr2, wq2, wkv = _prepare_weights(w_in_b, w_uq_b, w_ukv_b)

    zqkv, gates, qf, kf, v, qs, kd, vd, rope = _pre_call(x2d, pos_col, mod3, b_ada, ng, qg, kvg, inv128, wa, wkr2, wq2, wkv, seq)
    o_mla, lse_mla = _mla_fwd_call(qf, kf, v, n_seq, seq)
    o_swa, lse_swa = _swa_fwd_call(qs, kd, vd, pos_col, pos_row, sinks, n_seq, seq)
    dx2, do, dg, g_out, g_fg, dgate, loss = _post_call(x2d, t2d, o_mla, o_swa, gates, mod3, b_ada, fg2, w_out_b, seq)
    dqf, dkf, dv = _mla_bwd_call(qf, kf, v, do, o_mla, lse_mla, n_seq, seq)
    dqs, dkd, dvd, dsink = _swa_bwd_call(qs, kd, vd, do, o_swa, lse_swa, pos_col, pos_row, sinks, n_seq, seq)
    dz, dkr, g_wq2, g_wkv, g_qg, g_kvg = _mid_bwd_call(dqf, dkf, dv, zqkv, rope, qg, kvg, wq2, wkv, seq)
    gx, g_wa, g_wkr2, g_ng, dshift, dscale = _in_bwd_call(x2d, dx2, dz, dkr, dg, dqs, dkd, dvd, mod3, b_ada, ng,
                                                         wa, wkr2, seq)
    g_in, g_uq, g_ukv = _restore_grads(g_wa, g_wkr2, g_wq2, g_wkv)
    dmod = jnp.concatenate([dshift, dscale, dgate], axis=-1).reshape(n_seq, 3 * D_MODEL)
    small_row = jnp.concatenate([g_ng, g_fg, g_qg, g_kvg, jnp.pad(jnp.sum(dsink, axis=1).reshape(1, N_HEADS), ((0, 0), (0, 120))),
                                 loss, jnp.zeros((1, 128), F32)], axis=1)
    return gx.reshape(x.shape), (g_in, g_uq, g_ukv, g_out), small_row, dmod


def kernel(x, c, positions, w_ada, b_ada, norm_gain, w_in, q_norm_gain, kv_norm_gain, w_uq, w_ukv, swa_sinks, w_out, final_gain, loss_target, m_w_ada, m_b_ada, m_norm_gain, m_w_in, m_q_norm_gain, m_kv_norm_gain, m_w_uq, m_w_ukv, m_swa_sinks, m_w_out, m_final_gain, v_w_ada, v_b_ada, v_norm_gain, v_w_in, v_q_norm_gain, v_kv_norm_gain, v_w_uq, v_w_ukv, v_swa_sinks, v_w_out, v_final_gain):
    n_seq = x.shape[0]
    xi, yi, ci = lax.axis_index("x"), lax.axis_index("y"), lax.axis_index("c")
    dev = 4 * xi + 2 * yi + ci
    chip = 2 * xi + yi

    halves = lambda w: w.astype(BF16).reshape(2, w.shape[0] // 2, w.shape[1])
    c_blk = jnp.pad(c, ((0, ROWS_PER_DEVICE - n_seq), (0, 0)))
    act_all, pieces, f_in, f_uq, f_ukv, f_out = _comm_fwd_call(
        c_blk, w_ada[0], [halves(w_in[0]), halves(w_uq[0]), halves(w_ukv[0]), halves(w_out[0])])
    mine = lax.dynamic_slice_in_dim(pieces, dev * ROWS_PER_DEVICE, n_seq, axis=1)
    mod_rows = jnp.transpose(mine, (1, 0, 2)).reshape(n_seq, 3 * D_MODEL)
    cols = lambda t, r: jnp.transpose(t.reshape(4, r, -1), (1, 0, 2)).reshape(r, -1)
    w_in_blocks = [f_in[k].reshape(D_MODEL, -1) for k in range(4)]
    w_uq_b, w_ukv_b = cols(f_uq, Q_LORA), cols(f_ukv, KV_LORA)
    w_out_b = f_out.reshape(D_MODEL, D_MODEL)

    gx, (g_in_blocks, g_uq, g_ukv, g_out), small_row, dmod = _local_step(
        x, positions, loss_target, mod_rows, b_ada, norm_gain, q_norm_gain, kv_norm_gain, swa_sinks, final_gain,
        w_in_blocks, w_uq_b, w_ukv_b, w_out_b)

    by_owner = lambda g, n: jnp.transpose(g.reshape(g.shape[0], 4, n), (1, 0, 2)).reshape(4, 2, g.shape[0] // 2, n)
    grads = [jnp.stack(g_in_blocks).reshape(4, 2, D_MODEL // 2, -1), by_owner(g_uq, 192), by_owner(g_ukv, 256),
             g_out.reshape(4, 2, 128, D_MODEL)]
    part = jnp.concatenate([dmod, small_row, jnp.zeros((ROWS_PER_DEVICE - n_seq - 1, 3 * D_MODEL), F32)], axis=0)
    r_in, r_uq, r_ukv, r_out, parts_all = _comm_bwd_call(grads, part)
    g_in_s, g_uq_s = r_in.reshape(w_in.shape[1:]), r_uq.reshape(w_uq.shape[1:])
    g_ukv_s, g_out_s = r_ukv.reshape(w_ukv.shape[1:]), r_out.reshape(w_out.shape[1:])

    tr = lambda a: jnp.swapaxes(a[0], 0, 1)
    back = lambda ts: [jnp.swapaxes(t, 0, 1) for t in ts]
    d_in, nm_in, nv_in = back(_adam_call("adam_w_in", tr(w_in), g_in_s.T, tr(m_w_in), tr(v_w_in)))
    d_uq, nm_uq, nv_uq = back(_adam_call("adam_w_uq", tr(w_uq), g_uq_s.T, tr(m_w_uq), tr(v_w_uq)))
    d_ukv, nm_ukv, nv_ukv = _adam_call("adam_w_ukv", w_ukv[0], g_ukv_s, m_w_ukv[0], v_w_ukv[0])
    d_out, nm_out, nv_out = _adam_call("adam_w_out", w_out[0], g_out_s, m_w_out[0], v_w_out[0])
    dmod_cols = lax.dynamic_slice_in_dim(parts_all, chip * 768, 768, axis=1)
    g_ada, d_ada, nm_ada, nv_ada = _ada_bwd_call(act_all, dmod_cols, w_ada[0], m_w_ada[0], v_w_ada[0])

    row = lambda t: t.reshape(1, -1)
    small = {"b_ada": (b_ada, m_b_ada, v_b_ada), "norm_gain": (norm_gain, m_norm_gain, v_norm_gain),
             "q_norm_gain": (q_norm_gain, m_q_norm_gain, v_q_norm_gain),
             "kv_norm_gain": (kv_norm_gain, m_kv_norm_gain, v_kv_norm_gain),
             "swa_sinks": (swa_sinks, m_swa_sinks, v_swa_sinks),
             "final_gain": (row(final_gain), row(m_final_gain), row(v_final_gain))}
    res, loss_row = _small_call(parts_all, n_seq, [small[name] for name in SMALL_ORDER])
    res = dict(zip(SMALL_ORDER, res))
    res["final_gain"] = [t.reshape(-1) for t in res["final_gain"]]
    e = lambda t: t[None]
    big = {"w_ada": (e(g_ada), e(d_ada), e(nm_ada), e(nv_ada)), "w_in": (e(g_in_s), e(d_in), e(nm_in), e(nv_in)),
           "w_uq": (e(g_uq_s), e(d_uq), e(nm_uq), e(nv_uq)), "w_ukv": (e(g_ukv_s), e(d_ukv), e(nm_ukv), e(nv_ukv)),
           "w_out": (e(g_out_s), e(d_out), e(nm_out), e(nv_out))}
    order = ("w_ada", "b_ada", "norm_gain", "w_in", "q_norm_gain", "kv_norm_gain", "w_uq", "w_ukv", "swa_sinks", "w_out",
             "final_gain")
    pick = lambda kind: [(big[n] if n in big else res[n])[kind] for n in order]
    return (loss_row[0, 0], gx, *pick(0), *pick(1), *pick(2), *pick(3))
```

```python
import functools

import jax
import jax.numpy as jnp
from jax import lax
from jax.experimental import pallas as pl
from jax.experimental.pallas import tpu as pltpu

F32 = jnp.float32
BF16 = jnp.bfloat16

D_MODEL = 1024
Q_LORA = 384
KV_LORA = 256
N_HEADS = 8
MLA_NOPE = 64
MLA_ROPE = 32
HEAD_LANES = 128
HALF = 64
SWA_WINDOW = 128
EPS = 1e-6
ROPE_THETA = 10000.0
MLA_SCALE = (MLA_NOPE + MLA_ROPE) ** -0.5
LOG2E = 1.4426950408889634
LN2 = 0.6931471805599453
SWA_SCALE = 64 ** -0.5
NEG = -1e30

ADAM_LR = 0.001
ADAM_B1 = 0.9
ADAM_B2 = 0.999
ADAM_EPS = 1e-08
ADAM_WD = 0.01
ADAM_STEP = 10

A_ZQ, A_ZKV, A_KR, A_GM, A_QS, A_KS, A_VS, A_GS, A_END = 0, 384, 640, 768, 1280, 1792, 1920, 2048, 2560
IN_SPLITS = (384, 256, 32, 512, 512, 128, 128, 512)
D_IN = sum(IN_SPLITS)

TOKEN_TILE = 512
ATT_TILE = 256
VMEM_LIMIT = 56 * 1024 * 1024


def _dot(a, b):
    return jnp.dot(a, b, preferred_element_type=F32)


def _dot_nt(a, b):
    return lax.dot_general(a, b, (((1,), (1,)), ((), ())), preferred_element_type=F32)


def _dot_tn(a, b):
    return lax.dot_general(a, b, (((0,), (0,)), ((), ())), preferred_element_type=F32)


def _params(n_grid):
    return pltpu.CompilerParams(dimension_semantics=("arbitrary",) * n_grid, vmem_limit_bytes=VMEM_LIMIT)


def _full(shape):
    nd = len(shape)
    return pl.BlockSpec(shape, lambda *_: (0,) * nd, pipeline_mode=pl.Buffered(1))


def _sigmoid(g):
    return 1.0 / (1.0 + jnp.exp(-g))


SUB_TILE = 256


def _sub_tiles(tm):
    sub = min(SUB_TILE, tm)
    return [slice(s * sub, (s + 1) * sub) for s in range(tm // sub)]


MESH = pl.DeviceIdType.MESH
ROWS_PER_DEVICE = 8
VMEM_SPEC = pl.BlockSpec(memory_space=pltpu.VMEM)
ANY_SPEC = pl.BlockSpec(memory_space=pl.ANY)


def _position():
    x, y, c = lax.axis_index("x"), lax.axis_index("y"), lax.axis_index("c")
    sibling = (x, y, 1 - c)
    others = [(1 - x, y, c), (x, 1 - y, c), (1 - x, 1 - y, c)]
    return (x, y, c), 4 * x + 2 * y + c, 2 * x + y, sibling, others


def _rows_of(dev):
    return pl.ds(pl.multiple_of(dev * ROWS_PER_DEVICE, ROWS_PER_DEVICE), ROWS_PER_DEVICE)


def _all_to_all_rows(block_ref, table_ref, dev, me, send_sems, recv_sems):
    x, y, c = me
    waits = []
    for k in range(1, 8):
        peer = (1 - x if k & 4 else x, 1 - y if k & 2 else y, 1 - c if k & 1 else c)
        pltpu.make_async_remote_copy(src_ref=block_ref, dst_ref=table_ref.at[_rows_of(dev)], send_sem=send_sems.at[k - 1],
                                     recv_sem=recv_sems.at[k - 1], device_id=peer, device_id_type=MESH).start()
        waits.append(pltpu.make_async_remote_copy(
            src_ref=block_ref, dst_ref=table_ref.at[_rows_of(jnp.bitwise_xor(dev, k))], send_sem=send_sems.at[k - 1],
            recv_sem=recv_sems.at[k - 1], device_id=peer, device_id_type=MESH))
    return waits


def _comm_fwd_call(c_blk, w_ada, shards):
    n = len(shards)

    def body(c_ref, wada_ref, *refs):
        w_refs, act_ref, pieces_ref, full_refs = refs[:n], refs[n], refs[n + 1], refs[n + 2:2 * n + 2]
        c_all_ref = refs[2 * n + 2]
        c_send, c_recv, p_send, p_recv, w_send, w_recv, f_send, f_recv, loc_sem = refs[2 * n + 3:]
        me, dev, chip, sibling, others = _position()
        core = me[2]
        chip_of = [2 * p[0] + p[1] for p in others]

        local = [pltpu.make_async_copy(w_refs[i], full_refs[i].at[chip], loc_sem.at[i]) for i in range(n)]
        for cp in local:
            cp.start()

        def over_ici(i, j, src_chip):
            return pltpu.make_async_remote_copy(
                src_ref=w_refs[i].at[core], dst_ref=full_refs[i].at[src_chip, core], send_sem=w_send.at[3 * i + j],
                recv_sem=w_recv.at[3 * i + j], device_id=others[j], device_id_type=MESH)

        def to_sibling(i, j, half):
            return pltpu.make_async_remote_copy(
                src_ref=full_refs[i].at[chip_of[j], half], dst_ref=full_refs[i].at[chip_of[j], half],
                send_sem=f_send.at[3 * i + j], recv_sem=f_recv.at[3 * i + j], device_id=sibling, device_id_type=MESH)

        sent = [over_ici(i, j, chip) for i in range(n) for j in range(3)]
        for cp in sent:
            cp.start()

        c_all_ref[_rows_of(dev), :] = c_ref[...]
        c_waits = _all_to_all_rows(c_ref, c_all_ref, dev, me, c_send, c_recv)
        for cp in c_waits:
            cp.wait()
        cv = c_all_ref[...]
        act = cv * _sigmoid(cv)
        act_ref[...] = act
        pieces_ref[chip] = _dot(act.astype(BF16), wada_ref[...].astype(BF16))
        piece = lambda j, src_chip: pltpu.make_async_remote_copy(
            src_ref=pieces_ref.at[chip], dst_ref=pieces_ref.at[src_chip], send_sem=p_send.at[j], recv_sem=p_recv.at[j],
            device_id=others[j], device_id_type=MESH)
        for j in range(3):
            piece(j, chip).start()
        for j in range(3):
            piece(j, chip).wait_send()
            piece(j, chip_of[j]).wait_recv()

        for i in range(n):
            for j in range(3):
                over_ici(i, j, chip_of[j]).wait_recv()
                to_sibling(i, j, core).start()
        for i in range(n):
            for j in range(3):
                to_sibling(i, j, 1 - core).wait_recv()
                to_sibling(i, j, core).wait_send()
        for cp in sent:
            cp.wait_send()
        for cp in local:
            cp.wait()

    rows = 8 * ROWS_PER_DEVICE
    dma = pltpu.SemaphoreType.DMA
    return pl.pallas_call(
        body, name="comm_fwd",
        out_shape=[jax.ShapeDtypeStruct((rows, D_MODEL), F32), jax.ShapeDtypeStruct((4, rows, w_ada.shape[1]), F32)]
        + [jax.ShapeDtypeStruct((4,) + s.shape, s.dtype) for s in shards],
        in_specs=[VMEM_SPEC, VMEM_SPEC] + [ANY_SPEC] * n,
        out_specs=[VMEM_SPEC, VMEM_SPEC] + [ANY_SPEC] * n,
        scratch_shapes=[pltpu.VMEM((rows, D_MODEL), F32), dma((7,)), dma((7,)), dma((3,)), dma((3,)),
                        dma((3 * n,)), dma((3 * n,)), dma((3 * n,)), dma((3 * n,)), dma((n,))],
        compiler_params=pltpu.CompilerParams(vmem_limit_bytes=VMEM_LIMIT),
    )(c_blk, w_ada, *shards)


def _comm_bwd_call(grads, part):
    n = len(grads)

    def body(part_ref, *refs):
        g_refs, f_refs, parts_ref = refs[:n], refs[n:2 * n], refs[2 * n]
        scratch = refs[2 * n + 1:]
        a_refs, b_refs, p_refs, r_refs = (scratch[k * n:(k + 1) * n] for k in range(4))
        s_send, s_recv, d_send, d_recv, e_send, e_recv, h_send, h_recv, loc_sem = scratch[4 * n:]
        me, dev, chip, sibling, others = _position()
        core = me[2]
        chip_of = [2 * p[0] + p[1] for p in others]

        parts_ref[_rows_of(dev), :] = part_ref[...]
        s_waits = _all_to_all_rows(part_ref, parts_ref, dev, me, s_send, s_recv)

        mine = [pltpu.make_async_copy(g_refs[i].at[:, core], a_refs[i], loc_sem.at[i]) for i in range(n)]
        swap = [pltpu.make_async_remote_copy(src_ref=g_refs[i].at[:, 1 - core], dst_ref=b_refs[i], send_sem=d_send.at[i],
                                             recv_sem=d_recv.at[i], device_id=sibling, device_id_type=MESH) for i in range(n)]
        order = sorted(range(n), key=lambda i: g_refs[i].shape[2] * g_refs[i].shape[3])
        for i in order:
            mine[i].start()
            swap[i].start()
        cross = [pltpu.make_async_remote_copy(src_ref=p_refs[i].at[chip_of[j]], dst_ref=r_refs[i].at[j],
                                              send_sem=e_send.at[3 * i + j], recv_sem=e_recv.at[3 * i + j],
                                              device_id=others[j], device_id_type=MESH) for i in range(n) for j in range(3)]
        for i in order:
            mine[i].wait()
            swap[i].wait()
            for k in range(4):
                s = a_refs[i][k] + b_refs[i][k]
                a_refs[i][k] = s
                p_refs[i][k] = s.astype(BF16)
            for j in range(3):
                cross[3 * i + j].start()
        share = {}
        for i in order:
            for j in range(3):
                cross[3 * i + j].wait()
            f_refs[i][core] = (a_refs[i][chip] + r_refs[i][0].astype(F32) + r_refs[i][1].astype(F32)
                               + r_refs[i][2].astype(F32))
            share[i] = pltpu.make_async_remote_copy(src_ref=f_refs[i].at[core], dst_ref=f_refs[i].at[core],
                                                    send_sem=h_send.at[i], recv_sem=h_recv.at[i], device_id=sibling,
                                                    device_id_type=MESH)
            share[i].start()
        for i in range(n):
            share[i].wait_send()
            pltpu.make_async_remote_copy(src_ref=f_refs[i].at[core], dst_ref=f_refs[i].at[1 - core], send_sem=h_send.at[i],
                                         recv_sem=h_recv.at[i], device_id=sibling, device_id_type=MESH).wait_recv()
        for cp in s_waits:
            cp.wait()

    rows = 8 * ROWS_PER_DEVICE
    dma = pltpu.SemaphoreType.DMA
    quarter = [(4,) + g.shape[2:] for g in grads]
    return pl.pallas_call(
        body, name="comm_bwd",
        out_shape=[jax.ShapeDtypeStruct((2,) + g.shape[2:], F32) for g in grads]
        + [jax.ShapeDtypeStruct((rows, part.shape[1]), F32)],
        in_specs=[VMEM_SPEC] + [ANY_SPEC] * n,
        out_specs=[VMEM_SPEC] * (n + 1),
        scratch_shapes=[pltpu.VMEM(q, F32) for q in quarter] + [pltpu.VMEM(q, F32) for q in quarter]
        + [pltpu.VMEM(q, BF16) for q in quarter] + [pltpu.VMEM((3,) + q[1:], BF16) for q in quarter]
        + [dma((7,)), dma((7,)), dma((n,)), dma((n,)), dma((3 * n,)), dma((3 * n,)), dma((n,)), dma((n,)), dma((n,))],
        compiler_params=pltpu.CompilerParams(vmem_limit_bytes=VMEM_LIMIT),
    )(part, *grads)


def _twice(t):
    lo = _lane_lo()
    other = pltpu.roll(t, HALF, 1)
    return jnp.concatenate([jnp.where(lo, t, other), jnp.where(lo, other, t)], axis=1)


def _once(g):
    first, second = g[:, :HEAD_LANES], g[:, HEAD_LANES:]
    return jnp.where(_lane_lo(), first + pltpu.roll(first, HALF, 1), second + pltpu.roll(second, HALF, 1))


def _rope_tables(pos_col, inv_row):
    ang = pos_col * inv_row
    return jnp.cos(ang), jnp.sin(ang)


def _pre_call(x, pos_col, mod, b_ada, ng, qg, kvg, inv128, wa, wq2, wkv, seq):
    n_tok = x.shape[0]
    tm = min(TOKEN_TILE, seq)
    per_seq = seq // tm

    def body(x_ref, pos_ref, mod_ref, bada_ref, ng_ref, qg_ref, kvg_ref, inv_ref, wa_ref, wq_ref, wkv_ref,
             zqkv_ref, gates_ref, qf_ref, kf_ref, v_ref, qs_ref, kd_ref, vd_ref, rope_ref):
        xv = x_ref[...]
        modv = mod_ref[0] + bada_ref[...]
        shift, scale = modv[:, :D_MODEL], modv[:, D_MODEL:2 * D_MODEL]
        r1 = lax.rsqrt(jnp.mean(xv * xv, axis=-1, keepdims=True) + EPS)
        h = ((xv * r1) * ng_ref[...]) * (1.0 + scale) + shift
        hb = h.astype(BF16)
        za = _dot(hb, wa_ref[...])
        zkr = za[:, A_KR:A_GM]
        cos, sin = _rope_tables(pos_ref[...], inv_ref[...])
        rope_ref[:, :HEAD_LANES] = cos
        rope_ref[:, HEAD_LANES:] = sin
        zqkv_ref[...] = za[:, :A_KR]
        gates_ref[:, :512] = za[:, A_GM:A_QS]
        gates_ref[:, 512:] = za[:, A_GS:A_END]
        qs_ref[...] = (za[:, A_QS:A_KS] * (SWA_SCALE * LOG2E)).astype(BF16)
        kd_ref[...] = _twice(za[:, A_KS:A_VS]).astype(BF16)
        vd_ref[...] = _twice(za[:, A_VS:A_GS]).astype(BF16)
        zq, zkv = za[:, A_ZQ:A_ZKV], za[:, A_ZKV:A_KR]
        rq = lax.rsqrt(jnp.mean(zq * zq, axis=-1, keepdims=True) + EPS)
        qn = ((zq * rq) * qg_ref[...]).astype(BF16)
        qr = _dot(qn, wq_ref[...])
        cf, sf = jnp.tile(cos, (1, N_HEADS)), jnp.tile(sin, (1, N_HEADS))
        qf_ref[...] = ((qr[:, :1024] * cf + qr[:, 1024:] * sf) * (MLA_SCALE * LOG2E)).astype(BF16)
        rkv = lax.rsqrt(jnp.mean(zkv * zkv, axis=-1, keepdims=True) + EPS)
        kvn = ((zkv * rkv) * kvg_ref[...]).astype(BF16)
        kv = _dot(kvn, wkv_ref[...])
        kpe = jnp.where(_lane_lo(), 0.0, zkr * cos) + pltpu.roll(zkr, HALF, 1) * sin
        kf_ref[...] = (kv[:, :1024] + jnp.tile(kpe, (1, N_HEADS))).astype(BF16)
        v_ref[...] = kv[:, 1024:].astype(BF16)

    tok = lambda w: pl.BlockSpec((tm, w), lambda i: (i, 0))
    outs = [(640, F32), (1024, F32), (1024, BF16), (1024, BF16), (512, BF16), (512, BF16), (256, BF16), (256, BF16),
            (2 * HEAD_LANES, F32)]
    return pl.pallas_call(
        body, name="pre", grid=(n_tok // tm,),
        out_shape=[jax.ShapeDtypeStruct((n_tok, w), dt) for w, dt in outs],
        in_specs=[tok(D_MODEL), tok(1), pl.BlockSpec((1, 1, 3 * D_MODEL), lambda i: (i // per_seq, 0, 0)),
                  _full(b_ada.shape), _full(ng.shape), _full(qg.shape), _full(kvg.shape), _full(inv128.shape),
                  _full(wa.shape), _full(wq2.shape), _full(wkv.shape)],
        out_specs=[tok(w) for w, _ in outs],
        compiler_params=_params(1),
    )(x, pos_col, mod, b_ada, ng, qg, kvg, inv128, wa, wq2, wkv)


def _lane_lo(width=HEAD_LANES):
    return lax.broadcasted_iota(jnp.int32, (1, width), 1) < HALF


def _eye(n=HEAD_LANES):
    r = lax.broadcasted_iota(jnp.int32, (n, n), 0)
    c = lax.broadcasted_iota(jnp.int32, (n, n), 1)
    return jnp.where(r == c, 1.0, 0.0).astype(BF16)


def _mla_fwd_call(qf, kf, v, n_seq, seq):
    tq = min(ATT_TILE, seq)
    nq = seq // tq

    ext = HALF + 16

    def body(q_ref, k_ref, v_ref, o_ref, lse_ref, vt_ref):
        i = pl.program_id(1)
        eye = _eye()

        @pl.when(i == 0)
        def _():
            for h in range(N_HEADS):
                vt_ref[h * ext + HALF:(h + 1) * ext, :] = jnp.ones((16, seq), BF16)
            for t in range(nq):
                for p in range(N_HEADS // 2):
                    pair = slice(p * HEAD_LANES, (p + 1) * HEAD_LANES)
                    v_t = _dot_nt(eye, v_ref[t * tq:(t + 1) * tq, pair]).astype(BF16)
                    for hh in range(2):
                        r0 = (2 * p + hh) * ext
                        vt_ref[r0:r0 + HALF, t * tq:(t + 1) * tq] = v_t[hh * HALF:(hh + 1) * HALF, :]

        q = q_ref[...]
        qcol = i * tq + lax.broadcasted_iota(jnp.int32, (1, tq), 1)
        heads = range(N_HEADS)
        lanes = [slice(h * HEAD_LANES, (h + 1) * HEAD_LANES) for h in heads]

        def make_step(masked, n_tiles):
            def step(kt0, carry):
                tiles = range(n_tiles)
                start = pl.multiple_of(kt0 * tq, tq)
                ks = [k_ref[pl.ds(pl.multiple_of((kt0 + t) * tq, tq), tq), :] for t in tiles]
                vt = vt_ref[:, pl.ds(start, n_tiles * tq)]
                last = n_tiles - 1
                if masked:
                    keep = ((kt0 + last) * tq + lax.broadcasted_iota(jnp.int32, (tq, 1), 0)) <= qcol

                def scores(h):
                    sts = [_dot_nt(ks[t][:, lanes[h]], q[:, lanes[h]]) for t in tiles]
                    if masked:
                        sts[last] = jnp.where(keep, sts[last], NEG)
                    return sts

                def softmax(h, sts):
                    m_old = carry[2 * h]
                    m_new = m_old
                    for st in sts:
                        m_new = jnp.maximum(m_new, jnp.max(st, axis=0, keepdims=True))
                    pt = jnp.concatenate([jnp.exp2(st - m_new).astype(BF16) for st in sts], axis=0)
                    return m_new, jnp.exp2(m_old - m_new), pt

                def values(h, alpha, pt):
                    return carry[2 * h + 1] * alpha + _dot(vt[h * ext:(h + 1) * ext, :], pt)

                sts, soft, out = {0: scores(0), 1: scores(1)}, {}, {}
                for h in range(N_HEADS + 1):
                    if h + 2 < N_HEADS:
                        sts[h + 2] = scores(h + 2)
                    if h < N_HEADS:
                        soft[h] = softmax(h, sts.pop(h))
                    if h >= 1:
                        m_new, alpha, pt = soft.pop(h - 1)
                        out[h - 1] = (m_new, values(h - 1, alpha, pt))
                return tuple(v for h in heads for v in out[h])
            return step

        init = (jnp.full((1, tq), NEG, F32), jnp.zeros((ext, tq), F32)) * N_HEADS
        count = i + 1
        carry = lax.fori_loop(0, (count + 1) // 2 - 1, lambda j, c: make_step(False, 2)(2 * j, c), init)
        carry = lax.cond(count % 2 == 0, lambda c: make_step(True, 2)(i - 1, c), lambda c: make_step(True, 1)(i, c), carry)
        dens = [carry[2 * h + 1][HALF:HALF + 1, :] for h in heads]
        acc_t = jnp.concatenate([carry[2 * h + 1][:HALF, :] * (1.0 / dens[h]) for h in heads], axis=0)
        o_ref[...] = acc_t.T
        for h in heads:
            lse_ref[0, h // 4, h % 4:h % 4 + 1, :] = carry[2 * h] + jnp.log2(dens[h])

    n_tok = qf.shape[0]
    return pl.pallas_call(
        body, name="mla_fwd", grid=(n_seq, nq),
        out_shape=[jax.ShapeDtypeStruct((n_tok, 512), F32), jax.ShapeDtypeStruct((n_seq, 2, 4, seq), F32)],
        in_specs=[pl.BlockSpec((tq, 1024), lambda b, i: (b * nq + i, 0)),
                  pl.BlockSpec((seq, 1024), lambda b, i: (b, 0)),
                  pl.BlockSpec((seq, 512), lambda b, i: (b, 0))],
        out_specs=[pl.BlockSpec((tq, 512), lambda b, i: (b * nq + i, 0)),
                   pl.BlockSpec((1, 2, 4, tq), lambda b, i: (b, 0, 0, i))],
        scratch_shapes=[pltpu.VMEM((N_HEADS * ext, seq), BF16)],
        compiler_params=_params(2),
    )(qf, kf, v)


def _mla_bwd_call(qf, kf, v, do, o, lse, n_seq, seq):
    tq = min(ATT_TILE, seq)
    nq = seq // tq

    nh = 4
    heads = range(nh)
    lanes = [slice(h * HEAD_LANES, (h + 1) * HEAD_LANES) for h in heads]

    def body(q_ref, k_ref, v_ref, do_ref, o_ref, lse_ref, dq_ref, dk_ref, dv_ref,
             kt_ref, dot_ref, delta_ref, dqt_ref):
        eye = _eye()
        lo = _lane_lo()
        sub_lo = lax.broadcasted_iota(jnp.int32, (HEAD_LANES, 1), 0) < HALF
        ones_lo = jnp.where(jnp.broadcast_to(lo, (8, HEAD_LANES)), 1.0, 0.0).astype(BF16)
        ones_hi = jnp.where(jnp.broadcast_to(lo, (8, HEAD_LANES)), 0.0, 1.0).astype(BF16)

        for t in range(nq):
            r = slice(t * tq, (t + 1) * tq)
            kv = k_ref[r, :]
            for h in heads:
                kt_ref[lanes[h], r] = _dot_nt(eye, kv[:, lanes[h]]).astype(BF16)
            for p in range(nh // 2):
                dov = do_ref[r, lanes[p]]
                dt = _dot_nt(eye, dov)
                dot_ref[2 * p, :, r] = jnp.where(sub_lo, dt, 0.0).astype(BF16)
                dot_ref[2 * p + 1, :, r] = jnp.where(sub_lo, 0.0, dt).astype(BF16)
                prod = dov.astype(F32) * o_ref[r, lanes[p]]
                p_hi = prod.astype(BF16)
                p_lo = (prod - p_hi.astype(F32)).astype(BF16)
                delta_ref[2 * p, :, r] = _dot_nt(ones_lo, p_hi) + _dot_nt(ones_lo, p_lo)
                delta_ref[2 * p + 1, :, r] = _dot_nt(ones_hi, p_hi) + _dot_nt(ones_hi, p_lo)
        dqt_ref[...] = jnp.zeros_like(dqt_ref)

        def k_step(kt, _):
            kr = pl.ds(pl.multiple_of(kt * tq, tq), tq)
            k = k_ref[kr, :]
            vv = v_ref[kr, :]
            k_t = kt_ref[:, kr]
            krow = kt * tq + lax.broadcasted_iota(jnp.int32, (tq, 1), 0)

            def make_step(masked, n_tiles):
                def q_step(qt0, carry):
                    tiles = range(n_tiles)
                    qrs = [pl.ds(pl.multiple_of((qt0 + t) * tq, tq), tq) for t in tiles]
                    qs = [q_ref[qr, :] for qr in qrs]
                    if masked:
                        keep = krow <= (qt0 * tq + lax.broadcasted_iota(jnp.int32, (1, tq), 1))

                    def scores(h):
                        do_ts = [dot_ref[h, :, qr] for qr in qrs]
                        sts = [_dot_nt(k[:, lanes[h]], qs[t][:, lanes[h]]) for t in tiles]
                        dpts = [_dot(vv[:, lanes[h // 2]], do_ts[t]) for t in tiles]
                        return do_ts, sts, dpts

                    def softmax(h, sts, dpts):
                        pts, dsts = [], []
                        for t in tiles:
                            pt = jnp.exp2(sts[t] - lse_ref[0, 0, h:h + 1, qrs[t]])
                            if masked and t == 0:
                                pt = jnp.where(keep, pt, 0.0)
                            dsts.append((pt * (dpts[t] - delta_ref[h, 0:1, qrs[t]])).astype(BF16))
                            pts.append(pt.astype(BF16))
                        return pts, dsts

                    def grads(h, do_ts, pts, dsts):
                        half = slice((h % 2) * HALF, (h % 2 + 1) * HALF)
                        dst_all = jnp.concatenate(dsts, axis=1)
                        pt_all = jnp.concatenate(pts, axis=1)
                        do_all = jnp.concatenate([do_ts[t][half, :] for t in tiles], axis=1)
                        q_all = jnp.concatenate([qs[t][:, lanes[h]] for t in tiles], axis=0)
                        dvt = _dot_nt(do_all, pt_all)
                        dk = _dot(dst_all, q_all)
                        for t in tiles:
                            dqt_ref[lanes[h], qrs[t]] += _dot(k_t[lanes[h], :], dsts[t])
                        return carry[2 * h] + dk, carry[2 * h + 1] + dvt

                    first, second, out = {0: scores(0)}, {}, {}
                    for h in range(nh + 1):
                        if h + 1 < nh:
                            first[h + 1] = scores(h + 1)
                        if h < nh:
                            do_ts, sts, dpts = first.pop(h)
                            second[h] = (do_ts,) + softmax(h, sts, dpts)
                        if h >= 1:
                            out[h - 1] = grads(h - 1, *second.pop(h - 1))
                    return tuple(v for h in heads for v in out[h])
                return q_step

            init = (jnp.zeros((tq, HEAD_LANES), F32), jnp.zeros((HALF, tq), F32)) * nh
            count = nq - kt
            carry = lax.cond(count >= 2, lambda c: make_step(True, 2)(kt, c), lambda c: make_step(True, 1)(kt, c), init)
            carry = lax.fori_loop(1, count // 2, lambda j, c: make_step(False, 2)(kt + 2 * j, c), carry)
            carry = lax.cond(jnp.logical_and(count % 2 == 1, count >= 3),
                             lambda c: make_step(False, 1)(nq - 1, c), lambda c: c, carry)
            for h in heads:
                dk_ref[kr, lanes[h]] = carry[2 * h]
            for p in range(nh // 2):
                dv_ref[kr, lanes[p]] = jnp.concatenate([carry[4 * p + 1], carry[4 * p + 3]], axis=0).T
            return 0

        lax.fori_loop(0, nq, k_step, 0)
        for t in range(nq):
            r = slice(t * tq, (t + 1) * tq)
            for h in heads:
                dq_ref[r, lanes[h]] = dqt_ref[lanes[h], r].T

    n_tok = qf.shape[0]
    groups = N_HEADS // nh
    blk = lambda w: pl.BlockSpec((seq, w), lambda b, g: (b, g))
    return pl.pallas_call(
        body, name="mla_bwd", grid=(n_seq, groups),
        out_shape=[jax.ShapeDtypeStruct((n_tok, 1024), F32), jax.ShapeDtypeStruct((n_tok, 1024), F32),
                   jax.ShapeDtypeStruct((n_tok, 512), F32)],
        in_specs=[blk(512), blk(512), blk(256), blk(256), blk(256),
                  pl.BlockSpec((1, 1, nh, seq), lambda b, g: (b, g, 0, 0))],
        out_specs=[blk(512), blk(512), blk(256)],
        scratch_shapes=[pltpu.VMEM((nh * HEAD_LANES, seq), BF16), pltpu.VMEM((nh, HEAD_LANES, seq), BF16),
                        pltpu.VMEM((nh, 8, seq), F32), pltpu.VMEM((nh * HEAD_LANES, seq), F32)],
        compiler_params=_params(2),
    )(qf, kf, v, do, o, lse)


SWA_BLOCKS = 4


def _swa_block(n, pos_col_ref, posq):
    w = SWA_WINDOW
    start = pl.multiple_of(jnp.maximum(n - 1, 0) * w, w)
    posk = pos_col_ref[pl.ds(start, 2 * w), :]
    rel = (n * w + lax.broadcasted_iota(jnp.int32, (1, w), 1)) - (start + lax.broadcasted_iota(jnp.int32, (2 * w, 1), 0))
    valid = jnp.logical_and(rel >= 0, rel < w)
    return start, jnp.where(valid, posq - posk, 1e30)


def _alibi(h):
    return LOG2E * 2.0 ** -(h + 1)


def _transpose_rows(eye, src_ref, dst_ref, seq, width):
    step = 2 * SWA_WINDOW
    for t in range(seq // step):
        for p in range(width // HEAD_LANES):
            lanes = slice(p * HEAD_LANES, (p + 1) * HEAD_LANES)
            dst_ref[lanes, t * step:(t + 1) * step] = _dot_nt(eye, src_ref[t * step:(t + 1) * step, lanes]).astype(BF16)


def _swa_fwd_call(qs, kd, vd, pos_col, pos_row, sinks, n_seq, seq):
    w = SWA_WINDOW
    qb = SWA_BLOCKS
    steps = seq // (qb * w)
    ext = HALF + 16

    def body(q_ref, k_ref, v_ref, pc_ref, pr_ref, sink_ref, o_ref, lse_ref, vt_ref):
        n = pl.program_id(1)
        lo = _lane_lo()
        hi = jnp.logical_not(lo)
        eye = _eye()

        @pl.when(n == 0)
        def _():
            step = 2 * w
            for kv in range(2):
                vt_ref[kv * ext + HALF:(kv + 1) * ext, :] = jnp.ones((16, seq), BF16)
                for t in range(seq // step):
                    v_t = _dot_nt(eye, v_ref[t * step:(t + 1) * step, kv * HEAD_LANES:(kv + 1) * HEAD_LANES])
                    vt_ref[kv * ext:kv * ext + HALF, t * step:(t + 1) * step] = v_t[:HALF, :].astype(BF16)

        heads = range(N_HEADS)
        blocks = range(qb)
        geo = [_swa_block(n * qb + bi, pc_ref, pr_ref[bi]) for bi in blocks]
        wins = [pl.ds(g[0], 2 * w) for g in geo]
        kwins = [k_ref[win, :] for win in wins]
        vts = [vt_ref[:, win] for win in wins]
        sts = []
        for bi in blocks:
            q = q_ref[bi * w:(bi + 1) * w, :]
            sts.append([])
            for h in heads:
                qp = q[:, (h // 2) * HEAD_LANES:(h // 2 + 1) * HEAD_LANES]
                qh = jnp.where(lo if h % 2 == 0 else hi, qp, jnp.zeros_like(qp))
                sts[bi].append(_dot_nt(kwins[bi][:, (h // 4) * HEAD_LANES:(h // 4 + 1) * HEAD_LANES], qh))
        ps, ms = [], []
        for bi in blocks:
            ps.append([])
            ms.append([])
            for h in heads:
                s = sts[bi][h] - _alibi(h) * geo[bi][1]
                m = jnp.maximum(jnp.max(s, axis=0, keepdims=True), sink_ref[0, h] * LOG2E)
                ps[bi].append(jnp.exp2(s - m).astype(BF16))
                ms[bi].append(m)
        for bi in blocks:
            ots = []
            for h in heads:
                pv = _dot(vts[bi][(h // 4) * ext:(h // 4 + 1) * ext, :], ps[bi][h])
                l = pv[HALF:HALF + 1, :] + jnp.exp2(sink_ref[0, h] * LOG2E - ms[bi][h])
                ots.append(pv[:HALF, :] * (1.0 / l))
                lse_ref[0, h:h + 1, bi * w:(bi + 1) * w] = ms[bi][h] + jnp.log2(l)
            o_ref[bi * w:(bi + 1) * w, :] = jnp.concatenate(ots, axis=0).T

    n_tok = qs.shape[0]
    tok = lambda width: pl.BlockSpec((qb * w, width), lambda b, n: (b * steps + n, 0))
    whole = lambda width: pl.BlockSpec((seq, width), lambda b, n: (b, 0))
    return pl.pallas_call(
        body, name="swa_fwd", grid=(n_seq, steps),
        out_shape=[jax.ShapeDtypeStruct((n_tok, 512), F32), jax.ShapeDtypeStruct((n_seq, N_HEADS, seq), F32)],
        in_specs=[tok(512), whole(256), whole(256), whole(1), pl.BlockSpec((qb, 1, w), lambda b, n: (b * steps + n, 0, 0)),
                  pl.BlockSpec(memory_space=pltpu.SMEM)],
        out_specs=[tok(512), pl.BlockSpec((1, N_HEADS, qb * w), lambda b, n: (b, 0, n))],
        scratch_shapes=[pltpu.VMEM((2 * ext, seq), BF16)],
        compiler_params=_params(2),
    )(qs, kd, vd, pos_col, pos_row, sinks)


def _swa_bwd_call(qs, kd, vd, do, o, lse, pos_col, pos_row, sinks, n_seq, seq):
    w = SWA_WINDOW
    qb = SWA_BLOCKS
    steps = seq // (qb * w)

    def body(q_ref, k_ref, v_ref, do_ref, o_ref, lse_ref, pc_ref, pr_ref, sink_ref, dq_ref, dk_ref, dv_ref, dsink_ref,
             kt_ref):
        b, n = pl.program_id(0), pl.program_id(1)
        lo = _lane_lo()
        hi = jnp.logical_not(lo)
        sub_lo = lax.broadcasted_iota(jnp.int32, (HEAD_LANES, 1), 0) < HALF
        eye = _eye()
        ones_lo = jnp.where(jnp.broadcast_to(lo, (8, HEAD_LANES)), 1.0, 0.0).astype(BF16)
        ones_hi = jnp.where(jnp.broadcast_to(lo, (8, HEAD_LANES)), 0.0, 1.0).astype(BF16)

        @pl.when(n == 0)
        def _():
            dk_ref[...] = jnp.zeros_like(dk_ref)
            dv_ref[...] = jnp.zeros_like(dv_ref)
            _transpose_rows(eye, k_ref, kt_ref, seq, 2 * HEAD_LANES)

        @pl.when(jnp.logical_and(n == 0, b == 0))
        def _():
            dsink_ref[...] = jnp.zeros_like(dsink_ref)

        heads = range(N_HEADS)
        blocks = range(qb)
        kv_lanes = lambda h: slice((h // 4) * HEAD_LANES, (h // 4 + 1) * HEAD_LANES)
        geo = [_swa_block(n * qb + bi, pc_ref, pr_ref[bi]) for bi in blocks]
        wins = [pl.ds(g[0], 2 * w) for g in geo]
        kwins = [k_ref[win, :] for win in wins]
        vwins = [v_ref[win, :] for win in wins]

        do_ts, deltas, qms, doms = [], [], [], []
        for bi in blocks:
            rows = slice(bi * w, (bi + 1) * w)
            for lst in (do_ts, deltas, qms, doms):
                lst.append([])
            for j in range(N_HEADS // 2):
                pair = slice(j * HEAD_LANES, (j + 1) * HEAD_LANES)
                dop = do_ref[rows, pair]
                qp = q_ref[rows, pair]
                dt = _dot_nt(eye, dop)
                prod = dop.astype(F32) * o_ref[rows, pair]
                p_hi = prod.astype(BF16)
                p_lo = (prod - p_hi.astype(F32)).astype(BF16)
                for hh in range(2):
                    half, ones = (lo, ones_lo) if hh == 0 else (hi, ones_hi)
                    do_ts[bi].append(jnp.where(sub_lo, dt, 0.0).astype(BF16) if hh == 0
                                     else jnp.where(sub_lo, 0.0, dt).astype(BF16))
                    deltas[bi].append((_dot_nt(ones, p_hi) + _dot_nt(ones, p_lo))[0:1, :])
                    qms[bi].append(jnp.where(half, qp, jnp.zeros_like(qp)))
                    doms[bi].append(jnp.where(half, dop, jnp.zeros_like(dop)))
        sts = [[_dot_nt(kwins[bi][:, kv_lanes(h)], qms[bi][h]) for h in heads] for bi in blocks]
        dpts = [[_dot(vwins[bi][:, kv_lanes(h)], do_ts[bi][h]) for h in heads] for bi in blocks]
        pts, dsts = [], []
        for bi in blocks:
            pts.append([])
            dsts.append([])
            for h in heads:
                lse_h = lse_ref[0, h:h + 1, bi * w:(bi + 1) * w]
                pt = jnp.exp2(sts[bi][h] - _alibi(h) * geo[bi][1] - lse_h)
                dsts[bi].append((pt * (dpts[bi][h] - deltas[bi][h])).astype(BF16))
                pts[bi].append(pt.astype(BF16))
                dsink_ref[h:h + 1, :] += -jnp.exp2(sink_ref[0, h] * LOG2E - lse_h) * deltas[bi][h]
        for bi in blocks:
            for kv in range(2):
                group = range(4 * kv, 4 * kv + 4)
                dst_all = jnp.concatenate([dsts[bi][h] for h in group], axis=1)
                pt_all = jnp.concatenate([pts[bi][h] for h in group], axis=1)
                q_all = jnp.concatenate([qms[bi][h] for h in group], axis=0)
                do_all = jnp.concatenate([doms[bi][h] for h in group], axis=0)
                dk_ref[wins[bi], kv_lanes(4 * kv)] += _dot(dst_all, q_all)
                dv_ref[wins[bi], kv_lanes(4 * kv)] += _dot(pt_all, do_all)
        for bi in blocks:
            ktw = kt_ref[:, wins[bi]]
            for j in range(N_HEADS // 2):
                k_t = ktw[kv_lanes(2 * j), :]
                dq_t = jnp.where(sub_lo, _dot(k_t, dsts[bi][2 * j]), _dot(k_t, dsts[bi][2 * j + 1]))
                dq_ref[bi * w:(bi + 1) * w, j * HEAD_LANES:(j + 1) * HEAD_LANES] = dq_t.T * SWA_SCALE

    n_tok = qs.shape[0]
    tok = lambda width: pl.BlockSpec((qb * w, width), lambda b, n: (b * steps + n, 0))
    whole = lambda width: pl.BlockSpec((seq, width), lambda b, n: (b, 0))
    return pl.pallas_call(
        body, name="swa_bwd", grid=(n_seq, steps),
        out_shape=[jax.ShapeDtypeStruct((n_tok, 512), F32), jax.ShapeDtypeStruct((n_tok, 256), F32),
                   jax.ShapeDtypeStruct((n_tok, 256), F32), jax.ShapeDtypeStruct((N_HEADS, HEAD_LANES), F32)],
        in_specs=[tok(512), whole(256), whole(256), pl.BlockSpec((qb * w, 512), lambda b, n: (b * steps + n, 1)), tok(512),
                  pl.BlockSpec((1, N_HEADS, qb * w), lambda b, n: (b, 0, n)),
                  whole(1), pl.BlockSpec((qb, 1, w), lambda b, n: (b * steps + n, 0, 0)),
                  pl.BlockSpec(memory_space=pltpu.SMEM)],
        out_specs=[tok(512), whole(256), whole(256), _full((N_HEADS, HEAD_LANES))],
        scratch_shapes=[pltpu.VMEM((2 * HEAD_LANES, seq), BF16)],
        compiler_params=_params(2),
    )(qs, kd, vd, do, o, lse, pos_col, pos_row, sinks)


def _post_call(x, target, o_mla, o_swa, gates, mod, b_ada, fg, w_out, seq):
    n_tok = x.shape[0]
    tm = min(TOKEN_TILE, seq)
    per_seq = seq // tm
    n_seq = n_tok // seq

    def body(x_ref, t_ref, om_ref, os_ref, g_ref, mod_ref, bada_ref, fg_ref, w_ref,
             dx2_ref, do_ref, dg_ref, gw_ref, gfg_ref, dgate_ref, loss_ref):
        i = pl.program_id(0)

        @pl.when(i == 0)
        def _():
            gw_ref[...] = jnp.zeros_like(gw_ref)
            gfg_ref[...] = jnp.zeros_like(gfg_ref)
            loss_ref[...] = jnp.zeros_like(loss_ref)

        @pl.when(i % per_seq == 0)
        def _():
            dgate_ref[...] = jnp.zeros_like(dgate_ref)

        gate = mod_ref[0][:, 2 * D_MODEL:] + bada_ref[:, 2 * D_MODEL:]
        fgv = fg_ref[...]
        subs = _sub_tiles(tm)
        gs = [g_ref[r, :] for r in subs]
        os_ = [jnp.concatenate([om_ref[r, :], os_ref[r, :]], axis=-1) for r in subs]
        sgs = [_sigmoid(g) for g in gs]
        sils = [g * sg for g, sg in zip(gs, sgs)]
        ypres = [(o * sil).astype(BF16) for o, sil in zip(os_, sils)]
        ys = [_dot(ypre, w_ref[...]) for ypre in ypres]
        dys, loss, gfg, dgate = [], 0.0, 0.0, 0.0
        for r, y in zip(subs, ys):
            x2 = x_ref[r, :] + gate * y
            r2 = lax.rsqrt(jnp.mean(x2 * x2, axis=-1, keepdims=True) + EPS)
            xn2 = x2 * r2
            err = xn2 * fgv - t_ref[r, :]
            loss = loss + jnp.sum(jnp.sum(err * err, axis=-1, keepdims=True), axis=0, keepdims=True)
            dout = err * (1.0 / D_MODEL)
            gfg = gfg + jnp.sum(dout * xn2, axis=0, keepdims=True)
            dxn2 = dout * fgv
            dx2 = r2 * (dxn2 - xn2 * jnp.mean(dxn2 * xn2, axis=-1, keepdims=True))
            dx2_ref[r, :] = dx2
            dgate = dgate + jnp.sum(dx2 * y, axis=0, keepdims=True)
            dys.append((dx2 * gate).astype(BF16))
        loss_ref[...] += jnp.broadcast_to(loss * (0.5 / D_MODEL), loss_ref.shape)
        gfg_ref[...] += gfg
        dgate_ref[0] += dgate
        gw_ref[...] += _dot_tn(jnp.concatenate(ypres, axis=0), jnp.concatenate(dys, axis=0))
        dypres = [_dot_nt(dy, w_ref[...]) for dy in dys]
        for r, dypre, o, g, sg, sil in zip(subs, dypres, os_, gs, sgs, sils):
            do_ref[r, :] = (dypre * sil).astype(BF16)
            dg_ref[r, :] = (dypre * o * (sg * (1.0 + g * (1.0 - sg)))).astype(BF16)

    tok = lambda w: pl.BlockSpec((tm, w), lambda i: (i, 0))
    per_b = pl.BlockSpec((1, 1, 3 * D_MODEL), lambda i: (i // per_seq, 0, 0))
    return pl.pallas_call(
        body, name="post", grid=(n_tok // tm,),
        out_shape=[jax.ShapeDtypeStruct((n_tok, D_MODEL), F32), jax.ShapeDtypeStruct((n_tok, D_MODEL), BF16),
                   jax.ShapeDtypeStruct((n_tok, D_MODEL), BF16), jax.ShapeDtypeStruct((D_MODEL, D_MODEL), F32),
                   jax.ShapeDtypeStruct((1, D_MODEL), F32), jax.ShapeDtypeStruct((n_seq, 1, D_MODEL), F32),
                   jax.ShapeDtypeStruct((1, HEAD_LANES), F32)],
        in_specs=[tok(D_MODEL), tok(D_MODEL), tok(512), tok(512), tok(D_MODEL), per_b, _full(b_ada.shape),
                  _full(fg.shape), _full(w_out.shape)],
        out_specs=[tok(D_MODEL), tok(D_MODEL), tok(D_MODEL), _full((D_MODEL, D_MODEL)), _full((1, D_MODEL)),
                   pl.BlockSpec((1, 1, D_MODEL), lambda i: (i // per_seq, 0, 0)), _full((1, HEAD_LANES))],
        compiler_params=_params(1),
    )(x, target, o_mla, o_swa, gates, mod, b_ada, fg, w_out)


def _mid_bwd_call(dqf, dkf, dv, zqkv, rope, qg, kvg, wq2, wkv, seq):
    n_tok = dqf.shape[0]
    tm = min(TOKEN_TILE, seq)

    def body(dq_ref, dk_ref, dv_ref, z_ref, rope_ref, qg_ref, kvg_ref, wq_ref, wkv_ref,
             dz_ref, gwq_ref, gwkv_ref, gqg_ref, gkvg_ref):
        i = pl.program_id(0)

        @pl.when(i == 0)
        def _():
            gwq_ref[...] = jnp.zeros_like(gwq_ref)
            gwkv_ref[...] = jnp.zeros_like(gwkv_ref)
            gqg_ref[...] = jnp.zeros_like(gqg_ref)
            gkvg_ref[...] = jnp.zeros_like(gkvg_ref)

        cos, sin = rope_ref[:, :HEAD_LANES], rope_ref[:, HEAD_LANES:]
        cf, sf = jnp.tile(cos, (1, N_HEADS)), jnp.tile(sin, (1, N_HEADS))
        dq = dq_ref[...] * MLA_SCALE
        dqr = jnp.concatenate([dq * cf, dq * sf], axis=-1).astype(BF16)
        zq, zkv = z_ref[:, :Q_LORA], z_ref[:, Q_LORA:]
        qgv, kvgv = qg_ref[...], kvg_ref[...]

        rq = lax.rsqrt(jnp.mean(zq * zq, axis=-1, keepdims=True) + EPS)
        xq = zq * rq
        gwq_ref[...] += _dot_tn((xq * qgv).astype(BF16), dqr)
        dqn = _dot_nt(dqr, wq_ref[...])
        gqg_ref[...] += jnp.sum(dqn * xq, axis=0, keepdims=True)
        dxq = dqn * qgv
        dz_ref[:, :Q_LORA] = (rq * (dxq - xq * jnp.mean(dxq * xq, axis=-1, keepdims=True))).astype(BF16)

        dk = dk_ref[...] * LN2
        dkv = jnp.concatenate([dk, dv_ref[...]], axis=-1).astype(BF16)
        rkv = lax.rsqrt(jnp.mean(zkv * zkv, axis=-1, keepdims=True) + EPS)
        xkv = zkv * rkv
        gwkv_ref[...] += _dot_tn((xkv * kvgv).astype(BF16), dkv)
        dkvn = _dot_nt(dkv, wkv_ref[...])
        gkvg_ref[...] += jnp.sum(dkvn * xkv, axis=0, keepdims=True)
        dxkv = dkvn * kvgv
        dz_ref[:, Q_LORA:A_KR] = (rkv * (dxkv - xkv * jnp.mean(dxkv * xkv, axis=-1, keepdims=True))).astype(BF16)

        dkpe = dk[:, :HEAD_LANES]
        for h in range(1, N_HEADS):
            dkpe = dkpe + dk[:, h * HEAD_LANES:(h + 1) * HEAD_LANES]
        dz_ref[:, A_KR:] = (jnp.where(_lane_lo(), 0.0, dkpe * cos) + pltpu.roll(dkpe * sin, HALF, 1)).astype(BF16)

    tok = lambda w: pl.BlockSpec((tm, w), lambda i: (i, 0))
    return pl.pallas_call(
        body, name="mid_bwd", grid=(n_tok // tm,),
        out_shape=[jax.ShapeDtypeStruct((n_tok, A_GM), BF16),
                   jax.ShapeDtypeStruct(wq2.shape, F32), jax.ShapeDtypeStruct(wkv.shape, F32),
                   jax.ShapeDtypeStruct((1, Q_LORA), F32), jax.ShapeDtypeStruct((1, KV_LORA), F32)],
        in_specs=[tok(1024), tok(1024), tok(512), tok(640), tok(2 * HEAD_LANES), _full(qg.shape), _full(kvg.shape),
                  _full(wq2.shape), _full(wkv.shape)],
        out_specs=[tok(A_GM), _full(wq2.shape), _full(wkv.shape), _full((1, Q_LORA)), _full((1, KV_LORA))],
        compiler_params=_params(1),
    )(dqf, dkf, dv, zqkv, rope, qg, kvg, wq2, wkv)


def _in_bwd_call(x, dx2, dz, dg, dqs, dkd, dvd, mod, b_ada, ng, wa, seq):
    n_tok = x.shape[0]
    tm = min(TOKEN_TILE, seq)
    per_seq = seq // tm
    n_seq = n_tok // seq

    def body(x_ref, dx2_ref, dz_ref, dg_ref, dqs_ref, dkd_ref, dvd_ref, mod_ref, bada_ref, ng_ref,
             wa_ref, gx_ref, gwa_ref, gng_ref, dshift_ref, dscale_ref):
        i = pl.program_id(0)

        @pl.when(i == 0)
        def _():
            gwa_ref[...] = jnp.zeros_like(gwa_ref)
            gng_ref[...] = jnp.zeros_like(gng_ref)

        @pl.when(i % per_seq == 0)
        def _():
            dshift_ref[...] = jnp.zeros_like(dshift_ref)
            dscale_ref[...] = jnp.zeros_like(dscale_ref)

        xv = x_ref[...]
        modv = mod_ref[0] + bada_ref[...]
        shift, scale = modv[:, :D_MODEL], modv[:, D_MODEL:2 * D_MODEL]
        ngv = ng_ref[...]
        r1 = lax.rsqrt(jnp.mean(xv * xv, axis=-1, keepdims=True) + EPS)
        xn = xv * r1
        hb = ((xn * ngv) * (1.0 + scale) + shift).astype(BF16)

        dgv = dg_ref[...]
        pieces = [(A_ZQ, dz_ref[...]), (A_GM, dgv[:, :512]), (A_QS, dqs_ref[...].astype(BF16)),
                  (A_KS, jnp.concatenate([_once(dkd_ref[...]) * LN2, _once(dvd_ref[...])], axis=1).astype(BF16)),
                  (A_GS, dgv[:, 512:])]
        dh = None
        for off, piece in pieces:
            wd = piece.shape[1]
            gwa_ref[:, off:off + wd] += _dot_tn(hb, piece)
            term = _dot_nt(piece, wa_ref[:, off:off + wd])
            dh = term if dh is None else dh + term

        dshift_ref[0] += jnp.sum(dh, axis=0, keepdims=True)
        dscale_ref[0] += jnp.sum(dh * (xn * ngv), axis=0, keepdims=True)
        gng_ref[...] += jnp.sum(dh * xn * (1.0 + scale), axis=0, keepdims=True)
        dxn = dh * ngv * (1.0 + scale)
        gx_ref[...] = dx2_ref[...] + r1 * (dxn - xn * jnp.mean(dxn * xn, axis=-1, keepdims=True))

    tok = lambda w: pl.BlockSpec((tm, w), lambda i: (i, 0))
    per_b = lambda w: pl.BlockSpec((1, 1, w), lambda i: (i // per_seq, 0, 0))
    return pl.pallas_call(
        body, name="in_bwd", grid=(n_tok // tm,),
        out_shape=[jax.ShapeDtypeStruct((n_tok, D_MODEL), F32), jax.ShapeDtypeStruct((D_MODEL, A_END), F32),
                   jax.ShapeDtypeStruct((1, D_MODEL), F32),
                   jax.ShapeDtypeStruct((n_seq, 1, D_MODEL), F32), jax.ShapeDtypeStruct((n_seq, 1, D_MODEL), F32)],
        in_specs=[tok(D_MODEL), tok(D_MODEL), tok(A_GM), tok(D_MODEL), tok(512), tok(256), tok(256),
                  per_b(3 * D_MODEL), _full(b_ada.shape), _full(ng.shape), _full(wa.shape)],
        out_specs=[tok(D_MODEL), _full((D_MODEL, A_END)), _full((1, D_MODEL)), per_b(D_MODEL), per_b(D_MODEL)],
        compiler_params=_params(1),
    )(x, dx2, dz, dg, dqs, dkd, dvd, mod, b_ada, ng, wa)


def _adam_math(w, g, m, v):
    m_new = ADAM_B1 * m + (1.0 - ADAM_B1) * g
    v_new = ADAM_B2 * v + (1.0 - ADAM_B2) * (g * g)
    m_hat = m_new / (1.0 - ADAM_B1 ** ADAM_STEP)
    v_hat = v_new / (1.0 - ADAM_B2 ** ADAM_STEP)
    delta = -ADAM_LR * (m_hat / (jnp.sqrt(v_hat) + ADAM_EPS) + ADAM_WD * w)
    return delta, m_new, v_new


def _adam_call(name, w, g, m, v):
    rows, cols = w.shape
    tr = next((t for t in (256, 128, 88) if rows % t == 0), rows)

    def body(w_ref, g_ref, m_ref, v_ref, d_ref, mo_ref, vo_ref):
        d, mn, vn = _adam_math(w_ref[...], g_ref[...], m_ref[...], v_ref[...])
        d_ref[...] = d
        mo_ref[...] = mn
        vo_ref[...] = vn

    spec = pl.BlockSpec((tr, cols), lambda i: (i, 0))
    return pl.pallas_call(
        body, name=name, grid=(rows // tr,),
        out_shape=[jax.ShapeDtypeStruct(w.shape, F32)] * 3,
        in_specs=[spec] * 4, out_specs=[spec] * 3,
        compiler_params=_params(1),
    )(w, g, m, v)


def _ada_bwd_call(act_all, dmod_cols, w, m, v):
    rows, cols = w.shape
    tr = 256

    def body(a_ref, dm_ref, w_ref, m_ref, v_ref, g_ref, d_ref, mo_ref, vo_ref):
        g = _dot_tn(a_ref[...].astype(BF16), dm_ref[...].astype(BF16))
        d, mn, vn = _adam_math(w_ref[...], g, m_ref[...], v_ref[...])
        g_ref[...] = g
        d_ref[...] = d
        mo_ref[...] = mn
        vo_ref[...] = vn

    spec = pl.BlockSpec((tr, cols), lambda i: (i, 0))
    nb = act_all.shape[0]
    return pl.pallas_call(
        body, name="ada_bwd", grid=(rows // tr,),
        out_shape=[jax.ShapeDtypeStruct(w.shape, F32)] * 4,
        in_specs=[pl.BlockSpec((nb, tr), lambda i: (0, i)), _full(dmod_cols.shape), spec, spec, spec],
        out_specs=[spec] * 4,
        compiler_params=_params(1),
    )(act_all, dmod_cols, w, m, v)


SMALL_ROW = {"norm_gain": (0, 1024), "final_gain": (1024, 2048), "q_norm_gain": (2048, 2432),
             "kv_norm_gain": (2432, 2688), "swa_sinks": (2688, 2696), "loss": (2816, 2944)}
SMALL_ORDER = ("b_ada", "norm_gain", "q_norm_gain", "kv_norm_gain", "swa_sinks", "final_gain")


def _small_call(parts_all, n_seq, params):
    k = len(params)

    def body(p_ref, *refs):
        ins, outs, loss_ref = refs[:3 * k], refs[3 * k:7 * k], refs[7 * k]
        row = p_ref[n_seq:n_seq + 1, :]
        for dv in range(1, 8):
            r0 = dv * ROWS_PER_DEVICE + n_seq
            row = row + p_ref[r0:r0 + 1, :]
        gb = None
        for dv in range(8):
            for r in range(n_seq):
                r0 = dv * ROWS_PER_DEVICE + r
                gb = p_ref[r0:r0 + 1, :] if gb is None else gb + p_ref[r0:r0 + 1, :]
        for j, name in enumerate(SMALL_ORDER):
            g = gb if name == "b_ada" else row[:, SMALL_ROW[name][0]:SMALL_ROW[name][1]]
            d, mn, vn = _adam_math(ins[3 * j][...], g, ins[3 * j + 1][...], ins[3 * j + 2][...])
            outs[4 * j][...] = g
            outs[4 * j + 1][...] = d
            outs[4 * j + 2][...] = mn
            outs[4 * j + 3][...] = vn
        loss_ref[...] = row[:, SMALL_ROW["loss"][0]:SMALL_ROW["loss"][1]]

    flat = [t for p in params for t in p]
    res = pl.pallas_call(
        body, name="small_update", grid=(1,),
        out_shape=[jax.ShapeDtypeStruct(p[0].shape, F32) for p in params for _ in range(4)]
        + [jax.ShapeDtypeStruct((1, HEAD_LANES), F32)],
        in_specs=[_full(parts_all.shape)] + [_full(t.shape) for t in flat],
        out_specs=[_full(p[0].shape) for p in params for _ in range(4)] + [_full((1, HEAD_LANES))],
        compiler_params=_params(1),
    )(parts_all, *flat)
    return [res[4 * j:4 * j + 4] for j in range(k)], res[4 * k]


def _rot(t):
    half = t.shape[-1] // 2
    return jnp.concatenate([-t[..., half:], t[..., :half]], axis=-1)


def _rot_t(g):
    half = g.shape[-1] // 2
    return jnp.concatenate([g[..., half:], -g[..., :half]], axis=-1)


def _columns(segments, lo, hi):
    out, at = [], 0
    for seg in segments:
        n = seg.shape[1]
        a, b = max(lo, at), min(hi, at + n)
        if a < b:
            out.append(seg[:, a - at:b - at])
        at += n
    return out


def _prepare_weights(w_in_blocks, w_uq, w_ukv):
    o = [0]
    for s in IN_SPLITS:
        o.append(o[-1] + s)
    part = lambda a, b: _columns(w_in_blocks, a, b)
    kr = jnp.concatenate(part(o[2], o[3]), axis=1)
    zero = jnp.zeros((kr.shape[0], 32), kr.dtype)
    wa = jnp.concatenate(part(0, o[2]) + [_rot(kr), zero, kr, zero] + part(o[3], o[8]), axis=1)
    uq = w_uq.reshape(Q_LORA, N_HEADS, MLA_NOPE + MLA_ROPE)
    zq = jnp.zeros((Q_LORA, N_HEADS, 32), w_uq.dtype)
    uq_full = jnp.concatenate([uq, zq], axis=-1).reshape(Q_LORA, 1024)
    uq_rot = jnp.concatenate([jnp.zeros((Q_LORA, N_HEADS, 64), w_uq.dtype), _rot(uq[..., MLA_NOPE:]), zq],
                             axis=-1).reshape(Q_LORA, 1024)
    wq2 = jnp.concatenate([uq_full, uq_rot], axis=1)
    ukv = w_ukv.reshape(KV_LORA, N_HEADS, 128)
    k_full = jnp.concatenate([ukv[..., :64], jnp.zeros((KV_LORA, N_HEADS, 64), w_ukv.dtype)], axis=-1).reshape(KV_LORA, 1024)
    wkv = jnp.concatenate([k_full, ukv[..., 64:].reshape(KV_LORA, 512)], axis=1)
    return wa, wq2, wkv


def _restore_grads(gwa, gwq2, gwkv):
    gkr = gwa[:, A_KR + 64:A_KR + 96] + _rot_t(gwa[:, A_KR:A_KR + 32])
    in_order = [gwa[:, :A_KR], gkr, gwa[:, A_GM:]]
    n = D_IN // 4
    g_in = [jnp.concatenate(_columns(in_order, k * n, (k + 1) * n), axis=1) for k in range(4)]
    gf = gwq2[:, :1024].reshape(Q_LORA, N_HEADS, 128)
    gr = gwq2[:, 1024:].reshape(Q_LORA, N_HEADS, 128)
    g_uq = jnp.concatenate([gf[..., :64], gf[..., 64:96] + _rot_t(gr[..., 64:96])], axis=-1).reshape(Q_LORA, 768)
    gk = gwkv[:, :1024].reshape(KV_LORA, N_HEADS, 128)[..., :64]
    gv = gwkv[:, 1024:].reshape(KV_LORA, N_HEADS, 64)
    g_ukv = jnp.concatenate([gk, gv], axis=-1).reshape(KV_LORA, 1024)
    return g_in, g_uq, g_ukv


def _local_step(x, positions, target, mod_rows, b_ada, ng, qg, kvg, sinks, fg, w_in_b, w_uq_b, w_ukv_b, w_out_b):
    n_seq, seq, _ = x.shape
    n_tok = n_seq * seq
    x2d = x.reshape(n_tok, D_MODEL)
    t2d = target.reshape(n_tok, D_MODEL)
    pos_f = positions.astype(F32)
    pos_col = pos_f.reshape(n_tok, 1)
    pos_row = pos_f.reshape(n_tok // SWA_WINDOW, 1, SWA_WINDOW)
    mod3 = mod_rows.reshape(n_seq, 1, 3 * D_MODEL)
    inv = ROPE_THETA ** (-jnp.arange(0, MLA_ROPE, 2, dtype=F32) / MLA_ROPE)
    inv128 = jnp.concatenate([jnp.zeros((64,), F32), inv, inv, jnp.zeros((32,), F32)]).reshape(1, 128)
    fg2 = fg.reshape(1, D_MODEL)

    wa, wq2, wkv = _prepare_weights(w_in_b, w_uq_b, w_ukv_b)

    zqkv, gates, qf, kf, v, qs, kd, vd, rope = _pre_call(x2d, pos_col, mod3, b_ada, ng, qg, kvg, inv128, wa, wq2, wkv, seq)
    o_mla, lse_mla = _mla_fwd_call(qf, kf, v, n_seq, seq)
    o_swa, lse_swa = _swa_fwd_call(qs, kd, vd, pos_col, pos_row, sinks, n_seq, seq)
    dx2, do, dg, g_out, g_fg, dgate, loss = _post_call(x2d, t2d, o_mla, o_swa, gates, mod3, b_ada, fg2, w_out_b, seq)
    dqf, dkf, dv = _mla_bwd_call(qf, kf, v, do, o_mla, lse_mla, n_seq, seq)
    dqs, dkd, dvd, dsink = _swa_bwd_call(qs, kd, vd, do, o_swa, lse_swa, pos_col, pos_row, sinks, n_seq, seq)
    dz, g_wq2, g_wkv, g_qg, g_kvg = _mid_bwd_call(dqf, dkf, dv, zqkv, rope, qg, kvg, wq2, wkv, seq)
    gx, g_wa, g_ng, dshift, dscale = _in_bwd_call(x2d, dx2, dz, dg, dqs, dkd, dvd, mod3, b_ada, ng, wa, seq)
    g_in, g_uq, g_ukv = _restore_grads(g_wa, g_wq2, g_wkv)
    dmod = jnp.concatenate([dshift, dscale, dgate], axis=-1).reshape(n_seq, 3 * D_MODEL)
    small_row = jnp.concatenate([g_ng, g_fg, g_qg, g_kvg, jnp.pad(jnp.sum(dsink, axis=1).reshape(1, N_HEADS), ((0, 0), (0, 120))),
                                 loss, jnp.zeros((1, 128), F32)], axis=1)
    return gx.reshape(x.shape), (g_in, g_uq, g_ukv, g_out), small_row, dmod


def kernel(x, c, positions, w_ada, b_ada, norm_gain, w_in, q_norm_gain, kv_norm_gain, w_uq, w_ukv, swa_sinks, w_out, final_gain, loss_target, m_w_ada, m_b_ada, m_norm_gain, m_w_in, m_q_norm_gain, m_kv_norm_gain, m_w_uq, m_w_ukv, m_swa_sinks, m_w_out, m_final_gain, v_w_ada, v_b_ada, v_norm_gain, v_w_in, v_q_norm_gain, v_kv_norm_gain, v_w_uq, v_w_ukv, v_swa_sinks, v_w_out, v_final_gain):
    n_seq = x.shape[0]
    xi, yi, ci = lax.axis_index("x"), lax.axis_index("y"), lax.axis_index("c")
    dev = 4 * xi + 2 * yi + ci
    chip = 2 * xi + yi

    halves = lambda w: w.astype(BF16).reshape(2, w.shape[0] // 2, w.shape[1])
    c_blk = jnp.pad(c, ((0, ROWS_PER_DEVICE - n_seq), (0, 0)))
    act_all, pieces, f_in, f_uq, f_ukv, f_out = _comm_fwd_call(
        c_blk, w_ada[0], [halves(w_in[0]), halves(w_uq[0]), halves(w_ukv[0]), halves(w_out[0])])
    mine = lax.dynamic_slice_in_dim(pieces, dev * ROWS_PER_DEVICE, n_seq, axis=1)
    mod_rows = jnp.transpose(mine, (1, 0, 2)).reshape(n_seq, 3 * D_MODEL)
    cols = lambda t, r: jnp.transpose(t.reshape(4, r, -1), (1, 0, 2)).reshape(r, -1)
    w_in_blocks = [f_in[k].reshape(D_MODEL, -1) for k in range(4)]
    w_uq_b, w_ukv_b = cols(f_uq, Q_LORA), cols(f_ukv, KV_LORA)
    w_out_b = f_out.reshape(D_MODEL, D_MODEL)

    gx, (g_in_blocks, g_uq, g_ukv, g_out), small_row, dmod = _local_step(
        x, positions, loss_target, mod_rows, b_ada, norm_gain, q_norm_gain, kv_norm_gain, swa_sinks, final_gain,
        w_in_blocks, w_uq_b, w_ukv_b, w_out_b)

    by_owner = lambda g, n: jnp.transpose(g.reshape(g.shape[0], 4, n), (1, 0, 2)).reshape(4, 2, g.shape[0] // 2, n)
    grads = [jnp.stack(g_in_blocks).reshape(4, 2, D_MODEL // 2, -1), by_owner(g_uq, 192), by_owner(g_ukv, 256),
             g_out.reshape(4, 2, 128, D_MODEL)]
    part = jnp.concatenate([dmod, small_row, jnp.zeros((ROWS_PER_DEVICE - n_seq - 1, 3 * D_MODEL), F32)], axis=0)
    r_in, r_uq, r_ukv, r_out, parts_all = _comm_bwd_call(grads, part)
    g_in_s, g_uq_s = r_in.reshape(w_in.shape[1:]), r_uq.reshape(w_uq.shape[1:])
    g_ukv_s, g_out_s = r_ukv.reshape(w_ukv.shape[1:]), r_out.reshape(w_out.shape[1:])

    tr = lambda a: jnp.swapaxes(a[0], 0, 1)
    back = lambda ts: [jnp.swapaxes(t, 0, 1) for t in ts]
    d_in, nm_in, nv_in = back(_adam_call("adam_w_in", tr(w_in), g_in_s.T, tr(m_w_in), tr(v_w_in)))
    d_uq, nm_uq, nv_uq = back(_adam_call("adam_w_uq", tr(w_uq), g_uq_s.T, tr(m_w_uq), tr(v_w_uq)))
    d_ukv, nm_ukv, nv_ukv = _adam_call("adam_w_ukv", w_ukv[0], g_ukv_s, m_w_ukv[0], v_w_ukv[0])
    d_out, nm_out, nv_out = _adam_call("adam_w_out", w_out[0], g_out_s, m_w_out[0], v_w_out[0])
    dmod_cols = lax.dynamic_slice_in_dim(parts_all, chip * 768, 768, axis=1)
    g_ada, d_ada, nm_ada, nv_ada = _ada_bwd_call(act_all, dmod_cols, w_ada[0], m_w_ada[0], v_w_ada[0])

    row = lambda t: t.reshape(1, -1)
    small = {"b_ada": (b_ada, m_b_ada, v_b_ada), "norm_gain": (norm_gain, m_norm_gain, v_norm_gain),
             "q_norm_gain": (q_norm_gain, m_q_norm_gain, v_q_norm_gain),
             "kv_norm_gain": (kv_norm_gain, m_kv_norm_gain, v_kv_norm_gain),
             "swa_sinks": (swa_sinks, m_swa_sinks, v_swa_sinks),
             "final_gain": (row(final_gain), row(m_final_gain), row(v_final_gain))}
    res, loss_row = _small_call(parts_all, n_seq, [small[name] for name in SMALL_ORDER])
    res = dict(zip(SMALL_ORDER, res))
    res["final_gain"] = [t.reshape(-1) for t in res["final_gain"]]
    e = lambda t: t[None]
    big = {"w_ada": (e(g_ada), e(d_ada), e(nm_ada), e(nv_ada)), "w_in": (e(g_in_s), e(d_in), e(nm_in), e(nv_in)),
           "w_uq": (e(g_uq_s), e(d_uq), e(nm_uq), e(nv_uq)), "w_ukv": (e(g_ukv_s), e(d_ukv), e(nm_ukv), e(nv_ukv)),
           "w_out": (e(g_out_s), e(d_out), e(nm_out), e(nv_out))}
    order = ("w_ada", "b_ada", "norm_gain", "w_in", "q_norm_gain", "kv_norm_gain", "w_uq", "w_ukv", "swa_sinks", "w_out",
             "final_gain")
    pick = lambda kind: [(big[n] if n in big else res[n])[kind] for n in order]
    return (loss_row[0, 0], gx, *pick(0), *pick(1), *pick(2), *pick(3))
```

```python
import functools

import jax
import jax.numpy as jnp
from jax import lax
from jax.experimental import pallas as pl
from jax.experimental.pallas import tpu as pltpu

F32 = jnp.float32
BF16 = jnp.bfloat16

D_MODEL = 1024
Q_LORA = 384
KV_LORA = 256
N_HEADS = 8
MLA_NOPE = 64
MLA_ROPE = 32
HEAD_LANES = 128
HALF = 64
SWA_WINDOW = 128
EPS = 1e-6
ROPE_THETA = 10000.0
MLA_SCALE = (MLA_NOPE + MLA_ROPE) ** -0.5
LOG2E = 1.4426950408889634
LN2 = 0.6931471805599453
SWA_SCALE = 64 ** -0.5
NEG = -1e30

ADAM_LR = 0.001
ADAM_B1 = 0.9
ADAM_B2 = 0.999
ADAM_EPS = 1e-08
ADAM_WD = 0.01
ADAM_STEP = 10

A_ZQ, A_ZKV, A_KR, A_GM, A_QS, A_KS, A_VS, A_GS, A_END = 0, 384, 640, 768, 1280, 1792, 1920, 2048, 2560
IN_SPLITS = (384, 256, 32, 512, 512, 128, 128, 512)
D_IN = sum(IN_SPLITS)

TOKEN_TILE = 512
ATT_TILE = 256
VMEM_LIMIT = 56 * 1024 * 1024


def _dot(a, b):
    return jnp.dot(a, b, preferred_element_type=F32)


def _dot_nt(a, b):
    return lax.dot_general(a, b, (((1,), (1,)), ((), ())), preferred_element_type=F32)


def _dot_tn(a, b):
    return lax.dot_general(a, b, (((0,), (0,)), ((), ())), preferred_element_type=F32)


def _params(n_grid):
    return pltpu.CompilerParams(dimension_semantics=("arbitrary",) * n_grid, vmem_limit_bytes=VMEM_LIMIT)


def _full(shape):
    nd = len(shape)
    return pl.BlockSpec(shape, lambda *_: (0,) * nd, pipeline_mode=pl.Buffered(1))


def _sigmoid(g):
    return 1.0 / (1.0 + jnp.exp(-g))


SUB_TILE = 256


def _sub_tiles(tm):
    sub = min(SUB_TILE, tm)
    return [slice(s * sub, (s + 1) * sub) for s in range(tm // sub)]


MESH = pl.DeviceIdType.MESH
ROWS_PER_DEVICE = 8
VMEM_SPEC = pl.BlockSpec(memory_space=pltpu.VMEM)
ANY_SPEC = pl.BlockSpec(memory_space=pl.ANY)


def _position():
    x, y, c = lax.axis_index("x"), lax.axis_index("y"), lax.axis_index("c")
    sibling = (x, y, 1 - c)
    others = [(1 - x, y, c), (x, 1 - y, c), (1 - x, 1 - y, c)]
    return (x, y, c), 4 * x + 2 * y + c, 2 * x + y, sibling, others


def _rows_of(dev):
    return pl.ds(pl.multiple_of(dev * ROWS_PER_DEVICE, ROWS_PER_DEVICE), ROWS_PER_DEVICE)


def _all_to_all_rows(block_ref, table_ref, dev, me, send_sems, recv_sems):
    x, y, c = me
    waits = []
    for k in range(1, 8):
        peer = (1 - x if k & 4 else x, 1 - y if k & 2 else y, 1 - c if k & 1 else c)
        pltpu.make_async_remote_copy(src_ref=block_ref, dst_ref=table_ref.at[_rows_of(dev)], send_sem=send_sems.at[k - 1],
                                     recv_sem=recv_sems.at[k - 1], device_id=peer, device_id_type=MESH).start()
        waits.append(pltpu.make_async_remote_copy(
            src_ref=block_ref, dst_ref=table_ref.at[_rows_of(jnp.bitwise_xor(dev, k))], send_sem=send_sems.at[k - 1],
            recv_sem=recv_sems.at[k - 1], device_id=peer, device_id_type=MESH))
    return waits


def _comm_fwd_call(c_blk, w_ada, shards):
    n = len(shards)

    def body(c_ref, wada_ref, *refs):
        w_refs, act_ref, pieces_ref, full_refs = refs[:n], refs[n], refs[n + 1], refs[n + 2:2 * n + 2]
        c_all_ref = refs[2 * n + 2]
        c_send, c_recv, p_send, p_recv, w_send, w_recv, f_send, f_recv, loc_sem = refs[2 * n + 3:]
        me, dev, chip, sibling, others = _position()
        core = me[2]
        chip_of = [2 * p[0] + p[1] for p in others]

        local = [pltpu.make_async_copy(w_refs[i], full_refs[i].at[chip], loc_sem.at[i]) for i in range(n)]
        for cp in local:
            cp.start()

        def over_ici(i, j, src_chip):
            return pltpu.make_async_remote_copy(
                src_ref=w_refs[i].at[core], dst_ref=full_refs[i].at[src_chip, core], send_sem=w_send.at[3 * i + j],
                recv_sem=w_recv.at[3 * i + j], device_id=others[j], device_id_type=MESH)

        def to_sibling(i, j, half):
            return pltpu.make_async_remote_copy(
                src_ref=full_refs[i].at[chip_of[j], half], dst_ref=full_refs[i].at[chip_of[j], half],
                send_sem=f_send.at[3 * i + j], recv_sem=f_recv.at[3 * i + j], device_id=sibling, device_id_type=MESH)

        sent = [over_ici(i, j, chip) for i in range(n) for j in range(3)]
        for cp in sent:
            cp.start()

        c_all_ref[_rows_of(dev), :] = c_ref[...]
        c_waits = _all_to_all_rows(c_ref, c_all_ref, dev, me, c_send, c_recv)
        for cp in c_waits:
            cp.wait()
        cv = c_all_ref[...]
        act = cv * _sigmoid(cv)
        act_ref[...] = act
        pieces_ref[chip] = _dot(act.astype(BF16), wada_ref[...].astype(BF16))
        piece = lambda j, src_chip: pltpu.make_async_remote_copy(
            src_ref=pieces_ref.at[chip], dst_ref=pieces_ref.at[src_chip], send_sem=p_send.at[j], recv_sem=p_recv.at[j],
            device_id=others[j], device_id_type=MESH)
        for j in range(3):
            piece(j, chip).start()
        for j in range(3):
            piece(j, chip).wait_send()
            piece(j, chip_of[j]).wait_recv()

        for i in range(n):
            for j in range(3):
                over_ici(i, j, chip_of[j]).wait_recv()
                to_sibling(i, j, core).start()
        for i in range(n):
            for j in range(3):
                to_sibling(i, j, 1 - core).wait_recv()
                to_sibling(i, j, core).wait_send()
        for cp in sent:
            cp.wait_send()
        for cp in local:
            cp.wait()

    rows = 8 * ROWS_PER_DEVICE
    dma = pltpu.SemaphoreType.DMA
    return pl.pallas_call(
        body, name="comm_fwd",
        out_shape=[jax.ShapeDtypeStruct((rows, D_MODEL), F32), jax.ShapeDtypeStruct((4, rows, w_ada.shape[1]), F32)]
        + [jax.ShapeDtypeStruct((4,) + s.shape, s.dtype) for s in shards],
        in_specs=[VMEM_SPEC, VMEM_SPEC] + [ANY_SPEC] * n,
        out_specs=[VMEM_SPEC, VMEM_SPEC] + [ANY_SPEC] * n,
        scratch_shapes=[pltpu.VMEM((rows, D_MODEL), F32), dma((7,)), dma((7,)), dma((3,)), dma((3,)),
                        dma((3 * n,)), dma((3 * n,)), dma((3 * n,)), dma((3 * n,)), dma((n,))],
        compiler_params=pltpu.CompilerParams(vmem_limit_bytes=VMEM_LIMIT),
    )(c_blk, w_ada, *shards)


def _comm_bwd_call(grads, part):
    n = len(grads)

    def body(part_ref, *refs):
        g_refs, f_refs, parts_ref = refs[:n], refs[n:2 * n], refs[2 * n]
        scratch = refs[2 * n + 1:]
        a_refs, b_refs, p_refs, r_refs = (scratch[k * n:(k + 1) * n] for k in range(4))
        s_send, s_recv, d_send, d_recv, e_send, e_recv, h_send, h_recv, loc_sem = scratch[4 * n:]
        me, dev, chip, sibling, others = _position()
        core = me[2]
        chip_of = [2 * p[0] + p[1] for p in others]

        parts_ref[_rows_of(dev), :] = part_ref[...]
        s_waits = _all_to_all_rows(part_ref, parts_ref, dev, me, s_send, s_recv)

        mine = [pltpu.make_async_copy(g_refs[i].at[:, core], a_refs[i], loc_sem.at[i]) for i in range(n)]
        swap = [pltpu.make_async_remote_copy(src_ref=g_refs[i].at[:, 1 - core], dst_ref=b_refs[i], send_sem=d_send.at[i],
                                             recv_sem=d_recv.at[i], device_id=sibling, device_id_type=MESH) for i in range(n)]
        order = sorted(range(n), key=lambda i: g_refs[i].shape[2] * g_refs[i].shape[3])
        for i in order:
            mine[i].start()
            swap[i].start()
        cross = [pltpu.make_async_remote_copy(src_ref=p_refs[i].at[chip_of[j]], dst_ref=r_refs[i].at[j],
                                              send_sem=e_send.at[3 * i + j], recv_sem=e_recv.at[3 * i + j],
                                              device_id=others[j], device_id_type=MESH) for i in range(n) for j in range(3)]
        for i in order:
            mine[i].wait()
            swap[i].wait()
            for k in range(4):
                s = a_refs[i][k] + b_refs[i][k]
                a_refs[i][k] = s
                p_refs[i][k] = s.astype(BF16)
            for j in range(3):
                cross[3 * i + j].start()
        share = {}
        for i in order:
            for j in range(3):
                cross[3 * i + j].wait()
            f_refs[i][core] = (a_refs[i][chip] + r_refs[i][0].astype(F32) + r_refs[i][1].astype(F32)
                               + r_refs[i][2].astype(F32))
            share[i] = pltpu.make_async_remote_copy(src_ref=f_refs[i].at[core], dst_ref=f_refs[i].at[core],
                                                    send_sem=h_send.at[i], recv_sem=h_recv.at[i], device_id=sibling,
                                                    device_id_type=MESH)
            share[i].start()
        for i in range(n):
            share[i].wait_send()
            pltpu.make_async_remote_copy(src_ref=f_refs[i].at[core], dst_ref=f_refs[i].at[1 - core], send_sem=h_send.at[i],
                                         recv_sem=h_recv.at[i], device_id=sibling, device_id_type=MESH).wait_recv()
        for cp in s_waits:
            cp.wait()

    rows = 8 * ROWS_PER_DEVICE
    dma = pltpu.SemaphoreType.DMA
    quarter = [(4,) + g.shape[2:] for g in grads]
    return pl.pallas_call(
        body, name="comm_bwd",
        out_shape=[jax.ShapeDtypeStruct((2,) + g.shape[2:], F32) for g in grads]
        + [jax.ShapeDtypeStruct((rows, part.shape[1]), F32)],
        in_specs=[VMEM_SPEC] + [ANY_SPEC] * n,
        out_specs=[VMEM_SPEC] * (n + 1),
        scratch_shapes=[pltpu.VMEM(q, F32) for q in quarter] + [pltpu.VMEM(q, F32) for q in quarter]
        + [pltpu.VMEM(q, BF16) for q in quarter] + [pltpu.VMEM((3,) + q[1:], BF16) for q in quarter]
        + [dma((7,)), dma((7,)), dma((n,)), dma((n,)), dma((3 * n,)), dma((3 * n,)), dma((n,)), dma((n,)), dma((n,))],
        compiler_params=pltpu.CompilerParams(vmem_limit_bytes=VMEM_LIMIT),
    )(part, *grads)


def _twice(t):
    lo = _lane_lo()
    other = pltpu.roll(t, HALF, 1)
    return jnp.concatenate([jnp.where(lo, t, other), jnp.where(lo, other, t)], axis=1)


def _once(g):
    first, second = g[:, :HEAD_LANES], g[:, HEAD_LANES:]
    return jnp.where(_lane_lo(), first + pltpu.roll(first, HALF, 1), second + pltpu.roll(second, HALF, 1))


def _rope_tables(pos_col, inv_row):
    ang = pos_col * inv_row
    return jnp.cos(ang), jnp.sin(ang)


def _pre_call(x, pos_col, mod, b_ada, ng, qg, kvg, inv128, wa, wq2, wkv, seq):
    n_tok = x.shape[0]
    tm = min(TOKEN_TILE, seq)
    per_seq = seq // tm

    def body(x_ref, pos_ref, mod_ref, bada_ref, ng_ref, qg_ref, kvg_ref, inv_ref, wa_ref, wq_ref, wkv_ref,
             zqkv_ref, gates_ref, qf_ref, kf_ref, v_ref, qs_ref, kd_ref, vd_ref, rope_ref):
        xv = x_ref[...]
        modv = mod_ref[0] + bada_ref[...]
        shift, scale = modv[:, :D_MODEL], modv[:, D_MODEL:2 * D_MODEL]
        r1 = lax.rsqrt(jnp.mean(xv * xv, axis=-1, keepdims=True) + EPS)
        h = ((xv * r1) * ng_ref[...]) * (1.0 + scale) + shift
        hb = h.astype(BF16)
        za = _dot(hb, wa_ref[...])
        zkr = za[:, A_KR:A_GM]
        cos, sin = _rope_tables(pos_ref[...], inv_ref[...])
        rope_ref[:, :HEAD_LANES] = cos
        rope_ref[:, HEAD_LANES:] = sin
        zqkv_ref[...] = za[:, :A_KR]
        gates_ref[:, :512] = za[:, A_GM:A_QS]
        gates_ref[:, 512:] = za[:, A_GS:A_END]
        qs_ref[...] = (za[:, A_QS:A_KS] * (SWA_SCALE * LOG2E)).astype(BF16)
        kd_ref[...] = _twice(za[:, A_KS:A_VS]).astype(BF16)
        vd_ref[...] = _twice(za[:, A_VS:A_GS]).astype(BF16)
        zq, zkv = za[:, A_ZQ:A_ZKV], za[:, A_ZKV:A_KR]
        rq = lax.rsqrt(jnp.mean(zq * zq, axis=-1, keepdims=True) + EPS)
        qn = ((zq * rq) * qg_ref[...]).astype(BF16)
        qr = _dot(qn, wq_ref[...])
        cf, sf = jnp.tile(cos, (1, N_HEADS)), jnp.tile(sin, (1, N_HEADS))
        qf_ref[...] = ((qr[:, :1024] * cf + qr[:, 1024:] * sf) * (MLA_SCALE * LOG2E)).astype(BF16)
        rkv = lax.rsqrt(jnp.mean(zkv * zkv, axis=-1, keepdims=True) + EPS)
        kvn = ((zkv * rkv) * kvg_ref[...]).astype(BF16)
        kv = _dot(kvn, wkv_ref[...])
        kpe = jnp.where(_lane_lo(), 0.0, zkr * cos) + pltpu.roll(zkr, HALF, 1) * sin
        kf_ref[...] = (kv[:, :1024] + jnp.tile(kpe, (1, N_HEADS))).astype(BF16)
        v_ref[...] = kv[:, 1024:].astype(BF16)

    tok = lambda w: pl.BlockSpec((tm, w), lambda i: (i, 0))
    outs = [(640, F32), (1024, F32), (1024, BF16), (1024, BF16), (512, BF16), (512, BF16), (256, BF16), (256, BF16),
            (2 * HEAD_LANES, F32)]
    return pl.pallas_call(
        body, name="pre", grid=(n_tok // tm,),
        out_shape=[jax.ShapeDtypeStruct((n_tok, w), dt) for w, dt in outs],
        in_specs=[tok(D_MODEL), tok(1), pl.BlockSpec((1, 1, 3 * D_MODEL), lambda i: (i // per_seq, 0, 0)),
                  _full(b_ada.shape), _full(ng.shape), _full(qg.shape), _full(kvg.shape), _full(inv128.shape),
                  _full(wa.shape), _full(wq2.shape), _full(wkv.shape)],
        out_specs=[tok(w) for w, _ in outs],
        compiler_params=_params(1),
    )(x, pos_col, mod, b_ada, ng, qg, kvg, inv128, wa, wq2, wkv)


def _lane_lo(width=HEAD_LANES):
    return lax.broadcasted_iota(jnp.int32, (1, width), 1) < HALF


def _eye(n=HEAD_LANES):
    r = lax.broadcasted_iota(jnp.int32, (n, n), 0)
    c = lax.broadcasted_iota(jnp.int32, (n, n), 1)
    return jnp.where(r == c, 1.0, 0.0).astype(BF16)


def _mla_fwd_call(qf, kf, v, n_seq, seq):
    tq = min(ATT_TILE, seq)
    nq = seq // tq

    ext = HALF + 16

    def body(q_ref, k_ref, v_ref, o_ref, lse_ref, vt_ref):
        i = pl.program_id(1)
        eye = _eye()

        @pl.when(i == 0)
        def _():
            for h in range(N_HEADS):
                vt_ref[h * ext + HALF:(h + 1) * ext, :] = jnp.ones((16, seq), BF16)
            for t in range(nq):
                for p in range(N_HEADS // 2):
                    pair = slice(p * HEAD_LANES, (p + 1) * HEAD_LANES)
                    v_t = _dot_nt(eye, v_ref[t * tq:(t + 1) * tq, pair]).astype(BF16)
                    for hh in range(2):
                        r0 = (2 * p + hh) * ext
                        vt_ref[r0:r0 + HALF, t * tq:(t + 1) * tq] = v_t[hh * HALF:(hh + 1) * HALF, :]

        q = q_ref[...]
        qcol = i * tq + lax.broadcasted_iota(jnp.int32, (1, tq), 1)
        heads = range(N_HEADS)
        lanes = [slice(h * HEAD_LANES, (h + 1) * HEAD_LANES) for h in heads]

        def make_step(masked, n_tiles):
            def step(kt0, carry):
                tiles = range(n_tiles)
                start = pl.multiple_of(kt0 * tq, tq)
                ks = [k_ref[pl.ds(pl.multiple_of((kt0 + t) * tq, tq), tq), :] for t in tiles]
                vt = vt_ref[:, pl.ds(start, n_tiles * tq)]
                last = n_tiles - 1
                if masked:
                    keep = ((kt0 + last) * tq + lax.broadcasted_iota(jnp.int32, (tq, 1), 0)) <= qcol

                def scores(h):
                    sts = [_dot_nt(ks[t][:, lanes[h]], q[:, lanes[h]]) for t in tiles]
                    if masked:
                        sts[last] = jnp.where(keep, sts[last], NEG)
                    return sts

                def softmax(h, sts):
                    m_old = carry[2 * h]
                    m_new = m_old
                    for st in sts:
                        m_new = jnp.maximum(m_new, jnp.max(st, axis=0, keepdims=True))
                    pt = jnp.concatenate([jnp.exp2(st - m_new).astype(BF16) for st in sts], axis=0)
                    return m_new, jnp.exp2(m_old - m_new), pt

                def values(h, alpha, pt):
                    return carry[2 * h + 1] * alpha + _dot(vt[h * ext:(h + 1) * ext, :], pt)

                sts, soft, out = {0: scores(0), 1: scores(1)}, {}, {}
                for h in range(N_HEADS + 1):
                    if h + 2 < N_HEADS:
                        sts[h + 2] = scores(h + 2)
                    if h < N_HEADS:
                        soft[h] = softmax(h, sts.pop(h))
                    if h >= 1:
                        m_new, alpha, pt = soft.pop(h - 1)
                        out[h - 1] = (m_new, values(h - 1, alpha, pt))
                return tuple(v for h in heads for v in out[h])
            return step

        init = (jnp.full((1, tq), NEG, F32), jnp.zeros((ext, tq), F32)) * N_HEADS
        count = i + 1
        carry = lax.fori_loop(0, (count + 1) // 2 - 1, lambda j, c: make_step(False, 2)(2 * j, c), init)
        carry = lax.cond(count % 2 == 0, lambda c: make_step(True, 2)(i - 1, c), lambda c: make_step(True, 1)(i, c), carry)
        dens = [carry[2 * h + 1][HALF:HALF + 1, :] for h in heads]
        acc_t = jnp.concatenate([carry[2 * h + 1][:HALF, :] * (1.0 / dens[h]) for h in heads], axis=0)
        o_ref[...] = acc_t.T
        for h in heads:
            lse_ref[0, h // 4, h % 4:h % 4 + 1, :] = carry[2 * h] + jnp.log2(dens[h])

    n_tok = qf.shape[0]
    return pl.pallas_call(
        body, name="mla_fwd", grid=(n_seq, nq),
        out_shape=[jax.ShapeDtypeStruct((n_tok, 512), F32), jax.ShapeDtypeStruct((n_seq, 2, 4, seq), F32)],
        in_specs=[pl.BlockSpec((tq, 1024), lambda b, i: (b * nq + i, 0)),
                  pl.BlockSpec((seq, 1024), lambda b, i: (b, 0)),
                  pl.BlockSpec((seq, 512), lambda b, i: (b, 0))],
        out_specs=[pl.BlockSpec((tq, 512), lambda b, i: (b * nq + i, 0)),
                   pl.BlockSpec((1, 2, 4, tq), lambda b, i: (b, 0, 0, i))],
        scratch_shapes=[pltpu.VMEM((N_HEADS * ext, seq), BF16)],
        compiler_params=_params(2),
    )(qf, kf, v)


def _mla_bwd_call(qf, kf, v, do, o, lse, n_seq, seq):
    tq = min(ATT_TILE, seq)
    nq = seq // tq

    nh = 4
    heads = range(nh)
    lanes = [slice(h * HEAD_LANES, (h + 1) * HEAD_LANES) for h in heads]

    def body(q_ref, k_ref, v_ref, do_ref, o_ref, lse_ref, dq_ref, dk_ref, dv_ref,
             kt_ref, dot_ref, delta_ref, dqt_ref):
        eye = _eye()
        lo = _lane_lo()
        sub_lo = lax.broadcasted_iota(jnp.int32, (HEAD_LANES, 1), 0) < HALF
        ones_lo = jnp.where(jnp.broadcast_to(lo, (8, HEAD_LANES)), 1.0, 0.0).astype(BF16)
        ones_hi = jnp.where(jnp.broadcast_to(lo, (8, HEAD_LANES)), 0.0, 1.0).astype(BF16)

        for t in range(nq):
            r = slice(t * tq, (t + 1) * tq)
            kv = k_ref[r, :]
            for h in heads:
                kt_ref[lanes[h], r] = _dot_nt(eye, kv[:, lanes[h]]).astype(BF16)
            for p in range(nh // 2):
                dov = do_ref[r, lanes[p]]
                dt = _dot_nt(eye, dov)
                dot_ref[2 * p, :, r] = jnp.where(sub_lo, dt, 0.0).astype(BF16)
                dot_ref[2 * p + 1, :, r] = jnp.where(sub_lo, 0.0, dt).astype(BF16)
                prod = dov.astype(F32) * o_ref[r, lanes[p]]
                p_hi = prod.astype(BF16)
                p_lo = (prod - p_hi.astype(F32)).astype(BF16)
                delta_ref[2 * p, :, r] = _dot_nt(ones_lo, p_hi) + _dot_nt(ones_lo, p_lo)
                delta_ref[2 * p + 1, :, r] = _dot_nt(ones_hi, p_hi) + _dot_nt(ones_hi, p_lo)
        dqt_ref[...] = jnp.zeros_like(dqt_ref)

        def k_step(kt, _):
            kr = pl.ds(pl.multiple_of(kt * tq, tq), tq)
            k = k_ref[kr, :]
            vv = v_ref[kr, :]
            k_t = kt_ref[:, kr]
            krow = kt * tq + lax.broadcasted_iota(jnp.int32, (tq, 1), 0)

            def make_step(masked, n_tiles):
                def q_step(qt0, carry):
                    tiles = range(n_tiles)
                    qrs = [pl.ds(pl.multiple_of((qt0 + t) * tq, tq), tq) for t in tiles]
                    qs = [q_ref[qr, :] for qr in qrs]
                    if masked:
                        keep = krow <= (qt0 * tq + lax.broadcasted_iota(jnp.int32, (1, tq), 1))

                    def scores(h):
                        do_ts = [dot_ref[h, :, qr] for qr in qrs]
                        sts = [_dot_nt(k[:, lanes[h]], qs[t][:, lanes[h]]) for t in tiles]
                        dpts = [_dot(vv[:, lanes[h // 2]], do_ts[t]) for t in tiles]
                        return do_ts, sts, dpts

                    def softmax(h, sts, dpts):
                        pts, dsts = [], []
                        for t in tiles:
                            pt = jnp.exp2(sts[t] - lse_ref[0, 0, h:h + 1, qrs[t]])
                            if masked and t == 0:
                                pt = jnp.where(keep, pt, 0.0)
                            dsts.append((pt * (dpts[t] - delta_ref[h, 0:1, qrs[t]])).astype(BF16))
                            pts.append(pt.astype(BF16))
                        return pts, dsts

                    def grads(h, do_ts, pts, dsts):
                        half = slice((h % 2) * HALF, (h % 2 + 1) * HALF)
                        dst_all = jnp.concatenate(dsts, axis=1)
                        pt_all = jnp.concatenate(pts, axis=1)
                        do_all = jnp.concatenate([do_ts[t][half, :] for t in tiles], axis=1)
                        q_all = jnp.concatenate([qs[t][:, lanes[h]] for t in tiles], axis=0)
                        dvt = _dot_nt(do_all, pt_all)
                        dk = _dot(dst_all, q_all)
                        for t in tiles:
                            dqt_ref[lanes[h], qrs[t]] += _dot(k_t[lanes[h], :], dsts[t])
                        return carry[2 * h] + dk, carry[2 * h + 1] + dvt

                    first, second, out = {0: scores(0)}, {}, {}
                    for h in range(nh + 1):
                        if h + 1 < nh:
                            first[h + 1] = scores(h + 1)
                        if h < nh:
                            do_ts, sts, dpts = first.pop(h)
                            second[h] = (do_ts,) + softmax(h, sts, dpts)
                        if h >= 1:
                            out[h - 1] = grads(h - 1, *second.pop(h - 1))
                    return tuple(v for h in heads for v in out[h])
                return q_step

            init = (jnp.zeros((tq, HEAD_LANES), F32), jnp.zeros((HALF, tq), F32)) * nh
            count = nq - kt
            carry = lax.cond(count >= 2, lambda c: make_step(True, 2)(kt, c), lambda c: make_step(True, 1)(kt, c), init)
            carry = lax.fori_loop(1, count // 2, lambda j, c: make_step(False, 2)(kt + 2 * j, c), carry)
            carry = lax.cond(jnp.logical_and(count % 2 == 1, count >= 3),
                             lambda c: make_step(False, 1)(nq - 1, c), lambda c: c, carry)
            for h in heads:
                dk_ref[kr, lanes[h]] = carry[2 * h]
            for p in range(nh // 2):
                dv_ref[kr, lanes[p]] = jnp.concatenate([carry[4 * p + 1], carry[4 * p + 3]], axis=0).T
            return 0

        lax.fori_loop(0, nq, k_step, 0)
        for t in range(nq):
            r = slice(t * tq, (t + 1) * tq)
            for h in heads:
                dq_ref[r, lanes[h]] = dqt_ref[lanes[h], r].T

    n_tok = qf.shape[0]
    groups = N_HEADS // nh
    blk = lambda w: pl.BlockSpec((seq, w), lambda b, g: (b, g))
    return pl.pallas_call(
        body, name="mla_bwd", grid=(n_seq, groups),
        out_shape=[jax.ShapeDtypeStruct((n_tok, 1024), F32), jax.ShapeDtypeStruct((n_tok, 1024), F32),
                   jax.ShapeDtypeStruct((n_tok, 512), F32)],
        in_specs=[blk(512), blk(512), blk(256), blk(256), blk(256),
                  pl.BlockSpec((1, 1, nh, seq), lambda b, g: (b, g, 0, 0))],
        out_specs=[blk(512), blk(512), blk(256)],
        scratch_shapes=[pltpu.VMEM((nh * HEAD_LANES, seq), BF16), pltpu.VMEM((nh, HEAD_LANES, seq), BF16),
                        pltpu.VMEM((nh, 8, seq), F32), pltpu.VMEM((nh * HEAD_LANES, seq), F32)],
        compiler_params=_params(2),
    )(qf, kf, v, do, o, lse)


SWA_BLOCKS = 4


def _swa_block(n, pos_col_ref, posq):
    w = SWA_WINDOW
    start = pl.multiple_of(jnp.maximum(n - 1, 0) * w, w)
    posk = pos_col_ref[pl.ds(start, 2 * w), :]
    rel = (n * w + lax.broadcasted_iota(jnp.int32, (1, w), 1)) - (start + lax.broadcasted_iota(jnp.int32, (2 * w, 1), 0))
    valid = jnp.logical_and(rel >= 0, rel < w)
    return start, jnp.where(valid, posq - posk, 1e30)


def _alibi(h):
    return LOG2E * 2.0 ** -(h + 1)


def _transpose_rows(eye, src_ref, dst_ref, seq, width):
    step = 2 * SWA_WINDOW
    for t in range(seq // step):
        for p in range(width // HEAD_LANES):
            lanes = slice(p * HEAD_LANES, (p + 1) * HEAD_LANES)
            dst_ref[lanes, t * step:(t + 1) * step] = _dot_nt(eye, src_ref[t * step:(t + 1) * step, lanes]).astype(BF16)


def _swa_fwd_call(qs, kd, vd, pos_col, pos_row, sinks, n_seq, seq):
    w = SWA_WINDOW
    qb = SWA_BLOCKS
    steps = seq // (qb * w)
    ext = HALF + 16

    def body(q_ref, k_ref, v_ref, pc_ref, pr_ref, sink_ref, o_ref, lse_ref, vt_ref):
        n = pl.program_id(1)
        lo = _lane_lo()
        hi = jnp.logical_not(lo)
        eye = _eye()

        @pl.when(n == 0)
        def _():
            step = 2 * w
            for kv in range(2):
                vt_ref[kv * ext + HALF:(kv + 1) * ext, :] = jnp.ones((16, seq), BF16)
                for t in range(seq // step):
                    v_t = _dot_nt(eye, v_ref[t * step:(t + 1) * step, kv * HEAD_LANES:(kv + 1) * HEAD_LANES])
                    vt_ref[kv * ext:kv * ext + HALF, t * step:(t + 1) * step] = v_t[:HALF, :].astype(BF16)

        heads = range(N_HEADS)
        blocks = range(qb)
        geo = [_swa_block(n * qb + bi, pc_ref, pr_ref[bi]) for bi in blocks]
        wins = [pl.ds(g[0], 2 * w) for g in geo]
        kwins = [k_ref[win, :] for win in wins]
        vts = [vt_ref[:, win] for win in wins]
        sts = []
        for bi in blocks:
            q = q_ref[bi * w:(bi + 1) * w, :]
            sts.append([])
            for j in range(N_HEADS // 2):
                qp = q[:, j * HEAD_LANES:(j + 1) * HEAD_LANES]
                both = jnp.concatenate([jnp.where(lo, qp, jnp.zeros_like(qp)), jnp.where(hi, qp, jnp.zeros_like(qp))], axis=0)
                st = _dot_nt(kwins[bi][:, (j // 2) * HEAD_LANES:(j // 2 + 1) * HEAD_LANES], both)
                sts[bi] += [st[:, :w], st[:, w:]]
        ps, ms = [], []
        for bi in blocks:
            ps.append([])
            ms.append([])
            for h in heads:
                s = sts[bi][h] - _alibi(h) * geo[bi][1]
                m = jnp.maximum(jnp.max(s, axis=0, keepdims=True), sink_ref[0, h] * LOG2E)
                ps[bi].append(jnp.exp2(s - m).astype(BF16))
                ms[bi].append(m)
        for bi in blocks:
            ots = []
            for h in heads:
                pv = _dot(vts[bi][(h // 4) * ext:(h // 4 + 1) * ext, :], ps[bi][h])
                l = pv[HALF:HALF + 1, :] + jnp.exp2(sink_ref[0, h] * LOG2E - ms[bi][h])
                ots.append(pv[:HALF, :] * (1.0 / l))
                lse_ref[0, h:h + 1, bi * w:(bi + 1) * w] = ms[bi][h] + jnp.log2(l)
            o_ref[bi * w:(bi + 1) * w, :] = jnp.concatenate(ots, axis=0).T

    n_tok = qs.shape[0]
    tok = lambda width: pl.BlockSpec((qb * w, width), lambda b, n: (b * steps + n, 0))
    whole = lambda width: pl.BlockSpec((seq, width), lambda b, n: (b, 0))
    return pl.pallas_call(
        body, name="swa_fwd", grid=(n_seq, steps),
        out_shape=[jax.ShapeDtypeStruct((n_tok, 512), F32), jax.ShapeDtypeStruct((n_seq, N_HEADS, seq), F32)],
        in_specs=[tok(512), whole(256), whole(256), whole(1), pl.BlockSpec((qb, 1, w), lambda b, n: (b * steps + n, 0, 0)),
                  pl.BlockSpec(memory_space=pltpu.SMEM)],
        out_specs=[tok(512), pl.BlockSpec((1, N_HEADS, qb * w), lambda b, n: (b, 0, n))],
        scratch_shapes=[pltpu.VMEM((2 * ext, seq), BF16)],
        compiler_params=_params(2),
    )(qs, kd, vd, pos_col, pos_row, sinks)


def _swa_bwd_call(qs, kd, vd, do, o, lse, pos_col, pos_row, sinks, n_seq, seq):
    w = SWA_WINDOW
    qb = SWA_BLOCKS
    steps = seq // (qb * w)

    def body(q_ref, k_ref, v_ref, do_ref, o_ref, lse_ref, pc_ref, pr_ref, sink_ref, dq_ref, dk_ref, dv_ref, dsink_ref,
             kt_ref):
        b, n = pl.program_id(0), pl.program_id(1)
        lo = _lane_lo()
        hi = jnp.logical_not(lo)
        sub_lo = lax.broadcasted_iota(jnp.int32, (HEAD_LANES, 1), 0) < HALF
        eye = _eye()
        ones_lo = jnp.where(jnp.broadcast_to(lo, (8, HEAD_LANES)), 1.0, 0.0).astype(BF16)
        ones_hi = jnp.where(jnp.broadcast_to(lo, (8, HEAD_LANES)), 0.0, 1.0).astype(BF16)

        @pl.when(n == 0)
        def _():
            dk_ref[...] = jnp.zeros_like(dk_ref)
            dv_ref[...] = jnp.zeros_like(dv_ref)
            _transpose_rows(eye, k_ref, kt_ref, seq, 2 * HEAD_LANES)

        @pl.when(jnp.logical_and(n == 0, b == 0))
        def _():
            dsink_ref[...] = jnp.zeros_like(dsink_ref)

        heads = range(N_HEADS)
        blocks = range(qb)
        kv_lanes = lambda h: slice((h // 4) * HEAD_LANES, (h // 4 + 1) * HEAD_LANES)
        geo = [_swa_block(n * qb + bi, pc_ref, pr_ref[bi]) for bi in blocks]
        wins = [pl.ds(g[0], 2 * w) for g in geo]
        kwins = [k_ref[win, :] for win in wins]
        vwins = [v_ref[win, :] for win in wins]

        do_ts, deltas, qms, doms = [], [], [], []
        for bi in blocks:
            rows = slice(bi * w, (bi + 1) * w)
            for lst in (do_ts, deltas, qms, doms):
                lst.append([])
            for j in range(N_HEADS // 2):
                pair = slice(j * HEAD_LANES, (j + 1) * HEAD_LANES)
                dop = do_ref[rows, pair]
                qp = q_ref[rows, pair]
                dt = _dot_nt(eye, dop)
                prod = dop.astype(F32) * o_ref[rows, pair]
                p_hi = prod.astype(BF16)
                p_lo = (prod - p_hi.astype(F32)).astype(BF16)
                for hh in range(2):
                    half, ones = (lo, ones_lo) if hh == 0 else (hi, ones_hi)
                    do_ts[bi].append(jnp.where(sub_lo, dt, 0.0).astype(BF16) if hh == 0
                                     else jnp.where(sub_lo, 0.0, dt).astype(BF16))
                    deltas[bi].append((_dot_nt(ones, p_hi) + _dot_nt(ones, p_lo))[0:1, :])
                    qms[bi].append(jnp.where(half, qp, jnp.zeros_like(qp)))
                    doms[bi].append(jnp.where(half, dop, jnp.zeros_like(dop)))
        sts, dpts = [], []
        for bi in blocks:
            sts.append([])
            dpts.append([])
            for j in range(N_HEADS // 2):
                a, b = 2 * j, 2 * j + 1
                st = _dot_nt(kwins[bi][:, kv_lanes(a)], jnp.concatenate([qms[bi][a], qms[bi][b]], axis=0))
                dpt = _dot(vwins[bi][:, kv_lanes(a)], jnp.concatenate([do_ts[bi][a], do_ts[bi][b]], axis=1))
                sts[bi] += [st[:, :w], st[:, w:]]
                dpts[bi] += [dpt[:, :w], dpt[:, w:]]
        pts, dsts = [], []
        for bi in blocks:
            pts.append([])
            dsts.append([])
            for h in heads:
                lse_h = lse_ref[0, h:h + 1, bi * w:(bi + 1) * w]
                pt = jnp.exp2(sts[bi][h] - _alibi(h) * geo[bi][1] - lse_h)
                dsts[bi].append((pt * (dpts[bi][h] - deltas[bi][h])).astype(BF16))
                pts[bi].append(pt.astype(BF16))
                dsink_ref[h:h + 1, :] += -jnp.exp2(sink_ref[0, h] * LOG2E - lse_h) * deltas[bi][h]
        for bi in blocks:
            for kv in range(2):
                group = range(4 * kv, 4 * kv + 4)
                dst_all = jnp.concatenate([dsts[bi][h] for h in group], axis=1)
                pt_all = jnp.concatenate([pts[bi][h] for h in group], axis=1)
                q_all = jnp.concatenate([qms[bi][h] for h in group], axis=0)
                do_all = jnp.concatenate([doms[bi][h] for h in group], axis=0)
                dk_ref[wins[bi], kv_lanes(4 * kv)] += _dot(dst_all, q_all)
                dv_ref[wins[bi], kv_lanes(4 * kv)] += _dot(pt_all, do_all)
        for bi in blocks:
            ktw = kt_ref[:, wins[bi]]
            for j in range(N_HEADS // 2):
                k_t = ktw[kv_lanes(2 * j), :]
                both = _dot(k_t, jnp.concatenate([dsts[bi][2 * j], dsts[bi][2 * j + 1]], axis=1))
                dq_t = jnp.where(sub_lo, both[:, :w], both[:, w:])
                dq_ref[bi * w:(bi + 1) * w, j * HEAD_LANES:(j + 1) * HEAD_LANES] = dq_t.T * SWA_SCALE

    n_tok = qs.shape[0]
    tok = lambda width: pl.BlockSpec((qb * w, width), lambda b, n: (b * steps + n, 0))
    whole = lambda width: pl.BlockSpec((seq, width), lambda b, n: (b, 0))
    return pl.pallas_call(
        body, name="swa_bwd", grid=(n_seq, steps),
        out_shape=[jax.ShapeDtypeStruct((n_tok, 512), F32), jax.ShapeDtypeStruct((n_tok, 256), F32),
                   jax.ShapeDtypeStruct((n_tok, 256), F32), jax.ShapeDtypeStruct((N_HEADS, HEAD_LANES), F32)],
        in_specs=[tok(512), whole(256), whole(256), pl.BlockSpec((qb * w, 512), lambda b, n: (b * steps + n, 1)), tok(512),
                  pl.BlockSpec((1, N_HEADS, qb * w), lambda b, n: (b, 0, n)),
                  whole(1), pl.BlockSpec((qb, 1, w), lambda b, n: (b * steps + n, 0, 0)),
                  pl.BlockSpec(memory_space=pltpu.SMEM)],
        out_specs=[tok(512), whole(256), whole(256), _full((N_HEADS, HEAD_LANES))],
        scratch_shapes=[pltpu.VMEM((2 * HEAD_LANES, seq), BF16)],
        compiler_params=_params(2),
    )(qs, kd, vd, do, o, lse, pos_col, pos_row, sinks)


def _post_call(x, target, o_mla, o_swa, gates, mod, b_ada, fg, w_out, seq):
    n_tok = x.shape[0]
    tm = min(TOKEN_TILE, seq)
    per_seq = seq // tm
    n_seq = n_tok // seq

    def body(x_ref, t_ref, om_ref, os_ref, g_ref, mod_ref, bada_ref, fg_ref, w_ref,
             dx2_ref, do_ref, dg_ref, gw_ref, gfg_ref, dgate_ref, loss_ref):
        i = pl.program_id(0)

        @pl.when(i == 0)
        def _():
            gw_ref[...] = jnp.zeros_like(gw_ref)
            gfg_ref[...] = jnp.zeros_like(gfg_ref)
            loss_ref[...] = jnp.zeros_like(loss_ref)

        @pl.when(i % per_seq == 0)
        def _():
            dgate_ref[...] = jnp.zeros_like(dgate_ref)

        gate = mod_ref[0][:, 2 * D_MODEL:] + bada_ref[:, 2 * D_MODEL:]
        fgv = fg_ref[...]
        subs = _sub_tiles(tm)
        gs = [g_ref[r, :] for r in subs]
        os_ = [jnp.concatenate([om_ref[r, :], os_ref[r, :]], axis=-1) for r in subs]
        sgs = [_sigmoid(g) for g in gs]
        sils = [g * sg for g, sg in zip(gs, sgs)]
        ypres = [(o * sil).astype(BF16) for o, sil in zip(os_, sils)]
        ys = [_dot(ypre, w_ref[...]) for ypre in ypres]
        dys, loss, gfg, dgate = [], 0.0, 0.0, 0.0
        for r, y in zip(subs, ys):
            x2 = x_ref[r, :] + gate * y
            r2 = lax.rsqrt(jnp.mean(x2 * x2, axis=-1, keepdims=True) + EPS)
            xn2 = x2 * r2
            err = xn2 * fgv - t_ref[r, :]
            loss = loss + jnp.sum(jnp.sum(err * err, axis=-1, keepdims=True), axis=0, keepdims=True)
            dout = err * (1.0 / D_MODEL)
            gfg = gfg + jnp.sum(dout * xn2, axis=0, keepdims=True)
            dxn2 = dout * fgv
            dx2 = r2 * (dxn2 - xn2 * jnp.mean(dxn2 * xn2, axis=-1, keepdims=True))
            dx2_ref[r, :] = dx2
            dgate = dgate + jnp.sum(dx2 * y, axis=0, keepdims=True)
            dys.append((dx2 * gate).astype(BF16))
        loss_ref[...] += jnp.broadcast_to(loss * (0.5 / D_MODEL), loss_ref.shape)
        gfg_ref[...] += gfg
        dgate_ref[0] += dgate
        gw_ref[...] += _dot_tn(jnp.concatenate(ypres, axis=0), jnp.concatenate(dys, axis=0))
        dypres = [_dot_nt(dy, w_ref[...]) for dy in dys]
        for r, dypre, o, g, sg, sil in zip(subs, dypres, os_, gs, sgs, sils):
            do_ref[r, :] = (dypre * sil).astype(BF16)
            dg_ref[r, :] = (dypre * o * (sg * (1.0 + g * (1.0 - sg)))).astype(BF16)

    tok = lambda w: pl.BlockSpec((tm, w), lambda i: (i, 0))
    per_b = pl.BlockSpec((1, 1, 3 * D_MODEL), lambda i: (i // per_seq, 0, 0))
    return pl.pallas_call(
        body, name="post", grid=(n_tok // tm,),
        out_shape=[jax.ShapeDtypeStruct((n_tok, D_MODEL), F32), jax.ShapeDtypeStruct((n_tok, D_MODEL), BF16),
                   jax.ShapeDtypeStruct((n_tok, D_MODEL), BF16), jax.ShapeDtypeStruct((D_MODEL, D_MODEL), F32),
                   jax.ShapeDtypeStruct((1, D_MODEL), F32), jax.ShapeDtypeStruct((n_seq, 1, D_MODEL), F32),
                   jax.ShapeDtypeStruct((1, HEAD_LANES), F32)],
        in_specs=[tok(D_MODEL), tok(D_MODEL), tok(512), tok(512), tok(D_MODEL), per_b, _full(b_ada.shape),
                  _full(fg.shape), _full(w_out.shape)],
        out_specs=[tok(D_MODEL), tok(D_MODEL), tok(D_MODEL), _full((D_MODEL, D_MODEL)), _full((1, D_MODEL)),
                   pl.BlockSpec((1, 1, D_MODEL), lambda i: (i // per_seq, 0, 0)), _full((1, HEAD_LANES))],
        compiler_params=_params(1),
    )(x, target, o_mla, o_swa, gates, mod, b_ada, fg, w_out)


def _mid_bwd_call(dqf, dkf, dv, zqkv, rope, qg, kvg, wq2, wkv, seq):
    n_tok = dqf.shape[0]
    tm = min(TOKEN_TILE, seq)

    def body(dq_ref, dk_ref, dv_ref, z_ref, rope_ref, qg_ref, kvg_ref, wq_ref, wkv_ref,
             dz_ref, gwq_ref, gwkv_ref, gqg_ref, gkvg_ref):
        i = pl.program_id(0)

        @pl.when(i == 0)
        def _():
            gwq_ref[...] = jnp.zeros_like(gwq_ref)
            gwkv_ref[...] = jnp.zeros_like(gwkv_ref)
            gqg_ref[...] = jnp.zeros_like(gqg_ref)
            gkvg_ref[...] = jnp.zeros_like(gkvg_ref)

        cos, sin = rope_ref[:, :HEAD_LANES], rope_ref[:, HEAD_LANES:]
        cf, sf = jnp.tile(cos, (1, N_HEADS)), jnp.tile(sin, (1, N_HEADS))
        dq = dq_ref[...] * MLA_SCALE
        dqr = jnp.concatenate([dq * cf, dq * sf], axis=-1).astype(BF16)
        zq, zkv = z_ref[:, :Q_LORA], z_ref[:, Q_LORA:]
        qgv, kvgv = qg_ref[...], kvg_ref[...]

        rq = lax.rsqrt(jnp.mean(zq * zq, axis=-1, keepdims=True) + EPS)
        xq = zq * rq
        gwq_ref[...] += _dot_tn((xq * qgv).astype(BF16), dqr)
        dqn = _dot_nt(dqr, wq_ref[...])
        gqg_ref[...] += jnp.sum(dqn * xq, axis=0, keepdims=True)
        dxq = dqn * qgv
        dz_ref[:, :Q_LORA] = (rq * (dxq - xq * jnp.mean(dxq * xq, axis=-1, keepdims=True))).astype(BF16)

        dk = dk_ref[...] * LN2
        dkv = jnp.concatenate([dk, dv_ref[...]], axis=-1).astype(BF16)
        rkv = lax.rsqrt(jnp.mean(zkv * zkv, axis=-1, keepdims=True) + EPS)
        xkv = zkv * rkv
        gwkv_ref[...] += _dot_tn((xkv * kvgv).astype(BF16), dkv)
        dkvn = _dot_nt(dkv, wkv_ref[...])
        gkvg_ref[...] += jnp.sum(dkvn * xkv, axis=0, keepdims=True)
        dxkv = dkvn * kvgv
        dz_ref[:, Q_LORA:A_KR] = (rkv * (dxkv - xkv * jnp.mean(dxkv * xkv, axis=-1, keepdims=True))).astype(BF16)

        dkpe = dk[:, :HEAD_LANES]
        for h in range(1, N_HEADS):
            dkpe = dkpe + dk[:, h * HEAD_LANES:(h + 1) * HEAD_LANES]
        dz_ref[:, A_KR:] = (jnp.where(_lane_lo(), 0.0, dkpe * cos) + pltpu.roll(dkpe * sin, HALF, 1)).astype(BF16)

    tok = lambda w: pl.BlockSpec((tm, w), lambda i: (i, 0))
    return pl.pallas_call(
        body, name="mid_bwd", grid=(n_tok // tm,),
        out_shape=[jax.ShapeDtypeStruct((n_tok, A_GM), BF16),
                   jax.ShapeDtypeStruct(wq2.shape, F32), jax.ShapeDtypeStruct(wkv.shape, F32),
                   jax.ShapeDtypeStruct((1, Q_LORA), F32), jax.ShapeDtypeStruct((1, KV_LORA), F32)],
        in_specs=[tok(1024), tok(1024), tok(512), tok(640), tok(2 * HEAD_LANES), _full(qg.shape), _full(kvg.shape),
                  _full(wq2.shape), _full(wkv.shape)],
        out_specs=[tok(A_GM), _full(wq2.shape), _full(wkv.shape), _full((1, Q_LORA)), _full((1, KV_LORA))],
        compiler_params=_params(1),
    )(dqf, dkf, dv, zqkv, rope, qg, kvg, wq2, wkv)


def _in_bwd_call(x, dx2, dz, dg, dqs, dkd, dvd, mod, b_ada, ng, wa, seq):
    n_tok = x.shape[0]
    tm = min(TOKEN_TILE, seq)
    per_seq = seq // tm
    n_seq = n_tok // seq

    def body(x_ref, dx2_ref, dz_ref, dg_ref, dqs_ref, dkd_ref, dvd_ref, mod_ref, bada_ref, ng_ref,
             wa_ref, gx_ref, gwa_ref, gng_ref, dshift_ref, dscale_ref):
        i = pl.program_id(0)

        @pl.when(i == 0)
        def _():
            gwa_ref[...] = jnp.zeros_like(gwa_ref)
            gng_ref[...] = jnp.zeros_like(gng_ref)

        @pl.when(i % per_seq == 0)
        def _():
            dshift_ref[...] = jnp.zeros_like(dshift_ref)
            dscale_ref[...] = jnp.zeros_like(dscale_ref)

        xv = x_ref[...]
        modv = mod_ref[0] + bada_ref[...]
        shift, scale = modv[:, :D_MODEL], modv[:, D_MODEL:2 * D_MODEL]
        ngv = ng_ref[...]
        r1 = lax.rsqrt(jnp.mean(xv * xv, axis=-1, keepdims=True) + EPS)
        xn = xv * r1
        hb = ((xn * ngv) * (1.0 + scale) + shift).astype(BF16)

        dgv = dg_ref[...]
        pieces = [(A_ZQ, dz_ref[...]), (A_GM, dgv[:, :512]), (A_QS, dqs_ref[...].astype(BF16)),
                  (A_KS, jnp.concatenate([_once(dkd_ref[...]) * LN2, _once(dvd_ref[...])], axis=1).astype(BF16)),
                  (A_GS, dgv[:, 512:])]
        dh = None
        for off, piece in pieces:
            wd = piece.shape[1]
            gwa_ref[:, off:off + wd] += _dot_tn(hb, piece)
            term = _dot_nt(piece, wa_ref[:, off:off + wd])
            dh = term if dh is None else dh + term

        dshift_ref[0] += jnp.sum(dh, axis=0, keepdims=True)
        dscale_ref[0] += jnp.sum(dh * (xn * ngv), axis=0, keepdims=True)
        gng_ref[...] += jnp.sum(dh * xn * (1.0 + scale), axis=0, keepdims=True)
        dxn = dh * ngv * (1.0 + scale)
        gx_ref[...] = dx2_ref[...] + r1 * (dxn - xn * jnp.mean(dxn * xn, axis=-1, keepdims=True))

    tok = lambda w: pl.BlockSpec((tm, w), lambda i: (i, 0))
    per_b = lambda w: pl.BlockSpec((1, 1, w), lambda i: (i // per_seq, 0, 0))
    return pl.pallas_call(
        body, name="in_bwd", grid=(n_tok // tm,),
        out_shape=[jax.ShapeDtypeStruct((n_tok, D_MODEL), F32), jax.ShapeDtypeStruct((D_MODEL, A_END), F32),
                   jax.ShapeDtypeStruct((1, D_MODEL), F32),
                   jax.ShapeDtypeStruct((n_seq, 1, D_MODEL), F32), jax.ShapeDtypeStruct((n_seq, 1, D_MODEL), F32)],
        in_specs=[tok(D_MODEL), tok(D_MODEL), tok(A_GM), tok(D_MODEL), tok(512), tok(256), tok(256),
                  per_b(3 * D_MODEL), _full(b_ada.shape), _full(ng.shape), _full(wa.shape)],
        out_specs=[tok(D_MODEL), _full((D_MODEL, A_END)), _full((1, D_MODEL)), per_b(D_MODEL), per_b(D_MODEL)],
        compiler_params=_params(1),
    )(x, dx2, dz, dg, dqs, dkd, dvd, mod, b_ada, ng, wa)


def _adam_math(w, g, m, v):
    m_new = ADAM_B1 * m + (1.0 - ADAM_B1) * g
    v_new = ADAM_B2 * v + (1.0 - ADAM_B2) * (g * g)
    m_hat = m_new / (1.0 - ADAM_B1 ** ADAM_STEP)
    v_hat = v_new / (1.0 - ADAM_B2 ** ADAM_STEP)
    delta = -ADAM_LR * (m_hat / (jnp.sqrt(v_hat) + ADAM_EPS) + ADAM_WD * w)
    return delta, m_new, v_new


def _adam_call(name, w, g, m, v):
    rows, cols = w.shape
    tr = next((t for t in (256, 128, 88) if rows % t == 0), rows)

    def body(w_ref, g_ref, m_ref, v_ref, d_ref, mo_ref, vo_ref):
        d, mn, vn = _adam_math(w_ref[...], g_ref[...], m_ref[...], v_ref[...])
        d_ref[...] = d
        mo_ref[...] = mn
        vo_ref[...] = vn

    spec = pl.BlockSpec((tr, cols), lambda i: (i, 0))
    return pl.pallas_call(
        body, name=name, grid=(rows // tr,),
        out_shape=[jax.ShapeDtypeStruct(w.shape, F32)] * 3,
        in_specs=[spec] * 4, out_specs=[spec] * 3,
        compiler_params=_params(1),
    )(w, g, m, v)


def _ada_bwd_call(act_all, dmod_cols, w, m, v):
    rows, cols = w.shape
    tr = 256

    def body(a_ref, dm_ref, w_ref, m_ref, v_ref, g_ref, d_ref, mo_ref, vo_ref):
        g = _dot_tn(a_ref[...].astype(BF16), dm_ref[...].astype(BF16))
        d, mn, vn = _adam_math(w_ref[...], g, m_ref[...], v_ref[...])
        g_ref[...] = g
        d_ref[...] = d
        mo_ref[...] = mn
        vo_ref[...] = vn

    spec = pl.BlockSpec((tr, cols), lambda i: (i, 0))
    nb = act_all.shape[0]
    return pl.pallas_call(
        body, name="ada_bwd", grid=(rows // tr,),
        out_shape=[jax.ShapeDtypeStruct(w.shape, F32)] * 4,
        in_specs=[pl.BlockSpec((nb, tr), lambda i: (0, i)), _full(dmod_cols.shape), spec, spec, spec],
        out_specs=[spec] * 4,
        compiler_params=_params(1),
    )(act_all, dmod_cols, w, m, v)


SMALL_ROW = {"norm_gain": (0, 1024), "final_gain": (1024, 2048), "q_norm_gain": (2048, 2432),
             "kv_norm_gain": (2432, 2688), "swa_sinks": (2688, 2696), "loss": (2816, 2944)}
SMALL_ORDER = ("b_ada", "norm_gain", "q_norm_gain", "kv_norm_gain", "swa_sinks", "final_gain")


def _small_call(parts_all, n_seq, params):
    k = len(params)

    def body(p_ref, *refs):
        ins, outs, loss_ref = refs[:3 * k], refs[3 * k:7 * k], refs[7 * k]
        row = p_ref[n_seq:n_seq + 1, :]
        for dv in range(1, 8):
            r0 = dv * ROWS_PER_DEVICE + n_seq
            row = row + p_ref[r0:r0 + 1, :]
        gb = None
        for dv in range(8):
            for r in range(n_seq):
                r0 = dv * ROWS_PER_DEVICE + r
                gb = p_ref[r0:r0 + 1, :] if gb is None else gb + p_ref[r0:r0 + 1, :]
        for j, name in enumerate(SMALL_ORDER):
            g = gb if name == "b_ada" else row[:, SMALL_ROW[name][0]:SMALL_ROW[name][1]]
            d, mn, vn = _adam_math(ins[3 * j][...], g, ins[3 * j + 1][...], ins[3 * j + 2][...])
            outs[4 * j][...] = g
            outs[4 * j + 1][...] = d
            outs[4 * j + 2][...] = mn
            outs[4 * j + 3][...] = vn
        loss_ref[...] = row[:, SMALL_ROW["loss"][0]:SMALL_ROW["loss"][1]]

    flat = [t for p in params for t in p]
    res = pl.pallas_call(
        body, name="small_update", grid=(1,),
        out_shape=[jax.ShapeDtypeStruct(p[0].shape, F32) for p in params for _ in range(4)]
        + [jax.ShapeDtypeStruct((1, HEAD_LANES), F32)],
        in_specs=[_full(parts_all.shape)] + [_full(t.shape) for t in flat],
        out_specs=[_full(p[0].shape) for p in params for _ in range(4)] + [_full((1, HEAD_LANES))],
        compiler_params=_params(1),
    )(parts_all, *flat)
    return [res[4 * j:4 * j + 4] for j in range(k)], res[4 * k]


def _rot(t):
    half = t.shape[-1] // 2
    return jnp.concatenate([-t[..., half:], t[..., :half]], axis=-1)


def _rot_t(g):
    half = g.shape[-1] // 2
    return jnp.concatenate([g[..., half:], -g[..., :half]], axis=-1)


def _columns(segments, lo, hi):
    out, at = [], 0
    for seg in segments:
        n = seg.shape[1]
        a, b = max(lo, at), min(hi, at + n)
        if a < b:
            out.append(seg[:, a - at:b - at])
        at += n
    return out


def _prepare_weights(w_in_blocks, w_uq, w_ukv):
    o = [0]
    for s in IN_SPLITS:
        o.append(o[-1] + s)
    part = lambda a, b: _columns(w_in_blocks, a, b)
    kr = jnp.concatenate(part(o[2], o[3]), axis=1)
    zero = jnp.zeros((kr.shape[0], 32), kr.dtype)
    wa = jnp.concatenate(part(0, o[2]) + [_rot(kr), zero, kr, zero] + part(o[3], o[8]), axis=1)
    uq = w_uq.reshape(Q_LORA, N_HEADS, MLA_NOPE + MLA_ROPE)
    zq = jnp.zeros((Q_LORA, N_HEADS, 32), w_uq.dtype)
    uq_full = jnp.concatenate([uq, zq], axis=-1).reshape(Q_LORA, 1024)
    uq_rot = jnp.concatenate([jnp.zeros((Q_LORA, N_HEADS, 64), w_uq.dtype), _rot(uq[..., MLA_NOPE:]), zq],
                             axis=-1).reshape(Q_LORA, 1024)
    wq2 = jnp.concatenate([uq_full, uq_rot], axis=1)
    ukv = w_ukv.reshape(KV_LORA, N_HEADS, 128)
    k_full = jnp.concatenate([ukv[..., :64], jnp.zeros((KV_LORA, N_HEADS, 64), w_ukv.dtype)], axis=-1).reshape(KV_LORA, 1024)
    wkv = jnp.concatenate([k_full, ukv[..., 64:].reshape(KV_LORA, 512)], axis=1)
    return wa, wq2, wkv


def _restore_grads(gwa, gwq2, gwkv):
    gkr = gwa[:, A_KR + 64:A_KR + 96] + _rot_t(gwa[:, A_KR:A_KR + 32])
    in_order = [gwa[:, :A_KR], gkr, gwa[:, A_GM:]]
    n = D_IN // 4
    g_in = [jnp.concatenate(_columns(in_order, k * n, (k + 1) * n), axis=1) for k in range(4)]
    gf = gwq2[:, :1024].reshape(Q_LORA, N_HEADS, 128)
    gr = gwq2[:, 1024:].reshape(Q_LORA, N_HEADS, 128)
    g_uq = jnp.concatenate([gf[..., :64], gf[..., 64:96] + _rot_t(gr[..., 64:96])], axis=-1).reshape(Q_LORA, 768)
    gk = gwkv[:, :1024].reshape(KV_LORA, N_HEADS, 128)[..., :64]
    gv = gwkv[:, 1024:].reshape(KV_LORA, N_HEADS, 64)
    g_ukv = jnp.concatenate([gk, gv], axis=-1).reshape(KV_LORA, 1024)
    return g_in, g_uq, g_ukv


def _local_step(x, positions, target, mod_rows, b_ada, ng, qg, kvg, sinks, fg, w_in_b, w_uq_b, w_ukv_b, w_out_b):
    n_seq, seq, _ = x.shape
    n_tok = n_seq * seq
    x2d = x.reshape(n_tok, D_MODEL)
    t2d = target.reshape(n_tok, D_MODEL)
    pos_f = positions.astype(F32)
    pos_col = pos_f.reshape(n_tok, 1)
    pos_row = pos_f.reshape(n_tok // SWA_WINDOW, 1, SWA_WINDOW)
    mod3 = mod_rows.reshape(n_seq, 1, 3 * D_MODEL)
    inv = ROPE_THETA ** (-jnp.arange(0, MLA_ROPE, 2, dtype=F32) / MLA_ROPE)
    inv128 = jnp.concatenate([jnp.zeros((64,), F32), inv, inv, jnp.zeros((32,), F32)]).reshape(1, 128)
    fg2 = fg.reshape(1, D_MODEL)

    wa, wq2, wkv = _prepare_weights(w_in_b, w_uq_b, w_ukv_b)

    zqkv, gates, qf, kf, v, qs, kd, vd, rope = _pre_call(x2d, pos_col, mod3, b_ada, ng, qg, kvg, inv128, wa, wq2, wkv, seq)
    o_mla, lse_mla = _mla_fwd_call(qf, kf, v, n_seq, seq)
    o_swa, lse_swa = _swa_fwd_call(qs, kd, vd, pos_col, pos_row, sinks, n_seq, seq)
    dx2, do, dg, g_out, g_fg, dgate, loss = _post_call(x2d, t2d, o_mla, o_swa, gates, mod3, b_ada, fg2, w_out_b, seq)
    dqf, dkf, dv = _mla_bwd_call(qf, kf, v, do, o_mla, lse_mla, n_seq, seq)
    dqs, dkd, dvd, dsink = _swa_bwd_call(qs, kd, vd, do, o_swa, lse_swa, pos_col, pos_row, sinks, n_seq, seq)
    dz, g_wq2, g_wkv, g_qg, g_kvg = _mid_bwd_call(dqf, dkf, dv, zqkv, rope, qg, kvg, wq2, wkv, seq)
    gx, g_wa, g_ng, dshift, dscale = _in_bwd_call(x2d, dx2, dz, dg, dqs, dkd, dvd, mod3, b_ada, ng, wa, seq)
    g_in, g_uq, g_ukv = _restore_grads(g_wa, g_wq2, g_wkv)
    dmod = jnp.concatenate([dshift, dscale, dgate], axis=-1).reshape(n_seq, 3 * D_MODEL)
    small_row = jnp.concatenate([g_ng, g_fg, g_qg, g_kvg, jnp.pad(jnp.sum(dsink, axis=1).reshape(1, N_HEADS), ((0, 0), (0, 120))),
                                 loss, jnp.zeros((1, 128), F32)], axis=1)
    return gx.reshape(x.shape), (g_in, g_uq, g_ukv, g_out), small_row, dmod


def kernel(x, c, positions, w_ada, b_ada, norm_gain, w_in, q_norm_gain, kv_norm_gain, w_uq, w_ukv, swa_sinks, w_out, final_gain, loss_target, m_w_ada, m_b_ada, m_norm_gain, m_w_in, m_q_norm_gain, m_kv_norm_gain, m_w_uq, m_w_ukv, m_swa_sinks, m_w_out, m_final_gain, v_w_ada, v_b_ada, v_norm_gain, v_w_in, v_q_norm_gain, v_kv_norm_gain, v_w_uq, v_w_ukv, v_swa_sinks, v_w_out, v_final_gain):
    n_seq = x.shape[0]
    xi, yi, ci = lax.axis_index("x"), lax.axis_index("y"), lax.axis_index("c")
    dev = 4 * xi + 2 * yi + ci
    chip = 2 * xi + yi

    halves = lambda w: w.astype(BF16).reshape(2, w.shape[0] // 2, w.shape[1])
    c_blk = jnp.pad(c, ((0, ROWS_PER_DEVICE - n_seq), (0, 0)))
    act_all, pieces, f_in, f_uq, f_ukv, f_out = _comm_fwd_call(
        c_blk, w_ada[0], [halves(w_in[0]), halves(w_uq[0]), halves(w_ukv[0]), halves(w_out[0])])
    mine = lax.dynamic_slice_in_dim(pieces, dev * ROWS_PER_DEVICE, n_seq, axis=1)
    mod_rows = jnp.transpose(mine, (1, 0, 2)).reshape(n_seq, 3 * D_MODEL)
    cols = lambda t, r: jnp.transpose(t.reshape(4, r, -1), (1, 0, 2)).reshape(r, -1)
    w_in_blocks = [f_in[k].reshape(D_MODEL, -1) for k in range(4)]
    w_uq_b, w_ukv_b = cols(f_uq, Q_LORA), cols(f_ukv, KV_LORA)
    w_out_b = f_out.reshape(D_MODEL, D_MODEL)

    gx, (g_in_blocks, g_uq, g_ukv, g_out), small_row, dmod = _local_step(
        x, positions, loss_target, mod_rows, b_ada, norm_gain, q_norm_gain, kv_norm_gain, swa_sinks, final_gain,
        w_in_blocks, w_uq_b, w_ukv_b, w_out_b)

    by_owner = lambda g, n: jnp.transpose(g.reshape(g.shape[0], 4, n), (1, 0, 2)).reshape(4, 2, g.shape[0] // 2, n)
    grads = [jnp.stack(g_in_blocks).reshape(4, 2, D_MODEL // 2, -1), by_owner(g_uq, 192), by_owner(g_ukv, 256),
             g_out.reshape(4, 2, 128, D_MODEL)]
    part = jnp.concatenate([dmod, small_row, jnp.zeros((ROWS_PER_DEVICE - n_seq - 1, 3 * D_MODEL), F32)], axis=0)
    r_in, r_uq, r_ukv, r_out, parts_all = _comm_bwd_call(grads, part)
    g_in_s, g_uq_s = r_in.reshape(w_in.shape[1:]), r_uq.reshape(w_uq.shape[1:])
    g_ukv_s, g_out_s = r_ukv.reshape(w_ukv.shape[1:]), r_out.reshape(w_out.shape[1:])

    tr = lambda a: jnp.swapaxes(a[0], 0, 1)
    back = lambda ts: [jnp.swapaxes(t, 0, 1) for t in ts]
    d_in, nm_in, nv_in = back(_adam_call("adam_w_in", tr(w_in), g_in_s.T, tr(m_w_in), tr(v_w_in)))
    d_uq, nm_uq, nv_uq = back(_adam_call("adam_w_uq", tr(w_uq), g_uq_s.T, tr(m_w_uq), tr(v_w_uq)))
    d_ukv, nm_ukv, nv_ukv = _adam_call("adam_w_ukv", w_ukv[0], g_ukv_s, m_w_ukv[0], v_w_ukv[0])
    d_out, nm_out, nv_out = _adam_call("adam_w_out", w_out[0], g_out_s, m_w_out[0], v_w_out[0])
    dmod_cols = lax.dynamic_slice_in_dim(parts_all, chip * 768, 768, axis=1)
    g_ada, d_ada, nm_ada, nv_ada = _ada_bwd_call(act_all, dmod_cols, w_ada[0], m_w_ada[0], v_w_ada[0])

    row = lambda t: t.reshape(1, -1)
    small = {"b_ada": (b_ada, m_b_ada, v_b_ada), "norm_gain": (norm_gain, m_norm_gain, v_norm_gain),
             "q_norm_gain": (q_norm_gain, m_q_norm_gain, v_q_norm_gain),
             "kv_norm_gain": (kv_norm_gain, m_kv_norm_gain, v_kv_norm_gain),
             "swa_sinks": (swa_sinks, m_swa_sinks, v_swa_sinks),
             "final_gain": (row(final_gain), row(m_final_gain), row(v_final_gain))}
    res, loss_row = _small_call(parts_all, n_seq, [small[name] for name in SMALL_ORDER])
    res = dict(zip(SMALL_ORDER, res))
    res["final_gain"] = [t.reshape(-1) for t in res["final_gain"]]
    e = lambda t: t[None]
    big = {"w_ada": (e(g_ada), e(d_ada), e(nm_ada), e(nv_ada)), "w_in": (e(g_in_s), e(d_in), e(nm_in), e(nv_in)),
           "w_uq": (e(g_uq_s), e(d_uq), e(nm_uq), e(nv_uq)), "w_ukv": (e(g_ukv_s), e(d_ukv), e(nm_ukv), e(nv_ukv)),
           "w_out": (e(g_out_s), e(d_out), e(nm_out), e(nv_out))}
    order = ("w_ada", "b_ada", "norm_gain", "w_in", "q_norm_gain", "kv_norm_gain", "w_uq", "w_ukv", "swa_sinks", "w_out",
             "final_gain")
    pick = lambda kind: [(big[n] if n in big else res[n])[kind] for n in order]
    return (loss_row[0, 0], gx, *pick(0), *pick(1), *pick(2), *pick(3))
```

```python
import jax
import jax.numpy as jnp
from jax import lax
from jax.experimental import pallas as pl
from jax.experimental.pallas import tpu as pltpu

F32 = jnp.float32
BF16 = jnp.bfloat16

D_MODEL = 1024
Q_LORA = 384
KV_LORA = 256
N_HEADS = 8
MLA_NOPE = 64
MLA_ROPE = 32
HEAD_LANES = 128
HALF = 64
SWA_WINDOW = 128
EPS = 1e-6
ROPE_THETA = 10000.0
MLA_SCALE = (MLA_NOPE + MLA_ROPE) ** -0.5
LOG2E = 1.4426950408889634
LN2 = 0.6931471805599453
SWA_SCALE = 64 ** -0.5
NEG = -1e30

ADAM_LR = 0.001
ADAM_B1 = 0.9
ADAM_B2 = 0.999
ADAM_EPS = 1e-08
ADAM_WD = 0.01
ADAM_STEP = 10

A_ZQ, A_ZKV, A_KR, A_GM, A_QS, A_KS, A_VS, A_GS, A_END = 0, 384, 640, 768, 1280, 1792, 1920, 2048, 2560
IN_SPLITS = (384, 256, 32, 512, 512, 128, 128, 512)
D_IN = sum(IN_SPLITS)

TOKEN_TILE = 512
ATT_TILE = 256
VMEM_LIMIT = 56 * 1024 * 1024


def _dot(a, b):
    return jnp.dot(a, b, preferred_element_type=F32)


def _dot_nt(a, b):
    return lax.dot_general(a, b, (((1,), (1,)), ((), ())), preferred_element_type=F32)


def _dot_tn(a, b):
    return lax.dot_general(a, b, (((0,), (0,)), ((), ())), preferred_element_type=F32)


def _params(n_grid):
    return pltpu.CompilerParams(dimension_semantics=("arbitrary",) * n_grid, vmem_limit_bytes=VMEM_LIMIT)


def _full(shape):
    nd = len(shape)
    return pl.BlockSpec(shape, lambda *_: (0,) * nd, pipeline_mode=pl.Buffered(1))


def _sigmoid(g):
    return 1.0 / (1.0 + jnp.exp(-g))


SUB_TILE = 256


def _sub_tiles(tm):
    sub = min(SUB_TILE, tm)
    return [slice(s * sub, (s + 1) * sub) for s in range(tm // sub)]


MESH = pl.DeviceIdType.MESH
ROWS_PER_DEVICE = 8
VMEM_SPEC = pl.BlockSpec(memory_space=pltpu.VMEM)
ANY_SPEC = pl.BlockSpec(memory_space=pl.ANY)


def _position():
    x, y, c = lax.axis_index("x"), lax.axis_index("y"), lax.axis_index("c")
    sibling = (x, y, 1 - c)
    others = [(1 - x, y, c), (x, 1 - y, c), (1 - x, 1 - y, c)]
    return (x, y, c), 4 * x + 2 * y + c, 2 * x + y, sibling, others


def _rows_of(dev):
    return pl.ds(pl.multiple_of(dev * ROWS_PER_DEVICE, ROWS_PER_DEVICE), ROWS_PER_DEVICE)


def _all_to_all_rows(block_ref, table_ref, dev, me, send_sems, recv_sems):
    x, y, c = me
    waits = []
    for k in range(1, 8):
        peer = (1 - x if k & 4 else x, 1 - y if k & 2 else y, 1 - c if k & 1 else c)
        pltpu.make_async_remote_copy(src_ref=block_ref, dst_ref=table_ref.at[_rows_of(dev)], send_sem=send_sems.at[k - 1],
                                     recv_sem=recv_sems.at[k - 1], device_id=peer, device_id_type=MESH).start()
        waits.append(pltpu.make_async_remote_copy(
            src_ref=block_ref, dst_ref=table_ref.at[_rows_of(jnp.bitwise_xor(dev, k))], send_sem=send_sems.at[k - 1],
            recv_sem=recv_sems.at[k - 1], device_id=peer, device_id_type=MESH))
    return waits


def _comm_fwd_call(c_blk, w_ada, shards):
    n = len(shards)

    def body(c_ref, wada_ref, *refs):
        w_refs, act_ref, pieces_ref, full_refs = refs[:n], refs[n], refs[n + 1], refs[n + 2:2 * n + 2]
        c_all_ref = refs[2 * n + 2]
        c_send, c_recv, p_send, p_recv, w_send, w_recv, f_send, f_recv, loc_sem = refs[2 * n + 3:]
        me, dev, chip, sibling, others = _position()
        core = me[2]
        chip_of = [2 * p[0] + p[1] for p in others]

        local = [pltpu.make_async_copy(w_refs[i], full_refs[i].at[chip], loc_sem.at[i]) for i in range(n)]
        for cp in local:
            cp.start()

        def over_ici(i, j, src_chip):
            return pltpu.make_async_remote_copy(
                src_ref=w_refs[i].at[core], dst_ref=full_refs[i].at[src_chip, core], send_sem=w_send.at[3 * i + j],
                recv_sem=w_recv.at[3 * i + j], device_id=others[j], device_id_type=MESH)

        def to_sibling(i, j, half):
            return pltpu.make_async_remote_copy(
                src_ref=full_refs[i].at[chip_of[j], half], dst_ref=full_refs[i].at[chip_of[j], half],
                send_sem=f_send.at[3 * i + j], recv_sem=f_recv.at[3 * i + j], device_id=sibling, device_id_type=MESH)

        sent = [over_ici(i, j, chip) for i in range(n) for j in range(3)]
        for cp in sent:
            cp.start()

        c_all_ref[_rows_of(dev), :] = c_ref[...]
        c_waits = _all_to_all_rows(c_ref, c_all_ref, dev, me, c_send, c_recv)
        for cp in c_waits:
            cp.wait()
        cv = c_all_ref[...]
        act = cv * _sigmoid(cv)
        act_ref[...] = act
        pieces_ref[chip] = _dot(act.astype(BF16), wada_ref[...].astype(BF16))
        piece = lambda j, src_chip: pltpu.make_async_remote_copy(
            src_ref=pieces_ref.at[chip], dst_ref=pieces_ref.at[src_chip], send_sem=p_send.at[j], recv_sem=p_recv.at[j],
            device_id=others[j], device_id_type=MESH)
        for j in range(3):
            piece(j, chip).start()
        for j in range(3):
            piece(j, chip).wait_send()
            piece(j, chip_of[j]).wait_recv()

        for i in range(n):
            for j in range(3):
                over_ici(i, j, chip_of[j]).wait_recv()
                to_sibling(i, j, core).start()
        for i in range(n):
            for j in range(3):
                to_sibling(i, j, 1 - core).wait_recv()
                to_sibling(i, j, core).wait_send()
        for cp in sent:
            cp.wait_send()
        for cp in local:
            cp.wait()

    rows = 8 * ROWS_PER_DEVICE
    dma = pltpu.SemaphoreType.DMA
    return pl.pallas_call(
        body, name="comm_fwd",
        out_shape=[jax.ShapeDtypeStruct((rows, D_MODEL), F32), jax.ShapeDtypeStruct((4, rows, w_ada.shape[1]), F32)]
        + [jax.ShapeDtypeStruct((4,) + s.shape, s.dtype) for s in shards],
        in_specs=[VMEM_SPEC, VMEM_SPEC] + [ANY_SPEC] * n,
        out_specs=[VMEM_SPEC, VMEM_SPEC] + [ANY_SPEC] * n,
        scratch_shapes=[pltpu.VMEM((rows, D_MODEL), F32), dma((7,)), dma((7,)), dma((3,)), dma((3,)),
                        dma((3 * n,)), dma((3 * n,)), dma((3 * n,)), dma((3 * n,)), dma((n,))],
        compiler_params=pltpu.CompilerParams(vmem_limit_bytes=VMEM_LIMIT),
    )(c_blk, w_ada, *shards)


def _comm_bwd_call(grads, part):
    n = len(grads)

    def body(part_ref, *refs):
        g_refs, f_refs, parts_ref = refs[:n], refs[n:2 * n], refs[2 * n]
        scratch = refs[2 * n + 1:]
        a_refs, b_refs, p_refs, r_refs = (scratch[k * n:(k + 1) * n] for k in range(4))
        s_send, s_recv, d_send, d_recv, e_send, e_recv, h_send, h_recv, loc_sem = scratch[4 * n:]
        me, dev, chip, sibling, others = _position()
        core = me[2]
        chip_of = [2 * p[0] + p[1] for p in others]

        parts_ref[_rows_of(dev), :] = part_ref[...]
        s_waits = _all_to_all_rows(part_ref, parts_ref, dev, me, s_send, s_recv)

        mine = [pltpu.make_async_copy(g_refs[i].at[:, core], a_refs[i], loc_sem.at[i]) for i in range(n)]
        swap = [pltpu.make_async_remote_copy(src_ref=g_refs[i].at[:, 1 - core], dst_ref=b_refs[i], send_sem=d_send.at[i],
                                             recv_sem=d_recv.at[i], device_id=sibling, device_id_type=MESH) for i in range(n)]
        order = sorted(range(n), key=lambda i: g_refs[i].shape[2] * g_refs[i].shape[3])
        for i in order:
            mine[i].start()
            swap[i].start()
        cross = [pltpu.make_async_remote_copy(src_ref=p_refs[i].at[chip_of[j]], dst_ref=r_refs[i].at[j],
                                              send_sem=e_send.at[3 * i + j], recv_sem=e_recv.at[3 * i + j],
                                              device_id=others[j], device_id_type=MESH) for i in range(n) for j in range(3)]
        for i in order:
            mine[i].wait()
            swap[i].wait()
            for k in range(4):
                s = a_refs[i][k] + b_refs[i][k]
                a_refs[i][k] = s
                p_refs[i][k] = s.astype(BF16)
            for j in range(3):
                cross[3 * i + j].start()
        share = {}
        for i in order:
            for j in range(3):
                cross[3 * i + j].wait()
            f_refs[i][core] = (a_refs[i][chip] + r_refs[i][0].astype(F32) + r_refs[i][1].astype(F32)
                               + r_refs[i][2].astype(F32))
            share[i] = pltpu.make_async_remote_copy(src_ref=f_refs[i].at[core], dst_ref=f_refs[i].at[core],
                                                    send_sem=h_send.at[i], recv_sem=h_recv.at[i], device_id=sibling,
                                                    device_id_type=MESH)
            share[i].start()
        for i in range(n):
            share[i].wait_send()
            pltpu.make_async_remote_copy(src_ref=f_refs[i].at[core], dst_ref=f_refs[i].at[1 - core], send_sem=h_send.at[i],
                                         recv_sem=h_recv.at[i], device_id=sibling, device_id_type=MESH).wait_recv()
        for cp in s_waits:
            cp.wait()

    rows = 8 * ROWS_PER_DEVICE
    dma = pltpu.SemaphoreType.DMA
    quarter = [(4,) + g.shape[2:] for g in grads]
    return pl.pallas_call(
        body, name="comm_bwd",
        out_shape=[jax.ShapeDtypeStruct((2,) + g.shape[2:], F32) for g in grads]
        + [jax.ShapeDtypeStruct((rows, part.shape[1]), F32)],
        in_specs=[VMEM_SPEC] + [ANY_SPEC] * n,
        out_specs=[VMEM_SPEC] * (n + 1),
        scratch_shapes=[pltpu.VMEM(q, F32) for q in quarter] + [pltpu.VMEM(q, F32) for q in quarter]
        + [pltpu.VMEM(q, BF16) for q in quarter] + [pltpu.VMEM((3,) + q[1:], BF16) for q in quarter]
        + [dma((7,)), dma((7,)), dma((n,)), dma((n,)), dma((3 * n,)), dma((3 * n,)), dma((n,)), dma((n,)), dma((n,))],
        compiler_params=pltpu.CompilerParams(vmem_limit_bytes=VMEM_LIMIT),
    )(part, *grads)


def _twice(t):
    lo = _lane_lo()
    other = pltpu.roll(t, HALF, 1)
    return jnp.concatenate([jnp.where(lo, t, other), jnp.where(lo, other, t)], axis=1)


def _once(g):
    first, second = g[:, :HEAD_LANES], g[:, HEAD_LANES:]
    return jnp.where(_lane_lo(), first + pltpu.roll(first, HALF, 1), second + pltpu.roll(second, HALF, 1))


def _rope_tables(pos_col, inv_row):
    ang = pos_col * inv_row
    return jnp.cos(ang), jnp.sin(ang)


def _pre_call(x, pos_col, mod, b_ada, ng, qg, kvg, inv128, wa, wq2, wkv, seq):
    n_tok = x.shape[0]
    tm = min(TOKEN_TILE, seq)
    per_seq = seq // tm

    def body(x_ref, pos_ref, mod_ref, bada_ref, ng_ref, qg_ref, kvg_ref, inv_ref, wa_ref, wq_ref, wkv_ref,
             zqkv_ref, gates_ref, qf_ref, kf_ref, v_ref, qs_ref, kd_ref, vd_ref, rope_ref):
        xv = x_ref[...]
        modv = mod_ref[0] + bada_ref[...]
        shift, scale = modv[:, :D_MODEL], modv[:, D_MODEL:2 * D_MODEL]
        r1 = lax.rsqrt(jnp.mean(xv * xv, axis=-1, keepdims=True) + EPS)
        h = ((xv * r1) * ng_ref[...]) * (1.0 + scale) + shift
        hb = h.astype(BF16)
        za = _dot(hb, wa_ref[...])
        zkr = za[:, A_KR:A_GM]
        cos, sin = _rope_tables(pos_ref[...], inv_ref[...])
        rope_ref[:, :HEAD_LANES] = cos
        rope_ref[:, HEAD_LANES:] = sin
        zqkv_ref[...] = za[:, :A_KR]
        gates_ref[:, :512] = za[:, A_GM:A_QS]
        gates_ref[:, 512:] = za[:, A_GS:A_END]
        qs_ref[...] = (za[:, A_QS:A_KS] * (SWA_SCALE * LOG2E)).astype(BF16)
        kd_ref[...] = _twice(za[:, A_KS:A_VS]).astype(BF16)
        vd_ref[...] = _twice(za[:, A_VS:A_GS]).astype(BF16)
        zq, zkv = za[:, A_ZQ:A_ZKV], za[:, A_ZKV:A_KR]
        rq = lax.rsqrt(jnp.mean(zq * zq, axis=-1, keepdims=True) + EPS)
        qn = ((zq * rq) * qg_ref[...]).astype(BF16)
        qr = _dot(qn, wq_ref[...])
        cf, sf = jnp.tile(cos, (1, N_HEADS)), jnp.tile(sin, (1, N_HEADS))
        qf_ref[...] = ((qr[:, :1024] * cf + qr[:, 1024:] * sf) * (MLA_SCALE * LOG2E)).astype(BF16)
        rkv = lax.rsqrt(jnp.mean(zkv * zkv, axis=-1, keepdims=True) + EPS)
        kvn = ((zkv * rkv) * kvg_ref[...]).astype(BF16)
        kv = _dot(kvn, wkv_ref[...])
        kpe = jnp.where(_lane_lo(), 0.0, zkr * cos) + pltpu.roll(zkr, HALF, 1) * sin
        kf_ref[...] = (kv[:, :1024] + jnp.tile(kpe, (1, N_HEADS))).astype(BF16)
        v_ref[...] = kv[:, 1024:].astype(BF16)

    tok = lambda w: pl.BlockSpec((tm, w), lambda i: (i, 0))
    outs = [(640, F32), (1024, F32), (1024, BF16), (1024, BF16), (512, BF16), (512, BF16), (256, BF16), (256, BF16),
            (2 * HEAD_LANES, F32)]
    return pl.pallas_call(
        body, name="pre", grid=(n_tok // tm,),
        out_shape=[jax.ShapeDtypeStruct((n_tok, w), dt) for w, dt in outs],
        in_specs=[tok(D_MODEL), tok(1), pl.BlockSpec((1, 1, 3 * D_MODEL), lambda i: (i // per_seq, 0, 0)),
                  _full(b_ada.shape), _full(ng.shape), _full(qg.shape), _full(kvg.shape), _full(inv128.shape),
                  _full(wa.shape), _full(wq2.shape), _full(wkv.shape)],
        out_specs=[tok(w) for w, _ in outs],
        compiler_params=_params(1),
    )(x, pos_col, mod, b_ada, ng, qg, kvg, inv128, wa, wq2, wkv)


def _lane_lo(width=HEAD_LANES):
    return lax.broadcasted_iota(jnp.int32, (1, width), 1) < HALF


def _eye(n=HEAD_LANES):
    r = lax.broadcasted_iota(jnp.int32, (n, n), 0)
    c = lax.broadcasted_iota(jnp.int32, (n, n), 1)
    return jnp.where(r == c, 1.0, 0.0).astype(BF16)


def _mla_fwd_call(qf, kf, v, n_seq, seq):
    tq = min(ATT_TILE, seq)
    nq = seq // tq

    ext = HALF + 16

    def body(q_ref, k_ref, v_ref, o_ref, lse_ref, vt_ref, acc_ref):
        i = pl.program_id(1)
        eye = _eye()

        @pl.when(i == 0)
        def _():
            for h in range(N_HEADS):
                vt_ref[h * ext + HALF:(h + 1) * ext, :] = jnp.ones((16, seq), BF16)
            for t in range(nq):
                for p in range(N_HEADS // 2):
                    pair = slice(p * HEAD_LANES, (p + 1) * HEAD_LANES)
                    v_t = _dot_nt(eye, v_ref[t * tq:(t + 1) * tq, pair]).astype(BF16)
                    for hh in range(2):
                        r0 = (2 * p + hh) * ext
                        vt_ref[r0:r0 + HALF, t * tq:(t + 1) * tq] = v_t[hh * HALF:(hh + 1) * HALF, :]

        q = q_ref[...]
        qcol = i * tq + lax.broadcasted_iota(jnp.int32, (1, tq), 1)
        heads = range(N_HEADS)
        lanes = [slice(h * HEAD_LANES, (h + 1) * HEAD_LANES) for h in heads]

        def make_step(masked, n_tiles):
            def step(kt0, carry):
                tiles = range(n_tiles)
                start = pl.multiple_of(kt0 * tq, tq)
                ks = [k_ref[pl.ds(pl.multiple_of((kt0 + t) * tq, tq), tq), :] for t in tiles]
                vt = vt_ref[:, pl.ds(start, n_tiles * tq)]
                last = n_tiles - 1
                if masked:
                    keep = ((kt0 + last) * tq + lax.broadcasted_iota(jnp.int32, (tq, 1), 0)) <= qcol

                def scores(h):
                    sts = [_dot_nt(ks[t][:, lanes[h]], q[:, lanes[h]]) for t in tiles]
                    if masked:
                        sts[last] = jnp.where(keep, sts[last], NEG)
                    return sts

                def softmax(h, sts):
                    m_old = carry[h]
                    m_new = m_old
                    for st in sts:
                        m_new = jnp.maximum(m_new, jnp.max(st, axis=0, keepdims=True))
                    pt = jnp.concatenate([jnp.exp2(st - m_new).astype(BF16) for st in sts], axis=0)
                    return m_new, jnp.exp2(m_old - m_new), pt

                def values(h, alpha, pt):
                    rows = slice(h * ext, (h + 1) * ext)
                    acc_ref[rows, :] = acc_ref[rows, :] * alpha + _dot(vt[rows, :], pt)

                sts, soft, out = {0: scores(0), 1: scores(1)}, {}, {}
                for h in range(N_HEADS + 1):
                    if h + 2 < N_HEADS:
                        sts[h + 2] = scores(h + 2)
                    if h < N_HEADS:
                        soft[h] = softmax(h, sts.pop(h))
                    if h >= 1:
                        m_new, alpha, pt = soft.pop(h - 1)
                        values(h - 1, alpha, pt)
                        out[h - 1] = m_new
                return tuple(out[h] for h in heads)
            return step

        acc_ref[...] = jnp.zeros_like(acc_ref)
        init = (jnp.full((1, tq), NEG, F32),) * N_HEADS
        count = i + 1
        carry = lax.fori_loop(0, (count + 1) // 2 - 1, lambda j, c: make_step(False, 2)(2 * j, c), init)
        carry = lax.cond(count % 2 == 0, lambda c: make_step(True, 2)(i - 1, c), lambda c: make_step(True, 1)(i, c), carry)
        dens = [acc_ref[h * ext + HALF:h * ext + HALF + 1, :] for h in heads]
        acc_t = jnp.concatenate([acc_ref[h * ext:h * ext + HALF, :] * (1.0 / dens[h]) for h in heads], axis=0)
        o_ref[...] = acc_t.T
        for h in heads:
            lse_ref[0, h // 4, h % 4:h % 4 + 1, :] = carry[h] + jnp.log2(dens[h])

    n_tok = qf.shape[0]
    return pl.pallas_call(
        body, name="mla_fwd", grid=(n_seq, nq),
        out_shape=[jax.ShapeDtypeStruct((n_tok, 512), F32), jax.ShapeDtypeStruct((n_seq, 2, 4, seq), F32)],
        in_specs=[pl.BlockSpec((tq, 1024), lambda b, i: (b * nq + i, 0)),
                  pl.BlockSpec((seq, 1024), lambda b, i: (b, 0)),
                  pl.BlockSpec((seq, 512), lambda b, i: (b, 0))],
        out_specs=[pl.BlockSpec((tq, 512), lambda b, i: (b * nq + i, 0)),
                   pl.BlockSpec((1, 2, 4, tq), lambda b, i: (b, 0, 0, i))],
        scratch_shapes=[pltpu.VMEM((N_HEADS * ext, seq), BF16), pltpu.VMEM((N_HEADS * ext, tq), F32)],
        compiler_params=_params(2),
    )(qf, kf, v)


def _mla_bwd_call(qf, kf, v, do, o, lse, n_seq, seq):
    tq = min(ATT_TILE, seq)
    nq = seq // tq

    nh = 4
    heads = range(nh)
    lanes = [slice(h * HEAD_LANES, (h + 1) * HEAD_LANES) for h in heads]

    def body(q_ref, k_ref, v_ref, do_ref, o_ref, lse_ref, dq_ref, dk_ref, dv_ref,
             kt_ref, dot_ref, delta_ref, dqt_ref, dvt_ref):
        eye = _eye()
        lo = _lane_lo()
        sub_lo = lax.broadcasted_iota(jnp.int32, (HEAD_LANES, 1), 0) < HALF
        ones_lo = jnp.where(jnp.broadcast_to(lo, (8, HEAD_LANES)), 1.0, 0.0).astype(BF16)
        ones_hi = jnp.where(jnp.broadcast_to(lo, (8, HEAD_LANES)), 0.0, 1.0).astype(BF16)

        for t in range(nq):
            r = slice(t * tq, (t + 1) * tq)
            kv = k_ref[r, :]
            for h in heads:
                kt_ref[lanes[h], r] = _dot_nt(eye, kv[:, lanes[h]]).astype(BF16)
            for p in range(nh // 2):
                dov = do_ref[r, lanes[p]]
                dt = _dot_nt(eye, dov)
                dot_ref[2 * p, :, r] = jnp.where(sub_lo, dt, 0.0).astype(BF16)
                dot_ref[2 * p + 1, :, r] = jnp.where(sub_lo, 0.0, dt).astype(BF16)
                prod = dov.astype(F32) * o_ref[r, lanes[p]]
                p_hi = prod.astype(BF16)
                p_lo = (prod - p_hi.astype(F32)).astype(BF16)
                delta_ref[2 * p, :, r] = _dot_nt(ones_lo, p_hi) + _dot_nt(ones_lo, p_lo)
                delta_ref[2 * p + 1, :, r] = _dot_nt(ones_hi, p_hi) + _dot_nt(ones_hi, p_lo)
        dqt_ref[...] = jnp.zeros_like(dqt_ref)

        def k_step(kt, _):
            kr = pl.ds(pl.multiple_of(kt * tq, tq), tq)
            k = k_ref[kr, :]
            vv = v_ref[kr, :]
            k_t = kt_ref[:, kr]
            krow = kt * tq + lax.broadcasted_iota(jnp.int32, (tq, 1), 0)

            def make_step(masked, n_tiles):
                def q_step(qt0, carry):
                    tiles = range(n_tiles)
                    qrs = [pl.ds(pl.multiple_of((qt0 + t) * tq, tq), tq) for t in tiles]
                    qs = [q_ref[qr, :] for qr in qrs]
                    if masked:
                        keep = krow <= (qt0 * tq + lax.broadcasted_iota(jnp.int32, (1, tq), 1))

                    def scores(h):
                        do_ts = [dot_ref[h, :, qr] for qr in qrs]
                        sts = [_dot_nt(k[:, lanes[h]], qs[t][:, lanes[h]]) for t in tiles]
                        dpts = [_dot(vv[:, lanes[h // 2]], do_ts[t]) for t in tiles]
                        return do_ts, sts, dpts

                    def softmax(h, sts, dpts):
                        pts, dsts = [], []
                        for t in tiles:
                            pt = jnp.exp2(sts[t] - lse_ref[0, 0, h:h + 1, qrs[t]])
                            if masked and t == 0:
                                pt = jnp.where(keep, pt, 0.0)
                            dsts.append((pt * (dpts[t] - delta_ref[h, 0:1, qrs[t]])).astype(BF16))
                            pts.append(pt.astype(BF16))
                        return pts, dsts

                    def grads(h, do_ts, pts, dsts):
                        half = slice((h % 2) * HALF, (h % 2 + 1) * HALF)
                        dst_all = jnp.concatenate(dsts, axis=1)
                        pt_all = jnp.concatenate(pts, axis=1)
                        do_all = jnp.concatenate([do_ts[t][half, :] for t in tiles], axis=1)
                        q_all = jnp.concatenate([qs[t][:, lanes[h]] for t in tiles], axis=0)
                        dvt_ref[h * HALF:(h + 1) * HALF, :] += _dot_nt(do_all, pt_all)
                        dk_ref[kr, lanes[h]] += _dot(dst_all, q_all)
                        for t in tiles:
                            dqt_ref[lanes[h], qrs[t]] += _dot(k_t[lanes[h], :], dsts[t])

                    first, second = {0: scores(0)}, {}
                    for h in range(nh + 1):
                        if h + 1 < nh:
                            first[h + 1] = scores(h + 1)
                        if h < nh:
                            do_ts, sts, dpts = first.pop(h)
                            second[h] = (do_ts,) + softmax(h, sts, dpts)
                        if h >= 1:
                            grads(h - 1, *second.pop(h - 1))
                    return carry
                return q_step

            dk_ref[kr, :] = jnp.zeros((tq, nh * HEAD_LANES), F32)
            dvt_ref[...] = jnp.zeros_like(dvt_ref)
            count = nq - kt
            lax.cond(count >= 2, lambda c: make_step(True, 2)(kt, c), lambda c: make_step(True, 1)(kt, c), 0)
            lax.fori_loop(1, count // 2, lambda j, c: make_step(False, 2)(kt + 2 * j, c), 0)
            lax.cond(jnp.logical_and(count % 2 == 1, count >= 3), lambda c: make_step(False, 1)(nq - 1, c), lambda c: c, 0)
            for p in range(nh // 2):
                dv_ref[kr, lanes[p]] = dvt_ref[p * HEAD_LANES:(p + 1) * HEAD_LANES, :].T
            return 0

        lax.fori_loop(0, nq, k_step, 0)
        for t in range(nq):
            r = slice(t * tq, (t + 1) * tq)
            for h in heads:
                dq_ref[r, lanes[h]] = dqt_ref[lanes[h], r].T

    n_tok = qf.shape[0]
    groups = N_HEADS // nh
    blk = lambda w: pl.BlockSpec((seq, w), lambda b, g: (b, g))
    return pl.pallas_call(
        body, name="mla_bwd", grid=(n_seq, groups),
        out_shape=[jax.ShapeDtypeStruct((n_tok, 1024), F32), jax.ShapeDtypeStruct((n_tok, 1024), F32),
                   jax.ShapeDtypeStruct((n_tok, 512), F32)],
        in_specs=[blk(512), blk(512), blk(256), blk(256), blk(256),
                  pl.BlockSpec((1, 1, nh, seq), lambda b, g: (b, g, 0, 0))],
        out_specs=[blk(512), blk(512), blk(256)],
        scratch_shapes=[pltpu.VMEM((nh * HEAD_LANES, seq), BF16), pltpu.VMEM((nh, HEAD_LANES, seq), BF16),
                        pltpu.VMEM((nh, 8, seq), F32), pltpu.VMEM((nh * HEAD_LANES, seq), F32),
                        pltpu.VMEM((nh * HALF, tq), F32)],
        compiler_params=_params(2),
    )(qf, kf, v, do, o, lse)


SWA_BLOCKS = 4


def _swa_block(n, pos_col_ref, posq):
    w = SWA_WINDOW
    start = pl.multiple_of(jnp.maximum(n - 1, 0) * w, w)
    posk = pos_col_ref[pl.ds(start, 2 * w), :]
    rel = (n * w + lax.broadcasted_iota(jnp.int32, (1, w), 1)) - (start + lax.broadcasted_iota(jnp.int32, (2 * w, 1), 0))
    valid = jnp.logical_and(rel >= 0, rel < w)
    return start, jnp.where(valid, posq - posk, 1e30)


def _alibi(h):
    return LOG2E * 2.0 ** -(h + 1)


def _transpose_rows(eye, src_ref, dst_ref, seq, width):
    step = 2 * SWA_WINDOW
    for t in range(seq // step):
        for p in range(width // HEAD_LANES):
            lanes = slice(p * HEAD_LANES, (p + 1) * HEAD_LANES)
            dst_ref[lanes, t * step:(t + 1) * step] = _dot_nt(eye, src_ref[t * step:(t + 1) * step, lanes]).astype(BF16)


def _swa_fwd_call(qs, kd, vd, pos_col, pos_row, sinks, n_seq, seq):
    w = SWA_WINDOW
    qb = SWA_BLOCKS
    steps = seq // (qb * w)
    ext = HALF + 16

    def body(q_ref, k_ref, v_ref, pc_ref, pr_ref, sink_ref, o_ref, lse_ref, vt_ref):
        n = pl.program_id(1)
        lo = _lane_lo()
        hi = jnp.logical_not(lo)
        eye = _eye()

        @pl.when(n == 0)
        def _():
            step = 2 * w
            for kv in range(2):
                vt_ref[kv * ext + HALF:(kv + 1) * ext, :] = jnp.ones((16, seq), BF16)
                for t in range(seq // step):
                    v_t = _dot_nt(eye, v_ref[t * step:(t + 1) * step, kv * HEAD_LANES:(kv + 1) * HEAD_LANES])
                    vt_ref[kv * ext:kv * ext + HALF, t * step:(t + 1) * step] = v_t[:HALF, :].astype(BF16)

        heads = range(N_HEADS)
        blocks = range(qb)
        geo = [_swa_block(n * qb + bi, pc_ref, pr_ref[bi]) for bi in blocks]
        wins = [pl.ds(g[0], 2 * w) for g in geo]
        kwins = [k_ref[win, :] for win in wins]
        vts = [vt_ref[:, win] for win in wins]
        sts = []
        for bi in blocks:
            q = q_ref[bi * w:(bi + 1) * w, :]
            sts.append([])
            for j in range(N_HEADS // 2):
                qp = q[:, j * HEAD_LANES:(j + 1) * HEAD_LANES]
                both = jnp.concatenate([jnp.where(lo, qp, jnp.zeros_like(qp)), jnp.where(hi, qp, jnp.zeros_like(qp))], axis=0)
                st = _dot_nt(kwins[bi][:, (j // 2) * HEAD_LANES:(j // 2 + 1) * HEAD_LANES], both)
                sts[bi] += [st[:, :w], st[:, w:]]
        ps, ms = [], []
        for bi in blocks:
            ps.append([])
            ms.append([])
            for h in heads:
                s = sts[bi][h] - _alibi(h) * geo[bi][1]
                m = jnp.maximum(jnp.max(s, axis=0, keepdims=True), sink_ref[0, h] * LOG2E)
                ps[bi].append(jnp.exp2(s - m).astype(BF16))
                ms[bi].append(m)
        for bi in blocks:
            ots = []
            for h in heads:
                pv = _dot(vts[bi][(h // 4) * ext:(h // 4 + 1) * ext, :], ps[bi][h])
                l = pv[HALF:HALF + 1, :] + jnp.exp2(sink_ref[0, h] * LOG2E - ms[bi][h])
                ots.append(pv[:HALF, :] * (1.0 / l))
                lse_ref[0, h:h + 1, bi * w:(bi + 1) * w] = ms[bi][h] + jnp.log2(l)
            o_ref[bi * w:(bi + 1) * w, :] = jnp.concatenate(ots, axis=0).T

    n_tok = qs.shape[0]
    tok = lambda width: pl.BlockSpec((qb * w, width), lambda b, n: (b * steps + n, 0))
    whole = lambda width: pl.BlockSpec((seq, width), lambda b, n: (b, 0))
    return pl.pallas_call(
        body, name="swa_fwd", grid=(n_seq, steps),
        out_shape=[jax.ShapeDtypeStruct((n_tok, 512), F32), jax.ShapeDtypeStruct((n_seq, N_HEADS, seq), F32)],
        in_specs=[tok(512), whole(256), whole(256), whole(1), pl.BlockSpec((qb, 1, w), lambda b, n: (b * steps + n, 0, 0)),
                  pl.BlockSpec(memory_space=pltpu.SMEM)],
        out_specs=[tok(512), pl.BlockSpec((1, N_HEADS, qb * w), lambda b, n: (b, 0, n))],
        scratch_shapes=[pltpu.VMEM((2 * ext, seq), BF16)],
        compiler_params=_params(2),
    )(qs, kd, vd, pos_col, pos_row, sinks)


def _swa_bwd_call(qs, kd, vd, do, o, lse, pos_col, pos_row, sinks, n_seq, seq):
    w = SWA_WINDOW
    qb = SWA_BLOCKS
    steps = seq // (qb * w)

    def body(q_ref, k_ref, v_ref, do_ref, o_ref, lse_ref, pc_ref, pr_ref, sink_ref, dq_ref, dk_ref, dv_ref, dsink_ref,
             kt_ref):
        b, n = pl.program_id(0), pl.program_id(1)
        lo = _lane_lo()
        hi = jnp.logical_not(lo)
        sub_lo = lax.broadcasted_iota(jnp.int32, (HEAD_LANES, 1), 0) < HALF
        eye = _eye()
        ones_lo = jnp.where(jnp.broadcast_to(lo, (8, HEAD_LANES)), 1.0, 0.0).astype(BF16)
        ones_hi = jnp.where(jnp.broadcast_to(lo, (8, HEAD_LANES)), 0.0, 1.0).astype(BF16)

        @pl.when(n == 0)
        def _():
            dk_ref[...] = jnp.zeros_like(dk_ref)
            dv_ref[...] = jnp.zeros_like(dv_ref)
            _transpose_rows(eye, k_ref, kt_ref, seq, 2 * HEAD_LANES)

        @pl.when(jnp.logical_and(n == 0, b == 0))
        def _():
            dsink_ref[...] = jnp.zeros_like(dsink_ref)

        heads = range(N_HEADS)
        blocks = range(qb)
        kv_lanes = lambda h: slice((h // 4) * HEAD_LANES, (h // 4 + 1) * HEAD_LANES)
        geo = [_swa_block(n * qb + bi, pc_ref, pr_ref[bi]) for bi in blocks]
        wins = [pl.ds(g[0], 2 * w) for g in geo]
        kwins = [k_ref[win, :] for win in wins]
        vwins = [v_ref[win, :] for win in wins]

        do_ts, deltas, qms, doms = [], [], [], []
        for bi in blocks:
            rows = slice(bi * w, (bi + 1) * w)
            for lst in (do_ts, deltas, qms, doms):
                lst.append([])
            for j in range(N_HEADS // 2):
                pair = slice(j * HEAD_LANES, (j + 1) * HEAD_LANES)
                dop = do_ref[rows, pair]
                qp = q_ref[rows, pair]
                dt = _dot_nt(eye, dop)
                prod = dop.astype(F32) * o_ref[rows, pair]
                p_hi = prod.astype(BF16)
                p_lo = (prod - p_hi.astype(F32)).astype(BF16)
                for hh in range(2):
                    half, ones = (lo, ones_lo) if hh == 0 else (hi, ones_hi)
                    do_ts[bi].append(jnp.where(sub_lo, dt, 0.0).astype(BF16) if hh == 0
                                     else jnp.where(sub_lo, 0.0, dt).astype(BF16))
                    deltas[bi].append((_dot_nt(ones, p_hi) + _dot_nt(ones, p_lo))[0:1, :])
                    qms[bi].append(jnp.where(half, qp, jnp.zeros_like(qp)))
                    doms[bi].append(jnp.where(half, dop, jnp.zeros_like(dop)))
        sts, dpts = [], []
        for bi in blocks:
            sts.append([])
            dpts.append([])
            for j in range(N_HEADS // 2):
                a, b = 2 * j, 2 * j + 1
                st = _dot_nt(kwins[bi][:, kv_lanes(a)], jnp.concatenate([qms[bi][a], qms[bi][b]], axis=0))
                dpt = _dot(vwins[bi][:, kv_lanes(a)], jnp.concatenate([do_ts[bi][a], do_ts[bi][b]], axis=1))
                sts[bi] += [st[:, :w], st[:, w:]]
                dpts[bi] += [dpt[:, :w], dpt[:, w:]]
        pts, dsts = [], []
        for bi in blocks:
            pts.append([])
            dsts.append([])
            for h in heads:
                lse_h = lse_ref[0, h:h + 1, bi * w:(bi + 1) * w]
                pt = jnp.exp2(sts[bi][h] - _alibi(h) * geo[bi][1] - lse_h)
                dsts[bi].append((pt * (dpts[bi][h] - deltas[bi][h])).astype(BF16))
                pts[bi].append(pt.astype(BF16))
                dsink_ref[h:h + 1, :] += -jnp.exp2(sink_ref[0, h] * LOG2E - lse_h) * deltas[bi][h]
        for bi in blocks:
            for kv in range(2):
                group = range(4 * kv, 4 * kv + 4)
                dst_all = jnp.concatenate([dsts[bi][h] for h in group], axis=1)
                pt_all = jnp.concatenate([pts[bi][h] for h in group], axis=1)
                q_all = jnp.concatenate([qms[bi][h] for h in group], axis=0)
                do_all = jnp.concatenate([doms[bi][h] for h in group], axis=0)
                dk_ref[wins[bi], kv_lanes(4 * kv)] += _dot(dst_all, q_all)
                dv_ref[wins[bi], kv_lanes(4 * kv)] += _dot(pt_all, do_all)
        for bi in blocks:
            ktw = kt_ref[:, wins[bi]]
            for j in range(N_HEADS // 2):
                k_t = ktw[kv_lanes(2 * j), :]
                both = _dot(k_t, jnp.concatenate([dsts[bi][2 * j], dsts[bi][2 * j + 1]], axis=1))
                dq_t = jnp.where(sub_lo, both[:, :w], both[:, w:])
                dq_ref[bi * w:(bi + 1) * w, j * HEAD_LANES:(j + 1) * HEAD_LANES] = dq_t.T * SWA_SCALE

    n_tok = qs.shape[0]
    tok = lambda width: pl.BlockSpec((qb * w, width), lambda b, n: (b * steps + n, 0))
    whole = lambda width: pl.BlockSpec((seq, width), lambda b, n: (b, 0))
    return pl.pallas_call(
        body, name="swa_bwd", grid=(n_seq, steps),
        out_shape=[jax.ShapeDtypeStruct((n_tok, 512), F32), jax.ShapeDtypeStruct((n_tok, 256), F32),
                   jax.ShapeDtypeStruct((n_tok, 256), F32), jax.ShapeDtypeStruct((N_HEADS, HEAD_LANES), F32)],
        in_specs=[tok(512), whole(256), whole(256), pl.BlockSpec((qb * w, 512), lambda b, n: (b * steps + n, 1)), tok(512),
                  pl.BlockSpec((1, N_HEADS, qb * w), lambda b, n: (b, 0, n)),
                  whole(1), pl.BlockSpec((qb, 1, w), lambda b, n: (b * steps + n, 0, 0)),
                  pl.BlockSpec(memory_space=pltpu.SMEM)],
        out_specs=[tok(512), whole(256), whole(256), _full((N_HEADS, HEAD_LANES))],
        scratch_shapes=[pltpu.VMEM((2 * HEAD_LANES, seq), BF16)],
        compiler_params=_params(2),
    )(qs, kd, vd, do, o, lse, pos_col, pos_row, sinks)


def _post_call(x, target, o_mla, o_swa, gates, mod, b_ada, fg, w_out, seq):
    n_tok = x.shape[0]
    tm = min(TOKEN_TILE, seq)
    per_seq = seq // tm
    n_seq = n_tok // seq

    def body(x_ref, t_ref, om_ref, os_ref, g_ref, mod_ref, bada_ref, fg_ref, w_ref,
             dx2_ref, do_ref, dg_ref, gw_ref, gfg_ref, dgate_ref, loss_ref):
        i = pl.program_id(0)

        @pl.when(i == 0)
        def _():
            gw_ref[...] = jnp.zeros_like(gw_ref)
            gfg_ref[...] = jnp.zeros_like(gfg_ref)
            loss_ref[...] = jnp.zeros_like(loss_ref)

        @pl.when(i % per_seq == 0)
        def _():
            dgate_ref[...] = jnp.zeros_like(dgate_ref)

        gate = mod_ref[0][:, 2 * D_MODEL:] + bada_ref[:, 2 * D_MODEL:]
        fgv = fg_ref[...]
        subs = _sub_tiles(tm)
        gs = [g_ref[r, :] for r in subs]
        os_ = [jnp.concatenate([om_ref[r, :], os_ref[r, :]], axis=-1) for r in subs]
        sgs = [_sigmoid(g) for g in gs]
        sils = [g * sg for g, sg in zip(gs, sgs)]
        ypres = [(o * sil).astype(BF16) for o, sil in zip(os_, sils)]
        ys = [_dot(ypre, w_ref[...]) for ypre in ypres]
        dys, loss, gfg, dgate = [], 0.0, 0.0, 0.0
        for r, y in zip(subs, ys):
            x2 = x_ref[r, :] + gate * y
            r2 = lax.rsqrt(jnp.mean(x2 * x2, axis=-1, keepdims=True) + EPS)
            xn2 = x2 * r2
            err = xn2 * fgv - t_ref[r, :]
            loss = loss + jnp.sum(jnp.sum(err * err, axis=-1, keepdims=True), axis=0, keepdims=True)
            dout = err * (1.0 / D_MODEL)
            gfg = gfg + jnp.sum(dout * xn2, axis=0, keepdims=True)
            dxn2 = dout * fgv
            dx2 = r2 * (dxn2 - xn2 * jnp.mean(dxn2 * xn2, axis=-1, keepdims=True))
            dx2_ref[r, :] = dx2
            dgate = dgate + jnp.sum(dx2 * y, axis=0, keepdims=True)
            dys.append((dx2 * gate).astype(BF16))
        loss_ref[...] += jnp.broadcast_to(loss * (0.5 / D_MODEL), loss_ref.shape)
        gfg_ref[...] += gfg
        dgate_ref[0] += dgate
        gw_ref[...] += _dot_tn(jnp.concatenate(ypres, axis=0), jnp.concatenate(dys, axis=0))
        dypres = [_dot_nt(dy, w_ref[...]) for dy in dys]
        for r, dypre, o, g, sg, sil in zip(subs, dypres, os_, gs, sgs, sils):
            do_ref[r, :] = (dypre * sil).astype(BF16)
            dg_ref[r, :] = (dypre * o * (sg * (1.0 + g * (1.0 - sg)))).astype(BF16)

    tok = lambda w: pl.BlockSpec((tm, w), lambda i: (i, 0))
    per_b = pl.BlockSpec((1, 1, 3 * D_MODEL), lambda i: (i // per_seq, 0, 0))
    return pl.pallas_call(
        body, name="post", grid=(n_tok // tm,),
        out_shape=[jax.ShapeDtypeStruct((n_tok, D_MODEL), F32), jax.ShapeDtypeStruct((n_tok, D_MODEL), BF16),
                   jax.ShapeDtypeStruct((n_tok, D_MODEL), BF16), jax.ShapeDtypeStruct((D_MODEL, D_MODEL), F32),
                   jax.ShapeDtypeStruct((1, D_MODEL), F32), jax.ShapeDtypeStruct((n_seq, 1, D_MODEL), F32),
                   jax.ShapeDtypeStruct((1, HEAD_LANES), F32)],
        in_specs=[tok(D_MODEL), tok(D_MODEL), tok(512), tok(512), tok(D_MODEL), per_b, _full(b_ada.shape),
                  _full(fg.shape), _full(w_out.shape)],
        out_specs=[tok(D_MODEL), tok(D_MODEL), tok(D_MODEL), _full((D_MODEL, D_MODEL)), _full((1, D_MODEL)),
                   pl.BlockSpec((1, 1, D_MODEL), lambda i: (i // per_seq, 0, 0)), _full((1, HEAD_LANES))],
        compiler_params=_params(1),
    )(x, target, o_mla, o_swa, gates, mod, b_ada, fg, w_out)


def _mid_bwd_call(dqf, dkf, dv, zqkv, rope, qg, kvg, wq2, wkv, seq):
    n_tok = dqf.shape[0]
    tm = min(TOKEN_TILE, seq)

    def body(dq_ref, dk_ref, dv_ref, z_ref, rope_ref, qg_ref, kvg_ref, wq_ref, wkv_ref,
             dz_ref, gwq_ref, gwkv_ref, gqg_ref, gkvg_ref):
        i = pl.program_id(0)

        @pl.when(i == 0)
        def _():
            gwq_ref[...] = jnp.zeros_like(gwq_ref)
            gwkv_ref[...] = jnp.zeros_like(gwkv_ref)
            gqg_ref[...] = jnp.zeros_like(gqg_ref)
            gkvg_ref[...] = jnp.zeros_like(gkvg_ref)

        cos, sin = rope_ref[:, :HEAD_LANES], rope_ref[:, HEAD_LANES:]
        cf, sf = jnp.tile(cos, (1, N_HEADS)), jnp.tile(sin, (1, N_HEADS))
        dq = dq_ref[...] * MLA_SCALE
        dqr = jnp.concatenate([dq * cf, dq * sf], axis=-1).astype(BF16)
        zq, zkv = z_ref[:, :Q_LORA], z_ref[:, Q_LORA:]
        qgv, kvgv = qg_ref[...], kvg_ref[...]

        rq = lax.rsqrt(jnp.mean(zq * zq, axis=-1, keepdims=True) + EPS)
        xq = zq * rq
        gwq_ref[...] += _dot_tn((xq * qgv).astype(BF16), dqr)
        dqn = _dot_nt(dqr, wq_ref[...])
        gqg_ref[...] += jnp.sum(dqn * xq, axis=0, keepdims=True)
        dxq = dqn * qgv
        dz_ref[:, :Q_LORA] = (rq * (dxq - xq * jnp.mean(dxq * xq, axis=-1, keepdims=True))).astype(BF16)

        dk = dk_ref[...] * LN2
        dkv = jnp.concatenate([dk, dv_ref[...]], axis=-1).astype(BF16)
        rkv = lax.rsqrt(jnp.mean(zkv * zkv, axis=-1, keepdims=True) + EPS)
        xkv = zkv * rkv
        gwkv_ref[...] += _dot_tn((xkv * kvgv).astype(BF16), dkv)
        dkvn = _dot_nt(dkv, wkv_ref[...])
        gkvg_ref[...] += jnp.sum(dkvn * xkv, axis=0, keepdims=True)
        dxkv = dkvn * kvgv
        dz_ref[:, Q_LORA:A_KR] = (rkv * (dxkv - xkv * jnp.mean(dxkv * xkv, axis=-1, keepdims=True))).astype(BF16)

        dkpe = dk[:, :HEAD_LANES]
        for h in range(1, N_HEADS):
            dkpe = dkpe + dk[:, h * HEAD_LANES:(h + 1) * HEAD_LANES]
        dz_ref[:, A_KR:] = (jnp.where(_lane_lo(), 0.0, dkpe * cos) + pltpu.roll(dkpe * sin, HALF, 1)).astype(BF16)

    tok = lambda w: pl.BlockSpec((tm, w), lambda i: (i, 0))
    return pl.pallas_call(
        body, name="mid_bwd", grid=(n_tok // tm,),
        out_shape=[jax.ShapeDtypeStruct((n_tok, A_GM), BF16),
                   jax.ShapeDtypeStruct(wq2.shape, F32), jax.ShapeDtypeStruct(wkv.shape, F32),
                   jax.ShapeDtypeStruct((1, Q_LORA), F32), jax.ShapeDtypeStruct((1, KV_LORA), F32)],
        in_specs=[tok(1024), tok(1024), tok(512), tok(640), tok(2 * HEAD_LANES), _full(qg.shape), _full(kvg.shape),
                  _full(wq2.shape), _full(wkv.shape)],
        out_specs=[tok(A_GM), _full(wq2.shape), _full(wkv.shape), _full((1, Q_LORA)), _full((1, KV_LORA))],
        compiler_params=_params(1),
    )(dqf, dkf, dv, zqkv, rope, qg, kvg, wq2, wkv)


def _in_bwd_call(x, dx2, dz, dg, dqs, dkd, dvd, mod, b_ada, ng, wa, seq):
    n_tok = x.shape[0]
    tm = min(TOKEN_TILE, seq)
    per_seq = seq // tm
    n_seq = n_tok // seq

    def body(x_ref, dx2_ref, dz_ref, dg_ref, dqs_ref, dkd_ref, dvd_ref, mod_ref, bada_ref, ng_ref,
             wa_ref, gx_ref, gwa_ref, gng_ref, dshift_ref, dscale_ref):
        i = pl.program_id(0)

        @pl.when(i == 0)
        def _():
            gwa_ref[...] = jnp.zeros_like(gwa_ref)
            gng_ref[...] = jnp.zeros_like(gng_ref)

        @pl.when(i % per_seq == 0)
        def _():
            dshift_ref[...] = jnp.zeros_like(dshift_ref)
            dscale_ref[...] = jnp.zeros_like(dscale_ref)

        xv = x_ref[...]
        modv = mod_ref[0] + bada_ref[...]
        shift, scale = modv[:, :D_MODEL], modv[:, D_MODEL:2 * D_MODEL]
        ngv = ng_ref[...]
        r1 = lax.rsqrt(jnp.mean(xv * xv, axis=-1, keepdims=True) + EPS)
        xn = xv * r1
        hb = ((xn * ngv) * (1.0 + scale) + shift).astype(BF16)

        dgv = dg_ref[...]
        pieces = [(A_ZQ, dz_ref[...]), (A_GM, dgv[:, :512]), (A_QS, dqs_ref[...].astype(BF16)),
                  (A_KS, jnp.concatenate([_once(dkd_ref[...]) * LN2, _once(dvd_ref[...])], axis=1).astype(BF16)),
                  (A_GS, dgv[:, 512:])]
        dh = None
        for off, piece in pieces:
            wd = piece.shape[1]
            gwa_ref[:, off:off + wd] += _dot_tn(hb, piece)
            term = _dot_nt(piece, wa_ref[:, off:off + wd])
            dh = term if dh is None else dh + term

        dshift_ref[0] += jnp.sum(dh, axis=0, keepdims=True)
        dscale_ref[0] += jnp.sum(dh * (xn * ngv), axis=0, keepdims=True)
        gng_ref[...] += jnp.sum(dh * xn * (1.0 + scale), axis=0, keepdims=True)
        dxn = dh * ngv * (1.0 + scale)
        gx_ref[...] = dx2_ref[...] + r1 * (dxn - xn * jnp.mean(dxn * xn, axis=-1, keepdims=True))

    tok = lambda w: pl.BlockSpec((tm, w), lambda i: (i, 0))
    per_b = lambda w: pl.BlockSpec((1, 1, w), lambda i: (i // per_seq, 0, 0))
    return pl.pallas_call(
        body, name="in_bwd", grid=(n_tok // tm,),
        out_shape=[jax.ShapeDtypeStruct((n_tok, D_MODEL), F32), jax.ShapeDtypeStruct((D_MODEL, A_END), F32),
                   jax.ShapeDtypeStruct((1, D_MODEL), F32),
                   jax.ShapeDtypeStruct((n_seq, 1, D_MODEL), F32), jax.ShapeDtypeStruct((n_seq, 1, D_MODEL), F32)],
        in_specs=[tok(D_MODEL), tok(D_MODEL), tok(A_GM), tok(D_MODEL), tok(512), tok(256), tok(256),
                  per_b(3 * D_MODEL), _full(b_ada.shape), _full(ng.shape), _full(wa.shape)],
        out_specs=[tok(D_MODEL), _full((D_MODEL, A_END)), _full((1, D_MODEL)), per_b(D_MODEL), per_b(D_MODEL)],
        compiler_params=_params(1),
    )(x, dx2, dz, dg, dqs, dkd, dvd, mod, b_ada, ng, wa)


def _adam_math(w, g, m, v):
    m_new = ADAM_B1 * m + (1.0 - ADAM_B1) * g
    v_new = ADAM_B2 * v + (1.0 - ADAM_B2) * (g * g)
    m_hat = m_new / (1.0 - ADAM_B1 ** ADAM_STEP)
    v_hat = v_new / (1.0 - ADAM_B2 ** ADAM_STEP)
    delta = -ADAM_LR * (m_hat / (jnp.sqrt(v_hat) + ADAM_EPS) + ADAM_WD * w)
    return delta, m_new, v_new


def _adam_call(name, w, g, m, v):
    rows, cols = w.shape
    tr = next((t for t in (256, 128, 88) if rows % t == 0), rows)

    def body(w_ref, g_ref, m_ref, v_ref, d_ref, mo_ref, vo_ref):
        d, mn, vn = _adam_math(w_ref[...], g_ref[...], m_ref[...], v_ref[...])
        d_ref[...] = d
        mo_ref[...] = mn
        vo_ref[...] = vn

    spec = pl.BlockSpec((tr, cols), lambda i: (i, 0))
    return pl.pallas_call(
        body, name=name, grid=(rows // tr,),
        out_shape=[jax.ShapeDtypeStruct(w.shape, F32)] * 3,
        in_specs=[spec] * 4, out_specs=[spec] * 3,
        compiler_params=_params(1),
    )(w, g, m, v)


def _ada_bwd_call(act_all, dmod_cols, w, m, v):
    rows, cols = w.shape
    tr = 256

    def body(a_ref, dm_ref, w_ref, m_ref, v_ref, g_ref, d_ref, mo_ref, vo_ref):
        g = _dot_tn(a_ref[...].astype(BF16), dm_ref[...].astype(BF16))
        d, mn, vn = _adam_math(w_ref[...], g, m_ref[...], v_ref[...])
        g_ref[...] = g
        d_ref[...] = d
        mo_ref[...] = mn
        vo_ref[...] = vn

    spec = pl.BlockSpec((tr, cols), lambda i: (i, 0))
    nb = act_all.shape[0]
    return pl.pallas_call(
        body, name="ada_bwd", grid=(rows // tr,),
        out_shape=[jax.ShapeDtypeStruct(w.shape, F32)] * 4,
        in_specs=[pl.BlockSpec((nb, tr), lambda i: (0, i)), _full(dmod_cols.shape), spec, spec, spec],
        out_specs=[spec] * 4,
        compiler_params=_params(1),
    )(act_all, dmod_cols, w, m, v)


SMALL_ROW = {"norm_gain": (0, 1024), "final_gain": (1024, 2048), "q_norm_gain": (2048, 2432),
             "kv_norm_gain": (2432, 2688), "swa_sinks": (2688, 2696), "loss": (2816, 2944)}
SMALL_ORDER = ("b_ada", "norm_gain", "q_norm_gain", "kv_norm_gain", "swa_sinks", "final_gain")


def _small_call(parts_all, n_seq, params):
    k = len(params)

    def body(p_ref, *refs):
        ins, outs, loss_ref = refs[:3 * k], refs[3 * k:7 * k], refs[7 * k]
        row = p_ref[n_seq:n_seq + 1, :]
        for dv in range(1, 8):
            r0 = dv * ROWS_PER_DEVICE + n_seq
            row = row + p_ref[r0:r0 + 1, :]
        gb = None
        for dv in range(8):
            for r in range(n_seq):
                r0 = dv * ROWS_PER_DEVICE + r
                gb = p_ref[r0:r0 + 1, :] if gb is None else gb + p_ref[r0:r0 + 1, :]
        for j, name in enumerate(SMALL_ORDER):
            g = gb if name == "b_ada" else row[:, SMALL_ROW[name][0]:SMALL_ROW[name][1]]
            d, mn, vn = _adam_math(ins[3 * j][...], g, ins[3 * j + 1][...], ins[3 * j + 2][...])
            outs[4 * j][...] = g
            outs[4 * j + 1][...] = d
            outs[4 * j + 2][...] = mn
            outs[4 * j + 3][...] = vn
        loss_ref[...] = row[:, SMALL_ROW["loss"][0]:SMALL_ROW["loss"][1]]

    flat = [t for p in params for t in p]
    res = pl.pallas_call(
        body, name="small_update", grid=(1,),
        out_shape=[jax.ShapeDtypeStruct(p[0].shape, F32) for p in params for _ in range(4)]
        + [jax.ShapeDtypeStruct((1, HEAD_LANES), F32)],
        in_specs=[_full(parts_all.shape)] + [_full(t.shape) for t in flat],
        out_specs=[_full(p[0].shape) for p in params for _ in range(4)] + [_full((1, HEAD_LANES))],
        compiler_params=_params(1),
    )(parts_all, *flat)
    return [res[4 * j:4 * j + 4] for j in range(k)], res[4 * k]


def _rot(t):
    half = t.shape[-1] // 2
    return jnp.concatenate([-t[..., half:], t[..., :half]], axis=-1)


def _rot_t(g):
    half = g.shape[-1] // 2
    return jnp.concatenate([g[..., half:], -g[..., :half]], axis=-1)


def _columns(segments, lo, hi):
    out, at = [], 0
    for seg in segments:
        n = seg.shape[1]
        a, b = max(lo, at), min(hi, at + n)
        if a < b:
            out.append(seg[:, a - at:b - at])
        at += n
    return out


def _prepare_weights(w_in_blocks, w_uq, w_ukv):
    o = [0]
    for s in IN_SPLITS:
        o.append(o[-1] + s)
    part = lambda a, b: _columns(w_in_blocks, a, b)
    kr = jnp.concatenate(part(o[2], o[3]), axis=1)
    zero = jnp.zeros((kr.shape[0], 32), kr.dtype)
    wa = jnp.concatenate(part(0, o[2]) + [_rot(kr), zero, kr, zero] + part(o[3], o[8]), axis=1)
    uq = w_uq.reshape(Q_LORA, N_HEADS, MLA_NOPE + MLA_ROPE)
    zq = jnp.zeros((Q_LORA, N_HEADS, 32), w_uq.dtype)
    uq_full = jnp.concatenate([uq, zq], axis=-1).reshape(Q_LORA, 1024)
    uq_rot = jnp.concatenate([jnp.zeros((Q_LORA, N_HEADS, 64), w_uq.dtype), _rot(uq[..., MLA_NOPE:]), zq],
                             axis=-1).reshape(Q_LORA, 1024)
    wq2 = jnp.concatenate([uq_full, uq_rot], axis=1)
    ukv = w_ukv.reshape(KV_LORA, N_HEADS, 128)
    k_full = jnp.concatenate([ukv[..., :64], jnp.zeros((KV_LORA, N_HEADS, 64), w_ukv.dtype)], axis=-1).reshape(KV_LORA, 1024)
    wkv = jnp.concatenate([k_full, ukv[..., 64:].reshape(KV_LORA, 512)], axis=1)
    return wa, wq2, wkv


def _restore_grads(gwa, gwq2, gwkv):
    gkr = gwa[:, A_KR + 64:A_KR + 96] + _rot_t(gwa[:, A_KR:A_KR + 32])
    in_order = [gwa[:, :A_KR], gkr, gwa[:, A_GM:]]
    n = D_IN // 4
    g_in = [jnp.concatenate(_columns(in_order, k * n, (k + 1) * n), axis=1) for k in range(4)]
    gf = gwq2[:, :1024].reshape(Q_LORA, N_HEADS, 128)
    gr = gwq2[:, 1024:].reshape(Q_LORA, N_HEADS, 128)
    g_uq = jnp.concatenate([gf[..., :64], gf[..., 64:96] + _rot_t(gr[..., 64:96])], axis=-1).reshape(Q_LORA, 768)
    gk = gwkv[:, :1024].reshape(KV_LORA, N_HEADS, 128)[..., :64]
    gv = gwkv[:, 1024:].reshape(KV_LORA, N_HEADS, 64)
    g_ukv = jnp.concatenate([gk, gv], axis=-1).reshape(KV_LORA, 1024)
    return g_in, g_uq, g_ukv


def _local_step(x, positions, target, mod_rows, b_ada, ng, qg, kvg, sinks, fg, w_in_b, w_uq_b, w_ukv_b, w_out_b):
    n_seq, seq, _ = x.shape
    n_tok = n_seq * seq
    x2d = x.reshape(n_tok, D_MODEL)
    t2d = target.reshape(n_tok, D_MODEL)
    pos_f = positions.astype(F32)
    pos_col = pos_f.reshape(n_tok, 1)
    pos_row = pos_f.reshape(n_tok // SWA_WINDOW, 1, SWA_WINDOW)
    mod3 = mod_rows.reshape(n_seq, 1, 3 * D_MODEL)
    inv = ROPE_THETA ** (-jnp.arange(0, MLA_ROPE, 2, dtype=F32) / MLA_ROPE)
    inv128 = jnp.concatenate([jnp.zeros((64,), F32), inv, inv, jnp.zeros((32,), F32)]).reshape(1, 128)
    fg2 = fg.reshape(1, D_MODEL)

    wa, wq2, wkv = _prepare_weights(w_in_b, w_uq_b, w_ukv_b)

    zqkv, gates, qf, kf, v, qs, kd, vd, rope = _pre_call(x2d, pos_col, mod3, b_ada, ng, qg, kvg, inv128, wa, wq2, wkv, seq)
    o_mla, lse_mla = _mla_fwd_call(qf, kf, v, n_seq, seq)
    o_swa, lse_swa = _swa_fwd_call(qs, kd, vd, pos_col, pos_row, sinks, n_seq, seq)
    dx2, do, dg, g_out, g_fg, dgate, loss = _post_call(x2d, t2d, o_mla, o_swa, gates, mod3, b_ada, fg2, w_out_b, seq)
    dqf, dkf, dv = _mla_bwd_call(qf, kf, v, do, o_mla, lse_mla, n_seq, seq)
    dqs, dkd, dvd, dsink = _swa_bwd_call(qs, kd, vd, do, o_swa, lse_swa, pos_col, pos_row, sinks, n_seq, seq)
    dz, g_wq2, g_wkv, g_qg, g_kvg = _mid_bwd_call(dqf, dkf, dv, zqkv, rope, qg, kvg, wq2, wkv, seq)
    gx, g_wa, g_ng, dshift, dscale = _in_bwd_call(x2d, dx2, dz, dg, dqs, dkd, dvd, mod3, b_ada, ng, wa, seq)
    g_in, g_uq, g_ukv = _restore_grads(g_wa, g_wq2, g_wkv)
    dmod = jnp.concatenate([dshift, dscale, dgate], axis=-1).reshape(n_seq, 3 * D_MODEL)
    small_row = jnp.concatenate([g_ng, g_fg, g_qg, g_kvg, jnp.pad(jnp.sum(dsink, axis=1).reshape(1, N_HEADS), ((0, 0), (0, 120))),
                                 loss, jnp.zeros((1, 128), F32)], axis=1)
    return gx.reshape(x.shape), (g_in, g_uq, g_ukv, g_out), small_row, dmod


def kernel(x, c, positions, w_ada, b_ada, norm_gain, w_in, q_norm_gain, kv_norm_gain, w_uq, w_ukv, swa_sinks, w_out, final_gain, loss_target, m_w_ada, m_b_ada, m_norm_gain, m_w_in, m_q_norm_gain, m_kv_norm_gain, m_w_uq, m_w_ukv, m_swa_sinks, m_w_out, m_final_gain, v_w_ada, v_b_ada, v_norm_gain, v_w_in, v_q_norm_gain, v_kv_norm_gain, v_w_uq, v_w_ukv, v_swa_sinks, v_w_out, v_final_gain):
    n_seq = x.shape[0]
    xi, yi, ci = lax.axis_index("x"), lax.axis_index("y"), lax.axis_index("c")
    dev = 4 * xi + 2 * yi + ci
    chip = 2 * xi + yi

    halves = lambda w: w.astype(BF16).reshape(2, w.shape[0] // 2, w.shape[1])
    c_blk = jnp.pad(c, ((0, ROWS_PER_DEVICE - n_seq), (0, 0)))
    act_all, pieces, f_in, f_uq, f_ukv, f_out = _comm_fwd_call(
        c_blk, w_ada[0], [halves(w_in[0]), halves(w_uq[0]), halves(w_ukv[0]), halves(w_out[0])])
    mine = lax.dynamic_slice_in_dim(pieces, dev * ROWS_PER_DEVICE, n_seq, axis=1)
    mod_rows = jnp.transpose(mine, (1, 0, 2)).reshape(n_seq, 3 * D_MODEL)
    cols = lambda t, r: jnp.transpose(t.reshape(4, r, -1), (1, 0, 2)).reshape(r, -1)
    w_in_blocks = [f_in[k].reshape(D_MODEL, -1) for k in range(4)]
    w_uq_b, w_ukv_b = cols(f_uq, Q_LORA), cols(f_ukv, KV_LORA)
    w_out_b = f_out.reshape(D_MODEL, D_MODEL)

    gx, (g_in_blocks, g_uq, g_ukv, g_out), small_row, dmod = _local_step(
        x, positions, loss_target, mod_rows, b_ada, norm_gain, q_norm_gain, kv_norm_gain, swa_sinks, final_gain,
        w_in_blocks, w_uq_b, w_ukv_b, w_out_b)

    by_owner = lambda g, n: jnp.transpose(g.reshape(g.shape[0], 4, n), (1, 0, 2)).reshape(4, 2, g.shape[0] // 2, n)
    grads = [jnp.stack(g_in_blocks).reshape(4, 2, D_MODEL // 2, -1), by_owner(g_uq, 192), by_owner(g_ukv, 256),
             g_out.reshape(4, 2, 128, D_MODEL)]
    part = jnp.concatenate([dmod, small_row, jnp.zeros((ROWS_PER_DEVICE - n_seq - 1, 3 * D_MODEL), F32)], axis=0)
    r_in, r_uq, r_ukv, r_out, parts_all = _comm_bwd_call(grads, part)
    g_in_s, g_uq_s = r_in.reshape(w_in.shape[1:]), r_uq.reshape(w_uq.shape[1:])
    g_ukv_s, g_out_s = r_ukv.reshape(w_ukv.shape[1:]), r_out.reshape(w_out.shape[1:])

    tr = lambda a: jnp.swapaxes(a[0], 0, 1)
    back = lambda ts: [jnp.swapaxes(t, 0, 1) for t in ts]
    d_in, nm_in, nv_in = back(_adam_call("adam_w_in", tr(w_in), g_in_s.T, tr(m_w_in), tr(v_w_in)))
    d_uq, nm_uq, nv_uq = back(_adam_call("adam_w_uq", tr(w_uq), g_uq_s.T, tr(m_w_uq), tr(v_w_uq)))
    d_ukv, nm_ukv, nv_ukv = _adam_call("adam_w_ukv", w_ukv[0], g_ukv_s, m_w_ukv[0], v_w_ukv[0])
    d_out, nm_out, nv_out = _adam_call("adam_w_out", w_out[0], g_out_s, m_w_out[0], v_w_out[0])
    dmod_cols = lax.dynamic_slice_in_dim(parts_all, chip * 768, 768, axis=1)
    g_ada, d_ada, nm_ada, nv_ada = _ada_bwd_call(act_all, dmod_cols, w_ada[0], m_w_ada[0], v_w_ada[0])

    row = lambda t: t.reshape(1, -1)
    small = {"b_ada": (b_ada, m_b_ada, v_b_ada), "norm_gain": (norm_gain, m_norm_gain, v_norm_gain),
             "q_norm_gain": (q_norm_gain, m_q_norm_gain, v_q_norm_gain),
             "kv_norm_gain": (kv_norm_gain, m_kv_norm_gain, v_kv_norm_gain),
             "swa_sinks": (swa_sinks, m_swa_sinks, v_swa_sinks),
             "final_gain": (row(final_gain), row(m_final_gain), row(v_final_gain))}
    res, loss_row = _small_call(parts_all, n_seq, [small[name] for name in SMALL_ORDER])
    res = dict(zip(SMALL_ORDER, res))
    res["final_gain"] = [t.reshape(-1) for t in res["final_gain"]]
    e = lambda t: t[None]
    big = {"w_ada": (e(g_ada), e(d_ada), e(nm_ada), e(nv_ada)), "w_in": (e(g_in_s), e(d_in), e(nm_in), e(nv_in)),
           "w_uq": (e(g_uq_s), e(d_uq), e(nm_uq), e(nv_uq)), "w_ukv": (e(g_ukv_s), e(d_ukv), e(nm_ukv), e(nv_ukv)),
           "w_out": (e(g_out_s), e(d_out), e(nm_out), e(nv_out))}
    order = ("w_ada", "b_ada", "norm_gain", "w_in", "q_norm_gain", "kv_norm_gain", "w_uq", "w_ukv", "swa_sinks", "w_out",
             "final_gain")
    pick = lambda kind: [(big[n] if n in big else res[n])[kind] for n in order]
    return (loss_row[0, 0], gx, *pick(0), *pick(1), *pick(2), *pick(3))
```

```python
import jax
import jax.numpy as jnp
from jax import lax
from jax.experimental import pallas as pl
from jax.experimental.pallas import tpu as pltpu

F32 = jnp.float32
BF16 = jnp.bfloat16

D_MODEL = 1024
Q_LORA = 384
KV_LORA = 256
N_HEADS = 8
MLA_NOPE = 64
MLA_ROPE = 32
HEAD_LANES = 128
HALF = 64
SWA_WINDOW = 128
EPS = 1e-6
ROPE_THETA = 10000.0
MLA_SCALE = (MLA_NOPE + MLA_ROPE) ** -0.5
LOG2E = 1.4426950408889634
LN2 = 0.6931471805599453
SWA_SCALE = 64 ** -0.5
NEG = -1e30

ADAM_LR = 0.001
ADAM_B1 = 0.9
ADAM_B2 = 0.999
ADAM_EPS = 1e-08
ADAM_WD = 0.01
ADAM_STEP = 10

A_ZQ, A_ZKV, A_KR, A_GM, A_QS, A_KS, A_VS, A_GS, A_END = 0, 384, 640, 768, 1280, 1792, 1920, 2048, 2560
IN_SPLITS = (384, 256, 32, 512, 512, 128, 128, 512)
D_IN = sum(IN_SPLITS)

TOKEN_TILE = 512
ATT_TILE = 256
VMEM_LIMIT = 56 * 1024 * 1024


def _dot(a, b):
    return jnp.dot(a, b, preferred_element_type=F32)


def _dot_nt(a, b):
    return lax.dot_general(a, b, (((1,), (1,)), ((), ())), preferred_element_type=F32)


def _dot_tn(a, b):
    return lax.dot_general(a, b, (((0,), (0,)), ((), ())), preferred_element_type=F32)


def _params(n_grid):
    return pltpu.CompilerParams(dimension_semantics=("arbitrary",) * n_grid, vmem_limit_bytes=VMEM_LIMIT)


def _full(shape):
    nd = len(shape)
    return pl.BlockSpec(shape, lambda *_: (0,) * nd, pipeline_mode=pl.Buffered(1))


def _sigmoid(g):
    return 1.0 / (1.0 + jnp.exp(-g))


SUB_TILE = 256


def _sub_tiles(tm):
    sub = min(SUB_TILE, tm)
    return [slice(s * sub, (s + 1) * sub) for s in range(tm // sub)]


MESH = pl.DeviceIdType.MESH
ROWS_PER_DEVICE = 8
VMEM_SPEC = pl.BlockSpec(memory_space=pltpu.VMEM)
ANY_SPEC = pl.BlockSpec(memory_space=pl.ANY)


def _position():
    x, y, c = lax.axis_index("x"), lax.axis_index("y"), lax.axis_index("c")
    sibling = (x, y, 1 - c)
    others = [(1 - x, y, c), (x, 1 - y, c), (1 - x, 1 - y, c)]
    return (x, y, c), 4 * x + 2 * y + c, 2 * x + y, sibling, others


def _rows_of(dev):
    return pl.ds(pl.multiple_of(dev * ROWS_PER_DEVICE, ROWS_PER_DEVICE), ROWS_PER_DEVICE)


def _all_to_all_rows(block_ref, table_ref, dev, me, send_sems, recv_sems):
    x, y, c = me
    waits = []
    for k in range(1, 8):
        peer = (1 - x if k & 4 else x, 1 - y if k & 2 else y, 1 - c if k & 1 else c)
        pltpu.make_async_remote_copy(src_ref=block_ref, dst_ref=table_ref.at[_rows_of(dev)], send_sem=send_sems.at[k - 1],
                                     recv_sem=recv_sems.at[k - 1], device_id=peer, device_id_type=MESH).start()
        waits.append(pltpu.make_async_remote_copy(
            src_ref=block_ref, dst_ref=table_ref.at[_rows_of(jnp.bitwise_xor(dev, k))], send_sem=send_sems.at[k - 1],
            recv_sem=recv_sems.at[k - 1], device_id=peer, device_id_type=MESH))
    return waits


def _comm_fwd_call(c_blk, w_ada, shards):
    n = len(shards)

    def body(c_ref, wada_ref, *refs):
        w_refs, act_ref, pieces_ref, full_refs = refs[:n], refs[n], refs[n + 1], refs[n + 2:2 * n + 2]
        c_all_ref = refs[2 * n + 2]
        c_send, c_recv, p_send, p_recv, w_send, w_recv, f_send, f_recv, loc_sem = refs[2 * n + 3:]
        me, dev, chip, sibling, others = _position()
        core = me[2]
        chip_of = [2 * p[0] + p[1] for p in others]

        local = [pltpu.make_async_copy(w_refs[i], full_refs[i].at[chip], loc_sem.at[i]) for i in range(n)]
        for cp in local:
            cp.start()

        def over_ici(i, j, src_chip):
            return pltpu.make_async_remote_copy(
                src_ref=w_refs[i].at[core], dst_ref=full_refs[i].at[src_chip, core], send_sem=w_send.at[3 * i + j],
                recv_sem=w_recv.at[3 * i + j], device_id=others[j], device_id_type=MESH)

        def to_sibling(i, j, half):
            return pltpu.make_async_remote_copy(
                src_ref=full_refs[i].at[chip_of[j], half], dst_ref=full_refs[i].at[chip_of[j], half],
                send_sem=f_send.at[3 * i + j], recv_sem=f_recv.at[3 * i + j], device_id=sibling, device_id_type=MESH)

        sent = [over_ici(i, j, chip) for i in range(n) for j in range(3)]
        for cp in sent:
            cp.start()

        c_all_ref[_rows_of(dev), :] = c_ref[...]
        c_waits = _all_to_all_rows(c_ref, c_all_ref, dev, me, c_send, c_recv)
        for cp in c_waits:
            cp.wait()
        cv = c_all_ref[...]
        act = cv * _sigmoid(cv)
        act_ref[...] = act
        pieces_ref[chip] = _dot(act.astype(BF16), wada_ref[...].astype(BF16))
        piece = lambda j, src_chip: pltpu.make_async_remote_copy(
            src_ref=pieces_ref.at[chip], dst_ref=pieces_ref.at[src_chip], send_sem=p_send.at[j], recv_sem=p_recv.at[j],
            device_id=others[j], device_id_type=MESH)
        for j in range(3):
            piece(j, chip).start()
        for j in range(3):
            piece(j, chip).wait_send()
            piece(j, chip_of[j]).wait_recv()

        for i in range(n):
            for j in range(3):
                over_ici(i, j, chip_of[j]).wait_recv()
                to_sibling(i, j, core).start()
        for i in range(n):
            for j in range(3):
                to_sibling(i, j, 1 - core).wait_recv()
                to_sibling(i, j, core).wait_send()
        for cp in sent:
            cp.wait_send()
        for cp in local:
            cp.wait()

    rows = 8 * ROWS_PER_DEVICE
    dma = pltpu.SemaphoreType.DMA
    return pl.pallas_call(
        body, name="comm_fwd",
        out_shape=[jax.ShapeDtypeStruct((rows, D_MODEL), F32), jax.ShapeDtypeStruct((4, rows, w_ada.shape[1]), F32)]
        + [jax.ShapeDtypeStruct((4,) + s.shape, s.dtype) for s in shards],
        in_specs=[VMEM_SPEC, VMEM_SPEC] + [ANY_SPEC] * n,
        out_specs=[VMEM_SPEC, VMEM_SPEC] + [ANY_SPEC] * n,
        scratch_shapes=[pltpu.VMEM((rows, D_MODEL), F32), dma((7,)), dma((7,)), dma((3,)), dma((3,)),
                        dma((3 * n,)), dma((3 * n,)), dma((3 * n,)), dma((3 * n,)), dma((n,))],
        compiler_params=pltpu.CompilerParams(vmem_limit_bytes=VMEM_LIMIT),
    )(c_blk, w_ada, *shards)


def _comm_bwd_call(grads, part):
    n = len(grads)

    def body(part_ref, *refs):
        g_refs, f_refs, parts_ref = refs[:n], refs[n:2 * n], refs[2 * n]
        scratch = refs[2 * n + 1:]
        a_refs, b_refs, p_refs, r_refs = (scratch[k * n:(k + 1) * n] for k in range(4))
        s_send, s_recv, d_send, d_recv, e_send, e_recv, h_send, h_recv, loc_sem = scratch[4 * n:]
        me, dev, chip, sibling, others = _position()
        core = me[2]
        chip_of = [2 * p[0] + p[1] for p in others]

        parts_ref[_rows_of(dev), :] = part_ref[...]
        s_waits = _all_to_all_rows(part_ref, parts_ref, dev, me, s_send, s_recv)

        mine = [pltpu.make_async_copy(g_refs[i].at[:, core], a_refs[i], loc_sem.at[i]) for i in range(n)]
        swap = [pltpu.make_async_remote_copy(src_ref=g_refs[i].at[:, 1 - core], dst_ref=b_refs[i], send_sem=d_send.at[i],
                                             recv_sem=d_recv.at[i], device_id=sibling, device_id_type=MESH) for i in range(n)]
        order = sorted(range(n), key=lambda i: g_refs[i].shape[2] * g_refs[i].shape[3])
        for i in order:
            mine[i].start()
            swap[i].start()
        cross = [pltpu.make_async_remote_copy(src_ref=p_refs[i].at[chip_of[j]], dst_ref=r_refs[i].at[j],
                                              send_sem=e_send.at[3 * i + j], recv_sem=e_recv.at[3 * i + j],
                                              device_id=others[j], device_id_type=MESH) for i in range(n) for j in range(3)]
        for i in order:
            mine[i].wait()
            swap[i].wait()
            for k in range(4):
                s = a_refs[i][k] + b_refs[i][k]
                a_refs[i][k] = s
                p_refs[i][k] = s.astype(BF16)
            for j in range(3):
                cross[3 * i + j].start()
        share = {}
        for i in order:
            for j in range(3):
                cross[3 * i + j].wait()
            f_refs[i][core] = (a_refs[i][chip] + r_refs[i][0].astype(F32) + r_refs[i][1].astype(F32)
                               + r_refs[i][2].astype(F32))
            share[i] = pltpu.make_async_remote_copy(src_ref=f_refs[i].at[core], dst_ref=f_refs[i].at[core],
                                                    send_sem=h_send.at[i], recv_sem=h_recv.at[i], device_id=sibling,
                                                    device_id_type=MESH)
            share[i].start()
        for i in range(n):
            share[i].wait_send()
            pltpu.make_async_remote_copy(src_ref=f_refs[i].at[core], dst_ref=f_refs[i].at[1 - core], send_sem=h_send.at[i],
                                         recv_sem=h_recv.at[i], device_id=sibling, device_id_type=MESH).wait_recv()
        for cp in s_waits:
            cp.wait()

    rows = 8 * ROWS_PER_DEVICE
    dma = pltpu.SemaphoreType.DMA
    quarter = [(4,) + g.shape[2:] for g in grads]
    return pl.pallas_call(
        body, name="comm_bwd",
        out_shape=[jax.ShapeDtypeStruct((2,) + g.shape[2:], F32) for g in grads]
        + [jax.ShapeDtypeStruct((rows, part.shape[1]), F32)],
        in_specs=[VMEM_SPEC] + [ANY_SPEC] * n,
        out_specs=[VMEM_SPEC] * (n + 1),
        scratch_shapes=[pltpu.VMEM(q, F32) for q in quarter] + [pltpu.VMEM(q, F32) for q in quarter]
        + [pltpu.VMEM(q, BF16) for q in quarter] + [pltpu.VMEM((3,) + q[1:], BF16) for q in quarter]
        + [dma((7,)), dma((7,)), dma((n,)), dma((n,)), dma((3 * n,)), dma((3 * n,)), dma((n,)), dma((n,)), dma((n,))],
        compiler_params=pltpu.CompilerParams(vmem_limit_bytes=VMEM_LIMIT),
    )(part, *grads)


def _twice(t):
    lo = _lane_lo()
    other = pltpu.roll(t, HALF, 1)
    return jnp.concatenate([jnp.where(lo, t, other), jnp.where(lo, other, t)], axis=1)


def _once(g):
    first, second = g[:, :HEAD_LANES], g[:, HEAD_LANES:]
    return jnp.where(_lane_lo(), first + pltpu.roll(first, HALF, 1), second + pltpu.roll(second, HALF, 1))


def _rope_tables(pos_col, inv_row):
    ang = pos_col * inv_row
    return jnp.cos(ang), jnp.sin(ang)


def _pre_call(x, pos_col, mod, b_ada, ng, qg, kvg, inv128, wa, wq2, wkv, seq):
    n_tok = x.shape[0]
    tm = min(TOKEN_TILE, seq)
    per_seq = seq // tm

    def body(x_ref, pos_ref, mod_ref, bada_ref, ng_ref, qg_ref, kvg_ref, inv_ref, wa_ref, wq_ref, wkv_ref,
             zqkv_ref, gates_ref, qf_ref, kf_ref, v_ref, qs_ref, kd_ref, vd_ref, rope_ref):
        xv = x_ref[...]
        modv = mod_ref[0] + bada_ref[...]
        shift, scale = modv[:, :D_MODEL], modv[:, D_MODEL:2 * D_MODEL]
        r1 = lax.rsqrt(jnp.mean(xv * xv, axis=-1, keepdims=True) + EPS)
        h = ((xv * r1) * ng_ref[...]) * (1.0 + scale) + shift
        hb = h.astype(BF16)
        za = _dot(hb, wa_ref[...])
        zkr = za[:, A_KR:A_GM]
        cos, sin = _rope_tables(pos_ref[...], inv_ref[...])
        rope_ref[:, :HEAD_LANES] = cos
        rope_ref[:, HEAD_LANES:] = sin
        zqkv_ref[...] = za[:, :A_KR]
        gates_ref[:, :512] = za[:, A_GM:A_QS]
        gates_ref[:, 512:] = za[:, A_GS:A_END]
        qs_ref[...] = (za[:, A_QS:A_KS] * (SWA_SCALE * LOG2E)).astype(BF16)
        kd_ref[...] = _twice(za[:, A_KS:A_VS]).astype(BF16)
        vd_ref[...] = _twice(za[:, A_VS:A_GS]).astype(BF16)
        zq, zkv = za[:, A_ZQ:A_ZKV], za[:, A_ZKV:A_KR]
        rq = lax.rsqrt(jnp.mean(zq * zq, axis=-1, keepdims=True) + EPS)
        qn = ((zq * rq) * qg_ref[...]).astype(BF16)
        qr = _dot(qn, wq_ref[...])
        cf, sf = jnp.tile(cos, (1, N_HEADS)), jnp.tile(sin, (1, N_HEADS))
        qf_ref[...] = ((qr[:, :1024] * cf + qr[:, 1024:] * sf) * (MLA_SCALE * LOG2E)).astype(BF16)
        rkv = lax.rsqrt(jnp.mean(zkv * zkv, axis=-1, keepdims=True) + EPS)
        kvn = ((zkv * rkv) * kvg_ref[...]).astype(BF16)
        kv = _dot(kvn, wkv_ref[...])
        kpe = jnp.where(_lane_lo(), 0.0, zkr * cos) + pltpu.roll(zkr, HALF, 1) * sin
        kf_ref[...] = (kv[:, :1024] + jnp.tile(kpe, (1, N_HEADS))).astype(BF16)
        v_ref[...] = kv[:, 1024:].astype(BF16)

    tok = lambda w: pl.BlockSpec((tm, w), lambda i: (i, 0))
    outs = [(640, F32), (1024, F32), (1024, BF16), (1024, BF16), (512, BF16), (512, BF16), (256, BF16), (256, BF16),
            (2 * HEAD_LANES, F32)]
    return pl.pallas_call(
        body, name="pre", grid=(n_tok // tm,),
        out_shape=[jax.ShapeDtypeStruct((n_tok, w), dt) for w, dt in outs],
        in_specs=[tok(D_MODEL), tok(1), pl.BlockSpec((1, 1, 3 * D_MODEL), lambda i: (i // per_seq, 0, 0)),
                  _full(b_ada.shape), _full(ng.shape), _full(qg.shape), _full(kvg.shape), _full(inv128.shape),
                  _full(wa.shape), _full(wq2.shape), _full(wkv.shape)],
        out_specs=[tok(w) for w, _ in outs],
        compiler_params=_params(1),
    )(x, pos_col, mod, b_ada, ng, qg, kvg, inv128, wa, wq2, wkv)


def _lane_lo(width=HEAD_LANES):
    return lax.broadcasted_iota(jnp.int32, (1, width), 1) < HALF


def _eye(n=HEAD_LANES):
    r = lax.broadcasted_iota(jnp.int32, (n, n), 0)
    c = lax.broadcasted_iota(jnp.int32, (n, n), 1)
    return jnp.where(r == c, 1.0, 0.0).astype(BF16)


def _mla_fwd_call(qf, kf, v, n_seq, seq):
    tq = min(ATT_TILE, seq)
    nq = seq // tq

    ext = HALF + 16

    def body(q_ref, k_ref, v_ref, o_ref, lse_ref, vt_ref, acc_ref):
        i = pl.program_id(1)
        eye = _eye()

        @pl.when(i == 0)
        def _():
            for h in range(N_HEADS):
                vt_ref[h * ext + HALF:(h + 1) * ext, :] = jnp.ones((16, seq), BF16)
            for t in range(nq):
                for p in range(N_HEADS // 2):
                    pair = slice(p * HEAD_LANES, (p + 1) * HEAD_LANES)
                    v_t = _dot_nt(eye, v_ref[t * tq:(t + 1) * tq, pair]).astype(BF16)
                    for hh in range(2):
                        r0 = (2 * p + hh) * ext
                        vt_ref[r0:r0 + HALF, t * tq:(t + 1) * tq] = v_t[hh * HALF:(hh + 1) * HALF, :]

        q = q_ref[...]
        qcol = i * tq + lax.broadcasted_iota(jnp.int32, (1, tq), 1)
        heads = range(N_HEADS)
        lanes = [slice(h * HEAD_LANES, (h + 1) * HEAD_LANES) for h in heads]

        def make_step(masked, n_tiles):
            def step(kt0, carry):
                tiles = range(n_tiles)
                start = pl.multiple_of(kt0 * tq, tq)
                ks = [k_ref[pl.ds(pl.multiple_of((kt0 + t) * tq, tq), tq), :] for t in tiles]
                vt = vt_ref[:, pl.ds(start, n_tiles * tq)]
                last = n_tiles - 1
                if masked:
                    keep = ((kt0 + last) * tq + lax.broadcasted_iota(jnp.int32, (tq, 1), 0)) <= qcol

                def scores(h):
                    sts = [_dot_nt(ks[t][:, lanes[h]], q[:, lanes[h]]) for t in tiles]
                    if masked:
                        sts[last] = jnp.where(keep, sts[last], NEG)
                    return sts

                def softmax(h, sts):
                    m_old = carry[h]
                    m_new = m_old
                    for st in sts:
                        m_new = jnp.maximum(m_new, jnp.max(st, axis=0, keepdims=True))
                    pt = jnp.concatenate([jnp.exp2(st - m_new).astype(BF16) for st in sts], axis=0)
                    return m_new, jnp.exp2(m_old - m_new), pt

                def values(h, alpha, pt):
                    rows = slice(h * ext, (h + 1) * ext)
                    acc_ref[rows, :] = acc_ref[rows, :] * alpha + _dot(vt[rows, :], pt)

                sts, soft, out = {0: scores(0), 1: scores(1)}, {}, {}
                for h in range(N_HEADS + 1):
                    if h + 2 < N_HEADS:
                        sts[h + 2] = scores(h + 2)
                    if h < N_HEADS:
                        soft[h] = softmax(h, sts.pop(h))
                    if h >= 1:
                        m_new, alpha, pt = soft.pop(h - 1)
                        values(h - 1, alpha, pt)
                        out[h - 1] = m_new
                return tuple(out[h] for h in heads)
            return step

        acc_ref[...] = jnp.zeros_like(acc_ref)
        init = (jnp.full((1, tq), NEG, F32),) * N_HEADS
        count = i + 1
        carry = lax.fori_loop(0, (count + 1) // 2 - 1, lambda j, c: make_step(False, 2)(2 * j, c), init)
        carry = lax.cond(count % 2 == 0, lambda c: make_step(True, 2)(i - 1, c), lambda c: make_step(True, 1)(i, c), carry)
        dens = [acc_ref[h * ext + HALF:h * ext + HALF + 1, :] for h in heads]
        acc_t = jnp.concatenate([acc_ref[h * ext:h * ext + HALF, :] * (1.0 / dens[h]) for h in heads], axis=0)
        o_ref[...] = acc_t.T
        for h in heads:
            lse_ref[0, h // 4, h % 4:h % 4 + 1, :] = carry[h] + jnp.log2(dens[h])

    n_tok = qf.shape[0]
    return pl.pallas_call(
        body, name="mla_fwd", grid=(n_seq, nq),
        out_shape=[jax.ShapeDtypeStruct((n_tok, 512), F32), jax.ShapeDtypeStruct((n_seq, 2, 4, seq), F32)],
        in_specs=[pl.BlockSpec((tq, 1024), lambda b, i: (b * nq + i, 0)),
                  pl.BlockSpec((seq, 1024), lambda b, i: (b, 0)),
                  pl.BlockSpec((seq, 512), lambda b, i: (b, 0))],
        out_specs=[pl.BlockSpec((tq, 512), lambda b, i: (b * nq + i, 0)),
                   pl.BlockSpec((1, 2, 4, tq), lambda b, i: (b, 0, 0, i))],
        scratch_shapes=[pltpu.VMEM((N_HEADS * ext, seq), BF16), pltpu.VMEM((N_HEADS * ext, tq), F32)],
        compiler_params=_params(2),
    )(qf, kf, v)


def _mla_bwd_call(qf, kf, v, do, o, lse, n_seq, seq):
    tq = min(ATT_TILE, seq)
    nq = seq // tq

    nh = 4
    heads = range(nh)
    lanes = [slice(h * HEAD_LANES, (h + 1) * HEAD_LANES) for h in heads]

    def body(q_ref, k_ref, v_ref, do_ref, o_ref, lse_ref, dq_ref, dk_ref, dv_ref,
             kt_ref, dot_ref, delta_ref, dqt_ref, dvt_ref):
        eye = _eye()
        lo = _lane_lo()
        sub_lo = lax.broadcasted_iota(jnp.int32, (HEAD_LANES, 1), 0) < HALF
        ones_lo = jnp.where(jnp.broadcast_to(lo, (8, HEAD_LANES)), 1.0, 0.0).astype(BF16)
        ones_hi = jnp.where(jnp.broadcast_to(lo, (8, HEAD_LANES)), 0.0, 1.0).astype(BF16)

        for t in range(nq):
            r = slice(t * tq, (t + 1) * tq)
            kv = k_ref[r, :]
            for h in heads:
                kt_ref[lanes[h], r] = _dot_nt(eye, kv[:, lanes[h]]).astype(BF16)
            for p in range(nh // 2):
                dov = do_ref[r, lanes[p]]
                dt = _dot_nt(eye, dov)
                dot_ref[2 * p, :, r] = jnp.where(sub_lo, dt, 0.0).astype(BF16)
                dot_ref[2 * p + 1, :, r] = jnp.where(sub_lo, 0.0, dt).astype(BF16)
                prod = dov.astype(F32) * o_ref[r, lanes[p]]
                p_hi = prod.astype(BF16)
                p_lo = (prod - p_hi.astype(F32)).astype(BF16)
                delta_ref[2 * p, :, r] = _dot_nt(ones_lo, p_hi) + _dot_nt(ones_lo, p_lo)
                delta_ref[2 * p + 1, :, r] = _dot_nt(ones_hi, p_hi) + _dot_nt(ones_hi, p_lo)
        dqt_ref[...] = jnp.zeros_like(dqt_ref)
        dvt_ref[...] = jnp.zeros_like(dvt_ref)

        def flush_dv(tile, which):
            rows = pl.ds(pl.multiple_of(tile * tq, tq), tq)
            for p in range(nh // 2):
                dv_ref[rows, lanes[p]] = dvt_ref[which, p * HEAD_LANES:(p + 1) * HEAD_LANES, :].T

        def k_step(kt, _):
            slot = kt % 2
            kr = pl.ds(pl.multiple_of(kt * tq, tq), tq)
            k = k_ref[kr, :]
            vv = v_ref[kr, :]
            k_t = kt_ref[:, kr]
            krow = kt * tq + lax.broadcasted_iota(jnp.int32, (tq, 1), 0)

            def make_step(masked, n_tiles):
                def q_step(qt0, carry):
                    tiles = range(n_tiles)
                    qrs = [pl.ds(pl.multiple_of((qt0 + t) * tq, tq), tq) for t in tiles]
                    if masked:
                        flush_dv(jnp.maximum(kt - 1, 0), 1 - slot)
                    qs = [q_ref[qr, :] for qr in qrs]
                    if masked:
                        keep = krow <= (qt0 * tq + lax.broadcasted_iota(jnp.int32, (1, tq), 1))

                    def scores(h):
                        do_ts = [dot_ref[h, :, qr] for qr in qrs]
                        sts = [_dot_nt(k[:, lanes[h]], qs[t][:, lanes[h]]) for t in tiles]
                        dpts = [_dot(vv[:, lanes[h // 2]], do_ts[t]) for t in tiles]
                        return do_ts, sts, dpts

                    def softmax(h, sts, dpts):
                        pts, dsts = [], []
                        for t in tiles:
                            pt = jnp.exp2(sts[t] - lse_ref[0, 0, h:h + 1, qrs[t]])
                            if masked and t == 0:
                                pt = jnp.where(keep, pt, 0.0)
                            dsts.append((pt * (dpts[t] - delta_ref[h, 0:1, qrs[t]])).astype(BF16))
                            pts.append(pt.astype(BF16))
                        return pts, dsts

                    def grads(h, do_ts, pts, dsts):
                        half = slice((h % 2) * HALF, (h % 2 + 1) * HALF)
                        dst_all = jnp.concatenate(dsts, axis=1)
                        pt_all = jnp.concatenate(pts, axis=1)
                        do_all = jnp.concatenate([do_ts[t][half, :] for t in tiles], axis=1)
                        q_all = jnp.concatenate([qs[t][:, lanes[h]] for t in tiles], axis=0)
                        dvt_ref[slot, h * HALF:(h + 1) * HALF, :] += _dot_nt(do_all, pt_all)
                        dk_ref[kr, lanes[h]] += _dot(dst_all, q_all)
                        for t in tiles:
                            dqt_ref[lanes[h], qrs[t]] += _dot(k_t[lanes[h], :], dsts[t])

                    first, second = {0: scores(0)}, {}
                    for h in range(nh + 1):
                        if h + 1 < nh:
                            first[h + 1] = scores(h + 1)
                        if h < nh:
                            do_ts, sts, dpts = first.pop(h)
                            second[h] = (do_ts,) + softmax(h, sts, dpts)
                        if h >= 1:
                            grads(h - 1, *second.pop(h - 1))
                    return carry
                return q_step

            dk_ref[kr, :] = jnp.zeros((tq, nh * HEAD_LANES), F32)
            dvt_ref[slot] = jnp.zeros(dvt_ref.shape[1:], F32)
            count = nq - kt
            lax.cond(count >= 2, lambda c: make_step(True, 2)(kt, c), lambda c: make_step(True, 1)(kt, c), 0)
            lax.fori_loop(1, count // 2, lambda j, c: make_step(False, 2)(kt + 2 * j, c), 0)
            lax.cond(jnp.logical_and(count % 2 == 1, count >= 3), lambda c: make_step(False, 1)(nq - 1, c), lambda c: c, 0)
            return 0

        lax.fori_loop(0, nq, k_step, 0)
        flush_dv(nq - 1, (nq - 1) % 2)
        for t in range(nq):
            r = slice(t * tq, (t + 1) * tq)
            for h in heads:
                dq_ref[r, lanes[h]] = dqt_ref[lanes[h], r].T

    n_tok = qf.shape[0]
    groups = N_HEADS // nh
    blk = lambda w: pl.BlockSpec((seq, w), lambda b, g: (b, g))
    return pl.pallas_call(
        body, name="mla_bwd", grid=(n_seq, groups),
        out_shape=[jax.ShapeDtypeStruct((n_tok, 1024), F32), jax.ShapeDtypeStruct((n_tok, 1024), F32),
                   jax.ShapeDtypeStruct((n_tok, 512), F32)],
        in_specs=[blk(512), blk(512), blk(256), blk(256), blk(256),
                  pl.BlockSpec((1, 1, nh, seq), lambda b, g: (b, g, 0, 0))],
        out_specs=[blk(512), blk(512), blk(256)],
        scratch_shapes=[pltpu.VMEM((nh * HEAD_LANES, seq), BF16), pltpu.VMEM((nh, HEAD_LANES, seq), BF16),
                        pltpu.VMEM((nh, 8, seq), F32), pltpu.VMEM((nh * HEAD_LANES, seq), F32),
                        pltpu.VMEM((2, nh * HALF, tq), F32)],
        compiler_params=_params(2),
    )(qf, kf, v, do, o, lse)


SWA_BLOCKS = 4


def _swa_block(n, pos_col_ref, posq):
    w = SWA_WINDOW
    start = pl.multiple_of(jnp.maximum(n - 1, 0) * w, w)
    posk = pos_col_ref[pl.ds(start, 2 * w), :]
    rel = (n * w + lax.broadcasted_iota(jnp.int32, (1, w), 1)) - (start + lax.broadcasted_iota(jnp.int32, (2 * w, 1), 0))
    valid = jnp.logical_and(rel >= 0, rel < w)
    return start, jnp.where(valid, posq - posk, 1e30)


def _alibi(h):
    return LOG2E * 2.0 ** -(h + 1)


def _transpose_rows(eye, src_ref, dst_ref, seq, width):
    step = 2 * SWA_WINDOW
    for t in range(seq // step):
        for p in range(width // HEAD_LANES):
            lanes = slice(p * HEAD_LANES, (p + 1) * HEAD_LANES)
            dst_ref[lanes, t * step:(t + 1) * step] = _dot_nt(eye, src_ref[t * step:(t + 1) * step, lanes]).astype(BF16)


def _swa_fwd_call(qs, kd, vd, pos_col, pos_row, sinks, n_seq, seq):
    w = SWA_WINDOW
    qb = SWA_BLOCKS
    steps = seq // (qb * w)
    ext = HALF + 16

    def body(q_ref, k_ref, v_ref, pc_ref, pr_ref, sink_ref, o_ref, lse_ref, vt_ref):
        n = pl.program_id(1)
        lo = _lane_lo()
        hi = jnp.logical_not(lo)
        eye = _eye()

        @pl.when(n == 0)
        def _():
            step = 2 * w
            for kv in range(2):
                vt_ref[kv * ext + HALF:(kv + 1) * ext, :] = jnp.ones((16, seq), BF16)
                for t in range(seq // step):
                    v_t = _dot_nt(eye, v_ref[t * step:(t + 1) * step, kv * HEAD_LANES:(kv + 1) * HEAD_LANES])
                    vt_ref[kv * ext:kv * ext + HALF, t * step:(t + 1) * step] = v_t[:HALF, :].astype(BF16)

        heads = range(N_HEADS)
        blocks = range(qb)
        geo = [_swa_block(n * qb + bi, pc_ref, pr_ref[bi]) for bi in blocks]
        wins = [pl.ds(g[0], 2 * w) for g in geo]
        kwins = [k_ref[win, :] for win in wins]
        vts = [vt_ref[:, win] for win in wins]
        sts = []
        for bi in blocks:
            q = q_ref[bi * w:(bi + 1) * w, :]
            sts.append([])
            for j in range(N_HEADS // 2):
                qp = q[:, j * HEAD_LANES:(j + 1) * HEAD_LANES]
                both = jnp.concatenate([jnp.where(lo, qp, jnp.zeros_like(qp)), jnp.where(hi, qp, jnp.zeros_like(qp))], axis=0)
                st = _dot_nt(kwins[bi][:, (j // 2) * HEAD_LANES:(j // 2 + 1) * HEAD_LANES], both)
                sts[bi] += [st[:, :w], st[:, w:]]
        ps, ms = [], []
        for bi in blocks:
            ps.append([])
            ms.append([])
            for h in heads:
                s = sts[bi][h] - _alibi(h) * geo[bi][1]
                m = jnp.maximum(jnp.max(s, axis=0, keepdims=True), sink_ref[0, h] * LOG2E)
                ps[bi].append(jnp.exp2(s - m).astype(BF16))
                ms[bi].append(m)
        for bi in blocks:
            ots = []
            for h in heads:
                pv = _dot(vts[bi][(h // 4) * ext:(h // 4 + 1) * ext, :], ps[bi][h])
                l = pv[HALF:HALF + 1, :] + jnp.exp2(sink_ref[0, h] * LOG2E - ms[bi][h])
                ots.append(pv[:HALF, :] * (1.0 / l))
                lse_ref[0, h:h + 1, bi * w:(bi + 1) * w] = ms[bi][h] + jnp.log2(l)
            o_ref[bi * w:(bi + 1) * w, :] = jnp.concatenate(ots, axis=0).T

    n_tok = qs.shape[0]
    tok = lambda width: pl.BlockSpec((qb * w, width), lambda b, n: (b * steps + n, 0))
    whole = lambda width: pl.BlockSpec((seq, width), lambda b, n: (b, 0))
    return pl.pallas_call(
        body, name="swa_fwd", grid=(n_seq, steps),
        out_shape=[jax.ShapeDtypeStruct((n_tok, 512), F32), jax.ShapeDtypeStruct((n_seq, N_HEADS, seq), F32)],
        in_specs=[tok(512), whole(256), whole(256), whole(1), pl.BlockSpec((qb, 1, w), lambda b, n: (b * steps + n, 0, 0)),
                  pl.BlockSpec(memory_space=pltpu.SMEM)],
        out_specs=[tok(512), pl.BlockSpec((1, N_HEADS, qb * w), lambda b, n: (b, 0, n))],
        scratch_shapes=[pltpu.VMEM((2 * ext, seq), BF16)],
        compiler_params=_params(2),
    )(qs, kd, vd, pos_col, pos_row, sinks)


def _swa_bwd_call(qs, kd, vd, do, o, lse, pos_col, pos_row, sinks, n_seq, seq):
    w = SWA_WINDOW
    qb = SWA_BLOCKS
    steps = seq // (qb * w)

    def body(q_ref, k_ref, v_ref, do_ref, o_ref, lse_ref, pc_ref, pr_ref, sink_ref, dq_ref, dk_ref, dv_ref, dsink_ref,
             kt_ref):
        b, n = pl.program_id(0), pl.program_id(1)
        lo = _lane_lo()
        hi = jnp.logical_not(lo)
        sub_lo = lax.broadcasted_iota(jnp.int32, (HEAD_LANES, 1), 0) < HALF
        eye = _eye()
        ones_lo = jnp.where(jnp.broadcast_to(lo, (8, HEAD_LANES)), 1.0, 0.0).astype(BF16)
        ones_hi = jnp.where(jnp.broadcast_to(lo, (8, HEAD_LANES)), 0.0, 1.0).astype(BF16)

        @pl.when(n == 0)
        def _():
            dk_ref[...] = jnp.zeros_like(dk_ref)
            dv_ref[...] = jnp.zeros_like(dv_ref)
            _transpose_rows(eye, k_ref, kt_ref, seq, 2 * HEAD_LANES)

        @pl.when(jnp.logical_and(n == 0, b == 0))
        def _():
            dsink_ref[...] = jnp.zeros_like(dsink_ref)

        heads = range(N_HEADS)
        blocks = range(qb)
        kv_lanes = lambda h: slice((h // 4) * HEAD_LANES, (h // 4 + 1) * HEAD_LANES)
        geo = [_swa_block(n * qb + bi, pc_ref, pr_ref[bi]) for bi in blocks]
        wins = [pl.ds(g[0], 2 * w) for g in geo]
        kwins = [k_ref[win, :] for win in wins]
        vwins = [v_ref[win, :] for win in wins]

        do_ts, deltas, qms, doms = [], [], [], []
        for bi in blocks:
            rows = slice(bi * w, (bi + 1) * w)
            for lst in (do_ts, deltas, qms, doms):
                lst.append([])
            for j in range(N_HEADS // 2):
                pair = slice(j * HEAD_LANES, (j + 1) * HEAD_LANES)
                dop = do_ref[rows, pair]
                qp = q_ref[rows, pair]
                dt = _dot_nt(eye, dop)
                prod = dop.astype(F32) * o_ref[rows, pair]
                p_hi = prod.astype(BF16)
                p_lo = (prod - p_hi.astype(F32)).astype(BF16)
                for hh in range(2):
                    half, ones = (lo, ones_lo) if hh == 0 else (hi, ones_hi)
                    do_ts[bi].append(jnp.where(sub_lo, dt, 0.0).astype(BF16) if hh == 0
                                     else jnp.where(sub_lo, 0.0, dt).astype(BF16))
                    deltas[bi].append((_dot_nt(ones, p_hi) + _dot_nt(ones, p_lo))[0:1, :])
                    qms[bi].append(jnp.where(half, qp, jnp.zeros_like(qp)))
                    doms[bi].append(jnp.where(half, dop, jnp.zeros_like(dop)))
        sts, dpts = [], []
        for bi in blocks:
            sts.append([])
            dpts.append([])
            for j in range(N_HEADS // 2):
                a, b = 2 * j, 2 * j + 1
                st = _dot_nt(kwins[bi][:, kv_lanes(a)], jnp.concatenate([qms[bi][a], qms[bi][b]], axis=0))
                dpt = _dot(vwins[bi][:, kv_lanes(a)], jnp.concatenate([do_ts[bi][a], do_ts[bi][b]], axis=1))
                sts[bi] += [st[:, :w], st[:, w:]]
                dpts[bi] += [dpt[:, :w], dpt[:, w:]]
        pts, dsts = [], []
        for bi in blocks:
            pts.append([])
            dsts.append([])
            for h in heads:
                lse_h = lse_ref[0, h:h + 1, bi * w:(bi + 1) * w]
                pt = jnp.exp2(sts[bi][h] - _alibi(h) * geo[bi][1] - lse_h)
                dsts[bi].append((pt * (dpts[bi][h] - deltas[bi][h])).astype(BF16))
                pts[bi].append(pt.astype(BF16))
                dsink_ref[h:h + 1, :] += -jnp.exp2(sink_ref[0, h] * LOG2E - lse_h) * deltas[bi][h]
        for bi in blocks:
            for kv in range(2):
                group = range(4 * kv, 4 * kv + 4)
                dst_all = jnp.concatenate([dsts[bi][h] for h in group], axis=1)
                pt_all = jnp.concatenate([pts[bi][h] for h in group], axis=1)
                q_all = jnp.concatenate([qms[bi][h] for h in group], axis=0)
                do_all = jnp.concatenate([doms[bi][h] for h in group], axis=0)
                dk_ref[wins[bi], kv_lanes(4 * kv)] += _dot(dst_all, q_all)
                dv_ref[wins[bi], kv_lanes(4 * kv)] += _dot(pt_all, do_all)
        for bi in blocks:
            ktw = kt_ref[:, wins[bi]]
            for j in range(N_HEADS // 2):
                k_t = ktw[kv_lanes(2 * j), :]
                both = _dot(k_t, jnp.concatenate([dsts[bi][2 * j], dsts[bi][2 * j + 1]], axis=1))
                dq_t = jnp.where(sub_lo, both[:, :w], both[:, w:])
                dq_ref[bi * w:(bi + 1) * w, j * HEAD_LANES:(j + 1) * HEAD_LANES] = dq_t.T * SWA_SCALE

    n_tok = qs.shape[0]
    tok = lambda width: pl.BlockSpec((qb * w, width), lambda b, n: (b * steps + n, 0))
    whole = lambda width: pl.BlockSpec((seq, width), lambda b, n: (b, 0))
    return pl.pallas_call(
        body, name="swa_bwd", grid=(n_seq, steps),
        out_shape=[jax.ShapeDtypeStruct((n_tok, 512), F32), jax.ShapeDtypeStruct((n_tok, 256), F32),
                   jax.ShapeDtypeStruct((n_tok, 256), F32), jax.ShapeDtypeStruct((N_HEADS, HEAD_LANES), F32)],
        in_specs=[tok(512), whole(256), whole(256), pl.BlockSpec((qb * w, 512), lambda b, n: (b * steps + n, 1)), tok(512),
                  pl.BlockSpec((1, N_HEADS, qb * w), lambda b, n: (b, 0, n)),
                  whole(1), pl.BlockSpec((qb, 1, w), lambda b, n: (b * steps + n, 0, 0)),
                  pl.BlockSpec(memory_space=pltpu.SMEM)],
        out_specs=[tok(512), whole(256), whole(256), _full((N_HEADS, HEAD_LANES))],
        scratch_shapes=[pltpu.VMEM((2 * HEAD_LANES, seq), BF16)],
        compiler_params=_params(2),
    )(qs, kd, vd, do, o, lse, pos_col, pos_row, sinks)


def _post_call(x, target, o_mla, o_swa, gates, mod, b_ada, fg, w_out, seq):
    n_tok = x.shape[0]
    tm = min(TOKEN_TILE, seq)
    per_seq = seq // tm
    n_seq = n_tok // seq

    def body(x_ref, t_ref, om_ref, os_ref, g_ref, mod_ref, bada_ref, fg_ref, w_ref,
             dx2_ref, do_ref, dg_ref, gw_ref, gfg_ref, dgate_ref, loss_ref):
        i = pl.program_id(0)

        @pl.when(i == 0)
        def _():
            gw_ref[...] = jnp.zeros_like(gw_ref)
            gfg_ref[...] = jnp.zeros_like(gfg_ref)
            loss_ref[...] = jnp.zeros_like(loss_ref)

        @pl.when(i % per_seq == 0)
        def _():
            dgate_ref[...] = jnp.zeros_like(dgate_ref)

        gate = mod_ref[0][:, 2 * D_MODEL:] + bada_ref[:, 2 * D_MODEL:]
        fgv = fg_ref[...]
        subs = _sub_tiles(tm)
        gs = [g_ref[r, :] for r in subs]
        os_ = [jnp.concatenate([om_ref[r, :], os_ref[r, :]], axis=-1) for r in subs]
        sgs = [_sigmoid(g) for g in gs]
        sils = [g * sg for g, sg in zip(gs, sgs)]
        ypres = [(o * sil).astype(BF16) for o, sil in zip(os_, sils)]
        ys = [_dot(ypre, w_ref[...]) for ypre in ypres]
        dys, loss, gfg, dgate = [], 0.0, 0.0, 0.0
        for r, y in zip(subs, ys):
            x2 = x_ref[r, :] + gate * y
            r2 = lax.rsqrt(jnp.mean(x2 * x2, axis=-1, keepdims=True) + EPS)
            xn2 = x2 * r2
            err = xn2 * fgv - t_ref[r, :]
            loss = loss + jnp.sum(jnp.sum(err * err, axis=-1, keepdims=True), axis=0, keepdims=True)
            dout = err * (1.0 / D_MODEL)
            gfg = gfg + jnp.sum(dout * xn2, axis=0, keepdims=True)
            dxn2 = dout * fgv
            dx2 = r2 * (dxn2 - xn2 * jnp.mean(dxn2 * xn2, axis=-1, keepdims=True))
            dx2_ref[r, :] = dx2
            dgate = dgate + jnp.sum(dx2 * y, axis=0, keepdims=True)
            dys.append((dx2 * gate).astype(BF16))
        loss_ref[...] += jnp.broadcast_to(loss * (0.5 / D_MODEL), loss_ref.shape)
        gfg_ref[...] += gfg
        dgate_ref[0] += dgate
        gw_ref[...] += _dot_tn(jnp.concatenate(ypres, axis=0), jnp.concatenate(dys, axis=0))
        dypres = [_dot_nt(dy, w_ref[...]) for dy in dys]
        for r, dypre, o, g, sg, sil in zip(subs, dypres, os_, gs, sgs, sils):
            do_ref[r, :] = (dypre * sil).astype(BF16)
            dg_ref[r, :] = (dypre * o * (sg * (1.0 + g * (1.0 - sg)))).astype(BF16)

    tok = lambda w: pl.BlockSpec((tm, w), lambda i: (i, 0))
    per_b = pl.BlockSpec((1, 1, 3 * D_MODEL), lambda i: (i // per_seq, 0, 0))
    return pl.pallas_call(
        body, name="post", grid=(n_tok // tm,),
        out_shape=[jax.ShapeDtypeStruct((n_tok, D_MODEL), F32), jax.ShapeDtypeStruct((n_tok, D_MODEL), BF16),
                   jax.ShapeDtypeStruct((n_tok, D_MODEL), BF16), jax.ShapeDtypeStruct((D_MODEL, D_MODEL), F32),
                   jax.ShapeDtypeStruct((1, D_MODEL), F32), jax.ShapeDtypeStruct((n_seq, 1, D_MODEL), F32),
                   jax.ShapeDtypeStruct((1, HEAD_LANES), F32)],
        in_specs=[tok(D_MODEL), tok(D_MODEL), tok(512), tok(512), tok(D_MODEL), per_b, _full(b_ada.shape),
                  _full(fg.shape), _full(w_out.shape)],
        out_specs=[tok(D_MODEL), tok(D_MODEL), tok(D_MODEL), _full((D_MODEL, D_MODEL)), _full((1, D_MODEL)),
                   pl.BlockSpec((1, 1, D_MODEL), lambda i: (i // per_seq, 0, 0)), _full((1, HEAD_LANES))],
        compiler_params=_params(1),
    )(x, target, o_mla, o_swa, gates, mod, b_ada, fg, w_out)


def _mid_bwd_call(dqf, dkf, dv, zqkv, rope, qg, kvg, wq2, wkv, seq):
    n_tok = dqf.shape[0]
    tm = min(TOKEN_TILE, seq)

    def body(dq_ref, dk_ref, dv_ref, z_ref, rope_ref, qg_ref, kvg_ref, wq_ref, wkv_ref,
             dz_ref, gwq_ref, gwkv_ref, gqg_ref, gkvg_ref):
        i = pl.program_id(0)

        @pl.when(i == 0)
        def _():
            gwq_ref[...] = jnp.zeros_like(gwq_ref)
            gwkv_ref[...] = jnp.zeros_like(gwkv_ref)
            gqg_ref[...] = jnp.zeros_like(gqg_ref)
            gkvg_ref[...] = jnp.zeros_like(gkvg_ref)

        cos, sin = rope_ref[:, :HEAD_LANES], rope_ref[:, HEAD_LANES:]
        cf, sf = jnp.tile(cos, (1, N_HEADS)), jnp.tile(sin, (1, N_HEADS))
        dq = dq_ref[...] * MLA_SCALE
        dqr = jnp.concatenate([dq * cf, dq * sf], axis=-1).astype(BF16)
        zq, zkv = z_ref[:, :Q_LORA], z_ref[:, Q_LORA:]
        qgv, kvgv = qg_ref[...], kvg_ref[...]

        rq = lax.rsqrt(jnp.mean(zq * zq, axis=-1, keepdims=True) + EPS)
        xq = zq * rq
        gwq_ref[...] += _dot_tn((xq * qgv).astype(BF16), dqr)
        dqn = _dot_nt(dqr, wq_ref[...])
        gqg_ref[...] += jnp.sum(dqn * xq, axis=0, keepdims=True)
        dxq = dqn * qgv
        dz_ref[:, :Q_LORA] = (rq * (dxq - xq * jnp.mean(dxq * xq, axis=-1, keepdims=True))).astype(BF16)

        dk = dk_ref[...] * LN2
        dkv = jnp.concatenate([dk, dv_ref[...]], axis=-1).astype(BF16)
        rkv = lax.rsqrt(jnp.mean(zkv * zkv, axis=-1, keepdims=True) + EPS)
        xkv = zkv * rkv
        gwkv_ref[...] += _dot_tn((xkv * kvgv).astype(BF16), dkv)
        dkvn = _dot_nt(dkv, wkv_ref[...])
        gkvg_ref[...] += jnp.sum(dkvn * xkv, axis=0, keepdims=True)
        dxkv = dkvn * kvgv
        dz_ref[:, Q_LORA:A_KR] = (rkv * (dxkv - xkv * jnp.mean(dxkv * xkv, axis=-1, keepdims=True))).astype(BF16)

        dkpe = dk[:, :HEAD_LANES]
        for h in range(1, N_HEADS):
            dkpe = dkpe + dk[:, h * HEAD_LANES:(h + 1) * HEAD_LANES]
        dz_ref[:, A_KR:] = (jnp.where(_lane_lo(), 0.0, dkpe * cos) + pltpu.roll(dkpe * sin, HALF, 1)).astype(BF16)

    tok = lambda w: pl.BlockSpec((tm, w), lambda i: (i, 0))
    return pl.pallas_call(
        body, name="mid_bwd", grid=(n_tok // tm,),
        out_shape=[jax.ShapeDtypeStruct((n_tok, A_GM), BF16),
                   jax.ShapeDtypeStruct(wq2.shape, F32), jax.ShapeDtypeStruct(wkv.shape, F32),
                   jax.ShapeDtypeStruct((1, Q_LORA), F32), jax.ShapeDtypeStruct((1, KV_LORA), F32)],
        in_specs=[tok(1024), tok(1024), tok(512), tok(640), tok(2 * HEAD_LANES), _full(qg.shape), _full(kvg.shape),
                  _full(wq2.shape), _full(wkv.shape)],
        out_specs=[tok(A_GM), _full(wq2.shape), _full(wkv.shape), _full((1, Q_LORA)), _full((1, KV_LORA))],
        compiler_params=_params(1),
    )(dqf, dkf, dv, zqkv, rope, qg, kvg, wq2, wkv)


def _in_bwd_call(x, dx2, dz, dg, dqs, dkd, dvd, mod, b_ada, ng, wa, seq):
    n_tok = x.shape[0]
    tm = min(TOKEN_TILE, seq)
    per_seq = seq // tm
    n_seq = n_tok // seq

    def body(x_ref, dx2_ref, dz_ref, dg_ref, dqs_ref, dkd_ref, dvd_ref, mod_ref, bada_ref, ng_ref,
             wa_ref, gx_ref, gwa_ref, gng_ref, dshift_ref, dscale_ref):
        i = pl.program_id(0)

        @pl.when(i == 0)
        def _():
            gwa_ref[...] = jnp.zeros_like(gwa_ref)
            gng_ref[...] = jnp.zeros_like(gng_ref)

        @pl.when(i % per_seq == 0)
        def _():
            dshift_ref[...] = jnp.zeros_like(dshift_ref)
            dscale_ref[...] = jnp.zeros_like(dscale_ref)

        xv = x_ref[...]
        modv = mod_ref[0] + bada_ref[...]
        shift, scale = modv[:, :D_MODEL], modv[:, D_MODEL:2 * D_MODEL]
        ngv = ng_ref[...]
        r1 = lax.rsqrt(jnp.mean(xv * xv, axis=-1, keepdims=True) + EPS)
        xn = xv * r1
        hb = ((xn * ngv) * (1.0 + scale) + shift).astype(BF16)

        dgv = dg_ref[...]
        pieces = [(A_ZQ, dz_ref[...]), (A_GM, dgv[:, :512]), (A_QS, dqs_ref[...].astype(BF16)),
                  (A_KS, jnp.concatenate([_once(dkd_ref[...]) * LN2, _once(dvd_ref[...])], axis=1).astype(BF16)),
                  (A_GS, dgv[:, 512:])]
        dh = None
        for off, piece in pieces:
            wd = piece.shape[1]
            gwa_ref[:, off:off + wd] += _dot_tn(hb, piece)
            term = _dot_nt(piece, wa_ref[:, off:off + wd])
            dh = term if dh is None else dh + term

        dshift_ref[0] += jnp.sum(dh, axis=0, keepdims=True)
        dscale_ref[0] += jnp.sum(dh * (xn * ngv), axis=0, keepdims=True)
        gng_ref[...] += jnp.sum(dh * xn * (1.0 + scale), axis=0, keepdims=True)
        dxn = dh * ngv * (1.0 + scale)
        gx_ref[...] = dx2_ref[...] + r1 * (dxn - xn * jnp.mean(dxn * xn, axis=-1, keepdims=True))

    tok = lambda w: pl.BlockSpec((tm, w), lambda i: (i, 0))
    per_b = lambda w: pl.BlockSpec((1, 1, w), lambda i: (i // per_seq, 0, 0))
    return pl.pallas_call(
        body, name="in_bwd", grid=(n_tok // tm,),
        out_shape=[jax.ShapeDtypeStruct((n_tok, D_MODEL), F32), jax.ShapeDtypeStruct((D_MODEL, A_END), F32),
                   jax.ShapeDtypeStruct((1, D_MODEL), F32),
                   jax.ShapeDtypeStruct((n_seq, 1, D_MODEL), F32), jax.ShapeDtypeStruct((n_seq, 1, D_MODEL), F32)],
        in_specs=[tok(D_MODEL), tok(D_MODEL), tok(A_GM), tok(D_MODEL), tok(512), tok(256), tok(256),
                  per_b(3 * D_MODEL), _full(b_ada.shape), _full(ng.shape), _full(wa.shape)],
        out_specs=[tok(D_MODEL), _full((D_MODEL, A_END)), _full((1, D_MODEL)), per_b(D_MODEL), per_b(D_MODEL)],
        compiler_params=_params(1),
    )(x, dx2, dz, dg, dqs, dkd, dvd, mod, b_ada, ng, wa)


def _adam_math(w, g, m, v):
    m_new = ADAM_B1 * m + (1.0 - ADAM_B1) * g
    v_new = ADAM_B2 * v + (1.0 - ADAM_B2) * (g * g)
    m_hat = m_new / (1.0 - ADAM_B1 ** ADAM_STEP)
    v_hat = v_new / (1.0 - ADAM_B2 ** ADAM_STEP)
    delta = -ADAM_LR * (m_hat / (jnp.sqrt(v_hat) + ADAM_EPS) + ADAM_WD * w)
    return delta, m_new, v_new


def _adam_call(name, w, g, m, v):
    rows, cols = w.shape
    tr = next((t for t in (256, 128, 88) if rows % t == 0), rows)

    def body(w_ref, g_ref, m_ref, v_ref, d_ref, mo_ref, vo_ref):
        d, mn, vn = _adam_math(w_ref[...], g_ref[...], m_ref[...], v_ref[...])
        d_ref[...] = d
        mo_ref[...] = mn
        vo_ref[...] = vn

    spec = pl.BlockSpec((tr, cols), lambda i: (i, 0))
    return pl.pallas_call(
        body, name=name, grid=(rows // tr,),
        out_shape=[jax.ShapeDtypeStruct(w.shape, F32)] * 3,
        in_specs=[spec] * 4, out_specs=[spec] * 3,
        compiler_params=_params(1),
    )(w, g, m, v)


def _ada_bwd_call(act_all, dmod_cols, w, m, v):
    rows, cols = w.shape
    tr = 256

    def body(a_ref, dm_ref, w_ref, m_ref, v_ref, g_ref, d_ref, mo_ref, vo_ref):
        g = _dot_tn(a_ref[...].astype(BF16), dm_ref[...].astype(BF16))
        d, mn, vn = _adam_math(w_ref[...], g, m_ref[...], v_ref[...])
        g_ref[...] = g
        d_ref[...] = d
        mo_ref[...] = mn
        vo_ref[...] = vn

    spec = pl.BlockSpec((tr, cols), lambda i: (i, 0))
    nb = act_all.shape[0]
    return pl.pallas_call(
        body, name="ada_bwd", grid=(rows // tr,),
        out_shape=[jax.ShapeDtypeStruct(w.shape, F32)] * 4,
        in_specs=[pl.BlockSpec((nb, tr), lambda i: (0, i)), _full(dmod_cols.shape), spec, spec, spec],
        out_specs=[spec] * 4,
        compiler_params=_params(1),
    )(act_all, dmod_cols, w, m, v)


SMALL_ROW = {"norm_gain": (0, 1024), "final_gain": (1024, 2048), "q_norm_gain": (2048, 2432),
             "kv_norm_gain": (2432, 2688), "swa_sinks": (2688, 2696), "loss": (2816, 2944)}
SMALL_ORDER = ("b_ada", "norm_gain", "q_norm_gain", "kv_norm_gain", "swa_sinks", "final_gain")


def _small_call(parts_all, n_seq, params):
    k = len(params)

    def body(p_ref, *refs):
        ins, outs, loss_ref = refs[:3 * k], refs[3 * k:7 * k], refs[7 * k]
        row = p_ref[n_seq:n_seq + 1, :]
        for dv in range(1, 8):
            r0 = dv * ROWS_PER_DEVICE + n_seq
            row = row + p_ref[r0:r0 + 1, :]
        gb = None
        for dv in range(8):
            for r in range(n_seq):
                r0 = dv * ROWS_PER_DEVICE + r
                gb = p_ref[r0:r0 + 1, :] if gb is None else gb + p_ref[r0:r0 + 1, :]
        for j, name in enumerate(SMALL_ORDER):
            g = gb if name == "b_ada" else row[:, SMALL_ROW[name][0]:SMALL_ROW[name][1]]
            d, mn, vn = _adam_math(ins[3 * j][...], g, ins[3 * j + 1][...], ins[3 * j + 2][...])
            outs[4 * j][...] = g
            outs[4 * j + 1][...] = d
            outs[4 * j + 2][...] = mn
            outs[4 * j + 3][...] = vn
        loss_ref[...] = row[:, SMALL_ROW["loss"][0]:SMALL_ROW["loss"][1]]

    flat = [t for p in params for t in p]
    res = pl.pallas_call(
        body, name="small_update", grid=(1,),
        out_shape=[jax.ShapeDtypeStruct(p[0].shape, F32) for p in params for _ in range(4)]
        + [jax.ShapeDtypeStruct((1, HEAD_LANES), F32)],
        in_specs=[_full(parts_all.shape)] + [_full(t.shape) for t in flat],
        out_specs=[_full(p[0].shape) for p in params for _ in range(4)] + [_full((1, HEAD_LANES))],
        compiler_params=_params(1),
    )(parts_all, *flat)
    return [res[4 * j:4 * j + 4] for j in range(k)], res[4 * k]


def _rot(t):
    half = t.shape[-1] // 2
    return jnp.concatenate([-t[..., half:], t[..., :half]], axis=-1)


def _rot_t(g):
    half = g.shape[-1] // 2
    return jnp.concatenate([g[..., half:], -g[..., :half]], axis=-1)


def _columns(segments, lo, hi):
    out, at = [], 0
    for seg in segments:
        n = seg.shape[1]
        a, b = max(lo, at), min(hi, at + n)
        if a < b:
            out.append(seg[:, a - at:b - at])
        at += n
    return out


def _prepare_weights(w_in_blocks, w_uq, w_ukv):
    o = [0]
    for s in IN_SPLITS:
        o.append(o[-1] + s)
    part = lambda a, b: _columns(w_in_blocks, a, b)
    kr = jnp.concatenate(part(o[2], o[3]), axis=1)
    zero = jnp.zeros((kr.shape[0], 32), kr.dtype)
    wa = jnp.concatenate(part(0, o[2]) + [_rot(kr), zero, kr, zero] + part(o[3], o[8]), axis=1)
    uq = w_uq.reshape(Q_LORA, N_HEADS, MLA_NOPE + MLA_ROPE)
    zq = jnp.zeros((Q_LORA, N_HEADS, 32), w_uq.dtype)
    uq_full = jnp.concatenate([uq, zq], axis=-1).reshape(Q_LORA, 1024)
    uq_rot = jnp.concatenate([jnp.zeros((Q_LORA, N_HEADS, 64), w_uq.dtype), _rot(uq[..., MLA_NOPE:]), zq],
                             axis=-1).reshape(Q_LORA, 1024)
    wq2 = jnp.concatenate([uq_full, uq_rot], axis=1)
    ukv = w_ukv.reshape(KV_LORA, N_HEADS, 128)
    k_full = jnp.concatenate([ukv[..., :64], jnp.zeros((KV_LORA, N_HEADS, 64), w_ukv.dtype)], axis=-1).reshape(KV_LORA, 1024)
    wkv = jnp.concatenate([k_full, ukv[..., 64:].reshape(KV_LORA, 512)], axis=1)
    return wa, wq2, wkv


def _restore_grads(gwa, gwq2, gwkv):
    gkr = gwa[:, A_KR + 64:A_KR + 96] + _rot_t(gwa[:, A_KR:A_KR + 32])
    in_order = [gwa[:, :A_KR], gkr, gwa[:, A_GM:]]
    n = D_IN // 4
    g_in = [jnp.concatenate(_columns(in_order, k * n, (k + 1) * n), axis=1) for k in range(4)]
    gf = gwq2[:, :1024].reshape(Q_LORA, N_HEADS, 128)
    gr = gwq2[:, 1024:].reshape(Q_LORA, N_HEADS, 128)
    g_uq = jnp.concatenate([gf[..., :64], gf[..., 64:96] + _rot_t(gr[..., 64:96])], axis=-1).reshape(Q_LORA, 768)
    gk = gwkv[:, :1024].reshape(KV_LORA, N_HEADS, 128)[..., :64]
    gv = gwkv[:, 1024:].reshape(KV_LORA, N_HEADS, 64)
    g_ukv = jnp.concatenate([gk, gv], axis=-1).reshape(KV_LORA, 1024)
    return g_in, g_uq, g_ukv


def _local_step(x, positions, target, mod_rows, b_ada, ng, qg, kvg, sinks, fg, w_in_b, w_uq_b, w_ukv_b, w_out_b):
    n_seq, seq, _ = x.shape
    n_tok = n_seq * seq
    x2d = x.reshape(n_tok, D_MODEL)
    t2d = target.reshape(n_tok, D_MODEL)
    pos_f = positions.astype(F32)
    pos_col = pos_f.reshape(n_tok, 1)
    pos_row = pos_f.reshape(n_tok // SWA_WINDOW, 1, SWA_WINDOW)
    mod3 = mod_rows.reshape(n_seq, 1, 3 * D_MODEL)
    inv = ROPE_THETA ** (-jnp.arange(0, MLA_ROPE, 2, dtype=F32) / MLA_ROPE)
    inv128 = jnp.concatenate([jnp.zeros((64,), F32), inv, inv, jnp.zeros((32,), F32)]).reshape(1, 128)
    fg2 = fg.reshape(1, D_MODEL)

    wa, wq2, wkv = _prepare_weights(w_in_b, w_uq_b, w_ukv_b)

    zqkv, gates, qf, kf, v, qs, kd, vd, rope = _pre_call(x2d, pos_col, mod3, b_ada, ng, qg, kvg, inv128, wa, wq2, wkv, seq)
    o_mla, lse_mla = _mla_fwd_call(qf, kf, v, n_seq, seq)
    o_swa, lse_swa = _swa_fwd_call(qs, kd, vd, pos_col, pos_row, sinks, n_seq, seq)
    dx2, do, dg, g_out, g_fg, dgate, loss = _post_call(x2d, t2d, o_mla, o_swa, gates, mod3, b_ada, fg2, w_out_b, seq)
    dqf, dkf, dv = _mla_bwd_call(qf, kf, v, do, o_mla, lse_mla, n_seq, seq)
    dqs, dkd, dvd, dsink = _swa_bwd_call(qs, kd, vd, do, o_swa, lse_swa, pos_col, pos_row, sinks, n_seq, seq)
    dz, g_wq2, g_wkv, g_qg, g_kvg = _mid_bwd_call(dqf, dkf, dv, zqkv, rope, qg, kvg, wq2, wkv, seq)
    gx, g_wa, g_ng, dshift, dscale = _in_bwd_call(x2d, dx2, dz, dg, dqs, dkd, dvd, mod3, b_ada, ng, wa, seq)
    g_in, g_uq, g_ukv = _restore_grads(g_wa, g_wq2, g_wkv)
    dmod = jnp.concatenate([dshift, dscale, dgate], axis=-1).reshape(n_seq, 3 * D_MODEL)
    small_row = jnp.concatenate([g_ng, g_fg, g_qg, g_kvg, jnp.pad(jnp.sum(dsink, axis=1).reshape(1, N_HEADS), ((0, 0), (0, 120))),
                                 loss, jnp.zeros((1, 128), F32)], axis=1)
    return gx.reshape(x.shape), (g_in, g_uq, g_ukv, g_out), small_row, dmod


def kernel(x, c, positions, w_ada, b_ada, norm_gain, w_in, q_norm_gain, kv_norm_gain, w_uq, w_ukv, swa_sinks, w_out, final_gain, loss_target, m_w_ada, m_b_ada, m_norm_gain, m_w_in, m_q_norm_gain, m_kv_norm_gain, m_w_uq, m_w_ukv, m_swa_sinks, m_w_out, m_final_gain, v_w_ada, v_b_ada, v_norm_gain, v_w_in, v_q_norm_gain, v_kv_norm_gain, v_w_uq, v_w_ukv, v_swa_sinks, v_w_out, v_final_gain):
    n_seq = x.shape[0]
    xi, yi, ci = lax.axis_index("x"), lax.axis_index("y"), lax.axis_index("c")
    dev = 4 * xi + 2 * yi + ci
    chip = 2 * xi + yi

    halves = lambda w: w.astype(BF16).reshape(2, w.shape[0] // 2, w.shape[1])
    c_blk = jnp.pad(c, ((0, ROWS_PER_DEVICE - n_seq), (0, 0)))
    act_all, pieces, f_in, f_uq, f_ukv, f_out = _comm_fwd_call(
        c_blk, w_ada[0], [halves(w_in[0]), halves(w_uq[0]), halves(w_ukv[0]), halves(w_out[0])])
    mine = lax.dynamic_slice_in_dim(pieces, dev * ROWS_PER_DEVICE, n_seq, axis=1)
    mod_rows = jnp.transpose(mine, (1, 0, 2)).reshape(n_seq, 3 * D_MODEL)
    cols = lambda t, r: jnp.transpose(t.reshape(4, r, -1), (1, 0, 2)).reshape(r, -1)
    w_in_blocks = [f_in[k].reshape(D_MODEL, -1) for k in range(4)]
    w_uq_b, w_ukv_b = cols(f_uq, Q_LORA), cols(f_ukv, KV_LORA)
    w_out_b = f_out.reshape(D_MODEL, D_MODEL)

    gx, (g_in_blocks, g_uq, g_ukv, g_out), small_row, dmod = _local_step(
        x, positions, loss_target, mod_rows, b_ada, norm_gain, q_norm_gain, kv_norm_gain, swa_sinks, final_gain,
        w_in_blocks, w_uq_b, w_ukv_b, w_out_b)

    by_owner = lambda g, n: jnp.transpose(g.reshape(g.shape[0], 4, n), (1, 0, 2)).reshape(4, 2, g.shape[0] // 2, n)
    grads = [jnp.stack(g_in_blocks).reshape(4, 2, D_MODEL // 2, -1), by_owner(g_uq, 192), by_owner(g_ukv, 256),
             g_out.reshape(4, 2, 128, D_MODEL)]
    part = jnp.concatenate([dmod, small_row, jnp.zeros((ROWS_PER_DEVICE - n_seq - 1, 3 * D_MODEL), F32)], axis=0)
    r_in, r_uq, r_ukv, r_out, parts_all = _comm_bwd_call(grads, part)
    g_in_s, g_uq_s = r_in.reshape(w_in.shape[1:]), r_uq.reshape(w_uq.shape[1:])
    g_ukv_s, g_out_s = r_ukv.reshape(w_ukv.shape[1:]), r_out.reshape(w_out.shape[1:])

    tr = lambda a: jnp.swapaxes(a[0], 0, 1)
    back = lambda ts: [jnp.swapaxes(t, 0, 1) for t in ts]
    d_in, nm_in, nv_in = back(_adam_call("adam_w_in", tr(w_in), g_in_s.T, tr(m_w_in), tr(v_w_in)))
    d_uq, nm_uq, nv_uq = back(_adam_call("adam_w_uq", tr(w_uq), g_uq_s.T, tr(m_w_uq), tr(v_w_uq)))
    d_ukv, nm_ukv, nv_ukv = _adam_call("adam_w_ukv", w_ukv[0], g_ukv_s, m_w_ukv[0], v_w_ukv[0])
    d_out, nm_out, nv_out = _adam_call("adam_w_out", w_out[0], g_out_s, m_w_out[0], v_w_out[0])
    dmod_cols = lax.dynamic_slice_in_dim(parts_all, chip * 768, 768, axis=1)
    g_ada, d_ada, nm_ada, nv_ada = _ada_bwd_call(act_all, dmod_cols, w_ada[0], m_w_ada[0], v_w_ada[0])

    row = lambda t: t.reshape(1, -1)
    small = {"b_ada": (b_ada, m_b_ada, v_b_ada), "norm_gain": (norm_gain, m_norm_gain, v_norm_gain),
             "q_norm_gain": (q_norm_gain, m_q_norm_gain, v_q_norm_gain),
             "kv_norm_gain": (kv_norm_gain, m_kv_norm_gain, v_kv_norm_gain),
             "swa_sinks": (swa_sinks, m_swa_sinks, v_swa_sinks),
             "final_gain": (row(final_gain), row(m_final_gain), row(v_final_gain))}
    res, loss_row = _small_call(parts_all, n_seq, [small[name] for name in SMALL_ORDER])
    res = dict(zip(SMALL_ORDER, res))
    res["final_gain"] = [t.reshape(-1) for t in res["final_gain"]]
    e = lambda t: t[None]
    big = {"w_ada": (e(g_ada), e(d_ada), e(nm_ada), e(nv_ada)), "w_in": (e(g_in_s), e(d_in), e(nm_in), e(nv_in)),
           "w_uq": (e(g_uq_s), e(d_uq), e(nm_uq), e(nv_uq)), "w_ukv": (e(g_ukv_s), e(d_ukv), e(nm_ukv), e(nv_ukv)),
           "w_out": (e(g_out_s), e(d_out), e(nm_out), e(nv_out))}
    order = ("w_ada", "b_ada", "norm_gain", "w_in", "q_norm_gain", "kv_norm_gain", "w_uq", "w_ukv", "swa_sinks", "w_out",
             "final_gain")
    pick = lambda kind: [(big[n] if n in big else res[n])[kind] for n in order]
    return (loss_row[0, 0], gx, *pick(0), *pick(1), *pick(2), *pick(3))
```

```python
import jax
import jax.numpy as jnp
from jax import lax
from jax.experimental import pallas as pl
from jax.experimental.pallas import tpu as pltpu

F32 = jnp.float32
BF16 = jnp.bfloat16

D_MODEL = 1024
Q_LORA = 384
KV_LORA = 256
N_HEADS = 8
MLA_NOPE = 64
MLA_ROPE = 32
HEAD_LANES = 128
HALF = 64
SWA_WINDOW = 128
EPS = 1e-6
ROPE_THETA = 10000.0
MLA_SCALE = (MLA_NOPE + MLA_ROPE) ** -0.5
LOG2E = 1.4426950408889634
LN2 = 0.6931471805599453
SWA_SCALE = 64 ** -0.5
NEG = -1e30

ADAM_LR = 0.001
ADAM_B1 = 0.9
ADAM_B2 = 0.999
ADAM_EPS = 1e-08
ADAM_WD = 0.01
ADAM_STEP = 10

A_ZQ, A_ZKV, A_KR, A_GM, A_QS, A_KS, A_VS, A_GS, A_END = 0, 384, 640, 768, 1280, 1792, 1920, 2048, 2560
IN_SPLITS = (384, 256, 32, 512, 512, 128, 128, 512)
D_IN = sum(IN_SPLITS)

TOKEN_TILE = 512
ATT_TILE = 256
VMEM_LIMIT = 56 * 1024 * 1024


def _dot(a, b):
    return jnp.dot(a, b, preferred_element_type=F32)


def _dot_nt(a, b):
    return lax.dot_general(a, b, (((1,), (1,)), ((), ())), preferred_element_type=F32)


def _dot_tn(a, b):
    return lax.dot_general(a, b, (((0,), (0,)), ((), ())), preferred_element_type=F32)


def _params(n_grid):
    return pltpu.CompilerParams(dimension_semantics=("arbitrary",) * n_grid, vmem_limit_bytes=VMEM_LIMIT)


def _full(shape):
    nd = len(shape)
    return pl.BlockSpec(shape, lambda *_: (0,) * nd, pipeline_mode=pl.Buffered(1))


def _sigmoid(g):
    return 1.0 / (1.0 + jnp.exp(-g))


SUB_TILE = 256


def _sub_tiles(tm):
    sub = min(SUB_TILE, tm)
    return [slice(s * sub, (s + 1) * sub) for s in range(tm // sub)]


MESH = pl.DeviceIdType.MESH
ROWS_PER_DEVICE = 8
VMEM_SPEC = pl.BlockSpec(memory_space=pltpu.VMEM)
ANY_SPEC = pl.BlockSpec(memory_space=pl.ANY)


def _position():
    x, y, c = lax.axis_index("x"), lax.axis_index("y"), lax.axis_index("c")
    sibling = (x, y, 1 - c)
    others = [(1 - x, y, c), (x, 1 - y, c), (1 - x, 1 - y, c)]
    return (x, y, c), 4 * x + 2 * y + c, 2 * x + y, sibling, others


def _rows_of(dev):
    return pl.ds(pl.multiple_of(dev * ROWS_PER_DEVICE, ROWS_PER_DEVICE), ROWS_PER_DEVICE)


def _all_to_all_rows(block_ref, table_ref, dev, me, send_sems, recv_sems):
    x, y, c = me
    waits = []
    for k in range(1, 8):
        peer = (1 - x if k & 4 else x, 1 - y if k & 2 else y, 1 - c if k & 1 else c)
        pltpu.make_async_remote_copy(src_ref=block_ref, dst_ref=table_ref.at[_rows_of(dev)], send_sem=send_sems.at[k - 1],
                                     recv_sem=recv_sems.at[k - 1], device_id=peer, device_id_type=MESH).start()
        waits.append(pltpu.make_async_remote_copy(
            src_ref=block_ref, dst_ref=table_ref.at[_rows_of(jnp.bitwise_xor(dev, k))], send_sem=send_sems.at[k - 1],
            recv_sem=recv_sems.at[k - 1], device_id=peer, device_id_type=MESH))
    return waits


def _comm_fwd_call(c_blk, w_ada, shards):
    n = len(shards)

    def body(c_ref, wada_ref, *refs):
        w_refs, act_ref, pieces_ref, full_refs = refs[:n], refs[n], refs[n + 1], refs[n + 2:2 * n + 2]
        c_all_ref = refs[2 * n + 2]
        c_send, c_recv, p_send, p_recv, w_send, w_recv, f_send, f_recv, loc_sem = refs[2 * n + 3:]
        me, dev, chip, sibling, others = _position()
        core = me[2]
        chip_of = [2 * p[0] + p[1] for p in others]

        local = [pltpu.make_async_copy(w_refs[i], full_refs[i].at[chip], loc_sem.at[i]) for i in range(n)]
        for cp in local:
            cp.start()

        def over_ici(i, j, src_chip):
            return pltpu.make_async_remote_copy(
                src_ref=w_refs[i].at[core], dst_ref=full_refs[i].at[src_chip, core], send_sem=w_send.at[3 * i + j],
                recv_sem=w_recv.at[3 * i + j], device_id=others[j], device_id_type=MESH)

        def to_sibling(i, j, half):
            return pltpu.make_async_remote_copy(
                src_ref=full_refs[i].at[chip_of[j], half], dst_ref=full_refs[i].at[chip_of[j], half],
                send_sem=f_send.at[3 * i + j], recv_sem=f_recv.at[3 * i + j], device_id=sibling, device_id_type=MESH)

        c_all_ref[_rows_of(dev), :] = c_ref[...]
        c_waits = _all_to_all_rows(c_ref, c_all_ref, dev, me, c_send, c_recv)
        sent = [over_ici(i, j, chip) for i in range(n) for j in range(3)]
        for cp in sent:
            cp.start()

        for cp in c_waits:
            cp.wait()
        cv = c_all_ref[...]
        act = cv * _sigmoid(cv)
        act_ref[...] = act
        pieces_ref[chip] = _dot(act.astype(BF16), wada_ref[...].astype(BF16))
        piece = lambda j, src_chip: pltpu.make_async_remote_copy(
            src_ref=pieces_ref.at[chip], dst_ref=pieces_ref.at[src_chip], send_sem=p_send.at[j], recv_sem=p_recv.at[j],
            device_id=others[j], device_id_type=MESH)
        for j in range(3):
            piece(j, chip).start()

        for i in range(n):
            for j in range(3):
                over_ici(i, j, chip_of[j]).wait_recv()
                to_sibling(i, j, core).start()
        for j in range(3):
            piece(j, chip).wait_send()
            piece(j, chip_of[j]).wait_recv()
        for i in range(n):
            for j in range(3):
                to_sibling(i, j, 1 - core).wait_recv()
                to_sibling(i, j, core).wait_send()
        for cp in sent:
            cp.wait_send()
        for cp in local:
            cp.wait()

    rows = 8 * ROWS_PER_DEVICE
    dma = pltpu.SemaphoreType.DMA
    return pl.pallas_call(
        body, name="comm_fwd",
        out_shape=[jax.ShapeDtypeStruct((rows, D_MODEL), F32), jax.ShapeDtypeStruct((4, rows, w_ada.shape[1]), F32)]
        + [jax.ShapeDtypeStruct((4,) + s.shape, s.dtype) for s in shards],
        in_specs=[VMEM_SPEC, VMEM_SPEC] + [ANY_SPEC] * n,
        out_specs=[VMEM_SPEC, VMEM_SPEC] + [ANY_SPEC] * n,
        scratch_shapes=[pltpu.VMEM((rows, D_MODEL), F32), dma((7,)), dma((7,)), dma((3,)), dma((3,)),
                        dma((3 * n,)), dma((3 * n,)), dma((3 * n,)), dma((3 * n,)), dma((n,))],
        compiler_params=pltpu.CompilerParams(vmem_limit_bytes=VMEM_LIMIT),
    )(c_blk, w_ada, *shards)


def _comm_bwd_call(grads, part):
    n = len(grads)

    def body(part_ref, *refs):
        g_refs, f_refs, parts_ref = refs[:n], refs[n:2 * n], refs[2 * n]
        scratch = refs[2 * n + 1:]
        a_refs, b_refs, p_refs, r_refs = (scratch[k * n:(k + 1) * n] for k in range(4))
        s_send, s_recv, d_send, d_recv, e_send, e_recv, h_send, h_recv, loc_sem = scratch[4 * n:]
        me, dev, chip, sibling, others = _position()
        core = me[2]
        chip_of = [2 * p[0] + p[1] for p in others]

        parts_ref[_rows_of(dev), :] = part_ref[...]
        s_waits = _all_to_all_rows(part_ref, parts_ref, dev, me, s_send, s_recv)

        mine = [pltpu.make_async_copy(g_refs[i].at[:, core], a_refs[i], loc_sem.at[i]) for i in range(n)]
        swap = [pltpu.make_async_remote_copy(src_ref=g_refs[i].at[:, 1 - core], dst_ref=b_refs[i], send_sem=d_send.at[i],
                                             recv_sem=d_recv.at[i], device_id=sibling, device_id_type=MESH) for i in range(n)]
        order = sorted(range(n), key=lambda i: g_refs[i].shape[2] * g_refs[i].shape[3])
        for i in order:
            mine[i].start()
            swap[i].start()
        cross = [pltpu.make_async_remote_copy(src_ref=p_refs[i].at[chip_of[j]], dst_ref=r_refs[i].at[j],
                                              send_sem=e_send.at[3 * i + j], recv_sem=e_recv.at[3 * i + j],
                                              device_id=others[j], device_id_type=MESH) for i in range(n) for j in range(3)]
        for i in order:
            mine[i].wait()
            swap[i].wait()
            for k in range(4):
                s = a_refs[i][k] + b_refs[i][k]
                a_refs[i][k] = s
                p_refs[i][k] = s.astype(BF16)
            for j in range(3):
                cross[3 * i + j].start()
        share = {}
        for i in order:
            for j in range(3):
                cross[3 * i + j].wait()
            f_refs[i][core] = (a_refs[i][chip] + r_refs[i][0].astype(F32) + r_refs[i][1].astype(F32)
                               + r_refs[i][2].astype(F32))
            share[i] = pltpu.make_async_remote_copy(src_ref=f_refs[i].at[core], dst_ref=f_refs[i].at[core],
                                                    send_sem=h_send.at[i], recv_sem=h_recv.at[i], device_id=sibling,
                                                    device_id_type=MESH)
            share[i].start()
        for i in range(n):
            share[i].wait_send()
            pltpu.make_async_remote_copy(src_ref=f_refs[i].at[core], dst_ref=f_refs[i].at[1 - core], send_sem=h_send.at[i],
                                         recv_sem=h_recv.at[i], device_id=sibling, device_id_type=MESH).wait_recv()
        for cp in s_waits:
            cp.wait()

    rows = 8 * ROWS_PER_DEVICE
    dma = pltpu.SemaphoreType.DMA
    quarter = [(4,) + g.shape[2:] for g in grads]
    return pl.pallas_call(
        body, name="comm_bwd",
        out_shape=[jax.ShapeDtypeStruct((2,) + g.shape[2:], F32) for g in grads]
        + [jax.ShapeDtypeStruct((rows, part.shape[1]), F32)],
        in_specs=[VMEM_SPEC] + [ANY_SPEC] * n,
        out_specs=[VMEM_SPEC] * (n + 1),
        scratch_shapes=[pltpu.VMEM(q, F32) for q in quarter] + [pltpu.VMEM(q, F32) for q in quarter]
        + [pltpu.VMEM(q, BF16) for q in quarter] + [pltpu.VMEM((3,) + q[1:], BF16) for q in quarter]
        + [dma((7,)), dma((7,)), dma((n,)), dma((n,)), dma((3 * n,)), dma((3 * n,)), dma((n,)), dma((n,)), dma((n,))],
        compiler_params=pltpu.CompilerParams(vmem_limit_bytes=VMEM_LIMIT),
    )(part, *grads)


def _twice(t):
    lo = _lane_lo()
    other = pltpu.roll(t, HALF, 1)
    return jnp.concatenate([jnp.where(lo, t, other), jnp.where(lo, other, t)], axis=1)


def _once(g):
    first, second = g[:, :HEAD_LANES], g[:, HEAD_LANES:]
    return jnp.where(_lane_lo(), first + pltpu.roll(first, HALF, 1), second + pltpu.roll(second, HALF, 1))


def _rope_tables(pos_col, inv_row):
    ang = pos_col * inv_row
    return jnp.cos(ang), jnp.sin(ang)


def _pre_call(x, pos_col, mod, b_ada, ng, qg, kvg, inv128, wa, wq2, wkv, seq):
    n_tok = x.shape[0]
    tm = min(TOKEN_TILE, seq)
    per_seq = seq // tm

    def body(x_ref, pos_ref, mod_ref, bada_ref, ng_ref, qg_ref, kvg_ref, inv_ref, wa_ref, wq_ref, wkv_ref,
             zqkv_ref, gates_ref, qf_ref, kf_ref, v_ref, qs_ref, kd_ref, vd_ref, rope_ref):
        xv = x_ref[...]
        modv = mod_ref[0] + bada_ref[...]
        shift, scale = modv[:, :D_MODEL], modv[:, D_MODEL:2 * D_MODEL]
        r1 = lax.rsqrt(jnp.mean(xv * xv, axis=-1, keepdims=True) + EPS)
        h = ((xv * r1) * ng_ref[...]) * (1.0 + scale) + shift
        hb = h.astype(BF16)
        za = _dot(hb, wa_ref[...])
        zkr = za[:, A_KR:A_GM]
        cos, sin = _rope_tables(pos_ref[...], inv_ref[...])
        rope_ref[:, :HEAD_LANES] = cos
        rope_ref[:, HEAD_LANES:] = sin
        zqkv_ref[...] = za[:, :A_KR]
        gates_ref[:, :512] = za[:, A_GM:A_QS]
        gates_ref[:, 512:] = za[:, A_GS:A_END]
        qs_ref[...] = (za[:, A_QS:A_KS] * (SWA_SCALE * LOG2E)).astype(BF16)
        kd_ref[...] = _twice(za[:, A_KS:A_VS]).astype(BF16)
        vd_ref[...] = _twice(za[:, A_VS:A_GS]).astype(BF16)
        zq, zkv = za[:, A_ZQ:A_ZKV], za[:, A_ZKV:A_KR]
        rq = lax.rsqrt(jnp.mean(zq * zq, axis=-1, keepdims=True) + EPS)
        qn = ((zq * rq) * qg_ref[...]).astype(BF16)
        qr = _dot(qn, wq_ref[...])
        cf, sf = jnp.tile(cos, (1, N_HEADS)), jnp.tile(sin, (1, N_HEADS))
        qf_ref[...] = ((qr[:, :1024] * cf + qr[:, 1024:] * sf) * (MLA_SCALE * LOG2E)).astype(BF16)
        rkv = lax.rsqrt(jnp.mean(zkv * zkv, axis=-1, keepdims=True) + EPS)
        kvn = ((zkv * rkv) * kvg_ref[...]).astype(BF16)
        kv = _dot(kvn, wkv_ref[...])
        kpe = jnp.where(_lane_lo(), 0.0, zkr * cos) + pltpu.roll(zkr, HALF, 1) * sin
        kf_ref[...] = (kv[:, :1024] + jnp.tile(kpe, (1, N_HEADS))).astype(BF16)
        v_ref[...] = kv[:, 1024:].astype(BF16)

    tok = lambda w: pl.BlockSpec((tm, w), lambda i: (i, 0))
    outs = [(640, F32), (1024, F32), (1024, BF16), (1024, BF16), (512, BF16), (512, BF16), (256, BF16), (256, BF16),
            (2 * HEAD_LANES, F32)]
    return pl.pallas_call(
        body, name="pre", grid=(n_tok // tm,),
        out_shape=[jax.ShapeDtypeStruct((n_tok, w), dt) for w, dt in outs],
        in_specs=[tok(D_MODEL), tok(1), pl.BlockSpec((1, 1, 3 * D_MODEL), lambda i: (i // per_seq, 0, 0)),
                  _full(b_ada.shape), _full(ng.shape), _full(qg.shape), _full(kvg.shape), _full(inv128.shape),
                  _full(wa.shape), _full(wq2.shape), _full(wkv.shape)],
        out_specs=[tok(w) for w, _ in outs],
        compiler_params=_params(1),
    )(x, pos_col, mod, b_ada, ng, qg, kvg, inv128, wa, wq2, wkv)


def _lane_lo(width=HEAD_LANES):
    return lax.broadcasted_iota(jnp.int32, (1, width), 1) < HALF


def _eye(n=HEAD_LANES):
    r = lax.broadcasted_iota(jnp.int32, (n, n), 0)
    c = lax.broadcasted_iota(jnp.int32, (n, n), 1)
    return jnp.where(r == c, 1.0, 0.0).astype(BF16)


def _mla_fwd_call(qf, kf, v, n_seq, seq):
    tq = min(ATT_TILE, seq)
    nq = seq // tq

    ext = HALF + 16

    def body(q_ref, k_ref, v_ref, o_ref, lse_ref, vt_ref, acc_ref):
        i = pl.program_id(1)
        eye = _eye()

        @pl.when(i == 0)
        def _():
            for h in range(N_HEADS):
                vt_ref[h * ext + HALF:(h + 1) * ext, :] = jnp.ones((16, seq), BF16)
            for t in range(nq):
                for p in range(N_HEADS // 2):
                    pair = slice(p * HEAD_LANES, (p + 1) * HEAD_LANES)
                    v_t = _dot_nt(eye, v_ref[t * tq:(t + 1) * tq, pair]).astype(BF16)
                    for hh in range(2):
                        r0 = (2 * p + hh) * ext
                        vt_ref[r0:r0 + HALF, t * tq:(t + 1) * tq] = v_t[hh * HALF:(hh + 1) * HALF, :]

        q = q_ref[...]
        qcol = i * tq + lax.broadcasted_iota(jnp.int32, (1, tq), 1)
        heads = range(N_HEADS)
        lanes = [slice(h * HEAD_LANES, (h + 1) * HEAD_LANES) for h in heads]

        def make_step(masked, n_tiles):
            def step(kt0, carry):
                tiles = range(n_tiles)
                start = pl.multiple_of(kt0 * tq, tq)
                ks = [k_ref[pl.ds(pl.multiple_of((kt0 + t) * tq, tq), tq), :] for t in tiles]
                vt = vt_ref[:, pl.ds(start, n_tiles * tq)]
                last = n_tiles - 1
                if masked:
                    keep = ((kt0 + last) * tq + lax.broadcasted_iota(jnp.int32, (tq, 1), 0)) <= qcol

                def scores(h):
                    sts = [_dot_nt(ks[t][:, lanes[h]], q[:, lanes[h]]) for t in tiles]
                    if masked:
                        sts[last] = jnp.where(keep, sts[last], NEG)
                    return sts

                def softmax(h, sts):
                    m_old = carry[h]
                    m_new = m_old
                    for st in sts:
                        m_new = jnp.maximum(m_new, jnp.max(st, axis=0, keepdims=True))
                    pt = jnp.concatenate([jnp.exp2(st - m_new).astype(BF16) for st in sts], axis=0)
                    return m_new, jnp.exp2(m_old - m_new), pt

                def values(h, alpha, pt):
                    rows = slice(h * ext, (h + 1) * ext)
                    acc_ref[rows, :] = acc_ref[rows, :] * alpha + _dot(vt[rows, :], pt)

                sts, soft, out = {0: scores(0), 1: scores(1)}, {}, {}
                for h in range(N_HEADS + 1):
                    if h + 2 < N_HEADS:
                        sts[h + 2] = scores(h + 2)
                    if h < N_HEADS:
                        soft[h] = softmax(h, sts.pop(h))
                    if h >= 1:
                        m_new, alpha, pt = soft.pop(h - 1)
                        values(h - 1, alpha, pt)
                        out[h - 1] = m_new
                return tuple(out[h] for h in heads)
            return step

        acc_ref[...] = jnp.zeros_like(acc_ref)
        init = (jnp.full((1, tq), NEG, F32),) * N_HEADS
        count = i + 1
        carry = lax.fori_loop(0, (count + 1) // 2 - 1, lambda j, c: make_step(False, 2)(2 * j, c), init)
        carry = lax.cond(count % 2 == 0, lambda c: make_step(True, 2)(i - 1, c), lambda c: make_step(True, 1)(i, c), carry)
        dens = [acc_ref[h * ext + HALF:h * ext + HALF + 1, :] for h in heads]
        acc_t = jnp.concatenate([acc_ref[h * ext:h * ext + HALF, :] * (1.0 / dens[h]) for h in heads], axis=0)
        o_ref[...] = acc_t.T
        for h in heads:
            lse_ref[0, h // 4, h % 4:h % 4 + 1, :] = carry[h] + jnp.log2(dens[h])

    n_tok = qf.shape[0]
    return pl.pallas_call(
        body, name="mla_fwd", grid=(n_seq, nq),
        out_shape=[jax.ShapeDtypeStruct((n_tok, 512), F32), jax.ShapeDtypeStruct((n_seq, 2, 4, seq), F32)],
        in_specs=[pl.BlockSpec((tq, 1024), lambda b, i: (b * nq + i, 0)),
                  pl.BlockSpec((seq, 1024), lambda b, i: (b, 0)),
                  pl.BlockSpec((seq, 512), lambda b, i: (b, 0))],
        out_specs=[pl.BlockSpec((tq, 512), lambda b, i: (b * nq + i, 0)),
                   pl.BlockSpec((1, 2, 4, tq), lambda b, i: (b, 0, 0, i))],
        scratch_shapes=[pltpu.VMEM((N_HEADS * ext, seq), BF16), pltpu.VMEM((N_HEADS * ext, tq), F32)],
        compiler_params=_params(2),
    )(qf, kf, v)


def _mla_bwd_call(qf, kf, v, do, o, lse, n_seq, seq):
    tq = min(ATT_TILE, seq)
    nq = seq // tq

    nh = 4
    heads = range(nh)
    lanes = [slice(h * HEAD_LANES, (h + 1) * HEAD_LANES) for h in heads]

    def body(q_ref, k_ref, v_ref, do_ref, o_ref, lse_ref, dq_ref, dk_ref, dv_ref,
             kt_ref, dot_ref, delta_ref, dqt_ref, dvt_ref):
        eye = _eye()
        lo = _lane_lo()
        sub_lo = lax.broadcasted_iota(jnp.int32, (HEAD_LANES, 1), 0) < HALF
        ones_lo = jnp.where(jnp.broadcast_to(lo, (8, HEAD_LANES)), 1.0, 0.0).astype(BF16)
        ones_hi = jnp.where(jnp.broadcast_to(lo, (8, HEAD_LANES)), 0.0, 1.0).astype(BF16)

        for t in range(nq):
            r = slice(t * tq, (t + 1) * tq)
            kv = k_ref[r, :]
            for h in heads:
                kt_ref[lanes[h], r] = _dot_nt(eye, kv[:, lanes[h]]).astype(BF16)
            for p in range(nh // 2):
                dov = do_ref[r, lanes[p]]
                dt = _dot_nt(eye, dov)
                dot_ref[2 * p, :, r] = jnp.where(sub_lo, dt, 0.0).astype(BF16)
                dot_ref[2 * p + 1, :, r] = jnp.where(sub_lo, 0.0, dt).astype(BF16)
                prod = dov.astype(F32) * o_ref[r, lanes[p]]
                p_hi = prod.astype(BF16)
                p_lo = (prod - p_hi.astype(F32)).astype(BF16)
                delta_ref[2 * p, :, r] = _dot_nt(ones_lo, p_hi) + _dot_nt(ones_lo, p_lo)
                delta_ref[2 * p + 1, :, r] = _dot_nt(ones_hi, p_hi) + _dot_nt(ones_hi, p_lo)
        dqt_ref[...] = jnp.zeros_like(dqt_ref)
        dvt_ref[...] = jnp.zeros_like(dvt_ref)

        def flush_dv(tile, which):
            rows = pl.ds(pl.multiple_of(tile * tq, tq), tq)
            for p in range(nh // 2):
                dv_ref[rows, lanes[p]] = dvt_ref[which, p * HEAD_LANES:(p + 1) * HEAD_LANES, :].T

        def k_step(kt, _):
            slot = kt % 2
            kr = pl.ds(pl.multiple_of(kt * tq, tq), tq)
            k = k_ref[kr, :]
            vv = v_ref[kr, :]
            k_t = kt_ref[:, kr]
            krow = kt * tq + lax.broadcasted_iota(jnp.int32, (tq, 1), 0)

            def make_step(masked, n_tiles):
                def q_step(qt0, carry):
                    tiles = range(n_tiles)
                    qrs = [pl.ds(pl.multiple_of((qt0 + t) * tq, tq), tq) for t in tiles]
                    if masked:
                        flush_dv(jnp.maximum(kt - 1, 0), 1 - slot)
                    qs = [q_ref[qr, :] for qr in qrs]
                    if masked:
                        keep = krow <= (qt0 * tq + lax.broadcasted_iota(jnp.int32, (1, tq), 1))

                    def scores(h):
                        do_ts = [dot_ref[h, :, qr] for qr in qrs]
                        sts = [_dot_nt(k[:, lanes[h]], qs[t][:, lanes[h]]) for t in tiles]
                        dpts = [_dot(vv[:, lanes[h // 2]], do_ts[t]) for t in tiles]
                        return do_ts, sts, dpts

                    def softmax(h, sts, dpts):
                        pts, dsts = [], []
                        for t in tiles:
                            pt = jnp.exp2(sts[t] - lse_ref[0, 0, h:h + 1, qrs[t]])
                            if masked and t == 0:
                                pt = jnp.where(keep, pt, 0.0)
                            dsts.append((pt * (dpts[t] - delta_ref[h, 0:1, qrs[t]])).astype(BF16))
                            pts.append(pt.astype(BF16))
                        return pts, dsts

                    def grads(h, do_ts, pts, dsts):
                        half = slice((h % 2) * HALF, (h % 2 + 1) * HALF)
                        dst_all = jnp.concatenate(dsts, axis=1)
                        pt_all = jnp.concatenate(pts, axis=1)
                        do_all = jnp.concatenate([do_ts[t][half, :] for t in tiles], axis=1)
                        q_all = jnp.concatenate([qs[t][:, lanes[h]] for t in tiles], axis=0)
                        dvt_ref[slot, h * HALF:(h + 1) * HALF, :] += _dot_nt(do_all, pt_all)
                        dk_ref[kr, lanes[h]] += _dot(dst_all, q_all)
                        for t in tiles:
                            dqt_ref[lanes[h], qrs[t]] += _dot(k_t[lanes[h], :], dsts[t])

                    first, second = {0: scores(0)}, {}
                    for h in range(nh + 1):
                        if h + 1 < nh:
                            first[h + 1] = scores(h + 1)
                        if h < nh:
                            do_ts, sts, dpts = first.pop(h)
                            second[h] = (do_ts,) + softmax(h, sts, dpts)
                        if h >= 1:
                            grads(h - 1, *second.pop(h - 1))
                    return carry
                return q_step

            dk_ref[kr, :] = jnp.zeros((tq, nh * HEAD_LANES), F32)
            dvt_ref[slot] = jnp.zeros(dvt_ref.shape[1:], F32)
            count = nq - kt
            lax.cond(count >= 2, lambda c: make_step(True, 2)(kt, c), lambda c: make_step(True, 1)(kt, c), 0)
            lax.fori_loop(1, count // 2, lambda j, c: make_step(False, 2)(kt + 2 * j, c), 0)
            lax.cond(jnp.logical_and(count % 2 == 1, count >= 3), lambda c: make_step(False, 1)(nq - 1, c), lambda c: c, 0)
            return 0

        lax.fori_loop(0, nq, k_step, 0)
        flush_dv(nq - 1, (nq - 1) % 2)
        for t in range(nq):
            r = slice(t * tq, (t + 1) * tq)
            for h in heads:
                dq_ref[r, lanes[h]] = dqt_ref[lanes[h], r].T

    n_tok = qf.shape[0]
    groups = N_HEADS // nh
    blk = lambda w: pl.BlockSpec((seq, w), lambda b, g: (b, g))
    return pl.pallas_call(
        body, name="mla_bwd", grid=(n_seq, groups),
        out_shape=[jax.ShapeDtypeStruct((n_tok, 1024), F32), jax.ShapeDtypeStruct((n_tok, 1024), F32),
                   jax.ShapeDtypeStruct((n_tok, 512), F32)],
        in_specs=[blk(512), blk(512), blk(256), blk(256), blk(256),
                  pl.BlockSpec((1, 1, nh, seq), lambda b, g: (b, g, 0, 0))],
        out_specs=[blk(512), blk(512), blk(256)],
        scratch_shapes=[pltpu.VMEM((nh * HEAD_LANES, seq), BF16), pltpu.VMEM((nh, HEAD_LANES, seq), BF16),
                        pltpu.VMEM((nh, 8, seq), F32), pltpu.VMEM((nh * HEAD_LANES, seq), F32),
                        pltpu.VMEM((2, nh * HALF, tq), F32)],
        compiler_params=_params(2),
    )(qf, kf, v, do, o, lse)


SWA_BLOCKS = 4


def _swa_block(n, pos_col_ref, posq):
    w = SWA_WINDOW
    start = pl.multiple_of(jnp.maximum(n - 1, 0) * w, w)
    posk = pos_col_ref[pl.ds(start, 2 * w), :]
    rel = (n * w + lax.broadcasted_iota(jnp.int32, (1, w), 1)) - (start + lax.broadcasted_iota(jnp.int32, (2 * w, 1), 0))
    valid = jnp.logical_and(rel >= 0, rel < w)
    return start, jnp.where(valid, posq - posk, 1e30)


def _alibi(h):
    return LOG2E * 2.0 ** -(h + 1)


def _transpose_rows(eye, src_ref, dst_ref, seq, width):
    step = 2 * SWA_WINDOW
    for t in range(seq // step):
        for p in range(width // HEAD_LANES):
            lanes = slice(p * HEAD_LANES, (p + 1) * HEAD_LANES)
            dst_ref[lanes, t * step:(t + 1) * step] = _dot_nt(eye, src_ref[t * step:(t + 1) * step, lanes]).astype(BF16)


def _swa_fwd_call(qs, kd, vd, pos_col, pos_row, sinks, n_seq, seq):
    w = SWA_WINDOW
    qb = SWA_BLOCKS
    steps = seq // (qb * w)
    ext = HALF + 16

    def body(q_ref, k_ref, v_ref, pc_ref, pr_ref, sink_ref, o_ref, lse_ref, vt_ref):
        n = pl.program_id(1)
        lo = _lane_lo()
        hi = jnp.logical_not(lo)
        eye = _eye()

        @pl.when(n == 0)
        def _():
            step = 2 * w
            for kv in range(2):
                vt_ref[kv * ext + HALF:(kv + 1) * ext, :] = jnp.ones((16, seq), BF16)
                for t in range(seq // step):
                    v_t = _dot_nt(eye, v_ref[t * step:(t + 1) * step, kv * HEAD_LANES:(kv + 1) * HEAD_LANES])
                    vt_ref[kv * ext:kv * ext + HALF, t * step:(t + 1) * step] = v_t[:HALF, :].astype(BF16)

        heads = range(N_HEADS)
        blocks = range(qb)
        geo = [_swa_block(n * qb + bi, pc_ref, pr_ref[bi]) for bi in blocks]
        wins = [pl.ds(g[0], 2 * w) for g in geo]
        kwins = [k_ref[win, :] for win in wins]
        vts = [vt_ref[:, win] for win in wins]
        sts = []
        for bi in blocks:
            q = q_ref[bi * w:(bi + 1) * w, :]
            sts.append([])
            for j in range(N_HEADS // 2):
                qp = q[:, j * HEAD_LANES:(j + 1) * HEAD_LANES]
                both = jnp.concatenate([jnp.where(lo, qp, jnp.zeros_like(qp)), jnp.where(hi, qp, jnp.zeros_like(qp))], axis=0)
                st = _dot_nt(kwins[bi][:, (j // 2) * HEAD_LANES:(j // 2 + 1) * HEAD_LANES], both)
                sts[bi] += [st[:, :w], st[:, w:]]
        ps, ms = [], []
        for bi in blocks:
            ps.append([])
            ms.append([])
            for h in heads:
                s = sts[bi][h] - _alibi(h) * geo[bi][1]
                m = jnp.maximum(jnp.max(s, axis=0, keepdims=True), sink_ref[0, h] * LOG2E)
                ps[bi].append(jnp.exp2(s - m).astype(BF16))
                ms[bi].append(m)
        for bi in blocks:
            ots = []
            for h in heads:
                pv = _dot(vts[bi][(h // 4) * ext:(h // 4 + 1) * ext, :], ps[bi][h])
                l = pv[HALF:HALF + 1, :] + jnp.exp2(sink_ref[0, h] * LOG2E - ms[bi][h])
                ots.append(pv[:HALF, :] * (1.0 / l))
                lse_ref[0, h:h + 1, bi * w:(bi + 1) * w] = ms[bi][h] + jnp.log2(l)
            o_ref[bi * w:(bi + 1) * w, :] = jnp.concatenate(ots, axis=0).T

    n_tok = qs.shape[0]
    tok = lambda width: pl.BlockSpec((qb * w, width), lambda b, n: (b * steps + n, 0))
    whole = lambda width: pl.BlockSpec((seq, width), lambda b, n: (b, 0))
    return pl.pallas_call(
        body, name="swa_fwd", grid=(n_seq, steps),
        out_shape=[jax.ShapeDtypeStruct((n_tok, 512), F32), jax.ShapeDtypeStruct((n_seq, N_HEADS, seq), F32)],
        in_specs=[tok(512), whole(256), whole(256), whole(1), pl.BlockSpec((qb, 1, w), lambda b, n: (b * steps + n, 0, 0)),
                  pl.BlockSpec(memory_space=pltpu.SMEM)],
        out_specs=[tok(512), pl.BlockSpec((1, N_HEADS, qb * w), lambda b, n: (b, 0, n))],
        scratch_shapes=[pltpu.VMEM((2 * ext, seq), BF16)],
        compiler_params=_params(2),
    )(qs, kd, vd, pos_col, pos_row, sinks)


def _swa_bwd_call(qs, kd, vd, do, o, lse, pos_col, pos_row, sinks, n_seq, seq):
    w = SWA_WINDOW
    qb = SWA_BLOCKS
    steps = seq // (qb * w)

    def body(q_ref, k_ref, v_ref, do_ref, o_ref, lse_ref, pc_ref, pr_ref, sink_ref, dq_ref, dk_ref, dv_ref, dsink_ref,
             kt_ref):
        b, n = pl.program_id(0), pl.program_id(1)
        lo = _lane_lo()
        hi = jnp.logical_not(lo)
        sub_lo = lax.broadcasted_iota(jnp.int32, (HEAD_LANES, 1), 0) < HALF
        eye = _eye()
        ones_lo = jnp.where(jnp.broadcast_to(lo, (8, HEAD_LANES)), 1.0, 0.0).astype(BF16)
        ones_hi = jnp.where(jnp.broadcast_to(lo, (8, HEAD_LANES)), 0.0, 1.0).astype(BF16)

        @pl.when(n == 0)
        def _():
            dk_ref[...] = jnp.zeros_like(dk_ref)
            dv_ref[...] = jnp.zeros_like(dv_ref)
            _transpose_rows(eye, k_ref, kt_ref, seq, 2 * HEAD_LANES)

        @pl.when(jnp.logical_and(n == 0, b == 0))
        def _():
            dsink_ref[...] = jnp.zeros_like(dsink_ref)

        heads = range(N_HEADS)
        blocks = range(qb)
        kv_lanes = lambda h: slice((h // 4) * HEAD_LANES, (h // 4 + 1) * HEAD_LANES)
        geo = [_swa_block(n * qb + bi, pc_ref, pr_ref[bi]) for bi in blocks]
        wins = [pl.ds(g[0], 2 * w) for g in geo]
        kwins = [k_ref[win, :] for win in wins]
        vwins = [v_ref[win, :] for win in wins]

        do_ts, deltas, qms, doms = [], [], [], []
        for bi in blocks:
            rows = slice(bi * w, (bi + 1) * w)
            for lst in (do_ts, deltas, qms, doms):
                lst.append([])
            for j in range(N_HEADS // 2):
                pair = slice(j * HEAD_LANES, (j + 1) * HEAD_LANES)
                dop = do_ref[rows, pair]
                qp = q_ref[rows, pair]
                dt = _dot_nt(eye, dop)
                prod = dop.astype(F32) * o_ref[rows, pair]
                p_hi = prod.astype(BF16)
                p_lo = (prod - p_hi.astype(F32)).astype(BF16)
                for hh in range(2):
                    half, ones = (lo, ones_lo) if hh == 0 else (hi, ones_hi)
                    do_ts[bi].append(jnp.where(sub_lo, dt, 0.0).astype(BF16) if hh == 0
                                     else jnp.where(sub_lo, 0.0, dt).astype(BF16))
                    deltas[bi].append((_dot_nt(ones, p_hi) + _dot_nt(ones, p_lo))[0:1, :])
                    qms[bi].append(jnp.where(half, qp, jnp.zeros_like(qp)))
                    doms[bi].append(jnp.where(half, dop, jnp.zeros_like(dop)))
        sts, dpts = [], []
        for bi in blocks:
            sts.append([])
            dpts.append([])
            for j in range(N_HEADS // 2):
                a, b = 2 * j, 2 * j + 1
                st = _dot_nt(kwins[bi][:, kv_lanes(a)], jnp.concatenate([qms[bi][a], qms[bi][b]], axis=0))
                dpt = _dot(vwins[bi][:, kv_lanes(a)], jnp.concatenate([do_ts[bi][a], do_ts[bi][b]], axis=1))
                sts[bi] += [st[:, :w], st[:, w:]]
                dpts[bi] += [dpt[:, :w], dpt[:, w:]]
        pts, dsts = [], []
        for bi in blocks:
            pts.append([])
            dsts.append([])
            for h in heads:
                lse_h = lse_ref[0, h:h + 1, bi * w:(bi + 1) * w]
                pt = jnp.exp2(sts[bi][h] - _alibi(h) * geo[bi][1] - lse_h)
                dsts[bi].append((pt * (dpts[bi][h] - deltas[bi][h])).astype(BF16))
                pts[bi].append(pt.astype(BF16))
                dsink_ref[h:h + 1, :] += -jnp.exp2(sink_ref[0, h] * LOG2E - lse_h) * deltas[bi][h]
        for bi in blocks:
            for kv in range(2):
                group = range(4 * kv, 4 * kv + 4)
                dst_all = jnp.concatenate([dsts[bi][h] for h in group], axis=1)
                pt_all = jnp.concatenate([pts[bi][h] for h in group], axis=1)
                q_all = jnp.concatenate([qms[bi][h] for h in group], axis=0)
                do_all = jnp.concatenate([doms[bi][h] for h in group], axis=0)
                dk_ref[wins[bi], kv_lanes(4 * kv)] += _dot(dst_all, q_all)
                dv_ref[wins[bi], kv_lanes(4 * kv)] += _dot(pt_all, do_all)
        for bi in blocks:
            ktw = kt_ref[:, wins[bi]]
            for j in range(N_HEADS // 2):
                k_t = ktw[kv_lanes(2 * j), :]
                both = _dot(k_t, jnp.concatenate([dsts[bi][2 * j], dsts[bi][2 * j + 1]], axis=1))
                dq_t = jnp.where(sub_lo, both[:, :w], both[:, w:])
                dq_ref[bi * w:(bi + 1) * w, j * HEAD_LANES:(j + 1) * HEAD_LANES] = dq_t.T * SWA_SCALE

    n_tok = qs.shape[0]
    tok = lambda width: pl.BlockSpec((qb * w, width), lambda b, n: (b * steps + n, 0))
    whole = lambda width: pl.BlockSpec((seq, width), lambda b, n: (b, 0))
    return pl.pallas_call(
        body, name="swa_bwd", grid=(n_seq, steps),
        out_shape=[jax.ShapeDtypeStruct((n_tok, 512), F32), jax.ShapeDtypeStruct((n_tok, 256), F32),
                   jax.ShapeDtypeStruct((n_tok, 256), F32), jax.ShapeDtypeStruct((N_HEADS, HEAD_LANES), F32)],
        in_specs=[tok(512), whole(256), whole(256), pl.BlockSpec((qb * w, 512), lambda b, n: (b * steps + n, 1)), tok(512),
                  pl.BlockSpec((1, N_HEADS, qb * w), lambda b, n: (b, 0, n)),
                  whole(1), pl.BlockSpec((qb, 1, w), lambda b, n: (b * steps + n, 0, 0)),
                  pl.BlockSpec(memory_space=pltpu.SMEM)],
        out_specs=[tok(512), whole(256), whole(256), _full((N_HEADS, HEAD_LANES))],
        scratch_shapes=[pltpu.VMEM((2 * HEAD_LANES, seq), BF16)],
        compiler_params=_params(2),
    )(qs, kd, vd, do, o, lse, pos_col, pos_row, sinks)


def _post_call(x, target, o_mla, o_swa, gates, mod, b_ada, fg, w_out, seq):
    n_tok = x.shape[0]
    tm = min(TOKEN_TILE, seq)
    per_seq = seq // tm
    n_seq = n_tok // seq

    def body(x_ref, t_ref, om_ref, os_ref, g_ref, mod_ref, bada_ref, fg_ref, w_ref,
             dx2_ref, do_ref, dg_ref, gw_ref, gfg_ref, dgate_ref, loss_ref):
        i = pl.program_id(0)

        @pl.when(i == 0)
        def _():
            gw_ref[...] = jnp.zeros_like(gw_ref)
            gfg_ref[...] = jnp.zeros_like(gfg_ref)
            loss_ref[...] = jnp.zeros_like(loss_ref)

        @pl.when(i % per_seq == 0)
        def _():
            dgate_ref[...] = jnp.zeros_like(dgate_ref)

        gate = mod_ref[0][:, 2 * D_MODEL:] + bada_ref[:, 2 * D_MODEL:]
        fgv = fg_ref[...]
        fgd = fgv * (1.0 / D_MODEL)
        subs = _sub_tiles(tm)
        gs = [g_ref[r, :] for r in subs]
        os_ = [jnp.concatenate([om_ref[r, :], os_ref[r, :]], axis=-1) for r in subs]
        sgs = [_sigmoid(g) for g in gs]
        sils = [g * sg for g, sg in zip(gs, sgs)]
        ypres = [(o * sil).astype(BF16) for o, sil in zip(os_, sils)]
        ys = [_dot(ypre, w_ref[...]) for ypre in ypres]
        dys, loss, gfg, dgate = [], 0.0, 0.0, 0.0
        for r, y in zip(subs, ys):
            x2 = x_ref[r, :] + gate * y
            r2 = lax.rsqrt(jnp.mean(x2 * x2, axis=-1, keepdims=True) + EPS)
            xn2 = x2 * r2
            err = xn2 * fgv - t_ref[r, :]
            loss = loss + jnp.sum(jnp.sum(err * err, axis=-1, keepdims=True), axis=0, keepdims=True)
            gfg = gfg + jnp.sum(err * xn2, axis=0, keepdims=True)
            dxn2 = err * fgd
            dx2 = r2 * (dxn2 - xn2 * jnp.mean(dxn2 * xn2, axis=-1, keepdims=True))
            dx2_ref[r, :] = dx2
            dgate = dgate + jnp.sum(dx2 * y, axis=0, keepdims=True)
            dys.append((dx2 * gate).astype(BF16))
        loss_ref[...] += jnp.broadcast_to(loss * (0.5 / D_MODEL), loss_ref.shape)
        gfg_ref[...] += gfg * (1.0 / D_MODEL)
        dgate_ref[0] += dgate
        gw_ref[...] += _dot_tn(jnp.concatenate(ypres, axis=0), jnp.concatenate(dys, axis=0))
        dypres = [_dot_nt(dy, w_ref[...]) for dy in dys]
        for r, dypre, o, g, sg, sil in zip(subs, dypres, os_, gs, sgs, sils):
            do_ref[r, :] = (dypre * sil).astype(BF16)
            dg_ref[r, :] = (dypre * o * (sg + sil * (1.0 - sg))).astype(BF16)

    tok = lambda w: pl.BlockSpec((tm, w), lambda i: (i, 0))
    per_b = pl.BlockSpec((1, 1, 3 * D_MODEL), lambda i: (i // per_seq, 0, 0))
    return pl.pallas_call(
        body, name="post", grid=(n_tok // tm,),
        out_shape=[jax.ShapeDtypeStruct((n_tok, D_MODEL), F32), jax.ShapeDtypeStruct((n_tok, D_MODEL), BF16),
                   jax.ShapeDtypeStruct((n_tok, D_MODEL), BF16), jax.ShapeDtypeStruct((D_MODEL, D_MODEL), F32),
                   jax.ShapeDtypeStruct((1, D_MODEL), F32), jax.ShapeDtypeStruct((n_seq, 1, D_MODEL), F32),
                   jax.ShapeDtypeStruct((1, HEAD_LANES), F32)],
        in_specs=[tok(D_MODEL), tok(D_MODEL), tok(512), tok(512), tok(D_MODEL), per_b, _full(b_ada.shape),
                  _full(fg.shape), _full(w_out.shape)],
        out_specs=[tok(D_MODEL), tok(D_MODEL), tok(D_MODEL), _full((D_MODEL, D_MODEL)), _full((1, D_MODEL)),
                   pl.BlockSpec((1, 1, D_MODEL), lambda i: (i // per_seq, 0, 0)), _full((1, HEAD_LANES))],
        compiler_params=_params(1),
    )(x, target, o_mla, o_swa, gates, mod, b_ada, fg, w_out)


def _mid_bwd_call(dqf, dkf, dv, zqkv, rope, qg, kvg, wq2, wkv, seq):
    n_tok = dqf.shape[0]
    tm = min(TOKEN_TILE, seq)

    def body(dq_ref, dk_ref, dv_ref, z_ref, rope_ref, qg_ref, kvg_ref, wq_ref, wkv_ref,
             dz_ref, gwq_ref, gwkv_ref, gqg_ref, gkvg_ref):
        i = pl.program_id(0)

        @pl.when(i == 0)
        def _():
            gwq_ref[...] = jnp.zeros_like(gwq_ref)
            gwkv_ref[...] = jnp.zeros_like(gwkv_ref)
            gqg_ref[...] = jnp.zeros_like(gqg_ref)
            gkvg_ref[...] = jnp.zeros_like(gkvg_ref)

        cos, sin = rope_ref[:, :HEAD_LANES], rope_ref[:, HEAD_LANES:]
        cf, sf = jnp.tile(cos, (1, N_HEADS)), jnp.tile(sin, (1, N_HEADS))
        dq = dq_ref[...] * MLA_SCALE
        dqr = jnp.concatenate([dq * cf, dq * sf], axis=-1).astype(BF16)
        zq, zkv = z_ref[:, :Q_LORA], z_ref[:, Q_LORA:]
        qgv, kvgv = qg_ref[...], kvg_ref[...]

        rq = lax.rsqrt(jnp.mean(zq * zq, axis=-1, keepdims=True) + EPS)
        xq = zq * rq
        gwq_ref[...] += _dot_tn((xq * qgv).astype(BF16), dqr)
        dqn = _dot_nt(dqr, wq_ref[...])
        gqg_ref[...] += jnp.sum(dqn * xq, axis=0, keepdims=True)
        dxq = dqn * qgv
        dz_ref[:, :Q_LORA] = (rq * (dxq - xq * jnp.mean(dxq * xq, axis=-1, keepdims=True))).astype(BF16)

        dk = dk_ref[...] * LN2
        dkv = jnp.concatenate([dk, dv_ref[...]], axis=-1).astype(BF16)
        rkv = lax.rsqrt(jnp.mean(zkv * zkv, axis=-1, keepdims=True) + EPS)
        xkv = zkv * rkv
        gwkv_ref[...] += _dot_tn((xkv * kvgv).astype(BF16), dkv)
        dkvn = _dot_nt(dkv, wkv_ref[...])
        gkvg_ref[...] += jnp.sum(dkvn * xkv, axis=0, keepdims=True)
        dxkv = dkvn * kvgv
        dz_ref[:, Q_LORA:A_KR] = (rkv * (dxkv - xkv * jnp.mean(dxkv * xkv, axis=-1, keepdims=True))).astype(BF16)

        dkpe = dk[:, :HEAD_LANES]
        for h in range(1, N_HEADS):
            dkpe = dkpe + dk[:, h * HEAD_LANES:(h + 1) * HEAD_LANES]
        dz_ref[:, A_KR:] = (jnp.where(_lane_lo(), 0.0, dkpe * cos) + pltpu.roll(dkpe * sin, HALF, 1)).astype(BF16)

    tok = lambda w: pl.BlockSpec((tm, w), lambda i: (i, 0))
    return pl.pallas_call(
        body, name="mid_bwd", grid=(n_tok // tm,),
        out_shape=[jax.ShapeDtypeStruct((n_tok, A_GM), BF16),
                   jax.ShapeDtypeStruct(wq2.shape, F32), jax.ShapeDtypeStruct(wkv.shape, F32),
                   jax.ShapeDtypeStruct((1, Q_LORA), F32), jax.ShapeDtypeStruct((1, KV_LORA), F32)],
        in_specs=[tok(1024), tok(1024), tok(512), tok(640), tok(2 * HEAD_LANES), _full(qg.shape), _full(kvg.shape),
                  _full(wq2.shape), _full(wkv.shape)],
        out_specs=[tok(A_GM), _full(wq2.shape), _full(wkv.shape), _full((1, Q_LORA)), _full((1, KV_LORA))],
        compiler_params=_params(1),
    )(dqf, dkf, dv, zqkv, rope, qg, kvg, wq2, wkv)


def _in_bwd_call(x, dx2, dz, dg, dqs, dkd, dvd, mod, b_ada, ng, wa, seq):
    n_tok = x.shape[0]
    tm = min(TOKEN_TILE, seq)
    per_seq = seq // tm
    n_seq = n_tok // seq

    def body(x_ref, dx2_ref, dz_ref, dg_ref, dqs_ref, dkd_ref, dvd_ref, mod_ref, bada_ref, ng_ref,
             wa_ref, gx_ref, gwa_ref, gng_ref, dshift_ref, dscale_ref):
        i = pl.program_id(0)

        @pl.when(i == 0)
        def _():
            gwa_ref[...] = jnp.zeros_like(gwa_ref)
            gng_ref[...] = jnp.zeros_like(gng_ref)

        @pl.when(i % per_seq == 0)
        def _():
            dshift_ref[...] = jnp.zeros_like(dshift_ref)
            dscale_ref[...] = jnp.zeros_like(dscale_ref)

        xv = x_ref[...]
        modv = mod_ref[0] + bada_ref[...]
        shift, scale = modv[:, :D_MODEL], modv[:, D_MODEL:2 * D_MODEL]
        ngv = ng_ref[...]
        r1 = lax.rsqrt(jnp.mean(xv * xv, axis=-1, keepdims=True) + EPS)
        xn = xv * r1
        hb = ((xn * ngv) * (1.0 + scale) + shift).astype(BF16)

        dgv = dg_ref[...]
        pieces = [(A_ZQ, dz_ref[...]), (A_GM, dgv[:, :512]), (A_QS, dqs_ref[...].astype(BF16)),
                  (A_KS, jnp.concatenate([_once(dkd_ref[...]) * LN2, _once(dvd_ref[...])], axis=1).astype(BF16)),
                  (A_GS, dgv[:, 512:])]
        dh = None
        for off, piece in pieces:
            wd = piece.shape[1]
            gwa_ref[:, off:off + wd] += _dot_tn(hb, piece)
            term = _dot_nt(piece, wa_ref[:, off:off + wd])
            dh = term if dh is None else dh + term

        dshift_ref[0] += jnp.sum(dh, axis=0, keepdims=True)
        dscale_ref[0] += jnp.sum(dh * (xn * ngv), axis=0, keepdims=True)
        gng_ref[...] += jnp.sum(dh * xn * (1.0 + scale), axis=0, keepdims=True)
        dxn = dh * ngv * (1.0 + scale)
        gx_ref[...] = dx2_ref[...] + r1 * (dxn - xn * jnp.mean(dxn * xn, axis=-1, keepdims=True))

    tok = lambda w: pl.BlockSpec((tm, w), lambda i: (i, 0))
    per_b = lambda w: pl.BlockSpec((1, 1, w), lambda i: (i // per_seq, 0, 0))
    return pl.pallas_call(
        body, name="in_bwd", grid=(n_tok // tm,),
        out_shape=[jax.ShapeDtypeStruct((n_tok, D_MODEL), F32), jax.ShapeDtypeStruct((D_MODEL, A_END), F32),
                   jax.ShapeDtypeStruct((1, D_MODEL), F32),
                   jax.ShapeDtypeStruct((n_seq, 1, D_MODEL), F32), jax.ShapeDtypeStruct((n_seq, 1, D_MODEL), F32)],
        in_specs=[tok(D_MODEL), tok(D_MODEL), tok(A_GM), tok(D_MODEL), tok(512), tok(256), tok(256),
                  per_b(3 * D_MODEL), _full(b_ada.shape), _full(ng.shape), _full(wa.shape)],
        out_specs=[tok(D_MODEL), _full((D_MODEL, A_END)), _full((1, D_MODEL)), per_b(D_MODEL), per_b(D_MODEL)],
        compiler_params=_params(1),
    )(x, dx2, dz, dg, dqs, dkd, dvd, mod, b_ada, ng, wa)


def _adam_math(w, g, m, v):
    m_new = ADAM_B1 * m + (1.0 - ADAM_B1) * g
    v_new = ADAM_B2 * v + (1.0 - ADAM_B2) * (g * g)
    m_hat = m_new / (1.0 - ADAM_B1 ** ADAM_STEP)
    v_hat = v_new / (1.0 - ADAM_B2 ** ADAM_STEP)
    delta = -ADAM_LR * (m_hat / (jnp.sqrt(v_hat) + ADAM_EPS) + ADAM_WD * w)
    return delta, m_new, v_new


def _adam_call(name, w, g, m, v):
    rows, cols = w.shape
    tr = next((t for t in (256, 128, 88) if rows % t == 0), rows)

    def body(w_ref, g_ref, m_ref, v_ref, d_ref, mo_ref, vo_ref):
        d, mn, vn = _adam_math(w_ref[...], g_ref[...], m_ref[...], v_ref[...])
        d_ref[...] = d
        mo_ref[...] = mn
        vo_ref[...] = vn

    spec = pl.BlockSpec((tr, cols), lambda i: (i, 0))
    return pl.pallas_call(
        body, name=name, grid=(rows // tr,),
        out_shape=[jax.ShapeDtypeStruct(w.shape, F32)] * 3,
        in_specs=[spec] * 4, out_specs=[spec] * 3,
        compiler_params=_params(1),
    )(w, g, m, v)


def _ada_bwd_call(act_all, dmod_cols, w, m, v):
    rows, cols = w.shape
    tr = 256

    def body(a_ref, dm_ref, w_ref, m_ref, v_ref, g_ref, d_ref, mo_ref, vo_ref):
        g = _dot_tn(a_ref[...].astype(BF16), dm_ref[...].astype(BF16))
        d, mn, vn = _adam_math(w_ref[...], g, m_ref[...], v_ref[...])
        g_ref[...] = g
        d_ref[...] = d
        mo_ref[...] = mn
        vo_ref[...] = vn

    spec = pl.BlockSpec((tr, cols), lambda i: (i, 0))
    nb = act_all.shape[0]
    return pl.pallas_call(
        body, name="ada_bwd", grid=(rows // tr,),
        out_shape=[jax.ShapeDtypeStruct(w.shape, F32)] * 4,
        in_specs=[pl.BlockSpec((nb, tr), lambda i: (0, i)), _full(dmod_cols.shape), spec, spec, spec],
        out_specs=[spec] * 4,
        compiler_params=_params(1),
    )(act_all, dmod_cols, w, m, v)


SMALL_ROW = {"norm_gain": (0, 1024), "final_gain": (1024, 2048), "q_norm_gain": (2048, 2432),
             "kv_norm_gain": (2432, 2688), "swa_sinks": (2688, 2696), "loss": (2816, 2944)}
SMALL_ORDER = ("b_ada", "norm_gain", "q_norm_gain", "kv_norm_gain", "swa_sinks", "final_gain")


def _small_call(parts_all, n_seq, params):
    k = len(params)

    def body(p_ref, *refs):
        ins, outs, loss_ref = refs[:3 * k], refs[3 * k:7 * k], refs[7 * k]
        row = p_ref[n_seq:n_seq + 1, :]
        for dv in range(1, 8):
            r0 = dv * ROWS_PER_DEVICE + n_seq
            row = row + p_ref[r0:r0 + 1, :]
        gb = None
        for dv in range(8):
            for r in range(n_seq):
                r0 = dv * ROWS_PER_DEVICE + r
                gb = p_ref[r0:r0 + 1, :] if gb is None else gb + p_ref[r0:r0 + 1, :]
        for j, name in enumerate(SMALL_ORDER):
            g = gb if name == "b_ada" else row[:, SMALL_ROW[name][0]:SMALL_ROW[name][1]]
            d, mn, vn = _adam_math(ins[3 * j][...], g, ins[3 * j + 1][...], ins[3 * j + 2][...])
            outs[4 * j][...] = g
            outs[4 * j + 1][...] = d
            outs[4 * j + 2][...] = mn
            outs[4 * j + 3][...] = vn
        loss_ref[...] = row[:, SMALL_ROW["loss"][0]:SMALL_ROW["loss"][1]]

    flat = [t for p in params for t in p]
    res = pl.pallas_call(
        body, name="small_update", grid=(1,),
        out_shape=[jax.ShapeDtypeStruct(p[0].shape, F32) for p in params for _ in range(4)]
        + [jax.ShapeDtypeStruct((1, HEAD_LANES), F32)],
        in_specs=[_full(parts_all.shape)] + [_full(t.shape) for t in flat],
        out_specs=[_full(p[0].shape) for p in params for _ in range(4)] + [_full((1, HEAD_LANES))],
        compiler_params=_params(1),
    )(parts_all, *flat)
    return [res[4 * j:4 * j + 4] for j in range(k)], res[4 * k]


def _rot(t):
    half = t.shape[-1] // 2
    return jnp.concatenate([-t[..., half:], t[..., :half]], axis=-1)


def _rot_t(g):
    half = g.shape[-1] // 2
    return jnp.concatenate([g[..., half:], -g[..., :half]], axis=-1)


def _columns(segments, lo, hi):
    out, at = [], 0
    for seg in segments:
        n = seg.shape[1]
        a, b = max(lo, at), min(hi, at + n)
        if a < b:
            out.append(seg[:, a - at:b - at])
        at += n
    return out


def _prepare_weights(w_in_blocks, w_uq, w_ukv):
    o = [0]
    for s in IN_SPLITS:
        o.append(o[-1] + s)
    part = lambda a, b: _columns(w_in_blocks, a, b)
    kr = jnp.concatenate(part(o[2], o[3]), axis=1)
    zero = jnp.zeros((kr.shape[0], 32), kr.dtype)
    wa = jnp.concatenate(part(0, o[2]) + [_rot(kr), zero, kr, zero] + part(o[3], o[8]), axis=1)
    uq = w_uq.reshape(Q_LORA, N_HEADS, MLA_NOPE + MLA_ROPE)
    zq = jnp.zeros((Q_LORA, N_HEADS, 32), w_uq.dtype)
    uq_full = jnp.concatenate([uq, zq], axis=-1).reshape(Q_LORA, 1024)
    uq_rot = jnp.concatenate([jnp.zeros((Q_LORA, N_HEADS, 64), w_uq.dtype), _rot(uq[..., MLA_NOPE:]), zq],
                             axis=-1).reshape(Q_LORA, 1024)
    wq2 = jnp.concatenate([uq_full, uq_rot], axis=1)
    ukv = w_ukv.reshape(KV_LORA, N_HEADS, 128)
    k_full = jnp.concatenate([ukv[..., :64], jnp.zeros((KV_LORA, N_HEADS, 64), w_ukv.dtype)], axis=-1).reshape(KV_LORA, 1024)
    wkv = jnp.concatenate([k_full, ukv[..., 64:].reshape(KV_LORA, 512)], axis=1)
    return wa, wq2, wkv


def _restore_grads(gwa, gwq2, gwkv):
    gkr = gwa[:, A_KR + 64:A_KR + 96] + _rot_t(gwa[:, A_KR:A_KR + 32])
    in_order = [gwa[:, :A_KR], gkr, gwa[:, A_GM:]]
    n = D_IN // 4
    g_in = [jnp.concatenate(_columns(in_order, k * n, (k + 1) * n), axis=1) for k in range(4)]
    gf = gwq2[:, :1024].reshape(Q_LORA, N_HEADS, 128)
    gr = gwq2[:, 1024:].reshape(Q_LORA, N_HEADS, 128)
    g_uq = jnp.concatenate([gf[..., :64], gf[..., 64:96] + _rot_t(gr[..., 64:96])], axis=-1).reshape(Q_LORA, 768)
    gk = gwkv[:, :1024].reshape(KV_LORA, N_HEADS, 128)[..., :64]
    gv = gwkv[:, 1024:].reshape(KV_LORA, N_HEADS, 64)
    g_ukv = jnp.concatenate([gk, gv], axis=-1).reshape(KV_LORA, 1024)
    return g_in, g_uq, g_ukv


def _local_step(x, positions, target, mod_rows, b_ada, ng, qg, kvg, sinks, fg, w_in_b, w_uq_b, w_ukv_b, w_out_b):
    n_seq, seq, _ = x.shape
    n_tok = n_seq * seq
    x2d = x.reshape(n_tok, D_MODEL)
    t2d = target.reshape(n_tok, D_MODEL)
    pos_f = positions.astype(F32)
    pos_col = pos_f.reshape(n_tok, 1)
    pos_row = pos_f.reshape(n_tok // SWA_WINDOW, 1, SWA_WINDOW)
    mod3 = mod_rows.reshape(n_seq, 1, 3 * D_MODEL)
    inv = ROPE_THETA ** (-jnp.arange(0, MLA_ROPE, 2, dtype=F32) / MLA_ROPE)
    inv128 = jnp.concatenate([jnp.zeros((64,), F32), inv, inv, jnp.zeros((32,), F32)]).reshape(1, 128)
    fg2 = fg.reshape(1, D_MODEL)

    wa, wq2, wkv = _prepare_weights(w_in_b, w_uq_b, w_ukv_b)

    zqkv, gates, qf, kf, v, qs, kd, vd, rope = _pre_call(x2d, pos_col, mod3, b_ada, ng, qg, kvg, inv128, wa, wq2, wkv, seq)
    o_mla, lse_mla = _mla_fwd_call(qf, kf, v, n_seq, seq)
    o_swa, lse_swa = _swa_fwd_call(qs, kd, vd, pos_col, pos_row, sinks, n_seq, seq)
    dx2, do, dg, g_out, g_fg, dgate, loss = _post_call(x2d, t2d, o_mla, o_swa, gates, mod3, b_ada, fg2, w_out_b, seq)
    dqf, dkf, dv = _mla_bwd_call(qf, kf, v, do, o_mla, lse_mla, n_seq, seq)
    dqs, dkd, dvd, dsink = _swa_bwd_call(qs, kd, vd, do, o_swa, lse_swa, pos_col, pos_row, sinks, n_seq, seq)
    dz, g_wq2, g_wkv, g_qg, g_kvg = _mid_bwd_call(dqf, dkf, dv, zqkv, rope, qg, kvg, wq2, wkv, seq)
    gx, g_wa, g_ng, dshift, dscale = _in_bwd_call(x2d, dx2, dz, dg, dqs, dkd, dvd, mod3, b_ada, ng, wa, seq)
    g_in, g_uq, g_ukv = _restore_grads(g_wa, g_wq2, g_wkv)
    dmod = jnp.concatenate([dshift, dscale, dgate], axis=-1).reshape(n_seq, 3 * D_MODEL)
    small_row = jnp.concatenate([g_ng, g_fg, g_qg, g_kvg, jnp.pad(jnp.sum(dsink, axis=1).reshape(1, N_HEADS), ((0, 0), (0, 120))),
                                 loss, jnp.zeros((1, 128), F32)], axis=1)
    return gx.reshape(x.shape), (g_in, g_uq, g_ukv, g_out), small_row, dmod


def kernel(x, c, positions, w_ada, b_ada, norm_gain, w_in, q_norm_gain, kv_norm_gain, w_uq, w_ukv, swa_sinks, w_out, final_gain, loss_target, m_w_ada, m_b_ada, m_norm_gain, m_w_in, m_q_norm_gain, m_kv_norm_gain, m_w_uq, m_w_ukv, m_swa_sinks, m_w_out, m_final_gain, v_w_ada, v_b_ada, v_norm_gain, v_w_in, v_q_norm_gain, v_kv_norm_gain, v_w_uq, v_w_ukv, v_swa_sinks, v_w_out, v_final_gain):
    n_seq = x.shape[0]
    xi, yi, ci = lax.axis_index("x"), lax.axis_index("y"), lax.axis_index("c")
    dev = 4 * xi + 2 * yi + ci
    chip = 2 * xi + yi

    halves = lambda w: w.astype(BF16).reshape(2, w.shape[0] // 2, w.shape[1])
    c_blk = jnp.pad(c, ((0, ROWS_PER_DEVICE - n_seq), (0, 0)))
    act_all, pieces, f_in, f_uq, f_ukv, f_out = _comm_fwd_call(
        c_blk, w_ada[0], [halves(w_in[0]), halves(w_uq[0]), halves(w_ukv[0]), halves(w_out[0])])
    mine = lax.dynamic_slice_in_dim(pieces, dev * ROWS_PER_DEVICE, n_seq, axis=1)
    mod_rows = jnp.transpose(mine, (1, 0, 2)).reshape(n_seq, 3 * D_MODEL)
    cols = lambda t, r: jnp.transpose(t.reshape(4, r, -1), (1, 0, 2)).reshape(r, -1)
    w_in_blocks = [f_in[k].reshape(D_MODEL, -1) for k in range(4)]
    w_uq_b, w_ukv_b = cols(f_uq, Q_LORA), cols(f_ukv, KV_LORA)
    w_out_b = f_out.reshape(D_MODEL, D_MODEL)

    gx, (g_in_blocks, g_uq, g_ukv, g_out), small_row, dmod = _local_step(
        x, positions, loss_target, mod_rows, b_ada, norm_gain, q_norm_gain, kv_norm_gain, swa_sinks, final_gain,
        w_in_blocks, w_uq_b, w_ukv_b, w_out_b)

    by_owner = lambda g, n: jnp.transpose(g.reshape(g.shape[0], 4, n), (1, 0, 2)).reshape(4, 2, g.shape[0] // 2, n)
    grads = [jnp.stack(g_in_blocks).reshape(4, 2, D_MODEL // 2, -1), by_owner(g_uq, 192), by_owner(g_ukv, 256),
             g_out.reshape(4, 2, 128, D_MODEL)]
    part = jnp.concatenate([dmod, small_row, jnp.zeros((ROWS_PER_DEVICE - n_seq - 1, 3 * D_MODEL), F32)], axis=0)
    r_in, r_uq, r_ukv, r_out, parts_all = _comm_bwd_call(grads, part)
    g_in_s, g_uq_s = r_in.reshape(w_in.shape[1:]), r_uq.reshape(w_uq.shape[1:])
    g_ukv_s, g_out_s = r_ukv.reshape(w_ukv.shape[1:]), r_out.reshape(w_out.shape[1:])

    tr = lambda a: jnp.swapaxes(a[0], 0, 1)
    back = lambda ts: [jnp.swapaxes(t, 0, 1) for t in ts]
    d_in, nm_in, nv_in = back(_adam_call("adam_w_in", tr(w_in), g_in_s.T, tr(m_w_in), tr(v_w_in)))
    d_uq, nm_uq, nv_uq = back(_adam_call("adam_w_uq", tr(w_uq), g_uq_s.T, tr(m_w_uq), tr(v_w_uq)))
    d_ukv, nm_ukv, nv_ukv = _adam_call("adam_w_ukv", w_ukv[0], g_ukv_s, m_w_ukv[0], v_w_ukv[0])
    d_out, nm_out, nv_out = _adam_call("adam_w_out", w_out[0], g_out_s, m_w_out[0], v_w_out[0])
    dmod_cols = lax.dynamic_slice_in_dim(parts_all, chip * 768, 768, axis=1)
    g_ada, d_ada, nm_ada, nv_ada = _ada_bwd_call(act_all, dmod_cols, w_ada[0], m_w_ada[0], v_w_ada[0])

    row = lambda t: t.reshape(1, -1)
    small = {"b_ada": (b_ada, m_b_ada, v_b_ada), "norm_gain": (norm_gain, m_norm_gain, v_norm_gain),
             "q_norm_gain": (q_norm_gain, m_q_norm_gain, v_q_norm_gain),
             "kv_norm_gain": (kv_norm_gain, m_kv_norm_gain, v_kv_norm_gain),
             "swa_sinks": (swa_sinks, m_swa_sinks, v_swa_sinks),
             "final_gain": (row(final_gain), row(m_final_gain), row(v_final_gain))}
    res, loss_row = _small_call(parts_all, n_seq, [small[name] for name in SMALL_ORDER])
    res = dict(zip(SMALL_ORDER, res))
    res["final_gain"] = [t.reshape(-1) for t in res["final_gain"]]
    e = lambda t: t[None]
    big = {"w_ada": (e(g_ada), e(d_ada), e(nm_ada), e(nv_ada)), "w_in": (e(g_in_s), e(d_in), e(nm_in), e(nv_in)),
           "w_uq": (e(g_uq_s), e(d_uq), e(nm_uq), e(nv_uq)), "w_ukv": (e(g_ukv_s), e(d_ukv), e(nm_ukv), e(nv_ukv)),
           "w_out": (e(g_out_s), e(d_out), e(nm_out), e(nv_out))}
    order = ("w_ada", "b_ada", "norm_gain", "w_in", "q_norm_gain", "kv_norm_gain", "w_uq", "w_ukv", "swa_sinks", "w_out",
             "final_gain")
    pick = lambda kind: [(big[n] if n in big else res[n])[kind] for n in order]
    return (loss_row[0, 0], gx, *pick(0), *pick(1), *pick(2), *pick(3))
```

```python
import jax
import jax.numpy as jnp
from jax import lax
from jax.experimental import pallas as pl
from jax.experimental.pallas import tpu as pltpu

F32 = jnp.float32
BF16 = jnp.bfloat16

D_MODEL = 1024
Q_LORA = 384
KV_LORA = 256
N_HEADS = 8
MLA_NOPE = 64
MLA_ROPE = 32
HEAD_LANES = 128
HALF = 64
SWA_WINDOW = 128
EPS = 1e-6
ROPE_THETA = 10000.0
MLA_SCALE = (MLA_NOPE + MLA_ROPE) ** -0.5
LOG2E = 1.4426950408889634
LN2 = 0.6931471805599453
SWA_SCALE = 64 ** -0.5
NEG = -1e30

ADAM_LR = 0.001
ADAM_B1 = 0.9
ADAM_B2 = 0.999
ADAM_EPS = 1e-08
ADAM_WD = 0.01
ADAM_STEP = 10

A_ZQ, A_ZKV, A_KR, A_GM, A_QS, A_KS, A_VS, A_GS, A_END = 0, 384, 640, 768, 1280, 1792, 1920, 2048, 2560
IN_SPLITS = (384, 256, 32, 512, 512, 128, 128, 512)
D_IN = sum(IN_SPLITS)

TOKEN_TILE = 512
ATT_TILE = 256
VMEM_LIMIT = 56 * 1024 * 1024


def _dot(a, b):
    return jnp.dot(a, b, preferred_element_type=F32)


def _dot_nt(a, b):
    return lax.dot_general(a, b, (((1,), (1,)), ((), ())), preferred_element_type=F32)


def _dot_tn(a, b):
    return lax.dot_general(a, b, (((0,), (0,)), ((), ())), preferred_element_type=F32)


def _params(n_grid):
    return pltpu.CompilerParams(dimension_semantics=("arbitrary",) * n_grid, vmem_limit_bytes=VMEM_LIMIT)


def _full(shape):
    nd = len(shape)
    return pl.BlockSpec(shape, lambda *_: (0,) * nd, pipeline_mode=pl.Buffered(1))


def _sigmoid(g):
    return 1.0 / (1.0 + jnp.exp(-g))


SUB_TILE = 256


def _sub_tiles(tm):
    sub = min(SUB_TILE, tm)
    return [slice(s * sub, (s + 1) * sub) for s in range(tm // sub)]


MESH = pl.DeviceIdType.MESH
ROWS_PER_DEVICE = 8
VMEM_SPEC = pl.BlockSpec(memory_space=pltpu.VMEM)
ANY_SPEC = pl.BlockSpec(memory_space=pl.ANY)


def _position():
    x, y, c = lax.axis_index("x"), lax.axis_index("y"), lax.axis_index("c")
    sibling = (x, y, 1 - c)
    others = [(1 - x, y, c), (x, 1 - y, c), (1 - x, 1 - y, c)]
    return (x, y, c), 4 * x + 2 * y + c, 2 * x + y, sibling, others


def _rows_of(dev):
    return pl.ds(pl.multiple_of(dev * ROWS_PER_DEVICE, ROWS_PER_DEVICE), ROWS_PER_DEVICE)


def _all_to_all_rows(block_ref, table_ref, dev, me, send_sems, recv_sems):
    x, y, c = me
    waits = []
    for k in range(1, 8):
        peer = (1 - x if k & 4 else x, 1 - y if k & 2 else y, 1 - c if k & 1 else c)
        pltpu.make_async_remote_copy(src_ref=block_ref, dst_ref=table_ref.at[_rows_of(dev)], send_sem=send_sems.at[k - 1],
                                     recv_sem=recv_sems.at[k - 1], device_id=peer, device_id_type=MESH).start()
        waits.append(pltpu.make_async_remote_copy(
            src_ref=block_ref, dst_ref=table_ref.at[_rows_of(jnp.bitwise_xor(dev, k))], send_sem=send_sems.at[k - 1],
            recv_sem=recv_sems.at[k - 1], device_id=peer, device_id_type=MESH))
    return waits


def _comm_fwd_call(c_blk, w_ada, shards):
    n = len(shards)

    def body(c_ref, wada_ref, *refs):
        w_refs, act_ref, pieces_ref, full_refs = refs[:n], refs[n], refs[n + 1], refs[n + 2:2 * n + 2]
        c_all_ref = refs[2 * n + 2]
        c_send, c_recv, p_send, p_recv, w_send, w_recv, f_send, f_recv, loc_sem = refs[2 * n + 3:]
        me, dev, chip, sibling, others = _position()
        core = me[2]
        chip_of = [2 * p[0] + p[1] for p in others]

        local = [pltpu.make_async_copy(w_refs[i], full_refs[i].at[chip], loc_sem.at[i]) for i in range(n)]
        for cp in local:
            cp.start()

        def over_ici(i, j, src_chip):
            return pltpu.make_async_remote_copy(
                src_ref=w_refs[i].at[core], dst_ref=full_refs[i].at[src_chip, core], send_sem=w_send.at[3 * i + j],
                recv_sem=w_recv.at[3 * i + j], device_id=others[j], device_id_type=MESH)

        def to_sibling(i, j, half):
            return pltpu.make_async_remote_copy(
                src_ref=full_refs[i].at[chip_of[j], half], dst_ref=full_refs[i].at[chip_of[j], half],
                send_sem=f_send.at[3 * i + j], recv_sem=f_recv.at[3 * i + j], device_id=sibling, device_id_type=MESH)

        c_all_ref[_rows_of(dev), :] = c_ref[...]
        c_waits = _all_to_all_rows(c_ref, c_all_ref, dev, me, c_send, c_recv)
        sent = [over_ici(i, j, chip) for i in range(n) for j in range(3)]
        for cp in sent:
            cp.start()

        for cp in c_waits:
            cp.wait()
        cv = c_all_ref[...]
        act = cv * _sigmoid(cv)
        act_ref[...] = act
        pieces_ref[chip] = _dot(act.astype(BF16), wada_ref[...].astype(BF16))
        piece = lambda j, src_chip: pltpu.make_async_remote_copy(
            src_ref=pieces_ref.at[chip], dst_ref=pieces_ref.at[src_chip], send_sem=p_send.at[j], recv_sem=p_recv.at[j],
            device_id=others[j], device_id_type=MESH)
        for j in range(3):
            piece(j, chip).start()

        for i in range(n):
            for j in range(3):
                over_ici(i, j, chip_of[j]).wait_recv()
                to_sibling(i, j, core).start()
        for j in range(3):
            piece(j, chip).wait_send()
            piece(j, chip_of[j]).wait_recv()
        for i in range(n):
            for j in range(3):
                to_sibling(i, j, 1 - core).wait_recv()
                to_sibling(i, j, core).wait_send()
        for cp in sent:
            cp.wait_send()
        for cp in local:
            cp.wait()

    rows = 8 * ROWS_PER_DEVICE
    dma = pltpu.SemaphoreType.DMA
    return pl.pallas_call(
        body, name="comm_fwd",
        out_shape=[jax.ShapeDtypeStruct((rows, D_MODEL), F32), jax.ShapeDtypeStruct((4, rows, w_ada.shape[1]), F32)]
        + [jax.ShapeDtypeStruct((4,) + s.shape, s.dtype) for s in shards],
        in_specs=[VMEM_SPEC, VMEM_SPEC] + [ANY_SPEC] * n,
        out_specs=[VMEM_SPEC, VMEM_SPEC] + [ANY_SPEC] * n,
        scratch_shapes=[pltpu.VMEM((rows, D_MODEL), F32), dma((7,)), dma((7,)), dma((3,)), dma((3,)),
                        dma((3 * n,)), dma((3 * n,)), dma((3 * n,)), dma((3 * n,)), dma((n,))],
        compiler_params=pltpu.CompilerParams(vmem_limit_bytes=VMEM_LIMIT),
    )(c_blk, w_ada, *shards)


def _comm_bwd_call(grads, part):
    n = len(grads)

    def body(part_ref, *refs):
        g_refs, f_refs, parts_ref = refs[:n], refs[n:2 * n], refs[2 * n]
        scratch = refs[2 * n + 1:]
        a_refs, b_refs, p_refs, r_refs = (scratch[k * n:(k + 1) * n] for k in range(4))
        s_send, s_recv, d_send, d_recv, e_send, e_recv, h_send, h_recv, loc_sem = scratch[4 * n:]
        me, dev, chip, sibling, others = _position()
        core = me[2]
        chip_of = [2 * p[0] + p[1] for p in others]

        parts_ref[_rows_of(dev), :] = part_ref[...]
        s_waits = _all_to_all_rows(part_ref, parts_ref, dev, me, s_send, s_recv)

        mine = [pltpu.make_async_copy(g_refs[i].at[:, core], a_refs[i], loc_sem.at[i]) for i in range(n)]
        swap = [pltpu.make_async_remote_copy(src_ref=g_refs[i].at[:, 1 - core], dst_ref=b_refs[i], send_sem=d_send.at[i],
                                             recv_sem=d_recv.at[i], device_id=sibling, device_id_type=MESH) for i in range(n)]
        order = sorted(range(n), key=lambda i: g_refs[i].shape[2] * g_refs[i].shape[3])
        for i in order:
            mine[i].start()
            swap[i].start()
        cross = [pltpu.make_async_remote_copy(src_ref=p_refs[i].at[chip_of[j]], dst_ref=r_refs[i].at[j],
                                              send_sem=e_send.at[3 * i + j], recv_sem=e_recv.at[3 * i + j],
                                              device_id=others[j], device_id_type=MESH) for i in range(n) for j in range(3)]
        for i in order:
            mine[i].wait()
            swap[i].wait()
            for k in range(4):
                s = a_refs[i][k] + b_refs[i][k]
                a_refs[i][k] = s
                p_refs[i][k] = s.astype(BF16)
            for j in range(3):
                cross[3 * i + j].start()
        share = {}
        for i in order:
            for j in range(3):
                cross[3 * i + j].wait()
            f_refs[i][core] = (a_refs[i][chip] + r_refs[i][0].astype(F32) + r_refs[i][1].astype(F32)
                               + r_refs[i][2].astype(F32))
            share[i] = pltpu.make_async_remote_copy(src_ref=f_refs[i].at[core], dst_ref=f_refs[i].at[core],
                                                    send_sem=h_send.at[i], recv_sem=h_recv.at[i], device_id=sibling,
                                                    device_id_type=MESH)
            share[i].start()
        for i in range(n):
            share[i].wait_send()
            pltpu.make_async_remote_copy(src_ref=f_refs[i].at[core], dst_ref=f_refs[i].at[1 - core], send_sem=h_send.at[i],
                                         recv_sem=h_recv.at[i], device_id=sibling, device_id_type=MESH).wait_recv()
        for cp in s_waits:
            cp.wait()

    rows = 8 * ROWS_PER_DEVICE
    dma = pltpu.SemaphoreType.DMA
    quarter = [(4,) + g.shape[2:] for g in grads]
    return pl.pallas_call(
        body, name="comm_bwd",
        out_shape=[jax.ShapeDtypeStruct((2,) + g.shape[2:], F32) for g in grads]
        + [jax.ShapeDtypeStruct((rows, part.shape[1]), F32)],
        in_specs=[VMEM_SPEC] + [ANY_SPEC] * n,
        out_specs=[VMEM_SPEC] * (n + 1),
        scratch_shapes=[pltpu.VMEM(q, F32) for q in quarter] + [pltpu.VMEM(q, F32) for q in quarter]
        + [pltpu.VMEM(q, BF16) for q in quarter] + [pltpu.VMEM((3,) + q[1:], BF16) for q in quarter]
        + [dma((7,)), dma((7,)), dma((n,)), dma((n,)), dma((3 * n,)), dma((3 * n,)), dma((n,)), dma((n,)), dma((n,))],
        compiler_params=pltpu.CompilerParams(vmem_limit_bytes=VMEM_LIMIT),
    )(part, *grads)


def _twice(t):
    lo = _lane_lo()
    other = pltpu.roll(t, HALF, 1)
    return jnp.concatenate([jnp.where(lo, t, other), jnp.where(lo, other, t)], axis=1)


def _once(g):
    first, second = g[:, :HEAD_LANES], g[:, HEAD_LANES:]
    return jnp.where(_lane_lo(), first + pltpu.roll(first, HALF, 1), second + pltpu.roll(second, HALF, 1))


def _rope_tables(pos_col, inv_row):
    ang = pos_col * inv_row
    return jnp.cos(ang), jnp.sin(ang)


def _pre_call(x, pos_col, mod, b_ada, ng, qg, kvg, inv128, wa, wq2, wkv, seq):
    n_tok = x.shape[0]
    tm = min(TOKEN_TILE, seq)
    per_seq = seq // tm

    def body(x_ref, pos_ref, mod_ref, bada_ref, ng_ref, qg_ref, kvg_ref, inv_ref, wa_ref, wq_ref, wkv_ref,
             zqkv_ref, gates_ref, qf_ref, kf_ref, v_ref, qs_ref, kd_ref, vd_ref, rope_ref):
        xv = x_ref[...]
        modv = mod_ref[0] + bada_ref[...]
        shift, scale = modv[:, :D_MODEL], modv[:, D_MODEL:2 * D_MODEL]
        r1 = lax.rsqrt(jnp.mean(xv * xv, axis=-1, keepdims=True) + EPS)
        h = ((xv * r1) * ng_ref[...]) * (1.0 + scale) + shift
        hb = h.astype(BF16)
        za = _dot(hb, wa_ref[...])
        zkr = za[:, A_KR:A_GM]
        cos, sin = _rope_tables(pos_ref[...], inv_ref[...])
        rope_ref[:, :HEAD_LANES] = cos
        rope_ref[:, HEAD_LANES:] = sin
        zqkv_ref[...] = za[:, :A_KR]
        gates_ref[:, :512] = za[:, A_GM:A_QS]
        gates_ref[:, 512:] = za[:, A_GS:A_END]
        qs_ref[...] = (za[:, A_QS:A_KS] * (SWA_SCALE * LOG2E)).astype(BF16)
        kd_ref[...] = _twice(za[:, A_KS:A_VS]).astype(BF16)
        vd_ref[...] = _twice(za[:, A_VS:A_GS]).astype(BF16)
        zq, zkv = za[:, A_ZQ:A_ZKV], za[:, A_ZKV:A_KR]
        rq = lax.rsqrt(jnp.mean(zq * zq, axis=-1, keepdims=True) + EPS)
        qn = ((zq * rq) * qg_ref[...]).astype(BF16)
        qr = _dot(qn, wq_ref[...])
        cf, sf = jnp.tile(cos, (1, N_HEADS)), jnp.tile(sin, (1, N_HEADS))
        qf_ref[...] = ((qr[:, :1024] * cf + qr[:, 1024:] * sf) * (MLA_SCALE * LOG2E)).astype(BF16)
        rkv = lax.rsqrt(jnp.mean(zkv * zkv, axis=-1, keepdims=True) + EPS)
        kvn = ((zkv * rkv) * kvg_ref[...]).astype(BF16)
        kv = _dot(kvn, wkv_ref[...])
        kpe = jnp.where(_lane_lo(), 0.0, zkr * cos) + pltpu.roll(zkr, HALF, 1) * sin
        kf_ref[...] = (kv[:, :1024] + jnp.tile(kpe, (1, N_HEADS))).astype(BF16)
        v_ref[...] = kv[:, 1024:].astype(BF16)

    tok = lambda w: pl.BlockSpec((tm, w), lambda i: (i, 0))
    outs = [(640, F32), (1024, F32), (1024, BF16), (1024, BF16), (512, BF16), (512, BF16), (256, BF16), (256, BF16),
            (2 * HEAD_LANES, F32)]
    return pl.pallas_call(
        body, name="pre", grid=(n_tok // tm,),
        out_shape=[jax.ShapeDtypeStruct((n_tok, w), dt) for w, dt in outs],
        in_specs=[tok(D_MODEL), tok(1), pl.BlockSpec((1, 1, 3 * D_MODEL), lambda i: (i // per_seq, 0, 0)),
                  _full(b_ada.shape), _full(ng.shape), _full(qg.shape), _full(kvg.shape), _full(inv128.shape),
                  _full(wa.shape), _full(wq2.shape), _full(wkv.shape)],
        out_specs=[tok(w) for w, _ in outs],
        compiler_params=_params(1),
    )(x, pos_col, mod, b_ada, ng, qg, kvg, inv128, wa, wq2, wkv)


def _lane_lo(width=HEAD_LANES):
    return lax.broadcasted_iota(jnp.int32, (1, width), 1) < HALF


def _eye(n=HEAD_LANES):
    r = lax.broadcasted_iota(jnp.int32, (n, n), 0)
    c = lax.broadcasted_iota(jnp.int32, (n, n), 1)
    return jnp.where(r == c, 1.0, 0.0).astype(BF16)


def _mla_fwd_call(qf, kf, v, n_seq, seq):
    tq = min(ATT_TILE, seq)
    nq = seq // tq

    ext = HALF + 16

    def body(q_ref, k_ref, v_ref, o_ref, lse_ref, vt_ref, acc_ref):
        i = pl.program_id(1)
        eye = _eye()

        @pl.when(i == 0)
        def _():
            for h in range(N_HEADS):
                vt_ref[h * ext + HALF:(h + 1) * ext, :] = jnp.ones((16, seq), BF16)
            for t in range(nq):
                for p in range(N_HEADS // 2):
                    pair = slice(p * HEAD_LANES, (p + 1) * HEAD_LANES)
                    v_t = _dot_nt(eye, v_ref[t * tq:(t + 1) * tq, pair]).astype(BF16)
                    for hh in range(2):
                        r0 = (2 * p + hh) * ext
                        vt_ref[r0:r0 + HALF, t * tq:(t + 1) * tq] = v_t[hh * HALF:(hh + 1) * HALF, :]

        q = q_ref[...]
        qcol = i * tq + lax.broadcasted_iota(jnp.int32, (1, tq), 1)
        heads = range(N_HEADS)
        lanes = [slice(h * HEAD_LANES, (h + 1) * HEAD_LANES) for h in heads]

        def make_step(masked, n_tiles):
            def step(kt0, carry):
                tiles = range(n_tiles)
                start = pl.multiple_of(kt0 * tq, tq)
                ks = [k_ref[pl.ds(pl.multiple_of((kt0 + t) * tq, tq), tq), :] for t in tiles]
                vt = vt_ref[:, pl.ds(start, n_tiles * tq)]
                last = n_tiles - 1
                if masked:
                    keep = ((kt0 + last) * tq + lax.broadcasted_iota(jnp.int32, (tq, 1), 0)) <= qcol

                def scores(h):
                    sts = [_dot_nt(ks[t][:, lanes[h]], q[:, lanes[h]]) for t in tiles]
                    if masked:
                        sts[last] = jnp.where(keep, sts[last], NEG)
                    return sts

                def softmax(h, sts):
                    m_old = carry[h]
                    m_new = m_old
                    for st in sts:
                        m_new = jnp.maximum(m_new, jnp.max(st, axis=0, keepdims=True))
                    pt = jnp.concatenate([jnp.exp2(st - m_new).astype(BF16) for st in sts], axis=0)
                    return m_new, jnp.exp2(m_old - m_new), pt

                def values(h, alpha, pt):
                    rows = slice(h * ext, (h + 1) * ext)
                    acc_ref[rows, :] = acc_ref[rows, :] * alpha + _dot(vt[rows, :], pt)

                sts, soft, out = {0: scores(0), 1: scores(1)}, {}, {}
                for h in range(N_HEADS + 1):
                    if h + 2 < N_HEADS:
                        sts[h + 2] = scores(h + 2)
                    if h < N_HEADS:
                        soft[h] = softmax(h, sts.pop(h))
                    if h >= 1:
                        m_new, alpha, pt = soft.pop(h - 1)
                        values(h - 1, alpha, pt)
                        out[h - 1] = m_new
                return tuple(out[h] for h in heads)
            return step

        acc_ref[...] = jnp.zeros_like(acc_ref)
        init = (jnp.full((1, tq), NEG, F32),) * N_HEADS
        count = i + 1
        carry = lax.fori_loop(0, (count + 1) // 2 - 1, lambda j, c: make_step(False, 2)(2 * j, c), init)
        carry = lax.cond(count % 2 == 0, lambda c: make_step(True, 2)(i - 1, c), lambda c: make_step(True, 1)(i, c), carry)
        dens = [acc_ref[h * ext + HALF:h * ext + HALF + 1, :] for h in heads]
        acc_t = jnp.concatenate([acc_ref[h * ext:h * ext + HALF, :] * (1.0 / dens[h]) for h in heads], axis=0)
        o_ref[...] = acc_t.T
        for h in heads:
            lse_ref[0, h // 4, h % 4:h % 4 + 1, :] = carry[h] + jnp.log2(dens[h])

    n_tok = qf.shape[0]
    return pl.pallas_call(
        body, name="mla_fwd", grid=(n_seq, nq),
        out_shape=[jax.ShapeDtypeStruct((n_tok, 512), F32), jax.ShapeDtypeStruct((n_seq, 2, 4, seq), F32)],
        in_specs=[pl.BlockSpec((tq, 1024), lambda b, i: (b * nq + i, 0)),
                  pl.BlockSpec((seq, 1024), lambda b, i: (b, 0)),
                  pl.BlockSpec((seq, 512), lambda b, i: (b, 0))],
        out_specs=[pl.BlockSpec((tq, 512), lambda b, i: (b * nq + i, 0)),
                   pl.BlockSpec((1, 2, 4, tq), lambda b, i: (b, 0, 0, i))],
        scratch_shapes=[pltpu.VMEM((N_HEADS * ext, seq), BF16), pltpu.VMEM((N_HEADS * ext, tq), F32)],
        compiler_params=_params(2),
    )(qf, kf, v)


def _mla_bwd_call(qf, kf, v, do, delta, lse, n_seq, seq):
    tq = min(ATT_TILE, seq)
    nq = seq // tq

    nh = 4
    heads = range(nh)
    lanes = [slice(h * HEAD_LANES, (h + 1) * HEAD_LANES) for h in heads]

    def body(q_ref, k_ref, v_ref, do_ref, dl_ref, lse_ref, dq_ref, dk_ref, dv_ref,
             kt_ref, dot_ref, dqt_ref, dvt_ref):
        eye = _eye()
        sub_lo = lax.broadcasted_iota(jnp.int32, (HEAD_LANES, 1), 0) < HALF

        for t in range(nq):
            r = slice(t * tq, (t + 1) * tq)
            kv = k_ref[r, :]
            for h in heads:
                kt_ref[lanes[h], r] = _dot_nt(eye, kv[:, lanes[h]]).astype(BF16)
            for p in range(nh // 2):
                dov = do_ref[r, lanes[p]]
                dt = _dot_nt(eye, dov)
                dot_ref[2 * p, :, r] = jnp.where(sub_lo, dt, 0.0).astype(BF16)
                dot_ref[2 * p + 1, :, r] = jnp.where(sub_lo, 0.0, dt).astype(BF16)
        dqt_ref[...] = jnp.zeros_like(dqt_ref)
        dvt_ref[...] = jnp.zeros_like(dvt_ref)

        def flush_dv(tile, which):
            rows = pl.ds(pl.multiple_of(tile * tq, tq), tq)
            for p in range(nh // 2):
                dv_ref[rows, lanes[p]] = dvt_ref[which, p * HEAD_LANES:(p + 1) * HEAD_LANES, :].T

        def k_step(kt, _):
            slot = kt % 2
            kr = pl.ds(pl.multiple_of(kt * tq, tq), tq)
            k = k_ref[kr, :]
            vv = v_ref[kr, :]
            k_t = kt_ref[:, kr]
            krow = kt * tq + lax.broadcasted_iota(jnp.int32, (tq, 1), 0)

            def make_step(masked, n_tiles):
                def q_step(qt0, carry):
                    tiles = range(n_tiles)
                    qrs = [pl.ds(pl.multiple_of((qt0 + t) * tq, tq), tq) for t in tiles]
                    if masked:
                        flush_dv(jnp.maximum(kt - 1, 0), 1 - slot)
                    qs = [q_ref[qr, :] for qr in qrs]
                    if masked:
                        keep = krow <= (qt0 * tq + lax.broadcasted_iota(jnp.int32, (1, tq), 1))

                    def scores(h):
                        do_ts = [dot_ref[h, :, qr] for qr in qrs]
                        sts = [_dot_nt(k[:, lanes[h]], qs[t][:, lanes[h]]) for t in tiles]
                        dpts = [_dot(vv[:, lanes[h // 2]], do_ts[t]) for t in tiles]
                        return do_ts, sts, dpts

                    def softmax(h, sts, dpts):
                        pts, dsts = [], []
                        for t in tiles:
                            pt = jnp.exp2(sts[t] - lse_ref[0, 0, h:h + 1, qrs[t]])
                            if masked and t == 0:
                                pt = jnp.where(keep, pt, 0.0)
                            dsts.append((pt * (dpts[t] - dl_ref[0, h:h + 1, qrs[t]])).astype(BF16))
                            pts.append(pt.astype(BF16))
                        return pts, dsts

                    def grads(h, do_ts, pts, dsts):
                        half = slice((h % 2) * HALF, (h % 2 + 1) * HALF)
                        dst_all = jnp.concatenate(dsts, axis=1)
                        pt_all = jnp.concatenate(pts, axis=1)
                        do_all = jnp.concatenate([do_ts[t][half, :] for t in tiles], axis=1)
                        q_all = jnp.concatenate([qs[t][:, lanes[h]] for t in tiles], axis=0)
                        dvt_ref[slot, h * HALF:(h + 1) * HALF, :] += _dot_nt(do_all, pt_all)
                        dk_ref[kr, lanes[h]] += _dot(dst_all, q_all)
                        for t in tiles:
                            dqt_ref[lanes[h], qrs[t]] += _dot(k_t[lanes[h], :], dsts[t])

                    first, second = {0: scores(0)}, {}
                    for h in range(nh + 1):
                        if h + 1 < nh:
                            first[h + 1] = scores(h + 1)
                        if h < nh:
                            do_ts, sts, dpts = first.pop(h)
                            second[h] = (do_ts,) + softmax(h, sts, dpts)
                        if h >= 1:
                            grads(h - 1, *second.pop(h - 1))
                    return carry
                return q_step

            dk_ref[kr, :] = jnp.zeros((tq, nh * HEAD_LANES), F32)
            dvt_ref[slot] = jnp.zeros(dvt_ref.shape[1:], F32)
            count = nq - kt
            lax.cond(count >= 2, lambda c: make_step(True, 2)(kt, c), lambda c: make_step(True, 1)(kt, c), 0)
            lax.fori_loop(1, count // 2, lambda j, c: make_step(False, 2)(kt + 2 * j, c), 0)
            lax.cond(jnp.logical_and(count % 2 == 1, count >= 3), lambda c: make_step(False, 1)(nq - 1, c), lambda c: c, 0)
            return 0

        lax.fori_loop(0, nq, k_step, 0)
        flush_dv(nq - 1, (nq - 1) % 2)
        for t in range(nq):
            r = slice(t * tq, (t + 1) * tq)
            for h in heads:
                dq_ref[r, lanes[h]] = dqt_ref[lanes[h], r].T

    n_tok = qf.shape[0]
    groups = N_HEADS // nh
    blk = lambda w: pl.BlockSpec((seq, w), lambda b, g: (b, g))
    return pl.pallas_call(
        body, name="mla_bwd", grid=(n_seq, groups),
        out_shape=[jax.ShapeDtypeStruct((n_tok, 1024), F32), jax.ShapeDtypeStruct((n_tok, 1024), F32),
                   jax.ShapeDtypeStruct((n_tok, 512), F32)],
        in_specs=[blk(512), blk(512), blk(256), blk(256), pl.BlockSpec((1, nh, seq), lambda b, g: (g, 0, b)),
                  pl.BlockSpec((1, 1, nh, seq), lambda b, g: (b, g, 0, 0))],
        out_specs=[blk(512), blk(512), blk(256)],
        scratch_shapes=[pltpu.VMEM((nh * HEAD_LANES, seq), BF16), pltpu.VMEM((nh, HEAD_LANES, seq), BF16),
                        pltpu.VMEM((nh * HEAD_LANES, seq), F32), pltpu.VMEM((2, nh * HALF, tq), F32)],
        compiler_params=_params(2),
    )(qf, kf, v, do, delta, lse)


SWA_BLOCKS = 4


def _swa_block(n, pos_col_ref, posq):
    w = SWA_WINDOW
    start = pl.multiple_of(jnp.maximum(n - 1, 0) * w, w)
    posk = pos_col_ref[pl.ds(start, 2 * w), :]
    rel = (n * w + lax.broadcasted_iota(jnp.int32, (1, w), 1)) - (start + lax.broadcasted_iota(jnp.int32, (2 * w, 1), 0))
    valid = jnp.logical_and(rel >= 0, rel < w)
    return start, jnp.where(valid, posq - posk, 1e30)


def _alibi(h):
    return LOG2E * 2.0 ** -(h + 1)


def _transpose_rows(eye, src_ref, dst_ref, seq, width):
    step = 2 * SWA_WINDOW
    for t in range(seq // step):
        for p in range(width // HEAD_LANES):
            lanes = slice(p * HEAD_LANES, (p + 1) * HEAD_LANES)
            dst_ref[lanes, t * step:(t + 1) * step] = _dot_nt(eye, src_ref[t * step:(t + 1) * step, lanes]).astype(BF16)


def _swa_fwd_call(qs, kd, vd, pos_col, pos_row, sinks, n_seq, seq):
    w = SWA_WINDOW
    qb = SWA_BLOCKS
    steps = seq // (qb * w)
    ext = HALF + 16

    def body(q_ref, k_ref, v_ref, pc_ref, pr_ref, sink_ref, o_ref, lse_ref, vt_ref):
        n = pl.program_id(1)
        lo = _lane_lo()
        hi = jnp.logical_not(lo)
        eye = _eye()

        @pl.when(n == 0)
        def _():
            step = 2 * w
            for kv in range(2):
                vt_ref[kv * ext + HALF:(kv + 1) * ext, :] = jnp.ones((16, seq), BF16)
                for t in range(seq // step):
                    v_t = _dot_nt(eye, v_ref[t * step:(t + 1) * step, kv * HEAD_LANES:(kv + 1) * HEAD_LANES])
                    vt_ref[kv * ext:kv * ext + HALF, t * step:(t + 1) * step] = v_t[:HALF, :].astype(BF16)

        heads = range(N_HEADS)
        blocks = range(qb)
        geo = [_swa_block(n * qb + bi, pc_ref, pr_ref[bi]) for bi in blocks]
        wins = [pl.ds(g[0], 2 * w) for g in geo]
        kwins = [k_ref[win, :] for win in wins]
        vts = [vt_ref[:, win] for win in wins]
        sts = []
        for bi in blocks:
            q = q_ref[bi * w:(bi + 1) * w, :]
            sts.append([])
            for j in range(N_HEADS // 2):
                qp = q[:, j * HEAD_LANES:(j + 1) * HEAD_LANES]
                both = jnp.concatenate([jnp.where(lo, qp, jnp.zeros_like(qp)), jnp.where(hi, qp, jnp.zeros_like(qp))], axis=0)
                st = _dot_nt(kwins[bi][:, (j // 2) * HEAD_LANES:(j // 2 + 1) * HEAD_LANES], both)
                sts[bi] += [st[:, :w], st[:, w:]]
        ps, ms = [], []
        for bi in blocks:
            ps.append([])
            ms.append([])
            for h in heads:
                s = sts[bi][h] - _alibi(h) * geo[bi][1]
                m = jnp.maximum(jnp.max(s, axis=0, keepdims=True), sink_ref[0, h] * LOG2E)
                ps[bi].append(jnp.exp2(s - m).astype(BF16))
                ms[bi].append(m)
        for bi in blocks:
            ots = []
            for h in heads:
                pv = _dot(vts[bi][(h // 4) * ext:(h // 4 + 1) * ext, :], ps[bi][h])
                l = pv[HALF:HALF + 1, :] + jnp.exp2(sink_ref[0, h] * LOG2E - ms[bi][h])
                ots.append(pv[:HALF, :] * (1.0 / l))
                lse_ref[0, h:h + 1, bi * w:(bi + 1) * w] = ms[bi][h] + jnp.log2(l)
            o_ref[bi * w:(bi + 1) * w, :] = jnp.concatenate(ots, axis=0).T

    n_tok = qs.shape[0]
    tok = lambda width: pl.BlockSpec((qb * w, width), lambda b, n: (b * steps + n, 0))
    whole = lambda width: pl.BlockSpec((seq, width), lambda b, n: (b, 0))
    return pl.pallas_call(
        body, name="swa_fwd", grid=(n_seq, steps),
        out_shape=[jax.ShapeDtypeStruct((n_tok, 512), F32), jax.ShapeDtypeStruct((n_seq, N_HEADS, seq), F32)],
        in_specs=[tok(512), whole(256), whole(256), whole(1), pl.BlockSpec((qb, 1, w), lambda b, n: (b * steps + n, 0, 0)),
                  pl.BlockSpec(memory_space=pltpu.SMEM)],
        out_specs=[tok(512), pl.BlockSpec((1, N_HEADS, qb * w), lambda b, n: (b, 0, n))],
        scratch_shapes=[pltpu.VMEM((2 * ext, seq), BF16)],
        compiler_params=_params(2),
    )(qs, kd, vd, pos_col, pos_row, sinks)


def _swa_bwd_call(qs, kd, vd, do, delta, lse, pos_col, pos_row, sinks, n_seq, seq):
    w = SWA_WINDOW
    qb = SWA_BLOCKS
    steps = seq // (qb * w)

    def body(q_ref, k_ref, v_ref, do_ref, dl_ref, lse_ref, pc_ref, pr_ref, sink_ref, dq_ref, dk_ref, dv_ref, dsink_ref,
             kt_ref):
        b, n = pl.program_id(0), pl.program_id(1)
        lo = _lane_lo()
        hi = jnp.logical_not(lo)
        sub_lo = lax.broadcasted_iota(jnp.int32, (HEAD_LANES, 1), 0) < HALF
        eye = _eye()

        @pl.when(n == 0)
        def _():
            dk_ref[...] = jnp.zeros_like(dk_ref)
            dv_ref[...] = jnp.zeros_like(dv_ref)
            _transpose_rows(eye, k_ref, kt_ref, seq, 2 * HEAD_LANES)

        @pl.when(jnp.logical_and(n == 0, b == 0))
        def _():
            dsink_ref[...] = jnp.zeros_like(dsink_ref)

        heads = range(N_HEADS)
        blocks = range(qb)
        kv_lanes = lambda h: slice((h // 4) * HEAD_LANES, (h // 4 + 1) * HEAD_LANES)
        geo = [_swa_block(n * qb + bi, pc_ref, pr_ref[bi]) for bi in blocks]
        wins = [pl.ds(g[0], 2 * w) for g in geo]
        kwins = [k_ref[win, :] for win in wins]
        vwins = [v_ref[win, :] for win in wins]

        do_ts, deltas, qms, doms = [], [], [], []
        for bi in blocks:
            rows = slice(bi * w, (bi + 1) * w)
            for lst in (do_ts, deltas, qms, doms):
                lst.append([])
            for j in range(N_HEADS // 2):
                pair = slice(j * HEAD_LANES, (j + 1) * HEAD_LANES)
                dop = do_ref[rows, pair]
                qp = q_ref[rows, pair]
                dt = _dot_nt(eye, dop)
                for hh in range(2):
                    half = lo if hh == 0 else hi
                    do_ts[bi].append(jnp.where(sub_lo, dt, 0.0).astype(BF16) if hh == 0
                                     else jnp.where(sub_lo, 0.0, dt).astype(BF16))
                    deltas[bi].append(dl_ref[2 * j + hh:2 * j + hh + 1, rows])
                    qms[bi].append(jnp.where(half, qp, jnp.zeros_like(qp)))
                    doms[bi].append(jnp.where(half, dop, jnp.zeros_like(dop)))
        sts, dpts = [], []
        for bi in blocks:
            sts.append([])
            dpts.append([])
            for j in range(N_HEADS // 2):
                a, b = 2 * j, 2 * j + 1
                st = _dot_nt(kwins[bi][:, kv_lanes(a)], jnp.concatenate([qms[bi][a], qms[bi][b]], axis=0))
                dpt = _dot(vwins[bi][:, kv_lanes(a)], jnp.concatenate([do_ts[bi][a], do_ts[bi][b]], axis=1))
                sts[bi] += [st[:, :w], st[:, w:]]
                dpts[bi] += [dpt[:, :w], dpt[:, w:]]
        pts, dsts = [], []
        for bi in blocks:
            pts.append([])
            dsts.append([])
            for h in heads:
                lse_h = lse_ref[0, h:h + 1, bi * w:(bi + 1) * w]
                pt = jnp.exp2(sts[bi][h] - _alibi(h) * geo[bi][1] - lse_h)
                dsts[bi].append((pt * (dpts[bi][h] - deltas[bi][h])).astype(BF16))
                pts[bi].append(pt.astype(BF16))
                dsink_ref[h:h + 1, :] += -jnp.exp2(sink_ref[0, h] * LOG2E - lse_h) * deltas[bi][h]
        for bi in blocks:
            for kv in range(2):
                group = range(4 * kv, 4 * kv + 4)
                dst_all = jnp.concatenate([dsts[bi][h] for h in group], axis=1)
                pt_all = jnp.concatenate([pts[bi][h] for h in group], axis=1)
                q_all = jnp.concatenate([qms[bi][h] for h in group], axis=0)
                do_all = jnp.concatenate([doms[bi][h] for h in group], axis=0)
                dk_ref[wins[bi], kv_lanes(4 * kv)] += _dot(dst_all, q_all)
                dv_ref[wins[bi], kv_lanes(4 * kv)] += _dot(pt_all, do_all)
        for bi in blocks:
            ktw = kt_ref[:, wins[bi]]
            for j in range(N_HEADS // 2):
                k_t = ktw[kv_lanes(2 * j), :]
                both = _dot(k_t, jnp.concatenate([dsts[bi][2 * j], dsts[bi][2 * j + 1]], axis=1))
                dq_t = jnp.where(sub_lo, both[:, :w], both[:, w:])
                dq_ref[bi * w:(bi + 1) * w, j * HEAD_LANES:(j + 1) * HEAD_LANES] = dq_t.T * SWA_SCALE

    n_tok = qs.shape[0]
    tok = lambda width: pl.BlockSpec((qb * w, width), lambda b, n: (b * steps + n, 0))
    whole = lambda width: pl.BlockSpec((seq, width), lambda b, n: (b, 0))
    return pl.pallas_call(
        body, name="swa_bwd", grid=(n_seq, steps),
        out_shape=[jax.ShapeDtypeStruct((n_tok, 512), F32), jax.ShapeDtypeStruct((n_tok, 256), F32),
                   jax.ShapeDtypeStruct((n_tok, 256), F32), jax.ShapeDtypeStruct((N_HEADS, HEAD_LANES), F32)],
        in_specs=[tok(512), whole(256), whole(256), pl.BlockSpec((qb * w, 512), lambda b, n: (b * steps + n, 1)),
                  pl.BlockSpec((N_HEADS, qb * w), lambda b, n: (0, b * steps + n)),
                  pl.BlockSpec((1, N_HEADS, qb * w), lambda b, n: (b, 0, n)),
                  whole(1), pl.BlockSpec((qb, 1, w), lambda b, n: (b * steps + n, 0, 0)),
                  pl.BlockSpec(memory_space=pltpu.SMEM)],
        out_specs=[tok(512), whole(256), whole(256), _full((N_HEADS, HEAD_LANES))],
        scratch_shapes=[pltpu.VMEM((2 * HEAD_LANES, seq), BF16)],
        compiler_params=_params(2),
    )(qs, kd, vd, do, delta, lse, pos_col, pos_row, sinks)


def _post_call(x, target, o_mla, o_swa, gates, mod, b_ada, fg, w_out, seq):
    n_tok = x.shape[0]
    tm = min(TOKEN_TILE, seq)
    per_seq = seq // tm
    n_seq = n_tok // seq

    def body(x_ref, t_ref, om_ref, os_ref, g_ref, mod_ref, bada_ref, fg_ref, w_ref,
             dx2_ref, do_ref, dg_ref, gw_ref, gfg_ref, dgate_ref, loss_ref, dmla_ref, dswa_ref):
        i = pl.program_id(0)

        @pl.when(i == 0)
        def _():
            gw_ref[...] = jnp.zeros_like(gw_ref)
            gfg_ref[...] = jnp.zeros_like(gfg_ref)
            loss_ref[...] = jnp.zeros_like(loss_ref)

        @pl.when(i % per_seq == 0)
        def _():
            dgate_ref[...] = jnp.zeros_like(dgate_ref)

        gate = mod_ref[0][:, 2 * D_MODEL:] + bada_ref[:, 2 * D_MODEL:]
        fgv = fg_ref[...]
        fgd = fgv * (1.0 / D_MODEL)
        subs = _sub_tiles(tm)
        gs = [g_ref[r, :] for r in subs]
        os_ = [jnp.concatenate([om_ref[r, :], os_ref[r, :]], axis=-1) for r in subs]
        sgs = [_sigmoid(g) for g in gs]
        sils = [g * sg for g, sg in zip(gs, sgs)]
        ypres = [(o * sil).astype(BF16) for o, sil in zip(os_, sils)]
        ys = [_dot(ypre, w_ref[...]) for ypre in ypres]
        dys, loss, gfg, dgate = [], 0.0, 0.0, 0.0
        for r, y in zip(subs, ys):
            x2 = x_ref[r, :] + gate * y
            r2 = lax.rsqrt(jnp.mean(x2 * x2, axis=-1, keepdims=True) + EPS)
            xn2 = x2 * r2
            err = xn2 * fgv - t_ref[r, :]
            loss = loss + jnp.sum(jnp.sum(err * err, axis=-1, keepdims=True), axis=0, keepdims=True)
            gfg = gfg + jnp.sum(err * xn2, axis=0, keepdims=True)
            dxn2 = err * fgd
            dx2 = r2 * (dxn2 - xn2 * jnp.mean(dxn2 * xn2, axis=-1, keepdims=True))
            dx2_ref[r, :] = dx2
            dgate = dgate + jnp.sum(dx2 * y, axis=0, keepdims=True)
            dys.append((dx2 * gate).astype(BF16))
        loss_ref[...] += jnp.broadcast_to(loss * (0.5 / D_MODEL), loss_ref.shape)
        gfg_ref[...] += gfg * (1.0 / D_MODEL)
        dgate_ref[0] += dgate
        gw_ref[...] += _dot_tn(jnp.concatenate(ypres, axis=0), jnp.concatenate(dys, axis=0))
        dypres = [_dot_nt(dy, w_ref[...]) for dy in dys]
        pick = jnp.where(jnp.right_shift(lax.broadcasted_iota(jnp.int32, (2 * N_HEADS, D_MODEL), 1), 6)
                         == lax.broadcasted_iota(jnp.int32, (2 * N_HEADS, D_MODEL), 0), 1.0, 0.0).astype(BF16)
        for r, dypre, o, g, sg, sil in zip(subs, dypres, os_, gs, sgs, sils):
            dov = (dypre * sil).astype(BF16)
            do_ref[r, :] = dov
            delta = _dot_nt(pick, (dov.astype(F32) * o).astype(BF16))
            for grp in range(2):
                dmla_ref[grp, :, r] = delta[4 * grp:4 * grp + 4, :]
            dswa_ref[:, r] = delta[N_HEADS:, :]
            dg_ref[r, :] = (dypre * o * (sg + sil * (1.0 - sg))).astype(BF16)

    tok = lambda w: pl.BlockSpec((tm, w), lambda i: (i, 0))
    per_b = pl.BlockSpec((1, 1, 3 * D_MODEL), lambda i: (i // per_seq, 0, 0))
    return pl.pallas_call(
        body, name="post", grid=(n_tok // tm,),
        out_shape=[jax.ShapeDtypeStruct((n_tok, D_MODEL), F32), jax.ShapeDtypeStruct((n_tok, D_MODEL), BF16),
                   jax.ShapeDtypeStruct((n_tok, D_MODEL), BF16), jax.ShapeDtypeStruct((D_MODEL, D_MODEL), F32),
                   jax.ShapeDtypeStruct((1, D_MODEL), F32), jax.ShapeDtypeStruct((n_seq, 1, D_MODEL), F32),
                   jax.ShapeDtypeStruct((1, HEAD_LANES), F32),
                   jax.ShapeDtypeStruct((2, N_HEADS // 2, n_tok), F32), jax.ShapeDtypeStruct((N_HEADS, n_tok), F32)],
        in_specs=[tok(D_MODEL), tok(D_MODEL), tok(512), tok(512), tok(D_MODEL), per_b, _full(b_ada.shape),
                  _full(fg.shape), _full(w_out.shape)],
        out_specs=[tok(D_MODEL), tok(D_MODEL), tok(D_MODEL), _full((D_MODEL, D_MODEL)), _full((1, D_MODEL)),
                   pl.BlockSpec((1, 1, D_MODEL), lambda i: (i // per_seq, 0, 0)), _full((1, HEAD_LANES)),
                   pl.BlockSpec((2, N_HEADS // 2, tm), lambda i: (0, 0, i)), pl.BlockSpec((N_HEADS, tm), lambda i: (0, i))],
        compiler_params=_params(1),
    )(x, target, o_mla, o_swa, gates, mod, b_ada, fg, w_out)


def _mid_bwd_call(dqf, dkf, dv, zqkv, rope, qg, kvg, wq2, wkv, seq):
    n_tok = dqf.shape[0]
    tm = min(TOKEN_TILE, seq)

    def body(dq_ref, dk_ref, dv_ref, z_ref, rope_ref, qg_ref, kvg_ref, wq_ref, wkv_ref,
             dz_ref, gwq_ref, gwkv_ref, gqg_ref, gkvg_ref):
        i = pl.program_id(0)

        @pl.when(i == 0)
        def _():
            gwq_ref[...] = jnp.zeros_like(gwq_ref)
            gwkv_ref[...] = jnp.zeros_like(gwkv_ref)
            gqg_ref[...] = jnp.zeros_like(gqg_ref)
            gkvg_ref[...] = jnp.zeros_like(gkvg_ref)

        cos, sin = rope_ref[:, :HEAD_LANES], rope_ref[:, HEAD_LANES:]
        cf, sf = jnp.tile(cos, (1, N_HEADS)), jnp.tile(sin, (1, N_HEADS))
        dq = dq_ref[...] * MLA_SCALE
        dqr = jnp.concatenate([dq * cf, dq * sf], axis=-1).astype(BF16)
        zq, zkv = z_ref[:, :Q_LORA], z_ref[:, Q_LORA:]
        qgv, kvgv = qg_ref[...], kvg_ref[...]

        rq = lax.rsqrt(jnp.mean(zq * zq, axis=-1, keepdims=True) + EPS)
        xq = zq * rq
        gwq_ref[...] += _dot_tn((xq * qgv).astype(BF16), dqr)
        dqn = _dot_nt(dqr, wq_ref[...])
        gqg_ref[...] += jnp.sum(dqn * xq, axis=0, keepdims=True)
        dxq = dqn * qgv
        dz_ref[:, :Q_LORA] = (rq * (dxq - xq * jnp.mean(dxq * xq, axis=-1, keepdims=True))).astype(BF16)

        dk = dk_ref[...] * LN2
        dkv = jnp.concatenate([dk, dv_ref[...]], axis=-1).astype(BF16)
        rkv = lax.rsqrt(jnp.mean(zkv * zkv, axis=-1, keepdims=True) + EPS)
        xkv = zkv * rkv
        gwkv_ref[...] += _dot_tn((xkv * kvgv).astype(BF16), dkv)
        dkvn = _dot_nt(dkv, wkv_ref[...])
        gkvg_ref[...] += jnp.sum(dkvn * xkv, axis=0, keepdims=True)
        dxkv = dkvn * kvgv
        dz_ref[:, Q_LORA:A_KR] = (rkv * (dxkv - xkv * jnp.mean(dxkv * xkv, axis=-1, keepdims=True))).astype(BF16)

        dkpe = dk[:, :HEAD_LANES]
        for h in range(1, N_HEADS):
            dkpe = dkpe + dk[:, h * HEAD_LANES:(h + 1) * HEAD_LANES]
        dz_ref[:, A_KR:] = (jnp.where(_lane_lo(), 0.0, dkpe * cos) + pltpu.roll(dkpe * sin, HALF, 1)).astype(BF16)

    tok = lambda w: pl.BlockSpec((tm, w), lambda i: (i, 0))
    return pl.pallas_call(
        body, name="mid_bwd", grid=(n_tok // tm,),
        out_shape=[jax.ShapeDtypeStruct((n_tok, A_GM), BF16),
                   jax.ShapeDtypeStruct(wq2.shape, F32), jax.ShapeDtypeStruct(wkv.shape, F32),
                   jax.ShapeDtypeStruct((1, Q_LORA), F32), jax.ShapeDtypeStruct((1, KV_LORA), F32)],
        in_specs=[tok(1024), tok(1024), tok(512), tok(640), tok(2 * HEAD_LANES), _full(qg.shape), _full(kvg.shape),
                  _full(wq2.shape), _full(wkv.shape)],
        out_specs=[tok(A_GM), _full(wq2.shape), _full(wkv.shape), _full((1, Q_LORA)), _full((1, KV_LORA))],
        compiler_params=_params(1),
    )(dqf, dkf, dv, zqkv, rope, qg, kvg, wq2, wkv)


def _in_bwd_call(x, dx2, dz, dg, dqs, dkd, dvd, mod, b_ada, ng, wa, seq):
    n_tok = x.shape[0]
    tm = min(TOKEN_TILE, seq)
    per_seq = seq // tm
    n_seq = n_tok // seq

    def body(x_ref, dx2_ref, dz_ref, dg_ref, dqs_ref, dkd_ref, dvd_ref, mod_ref, bada_ref, ng_ref,
             wa_ref, gx_ref, gwa_ref, gng_ref, dshift_ref, dscale_ref):
        i = pl.program_id(0)

        @pl.when(i == 0)
        def _():
            gwa_ref[...] = jnp.zeros_like(gwa_ref)
            gng_ref[...] = jnp.zeros_like(gng_ref)

        @pl.when(i % per_seq == 0)
        def _():
            dshift_ref[...] = jnp.zeros_like(dshift_ref)
            dscale_ref[...] = jnp.zeros_like(dscale_ref)

        xv = x_ref[...]
        modv = mod_ref[0] + bada_ref[...]
        shift, scale = modv[:, :D_MODEL], modv[:, D_MODEL:2 * D_MODEL]
        ngv = ng_ref[...]
        r1 = lax.rsqrt(jnp.mean(xv * xv, axis=-1, keepdims=True) + EPS)
        xn = xv * r1
        hb = ((xn * ngv) * (1.0 + scale) + shift).astype(BF16)

        dgv = dg_ref[...]
        pieces = [(A_ZQ, dz_ref[...]), (A_GM, dgv[:, :512]), (A_QS, dqs_ref[...].astype(BF16)),
                  (A_KS, jnp.concatenate([_once(dkd_ref[...]) * LN2, _once(dvd_ref[...])], axis=1).astype(BF16)),
                  (A_GS, dgv[:, 512:])]
        dh = None
        for off, piece in pieces:
            wd = piece.shape[1]
            gwa_ref[:, off:off + wd] += _dot_tn(hb, piece)
            term = _dot_nt(piece, wa_ref[:, off:off + wd])
            dh = term if dh is None else dh + term

        dshift_ref[0] += jnp.sum(dh, axis=0, keepdims=True)
        dscale_ref[0] += jnp.sum(dh * (xn * ngv), axis=0, keepdims=True)
        gng_ref[...] += jnp.sum(dh * xn * (1.0 + scale), axis=0, keepdims=True)
        dxn = dh * ngv * (1.0 + scale)
        gx_ref[...] = dx2_ref[...] + r1 * (dxn - xn * jnp.mean(dxn * xn, axis=-1, keepdims=True))

    tok = lambda w: pl.BlockSpec((tm, w), lambda i: (i, 0))
    per_b = lambda w: pl.BlockSpec((1, 1, w), lambda i: (i // per_seq, 0, 0))
    return pl.pallas_call(
        body, name="in_bwd", grid=(n_tok // tm,),
        out_shape=[jax.ShapeDtypeStruct((n_tok, D_MODEL), F32), jax.ShapeDtypeStruct((D_MODEL, A_END), F32),
                   jax.ShapeDtypeStruct((1, D_MODEL), F32),
                   jax.ShapeDtypeStruct((n_seq, 1, D_MODEL), F32), jax.ShapeDtypeStruct((n_seq, 1, D_MODEL), F32)],
        in_specs=[tok(D_MODEL), tok(D_MODEL), tok(A_GM), tok(D_MODEL), tok(512), tok(256), tok(256),
                  per_b(3 * D_MODEL), _full(b_ada.shape), _full(ng.shape), _full(wa.shape)],
        out_specs=[tok(D_MODEL), _full((D_MODEL, A_END)), _full((1, D_MODEL)), per_b(D_MODEL), per_b(D_MODEL)],
        compiler_params=_params(1),
    )(x, dx2, dz, dg, dqs, dkd, dvd, mod, b_ada, ng, wa)


def _adam_math(w, g, m, v):
    m_new = ADAM_B1 * m + (1.0 - ADAM_B1) * g
    v_new = ADAM_B2 * v + (1.0 - ADAM_B2) * (g * g)
    m_hat = m_new / (1.0 - ADAM_B1 ** ADAM_STEP)
    v_hat = v_new / (1.0 - ADAM_B2 ** ADAM_STEP)
    delta = -ADAM_LR * (m_hat / (jnp.sqrt(v_hat) + ADAM_EPS) + ADAM_WD * w)
    return delta, m_new, v_new


def _adam_call(name, w, g, m, v):
    rows, cols = w.shape
    tr = next((t for t in (256, 128, 88) if rows % t == 0), rows)

    def body(w_ref, g_ref, m_ref, v_ref, d_ref, mo_ref, vo_ref):
        d, mn, vn = _adam_math(w_ref[...], g_ref[...], m_ref[...], v_ref[...])
        d_ref[...] = d
        mo_ref[...] = mn
        vo_ref[...] = vn

    spec = pl.BlockSpec((tr, cols), lambda i: (i, 0))
    return pl.pallas_call(
        body, name=name, grid=(rows // tr,),
        out_shape=[jax.ShapeDtypeStruct(w.shape, F32)] * 3,
        in_specs=[spec] * 4, out_specs=[spec] * 3,
        compiler_params=_params(1),
    )(w, g, m, v)


def _ada_bwd_call(act_all, dmod_cols, w, m, v):
    rows, cols = w.shape
    tr = 256

    def body(a_ref, dm_ref, w_ref, m_ref, v_ref, g_ref, d_ref, mo_ref, vo_ref):
        g = _dot_tn(a_ref[...].astype(BF16), dm_ref[...].astype(BF16))
        d, mn, vn = _adam_math(w_ref[...], g, m_ref[...], v_ref[...])
        g_ref[...] = g
        d_ref[...] = d
        mo_ref[...] = mn
        vo_ref[...] = vn

    spec = pl.BlockSpec((tr, cols), lambda i: (i, 0))
    nb = act_all.shape[0]
    return pl.pallas_call(
        body, name="ada_bwd", grid=(rows // tr,),
        out_shape=[jax.ShapeDtypeStruct(w.shape, F32)] * 4,
        in_specs=[pl.BlockSpec((nb, tr), lambda i: (0, i)), _full(dmod_cols.shape), spec, spec, spec],
        out_specs=[spec] * 4,
        compiler_params=_params(1),
    )(act_all, dmod_cols, w, m, v)


SMALL_ROW = {"norm_gain": (0, 1024), "final_gain": (1024, 2048), "q_norm_gain": (2048, 2432),
             "kv_norm_gain": (2432, 2688), "swa_sinks": (2688, 2696), "loss": (2816, 2944)}
SMALL_ORDER = ("b_ada", "norm_gain", "q_norm_gain", "kv_norm_gain", "swa_sinks", "final_gain")


def _small_call(parts_all, n_seq, params):
    k = len(params)

    def body(p_ref, *refs):
        ins, outs, loss_ref = refs[:3 * k], refs[3 * k:7 * k], refs[7 * k]
        row = p_ref[n_seq:n_seq + 1, :]
        for dv in range(1, 8):
            r0 = dv * ROWS_PER_DEVICE + n_seq
            row = row + p_ref[r0:r0 + 1, :]
        gb = None
        for dv in range(8):
            for r in range(n_seq):
                r0 = dv * ROWS_PER_DEVICE + r
                gb = p_ref[r0:r0 + 1, :] if gb is None else gb + p_ref[r0:r0 + 1, :]
        for j, name in enumerate(SMALL_ORDER):
            g = gb if name == "b_ada" else row[:, SMALL_ROW[name][0]:SMALL_ROW[name][1]]
            d, mn, vn = _adam_math(ins[3 * j][...], g, ins[3 * j + 1][...], ins[3 * j + 2][...])
            outs[4 * j][...] = g
            outs[4 * j + 1][...] = d
            outs[4 * j + 2][...] = mn
            outs[4 * j + 3][...] = vn
        loss_ref[...] = row[:, SMALL_ROW["loss"][0]:SMALL_ROW["loss"][1]]

    flat = [t for p in params for t in p]
    res = pl.pallas_call(
        body, name="small_update", grid=(1,),
        out_shape=[jax.ShapeDtypeStruct(p[0].shape, F32) for p in params for _ in range(4)]
        + [jax.ShapeDtypeStruct((1, HEAD_LANES), F32)],
        in_specs=[_full(parts_all.shape)] + [_full(t.shape) for t in flat],
        out_specs=[_full(p[0].shape) for p in params for _ in range(4)] + [_full((1, HEAD_LANES))],
        compiler_params=_params(1),
    )(parts_all, *flat)
    return [res[4 * j:4 * j + 4] for j in range(k)], res[4 * k]


def _rot(t):
    half = t.shape[-1] // 2
    return jnp.concatenate([-t[..., half:], t[..., :half]], axis=-1)


def _rot_t(g):
    half = g.shape[-1] // 2
    return jnp.concatenate([g[..., half:], -g[..., :half]], axis=-1)


def _columns(segments, lo, hi):
    out, at = [], 0
    for seg in segments:
        n = seg.shape[1]
        a, b = max(lo, at), min(hi, at + n)
        if a < b:
            out.append(seg[:, a - at:b - at])
        at += n
    return out


def _prepare_weights(w_in_blocks, w_uq, w_ukv):
    o = [0]
    for s in IN_SPLITS:
        o.append(o[-1] + s)
    part = lambda a, b: _columns(w_in_blocks, a, b)
    kr = jnp.concatenate(part(o[2], o[3]), axis=1)
    zero = jnp.zeros((kr.shape[0], 32), kr.dtype)
    wa = jnp.concatenate(part(0, o[2]) + [_rot(kr), zero, kr, zero] + part(o[3], o[8]), axis=1)
    uq = w_uq.reshape(Q_LORA, N_HEADS, MLA_NOPE + MLA_ROPE)
    zq = jnp.zeros((Q_LORA, N_HEADS, 32), w_uq.dtype)
    uq_full = jnp.concatenate([uq, zq], axis=-1).reshape(Q_LORA, 1024)
    uq_rot = jnp.concatenate([jnp.zeros((Q_LORA, N_HEADS, 64), w_uq.dtype), _rot(uq[..., MLA_NOPE:]), zq],
                             axis=-1).reshape(Q_LORA, 1024)
    wq2 = jnp.concatenate([uq_full, uq_rot], axis=1)
    ukv = w_ukv.reshape(KV_LORA, N_HEADS, 128)
    k_full = jnp.concatenate([ukv[..., :64], jnp.zeros((KV_LORA, N_HEADS, 64), w_ukv.dtype)], axis=-1).reshape(KV_LORA, 1024)
    wkv = jnp.concatenate([k_full, ukv[..., 64:].reshape(KV_LORA, 512)], axis=1)
    return wa, wq2, wkv


def _restore_grads(gwa, gwq2, gwkv):
    gkr = gwa[:, A_KR + 64:A_KR + 96] + _rot_t(gwa[:, A_KR:A_KR + 32])
    in_order = [gwa[:, :A_KR], gkr, gwa[:, A_GM:]]
    n = D_IN // 4
    g_in = [jnp.concatenate(_columns(in_order, k * n, (k + 1) * n), axis=1) for k in range(4)]
    gf = gwq2[:, :1024].reshape(Q_LORA, N_HEADS, 128)
    gr = gwq2[:, 1024:].reshape(Q_LORA, N_HEADS, 128)
    g_uq = jnp.concatenate([gf[..., :64], gf[..., 64:96] + _rot_t(gr[..., 64:96])], axis=-1).reshape(Q_LORA, 768)
    gk = gwkv[:, :1024].reshape(KV_LORA, N_HEADS, 128)[..., :64]
    gv = gwkv[:, 1024:].reshape(KV_LORA, N_HEADS, 64)
    g_ukv = jnp.concatenate([gk, gv], axis=-1).reshape(KV_LORA, 1024)
    return g_in, g_uq, g_ukv


def _local_step(x, positions, target, mod_rows, b_ada, ng, qg, kvg, sinks, fg, w_in_b, w_uq_b, w_ukv_b, w_out_b):
    n_seq, seq, _ = x.shape
    n_tok = n_seq * seq
    x2d = x.reshape(n_tok, D_MODEL)
    t2d = target.reshape(n_tok, D_MODEL)
    pos_f = positions.astype(F32)
    pos_col = pos_f.reshape(n_tok, 1)
    pos_row = pos_f.reshape(n_tok // SWA_WINDOW, 1, SWA_WINDOW)
    mod3 = mod_rows.reshape(n_seq, 1, 3 * D_MODEL)
    inv = ROPE_THETA ** (-jnp.arange(0, MLA_ROPE, 2, dtype=F32) / MLA_ROPE)
    inv128 = jnp.concatenate([jnp.zeros((64,), F32), inv, inv, jnp.zeros((32,), F32)]).reshape(1, 128)
    fg2 = fg.reshape(1, D_MODEL)

    wa, wq2, wkv = _prepare_weights(w_in_b, w_uq_b, w_ukv_b)

    zqkv, gates, qf, kf, v, qs, kd, vd, rope = _pre_call(x2d, pos_col, mod3, b_ada, ng, qg, kvg, inv128, wa, wq2, wkv, seq)
    o_mla, lse_mla = _mla_fwd_call(qf, kf, v, n_seq, seq)
    o_swa, lse_swa = _swa_fwd_call(qs, kd, vd, pos_col, pos_row, sinks, n_seq, seq)
    dx2, do, dg, g_out, g_fg, dgate, loss, delta_mla, delta_swa = _post_call(x2d, t2d, o_mla, o_swa, gates, mod3, b_ada, fg2, w_out_b, seq)
    dqf, dkf, dv = _mla_bwd_call(qf, kf, v, do, delta_mla, lse_mla, n_seq, seq)
    dqs, dkd, dvd, dsink = _swa_bwd_call(qs, kd, vd, do, delta_swa, lse_swa, pos_col, pos_row, sinks, n_seq, seq)
    dz, g_wq2, g_wkv, g_qg, g_kvg = _mid_bwd_call(dqf, dkf, dv, zqkv, rope, qg, kvg, wq2, wkv, seq)
    gx, g_wa, g_ng, dshift, dscale = _in_bwd_call(x2d, dx2, dz, dg, dqs, dkd, dvd, mod3, b_ada, ng, wa, seq)
    g_in, g_uq, g_ukv = _restore_grads(g_wa, g_wq2, g_wkv)
    dmod = jnp.concatenate([dshift, dscale, dgate], axis=-1).reshape(n_seq, 3 * D_MODEL)
    small_row = jnp.concatenate([g_ng, g_fg, g_qg, g_kvg, jnp.pad(jnp.sum(dsink, axis=1).reshape(1, N_HEADS), ((0, 0), (0, 120))),
                                 loss, jnp.zeros((1, 128), F32)], axis=1)
    return gx.reshape(x.shape), (g_in, g_uq, g_ukv, g_out), small_row, dmod


def kernel(x, c, positions, w_ada, b_ada, norm_gain, w_in, q_norm_gain, kv_norm_gain, w_uq, w_ukv, swa_sinks, w_out, final_gain, loss_target, m_w_ada, m_b_ada, m_norm_gain, m_w_in, m_q_norm_gain, m_kv_norm_gain, m_w_uq, m_w_ukv, m_swa_sinks, m_w_out, m_final_gain, v_w_ada, v_b_ada, v_norm_gain, v_w_in, v_q_norm_gain, v_kv_norm_gain, v_w_uq, v_w_ukv, v_swa_sinks, v_w_out, v_final_gain):
    n_seq = x.shape[0]
    xi, yi, ci = lax.axis_index("x"), lax.axis_index("y"), lax.axis_index("c")
    dev = 4 * xi + 2 * yi + ci
    chip = 2 * xi + yi

    halves = lambda w: w.astype(BF16).reshape(2, w.shape[0] // 2, w.shape[1])
    c_blk = jnp.pad(c, ((0, ROWS_PER_DEVICE - n_seq), (0, 0)))
    act_all, pieces, f_in, f_uq, f_ukv, f_out = _comm_fwd_call(
        c_blk, w_ada[0], [halves(w_in[0]), halves(w_uq[0]), halves(w_ukv[0]), halves(w_out[0])])
    mine = lax.dynamic_slice_in_dim(pieces, dev * ROWS_PER_DEVICE, n_seq, axis=1)
    mod_rows = jnp.transpose(mine, (1, 0, 2)).reshape(n_seq, 3 * D_MODEL)
    cols = lambda t, r: jnp.transpose(t.reshape(4, r, -1), (1, 0, 2)).reshape(r, -1)
    w_in_blocks = [f_in[k].reshape(D_MODEL, -1) for k in range(4)]
    w_uq_b, w_ukv_b = cols(f_uq, Q_LORA), cols(f_ukv, KV_LORA)
    w_out_b = f_out.reshape(D_MODEL, D_MODEL)

    gx, (g_in_blocks, g_uq, g_ukv, g_out), small_row, dmod = _local_step(
        x, positions, loss_target, mod_rows, b_ada, norm_gain, q_norm_gain, kv_norm_gain, swa_sinks, final_gain,
        w_in_blocks, w_uq_b, w_ukv_b, w_out_b)

    by_owner = lambda g, n: jnp.transpose(g.reshape(g.shape[0], 4, n), (1, 0, 2)).reshape(4, 2, g.shape[0] // 2, n)
    grads = [jnp.stack(g_in_blocks).reshape(4, 2, D_MODEL // 2, -1), by_owner(g_uq, 192), by_owner(g_ukv, 256),
             g_out.reshape(4, 2, 128, D_MODEL)]
    part = jnp.concatenate([dmod, small_row, jnp.zeros((ROWS_PER_DEVICE - n_seq - 1, 3 * D_MODEL), F32)], axis=0)
    r_in, r_uq, r_ukv, r_out, parts_all = _comm_bwd_call(grads, part)
    g_in_s, g_uq_s = r_in.reshape(w_in.shape[1:]), r_uq.reshape(w_uq.shape[1:])
    g_ukv_s, g_out_s = r_ukv.reshape(w_ukv.shape[1:]), r_out.reshape(w_out.shape[1:])

    tr = lambda a: jnp.swapaxes(a[0], 0, 1)
    back = lambda ts: [jnp.swapaxes(t, 0, 1) for t in ts]
    d_in, nm_in, nv_in = back(_adam_call("adam_w_in", tr(w_in), g_in_s.T, tr(m_w_in), tr(v_w_in)))
    d_uq, nm_uq, nv_uq = back(_adam_call("adam_w_uq", tr(w_uq), g_uq_s.T, tr(m_w_uq), tr(v_w_uq)))
    d_ukv, nm_ukv, nv_ukv = _adam_call("adam_w_ukv", w_ukv[0], g_ukv_s, m_w_ukv[0], v_w_ukv[0])
    d_out, nm_out, nv_out = _adam_call("adam_w_out", w_out[0], g_out_s, m_w_out[0], v_w_out[0])
    dmod_cols = lax.dynamic_slice_in_dim(parts_all, chip * 768, 768, axis=1)
    g_ada, d_ada, nm_ada, nv_ada = _ada_bwd_call(act_all, dmod_cols, w_ada[0], m_w_ada[0], v_w_ada[0])

    row = lambda t: t.reshape(1, -1)
    small = {"b_ada": (b_ada, m_b_ada, v_b_ada), "norm_gain": (norm_gain, m_norm_gain, v_norm_gain),
             "q_norm_gain": (q_norm_gain, m_q_norm_gain, v_q_norm_gain),
             "kv_norm_gain": (kv_norm_gain, m_kv_norm_gain, v_kv_norm_gain),
             "swa_sinks": (swa_sinks, m_swa_sinks, v_swa_sinks),
             "final_gain": (row(final_gain), row(m_final_gain), row(v_final_gain))}
    res, loss_row = _small_call(parts_all, n_seq, [small[name] for name in SMALL_ORDER])
    res = dict(zip(SMALL_ORDER, res))
    res["final_gain"] = [t.reshape(-1) for t in res["final_gain"]]
    e = lambda t: t[None]
    big = {"w_ada": (e(g_ada), e(d_ada), e(nm_ada), e(nv_ada)), "w_in": (e(g_in_s), e(d_in), e(nm_in), e(nv_in)),
           "w_uq": (e(g_uq_s), e(d_uq), e(nm_uq), e(nv_uq)), "w_ukv": (e(g_ukv_s), e(d_ukv), e(nm_ukv), e(nv_ukv)),
           "w_out": (e(g_out_s), e(d_out), e(nm_out), e(nv_out))}
    order = ("w_ada", "b_ada", "norm_gain", "w_in", "q_norm_gain", "kv_norm_gain", "w_uq", "w_ukv", "swa_sinks", "w_out",
             "final_gain")
    pick = lambda kind: [(big[n] if n in big else res[n])[kind] for n in order]
    return (loss_row[0, 0], gx, *pick(0), *pick(1), *pick(2), *pick(3))
```

```python
import jax
import jax.numpy as jnp
from jax import lax
from jax.experimental import pallas as pl
from jax.experimental.pallas import tpu as pltpu

F32 = jnp.float32
BF16 = jnp.bfloat16

D_MODEL = 1024
Q_LORA = 384
KV_LORA = 256
N_HEADS = 8
MLA_NOPE = 64
MLA_ROPE = 32
HEAD_LANES = 128
HALF = 64
SWA_WINDOW = 128
EPS = 1e-6
ROPE_THETA = 10000.0
MLA_SCALE = (MLA_NOPE + MLA_ROPE) ** -0.5
LOG2E = 1.4426950408889634
LN2 = 0.6931471805599453
SWA_SCALE = 64 ** -0.5
NEG = -1e30

ADAM_LR = 0.001
ADAM_B1 = 0.9
ADAM_B2 = 0.999
ADAM_EPS = 1e-08
ADAM_WD = 0.01
ADAM_STEP = 10

A_ZQ, A_ZKV, A_KR, A_GM, A_QS, A_KS, A_VS, A_GS, A_END = 0, 384, 640, 768, 1280, 1792, 1920, 2048, 2560
IN_SPLITS = (384, 256, 32, 512, 512, 128, 128, 512)
D_IN = sum(IN_SPLITS)

TOKEN_TILE = 512
ATT_TILE = 256
VMEM_LIMIT = 56 * 1024 * 1024


def _dot(a, b):
    return jnp.dot(a, b, preferred_element_type=F32)


def _dot_nt(a, b):
    return lax.dot_general(a, b, (((1,), (1,)), ((), ())), preferred_element_type=F32)


def _dot_tn(a, b):
    return lax.dot_general(a, b, (((0,), (0,)), ((), ())), preferred_element_type=F32)


def _params(n_grid):
    return pltpu.CompilerParams(dimension_semantics=("arbitrary",) * n_grid, vmem_limit_bytes=VMEM_LIMIT)


def _full(shape):
    nd = len(shape)
    return pl.BlockSpec(shape, lambda *_: (0,) * nd, pipeline_mode=pl.Buffered(1))


def _sigmoid(g):
    return 1.0 / (1.0 + jnp.exp(-g))


SUB_TILE = 256


def _sub_tiles(tm):
    sub = min(SUB_TILE, tm)
    return [slice(s * sub, (s + 1) * sub) for s in range(tm // sub)]


MESH = pl.DeviceIdType.MESH
ROWS_PER_DEVICE = 8
VMEM_SPEC = pl.BlockSpec(memory_space=pltpu.VMEM)
ANY_SPEC = pl.BlockSpec(memory_space=pl.ANY)


def _position():
    x, y, c = lax.axis_index("x"), lax.axis_index("y"), lax.axis_index("c")
    sibling = (x, y, 1 - c)
    others = [(1 - x, y, c), (x, 1 - y, c), (1 - x, 1 - y, c)]
    return (x, y, c), 4 * x + 2 * y + c, 2 * x + y, sibling, others


def _rows_of(dev):
    return pl.ds(pl.multiple_of(dev * ROWS_PER_DEVICE, ROWS_PER_DEVICE), ROWS_PER_DEVICE)


def _all_to_all_rows(block_ref, table_ref, dev, me, send_sems, recv_sems):
    x, y, c = me
    waits = []
    for k in range(1, 8):
        peer = (1 - x if k & 4 else x, 1 - y if k & 2 else y, 1 - c if k & 1 else c)
        pltpu.make_async_remote_copy(src_ref=block_ref, dst_ref=table_ref.at[_rows_of(dev)], send_sem=send_sems.at[k - 1],
                                     recv_sem=recv_sems.at[k - 1], device_id=peer, device_id_type=MESH).start()
        waits.append(pltpu.make_async_remote_copy(
            src_ref=block_ref, dst_ref=table_ref.at[_rows_of(jnp.bitwise_xor(dev, k))], send_sem=send_sems.at[k - 1],
            recv_sem=recv_sems.at[k - 1], device_id=peer, device_id_type=MESH))
    return waits


def _comm_fwd_call(c_blk, w_ada, shards):
    n = len(shards)

    def body(c_ref, wada_ref, *refs):
        w_refs, act_ref, pieces_ref, full_refs = refs[:n], refs[n], refs[n + 1], refs[n + 2:2 * n + 2]
        c_all_ref = refs[2 * n + 2]
        c_send, c_recv, p_send, p_recv, w_send, w_recv, f_send, f_recv, loc_sem = refs[2 * n + 3:]
        me, dev, chip, sibling, others = _position()
        core = me[2]
        chip_of = [2 * p[0] + p[1] for p in others]

        local = [pltpu.make_async_copy(w_refs[i], full_refs[i].at[chip], loc_sem.at[i]) for i in range(n)]
        for cp in local:
            cp.start()

        def over_ici(i, j, src_chip):
            return pltpu.make_async_remote_copy(
                src_ref=w_refs[i].at[core], dst_ref=full_refs[i].at[src_chip, core], send_sem=w_send.at[3 * i + j],
                recv_sem=w_recv.at[3 * i + j], device_id=others[j], device_id_type=MESH)

        def to_sibling(i, j, half):
            return pltpu.make_async_remote_copy(
                src_ref=full_refs[i].at[chip_of[j], half], dst_ref=full_refs[i].at[chip_of[j], half],
                send_sem=f_send.at[3 * i + j], recv_sem=f_recv.at[3 * i + j], device_id=sibling, device_id_type=MESH)

        c_all_ref[_rows_of(dev), :] = c_ref[...]
        c_waits = _all_to_all_rows(c_ref, c_all_ref, dev, me, c_send, c_recv)
        sent = [over_ici(i, j, chip) for i in range(n) for j in range(3)]
        for cp in sent:
            cp.start()

        for cp in c_waits:
            cp.wait()
        cv = c_all_ref[...]
        act = cv * _sigmoid(cv)
        act_ref[...] = act
        pieces_ref[chip] = _dot(act.astype(BF16), wada_ref[...].astype(BF16))
        piece = lambda j, src_chip: pltpu.make_async_remote_copy(
            src_ref=pieces_ref.at[chip], dst_ref=pieces_ref.at[src_chip], send_sem=p_send.at[j], recv_sem=p_recv.at[j],
            device_id=others[j], device_id_type=MESH)
        for j in range(3):
            piece(j, chip).start()

        for i in range(n):
            for j in range(3):
                over_ici(i, j, chip_of[j]).wait_recv()
                to_sibling(i, j, core).start()
        for j in range(3):
            piece(j, chip).wait_send()
            piece(j, chip_of[j]).wait_recv()
        for i in range(n):
            for j in range(3):
                to_sibling(i, j, 1 - core).wait_recv()
                to_sibling(i, j, core).wait_send()
        for cp in sent:
            cp.wait_send()
        for cp in local:
            cp.wait()

    rows = 8 * ROWS_PER_DEVICE
    dma = pltpu.SemaphoreType.DMA
    return pl.pallas_call(
        body, name="comm_fwd",
        out_shape=[jax.ShapeDtypeStruct((rows, D_MODEL), F32), jax.ShapeDtypeStruct((4, rows, w_ada.shape[1]), F32)]
        + [jax.ShapeDtypeStruct((4,) + s.shape, s.dtype) for s in shards],
        in_specs=[VMEM_SPEC, VMEM_SPEC] + [ANY_SPEC] * n,
        out_specs=[VMEM_SPEC, VMEM_SPEC] + [ANY_SPEC] * n,
        scratch_shapes=[pltpu.VMEM((rows, D_MODEL), F32), dma((7,)), dma((7,)), dma((3,)), dma((3,)),
                        dma((3 * n,)), dma((3 * n,)), dma((3 * n,)), dma((3 * n,)), dma((n,))],
        compiler_params=pltpu.CompilerParams(vmem_limit_bytes=VMEM_LIMIT),
    )(c_blk, w_ada, *shards)


def _comm_bwd_call(grads, part):
    n = len(grads)

    def body(part_ref, *refs):
        g_refs, f_refs, parts_ref = refs[:n], refs[n:2 * n], refs[2 * n]
        scratch = refs[2 * n + 1:]
        a_refs, b_refs, p_refs, r_refs = (scratch[k * n:(k + 1) * n] for k in range(4))
        s_send, s_recv, d_send, d_recv, e_send, e_recv, h_send, h_recv, loc_sem = scratch[4 * n:]
        me, dev, chip, sibling, others = _position()
        core = me[2]
        chip_of = [2 * p[0] + p[1] for p in others]

        parts_ref[_rows_of(dev), :] = part_ref[...]
        s_waits = _all_to_all_rows(part_ref, parts_ref, dev, me, s_send, s_recv)

        mine = [pltpu.make_async_copy(g_refs[i].at[:, core], a_refs[i], loc_sem.at[i]) for i in range(n)]
        swap = [pltpu.make_async_remote_copy(src_ref=g_refs[i].at[:, 1 - core], dst_ref=b_refs[i], send_sem=d_send.at[i],
                                             recv_sem=d_recv.at[i], device_id=sibling, device_id_type=MESH) for i in range(n)]
        order = sorted(range(n), key=lambda i: g_refs[i].shape[2] * g_refs[i].shape[3])
        for i in order:
            mine[i].start()
            swap[i].start()
        cross = [pltpu.make_async_remote_copy(src_ref=p_refs[i].at[chip_of[j]], dst_ref=r_refs[i].at[j],
                                              send_sem=e_send.at[3 * i + j], recv_sem=e_recv.at[3 * i + j],
                                              device_id=others[j], device_id_type=MESH) for i in range(n) for j in range(3)]
        for i in order:
            mine[i].wait()
            swap[i].wait()
            for k in range(4):
                s = a_refs[i][k] + b_refs[i][k]
                a_refs[i][k] = s
                p_refs[i][k] = s.astype(BF16)
            for j in range(3):
                cross[3 * i + j].start()
        share = {}
        for i in order:
            for j in range(3):
                cross[3 * i + j].wait()
            f_refs[i][core] = (a_refs[i][chip] + r_refs[i][0].astype(F32) + r_refs[i][1].astype(F32)
                               + r_refs[i][2].astype(F32))
            share[i] = pltpu.make_async_remote_copy(src_ref=f_refs[i].at[core], dst_ref=f_refs[i].at[core],
                                                    send_sem=h_send.at[i], recv_sem=h_recv.at[i], device_id=sibling,
                                                    device_id_type=MESH)
            share[i].start()
        for i in range(n):
            share[i].wait_send()
            pltpu.make_async_remote_copy(src_ref=f_refs[i].at[core], dst_ref=f_refs[i].at[1 - core], send_sem=h_send.at[i],
                                         recv_sem=h_recv.at[i], device_id=sibling, device_id_type=MESH).wait_recv()
        for cp in s_waits:
            cp.wait()

    rows = 8 * ROWS_PER_DEVICE
    dma = pltpu.SemaphoreType.DMA
    quarter = [(4,) + g.shape[2:] for g in grads]
    return pl.pallas_call(
        body, name="comm_bwd",
        out_shape=[jax.ShapeDtypeStruct((2,) + g.shape[2:], F32) for g in grads]
        + [jax.ShapeDtypeStruct((rows, part.shape[1]), F32)],
        in_specs=[VMEM_SPEC] + [ANY_SPEC] * n,
        out_specs=[VMEM_SPEC] * (n + 1),
        scratch_shapes=[pltpu.VMEM(q, F32) for q in quarter] + [pltpu.VMEM(q, F32) for q in quarter]
        + [pltpu.VMEM(q, BF16) for q in quarter] + [pltpu.VMEM((3,) + q[1:], BF16) for q in quarter]
        + [dma((7,)), dma((7,)), dma((n,)), dma((n,)), dma((3 * n,)), dma((3 * n,)), dma((n,)), dma((n,)), dma((n,))],
        compiler_params=pltpu.CompilerParams(vmem_limit_bytes=VMEM_LIMIT),
    )(part, *grads)


def _twice(t):
    lo = _lane_lo()
    other = pltpu.roll(t, HALF, 1)
    return jnp.concatenate([jnp.where(lo, t, other), jnp.where(lo, other, t)], axis=1)


def _once(g):
    first, second = g[:, :HEAD_LANES], g[:, HEAD_LANES:]
    return jnp.where(_lane_lo(), first + pltpu.roll(first, HALF, 1), second + pltpu.roll(second, HALF, 1))


def _rope_tables(pos_col, inv_row):
    ang = pos_col * inv_row
    return jnp.cos(ang), jnp.sin(ang)


def _pre_call(x, pos_col, mod, b_ada, ng, qg, kvg, inv128, wa, wq2, wkv, w_out_half, seq):
    n_tok = x.shape[0]
    tm = min(TOKEN_TILE, seq)
    per_seq = seq // tm
    n_steps = n_tok // tm

    def gather_w_out(step, wo_ref, full_ref, w_send, w_recv, f_send, f_recv, loc_sem):
        me, _, chip, sibling, others = _position()
        core = me[2]
        chip_of = [2 * p[0] + p[1] for p in others]
        local = pltpu.make_async_copy(wo_ref, full_ref.at[chip], loc_sem.at[0])

        def over_ici(j, src_chip):
            return pltpu.make_async_remote_copy(
                src_ref=wo_ref.at[core], dst_ref=full_ref.at[src_chip, core], send_sem=w_send.at[j], recv_sem=w_recv.at[j],
                device_id=others[j], device_id_type=MESH)

        def to_sibling(j, half):
            return pltpu.make_async_remote_copy(
                src_ref=full_ref.at[chip_of[j], half], dst_ref=full_ref.at[chip_of[j], half], send_sem=f_send.at[j],
                recv_sem=f_recv.at[j], device_id=sibling, device_id_type=MESH)

        @pl.when(step == 0)
        def _():
            local.start()
            for j in range(3):
                over_ici(j, chip).start()

        @pl.when(step == n_steps // 2)
        def _():
            for j in range(3):
                over_ici(j, chip_of[j]).wait_recv()
                to_sibling(j, core).start()

        @pl.when(step == n_steps - 1)
        def _():
            for j in range(3):
                to_sibling(j, 1 - core).wait_recv()
                to_sibling(j, core).wait_send()
                over_ici(j, chip).wait_send()
            local.wait()

    def body(x_ref, pos_ref, mod_ref, bada_ref, ng_ref, qg_ref, kvg_ref, inv_ref, wa_ref, wq_ref, wkv_ref, wo_ref,
             zqkv_ref, gates_ref, qf_ref, kf_ref, v_ref, qs_ref, kd_ref, vd_ref, rope_ref, full_ref, *sems):
        gather_w_out(pl.program_id(0), wo_ref, full_ref, *sems)
        xv = x_ref[...]
        modv = mod_ref[0] + bada_ref[...]
        shift, scale = modv[:, :D_MODEL], modv[:, D_MODEL:2 * D_MODEL]
        r1 = lax.rsqrt(jnp.mean(xv * xv, axis=-1, keepdims=True) + EPS)
        h = ((xv * r1) * ng_ref[...]) * (1.0 + scale) + shift
        hb = h.astype(BF16)
        za = _dot(hb, wa_ref[...])
        zkr = za[:, A_KR:A_GM]
        cos, sin = _rope_tables(pos_ref[...], inv_ref[...])
        rope_ref[:, :HEAD_LANES] = cos
        rope_ref[:, HEAD_LANES:] = sin
        zqkv_ref[...] = za[:, :A_KR]
        gates_ref[:, :512] = za[:, A_GM:A_QS]
        gates_ref[:, 512:] = za[:, A_GS:A_END]
        qs_ref[...] = (za[:, A_QS:A_KS] * (SWA_SCALE * LOG2E)).astype(BF16)
        kd_ref[...] = _twice(za[:, A_KS:A_VS]).astype(BF16)
        vd_ref[...] = _twice(za[:, A_VS:A_GS]).astype(BF16)
        zq, zkv = za[:, A_ZQ:A_ZKV], za[:, A_ZKV:A_KR]
        rq = lax.rsqrt(jnp.mean(zq * zq, axis=-1, keepdims=True) + EPS)
        qn = ((zq * rq) * qg_ref[...]).astype(BF16)
        qr = _dot(qn, wq_ref[...])
        cf, sf = jnp.tile(cos, (1, N_HEADS)), jnp.tile(sin, (1, N_HEADS))
        qf_ref[...] = ((qr[:, :1024] * cf + qr[:, 1024:] * sf) * (MLA_SCALE * LOG2E)).astype(BF16)
        rkv = lax.rsqrt(jnp.mean(zkv * zkv, axis=-1, keepdims=True) + EPS)
        kvn = ((zkv * rkv) * kvg_ref[...]).astype(BF16)
        kv = _dot(kvn, wkv_ref[...])
        kpe = jnp.where(_lane_lo(), 0.0, zkr * cos) + pltpu.roll(zkr, HALF, 1) * sin
        kf_ref[...] = (kv[:, :1024] + jnp.tile(kpe, (1, N_HEADS))).astype(BF16)
        v_ref[...] = kv[:, 1024:].astype(BF16)

    tok = lambda w: pl.BlockSpec((tm, w), lambda i: (i, 0))
    outs = [(640, F32), (1024, F32), (1024, BF16), (1024, BF16), (512, BF16), (512, BF16), (256, BF16), (256, BF16),
            (2 * HEAD_LANES, F32)]
    dma = pltpu.SemaphoreType.DMA
    return pl.pallas_call(
        body, name="pre", grid=(n_steps,),
        out_shape=[jax.ShapeDtypeStruct((n_tok, w), dt) for w, dt in outs]
        + [jax.ShapeDtypeStruct((4,) + w_out_half.shape, w_out_half.dtype)],
        in_specs=[tok(D_MODEL), tok(1), pl.BlockSpec((1, 1, 3 * D_MODEL), lambda i: (i // per_seq, 0, 0)),
                  _full(b_ada.shape), _full(ng.shape), _full(qg.shape), _full(kvg.shape), _full(inv128.shape),
                  _full(wa.shape), _full(wq2.shape), _full(wkv.shape), ANY_SPEC],
        out_specs=[tok(w) for w, _ in outs] + [ANY_SPEC],
        scratch_shapes=[dma((3,)), dma((3,)), dma((3,)), dma((3,)), dma((1,))],
        compiler_params=_params(1),
    )(x, pos_col, mod, b_ada, ng, qg, kvg, inv128, wa, wq2, wkv, w_out_half)


def _lane_lo(width=HEAD_LANES):
    return lax.broadcasted_iota(jnp.int32, (1, width), 1) < HALF


def _eye(n=HEAD_LANES):
    r = lax.broadcasted_iota(jnp.int32, (n, n), 0)
    c = lax.broadcasted_iota(jnp.int32, (n, n), 1)
    return jnp.where(r == c, 1.0, 0.0).astype(BF16)


def _mla_fwd_call(qf, kf, v, n_seq, seq):
    tq = min(ATT_TILE, seq)
    nq = seq // tq

    ext = HALF + 16

    def body(q_ref, k_ref, v_ref, o_ref, lse_ref, vt_ref, acc_ref):
        i = pl.program_id(1)
        eye = _eye()

        @pl.when(i == 0)
        def _():
            for h in range(N_HEADS):
                vt_ref[h * ext + HALF:(h + 1) * ext, :] = jnp.ones((16, seq), BF16)
            for t in range(nq):
                for p in range(N_HEADS // 2):
                    pair = slice(p * HEAD_LANES, (p + 1) * HEAD_LANES)
                    v_t = _dot_nt(eye, v_ref[t * tq:(t + 1) * tq, pair]).astype(BF16)
                    for hh in range(2):
                        r0 = (2 * p + hh) * ext
                        vt_ref[r0:r0 + HALF, t * tq:(t + 1) * tq] = v_t[hh * HALF:(hh + 1) * HALF, :]

        q = q_ref[...]
        qcol = i * tq + lax.broadcasted_iota(jnp.int32, (1, tq), 1)
        heads = range(N_HEADS)
        lanes = [slice(h * HEAD_LANES, (h + 1) * HEAD_LANES) for h in heads]

        def make_step(masked, n_tiles):
            def step(kt0, carry):
                tiles = range(n_tiles)
                start = pl.multiple_of(kt0 * tq, tq)
                ks = [k_ref[pl.ds(pl.multiple_of((kt0 + t) * tq, tq), tq), :] for t in tiles]
                vt = vt_ref[:, pl.ds(start, n_tiles * tq)]
                last = n_tiles - 1
                if masked:
                    keep = ((kt0 + last) * tq + lax.broadcasted_iota(jnp.int32, (tq, 1), 0)) <= qcol

                def scores(h):
                    sts = [_dot_nt(ks[t][:, lanes[h]], q[:, lanes[h]]) for t in tiles]
                    if masked:
                        sts[last] = jnp.where(keep, sts[last], NEG)
                    return sts

                def softmax(h, sts):
                    m_old = carry[h]
                    m_new = m_old
                    for st in sts:
                        m_new = jnp.maximum(m_new, jnp.max(st, axis=0, keepdims=True))
                    pt = jnp.concatenate([jnp.exp2(st - m_new).astype(BF16) for st in sts], axis=0)
                    return m_new, jnp.exp2(m_old - m_new), pt

                def values(h, alpha, pt):
                    rows = slice(h * ext, (h + 1) * ext)
                    acc_ref[rows, :] = acc_ref[rows, :] * alpha + _dot(vt[rows, :], pt)

                sts, soft, out = {0: scores(0), 1: scores(1)}, {}, {}
                for h in range(N_HEADS + 1):
                    if h + 2 < N_HEADS:
                        sts[h + 2] = scores(h + 2)
                    if h < N_HEADS:
                        soft[h] = softmax(h, sts.pop(h))
                    if h >= 1:
                        m_new, alpha, pt = soft.pop(h - 1)
                        values(h - 1, alpha, pt)
                        out[h - 1] = m_new
                return tuple(out[h] for h in heads)
            return step

        acc_ref[...] = jnp.zeros_like(acc_ref)
        init = (jnp.full((1, tq), NEG, F32),) * N_HEADS
        count = i + 1
        carry = lax.fori_loop(0, (count + 1) // 2 - 1, lambda j, c: make_step(False, 2)(2 * j, c), init)
        carry = lax.cond(count % 2 == 0, lambda c: make_step(True, 2)(i - 1, c), lambda c: make_step(True, 1)(i, c), carry)
        dens = [acc_ref[h * ext + HALF:h * ext + HALF + 1, :] for h in heads]
        acc_t = jnp.concatenate([acc_ref[h * ext:h * ext + HALF, :] * (1.0 / dens[h]) for h in heads], axis=0)
        o_ref[...] = acc_t.T
        for h in heads:
            lse_ref[0, h // 4, h % 4:h % 4 + 1, :] = carry[h] + jnp.log2(dens[h])

    n_tok = qf.shape[0]
    return pl.pallas_call(
        body, name="mla_fwd", grid=(n_seq, nq),
        out_shape=[jax.ShapeDtypeStruct((n_tok, 512), F32), jax.ShapeDtypeStruct((n_seq, 2, 4, seq), F32)],
        in_specs=[pl.BlockSpec((tq, 1024), lambda b, i: (b * nq + i, 0)),
                  pl.BlockSpec((seq, 1024), lambda b, i: (b, 0)),
                  pl.BlockSpec((seq, 512), lambda b, i: (b, 0))],
        out_specs=[pl.BlockSpec((tq, 512), lambda b, i: (b * nq + i, 0)),
                   pl.BlockSpec((1, 2, 4, tq), lambda b, i: (b, 0, 0, i))],
        scratch_shapes=[pltpu.VMEM((N_HEADS * ext, seq), BF16), pltpu.VMEM((N_HEADS * ext, tq), F32)],
        compiler_params=_params(2),
    )(qf, kf, v)


def _mla_bwd_call(qf, kf, v, do, delta, lse, n_seq, seq):
    tq = min(ATT_TILE, seq)
    nq = seq // tq

    nh = 4
    heads = range(nh)
    lanes = [slice(h * HEAD_LANES, (h + 1) * HEAD_LANES) for h in heads]

    def body(q_ref, k_ref, v_ref, do_ref, dl_ref, lse_ref, dq_ref, dk_ref, dv_ref,
             kt_ref, dot_ref, dqt_ref, dvt_ref):
        eye = _eye()
        sub_lo = lax.broadcasted_iota(jnp.int32, (HEAD_LANES, 1), 0) < HALF

        for t in range(nq):
            r = slice(t * tq, (t + 1) * tq)
            kv = k_ref[r, :]
            for h in heads:
                kt_ref[lanes[h], r] = _dot_nt(eye, kv[:, lanes[h]]).astype(BF16)
            for p in range(nh // 2):
                dov = do_ref[r, lanes[p]]
                dt = _dot_nt(eye, dov)
                dot_ref[2 * p, :, r] = jnp.where(sub_lo, dt, 0.0).astype(BF16)
                dot_ref[2 * p + 1, :, r] = jnp.where(sub_lo, 0.0, dt).astype(BF16)
        dqt_ref[...] = jnp.zeros_like(dqt_ref)
        dvt_ref[...] = jnp.zeros_like(dvt_ref)

        def flush_dv(tile, which):
            rows = pl.ds(pl.multiple_of(tile * tq, tq), tq)
            for p in range(nh // 2):
                dv_ref[rows, lanes[p]] = dvt_ref[which, p * HEAD_LANES:(p + 1) * HEAD_LANES, :].T

        def k_step(kt, _):
            slot = kt % 2
            kr = pl.ds(pl.multiple_of(kt * tq, tq), tq)
            k = k_ref[kr, :]
            vv = v_ref[kr, :]
            k_t = kt_ref[:, kr]
            krow = kt * tq + lax.broadcasted_iota(jnp.int32, (tq, 1), 0)

            def make_step(masked, n_tiles):
                def q_step(qt0, carry):
                    tiles = range(n_tiles)
                    qrs = [pl.ds(pl.multiple_of((qt0 + t) * tq, tq), tq) for t in tiles]
                    if masked:
                        flush_dv(jnp.maximum(kt - 1, 0), 1 - slot)
                    qs = [q_ref[qr, :] for qr in qrs]
                    if masked:
                        keep = krow <= (qt0 * tq + lax.broadcasted_iota(jnp.int32, (1, tq), 1))

                    def scores(h):
                        do_ts = [dot_ref[h, :, qr] for qr in qrs]
                        sts = [_dot_nt(k[:, lanes[h]], qs[t][:, lanes[h]]) for t in tiles]
                        dpts = [_dot(vv[:, lanes[h // 2]], do_ts[t]) for t in tiles]
                        return do_ts, sts, dpts

                    def softmax(h, sts, dpts):
                        pts, dsts = [], []
                        for t in tiles:
                            pt = jnp.exp2(sts[t] - lse_ref[0, 0, h:h + 1, qrs[t]])
                            if masked and t == 0:
                                pt = jnp.where(keep, pt, 0.0)
                            dsts.append((pt * (dpts[t] - dl_ref[0, h:h + 1, qrs[t]])).astype(BF16))
                            pts.append(pt.astype(BF16))
                        return pts, dsts

                    def grads(h, do_ts, pts, dsts):
                        half = slice((h % 2) * HALF, (h % 2 + 1) * HALF)
                        dst_all = jnp.concatenate(dsts, axis=1)
                        pt_all = jnp.concatenate(pts, axis=1)
                        do_all = jnp.concatenate([do_ts[t][half, :] for t in tiles], axis=1)
                        q_all = jnp.concatenate([qs[t][:, lanes[h]] for t in tiles], axis=0)
                        dvt_ref[slot, h * HALF:(h + 1) * HALF, :] += _dot_nt(do_all, pt_all)
                        dk_ref[kr, lanes[h]] += _dot(dst_all, q_all)
                        for t in tiles:
                            dqt_ref[lanes[h], qrs[t]] += _dot(k_t[lanes[h], :], dsts[t])

                    first, second = {0: scores(0)}, {}
                    for h in range(nh + 1):
                        if h + 1 < nh:
                            first[h + 1] = scores(h + 1)
                        if h < nh:
                            do_ts, sts, dpts = first.pop(h)
                            second[h] = (do_ts,) + softmax(h, sts, dpts)
                        if h >= 1:
                            grads(h - 1, *second.pop(h - 1))
                    return carry
                return q_step

            dk_ref[kr, :] = jnp.zeros((tq, nh * HEAD_LANES), F32)
            dvt_ref[slot] = jnp.zeros(dvt_ref.shape[1:], F32)
            count = nq - kt
            lax.cond(count >= 2, lambda c: make_step(True, 2)(kt, c), lambda c: make_step(True, 1)(kt, c), 0)
            lax.fori_loop(1, count // 2, lambda j, c: make_step(False, 2)(kt + 2 * j, c), 0)
            lax.cond(jnp.logical_and(count % 2 == 1, count >= 3), lambda c: make_step(False, 1)(nq - 1, c), lambda c: c, 0)
            return 0

        lax.fori_loop(0, nq, k_step, 0)
        flush_dv(nq - 1, (nq - 1) % 2)
        for t in range(nq):
            r = slice(t * tq, (t + 1) * tq)
            for h in heads:
                dq_ref[r, lanes[h]] = dqt_ref[lanes[h], r].T

    n_tok = qf.shape[0]
    groups = N_HEADS // nh
    blk = lambda w: pl.BlockSpec((seq, w), lambda b, g: (b, g))
    return pl.pallas_call(
        body, name="mla_bwd", grid=(n_seq, groups),
        out_shape=[jax.ShapeDtypeStruct((n_tok, 1024), F32), jax.ShapeDtypeStruct((n_tok, 1024), F32),
                   jax.ShapeDtypeStruct((n_tok, 512), F32)],
        in_specs=[blk(512), blk(512), blk(256), blk(256), pl.BlockSpec((1, nh, seq), lambda b, g: (g, 0, b)),
                  pl.BlockSpec((1, 1, nh, seq), lambda b, g: (b, g, 0, 0))],
        out_specs=[blk(512), blk(512), blk(256)],
        scratch_shapes=[pltpu.VMEM((nh * HEAD_LANES, seq), BF16), pltpu.VMEM((nh, HEAD_LANES, seq), BF16),
                        pltpu.VMEM((nh * HEAD_LANES, seq), F32), pltpu.VMEM((2, nh * HALF, tq), F32)],
        compiler_params=_params(2),
    )(qf, kf, v, do, delta, lse)


SWA_BLOCKS = 4


def _swa_block(n, pos_col_ref, posq):
    w = SWA_WINDOW
    start = pl.multiple_of(jnp.maximum(n - 1, 0) * w, w)
    posk = pos_col_ref[pl.ds(start, 2 * w), :]
    rel = (n * w + lax.broadcasted_iota(jnp.int32, (1, w), 1)) - (start + lax.broadcasted_iota(jnp.int32, (2 * w, 1), 0))
    valid = jnp.logical_and(rel >= 0, rel < w)
    return start, jnp.where(valid, posq - posk, 1e30)


def _alibi(h):
    return LOG2E * 2.0 ** -(h + 1)


def _transpose_rows(eye, src_ref, dst_ref, seq, width):
    step = 2 * SWA_WINDOW
    for t in range(seq // step):
        for p in range(width // HEAD_LANES):
            lanes = slice(p * HEAD_LANES, (p + 1) * HEAD_LANES)
            dst_ref[lanes, t * step:(t + 1) * step] = _dot_nt(eye, src_ref[t * step:(t + 1) * step, lanes]).astype(BF16)


def _swa_fwd_call(qs, kd, vd, pos_col, pos_row, sinks, n_seq, seq):
    w = SWA_WINDOW
    qb = SWA_BLOCKS
    steps = seq // (qb * w)
    ext = HALF + 16

    def body(q_ref, k_ref, v_ref, pc_ref, pr_ref, sink_ref, o_ref, lse_ref, vt_ref):
        n = pl.program_id(1)
        lo = _lane_lo()
        hi = jnp.logical_not(lo)
        eye = _eye()

        @pl.when(n == 0)
        def _():
            step = 2 * w
            for kv in range(2):
                vt_ref[kv * ext + HALF:(kv + 1) * ext, :] = jnp.ones((16, seq), BF16)
                for t in range(seq // step):
                    v_t = _dot_nt(eye, v_ref[t * step:(t + 1) * step, kv * HEAD_LANES:(kv + 1) * HEAD_LANES])
                    vt_ref[kv * ext:kv * ext + HALF, t * step:(t + 1) * step] = v_t[:HALF, :].astype(BF16)

        heads = range(N_HEADS)
        blocks = range(qb)
        geo = [_swa_block(n * qb + bi, pc_ref, pr_ref[bi]) for bi in blocks]
        wins = [pl.ds(g[0], 2 * w) for g in geo]
        kwins = [k_ref[win, :] for win in wins]
        vts = [vt_ref[:, win] for win in wins]
        sts = []
        for bi in blocks:
            q = q_ref[bi * w:(bi + 1) * w, :]
            sts.append([])
            for j in range(N_HEADS // 2):
                qp = q[:, j * HEAD_LANES:(j + 1) * HEAD_LANES]
                both = jnp.concatenate([jnp.where(lo, qp, jnp.zeros_like(qp)), jnp.where(hi, qp, jnp.zeros_like(qp))], axis=0)
                st = _dot_nt(kwins[bi][:, (j // 2) * HEAD_LANES:(j // 2 + 1) * HEAD_LANES], both)
                sts[bi] += [st[:, :w], st[:, w:]]
        ps, ms = [], []
        for bi in blocks:
            ps.append([])
            ms.append([])
            for h in heads:
                s = sts[bi][h] - _alibi(h) * geo[bi][1]
                m = jnp.maximum(jnp.max(s, axis=0, keepdims=True), sink_ref[0, h] * LOG2E)
                ps[bi].append(jnp.exp2(s - m).astype(BF16))
                ms[bi].append(m)
        for bi in blocks:
            ots = []
            for h in heads:
                pv = _dot(vts[bi][(h // 4) * ext:(h // 4 + 1) * ext, :], ps[bi][h])
                l = pv[HALF:HALF + 1, :] + jnp.exp2(sink_ref[0, h] * LOG2E - ms[bi][h])
                ots.append(pv[:HALF, :] * (1.0 / l))
                lse_ref[0, h:h + 1, bi * w:(bi + 1) * w] = ms[bi][h] + jnp.log2(l)
            o_ref[bi * w:(bi + 1) * w, :] = jnp.concatenate(ots, axis=0).T

    n_tok = qs.shape[0]
    tok = lambda width: pl.BlockSpec((qb * w, width), lambda b, n: (b * steps + n, 0))
    whole = lambda width: pl.BlockSpec((seq, width), lambda b, n: (b, 0))
    return pl.pallas_call(
        body, name="swa_fwd", grid=(n_seq, steps),
        out_shape=[jax.ShapeDtypeStruct((n_tok, 512), F32), jax.ShapeDtypeStruct((n_seq, N_HEADS, seq), F32)],
        in_specs=[tok(512), whole(256), whole(256), whole(1), pl.BlockSpec((qb, 1, w), lambda b, n: (b * steps + n, 0, 0)),
                  pl.BlockSpec(memory_space=pltpu.SMEM)],
        out_specs=[tok(512), pl.BlockSpec((1, N_HEADS, qb * w), lambda b, n: (b, 0, n))],
        scratch_shapes=[pltpu.VMEM((2 * ext, seq), BF16)],
        compiler_params=_params(2),
    )(qs, kd, vd, pos_col, pos_row, sinks)


def _swa_bwd_call(qs, kd, vd, do, delta, lse, pos_col, pos_row, sinks, n_seq, seq):
    w = SWA_WINDOW
    qb = SWA_BLOCKS
    steps = seq // (qb * w)

    def body(q_ref, k_ref, v_ref, do_ref, dl_ref, lse_ref, pc_ref, pr_ref, sink_ref, dq_ref, dk_ref, dv_ref, dsink_ref,
             kt_ref):
        b, n = pl.program_id(0), pl.program_id(1)
        lo = _lane_lo()
        hi = jnp.logical_not(lo)
        sub_lo = lax.broadcasted_iota(jnp.int32, (HEAD_LANES, 1), 0) < HALF
        eye = _eye()

        @pl.when(n == 0)
        def _():
            dk_ref[...] = jnp.zeros_like(dk_ref)
            dv_ref[...] = jnp.zeros_like(dv_ref)
            _transpose_rows(eye, k_ref, kt_ref, seq, 2 * HEAD_LANES)

        @pl.when(jnp.logical_and(n == 0, b == 0))
        def _():
            dsink_ref[...] = jnp.zeros_like(dsink_ref)

        heads = range(N_HEADS)
        blocks = range(qb)
        kv_lanes = lambda h: slice((h // 4) * HEAD_LANES, (h // 4 + 1) * HEAD_LANES)
        geo = [_swa_block(n * qb + bi, pc_ref, pr_ref[bi]) for bi in blocks]
        wins = [pl.ds(g[0], 2 * w) for g in geo]
        kwins = [k_ref[win, :] for win in wins]
        vwins = [v_ref[win, :] for win in wins]

        do_ts, deltas, qms, doms = [], [], [], []
        for bi in blocks:
            rows = slice(bi * w, (bi + 1) * w)
            for lst in (do_ts, deltas, qms, doms):
                lst.append([])
            for j in range(N_HEADS // 2):
                pair = slice(j * HEAD_LANES, (j + 1) * HEAD_LANES)
                dop = do_ref[rows, pair]
                qp = q_ref[rows, pair]
                dt = _dot_nt(eye, dop)
                for hh in range(2):
                    half = lo if hh == 0 else hi
                    do_ts[bi].append(jnp.where(sub_lo, dt, 0.0).astype(BF16) if hh == 0
                                     else jnp.where(sub_lo, 0.0, dt).astype(BF16))
                    deltas[bi].append(dl_ref[2 * j + hh:2 * j + hh + 1, rows])
                    qms[bi].append(jnp.where(half, qp, jnp.zeros_like(qp)))
                    doms[bi].append(jnp.where(half, dop, jnp.zeros_like(dop)))
        sts, dpts = [], []
        for bi in blocks:
            sts.append([])
            dpts.append([])
            for j in range(N_HEADS // 2):
                a, b = 2 * j, 2 * j + 1
                st = _dot_nt(kwins[bi][:, kv_lanes(a)], jnp.concatenate([qms[bi][a], qms[bi][b]], axis=0))
                dpt = _dot(vwins[bi][:, kv_lanes(a)], jnp.concatenate([do_ts[bi][a], do_ts[bi][b]], axis=1))
                sts[bi] += [st[:, :w], st[:, w:]]
                dpts[bi] += [dpt[:, :w], dpt[:, w:]]
        pts, dsts = [], []
        for bi in blocks:
            pts.append([])
            dsts.append([])
            for h in heads:
                lse_h = lse_ref[0, h:h + 1, bi * w:(bi + 1) * w]
                pt = jnp.exp2(sts[bi][h] - _alibi(h) * geo[bi][1] - lse_h)
                dsts[bi].append((pt * (dpts[bi][h] - deltas[bi][h])).astype(BF16))
                pts[bi].append(pt.astype(BF16))
                dsink_ref[h:h + 1, :] += -jnp.exp2(sink_ref[0, h] * LOG2E - lse_h) * deltas[bi][h]
        for bi in blocks:
            for kv in range(2):
                group = range(4 * kv, 4 * kv + 4)
                dst_all = jnp.concatenate([dsts[bi][h] for h in group], axis=1)
                pt_all = jnp.concatenate([pts[bi][h] for h in group], axis=1)
                q_all = jnp.concatenate([qms[bi][h] for h in group], axis=0)
                do_all = jnp.concatenate([doms[bi][h] for h in group], axis=0)
                dk_ref[wins[bi], kv_lanes(4 * kv)] += _dot(dst_all, q_all)
                dv_ref[wins[bi], kv_lanes(4 * kv)] += _dot(pt_all, do_all)
        for bi in blocks:
            ktw = kt_ref[:, wins[bi]]
            for j in range(N_HEADS // 2):
                k_t = ktw[kv_lanes(2 * j), :]
                both = _dot(k_t, jnp.concatenate([dsts[bi][2 * j], dsts[bi][2 * j + 1]], axis=1))
                dq_t = jnp.where(sub_lo, both[:, :w], both[:, w:])
                dq_ref[bi * w:(bi + 1) * w, j * HEAD_LANES:(j + 1) * HEAD_LANES] = dq_t.T * SWA_SCALE

    n_tok = qs.shape[0]
    tok = lambda width: pl.BlockSpec((qb * w, width), lambda b, n: (b * steps + n, 0))
    whole = lambda width: pl.BlockSpec((seq, width), lambda b, n: (b, 0))
    return pl.pallas_call(
        body, name="swa_bwd", grid=(n_seq, steps),
        out_shape=[jax.ShapeDtypeStruct((n_tok, 512), F32), jax.ShapeDtypeStruct((n_tok, 256), F32),
                   jax.ShapeDtypeStruct((n_tok, 256), F32), jax.ShapeDtypeStruct((N_HEADS, HEAD_LANES), F32)],
        in_specs=[tok(512), whole(256), whole(256), pl.BlockSpec((qb * w, 512), lambda b, n: (b * steps + n, 1)),
                  pl.BlockSpec((N_HEADS, qb * w), lambda b, n: (0, b * steps + n)),
                  pl.BlockSpec((1, N_HEADS, qb * w), lambda b, n: (b, 0, n)),
                  whole(1), pl.BlockSpec((qb, 1, w), lambda b, n: (b * steps + n, 0, 0)),
                  pl.BlockSpec(memory_space=pltpu.SMEM)],
        out_specs=[tok(512), whole(256), whole(256), _full((N_HEADS, HEAD_LANES))],
        scratch_shapes=[pltpu.VMEM((2 * HEAD_LANES, seq), BF16)],
        compiler_params=_params(2),
    )(qs, kd, vd, do, delta, lse, pos_col, pos_row, sinks)


def _post_call(x, target, o_mla, o_swa, gates, mod, b_ada, fg, w_out, seq):
    n_tok = x.shape[0]
    tm = min(TOKEN_TILE, seq)
    per_seq = seq // tm
    n_seq = n_tok // seq

    def body(x_ref, t_ref, om_ref, os_ref, g_ref, mod_ref, bada_ref, fg_ref, w_ref,
             dx2_ref, do_ref, dg_ref, gw_ref, gfg_ref, dgate_ref, loss_ref, dmla_ref, dswa_ref):
        i = pl.program_id(0)

        @pl.when(i == 0)
        def _():
            gw_ref[...] = jnp.zeros_like(gw_ref)
            gfg_ref[...] = jnp.zeros_like(gfg_ref)
            loss_ref[...] = jnp.zeros_like(loss_ref)

        @pl.when(i % per_seq == 0)
        def _():
            dgate_ref[...] = jnp.zeros_like(dgate_ref)

        gate = mod_ref[0][:, 2 * D_MODEL:] + bada_ref[:, 2 * D_MODEL:]
        fgv = fg_ref[...]
        fgd = fgv * (1.0 / D_MODEL)
        subs = _sub_tiles(tm)
        gs = [g_ref[r, :] for r in subs]
        os_ = [jnp.concatenate([om_ref[r, :], os_ref[r, :]], axis=-1) for r in subs]
        sgs = [_sigmoid(g) for g in gs]
        sils = [g * sg for g, sg in zip(gs, sgs)]
        ypres = [(o * sil).astype(BF16) for o, sil in zip(os_, sils)]
        ys = [_dot(ypre, w_ref[...]) for ypre in ypres]
        dys, loss, gfg, dgate = [], 0.0, 0.0, 0.0
        for r, y in zip(subs, ys):
            x2 = x_ref[r, :] + gate * y
            r2 = lax.rsqrt(jnp.mean(x2 * x2, axis=-1, keepdims=True) + EPS)
            xn2 = x2 * r2
            err = xn2 * fgv - t_ref[r, :]
            loss = loss + jnp.sum(jnp.sum(err * err, axis=-1, keepdims=True), axis=0, keepdims=True)
            gfg = gfg + jnp.sum(err * xn2, axis=0, keepdims=True)
            dxn2 = err * fgd
            dx2 = r2 * (dxn2 - xn2 * jnp.mean(dxn2 * xn2, axis=-1, keepdims=True))
            dx2_ref[r, :] = dx2
            dgate = dgate + jnp.sum(dx2 * y, axis=0, keepdims=True)
            dys.append((dx2 * gate).astype(BF16))
        loss_ref[...] += jnp.broadcast_to(loss * (0.5 / D_MODEL), loss_ref.shape)
        gfg_ref[...] += gfg * (1.0 / D_MODEL)
        dgate_ref[0] += dgate
        gw_ref[...] += _dot_tn(jnp.concatenate(ypres, axis=0), jnp.concatenate(dys, axis=0))
        dypres = [_dot_nt(dy, w_ref[...]) for dy in dys]
        pick = jnp.where(jnp.right_shift(lax.broadcasted_iota(jnp.int32, (2 * N_HEADS, D_MODEL), 1), 6)
                         == lax.broadcasted_iota(jnp.int32, (2 * N_HEADS, D_MODEL), 0), 1.0, 0.0).astype(BF16)
        for r, dypre, o, g, sg, sil in zip(subs, dypres, os_, gs, sgs, sils):
            dov = (dypre * sil).astype(BF16)
            do_ref[r, :] = dov
            delta = _dot_nt(pick, (dov.astype(F32) * o).astype(BF16))
            for grp in range(2):
                dmla_ref[grp, :, r] = delta[4 * grp:4 * grp + 4, :]
            dswa_ref[:, r] = delta[N_HEADS:, :]
            dg_ref[r, :] = (dypre * o * (sg + sil * (1.0 - sg))).astype(BF16)

    tok = lambda w: pl.BlockSpec((tm, w), lambda i: (i, 0))
    per_b = pl.BlockSpec((1, 1, 3 * D_MODEL), lambda i: (i // per_seq, 0, 0))
    return pl.pallas_call(
        body, name="post", grid=(n_tok // tm,),
        out_shape=[jax.ShapeDtypeStruct((n_tok, D_MODEL), F32), jax.ShapeDtypeStruct((n_tok, D_MODEL), BF16),
                   jax.ShapeDtypeStruct((n_tok, D_MODEL), BF16), jax.ShapeDtypeStruct((D_MODEL, D_MODEL), F32),
                   jax.ShapeDtypeStruct((1, D_MODEL), F32), jax.ShapeDtypeStruct((n_seq, 1, D_MODEL), F32),
                   jax.ShapeDtypeStruct((1, HEAD_LANES), F32),
                   jax.ShapeDtypeStruct((2, N_HEADS // 2, n_tok), F32), jax.ShapeDtypeStruct((N_HEADS, n_tok), F32)],
        in_specs=[tok(D_MODEL), tok(D_MODEL), tok(512), tok(512), tok(D_MODEL), per_b, _full(b_ada.shape),
                  _full(fg.shape), _full(w_out.shape)],
        out_specs=[tok(D_MODEL), tok(D_MODEL), tok(D_MODEL), _full((D_MODEL, D_MODEL)), _full((1, D_MODEL)),
                   pl.BlockSpec((1, 1, D_MODEL), lambda i: (i // per_seq, 0, 0)), _full((1, HEAD_LANES)),
                   pl.BlockSpec((2, N_HEADS // 2, tm), lambda i: (0, 0, i)), pl.BlockSpec((N_HEADS, tm), lambda i: (0, i))],
        compiler_params=_params(1),
    )(x, target, o_mla, o_swa, gates, mod, b_ada, fg, w_out)


def _mid_bwd_call(dqf, dkf, dv, zqkv, rope, qg, kvg, wq2, wkv, seq):
    n_tok = dqf.shape[0]
    tm = min(TOKEN_TILE, seq)

    def body(dq_ref, dk_ref, dv_ref, z_ref, rope_ref, qg_ref, kvg_ref, wq_ref, wkv_ref,
             dz_ref, gwq_ref, gwkv_ref, gqg_ref, gkvg_ref):
        i = pl.program_id(0)

        @pl.when(i == 0)
        def _():
            gwq_ref[...] = jnp.zeros_like(gwq_ref)
            gwkv_ref[...] = jnp.zeros_like(gwkv_ref)
            gqg_ref[...] = jnp.zeros_like(gqg_ref)
            gkvg_ref[...] = jnp.zeros_like(gkvg_ref)

        cos, sin = rope_ref[:, :HEAD_LANES], rope_ref[:, HEAD_LANES:]
        cf, sf = jnp.tile(cos, (1, N_HEADS)), jnp.tile(sin, (1, N_HEADS))
        dq = dq_ref[...] * MLA_SCALE
        dqr = jnp.concatenate([dq * cf, dq * sf], axis=-1).astype(BF16)
        zq, zkv = z_ref[:, :Q_LORA], z_ref[:, Q_LORA:]
        qgv, kvgv = qg_ref[...], kvg_ref[...]

        rq = lax.rsqrt(jnp.mean(zq * zq, axis=-1, keepdims=True) + EPS)
        xq = zq * rq
        gwq_ref[...] += _dot_tn((xq * qgv).astype(BF16), dqr)
        dqn = _dot_nt(dqr, wq_ref[...])
        gqg_ref[...] += jnp.sum(dqn * xq, axis=0, keepdims=True)
        dxq = dqn * qgv
        dz_ref[:, :Q_LORA] = (rq * (dxq - xq * jnp.mean(dxq * xq, axis=-1, keepdims=True))).astype(BF16)

        dk = dk_ref[...] * LN2
        dkv = jnp.concatenate([dk, dv_ref[...]], axis=-1).astype(BF16)
        rkv = lax.rsqrt(jnp.mean(zkv * zkv, axis=-1, keepdims=True) + EPS)
        xkv = zkv * rkv
        gwkv_ref[...] += _dot_tn((xkv * kvgv).astype(BF16), dkv)
        dkvn = _dot_nt(dkv, wkv_ref[...])
        gkvg_ref[...] += jnp.sum(dkvn * xkv, axis=0, keepdims=True)
        dxkv = dkvn * kvgv
        dz_ref[:, Q_LORA:A_KR] = (rkv * (dxkv - xkv * jnp.mean(dxkv * xkv, axis=-1, keepdims=True))).astype(BF16)

        dkpe = dk[:, :HEAD_LANES]
        for h in range(1, N_HEADS):
            dkpe = dkpe + dk[:, h * HEAD_LANES:(h + 1) * HEAD_LANES]
        dz_ref[:, A_KR:] = (jnp.where(_lane_lo(), 0.0, dkpe * cos) + pltpu.roll(dkpe * sin, HALF, 1)).astype(BF16)

    tok = lambda w: pl.BlockSpec((tm, w), lambda i: (i, 0))
    return pl.pallas_call(
        body, name="mid_bwd", grid=(n_tok // tm,),
        out_shape=[jax.ShapeDtypeStruct((n_tok, A_GM), BF16),
                   jax.ShapeDtypeStruct(wq2.shape, F32), jax.ShapeDtypeStruct(wkv.shape, F32),
                   jax.ShapeDtypeStruct((1, Q_LORA), F32), jax.ShapeDtypeStruct((1, KV_LORA), F32)],
        in_specs=[tok(1024), tok(1024), tok(512), tok(640), tok(2 * HEAD_LANES), _full(qg.shape), _full(kvg.shape),
                  _full(wq2.shape), _full(wkv.shape)],
        out_specs=[tok(A_GM), _full(wq2.shape), _full(wkv.shape), _full((1, Q_LORA)), _full((1, KV_LORA))],
        compiler_params=_params(1),
    )(dqf, dkf, dv, zqkv, rope, qg, kvg, wq2, wkv)


def _in_bwd_call(x, dx2, dz, dg, dqs, dkd, dvd, mod, b_ada, ng, wa, seq):
    n_tok = x.shape[0]
    tm = min(TOKEN_TILE, seq)
    per_seq = seq // tm
    n_seq = n_tok // seq

    def body(x_ref, dx2_ref, dz_ref, dg_ref, dqs_ref, dkd_ref, dvd_ref, mod_ref, bada_ref, ng_ref,
             wa_ref, gx_ref, gwa_ref, gng_ref, dshift_ref, dscale_ref):
        i = pl.program_id(0)

        @pl.when(i == 0)
        def _():
            gwa_ref[...] = jnp.zeros_like(gwa_ref)
            gng_ref[...] = jnp.zeros_like(gng_ref)

        @pl.when(i % per_seq == 0)
        def _():
            dshift_ref[...] = jnp.zeros_like(dshift_ref)
            dscale_ref[...] = jnp.zeros_like(dscale_ref)

        xv = x_ref[...]
        modv = mod_ref[0] + bada_ref[...]
        shift, scale = modv[:, :D_MODEL], modv[:, D_MODEL:2 * D_MODEL]
        ngv = ng_ref[...]
        r1 = lax.rsqrt(jnp.mean(xv * xv, axis=-1, keepdims=True) + EPS)
        xn = xv * r1
        hb = ((xn * ngv) * (1.0 + scale) + shift).astype(BF16)

        dgv = dg_ref[...]
        pieces = [(A_ZQ, dz_ref[...]), (A_GM, dgv[:, :512]), (A_QS, dqs_ref[...].astype(BF16)),
                  (A_KS, jnp.concatenate([_once(dkd_ref[...]) * LN2, _once(dvd_ref[...])], axis=1).astype(BF16)),
                  (A_GS, dgv[:, 512:])]
        dh = None
        for off, piece in pieces:
            wd = piece.shape[1]
            gwa_ref[:, off:off + wd] += _dot_tn(hb, piece)
            term = _dot_nt(piece, wa_ref[:, off:off + wd])
            dh = term if dh is None else dh + term

        dshift_ref[0] += jnp.sum(dh, axis=0, keepdims=True)
        dscale_ref[0] += jnp.sum(dh * (xn * ngv), axis=0, keepdims=True)
        gng_ref[...] += jnp.sum(dh * xn * (1.0 + scale), axis=0, keepdims=True)
        dxn = dh * ngv * (1.0 + scale)
        gx_ref[...] = dx2_ref[...] + r1 * (dxn - xn * jnp.mean(dxn * xn, axis=-1, keepdims=True))

    tok = lambda w: pl.BlockSpec((tm, w), lambda i: (i, 0))
    per_b = lambda w: pl.BlockSpec((1, 1, w), lambda i: (i // per_seq, 0, 0))
    return pl.pallas_call(
        body, name="in_bwd", grid=(n_tok // tm,),
        out_shape=[jax.ShapeDtypeStruct((n_tok, D_MODEL), F32), jax.ShapeDtypeStruct((D_MODEL, A_END), F32),
                   jax.ShapeDtypeStruct((1, D_MODEL), F32),
                   jax.ShapeDtypeStruct((n_seq, 1, D_MODEL), F32), jax.ShapeDtypeStruct((n_seq, 1, D_MODEL), F32)],
        in_specs=[tok(D_MODEL), tok(D_MODEL), tok(A_GM), tok(D_MODEL), tok(512), tok(256), tok(256),
                  per_b(3 * D_MODEL), _full(b_ada.shape), _full(ng.shape), _full(wa.shape)],
        out_specs=[tok(D_MODEL), _full((D_MODEL, A_END)), _full((1, D_MODEL)), per_b(D_MODEL), per_b(D_MODEL)],
        compiler_params=_params(1),
    )(x, dx2, dz, dg, dqs, dkd, dvd, mod, b_ada, ng, wa)


def _adam_math(w, g, m, v):
    m_new = ADAM_B1 * m + (1.0 - ADAM_B1) * g
    v_new = ADAM_B2 * v + (1.0 - ADAM_B2) * (g * g)
    m_hat = m_new / (1.0 - ADAM_B1 ** ADAM_STEP)
    v_hat = v_new / (1.0 - ADAM_B2 ** ADAM_STEP)
    delta = -ADAM_LR * (m_hat / (jnp.sqrt(v_hat) + ADAM_EPS) + ADAM_WD * w)
    return delta, m_new, v_new


def _adam_call(name, w, g, m, v):
    rows, cols = w.shape
    tr = next((t for t in (256, 128, 88) if rows % t == 0), rows)

    def body(w_ref, g_ref, m_ref, v_ref, d_ref, mo_ref, vo_ref):
        d, mn, vn = _adam_math(w_ref[...], g_ref[...], m_ref[...], v_ref[...])
        d_ref[...] = d
        mo_ref[...] = mn
        vo_ref[...] = vn

    spec = pl.BlockSpec((tr, cols), lambda i: (i, 0))
    return pl.pallas_call(
        body, name=name, grid=(rows // tr,),
        out_shape=[jax.ShapeDtypeStruct(w.shape, F32)] * 3,
        in_specs=[spec] * 4, out_specs=[spec] * 3,
        compiler_params=_params(1),
    )(w, g, m, v)


def _ada_bwd_call(act_all, dmod_cols, w, m, v):
    rows, cols = w.shape
    tr = 256

    def body(a_ref, dm_ref, w_ref, m_ref, v_ref, g_ref, d_ref, mo_ref, vo_ref):
        g = _dot_tn(a_ref[...].astype(BF16), dm_ref[...].astype(BF16))
        d, mn, vn = _adam_math(w_ref[...], g, m_ref[...], v_ref[...])
        g_ref[...] = g
        d_ref[...] = d
        mo_ref[...] = mn
        vo_ref[...] = vn

    spec = pl.BlockSpec((tr, cols), lambda i: (i, 0))
    nb = act_all.shape[0]
    return pl.pallas_call(
        body, name="ada_bwd", grid=(rows // tr,),
        out_shape=[jax.ShapeDtypeStruct(w.shape, F32)] * 4,
        in_specs=[pl.BlockSpec((nb, tr), lambda i: (0, i)), _full(dmod_cols.shape), spec, spec, spec],
        out_specs=[spec] * 4,
        compiler_params=_params(1),
    )(act_all, dmod_cols, w, m, v)


SMALL_ROW = {"norm_gain": (0, 1024), "final_gain": (1024, 2048), "q_norm_gain": (2048, 2432),
             "kv_norm_gain": (2432, 2688), "swa_sinks": (2688, 2696), "loss": (2816, 2944)}
SMALL_ORDER = ("b_ada", "norm_gain", "q_norm_gain", "kv_norm_gain", "swa_sinks", "final_gain")


def _small_call(parts_all, n_seq, params):
    k = len(params)

    def body(p_ref, *refs):
        ins, outs, loss_ref = refs[:3 * k], refs[3 * k:7 * k], refs[7 * k]
        row = p_ref[n_seq:n_seq + 1, :]
        for dv in range(1, 8):
            r0 = dv * ROWS_PER_DEVICE + n_seq
            row = row + p_ref[r0:r0 + 1, :]
        gb = None
        for dv in range(8):
            for r in range(n_seq):
                r0 = dv * ROWS_PER_DEVICE + r
                gb = p_ref[r0:r0 + 1, :] if gb is None else gb + p_ref[r0:r0 + 1, :]
        for j, name in enumerate(SMALL_ORDER):
            g = gb if name == "b_ada" else row[:, SMALL_ROW[name][0]:SMALL_ROW[name][1]]
            d, mn, vn = _adam_math(ins[3 * j][...], g, ins[3 * j + 1][...], ins[3 * j + 2][...])
            outs[4 * j][...] = g
            outs[4 * j + 1][...] = d
            outs[4 * j + 2][...] = mn
            outs[4 * j + 3][...] = vn
        loss_ref[...] = row[:, SMALL_ROW["loss"][0]:SMALL_ROW["loss"][1]]

    flat = [t for p in params for t in p]
    res = pl.pallas_call(
        body, name="small_update", grid=(1,),
        out_shape=[jax.ShapeDtypeStruct(p[0].shape, F32) for p in params for _ in range(4)]
        + [jax.ShapeDtypeStruct((1, HEAD_LANES), F32)],
        in_specs=[_full(parts_all.shape)] + [_full(t.shape) for t in flat],
        out_specs=[_full(p[0].shape) for p in params for _ in range(4)] + [_full((1, HEAD_LANES))],
        compiler_params=_params(1),
    )(parts_all, *flat)
    return [res[4 * j:4 * j + 4] for j in range(k)], res[4 * k]


def _rot(t):
    half = t.shape[-1] // 2
    return jnp.concatenate([-t[..., half:], t[..., :half]], axis=-1)


def _rot_t(g):
    half = g.shape[-1] // 2
    return jnp.concatenate([g[..., half:], -g[..., :half]], axis=-1)


def _columns(segments, lo, hi):
    out, at = [], 0
    for seg in segments:
        n = seg.shape[1]
        a, b = max(lo, at), min(hi, at + n)
        if a < b:
            out.append(seg[:, a - at:b - at])
        at += n
    return out


def _prepare_weights(w_in_blocks, w_uq, w_ukv):
    o = [0]
    for s in IN_SPLITS:
        o.append(o[-1] + s)
    part = lambda a, b: _columns(w_in_blocks, a, b)
    kr = jnp.concatenate(part(o[2], o[3]), axis=1)
    zero = jnp.zeros((kr.shape[0], 32), kr.dtype)
    wa = jnp.concatenate(part(0, o[2]) + [_rot(kr), zero, kr, zero] + part(o[3], o[8]), axis=1)
    uq = w_uq.reshape(Q_LORA, N_HEADS, MLA_NOPE + MLA_ROPE)
    zq = jnp.zeros((Q_LORA, N_HEADS, 32), w_uq.dtype)
    uq_full = jnp.concatenate([uq, zq], axis=-1).reshape(Q_LORA, 1024)
    uq_rot = jnp.concatenate([jnp.zeros((Q_LORA, N_HEADS, 64), w_uq.dtype), _rot(uq[..., MLA_NOPE:]), zq],
                             axis=-1).reshape(Q_LORA, 1024)
    wq2 = jnp.concatenate([uq_full, uq_rot], axis=1)
    ukv = w_ukv.reshape(KV_LORA, N_HEADS, 128)
    k_full = jnp.concatenate([ukv[..., :64], jnp.zeros((KV_LORA, N_HEADS, 64), w_ukv.dtype)], axis=-1).reshape(KV_LORA, 1024)
    wkv = jnp.concatenate([k_full, ukv[..., 64:].reshape(KV_LORA, 512)], axis=1)
    return wa, wq2, wkv


def _restore_grads(gwa, gwq2, gwkv):
    gkr = gwa[:, A_KR + 64:A_KR + 96] + _rot_t(gwa[:, A_KR:A_KR + 32])
    in_order = [gwa[:, :A_KR], gkr, gwa[:, A_GM:]]
    n = D_IN // 4
    g_in = [jnp.concatenate(_columns(in_order, k * n, (k + 1) * n), axis=1) for k in range(4)]
    gf = gwq2[:, :1024].reshape(Q_LORA, N_HEADS, 128)
    gr = gwq2[:, 1024:].reshape(Q_LORA, N_HEADS, 128)
    g_uq = jnp.concatenate([gf[..., :64], gf[..., 64:96] + _rot_t(gr[..., 64:96])], axis=-1).reshape(Q_LORA, 768)
    gk = gwkv[:, :1024].reshape(KV_LORA, N_HEADS, 128)[..., :64]
    gv = gwkv[:, 1024:].reshape(KV_LORA, N_HEADS, 64)
    g_ukv = jnp.concatenate([gk, gv], axis=-1).reshape(KV_LORA, 1024)
    return g_in, g_uq, g_ukv


def _local_step(x, positions, target, mod_rows, b_ada, ng, qg, kvg, sinks, fg, w_in_b, w_uq_b, w_ukv_b, w_out_half):
    n_seq, seq, _ = x.shape
    n_tok = n_seq * seq
    x2d = x.reshape(n_tok, D_MODEL)
    t2d = target.reshape(n_tok, D_MODEL)
    pos_f = positions.astype(F32)
    pos_col = pos_f.reshape(n_tok, 1)
    pos_row = pos_f.reshape(n_tok // SWA_WINDOW, 1, SWA_WINDOW)
    mod3 = mod_rows.reshape(n_seq, 1, 3 * D_MODEL)
    inv = ROPE_THETA ** (-jnp.arange(0, MLA_ROPE, 2, dtype=F32) / MLA_ROPE)
    inv128 = jnp.concatenate([jnp.zeros((64,), F32), inv, inv, jnp.zeros((32,), F32)]).reshape(1, 128)
    fg2 = fg.reshape(1, D_MODEL)

    wa, wq2, wkv = _prepare_weights(w_in_b, w_uq_b, w_ukv_b)

    zqkv, gates, qf, kf, v, qs, kd, vd, rope, f_out = _pre_call(x2d, pos_col, mod3, b_ada, ng, qg, kvg, inv128, wa, wq2,
                                                                 wkv, w_out_half, seq)
    w_out_b = f_out.reshape(D_MODEL, D_MODEL)
    o_mla, lse_mla = _mla_fwd_call(qf, kf, v, n_seq, seq)
    o_swa, lse_swa = _swa_fwd_call(qs, kd, vd, pos_col, pos_row, sinks, n_seq, seq)
    dx2, do, dg, g_out, g_fg, dgate, loss, delta_mla, delta_swa = _post_call(x2d, t2d, o_mla, o_swa, gates, mod3, b_ada, fg2, w_out_b, seq)
    dqf, dkf, dv = _mla_bwd_call(qf, kf, v, do, delta_mla, lse_mla, n_seq, seq)
    dqs, dkd, dvd, dsink = _swa_bwd_call(qs, kd, vd, do, delta_swa, lse_swa, pos_col, pos_row, sinks, n_seq, seq)
    dz, g_wq2, g_wkv, g_qg, g_kvg = _mid_bwd_call(dqf, dkf, dv, zqkv, rope, qg, kvg, wq2, wkv, seq)
    gx, g_wa, g_ng, dshift, dscale = _in_bwd_call(x2d, dx2, dz, dg, dqs, dkd, dvd, mod3, b_ada, ng, wa, seq)
    g_in, g_uq, g_ukv = _restore_grads(g_wa, g_wq2, g_wkv)
    dmod = jnp.concatenate([dshift, dscale, dgate], axis=-1).reshape(n_seq, 3 * D_MODEL)
    small_row = jnp.concatenate([g_ng, g_fg, g_qg, g_kvg, jnp.pad(jnp.sum(dsink, axis=1).reshape(1, N_HEADS), ((0, 0), (0, 120))),
                                 loss, jnp.zeros((1, 128), F32)], axis=1)
    return gx.reshape(x.shape), (g_in, g_uq, g_ukv, g_out), small_row, dmod


def kernel(x, c, positions, w_ada, b_ada, norm_gain, w_in, q_norm_gain, kv_norm_gain, w_uq, w_ukv, swa_sinks, w_out, final_gain, loss_target, m_w_ada, m_b_ada, m_norm_gain, m_w_in, m_q_norm_gain, m_kv_norm_gain, m_w_uq, m_w_ukv, m_swa_sinks, m_w_out, m_final_gain, v_w_ada, v_b_ada, v_norm_gain, v_w_in, v_q_norm_gain, v_kv_norm_gain, v_w_uq, v_w_ukv, v_swa_sinks, v_w_out, v_final_gain):
    n_seq = x.shape[0]
    xi, yi, ci = lax.axis_index("x"), lax.axis_index("y"), lax.axis_index("c")
    dev = 4 * xi + 2 * yi + ci
    chip = 2 * xi + yi

    halves = lambda w: w.astype(BF16).reshape(2, w.shape[0] // 2, w.shape[1])
    c_blk = jnp.pad(c, ((0, ROWS_PER_DEVICE - n_seq), (0, 0)))
    act_all, pieces, f_in, f_uq, f_ukv = _comm_fwd_call(c_blk, w_ada[0], [halves(w_in[0]), halves(w_uq[0]), halves(w_ukv[0])])
    mine = lax.dynamic_slice_in_dim(pieces, dev * ROWS_PER_DEVICE, n_seq, axis=1)
    mod_rows = jnp.transpose(mine, (1, 0, 2)).reshape(n_seq, 3 * D_MODEL)
    cols = lambda t, r: jnp.transpose(t.reshape(4, r, -1), (1, 0, 2)).reshape(r, -1)
    w_in_blocks = [f_in[k].reshape(D_MODEL, -1) for k in range(4)]
    w_uq_b, w_ukv_b = cols(f_uq, Q_LORA), cols(f_ukv, KV_LORA)

    gx, (g_in_blocks, g_uq, g_ukv, g_out), small_row, dmod = _local_step(
        x, positions, loss_target, mod_rows, b_ada, norm_gain, q_norm_gain, kv_norm_gain, swa_sinks, final_gain,
        w_in_blocks, w_uq_b, w_ukv_b, halves(w_out[0]))

    by_owner = lambda g, n: jnp.transpose(g.reshape(g.shape[0], 4, n), (1, 0, 2)).reshape(4, 2, g.shape[0] // 2, n)
    grads = [jnp.stack(g_in_blocks).reshape(4, 2, D_MODEL // 2, -1), by_owner(g_uq, 192), by_owner(g_ukv, 256),
             g_out.reshape(4, 2, 128, D_MODEL)]
    part = jnp.concatenate([dmod, small_row, jnp.zeros((ROWS_PER_DEVICE - n_seq - 1, 3 * D_MODEL), F32)], axis=0)
    r_in, r_uq, r_ukv, r_out, parts_all = _comm_bwd_call(grads, part)
    g_in_s, g_uq_s = r_in.reshape(w_in.shape[1:]), r_uq.reshape(w_uq.shape[1:])
    g_ukv_s, g_out_s = r_ukv.reshape(w_ukv.shape[1:]), r_out.reshape(w_out.shape[1:])

    tr = lambda a: jnp.swapaxes(a[0], 0, 1)
    back = lambda ts: [jnp.swapaxes(t, 0, 1) for t in ts]
    d_in, nm_in, nv_in = back(_adam_call("adam_w_in", tr(w_in), g_in_s.T, tr(m_w_in), tr(v_w_in)))
    d_uq, nm_uq, nv_uq = back(_adam_call("adam_w_uq", tr(w_uq), g_uq_s.T, tr(m_w_uq), tr(v_w_uq)))
    d_ukv, nm_ukv, nv_ukv = _adam_call("adam_w_ukv", w_ukv[0], g_ukv_s, m_w_ukv[0], v_w_ukv[0])
    d_out, nm_out, nv_out = _adam_call("adam_w_out", w_out[0], g_out_s, m_w_out[0], v_w_out[0])
    dmod_cols = lax.dynamic_slice_in_dim(parts_all, chip * 768, 768, axis=1)
    g_ada, d_ada, nm_ada, nv_ada = _ada_bwd_call(act_all, dmod_cols, w_ada[0], m_w_ada[0], v_w_ada[0])

    row = lambda t: t.reshape(1, -1)
    small = {"b_ada": (b_ada, m_b_ada, v_b_ada), "norm_gain": (norm_gain, m_norm_gain, v_norm_gain),
             "q_norm_gain": (q_norm_gain, m_q_norm_gain, v_q_norm_gain),
             "kv_norm_gain": (kv_norm_gain, m_kv_norm_gain, v_kv_norm_gain),
             "swa_sinks": (swa_sinks, m_swa_sinks, v_swa_sinks),
             "final_gain": (row(final_gain), row(m_final_gain), row(v_final_gain))}
    res, loss_row = _small_call(parts_all, n_seq, [small[name] for name in SMALL_ORDER])
    res = dict(zip(SMALL_ORDER, res))
    res["final_gain"] = [t.reshape(-1) for t in res["final_gain"]]
    e = lambda t: t[None]
    big = {"w_ada": (e(g_ada), e(d_ada), e(nm_ada), e(nv_ada)), "w_in": (e(g_in_s), e(d_in), e(nm_in), e(nv_in)),
           "w_uq": (e(g_uq_s), e(d_uq), e(nm_uq), e(nv_uq)), "w_ukv": (e(g_ukv_s), e(d_ukv), e(nm_ukv), e(nv_ukv)),
           "w_out": (e(g_out_s), e(d_out), e(nm_out), e(nv_out))}
    order = ("w_ada", "b_ada", "norm_gain", "w_in", "q_norm_gain", "kv_norm_gain", "w_uq", "w_ukv", "swa_sinks", "w_out",
             "final_gain")
    pick = lambda kind: [(big[n] if n in big else res[n])[kind] for n in order]
    return (loss_row[0, 0], gx, *pick(0), *pick(1), *pick(2), *pick(3))
```

```python
import jax
import jax.numpy as jnp
from jax import lax
from jax.experimental import pallas as pl
from jax.experimental.pallas import tpu as pltpu

F32 = jnp.float32
BF16 = jnp.bfloat16

D_MODEL = 1024
Q_LORA = 384
KV_LORA = 256
N_HEADS = 8
MLA_NOPE = 64
MLA_ROPE = 32
HEAD_LANES = 128
HALF = 64
SWA_WINDOW = 128
EPS = 1e-6
ROPE_THETA = 10000.0
MLA_SCALE = (MLA_NOPE + MLA_ROPE) ** -0.5
LOG2E = 1.4426950408889634
LN2 = 0.6931471805599453
SWA_SCALE = 64 ** -0.5
NEG = -1e30

ADAM_LR = 0.001
ADAM_B1 = 0.9
ADAM_B2 = 0.999
ADAM_EPS = 1e-08
ADAM_WD = 0.01
ADAM_STEP = 10

A_ZQ, A_ZKV, A_KR, A_GM, A_QS, A_KS, A_VS, A_GS, A_END = 0, 384, 640, 768, 1280, 1792, 1920, 2048, 2560
IN_SPLITS = (384, 256, 32, 512, 512, 128, 128, 512)
D_IN = sum(IN_SPLITS)

TOKEN_TILE = 512
ATT_TILE = 256
VMEM_LIMIT = 56 * 1024 * 1024


def _dot(a, b):
    return jnp.dot(a, b, preferred_element_type=F32)


def _dot_nt(a, b):
    return lax.dot_general(a, b, (((1,), (1,)), ((), ())), preferred_element_type=F32)


def _dot_tn(a, b):
    return lax.dot_general(a, b, (((0,), (0,)), ((), ())), preferred_element_type=F32)


def _params(n_grid):
    return pltpu.CompilerParams(dimension_semantics=("arbitrary",) * n_grid, vmem_limit_bytes=VMEM_LIMIT)


def _full(shape):
    nd = len(shape)
    return pl.BlockSpec(shape, lambda *_: (0,) * nd, pipeline_mode=pl.Buffered(1))


def _sigmoid(g):
    return 1.0 / (1.0 + jnp.exp(-g))


SUB_TILE = 256


def _sub_tiles(tm):
    sub = min(SUB_TILE, tm)
    return [slice(s * sub, (s + 1) * sub) for s in range(tm // sub)]


MESH = pl.DeviceIdType.MESH
ROWS_PER_DEVICE = 8
VMEM_SPEC = pl.BlockSpec(memory_space=pltpu.VMEM)
ANY_SPEC = pl.BlockSpec(memory_space=pl.ANY)


def _position():
    x, y, c = lax.axis_index("x"), lax.axis_index("y"), lax.axis_index("c")
    sibling = (x, y, 1 - c)
    others = [(1 - x, y, c), (x, 1 - y, c), (1 - x, 1 - y, c)]
    return (x, y, c), 4 * x + 2 * y + c, 2 * x + y, sibling, others


def _rows_of(dev):
    return pl.ds(pl.multiple_of(dev * ROWS_PER_DEVICE, ROWS_PER_DEVICE), ROWS_PER_DEVICE)


def _all_to_all_rows(block_ref, table_ref, dev, me, send_sems, recv_sems):
    x, y, c = me
    waits = []
    for k in range(1, 8):
        peer = (1 - x if k & 4 else x, 1 - y if k & 2 else y, 1 - c if k & 1 else c)
        pltpu.make_async_remote_copy(src_ref=block_ref, dst_ref=table_ref.at[_rows_of(dev)], send_sem=send_sems.at[k - 1],
                                     recv_sem=recv_sems.at[k - 1], device_id=peer, device_id_type=MESH).start()
        waits.append(pltpu.make_async_remote_copy(
            src_ref=block_ref, dst_ref=table_ref.at[_rows_of(jnp.bitwise_xor(dev, k))], send_sem=send_sems.at[k - 1],
            recv_sem=recv_sems.at[k - 1], device_id=peer, device_id_type=MESH))
    return waits


def _comm_fwd_call(c_blk, w_ada, shards):
    n = len(shards)

    def body(c_ref, wada_ref, *refs):
        w_refs, act_ref, pieces_ref, full_refs = refs[:n], refs[n], refs[n + 1], refs[n + 2:2 * n + 2]
        c_all_ref = refs[2 * n + 2]
        c_send, c_recv, p_send, p_recv, w_send, w_recv, f_send, f_recv, loc_sem = refs[2 * n + 3:]
        me, dev, chip, sibling, others = _position()
        core = me[2]
        chip_of = [2 * p[0] + p[1] for p in others]

        local = [pltpu.make_async_copy(w_refs[i], full_refs[i].at[chip], loc_sem.at[i]) for i in range(n)]
        for cp in local:
            cp.start()

        def over_ici(i, j, src_chip):
            return pltpu.make_async_remote_copy(
                src_ref=w_refs[i].at[core], dst_ref=full_refs[i].at[src_chip, core], send_sem=w_send.at[3 * i + j],
                recv_sem=w_recv.at[3 * i + j], device_id=others[j], device_id_type=MESH)

        def to_sibling(i, j, half):
            return pltpu.make_async_remote_copy(
                src_ref=full_refs[i].at[chip_of[j], half], dst_ref=full_refs[i].at[chip_of[j], half],
                send_sem=f_send.at[3 * i + j], recv_sem=f_recv.at[3 * i + j], device_id=sibling, device_id_type=MESH)

        c_all_ref[_rows_of(dev), :] = c_ref[...]
        c_waits = _all_to_all_rows(c_ref, c_all_ref, dev, me, c_send, c_recv)
        sent = [over_ici(i, j, chip) for i in range(n) for j in range(3)]
        for cp in sent:
            cp.start()

        for cp in c_waits:
            cp.wait()
        cv = c_all_ref[...]
        act = cv * _sigmoid(cv)
        act_ref[...] = act
        pieces_ref[chip] = _dot(act.astype(BF16), wada_ref[...].astype(BF16))
        piece = lambda j, src_chip: pltpu.make_async_remote_copy(
            src_ref=pieces_ref.at[chip], dst_ref=pieces_ref.at[src_chip], send_sem=p_send.at[j], recv_sem=p_recv.at[j],
            device_id=others[j], device_id_type=MESH)
        for j in range(3):
            piece(j, chip).start()

        for i in range(n):
            for j in range(3):
                over_ici(i, j, chip_of[j]).wait_recv()
                to_sibling(i, j, core).start()
        for j in range(3):
            piece(j, chip).wait_send()
            piece(j, chip_of[j]).wait_recv()
        for i in range(n):
            for j in range(3):
                to_sibling(i, j, 1 - core).wait_recv()
                to_sibling(i, j, core).wait_send()
        for cp in sent:
            cp.wait_send()
        for cp in local:
            cp.wait()

    rows = 8 * ROWS_PER_DEVICE
    dma = pltpu.SemaphoreType.DMA
    return pl.pallas_call(
        body, name="comm_fwd",
        out_shape=[jax.ShapeDtypeStruct((rows, D_MODEL), F32), jax.ShapeDtypeStruct((4, rows, w_ada.shape[1]), F32)]
        + [jax.ShapeDtypeStruct((4,) + s.shape, s.dtype) for s in shards],
        in_specs=[VMEM_SPEC, VMEM_SPEC] + [ANY_SPEC] * n,
        out_specs=[VMEM_SPEC, VMEM_SPEC] + [ANY_SPEC] * n,
        scratch_shapes=[pltpu.VMEM((rows, D_MODEL), F32), dma((7,)), dma((7,)), dma((3,)), dma((3,)),
                        dma((3 * n,)), dma((3 * n,)), dma((3 * n,)), dma((3 * n,)), dma((n,))],
        compiler_params=pltpu.CompilerParams(vmem_limit_bytes=VMEM_LIMIT),
    )(c_blk, w_ada, *shards)


def _comm_bwd_call(grads, part):
    n = len(grads)

    def body(part_ref, *refs):
        g_refs, f_refs, parts_ref = refs[:n], refs[n:2 * n], refs[2 * n]
        scratch = refs[2 * n + 1:]
        a_refs, b_refs, p_refs, r_refs = (scratch[k * n:(k + 1) * n] for k in range(4))
        s_send, s_recv, d_send, d_recv, e_send, e_recv, h_send, h_recv, loc_sem = scratch[4 * n:]
        me, dev, chip, sibling, others = _position()
        core = me[2]
        chip_of = [2 * p[0] + p[1] for p in others]

        parts_ref[_rows_of(dev), :] = part_ref[...]
        s_waits = _all_to_all_rows(part_ref, parts_ref, dev, me, s_send, s_recv)

        mine = [pltpu.make_async_copy(g_refs[i].at[:, core], a_refs[i], loc_sem.at[i]) for i in range(n)]
        swap = [pltpu.make_async_remote_copy(src_ref=g_refs[i].at[:, 1 - core], dst_ref=b_refs[i], send_sem=d_send.at[i],
                                             recv_sem=d_recv.at[i], device_id=sibling, device_id_type=MESH) for i in range(n)]
        order = sorted(range(n), key=lambda i: g_refs[i].shape[2] * g_refs[i].shape[3])
        for i in order:
            mine[i].start()
            swap[i].start()
        cross = [pltpu.make_async_remote_copy(src_ref=p_refs[i].at[chip_of[j]], dst_ref=r_refs[i].at[j],
                                              send_sem=e_send.at[3 * i + j], recv_sem=e_recv.at[3 * i + j],
                                              device_id=others[j], device_id_type=MESH) for i in range(n) for j in range(3)]
        for i in order:
            mine[i].wait()
            swap[i].wait()
            for k in range(4):
                s = a_refs[i][k] + b_refs[i][k]
                a_refs[i][k] = s
                p_refs[i][k] = s.astype(BF16)
            for j in range(3):
                cross[3 * i + j].start()
        share = {}
        for i in order:
            for j in range(3):
                cross[3 * i + j].wait()
            f_refs[i][core] = (a_refs[i][chip] + r_refs[i][0].astype(F32) + r_refs[i][1].astype(F32)
                               + r_refs[i][2].astype(F32))
            share[i] = pltpu.make_async_remote_copy(src_ref=f_refs[i].at[core], dst_ref=f_refs[i].at[core],
                                                    send_sem=h_send.at[i], recv_sem=h_recv.at[i], device_id=sibling,
                                                    device_id_type=MESH)
            share[i].start()
        for i in range(n):
            share[i].wait_send()
            pltpu.make_async_remote_copy(src_ref=f_refs[i].at[core], dst_ref=f_refs[i].at[1 - core], send_sem=h_send.at[i],
                                         recv_sem=h_recv.at[i], device_id=sibling, device_id_type=MESH).wait_recv()
        for cp in s_waits:
            cp.wait()

    rows = 8 * ROWS_PER_DEVICE
    dma = pltpu.SemaphoreType.DMA
    quarter = [(4,) + g.shape[2:] for g in grads]
    return pl.pallas_call(
        body, name="comm_bwd",
        out_shape=[jax.ShapeDtypeStruct((2,) + g.shape[2:], F32) for g in grads]
        + [jax.ShapeDtypeStruct((rows, part.shape[1]), F32)],
        in_specs=[VMEM_SPEC] + [ANY_SPEC] * n,
        out_specs=[VMEM_SPEC] * (n + 1),
        scratch_shapes=[pltpu.VMEM(q, F32) for q in quarter] + [pltpu.VMEM(q, F32) for q in quarter]
        + [pltpu.VMEM(q, BF16) for q in quarter] + [pltpu.VMEM((3,) + q[1:], BF16) for q in quarter]
        + [dma((7,)), dma((7,)), dma((n,)), dma((n,)), dma((3 * n,)), dma((3 * n,)), dma((n,)), dma((n,)), dma((n,))],
        compiler_params=pltpu.CompilerParams(vmem_limit_bytes=VMEM_LIMIT),
    )(part, *grads)


def _by_owner(g, n):
    return jnp.transpose(g.reshape(g.shape[0], 4, n), (1, 0, 2)).reshape(4, 2, g.shape[0] // 2, n)


def _reduce_operands(g):
    quarter = (4,) + g.shape[2:]
    dma = pltpu.SemaphoreType.DMA
    scratch = [pltpu.VMEM(quarter, F32), pltpu.VMEM(quarter, F32), pltpu.VMEM(quarter, BF16),
               pltpu.VMEM((3,) + quarter[1:], BF16), dma((5,)), dma((5,)), dma((2,))]
    return jax.ShapeDtypeStruct((2,) + g.shape[2:], F32), scratch


REDUCE_SCRATCH = 7


def _grad_reduce(step, n_steps, g_ref, f_ref, a_ref, b_ref, p_ref, r_ref, send, recv, loc_sem):
    me, _, chip, sibling, others = _position()
    core = me[2]
    chip_of = [2 * p[0] + p[1] for p in others]
    remote = lambda src, dst, k, to: pltpu.make_async_remote_copy(
        src_ref=src, dst_ref=dst, send_sem=send.at[k], recv_sem=recv.at[k], device_id=to, device_id_type=MESH)
    mine = pltpu.make_async_copy(g_ref.at[:, core], a_ref, loc_sem.at[0])
    swap = remote(g_ref.at[:, 1 - core], b_ref, 0, sibling)
    cross = [remote(p_ref.at[chip_of[j]], r_ref.at[j], 1 + j, others[j]) for j in range(3)]
    total_ref = b_ref.at[0]
    keep = pltpu.make_async_copy(total_ref, f_ref.at[core], loc_sem.at[1])
    share = lambda half: remote(total_ref, f_ref.at[half], 4, sibling)
    at = [k * (n_steps - 1) // 3 for k in range(4)]

    @pl.when(step == at[0])
    def _():
        mine.start()
        swap.start()

    @pl.when(step == at[1])
    def _():
        mine.wait()
        swap.wait()
        for k in range(4):
            s = a_ref[k] + b_ref[k]
            a_ref[k] = s
            p_ref[k] = s.astype(BF16)
        for cp in cross:
            cp.start()

    @pl.when(step == at[2])
    def _():
        for cp in cross:
            cp.wait()
        total_ref[...] = a_ref[chip] + r_ref[0].astype(F32) + r_ref[1].astype(F32) + r_ref[2].astype(F32)
        keep.start()
        share(core).start()

    @pl.when(step == at[3])
    def _():
        keep.wait()
        share(core).wait_send()
        share(1 - core).wait_recv()


def _twice(t):
    lo = _lane_lo()
    other = pltpu.roll(t, HALF, 1)
    return jnp.concatenate([jnp.where(lo, t, other), jnp.where(lo, other, t)], axis=1)


def _once(g):
    first, second = g[:, :HEAD_LANES], g[:, HEAD_LANES:]
    return jnp.where(_lane_lo(), first + pltpu.roll(first, HALF, 1), second + pltpu.roll(second, HALF, 1))


def _rope_tables(pos_col, inv_row):
    ang = pos_col * inv_row
    return jnp.cos(ang), jnp.sin(ang)


def _pre_call(x, pos_col, mod, b_ada, ng, qg, kvg, inv128, wa, wq2, wkv, w_out_half, seq):
    n_tok = x.shape[0]
    tm = min(TOKEN_TILE, seq)
    per_seq = seq // tm
    n_steps = n_tok // tm

    def gather_w_out(step, wo_ref, full_ref, w_send, w_recv, f_send, f_recv, loc_sem):
        me, _, chip, sibling, others = _position()
        core = me[2]
        chip_of = [2 * p[0] + p[1] for p in others]
        local = pltpu.make_async_copy(wo_ref, full_ref.at[chip], loc_sem.at[0])

        def over_ici(j, src_chip):
            return pltpu.make_async_remote_copy(
                src_ref=wo_ref.at[core], dst_ref=full_ref.at[src_chip, core], send_sem=w_send.at[j], recv_sem=w_recv.at[j],
                device_id=others[j], device_id_type=MESH)

        def to_sibling(j, half):
            return pltpu.make_async_remote_copy(
                src_ref=full_ref.at[chip_of[j], half], dst_ref=full_ref.at[chip_of[j], half], send_sem=f_send.at[j],
                recv_sem=f_recv.at[j], device_id=sibling, device_id_type=MESH)

        @pl.when(step == 0)
        def _():
            local.start()
            for j in range(3):
                over_ici(j, chip).start()

        @pl.when(step == n_steps // 2)
        def _():
            for j in range(3):
                over_ici(j, chip_of[j]).wait_recv()
                to_sibling(j, core).start()

        @pl.when(step == n_steps - 1)
        def _():
            for j in range(3):
                to_sibling(j, 1 - core).wait_recv()
                to_sibling(j, core).wait_send()
                over_ici(j, chip).wait_send()
            local.wait()

    def body(x_ref, pos_ref, mod_ref, bada_ref, ng_ref, qg_ref, kvg_ref, inv_ref, wa_ref, wq_ref, wkv_ref, wo_ref,
             zqkv_ref, gates_ref, qf_ref, kf_ref, v_ref, qs_ref, kd_ref, vd_ref, rope_ref, full_ref, *sems):
        gather_w_out(pl.program_id(0), wo_ref, full_ref, *sems)
        xv = x_ref[...]
        modv = mod_ref[0] + bada_ref[...]
        shift, scale = modv[:, :D_MODEL], modv[:, D_MODEL:2 * D_MODEL]
        r1 = lax.rsqrt(jnp.mean(xv * xv, axis=-1, keepdims=True) + EPS)
        h = ((xv * r1) * ng_ref[...]) * (1.0 + scale) + shift
        hb = h.astype(BF16)
        za = _dot(hb, wa_ref[...])
        zkr = za[:, A_KR:A_GM]
        cos, sin = _rope_tables(pos_ref[...], inv_ref[...])
        rope_ref[:, :HEAD_LANES] = cos
        rope_ref[:, HEAD_LANES:] = sin
        zqkv_ref[...] = za[:, :A_KR]
        gates_ref[:, :512] = za[:, A_GM:A_QS]
        gates_ref[:, 512:] = za[:, A_GS:A_END]
        qs_ref[...] = (za[:, A_QS:A_KS] * (SWA_SCALE * LOG2E)).astype(BF16)
        kd_ref[...] = _twice(za[:, A_KS:A_VS]).astype(BF16)
        vd_ref[...] = _twice(za[:, A_VS:A_GS]).astype(BF16)
        zq, zkv = za[:, A_ZQ:A_ZKV], za[:, A_ZKV:A_KR]
        rq = lax.rsqrt(jnp.mean(zq * zq, axis=-1, keepdims=True) + EPS)
        qn = ((zq * rq) * qg_ref[...]).astype(BF16)
        qr = _dot(qn, wq_ref[...])
        cf, sf = jnp.tile(cos, (1, N_HEADS)), jnp.tile(sin, (1, N_HEADS))
        qf_ref[...] = ((qr[:, :1024] * cf + qr[:, 1024:] * sf) * (MLA_SCALE * LOG2E)).astype(BF16)
        rkv = lax.rsqrt(jnp.mean(zkv * zkv, axis=-1, keepdims=True) + EPS)
        kvn = ((zkv * rkv) * kvg_ref[...]).astype(BF16)
        kv = _dot(kvn, wkv_ref[...])
        kpe = jnp.where(_lane_lo(), 0.0, zkr * cos) + pltpu.roll(zkr, HALF, 1) * sin
        kf_ref[...] = (kv[:, :1024] + jnp.tile(kpe, (1, N_HEADS))).astype(BF16)
        v_ref[...] = kv[:, 1024:].astype(BF16)

    tok = lambda w: pl.BlockSpec((tm, w), lambda i: (i, 0))
    outs = [(640, F32), (1024, F32), (1024, BF16), (1024, BF16), (512, BF16), (512, BF16), (256, BF16), (256, BF16),
            (2 * HEAD_LANES, F32)]
    dma = pltpu.SemaphoreType.DMA
    return pl.pallas_call(
        body, name="pre", grid=(n_steps,),
        out_shape=[jax.ShapeDtypeStruct((n_tok, w), dt) for w, dt in outs]
        + [jax.ShapeDtypeStruct((4,) + w_out_half.shape, w_out_half.dtype)],
        in_specs=[tok(D_MODEL), tok(1), pl.BlockSpec((1, 1, 3 * D_MODEL), lambda i: (i // per_seq, 0, 0)),
                  _full(b_ada.shape), _full(ng.shape), _full(qg.shape), _full(kvg.shape), _full(inv128.shape),
                  _full(wa.shape), _full(wq2.shape), _full(wkv.shape), ANY_SPEC],
        out_specs=[tok(w) for w, _ in outs] + [ANY_SPEC],
        scratch_shapes=[dma((3,)), dma((3,)), dma((3,)), dma((3,)), dma((1,))],
        compiler_params=_params(1),
    )(x, pos_col, mod, b_ada, ng, qg, kvg, inv128, wa, wq2, wkv, w_out_half)


def _lane_lo(width=HEAD_LANES):
    return lax.broadcasted_iota(jnp.int32, (1, width), 1) < HALF


def _eye(n=HEAD_LANES):
    r = lax.broadcasted_iota(jnp.int32, (n, n), 0)
    c = lax.broadcasted_iota(jnp.int32, (n, n), 1)
    return jnp.where(r == c, 1.0, 0.0).astype(BF16)


def _mla_fwd_call(qf, kf, v, n_seq, seq):
    tq = min(ATT_TILE, seq)
    nq = seq // tq

    ext = HALF + 16

    def body(q_ref, k_ref, v_ref, o_ref, lse_ref, vt_ref, acc_ref):
        i = pl.program_id(1)
        eye = _eye()

        @pl.when(i == 0)
        def _():
            for h in range(N_HEADS):
                vt_ref[h * ext + HALF:(h + 1) * ext, :] = jnp.ones((16, seq), BF16)
            for t in range(nq):
                for p in range(N_HEADS // 2):
                    pair = slice(p * HEAD_LANES, (p + 1) * HEAD_LANES)
                    v_t = _dot_nt(eye, v_ref[t * tq:(t + 1) * tq, pair]).astype(BF16)
                    for hh in range(2):
                        r0 = (2 * p + hh) * ext
                        vt_ref[r0:r0 + HALF, t * tq:(t + 1) * tq] = v_t[hh * HALF:(hh + 1) * HALF, :]

        q = q_ref[...]
        qcol = i * tq + lax.broadcasted_iota(jnp.int32, (1, tq), 1)
        heads = range(N_HEADS)
        lanes = [slice(h * HEAD_LANES, (h + 1) * HEAD_LANES) for h in heads]

        def make_step(masked, n_tiles):
            def step(kt0, carry):
                tiles = range(n_tiles)
                start = pl.multiple_of(kt0 * tq, tq)
                ks = [k_ref[pl.ds(pl.multiple_of((kt0 + t) * tq, tq), tq), :] for t in tiles]
                vt = vt_ref[:, pl.ds(start, n_tiles * tq)]
                last = n_tiles - 1
                if masked:
                    keep = ((kt0 + last) * tq + lax.broadcasted_iota(jnp.int32, (tq, 1), 0)) <= qcol

                def scores(h):
                    sts = [_dot_nt(ks[t][:, lanes[h]], q[:, lanes[h]]) for t in tiles]
                    if masked:
                        sts[last] = jnp.where(keep, sts[last], NEG)
                    return sts

                def softmax(h, sts):
                    m_old = carry[h]
                    m_new = m_old
                    for st in sts:
                        m_new = jnp.maximum(m_new, jnp.max(st, axis=0, keepdims=True))
                    pt = jnp.concatenate([jnp.exp2(st - m_new).astype(BF16) for st in sts], axis=0)
                    return m_new, jnp.exp2(m_old - m_new), pt

                def values(h, alpha, pt):
                    rows = slice(h * ext, (h + 1) * ext)
                    acc_ref[rows, :] = acc_ref[rows, :] * alpha + _dot(vt[rows, :], pt)

                sts, soft, out = {0: scores(0), 1: scores(1)}, {}, {}
                for h in range(N_HEADS + 1):
                    if h + 2 < N_HEADS:
                        sts[h + 2] = scores(h + 2)
                    if h < N_HEADS:
                        soft[h] = softmax(h, sts.pop(h))
                    if h >= 1:
                        m_new, alpha, pt = soft.pop(h - 1)
                        values(h - 1, alpha, pt)
                        out[h - 1] = m_new
                return tuple(out[h] for h in heads)
            return step

        acc_ref[...] = jnp.zeros_like(acc_ref)
        init = (jnp.full((1, tq), NEG, F32),) * N_HEADS
        count = i + 1
        carry = lax.fori_loop(0, (count + 1) // 2 - 1, lambda j, c: make_step(False, 2)(2 * j, c), init)
        carry = lax.cond(count % 2 == 0, lambda c: make_step(True, 2)(i - 1, c), lambda c: make_step(True, 1)(i, c), carry)
        dens = [acc_ref[h * ext + HALF:h * ext + HALF + 1, :] for h in heads]
        acc_t = jnp.concatenate([acc_ref[h * ext:h * ext + HALF, :] * (1.0 / dens[h]) for h in heads], axis=0)
        o_ref[...] = acc_t.T
        for h in heads:
            lse_ref[0, h // 4, h % 4:h % 4 + 1, :] = carry[h] + jnp.log2(dens[h])

    n_tok = qf.shape[0]
    return pl.pallas_call(
        body, name="mla_fwd", grid=(n_seq, nq),
        out_shape=[jax.ShapeDtypeStruct((n_tok, 512), F32), jax.ShapeDtypeStruct((n_seq, 2, 4, seq), F32)],
        in_specs=[pl.BlockSpec((tq, 1024), lambda b, i: (b * nq + i, 0)),
                  pl.BlockSpec((seq, 1024), lambda b, i: (b, 0)),
                  pl.BlockSpec((seq, 512), lambda b, i: (b, 0))],
        out_specs=[pl.BlockSpec((tq, 512), lambda b, i: (b * nq + i, 0)),
                   pl.BlockSpec((1, 2, 4, tq), lambda b, i: (b, 0, 0, i))],
        scratch_shapes=[pltpu.VMEM((N_HEADS * ext, seq), BF16), pltpu.VMEM((N_HEADS * ext, tq), F32)],
        compiler_params=_params(2),
    )(qf, kf, v)


def _mla_bwd_call(qf, kf, v, do, delta, lse, n_seq, seq):
    tq = min(ATT_TILE, seq)
    nq = seq // tq

    nh = 4
    heads = range(nh)
    lanes = [slice(h * HEAD_LANES, (h + 1) * HEAD_LANES) for h in heads]

    def body(q_ref, k_ref, v_ref, do_ref, dl_ref, lse_ref, dq_ref, dk_ref, dv_ref,
             kt_ref, dot_ref, dqt_ref, dvt_ref):
        eye = _eye()
        sub_lo = lax.broadcasted_iota(jnp.int32, (HEAD_LANES, 1), 0) < HALF

        for t in range(nq):
            r = slice(t * tq, (t + 1) * tq)
            kv = k_ref[r, :]
            for h in heads:
                kt_ref[lanes[h], r] = _dot_nt(eye, kv[:, lanes[h]]).astype(BF16)
            for p in range(nh // 2):
                dov = do_ref[r, lanes[p]]
                dt = _dot_nt(eye, dov)
                dot_ref[2 * p, :, r] = jnp.where(sub_lo, dt, 0.0).astype(BF16)
                dot_ref[2 * p + 1, :, r] = jnp.where(sub_lo, 0.0, dt).astype(BF16)
        dqt_ref[...] = jnp.zeros_like(dqt_ref)
        dvt_ref[...] = jnp.zeros_like(dvt_ref)

        def flush_dv(tile, which):
            rows = pl.ds(pl.multiple_of(tile * tq, tq), tq)
            for p in range(nh // 2):
                dv_ref[rows, lanes[p]] = dvt_ref[which, p * HEAD_LANES:(p + 1) * HEAD_LANES, :].T

        def k_step(kt, _):
            slot = kt % 2
            kr = pl.ds(pl.multiple_of(kt * tq, tq), tq)
            k = k_ref[kr, :]
            vv = v_ref[kr, :]
            k_t = kt_ref[:, kr]
            krow = kt * tq + lax.broadcasted_iota(jnp.int32, (tq, 1), 0)

            def make_step(masked, n_tiles):
                def q_step(qt0, carry):
                    tiles = range(n_tiles)
                    qrs = [pl.ds(pl.multiple_of((qt0 + t) * tq, tq), tq) for t in tiles]
                    if masked:
                        flush_dv(jnp.maximum(kt - 1, 0), 1 - slot)
                    qs = [q_ref[qr, :] for qr in qrs]
                    if masked:
                        keep = krow <= (qt0 * tq + lax.broadcasted_iota(jnp.int32, (1, tq), 1))

                    def scores(h):
                        do_ts = [dot_ref[h, :, qr] for qr in qrs]
                        sts = [_dot_nt(k[:, lanes[h]], qs[t][:, lanes[h]]) for t in tiles]
                        dpts = [_dot(vv[:, lanes[h // 2]], do_ts[t]) for t in tiles]
                        return do_ts, sts, dpts

                    def softmax(h, sts, dpts):
                        pts, dsts = [], []
                        for t in tiles:
                            pt = jnp.exp2(sts[t] - lse_ref[0, 0, h:h + 1, qrs[t]])
                            if masked and t == 0:
                                pt = jnp.where(keep, pt, 0.0)
                            dsts.append((pt * (dpts[t] - dl_ref[0, h:h + 1, qrs[t]])).astype(BF16))
                            pts.append(pt.astype(BF16))
                        return pts, dsts

                    def grads(h, do_ts, pts, dsts):
                        half = slice((h % 2) * HALF, (h % 2 + 1) * HALF)
                        dst_all = jnp.concatenate(dsts, axis=1)
                        pt_all = jnp.concatenate(pts, axis=1)
                        do_all = jnp.concatenate([do_ts[t][half, :] for t in tiles], axis=1)
                        q_all = jnp.concatenate([qs[t][:, lanes[h]] for t in tiles], axis=0)
                        dvt_ref[slot, h * HALF:(h + 1) * HALF, :] += _dot_nt(do_all, pt_all)
                        dk_ref[kr, lanes[h]] += _dot(dst_all, q_all)
                        for t in tiles:
                            dqt_ref[lanes[h], qrs[t]] += _dot(k_t[lanes[h], :], dsts[t])

                    first, second = {0: scores(0)}, {}
                    for h in range(nh + 1):
                        if h + 1 < nh:
                            first[h + 1] = scores(h + 1)
                        if h < nh:
                            do_ts, sts, dpts = first.pop(h)
                            second[h] = (do_ts,) + softmax(h, sts, dpts)
                        if h >= 1:
                            grads(h - 1, *second.pop(h - 1))
                    return carry
                return q_step

            dk_ref[kr, :] = jnp.zeros((tq, nh * HEAD_LANES), F32)
            dvt_ref[slot] = jnp.zeros(dvt_ref.shape[1:], F32)
            count = nq - kt
            lax.cond(count >= 2, lambda c: make_step(True, 2)(kt, c), lambda c: make_step(True, 1)(kt, c), 0)
            lax.fori_loop(1, count // 2, lambda j, c: make_step(False, 2)(kt + 2 * j, c), 0)
            lax.cond(jnp.logical_and(count % 2 == 1, count >= 3), lambda c: make_step(False, 1)(nq - 1, c), lambda c: c, 0)
            return 0

        lax.fori_loop(0, nq, k_step, 0)
        flush_dv(nq - 1, (nq - 1) % 2)
        for t in range(nq):
            r = slice(t * tq, (t + 1) * tq)
            for h in heads:
                dq_ref[r, lanes[h]] = dqt_ref[lanes[h], r].T

    n_tok = qf.shape[0]
    groups = N_HEADS // nh
    blk = lambda w: pl.BlockSpec((seq, w), lambda b, g: (b, g))
    return pl.pallas_call(
        body, name="mla_bwd", grid=(n_seq, groups),
        out_shape=[jax.ShapeDtypeStruct((n_tok, 1024), F32), jax.ShapeDtypeStruct((n_tok, 1024), F32),
                   jax.ShapeDtypeStruct((n_tok, 512), F32)],
        in_specs=[blk(512), blk(512), blk(256), blk(256), pl.BlockSpec((1, nh, seq), lambda b, g: (g, 0, b)),
                  pl.BlockSpec((1, 1, nh, seq), lambda b, g: (b, g, 0, 0))],
        out_specs=[blk(512), blk(512), blk(256)],
        scratch_shapes=[pltpu.VMEM((nh * HEAD_LANES, seq), BF16), pltpu.VMEM((nh, HEAD_LANES, seq), BF16),
                        pltpu.VMEM((nh * HEAD_LANES, seq), F32), pltpu.VMEM((2, nh * HALF, tq), F32)],
        compiler_params=_params(2),
    )(qf, kf, v, do, delta, lse)


SWA_BLOCKS = 4


def _swa_block(n, pos_col_ref, posq):
    w = SWA_WINDOW
    start = pl.multiple_of(jnp.maximum(n - 1, 0) * w, w)
    posk = pos_col_ref[pl.ds(start, 2 * w), :]
    rel = (n * w + lax.broadcasted_iota(jnp.int32, (1, w), 1)) - (start + lax.broadcasted_iota(jnp.int32, (2 * w, 1), 0))
    valid = jnp.logical_and(rel >= 0, rel < w)
    return start, jnp.where(valid, posq - posk, 1e30)


def _alibi(h):
    return LOG2E * 2.0 ** -(h + 1)


def _transpose_rows(eye, src_ref, dst_ref, seq, width):
    step = 2 * SWA_WINDOW
    for t in range(seq // step):
        for p in range(width // HEAD_LANES):
            lanes = slice(p * HEAD_LANES, (p + 1) * HEAD_LANES)
            dst_ref[lanes, t * step:(t + 1) * step] = _dot_nt(eye, src_ref[t * step:(t + 1) * step, lanes]).astype(BF16)


def _swa_fwd_call(qs, kd, vd, pos_col, pos_row, sinks, n_seq, seq):
    w = SWA_WINDOW
    qb = SWA_BLOCKS
    steps = seq // (qb * w)
    ext = HALF + 16

    def body(q_ref, k_ref, v_ref, pc_ref, pr_ref, sink_ref, o_ref, lse_ref, vt_ref):
        n = pl.program_id(1)
        lo = _lane_lo()
        hi = jnp.logical_not(lo)
        eye = _eye()

        @pl.when(n == 0)
        def _():
            step = 2 * w
            for kv in range(2):
                vt_ref[kv * ext + HALF:(kv + 1) * ext, :] = jnp.ones((16, seq), BF16)
                for t in range(seq // step):
                    v_t = _dot_nt(eye, v_ref[t * step:(t + 1) * step, kv * HEAD_LANES:(kv + 1) * HEAD_LANES])
                    vt_ref[kv * ext:kv * ext + HALF, t * step:(t + 1) * step] = v_t[:HALF, :].astype(BF16)

        heads = range(N_HEADS)
        blocks = range(qb)
        geo = [_swa_block(n * qb + bi, pc_ref, pr_ref[bi]) for bi in blocks]
        wins = [pl.ds(g[0], 2 * w) for g in geo]
        kwins = [k_ref[win, :] for win in wins]
        vts = [vt_ref[:, win] for win in wins]
        sts = []
        for bi in blocks:
            q = q_ref[bi * w:(bi + 1) * w, :]
            sts.append([])
            for j in range(N_HEADS // 2):
                qp = q[:, j * HEAD_LANES:(j + 1) * HEAD_LANES]
                both = jnp.concatenate([jnp.where(lo, qp, jnp.zeros_like(qp)), jnp.where(hi, qp, jnp.zeros_like(qp))], axis=0)
                st = _dot_nt(kwins[bi][:, (j // 2) * HEAD_LANES:(j // 2 + 1) * HEAD_LANES], both)
                sts[bi] += [st[:, :w], st[:, w:]]
        ps, ms = [], []
        for bi in blocks:
            ps.append([])
            ms.append([])
            for h in heads:
                s = sts[bi][h] - _alibi(h) * geo[bi][1]
                m = jnp.maximum(jnp.max(s, axis=0, keepdims=True), sink_ref[0, h] * LOG2E)
                ps[bi].append(jnp.exp2(s - m).astype(BF16))
                ms[bi].append(m)
        for bi in blocks:
            ots = []
            for h in heads:
                pv = _dot(vts[bi][(h // 4) * ext:(h // 4 + 1) * ext, :], ps[bi][h])
                l = pv[HALF:HALF + 1, :] + jnp.exp2(sink_ref[0, h] * LOG2E - ms[bi][h])
                ots.append(pv[:HALF, :] * (1.0 / l))
                lse_ref[0, h:h + 1, bi * w:(bi + 1) * w] = ms[bi][h] + jnp.log2(l)
            o_ref[bi * w:(bi + 1) * w, :] = jnp.concatenate(ots, axis=0).T

    n_tok = qs.shape[0]
    tok = lambda width: pl.BlockSpec((qb * w, width), lambda b, n: (b * steps + n, 0))
    whole = lambda width: pl.BlockSpec((seq, width), lambda b, n: (b, 0))
    return pl.pallas_call(
        body, name="swa_fwd", grid=(n_seq, steps),
        out_shape=[jax.ShapeDtypeStruct((n_tok, 512), F32), jax.ShapeDtypeStruct((n_seq, N_HEADS, seq), F32)],
        in_specs=[tok(512), whole(256), whole(256), whole(1), pl.BlockSpec((qb, 1, w), lambda b, n: (b * steps + n, 0, 0)),
                  pl.BlockSpec(memory_space=pltpu.SMEM)],
        out_specs=[tok(512), pl.BlockSpec((1, N_HEADS, qb * w), lambda b, n: (b, 0, n))],
        scratch_shapes=[pltpu.VMEM((2 * ext, seq), BF16)],
        compiler_params=_params(2),
    )(qs, kd, vd, pos_col, pos_row, sinks)


def _swa_bwd_call(qs, kd, vd, do, delta, lse, pos_col, pos_row, sinks, g_out, n_seq, seq):
    w = SWA_WINDOW
    qb = SWA_BLOCKS
    steps = seq // (qb * w)
    reduced, reduce_scratch = _reduce_operands(g_out)

    def body(q_ref, k_ref, v_ref, do_ref, dl_ref, lse_ref, pc_ref, pr_ref, sink_ref, g_ref, dq_ref, dk_ref, dv_ref,
             dsink_ref, f_ref, kt_ref, *reduce_refs):
        b, n = pl.program_id(0), pl.program_id(1)
        _grad_reduce(b * steps + n, n_seq * steps, g_ref, f_ref, *reduce_refs)
        lo = _lane_lo()
        hi = jnp.logical_not(lo)
        sub_lo = lax.broadcasted_iota(jnp.int32, (HEAD_LANES, 1), 0) < HALF
        eye = _eye()

        @pl.when(n == 0)
        def _():
            dk_ref[...] = jnp.zeros_like(dk_ref)
            dv_ref[...] = jnp.zeros_like(dv_ref)
            _transpose_rows(eye, k_ref, kt_ref, seq, 2 * HEAD_LANES)

        @pl.when(jnp.logical_and(n == 0, b == 0))
        def _():
            dsink_ref[...] = jnp.zeros_like(dsink_ref)

        heads = range(N_HEADS)
        blocks = range(qb)
        kv_lanes = lambda h: slice((h // 4) * HEAD_LANES, (h // 4 + 1) * HEAD_LANES)
        geo = [_swa_block(n * qb + bi, pc_ref, pr_ref[bi]) for bi in blocks]
        wins = [pl.ds(g[0], 2 * w) for g in geo]
        kwins = [k_ref[win, :] for win in wins]
        vwins = [v_ref[win, :] for win in wins]

        do_ts, deltas, qms, doms = [], [], [], []
        for bi in blocks:
            rows = slice(bi * w, (bi + 1) * w)
            for lst in (do_ts, deltas, qms, doms):
                lst.append([])
            for j in range(N_HEADS // 2):
                pair = slice(j * HEAD_LANES, (j + 1) * HEAD_LANES)
                dop = do_ref[rows, pair]
                qp = q_ref[rows, pair]
                dt = _dot_nt(eye, dop)
                for hh in range(2):
                    half = lo if hh == 0 else hi
                    do_ts[bi].append(jnp.where(sub_lo, dt, 0.0).astype(BF16) if hh == 0
                                     else jnp.where(sub_lo, 0.0, dt).astype(BF16))
                    deltas[bi].append(dl_ref[2 * j + hh:2 * j + hh + 1, rows])
                    qms[bi].append(jnp.where(half, qp, jnp.zeros_like(qp)))
                    doms[bi].append(jnp.where(half, dop, jnp.zeros_like(dop)))
        sts, dpts = [], []
        for bi in blocks:
            sts.append([])
            dpts.append([])
            for j in range(N_HEADS // 2):
                a, b = 2 * j, 2 * j + 1
                st = _dot_nt(kwins[bi][:, kv_lanes(a)], jnp.concatenate([qms[bi][a], qms[bi][b]], axis=0))
                dpt = _dot(vwins[bi][:, kv_lanes(a)], jnp.concatenate([do_ts[bi][a], do_ts[bi][b]], axis=1))
                sts[bi] += [st[:, :w], st[:, w:]]
                dpts[bi] += [dpt[:, :w], dpt[:, w:]]
        pts, dsts = [], []
        for bi in blocks:
            pts.append([])
            dsts.append([])
            for h in heads:
                lse_h = lse_ref[0, h:h + 1, bi * w:(bi + 1) * w]
                pt = jnp.exp2(sts[bi][h] - _alibi(h) * geo[bi][1] - lse_h)
                dsts[bi].append((pt * (dpts[bi][h] - deltas[bi][h])).astype(BF16))
                pts[bi].append(pt.astype(BF16))
                dsink_ref[h:h + 1, :] += -jnp.exp2(sink_ref[0, h] * LOG2E - lse_h) * deltas[bi][h]
        for bi in blocks:
            for kv in range(2):
                group = range(4 * kv, 4 * kv + 4)
                dst_all = jnp.concatenate([dsts[bi][h] for h in group], axis=1)
                pt_all = jnp.concatenate([pts[bi][h] for h in group], axis=1)
                q_all = jnp.concatenate([qms[bi][h] for h in group], axis=0)
                do_all = jnp.concatenate([doms[bi][h] for h in group], axis=0)
                dk_ref[wins[bi], kv_lanes(4 * kv)] += _dot(dst_all, q_all)
                dv_ref[wins[bi], kv_lanes(4 * kv)] += _dot(pt_all, do_all)
        for bi in blocks:
            ktw = kt_ref[:, wins[bi]]
            for j in range(N_HEADS // 2):
                k_t = ktw[kv_lanes(2 * j), :]
                both = _dot(k_t, jnp.concatenate([dsts[bi][2 * j], dsts[bi][2 * j + 1]], axis=1))
                dq_t = jnp.where(sub_lo, both[:, :w], both[:, w:])
                dq_ref[bi * w:(bi + 1) * w, j * HEAD_LANES:(j + 1) * HEAD_LANES] = dq_t.T * SWA_SCALE

    n_tok = qs.shape[0]
    tok = lambda width: pl.BlockSpec((qb * w, width), lambda b, n: (b * steps + n, 0))
    whole = lambda width: pl.BlockSpec((seq, width), lambda b, n: (b, 0))
    return pl.pallas_call(
        body, name="swa_bwd", grid=(n_seq, steps),
        out_shape=[jax.ShapeDtypeStruct((n_tok, 512), F32), jax.ShapeDtypeStruct((n_tok, 256), F32),
                   jax.ShapeDtypeStruct((n_tok, 256), F32), jax.ShapeDtypeStruct((N_HEADS, HEAD_LANES), F32), reduced],
        in_specs=[tok(512), whole(256), whole(256), pl.BlockSpec((qb * w, 512), lambda b, n: (b * steps + n, 1)),
                  pl.BlockSpec((N_HEADS, qb * w), lambda b, n: (0, b * steps + n)),
                  pl.BlockSpec((1, N_HEADS, qb * w), lambda b, n: (b, 0, n)),
                  whole(1), pl.BlockSpec((qb, 1, w), lambda b, n: (b * steps + n, 0, 0)),
                  pl.BlockSpec(memory_space=pltpu.SMEM), ANY_SPEC],
        out_specs=[tok(512), whole(256), whole(256), _full((N_HEADS, HEAD_LANES)), ANY_SPEC],
        scratch_shapes=[pltpu.VMEM((2 * HEAD_LANES, seq), BF16)] + reduce_scratch,
        compiler_params=_params(2),
    )(qs, kd, vd, do, delta, lse, pos_col, pos_row, sinks, g_out)


def _post_call(x, target, o_mla, o_swa, gates, mod, b_ada, fg, w_out, seq):
    n_tok = x.shape[0]
    tm = min(TOKEN_TILE, seq)
    per_seq = seq // tm
    n_seq = n_tok // seq

    def body(x_ref, t_ref, om_ref, os_ref, g_ref, mod_ref, bada_ref, fg_ref, w_ref,
             dx2_ref, do_ref, dg_ref, gw_ref, gfg_ref, dgate_ref, loss_ref, dmla_ref, dswa_ref):
        i = pl.program_id(0)

        @pl.when(i == 0)
        def _():
            gw_ref[...] = jnp.zeros_like(gw_ref)
            gfg_ref[...] = jnp.zeros_like(gfg_ref)
            loss_ref[...] = jnp.zeros_like(loss_ref)

        @pl.when(i % per_seq == 0)
        def _():
            dgate_ref[...] = jnp.zeros_like(dgate_ref)

        gate = mod_ref[0][:, 2 * D_MODEL:] + bada_ref[:, 2 * D_MODEL:]
        fgv = fg_ref[...]
        fgd = fgv * (1.0 / D_MODEL)
        subs = _sub_tiles(tm)
        gs = [g_ref[r, :] for r in subs]
        os_ = [jnp.concatenate([om_ref[r, :], os_ref[r, :]], axis=-1) for r in subs]
        sgs = [_sigmoid(g) for g in gs]
        sils = [g * sg for g, sg in zip(gs, sgs)]
        ypres = [(o * sil).astype(BF16) for o, sil in zip(os_, sils)]
        ys = [_dot(ypre, w_ref[...]) for ypre in ypres]
        dys, loss, gfg, dgate = [], 0.0, 0.0, 0.0
        for r, y in zip(subs, ys):
            x2 = x_ref[r, :] + gate * y
            r2 = lax.rsqrt(jnp.mean(x2 * x2, axis=-1, keepdims=True) + EPS)
            xn2 = x2 * r2
            err = xn2 * fgv - t_ref[r, :]
            loss = loss + jnp.sum(jnp.sum(err * err, axis=-1, keepdims=True), axis=0, keepdims=True)
            gfg = gfg + jnp.sum(err * xn2, axis=0, keepdims=True)
            dxn2 = err * fgd
            dx2 = r2 * (dxn2 - xn2 * jnp.mean(dxn2 * xn2, axis=-1, keepdims=True))
            dx2_ref[r, :] = dx2
            dgate = dgate + jnp.sum(dx2 * y, axis=0, keepdims=True)
            dys.append((dx2 * gate).astype(BF16))
        loss_ref[...] += jnp.broadcast_to(loss * (0.5 / D_MODEL), loss_ref.shape)
        gfg_ref[...] += gfg * (1.0 / D_MODEL)
        dgate_ref[0] += dgate
        gw_ref[...] += _dot_tn(jnp.concatenate(ypres, axis=0), jnp.concatenate(dys, axis=0))
        dypres = [_dot_nt(dy, w_ref[...]) for dy in dys]
        pick = jnp.where(jnp.right_shift(lax.broadcasted_iota(jnp.int32, (2 * N_HEADS, D_MODEL), 1), 6)
                         == lax.broadcasted_iota(jnp.int32, (2 * N_HEADS, D_MODEL), 0), 1.0, 0.0).astype(BF16)
        for r, dypre, o, g, sg, sil in zip(subs, dypres, os_, gs, sgs, sils):
            dov = (dypre * sil).astype(BF16)
            do_ref[r, :] = dov
            delta = _dot_nt(pick, (dov.astype(F32) * o).astype(BF16))
            for grp in range(2):
                dmla_ref[grp, :, r] = delta[4 * grp:4 * grp + 4, :]
            dswa_ref[:, r] = delta[N_HEADS:, :]
            dg_ref[r, :] = (dypre * o * (sg + sil * (1.0 - sg))).astype(BF16)

    tok = lambda w: pl.BlockSpec((tm, w), lambda i: (i, 0))
    per_b = pl.BlockSpec((1, 1, 3 * D_MODEL), lambda i: (i // per_seq, 0, 0))
    return pl.pallas_call(
        body, name="post", grid=(n_tok // tm,),
        out_shape=[jax.ShapeDtypeStruct((n_tok, D_MODEL), F32), jax.ShapeDtypeStruct((n_tok, D_MODEL), BF16),
                   jax.ShapeDtypeStruct((n_tok, D_MODEL), BF16), jax.ShapeDtypeStruct((D_MODEL, D_MODEL), F32),
                   jax.ShapeDtypeStruct((1, D_MODEL), F32), jax.ShapeDtypeStruct((n_seq, 1, D_MODEL), F32),
                   jax.ShapeDtypeStruct((1, HEAD_LANES), F32),
                   jax.ShapeDtypeStruct((2, N_HEADS // 2, n_tok), F32), jax.ShapeDtypeStruct((N_HEADS, n_tok), F32)],
        in_specs=[tok(D_MODEL), tok(D_MODEL), tok(512), tok(512), tok(D_MODEL), per_b, _full(b_ada.shape),
                  _full(fg.shape), _full(w_out.shape)],
        out_specs=[tok(D_MODEL), tok(D_MODEL), tok(D_MODEL), _full((D_MODEL, D_MODEL)), _full((1, D_MODEL)),
                   pl.BlockSpec((1, 1, D_MODEL), lambda i: (i // per_seq, 0, 0)), _full((1, HEAD_LANES)),
                   pl.BlockSpec((2, N_HEADS // 2, tm), lambda i: (0, 0, i)), pl.BlockSpec((N_HEADS, tm), lambda i: (0, i))],
        compiler_params=_params(1),
    )(x, target, o_mla, o_swa, gates, mod, b_ada, fg, w_out)


def _mid_bwd_call(dqf, dkf, dv, zqkv, rope, qg, kvg, wq2, wkv, seq):
    n_tok = dqf.shape[0]
    tm = min(TOKEN_TILE, seq)

    def body(dq_ref, dk_ref, dv_ref, z_ref, rope_ref, qg_ref, kvg_ref, wq_ref, wkv_ref,
             dz_ref, gwq_ref, gwkv_ref, gqg_ref, gkvg_ref):
        i = pl.program_id(0)

        @pl.when(i == 0)
        def _():
            gwq_ref[...] = jnp.zeros_like(gwq_ref)
            gwkv_ref[...] = jnp.zeros_like(gwkv_ref)
            gqg_ref[...] = jnp.zeros_like(gqg_ref)
            gkvg_ref[...] = jnp.zeros_like(gkvg_ref)

        cos, sin = rope_ref[:, :HEAD_LANES], rope_ref[:, HEAD_LANES:]
        cf, sf = jnp.tile(cos, (1, N_HEADS)), jnp.tile(sin, (1, N_HEADS))
        dq = dq_ref[...] * MLA_SCALE
        dqr = jnp.concatenate([dq * cf, dq * sf], axis=-1).astype(BF16)
        zq, zkv = z_ref[:, :Q_LORA], z_ref[:, Q_LORA:]
        qgv, kvgv = qg_ref[...], kvg_ref[...]

        rq = lax.rsqrt(jnp.mean(zq * zq, axis=-1, keepdims=True) + EPS)
        xq = zq * rq
        gwq_ref[...] += _dot_tn((xq * qgv).astype(BF16), dqr)
        dqn = _dot_nt(dqr, wq_ref[...])
        gqg_ref[...] += jnp.sum(dqn * xq, axis=0, keepdims=True)
        dxq = dqn * qgv
        dz_ref[:, :Q_LORA] = (rq * (dxq - xq * jnp.mean(dxq * xq, axis=-1, keepdims=True))).astype(BF16)

        dk = dk_ref[...] * LN2
        dkv = jnp.concatenate([dk, dv_ref[...]], axis=-1).astype(BF16)
        rkv = lax.rsqrt(jnp.mean(zkv * zkv, axis=-1, keepdims=True) + EPS)
        xkv = zkv * rkv
        gwkv_ref[...] += _dot_tn((xkv * kvgv).astype(BF16), dkv)
        dkvn = _dot_nt(dkv, wkv_ref[...])
        gkvg_ref[...] += jnp.sum(dkvn * xkv, axis=0, keepdims=True)
        dxkv = dkvn * kvgv
        dz_ref[:, Q_LORA:A_KR] = (rkv * (dxkv - xkv * jnp.mean(dxkv * xkv, axis=-1, keepdims=True))).astype(BF16)

        dkpe = dk[:, :HEAD_LANES]
        for h in range(1, N_HEADS):
            dkpe = dkpe + dk[:, h * HEAD_LANES:(h + 1) * HEAD_LANES]
        dz_ref[:, A_KR:] = (jnp.where(_lane_lo(), 0.0, dkpe * cos) + pltpu.roll(dkpe * sin, HALF, 1)).astype(BF16)

    tok = lambda w: pl.BlockSpec((tm, w), lambda i: (i, 0))
    return pl.pallas_call(
        body, name="mid_bwd", grid=(n_tok // tm,),
        out_shape=[jax.ShapeDtypeStruct((n_tok, A_GM), BF16),
                   jax.ShapeDtypeStruct(wq2.shape, F32), jax.ShapeDtypeStruct(wkv.shape, F32),
                   jax.ShapeDtypeStruct((1, Q_LORA), F32), jax.ShapeDtypeStruct((1, KV_LORA), F32)],
        in_specs=[tok(1024), tok(1024), tok(512), tok(640), tok(2 * HEAD_LANES), _full(qg.shape), _full(kvg.shape),
                  _full(wq2.shape), _full(wkv.shape)],
        out_specs=[tok(A_GM), _full(wq2.shape), _full(wkv.shape), _full((1, Q_LORA)), _full((1, KV_LORA))],
        compiler_params=_params(1),
    )(dqf, dkf, dv, zqkv, rope, qg, kvg, wq2, wkv)


def _in_bwd_call(x, dx2, dz, dg, dqs, dkd, dvd, mod, b_ada, ng, wa, g_uq, g_ukv, seq):
    n_tok = x.shape[0]
    tm = min(TOKEN_TILE, seq)
    per_seq = seq // tm
    n_seq = n_tok // seq
    n_steps = n_tok // tm
    (red_uq, scratch_uq), (red_ukv, scratch_ukv) = _reduce_operands(g_uq), _reduce_operands(g_ukv)

    def body(x_ref, dx2_ref, dz_ref, dg_ref, dqs_ref, dkd_ref, dvd_ref, mod_ref, bada_ref, ng_ref,
             wa_ref, guq_ref, gukv_ref, gx_ref, gwa_ref, gng_ref, dshift_ref, dscale_ref, fuq_ref, fukv_ref, *reduce_refs):
        i = pl.program_id(0)
        _grad_reduce(i, n_steps, guq_ref, fuq_ref, *reduce_refs[:REDUCE_SCRATCH])
        _grad_reduce(i, n_steps, gukv_ref, fukv_ref, *reduce_refs[REDUCE_SCRATCH:])

        @pl.when(i == 0)
        def _():
            gwa_ref[...] = jnp.zeros_like(gwa_ref)
            gng_ref[...] = jnp.zeros_like(gng_ref)

        @pl.when(i % per_seq == 0)
        def _():
            dshift_ref[...] = jnp.zeros_like(dshift_ref)
            dscale_ref[...] = jnp.zeros_like(dscale_ref)

        xv = x_ref[...]
        modv = mod_ref[0] + bada_ref[...]
        shift, scale = modv[:, :D_MODEL], modv[:, D_MODEL:2 * D_MODEL]
        ngv = ng_ref[...]
        r1 = lax.rsqrt(jnp.mean(xv * xv, axis=-1, keepdims=True) + EPS)
        xn = xv * r1
        hb = ((xn * ngv) * (1.0 + scale) + shift).astype(BF16)

        dgv = dg_ref[...]
        pieces = [(A_ZQ, dz_ref[...]), (A_GM, dgv[:, :512]), (A_QS, dqs_ref[...].astype(BF16)),
                  (A_KS, jnp.concatenate([_once(dkd_ref[...]) * LN2, _once(dvd_ref[...])], axis=1).astype(BF16)),
                  (A_GS, dgv[:, 512:])]
        dh = None
        for off, piece in pieces:
            wd = piece.shape[1]
            gwa_ref[:, off:off + wd] += _dot_tn(hb, piece)
            term = _dot_nt(piece, wa_ref[:, off:off + wd])
            dh = term if dh is None else dh + term

        dshift_ref[0] += jnp.sum(dh, axis=0, keepdims=True)
        dscale_ref[0] += jnp.sum(dh * (xn * ngv), axis=0, keepdims=True)
        gng_ref[...] += jnp.sum(dh * xn * (1.0 + scale), axis=0, keepdims=True)
        dxn = dh * ngv * (1.0 + scale)
        gx_ref[...] = dx2_ref[...] + r1 * (dxn - xn * jnp.mean(dxn * xn, axis=-1, keepdims=True))

    tok = lambda w: pl.BlockSpec((tm, w), lambda i: (i, 0))
    per_b = lambda w: pl.BlockSpec((1, 1, w), lambda i: (i // per_seq, 0, 0))
    return pl.pallas_call(
        body, name="in_bwd", grid=(n_steps,),
        out_shape=[jax.ShapeDtypeStruct((n_tok, D_MODEL), F32), jax.ShapeDtypeStruct((D_MODEL, A_END), F32),
                   jax.ShapeDtypeStruct((1, D_MODEL), F32),
                   jax.ShapeDtypeStruct((n_seq, 1, D_MODEL), F32), jax.ShapeDtypeStruct((n_seq, 1, D_MODEL), F32),
                   red_uq, red_ukv],
        in_specs=[tok(D_MODEL), tok(D_MODEL), tok(A_GM), tok(D_MODEL), tok(512), tok(256), tok(256),
                  per_b(3 * D_MODEL), _full(b_ada.shape), _full(ng.shape), _full(wa.shape), ANY_SPEC, ANY_SPEC],
        out_specs=[tok(D_MODEL), _full((D_MODEL, A_END)), _full((1, D_MODEL)), per_b(D_MODEL), per_b(D_MODEL),
                   ANY_SPEC, ANY_SPEC],
        scratch_shapes=scratch_uq + scratch_ukv,
        compiler_params=_params(1),
    )(x, dx2, dz, dg, dqs, dkd, dvd, mod, b_ada, ng, wa, g_uq, g_ukv)


def _adam_math(w, g, m, v):
    m_new = ADAM_B1 * m + (1.0 - ADAM_B1) * g
    v_new = ADAM_B2 * v + (1.0 - ADAM_B2) * (g * g)
    m_hat = m_new / (1.0 - ADAM_B1 ** ADAM_STEP)
    v_hat = v_new / (1.0 - ADAM_B2 ** ADAM_STEP)
    delta = -ADAM_LR * (m_hat / (jnp.sqrt(v_hat) + ADAM_EPS) + ADAM_WD * w)
    return delta, m_new, v_new


def _adam_call(name, w, g, m, v):
    rows, cols = w.shape
    tr = next((t for t in (256, 128, 88) if rows % t == 0), rows)

    def body(w_ref, g_ref, m_ref, v_ref, d_ref, mo_ref, vo_ref):
        d, mn, vn = _adam_math(w_ref[...], g_ref[...], m_ref[...], v_ref[...])
        d_ref[...] = d
        mo_ref[...] = mn
        vo_ref[...] = vn

    spec = pl.BlockSpec((tr, cols), lambda i: (i, 0))
    return pl.pallas_call(
        body, name=name, grid=(rows // tr,),
        out_shape=[jax.ShapeDtypeStruct(w.shape, F32)] * 3,
        in_specs=[spec] * 4, out_specs=[spec] * 3,
        compiler_params=_params(1),
    )(w, g, m, v)


def _ada_bwd_call(act_all, dmod_cols, w, m, v):
    rows, cols = w.shape
    tr = 256

    def body(a_ref, dm_ref, w_ref, m_ref, v_ref, g_ref, d_ref, mo_ref, vo_ref):
        g = _dot_tn(a_ref[...].astype(BF16), dm_ref[...].astype(BF16))
        d, mn, vn = _adam_math(w_ref[...], g, m_ref[...], v_ref[...])
        g_ref[...] = g
        d_ref[...] = d
        mo_ref[...] = mn
        vo_ref[...] = vn

    spec = pl.BlockSpec((tr, cols), lambda i: (i, 0))
    nb = act_all.shape[0]
    return pl.pallas_call(
        body, name="ada_bwd", grid=(rows // tr,),
        out_shape=[jax.ShapeDtypeStruct(w.shape, F32)] * 4,
        in_specs=[pl.BlockSpec((nb, tr), lambda i: (0, i)), _full(dmod_cols.shape), spec, spec, spec],
        out_specs=[spec] * 4,
        compiler_params=_params(1),
    )(act_all, dmod_cols, w, m, v)


SMALL_ROW = {"norm_gain": (0, 1024), "final_gain": (1024, 2048), "q_norm_gain": (2048, 2432),
             "kv_norm_gain": (2432, 2688), "swa_sinks": (2688, 2696), "loss": (2816, 2944)}
SMALL_ORDER = ("b_ada", "norm_gain", "q_norm_gain", "kv_norm_gain", "swa_sinks", "final_gain")


def _small_call(parts_all, n_seq, params):
    k = len(params)

    def body(p_ref, *refs):
        ins, outs, loss_ref = refs[:3 * k], refs[3 * k:7 * k], refs[7 * k]
        row = p_ref[n_seq:n_seq + 1, :]
        for dv in range(1, 8):
            r0 = dv * ROWS_PER_DEVICE + n_seq
            row = row + p_ref[r0:r0 + 1, :]
        gb = None
        for dv in range(8):
            for r in range(n_seq):
                r0 = dv * ROWS_PER_DEVICE + r
                gb = p_ref[r0:r0 + 1, :] if gb is None else gb + p_ref[r0:r0 + 1, :]
        for j, name in enumerate(SMALL_ORDER):
            g = gb if name == "b_ada" else row[:, SMALL_ROW[name][0]:SMALL_ROW[name][1]]
            d, mn, vn = _adam_math(ins[3 * j][...], g, ins[3 * j + 1][...], ins[3 * j + 2][...])
            outs[4 * j][...] = g
            outs[4 * j + 1][...] = d
            outs[4 * j + 2][...] = mn
            outs[4 * j + 3][...] = vn
        loss_ref[...] = row[:, SMALL_ROW["loss"][0]:SMALL_ROW["loss"][1]]

    flat = [t for p in params for t in p]
    res = pl.pallas_call(
        body, name="small_update", grid=(1,),
        out_shape=[jax.ShapeDtypeStruct(p[0].shape, F32) for p in params for _ in range(4)]
        + [jax.ShapeDtypeStruct((1, HEAD_LANES), F32)],
        in_specs=[_full(parts_all.shape)] + [_full(t.shape) for t in flat],
        out_specs=[_full(p[0].shape) for p in params for _ in range(4)] + [_full((1, HEAD_LANES))],
        compiler_params=_params(1),
    )(parts_all, *flat)
    return [res[4 * j:4 * j + 4] for j in range(k)], res[4 * k]


def _rot(t):
    half = t.shape[-1] // 2
    return jnp.concatenate([-t[..., half:], t[..., :half]], axis=-1)


def _rot_t(g):
    half = g.shape[-1] // 2
    return jnp.concatenate([g[..., half:], -g[..., :half]], axis=-1)


def _columns(segments, lo, hi):
    out, at = [], 0
    for seg in segments:
        n = seg.shape[1]
        a, b = max(lo, at), min(hi, at + n)
        if a < b:
            out.append(seg[:, a - at:b - at])
        at += n
    return out


def _prepare_weights(w_in_blocks, w_uq, w_ukv):
    o = [0]
    for s in IN_SPLITS:
        o.append(o[-1] + s)
    part = lambda a, b: _columns(w_in_blocks, a, b)
    kr = jnp.concatenate(part(o[2], o[3]), axis=1)
    zero = jnp.zeros((kr.shape[0], 32), kr.dtype)
    wa = jnp.concatenate(part(0, o[2]) + [_rot(kr), zero, kr, zero] + part(o[3], o[8]), axis=1)
    uq = w_uq.reshape(Q_LORA, N_HEADS, MLA_NOPE + MLA_ROPE)
    zq = jnp.zeros((Q_LORA, N_HEADS, 32), w_uq.dtype)
    uq_full = jnp.concatenate([uq, zq], axis=-1).reshape(Q_LORA, 1024)
    uq_rot = jnp.concatenate([jnp.zeros((Q_LORA, N_HEADS, 64), w_uq.dtype), _rot(uq[..., MLA_NOPE:]), zq],
                             axis=-1).reshape(Q_LORA, 1024)
    wq2 = jnp.concatenate([uq_full, uq_rot], axis=1)
    ukv = w_ukv.reshape(KV_LORA, N_HEADS, 128)
    k_full = jnp.concatenate([ukv[..., :64], jnp.zeros((KV_LORA, N_HEADS, 64), w_ukv.dtype)], axis=-1).reshape(KV_LORA, 1024)
    wkv = jnp.concatenate([k_full, ukv[..., 64:].reshape(KV_LORA, 512)], axis=1)
    return wa, wq2, wkv


def _restore_in(gwa):
    gkr = gwa[:, A_KR + 64:A_KR + 96] + _rot_t(gwa[:, A_KR:A_KR + 32])
    in_order = [gwa[:, :A_KR], gkr, gwa[:, A_GM:]]
    n = D_IN // 4
    return [jnp.concatenate(_columns(in_order, k * n, (k + 1) * n), axis=1) for k in range(4)]


def _restore_up(gwq2, gwkv):
    gf = gwq2[:, :1024].reshape(Q_LORA, N_HEADS, 128)
    gr = gwq2[:, 1024:].reshape(Q_LORA, N_HEADS, 128)
    g_uq = jnp.concatenate([gf[..., :64], gf[..., 64:96] + _rot_t(gr[..., 64:96])], axis=-1).reshape(Q_LORA, 768)
    gk = gwkv[:, :1024].reshape(KV_LORA, N_HEADS, 128)[..., :64]
    gv = gwkv[:, 1024:].reshape(KV_LORA, N_HEADS, 64)
    g_ukv = jnp.concatenate([gk, gv], axis=-1).reshape(KV_LORA, 1024)
    return g_uq, g_ukv


def _local_step(x, positions, target, mod_rows, b_ada, ng, qg, kvg, sinks, fg, w_in_b, w_uq_b, w_ukv_b, w_out_half):
    n_seq, seq, _ = x.shape
    n_tok = n_seq * seq
    x2d = x.reshape(n_tok, D_MODEL)
    t2d = target.reshape(n_tok, D_MODEL)
    pos_f = positions.astype(F32)
    pos_col = pos_f.reshape(n_tok, 1)
    pos_row = pos_f.reshape(n_tok // SWA_WINDOW, 1, SWA_WINDOW)
    mod3 = mod_rows.reshape(n_seq, 1, 3 * D_MODEL)
    inv = ROPE_THETA ** (-jnp.arange(0, MLA_ROPE, 2, dtype=F32) / MLA_ROPE)
    inv128 = jnp.concatenate([jnp.zeros((64,), F32), inv, inv, jnp.zeros((32,), F32)]).reshape(1, 128)
    fg2 = fg.reshape(1, D_MODEL)

    wa, wq2, wkv = _prepare_weights(w_in_b, w_uq_b, w_ukv_b)

    zqkv, gates, qf, kf, v, qs, kd, vd, rope, f_out = _pre_call(x2d, pos_col, mod3, b_ada, ng, qg, kvg, inv128, wa, wq2,
                                                                 wkv, w_out_half, seq)
    w_out_b = f_out.reshape(D_MODEL, D_MODEL)
    o_mla, lse_mla = _mla_fwd_call(qf, kf, v, n_seq, seq)
    o_swa, lse_swa = _swa_fwd_call(qs, kd, vd, pos_col, pos_row, sinks, n_seq, seq)
    dx2, do, dg, g_out, g_fg, dgate, loss, delta_mla, delta_swa = _post_call(x2d, t2d, o_mla, o_swa, gates, mod3, b_ada, fg2, w_out_b, seq)
    dqf, dkf, dv = _mla_bwd_call(qf, kf, v, do, delta_mla, lse_mla, n_seq, seq)
    dqs, dkd, dvd, dsink, r_out = _swa_bwd_call(qs, kd, vd, do, delta_swa, lse_swa, pos_col, pos_row, sinks,
                                                g_out.reshape(4, 2, D_MODEL // 8, D_MODEL), n_seq, seq)
    dz, g_wq2, g_wkv, g_qg, g_kvg = _mid_bwd_call(dqf, dkf, dv, zqkv, rope, qg, kvg, wq2, wkv, seq)
    g_uq, g_ukv = _restore_up(g_wq2, g_wkv)
    gx, g_wa, g_ng, dshift, dscale, r_uq, r_ukv = _in_bwd_call(
        x2d, dx2, dz, dg, dqs, dkd, dvd, mod3, b_ada, ng, wa, _by_owner(g_uq, g_uq.shape[1] // 4),
        _by_owner(g_ukv, g_ukv.shape[1] // 4), seq)
    g_in = _restore_in(g_wa)
    dmod = jnp.concatenate([dshift, dscale, dgate], axis=-1).reshape(n_seq, 3 * D_MODEL)
    small_row = jnp.concatenate([g_ng, g_fg, g_qg, g_kvg, jnp.pad(jnp.sum(dsink, axis=1).reshape(1, N_HEADS), ((0, 0), (0, 120))),
                                 loss, jnp.zeros((1, 128), F32)], axis=1)
    return gx.reshape(x.shape), g_in, (r_uq, r_ukv, r_out), small_row, dmod


def kernel(x, c, positions, w_ada, b_ada, norm_gain, w_in, q_norm_gain, kv_norm_gain, w_uq, w_ukv, swa_sinks, w_out, final_gain, loss_target, m_w_ada, m_b_ada, m_norm_gain, m_w_in, m_q_norm_gain, m_kv_norm_gain, m_w_uq, m_w_ukv, m_swa_sinks, m_w_out, m_final_gain, v_w_ada, v_b_ada, v_norm_gain, v_w_in, v_q_norm_gain, v_kv_norm_gain, v_w_uq, v_w_ukv, v_swa_sinks, v_w_out, v_final_gain):
    n_seq = x.shape[0]
    xi, yi, ci = lax.axis_index("x"), lax.axis_index("y"), lax.axis_index("c")
    dev = 4 * xi + 2 * yi + ci
    chip = 2 * xi + yi

    halves = lambda w: w.astype(BF16).reshape(2, w.shape[0] // 2, w.shape[1])
    c_blk = jnp.pad(c, ((0, ROWS_PER_DEVICE - n_seq), (0, 0)))
    act_all, pieces, f_in, f_uq, f_ukv = _comm_fwd_call(c_blk, w_ada[0], [halves(w_in[0]), halves(w_uq[0]), halves(w_ukv[0])])
    mine = lax.dynamic_slice_in_dim(pieces, dev * ROWS_PER_DEVICE, n_seq, axis=1)
    mod_rows = jnp.transpose(mine, (1, 0, 2)).reshape(n_seq, 3 * D_MODEL)
    cols = lambda t, r: jnp.transpose(t.reshape(4, r, -1), (1, 0, 2)).reshape(r, -1)
    w_in_blocks = [f_in[k].reshape(D_MODEL, -1) for k in range(4)]
    w_uq_b, w_ukv_b = cols(f_uq, Q_LORA), cols(f_ukv, KV_LORA)

    gx, g_in_blocks, (r_uq, r_ukv, r_out), small_row, dmod = _local_step(
        x, positions, loss_target, mod_rows, b_ada, norm_gain, q_norm_gain, kv_norm_gain, swa_sinks, final_gain,
        w_in_blocks, w_uq_b, w_ukv_b, halves(w_out[0]))

    grads = [jnp.stack(g_in_blocks).reshape(4, 2, D_MODEL // 2, -1)]
    part = jnp.concatenate([dmod, small_row, jnp.zeros((ROWS_PER_DEVICE - n_seq - 1, 3 * D_MODEL), F32)], axis=0)
    r_in, parts_all = _comm_bwd_call(grads, part)
    g_in_s, g_uq_s = r_in.reshape(w_in.shape[1:]), r_uq.reshape(w_uq.shape[1:])
    g_ukv_s, g_out_s = r_ukv.reshape(w_ukv.shape[1:]), r_out.reshape(w_out.shape[1:])

    tr = lambda a: jnp.swapaxes(a[0], 0, 1)
    back = lambda ts: [jnp.swapaxes(t, 0, 1) for t in ts]
    d_in, nm_in, nv_in = back(_adam_call("adam_w_in", tr(w_in), g_in_s.T, tr(m_w_in), tr(v_w_in)))
    d_uq, nm_uq, nv_uq = back(_adam_call("adam_w_uq", tr(w_uq), g_uq_s.T, tr(m_w_uq), tr(v_w_uq)))
    d_ukv, nm_ukv, nv_ukv = _adam_call("adam_w_ukv", w_ukv[0], g_ukv_s, m_w_ukv[0], v_w_ukv[0])
    d_out, nm_out, nv_out = _adam_call("adam_w_out", w_out[0], g_out_s, m_w_out[0], v_w_out[0])
    dmod_cols = lax.dynamic_slice_in_dim(parts_all, chip * 768, 768, axis=1)
    g_ada, d_ada, nm_ada, nv_ada = _ada_bwd_call(act_all, dmod_cols, w_ada[0], m_w_ada[0], v_w_ada[0])

    row = lambda t: t.reshape(1, -1)
    small = {"b_ada": (b_ada, m_b_ada, v_b_ada), "norm_gain": (norm_gain, m_norm_gain, v_norm_gain),
             "q_norm_gain": (q_norm_gain, m_q_norm_gain, v_q_norm_gain),
             "kv_norm_gain": (kv_norm_gain, m_kv_norm_gain, v_kv_norm_gain),
             "swa_sinks": (swa_sinks, m_swa_sinks, v_swa_sinks),
             "final_gain": (row(final_gain), row(m_final_gain), row(v_final_gain))}
    res, loss_row = _small_call(parts_all, n_seq, [small[name] for name in SMALL_ORDER])
    res = dict(zip(SMALL_ORDER, res))
    res["final_gain"] = [t.reshape(-1) for t in res["final_gain"]]
    e = lambda t: t[None]
    big = {"w_ada": (e(g_ada), e(d_ada), e(nm_ada), e(nv_ada)), "w_in": (e(g_in_s), e(d_in), e(nm_in), e(nv_in)),
           "w_uq": (e(g_uq_s), e(d_uq), e(nm_uq), e(nv_uq)), "w_ukv": (e(g_ukv_s), e(d_ukv), e(nm_ukv), e(nv_ukv)),
           "w_out": (e(g_out_s), e(d_out), e(nm_out), e(nv_out))}
    order = ("w_ada", "b_ada", "norm_gain", "w_in", "q_norm_gain", "kv_norm_gain", "w_uq", "w_ukv", "swa_sinks", "w_out",
             "final_gain")
    pick = lambda kind: [(big[n] if n in big else res[n])[kind] for n in order]
    return (loss_row[0, 0], gx, *pick(0), *pick(1), *pick(2), *pick(3))
```

```python
import jax
import jax.numpy as jnp
from jax import lax
from jax.experimental import pallas as pl
from jax.experimental.pallas import tpu as pltpu

F32 = jnp.float32
BF16 = jnp.bfloat16

D_MODEL = 1024
Q_LORA = 384
KV_LORA = 256
N_HEADS = 8
MLA_NOPE = 64
MLA_ROPE = 32
HEAD_LANES = 128
HALF = 64
SWA_WINDOW = 128
EPS = 1e-6
ROPE_THETA = 10000.0
MLA_SCALE = (MLA_NOPE + MLA_ROPE) ** -0.5
LOG2E = 1.4426950408889634
LN2 = 0.6931471805599453
SWA_SCALE = 64 ** -0.5
NEG = -1e30

ADAM_LR = 0.001
ADAM_B1 = 0.9
ADAM_B2 = 0.999
ADAM_EPS = 1e-08
ADAM_WD = 0.01
ADAM_STEP = 10

A_ZQ, A_ZKV, A_KR, A_GM, A_QS, A_KS, A_VS, A_GS, A_END = 0, 384, 640, 768, 1280, 1792, 1920, 2048, 2560
IN_SPLITS = (384, 256, 32, 512, 512, 128, 128, 512)
D_IN = sum(IN_SPLITS)

TOKEN_TILE = 512
ATT_TILE = 256
VMEM_LIMIT = 56 * 1024 * 1024


def _dot(a, b):
    return jnp.dot(a, b, preferred_element_type=F32)


def _dot_nt(a, b):
    return lax.dot_general(a, b, (((1,), (1,)), ((), ())), preferred_element_type=F32)


def _dot_tn(a, b):
    return lax.dot_general(a, b, (((0,), (0,)), ((), ())), preferred_element_type=F32)


def _params(n_grid):
    return pltpu.CompilerParams(dimension_semantics=("arbitrary",) * n_grid, vmem_limit_bytes=VMEM_LIMIT)


def _full(shape):
    nd = len(shape)
    return pl.BlockSpec(shape, lambda *_: (0,) * nd, pipeline_mode=pl.Buffered(1))


def _sigmoid(g):
    return 1.0 / (1.0 + jnp.exp(-g))


SUB_TILE = 256


def _sub_tiles(tm):
    sub = min(SUB_TILE, tm)
    return [slice(s * sub, (s + 1) * sub) for s in range(tm // sub)]


MESH = pl.DeviceIdType.MESH
ROWS_PER_DEVICE = 8
VMEM_SPEC = pl.BlockSpec(memory_space=pltpu.VMEM)
ANY_SPEC = pl.BlockSpec(memory_space=pl.ANY)


def _position():
    x, y, c = lax.axis_index("x"), lax.axis_index("y"), lax.axis_index("c")
    sibling = (x, y, 1 - c)
    others = [(1 - x, y, c), (x, 1 - y, c), (1 - x, 1 - y, c)]
    return (x, y, c), 4 * x + 2 * y + c, 2 * x + y, sibling, others


def _rows_of(dev):
    return pl.ds(pl.multiple_of(dev * ROWS_PER_DEVICE, ROWS_PER_DEVICE), ROWS_PER_DEVICE)


def _all_to_all_rows(block_ref, table_ref, dev, me, send_sems, recv_sems):
    x, y, c = me
    waits = []
    for k in range(1, 8):
        peer = (1 - x if k & 4 else x, 1 - y if k & 2 else y, 1 - c if k & 1 else c)
        pltpu.make_async_remote_copy(src_ref=block_ref, dst_ref=table_ref.at[_rows_of(dev)], send_sem=send_sems.at[k - 1],
                                     recv_sem=recv_sems.at[k - 1], device_id=peer, device_id_type=MESH).start()
        waits.append(pltpu.make_async_remote_copy(
            src_ref=block_ref, dst_ref=table_ref.at[_rows_of(jnp.bitwise_xor(dev, k))], send_sem=send_sems.at[k - 1],
            recv_sem=recv_sems.at[k - 1], device_id=peer, device_id_type=MESH))
    return waits


def _comm_fwd_call(c_blk, w_ada, shards):
    n = len(shards)

    def body(c_ref, wada_ref, *refs):
        w_refs, act_ref, pieces_ref, full_refs = refs[:n], refs[n], refs[n + 1], refs[n + 2:2 * n + 2]
        c_all_ref = refs[2 * n + 2]
        c_send, c_recv, p_send, p_recv, w_send, w_recv, f_send, f_recv, loc_sem = refs[2 * n + 3:]
        me, dev, chip, sibling, others = _position()
        core = me[2]
        chip_of = [2 * p[0] + p[1] for p in others]

        local = [pltpu.make_async_copy(w_refs[i], full_refs[i].at[chip], loc_sem.at[i]) for i in range(n)]
        for cp in local:
            cp.start()

        def over_ici(i, j, src_chip):
            return pltpu.make_async_remote_copy(
                src_ref=w_refs[i].at[core], dst_ref=full_refs[i].at[src_chip, core], send_sem=w_send.at[3 * i + j],
                recv_sem=w_recv.at[3 * i + j], device_id=others[j], device_id_type=MESH)

        def to_sibling(i, j, half):
            return pltpu.make_async_remote_copy(
                src_ref=full_refs[i].at[chip_of[j], half], dst_ref=full_refs[i].at[chip_of[j], half],
                send_sem=f_send.at[3 * i + j], recv_sem=f_recv.at[3 * i + j], device_id=sibling, device_id_type=MESH)

        c_all_ref[_rows_of(dev), :] = c_ref[...]
        c_waits = _all_to_all_rows(c_ref, c_all_ref, dev, me, c_send, c_recv)
        sent = [over_ici(i, j, chip) for i in range(n) for j in range(3)]
        for cp in sent:
            cp.start()

        for cp in c_waits:
            cp.wait()
        cv = c_all_ref[...]
        act = cv * _sigmoid(cv)
        act_ref[...] = act
        pieces_ref[chip] = _dot(act.astype(BF16), wada_ref[...].astype(BF16))
        piece = lambda j, src_chip: pltpu.make_async_remote_copy(
            src_ref=pieces_ref.at[chip], dst_ref=pieces_ref.at[src_chip], send_sem=p_send.at[j], recv_sem=p_recv.at[j],
            device_id=others[j], device_id_type=MESH)
        for j in range(3):
            piece(j, chip).start()

        for i in range(n):
            for j in range(3):
                over_ici(i, j, chip_of[j]).wait_recv()
                to_sibling(i, j, core).start()
        for j in range(3):
            piece(j, chip).wait_send()
            piece(j, chip_of[j]).wait_recv()
        for i in range(n):
            for j in range(3):
                to_sibling(i, j, 1 - core).wait_recv()
                to_sibling(i, j, core).wait_send()
        for cp in sent:
            cp.wait_send()
        for cp in local:
            cp.wait()

    rows = 8 * ROWS_PER_DEVICE
    dma = pltpu.SemaphoreType.DMA
    return pl.pallas_call(
        body, name="comm_fwd",
        out_shape=[jax.ShapeDtypeStruct((rows, D_MODEL), F32), jax.ShapeDtypeStruct((4, rows, w_ada.shape[1]), F32)]
        + [jax.ShapeDtypeStruct((4,) + s.shape, s.dtype) for s in shards],
        in_specs=[VMEM_SPEC, VMEM_SPEC] + [ANY_SPEC] * n,
        out_specs=[VMEM_SPEC, VMEM_SPEC] + [ANY_SPEC] * n,
        scratch_shapes=[pltpu.VMEM((rows, D_MODEL), F32), dma((7,)), dma((7,)), dma((3,)), dma((3,)),
                        dma((3 * n,)), dma((3 * n,)), dma((3 * n,)), dma((3 * n,)), dma((n,))],
        compiler_params=pltpu.CompilerParams(vmem_limit_bytes=VMEM_LIMIT),
    )(c_blk, w_ada, *shards)


def _comm_bwd_call(grads, part):
    n = len(grads)

    def body(part_ref, *refs):
        g_refs, f_refs, parts_ref = refs[:n], refs[n:2 * n], refs[2 * n]
        scratch = refs[2 * n + 1:]
        a_refs, b_refs, p_refs, r_refs = (scratch[k * n:(k + 1) * n] for k in range(4))
        s_send, s_recv, d_send, d_recv, e_send, e_recv, h_send, h_recv, loc_sem = scratch[4 * n:]
        me, dev, chip, sibling, others = _position()
        core = me[2]
        chip_of = [2 * p[0] + p[1] for p in others]

        parts_ref[_rows_of(dev), :] = part_ref[...]
        s_waits = _all_to_all_rows(part_ref, parts_ref, dev, me, s_send, s_recv)

        mine = [pltpu.make_async_copy(g_refs[i].at[:, core], a_refs[i], loc_sem.at[i]) for i in range(n)]
        swap = [pltpu.make_async_remote_copy(src_ref=g_refs[i].at[:, 1 - core], dst_ref=b_refs[i], send_sem=d_send.at[i],
                                             recv_sem=d_recv.at[i], device_id=sibling, device_id_type=MESH) for i in range(n)]
        order = sorted(range(n), key=lambda i: g_refs[i].shape[2] * g_refs[i].shape[3])
        for i in order:
            mine[i].start()
            swap[i].start()
        cross = [pltpu.make_async_remote_copy(src_ref=p_refs[i].at[chip_of[j]], dst_ref=r_refs[i].at[j],
                                              send_sem=e_send.at[3 * i + j], recv_sem=e_recv.at[3 * i + j],
                                              device_id=others[j], device_id_type=MESH) for i in range(n) for j in range(3)]
        for i in order:
            mine[i].wait()
            swap[i].wait()
            for k in range(4):
                s = a_refs[i][k] + b_refs[i][k]
                a_refs[i][k] = s
                p_refs[i][k] = s.astype(BF16)
            for j in range(3):
                cross[3 * i + j].start()
        share = {}
        for i in order:
            for j in range(3):
                cross[3 * i + j].wait()
            f_refs[i][core] = (a_refs[i][chip] + r_refs[i][0].astype(F32) + r_refs[i][1].astype(F32)
                               + r_refs[i][2].astype(F32))
            share[i] = pltpu.make_async_remote_copy(src_ref=f_refs[i].at[core], dst_ref=f_refs[i].at[core],
                                                    send_sem=h_send.at[i], recv_sem=h_recv.at[i], device_id=sibling,
                                                    device_id_type=MESH)
            share[i].start()
        for i in range(n):
            share[i].wait_send()
            pltpu.make_async_remote_copy(src_ref=f_refs[i].at[core], dst_ref=f_refs[i].at[1 - core], send_sem=h_send.at[i],
                                         recv_sem=h_recv.at[i], device_id=sibling, device_id_type=MESH).wait_recv()
        for cp in s_waits:
            cp.wait()

    rows = 8 * ROWS_PER_DEVICE
    dma = pltpu.SemaphoreType.DMA
    quarter = [(4,) + g.shape[2:] for g in grads]
    return pl.pallas_call(
        body, name="comm_bwd",
        out_shape=[jax.ShapeDtypeStruct((2,) + g.shape[2:], F32) for g in grads]
        + [jax.ShapeDtypeStruct((rows, part.shape[1]), F32)],
        in_specs=[VMEM_SPEC] + [ANY_SPEC] * n,
        out_specs=[VMEM_SPEC] * (n + 1),
        scratch_shapes=[pltpu.VMEM(q, F32) for q in quarter] + [pltpu.VMEM(q, F32) for q in quarter]
        + [pltpu.VMEM(q, BF16) for q in quarter] + [pltpu.VMEM((3,) + q[1:], BF16) for q in quarter]
        + [dma((7,)), dma((7,)), dma((n,)), dma((n,)), dma((3 * n,)), dma((3 * n,)), dma((n,)), dma((n,)), dma((n,))],
        compiler_params=pltpu.CompilerParams(vmem_limit_bytes=VMEM_LIMIT),
    )(part, *grads)


def _by_owner(g, n):
    return jnp.transpose(g.reshape(g.shape[0], 4, n), (1, 0, 2)).reshape(4, 2, g.shape[0] // 2, n)


def _reduce_operands(g):
    quarter = (4,) + g.shape[2:]
    dma = pltpu.SemaphoreType.DMA
    scratch = [pltpu.VMEM(quarter, F32), pltpu.VMEM(quarter, F32), pltpu.VMEM(quarter, BF16),
               pltpu.VMEM((3,) + quarter[1:], BF16), dma((5,)), dma((5,)), dma((2,))]
    return jax.ShapeDtypeStruct((2,) + g.shape[2:], F32), scratch


def _grad_reduce(step, n_steps, g_ref, f_ref, a_ref, b_ref, p_ref, r_ref, send, recv, loc_sem):
    me, _, chip, sibling, others = _position()
    core = me[2]
    chip_of = [2 * p[0] + p[1] for p in others]
    remote = lambda src, dst, k, to: pltpu.make_async_remote_copy(
        src_ref=src, dst_ref=dst, send_sem=send.at[k], recv_sem=recv.at[k], device_id=to, device_id_type=MESH)
    mine = pltpu.make_async_copy(g_ref.at[:, core], a_ref, loc_sem.at[0])
    swap = remote(g_ref.at[:, 1 - core], b_ref, 0, sibling)
    cross = [remote(p_ref.at[chip_of[j]], r_ref.at[j], 1 + j, others[j]) for j in range(3)]
    total_ref = b_ref.at[0]
    keep = pltpu.make_async_copy(total_ref, f_ref.at[core], loc_sem.at[1])
    share = lambda half: remote(total_ref, f_ref.at[half], 4, sibling)
    at = [k * (n_steps - 1) // 3 for k in range(4)]

    @pl.when(step == at[0])
    def _():
        mine.start()
        swap.start()

    @pl.when(step == at[1])
    def _():
        mine.wait()
        swap.wait()
        for k in range(4):
            s = a_ref[k] + b_ref[k]
            a_ref[k] = s
            p_ref[k] = s.astype(BF16)
        for cp in cross:
            cp.start()

    @pl.when(step == at[2])
    def _():
        for cp in cross:
            cp.wait()
        total_ref[...] = a_ref[chip] + r_ref[0].astype(F32) + r_ref[1].astype(F32) + r_ref[2].astype(F32)
        keep.start()
        share(core).start()

    @pl.when(step == at[3])
    def _():
        keep.wait()
        share(core).wait_send()
        share(1 - core).wait_recv()


def _twice(t):
    lo = _lane_lo()
    other = pltpu.roll(t, HALF, 1)
    return jnp.concatenate([jnp.where(lo, t, other), jnp.where(lo, other, t)], axis=1)


def _once(g):
    first, second = g[:, :HEAD_LANES], g[:, HEAD_LANES:]
    return jnp.where(_lane_lo(), first + pltpu.roll(first, HALF, 1), second + pltpu.roll(second, HALF, 1))


def _rope_tables(pos_col, inv_row):
    ang = pos_col * inv_row
    return jnp.cos(ang), jnp.sin(ang)


def _pre_call(x, pos_col, mod, b_ada, ng, qg, kvg, inv128, wa, wq2, wkv, w_out_half, seq):
    n_tok = x.shape[0]
    tm = min(TOKEN_TILE, seq)
    per_seq = seq // tm
    n_steps = n_tok // tm

    def gather_w_out(step, wo_ref, full_ref, w_send, w_recv, f_send, f_recv, loc_sem):
        me, _, chip, sibling, others = _position()
        core = me[2]
        chip_of = [2 * p[0] + p[1] for p in others]
        local = pltpu.make_async_copy(wo_ref, full_ref.at[chip], loc_sem.at[0])

        def over_ici(j, src_chip):
            return pltpu.make_async_remote_copy(
                src_ref=wo_ref.at[core], dst_ref=full_ref.at[src_chip, core], send_sem=w_send.at[j], recv_sem=w_recv.at[j],
                device_id=others[j], device_id_type=MESH)

        def to_sibling(j, half):
            return pltpu.make_async_remote_copy(
                src_ref=full_ref.at[chip_of[j], half], dst_ref=full_ref.at[chip_of[j], half], send_sem=f_send.at[j],
                recv_sem=f_recv.at[j], device_id=sibling, device_id_type=MESH)

        @pl.when(step == 0)
        def _():
            local.start()
            for j in range(3):
                over_ici(j, chip).start()

        @pl.when(step == n_steps // 2)
        def _():
            for j in range(3):
                over_ici(j, chip_of[j]).wait_recv()
                to_sibling(j, core).start()

        @pl.when(step == n_steps - 1)
        def _():
            for j in range(3):
                to_sibling(j, 1 - core).wait_recv()
                to_sibling(j, core).wait_send()
                over_ici(j, chip).wait_send()
            local.wait()

    def body(x_ref, pos_ref, mod_ref, bada_ref, ng_ref, qg_ref, kvg_ref, inv_ref, wa_ref, wq_ref, wkv_ref, wo_ref,
             zqkv_ref, gates_ref, qf_ref, kf_ref, v_ref, qs_ref, kd_ref, vd_ref, rope_ref, full_ref, *sems):
        gather_w_out(pl.program_id(0), wo_ref, full_ref, *sems)
        xv = x_ref[...]
        modv = mod_ref[0] + bada_ref[...]
        shift, scale = modv[:, :D_MODEL], modv[:, D_MODEL:2 * D_MODEL]
        r1 = lax.rsqrt(jnp.mean(xv * xv, axis=-1, keepdims=True) + EPS)
        h = ((xv * r1) * ng_ref[...]) * (1.0 + scale) + shift
        hb = h.astype(BF16)
        za = _dot(hb, wa_ref[...])
        zkr = za[:, A_KR:A_GM]
        cos, sin = _rope_tables(pos_ref[...], inv_ref[...])
        rope_ref[:, :HEAD_LANES] = cos
        rope_ref[:, HEAD_LANES:] = sin
        zqkv_ref[...] = za[:, :A_KR]
        gates_ref[:, :512] = za[:, A_GM:A_QS]
        gates_ref[:, 512:] = za[:, A_GS:A_END]
        qs_ref[...] = (za[:, A_QS:A_KS] * (SWA_SCALE * LOG2E)).astype(BF16)
        kd_ref[...] = _twice(za[:, A_KS:A_VS]).astype(BF16)
        vd_ref[...] = _twice(za[:, A_VS:A_GS]).astype(BF16)
        zq, zkv = za[:, A_ZQ:A_ZKV], za[:, A_ZKV:A_KR]
        rq = lax.rsqrt(jnp.mean(zq * zq, axis=-1, keepdims=True) + EPS)
        qn = ((zq * rq) * qg_ref[...]).astype(BF16)
        qr = _dot(qn, wq_ref[...])
        cf, sf = jnp.tile(cos, (1, N_HEADS)), jnp.tile(sin, (1, N_HEADS))
        qf_ref[...] = ((qr[:, :1024] * cf + qr[:, 1024:] * sf) * (MLA_SCALE * LOG2E)).astype(BF16)
        rkv = lax.rsqrt(jnp.mean(zkv * zkv, axis=-1, keepdims=True) + EPS)
        kvn = ((zkv * rkv) * kvg_ref[...]).astype(BF16)
        kv = _dot(kvn, wkv_ref[...])
        kpe = jnp.where(_lane_lo(), 0.0, zkr * cos) + pltpu.roll(zkr, HALF, 1) * sin
        kf_ref[...] = (kv[:, :1024] + jnp.tile(kpe, (1, N_HEADS))).astype(BF16)
        v_ref[...] = kv[:, 1024:].astype(BF16)

    tok = lambda w: pl.BlockSpec((tm, w), lambda i: (i, 0))
    outs = [(640, F32), (1024, F32), (1024, BF16), (1024, BF16), (512, BF16), (512, BF16), (256, BF16), (256, BF16),
            (2 * HEAD_LANES, F32)]
    dma = pltpu.SemaphoreType.DMA
    return pl.pallas_call(
        body, name="pre", grid=(n_steps,),
        out_shape=[jax.ShapeDtypeStruct((n_tok, w), dt) for w, dt in outs]
        + [jax.ShapeDtypeStruct((4,) + w_out_half.shape, w_out_half.dtype)],
        in_specs=[tok(D_MODEL), tok(1), pl.BlockSpec((1, 1, 3 * D_MODEL), lambda i: (i // per_seq, 0, 0)),
                  _full(b_ada.shape), _full(ng.shape), _full(qg.shape), _full(kvg.shape), _full(inv128.shape),
                  _full(wa.shape), _full(wq2.shape), _full(wkv.shape), ANY_SPEC],
        out_specs=[tok(w) for w, _ in outs] + [ANY_SPEC],
        scratch_shapes=[dma((3,)), dma((3,)), dma((3,)), dma((3,)), dma((1,))],
        compiler_params=_params(1),
    )(x, pos_col, mod, b_ada, ng, qg, kvg, inv128, wa, wq2, wkv, w_out_half)


def _lane_lo(width=HEAD_LANES):
    return lax.broadcasted_iota(jnp.int32, (1, width), 1) < HALF


def _eye(n=HEAD_LANES):
    r = lax.broadcasted_iota(jnp.int32, (n, n), 0)
    c = lax.broadcasted_iota(jnp.int32, (n, n), 1)
    return jnp.where(r == c, 1.0, 0.0).astype(BF16)


def _mla_fwd_call(qf, kf, v, n_seq, seq):
    tq = min(ATT_TILE, seq)
    nq = seq // tq

    ext = HALF + 16

    def body(q_ref, k_ref, v_ref, o_ref, lse_ref, vt_ref, acc_ref):
        i = pl.program_id(1)
        eye = _eye()

        @pl.when(i == 0)
        def _():
            for h in range(N_HEADS):
                vt_ref[h * ext + HALF:(h + 1) * ext, :] = jnp.ones((16, seq), BF16)
            for t in range(nq):
                for p in range(N_HEADS // 2):
                    pair = slice(p * HEAD_LANES, (p + 1) * HEAD_LANES)
                    v_t = _dot_nt(eye, v_ref[t * tq:(t + 1) * tq, pair]).astype(BF16)
                    for hh in range(2):
                        r0 = (2 * p + hh) * ext
                        vt_ref[r0:r0 + HALF, t * tq:(t + 1) * tq] = v_t[hh * HALF:(hh + 1) * HALF, :]

        q = q_ref[...]
        qcol = i * tq + lax.broadcasted_iota(jnp.int32, (1, tq), 1)
        heads = range(N_HEADS)
        lanes = [slice(h * HEAD_LANES, (h + 1) * HEAD_LANES) for h in heads]

        def make_step(masked, n_tiles):
            def step(kt0, carry):
                tiles = range(n_tiles)
                start = pl.multiple_of(kt0 * tq, tq)
                ks = [k_ref[pl.ds(pl.multiple_of((kt0 + t) * tq, tq), tq), :] for t in tiles]
                vt = vt_ref[:, pl.ds(start, n_tiles * tq)]
                last = n_tiles - 1
                if masked:
                    keep = ((kt0 + last) * tq + lax.broadcasted_iota(jnp.int32, (tq, 1), 0)) <= qcol

                def scores(h):
                    sts = [_dot_nt(ks[t][:, lanes[h]], q[:, lanes[h]]) for t in tiles]
                    if masked:
                        sts[last] = jnp.where(keep, sts[last], NEG)
                    return sts

                def softmax(h, sts):
                    m_old = carry[h]
                    m_new = m_old
                    for st in sts:
                        m_new = jnp.maximum(m_new, jnp.max(st, axis=0, keepdims=True))
                    pt = jnp.concatenate([jnp.exp2(st - m_new).astype(BF16) for st in sts], axis=0)
                    return m_new, jnp.exp2(m_old - m_new), pt

                def values(h, alpha, pt):
                    rows = slice(h * ext, (h + 1) * ext)
                    acc_ref[rows, :] = acc_ref[rows, :] * alpha + _dot(vt[rows, :], pt)

                sts, soft, out = {0: scores(0), 1: scores(1)}, {}, {}
                for h in range(N_HEADS + 1):
                    if h + 2 < N_HEADS:
                        sts[h + 2] = scores(h + 2)
                    if h < N_HEADS:
                        soft[h] = softmax(h, sts.pop(h))
                    if h >= 1:
                        m_new, alpha, pt = soft.pop(h - 1)
                        values(h - 1, alpha, pt)
                        out[h - 1] = m_new
                return tuple(out[h] for h in heads)
            return step

        acc_ref[...] = jnp.zeros_like(acc_ref)
        init = (jnp.full((1, tq), NEG, F32),) * N_HEADS
        count = i + 1
        carry = lax.fori_loop(0, (count + 1) // 2 - 1, lambda j, c: make_step(False, 2)(2 * j, c), init)
        carry = lax.cond(count % 2 == 0, lambda c: make_step(True, 2)(i - 1, c), lambda c: make_step(True, 1)(i, c), carry)
        dens = [acc_ref[h * ext + HALF:h * ext + HALF + 1, :] for h in heads]
        acc_t = jnp.concatenate([acc_ref[h * ext:h * ext + HALF, :] * (1.0 / dens[h]) for h in heads], axis=0)
        o_ref[...] = acc_t.T
        for h in heads:
            lse_ref[0, h // 4, h % 4:h % 4 + 1, :] = carry[h] + jnp.log2(dens[h])

    n_tok = qf.shape[0]
    return pl.pallas_call(
        body, name="mla_fwd", grid=(n_seq, nq),
        out_shape=[jax.ShapeDtypeStruct((n_tok, 512), F32), jax.ShapeDtypeStruct((n_seq, 2, 4, seq), F32)],
        in_specs=[pl.BlockSpec((tq, 1024), lambda b, i: (b * nq + i, 0)),
                  pl.BlockSpec((seq, 1024), lambda b, i: (b, 0)),
                  pl.BlockSpec((seq, 512), lambda b, i: (b, 0))],
        out_specs=[pl.BlockSpec((tq, 512), lambda b, i: (b * nq + i, 0)),
                   pl.BlockSpec((1, 2, 4, tq), lambda b, i: (b, 0, 0, i))],
        scratch_shapes=[pltpu.VMEM((N_HEADS * ext, seq), BF16), pltpu.VMEM((N_HEADS * ext, tq), F32)],
        compiler_params=_params(2),
    )(qf, kf, v)


def _mla_bwd_call(qf, kf, v, do, delta, lse, n_seq, seq):
    tq = min(ATT_TILE, seq)
    nq = seq // tq

    nh = 4
    heads = range(nh)
    lanes = [slice(h * HEAD_LANES, (h + 1) * HEAD_LANES) for h in heads]

    def body(q_ref, k_ref, v_ref, do_ref, dl_ref, lse_ref, dq_ref, dk_ref, dv_ref,
             kt_ref, dot_ref, dqt_ref, dvt_ref):
        eye = _eye()
        sub_lo = lax.broadcasted_iota(jnp.int32, (HEAD_LANES, 1), 0) < HALF

        for t in range(nq):
            r = slice(t * tq, (t + 1) * tq)
            kv = k_ref[r, :]
            for h in heads:
                kt_ref[lanes[h], r] = _dot_nt(eye, kv[:, lanes[h]]).astype(BF16)
            for p in range(nh // 2):
                dov = do_ref[r, lanes[p]]
                dt = _dot_nt(eye, dov)
                dot_ref[2 * p, :, r] = jnp.where(sub_lo, dt, 0.0).astype(BF16)
                dot_ref[2 * p + 1, :, r] = jnp.where(sub_lo, 0.0, dt).astype(BF16)
        dqt_ref[...] = jnp.zeros_like(dqt_ref)
        dvt_ref[...] = jnp.zeros_like(dvt_ref)

        def flush_dv(tile, which):
            rows = pl.ds(pl.multiple_of(tile * tq, tq), tq)
            for p in range(nh // 2):
                dv_ref[rows, lanes[p]] = dvt_ref[which, p * HEAD_LANES:(p + 1) * HEAD_LANES, :].T

        def k_step(kt, _):
            slot = kt % 2
            kr = pl.ds(pl.multiple_of(kt * tq, tq), tq)
            k = k_ref[kr, :]
            vv = v_ref[kr, :]
            k_t = kt_ref[:, kr]
            krow = kt * tq + lax.broadcasted_iota(jnp.int32, (tq, 1), 0)

            def make_step(masked, n_tiles):
                def q_step(qt0, carry):
                    tiles = range(n_tiles)
                    qrs = [pl.ds(pl.multiple_of((qt0 + t) * tq, tq), tq) for t in tiles]
                    if masked:
                        flush_dv(jnp.maximum(kt - 1, 0), 1 - slot)
                    qs = [q_ref[qr, :] for qr in qrs]
                    if masked:
                        keep = krow <= (qt0 * tq + lax.broadcasted_iota(jnp.int32, (1, tq), 1))

                    def scores(h):
                        do_ts = [dot_ref[h, :, qr] for qr in qrs]
                        sts = [_dot_nt(k[:, lanes[h]], qs[t][:, lanes[h]]) for t in tiles]
                        dpts = [_dot(vv[:, lanes[h // 2]], do_ts[t]) for t in tiles]
                        return do_ts, sts, dpts

                    def softmax(h, sts, dpts):
                        pts, dsts = [], []
                        for t in tiles:
                            pt = jnp.exp2(sts[t] - lse_ref[0, 0, h:h + 1, qrs[t]])
                            if masked and t == 0:
                                pt = jnp.where(keep, pt, 0.0)
                            dsts.append((pt * (dpts[t] - dl_ref[0, h:h + 1, qrs[t]])).astype(BF16))
                            pts.append(pt.astype(BF16))
                        return pts, dsts

                    def grads(h, do_ts, pts, dsts):
                        half = slice((h % 2) * HALF, (h % 2 + 1) * HALF)
                        dst_all = jnp.concatenate(dsts, axis=1)
                        pt_all = jnp.concatenate(pts, axis=1)
                        do_all = jnp.concatenate([do_ts[t][half, :] for t in tiles], axis=1)
                        q_all = jnp.concatenate([qs[t][:, lanes[h]] for t in tiles], axis=0)
                        dvt_ref[slot, h * HALF:(h + 1) * HALF, :] += _dot_nt(do_all, pt_all)
                        dk_ref[kr, lanes[h]] += _dot(dst_all, q_all)
                        for t in tiles:
                            dqt_ref[lanes[h], qrs[t]] += _dot(k_t[lanes[h], :], dsts[t])

                    first, second = {0: scores(0)}, {}
                    for h in range(nh + 1):
                        if h + 1 < nh:
                            first[h + 1] = scores(h + 1)
                        if h < nh:
                            do_ts, sts, dpts = first.pop(h)
                            second[h] = (do_ts,) + softmax(h, sts, dpts)
                        if h >= 1:
                            grads(h - 1, *second.pop(h - 1))
                    return carry
                return q_step

            dk_ref[kr, :] = jnp.zeros((tq, nh * HEAD_LANES), F32)
            dvt_ref[slot] = jnp.zeros(dvt_ref.shape[1:], F32)
            count = nq - kt
            lax.cond(count >= 2, lambda c: make_step(True, 2)(kt, c), lambda c: make_step(True, 1)(kt, c), 0)
            lax.fori_loop(1, count // 2, lambda j, c: make_step(False, 2)(kt + 2 * j, c), 0)
            lax.cond(jnp.logical_and(count % 2 == 1, count >= 3), lambda c: make_step(False, 1)(nq - 1, c), lambda c: c, 0)
            return 0

        lax.fori_loop(0, nq, k_step, 0)
        flush_dv(nq - 1, (nq - 1) % 2)
        for t in range(nq):
            r = slice(t * tq, (t + 1) * tq)
            for h in heads:
                dq_ref[r, lanes[h]] = dqt_ref[lanes[h], r].T

    n_tok = qf.shape[0]
    groups = N_HEADS // nh
    blk = lambda w: pl.BlockSpec((seq, w), lambda b, g: (b, g))
    return pl.pallas_call(
        body, name="mla_bwd", grid=(n_seq, groups),
        out_shape=[jax.ShapeDtypeStruct((n_tok, 1024), F32), jax.ShapeDtypeStruct((n_tok, 1024), F32),
                   jax.ShapeDtypeStruct((n_tok, 512), F32)],
        in_specs=[blk(512), blk(512), blk(256), blk(256), pl.BlockSpec((1, nh, seq), lambda b, g: (g, 0, b)),
                  pl.BlockSpec((1, 1, nh, seq), lambda b, g: (b, g, 0, 0))],
        out_specs=[blk(512), blk(512), blk(256)],
        scratch_shapes=[pltpu.VMEM((nh * HEAD_LANES, seq), BF16), pltpu.VMEM((nh, HEAD_LANES, seq), BF16),
                        pltpu.VMEM((nh * HEAD_LANES, seq), F32), pltpu.VMEM((2, nh * HALF, tq), F32)],
        compiler_params=_params(2),
    )(qf, kf, v, do, delta, lse)


SWA_BLOCKS = 4


def _swa_block(n, pos_col_ref, posq):
    w = SWA_WINDOW
    start = pl.multiple_of(jnp.maximum(n - 1, 0) * w, w)
    posk = pos_col_ref[pl.ds(start, 2 * w), :]
    rel = (n * w + lax.broadcasted_iota(jnp.int32, (1, w), 1)) - (start + lax.broadcasted_iota(jnp.int32, (2 * w, 1), 0))
    valid = jnp.logical_and(rel >= 0, rel < w)
    return start, jnp.where(valid, posq - posk, 1e30)


def _alibi(h):
    return LOG2E * 2.0 ** -(h + 1)


def _transpose_rows(eye, src_ref, dst_ref, seq, width):
    step = 2 * SWA_WINDOW
    for t in range(seq // step):
        for p in range(width // HEAD_LANES):
            lanes = slice(p * HEAD_LANES, (p + 1) * HEAD_LANES)
            dst_ref[lanes, t * step:(t + 1) * step] = _dot_nt(eye, src_ref[t * step:(t + 1) * step, lanes]).astype(BF16)


def _swa_fwd_call(qs, kd, vd, pos_col, pos_row, sinks, n_seq, seq):
    w = SWA_WINDOW
    qb = SWA_BLOCKS
    steps = seq // (qb * w)
    ext = HALF + 16

    def body(q_ref, k_ref, v_ref, pc_ref, pr_ref, sink_ref, o_ref, lse_ref, vt_ref):
        n = pl.program_id(1)
        lo = _lane_lo()
        hi = jnp.logical_not(lo)
        eye = _eye()

        @pl.when(n == 0)
        def _():
            step = 2 * w
            for kv in range(2):
                vt_ref[kv * ext + HALF:(kv + 1) * ext, :] = jnp.ones((16, seq), BF16)
                for t in range(seq // step):
                    v_t = _dot_nt(eye, v_ref[t * step:(t + 1) * step, kv * HEAD_LANES:(kv + 1) * HEAD_LANES])
                    vt_ref[kv * ext:kv * ext + HALF, t * step:(t + 1) * step] = v_t[:HALF, :].astype(BF16)

        heads = range(N_HEADS)
        blocks = range(qb)
        geo = [_swa_block(n * qb + bi, pc_ref, pr_ref[bi]) for bi in blocks]
        wins = [pl.ds(g[0], 2 * w) for g in geo]
        kwins = [k_ref[win, :] for win in wins]
        vts = [vt_ref[:, win] for win in wins]
        sts = []
        for bi in blocks:
            q = q_ref[bi * w:(bi + 1) * w, :]
            sts.append([])
            for j in range(N_HEADS // 2):
                qp = q[:, j * HEAD_LANES:(j + 1) * HEAD_LANES]
                both = jnp.concatenate([jnp.where(lo, qp, jnp.zeros_like(qp)), jnp.where(hi, qp, jnp.zeros_like(qp))], axis=0)
                st = _dot_nt(kwins[bi][:, (j // 2) * HEAD_LANES:(j // 2 + 1) * HEAD_LANES], both)
                sts[bi] += [st[:, :w], st[:, w:]]
        ps, ms = [], []
        for bi in blocks:
            ps.append([])
            ms.append([])
            for h in heads:
                s = sts[bi][h] - _alibi(h) * geo[bi][1]
                m = jnp.maximum(jnp.max(s, axis=0, keepdims=True), sink_ref[0, h] * LOG2E)
                ps[bi].append(jnp.exp2(s - m).astype(BF16))
                ms[bi].append(m)
        for bi in blocks:
            ots = []
            for h in heads:
                pv = _dot(vts[bi][(h // 4) * ext:(h // 4 + 1) * ext, :], ps[bi][h])
                l = pv[HALF:HALF + 1, :] + jnp.exp2(sink_ref[0, h] * LOG2E - ms[bi][h])
                ots.append(pv[:HALF, :] * (1.0 / l))
                lse_ref[0, h:h + 1, bi * w:(bi + 1) * w] = ms[bi][h] + jnp.log2(l)
            o_ref[bi * w:(bi + 1) * w, :] = jnp.concatenate(ots, axis=0).T

    n_tok = qs.shape[0]
    tok = lambda width: pl.BlockSpec((qb * w, width), lambda b, n: (b * steps + n, 0))
    whole = lambda width: pl.BlockSpec((seq, width), lambda b, n: (b, 0))
    return pl.pallas_call(
        body, name="swa_fwd", grid=(n_seq, steps),
        out_shape=[jax.ShapeDtypeStruct((n_tok, 512), F32), jax.ShapeDtypeStruct((n_seq, N_HEADS, seq), F32)],
        in_specs=[tok(512), whole(256), whole(256), whole(1), pl.BlockSpec((qb, 1, w), lambda b, n: (b * steps + n, 0, 0)),
                  pl.BlockSpec(memory_space=pltpu.SMEM)],
        out_specs=[tok(512), pl.BlockSpec((1, N_HEADS, qb * w), lambda b, n: (b, 0, n))],
        scratch_shapes=[pltpu.VMEM((2 * ext, seq), BF16)],
        compiler_params=_params(2),
    )(qs, kd, vd, pos_col, pos_row, sinks)


def _swa_bwd_call(qs, kd, vd, do, delta, lse, pos_col, pos_row, sinks, g_out, n_seq, seq):
    w = SWA_WINDOW
    qb = SWA_BLOCKS
    steps = seq // (qb * w)
    reduced, reduce_scratch = _reduce_operands(g_out)

    def body(q_ref, k_ref, v_ref, do_ref, dl_ref, lse_ref, pc_ref, pr_ref, sink_ref, g_ref, dq_ref, dk_ref, dv_ref,
             dsink_ref, f_ref, kt_ref, *reduce_refs):
        b, n = pl.program_id(0), pl.program_id(1)
        _grad_reduce(b * steps + n, n_seq * steps, g_ref, f_ref, *reduce_refs)
        lo = _lane_lo()
        hi = jnp.logical_not(lo)
        sub_lo = lax.broadcasted_iota(jnp.int32, (HEAD_LANES, 1), 0) < HALF
        eye = _eye()

        @pl.when(n == 0)
        def _():
            dk_ref[...] = jnp.zeros_like(dk_ref)
            dv_ref[...] = jnp.zeros_like(dv_ref)
            _transpose_rows(eye, k_ref, kt_ref, seq, 2 * HEAD_LANES)

        @pl.when(jnp.logical_and(n == 0, b == 0))
        def _():
            dsink_ref[...] = jnp.zeros_like(dsink_ref)

        heads = range(N_HEADS)
        blocks = range(qb)
        kv_lanes = lambda h: slice((h // 4) * HEAD_LANES, (h // 4 + 1) * HEAD_LANES)
        geo = [_swa_block(n * qb + bi, pc_ref, pr_ref[bi]) for bi in blocks]
        wins = [pl.ds(g[0], 2 * w) for g in geo]
        kwins = [k_ref[win, :] for win in wins]
        vwins = [v_ref[win, :] for win in wins]

        do_ts, deltas, qms, doms = [], [], [], []
        for bi in blocks:
            rows = slice(bi * w, (bi + 1) * w)
            for lst in (do_ts, deltas, qms, doms):
                lst.append([])
            for j in range(N_HEADS // 2):
                pair = slice(j * HEAD_LANES, (j + 1) * HEAD_LANES)
                dop = do_ref[rows, pair]
                qp = q_ref[rows, pair]
                dt = _dot_nt(eye, dop)
                for hh in range(2):
                    half = lo if hh == 0 else hi
                    do_ts[bi].append(jnp.where(sub_lo, dt, 0.0).astype(BF16) if hh == 0
                                     else jnp.where(sub_lo, 0.0, dt).astype(BF16))
                    deltas[bi].append(dl_ref[2 * j + hh:2 * j + hh + 1, rows])
                    qms[bi].append(jnp.where(half, qp, jnp.zeros_like(qp)))
                    doms[bi].append(jnp.where(half, dop, jnp.zeros_like(dop)))
        sts, dpts = [], []
        for bi in blocks:
            sts.append([])
            dpts.append([])
            for j in range(N_HEADS // 2):
                a, b = 2 * j, 2 * j + 1
                st = _dot_nt(kwins[bi][:, kv_lanes(a)], jnp.concatenate([qms[bi][a], qms[bi][b]], axis=0))
                dpt = _dot(vwins[bi][:, kv_lanes(a)], jnp.concatenate([do_ts[bi][a], do_ts[bi][b]], axis=1))
                sts[bi] += [st[:, :w], st[:, w:]]
                dpts[bi] += [dpt[:, :w], dpt[:, w:]]
        pts, dsts = [], []
        for bi in blocks:
            pts.append([])
            dsts.append([])
            for h in heads:
                lse_h = lse_ref[0, h:h + 1, bi * w:(bi + 1) * w]
                pt = jnp.exp2(sts[bi][h] - _alibi(h) * geo[bi][1] - lse_h)
                dsts[bi].append((pt * (dpts[bi][h] - deltas[bi][h])).astype(BF16))
                pts[bi].append(pt.astype(BF16))
                dsink_ref[h:h + 1, :] += -jnp.exp2(sink_ref[0, h] * LOG2E - lse_h) * deltas[bi][h]
        for bi in blocks:
            for kv in range(2):
                group = range(4 * kv, 4 * kv + 4)
                dst_all = jnp.concatenate([dsts[bi][h] for h in group], axis=1)
                pt_all = jnp.concatenate([pts[bi][h] for h in group], axis=1)
                q_all = jnp.concatenate([qms[bi][h] for h in group], axis=0)
                do_all = jnp.concatenate([doms[bi][h] for h in group], axis=0)
                dk_ref[wins[bi], kv_lanes(4 * kv)] += _dot(dst_all, q_all)
                dv_ref[wins[bi], kv_lanes(4 * kv)] += _dot(pt_all, do_all)
        for bi in blocks:
            ktw = kt_ref[:, wins[bi]]
            for j in range(N_HEADS // 2):
                k_t = ktw[kv_lanes(2 * j), :]
                both = _dot(k_t, jnp.concatenate([dsts[bi][2 * j], dsts[bi][2 * j + 1]], axis=1))
                dq_t = jnp.where(sub_lo, both[:, :w], both[:, w:])
                dq_ref[bi * w:(bi + 1) * w, j * HEAD_LANES:(j + 1) * HEAD_LANES] = dq_t.T * SWA_SCALE

    n_tok = qs.shape[0]
    tok = lambda width: pl.BlockSpec((qb * w, width), lambda b, n: (b * steps + n, 0))
    whole = lambda width: pl.BlockSpec((seq, width), lambda b, n: (b, 0))
    return pl.pallas_call(
        body, name="swa_bwd", grid=(n_seq, steps),
        out_shape=[jax.ShapeDtypeStruct((n_tok, 512), F32), jax.ShapeDtypeStruct((n_tok, 256), F32),
                   jax.ShapeDtypeStruct((n_tok, 256), F32), jax.ShapeDtypeStruct((N_HEADS, HEAD_LANES), F32), reduced],
        in_specs=[tok(512), whole(256), whole(256), pl.BlockSpec((qb * w, 512), lambda b, n: (b * steps + n, 1)),
                  pl.BlockSpec((N_HEADS, qb * w), lambda b, n: (0, b * steps + n)),
                  pl.BlockSpec((1, N_HEADS, qb * w), lambda b, n: (b, 0, n)),
                  whole(1), pl.BlockSpec((qb, 1, w), lambda b, n: (b * steps + n, 0, 0)),
                  pl.BlockSpec(memory_space=pltpu.SMEM), ANY_SPEC],
        out_specs=[tok(512), whole(256), whole(256), _full((N_HEADS, HEAD_LANES)), ANY_SPEC],
        scratch_shapes=[pltpu.VMEM((2 * HEAD_LANES, seq), BF16)] + reduce_scratch,
        compiler_params=_params(2),
    )(qs, kd, vd, do, delta, lse, pos_col, pos_row, sinks, g_out)


def _post_call(x, target, o_mla, o_swa, gates, mod, b_ada, fg, w_out, seq):
    n_tok = x.shape[0]
    tm = min(TOKEN_TILE, seq)
    per_seq = seq // tm
    n_seq = n_tok // seq

    def body(x_ref, t_ref, om_ref, os_ref, g_ref, mod_ref, bada_ref, fg_ref, w_ref,
             dx2_ref, do_ref, dg_ref, gw_ref, gfg_ref, dgate_ref, loss_ref, dmla_ref, dswa_ref):
        i = pl.program_id(0)

        @pl.when(i == 0)
        def _():
            gw_ref[...] = jnp.zeros_like(gw_ref)
            gfg_ref[...] = jnp.zeros_like(gfg_ref)
            loss_ref[...] = jnp.zeros_like(loss_ref)

        @pl.when(i % per_seq == 0)
        def _():
            dgate_ref[...] = jnp.zeros_like(dgate_ref)

        gate = mod_ref[0][:, 2 * D_MODEL:] + bada_ref[:, 2 * D_MODEL:]
        fgv = fg_ref[...]
        fgd = fgv * (1.0 / D_MODEL)
        subs = _sub_tiles(tm)
        gs = [g_ref[r, :] for r in subs]
        os_ = [jnp.concatenate([om_ref[r, :], os_ref[r, :]], axis=-1) for r in subs]
        sgs = [_sigmoid(g) for g in gs]
        sils = [g * sg for g, sg in zip(gs, sgs)]
        ypres = [(o * sil).astype(BF16) for o, sil in zip(os_, sils)]
        ys = [_dot(ypre, w_ref[...]) for ypre in ypres]
        dys, loss, gfg, dgate = [], 0.0, 0.0, 0.0
        for r, y in zip(subs, ys):
            x2 = x_ref[r, :] + gate * y
            r2 = lax.rsqrt(jnp.mean(x2 * x2, axis=-1, keepdims=True) + EPS)
            xn2 = x2 * r2
            err = xn2 * fgv - t_ref[r, :]
            loss = loss + jnp.sum(jnp.sum(err * err, axis=-1, keepdims=True), axis=0, keepdims=True)
            gfg = gfg + jnp.sum(err * xn2, axis=0, keepdims=True)
            dxn2 = err * fgd
            dx2 = r2 * (dxn2 - xn2 * jnp.mean(dxn2 * xn2, axis=-1, keepdims=True))
            dx2_ref[r, :] = dx2
            dgate = dgate + jnp.sum(dx2 * y, axis=0, keepdims=True)
            dys.append((dx2 * gate).astype(BF16))
        loss_ref[...] += jnp.broadcast_to(loss * (0.5 / D_MODEL), loss_ref.shape)
        gfg_ref[...] += gfg * (1.0 / D_MODEL)
        dgate_ref[0] += dgate
        gw_ref[...] += _dot_tn(jnp.concatenate(ypres, axis=0), jnp.concatenate(dys, axis=0))
        dypres = [_dot_nt(dy, w_ref[...]) for dy in dys]
        pick = jnp.where(jnp.right_shift(lax.broadcasted_iota(jnp.int32, (2 * N_HEADS, D_MODEL), 1), 6)
                         == lax.broadcasted_iota(jnp.int32, (2 * N_HEADS, D_MODEL), 0), 1.0, 0.0).astype(BF16)
        for r, dypre, o, g, sg, sil in zip(subs, dypres, os_, gs, sgs, sils):
            dov = (dypre * sil).astype(BF16)
            do_ref[r, :] = dov
            delta = _dot_nt(pick, (dov.astype(F32) * o).astype(BF16))
            for grp in range(2):
                dmla_ref[grp, :, r] = delta[4 * grp:4 * grp + 4, :]
            dswa_ref[:, r] = delta[N_HEADS:, :]
            dg_ref[r, :] = (dypre * o * (sg + sil * (1.0 - sg))).astype(BF16)

    tok = lambda w: pl.BlockSpec((tm, w), lambda i: (i, 0))
    per_b = pl.BlockSpec((1, 1, 3 * D_MODEL), lambda i: (i // per_seq, 0, 0))
    return pl.pallas_call(
        body, name="post", grid=(n_tok // tm,),
        out_shape=[jax.ShapeDtypeStruct((n_tok, D_MODEL), F32), jax.ShapeDtypeStruct((n_tok, D_MODEL), BF16),
                   jax.ShapeDtypeStruct((n_tok, D_MODEL), BF16), jax.ShapeDtypeStruct((D_MODEL, D_MODEL), F32),
                   jax.ShapeDtypeStruct((1, D_MODEL), F32), jax.ShapeDtypeStruct((n_seq, 1, D_MODEL), F32),
                   jax.ShapeDtypeStruct((1, HEAD_LANES), F32),
                   jax.ShapeDtypeStruct((2, N_HEADS // 2, n_tok), F32), jax.ShapeDtypeStruct((N_HEADS, n_tok), F32)],
        in_specs=[tok(D_MODEL), tok(D_MODEL), tok(512), tok(512), tok(D_MODEL), per_b, _full(b_ada.shape),
                  _full(fg.shape), _full(w_out.shape)],
        out_specs=[tok(D_MODEL), tok(D_MODEL), tok(D_MODEL), _full((D_MODEL, D_MODEL)), _full((1, D_MODEL)),
                   pl.BlockSpec((1, 1, D_MODEL), lambda i: (i // per_seq, 0, 0)), _full((1, HEAD_LANES)),
                   pl.BlockSpec((2, N_HEADS // 2, tm), lambda i: (0, 0, i)), pl.BlockSpec((N_HEADS, tm), lambda i: (0, i))],
        compiler_params=_params(1),
    )(x, target, o_mla, o_swa, gates, mod, b_ada, fg, w_out)


def _mid_bwd_call(dqf, dkf, dv, zqkv, rope, qg, kvg, wq2, wkv, seq):
    n_tok = dqf.shape[0]
    tm = min(TOKEN_TILE, seq)

    def body(dq_ref, dk_ref, dv_ref, z_ref, rope_ref, qg_ref, kvg_ref, wq_ref, wkv_ref,
             dz_ref, gwq_ref, gwkv_ref, gqg_ref, gkvg_ref):
        i = pl.program_id(0)

        @pl.when(i == 0)
        def _():
            gwq_ref[...] = jnp.zeros_like(gwq_ref)
            gwkv_ref[...] = jnp.zeros_like(gwkv_ref)
            gqg_ref[...] = jnp.zeros_like(gqg_ref)
            gkvg_ref[...] = jnp.zeros_like(gkvg_ref)

        cos, sin = rope_ref[:, :HEAD_LANES], rope_ref[:, HEAD_LANES:]
        cf, sf = jnp.tile(cos, (1, N_HEADS)), jnp.tile(sin, (1, N_HEADS))
        dq = dq_ref[...] * MLA_SCALE
        dqr = jnp.concatenate([dq * cf, dq * sf], axis=-1).astype(BF16)
        zq, zkv = z_ref[:, :Q_LORA], z_ref[:, Q_LORA:]
        qgv, kvgv = qg_ref[...], kvg_ref[...]

        rq = lax.rsqrt(jnp.mean(zq * zq, axis=-1, keepdims=True) + EPS)
        xq = zq * rq
        gwq_ref[...] += _dot_tn((xq * qgv).astype(BF16), dqr)
        dqn = _dot_nt(dqr, wq_ref[...])
        gqg_ref[...] += jnp.sum(dqn * xq, axis=0, keepdims=True)
        dxq = dqn * qgv
        dz_ref[:, :Q_LORA] = (rq * (dxq - xq * jnp.mean(dxq * xq, axis=-1, keepdims=True))).astype(BF16)

        dk = dk_ref[...] * LN2
        dkv = jnp.concatenate([dk, dv_ref[...]], axis=-1).astype(BF16)
        rkv = lax.rsqrt(jnp.mean(zkv * zkv, axis=-1, keepdims=True) + EPS)
        xkv = zkv * rkv
        gwkv_ref[...] += _dot_tn((xkv * kvgv).astype(BF16), dkv)
        dkvn = _dot_nt(dkv, wkv_ref[...])
        gkvg_ref[...] += jnp.sum(dkvn * xkv, axis=0, keepdims=True)
        dxkv = dkvn * kvgv
        dz_ref[:, Q_LORA:A_KR] = (rkv * (dxkv - xkv * jnp.mean(dxkv * xkv, axis=-1, keepdims=True))).astype(BF16)

        dkpe = dk[:, :HEAD_LANES]
        for h in range(1, N_HEADS):
            dkpe = dkpe + dk[:, h * HEAD_LANES:(h + 1) * HEAD_LANES]
        dz_ref[:, A_KR:] = (jnp.where(_lane_lo(), 0.0, dkpe * cos) + pltpu.roll(dkpe * sin, HALF, 1)).astype(BF16)

    tok = lambda w: pl.BlockSpec((tm, w), lambda i: (i, 0))
    return pl.pallas_call(
        body, name="mid_bwd", grid=(n_tok // tm,),
        out_shape=[jax.ShapeDtypeStruct((n_tok, A_GM), BF16),
                   jax.ShapeDtypeStruct(wq2.shape, F32), jax.ShapeDtypeStruct(wkv.shape, F32),
                   jax.ShapeDtypeStruct((1, Q_LORA), F32), jax.ShapeDtypeStruct((1, KV_LORA), F32)],
        in_specs=[tok(1024), tok(1024), tok(512), tok(640), tok(2 * HEAD_LANES), _full(qg.shape), _full(kvg.shape),
                  _full(wq2.shape), _full(wkv.shape)],
        out_specs=[tok(A_GM), _full(wq2.shape), _full(wkv.shape), _full((1, Q_LORA)), _full((1, KV_LORA))],
        compiler_params=_params(1),
    )(dqf, dkf, dv, zqkv, rope, qg, kvg, wq2, wkv)


def _in_bwd_call(x, dx2, dz, dg, dqs, dkd, dvd, mod, b_ada, ng, wa, seq):
    n_tok = x.shape[0]
    tm = min(TOKEN_TILE, seq)
    per_seq = seq // tm
    n_seq = n_tok // seq

    def body(x_ref, dx2_ref, dz_ref, dg_ref, dqs_ref, dkd_ref, dvd_ref, mod_ref, bada_ref, ng_ref,
             wa_ref, gx_ref, gwa_ref, gng_ref, dshift_ref, dscale_ref):
        i = pl.program_id(0)

        @pl.when(i == 0)
        def _():
            gwa_ref[...] = jnp.zeros_like(gwa_ref)
            gng_ref[...] = jnp.zeros_like(gng_ref)

        @pl.when(i % per_seq == 0)
        def _():
            dshift_ref[...] = jnp.zeros_like(dshift_ref)
            dscale_ref[...] = jnp.zeros_like(dscale_ref)

        xv = x_ref[...]
        modv = mod_ref[0] + bada_ref[...]
        shift, scale = modv[:, :D_MODEL], modv[:, D_MODEL:2 * D_MODEL]
        ngv = ng_ref[...]
        r1 = lax.rsqrt(jnp.mean(xv * xv, axis=-1, keepdims=True) + EPS)
        xn = xv * r1
        hb = ((xn * ngv) * (1.0 + scale) + shift).astype(BF16)

        dgv = dg_ref[...]
        pieces = [(A_ZQ, dz_ref[...]), (A_GM, dgv[:, :512]), (A_QS, dqs_ref[...].astype(BF16)),
                  (A_KS, jnp.concatenate([_once(dkd_ref[...]) * LN2, _once(dvd_ref[...])], axis=1).astype(BF16)),
                  (A_GS, dgv[:, 512:])]
        dh = None
        for off, piece in pieces:
            wd = piece.shape[1]
            gwa_ref[:, off:off + wd] += _dot_tn(hb, piece)
            term = _dot_nt(piece, wa_ref[:, off:off + wd])
            dh = term if dh is None else dh + term

        dshift_ref[0] += jnp.sum(dh, axis=0, keepdims=True)
        dscale_ref[0] += jnp.sum(dh * (xn * ngv), axis=0, keepdims=True)
        gng_ref[...] += jnp.sum(dh * xn * (1.0 + scale), axis=0, keepdims=True)
        dxn = dh * ngv * (1.0 + scale)
        gx_ref[...] = dx2_ref[...] + r1 * (dxn - xn * jnp.mean(dxn * xn, axis=-1, keepdims=True))

    tok = lambda w: pl.BlockSpec((tm, w), lambda i: (i, 0))
    per_b = lambda w: pl.BlockSpec((1, 1, w), lambda i: (i // per_seq, 0, 0))
    return pl.pallas_call(
        body, name="in_bwd", grid=(n_tok // tm,),
        out_shape=[jax.ShapeDtypeStruct((n_tok, D_MODEL), F32), jax.ShapeDtypeStruct((D_MODEL, A_END), F32),
                   jax.ShapeDtypeStruct((1, D_MODEL), F32),
                   jax.ShapeDtypeStruct((n_seq, 1, D_MODEL), F32), jax.ShapeDtypeStruct((n_seq, 1, D_MODEL), F32)],
        in_specs=[tok(D_MODEL), tok(D_MODEL), tok(A_GM), tok(D_MODEL), tok(512), tok(256), tok(256),
                  per_b(3 * D_MODEL), _full(b_ada.shape), _full(ng.shape), _full(wa.shape)],
        out_specs=[tok(D_MODEL), _full((D_MODEL, A_END)), _full((1, D_MODEL)), per_b(D_MODEL), per_b(D_MODEL)],
        compiler_params=_params(1),
    )(x, dx2, dz, dg, dqs, dkd, dvd, mod, b_ada, ng, wa)


def _adam_math(w, g, m, v):
    m_new = ADAM_B1 * m + (1.0 - ADAM_B1) * g
    v_new = ADAM_B2 * v + (1.0 - ADAM_B2) * (g * g)
    m_hat = m_new / (1.0 - ADAM_B1 ** ADAM_STEP)
    v_hat = v_new / (1.0 - ADAM_B2 ** ADAM_STEP)
    delta = -ADAM_LR * (m_hat / (jnp.sqrt(v_hat) + ADAM_EPS) + ADAM_WD * w)
    return delta, m_new, v_new


def _adam_call(name, w, g, m, v):
    rows, cols = w.shape
    tr = next((t for t in (256, 128, 88) if rows % t == 0), rows)

    def body(w_ref, g_ref, m_ref, v_ref, d_ref, mo_ref, vo_ref):
        d, mn, vn = _adam_math(w_ref[...], g_ref[...], m_ref[...], v_ref[...])
        d_ref[...] = d
        mo_ref[...] = mn
        vo_ref[...] = vn

    spec = pl.BlockSpec((tr, cols), lambda i: (i, 0))
    return pl.pallas_call(
        body, name=name, grid=(rows // tr,),
        out_shape=[jax.ShapeDtypeStruct(w.shape, F32)] * 3,
        in_specs=[spec] * 4, out_specs=[spec] * 3,
        compiler_params=_params(1),
    )(w, g, m, v)


def _ada_bwd_call(act_all, dmod_cols, w, m, v):
    rows, cols = w.shape
    tr = 256

    def body(a_ref, dm_ref, w_ref, m_ref, v_ref, g_ref, d_ref, mo_ref, vo_ref):
        g = _dot_tn(a_ref[...].astype(BF16), dm_ref[...].astype(BF16))
        d, mn, vn = _adam_math(w_ref[...], g, m_ref[...], v_ref[...])
        g_ref[...] = g
        d_ref[...] = d
        mo_ref[...] = mn
        vo_ref[...] = vn

    spec = pl.BlockSpec((tr, cols), lambda i: (i, 0))
    nb = act_all.shape[0]
    return pl.pallas_call(
        body, name="ada_bwd", grid=(rows // tr,),
        out_shape=[jax.ShapeDtypeStruct(w.shape, F32)] * 4,
        in_specs=[pl.BlockSpec((nb, tr), lambda i: (0, i)), _full(dmod_cols.shape), spec, spec, spec],
        out_specs=[spec] * 4,
        compiler_params=_params(1),
    )(act_all, dmod_cols, w, m, v)


SMALL_ROW = {"norm_gain": (0, 1024), "final_gain": (1024, 2048), "q_norm_gain": (2048, 2432),
             "kv_norm_gain": (2432, 2688), "swa_sinks": (2688, 2696), "loss": (2816, 2944)}
SMALL_ORDER = ("b_ada", "norm_gain", "q_norm_gain", "kv_norm_gain", "swa_sinks", "final_gain")


def _small_call(parts_all, n_seq, params):
    k = len(params)

    def body(p_ref, *refs):
        ins, outs, loss_ref = refs[:3 * k], refs[3 * k:7 * k], refs[7 * k]
        row = p_ref[n_seq:n_seq + 1, :]
        for dv in range(1, 8):
            r0 = dv * ROWS_PER_DEVICE + n_seq
            row = row + p_ref[r0:r0 + 1, :]
        gb = None
        for dv in range(8):
            for r in range(n_seq):
                r0 = dv * ROWS_PER_DEVICE + r
                gb = p_ref[r0:r0 + 1, :] if gb is None else gb + p_ref[r0:r0 + 1, :]
        for j, name in enumerate(SMALL_ORDER):
            g = gb if name == "b_ada" else row[:, SMALL_ROW[name][0]:SMALL_ROW[name][1]]
            d, mn, vn = _adam_math(ins[3 * j][...], g, ins[3 * j + 1][...], ins[3 * j + 2][...])
            outs[4 * j][...] = g
            outs[4 * j + 1][...] = d
            outs[4 * j + 2][...] = mn
            outs[4 * j + 3][...] = vn
        loss_ref[...] = row[:, SMALL_ROW["loss"][0]:SMALL_ROW["loss"][1]]

    flat = [t for p in params for t in p]
    res = pl.pallas_call(
        body, name="small_update", grid=(1,),
        out_shape=[jax.ShapeDtypeStruct(p[0].shape, F32) for p in params for _ in range(4)]
        + [jax.ShapeDtypeStruct((1, HEAD_LANES), F32)],
        in_specs=[_full(parts_all.shape)] + [_full(t.shape) for t in flat],
        out_specs=[_full(p[0].shape) for p in params for _ in range(4)] + [_full((1, HEAD_LANES))],
        compiler_params=_params(1),
    )(parts_all, *flat)
    return [res[4 * j:4 * j + 4] for j in range(k)], res[4 * k]


def _rot(t):
    half = t.shape[-1] // 2
    return jnp.concatenate([-t[..., half:], t[..., :half]], axis=-1)


def _rot_t(g):
    half = g.shape[-1] // 2
    return jnp.concatenate([g[..., half:], -g[..., :half]], axis=-1)


def _columns(segments, lo, hi):
    out, at = [], 0
    for seg in segments:
        n = seg.shape[1]
        a, b = max(lo, at), min(hi, at + n)
        if a < b:
            out.append(seg[:, a - at:b - at])
        at += n
    return out


def _prepare_weights(w_in_blocks, w_uq, w_ukv):
    o = [0]
    for s in IN_SPLITS:
        o.append(o[-1] + s)
    part = lambda a, b: _columns(w_in_blocks, a, b)
    kr = jnp.concatenate(part(o[2], o[3]), axis=1)
    zero = jnp.zeros((kr.shape[0], 32), kr.dtype)
    wa = jnp.concatenate(part(0, o[2]) + [_rot(kr), zero, kr, zero] + part(o[3], o[8]), axis=1)
    uq = w_uq.reshape(Q_LORA, N_HEADS, MLA_NOPE + MLA_ROPE)
    zq = jnp.zeros((Q_LORA, N_HEADS, 32), w_uq.dtype)
    uq_full = jnp.concatenate([uq, zq], axis=-1).reshape(Q_LORA, 1024)
    uq_rot = jnp.concatenate([jnp.zeros((Q_LORA, N_HEADS, 64), w_uq.dtype), _rot(uq[..., MLA_NOPE:]), zq],
                             axis=-1).reshape(Q_LORA, 1024)
    wq2 = jnp.concatenate([uq_full, uq_rot], axis=1)
    ukv = w_ukv.reshape(KV_LORA, N_HEADS, 128)
    k_full = jnp.concatenate([ukv[..., :64], jnp.zeros((KV_LORA, N_HEADS, 64), w_ukv.dtype)], axis=-1).reshape(KV_LORA, 1024)
    wkv = jnp.concatenate([k_full, ukv[..., 64:].reshape(KV_LORA, 512)], axis=1)
    return wa, wq2, wkv


def _restore_in(gwa):
    gkr = gwa[:, A_KR + 64:A_KR + 96] + _rot_t(gwa[:, A_KR:A_KR + 32])
    in_order = [gwa[:, :A_KR], gkr, gwa[:, A_GM:]]
    n = D_IN // 4
    return [jnp.concatenate(_columns(in_order, k * n, (k + 1) * n), axis=1) for k in range(4)]


def _restore_up(gwq2, gwkv):
    gf = gwq2[:, :1024].reshape(Q_LORA, N_HEADS, 128)
    gr = gwq2[:, 1024:].reshape(Q_LORA, N_HEADS, 128)
    g_uq = jnp.concatenate([gf[..., :64], gf[..., 64:96] + _rot_t(gr[..., 64:96])], axis=-1).reshape(Q_LORA, 768)
    gk = gwkv[:, :1024].reshape(KV_LORA, N_HEADS, 128)[..., :64]
    gv = gwkv[:, 1024:].reshape(KV_LORA, N_HEADS, 64)
    g_ukv = jnp.concatenate([gk, gv], axis=-1).reshape(KV_LORA, 1024)
    return g_uq, g_ukv


def _local_step(x, positions, target, mod_rows, b_ada, ng, qg, kvg, sinks, fg, w_in_b, w_uq_b, w_ukv_b, w_out_half):
    n_seq, seq, _ = x.shape
    n_tok = n_seq * seq
    x2d = x.reshape(n_tok, D_MODEL)
    t2d = target.reshape(n_tok, D_MODEL)
    pos_f = positions.astype(F32)
    pos_col = pos_f.reshape(n_tok, 1)
    pos_row = pos_f.reshape(n_tok // SWA_WINDOW, 1, SWA_WINDOW)
    mod3 = mod_rows.reshape(n_seq, 1, 3 * D_MODEL)
    inv = ROPE_THETA ** (-jnp.arange(0, MLA_ROPE, 2, dtype=F32) / MLA_ROPE)
    inv128 = jnp.concatenate([jnp.zeros((64,), F32), inv, inv, jnp.zeros((32,), F32)]).reshape(1, 128)
    fg2 = fg.reshape(1, D_MODEL)

    wa, wq2, wkv = _prepare_weights(w_in_b, w_uq_b, w_ukv_b)

    zqkv, gates, qf, kf, v, qs, kd, vd, rope, f_out = _pre_call(x2d, pos_col, mod3, b_ada, ng, qg, kvg, inv128, wa, wq2,
                                                                 wkv, w_out_half, seq)
    w_out_b = f_out.reshape(D_MODEL, D_MODEL)
    o_mla, lse_mla = _mla_fwd_call(qf, kf, v, n_seq, seq)
    o_swa, lse_swa = _swa_fwd_call(qs, kd, vd, pos_col, pos_row, sinks, n_seq, seq)
    dx2, do, dg, g_out, g_fg, dgate, loss, delta_mla, delta_swa = _post_call(x2d, t2d, o_mla, o_swa, gates, mod3, b_ada, fg2, w_out_b, seq)
    dqf, dkf, dv = _mla_bwd_call(qf, kf, v, do, delta_mla, lse_mla, n_seq, seq)
    dqs, dkd, dvd, dsink, r_out = _swa_bwd_call(qs, kd, vd, do, delta_swa, lse_swa, pos_col, pos_row, sinks,
                                                g_out.reshape(4, 2, D_MODEL // 8, D_MODEL), n_seq, seq)
    dz, g_wq2, g_wkv, g_qg, g_kvg = _mid_bwd_call(dqf, dkf, dv, zqkv, rope, qg, kvg, wq2, wkv, seq)
    gx, g_wa, g_ng, dshift, dscale = _in_bwd_call(x2d, dx2, dz, dg, dqs, dkd, dvd, mod3, b_ada, ng, wa, seq)
    g_in = _restore_in(g_wa)
    g_uq, g_ukv = _restore_up(g_wq2, g_wkv)
    dmod = jnp.concatenate([dshift, dscale, dgate], axis=-1).reshape(n_seq, 3 * D_MODEL)
    small_row = jnp.concatenate([g_ng, g_fg, g_qg, g_kvg, jnp.pad(jnp.sum(dsink, axis=1).reshape(1, N_HEADS), ((0, 0), (0, 120))),
                                 loss, jnp.zeros((1, 128), F32)], axis=1)
    return gx.reshape(x.shape), (g_in, g_uq, g_ukv), r_out, small_row, dmod


def kernel(x, c, positions, w_ada, b_ada, norm_gain, w_in, q_norm_gain, kv_norm_gain, w_uq, w_ukv, swa_sinks, w_out, final_gain, loss_target, m_w_ada, m_b_ada, m_norm_gain, m_w_in, m_q_norm_gain, m_kv_norm_gain, m_w_uq, m_w_ukv, m_swa_sinks, m_w_out, m_final_gain, v_w_ada, v_b_ada, v_norm_gain, v_w_in, v_q_norm_gain, v_kv_norm_gain, v_w_uq, v_w_ukv, v_swa_sinks, v_w_out, v_final_gain):
    n_seq = x.shape[0]
    xi, yi, ci = lax.axis_index("x"), lax.axis_index("y"), lax.axis_index("c")
    dev = 4 * xi + 2 * yi + ci
    chip = 2 * xi + yi

    halves = lambda w: w.astype(BF16).reshape(2, w.shape[0] // 2, w.shape[1])
    c_blk = jnp.pad(c, ((0, ROWS_PER_DEVICE - n_seq), (0, 0)))
    act_all, pieces, f_in, f_uq, f_ukv = _comm_fwd_call(c_blk, w_ada[0], [halves(w_in[0]), halves(w_uq[0]), halves(w_ukv[0])])
    mine = lax.dynamic_slice_in_dim(pieces, dev * ROWS_PER_DEVICE, n_seq, axis=1)
    mod_rows = jnp.transpose(mine, (1, 0, 2)).reshape(n_seq, 3 * D_MODEL)
    cols = lambda t, r: jnp.transpose(t.reshape(4, r, -1), (1, 0, 2)).reshape(r, -1)
    w_in_blocks = [f_in[k].reshape(D_MODEL, -1) for k in range(4)]
    w_uq_b, w_ukv_b = cols(f_uq, Q_LORA), cols(f_ukv, KV_LORA)

    gx, (g_in_blocks, g_uq, g_ukv), r_out, small_row, dmod = _local_step(
        x, positions, loss_target, mod_rows, b_ada, norm_gain, q_norm_gain, kv_norm_gain, swa_sinks, final_gain,
        w_in_blocks, w_uq_b, w_ukv_b, halves(w_out[0]))

    grads = [jnp.stack(g_in_blocks).reshape(4, 2, D_MODEL // 2, -1), _by_owner(g_uq, g_uq.shape[1] // 4),
             _by_owner(g_ukv, g_ukv.shape[1] // 4)]
    part = jnp.concatenate([dmod, small_row, jnp.zeros((ROWS_PER_DEVICE - n_seq - 1, 3 * D_MODEL), F32)], axis=0)
    r_in, r_uq, r_ukv, parts_all = _comm_bwd_call(grads, part)
    g_in_s, g_uq_s = r_in.reshape(w_in.shape[1:]), r_uq.reshape(w_uq.shape[1:])
    g_ukv_s, g_out_s = r_ukv.reshape(w_ukv.shape[1:]), r_out.reshape(w_out.shape[1:])

    tr = lambda a: jnp.swapaxes(a[0], 0, 1)
    back = lambda ts: [jnp.swapaxes(t, 0, 1) for t in ts]
    d_in, nm_in, nv_in = back(_adam_call("adam_w_in", tr(w_in), g_in_s.T, tr(m_w_in), tr(v_w_in)))
    d_uq, nm_uq, nv_uq = back(_adam_call("adam_w_uq", tr(w_uq), g_uq_s.T, tr(m_w_uq), tr(v_w_uq)))
    d_ukv, nm_ukv, nv_ukv = _adam_call("adam_w_ukv", w_ukv[0], g_ukv_s, m_w_ukv[0], v_w_ukv[0])
    d_out, nm_out, nv_out = _adam_call("adam_w_out", w_out[0], g_out_s, m_w_out[0], v_w_out[0])
    dmod_cols = lax.dynamic_slice_in_dim(parts_all, chip * 768, 768, axis=1)
    g_ada, d_ada, nm_ada, nv_ada = _ada_bwd_call(act_all, dmod_cols, w_ada[0], m_w_ada[0], v_w_ada[0])

    row = lambda t: t.reshape(1, -1)
    small = {"b_ada": (b_ada, m_b_ada, v_b_ada), "norm_gain": (norm_gain, m_norm_gain, v_norm_gain),
             "q_norm_gain": (q_norm_gain, m_q_norm_gain, v_q_norm_gain),
             "kv_norm_gain": (kv_norm_gain, m_kv_norm_gain, v_kv_norm_gain),
             "swa_sinks": (swa_sinks, m_swa_sinks, v_swa_sinks),
             "final_gain": (row(final_gain), row(m_final_gain), row(v_final_gain))}
    res, loss_row = _small_call(parts_all, n_seq, [small[name] for name in SMALL_ORDER])
    res = dict(zip(SMALL_ORDER, res))
    res["final_gain"] = [t.reshape(-1) for t in res["final_gain"]]
    e = lambda t: t[None]
    big = {"w_ada": (e(g_ada), e(d_ada), e(nm_ada), e(nv_ada)), "w_in": (e(g_in_s), e(d_in), e(nm_in), e(nv_in)),
           "w_uq": (e(g_uq_s), e(d_uq), e(nm_uq), e(nv_uq)), "w_ukv": (e(g_ukv_s), e(d_ukv), e(nm_ukv), e(nv_ukv)),
           "w_out": (e(g_out_s), e(d_out), e(nm_out), e(nv_out))}
    order = ("w_ada", "b_ada", "norm_gain", "w_in", "q_norm_gain", "kv_norm_gain", "w_uq", "w_ukv", "swa_sinks", "w_out",
             "final_gain")
    pick = lambda kind: [(big[n] if n in big else res[n])[kind] for n in order]
    return (loss_row[0, 0], gx, *pick(0), *pick(1), *pick(2), *pick(3))
```

```python
import jax
import jax.numpy as jnp
from jax import lax
from jax.experimental import pallas as pl
from jax.experimental.pallas import tpu as pltpu

F32 = jnp.float32
BF16 = jnp.bfloat16

D_MODEL = 1024
Q_LORA = 384
KV_LORA = 256
N_HEADS = 8
MLA_NOPE = 64
MLA_ROPE = 32
HEAD_LANES = 128
HALF = 64
SWA_WINDOW = 128
EPS = 1e-6
ROPE_THETA = 10000.0
MLA_SCALE = (MLA_NOPE + MLA_ROPE) ** -0.5
LOG2E = 1.4426950408889634
LN2 = 0.6931471805599453
SWA_SCALE = 64 ** -0.5
NEG = -1e30

ADAM_LR = 0.001
ADAM_B1 = 0.9
ADAM_B2 = 0.999
ADAM_EPS = 1e-08
ADAM_WD = 0.01
ADAM_STEP = 10

A_ZQ, A_ZKV, A_KR, A_GM, A_QS, A_KS, A_VS, A_GS, A_END = 0, 384, 640, 768, 1280, 1792, 1920, 2048, 2560
IN_SPLITS = (384, 256, 32, 512, 512, 128, 128, 512)
D_IN = sum(IN_SPLITS)

TOKEN_TILE = 512
ATT_TILE = 256
VMEM_LIMIT = 56 * 1024 * 1024


def _dot(a, b):
    return jnp.dot(a, b, preferred_element_type=F32)


def _dot_nt(a, b):
    return lax.dot_general(a, b, (((1,), (1,)), ((), ())), preferred_element_type=F32)


def _dot_tn(a, b):
    return lax.dot_general(a, b, (((0,), (0,)), ((), ())), preferred_element_type=F32)


def _params(n_grid):
    return pltpu.CompilerParams(dimension_semantics=("arbitrary",) * n_grid, vmem_limit_bytes=VMEM_LIMIT)


def _full(shape):
    nd = len(shape)
    return pl.BlockSpec(shape, lambda *_: (0,) * nd, pipeline_mode=pl.Buffered(1))


def _sigmoid(g):
    return 1.0 / (1.0 + jnp.exp(-g))


SUB_TILE = 256


def _sub_tiles(tm):
    sub = min(SUB_TILE, tm)
    return [slice(s * sub, (s + 1) * sub) for s in range(tm // sub)]


MESH = pl.DeviceIdType.MESH
ROWS_PER_DEVICE = 8
VMEM_SPEC = pl.BlockSpec(memory_space=pltpu.VMEM)
ANY_SPEC = pl.BlockSpec(memory_space=pl.ANY)


def _position():
    x, y, c = lax.axis_index("x"), lax.axis_index("y"), lax.axis_index("c")
    sibling = (x, y, 1 - c)
    others = [(1 - x, y, c), (x, 1 - y, c), (1 - x, 1 - y, c)]
    return (x, y, c), 4 * x + 2 * y + c, 2 * x + y, sibling, others


def _rows_of(dev):
    return pl.ds(pl.multiple_of(dev * ROWS_PER_DEVICE, ROWS_PER_DEVICE), ROWS_PER_DEVICE)


def _all_to_all_rows(block_ref, table_ref, dev, me, send_sems, recv_sems):
    x, y, c = me
    waits = []
    for k in range(1, 8):
        peer = (1 - x if k & 4 else x, 1 - y if k & 2 else y, 1 - c if k & 1 else c)
        pltpu.make_async_remote_copy(src_ref=block_ref, dst_ref=table_ref.at[_rows_of(dev)], send_sem=send_sems.at[k - 1],
                                     recv_sem=recv_sems.at[k - 1], device_id=peer, device_id_type=MESH).start()
        waits.append(pltpu.make_async_remote_copy(
            src_ref=block_ref, dst_ref=table_ref.at[_rows_of(jnp.bitwise_xor(dev, k))], send_sem=send_sems.at[k - 1],
            recv_sem=recv_sems.at[k - 1], device_id=peer, device_id_type=MESH))
    return waits


def _comm_fwd_call(c_blk, w_ada, shards):
    n = len(shards)

    def body(c_ref, wada_ref, *refs):
        w_refs, act_ref, pieces_ref, full_refs = refs[:n], refs[n], refs[n + 1], refs[n + 2:2 * n + 2]
        c_all_ref = refs[2 * n + 2]
        c_send, c_recv, p_send, p_recv, w_send, w_recv, f_send, f_recv, loc_sem = refs[2 * n + 3:]
        me, dev, chip, sibling, others = _position()
        core = me[2]
        chip_of = [2 * p[0] + p[1] for p in others]

        local = [pltpu.make_async_copy(w_refs[i], full_refs[i].at[chip], loc_sem.at[i]) for i in range(n)]
        for cp in local:
            cp.start()

        def over_ici(i, j, src_chip):
            return pltpu.make_async_remote_copy(
                src_ref=w_refs[i].at[core], dst_ref=full_refs[i].at[src_chip, core], send_sem=w_send.at[3 * i + j],
                recv_sem=w_recv.at[3 * i + j], device_id=others[j], device_id_type=MESH)

        def to_sibling(i, j, half):
            return pltpu.make_async_remote_copy(
                src_ref=full_refs[i].at[chip_of[j], half], dst_ref=full_refs[i].at[chip_of[j], half],
                send_sem=f_send.at[3 * i + j], recv_sem=f_recv.at[3 * i + j], device_id=sibling, device_id_type=MESH)

        c_all_ref[_rows_of(dev), :] = c_ref[...]
        c_waits = _all_to_all_rows(c_ref, c_all_ref, dev, me, c_send, c_recv)
        sent = [over_ici(i, j, chip) for i in range(n) for j in range(3)]
        for cp in sent:
            cp.start()

        for cp in c_waits:
            cp.wait()
        cv = c_all_ref[...]
        act = cv * _sigmoid(cv)
        act_ref[...] = act
        pieces_ref[chip] = _dot(act.astype(BF16), wada_ref[...].astype(BF16))
        piece = lambda j, src_chip: pltpu.make_async_remote_copy(
            src_ref=pieces_ref.at[chip], dst_ref=pieces_ref.at[src_chip], send_sem=p_send.at[j], recv_sem=p_recv.at[j],
            device_id=others[j], device_id_type=MESH)
        for j in range(3):
            piece(j, chip).start()

        for i in range(n):
            for j in range(3):
                over_ici(i, j, chip_of[j]).wait_recv()
                to_sibling(i, j, core).start()
        for j in range(3):
            piece(j, chip).wait_send()
            piece(j, chip_of[j]).wait_recv()
        for i in range(n):
            for j in range(3):
                to_sibling(i, j, 1 - core).wait_recv()
                to_sibling(i, j, core).wait_send()
        for cp in sent:
            cp.wait_send()
        for cp in local:
            cp.wait()

    rows = 8 * ROWS_PER_DEVICE
    dma = pltpu.SemaphoreType.DMA
    return pl.pallas_call(
        body, name="comm_fwd",
        out_shape=[jax.ShapeDtypeStruct((rows, D_MODEL), F32), jax.ShapeDtypeStruct((4, rows, w_ada.shape[1]), F32)]
        + [jax.ShapeDtypeStruct((4,) + s.shape, s.dtype) for s in shards],
        in_specs=[VMEM_SPEC, VMEM_SPEC] + [ANY_SPEC] * n,
        out_specs=[VMEM_SPEC, VMEM_SPEC] + [ANY_SPEC] * n,
        scratch_shapes=[pltpu.VMEM((rows, D_MODEL), F32), dma((7,)), dma((7,)), dma((3,)), dma((3,)),
                        dma((3 * n,)), dma((3 * n,)), dma((3 * n,)), dma((3 * n,)), dma((n,))],
        compiler_params=pltpu.CompilerParams(vmem_limit_bytes=VMEM_LIMIT),
    )(c_blk, w_ada, *shards)


def _comm_bwd_call(grads, part):
    n = len(grads)

    def body(part_ref, *refs):
        g_refs, f_refs, parts_ref = refs[:n], refs[n:2 * n], refs[2 * n]
        scratch = refs[2 * n + 1:]
        a_refs, b_refs, p_refs, r_refs = (scratch[k * n:(k + 1) * n] for k in range(4))
        s_send, s_recv, d_send, d_recv, e_send, e_recv, h_send, h_recv, loc_sem = scratch[4 * n:]
        me, dev, chip, sibling, others = _position()
        core = me[2]
        chip_of = [2 * p[0] + p[1] for p in others]

        parts_ref[_rows_of(dev), :] = part_ref[...]
        s_waits = _all_to_all_rows(part_ref, parts_ref, dev, me, s_send, s_recv)

        mine = [pltpu.make_async_copy(g_refs[i].at[:, core], a_refs[i], loc_sem.at[i]) for i in range(n)]
        swap = [pltpu.make_async_remote_copy(src_ref=g_refs[i].at[:, 1 - core], dst_ref=b_refs[i], send_sem=d_send.at[i],
                                             recv_sem=d_recv.at[i], device_id=sibling, device_id_type=MESH) for i in range(n)]
        order = sorted(range(n), key=lambda i: g_refs[i].shape[2] * g_refs[i].shape[3])
        for i in order:
            mine[i].start()
            swap[i].start()
        cross = [pltpu.make_async_remote_copy(src_ref=p_refs[i].at[chip_of[j]], dst_ref=r_refs[i].at[j],
                                              send_sem=e_send.at[3 * i + j], recv_sem=e_recv.at[3 * i + j],
                                              device_id=others[j], device_id_type=MESH) for i in range(n) for j in range(3)]
        for i in order:
            mine[i].wait()
            swap[i].wait()
            for k in range(4):
                s = a_refs[i][k] + b_refs[i][k]
                a_refs[i][k] = s
                p_refs[i][k] = s.astype(BF16)
            for j in range(3):
                cross[3 * i + j].start()
        share = {}
        for i in order:
            for j in range(3):
                cross[3 * i + j].wait()
            f_refs[i][core] = (a_refs[i][chip] + r_refs[i][0].astype(F32) + r_refs[i][1].astype(F32)
                               + r_refs[i][2].astype(F32))
            share[i] = pltpu.make_async_remote_copy(src_ref=f_refs[i].at[core], dst_ref=f_refs[i].at[core],
                                                    send_sem=h_send.at[i], recv_sem=h_recv.at[i], device_id=sibling,
                                                    device_id_type=MESH)
            share[i].start()
        for i in range(n):
            share[i].wait_send()
            pltpu.make_async_remote_copy(src_ref=f_refs[i].at[core], dst_ref=f_refs[i].at[1 - core], send_sem=h_send.at[i],
                                         recv_sem=h_recv.at[i], device_id=sibling, device_id_type=MESH).wait_recv()
        for cp in s_waits:
            cp.wait()

    rows = 8 * ROWS_PER_DEVICE
    dma = pltpu.SemaphoreType.DMA
    quarter = [(4,) + g.shape[2:] for g in grads]
    return pl.pallas_call(
        body, name="comm_bwd",
        out_shape=[jax.ShapeDtypeStruct((2,) + g.shape[2:], F32) for g in grads]
        + [jax.ShapeDtypeStruct((rows, part.shape[1]), F32)],
        in_specs=[VMEM_SPEC] + [ANY_SPEC] * n,
        out_specs=[VMEM_SPEC] * (n + 1),
        scratch_shapes=[pltpu.VMEM(q, F32) for q in quarter] + [pltpu.VMEM(q, F32) for q in quarter]
        + [pltpu.VMEM(q, BF16) for q in quarter] + [pltpu.VMEM((3,) + q[1:], BF16) for q in quarter]
        + [dma((7,)), dma((7,)), dma((n,)), dma((n,)), dma((3 * n,)), dma((3 * n,)), dma((n,)), dma((n,)), dma((n,))],
        compiler_params=pltpu.CompilerParams(vmem_limit_bytes=VMEM_LIMIT),
    )(part, *grads)


def _by_owner(g, n):
    return jnp.transpose(g.reshape(g.shape[0], 4, n), (1, 0, 2)).reshape(4, 2, g.shape[0] // 2, n)


def _reduce_operands(g):
    quarter = (4,) + g.shape[2:]
    dma = pltpu.SemaphoreType.DMA
    scratch = [pltpu.VMEM(quarter, F32), pltpu.VMEM(quarter, F32), pltpu.VMEM(quarter, BF16),
               pltpu.VMEM((3,) + quarter[1:], BF16), dma((5,)), dma((5,)), dma((2,))]
    return jax.ShapeDtypeStruct((2,) + g.shape[2:], F32), scratch


def _grad_reduce(step, n_steps, g_ref, f_ref, a_ref, b_ref, p_ref, r_ref, send, recv, loc_sem):
    me, _, chip, sibling, others = _position()
    core = me[2]
    chip_of = [2 * p[0] + p[1] for p in others]
    remote = lambda src, dst, k, to: pltpu.make_async_remote_copy(
        src_ref=src, dst_ref=dst, send_sem=send.at[k], recv_sem=recv.at[k], device_id=to, device_id_type=MESH)
    mine = pltpu.make_async_copy(g_ref.at[:, core], a_ref, loc_sem.at[0])
    swap = remote(g_ref.at[:, 1 - core], b_ref, 0, sibling)
    cross = [remote(p_ref.at[chip_of[j]], r_ref.at[j], 1 + j, others[j]) for j in range(3)]
    total_ref = b_ref.at[0]
    keep = pltpu.make_async_copy(total_ref, f_ref.at[core], loc_sem.at[1])
    share = lambda half: remote(total_ref, f_ref.at[half], 4, sibling)
    at = [k * (n_steps - 1) // 3 for k in range(4)]

    @pl.when(step == at[0])
    def _():
        mine.start()
        swap.start()

    @pl.when(step == at[1])
    def _():
        mine.wait()
        swap.wait()
        for k in range(4):
            s = a_ref[k] + b_ref[k]
            a_ref[k] = s
            p_ref[k] = s.astype(BF16)
        for cp in cross:
            cp.start()

    @pl.when(step == at[2])
    def _():
        for cp in cross:
            cp.wait()
        total_ref[...] = a_ref[chip] + r_ref[0].astype(F32) + r_ref[1].astype(F32) + r_ref[2].astype(F32)
        keep.start()
        share(core).start()

    @pl.when(step == at[3])
    def _():
        keep.wait()
        share(core).wait_send()
        share(1 - core).wait_recv()


def _twice(t):
    lo = _lane_lo()
    other = pltpu.roll(t, HALF, 1)
    return jnp.concatenate([jnp.where(lo, t, other), jnp.where(lo, other, t)], axis=1)


def _once(g):
    first, second = g[:, :HEAD_LANES], g[:, HEAD_LANES:]
    return jnp.where(_lane_lo(), first + pltpu.roll(first, HALF, 1), second + pltpu.roll(second, HALF, 1))


def _rope_tables(pos_col, inv_row):
    ang = pos_col * inv_row
    return jnp.cos(ang), jnp.sin(ang)


def _gather_in_steps(step, n_steps, w_refs, full_refs, w_send, w_recv, f_send, f_recv, loc_sem):
    me, _, chip, sibling, others = _position()
    core = me[2]
    chip_of = [2 * p[0] + p[1] for p in others]
    n = len(w_refs)
    local = [pltpu.make_async_copy(w_refs[i], full_refs[i].at[chip], loc_sem.at[i]) for i in range(n)]

    def over_ici(i, j, src_chip):
        return pltpu.make_async_remote_copy(
            src_ref=w_refs[i].at[core], dst_ref=full_refs[i].at[src_chip, core], send_sem=w_send.at[3 * i + j],
            recv_sem=w_recv.at[3 * i + j], device_id=others[j], device_id_type=MESH)

    def to_sibling(i, j, half):
        return pltpu.make_async_remote_copy(
            src_ref=full_refs[i].at[chip_of[j], half], dst_ref=full_refs[i].at[chip_of[j], half],
            send_sem=f_send.at[3 * i + j], recv_sem=f_recv.at[3 * i + j], device_id=sibling, device_id_type=MESH)

    pairs = [(i, j) for i in range(n) for j in range(3)]

    @pl.when(step == 0)
    def _():
        for cp in local:
            cp.start()
        for i, j in pairs:
            over_ici(i, j, chip).start()

    @pl.when(step == 3 * n_steps // 4)
    def _():
        for i, j in pairs:
            over_ici(i, j, chip_of[j]).wait_recv()
            to_sibling(i, j, core).start()

    @pl.when(step == n_steps - 1)
    def _():
        for i, j in pairs:
            to_sibling(i, j, 1 - core).wait_recv()
            to_sibling(i, j, core).wait_send()
            over_ici(i, j, chip).wait_send()
        for cp in local:
            cp.wait()


def _pre_call(x, mod, b_ada, ng, wa, shards, seq):
    n_tok = x.shape[0]
    tm = min(TOKEN_TILE, seq)
    per_seq = seq // tm
    n_steps = n_tok // tm
    n = len(shards)

    def body(x_ref, mod_ref, bada_ref, ng_ref, wa_ref, *refs):
        w_refs, refs = refs[:n], refs[n:]
        zqkv_ref, zkr_ref, gates_ref, qs_ref, kd_ref, vd_ref = refs[:6]
        full_refs, sems = refs[6:6 + n], refs[6 + n:]
        _gather_in_steps(pl.program_id(0), n_steps, w_refs, full_refs, *sems)
        xv = x_ref[...]
        modv = mod_ref[0] + bada_ref[...]
        shift, scale = modv[:, :D_MODEL], modv[:, D_MODEL:2 * D_MODEL]
        r1 = lax.rsqrt(jnp.mean(xv * xv, axis=-1, keepdims=True) + EPS)
        h = ((xv * r1) * ng_ref[...]) * (1.0 + scale) + shift
        za = _dot(h.astype(BF16), wa_ref[...])
        zqkv_ref[...] = za[:, :A_KR]
        zkr_ref[...] = za[:, A_KR:A_GM]
        gates_ref[:, :512] = za[:, A_GM:A_QS]
        gates_ref[:, 512:] = za[:, A_GS:A_END]
        qs_ref[...] = (za[:, A_QS:A_KS] * (SWA_SCALE * LOG2E)).astype(BF16)
        kd_ref[...] = _twice(za[:, A_KS:A_VS]).astype(BF16)
        vd_ref[...] = _twice(za[:, A_VS:A_GS]).astype(BF16)

    tok = lambda w: pl.BlockSpec((tm, w), lambda i: (i, 0))
    outs = [(640, F32), (HEAD_LANES, F32), (1024, F32), (512, BF16), (256, BF16), (256, BF16)]
    dma = pltpu.SemaphoreType.DMA
    return pl.pallas_call(
        body, name="pre", grid=(n_steps,),
        out_shape=[jax.ShapeDtypeStruct((n_tok, w), dt) for w, dt in outs]
        + [jax.ShapeDtypeStruct((4,) + s.shape, s.dtype) for s in shards],
        in_specs=[tok(D_MODEL), pl.BlockSpec((1, 1, 3 * D_MODEL), lambda i: (i // per_seq, 0, 0)),
                  _full(b_ada.shape), _full(ng.shape), _full(wa.shape)] + [ANY_SPEC] * n,
        out_specs=[tok(w) for w, _ in outs] + [ANY_SPEC] * n,
        scratch_shapes=[dma((3 * n,)), dma((3 * n,)), dma((3 * n,)), dma((3 * n,)), dma((n,))],
        compiler_params=_params(1),
    )(x, mod, b_ada, ng, wa, *shards)


def _up_call(zqkv, zkr, pos_col, qg, kvg, inv128, wq2, wkv, seq):
    n_tok = zqkv.shape[0]
    tm = min(TOKEN_TILE, seq)

    def body(zqkv_ref, zkr_ref, pos_ref, qg_ref, kvg_ref, inv_ref, wq_ref, wkv_ref, qf_ref, kf_ref, v_ref, rope_ref):
        cos, sin = _rope_tables(pos_ref[...], inv_ref[...])
        rope_ref[:, :HEAD_LANES] = cos
        rope_ref[:, HEAD_LANES:] = sin
        zq, zkv = zqkv_ref[:, A_ZQ:A_ZKV], zqkv_ref[:, A_ZKV:A_KR]
        rq = lax.rsqrt(jnp.mean(zq * zq, axis=-1, keepdims=True) + EPS)
        qn = ((zq * rq) * qg_ref[...]).astype(BF16)
        qr = _dot(qn, wq_ref[...])
        cf, sf = jnp.tile(cos, (1, N_HEADS)), jnp.tile(sin, (1, N_HEADS))
        qf_ref[...] = ((qr[:, :1024] * cf + qr[:, 1024:] * sf) * (MLA_SCALE * LOG2E)).astype(BF16)
        rkv = lax.rsqrt(jnp.mean(zkv * zkv, axis=-1, keepdims=True) + EPS)
        kvn = ((zkv * rkv) * kvg_ref[...]).astype(BF16)
        kv = _dot(kvn, wkv_ref[...])
        zkr = zkr_ref[...]
        kpe = jnp.where(_lane_lo(), 0.0, zkr * cos) + pltpu.roll(zkr, HALF, 1) * sin
        kf_ref[...] = (kv[:, :1024] + jnp.tile(kpe, (1, N_HEADS))).astype(BF16)
        v_ref[...] = kv[:, 1024:].astype(BF16)

    tok = lambda w: pl.BlockSpec((tm, w), lambda i: (i, 0))
    outs = [(1024, BF16), (1024, BF16), (512, BF16), (2 * HEAD_LANES, F32)]
    return pl.pallas_call(
        body, name="up", grid=(n_tok // tm,),
        out_shape=[jax.ShapeDtypeStruct((n_tok, w), dt) for w, dt in outs],
        in_specs=[tok(640), tok(HEAD_LANES), tok(1), _full(qg.shape), _full(kvg.shape), _full(inv128.shape),
                  _full(wq2.shape), _full(wkv.shape)],
        out_specs=[tok(w) for w, _ in outs],
        compiler_params=_params(1),
    )(zqkv, zkr, pos_col, qg, kvg, inv128, wq2, wkv)


def _lane_lo(width=HEAD_LANES):
    return lax.broadcasted_iota(jnp.int32, (1, width), 1) < HALF


def _eye(n=HEAD_LANES):
    r = lax.broadcasted_iota(jnp.int32, (n, n), 0)
    c = lax.broadcasted_iota(jnp.int32, (n, n), 1)
    return jnp.where(r == c, 1.0, 0.0).astype(BF16)


def _mla_fwd_call(qf, kf, v, n_seq, seq):
    tq = min(ATT_TILE, seq)
    nq = seq // tq

    ext = HALF + 16

    def body(q_ref, k_ref, v_ref, o_ref, lse_ref, vt_ref, acc_ref):
        i = pl.program_id(1)
        eye = _eye()

        @pl.when(i == 0)
        def _():
            for h in range(N_HEADS):
                vt_ref[h * ext + HALF:(h + 1) * ext, :] = jnp.ones((16, seq), BF16)
            for t in range(nq):
                for p in range(N_HEADS // 2):
                    pair = slice(p * HEAD_LANES, (p + 1) * HEAD_LANES)
                    v_t = _dot_nt(eye, v_ref[t * tq:(t + 1) * tq, pair]).astype(BF16)
                    for hh in range(2):
                        r0 = (2 * p + hh) * ext
                        vt_ref[r0:r0 + HALF, t * tq:(t + 1) * tq] = v_t[hh * HALF:(hh + 1) * HALF, :]

        q = q_ref[...]
        qcol = i * tq + lax.broadcasted_iota(jnp.int32, (1, tq), 1)
        heads = range(N_HEADS)
        lanes = [slice(h * HEAD_LANES, (h + 1) * HEAD_LANES) for h in heads]

        def make_step(masked, n_tiles):
            def step(kt0, carry):
                tiles = range(n_tiles)
                start = pl.multiple_of(kt0 * tq, tq)
                ks = [k_ref[pl.ds(pl.multiple_of((kt0 + t) * tq, tq), tq), :] for t in tiles]
                vt = vt_ref[:, pl.ds(start, n_tiles * tq)]
                last = n_tiles - 1
                if masked:
                    keep = ((kt0 + last) * tq + lax.broadcasted_iota(jnp.int32, (tq, 1), 0)) <= qcol

                def scores(h):
                    sts = [_dot_nt(ks[t][:, lanes[h]], q[:, lanes[h]]) for t in tiles]
                    if masked:
                        sts[last] = jnp.where(keep, sts[last], NEG)
                    return sts

                def softmax(h, sts):
                    m_old = carry[h]
                    m_new = m_old
                    for st in sts:
                        m_new = jnp.maximum(m_new, jnp.max(st, axis=0, keepdims=True))
                    pt = jnp.concatenate([jnp.exp2(st - m_new).astype(BF16) for st in sts], axis=0)
                    return m_new, jnp.exp2(m_old - m_new), pt

                def values(h, alpha, pt):
                    rows = slice(h * ext, (h + 1) * ext)
                    acc_ref[rows, :] = acc_ref[rows, :] * alpha + _dot(vt[rows, :], pt)

                sts, soft, out = {0: scores(0), 1: scores(1)}, {}, {}
                for h in range(N_HEADS + 1):
                    if h + 2 < N_HEADS:
                        sts[h + 2] = scores(h + 2)
                    if h < N_HEADS:
                        soft[h] = softmax(h, sts.pop(h))
                    if h >= 1:
                        m_new, alpha, pt = soft.pop(h - 1)
                        values(h - 1, alpha, pt)
                        out[h - 1] = m_new
                return tuple(out[h] for h in heads)
            return step

        acc_ref[...] = jnp.zeros_like(acc_ref)
        init = (jnp.full((1, tq), NEG, F32),) * N_HEADS
        count = i + 1
        carry = lax.fori_loop(0, (count + 1) // 2 - 1, lambda j, c: make_step(False, 2)(2 * j, c), init)
        carry = lax.cond(count % 2 == 0, lambda c: make_step(True, 2)(i - 1, c), lambda c: make_step(True, 1)(i, c), carry)
        dens = [acc_ref[h * ext + HALF:h * ext + HALF + 1, :] for h in heads]
        acc_t = jnp.concatenate([acc_ref[h * ext:h * ext + HALF, :] * (1.0 / dens[h]) for h in heads], axis=0)
        o_ref[...] = acc_t.T
        for h in heads:
            lse_ref[0, h // 4, h % 4:h % 4 + 1, :] = carry[h] + jnp.log2(dens[h])

    n_tok = qf.shape[0]
    return pl.pallas_call(
        body, name="mla_fwd", grid=(n_seq, nq),
        out_shape=[jax.ShapeDtypeStruct((n_tok, 512), F32), jax.ShapeDtypeStruct((n_seq, 2, 4, seq), F32)],
        in_specs=[pl.BlockSpec((tq, 1024), lambda b, i: (b * nq + i, 0)),
                  pl.BlockSpec((seq, 1024), lambda b, i: (b, 0)),
                  pl.BlockSpec((seq, 512), lambda b, i: (b, 0))],
        out_specs=[pl.BlockSpec((tq, 512), lambda b, i: (b * nq + i, 0)),
                   pl.BlockSpec((1, 2, 4, tq), lambda b, i: (b, 0, 0, i))],
        scratch_shapes=[pltpu.VMEM((N_HEADS * ext, seq), BF16), pltpu.VMEM((N_HEADS * ext, tq), F32)],
        compiler_params=_params(2),
    )(qf, kf, v)


def _mla_bwd_call(qf, kf, v, do, delta, lse, n_seq, seq):
    tq = min(ATT_TILE, seq)
    nq = seq // tq

    nh = 4
    heads = range(nh)
    lanes = [slice(h * HEAD_LANES, (h + 1) * HEAD_LANES) for h in heads]

    def body(q_ref, k_ref, v_ref, do_ref, dl_ref, lse_ref, dq_ref, dk_ref, dv_ref,
             kt_ref, dot_ref, dqt_ref, dvt_ref):
        eye = _eye()
        sub_lo = lax.broadcasted_iota(jnp.int32, (HEAD_LANES, 1), 0) < HALF

        for t in range(nq):
            r = slice(t * tq, (t + 1) * tq)
            kv = k_ref[r, :]
            for h in heads:
                kt_ref[lanes[h], r] = _dot_nt(eye, kv[:, lanes[h]]).astype(BF16)
            for p in range(nh // 2):
                dov = do_ref[r, lanes[p]]
                dt = _dot_nt(eye, dov)
                dot_ref[2 * p, :, r] = jnp.where(sub_lo, dt, 0.0).astype(BF16)
                dot_ref[2 * p + 1, :, r] = jnp.where(sub_lo, 0.0, dt).astype(BF16)
        dqt_ref[...] = jnp.zeros_like(dqt_ref)
        dvt_ref[...] = jnp.zeros_like(dvt_ref)

        def flush_dv(tile, which):
            rows = pl.ds(pl.multiple_of(tile * tq, tq), tq)
            for p in range(nh // 2):
                dv_ref[rows, lanes[p]] = dvt_ref[which, p * HEAD_LANES:(p + 1) * HEAD_LANES, :].T

        def k_step(kt, _):
            slot = kt % 2
            kr = pl.ds(pl.multiple_of(kt * tq, tq), tq)
            k = k_ref[kr, :]
            vv = v_ref[kr, :]
            k_t = kt_ref[:, kr]
            krow = kt * tq + lax.broadcasted_iota(jnp.int32, (tq, 1), 0)

            def make_step(masked, n_tiles):
                def q_step(qt0, carry):
                    tiles = range(n_tiles)
                    qrs = [pl.ds(pl.multiple_of((qt0 + t) * tq, tq), tq) for t in tiles]
                    if masked:
                        flush_dv(jnp.maximum(kt - 1, 0), 1 - slot)
                    qs = [q_ref[qr, :] for qr in qrs]
                    if masked:
                        keep = krow <= (qt0 * tq + lax.broadcasted_iota(jnp.int32, (1, tq), 1))

                    def scores(h):
                        do_ts = [dot_ref[h, :, qr] for qr in qrs]
                        sts = [_dot_nt(k[:, lanes[h]], qs[t][:, lanes[h]]) for t in tiles]
                        dpts = [_dot(vv[:, lanes[h // 2]], do_ts[t]) for t in tiles]
                        return do_ts, sts, dpts

                    def softmax(h, sts, dpts):
                        pts, dsts = [], []
                        for t in tiles:
                            pt = jnp.exp2(sts[t] - lse_ref[0, 0, h:h + 1, qrs[t]])
                            if masked and t == 0:
                                pt = jnp.where(keep, pt, 0.0)
                            dsts.append((pt * (dpts[t] - dl_ref[0, h:h + 1, qrs[t]])).astype(BF16))
                            pts.append(pt.astype(BF16))
                        return pts, dsts

                    def grads(h, do_ts, pts, dsts):
                        half = slice((h % 2) * HALF, (h % 2 + 1) * HALF)
                        dst_all = jnp.concatenate(dsts, axis=1)
                        pt_all = jnp.concatenate(pts, axis=1)
                        do_all = jnp.concatenate([do_ts[t][half, :] for t in tiles], axis=1)
                        q_all = jnp.concatenate([qs[t][:, lanes[h]] for t in tiles], axis=0)
                        dvt_ref[slot, h * HALF:(h + 1) * HALF, :] += _dot_nt(do_all, pt_all)
                        dk_ref[kr, lanes[h]] += _dot(dst_all, q_all)
                        for t in tiles:
                            dqt_ref[lanes[h], qrs[t]] += _dot(k_t[lanes[h], :], dsts[t])

                    first, second = {0: scores(0)}, {}
                    for h in range(nh + 1):
                        if h + 1 < nh:
                            first[h + 1] = scores(h + 1)
                        if h < nh:
                            do_ts, sts, dpts = first.pop(h)
                            second[h] = (do_ts,) + softmax(h, sts, dpts)
                        if h >= 1:
                            grads(h - 1, *second.pop(h - 1))
                    return carry
                return q_step

            dk_ref[kr, :] = jnp.zeros((tq, nh * HEAD_LANES), F32)
            dvt_ref[slot] = jnp.zeros(dvt_ref.shape[1:], F32)
            count = nq - kt
            lax.cond(count >= 2, lambda c: make_step(True, 2)(kt, c), lambda c: make_step(True, 1)(kt, c), 0)
            lax.fori_loop(1, count // 2, lambda j, c: make_step(False, 2)(kt + 2 * j, c), 0)
            lax.cond(jnp.logical_and(count % 2 == 1, count >= 3), lambda c: make_step(False, 1)(nq - 1, c), lambda c: c, 0)
            return 0

        lax.fori_loop(0, nq, k_step, 0)
        flush_dv(nq - 1, (nq - 1) % 2)
        for t in range(nq):
            r = slice(t * tq, (t + 1) * tq)
            for h in heads:
                dq_ref[r, lanes[h]] = dqt_ref[lanes[h], r].T

    n_tok = qf.shape[0]
    groups = N_HEADS // nh
    blk = lambda w: pl.BlockSpec((seq, w), lambda b, g: (b, g))
    return pl.pallas_call(
        body, name="mla_bwd", grid=(n_seq, groups),
        out_shape=[jax.ShapeDtypeStruct((n_tok, 1024), F32), jax.ShapeDtypeStruct((n_tok, 1024), F32),
                   jax.ShapeDtypeStruct((n_tok, 512), F32)],
        in_specs=[blk(512), blk(512), blk(256), blk(256), pl.BlockSpec((1, nh, seq), lambda b, g: (g, 0, b)),
                  pl.BlockSpec((1, 1, nh, seq), lambda b, g: (b, g, 0, 0))],
        out_specs=[blk(512), blk(512), blk(256)],
        scratch_shapes=[pltpu.VMEM((nh * HEAD_LANES, seq), BF16), pltpu.VMEM((nh, HEAD_LANES, seq), BF16),
                        pltpu.VMEM((nh * HEAD_LANES, seq), F32), pltpu.VMEM((2, nh * HALF, tq), F32)],
        compiler_params=_params(2),
    )(qf, kf, v, do, delta, lse)


SWA_BLOCKS = 4


def _swa_block(n, pos_col_ref, posq):
    w = SWA_WINDOW
    start = pl.multiple_of(jnp.maximum(n - 1, 0) * w, w)
    posk = pos_col_ref[pl.ds(start, 2 * w), :]
    rel = (n * w + lax.broadcasted_iota(jnp.int32, (1, w), 1)) - (start + lax.broadcasted_iota(jnp.int32, (2 * w, 1), 0))
    valid = jnp.logical_and(rel >= 0, rel < w)
    return start, jnp.where(valid, posq - posk, 1e30)


def _alibi(h):
    return LOG2E * 2.0 ** -(h + 1)


def _transpose_rows(eye, src_ref, dst_ref, seq, width):
    step = 2 * SWA_WINDOW
    for t in range(seq // step):
        for p in range(width // HEAD_LANES):
            lanes = slice(p * HEAD_LANES, (p + 1) * HEAD_LANES)
            dst_ref[lanes, t * step:(t + 1) * step] = _dot_nt(eye, src_ref[t * step:(t + 1) * step, lanes]).astype(BF16)


def _swa_fwd_call(qs, kd, vd, pos_col, pos_row, sinks, n_seq, seq):
    w = SWA_WINDOW
    qb = SWA_BLOCKS
    steps = seq // (qb * w)
    ext = HALF + 16

    def body(q_ref, k_ref, v_ref, pc_ref, pr_ref, sink_ref, o_ref, lse_ref, vt_ref):
        n = pl.program_id(1)
        lo = _lane_lo()
        hi = jnp.logical_not(lo)
        eye = _eye()

        @pl.when(n == 0)
        def _():
            step = 2 * w
            for kv in range(2):
                vt_ref[kv * ext + HALF:(kv + 1) * ext, :] = jnp.ones((16, seq), BF16)
                for t in range(seq // step):
                    v_t = _dot_nt(eye, v_ref[t * step:(t + 1) * step, kv * HEAD_LANES:(kv + 1) * HEAD_LANES])
                    vt_ref[kv * ext:kv * ext + HALF, t * step:(t + 1) * step] = v_t[:HALF, :].astype(BF16)

        heads = range(N_HEADS)
        blocks = range(qb)
        geo = [_swa_block(n * qb + bi, pc_ref, pr_ref[bi]) for bi in blocks]
        wins = [pl.ds(g[0], 2 * w) for g in geo]
        kwins = [k_ref[win, :] for win in wins]
        vts = [vt_ref[:, win] for win in wins]
        sts = []
        for bi in blocks:
            q = q_ref[bi * w:(bi + 1) * w, :]
            sts.append([])
            for j in range(N_HEADS // 2):
                qp = q[:, j * HEAD_LANES:(j + 1) * HEAD_LANES]
                both = jnp.concatenate([jnp.where(lo, qp, jnp.zeros_like(qp)), jnp.where(hi, qp, jnp.zeros_like(qp))], axis=0)
                st = _dot_nt(kwins[bi][:, (j // 2) * HEAD_LANES:(j // 2 + 1) * HEAD_LANES], both)
                sts[bi] += [st[:, :w], st[:, w:]]
        ps, ms = [], []
        for bi in blocks:
            ps.append([])
            ms.append([])
            for h in heads:
                s = sts[bi][h] - _alibi(h) * geo[bi][1]
                m = jnp.maximum(jnp.max(s, axis=0, keepdims=True), sink_ref[0, h] * LOG2E)
                ps[bi].append(jnp.exp2(s - m).astype(BF16))
                ms[bi].append(m)
        for bi in blocks:
            ots = []
            for h in heads:
                pv = _dot(vts[bi][(h // 4) * ext:(h // 4 + 1) * ext, :], ps[bi][h])
                l = pv[HALF:HALF + 1, :] + jnp.exp2(sink_ref[0, h] * LOG2E - ms[bi][h])
                ots.append(pv[:HALF, :] * (1.0 / l))
                lse_ref[0, h:h + 1, bi * w:(bi + 1) * w] = ms[bi][h] + jnp.log2(l)
            o_ref[bi * w:(bi + 1) * w, :] = jnp.concatenate(ots, axis=0).T

    n_tok = qs.shape[0]
    tok = lambda width: pl.BlockSpec((qb * w, width), lambda b, n: (b * steps + n, 0))
    whole = lambda width: pl.BlockSpec((seq, width), lambda b, n: (b, 0))
    return pl.pallas_call(
        body, name="swa_fwd", grid=(n_seq, steps),
        out_shape=[jax.ShapeDtypeStruct((n_tok, 512), F32), jax.ShapeDtypeStruct((n_seq, N_HEADS, seq), F32)],
        in_specs=[tok(512), whole(256), whole(256), whole(1), pl.BlockSpec((qb, 1, w), lambda b, n: (b * steps + n, 0, 0)),
                  pl.BlockSpec(memory_space=pltpu.SMEM)],
        out_specs=[tok(512), pl.BlockSpec((1, N_HEADS, qb * w), lambda b, n: (b, 0, n))],
        scratch_shapes=[pltpu.VMEM((2 * ext, seq), BF16)],
        compiler_params=_params(2),
    )(qs, kd, vd, pos_col, pos_row, sinks)


def _swa_bwd_call(qs, kd, vd, do, delta, lse, pos_col, pos_row, sinks, g_out, n_seq, seq):
    w = SWA_WINDOW
    qb = SWA_BLOCKS
    steps = seq // (qb * w)
    reduced, reduce_scratch = _reduce_operands(g_out)

    def body(q_ref, k_ref, v_ref, do_ref, dl_ref, lse_ref, pc_ref, pr_ref, sink_ref, g_ref, dq_ref, dk_ref, dv_ref,
             dsink_ref, f_ref, kt_ref, *reduce_refs):
        b, n = pl.program_id(0), pl.program_id(1)
        _grad_reduce(b * steps + n, n_seq * steps, g_ref, f_ref, *reduce_refs)
        lo = _lane_lo()
        hi = jnp.logical_not(lo)
        sub_lo = lax.broadcasted_iota(jnp.int32, (HEAD_LANES, 1), 0) < HALF
        eye = _eye()

        @pl.when(n == 0)
        def _():
            dk_ref[...] = jnp.zeros_like(dk_ref)
            dv_ref[...] = jnp.zeros_like(dv_ref)
            _transpose_rows(eye, k_ref, kt_ref, seq, 2 * HEAD_LANES)

        @pl.when(jnp.logical_and(n == 0, b == 0))
        def _():
            dsink_ref[...] = jnp.zeros_like(dsink_ref)

        heads = range(N_HEADS)
        blocks = range(qb)
        kv_lanes = lambda h: slice((h // 4) * HEAD_LANES, (h // 4 + 1) * HEAD_LANES)
        geo = [_swa_block(n * qb + bi, pc_ref, pr_ref[bi]) for bi in blocks]
        wins = [pl.ds(g[0], 2 * w) for g in geo]
        kwins = [k_ref[win, :] for win in wins]
        vwins = [v_ref[win, :] for win in wins]

        do_ts, deltas, qms, doms = [], [], [], []
        for bi in blocks:
            rows = slice(bi * w, (bi + 1) * w)
            for lst in (do_ts, deltas, qms, doms):
                lst.append([])
            for j in range(N_HEADS // 2):
                pair = slice(j * HEAD_LANES, (j + 1) * HEAD_LANES)
                dop = do_ref[rows, pair]
                qp = q_ref[rows, pair]
                dt = _dot_nt(eye, dop)
                for hh in range(2):
                    half = lo if hh == 0 else hi
                    do_ts[bi].append(jnp.where(sub_lo, dt, 0.0).astype(BF16) if hh == 0
                                     else jnp.where(sub_lo, 0.0, dt).astype(BF16))
                    deltas[bi].append(dl_ref[2 * j + hh:2 * j + hh + 1, rows])
                    qms[bi].append(jnp.where(half, qp, jnp.zeros_like(qp)))
                    doms[bi].append(jnp.where(half, dop, jnp.zeros_like(dop)))
        sts, dpts = [], []
        for bi in blocks:
            sts.append([])
            dpts.append([])
            for j in range(N_HEADS // 2):
                a, b = 2 * j, 2 * j + 1
                st = _dot_nt(kwins[bi][:, kv_lanes(a)], jnp.concatenate([qms[bi][a], qms[bi][b]], axis=0))
                dpt = _dot(vwins[bi][:, kv_lanes(a)], jnp.concatenate([do_ts[bi][a], do_ts[bi][b]], axis=1))
                sts[bi] += [st[:, :w], st[:, w:]]
                dpts[bi] += [dpt[:, :w], dpt[:, w:]]
        pts, dsts = [], []
        for bi in blocks:
            pts.append([])
            dsts.append([])
            for h in heads:
                lse_h = lse_ref[0, h:h + 1, bi * w:(bi + 1) * w]
                pt = jnp.exp2(sts[bi][h] - _alibi(h) * geo[bi][1] - lse_h)
                dsts[bi].append((pt * (dpts[bi][h] - deltas[bi][h])).astype(BF16))
                pts[bi].append(pt.astype(BF16))
                dsink_ref[h:h + 1, :] += -jnp.exp2(sink_ref[0, h] * LOG2E - lse_h) * deltas[bi][h]
        for bi in blocks:
            for kv in range(2):
                group = range(4 * kv, 4 * kv + 4)
                dst_all = jnp.concatenate([dsts[bi][h] for h in group], axis=1)
                pt_all = jnp.concatenate([pts[bi][h] for h in group], axis=1)
                q_all = jnp.concatenate([qms[bi][h] for h in group], axis=0)
                do_all = jnp.concatenate([doms[bi][h] for h in group], axis=0)
                dk_ref[wins[bi], kv_lanes(4 * kv)] += _dot(dst_all, q_all)
                dv_ref[wins[bi], kv_lanes(4 * kv)] += _dot(pt_all, do_all)
        for bi in blocks:
            ktw = kt_ref[:, wins[bi]]
            for j in range(N_HEADS // 2):
                k_t = ktw[kv_lanes(2 * j), :]
                both = _dot(k_t, jnp.concatenate([dsts[bi][2 * j], dsts[bi][2 * j + 1]], axis=1))
                dq_t = jnp.where(sub_lo, both[:, :w], both[:, w:])
                dq_ref[bi * w:(bi + 1) * w, j * HEAD_LANES:(j + 1) * HEAD_LANES] = dq_t.T * SWA_SCALE

    n_tok = qs.shape[0]
    tok = lambda width: pl.BlockSpec((qb * w, width), lambda b, n: (b * steps + n, 0))
    whole = lambda width: pl.BlockSpec((seq, width), lambda b, n: (b, 0))
    return pl.pallas_call(
        body, name="swa_bwd", grid=(n_seq, steps),
        out_shape=[jax.ShapeDtypeStruct((n_tok, 512), F32), jax.ShapeDtypeStruct((n_tok, 256), F32),
                   jax.ShapeDtypeStruct((n_tok, 256), F32), jax.ShapeDtypeStruct((N_HEADS, HEAD_LANES), F32), reduced],
        in_specs=[tok(512), whole(256), whole(256), pl.BlockSpec((qb * w, 512), lambda b, n: (b * steps + n, 1)),
                  pl.BlockSpec((N_HEADS, qb * w), lambda b, n: (0, b * steps + n)),
                  pl.BlockSpec((1, N_HEADS, qb * w), lambda b, n: (b, 0, n)),
                  whole(1), pl.BlockSpec((qb, 1, w), lambda b, n: (b * steps + n, 0, 0)),
                  pl.BlockSpec(memory_space=pltpu.SMEM), ANY_SPEC],
        out_specs=[tok(512), whole(256), whole(256), _full((N_HEADS, HEAD_LANES)), ANY_SPEC],
        scratch_shapes=[pltpu.VMEM((2 * HEAD_LANES, seq), BF16)] + reduce_scratch,
        compiler_params=_params(2),
    )(qs, kd, vd, do, delta, lse, pos_col, pos_row, sinks, g_out)


def _post_call(x, target, o_mla, o_swa, gates, mod, b_ada, fg, w_out, seq):
    n_tok = x.shape[0]
    tm = min(TOKEN_TILE, seq)
    per_seq = seq // tm
    n_seq = n_tok // seq

    def body(x_ref, t_ref, om_ref, os_ref, g_ref, mod_ref, bada_ref, fg_ref, w_ref,
             dx2_ref, do_ref, dg_ref, gw_ref, gfg_ref, dgate_ref, loss_ref, dmla_ref, dswa_ref):
        i = pl.program_id(0)

        @pl.when(i == 0)
        def _():
            gw_ref[...] = jnp.zeros_like(gw_ref)
            gfg_ref[...] = jnp.zeros_like(gfg_ref)
            loss_ref[...] = jnp.zeros_like(loss_ref)

        @pl.when(i % per_seq == 0)
        def _():
            dgate_ref[...] = jnp.zeros_like(dgate_ref)

        gate = mod_ref[0][:, 2 * D_MODEL:] + bada_ref[:, 2 * D_MODEL:]
        fgv = fg_ref[...]
        fgd = fgv * (1.0 / D_MODEL)
        subs = _sub_tiles(tm)
        gs = [g_ref[r, :] for r in subs]
        os_ = [jnp.concatenate([om_ref[r, :], os_ref[r, :]], axis=-1) for r in subs]
        sgs = [_sigmoid(g) for g in gs]
        sils = [g * sg for g, sg in zip(gs, sgs)]
        ypres = [(o * sil).astype(BF16) for o, sil in zip(os_, sils)]
        ys = [_dot(ypre, w_ref[...]) for ypre in ypres]
        dys, loss, gfg, dgate = [], 0.0, 0.0, 0.0
        for r, y in zip(subs, ys):
            x2 = x_ref[r, :] + gate * y
            r2 = lax.rsqrt(jnp.mean(x2 * x2, axis=-1, keepdims=True) + EPS)
            xn2 = x2 * r2
            err = xn2 * fgv - t_ref[r, :]
            loss = loss + jnp.sum(jnp.sum(err * err, axis=-1, keepdims=True), axis=0, keepdims=True)
            gfg = gfg + jnp.sum(err * xn2, axis=0, keepdims=True)
            dxn2 = err * fgd
            dx2 = r2 * (dxn2 - xn2 * jnp.mean(dxn2 * xn2, axis=-1, keepdims=True))
            dx2_ref[r, :] = dx2
            dgate = dgate + jnp.sum(dx2 * y, axis=0, keepdims=True)
            dys.append((dx2 * gate).astype(BF16))
        loss_ref[...] += jnp.broadcast_to(loss * (0.5 / D_MODEL), loss_ref.shape)
        gfg_ref[...] += gfg * (1.0 / D_MODEL)
        dgate_ref[0] += dgate
        gw_ref[...] += _dot_tn(jnp.concatenate(ypres, axis=0), jnp.concatenate(dys, axis=0))
        dypres = [_dot_nt(dy, w_ref[...]) for dy in dys]
        pick = jnp.where(jnp.right_shift(lax.broadcasted_iota(jnp.int32, (2 * N_HEADS, D_MODEL), 1), 6)
                         == lax.broadcasted_iota(jnp.int32, (2 * N_HEADS, D_MODEL), 0), 1.0, 0.0).astype(BF16)
        for r, dypre, o, g, sg, sil in zip(subs, dypres, os_, gs, sgs, sils):
            dov = (dypre * sil).astype(BF16)
            do_ref[r, :] = dov
            delta = _dot_nt(pick, (dov.astype(F32) * o).astype(BF16))
            for grp in range(2):
                dmla_ref[grp, :, r] = delta[4 * grp:4 * grp + 4, :]
            dswa_ref[:, r] = delta[N_HEADS:, :]
            dg_ref[r, :] = (dypre * o * (sg + sil * (1.0 - sg))).astype(BF16)

    tok = lambda w: pl.BlockSpec((tm, w), lambda i: (i, 0))
    per_b = pl.BlockSpec((1, 1, 3 * D_MODEL), lambda i: (i // per_seq, 0, 0))
    return pl.pallas_call(
        body, name="post", grid=(n_tok // tm,),
        out_shape=[jax.ShapeDtypeStruct((n_tok, D_MODEL), F32), jax.ShapeDtypeStruct((n_tok, D_MODEL), BF16),
                   jax.ShapeDtypeStruct((n_tok, D_MODEL), BF16), jax.ShapeDtypeStruct((D_MODEL, D_MODEL), F32),
                   jax.ShapeDtypeStruct((1, D_MODEL), F32), jax.ShapeDtypeStruct((n_seq, 1, D_MODEL), F32),
                   jax.ShapeDtypeStruct((1, HEAD_LANES), F32),
                   jax.ShapeDtypeStruct((2, N_HEADS // 2, n_tok), F32), jax.ShapeDtypeStruct((N_HEADS, n_tok), F32)],
        in_specs=[tok(D_MODEL), tok(D_MODEL), tok(512), tok(512), tok(D_MODEL), per_b, _full(b_ada.shape),
                  _full(fg.shape), _full(w_out.shape)],
        out_specs=[tok(D_MODEL), tok(D_MODEL), tok(D_MODEL), _full((D_MODEL, D_MODEL)), _full((1, D_MODEL)),
                   pl.BlockSpec((1, 1, D_MODEL), lambda i: (i // per_seq, 0, 0)), _full((1, HEAD_LANES)),
                   pl.BlockSpec((2, N_HEADS // 2, tm), lambda i: (0, 0, i)), pl.BlockSpec((N_HEADS, tm), lambda i: (0, i))],
        compiler_params=_params(1),
    )(x, target, o_mla, o_swa, gates, mod, b_ada, fg, w_out)


def _mid_bwd_call(dqf, dkf, dv, zqkv, rope, qg, kvg, wq2, wkv, seq):
    n_tok = dqf.shape[0]
    tm = min(TOKEN_TILE, seq)

    def body(dq_ref, dk_ref, dv_ref, z_ref, rope_ref, qg_ref, kvg_ref, wq_ref, wkv_ref,
             dz_ref, gwq_ref, gwkv_ref, gqg_ref, gkvg_ref):
        i = pl.program_id(0)

        @pl.when(i == 0)
        def _():
            gwq_ref[...] = jnp.zeros_like(gwq_ref)
            gwkv_ref[...] = jnp.zeros_like(gwkv_ref)
            gqg_ref[...] = jnp.zeros_like(gqg_ref)
            gkvg_ref[...] = jnp.zeros_like(gkvg_ref)

        cos, sin = rope_ref[:, :HEAD_LANES], rope_ref[:, HEAD_LANES:]
        cf, sf = jnp.tile(cos, (1, N_HEADS)), jnp.tile(sin, (1, N_HEADS))
        dq = dq_ref[...] * MLA_SCALE
        dqr = jnp.concatenate([dq * cf, dq * sf], axis=-1).astype(BF16)
        zq, zkv = z_ref[:, :Q_LORA], z_ref[:, Q_LORA:]
        qgv, kvgv = qg_ref[...], kvg_ref[...]

        rq = lax.rsqrt(jnp.mean(zq * zq, axis=-1, keepdims=True) + EPS)
        xq = zq * rq
        gwq_ref[...] += _dot_tn((xq * qgv).astype(BF16), dqr)
        dqn = _dot_nt(dqr, wq_ref[...])
        gqg_ref[...] += jnp.sum(dqn * xq, axis=0, keepdims=True)
        dxq = dqn * qgv
        dz_ref[:, :Q_LORA] = (rq * (dxq - xq * jnp.mean(dxq * xq, axis=-1, keepdims=True))).astype(BF16)

        dk = dk_ref[...] * LN2
        dkv = jnp.concatenate([dk, dv_ref[...]], axis=-1).astype(BF16)
        rkv = lax.rsqrt(jnp.mean(zkv * zkv, axis=-1, keepdims=True) + EPS)
        xkv = zkv * rkv
        gwkv_ref[...] += _dot_tn((xkv * kvgv).astype(BF16), dkv)
        dkvn = _dot_nt(dkv, wkv_ref[...])
        gkvg_ref[...] += jnp.sum(dkvn * xkv, axis=0, keepdims=True)
        dxkv = dkvn * kvgv
        dz_ref[:, Q_LORA:A_KR] = (rkv * (dxkv - xkv * jnp.mean(dxkv * xkv, axis=-1, keepdims=True))).astype(BF16)

        dkpe = dk[:, :HEAD_LANES]
        for h in range(1, N_HEADS):
            dkpe = dkpe + dk[:, h * HEAD_LANES:(h + 1) * HEAD_LANES]
        dz_ref[:, A_KR:] = (jnp.where(_lane_lo(), 0.0, dkpe * cos) + pltpu.roll(dkpe * sin, HALF, 1)).astype(BF16)

    tok = lambda w: pl.BlockSpec((tm, w), lambda i: (i, 0))
    return pl.pallas_call(
        body, name="mid_bwd", grid=(n_tok // tm,),
        out_shape=[jax.ShapeDtypeStruct((n_tok, A_GM), BF16),
                   jax.ShapeDtypeStruct(wq2.shape, F32), jax.ShapeDtypeStruct(wkv.shape, F32),
                   jax.ShapeDtypeStruct((1, Q_LORA), F32), jax.ShapeDtypeStruct((1, KV_LORA), F32)],
        in_specs=[tok(1024), tok(1024), tok(512), tok(640), tok(2 * HEAD_LANES), _full(qg.shape), _full(kvg.shape),
                  _full(wq2.shape), _full(wkv.shape)],
        out_specs=[tok(A_GM), _full(wq2.shape), _full(wkv.shape), _full((1, Q_LORA)), _full((1, KV_LORA))],
        compiler_params=_params(1),
    )(dqf, dkf, dv, zqkv, rope, qg, kvg, wq2, wkv)


def _in_bwd_call(x, dx2, dz, dg, dqs, dkd, dvd, mod, b_ada, ng, wa, seq):
    n_tok = x.shape[0]
    tm = min(TOKEN_TILE, seq)
    per_seq = seq // tm
    n_seq = n_tok // seq

    def body(x_ref, dx2_ref, dz_ref, dg_ref, dqs_ref, dkd_ref, dvd_ref, mod_ref, bada_ref, ng_ref,
             wa_ref, gx_ref, gwa_ref, gng_ref, dshift_ref, dscale_ref):
        i = pl.program_id(0)

        @pl.when(i == 0)
        def _():
            gwa_ref[...] = jnp.zeros_like(gwa_ref)
            gng_ref[...] = jnp.zeros_like(gng_ref)

        @pl.when(i % per_seq == 0)
        def _():
            dshift_ref[...] = jnp.zeros_like(dshift_ref)
            dscale_ref[...] = jnp.zeros_like(dscale_ref)

        xv = x_ref[...]
        modv = mod_ref[0] + bada_ref[...]
        shift, scale = modv[:, :D_MODEL], modv[:, D_MODEL:2 * D_MODEL]
        ngv = ng_ref[...]
        r1 = lax.rsqrt(jnp.mean(xv * xv, axis=-1, keepdims=True) + EPS)
        xn = xv * r1
        hb = ((xn * ngv) * (1.0 + scale) + shift).astype(BF16)

        dgv = dg_ref[...]
        pieces = [(A_ZQ, dz_ref[...]), (A_GM, dgv[:, :512]), (A_QS, dqs_ref[...].astype(BF16)),
                  (A_KS, jnp.concatenate([_once(dkd_ref[...]) * LN2, _once(dvd_ref[...])], axis=1).astype(BF16)),
                  (A_GS, dgv[:, 512:])]
        dh = None
        for off, piece in pieces:
            wd = piece.shape[1]
            gwa_ref[:, off:off + wd] += _dot_tn(hb, piece)
            term = _dot_nt(piece, wa_ref[:, off:off + wd])
            dh = term if dh is None else dh + term

        dshift_ref[0] += jnp.sum(dh, axis=0, keepdims=True)
        dscale_ref[0] += jnp.sum(dh * (xn * ngv), axis=0, keepdims=True)
        gng_ref[...] += jnp.sum(dh * xn * (1.0 + scale), axis=0, keepdims=True)
        dxn = dh * ngv * (1.0 + scale)
        gx_ref[...] = dx2_ref[...] + r1 * (dxn - xn * jnp.mean(dxn * xn, axis=-1, keepdims=True))

    tok = lambda w: pl.BlockSpec((tm, w), lambda i: (i, 0))
    per_b = lambda w: pl.BlockSpec((1, 1, w), lambda i: (i // per_seq, 0, 0))
    return pl.pallas_call(
        body, name="in_bwd", grid=(n_tok // tm,),
        out_shape=[jax.ShapeDtypeStruct((n_tok, D_MODEL), F32), jax.ShapeDtypeStruct((D_MODEL, A_END), F32),
                   jax.ShapeDtypeStruct((1, D_MODEL), F32),
                   jax.ShapeDtypeStruct((n_seq, 1, D_MODEL), F32), jax.ShapeDtypeStruct((n_seq, 1, D_MODEL), F32)],
        in_specs=[tok(D_MODEL), tok(D_MODEL), tok(A_GM), tok(D_MODEL), tok(512), tok(256), tok(256),
                  per_b(3 * D_MODEL), _full(b_ada.shape), _full(ng.shape), _full(wa.shape)],
        out_specs=[tok(D_MODEL), _full((D_MODEL, A_END)), _full((1, D_MODEL)), per_b(D_MODEL), per_b(D_MODEL)],
        compiler_params=_params(1),
    )(x, dx2, dz, dg, dqs, dkd, dvd, mod, b_ada, ng, wa)


def _adam_math(w, g, m, v):
    m_new = ADAM_B1 * m + (1.0 - ADAM_B1) * g
    v_new = ADAM_B2 * v + (1.0 - ADAM_B2) * (g * g)
    m_hat = m_new / (1.0 - ADAM_B1 ** ADAM_STEP)
    v_hat = v_new / (1.0 - ADAM_B2 ** ADAM_STEP)
    delta = -ADAM_LR * (m_hat / (jnp.sqrt(v_hat) + ADAM_EPS) + ADAM_WD * w)
    return delta, m_new, v_new


def _adam_call(name, w, g, m, v):
    rows, cols = w.shape
    tr = next((t for t in (256, 128, 88) if rows % t == 0), rows)

    def body(w_ref, g_ref, m_ref, v_ref, d_ref, mo_ref, vo_ref):
        d, mn, vn = _adam_math(w_ref[...], g_ref[...], m_ref[...], v_ref[...])
        d_ref[...] = d
        mo_ref[...] = mn
        vo_ref[...] = vn

    spec = pl.BlockSpec((tr, cols), lambda i: (i, 0))
    return pl.pallas_call(
        body, name=name, grid=(rows // tr,),
        out_shape=[jax.ShapeDtypeStruct(w.shape, F32)] * 3,
        in_specs=[spec] * 4, out_specs=[spec] * 3,
        compiler_params=_params(1),
    )(w, g, m, v)


def _ada_bwd_call(act_all, dmod_cols, w, m, v):
    rows, cols = w.shape
    tr = 256

    def body(a_ref, dm_ref, w_ref, m_ref, v_ref, g_ref, d_ref, mo_ref, vo_ref):
        g = _dot_tn(a_ref[...].astype(BF16), dm_ref[...].astype(BF16))
        d, mn, vn = _adam_math(w_ref[...], g, m_ref[...], v_ref[...])
        g_ref[...] = g
        d_ref[...] = d
        mo_ref[...] = mn
        vo_ref[...] = vn

    spec = pl.BlockSpec((tr, cols), lambda i: (i, 0))
    nb = act_all.shape[0]
    return pl.pallas_call(
        body, name="ada_bwd", grid=(rows // tr,),
        out_shape=[jax.ShapeDtypeStruct(w.shape, F32)] * 4,
        in_specs=[pl.BlockSpec((nb, tr), lambda i: (0, i)), _full(dmod_cols.shape), spec, spec, spec],
        out_specs=[spec] * 4,
        compiler_params=_params(1),
    )(act_all, dmod_cols, w, m, v)


SMALL_ROW = {"norm_gain": (0, 1024), "final_gain": (1024, 2048), "q_norm_gain": (2048, 2432),
             "kv_norm_gain": (2432, 2688), "swa_sinks": (2688, 2696), "loss": (2816, 2944)}
SMALL_ORDER = ("b_ada", "norm_gain", "q_norm_gain", "kv_norm_gain", "swa_sinks", "final_gain")


def _small_call(parts_all, n_seq, params):
    k = len(params)

    def body(p_ref, *refs):
        ins, outs, loss_ref = refs[:3 * k], refs[3 * k:7 * k], refs[7 * k]
        row = p_ref[n_seq:n_seq + 1, :]
        for dv in range(1, 8):
            r0 = dv * ROWS_PER_DEVICE + n_seq
            row = row + p_ref[r0:r0 + 1, :]
        gb = None
        for dv in range(8):
            for r in range(n_seq):
                r0 = dv * ROWS_PER_DEVICE + r
                gb = p_ref[r0:r0 + 1, :] if gb is None else gb + p_ref[r0:r0 + 1, :]
        for j, name in enumerate(SMALL_ORDER):
            g = gb if name == "b_ada" else row[:, SMALL_ROW[name][0]:SMALL_ROW[name][1]]
            d, mn, vn = _adam_math(ins[3 * j][...], g, ins[3 * j + 1][...], ins[3 * j + 2][...])
            outs[4 * j][...] = g
            outs[4 * j + 1][...] = d
            outs[4 * j + 2][...] = mn
            outs[4 * j + 3][...] = vn
        loss_ref[...] = row[:, SMALL_ROW["loss"][0]:SMALL_ROW["loss"][1]]

    flat = [t for p in params for t in p]
    res = pl.pallas_call(
        body, name="small_update", grid=(1,),
        out_shape=[jax.ShapeDtypeStruct(p[0].shape, F32) for p in params for _ in range(4)]
        + [jax.ShapeDtypeStruct((1, HEAD_LANES), F32)],
        in_specs=[_full(parts_all.shape)] + [_full(t.shape) for t in flat],
        out_specs=[_full(p[0].shape) for p in params for _ in range(4)] + [_full((1, HEAD_LANES))],
        compiler_params=_params(1),
    )(parts_all, *flat)
    return [res[4 * j:4 * j + 4] for j in range(k)], res[4 * k]


def _rot(t):
    half = t.shape[-1] // 2
    return jnp.concatenate([-t[..., half:], t[..., :half]], axis=-1)


def _rot_t(g):
    half = g.shape[-1] // 2
    return jnp.concatenate([g[..., half:], -g[..., :half]], axis=-1)


def _columns(segments, lo, hi):
    out, at = [], 0
    for seg in segments:
        n = seg.shape[1]
        a, b = max(lo, at), min(hi, at + n)
        if a < b:
            out.append(seg[:, a - at:b - at])
        at += n
    return out


def _prepare_in(w_in_blocks):
    o = [0]
    for s in IN_SPLITS:
        o.append(o[-1] + s)
    part = lambda a, b: _columns(w_in_blocks, a, b)
    kr = jnp.concatenate(part(o[2], o[3]), axis=1)
    zero = jnp.zeros((kr.shape[0], 32), kr.dtype)
    return jnp.concatenate(part(0, o[2]) + [_rot(kr), zero, kr, zero] + part(o[3], o[8]), axis=1)


def _prepare_up(w_uq, w_ukv):
    uq = w_uq.reshape(Q_LORA, N_HEADS, MLA_NOPE + MLA_ROPE)
    zq = jnp.zeros((Q_LORA, N_HEADS, 32), w_uq.dtype)
    uq_full = jnp.concatenate([uq, zq], axis=-1).reshape(Q_LORA, 1024)
    uq_rot = jnp.concatenate([jnp.zeros((Q_LORA, N_HEADS, 64), w_uq.dtype), _rot(uq[..., MLA_NOPE:]), zq],
                             axis=-1).reshape(Q_LORA, 1024)
    wq2 = jnp.concatenate([uq_full, uq_rot], axis=1)
    ukv = w_ukv.reshape(KV_LORA, N_HEADS, 128)
    k_full = jnp.concatenate([ukv[..., :64], jnp.zeros((KV_LORA, N_HEADS, 64), w_ukv.dtype)], axis=-1).reshape(KV_LORA, 1024)
    wkv = jnp.concatenate([k_full, ukv[..., 64:].reshape(KV_LORA, 512)], axis=1)
    return wq2, wkv


def _restore_in(gwa):
    gkr = gwa[:, A_KR + 64:A_KR + 96] + _rot_t(gwa[:, A_KR:A_KR + 32])
    in_order = [gwa[:, :A_KR], gkr, gwa[:, A_GM:]]
    n = D_IN // 4
    return [jnp.concatenate(_columns(in_order, k * n, (k + 1) * n), axis=1) for k in range(4)]


def _restore_up(gwq2, gwkv):
    gf = gwq2[:, :1024].reshape(Q_LORA, N_HEADS, 128)
    gr = gwq2[:, 1024:].reshape(Q_LORA, N_HEADS, 128)
    g_uq = jnp.concatenate([gf[..., :64], gf[..., 64:96] + _rot_t(gr[..., 64:96])], axis=-1).reshape(Q_LORA, 768)
    gk = gwkv[:, :1024].reshape(KV_LORA, N_HEADS, 128)[..., :64]
    gv = gwkv[:, 1024:].reshape(KV_LORA, N_HEADS, 64)
    g_ukv = jnp.concatenate([gk, gv], axis=-1).reshape(KV_LORA, 1024)
    return g_uq, g_ukv


def _local_step(x, positions, target, mod_rows, b_ada, ng, qg, kvg, sinks, fg, w_in_b, later_shards):
    n_seq, seq, _ = x.shape
    n_tok = n_seq * seq
    x2d = x.reshape(n_tok, D_MODEL)
    t2d = target.reshape(n_tok, D_MODEL)
    pos_f = positions.astype(F32)
    pos_col = pos_f.reshape(n_tok, 1)
    pos_row = pos_f.reshape(n_tok // SWA_WINDOW, 1, SWA_WINDOW)
    mod3 = mod_rows.reshape(n_seq, 1, 3 * D_MODEL)
    inv = ROPE_THETA ** (-jnp.arange(0, MLA_ROPE, 2, dtype=F32) / MLA_ROPE)
    inv128 = jnp.concatenate([jnp.zeros((64,), F32), inv, inv, jnp.zeros((32,), F32)]).reshape(1, 128)
    fg2 = fg.reshape(1, D_MODEL)

    wa = _prepare_in(w_in_b)
    zqkv, zkr, gates, qs, kd, vd, f_uq, f_ukv, f_out = _pre_call(x2d, mod3, b_ada, ng, wa, later_shards, seq)
    cols = lambda t, r: jnp.transpose(t.reshape(4, r, -1), (1, 0, 2)).reshape(r, -1)
    wq2, wkv = _prepare_up(cols(f_uq, Q_LORA), cols(f_ukv, KV_LORA))
    w_out_b = f_out.reshape(D_MODEL, D_MODEL)
    qf, kf, v, rope = _up_call(zqkv, zkr, pos_col, qg, kvg, inv128, wq2, wkv, seq)
    o_mla, lse_mla = _mla_fwd_call(qf, kf, v, n_seq, seq)
    o_swa, lse_swa = _swa_fwd_call(qs, kd, vd, pos_col, pos_row, sinks, n_seq, seq)
    dx2, do, dg, g_out, g_fg, dgate, loss, delta_mla, delta_swa = _post_call(x2d, t2d, o_mla, o_swa, gates, mod3, b_ada, fg2, w_out_b, seq)
    dqf, dkf, dv = _mla_bwd_call(qf, kf, v, do, delta_mla, lse_mla, n_seq, seq)
    dqs, dkd, dvd, dsink, r_out = _swa_bwd_call(qs, kd, vd, do, delta_swa, lse_swa, pos_col, pos_row, sinks,
                                                g_out.reshape(4, 2, D_MODEL // 8, D_MODEL), n_seq, seq)
    dz, g_wq2, g_wkv, g_qg, g_kvg = _mid_bwd_call(dqf, dkf, dv, zqkv, rope, qg, kvg, wq2, wkv, seq)
    gx, g_wa, g_ng, dshift, dscale = _in_bwd_call(x2d, dx2, dz, dg, dqs, dkd, dvd, mod3, b_ada, ng, wa, seq)
    g_in = _restore_in(g_wa)
    g_uq, g_ukv = _restore_up(g_wq2, g_wkv)
    dmod = jnp.concatenate([dshift, dscale, dgate], axis=-1).reshape(n_seq, 3 * D_MODEL)
    small_row = jnp.concatenate([g_ng, g_fg, g_qg, g_kvg, jnp.pad(jnp.sum(dsink, axis=1).reshape(1, N_HEADS), ((0, 0), (0, 120))),
                                 loss, jnp.zeros((1, 128), F32)], axis=1)
    return gx.reshape(x.shape), (g_in, g_uq, g_ukv), r_out, small_row, dmod


def kernel(x, c, positions, w_ada, b_ada, norm_gain, w_in, q_norm_gain, kv_norm_gain, w_uq, w_ukv, swa_sinks, w_out, final_gain, loss_target, m_w_ada, m_b_ada, m_norm_gain, m_w_in, m_q_norm_gain, m_kv_norm_gain, m_w_uq, m_w_ukv, m_swa_sinks, m_w_out, m_final_gain, v_w_ada, v_b_ada, v_norm_gain, v_w_in, v_q_norm_gain, v_kv_norm_gain, v_w_uq, v_w_ukv, v_swa_sinks, v_w_out, v_final_gain):
    n_seq = x.shape[0]
    xi, yi, ci = lax.axis_index("x"), lax.axis_index("y"), lax.axis_index("c")
    dev = 4 * xi + 2 * yi + ci
    chip = 2 * xi + yi

    halves = lambda w: w.astype(BF16).reshape(2, w.shape[0] // 2, w.shape[1])
    c_blk = jnp.pad(c, ((0, ROWS_PER_DEVICE - n_seq), (0, 0)))
    act_all, pieces, f_in = _comm_fwd_call(c_blk, w_ada[0], [halves(w_in[0])])
    mine = lax.dynamic_slice_in_dim(pieces, dev * ROWS_PER_DEVICE, n_seq, axis=1)
    mod_rows = jnp.transpose(mine, (1, 0, 2)).reshape(n_seq, 3 * D_MODEL)
    w_in_blocks = [f_in[k].reshape(D_MODEL, -1) for k in range(4)]

    gx, (g_in_blocks, g_uq, g_ukv), r_out, small_row, dmod = _local_step(
        x, positions, loss_target, mod_rows, b_ada, norm_gain, q_norm_gain, kv_norm_gain, swa_sinks, final_gain,
        w_in_blocks, [halves(w_uq[0]), halves(w_ukv[0]), halves(w_out[0])])

    grads = [jnp.stack(g_in_blocks).reshape(4, 2, D_MODEL // 2, -1), _by_owner(g_uq, g_uq.shape[1] // 4),
             _by_owner(g_ukv, g_ukv.shape[1] // 4)]
    part = jnp.concatenate([dmod, small_row, jnp.zeros((ROWS_PER_DEVICE - n_seq - 1, 3 * D_MODEL), F32)], axis=0)
    r_in, r_uq, r_ukv, parts_all = _comm_bwd_call(grads, part)
    g_in_s, g_uq_s = r_in.reshape(w_in.shape[1:]), r_uq.reshape(w_uq.shape[1:])
    g_ukv_s, g_out_s = r_ukv.reshape(w_ukv.shape[1:]), r_out.reshape(w_out.shape[1:])

    tr = lambda a: jnp.swapaxes(a[0], 0, 1)
    back = lambda ts: [jnp.swapaxes(t, 0, 1) for t in ts]
    d_in, nm_in, nv_in = back(_adam_call("adam_w_in", tr(w_in), g_in_s.T, tr(m_w_in), tr(v_w_in)))
    d_uq, nm_uq, nv_uq = back(_adam_call("adam_w_uq", tr(w_uq), g_uq_s.T, tr(m_w_uq), tr(v_w_uq)))
    d_ukv, nm_ukv, nv_ukv = _adam_call("adam_w_ukv", w_ukv[0], g_ukv_s, m_w_ukv[0], v_w_ukv[0])
    d_out, nm_out, nv_out = _adam_call("adam_w_out", w_out[0], g_out_s, m_w_out[0], v_w_out[0])
    dmod_cols = lax.dynamic_slice_in_dim(parts_all, chip * 768, 768, axis=1)
    g_ada, d_ada, nm_ada, nv_ada = _ada_bwd_call(act_all, dmod_cols, w_ada[0], m_w_ada[0], v_w_ada[0])

    row = lambda t: t.reshape(1, -1)
    small = {"b_ada": (b_ada, m_b_ada, v_b_ada), "norm_gain": (norm_gain, m_norm_gain, v_norm_gain),
             "q_norm_gain": (q_norm_gain, m_q_norm_gain, v_q_norm_gain),
             "kv_norm_gain": (kv_norm_gain, m_kv_norm_gain, v_kv_norm_gain),
             "swa_sinks": (swa_sinks, m_swa_sinks, v_swa_sinks),
             "final_gain": (row(final_gain), row(m_final_gain), row(v_final_gain))}
    res, loss_row = _small_call(parts_all, n_seq, [small[name] for name in SMALL_ORDER])
    res = dict(zip(SMALL_ORDER, res))
    res["final_gain"] = [t.reshape(-1) for t in res["final_gain"]]
    e = lambda t: t[None]
    big = {"w_ada": (e(g_ada), e(d_ada), e(nm_ada), e(nv_ada)), "w_in": (e(g_in_s), e(d_in), e(nm_in), e(nv_in)),
           "w_uq": (e(g_uq_s), e(d_uq), e(nm_uq), e(nv_uq)), "w_ukv": (e(g_ukv_s), e(d_ukv), e(nm_ukv), e(nv_ukv)),
           "w_out": (e(g_out_s), e(d_out), e(nm_out), e(nv_out))}
    order = ("w_ada", "b_ada", "norm_gain", "w_in", "q_norm_gain", "kv_norm_gain", "w_uq", "w_ukv", "swa_sinks", "w_out",
             "final_gain")
    pick = lambda kind: [(big[n] if n in big else res[n])[kind] for n in order]
    return (loss_row[0, 0], gx, *pick(0), *pick(1), *pick(2), *pick(3))
```

```python
import jax
import jax.numpy as jnp
from jax import lax
from jax.experimental import pallas as pl
from jax.experimental.pallas import tpu as pltpu

F32 = jnp.float32
BF16 = jnp.bfloat16

D_MODEL = 1024
Q_LORA = 384
KV_LORA = 256
N_HEADS = 8
MLA_NOPE = 64
MLA_ROPE = 32
HEAD_LANES = 128
HALF = 64
SWA_WINDOW = 128
EPS = 1e-6
ROPE_THETA = 10000.0
MLA_SCALE = (MLA_NOPE + MLA_ROPE) ** -0.5
LOG2E = 1.4426950408889634
LN2 = 0.6931471805599453
SWA_SCALE = 64 ** -0.5
NEG = -1e30

ADAM_LR = 0.001
ADAM_B1 = 0.9
ADAM_B2 = 0.999
ADAM_EPS = 1e-08
ADAM_WD = 0.01
ADAM_STEP = 10

A_ZQ, A_ZKV, A_KR, A_GM, A_QS, A_KS, A_VS, A_GS, A_END = 0, 384, 640, 768, 1280, 1792, 1920, 2048, 2560
IN_SPLITS = (384, 256, 32, 512, 512, 128, 128, 512)
D_IN = sum(IN_SPLITS)

TOKEN_TILE = 512
ATT_TILE = 256
VMEM_LIMIT = 56 * 1024 * 1024


def _dot(a, b):
    return jnp.dot(a, b, preferred_element_type=F32)


def _dot_nt(a, b):
    return lax.dot_general(a, b, (((1,), (1,)), ((), ())), preferred_element_type=F32)


def _dot_tn(a, b):
    return lax.dot_general(a, b, (((0,), (0,)), ((), ())), preferred_element_type=F32)


def _params(n_grid):
    return pltpu.CompilerParams(dimension_semantics=("arbitrary",) * n_grid, vmem_limit_bytes=VMEM_LIMIT)


def _full(shape):
    nd = len(shape)
    return pl.BlockSpec(shape, lambda *_: (0,) * nd, pipeline_mode=pl.Buffered(1))


def _sigmoid(g):
    return 1.0 / (1.0 + jnp.exp(-g))


SUB_TILE = 256


def _sub_tiles(tm):
    sub = min(SUB_TILE, tm)
    return [slice(s * sub, (s + 1) * sub) for s in range(tm // sub)]


MESH = pl.DeviceIdType.MESH
ROWS_PER_DEVICE = 8
VMEM_SPEC = pl.BlockSpec(memory_space=pltpu.VMEM)
ANY_SPEC = pl.BlockSpec(memory_space=pl.ANY)


def _position():
    x, y, c = lax.axis_index("x"), lax.axis_index("y"), lax.axis_index("c")
    sibling = (x, y, 1 - c)
    others = [(1 - x, y, c), (x, 1 - y, c), (1 - x, 1 - y, c)]
    return (x, y, c), 4 * x + 2 * y + c, 2 * x + y, sibling, others


def _rows_of(dev):
    return pl.ds(pl.multiple_of(dev * ROWS_PER_DEVICE, ROWS_PER_DEVICE), ROWS_PER_DEVICE)


def _all_to_all_rows(block_ref, table_ref, dev, me, send_sems, recv_sems):
    x, y, c = me
    waits = []
    for k in range(1, 8):
        peer = (1 - x if k & 4 else x, 1 - y if k & 2 else y, 1 - c if k & 1 else c)
        pltpu.make_async_remote_copy(src_ref=block_ref, dst_ref=table_ref.at[_rows_of(dev)], send_sem=send_sems.at[k - 1],
                                     recv_sem=recv_sems.at[k - 1], device_id=peer, device_id_type=MESH).start()
        waits.append(pltpu.make_async_remote_copy(
            src_ref=block_ref, dst_ref=table_ref.at[_rows_of(jnp.bitwise_xor(dev, k))], send_sem=send_sems.at[k - 1],
            recv_sem=recv_sems.at[k - 1], device_id=peer, device_id_type=MESH))
    return waits


def _comm_fwd_call(c_blk, w_ada, shards):
    n = len(shards)

    def body(c_ref, wada_ref, *refs):
        w_refs, act_ref, pieces_ref, full_refs = refs[:n], refs[n], refs[n + 1], refs[n + 2:2 * n + 2]
        c_all_ref = refs[2 * n + 2]
        c_send, c_recv, p_send, p_recv, w_send, w_recv, f_send, f_recv, loc_sem = refs[2 * n + 3:]
        me, dev, chip, sibling, others = _position()
        core = me[2]
        chip_of = [2 * p[0] + p[1] for p in others]

        local = [pltpu.make_async_copy(w_refs[i], full_refs[i].at[chip], loc_sem.at[i]) for i in range(n)]
        for cp in local:
            cp.start()

        def over_ici(i, j, src_chip):
            return pltpu.make_async_remote_copy(
                src_ref=w_refs[i].at[core], dst_ref=full_refs[i].at[src_chip, core], send_sem=w_send.at[3 * i + j],
                recv_sem=w_recv.at[3 * i + j], device_id=others[j], device_id_type=MESH)

        def to_sibling(i, j, half):
            return pltpu.make_async_remote_copy(
                src_ref=full_refs[i].at[chip_of[j], half], dst_ref=full_refs[i].at[chip_of[j], half],
                send_sem=f_send.at[3 * i + j], recv_sem=f_recv.at[3 * i + j], device_id=sibling, device_id_type=MESH)

        c_all_ref[_rows_of(dev), :] = c_ref[...]
        c_waits = _all_to_all_rows(c_ref, c_all_ref, dev, me, c_send, c_recv)
        sent = [over_ici(i, j, chip) for i in range(n) for j in range(3)]
        for cp in sent:
            cp.start()

        for cp in c_waits:
            cp.wait()
        cv = c_all_ref[...]
        act = cv * _sigmoid(cv)
        act_ref[...] = act
        pieces_ref[chip] = _dot(act.astype(BF16), wada_ref[...].astype(BF16))
        piece = lambda j, src_chip: pltpu.make_async_remote_copy(
            src_ref=pieces_ref.at[chip], dst_ref=pieces_ref.at[src_chip], send_sem=p_send.at[j], recv_sem=p_recv.at[j],
            device_id=others[j], device_id_type=MESH)
        for j in range(3):
            piece(j, chip).start()

        for i in range(n):
            for j in range(3):
                over_ici(i, j, chip_of[j]).wait_recv()
                to_sibling(i, j, core).start()
        for j in range(3):
            piece(j, chip).wait_send()
            piece(j, chip_of[j]).wait_recv()
        for i in range(n):
            for j in range(3):
                to_sibling(i, j, 1 - core).wait_recv()
                to_sibling(i, j, core).wait_send()
        for cp in sent:
            cp.wait_send()
        for cp in local:
            cp.wait()

    rows = 8 * ROWS_PER_DEVICE
    dma = pltpu.SemaphoreType.DMA
    return pl.pallas_call(
        body, name="comm_fwd",
        out_shape=[jax.ShapeDtypeStruct((rows, D_MODEL), F32), jax.ShapeDtypeStruct((4, rows, w_ada.shape[1]), F32)]
        + [jax.ShapeDtypeStruct((4,) + s.shape, s.dtype) for s in shards],
        in_specs=[VMEM_SPEC, VMEM_SPEC] + [ANY_SPEC] * n,
        out_specs=[VMEM_SPEC, VMEM_SPEC] + [ANY_SPEC] * n,
        scratch_shapes=[pltpu.VMEM((rows, D_MODEL), F32), dma((7,)), dma((7,)), dma((3,)), dma((3,)),
                        dma((3 * n,)), dma((3 * n,)), dma((3 * n,)), dma((3 * n,)), dma((n,))],
        compiler_params=pltpu.CompilerParams(vmem_limit_bytes=VMEM_LIMIT),
    )(c_blk, w_ada, *shards)


def _comm_bwd_call(grads, part):
    n = len(grads)

    def body(part_ref, *refs):
        g_refs, f_refs, parts_ref = refs[:n], refs[n:2 * n], refs[2 * n]
        scratch = refs[2 * n + 1:]
        a_refs, b_refs, p_refs, r_refs = (scratch[k * n:(k + 1) * n] for k in range(4))
        s_send, s_recv, d_send, d_recv, e_send, e_recv, h_send, h_recv, loc_sem = scratch[4 * n:]
        me, dev, chip, sibling, others = _position()
        core = me[2]
        chip_of = [2 * p[0] + p[1] for p in others]

        parts_ref[_rows_of(dev), :] = part_ref[...]
        s_waits = _all_to_all_rows(part_ref, parts_ref, dev, me, s_send, s_recv)

        mine = [pltpu.make_async_copy(g_refs[i].at[:, core], a_refs[i], loc_sem.at[i]) for i in range(n)]
        swap = [pltpu.make_async_remote_copy(src_ref=g_refs[i].at[:, 1 - core], dst_ref=b_refs[i], send_sem=d_send.at[i],
                                             recv_sem=d_recv.at[i], device_id=sibling, device_id_type=MESH) for i in range(n)]
        order = sorted(range(n), key=lambda i: g_refs[i].shape[2] * g_refs[i].shape[3])
        for i in order:
            mine[i].start()
            swap[i].start()
        cross = [pltpu.make_async_remote_copy(src_ref=p_refs[i].at[chip_of[j]], dst_ref=r_refs[i].at[j],
                                              send_sem=e_send.at[3 * i + j], recv_sem=e_recv.at[3 * i + j],
                                              device_id=others[j], device_id_type=MESH) for i in range(n) for j in range(3)]
        for i in order:
            mine[i].wait()
            swap[i].wait()
            for k in range(4):
                s = a_refs[i][k] + b_refs[i][k]
                a_refs[i][k] = s
                p_refs[i][k] = s.astype(BF16)
            for j in range(3):
                cross[3 * i + j].start()
        share = {}
        for i in order:
            for j in range(3):
                cross[3 * i + j].wait()
            f_refs[i][core] = (a_refs[i][chip] + r_refs[i][0].astype(F32) + r_refs[i][1].astype(F32)
                               + r_refs[i][2].astype(F32))
            share[i] = pltpu.make_async_remote_copy(src_ref=f_refs[i].at[core], dst_ref=f_refs[i].at[core],
                                                    send_sem=h_send.at[i], recv_sem=h_recv.at[i], device_id=sibling,
                                                    device_id_type=MESH)
            share[i].start()
        for i in range(n):
            share[i].wait_send()
            pltpu.make_async_remote_copy(src_ref=f_refs[i].at[core], dst_ref=f_refs[i].at[1 - core], send_sem=h_send.at[i],
                                         recv_sem=h_recv.at[i], device_id=sibling, device_id_type=MESH).wait_recv()
        for cp in s_waits:
            cp.wait()

    rows = 8 * ROWS_PER_DEVICE
    dma = pltpu.SemaphoreType.DMA
    quarter = [(4,) + g.shape[2:] for g in grads]
    return pl.pallas_call(
        body, name="comm_bwd",
        out_shape=[jax.ShapeDtypeStruct((2,) + g.shape[2:], F32) for g in grads]
        + [jax.ShapeDtypeStruct((rows, part.shape[1]), F32)],
        in_specs=[VMEM_SPEC] + [ANY_SPEC] * n,
        out_specs=[VMEM_SPEC] * (n + 1),
        scratch_shapes=[pltpu.VMEM(q, F32) for q in quarter] + [pltpu.VMEM(q, F32) for q in quarter]
        + [pltpu.VMEM(q, BF16) for q in quarter] + [pltpu.VMEM((3,) + q[1:], BF16) for q in quarter]
        + [dma((7,)), dma((7,)), dma((n,)), dma((n,)), dma((3 * n,)), dma((3 * n,)), dma((n,)), dma((n,)), dma((n,))],
        compiler_params=pltpu.CompilerParams(vmem_limit_bytes=VMEM_LIMIT),
    )(part, *grads)


def _by_owner(g, n):
    return jnp.transpose(g.reshape(g.shape[0], 4, n), (1, 0, 2)).reshape(4, 2, g.shape[0] // 2, n)


def _reduce_operands(g):
    quarter = (4,) + g.shape[2:]
    dma = pltpu.SemaphoreType.DMA
    scratch = [pltpu.VMEM(quarter, F32), pltpu.VMEM(quarter, F32), pltpu.VMEM(quarter, BF16),
               pltpu.VMEM((3,) + quarter[1:], BF16), dma((5,)), dma((5,)), dma((2,))]
    return jax.ShapeDtypeStruct((2,) + g.shape[2:], F32), scratch


def _grad_reduce(step, n_steps, g_ref, f_ref, a_ref, b_ref, p_ref, r_ref, send, recv, loc_sem):
    me, _, chip, sibling, others = _position()
    core = me[2]
    chip_of = [2 * p[0] + p[1] for p in others]
    remote = lambda src, dst, k, to: pltpu.make_async_remote_copy(
        src_ref=src, dst_ref=dst, send_sem=send.at[k], recv_sem=recv.at[k], device_id=to, device_id_type=MESH)
    mine = pltpu.make_async_copy(g_ref.at[:, core], a_ref, loc_sem.at[0])
    swap = remote(g_ref.at[:, 1 - core], b_ref, 0, sibling)
    cross = [remote(p_ref.at[chip_of[j]], r_ref.at[j], 1 + j, others[j]) for j in range(3)]
    total_ref = b_ref.at[0]
    keep = pltpu.make_async_copy(total_ref, f_ref.at[core], loc_sem.at[1])
    share = lambda half: remote(total_ref, f_ref.at[half], 4, sibling)
    at = [k * (n_steps - 1) // 3 for k in range(4)]

    @pl.when(step == at[0])
    def _():
        mine.start()
        swap.start()

    @pl.when(step == at[1])
    def _():
        mine.wait()
        swap.wait()
        for k in range(4):
            s = a_ref[k] + b_ref[k]
            a_ref[k] = s
            p_ref[k] = s.astype(BF16)
        for cp in cross:
            cp.start()

    @pl.when(step == at[2])
    def _():
        for cp in cross:
            cp.wait()
        total_ref[...] = a_ref[chip] + r_ref[0].astype(F32) + r_ref[1].astype(F32) + r_ref[2].astype(F32)
        keep.start()
        share(core).start()

    @pl.when(step == at[3])
    def _():
        keep.wait()
        share(core).wait_send()
        share(1 - core).wait_recv()


def _twice(t):
    lo = _lane_lo()
    other = pltpu.roll(t, HALF, 1)
    return jnp.concatenate([jnp.where(lo, t, other), jnp.where(lo, other, t)], axis=1)


def _once(g):
    first, second = g[:, :HEAD_LANES], g[:, HEAD_LANES:]
    return jnp.where(_lane_lo(), first + pltpu.roll(first, HALF, 1), second + pltpu.roll(second, HALF, 1))


def _rope_tables(pos_col, inv_row):
    ang = pos_col * inv_row
    return jnp.cos(ang), jnp.sin(ang)


def _gather_in_steps(step, n_steps, w_refs, full_refs, w_send, w_recv, f_send, f_recv, loc_sem):
    me, _, chip, sibling, others = _position()
    core = me[2]
    chip_of = [2 * p[0] + p[1] for p in others]
    n = len(w_refs)
    local = [pltpu.make_async_copy(w_refs[i], full_refs[i].at[chip], loc_sem.at[i]) for i in range(n)]

    def over_ici(i, j, src_chip):
        return pltpu.make_async_remote_copy(
            src_ref=w_refs[i].at[core], dst_ref=full_refs[i].at[src_chip, core], send_sem=w_send.at[3 * i + j],
            recv_sem=w_recv.at[3 * i + j], device_id=others[j], device_id_type=MESH)

    def to_sibling(i, j, half):
        return pltpu.make_async_remote_copy(
            src_ref=full_refs[i].at[chip_of[j], half], dst_ref=full_refs[i].at[chip_of[j], half],
            send_sem=f_send.at[3 * i + j], recv_sem=f_recv.at[3 * i + j], device_id=sibling, device_id_type=MESH)

    pairs = [(i, j) for i in range(n) for j in range(3)]

    @pl.when(step == 0)
    def _():
        for cp in local:
            cp.start()
        for i, j in pairs:
            over_ici(i, j, chip).start()

    @pl.when(step == 3 * n_steps // 4)
    def _():
        for i, j in pairs:
            over_ici(i, j, chip_of[j]).wait_recv()
            to_sibling(i, j, core).start()

    @pl.when(step == n_steps - 1)
    def _():
        for i, j in pairs:
            to_sibling(i, j, 1 - core).wait_recv()
            to_sibling(i, j, core).wait_send()
            over_ici(i, j, chip).wait_send()
        for cp in local:
            cp.wait()


def _pre_call(x, pos_col, mod, b_ada, ng, inv128, wa, shards, seq):
    n_tok = x.shape[0]
    tm = min(TOKEN_TILE, seq)
    per_seq = seq // tm
    n_steps = n_tok // tm
    n = len(shards)

    def body(x_ref, pos_ref, mod_ref, bada_ref, ng_ref, inv_ref, wa_ref, *refs):
        w_refs, refs = refs[:n], refs[n:]
        zqkv_ref, zkr_ref, gates_ref, qs_ref, kd_ref, vd_ref, rope_ref = refs[:7]
        full_refs, sems = refs[7:7 + n], refs[7 + n:]
        _gather_in_steps(pl.program_id(0), n_steps, w_refs, full_refs, *sems)
        cos, sin = _rope_tables(pos_ref[...], inv_ref[...])
        rope_ref[:, :HEAD_LANES] = cos
        rope_ref[:, HEAD_LANES:] = sin
        xv = x_ref[...]
        modv = mod_ref[0] + bada_ref[...]
        shift, scale = modv[:, :D_MODEL], modv[:, D_MODEL:2 * D_MODEL]
        r1 = lax.rsqrt(jnp.mean(xv * xv, axis=-1, keepdims=True) + EPS)
        h = ((xv * r1) * ng_ref[...]) * (1.0 + scale) + shift
        za = _dot(h.astype(BF16), wa_ref[...])
        zqkv_ref[...] = za[:, :A_KR]
        zkr_ref[...] = za[:, A_KR:A_GM]
        gates_ref[:, :512] = za[:, A_GM:A_QS]
        gates_ref[:, 512:] = za[:, A_GS:A_END]
        qs_ref[...] = (za[:, A_QS:A_KS] * (SWA_SCALE * LOG2E)).astype(BF16)
        kd_ref[...] = _twice(za[:, A_KS:A_VS]).astype(BF16)
        vd_ref[...] = _twice(za[:, A_VS:A_GS]).astype(BF16)

    tok = lambda w: pl.BlockSpec((tm, w), lambda i: (i, 0))
    outs = [(640, F32), (HEAD_LANES, F32), (1024, F32), (512, BF16), (256, BF16), (256, BF16), (2 * HEAD_LANES, F32)]
    dma = pltpu.SemaphoreType.DMA
    return pl.pallas_call(
        body, name="pre", grid=(n_steps,),
        out_shape=[jax.ShapeDtypeStruct((n_tok, w), dt) for w, dt in outs]
        + [jax.ShapeDtypeStruct((4,) + s.shape, s.dtype) for s in shards],
        in_specs=[tok(D_MODEL), tok(1), pl.BlockSpec((1, 1, 3 * D_MODEL), lambda i: (i // per_seq, 0, 0)),
                  _full(b_ada.shape), _full(ng.shape), _full(inv128.shape), _full(wa.shape)] + [ANY_SPEC] * n,
        out_specs=[tok(w) for w, _ in outs] + [ANY_SPEC] * n,
        scratch_shapes=[dma((3 * n,)), dma((3 * n,)), dma((3 * n,)), dma((3 * n,)), dma((n,))],
        compiler_params=_params(1),
    )(x, pos_col, mod, b_ada, ng, inv128, wa, *shards)


def _up_call(zqkv, zkr, rope, qg, kvg, wq2, wkv, seq):
    n_tok = zqkv.shape[0]
    tm = min(TOKEN_TILE, seq)

    def body(zqkv_ref, zkr_ref, rope_ref, qg_ref, kvg_ref, wq_ref, wkv_ref, qf_ref, kf_ref, v_ref):
        cos, sin = rope_ref[:, :HEAD_LANES], rope_ref[:, HEAD_LANES:]
        zq, zkv = zqkv_ref[:, A_ZQ:A_ZKV], zqkv_ref[:, A_ZKV:A_KR]
        rq = lax.rsqrt(jnp.mean(zq * zq, axis=-1, keepdims=True) + EPS)
        qn = ((zq * rq) * qg_ref[...]).astype(BF16)
        qr = _dot(qn, wq_ref[...])
        cf, sf = jnp.tile(cos, (1, N_HEADS)), jnp.tile(sin, (1, N_HEADS))
        qf_ref[...] = ((qr[:, :1024] * cf + qr[:, 1024:] * sf) * (MLA_SCALE * LOG2E)).astype(BF16)
        rkv = lax.rsqrt(jnp.mean(zkv * zkv, axis=-1, keepdims=True) + EPS)
        kvn = ((zkv * rkv) * kvg_ref[...]).astype(BF16)
        kv = _dot(kvn, wkv_ref[...])
        zkr = zkr_ref[...]
        kpe = jnp.where(_lane_lo(), 0.0, zkr * cos) + pltpu.roll(zkr, HALF, 1) * sin
        kf_ref[...] = (kv[:, :1024] + jnp.tile(kpe, (1, N_HEADS))).astype(BF16)
        v_ref[...] = kv[:, 1024:].astype(BF16)

    tok = lambda w: pl.BlockSpec((tm, w), lambda i: (i, 0))
    outs = [(1024, BF16), (1024, BF16), (512, BF16)]
    return pl.pallas_call(
        body, name="up", grid=(n_tok // tm,),
        out_shape=[jax.ShapeDtypeStruct((n_tok, w), dt) for w, dt in outs],
        in_specs=[tok(640), tok(HEAD_LANES), tok(2 * HEAD_LANES), _full(qg.shape), _full(kvg.shape), _full(wq2.shape),
                  _full(wkv.shape)],
        out_specs=[tok(w) for w, _ in outs],
        compiler_params=_params(1),
    )(zqkv, zkr, rope, qg, kvg, wq2, wkv)


def _lane_lo(width=HEAD_LANES):
    return lax.broadcasted_iota(jnp.int32, (1, width), 1) < HALF


def _eye(n=HEAD_LANES):
    r = lax.broadcasted_iota(jnp.int32, (n, n), 0)
    c = lax.broadcasted_iota(jnp.int32, (n, n), 1)
    return jnp.where(r == c, 1.0, 0.0).astype(BF16)


def _mla_fwd_call(qf, kf, v, n_seq, seq):
    tq = min(ATT_TILE, seq)
    nq = seq // tq

    ext = HALF + 16

    def body(q_ref, k_ref, v_ref, o_ref, lse_ref, vt_ref, acc_ref):
        i = pl.program_id(1)
        eye = _eye()

        @pl.when(i == 0)
        def _():
            for h in range(N_HEADS):
                vt_ref[h * ext + HALF:(h + 1) * ext, :] = jnp.ones((16, seq), BF16)
            for t in range(nq):
                for p in range(N_HEADS // 2):
                    pair = slice(p * HEAD_LANES, (p + 1) * HEAD_LANES)
                    v_t = _dot_nt(eye, v_ref[t * tq:(t + 1) * tq, pair]).astype(BF16)
                    for hh in range(2):
                        r0 = (2 * p + hh) * ext
                        vt_ref[r0:r0 + HALF, t * tq:(t + 1) * tq] = v_t[hh * HALF:(hh + 1) * HALF, :]

        q = q_ref[...]
        qcol = i * tq + lax.broadcasted_iota(jnp.int32, (1, tq), 1)
        heads = range(N_HEADS)
        lanes = [slice(h * HEAD_LANES, (h + 1) * HEAD_LANES) for h in heads]

        def make_step(masked, n_tiles):
            def step(kt0, carry):
                tiles = range(n_tiles)
                start = pl.multiple_of(kt0 * tq, tq)
                ks = [k_ref[pl.ds(pl.multiple_of((kt0 + t) * tq, tq), tq), :] for t in tiles]
                vt = vt_ref[:, pl.ds(start, n_tiles * tq)]
                last = n_tiles - 1
                if masked:
                    keep = ((kt0 + last) * tq + lax.broadcasted_iota(jnp.int32, (tq, 1), 0)) <= qcol

                def scores(h):
                    sts = [_dot_nt(ks[t][:, lanes[h]], q[:, lanes[h]]) for t in tiles]
                    if masked:
                        sts[last] = jnp.where(keep, sts[last], NEG)
                    return sts

                def softmax(h, sts):
                    m_old = carry[h]
                    m_new = m_old
                    for st in sts:
                        m_new = jnp.maximum(m_new, jnp.max(st, axis=0, keepdims=True))
                    pt = jnp.concatenate([jnp.exp2(st - m_new).astype(BF16) for st in sts], axis=0)
                    return m_new, jnp.exp2(m_old - m_new), pt

                def values(h, alpha, pt):
                    rows = slice(h * ext, (h + 1) * ext)
                    acc_ref[rows, :] = acc_ref[rows, :] * alpha + _dot(vt[rows, :], pt)

                sts, soft, out = {0: scores(0), 1: scores(1)}, {}, {}
                for h in range(N_HEADS + 1):
                    if h + 2 < N_HEADS:
                        sts[h + 2] = scores(h + 2)
                    if h < N_HEADS:
                        soft[h] = softmax(h, sts.pop(h))
                    if h >= 1:
                        m_new, alpha, pt = soft.pop(h - 1)
                        values(h - 1, alpha, pt)
                        out[h - 1] = m_new
                return tuple(out[h] for h in heads)
            return step

        acc_ref[...] = jnp.zeros_like(acc_ref)
        init = (jnp.full((1, tq), NEG, F32),) * N_HEADS
        count = i + 1
        carry = lax.fori_loop(0, (count + 1) // 2 - 1, lambda j, c: make_step(False, 2)(2 * j, c), init)
        carry = lax.cond(count % 2 == 0, lambda c: make_step(True, 2)(i - 1, c), lambda c: make_step(True, 1)(i, c), carry)
        dens = [acc_ref[h * ext + HALF:h * ext + HALF + 1, :] for h in heads]
        acc_t = jnp.concatenate([acc_ref[h * ext:h * ext + HALF, :] * (1.0 / dens[h]) for h in heads], axis=0)
        o_ref[...] = acc_t.T
        for h in heads:
            lse_ref[0, h // 4, h % 4:h % 4 + 1, :] = carry[h] + jnp.log2(dens[h])

    n_tok = qf.shape[0]
    return pl.pallas_call(
        body, name="mla_fwd", grid=(n_seq, nq),
        out_shape=[jax.ShapeDtypeStruct((n_tok, 512), F32), jax.ShapeDtypeStruct((n_seq, 2, 4, seq), F32)],
        in_specs=[pl.BlockSpec((tq, 1024), lambda b, i: (b * nq + i, 0)),
                  pl.BlockSpec((seq, 1024), lambda b, i: (b, 0)),
                  pl.BlockSpec((seq, 512), lambda b, i: (b, 0))],
        out_specs=[pl.BlockSpec((tq, 512), lambda b, i: (b * nq + i, 0)),
                   pl.BlockSpec((1, 2, 4, tq), lambda b, i: (b, 0, 0, i))],
        scratch_shapes=[pltpu.VMEM((N_HEADS * ext, seq), BF16), pltpu.VMEM((N_HEADS * ext, tq), F32)],
        compiler_params=_params(2),
    )(qf, kf, v)


def _mla_bwd_call(qf, kf, v, do, delta, lse, n_seq, seq):
    tq = min(ATT_TILE, seq)
    nq = seq // tq

    nh = 4
    heads = range(nh)
    lanes = [slice(h * HEAD_LANES, (h + 1) * HEAD_LANES) for h in heads]

    def body(q_ref, k_ref, v_ref, do_ref, dl_ref, lse_ref, dq_ref, dk_ref, dv_ref,
             kt_ref, dot_ref, dqt_ref, dvt_ref):
        eye = _eye()
        sub_lo = lax.broadcasted_iota(jnp.int32, (HEAD_LANES, 1), 0) < HALF

        for t in range(nq):
            r = slice(t * tq, (t + 1) * tq)
            kv = k_ref[r, :]
            for h in heads:
                kt_ref[lanes[h], r] = _dot_nt(eye, kv[:, lanes[h]]).astype(BF16)
            for p in range(nh // 2):
                dov = do_ref[r, lanes[p]]
                dt = _dot_nt(eye, dov)
                dot_ref[2 * p, :, r] = jnp.where(sub_lo, dt, 0.0).astype(BF16)
                dot_ref[2 * p + 1, :, r] = jnp.where(sub_lo, 0.0, dt).astype(BF16)
        dqt_ref[...] = jnp.zeros_like(dqt_ref)
        dvt_ref[...] = jnp.zeros_like(dvt_ref)

        def flush_dv(tile, which):
            rows = pl.ds(pl.multiple_of(tile * tq, tq), tq)
            for p in range(nh // 2):
                dv_ref[rows, lanes[p]] = dvt_ref[which, p * HEAD_LANES:(p + 1) * HEAD_LANES, :].T

        def k_step(kt, _):
            slot = kt % 2
            kr = pl.ds(pl.multiple_of(kt * tq, tq), tq)
            k = k_ref[kr, :]
            vv = v_ref[kr, :]
            k_t = kt_ref[:, kr]
            krow = kt * tq + lax.broadcasted_iota(jnp.int32, (tq, 1), 0)

            def make_step(masked, n_tiles):
                def q_step(qt0, carry):
                    tiles = range(n_tiles)
                    qrs = [pl.ds(pl.multiple_of((qt0 + t) * tq, tq), tq) for t in tiles]
                    if masked:
                        flush_dv(jnp.maximum(kt - 1, 0), 1 - slot)
                    qs = [q_ref[qr, :] for qr in qrs]
                    if masked:
                        keep = krow <= (qt0 * tq + lax.broadcasted_iota(jnp.int32, (1, tq), 1))

                    def scores(h):
                        do_ts = [dot_ref[h, :, qr] for qr in qrs]
                        sts = [_dot_nt(k[:, lanes[h]], qs[t][:, lanes[h]]) for t in tiles]
                        dpts = [_dot(vv[:, lanes[h // 2]], do_ts[t]) for t in tiles]
                        return do_ts, sts, dpts

                    def softmax(h, sts, dpts):
                        pts, dsts = [], []
                        for t in tiles:
                            pt = jnp.exp2(sts[t] - lse_ref[0, 0, h:h + 1, qrs[t]])
                            if masked and t == 0:
                                pt = jnp.where(keep, pt, 0.0)
                            dsts.append((pt * (dpts[t] - dl_ref[0, h:h + 1, qrs[t]])).astype(BF16))
                            pts.append(pt.astype(BF16))
                        return pts, dsts

                    def grads(h, do_ts, pts, dsts):
                        half = slice((h % 2) * HALF, (h % 2 + 1) * HALF)
                        dst_all = jnp.concatenate(dsts, axis=1)
                        pt_all = jnp.concatenate(pts, axis=1)
                        do_all = jnp.concatenate([do_ts[t][half, :] for t in tiles], axis=1)
                        q_all = jnp.concatenate([qs[t][:, lanes[h]] for t in tiles], axis=0)
                        dvt_ref[slot, h * HALF:(h + 1) * HALF, :] += _dot_nt(do_all, pt_all)
                        dk_ref[kr, lanes[h]] += _dot(dst_all, q_all)
                        for t in tiles:
                            dqt_ref[lanes[h], qrs[t]] += _dot(k_t[lanes[h], :], dsts[t])

                    first, second = {0: scores(0)}, {}
                    for h in range(nh + 1):
                        if h + 1 < nh:
                            first[h + 1] = scores(h + 1)
                        if h < nh:
                            do_ts, sts, dpts = first.pop(h)
                            second[h] = (do_ts,) + softmax(h, sts, dpts)
                        if h >= 1:
                            grads(h - 1, *second.pop(h - 1))
                    return carry
                return q_step

            dk_ref[kr, :] = jnp.zeros((tq, nh * HEAD_LANES), F32)
            dvt_ref[slot] = jnp.zeros(dvt_ref.shape[1:], F32)
            count = nq - kt
            lax.cond(count >= 2, lambda c: make_step(True, 2)(kt, c), lambda c: make_step(True, 1)(kt, c), 0)
            lax.fori_loop(1, count // 2, lambda j, c: make_step(False, 2)(kt + 2 * j, c), 0)
            lax.cond(jnp.logical_and(count % 2 == 1, count >= 3), lambda c: make_step(False, 1)(nq - 1, c), lambda c: c, 0)
            return 0

        lax.fori_loop(0, nq, k_step, 0)
        flush_dv(nq - 1, (nq - 1) % 2)
        for t in range(nq):
            r = slice(t * tq, (t + 1) * tq)
            for h in heads:
                dq_ref[r, lanes[h]] = dqt_ref[lanes[h], r].T

    n_tok = qf.shape[0]
    groups = N_HEADS // nh
    blk = lambda w: pl.BlockSpec((seq, w), lambda b, g: (b, g))
    return pl.pallas_call(
        body, name="mla_bwd", grid=(n_seq, groups),
        out_shape=[jax.ShapeDtypeStruct((n_tok, 1024), F32), jax.ShapeDtypeStruct((n_tok, 1024), F32),
                   jax.ShapeDtypeStruct((n_tok, 512), F32)],
        in_specs=[blk(512), blk(512), blk(256), blk(256), pl.BlockSpec((1, nh, seq), lambda b, g: (g, 0, b)),
                  pl.BlockSpec((1, 1, nh, seq), lambda b, g: (b, g, 0, 0))],
        out_specs=[blk(512), blk(512), blk(256)],
        scratch_shapes=[pltpu.VMEM((nh * HEAD_LANES, seq), BF16), pltpu.VMEM((nh, HEAD_LANES, seq), BF16),
                        pltpu.VMEM((nh * HEAD_LANES, seq), F32), pltpu.VMEM((2, nh * HALF, tq), F32)],
        compiler_params=_params(2),
    )(qf, kf, v, do, delta, lse)


SWA_BLOCKS = 4


def _swa_block(n, pos_col_ref, posq):
    w = SWA_WINDOW
    start = pl.multiple_of(jnp.maximum(n - 1, 0) * w, w)
    posk = pos_col_ref[pl.ds(start, 2 * w), :]
    rel = (n * w + lax.broadcasted_iota(jnp.int32, (1, w), 1)) - (start + lax.broadcasted_iota(jnp.int32, (2 * w, 1), 0))
    valid = jnp.logical_and(rel >= 0, rel < w)
    return start, jnp.where(valid, posq - posk, 1e30)


def _alibi(h):
    return LOG2E * 2.0 ** -(h + 1)


def _transpose_rows(eye, src_ref, dst_ref, seq, width):
    step = 2 * SWA_WINDOW
    for t in range(seq // step):
        for p in range(width // HEAD_LANES):
            lanes = slice(p * HEAD_LANES, (p + 1) * HEAD_LANES)
            dst_ref[lanes, t * step:(t + 1) * step] = _dot_nt(eye, src_ref[t * step:(t + 1) * step, lanes]).astype(BF16)


def _swa_fwd_call(qs, kd, vd, pos_col, pos_row, sinks, n_seq, seq):
    w = SWA_WINDOW
    qb = SWA_BLOCKS
    steps = seq // (qb * w)
    ext = HALF + 16

    def body(q_ref, k_ref, v_ref, pc_ref, pr_ref, sink_ref, o_ref, lse_ref, vt_ref):
        n = pl.program_id(1)
        lo = _lane_lo()
        hi = jnp.logical_not(lo)
        eye = _eye()

        @pl.when(n == 0)
        def _():
            step = 2 * w
            for kv in range(2):
                vt_ref[kv * ext + HALF:(kv + 1) * ext, :] = jnp.ones((16, seq), BF16)
                for t in range(seq // step):
                    v_t = _dot_nt(eye, v_ref[t * step:(t + 1) * step, kv * HEAD_LANES:(kv + 1) * HEAD_LANES])
                    vt_ref[kv * ext:kv * ext + HALF, t * step:(t + 1) * step] = v_t[:HALF, :].astype(BF16)

        heads = range(N_HEADS)
        blocks = range(qb)
        geo = [_swa_block(n * qb + bi, pc_ref, pr_ref[bi]) for bi in blocks]
        wins = [pl.ds(g[0], 2 * w) for g in geo]
        kwins = [k_ref[win, :] for win in wins]
        vts = [vt_ref[:, win] for win in wins]
        sts = []
        for bi in blocks:
            q = q_ref[bi * w:(bi + 1) * w, :]
            sts.append([])
            for j in range(N_HEADS // 2):
                qp = q[:, j * HEAD_LANES:(j + 1) * HEAD_LANES]
                both = jnp.concatenate([jnp.where(lo, qp, jnp.zeros_like(qp)), jnp.where(hi, qp, jnp.zeros_like(qp))], axis=0)
                st = _dot_nt(kwins[bi][:, (j // 2) * HEAD_LANES:(j // 2 + 1) * HEAD_LANES], both)
                sts[bi] += [st[:, :w], st[:, w:]]
        ps, ms = [], []
        for bi in blocks:
            ps.append([])
            ms.append([])
            for h in heads:
                s = sts[bi][h] - _alibi(h) * geo[bi][1]
                m = jnp.maximum(jnp.max(s, axis=0, keepdims=True), sink_ref[0, h] * LOG2E)
                ps[bi].append(jnp.exp2(s - m).astype(BF16))
                ms[bi].append(m)
        for bi in blocks:
            ots = []
            for h in heads:
                pv = _dot(vts[bi][(h // 4) * ext:(h // 4 + 1) * ext, :], ps[bi][h])
                l = pv[HALF:HALF + 1, :] + jnp.exp2(sink_ref[0, h] * LOG2E - ms[bi][h])
                ots.append(pv[:HALF, :] * (1.0 / l))
                lse_ref[0, h:h + 1, bi * w:(bi + 1) * w] = ms[bi][h] + jnp.log2(l)
            o_ref[bi * w:(bi + 1) * w, :] = jnp.concatenate(ots, axis=0).T

    n_tok = qs.shape[0]
    tok = lambda width: pl.BlockSpec((qb * w, width), lambda b, n: (b * steps + n, 0))
    whole = lambda width: pl.BlockSpec((seq, width), lambda b, n: (b, 0))
    return pl.pallas_call(
        body, name="swa_fwd", grid=(n_seq, steps),
        out_shape=[jax.ShapeDtypeStruct((n_tok, 512), F32), jax.ShapeDtypeStruct((n_seq, N_HEADS, seq), F32)],
        in_specs=[tok(512), whole(256), whole(256), whole(1), pl.BlockSpec((qb, 1, w), lambda b, n: (b * steps + n, 0, 0)),
                  pl.BlockSpec(memory_space=pltpu.SMEM)],
        out_specs=[tok(512), pl.BlockSpec((1, N_HEADS, qb * w), lambda b, n: (b, 0, n))],
        scratch_shapes=[pltpu.VMEM((2 * ext, seq), BF16)],
        compiler_params=_params(2),
    )(qs, kd, vd, pos_col, pos_row, sinks)


def _swa_bwd_call(qs, kd, vd, do, delta, lse, pos_col, pos_row, sinks, g_out, n_seq, seq):
    w = SWA_WINDOW
    qb = SWA_BLOCKS
    steps = seq // (qb * w)
    reduced, reduce_scratch = _reduce_operands(g_out)

    def body(q_ref, k_ref, v_ref, do_ref, dl_ref, lse_ref, pc_ref, pr_ref, sink_ref, g_ref, dq_ref, dk_ref, dv_ref,
             dsink_ref, f_ref, kt_ref, *reduce_refs):
        b, n = pl.program_id(0), pl.program_id(1)
        _grad_reduce(b * steps + n, n_seq * steps, g_ref, f_ref, *reduce_refs)
        lo = _lane_lo()
        hi = jnp.logical_not(lo)
        sub_lo = lax.broadcasted_iota(jnp.int32, (HEAD_LANES, 1), 0) < HALF
        eye = _eye()

        @pl.when(n == 0)
        def _():
            dk_ref[...] = jnp.zeros_like(dk_ref)
            dv_ref[...] = jnp.zeros_like(dv_ref)
            _transpose_rows(eye, k_ref, kt_ref, seq, 2 * HEAD_LANES)

        @pl.when(jnp.logical_and(n == 0, b == 0))
        def _():
            dsink_ref[...] = jnp.zeros_like(dsink_ref)

        heads = range(N_HEADS)
        blocks = range(qb)
        kv_lanes = lambda h: slice((h // 4) * HEAD_LANES, (h // 4 + 1) * HEAD_LANES)
        geo = [_swa_block(n * qb + bi, pc_ref, pr_ref[bi]) for bi in blocks]
        wins = [pl.ds(g[0], 2 * w) for g in geo]
        kwins = [k_ref[win, :] for win in wins]
        vwins = [v_ref[win, :] for win in wins]

        do_ts, deltas, qms, doms = [], [], [], []
        for bi in blocks:
            rows = slice(bi * w, (bi + 1) * w)
            for lst in (do_ts, deltas, qms, doms):
                lst.append([])
            for j in range(N_HEADS // 2):
                pair = slice(j * HEAD_LANES, (j + 1) * HEAD_LANES)
                dop = do_ref[rows, pair]
                qp = q_ref[rows, pair]
                dt = _dot_nt(eye, dop)
                for hh in range(2):
                    half = lo if hh == 0 else hi
                    do_ts[bi].append(jnp.where(sub_lo, dt, 0.0).astype(BF16) if hh == 0
                                     else jnp.where(sub_lo, 0.0, dt).astype(BF16))
                    deltas[bi].append(dl_ref[2 * j + hh:2 * j + hh + 1, rows])
                    qms[bi].append(jnp.where(half, qp, jnp.zeros_like(qp)))
                    doms[bi].append(jnp.where(half, dop, jnp.zeros_like(dop)))
        sts, dpts = [], []
        for bi in blocks:
            sts.append([])
            dpts.append([])
            for j in range(N_HEADS // 2):
                a, b = 2 * j, 2 * j + 1
                st = _dot_nt(kwins[bi][:, kv_lanes(a)], jnp.concatenate([qms[bi][a], qms[bi][b]], axis=0))
                dpt = _dot(vwins[bi][:, kv_lanes(a)], jnp.concatenate([do_ts[bi][a], do_ts[bi][b]], axis=1))
                sts[bi] += [st[:, :w], st[:, w:]]
                dpts[bi] += [dpt[:, :w], dpt[:, w:]]
        pts, dsts = [], []
        for bi in blocks:
            pts.append([])
            dsts.append([])
            for h in heads:
                lse_h = lse_ref[0, h:h + 1, bi * w:(bi + 1) * w]
                pt = jnp.exp2(sts[bi][h] - _alibi(h) * geo[bi][1] - lse_h)
                dsts[bi].append((pt * (dpts[bi][h] - deltas[bi][h])).astype(BF16))
                pts[bi].append(pt.astype(BF16))
                dsink_ref[h:h + 1, :] += -jnp.exp2(sink_ref[0, h] * LOG2E - lse_h) * deltas[bi][h]
        for bi in blocks:
            for kv in range(2):
                group = range(4 * kv, 4 * kv + 4)
                dst_all = jnp.concatenate([dsts[bi][h] for h in group], axis=1)
                pt_all = jnp.concatenate([pts[bi][h] for h in group], axis=1)
                q_all = jnp.concatenate([qms[bi][h] for h in group], axis=0)
                do_all = jnp.concatenate([doms[bi][h] for h in group], axis=0)
                dk_ref[wins[bi], kv_lanes(4 * kv)] += _dot(dst_all, q_all)
                dv_ref[wins[bi], kv_lanes(4 * kv)] += _dot(pt_all, do_all)
        for bi in blocks:
            ktw = kt_ref[:, wins[bi]]
            for j in range(N_HEADS // 2):
                k_t = ktw[kv_lanes(2 * j), :]
                both = _dot(k_t, jnp.concatenate([dsts[bi][2 * j], dsts[bi][2 * j + 1]], axis=1))
                dq_t = jnp.where(sub_lo, both[:, :w], both[:, w:])
                dq_ref[bi * w:(bi + 1) * w, j * HEAD_LANES:(j + 1) * HEAD_LANES] = dq_t.T * SWA_SCALE

    n_tok = qs.shape[0]
    tok = lambda width: pl.BlockSpec((qb * w, width), lambda b, n: (b * steps + n, 0))
    whole = lambda width: pl.BlockSpec((seq, width), lambda b, n: (b, 0))
    return pl.pallas_call(
        body, name="swa_bwd", grid=(n_seq, steps),
        out_shape=[jax.ShapeDtypeStruct((n_tok, 512), F32), jax.ShapeDtypeStruct((n_tok, 256), F32),
                   jax.ShapeDtypeStruct((n_tok, 256), F32), jax.ShapeDtypeStruct((N_HEADS, HEAD_LANES), F32), reduced],
        in_specs=[tok(512), whole(256), whole(256), pl.BlockSpec((qb * w, 512), lambda b, n: (b * steps + n, 1)),
                  pl.BlockSpec((N_HEADS, qb * w), lambda b, n: (0, b * steps + n)),
                  pl.BlockSpec((1, N_HEADS, qb * w), lambda b, n: (b, 0, n)),
                  whole(1), pl.BlockSpec((qb, 1, w), lambda b, n: (b * steps + n, 0, 0)),
                  pl.BlockSpec(memory_space=pltpu.SMEM), ANY_SPEC],
        out_specs=[tok(512), whole(256), whole(256), _full((N_HEADS, HEAD_LANES)), ANY_SPEC],
        scratch_shapes=[pltpu.VMEM((2 * HEAD_LANES, seq), BF16)] + reduce_scratch,
        compiler_params=_params(2),
    )(qs, kd, vd, do, delta, lse, pos_col, pos_row, sinks, g_out)


def _post_call(x, target, o_mla, o_swa, gates, mod, b_ada, fg, w_out, seq):
    n_tok = x.shape[0]
    tm = min(TOKEN_TILE, seq)
    per_seq = seq // tm
    n_seq = n_tok // seq

    def body(x_ref, t_ref, om_ref, os_ref, g_ref, mod_ref, bada_ref, fg_ref, w_ref,
             dx2_ref, do_ref, dg_ref, gw_ref, gfg_ref, dgate_ref, loss_ref, dmla_ref, dswa_ref):
        i = pl.program_id(0)

        @pl.when(i == 0)
        def _():
            gw_ref[...] = jnp.zeros_like(gw_ref)
            gfg_ref[...] = jnp.zeros_like(gfg_ref)
            loss_ref[...] = jnp.zeros_like(loss_ref)

        @pl.when(i % per_seq == 0)
        def _():
            dgate_ref[...] = jnp.zeros_like(dgate_ref)

        gate = mod_ref[0][:, 2 * D_MODEL:] + bada_ref[:, 2 * D_MODEL:]
        fgv = fg_ref[...]
        fgd = fgv * (1.0 / D_MODEL)
        subs = _sub_tiles(tm)
        gs = [g_ref[r, :] for r in subs]
        os_ = [jnp.concatenate([om_ref[r, :], os_ref[r, :]], axis=-1) for r in subs]
        sgs = [_sigmoid(g) for g in gs]
        sils = [g * sg for g, sg in zip(gs, sgs)]
        ypres = [(o * sil).astype(BF16) for o, sil in zip(os_, sils)]
        ys = [_dot(ypre, w_ref[...]) for ypre in ypres]
        dys, loss, gfg, dgate = [], 0.0, 0.0, 0.0
        for r, y in zip(subs, ys):
            x2 = x_ref[r, :] + gate * y
            r2 = lax.rsqrt(jnp.mean(x2 * x2, axis=-1, keepdims=True) + EPS)
            xn2 = x2 * r2
            err = xn2 * fgv - t_ref[r, :]
            loss = loss + jnp.sum(jnp.sum(err * err, axis=-1, keepdims=True), axis=0, keepdims=True)
            gfg = gfg + jnp.sum(err * xn2, axis=0, keepdims=True)
            dxn2 = err * fgd
            dx2 = r2 * (dxn2 - xn2 * jnp.mean(dxn2 * xn2, axis=-1, keepdims=True))
            dx2_ref[r, :] = dx2
            dgate = dgate + jnp.sum(dx2 * y, axis=0, keepdims=True)
            dys.append((dx2 * gate).astype(BF16))
        loss_ref[...] += jnp.broadcast_to(loss * (0.5 / D_MODEL), loss_ref.shape)
        gfg_ref[...] += gfg * (1.0 / D_MODEL)
        dgate_ref[0] += dgate
        gw_ref[...] += _dot_tn(jnp.concatenate(ypres, axis=0), jnp.concatenate(dys, axis=0))
        dypres = [_dot_nt(dy, w_ref[...]) for dy in dys]
        pick = jnp.where(jnp.right_shift(lax.broadcasted_iota(jnp.int32, (2 * N_HEADS, D_MODEL), 1), 6)
                         == lax.broadcasted_iota(jnp.int32, (2 * N_HEADS, D_MODEL), 0), 1.0, 0.0).astype(BF16)
        for r, dypre, o, g, sg, sil in zip(subs, dypres, os_, gs, sgs, sils):
            dov = (dypre * sil).astype(BF16)
            do_ref[r, :] = dov
            delta = _dot_nt(pick, (dov.astype(F32) * o).astype(BF16))
            for grp in range(2):
                dmla_ref[grp, :, r] = delta[4 * grp:4 * grp + 4, :]
            dswa_ref[:, r] = delta[N_HEADS:, :]
            dg_ref[r, :] = (dypre * o * (sg + sil * (1.0 - sg))).astype(BF16)

    tok = lambda w: pl.BlockSpec((tm, w), lambda i: (i, 0))
    per_b = pl.BlockSpec((1, 1, 3 * D_MODEL), lambda i: (i // per_seq, 0, 0))
    return pl.pallas_call(
        body, name="post", grid=(n_tok // tm,),
        out_shape=[jax.ShapeDtypeStruct((n_tok, D_MODEL), F32), jax.ShapeDtypeStruct((n_tok, D_MODEL), BF16),
                   jax.ShapeDtypeStruct((n_tok, D_MODEL), BF16), jax.ShapeDtypeStruct((D_MODEL, D_MODEL), F32),
                   jax.ShapeDtypeStruct((1, D_MODEL), F32), jax.ShapeDtypeStruct((n_seq, 1, D_MODEL), F32),
                   jax.ShapeDtypeStruct((1, HEAD_LANES), F32),
                   jax.ShapeDtypeStruct((2, N_HEADS // 2, n_tok), F32), jax.ShapeDtypeStruct((N_HEADS, n_tok), F32)],
        in_specs=[tok(D_MODEL), tok(D_MODEL), tok(512), tok(512), tok(D_MODEL), per_b, _full(b_ada.shape),
                  _full(fg.shape), _full(w_out.shape)],
        out_specs=[tok(D_MODEL), tok(D_MODEL), tok(D_MODEL), _full((D_MODEL, D_MODEL)), _full((1, D_MODEL)),
                   pl.BlockSpec((1, 1, D_MODEL), lambda i: (i // per_seq, 0, 0)), _full((1, HEAD_LANES)),
                   pl.BlockSpec((2, N_HEADS // 2, tm), lambda i: (0, 0, i)), pl.BlockSpec((N_HEADS, tm), lambda i: (0, i))],
        compiler_params=_params(1),
    )(x, target, o_mla, o_swa, gates, mod, b_ada, fg, w_out)


def _mid_bwd_call(dqf, dkf, dv, zqkv, rope, qg, kvg, wq2, wkv, seq):
    n_tok = dqf.shape[0]
    tm = min(TOKEN_TILE, seq)

    def body(dq_ref, dk_ref, dv_ref, z_ref, rope_ref, qg_ref, kvg_ref, wq_ref, wkv_ref,
             dz_ref, gwq_ref, gwkv_ref, gqg_ref, gkvg_ref):
        i = pl.program_id(0)

        @pl.when(i == 0)
        def _():
            gwq_ref[...] = jnp.zeros_like(gwq_ref)
            gwkv_ref[...] = jnp.zeros_like(gwkv_ref)
            gqg_ref[...] = jnp.zeros_like(gqg_ref)
            gkvg_ref[...] = jnp.zeros_like(gkvg_ref)

        cos, sin = rope_ref[:, :HEAD_LANES], rope_ref[:, HEAD_LANES:]
        cf, sf = jnp.tile(cos, (1, N_HEADS)), jnp.tile(sin, (1, N_HEADS))
        dq = dq_ref[...] * MLA_SCALE
        dqr = jnp.concatenate([dq * cf, dq * sf], axis=-1).astype(BF16)
        zq, zkv = z_ref[:, :Q_LORA], z_ref[:, Q_LORA:]
        qgv, kvgv = qg_ref[...], kvg_ref[...]

        rq = lax.rsqrt(jnp.mean(zq * zq, axis=-1, keepdims=True) + EPS)
        xq = zq * rq
        gwq_ref[...] += _dot_tn((xq * qgv).astype(BF16), dqr)
        dqn = _dot_nt(dqr, wq_ref[...])
        gqg_ref[...] += jnp.sum(dqn * xq, axis=0, keepdims=True)
        dxq = dqn * qgv
        dz_ref[:, :Q_LORA] = (rq * (dxq - xq * jnp.mean(dxq * xq, axis=-1, keepdims=True))).astype(BF16)

        dk = dk_ref[...] * LN2
        dkv = jnp.concatenate([dk, dv_ref[...]], axis=-1).astype(BF16)
        rkv = lax.rsqrt(jnp.mean(zkv * zkv, axis=-1, keepdims=True) + EPS)
        xkv = zkv * rkv
        gwkv_ref[...] += _dot_tn((xkv * kvgv).astype(BF16), dkv)
        dkvn = _dot_nt(dkv, wkv_ref[...])
        gkvg_ref[...] += jnp.sum(dkvn * xkv, axis=0, keepdims=True)
        dxkv = dkvn * kvgv
        dz_ref[:, Q_LORA:A_KR] = (rkv * (dxkv - xkv * jnp.mean(dxkv * xkv, axis=-1, keepdims=True))).astype(BF16)

        dkpe = dk[:, :HEAD_LANES]
        for h in range(1, N_HEADS):
            dkpe = dkpe + dk[:, h * HEAD_LANES:(h + 1) * HEAD_LANES]
        dz_ref[:, A_KR:] = (jnp.where(_lane_lo(), 0.0, dkpe * cos) + pltpu.roll(dkpe * sin, HALF, 1)).astype(BF16)

    tok = lambda w: pl.BlockSpec((tm, w), lambda i: (i, 0))
    return pl.pallas_call(
        body, name="mid_bwd", grid=(n_tok // tm,),
        out_shape=[jax.ShapeDtypeStruct((n_tok, A_GM), BF16),
                   jax.ShapeDtypeStruct(wq2.shape, F32), jax.ShapeDtypeStruct(wkv.shape, F32),
                   jax.ShapeDtypeStruct((1, Q_LORA), F32), jax.ShapeDtypeStruct((1, KV_LORA), F32)],
        in_specs=[tok(1024), tok(1024), tok(512), tok(640), tok(2 * HEAD_LANES), _full(qg.shape), _full(kvg.shape),
                  _full(wq2.shape), _full(wkv.shape)],
        out_specs=[tok(A_GM), _full(wq2.shape), _full(wkv.shape), _full((1, Q_LORA)), _full((1, KV_LORA))],
        compiler_params=_params(1),
    )(dqf, dkf, dv, zqkv, rope, qg, kvg, wq2, wkv)


def _in_bwd_call(x, dx2, dz, dg, dqs, dkd, dvd, mod, b_ada, ng, wa, seq):
    n_tok = x.shape[0]
    tm = min(TOKEN_TILE, seq)
    per_seq = seq // tm
    n_seq = n_tok // seq

    def body(x_ref, dx2_ref, dz_ref, dg_ref, dqs_ref, dkd_ref, dvd_ref, mod_ref, bada_ref, ng_ref,
             wa_ref, gx_ref, gwa_ref, gng_ref, dshift_ref, dscale_ref):
        i = pl.program_id(0)

        @pl.when(i == 0)
        def _():
            gwa_ref[...] = jnp.zeros_like(gwa_ref)
            gng_ref[...] = jnp.zeros_like(gng_ref)

        @pl.when(i % per_seq == 0)
        def _():
            dshift_ref[...] = jnp.zeros_like(dshift_ref)
            dscale_ref[...] = jnp.zeros_like(dscale_ref)

        xv = x_ref[...]
        modv = mod_ref[0] + bada_ref[...]
        shift, scale = modv[:, :D_MODEL], modv[:, D_MODEL:2 * D_MODEL]
        ngv = ng_ref[...]
        r1 = lax.rsqrt(jnp.mean(xv * xv, axis=-1, keepdims=True) + EPS)
        xn = xv * r1
        hb = ((xn * ngv) * (1.0 + scale) + shift).astype(BF16)

        dgv = dg_ref[...]
        pieces = [(A_ZQ, dz_ref[...]), (A_GM, dgv[:, :512]), (A_QS, dqs_ref[...].astype(BF16)),
                  (A_KS, jnp.concatenate([_once(dkd_ref[...]) * LN2, _once(dvd_ref[...])], axis=1).astype(BF16)),
                  (A_GS, dgv[:, 512:])]
        dh = None
        for off, piece in pieces:
            wd = piece.shape[1]
            gwa_ref[:, off:off + wd] += _dot_tn(hb, piece)
            term = _dot_nt(piece, wa_ref[:, off:off + wd])
            dh = term if dh is None else dh + term

        dshift_ref[0] += jnp.sum(dh, axis=0, keepdims=True)
        dscale_ref[0] += jnp.sum(dh * (xn * ngv), axis=0, keepdims=True)
        gng_ref[...] += jnp.sum(dh * xn * (1.0 + scale), axis=0, keepdims=True)
        dxn = dh * ngv * (1.0 + scale)
        gx_ref[...] = dx2_ref[...] + r1 * (dxn - xn * jnp.mean(dxn * xn, axis=-1, keepdims=True))

    tok = lambda w: pl.BlockSpec((tm, w), lambda i: (i, 0))
    per_b = lambda w: pl.BlockSpec((1, 1, w), lambda i: (i // per_seq, 0, 0))
    return pl.pallas_call(
        body, name="in_bwd", grid=(n_tok // tm,),
        out_shape=[jax.ShapeDtypeStruct((n_tok, D_MODEL), F32), jax.ShapeDtypeStruct((D_MODEL, A_END), F32),
                   jax.ShapeDtypeStruct((1, D_MODEL), F32),
                   jax.ShapeDtypeStruct((n_seq, 1, D_MODEL), F32), jax.ShapeDtypeStruct((n_seq, 1, D_MODEL), F32)],
        in_specs=[tok(D_MODEL), tok(D_MODEL), tok(A_GM), tok(D_MODEL), tok(512), tok(256), tok(256),
                  per_b(3 * D_MODEL), _full(b_ada.shape), _full(ng.shape), _full(wa.shape)],
        out_specs=[tok(D_MODEL), _full((D_MODEL, A_END)), _full((1, D_MODEL)), per_b(D_MODEL), per_b(D_MODEL)],
        compiler_params=_params(1),
    )(x, dx2, dz, dg, dqs, dkd, dvd, mod, b_ada, ng, wa)


def _adam_math(w, g, m, v):
    m_new = ADAM_B1 * m + (1.0 - ADAM_B1) * g
    v_new = ADAM_B2 * v + (1.0 - ADAM_B2) * (g * g)
    m_hat = m_new / (1.0 - ADAM_B1 ** ADAM_STEP)
    v_hat = v_new / (1.0 - ADAM_B2 ** ADAM_STEP)
    delta = -ADAM_LR * (m_hat / (jnp.sqrt(v_hat) + ADAM_EPS) + ADAM_WD * w)
    return delta, m_new, v_new


def _adam_call(name, w, g, m, v):
    rows, cols = w.shape
    tr = next((t for t in (256, 128, 88) if rows % t == 0), rows)

    def body(w_ref, g_ref, m_ref, v_ref, d_ref, mo_ref, vo_ref):
        d, mn, vn = _adam_math(w_ref[...], g_ref[...], m_ref[...], v_ref[...])
        d_ref[...] = d
        mo_ref[...] = mn
        vo_ref[...] = vn

    spec = pl.BlockSpec((tr, cols), lambda i: (i, 0))
    return pl.pallas_call(
        body, name=name, grid=(rows // tr,),
        out_shape=[jax.ShapeDtypeStruct(w.shape, F32)] * 3,
        in_specs=[spec] * 4, out_specs=[spec] * 3,
        compiler_params=_params(1),
    )(w, g, m, v)


def _ada_bwd_call(act_all, dmod_cols, w, m, v):
    rows, cols = w.shape
    tr = 256

    def body(a_ref, dm_ref, w_ref, m_ref, v_ref, g_ref, d_ref, mo_ref, vo_ref):
        g = _dot_tn(a_ref[...].astype(BF16), dm_ref[...].astype(BF16))
        d, mn, vn = _adam_math(w_ref[...], g, m_ref[...], v_ref[...])
        g_ref[...] = g
        d_ref[...] = d
        mo_ref[...] = mn
        vo_ref[...] = vn

    spec = pl.BlockSpec((tr, cols), lambda i: (i, 0))
    nb = act_all.shape[0]
    return pl.pallas_call(
        body, name="ada_bwd", grid=(rows // tr,),
        out_shape=[jax.ShapeDtypeStruct(w.shape, F32)] * 4,
        in_specs=[pl.BlockSpec((nb, tr), lambda i: (0, i)), _full(dmod_cols.shape), spec, spec, spec],
        out_specs=[spec] * 4,
        compiler_params=_params(1),
    )(act_all, dmod_cols, w, m, v)


SMALL_ROW = {"norm_gain": (0, 1024), "final_gain": (1024, 2048), "q_norm_gain": (2048, 2432),
             "kv_norm_gain": (2432, 2688), "swa_sinks": (2688, 2696), "loss": (2816, 2944)}
SMALL_ORDER = ("b_ada", "norm_gain", "q_norm_gain", "kv_norm_gain", "swa_sinks", "final_gain")


def _small_call(parts_all, n_seq, params):
    k = len(params)

    def body(p_ref, *refs):
        ins, outs, loss_ref = refs[:3 * k], refs[3 * k:7 * k], refs[7 * k]
        row = p_ref[n_seq:n_seq + 1, :]
        for dv in range(1, 8):
            r0 = dv * ROWS_PER_DEVICE + n_seq
            row = row + p_ref[r0:r0 + 1, :]
        gb = None
        for dv in range(8):
            for r in range(n_seq):
                r0 = dv * ROWS_PER_DEVICE + r
                gb = p_ref[r0:r0 + 1, :] if gb is None else gb + p_ref[r0:r0 + 1, :]
        for j, name in enumerate(SMALL_ORDER):
            g = gb if name == "b_ada" else row[:, SMALL_ROW[name][0]:SMALL_ROW[name][1]]
            d, mn, vn = _adam_math(ins[3 * j][...], g, ins[3 * j + 1][...], ins[3 * j + 2][...])
            outs[4 * j][...] = g
            outs[4 * j + 1][...] = d
            outs[4 * j + 2][...] = mn
            outs[4 * j + 3][...] = vn
        loss_ref[...] = row[:, SMALL_ROW["loss"][0]:SMALL_ROW["loss"][1]]

    flat = [t for p in params for t in p]
    res = pl.pallas_call(
        body, name="small_update", grid=(1,),
        out_shape=[jax.ShapeDtypeStruct(p[0].shape, F32) for p in params for _ in range(4)]
        + [jax.ShapeDtypeStruct((1, HEAD_LANES), F32)],
        in_specs=[_full(parts_all.shape)] + [_full(t.shape) for t in flat],
        out_specs=[_full(p[0].shape) for p in params for _ in range(4)] + [_full((1, HEAD_LANES))],
        compiler_params=_params(1),
    )(parts_all, *flat)
    return [res[4 * j:4 * j + 4] for j in range(k)], res[4 * k]


def _rot(t):
    half = t.shape[-1] // 2
    return jnp.concatenate([-t[..., half:], t[..., :half]], axis=-1)


def _rot_t(g):
    half = g.shape[-1] // 2
    return jnp.concatenate([g[..., half:], -g[..., :half]], axis=-1)


def _columns(segments, lo, hi):
    out, at = [], 0
    for seg in segments:
        n = seg.shape[1]
        a, b = max(lo, at), min(hi, at + n)
        if a < b:
            out.append(seg[:, a - at:b - at])
        at += n
    return out


def _prepare_in(w_in_blocks):
    o = [0]
    for s in IN_SPLITS:
        o.append(o[-1] + s)
    part = lambda a, b: _columns(w_in_blocks, a, b)
    kr = jnp.concatenate(part(o[2], o[3]), axis=1)
    zero = jnp.zeros((kr.shape[0], 32), kr.dtype)
    return jnp.concatenate(part(0, o[2]) + [_rot(kr), zero, kr, zero] + part(o[3], o[8]), axis=1)


def _prepare_up(w_uq, w_ukv):
    uq = w_uq.reshape(Q_LORA, N_HEADS, MLA_NOPE + MLA_ROPE)
    zq = jnp.zeros((Q_LORA, N_HEADS, 32), w_uq.dtype)
    uq_full = jnp.concatenate([uq, zq], axis=-1).reshape(Q_LORA, 1024)
    uq_rot = jnp.concatenate([jnp.zeros((Q_LORA, N_HEADS, 64), w_uq.dtype), _rot(uq[..., MLA_NOPE:]), zq],
                             axis=-1).reshape(Q_LORA, 1024)
    wq2 = jnp.concatenate([uq_full, uq_rot], axis=1)
    ukv = w_ukv.reshape(KV_LORA, N_HEADS, 128)
    k_full = jnp.concatenate([ukv[..., :64], jnp.zeros((KV_LORA, N_HEADS, 64), w_ukv.dtype)], axis=-1).reshape(KV_LORA, 1024)
    wkv = jnp.concatenate([k_full, ukv[..., 64:].reshape(KV_LORA, 512)], axis=1)
    return wq2, wkv


def _restore_in(gwa):
    gkr = gwa[:, A_KR + 64:A_KR + 96] + _rot_t(gwa[:, A_KR:A_KR + 32])
    in_order = [gwa[:, :A_KR], gkr, gwa[:, A_GM:]]
    n = D_IN // 4
    return [jnp.concatenate(_columns(in_order, k * n, (k + 1) * n), axis=1) for k in range(4)]


def _restore_up(gwq2, gwkv):
    gf = gwq2[:, :1024].reshape(Q_LORA, N_HEADS, 128)
    gr = gwq2[:, 1024:].reshape(Q_LORA, N_HEADS, 128)
    g_uq = jnp.concatenate([gf[..., :64], gf[..., 64:96] + _rot_t(gr[..., 64:96])], axis=-1).reshape(Q_LORA, 768)
    gk = gwkv[:, :1024].reshape(KV_LORA, N_HEADS, 128)[..., :64]
    gv = gwkv[:, 1024:].reshape(KV_LORA, N_HEADS, 64)
    g_ukv = jnp.concatenate([gk, gv], axis=-1).reshape(KV_LORA, 1024)
    return g_uq, g_ukv


def _local_step(x, positions, target, mod_rows, b_ada, ng, qg, kvg, sinks, fg, w_in_b, later_shards):
    n_seq, seq, _ = x.shape
    n_tok = n_seq * seq
    x2d = x.reshape(n_tok, D_MODEL)
    t2d = target.reshape(n_tok, D_MODEL)
    pos_f = positions.astype(F32)
    pos_col = pos_f.reshape(n_tok, 1)
    pos_row = pos_f.reshape(n_tok // SWA_WINDOW, 1, SWA_WINDOW)
    mod3 = mod_rows.reshape(n_seq, 1, 3 * D_MODEL)
    inv = ROPE_THETA ** (-jnp.arange(0, MLA_ROPE, 2, dtype=F32) / MLA_ROPE)
    inv128 = jnp.concatenate([jnp.zeros((64,), F32), inv, inv, jnp.zeros((32,), F32)]).reshape(1, 128)
    fg2 = fg.reshape(1, D_MODEL)

    wa = _prepare_in(w_in_b)
    zqkv, zkr, gates, qs, kd, vd, rope, f_uq, f_ukv, f_out = _pre_call(x2d, pos_col, mod3, b_ada, ng, inv128, wa,
                                                                       later_shards, seq)
    cols = lambda t, r: jnp.transpose(t.reshape(4, r, -1), (1, 0, 2)).reshape(r, -1)
    wq2, wkv = _prepare_up(cols(f_uq, Q_LORA), cols(f_ukv, KV_LORA))
    w_out_b = f_out.reshape(D_MODEL, D_MODEL)
    qf, kf, v = _up_call(zqkv, zkr, rope, qg, kvg, wq2, wkv, seq)
    o_mla, lse_mla = _mla_fwd_call(qf, kf, v, n_seq, seq)
    o_swa, lse_swa = _swa_fwd_call(qs, kd, vd, pos_col, pos_row, sinks, n_seq, seq)
    dx2, do, dg, g_out, g_fg, dgate, loss, delta_mla, delta_swa = _post_call(x2d, t2d, o_mla, o_swa, gates, mod3, b_ada, fg2, w_out_b, seq)
    dqf, dkf, dv = _mla_bwd_call(qf, kf, v, do, delta_mla, lse_mla, n_seq, seq)
    dqs, dkd, dvd, dsink, r_out = _swa_bwd_call(qs, kd, vd, do, delta_swa, lse_swa, pos_col, pos_row, sinks,
                                                g_out.reshape(4, 2, D_MODEL // 8, D_MODEL), n_seq, seq)
    dz, g_wq2, g_wkv, g_qg, g_kvg = _mid_bwd_call(dqf, dkf, dv, zqkv, rope, qg, kvg, wq2, wkv, seq)
    gx, g_wa, g_ng, dshift, dscale = _in_bwd_call(x2d, dx2, dz, dg, dqs, dkd, dvd, mod3, b_ada, ng, wa, seq)
    g_in = _restore_in(g_wa)
    g_uq, g_ukv = _restore_up(g_wq2, g_wkv)
    dmod = jnp.concatenate([dshift, dscale, dgate], axis=-1).reshape(n_seq, 3 * D_MODEL)
    small_row = jnp.concatenate([g_ng, g_fg, g_qg, g_kvg, jnp.pad(jnp.sum(dsink, axis=1).reshape(1, N_HEADS), ((0, 0), (0, 120))),
                                 loss, jnp.zeros((1, 128), F32)], axis=1)
    return gx.reshape(x.shape), (g_in, g_uq, g_ukv), r_out, small_row, dmod


def kernel(x, c, positions, w_ada, b_ada, norm_gain, w_in, q_norm_gain, kv_norm_gain, w_uq, w_ukv, swa_sinks, w_out, final_gain, loss_target, m_w_ada, m_b_ada, m_norm_gain, m_w_in, m_q_norm_gain, m_kv_norm_gain, m_w_uq, m_w_ukv, m_swa_sinks, m_w_out, m_final_gain, v_w_ada, v_b_ada, v_norm_gain, v_w_in, v_q_norm_gain, v_kv_norm_gain, v_w_uq, v_w_ukv, v_swa_sinks, v_w_out, v_final_gain):
    n_seq = x.shape[0]
    xi, yi, ci = lax.axis_index("x"), lax.axis_index("y"), lax.axis_index("c")
    dev = 4 * xi + 2 * yi + ci
    chip = 2 * xi + yi

    halves = lambda w: w.astype(BF16).reshape(2, w.shape[0] // 2, w.shape[1])
    c_blk = jnp.pad(c, ((0, ROWS_PER_DEVICE - n_seq), (0, 0)))
    act_all, pieces, f_in = _comm_fwd_call(c_blk, w_ada[0], [halves(w_in[0])])
    mine = lax.dynamic_slice_in_dim(pieces, dev * ROWS_PER_DEVICE, n_seq, axis=1)
    mod_rows = jnp.transpose(mine, (1, 0, 2)).reshape(n_seq, 3 * D_MODEL)
    w_in_blocks = [f_in[k].reshape(D_MODEL, -1) for k in range(4)]

    gx, (g_in_blocks, g_uq, g_ukv), r_out, small_row, dmod = _local_step(
        x, positions, loss_target, mod_rows, b_ada, norm_gain, q_norm_gain, kv_norm_gain, swa_sinks, final_gain,
        w_in_blocks, [halves(w_uq[0]), halves(w_ukv[0]), halves(w_out[0])])

    grads = [jnp.stack(g_in_blocks).reshape(4, 2, D_MODEL // 2, -1), _by_owner(g_uq, g_uq.shape[1] // 4),
             _by_owner(g_ukv, g_ukv.shape[1] // 4)]
    part = jnp.concatenate([dmod, small_row, jnp.zeros((ROWS_PER_DEVICE - n_seq - 1, 3 * D_MODEL), F32)], axis=0)
    r_in, r_uq, r_ukv, parts_all = _comm_bwd_call(grads, part)
    g_in_s, g_uq_s = r_in.reshape(w_in.shape[1:]), r_uq.reshape(w_uq.shape[1:])
    g_ukv_s, g_out_s = r_ukv.reshape(w_ukv.shape[1:]), r_out.reshape(w_out.shape[1:])

    tr = lambda a: jnp.swapaxes(a[0], 0, 1)
    back = lambda ts: [jnp.swapaxes(t, 0, 1) for t in ts]
    d_in, nm_in, nv_in = back(_adam_call("adam_w_in", tr(w_in), g_in_s.T, tr(m_w_in), tr(v_w_in)))
    d_uq, nm_uq, nv_uq = back(_adam_call("adam_w_uq", tr(w_uq), g_uq_s.T, tr(m_w_uq), tr(v_w_uq)))
    d_ukv, nm_ukv, nv_ukv = _adam_call("adam_w_ukv", w_ukv[0], g_ukv_s, m_w_ukv[0], v_w_ukv[0])
    d_out, nm_out, nv_out = _adam_call("adam_w_out", w_out[0], g_out_s, m_w_out[0], v_w_out[0])
    dmod_cols = lax.dynamic_slice_in_dim(parts_all, chip * 768, 768, axis=1)
    g_ada, d_ada, nm_ada, nv_ada = _ada_bwd_call(act_all, dmod_cols, w_ada[0], m_w_ada[0], v_w_ada[0])

    row = lambda t: t.reshape(1, -1)
    small = {"b_ada": (b_ada, m_b_ada, v_b_ada), "norm_gain": (norm_gain, m_norm_gain, v_norm_gain),
             "q_norm_gain": (q_norm_gain, m_q_norm_gain, v_q_norm_gain),
             "kv_norm_gain": (kv_norm_gain, m_kv_norm_gain, v_kv_norm_gain),
             "swa_sinks": (swa_sinks, m_swa_sinks, v_swa_sinks),
             "final_gain": (row(final_gain), row(m_final_gain), row(v_final_gain))}
    res, loss_row = _small_call(parts_all, n_seq, [small[name] for name in SMALL_ORDER])
    res = dict(zip(SMALL_ORDER, res))
    res["final_gain"] = [t.reshape(-1) for t in res["final_gain"]]
    e = lambda t: t[None]
    big = {"w_ada": (e(g_ada), e(d_ada), e(nm_ada), e(nv_ada)), "w_in": (e(g_in_s), e(d_in), e(nm_in), e(nv_in)),
           "w_uq": (e(g_uq_s), e(d_uq), e(nm_uq), e(nv_uq)), "w_ukv": (e(g_ukv_s), e(d_ukv), e(nm_ukv), e(nv_ukv)),
           "w_out": (e(g_out_s), e(d_out), e(nm_out), e(nv_out))}
    order = ("w_ada", "b_ada", "norm_gain", "w_in", "q_norm_gain", "kv_norm_gain", "w_uq", "w_ukv", "swa_sinks", "w_out",
             "final_gain")
    pick = lambda kind: [(big[n] if n in big else res[n])[kind] for n in order]
    return (loss_row[0, 0], gx, *pick(0), *pick(1), *pick(2), *pick(3))
```

```python
import jax
import jax.numpy as jnp
from jax import lax
from jax.experimental import pallas as pl
from jax.experimental.pallas import tpu as pltpu

F32 = jnp.float32
BF16 = jnp.bfloat16

D_MODEL = 1024
Q_LORA = 384
KV_LORA = 256
N_HEADS = 8
MLA_NOPE = 64
MLA_ROPE = 32
HEAD_LANES = 128
HALF = 64
SWA_WINDOW = 128
EPS = 1e-6
ROPE_THETA = 10000.0
MLA_SCALE = (MLA_NOPE + MLA_ROPE) ** -0.5
LOG2E = 1.4426950408889634
LN2 = 0.6931471805599453
SWA_SCALE = 64 ** -0.5
NEG = -1e30

ADAM_LR = 0.001
ADAM_B1 = 0.9
ADAM_B2 = 0.999
ADAM_EPS = 1e-08
ADAM_WD = 0.01
ADAM_STEP = 10

A_ZQ, A_ZKV, A_KR, A_GM, A_QS, A_KS, A_VS, A_GS, A_END = 0, 384, 640, 768, 1280, 1792, 1920, 2048, 2560
IN_SPLITS = (384, 256, 32, 512, 512, 128, 128, 512)
D_IN = sum(IN_SPLITS)

TOKEN_TILE = 512
ATT_TILE = 256
VMEM_LIMIT = 56 * 1024 * 1024


def _dot(a, b):
    return jnp.dot(a, b, preferred_element_type=F32)


def _dot_nt(a, b):
    return lax.dot_general(a, b, (((1,), (1,)), ((), ())), preferred_element_type=F32)


def _dot_tn(a, b):
    return lax.dot_general(a, b, (((0,), (0,)), ((), ())), preferred_element_type=F32)


def _params(n_grid):
    return pltpu.CompilerParams(dimension_semantics=("arbitrary",) * n_grid, vmem_limit_bytes=VMEM_LIMIT)


def _full(shape):
    nd = len(shape)
    return pl.BlockSpec(shape, lambda *_: (0,) * nd, pipeline_mode=pl.Buffered(1))


def _sigmoid(g):
    return 1.0 / (1.0 + jnp.exp(-g))


SUB_TILE = 256


def _sub_tiles(tm):
    sub = min(SUB_TILE, tm)
    return [slice(s * sub, (s + 1) * sub) for s in range(tm // sub)]


MESH = pl.DeviceIdType.MESH
ROWS_PER_DEVICE = 8
VMEM_SPEC = pl.BlockSpec(memory_space=pltpu.VMEM)
ANY_SPEC = pl.BlockSpec(memory_space=pl.ANY)


def _position():
    x, y, c = lax.axis_index("x"), lax.axis_index("y"), lax.axis_index("c")
    sibling = (x, y, 1 - c)
    others = [(1 - x, y, c), (x, 1 - y, c), (1 - x, 1 - y, c)]
    return (x, y, c), 4 * x + 2 * y + c, 2 * x + y, sibling, others


def _rows_of(dev):
    return pl.ds(pl.multiple_of(dev * ROWS_PER_DEVICE, ROWS_PER_DEVICE), ROWS_PER_DEVICE)


def _all_to_all_rows(block_ref, table_ref, dev, me, send_sems, recv_sems):
    x, y, c = me
    waits = []
    for k in range(1, 8):
        peer = (1 - x if k & 4 else x, 1 - y if k & 2 else y, 1 - c if k & 1 else c)
        pltpu.make_async_remote_copy(src_ref=block_ref, dst_ref=table_ref.at[_rows_of(dev)], send_sem=send_sems.at[k - 1],
                                     recv_sem=recv_sems.at[k - 1], device_id=peer, device_id_type=MESH).start()
        waits.append(pltpu.make_async_remote_copy(
            src_ref=block_ref, dst_ref=table_ref.at[_rows_of(jnp.bitwise_xor(dev, k))], send_sem=send_sems.at[k - 1],
            recv_sem=recv_sems.at[k - 1], device_id=peer, device_id_type=MESH))
    return waits


def _comm_fwd_call(c_blk, w_ada, shards):
    n = len(shards)

    def body(c_ref, wada_ref, *refs):
        w_refs, act_ref, pieces_ref, full_refs = refs[:n], refs[n], refs[n + 1], refs[n + 2:2 * n + 2]
        c_all_ref = refs[2 * n + 2]
        c_send, c_recv, p_send, p_recv, w_send, w_recv, f_send, f_recv, loc_sem = refs[2 * n + 3:]
        me, dev, chip, sibling, others = _position()
        core = me[2]
        chip_of = [2 * p[0] + p[1] for p in others]

        local = [pltpu.make_async_copy(w_refs[i], full_refs[i].at[chip], loc_sem.at[i]) for i in range(n)]
        for cp in local:
            cp.start()

        def over_ici(i, j, src_chip):
            return pltpu.make_async_remote_copy(
                src_ref=w_refs[i].at[core], dst_ref=full_refs[i].at[src_chip, core], send_sem=w_send.at[3 * i + j],
                recv_sem=w_recv.at[3 * i + j], device_id=others[j], device_id_type=MESH)

        def to_sibling(i, j, half):
            return pltpu.make_async_remote_copy(
                src_ref=full_refs[i].at[chip_of[j], half], dst_ref=full_refs[i].at[chip_of[j], half],
                send_sem=f_send.at[3 * i + j], recv_sem=f_recv.at[3 * i + j], device_id=sibling, device_id_type=MESH)

        c_all_ref[_rows_of(dev), :] = c_ref[...]
        c_waits = _all_to_all_rows(c_ref, c_all_ref, dev, me, c_send, c_recv)
        sent = [over_ici(i, j, chip) for i in range(n) for j in range(3)]
        for cp in sent:
            cp.start()

        for cp in c_waits:
            cp.wait()
        cv = c_all_ref[...]
        act = cv * _sigmoid(cv)
        act_ref[...] = act
        pieces_ref[chip] = _dot(act.astype(BF16), wada_ref[...].astype(BF16))
        piece = lambda j, src_chip: pltpu.make_async_remote_copy(
            src_ref=pieces_ref.at[chip], dst_ref=pieces_ref.at[src_chip], send_sem=p_send.at[j], recv_sem=p_recv.at[j],
            device_id=others[j], device_id_type=MESH)
        for j in range(3):
            piece(j, chip).start()

        for i in range(n):
            for j in range(3):
                over_ici(i, j, chip_of[j]).wait_recv()
                to_sibling(i, j, core).start()
        for j in range(3):
            piece(j, chip).wait_send()
            piece(j, chip_of[j]).wait_recv()
        for i in range(n):
            for j in range(3):
                to_sibling(i, j, 1 - core).wait_recv()
                to_sibling(i, j, core).wait_send()
        for cp in sent:
            cp.wait_send()
        for cp in local:
            cp.wait()

    rows = 8 * ROWS_PER_DEVICE
    dma = pltpu.SemaphoreType.DMA
    return pl.pallas_call(
        body, name="comm_fwd",
        out_shape=[jax.ShapeDtypeStruct((rows, D_MODEL), F32), jax.ShapeDtypeStruct((4, rows, w_ada.shape[1]), F32)]
        + [jax.ShapeDtypeStruct((4,) + s.shape, s.dtype) for s in shards],
        in_specs=[VMEM_SPEC, VMEM_SPEC] + [ANY_SPEC] * n,
        out_specs=[VMEM_SPEC, VMEM_SPEC] + [ANY_SPEC] * n,
        scratch_shapes=[pltpu.VMEM((rows, D_MODEL), F32), dma((7,)), dma((7,)), dma((3,)), dma((3,)),
                        dma((3 * n,)), dma((3 * n,)), dma((3 * n,)), dma((3 * n,)), dma((n,))],
        compiler_params=pltpu.CompilerParams(vmem_limit_bytes=VMEM_LIMIT),
    )(c_blk, w_ada, *shards)


def _comm_bwd_call(grads, part):
    n = len(grads)

    def body(part_ref, *refs):
        g_refs, f_refs, parts_ref = refs[:n], refs[n:2 * n], refs[2 * n]
        scratch = refs[2 * n + 1:]
        a_refs, b_refs, p_refs, r_refs = (scratch[k * n:(k + 1) * n] for k in range(4))
        s_send, s_recv, d_send, d_recv, e_send, e_recv, h_send, h_recv, loc_sem = scratch[4 * n:]
        me, dev, chip, sibling, others = _position()
        core = me[2]
        chip_of = [2 * p[0] + p[1] for p in others]

        parts_ref[_rows_of(dev), :] = part_ref[...]
        s_waits = _all_to_all_rows(part_ref, parts_ref, dev, me, s_send, s_recv)

        mine = [pltpu.make_async_copy(g_refs[i].at[:, core], a_refs[i], loc_sem.at[i]) for i in range(n)]
        swap = [pltpu.make_async_remote_copy(src_ref=g_refs[i].at[:, 1 - core], dst_ref=b_refs[i], send_sem=d_send.at[i],
                                             recv_sem=d_recv.at[i], device_id=sibling, device_id_type=MESH) for i in range(n)]
        order = sorted(range(n), key=lambda i: g_refs[i].shape[2] * g_refs[i].shape[3])
        for i in order:
            mine[i].start()
            swap[i].start()
        cross = [pltpu.make_async_remote_copy(src_ref=p_refs[i].at[chip_of[j]], dst_ref=r_refs[i].at[j],
                                              send_sem=e_send.at[3 * i + j], recv_sem=e_recv.at[3 * i + j],
                                              device_id=others[j], device_id_type=MESH) for i in range(n) for j in range(3)]
        for i in order:
            mine[i].wait()
            swap[i].wait()
            for k in range(4):
                s = a_refs[i][k] + b_refs[i][k]
                a_refs[i][k] = s
                p_refs[i][k] = s.astype(BF16)
            for j in range(3):
                cross[3 * i + j].start()
        share = {}
        for i in order:
            for j in range(3):
                cross[3 * i + j].wait()
            f_refs[i][core] = (a_refs[i][chip] + r_refs[i][0].astype(F32) + r_refs[i][1].astype(F32)
                               + r_refs[i][2].astype(F32))
            share[i] = pltpu.make_async_remote_copy(src_ref=f_refs[i].at[core], dst_ref=f_refs[i].at[core],
                                                    send_sem=h_send.at[i], recv_sem=h_recv.at[i], device_id=sibling,
                                                    device_id_type=MESH)
            share[i].start()
        for i in range(n):
            share[i].wait_send()
            pltpu.make_async_remote_copy(src_ref=f_refs[i].at[core], dst_ref=f_refs[i].at[1 - core], send_sem=h_send.at[i],
                                         recv_sem=h_recv.at[i], device_id=sibling, device_id_type=MESH).wait_recv()
        for cp in s_waits:
            cp.wait()

    rows = 8 * ROWS_PER_DEVICE
    dma = pltpu.SemaphoreType.DMA
    quarter = [(4,) + g.shape[2:] for g in grads]
    return pl.pallas_call(
        body, name="comm_bwd",
        out_shape=[jax.ShapeDtypeStruct((2,) + g.shape[2:], F32) for g in grads]
        + [jax.ShapeDtypeStruct((rows, part.shape[1]), F32)],
        in_specs=[VMEM_SPEC] + [ANY_SPEC] * n,
        out_specs=[VMEM_SPEC] * (n + 1),
        scratch_shapes=[pltpu.VMEM(q, F32) for q in quarter] + [pltpu.VMEM(q, F32) for q in quarter]
        + [pltpu.VMEM(q, BF16) for q in quarter] + [pltpu.VMEM((3,) + q[1:], BF16) for q in quarter]
        + [dma((7,)), dma((7,)), dma((n,)), dma((n,)), dma((3 * n,)), dma((3 * n,)), dma((n,)), dma((n,)), dma((n,))],
        compiler_params=pltpu.CompilerParams(vmem_limit_bytes=VMEM_LIMIT),
    )(part, *grads)


def _by_owner(g, n):
    return jnp.transpose(g.reshape(g.shape[0], 4, n), (1, 0, 2)).reshape(4, 2, g.shape[0] // 2, n)


def _reduce_operands(g):
    quarter = (4,) + g.shape[2:]
    dma = pltpu.SemaphoreType.DMA
    scratch = [pltpu.VMEM(quarter, F32), pltpu.VMEM(quarter, F32), pltpu.VMEM(quarter, BF16),
               pltpu.VMEM((3,) + quarter[1:], BF16), dma((5,)), dma((5,)), dma((2,))]
    return jax.ShapeDtypeStruct((2,) + g.shape[2:], F32), scratch


def _grad_reduce(step, n_steps, g_ref, f_ref, a_ref, b_ref, p_ref, r_ref, send, recv, loc_sem):
    me, _, chip, sibling, others = _position()
    core = me[2]
    chip_of = [2 * p[0] + p[1] for p in others]
    remote = lambda src, dst, k, to: pltpu.make_async_remote_copy(
        src_ref=src, dst_ref=dst, send_sem=send.at[k], recv_sem=recv.at[k], device_id=to, device_id_type=MESH)
    mine = pltpu.make_async_copy(g_ref.at[:, core], a_ref, loc_sem.at[0])
    swap = remote(g_ref.at[:, 1 - core], b_ref, 0, sibling)
    cross = [remote(p_ref.at[chip_of[j]], r_ref.at[j], 1 + j, others[j]) for j in range(3)]
    total_ref = b_ref.at[0]
    keep = pltpu.make_async_copy(total_ref, f_ref.at[core], loc_sem.at[1])
    share = lambda half: remote(total_ref, f_ref.at[half], 4, sibling)
    at = [k * (n_steps - 1) // 3 for k in range(4)]

    @pl.when(step == at[0])
    def _():
        mine.start()
        swap.start()

    @pl.when(step == at[1])
    def _():
        mine.wait()
        swap.wait()
        for k in range(4):
            s = a_ref[k] + b_ref[k]
            a_ref[k] = s
            p_ref[k] = s.astype(BF16)
        for cp in cross:
            cp.start()

    @pl.when(step == at[2])
    def _():
        for cp in cross:
            cp.wait()
        total_ref[...] = a_ref[chip] + r_ref[0].astype(F32) + r_ref[1].astype(F32) + r_ref[2].astype(F32)
        keep.start()
        share(core).start()

    @pl.when(step == at[3])
    def _():
        keep.wait()
        share(core).wait_send()
        share(1 - core).wait_recv()


def _twice(t):
    lo = _lane_lo()
    other = pltpu.roll(t, HALF, 1)
    return jnp.concatenate([jnp.where(lo, t, other), jnp.where(lo, other, t)], axis=1)


def _once(g):
    first, second = g[:, :HEAD_LANES], g[:, HEAD_LANES:]
    return jnp.where(_lane_lo(), first + pltpu.roll(first, HALF, 1), second + pltpu.roll(second, HALF, 1))


def _rope_tables(pos_ref, inv_row, rope_ref):
    quarter = pos_ref.shape[0] // 4
    lane = lax.broadcasted_iota(jnp.int32, (1, HEAD_LANES), 1)
    pos = [pos_ref[g * quarter:(g + 1) * quarter, :] for g in range(4)]
    ang = jnp.where(lane < 32, pos[0], jnp.where(lane < 64, pos[1], jnp.where(lane < 96, pos[2], pos[3]))) * inv_row
    cos, sin = jnp.cos(ang), jnp.sin(ang)
    rope_lanes = jnp.logical_and(lane >= HALF, lane < HALF + MLA_ROPE)
    for g in range(4):
        rows = slice(g * quarter, (g + 1) * quarter)
        shift = (HALF - 32 * g) % HEAD_LANES
        at = lambda t: t if shift == 0 else pltpu.roll(t, shift, 1)
        rope_ref[rows, :HEAD_LANES] = jnp.where(rope_lanes, at(cos), 1.0)
        rope_ref[rows, HEAD_LANES:] = jnp.where(rope_lanes, at(sin), 0.0)


def _gather_in_steps(step, n_steps, w_refs, full_refs, w_send, w_recv, f_send, f_recv, loc_sem):
    me, _, chip, sibling, others = _position()
    core = me[2]
    chip_of = [2 * p[0] + p[1] for p in others]
    n = len(w_refs)
    local = [pltpu.make_async_copy(w_refs[i], full_refs[i].at[chip], loc_sem.at[i]) for i in range(n)]

    def over_ici(i, j, src_chip):
        return pltpu.make_async_remote_copy(
            src_ref=w_refs[i].at[core], dst_ref=full_refs[i].at[src_chip, core], send_sem=w_send.at[3 * i + j],
            recv_sem=w_recv.at[3 * i + j], device_id=others[j], device_id_type=MESH)

    def to_sibling(i, j, half):
        return pltpu.make_async_remote_copy(
            src_ref=full_refs[i].at[chip_of[j], half], dst_ref=full_refs[i].at[chip_of[j], half],
            send_sem=f_send.at[3 * i + j], recv_sem=f_recv.at[3 * i + j], device_id=sibling, device_id_type=MESH)

    pairs = [(i, j) for i in range(n) for j in range(3)]

    @pl.when(step == 0)
    def _():
        for cp in local:
            cp.start()
        for i, j in pairs:
            over_ici(i, j, chip).start()

    @pl.when(step == 3 * n_steps // 4)
    def _():
        for i, j in pairs:
            over_ici(i, j, chip_of[j]).wait_recv()
            to_sibling(i, j, core).start()

    @pl.when(step == n_steps - 1)
    def _():
        for i, j in pairs:
            to_sibling(i, j, 1 - core).wait_recv()
            to_sibling(i, j, core).wait_send()
            over_ici(i, j, chip).wait_send()
        for cp in local:
            cp.wait()


def _pre_call(x, pos_col, mod, b_ada, ng, inv128, wa, shards, seq):
    n_tok = x.shape[0]
    tm = min(TOKEN_TILE, seq)
    per_seq = seq // tm
    n_steps = n_tok // tm
    n = len(shards)

    def body(x_ref, pos_ref, mod_ref, bada_ref, ng_ref, inv_ref, wa_ref, *refs):
        w_refs, refs = refs[:n], refs[n:]
        zqkv_ref, zkr_ref, gates_ref, qs_ref, kd_ref, vd_ref, rope_ref = refs[:7]
        full_refs, sems = refs[7:7 + n], refs[7 + n:]
        _gather_in_steps(pl.program_id(0), n_steps, w_refs, full_refs, *sems)
        _rope_tables(pos_ref, inv_ref[...], rope_ref)
        xv = x_ref[...]
        modv = mod_ref[0] + bada_ref[...]
        shift, scale = modv[:, :D_MODEL], modv[:, D_MODEL:2 * D_MODEL]
        r1 = lax.rsqrt(jnp.mean(xv * xv, axis=-1, keepdims=True) + EPS)
        h = ((xv * r1) * ng_ref[...]) * (1.0 + scale) + shift
        za = _dot(h.astype(BF16), wa_ref[...])
        zqkv_ref[...] = za[:, :A_KR]
        zkr_ref[...] = za[:, A_KR:A_GM]
        gates_ref[:, :512] = za[:, A_GM:A_QS]
        gates_ref[:, 512:] = za[:, A_GS:A_END]
        qs_ref[...] = (za[:, A_QS:A_KS] * (SWA_SCALE * LOG2E)).astype(BF16)
        kd_ref[...] = _twice(za[:, A_KS:A_VS]).astype(BF16)
        vd_ref[...] = _twice(za[:, A_VS:A_GS]).astype(BF16)

    tok = lambda w: pl.BlockSpec((tm, w), lambda i: (i, 0))
    outs = [(640, F32), (HEAD_LANES, F32), (1024, F32), (512, BF16), (256, BF16), (256, BF16), (2 * HEAD_LANES, F32)]
    dma = pltpu.SemaphoreType.DMA
    return pl.pallas_call(
        body, name="pre", grid=(n_steps,),
        out_shape=[jax.ShapeDtypeStruct((n_tok, w), dt) for w, dt in outs]
        + [jax.ShapeDtypeStruct((4,) + s.shape, s.dtype) for s in shards],
        in_specs=[tok(D_MODEL), tok(1), pl.BlockSpec((1, 1, 3 * D_MODEL), lambda i: (i // per_seq, 0, 0)),
                  _full(b_ada.shape), _full(ng.shape), _full(inv128.shape), _full(wa.shape)] + [ANY_SPEC] * n,
        out_specs=[tok(w) for w, _ in outs] + [ANY_SPEC] * n,
        scratch_shapes=[dma((3 * n,)), dma((3 * n,)), dma((3 * n,)), dma((3 * n,)), dma((n,))],
        compiler_params=_params(1),
    )(x, pos_col, mod, b_ada, ng, inv128, wa, *shards)


def _up_call(zqkv, zkr, rope, qg, kvg, wq2, wkv, seq):
    n_tok = zqkv.shape[0]
    tm = min(TOKEN_TILE, seq)

    def body(zqkv_ref, zkr_ref, rope_ref, qg_ref, kvg_ref, wq_ref, wkv_ref, qf_ref, kf_ref, v_ref):
        cos, sin = rope_ref[:, :HEAD_LANES], rope_ref[:, HEAD_LANES:]
        zq, zkv = zqkv_ref[:, A_ZQ:A_ZKV], zqkv_ref[:, A_ZKV:A_KR]
        rq = lax.rsqrt(jnp.mean(zq * zq, axis=-1, keepdims=True) + EPS)
        qn = ((zq * rq) * qg_ref[...]).astype(BF16)
        qr = _dot(qn, wq_ref[...])
        cf, sf = jnp.tile(cos, (1, N_HEADS)), jnp.tile(sin, (1, N_HEADS))
        qf_ref[...] = ((qr[:, :1024] * cf + qr[:, 1024:] * sf) * (MLA_SCALE * LOG2E)).astype(BF16)
        rkv = lax.rsqrt(jnp.mean(zkv * zkv, axis=-1, keepdims=True) + EPS)
        kvn = ((zkv * rkv) * kvg_ref[...]).astype(BF16)
        kv = _dot(kvn, wkv_ref[...])
        zkr = zkr_ref[...]
        kpe = jnp.where(_lane_lo(), 0.0, zkr * cos) + pltpu.roll(zkr, HALF, 1) * sin
        kf_ref[...] = (kv[:, :1024] + jnp.tile(kpe, (1, N_HEADS))).astype(BF16)
        v_ref[...] = kv[:, 1024:].astype(BF16)

    tok = lambda w: pl.BlockSpec((tm, w), lambda i: (i, 0))
    outs = [(1024, BF16), (1024, BF16), (512, BF16)]
    return pl.pallas_call(
        body, name="up", grid=(n_tok // tm,),
        out_shape=[jax.ShapeDtypeStruct((n_tok, w), dt) for w, dt in outs],
        in_specs=[tok(640), tok(HEAD_LANES), tok(2 * HEAD_LANES), _full(qg.shape), _full(kvg.shape), _full(wq2.shape),
                  _full(wkv.shape)],
        out_specs=[tok(w) for w, _ in outs],
        compiler_params=_params(1),
    )(zqkv, zkr, rope, qg, kvg, wq2, wkv)


def _lane_lo(width=HEAD_LANES):
    return lax.broadcasted_iota(jnp.int32, (1, width), 1) < HALF


def _eye(n=HEAD_LANES):
    r = lax.broadcasted_iota(jnp.int32, (n, n), 0)
    c = lax.broadcasted_iota(jnp.int32, (n, n), 1)
    return jnp.where(r == c, 1.0, 0.0).astype(BF16)


def _mla_fwd_call(qf, kf, v, n_seq, seq):
    tq = min(ATT_TILE, seq)
    nq = seq // tq

    ext = HALF + 16

    def body(q_ref, k_ref, v_ref, o_ref, lse_ref, vt_ref, acc_ref):
        i = pl.program_id(1)
        eye = _eye()

        @pl.when(i == 0)
        def _():
            for h in range(N_HEADS):
                vt_ref[h * ext + HALF:(h + 1) * ext, :] = jnp.ones((16, seq), BF16)
            for t in range(nq):
                for p in range(N_HEADS // 2):
                    pair = slice(p * HEAD_LANES, (p + 1) * HEAD_LANES)
                    v_t = _dot_nt(eye, v_ref[t * tq:(t + 1) * tq, pair]).astype(BF16)
                    for hh in range(2):
                        r0 = (2 * p + hh) * ext
                        vt_ref[r0:r0 + HALF, t * tq:(t + 1) * tq] = v_t[hh * HALF:(hh + 1) * HALF, :]

        q = q_ref[...]
        qcol = i * tq + lax.broadcasted_iota(jnp.int32, (1, tq), 1)
        heads = range(N_HEADS)
        lanes = [slice(h * HEAD_LANES, (h + 1) * HEAD_LANES) for h in heads]

        def make_step(masked, n_tiles):
            def step(kt0, carry):
                tiles = range(n_tiles)
                start = pl.multiple_of(kt0 * tq, tq)
                ks = [k_ref[pl.ds(pl.multiple_of((kt0 + t) * tq, tq), tq), :] for t in tiles]
                vt = vt_ref[:, pl.ds(start, n_tiles * tq)]
                last = n_tiles - 1
                if masked:
                    keep = ((kt0 + last) * tq + lax.broadcasted_iota(jnp.int32, (tq, 1), 0)) <= qcol

                def scores(h):
                    sts = [_dot_nt(ks[t][:, lanes[h]], q[:, lanes[h]]) for t in tiles]
                    if masked:
                        sts[last] = jnp.where(keep, sts[last], NEG)
                    return sts

                def softmax(h, sts):
                    m_old = carry[h]
                    m_new = m_old
                    for st in sts:
                        m_new = jnp.maximum(m_new, jnp.max(st, axis=0, keepdims=True))
                    pt = jnp.concatenate([jnp.exp2(st - m_new).astype(BF16) for st in sts], axis=0)
                    return m_new, jnp.exp2(m_old - m_new), pt

                def values(h, alpha, pt):
                    rows = slice(h * ext, (h + 1) * ext)
                    acc_ref[rows, :] = acc_ref[rows, :] * alpha + _dot(vt[rows, :], pt)

                sts, soft, out = {0: scores(0), 1: scores(1)}, {}, {}
                for h in range(N_HEADS + 1):
                    if h + 2 < N_HEADS:
                        sts[h + 2] = scores(h + 2)
                    if h < N_HEADS:
                        soft[h] = softmax(h, sts.pop(h))
                    if h >= 1:
                        m_new, alpha, pt = soft.pop(h - 1)
                        values(h - 1, alpha, pt)
                        out[h - 1] = m_new
                return tuple(out[h] for h in heads)
            return step

        acc_ref[...] = jnp.zeros_like(acc_ref)
        init = (jnp.full((1, tq), NEG, F32),) * N_HEADS
        count = i + 1
        carry = lax.fori_loop(0, (count + 1) // 2 - 1, lambda j, c: make_step(False, 2)(2 * j, c), init)
        carry = lax.cond(count % 2 == 0, lambda c: make_step(True, 2)(i - 1, c), lambda c: make_step(True, 1)(i, c), carry)
        dens = [acc_ref[h * ext + HALF:h * ext + HALF + 1, :] for h in heads]
        acc_t = jnp.concatenate([acc_ref[h * ext:h * ext + HALF, :] * (1.0 / dens[h]) for h in heads], axis=0)
        o_ref[...] = acc_t.T
        for h in heads:
            lse_ref[0, h // 4, h % 4:h % 4 + 1, :] = carry[h] + jnp.log2(dens[h])

    n_tok = qf.shape[0]
    return pl.pallas_call(
        body, name="mla_fwd", grid=(n_seq, nq),
        out_shape=[jax.ShapeDtypeStruct((n_tok, 512), F32), jax.ShapeDtypeStruct((n_seq, 2, 4, seq), F32)],
        in_specs=[pl.BlockSpec((tq, 1024), lambda b, i: (b * nq + i, 0)),
                  pl.BlockSpec((seq, 1024), lambda b, i: (b, 0)),
                  pl.BlockSpec((seq, 512), lambda b, i: (b, 0))],
        out_specs=[pl.BlockSpec((tq, 512), lambda b, i: (b * nq + i, 0)),
                   pl.BlockSpec((1, 2, 4, tq), lambda b, i: (b, 0, 0, i))],
        scratch_shapes=[pltpu.VMEM((N_HEADS * ext, seq), BF16), pltpu.VMEM((N_HEADS * ext, tq), F32)],
        compiler_params=_params(2),
    )(qf, kf, v)


def _mla_bwd_call(qf, kf, v, do, delta, lse, n_seq, seq):
    tq = min(ATT_TILE, seq)
    nq = seq // tq

    nh = 4
    heads = range(nh)
    lanes = [slice(h * HEAD_LANES, (h + 1) * HEAD_LANES) for h in heads]

    def body(q_ref, k_ref, v_ref, do_ref, dl_ref, lse_ref, dq_ref, dk_ref, dv_ref,
             kt_ref, dot_ref, dqt_ref, dvt_ref):
        eye = _eye()
        sub_lo = lax.broadcasted_iota(jnp.int32, (HEAD_LANES, 1), 0) < HALF

        for t in range(nq):
            r = slice(t * tq, (t + 1) * tq)
            kv = k_ref[r, :]
            for h in heads:
                kt_ref[lanes[h], r] = _dot_nt(eye, kv[:, lanes[h]]).astype(BF16)
            for p in range(nh // 2):
                dov = do_ref[r, lanes[p]]
                dt = _dot_nt(eye, dov)
                dot_ref[2 * p, :, r] = jnp.where(sub_lo, dt, 0.0).astype(BF16)
                dot_ref[2 * p + 1, :, r] = jnp.where(sub_lo, 0.0, dt).astype(BF16)
        dqt_ref[...] = jnp.zeros_like(dqt_ref)
        dvt_ref[...] = jnp.zeros_like(dvt_ref)

        def flush_dv(tile, which):
            rows = pl.ds(pl.multiple_of(tile * tq, tq), tq)
            for p in range(nh // 2):
                dv_ref[rows, lanes[p]] = dvt_ref[which, p * HEAD_LANES:(p + 1) * HEAD_LANES, :].T

        def k_step(kt, _):
            slot = kt % 2
            kr = pl.ds(pl.multiple_of(kt * tq, tq), tq)
            k = k_ref[kr, :]
            vv = v_ref[kr, :]
            k_t = kt_ref[:, kr]
            krow = kt * tq + lax.broadcasted_iota(jnp.int32, (tq, 1), 0)

            def make_step(masked, n_tiles):
                def q_step(qt0, carry):
                    tiles = range(n_tiles)
                    qrs = [pl.ds(pl.multiple_of((qt0 + t) * tq, tq), tq) for t in tiles]
                    if masked:
                        flush_dv(jnp.maximum(kt - 1, 0), 1 - slot)
                    qs = [q_ref[qr, :] for qr in qrs]
                    if masked:
                        keep = krow <= (qt0 * tq + lax.broadcasted_iota(jnp.int32, (1, tq), 1))

                    def scores(h):
                        do_ts = [dot_ref[h, :, qr] for qr in qrs]
                        sts = [_dot_nt(k[:, lanes[h]], qs[t][:, lanes[h]]) for t in tiles]
                        dpts = [_dot(vv[:, lanes[h // 2]], do_ts[t]) for t in tiles]
                        return do_ts, sts, dpts

                    def softmax(h, sts, dpts):
                        pts, dsts = [], []
                        for t in tiles:
                            pt = jnp.exp2(sts[t] - lse_ref[0, 0, h:h + 1, qrs[t]])
                            if masked and t == 0:
                                pt = jnp.where(keep, pt, 0.0)
                            dsts.append((pt * (dpts[t] - dl_ref[0, h:h + 1, qrs[t]])).astype(BF16))
                            pts.append(pt.astype(BF16))
                        return pts, dsts

                    def grads(h, do_ts, pts, dsts):
                        half = slice((h % 2) * HALF, (h % 2 + 1) * HALF)
                        dst_all = jnp.concatenate(dsts, axis=1)
                        pt_all = jnp.concatenate(pts, axis=1)
                        do_all = jnp.concatenate([do_ts[t][half, :] for t in tiles], axis=1)
                        q_all = jnp.concatenate([qs[t][:, lanes[h]] for t in tiles], axis=0)
                        dvt_ref[slot, h * HALF:(h + 1) * HALF, :] += _dot_nt(do_all, pt_all)
                        dk_ref[kr, lanes[h]] += _dot(dst_all, q_all)
                        for t in tiles:
                            dqt_ref[lanes[h], qrs[t]] += _dot(k_t[lanes[h], :], dsts[t])

                    first, second = {0: scores(0)}, {}
                    for h in range(nh + 1):
                        if h + 1 < nh:
                            first[h + 1] = scores(h + 1)
                        if h < nh:
                            do_ts, sts, dpts = first.pop(h)
                            second[h] = (do_ts,) + softmax(h, sts, dpts)
                        if h >= 1:
                            grads(h - 1, *second.pop(h - 1))
                    return carry
                return q_step

            dk_ref[kr, :] = jnp.zeros((tq, nh * HEAD_LANES), F32)
            dvt_ref[slot] = jnp.zeros(dvt_ref.shape[1:], F32)
            count = nq - kt
            lax.cond(count >= 2, lambda c: make_step(True, 2)(kt, c), lambda c: make_step(True, 1)(kt, c), 0)
            lax.fori_loop(1, count // 2, lambda j, c: make_step(False, 2)(kt + 2 * j, c), 0)
            lax.cond(jnp.logical_and(count % 2 == 1, count >= 3), lambda c: make_step(False, 1)(nq - 1, c), lambda c: c, 0)
            return 0

        lax.fori_loop(0, nq, k_step, 0)
        flush_dv(nq - 1, (nq - 1) % 2)
        for t in range(nq):
            r = slice(t * tq, (t + 1) * tq)
            for h in heads:
                dq_ref[r, lanes[h]] = dqt_ref[lanes[h], r].T

    n_tok = qf.shape[0]
    groups = N_HEADS // nh
    blk = lambda w: pl.BlockSpec((seq, w), lambda b, g: (b, g))
    return pl.pallas_call(
        body, name="mla_bwd", grid=(n_seq, groups),
        out_shape=[jax.ShapeDtypeStruct((n_tok, 1024), F32), jax.ShapeDtypeStruct((n_tok, 1024), F32),
                   jax.ShapeDtypeStruct((n_tok, 512), F32)],
        in_specs=[blk(512), blk(512), blk(256), blk(256), pl.BlockSpec((1, nh, seq), lambda b, g: (g, 0, b)),
                  pl.BlockSpec((1, 1, nh, seq), lambda b, g: (b, g, 0, 0))],
        out_specs=[blk(512), blk(512), blk(256)],
        scratch_shapes=[pltpu.VMEM((nh * HEAD_LANES, seq), BF16), pltpu.VMEM((nh, HEAD_LANES, seq), BF16),
                        pltpu.VMEM((nh * HEAD_LANES, seq), F32), pltpu.VMEM((2, nh * HALF, tq), F32)],
        compiler_params=_params(2),
    )(qf, kf, v, do, delta, lse)


SWA_BLOCKS = 4


def _swa_block(n, pos_col_ref, posq):
    w = SWA_WINDOW
    start = pl.multiple_of(jnp.maximum(n - 1, 0) * w, w)
    posk = pos_col_ref[pl.ds(start, 2 * w), :]
    rel = (n * w + lax.broadcasted_iota(jnp.int32, (1, w), 1)) - (start + lax.broadcasted_iota(jnp.int32, (2 * w, 1), 0))
    valid = jnp.logical_and(rel >= 0, rel < w)
    return start, jnp.where(valid, posq - posk, 1e30)


def _alibi(h):
    return LOG2E * 2.0 ** -(h + 1)


def _transpose_rows(eye, src_ref, dst_ref, seq, width):
    step = 2 * SWA_WINDOW
    for t in range(seq // step):
        for p in range(width // HEAD_LANES):
            lanes = slice(p * HEAD_LANES, (p + 1) * HEAD_LANES)
            dst_ref[lanes, t * step:(t + 1) * step] = _dot_nt(eye, src_ref[t * step:(t + 1) * step, lanes]).astype(BF16)


def _swa_fwd_call(qs, kd, vd, pos_col, pos_row, sinks, n_seq, seq):
    w = SWA_WINDOW
    qb = SWA_BLOCKS
    steps = seq // (qb * w)
    ext = HALF + 16

    def body(q_ref, k_ref, v_ref, pc_ref, pr_ref, sink_ref, o_ref, lse_ref, vt_ref):
        n = pl.program_id(1)
        lo = _lane_lo()
        hi = jnp.logical_not(lo)
        eye = _eye()

        @pl.when(n == 0)
        def _():
            step = 2 * w
            for kv in range(2):
                vt_ref[kv * ext + HALF:(kv + 1) * ext, :] = jnp.ones((16, seq), BF16)
                for t in range(seq // step):
                    v_t = _dot_nt(eye, v_ref[t * step:(t + 1) * step, kv * HEAD_LANES:(kv + 1) * HEAD_LANES])
                    vt_ref[kv * ext:kv * ext + HALF, t * step:(t + 1) * step] = v_t[:HALF, :].astype(BF16)

        heads = range(N_HEADS)
        blocks = range(qb)
        geo = [_swa_block(n * qb + bi, pc_ref, pr_ref[bi]) for bi in blocks]
        wins = [pl.ds(g[0], 2 * w) for g in geo]
        kwins = [k_ref[win, :] for win in wins]
        vts = [vt_ref[:, win] for win in wins]
        sts = []
        for bi in blocks:
            q = q_ref[bi * w:(bi + 1) * w, :]
            sts.append([])
            for j in range(N_HEADS // 2):
                qp = q[:, j * HEAD_LANES:(j + 1) * HEAD_LANES]
                both = jnp.concatenate([jnp.where(lo, qp, jnp.zeros_like(qp)), jnp.where(hi, qp, jnp.zeros_like(qp))], axis=0)
                st = _dot_nt(kwins[bi][:, (j // 2) * HEAD_LANES:(j // 2 + 1) * HEAD_LANES], both)
                sts[bi] += [st[:, :w], st[:, w:]]
        ps, ms = [], []
        for bi in blocks:
            ps.append([])
            ms.append([])
            for h in heads:
                s = sts[bi][h] - _alibi(h) * geo[bi][1]
                m = jnp.maximum(jnp.max(s, axis=0, keepdims=True), sink_ref[0, h] * LOG2E)
                ps[bi].append(jnp.exp2(s - m).astype(BF16))
                ms[bi].append(m)
        for bi in blocks:
            ots = []
            for h in heads:
                pv = _dot(vts[bi][(h // 4) * ext:(h // 4 + 1) * ext, :], ps[bi][h])
                l = pv[HALF:HALF + 1, :] + jnp.exp2(sink_ref[0, h] * LOG2E - ms[bi][h])
                ots.append(pv[:HALF, :] * (1.0 / l))
                lse_ref[0, h:h + 1, bi * w:(bi + 1) * w] = ms[bi][h] + jnp.log2(l)
            o_ref[bi * w:(bi + 1) * w, :] = jnp.concatenate(ots, axis=0).T

    n_tok = qs.shape[0]
    tok = lambda width: pl.BlockSpec((qb * w, width), lambda b, n: (b * steps + n, 0))
    whole = lambda width: pl.BlockSpec((seq, width), lambda b, n: (b, 0))
    return pl.pallas_call(
        body, name="swa_fwd", grid=(n_seq, steps),
        out_shape=[jax.ShapeDtypeStruct((n_tok, 512), F32), jax.ShapeDtypeStruct((n_seq, N_HEADS, seq), F32)],
        in_specs=[tok(512), whole(256), whole(256), whole(1), pl.BlockSpec((qb, 1, w), lambda b, n: (b * steps + n, 0, 0)),
                  pl.BlockSpec(memory_space=pltpu.SMEM)],
        out_specs=[tok(512), pl.BlockSpec((1, N_HEADS, qb * w), lambda b, n: (b, 0, n))],
        scratch_shapes=[pltpu.VMEM((2 * ext, seq), BF16)],
        compiler_params=_params(2),
    )(qs, kd, vd, pos_col, pos_row, sinks)


def _swa_bwd_call(qs, kd, vd, do, delta, lse, pos_col, pos_row, sinks, g_out, n_seq, seq):
    w = SWA_WINDOW
    qb = SWA_BLOCKS
    steps = seq // (qb * w)
    reduced, reduce_scratch = _reduce_operands(g_out)

    def body(q_ref, k_ref, v_ref, do_ref, dl_ref, lse_ref, pc_ref, pr_ref, sink_ref, g_ref, dq_ref, dk_ref, dv_ref,
             dsink_ref, f_ref, kt_ref, *reduce_refs):
        b, n = pl.program_id(0), pl.program_id(1)
        _grad_reduce(b * steps + n, n_seq * steps, g_ref, f_ref, *reduce_refs)
        lo = _lane_lo()
        hi = jnp.logical_not(lo)
        sub_lo = lax.broadcasted_iota(jnp.int32, (HEAD_LANES, 1), 0) < HALF
        eye = _eye()

        @pl.when(n == 0)
        def _():
            dk_ref[...] = jnp.zeros_like(dk_ref)
            dv_ref[...] = jnp.zeros_like(dv_ref)
            _transpose_rows(eye, k_ref, kt_ref, seq, 2 * HEAD_LANES)

        @pl.when(jnp.logical_and(n == 0, b == 0))
        def _():
            dsink_ref[...] = jnp.zeros_like(dsink_ref)

        heads = range(N_HEADS)
        blocks = range(qb)
        kv_lanes = lambda h: slice((h // 4) * HEAD_LANES, (h // 4 + 1) * HEAD_LANES)
        geo = [_swa_block(n * qb + bi, pc_ref, pr_ref[bi]) for bi in blocks]
        wins = [pl.ds(g[0], 2 * w) for g in geo]
        kwins = [k_ref[win, :] for win in wins]
        vwins = [v_ref[win, :] for win in wins]

        do_ts, deltas, qms, doms = [], [], [], []
        for bi in blocks:
            rows = slice(bi * w, (bi + 1) * w)
            for lst in (do_ts, deltas, qms, doms):
                lst.append([])
            for j in range(N_HEADS // 2):
                pair = slice(j * HEAD_LANES, (j + 1) * HEAD_LANES)
                dop = do_ref[rows, pair]
                qp = q_ref[rows, pair]
                dt = _dot_nt(eye, dop)
                for hh in range(2):
                    half = lo if hh == 0 else hi
                    do_ts[bi].append(jnp.where(sub_lo, dt, 0.0).astype(BF16) if hh == 0
                                     else jnp.where(sub_lo, 0.0, dt).astype(BF16))
                    deltas[bi].append(dl_ref[2 * j + hh:2 * j + hh + 1, rows])
                    qms[bi].append(jnp.where(half, qp, jnp.zeros_like(qp)))
                    doms[bi].append(jnp.where(half, dop, jnp.zeros_like(dop)))
        sts, dpts = [], []
        for bi in blocks:
            sts.append([])
            dpts.append([])
            for j in range(N_HEADS // 2):
                a, b = 2 * j, 2 * j + 1
                st = _dot_nt(kwins[bi][:, kv_lanes(a)], jnp.concatenate([qms[bi][a], qms[bi][b]], axis=0))
                dpt = _dot(vwins[bi][:, kv_lanes(a)], jnp.concatenate([do_ts[bi][a], do_ts[bi][b]], axis=1))
                sts[bi] += [st[:, :w], st[:, w:]]
                dpts[bi] += [dpt[:, :w], dpt[:, w:]]
        pts, dsts = [], []
        for bi in blocks:
            pts.append([])
            dsts.append([])
            for h in heads:
                lse_h = lse_ref[0, h:h + 1, bi * w:(bi + 1) * w]
                pt = jnp.exp2(sts[bi][h] - _alibi(h) * geo[bi][1] - lse_h)
                dsts[bi].append((pt * (dpts[bi][h] - deltas[bi][h])).astype(BF16))
                pts[bi].append(pt.astype(BF16))
                dsink_ref[h:h + 1, :] += -jnp.exp2(sink_ref[0, h] * LOG2E - lse_h) * deltas[bi][h]
        for bi in blocks:
            for kv in range(2):
                group = range(4 * kv, 4 * kv + 4)
                dst_all = jnp.concatenate([dsts[bi][h] for h in group], axis=1)
                pt_all = jnp.concatenate([pts[bi][h] for h in group], axis=1)
                q_all = jnp.concatenate([qms[bi][h] for h in group], axis=0)
                do_all = jnp.concatenate([doms[bi][h] for h in group], axis=0)
                dk_ref[wins[bi], kv_lanes(4 * kv)] += _dot(dst_all, q_all)
                dv_ref[wins[bi], kv_lanes(4 * kv)] += _dot(pt_all, do_all)
        for bi in blocks:
            ktw = kt_ref[:, wins[bi]]
            for j in range(N_HEADS // 2):
                k_t = ktw[kv_lanes(2 * j), :]
                both = _dot(k_t, jnp.concatenate([dsts[bi][2 * j], dsts[bi][2 * j + 1]], axis=1))
                dq_t = jnp.where(sub_lo, both[:, :w], both[:, w:])
                dq_ref[bi * w:(bi + 1) * w, j * HEAD_LANES:(j + 1) * HEAD_LANES] = dq_t.T * SWA_SCALE

    n_tok = qs.shape[0]
    tok = lambda width: pl.BlockSpec((qb * w, width), lambda b, n: (b * steps + n, 0))
    whole = lambda width: pl.BlockSpec((seq, width), lambda b, n: (b, 0))
    return pl.pallas_call(
        body, name="swa_bwd", grid=(n_seq, steps),
        out_shape=[jax.ShapeDtypeStruct((n_tok, 512), F32), jax.ShapeDtypeStruct((n_tok, 256), F32),
                   jax.ShapeDtypeStruct((n_tok, 256), F32), jax.ShapeDtypeStruct((N_HEADS, HEAD_LANES), F32), reduced],
        in_specs=[tok(512), whole(256), whole(256), pl.BlockSpec((qb * w, 512), lambda b, n: (b * steps + n, 1)),
                  pl.BlockSpec((N_HEADS, qb * w), lambda b, n: (0, b * steps + n)),
                  pl.BlockSpec((1, N_HEADS, qb * w), lambda b, n: (b, 0, n)),
                  whole(1), pl.BlockSpec((qb, 1, w), lambda b, n: (b * steps + n, 0, 0)),
                  pl.BlockSpec(memory_space=pltpu.SMEM), ANY_SPEC],
        out_specs=[tok(512), whole(256), whole(256), _full((N_HEADS, HEAD_LANES)), ANY_SPEC],
        scratch_shapes=[pltpu.VMEM((2 * HEAD_LANES, seq), BF16)] + reduce_scratch,
        compiler_params=_params(2),
    )(qs, kd, vd, do, delta, lse, pos_col, pos_row, sinks, g_out)


def _post_call(x, target, o_mla, o_swa, gates, mod, b_ada, fg, w_out, seq):
    n_tok = x.shape[0]
    tm = min(TOKEN_TILE, seq)
    per_seq = seq // tm
    n_seq = n_tok // seq

    def body(x_ref, t_ref, om_ref, os_ref, g_ref, mod_ref, bada_ref, fg_ref, w_ref,
             dx2_ref, do_ref, dg_ref, gw_ref, gfg_ref, dgate_ref, loss_ref, dmla_ref, dswa_ref):
        i = pl.program_id(0)

        @pl.when(i == 0)
        def _():
            gw_ref[...] = jnp.zeros_like(gw_ref)
            gfg_ref[...] = jnp.zeros_like(gfg_ref)
            loss_ref[...] = jnp.zeros_like(loss_ref)

        @pl.when(i % per_seq == 0)
        def _():
            dgate_ref[...] = jnp.zeros_like(dgate_ref)

        gate = mod_ref[0][:, 2 * D_MODEL:] + bada_ref[:, 2 * D_MODEL:]
        fgv = fg_ref[...]
        fgd = fgv * (1.0 / D_MODEL)
        subs = _sub_tiles(tm)
        gs = [g_ref[r, :] for r in subs]
        os_ = [jnp.concatenate([om_ref[r, :], os_ref[r, :]], axis=-1) for r in subs]
        sgs = [_sigmoid(g) for g in gs]
        sils = [g * sg for g, sg in zip(gs, sgs)]
        ypres = [(o * sil).astype(BF16) for o, sil in zip(os_, sils)]
        ys = [_dot(ypre, w_ref[...]) for ypre in ypres]
        dys, loss, gfg, dgate = [], 0.0, 0.0, 0.0
        for r, y in zip(subs, ys):
            x2 = x_ref[r, :] + gate * y
            r2 = lax.rsqrt(jnp.mean(x2 * x2, axis=-1, keepdims=True) + EPS)
            xn2 = x2 * r2
            err = xn2 * fgv - t_ref[r, :]
            loss = loss + jnp.sum(jnp.sum(err * err, axis=-1, keepdims=True), axis=0, keepdims=True)
            gfg = gfg + jnp.sum(err * xn2, axis=0, keepdims=True)
            dxn2 = err * fgd
            dx2 = r2 * (dxn2 - xn2 * jnp.mean(dxn2 * xn2, axis=-1, keepdims=True))
            dx2_ref[r, :] = dx2
            dgate = dgate + jnp.sum(dx2 * y, axis=0, keepdims=True)
            dys.append((dx2 * gate).astype(BF16))
        loss_ref[...] += jnp.broadcast_to(loss * (0.5 / D_MODEL), loss_ref.shape)
        gfg_ref[...] += gfg * (1.0 / D_MODEL)
        dgate_ref[0] += dgate
        gw_ref[...] += _dot_tn(jnp.concatenate(ypres, axis=0), jnp.concatenate(dys, axis=0))
        dypres = [_dot_nt(dy, w_ref[...]) for dy in dys]
        pick = jnp.where(jnp.right_shift(lax.broadcasted_iota(jnp.int32, (2 * N_HEADS, D_MODEL), 1), 6)
                         == lax.broadcasted_iota(jnp.int32, (2 * N_HEADS, D_MODEL), 0), 1.0, 0.0).astype(BF16)
        for r, dypre, o, g, sg, sil in zip(subs, dypres, os_, gs, sgs, sils):
            dov = (dypre * sil).astype(BF16)
            do_ref[r, :] = dov
            delta = _dot_nt(pick, (dov.astype(F32) * o).astype(BF16))
            for grp in range(2):
                dmla_ref[grp, :, r] = delta[4 * grp:4 * grp + 4, :]
            dswa_ref[:, r] = delta[N_HEADS:, :]
            dg_ref[r, :] = (dypre * o * (sg + sil * (1.0 - sg))).astype(BF16)

    tok = lambda w: pl.BlockSpec((tm, w), lambda i: (i, 0))
    per_b = pl.BlockSpec((1, 1, 3 * D_MODEL), lambda i: (i // per_seq, 0, 0))
    return pl.pallas_call(
        body, name="post", grid=(n_tok // tm,),
        out_shape=[jax.ShapeDtypeStruct((n_tok, D_MODEL), F32), jax.ShapeDtypeStruct((n_tok, D_MODEL), BF16),
                   jax.ShapeDtypeStruct((n_tok, D_MODEL), BF16), jax.ShapeDtypeStruct((D_MODEL, D_MODEL), F32),
                   jax.ShapeDtypeStruct((1, D_MODEL), F32), jax.ShapeDtypeStruct((n_seq, 1, D_MODEL), F32),
                   jax.ShapeDtypeStruct((1, HEAD_LANES), F32),
                   jax.ShapeDtypeStruct((2, N_HEADS // 2, n_tok), F32), jax.ShapeDtypeStruct((N_HEADS, n_tok), F32)],
        in_specs=[tok(D_MODEL), tok(D_MODEL), tok(512), tok(512), tok(D_MODEL), per_b, _full(b_ada.shape),
                  _full(fg.shape), _full(w_out.shape)],
        out_specs=[tok(D_MODEL), tok(D_MODEL), tok(D_MODEL), _full((D_MODEL, D_MODEL)), _full((1, D_MODEL)),
                   pl.BlockSpec((1, 1, D_MODEL), lambda i: (i // per_seq, 0, 0)), _full((1, HEAD_LANES)),
                   pl.BlockSpec((2, N_HEADS // 2, tm), lambda i: (0, 0, i)), pl.BlockSpec((N_HEADS, tm), lambda i: (0, i))],
        compiler_params=_params(1),
    )(x, target, o_mla, o_swa, gates, mod, b_ada, fg, w_out)


def _mid_bwd_call(dqf, dkf, dv, zqkv, rope, qg, kvg, wq2, wkv, seq):
    n_tok = dqf.shape[0]
    tm = min(TOKEN_TILE, seq)

    def body(dq_ref, dk_ref, dv_ref, z_ref, rope_ref, qg_ref, kvg_ref, wq_ref, wkv_ref,
             dz_ref, gwq_ref, gwkv_ref, gqg_ref, gkvg_ref):
        i = pl.program_id(0)

        @pl.when(i == 0)
        def _():
            gwq_ref[...] = jnp.zeros_like(gwq_ref)
            gwkv_ref[...] = jnp.zeros_like(gwkv_ref)
            gqg_ref[...] = jnp.zeros_like(gqg_ref)
            gkvg_ref[...] = jnp.zeros_like(gkvg_ref)

        cos, sin = rope_ref[:, :HEAD_LANES], rope_ref[:, HEAD_LANES:]
        cf, sf = jnp.tile(cos, (1, N_HEADS)), jnp.tile(sin, (1, N_HEADS))
        dq = dq_ref[...] * MLA_SCALE
        dqr = jnp.concatenate([dq * cf, dq * sf], axis=-1).astype(BF16)
        zq, zkv = z_ref[:, :Q_LORA], z_ref[:, Q_LORA:]
        qgv, kvgv = qg_ref[...], kvg_ref[...]

        rq = lax.rsqrt(jnp.mean(zq * zq, axis=-1, keepdims=True) + EPS)
        xq = zq * rq
        gwq_ref[...] += _dot_tn((xq * qgv).astype(BF16), dqr)
        dqn = _dot_nt(dqr, wq_ref[...])
        gqg_ref[...] += jnp.sum(dqn * xq, axis=0, keepdims=True)
        dxq = dqn * qgv
        dz_ref[:, :Q_LORA] = (rq * (dxq - xq * jnp.mean(dxq * xq, axis=-1, keepdims=True))).astype(BF16)

        dk = dk_ref[...] * LN2
        dkv = jnp.concatenate([dk, dv_ref[...]], axis=-1).astype(BF16)
        rkv = lax.rsqrt(jnp.mean(zkv * zkv, axis=-1, keepdims=True) + EPS)
        xkv = zkv * rkv
        gwkv_ref[...] += _dot_tn((xkv * kvgv).astype(BF16), dkv)
        dkvn = _dot_nt(dkv, wkv_ref[...])
        gkvg_ref[...] += jnp.sum(dkvn * xkv, axis=0, keepdims=True)
        dxkv = dkvn * kvgv
        dz_ref[:, Q_LORA:A_KR] = (rkv * (dxkv - xkv * jnp.mean(dxkv * xkv, axis=-1, keepdims=True))).astype(BF16)

        dkpe = dk[:, :HEAD_LANES]
        for h in range(1, N_HEADS):
            dkpe = dkpe + dk[:, h * HEAD_LANES:(h + 1) * HEAD_LANES]
        dz_ref[:, A_KR:] = (jnp.where(_lane_lo(), 0.0, dkpe * cos) + pltpu.roll(dkpe * sin, HALF, 1)).astype(BF16)

    tok = lambda w: pl.BlockSpec((tm, w), lambda i: (i, 0))
    return pl.pallas_call(
        body, name="mid_bwd", grid=(n_tok // tm,),
        out_shape=[jax.ShapeDtypeStruct((n_tok, A_GM), BF16),
                   jax.ShapeDtypeStruct(wq2.shape, F32), jax.ShapeDtypeStruct(wkv.shape, F32),
                   jax.ShapeDtypeStruct((1, Q_LORA), F32), jax.ShapeDtypeStruct((1, KV_LORA), F32)],
        in_specs=[tok(1024), tok(1024), tok(512), tok(640), tok(2 * HEAD_LANES), _full(qg.shape), _full(kvg.shape),
                  _full(wq2.shape), _full(wkv.shape)],
        out_specs=[tok(A_GM), _full(wq2.shape), _full(wkv.shape), _full((1, Q_LORA)), _full((1, KV_LORA))],
        compiler_params=_params(1),
    )(dqf, dkf, dv, zqkv, rope, qg, kvg, wq2, wkv)


def _in_bwd_call(x, dx2, dz, dg, dqs, dkd, dvd, mod, b_ada, ng, wa, seq):
    n_tok = x.shape[0]
    tm = min(TOKEN_TILE, seq)
    per_seq = seq // tm
    n_seq = n_tok // seq

    def body(x_ref, dx2_ref, dz_ref, dg_ref, dqs_ref, dkd_ref, dvd_ref, mod_ref, bada_ref, ng_ref,
             wa_ref, gx_ref, gwa_ref, gng_ref, dshift_ref, dscale_ref):
        i = pl.program_id(0)

        @pl.when(i == 0)
        def _():
            gwa_ref[...] = jnp.zeros_like(gwa_ref)
            gng_ref[...] = jnp.zeros_like(gng_ref)

        @pl.when(i % per_seq == 0)
        def _():
            dshift_ref[...] = jnp.zeros_like(dshift_ref)
            dscale_ref[...] = jnp.zeros_like(dscale_ref)

        xv = x_ref[...]
        modv = mod_ref[0] + bada_ref[...]
        shift, scale = modv[:, :D_MODEL], modv[:, D_MODEL:2 * D_MODEL]
        ngv = ng_ref[...]
        r1 = lax.rsqrt(jnp.mean(xv * xv, axis=-1, keepdims=True) + EPS)
        xn = xv * r1
        hb = ((xn * ngv) * (1.0 + scale) + shift).astype(BF16)

        dgv = dg_ref[...]
        pieces = [(A_ZQ, dz_ref[...]), (A_GM, dgv[:, :512]), (A_QS, dqs_ref[...].astype(BF16)),
                  (A_KS, jnp.concatenate([_once(dkd_ref[...]) * LN2, _once(dvd_ref[...])], axis=1).astype(BF16)),
                  (A_GS, dgv[:, 512:])]
        dh = None
        for off, piece in pieces:
            wd = piece.shape[1]
            gwa_ref[:, off:off + wd] += _dot_tn(hb, piece)
            term = _dot_nt(piece, wa_ref[:, off:off + wd])
            dh = term if dh is None else dh + term

        dshift_ref[0] += jnp.sum(dh, axis=0, keepdims=True)
        dscale_ref[0] += jnp.sum(dh * (xn * ngv), axis=0, keepdims=True)
        gng_ref[...] += jnp.sum(dh * xn * (1.0 + scale), axis=0, keepdims=True)
        dxn = dh * ngv * (1.0 + scale)
        gx_ref[...] = dx2_ref[...] + r1 * (dxn - xn * jnp.mean(dxn * xn, axis=-1, keepdims=True))

    tok = lambda w: pl.BlockSpec((tm, w), lambda i: (i, 0))
    per_b = lambda w: pl.BlockSpec((1, 1, w), lambda i: (i // per_seq, 0, 0))
    return pl.pallas_call(
        body, name="in_bwd", grid=(n_tok // tm,),
        out_shape=[jax.ShapeDtypeStruct((n_tok, D_MODEL), F32), jax.ShapeDtypeStruct((D_MODEL, A_END), F32),
                   jax.ShapeDtypeStruct((1, D_MODEL), F32),
                   jax.ShapeDtypeStruct((n_seq, 1, D_MODEL), F32), jax.ShapeDtypeStruct((n_seq, 1, D_MODEL), F32)],
        in_specs=[tok(D_MODEL), tok(D_MODEL), tok(A_GM), tok(D_MODEL), tok(512), tok(256), tok(256),
                  per_b(3 * D_MODEL), _full(b_ada.shape), _full(ng.shape), _full(wa.shape)],
        out_specs=[tok(D_MODEL), _full((D_MODEL, A_END)), _full((1, D_MODEL)), per_b(D_MODEL), per_b(D_MODEL)],
        compiler_params=_params(1),
    )(x, dx2, dz, dg, dqs, dkd, dvd, mod, b_ada, ng, wa)


def _adam_math(w, g, m, v):
    m_new = ADAM_B1 * m + (1.0 - ADAM_B1) * g
    v_new = ADAM_B2 * v + (1.0 - ADAM_B2) * (g * g)
    m_hat = m_new / (1.0 - ADAM_B1 ** ADAM_STEP)
    v_hat = v_new / (1.0 - ADAM_B2 ** ADAM_STEP)
    delta = -ADAM_LR * (m_hat / (jnp.sqrt(v_hat) + ADAM_EPS) + ADAM_WD * w)
    return delta, m_new, v_new


def _adam_call(name, w, g, m, v):
    rows, cols = w.shape
    tr = next((t for t in (256, 128, 88) if rows % t == 0), rows)

    def body(w_ref, g_ref, m_ref, v_ref, d_ref, mo_ref, vo_ref):
        d, mn, vn = _adam_math(w_ref[...], g_ref[...], m_ref[...], v_ref[...])
        d_ref[...] = d
        mo_ref[...] = mn
        vo_ref[...] = vn

    spec = pl.BlockSpec((tr, cols), lambda i: (i, 0))
    return pl.pallas_call(
        body, name=name, grid=(rows // tr,),
        out_shape=[jax.ShapeDtypeStruct(w.shape, F32)] * 3,
        in_specs=[spec] * 4, out_specs=[spec] * 3,
        compiler_params=_params(1),
    )(w, g, m, v)


def _ada_bwd_call(act_all, dmod_cols, w, m, v):
    rows, cols = w.shape
    tr = 256

    def body(a_ref, dm_ref, w_ref, m_ref, v_ref, g_ref, d_ref, mo_ref, vo_ref):
        g = _dot_tn(a_ref[...].astype(BF16), dm_ref[...].astype(BF16))
        d, mn, vn = _adam_math(w_ref[...], g, m_ref[...], v_ref[...])
        g_ref[...] = g
        d_ref[...] = d
        mo_ref[...] = mn
        vo_ref[...] = vn

    spec = pl.BlockSpec((tr, cols), lambda i: (i, 0))
    nb = act_all.shape[0]
    return pl.pallas_call(
        body, name="ada_bwd", grid=(rows // tr,),
        out_shape=[jax.ShapeDtypeStruct(w.shape, F32)] * 4,
        in_specs=[pl.BlockSpec((nb, tr), lambda i: (0, i)), _full(dmod_cols.shape), spec, spec, spec],
        out_specs=[spec] * 4,
        compiler_params=_params(1),
    )(act_all, dmod_cols, w, m, v)


SMALL_ROW = {"norm_gain": (0, 1024), "final_gain": (1024, 2048), "q_norm_gain": (2048, 2432),
             "kv_norm_gain": (2432, 2688), "swa_sinks": (2688, 2696), "loss": (2816, 2944)}
SMALL_ORDER = ("b_ada", "norm_gain", "q_norm_gain", "kv_norm_gain", "swa_sinks", "final_gain")


def _small_call(parts_all, n_seq, params):
    k = len(params)

    def body(p_ref, *refs):
        ins, outs, loss_ref = refs[:3 * k], refs[3 * k:7 * k], refs[7 * k]
        row = p_ref[n_seq:n_seq + 1, :]
        for dv in range(1, 8):
            r0 = dv * ROWS_PER_DEVICE + n_seq
            row = row + p_ref[r0:r0 + 1, :]
        gb = None
        for dv in range(8):
            for r in range(n_seq):
                r0 = dv * ROWS_PER_DEVICE + r
                gb = p_ref[r0:r0 + 1, :] if gb is None else gb + p_ref[r0:r0 + 1, :]
        for j, name in enumerate(SMALL_ORDER):
            g = gb if name == "b_ada" else row[:, SMALL_ROW[name][0]:SMALL_ROW[name][1]]
            d, mn, vn = _adam_math(ins[3 * j][...], g, ins[3 * j + 1][...], ins[3 * j + 2][...])
            outs[4 * j][...] = g
            outs[4 * j + 1][...] = d
            outs[4 * j + 2][...] = mn
            outs[4 * j + 3][...] = vn
        loss_ref[...] = row[:, SMALL_ROW["loss"][0]:SMALL_ROW["loss"][1]]

    flat = [t for p in params for t in p]
    res = pl.pallas_call(
        body, name="small_update", grid=(1,),
        out_shape=[jax.ShapeDtypeStruct(p[0].shape, F32) for p in params for _ in range(4)]
        + [jax.ShapeDtypeStruct((1, HEAD_LANES), F32)],
        in_specs=[_full(parts_all.shape)] + [_full(t.shape) for t in flat],
        out_specs=[_full(p[0].shape) for p in params for _ in range(4)] + [_full((1, HEAD_LANES))],
        compiler_params=_params(1),
    )(parts_all, *flat)
    return [res[4 * j:4 * j + 4] for j in range(k)], res[4 * k]


def _rot(t):
    half = t.shape[-1] // 2
    return jnp.concatenate([-t[..., half:], t[..., :half]], axis=-1)


def _rot_t(g):
    half = g.shape[-1] // 2
    return jnp.concatenate([g[..., half:], -g[..., :half]], axis=-1)


def _columns(segments, lo, hi):
    out, at = [], 0
    for seg in segments:
        n = seg.shape[1]
        a, b = max(lo, at), min(hi, at + n)
        if a < b:
            out.append(seg[:, a - at:b - at])
        at += n
    return out


def _prepare_in(w_in_blocks):
    o = [0]
    for s in IN_SPLITS:
        o.append(o[-1] + s)
    part = lambda a, b: _columns(w_in_blocks, a, b)
    kr = jnp.concatenate(part(o[2], o[3]), axis=1)
    zero = jnp.zeros((kr.shape[0], 32), kr.dtype)
    return jnp.concatenate(part(0, o[2]) + [_rot(kr), zero, kr, zero] + part(o[3], o[8]), axis=1)


def _prepare_up(w_uq, w_ukv):
    uq = w_uq.reshape(Q_LORA, N_HEADS, MLA_NOPE + MLA_ROPE)
    zq = jnp.zeros((Q_LORA, N_HEADS, 32), w_uq.dtype)
    uq_full = jnp.concatenate([uq, zq], axis=-1).reshape(Q_LORA, 1024)
    uq_rot = jnp.concatenate([jnp.zeros((Q_LORA, N_HEADS, 64), w_uq.dtype), _rot(uq[..., MLA_NOPE:]), zq],
                             axis=-1).reshape(Q_LORA, 1024)
    wq2 = jnp.concatenate([uq_full, uq_rot], axis=1)
    ukv = w_ukv.reshape(KV_LORA, N_HEADS, 128)
    k_full = jnp.concatenate([ukv[..., :64], jnp.zeros((KV_LORA, N_HEADS, 64), w_ukv.dtype)], axis=-1).reshape(KV_LORA, 1024)
    wkv = jnp.concatenate([k_full, ukv[..., 64:].reshape(KV_LORA, 512)], axis=1)
    return wq2, wkv


def _restore_in(gwa):
    gkr = gwa[:, A_KR + 64:A_KR + 96] + _rot_t(gwa[:, A_KR:A_KR + 32])
    in_order = [gwa[:, :A_KR], gkr, gwa[:, A_GM:]]
    n = D_IN // 4
    return [jnp.concatenate(_columns(in_order, k * n, (k + 1) * n), axis=1) for k in range(4)]


def _restore_up(gwq2, gwkv):
    gf = gwq2[:, :1024].reshape(Q_LORA, N_HEADS, 128)
    gr = gwq2[:, 1024:].reshape(Q_LORA, N_HEADS, 128)
    g_uq = jnp.concatenate([gf[..., :64], gf[..., 64:96] + _rot_t(gr[..., 64:96])], axis=-1).reshape(Q_LORA, 768)
    gk = gwkv[:, :1024].reshape(KV_LORA, N_HEADS, 128)[..., :64]
    gv = gwkv[:, 1024:].reshape(KV_LORA, N_HEADS, 64)
    g_ukv = jnp.concatenate([gk, gv], axis=-1).reshape(KV_LORA, 1024)
    return g_uq, g_ukv


def _local_step(x, positions, target, mod_rows, b_ada, ng, qg, kvg, sinks, fg, w_in_b, later_shards):
    n_seq, seq, _ = x.shape
    n_tok = n_seq * seq
    x2d = x.reshape(n_tok, D_MODEL)
    t2d = target.reshape(n_tok, D_MODEL)
    pos_f = positions.astype(F32)
    pos_col = pos_f.reshape(n_tok, 1)
    pos_row = pos_f.reshape(n_tok // SWA_WINDOW, 1, SWA_WINDOW)
    mod3 = mod_rows.reshape(n_seq, 1, 3 * D_MODEL)
    inv = ROPE_THETA ** (-jnp.arange(0, MLA_ROPE, 2, dtype=F32) / MLA_ROPE)
    inv128 = jnp.tile(jnp.concatenate([inv, inv]), 4).reshape(1, HEAD_LANES)
    fg2 = fg.reshape(1, D_MODEL)

    wa = _prepare_in(w_in_b)
    zqkv, zkr, gates, qs, kd, vd, rope, f_uq, f_ukv, f_out = _pre_call(x2d, pos_col, mod3, b_ada, ng, inv128, wa,
                                                                       later_shards, seq)
    cols = lambda t, r: jnp.transpose(t.reshape(4, r, -1), (1, 0, 2)).reshape(r, -1)
    wq2, wkv = _prepare_up(cols(f_uq, Q_LORA), cols(f_ukv, KV_LORA))
    w_out_b = f_out.reshape(D_MODEL, D_MODEL)
    qf, kf, v = _up_call(zqkv, zkr, rope, qg, kvg, wq2, wkv, seq)
    o_mla, lse_mla = _mla_fwd_call(qf, kf, v, n_seq, seq)
    o_swa, lse_swa = _swa_fwd_call(qs, kd, vd, pos_col, pos_row, sinks, n_seq, seq)
    dx2, do, dg, g_out, g_fg, dgate, loss, delta_mla, delta_swa = _post_call(x2d, t2d, o_mla, o_swa, gates, mod3, b_ada, fg2, w_out_b, seq)
    dqf, dkf, dv = _mla_bwd_call(qf, kf, v, do, delta_mla, lse_mla, n_seq, seq)
    dqs, dkd, dvd, dsink, r_out = _swa_bwd_call(qs, kd, vd, do, delta_swa, lse_swa, pos_col, pos_row, sinks,
                                                g_out.reshape(4, 2, D_MODEL // 8, D_MODEL), n_seq, seq)
    dz, g_wq2, g_wkv, g_qg, g_kvg = _mid_bwd_call(dqf, dkf, dv, zqkv, rope, qg, kvg, wq2, wkv, seq)
    gx, g_wa, g_ng, dshift, dscale = _in_bwd_call(x2d, dx2, dz, dg, dqs, dkd, dvd, mod3, b_ada, ng, wa, seq)
    g_in = _restore_in(g_wa)
    g_uq, g_ukv = _restore_up(g_wq2, g_wkv)
    dmod = jnp.concatenate([dshift, dscale, dgate], axis=-1).reshape(n_seq, 3 * D_MODEL)
    small_row = jnp.concatenate([g_ng, g_fg, g_qg, g_kvg, jnp.pad(jnp.sum(dsink, axis=1).reshape(1, N_HEADS), ((0, 0), (0, 120))),
                                 loss, jnp.zeros((1, 128), F32)], axis=1)
    return gx.reshape(x.shape), (g_in, g_uq, g_ukv), r_out, small_row, dmod


def kernel(x, c, positions, w_ada, b_ada, norm_gain, w_in, q_norm_gain, kv_norm_gain, w_uq, w_ukv, swa_sinks, w_out, final_gain, loss_target, m_w_ada, m_b_ada, m_norm_gain, m_w_in, m_q_norm_gain, m_kv_norm_gain, m_w_uq, m_w_ukv, m_swa_sinks, m_w_out, m_final_gain, v_w_ada, v_b_ada, v_norm_gain, v_w_in, v_q_norm_gain, v_kv_norm_gain, v_w_uq, v_w_ukv, v_swa_sinks, v_w_out, v_final_gain):
    n_seq = x.shape[0]
    xi, yi, ci = lax.axis_index("x"), lax.axis_index("y"), lax.axis_index("c")
    dev = 4 * xi + 2 * yi + ci
    chip = 2 * xi + yi

    halves = lambda w: w.astype(BF16).reshape(2, w.shape[0] // 2, w.shape[1])
    c_blk = jnp.pad(c, ((0, ROWS_PER_DEVICE - n_seq), (0, 0)))
    act_all, pieces, f_in = _comm_fwd_call(c_blk, w_ada[0], [halves(w_in[0])])
    mine = lax.dynamic_slice_in_dim(pieces, dev * ROWS_PER_DEVICE, n_seq, axis=1)
    mod_rows = jnp.transpose(mine, (1, 0, 2)).reshape(n_seq, 3 * D_MODEL)
    w_in_blocks = [f_in[k].reshape(D_MODEL, -1) for k in range(4)]

    gx, (g_in_blocks, g_uq, g_ukv), r_out, small_row, dmod = _local_step(
        x, positions, loss_target, mod_rows, b_ada, norm_gain, q_norm_gain, kv_norm_gain, swa_sinks, final_gain,
        w_in_blocks, [halves(w_uq[0]), halves(w_ukv[0]), halves(w_out[0])])

    grads = [jnp.stack(g_in_blocks).reshape(4, 2, D_MODEL // 2, -1), _by_owner(g_uq, g_uq.shape[1] // 4),
             _by_owner(g_ukv, g_ukv.shape[1] // 4)]
    part = jnp.concatenate([dmod, small_row, jnp.zeros((ROWS_PER_DEVICE - n_seq - 1, 3 * D_MODEL), F32)], axis=0)
    r_in, r_uq, r_ukv, parts_all = _comm_bwd_call(grads, part)
    g_in_s, g_uq_s = r_in.reshape(w_in.shape[1:]), r_uq.reshape(w_uq.shape[1:])
    g_ukv_s, g_out_s = r_ukv.reshape(w_ukv.shape[1:]), r_out.reshape(w_out.shape[1:])

    tr = lambda a: jnp.swapaxes(a[0], 0, 1)
    back = lambda ts: [jnp.swapaxes(t, 0, 1) for t in ts]
    d_in, nm_in, nv_in = back(_adam_call("adam_w_in", tr(w_in), g_in_s.T, tr(m_w_in), tr(v_w_in)))
    d_uq, nm_uq, nv_uq = back(_adam_call("adam_w_uq", tr(w_uq), g_uq_s.T, tr(m_w_uq), tr(v_w_uq)))
    d_ukv, nm_ukv, nv_ukv = _adam_call("adam_w_ukv", w_ukv[0], g_ukv_s, m_w_ukv[0], v_w_ukv[0])
    d_out, nm_out, nv_out = _adam_call("adam_w_out", w_out[0], g_out_s, m_w_out[0], v_w_out[0])
    dmod_cols = lax.dynamic_slice_in_dim(parts_all, chip * 768, 768, axis=1)
    g_ada, d_ada, nm_ada, nv_ada = _ada_bwd_call(act_all, dmod_cols, w_ada[0], m_w_ada[0], v_w_ada[0])

    row = lambda t: t.reshape(1, -1)
    small = {"b_ada": (b_ada, m_b_ada, v_b_ada), "norm_gain": (norm_gain, m_norm_gain, v_norm_gain),
             "q_norm_gain": (q_norm_gain, m_q_norm_gain, v_q_norm_gain),
             "kv_norm_gain": (kv_norm_gain, m_kv_norm_gain, v_kv_norm_gain),
             "swa_sinks": (swa_sinks, m_swa_sinks, v_swa_sinks),
             "final_gain": (row(final_gain), row(m_final_gain), row(v_final_gain))}
    res, loss_row = _small_call(parts_all, n_seq, [small[name] for name in SMALL_ORDER])
    res = dict(zip(SMALL_ORDER, res))
    res["final_gain"] = [t.reshape(-1) for t in res["final_gain"]]
    e = lambda t: t[None]
    big = {"w_ada": (e(g_ada), e(d_ada), e(nm_ada), e(nv_ada)), "w_in": (e(g_in_s), e(d_in), e(nm_in), e(nv_in)),
           "w_uq": (e(g_uq_s), e(d_uq), e(nm_uq), e(nv_uq)), "w_ukv": (e(g_ukv_s), e(d_ukv), e(nm_ukv), e(nv_ukv)),
           "w_out": (e(g_out_s), e(d_out), e(nm_out), e(nv_out))}
    order = ("w_ada", "b_ada", "norm_gain", "w_in", "q_norm_gain", "kv_norm_gain", "w_uq", "w_ukv", "swa_sinks", "w_out",
             "final_gain")
    pick = lambda kind: [(big[n] if n in big else res[n])[kind] for n in order]
    return (loss_row[0, 0], gx, *pick(0), *pick(1), *pick(2), *pick(3))
```

```python
import jax
import jax.numpy as jnp
from jax import lax
from jax.experimental import pallas as pl
from jax.experimental.pallas import tpu as pltpu

F32 = jnp.float32
BF16 = jnp.bfloat16

D_MODEL = 1024
Q_LORA = 384
KV_LORA = 256
N_HEADS = 8
MLA_NOPE = 64
MLA_ROPE = 32
HEAD_LANES = 128
HALF = 64
SWA_WINDOW = 128
EPS = 1e-6
ROPE_THETA = 10000.0
MLA_SCALE = (MLA_NOPE + MLA_ROPE) ** -0.5
LOG2E = 1.4426950408889634
LN2 = 0.6931471805599453
SWA_SCALE = 64 ** -0.5
NEG = -1e30

ADAM_LR = 0.001
ADAM_B1 = 0.9
ADAM_B2 = 0.999
ADAM_EPS = 1e-08
ADAM_WD = 0.01
ADAM_STEP = 10

A_ZQ, A_ZKV, A_KR, A_GM, A_QS, A_KS, A_VS, A_GS, A_END = 0, 384, 640, 768, 1280, 1792, 1920, 2048, 2560
IN_SPLITS = (384, 256, 32, 512, 512, 128, 128, 512)
D_IN = sum(IN_SPLITS)

TOKEN_TILE = 512
ATT_TILE = 256
VMEM_LIMIT = 56 * 1024 * 1024


def _dot(a, b):
    return jnp.dot(a, b, preferred_element_type=F32)


def _dot_nt(a, b):
    return lax.dot_general(a, b, (((1,), (1,)), ((), ())), preferred_element_type=F32)


def _dot_tn(a, b):
    return lax.dot_general(a, b, (((0,), (0,)), ((), ())), preferred_element_type=F32)


def _params(n_grid):
    return pltpu.CompilerParams(dimension_semantics=("arbitrary",) * n_grid, vmem_limit_bytes=VMEM_LIMIT)


def _full(shape):
    nd = len(shape)
    return pl.BlockSpec(shape, lambda *_: (0,) * nd, pipeline_mode=pl.Buffered(1))


def _sigmoid(g):
    return 1.0 / (1.0 + jnp.exp(-g))


SUB_TILE = 256


def _sub_tiles(tm):
    sub = min(SUB_TILE, tm)
    return [slice(s * sub, (s + 1) * sub) for s in range(tm // sub)]


MESH = pl.DeviceIdType.MESH
ROWS_PER_DEVICE = 8
VMEM_SPEC = pl.BlockSpec(memory_space=pltpu.VMEM)
ANY_SPEC = pl.BlockSpec(memory_space=pl.ANY)


def _position():
    x, y, c = lax.axis_index("x"), lax.axis_index("y"), lax.axis_index("c")
    sibling = (x, y, 1 - c)
    others = [(1 - x, y, c), (x, 1 - y, c), (1 - x, 1 - y, c)]
    return (x, y, c), 4 * x + 2 * y + c, 2 * x + y, sibling, others


def _rows_of(dev):
    return pl.ds(pl.multiple_of(dev * ROWS_PER_DEVICE, ROWS_PER_DEVICE), ROWS_PER_DEVICE)


def _all_to_all_rows(block_ref, table_ref, dev, me, send_sems, recv_sems):
    x, y, c = me
    waits = []
    for k in range(1, 8):
        peer = (1 - x if k & 4 else x, 1 - y if k & 2 else y, 1 - c if k & 1 else c)
        pltpu.make_async_remote_copy(src_ref=block_ref, dst_ref=table_ref.at[_rows_of(dev)], send_sem=send_sems.at[k - 1],
                                     recv_sem=recv_sems.at[k - 1], device_id=peer, device_id_type=MESH).start()
        waits.append(pltpu.make_async_remote_copy(
            src_ref=block_ref, dst_ref=table_ref.at[_rows_of(jnp.bitwise_xor(dev, k))], send_sem=send_sems.at[k - 1],
            recv_sem=recv_sems.at[k - 1], device_id=peer, device_id_type=MESH))
    return waits


def _comm_fwd_call(c_blk, w_ada, shards):
    n = len(shards)

    def body(c_ref, wada_ref, *refs):
        w_refs, act_ref, pieces_ref, full_refs = refs[:n], refs[n], refs[n + 1], refs[n + 2:2 * n + 2]
        c_all_ref = refs[2 * n + 2]
        c_send, c_recv, p_send, p_recv, w_send, w_recv, f_send, f_recv, loc_sem = refs[2 * n + 3:]
        me, dev, chip, sibling, others = _position()
        core = me[2]
        chip_of = [2 * p[0] + p[1] for p in others]

        local = [pltpu.make_async_copy(w_refs[i], full_refs[i].at[chip], loc_sem.at[i]) for i in range(n)]
        for cp in local:
            cp.start()

        def over_ici(i, j, src_chip):
            return pltpu.make_async_remote_copy(
                src_ref=w_refs[i].at[core], dst_ref=full_refs[i].at[src_chip, core], send_sem=w_send.at[3 * i + j],
                recv_sem=w_recv.at[3 * i + j], device_id=others[j], device_id_type=MESH)

        def to_sibling(i, j, half):
            return pltpu.make_async_remote_copy(
                src_ref=full_refs[i].at[chip_of[j], half], dst_ref=full_refs[i].at[chip_of[j], half],
                send_sem=f_send.at[3 * i + j], recv_sem=f_recv.at[3 * i + j], device_id=sibling, device_id_type=MESH)

        c_all_ref[_rows_of(dev), :] = c_ref[...]
        c_waits = _all_to_all_rows(c_ref, c_all_ref, dev, me, c_send, c_recv)
        sent = [over_ici(i, j, chip) for i in range(n) for j in range(3)]
        for cp in sent:
            cp.start()

        for cp in c_waits:
            cp.wait()
        cv = c_all_ref[...]
        act = cv * _sigmoid(cv)
        act_ref[...] = act
        pieces_ref[chip] = _dot(act.astype(BF16), wada_ref[...].astype(BF16))
        piece = lambda j, src_chip: pltpu.make_async_remote_copy(
            src_ref=pieces_ref.at[chip], dst_ref=pieces_ref.at[src_chip], send_sem=p_send.at[j], recv_sem=p_recv.at[j],
            device_id=others[j], device_id_type=MESH)
        for j in range(3):
            piece(j, chip).start()

        for i in range(n):
            for j in range(3):
                over_ici(i, j, chip_of[j]).wait_recv()
                to_sibling(i, j, core).start()
        for j in range(3):
            piece(j, chip).wait_send()
            piece(j, chip_of[j]).wait_recv()
        for i in range(n):
            for j in range(3):
                to_sibling(i, j, 1 - core).wait_recv()
                to_sibling(i, j, core).wait_send()
        for cp in sent:
            cp.wait_send()
        for cp in local:
            cp.wait()

    rows = 8 * ROWS_PER_DEVICE
    dma = pltpu.SemaphoreType.DMA
    return pl.pallas_call(
        body, name="comm_fwd",
        out_shape=[jax.ShapeDtypeStruct((rows, D_MODEL), F32), jax.ShapeDtypeStruct((4, rows, w_ada.shape[1]), F32)]
        + [jax.ShapeDtypeStruct((4,) + s.shape, s.dtype) for s in shards],
        in_specs=[VMEM_SPEC, VMEM_SPEC] + [ANY_SPEC] * n,
        out_specs=[VMEM_SPEC, VMEM_SPEC] + [ANY_SPEC] * n,
        scratch_shapes=[pltpu.VMEM((rows, D_MODEL), F32), dma((7,)), dma((7,)), dma((3,)), dma((3,)),
                        dma((3 * n,)), dma((3 * n,)), dma((3 * n,)), dma((3 * n,)), dma((n,))],
        compiler_params=pltpu.CompilerParams(vmem_limit_bytes=VMEM_LIMIT),
    )(c_blk, w_ada, *shards)


def _comm_bwd_call(grads, part):
    n = len(grads)

    def body(part_ref, *refs):
        g_refs, f_refs, parts_ref = refs[:n], refs[n:2 * n], refs[2 * n]
        scratch = refs[2 * n + 1:]
        a_refs, b_refs, p_refs, r_refs = (scratch[k * n:(k + 1) * n] for k in range(4))
        s_send, s_recv, d_send, d_recv, e_send, e_recv, h_send, h_recv, loc_sem = scratch[4 * n:]
        me, dev, chip, sibling, others = _position()
        core = me[2]
        chip_of = [2 * p[0] + p[1] for p in others]

        parts_ref[_rows_of(dev), :] = part_ref[...]
        s_waits = _all_to_all_rows(part_ref, parts_ref, dev, me, s_send, s_recv)

        mine = [pltpu.make_async_copy(g_refs[i].at[:, core], a_refs[i], loc_sem.at[i]) for i in range(n)]
        swap = [pltpu.make_async_remote_copy(src_ref=g_refs[i].at[:, 1 - core], dst_ref=b_refs[i], send_sem=d_send.at[i],
                                             recv_sem=d_recv.at[i], device_id=sibling, device_id_type=MESH) for i in range(n)]
        order = sorted(range(n), key=lambda i: g_refs[i].shape[2] * g_refs[i].shape[3])
        for i in order:
            mine[i].start()
            swap[i].start()
        cross = [pltpu.make_async_remote_copy(src_ref=p_refs[i].at[chip_of[j]], dst_ref=r_refs[i].at[j],
                                              send_sem=e_send.at[3 * i + j], recv_sem=e_recv.at[3 * i + j],
                                              device_id=others[j], device_id_type=MESH) for i in range(n) for j in range(3)]
        for i in order:
            mine[i].wait()
            swap[i].wait()
            for k in range(4):
                s = a_refs[i][k] + b_refs[i][k]
                a_refs[i][k] = s
                p_refs[i][k] = s.astype(BF16)
            for j in range(3):
                cross[3 * i + j].start()
        share = {}
        for i in order:
            for j in range(3):
                cross[3 * i + j].wait()
            f_refs[i][core] = (a_refs[i][chip] + r_refs[i][0].astype(F32) + r_refs[i][1].astype(F32)
                               + r_refs[i][2].astype(F32))
            share[i] = pltpu.make_async_remote_copy(src_ref=f_refs[i].at[core], dst_ref=f_refs[i].at[core],
                                                    send_sem=h_send.at[i], recv_sem=h_recv.at[i], device_id=sibling,
                                                    device_id_type=MESH)
            share[i].start()
        for i in range(n):
            share[i].wait_send()
            pltpu.make_async_remote_copy(src_ref=f_refs[i].at[core], dst_ref=f_refs[i].at[1 - core], send_sem=h_send.at[i],
                                         recv_sem=h_recv.at[i], device_id=sibling, device_id_type=MESH).wait_recv()
        for cp in s_waits:
            cp.wait()

    rows = 8 * ROWS_PER_DEVICE
    dma = pltpu.SemaphoreType.DMA
    quarter = [(4,) + g.shape[2:] for g in grads]
    return pl.pallas_call(
        body, name="comm_bwd",
        out_shape=[jax.ShapeDtypeStruct((2,) + g.shape[2:], F32) for g in grads]
        + [jax.ShapeDtypeStruct((rows, part.shape[1]), F32)],
        in_specs=[VMEM_SPEC] + [ANY_SPEC] * n,
        out_specs=[VMEM_SPEC] * (n + 1),
        scratch_shapes=[pltpu.VMEM(q, F32) for q in quarter] + [pltpu.VMEM(q, F32) for q in quarter]
        + [pltpu.VMEM(q, BF16) for q in quarter] + [pltpu.VMEM((3,) + q[1:], BF16) for q in quarter]
        + [dma((7,)), dma((7,)), dma((n,)), dma((n,)), dma((3 * n,)), dma((3 * n,)), dma((n,)), dma((n,)), dma((n,))],
        compiler_params=pltpu.CompilerParams(vmem_limit_bytes=VMEM_LIMIT),
    )(part, *grads)


def _by_owner(g, n):
    return jnp.transpose(g.reshape(g.shape[0], 4, n), (1, 0, 2)).reshape(4, 2, g.shape[0] // 2, n)


def _reduce_operands(g):
    quarter = (4,) + g.shape[2:]
    dma = pltpu.SemaphoreType.DMA
    scratch = [pltpu.VMEM(quarter, F32), pltpu.VMEM(quarter, F32), pltpu.VMEM(quarter, BF16),
               pltpu.VMEM((3,) + quarter[1:], BF16), dma((5,)), dma((5,)), dma((2,))]
    return jax.ShapeDtypeStruct((2,) + g.shape[2:], F32), scratch


def _grad_reduce(step, n_steps, g_ref, f_ref, a_ref, b_ref, p_ref, r_ref, send, recv, loc_sem):
    me, _, chip, sibling, others = _position()
    core = me[2]
    chip_of = [2 * p[0] + p[1] for p in others]
    remote = lambda src, dst, k, to: pltpu.make_async_remote_copy(
        src_ref=src, dst_ref=dst, send_sem=send.at[k], recv_sem=recv.at[k], device_id=to, device_id_type=MESH)
    mine = pltpu.make_async_copy(g_ref.at[:, core], a_ref, loc_sem.at[0])
    swap = remote(g_ref.at[:, 1 - core], b_ref, 0, sibling)
    cross = [remote(p_ref.at[chip_of[j]], r_ref.at[j], 1 + j, others[j]) for j in range(3)]
    total_ref = b_ref.at[0]
    keep = pltpu.make_async_copy(total_ref, f_ref.at[core], loc_sem.at[1])
    share = lambda half: remote(total_ref, f_ref.at[half], 4, sibling)
    at = [k * (n_steps - 1) // 3 for k in range(4)]

    @pl.when(step == at[0])
    def _():
        mine.start()
        swap.start()

    @pl.when(step == at[1])
    def _():
        mine.wait()
        swap.wait()
        for k in range(4):
            s = a_ref[k] + b_ref[k]
            a_ref[k] = s
            p_ref[k] = s.astype(BF16)
        for cp in cross:
            cp.start()

    @pl.when(step == at[2])
    def _():
        for cp in cross:
            cp.wait()
        total_ref[...] = a_ref[chip] + r_ref[0].astype(F32) + r_ref[1].astype(F32) + r_ref[2].astype(F32)
        keep.start()
        share(core).start()

    @pl.when(step == at[3])
    def _():
        keep.wait()
        share(core).wait_send()
        share(1 - core).wait_recv()


def _twice(t):
    lo = _lane_lo()
    other = pltpu.roll(t, HALF, 1)
    return jnp.concatenate([jnp.where(lo, t, other), jnp.where(lo, other, t)], axis=1)


def _once(g):
    first, second = g[:, :HEAD_LANES], g[:, HEAD_LANES:]
    return jnp.where(_lane_lo(), first + pltpu.roll(first, HALF, 1), second + pltpu.roll(second, HALF, 1))


def _rope_tables(pos_ref, inv_row, rope_ref):
    quarter = pos_ref.shape[0] // 4
    lane = lax.broadcasted_iota(jnp.int32, (1, HEAD_LANES), 1)
    pos = [pos_ref[g * quarter:(g + 1) * quarter, :] for g in range(4)]
    ang = jnp.where(lane < 32, pos[0], jnp.where(lane < 64, pos[1], jnp.where(lane < 96, pos[2], pos[3]))) * inv_row
    cos, sin = jnp.cos(ang), jnp.sin(ang)
    rope_lanes = jnp.logical_and(lane >= HALF, lane < HALF + MLA_ROPE)
    for g in range(4):
        rows = slice(g * quarter, (g + 1) * quarter)
        shift = (HALF - 32 * g) % HEAD_LANES
        at = lambda t: t if shift == 0 else pltpu.roll(t, shift, 1)
        rope_ref[rows, :HEAD_LANES] = jnp.where(rope_lanes, at(cos), 1.0)
        rope_ref[rows, HEAD_LANES:] = jnp.where(rope_lanes, at(sin), 0.0)


def _gather_in_steps(step, n_steps, w_refs, full_refs, w_send, w_recv, f_send, f_recv, loc_sem):
    me, _, chip, sibling, others = _position()
    core = me[2]
    chip_of = [2 * p[0] + p[1] for p in others]
    n = len(w_refs)
    local = [pltpu.make_async_copy(w_refs[i], full_refs[i].at[chip], loc_sem.at[i]) for i in range(n)]

    def over_ici(i, j, src_chip):
        return pltpu.make_async_remote_copy(
            src_ref=w_refs[i].at[core], dst_ref=full_refs[i].at[src_chip, core], send_sem=w_send.at[3 * i + j],
            recv_sem=w_recv.at[3 * i + j], device_id=others[j], device_id_type=MESH)

    def to_sibling(i, j, half):
        return pltpu.make_async_remote_copy(
            src_ref=full_refs[i].at[chip_of[j], half], dst_ref=full_refs[i].at[chip_of[j], half],
            send_sem=f_send.at[3 * i + j], recv_sem=f_recv.at[3 * i + j], device_id=sibling, device_id_type=MESH)

    pairs = [(i, j) for i in range(n) for j in range(3)]

    @pl.when(step == 0)
    def _():
        for cp in local:
            cp.start()
        for i, j in pairs:
            over_ici(i, j, chip).start()

    @pl.when(step == 3 * n_steps // 4)
    def _():
        for i, j in pairs:
            over_ici(i, j, chip_of[j]).wait_recv()
            to_sibling(i, j, core).start()

    @pl.when(step == n_steps - 1)
    def _():
        for i, j in pairs:
            to_sibling(i, j, 1 - core).wait_recv()
            to_sibling(i, j, core).wait_send()
            over_ici(i, j, chip).wait_send()
        for cp in local:
            cp.wait()


def _pre_call(x, pos_col, mod, b_ada, ng, inv128, wa, shards, seq):
    n_tok = x.shape[0]
    tm = min(TOKEN_TILE, seq)
    per_seq = seq // tm
    n_steps = n_tok // tm
    n = len(shards)

    def body(x_ref, pos_ref, mod_ref, bada_ref, ng_ref, inv_ref, wa_ref, *refs):
        w_refs, refs = refs[:n], refs[n:]
        zqkv_ref, zkr_ref, gates_ref, qs_ref, kd_ref, vd_ref, rope_ref = refs[:7]
        full_refs, sems = refs[7:7 + n], refs[7 + n:]
        _gather_in_steps(pl.program_id(0), n_steps, w_refs, full_refs, *sems)
        _rope_tables(pos_ref, inv_ref[...], rope_ref)
        xv = x_ref[...]
        modv = mod_ref[0] + bada_ref[...]
        shift, scale = modv[:, :D_MODEL], modv[:, D_MODEL:2 * D_MODEL]
        r1 = lax.rsqrt(jnp.mean(xv * xv, axis=-1, keepdims=True) + EPS)
        h = ((xv * r1) * ng_ref[...]) * (1.0 + scale) + shift
        za = _dot(h.astype(BF16), wa_ref[...])
        zqkv_ref[...] = za[:, :A_KR]
        zkr_ref[...] = za[:, A_KR:A_GM]
        gates_ref[:, :512] = za[:, A_GM:A_QS]
        gates_ref[:, 512:] = za[:, A_GS:A_END]
        qs_ref[...] = (za[:, A_QS:A_KS] * (SWA_SCALE * LOG2E)).astype(BF16)
        kd_ref[...] = _twice(za[:, A_KS:A_VS]).astype(BF16)
        vd_ref[...] = _twice(za[:, A_VS:A_GS]).astype(BF16)

    tok = lambda w: pl.BlockSpec((tm, w), lambda i: (i, 0))
    outs = [(640, F32), (HEAD_LANES, F32), (1024, F32), (512, BF16), (256, BF16), (256, BF16), (2 * HEAD_LANES, F32)]
    dma = pltpu.SemaphoreType.DMA
    return pl.pallas_call(
        body, name="pre", grid=(n_steps,),
        out_shape=[jax.ShapeDtypeStruct((n_tok, w), dt) for w, dt in outs]
        + [jax.ShapeDtypeStruct((4,) + s.shape, s.dtype) for s in shards],
        in_specs=[tok(D_MODEL), tok(1), pl.BlockSpec((1, 1, 3 * D_MODEL), lambda i: (i // per_seq, 0, 0)),
                  _full(b_ada.shape), _full(ng.shape), _full(inv128.shape), _full(wa.shape)] + [ANY_SPEC] * n,
        out_specs=[tok(w) for w, _ in outs] + [ANY_SPEC] * n,
        scratch_shapes=[dma((3 * n,)), dma((3 * n,)), dma((3 * n,)), dma((3 * n,)), dma((n,))],
        compiler_params=_params(1),
    )(x, pos_col, mod, b_ada, ng, inv128, wa, *shards)


def _up_call(zqkv, zkr, rope, qg, kvg, wq2, wkv, seq):
    n_tok = zqkv.shape[0]
    tm = min(TOKEN_TILE, seq)

    def body(zqkv_ref, zkr_ref, rope_ref, qg_ref, kvg_ref, wq_ref, wkv_ref, qf_ref, kf_ref, v_ref):
        cos, sin = rope_ref[:, :HEAD_LANES], rope_ref[:, HEAD_LANES:]
        zq, zkv = zqkv_ref[:, A_ZQ:A_ZKV], zqkv_ref[:, A_ZKV:A_KR]
        rq = lax.rsqrt(jnp.mean(zq * zq, axis=-1, keepdims=True) + EPS)
        qn = ((zq * rq) * qg_ref[...]).astype(BF16)
        qr = _dot(qn, wq_ref[...])
        cf, sf = jnp.tile(cos, (1, N_HEADS)), jnp.tile(sin, (1, N_HEADS))
        qf_ref[...] = ((qr[:, :1024] * cf + qr[:, 1024:] * sf) * (MLA_SCALE * LOG2E)).astype(BF16)
        rkv = lax.rsqrt(jnp.mean(zkv * zkv, axis=-1, keepdims=True) + EPS)
        kvn = ((zkv * rkv) * kvg_ref[...]).astype(BF16)
        kv = _dot(kvn, wkv_ref[...])
        zkr = zkr_ref[...]
        kpe = jnp.where(_lane_lo(), 0.0, zkr * cos) + pltpu.roll(zkr, HALF, 1) * sin
        kf_ref[...] = (kv[:, :1024] + jnp.tile(kpe, (1, N_HEADS))).astype(BF16)
        v_ref[...] = kv[:, 1024:].astype(BF16)

    tok = lambda w: pl.BlockSpec((tm, w), lambda i: (i, 0))
    outs = [(1024, BF16), (1024, BF16), (512, BF16)]
    return pl.pallas_call(
        body, name="up", grid=(n_tok // tm,),
        out_shape=[jax.ShapeDtypeStruct((n_tok, w), dt) for w, dt in outs],
        in_specs=[tok(640), tok(HEAD_LANES), tok(2 * HEAD_LANES), _full(qg.shape), _full(kvg.shape), _full(wq2.shape),
                  _full(wkv.shape)],
        out_specs=[tok(w) for w, _ in outs],
        compiler_params=_params(1),
    )(zqkv, zkr, rope, qg, kvg, wq2, wkv)


def _lane_lo(width=HEAD_LANES):
    return lax.broadcasted_iota(jnp.int32, (1, width), 1) < HALF


def _eye(n=HEAD_LANES):
    r = lax.broadcasted_iota(jnp.int32, (n, n), 0)
    c = lax.broadcasted_iota(jnp.int32, (n, n), 1)
    return jnp.where(r == c, 1.0, 0.0).astype(BF16)


def _mla_fwd_call(qf, kf, v, n_seq, seq):
    tq = min(ATT_TILE, seq)
    nq = seq // tq

    ext = HALF + 16

    def body(q_ref, k_ref, v_ref, o_ref, lse_ref, vt_ref, acc_ref):
        i = pl.program_id(1)
        eye = _eye()

        @pl.when(i == 0)
        def _():
            for h in range(N_HEADS):
                vt_ref[h * ext + HALF:(h + 1) * ext, :] = jnp.ones((16, seq), BF16)
            for t in range(nq):
                for p in range(N_HEADS // 2):
                    pair = slice(p * HEAD_LANES, (p + 1) * HEAD_LANES)
                    v_t = _dot_nt(eye, v_ref[t * tq:(t + 1) * tq, pair]).astype(BF16)
                    for hh in range(2):
                        r0 = (2 * p + hh) * ext
                        vt_ref[r0:r0 + HALF, t * tq:(t + 1) * tq] = v_t[hh * HALF:(hh + 1) * HALF, :]

        q = q_ref[...]
        qcol = i * tq + lax.broadcasted_iota(jnp.int32, (1, tq), 1)
        heads = range(N_HEADS)
        lanes = [slice(h * HEAD_LANES, (h + 1) * HEAD_LANES) for h in heads]

        def make_step(masked, n_tiles):
            def step(kt0, carry):
                tiles = range(n_tiles)
                start = pl.multiple_of(kt0 * tq, tq)
                ks = [k_ref[pl.ds(pl.multiple_of((kt0 + t) * tq, tq), tq), :] for t in tiles]
                vt = vt_ref[:, pl.ds(start, n_tiles * tq)]
                last = n_tiles - 1
                if masked:
                    keep = ((kt0 + last) * tq + lax.broadcasted_iota(jnp.int32, (tq, 1), 0)) <= qcol

                def scores(h):
                    sts = [_dot_nt(ks[t][:, lanes[h]], q[:, lanes[h]]) for t in tiles]
                    if masked:
                        sts[last] = jnp.where(keep, sts[last], NEG)
                    return sts

                def softmax(h, sts):
                    m_old = carry[h]
                    m_new = m_old
                    for st in sts:
                        m_new = jnp.maximum(m_new, jnp.max(st, axis=0, keepdims=True))
                    pt = jnp.concatenate([jnp.exp2(st - m_new).astype(BF16) for st in sts], axis=0)
                    return m_new, jnp.exp2(m_old - m_new), pt

                def values(h, alpha, pt):
                    rows = slice(h * ext, (h + 1) * ext)
                    acc_ref[rows, :] = acc_ref[rows, :] * alpha + _dot(vt[rows, :], pt)

                sts, soft, out = {0: scores(0), 1: scores(1)}, {}, {}
                for h in range(N_HEADS + 1):
                    if h + 2 < N_HEADS:
                        sts[h + 2] = scores(h + 2)
                    if h < N_HEADS:
                        soft[h] = softmax(h, sts.pop(h))
                    if h >= 1:
                        m_new, alpha, pt = soft.pop(h - 1)
                        values(h - 1, alpha, pt)
                        out[h - 1] = m_new
                return tuple(out[h] for h in heads)
            return step

        acc_ref[...] = jnp.zeros_like(acc_ref)
        init = (jnp.full((1, tq), NEG, F32),) * N_HEADS
        count = i + 1
        carry = lax.fori_loop(0, (count + 1) // 2 - 1, lambda j, c: make_step(False, 2)(2 * j, c), init)
        carry = lax.cond(count % 2 == 0, lambda c: make_step(True, 2)(i - 1, c), lambda c: make_step(True, 1)(i, c), carry)
        dens = [acc_ref[h * ext + HALF:h * ext + HALF + 1, :] for h in heads]
        acc_t = jnp.concatenate([acc_ref[h * ext:h * ext + HALF, :] * (1.0 / dens[h]) for h in heads], axis=0)
        o_ref[...] = acc_t.T
        for h in heads:
            lse_ref[0, h // 4, h % 4:h % 4 + 1, :] = carry[h] + jnp.log2(dens[h])

    n_tok = qf.shape[0]
    return pl.pallas_call(
        body, name="mla_fwd", grid=(n_seq, nq),
        out_shape=[jax.ShapeDtypeStruct((n_tok, 512), F32), jax.ShapeDtypeStruct((n_seq, 2, 4, seq), F32)],
        in_specs=[pl.BlockSpec((tq, 1024), lambda b, i: (b * nq + i, 0)),
                  pl.BlockSpec((seq, 1024), lambda b, i: (b, 0)),
                  pl.BlockSpec((seq, 512), lambda b, i: (b, 0))],
        out_specs=[pl.BlockSpec((tq, 512), lambda b, i: (b * nq + i, 0)),
                   pl.BlockSpec((1, 2, 4, tq), lambda b, i: (b, 0, 0, i))],
        scratch_shapes=[pltpu.VMEM((N_HEADS * ext, seq), BF16), pltpu.VMEM((N_HEADS * ext, tq), F32)],
        compiler_params=_params(2),
    )(qf, kf, v)


def _mla_bwd_call(qf, kf, v, do, delta, lse, n_seq, seq):
    tq = min(ATT_TILE, seq)
    nq = seq // tq

    nh = 4
    heads = range(nh)
    lanes = [slice(h * HEAD_LANES, (h + 1) * HEAD_LANES) for h in heads]

    def body(q_ref, k_ref, v_ref, do_ref, dl_ref, lse_ref, dq_ref, dk_ref, dv_ref,
             kt_ref, dot_ref, dqt_ref, dvt_ref):
        eye = _eye()
        sub_lo = lax.broadcasted_iota(jnp.int32, (HEAD_LANES, 1), 0) < HALF

        for t in range(nq):
            r = slice(t * tq, (t + 1) * tq)
            kv = k_ref[r, :]
            for h in heads:
                kt_ref[lanes[h], r] = _dot_nt(eye, kv[:, lanes[h]]).astype(BF16)
            for p in range(nh // 2):
                dov = do_ref[r, lanes[p]]
                dt = _dot_nt(eye, dov)
                dot_ref[2 * p, :, r] = jnp.where(sub_lo, dt, 0.0).astype(BF16)
                dot_ref[2 * p + 1, :, r] = jnp.where(sub_lo, 0.0, dt).astype(BF16)
        dqt_ref[...] = jnp.zeros_like(dqt_ref)
        dvt_ref[...] = jnp.zeros_like(dvt_ref)

        def flush_dv(tile, which):
            rows = pl.ds(pl.multiple_of(tile * tq, tq), tq)
            for p in range(nh // 2):
                dv_ref[rows, lanes[p]] = dvt_ref[which, p * HEAD_LANES:(p + 1) * HEAD_LANES, :].T

        def k_step(kt, _):
            slot = kt % 2
            kr = pl.ds(pl.multiple_of(kt * tq, tq), tq)
            k = k_ref[kr, :]
            vv = v_ref[kr, :]
            k_t = kt_ref[:, kr]
            krow = kt * tq + lax.broadcasted_iota(jnp.int32, (tq, 1), 0)

            def make_step(masked, n_tiles):
                def q_step(qt0, carry):
                    tiles = range(n_tiles)
                    qrs = [pl.ds(pl.multiple_of((qt0 + t) * tq, tq), tq) for t in tiles]
                    if masked:
                        flush_dv(jnp.maximum(kt - 1, 0), 1 - slot)
                    qs = [q_ref[qr, :] for qr in qrs]
                    if masked:
                        keep = krow <= (qt0 * tq + lax.broadcasted_iota(jnp.int32, (1, tq), 1))

                    def scores(h):
                        do_ts = [dot_ref[h, :, qr] for qr in qrs]
                        sts = [_dot_nt(k[:, lanes[h]], qs[t][:, lanes[h]]) for t in tiles]
                        dpts = [_dot(vv[:, lanes[h // 2]], do_ts[t]) for t in tiles]
                        return do_ts, sts, dpts

                    def softmax(h, sts, dpts):
                        pts, dsts = [], []
                        for t in tiles:
                            pt = jnp.exp2(sts[t] - lse_ref[0, 0, h:h + 1, qrs[t]])
                            if masked and t == 0:
                                pt = jnp.where(keep, pt, 0.0)
                            dsts.append((pt * (dpts[t] - dl_ref[0, h:h + 1, qrs[t]])).astype(BF16))
                            pts.append(pt.astype(BF16))
                        return pts, dsts

                    def grads(h, do_ts, pts, dsts):
                        half = slice((h % 2) * HALF, (h % 2 + 1) * HALF)
                        dst_all = jnp.concatenate(dsts, axis=1)
                        pt_all = jnp.concatenate(pts, axis=1)
                        do_all = jnp.concatenate([do_ts[t][half, :] for t in tiles], axis=1)
                        q_all = jnp.concatenate([qs[t][:, lanes[h]] for t in tiles], axis=0)
                        dvt_ref[slot, h * HALF:(h + 1) * HALF, :] += _dot_nt(do_all, pt_all)
                        dk_ref[kr, lanes[h]] += _dot(dst_all, q_all)
                        for t in tiles:
                            dqt_ref[lanes[h], qrs[t]] += _dot(k_t[lanes[h], :], dsts[t])

                    first, second = {0: scores(0)}, {}
                    for h in range(nh + 1):
                        if h + 1 < nh:
                            first[h + 1] = scores(h + 1)
                        if h < nh:
                            do_ts, sts, dpts = first.pop(h)
                            second[h] = (do_ts,) + softmax(h, sts, dpts)
                        if h >= 1:
                            grads(h - 1, *second.pop(h - 1))
                    return carry
                return q_step

            dk_ref[kr, :] = jnp.zeros((tq, nh * HEAD_LANES), F32)
            dvt_ref[slot] = jnp.zeros(dvt_ref.shape[1:], F32)
            count = nq - kt
            lax.cond(count >= 2, lambda c: make_step(True, 2)(kt, c), lambda c: make_step(True, 1)(kt, c), 0)
            lax.fori_loop(1, count // 2, lambda j, c: make_step(False, 2)(kt + 2 * j, c), 0)
            lax.cond(jnp.logical_and(count % 2 == 1, count >= 3), lambda c: make_step(False, 1)(nq - 1, c), lambda c: c, 0)
            return 0

        lax.fori_loop(0, nq, k_step, 0)
        flush_dv(nq - 1, (nq - 1) % 2)
        for t in range(nq):
            r = slice(t * tq, (t + 1) * tq)
            for h in heads:
                dq_ref[r, lanes[h]] = dqt_ref[lanes[h], r].T

    n_tok = qf.shape[0]
    groups = N_HEADS // nh
    blk = lambda w: pl.BlockSpec((seq, w), lambda b, g: (b, g))
    return pl.pallas_call(
        body, name="mla_bwd", grid=(n_seq, groups),
        out_shape=[jax.ShapeDtypeStruct((n_tok, 1024), F32), jax.ShapeDtypeStruct((n_tok, 1024), F32),
                   jax.ShapeDtypeStruct((n_tok, 512), F32)],
        in_specs=[blk(512), blk(512), blk(256), blk(256), pl.BlockSpec((1, nh, seq), lambda b, g: (g, 0, b)),
                  pl.BlockSpec((1, 1, nh, seq), lambda b, g: (b, g, 0, 0))],
        out_specs=[blk(512), blk(512), blk(256)],
        scratch_shapes=[pltpu.VMEM((nh * HEAD_LANES, seq), BF16), pltpu.VMEM((nh, HEAD_LANES, seq), BF16),
                        pltpu.VMEM((nh * HEAD_LANES, seq), F32), pltpu.VMEM((2, nh * HALF, tq), F32)],
        compiler_params=_params(2),
    )(qf, kf, v, do, delta, lse)


SWA_BLOCKS = 4


def _swa_block(n, pos_col_ref, posq):
    w = SWA_WINDOW
    start = pl.multiple_of(jnp.maximum(n - 1, 0) * w, w)
    posk = pos_col_ref[pl.ds(start, 2 * w), :]
    rel = (n * w + lax.broadcasted_iota(jnp.int32, (1, w), 1)) - (start + lax.broadcasted_iota(jnp.int32, (2 * w, 1), 0))
    valid = jnp.logical_and(rel >= 0, rel < w)
    return start, jnp.where(valid, posq - posk, 1e30)


def _alibi(h):
    return LOG2E * 2.0 ** -(h + 1)


def _transpose_rows(eye, src_ref, dst_ref, seq, width):
    step = 2 * SWA_WINDOW
    for t in range(seq // step):
        for p in range(width // HEAD_LANES):
            lanes = slice(p * HEAD_LANES, (p + 1) * HEAD_LANES)
            dst_ref[lanes, t * step:(t + 1) * step] = _dot_nt(eye, src_ref[t * step:(t + 1) * step, lanes]).astype(BF16)


def _swa_fwd_call(qs, kd, vd, pos_col, pos_row, sinks, n_seq, seq):
    w = SWA_WINDOW
    qb = SWA_BLOCKS
    steps = seq // (qb * w)
    ext = HALF + 16

    def body(q_ref, k_ref, v_ref, pc_ref, pr_ref, sink_ref, o_ref, lse_ref, vt_ref):
        n = pl.program_id(1)
        lo = _lane_lo()
        hi = jnp.logical_not(lo)
        eye = _eye()

        @pl.when(n == 0)
        def _():
            step = 2 * w
            for kv in range(2):
                vt_ref[kv * ext + HALF:(kv + 1) * ext, :] = jnp.ones((16, seq), BF16)
                for t in range(seq // step):
                    v_t = _dot_nt(eye, v_ref[t * step:(t + 1) * step, kv * HEAD_LANES:(kv + 1) * HEAD_LANES])
                    vt_ref[kv * ext:kv * ext + HALF, t * step:(t + 1) * step] = v_t[:HALF, :].astype(BF16)

        heads = range(N_HEADS)
        blocks = range(qb)
        geo = [_swa_block(n * qb + bi, pc_ref, pr_ref[bi]) for bi in blocks]
        wins = [pl.ds(g[0], 2 * w) for g in geo]
        kwins = [k_ref[win, :] for win in wins]
        vts = [vt_ref[:, win] for win in wins]
        sts = []
        for bi in blocks:
            q = q_ref[bi * w:(bi + 1) * w, :]
            sts.append([])
            for j in range(N_HEADS // 2):
                qp = q[:, j * HEAD_LANES:(j + 1) * HEAD_LANES]
                both = jnp.concatenate([jnp.where(lo, qp, jnp.zeros_like(qp)), jnp.where(hi, qp, jnp.zeros_like(qp))], axis=0)
                st = _dot_nt(kwins[bi][:, (j // 2) * HEAD_LANES:(j // 2 + 1) * HEAD_LANES], both)
                sts[bi] += [st[:, :w], st[:, w:]]
        ps, ms = [], []
        for bi in blocks:
            ps.append([])
            ms.append([])
            for h in heads:
                s = sts[bi][h] - _alibi(h) * geo[bi][1]
                m = jnp.maximum(jnp.max(s, axis=0, keepdims=True), sink_ref[0, h] * LOG2E)
                ps[bi].append(jnp.exp2(s - m).astype(BF16))
                ms[bi].append(m)
        for bi in blocks:
            ots = []
            for h in heads:
                pv = _dot(vts[bi][(h // 4) * ext:(h // 4 + 1) * ext, :], ps[bi][h])
                l = pv[HALF:HALF + 1, :] + jnp.exp2(sink_ref[0, h] * LOG2E - ms[bi][h])
                ots.append(pv[:HALF, :] * (1.0 / l))
                lse_ref[0, h:h + 1, bi * w:(bi + 1) * w] = ms[bi][h] + jnp.log2(l)
            o_ref[bi * w:(bi + 1) * w, :] = jnp.concatenate(ots, axis=0).T

    n_tok = qs.shape[0]
    tok = lambda width: pl.BlockSpec((qb * w, width), lambda b, n: (b * steps + n, 0))
    whole = lambda width: pl.BlockSpec((seq, width), lambda b, n: (b, 0))
    return pl.pallas_call(
        body, name="swa_fwd", grid=(n_seq, steps),
        out_shape=[jax.ShapeDtypeStruct((n_tok, 512), F32), jax.ShapeDtypeStruct((n_seq, N_HEADS, seq), F32)],
        in_specs=[tok(512), whole(256), whole(256), whole(1), pl.BlockSpec((qb, 1, w), lambda b, n: (b * steps + n, 0, 0)),
                  pl.BlockSpec(memory_space=pltpu.SMEM)],
        out_specs=[tok(512), pl.BlockSpec((1, N_HEADS, qb * w), lambda b, n: (b, 0, n))],
        scratch_shapes=[pltpu.VMEM((2 * ext, seq), BF16)],
        compiler_params=_params(2),
    )(qs, kd, vd, pos_col, pos_row, sinks)


def _swa_bwd_call(qs, kd, vd, do, delta, lse, pos_col, pos_row, sinks, g_out, n_seq, seq):
    w = SWA_WINDOW
    qb = SWA_BLOCKS
    steps = seq // (qb * w)
    reduced, reduce_scratch = _reduce_operands(g_out)

    def body(q_ref, k_ref, v_ref, do_ref, dl_ref, lse_ref, pc_ref, pr_ref, sink_ref, g_ref, dq_ref, dk_ref, dv_ref,
             dsink_ref, f_ref, kt_ref, *reduce_refs):
        b, n = pl.program_id(0), pl.program_id(1)
        _grad_reduce(b * steps + n, n_seq * steps, g_ref, f_ref, *reduce_refs)
        lo = _lane_lo()
        hi = jnp.logical_not(lo)
        sub_lo = lax.broadcasted_iota(jnp.int32, (HEAD_LANES, 1), 0) < HALF
        eye = _eye()

        @pl.when(n == 0)
        def _():
            dk_ref[...] = jnp.zeros_like(dk_ref)
            dv_ref[...] = jnp.zeros_like(dv_ref)
            _transpose_rows(eye, k_ref, kt_ref, seq, 2 * HEAD_LANES)

        @pl.when(jnp.logical_and(n == 0, b == 0))
        def _():
            dsink_ref[...] = jnp.zeros_like(dsink_ref)

        heads = range(N_HEADS)
        blocks = range(qb)
        kv_lanes = lambda h: slice((h // 4) * HEAD_LANES, (h // 4 + 1) * HEAD_LANES)
        geo = [_swa_block(n * qb + bi, pc_ref, pr_ref[bi]) for bi in blocks]
        wins = [pl.ds(g[0], 2 * w) for g in geo]
        kwins = [k_ref[win, :] for win in wins]
        vwins = [v_ref[win, :] for win in wins]

        do_ts, deltas, qms, doms = [], [], [], []
        for bi in blocks:
            rows = slice(bi * w, (bi + 1) * w)
            for lst in (do_ts, deltas, qms, doms):
                lst.append([])
            for j in range(N_HEADS // 2):
                pair = slice(j * HEAD_LANES, (j + 1) * HEAD_LANES)
                dop = do_ref[rows, pair]
                qp = q_ref[rows, pair]
                dt = _dot_nt(eye, dop)
                for hh in range(2):
                    half = lo if hh == 0 else hi
                    do_ts[bi].append(jnp.where(sub_lo, dt, 0.0).astype(BF16) if hh == 0
                                     else jnp.where(sub_lo, 0.0, dt).astype(BF16))
                    deltas[bi].append(dl_ref[2 * j + hh:2 * j + hh + 1, rows])
                    qms[bi].append(jnp.where(half, qp, jnp.zeros_like(qp)))
                    doms[bi].append(jnp.where(half, dop, jnp.zeros_like(dop)))
        sts, dpts = [], []
        for bi in blocks:
            sts.append([])
            dpts.append([])
            for j in range(N_HEADS // 2):
                a, b = 2 * j, 2 * j + 1
                st = _dot_nt(kwins[bi][:, kv_lanes(a)], jnp.concatenate([qms[bi][a], qms[bi][b]], axis=0))
                dpt = _dot(vwins[bi][:, kv_lanes(a)], jnp.concatenate([do_ts[bi][a], do_ts[bi][b]], axis=1))
                sts[bi] += [st[:, :w], st[:, w:]]
                dpts[bi] += [dpt[:, :w], dpt[:, w:]]
        pts, dsts = [], []
        for bi in blocks:
            pts.append([])
            dsts.append([])
            for h in heads:
                lse_h = lse_ref[0, h:h + 1, bi * w:(bi + 1) * w]
                pt = jnp.exp2(sts[bi][h] - _alibi(h) * geo[bi][1] - lse_h)
                dsts[bi].append((pt * (dpts[bi][h] - deltas[bi][h])).astype(BF16))
                pts[bi].append(pt.astype(BF16))
                dsink_ref[h:h + 1, :] += -jnp.exp2(sink_ref[0, h] * LOG2E - lse_h) * deltas[bi][h]
        for bi in blocks:
            for kv in range(2):
                group = range(4 * kv, 4 * kv + 4)
                dst_all = jnp.concatenate([dsts[bi][h] for h in group], axis=1)
                pt_all = jnp.concatenate([pts[bi][h] for h in group], axis=1)
                q_all = jnp.concatenate([qms[bi][h] for h in group], axis=0)
                do_all = jnp.concatenate([doms[bi][h] for h in group], axis=0)
                dk_ref[wins[bi], kv_lanes(4 * kv)] += _dot(dst_all, q_all)
                dv_ref[wins[bi], kv_lanes(4 * kv)] += _dot(pt_all, do_all)
        for bi in blocks:
            ktw = kt_ref[:, wins[bi]]
            for j in range(N_HEADS // 2):
                k_t = ktw[kv_lanes(2 * j), :]
                both = _dot(k_t, jnp.concatenate([dsts[bi][2 * j], dsts[bi][2 * j + 1]], axis=1))
                dq_t = jnp.where(sub_lo, both[:, :w], both[:, w:])
                dq_ref[bi * w:(bi + 1) * w, j * HEAD_LANES:(j + 1) * HEAD_LANES] = dq_t.T * SWA_SCALE

    n_tok = qs.shape[0]
    tok = lambda width: pl.BlockSpec((qb * w, width), lambda b, n: (b * steps + n, 0))
    whole = lambda width: pl.BlockSpec((seq, width), lambda b, n: (b, 0))
    return pl.pallas_call(
        body, name="swa_bwd", grid=(n_seq, steps),
        out_shape=[jax.ShapeDtypeStruct((n_tok, 512), F32), jax.ShapeDtypeStruct((n_tok, 256), F32),
                   jax.ShapeDtypeStruct((n_tok, 256), F32), jax.ShapeDtypeStruct((N_HEADS, HEAD_LANES), F32), reduced],
        in_specs=[tok(512), whole(256), whole(256), pl.BlockSpec((qb * w, 512), lambda b, n: (b * steps + n, 1)),
                  pl.BlockSpec((N_HEADS, qb * w), lambda b, n: (0, b * steps + n)),
                  pl.BlockSpec((1, N_HEADS, qb * w), lambda b, n: (b, 0, n)),
                  whole(1), pl.BlockSpec((qb, 1, w), lambda b, n: (b * steps + n, 0, 0)),
                  pl.BlockSpec(memory_space=pltpu.SMEM), ANY_SPEC],
        out_specs=[tok(512), whole(256), whole(256), _full((N_HEADS, HEAD_LANES)), ANY_SPEC],
        scratch_shapes=[pltpu.VMEM((2 * HEAD_LANES, seq), BF16)] + reduce_scratch,
        compiler_params=_params(2),
    )(qs, kd, vd, do, delta, lse, pos_col, pos_row, sinks, g_out)


def _post_call(x, target, o_mla, o_swa, gates, mod, b_ada, fg, w_out, seq):
    n_tok = x.shape[0]
    tm = min(TOKEN_TILE, seq)
    per_seq = seq // tm
    n_seq = n_tok // seq

    def body(x_ref, t_ref, om_ref, os_ref, g_ref, mod_ref, bada_ref, fg_ref, w_ref,
             dx2_ref, do_ref, dg_ref, gw_ref, gfg_ref, dgate_ref, loss_ref, dmla_ref, dswa_ref):
        i = pl.program_id(0)

        @pl.when(i == 0)
        def _():
            gw_ref[...] = jnp.zeros_like(gw_ref)
            gfg_ref[...] = jnp.zeros_like(gfg_ref)
            loss_ref[...] = jnp.zeros_like(loss_ref)

        @pl.when(i % per_seq == 0)
        def _():
            dgate_ref[...] = jnp.zeros_like(dgate_ref)

        gate = mod_ref[0][:, 2 * D_MODEL:] + bada_ref[:, 2 * D_MODEL:]
        fgv = fg_ref[...]
        fgd = fgv * (1.0 / D_MODEL)
        subs = _sub_tiles(tm)
        gs = [g_ref[r, :] for r in subs]
        os_ = [jnp.concatenate([om_ref[r, :], os_ref[r, :]], axis=-1) for r in subs]
        sgs = [_sigmoid(g) for g in gs]
        sils = [g * sg for g, sg in zip(gs, sgs)]
        ypres = [(o * sil).astype(BF16) for o, sil in zip(os_, sils)]
        ys = [_dot(ypre, w_ref[...]) for ypre in ypres]
        dys, loss, gfg, dgate = [], 0.0, 0.0, 0.0
        for r, y in zip(subs, ys):
            x2 = x_ref[r, :] + gate * y
            r2 = lax.rsqrt(jnp.mean(x2 * x2, axis=-1, keepdims=True) + EPS)
            xn2 = x2 * r2
            err = xn2 * fgv - t_ref[r, :]
            loss = loss + jnp.sum(jnp.sum(err * err, axis=-1, keepdims=True), axis=0, keepdims=True)
            gfg = gfg + jnp.sum(err * xn2, axis=0, keepdims=True)
            dxn2 = err * fgd
            dx2 = r2 * (dxn2 - xn2 * jnp.mean(dxn2 * xn2, axis=-1, keepdims=True))
            dx2_ref[r, :] = dx2
            dgate = dgate + jnp.sum(dx2 * y, axis=0, keepdims=True)
            dys.append((dx2 * gate).astype(BF16))
        loss_ref[...] += jnp.broadcast_to(loss * (0.5 / D_MODEL), loss_ref.shape)
        gfg_ref[...] += gfg * (1.0 / D_MODEL)
        dgate_ref[0] += dgate
        gw_ref[...] += _dot_tn(jnp.concatenate(ypres, axis=0), jnp.concatenate(dys, axis=0))
        dypres = [_dot_nt(dy, w_ref[...]) for dy in dys]
        pick = jnp.where(jnp.right_shift(lax.broadcasted_iota(jnp.int32, (2 * N_HEADS, D_MODEL), 1), 6)
                         == lax.broadcasted_iota(jnp.int32, (2 * N_HEADS, D_MODEL), 0), 1.0, 0.0).astype(BF16)
        for r, dypre, o, g, sg, sil in zip(subs, dypres, os_, gs, sgs, sils):
            dov = (dypre * sil).astype(BF16)
            do_ref[r, :] = dov
            delta = _dot_nt(pick, (dov.astype(F32) * o).astype(BF16))
            for grp in range(2):
                dmla_ref[grp, :, r] = delta[4 * grp:4 * grp + 4, :]
            dswa_ref[:, r] = delta[N_HEADS:, :]
            dg_ref[r, :] = (dypre * o * (sg + sil * (1.0 - sg))).astype(BF16)

    tok = lambda w: pl.BlockSpec((tm, w), lambda i: (i, 0))
    per_b = pl.BlockSpec((1, 1, 3 * D_MODEL), lambda i: (i // per_seq, 0, 0))
    return pl.pallas_call(
        body, name="post", grid=(n_tok // tm,),
        out_shape=[jax.ShapeDtypeStruct((n_tok, D_MODEL), F32), jax.ShapeDtypeStruct((n_tok, D_MODEL), BF16),
                   jax.ShapeDtypeStruct((n_tok, D_MODEL), BF16), jax.ShapeDtypeStruct((D_MODEL, D_MODEL), F32),
                   jax.ShapeDtypeStruct((1, D_MODEL), F32), jax.ShapeDtypeStruct((n_seq, 1, D_MODEL), F32),
                   jax.ShapeDtypeStruct((1, HEAD_LANES), F32),
                   jax.ShapeDtypeStruct((2, N_HEADS // 2, n_tok), F32), jax.ShapeDtypeStruct((N_HEADS, n_tok), F32)],
        in_specs=[tok(D_MODEL), tok(D_MODEL), tok(512), tok(512), tok(D_MODEL), per_b, _full(b_ada.shape),
                  _full(fg.shape), _full(w_out.shape)],
        out_specs=[tok(D_MODEL), tok(D_MODEL), tok(D_MODEL), _full((D_MODEL, D_MODEL)), _full((1, D_MODEL)),
                   pl.BlockSpec((1, 1, D_MODEL), lambda i: (i // per_seq, 0, 0)), _full((1, HEAD_LANES)),
                   pl.BlockSpec((2, N_HEADS // 2, tm), lambda i: (0, 0, i)), pl.BlockSpec((N_HEADS, tm), lambda i: (0, i))],
        compiler_params=_params(1),
    )(x, target, o_mla, o_swa, gates, mod, b_ada, fg, w_out)


def _mid_bwd_call(dqf, dkf, dv, zqkv, rope, qg, kvg, wq2, wkv, seq):
    n_tok = dqf.shape[0]
    tm = min(TOKEN_TILE, seq)

    def body(dq_ref, dk_ref, dv_ref, z_ref, rope_ref, qg_ref, kvg_ref, wq_ref, wkv_ref,
             dz_ref, gwq_ref, gwkv_ref, gqg_ref, gkvg_ref):
        i = pl.program_id(0)

        @pl.when(i == 0)
        def _():
            gwq_ref[...] = jnp.zeros_like(gwq_ref)
            gwkv_ref[...] = jnp.zeros_like(gwkv_ref)
            gqg_ref[...] = jnp.zeros_like(gqg_ref)
            gkvg_ref[...] = jnp.zeros_like(gkvg_ref)

        cos, sin = rope_ref[:, :HEAD_LANES], rope_ref[:, HEAD_LANES:]
        cf, sf = jnp.tile(cos, (1, N_HEADS)), jnp.tile(sin, (1, N_HEADS))
        dq = dq_ref[...] * MLA_SCALE
        dqr = jnp.concatenate([dq * cf, dq * sf], axis=-1).astype(BF16)
        zq, zkv = z_ref[:, :Q_LORA], z_ref[:, Q_LORA:]
        qgv, kvgv = qg_ref[...], kvg_ref[...]

        rq = lax.rsqrt(jnp.mean(zq * zq, axis=-1, keepdims=True) + EPS)
        xq = zq * rq
        gwq_ref[...] += _dot_tn((xq * qgv).astype(BF16), dqr)
        dqn = _dot_nt(dqr, wq_ref[...])
        gqg_ref[...] += jnp.sum(dqn * xq, axis=0, keepdims=True)
        dxq = dqn * qgv
        dz_ref[:, :Q_LORA] = (rq * (dxq - xq * jnp.mean(dxq * xq, axis=-1, keepdims=True))).astype(BF16)

        dk = dk_ref[...] * LN2
        dkv = jnp.concatenate([dk, dv_ref[...]], axis=-1).astype(BF16)
        rkv = lax.rsqrt(jnp.mean(zkv * zkv, axis=-1, keepdims=True) + EPS)
        xkv = zkv * rkv
        gwkv_ref[...] += _dot_tn((xkv * kvgv).astype(BF16), dkv)
        dkvn = _dot_nt(dkv, wkv_ref[...])
        gkvg_ref[...] += jnp.sum(dkvn * xkv, axis=0, keepdims=True)
        dxkv = dkvn * kvgv
        dz_ref[:, Q_LORA:A_KR] = (rkv * (dxkv - xkv * jnp.mean(dxkv * xkv, axis=-1, keepdims=True))).astype(BF16)

        dkpe = dk[:, :HEAD_LANES]
        for h in range(1, N_HEADS):
            dkpe = dkpe + dk[:, h * HEAD_LANES:(h + 1) * HEAD_LANES]
        dz_ref[:, A_KR:] = (jnp.where(_lane_lo(), 0.0, dkpe * cos) + pltpu.roll(dkpe * sin, HALF, 1)).astype(BF16)

    tok = lambda w: pl.BlockSpec((tm, w), lambda i: (i, 0))
    return pl.pallas_call(
        body, name="mid_bwd", grid=(n_tok // tm,),
        out_shape=[jax.ShapeDtypeStruct((n_tok, A_GM), BF16),
                   jax.ShapeDtypeStruct(wq2.shape, F32), jax.ShapeDtypeStruct(wkv.shape, F32),
                   jax.ShapeDtypeStruct((1, Q_LORA), F32), jax.ShapeDtypeStruct((1, KV_LORA), F32)],
        in_specs=[tok(1024), tok(1024), tok(512), tok(640), tok(2 * HEAD_LANES), _full(qg.shape), _full(kvg.shape),
                  _full(wq2.shape), _full(wkv.shape)],
        out_specs=[tok(A_GM), _full(wq2.shape), _full(wkv.shape), _full((1, Q_LORA)), _full((1, KV_LORA))],
        compiler_params=_params(1),
    )(dqf, dkf, dv, zqkv, rope, qg, kvg, wq2, wkv)


def _in_bwd_call(x, dx2, dz, dg, dqs, dkd, dvd, mod, b_ada, ng, wa, seq):
    n_tok = x.shape[0]
    tm = min(TOKEN_TILE, seq)
    per_seq = seq // tm
    n_seq = n_tok // seq

    def body(x_ref, dx2_ref, dz_ref, dg_ref, dqs_ref, dkd_ref, dvd_ref, mod_ref, bada_ref, ng_ref,
             wa_ref, gx_ref, gwa_ref, gng_ref, dshift_ref, dscale_ref):
        i = pl.program_id(0)

        @pl.when(i == 0)
        def _():
            gwa_ref[...] = jnp.zeros_like(gwa_ref)
            gng_ref[...] = jnp.zeros_like(gng_ref)

        @pl.when(i % per_seq == 0)
        def _():
            dshift_ref[...] = jnp.zeros_like(dshift_ref)
            dscale_ref[...] = jnp.zeros_like(dscale_ref)

        xv = x_ref[...]
        modv = mod_ref[0] + bada_ref[...]
        shift, scale = modv[:, :D_MODEL], modv[:, D_MODEL:2 * D_MODEL]
        ngv = ng_ref[...]
        r1 = lax.rsqrt(jnp.mean(xv * xv, axis=-1, keepdims=True) + EPS)
        xn = xv * r1
        hb = ((xn * ngv) * (1.0 + scale) + shift).astype(BF16)

        dgv = dg_ref[...]
        pieces = [(A_ZQ, dz_ref[...]), (A_GM, dgv[:, :512]), (A_QS, dqs_ref[...].astype(BF16)),
                  (A_KS, jnp.concatenate([_once(dkd_ref[...]) * LN2, _once(dvd_ref[...])], axis=1).astype(BF16)),
                  (A_GS, dgv[:, 512:])]
        dh = None
        for off, piece in pieces:
            wd = piece.shape[1]
            gwa_ref[:, off:off + wd] += _dot_tn(hb, piece)
            term = _dot_nt(piece, wa_ref[:, off:off + wd])
            dh = term if dh is None else dh + term

        dshift_ref[0] += jnp.sum(dh, axis=0, keepdims=True)
        dscale_ref[0] += jnp.sum(dh * (xn * ngv), axis=0, keepdims=True)
        gng_ref[...] += jnp.sum(dh * xn * (1.0 + scale), axis=0, keepdims=True)
        dxn = dh * ngv * (1.0 + scale)
        gx_ref[...] = dx2_ref[...] + r1 * (dxn - xn * jnp.mean(dxn * xn, axis=-1, keepdims=True))

    tok = lambda w: pl.BlockSpec((tm, w), lambda i: (i, 0))
    per_b = lambda w: pl.BlockSpec((1, 1, w), lambda i: (i // per_seq, 0, 0))
    return pl.pallas_call(
        body, name="in_bwd", grid=(n_tok // tm,),
        out_shape=[jax.ShapeDtypeStruct((n_tok, D_MODEL), F32), jax.ShapeDtypeStruct((D_MODEL, A_END), F32),
                   jax.ShapeDtypeStruct((1, D_MODEL), F32),
                   jax.ShapeDtypeStruct((n_seq, 1, D_MODEL), F32), jax.ShapeDtypeStruct((n_seq, 1, D_MODEL), F32)],
        in_specs=[tok(D_MODEL), tok(D_MODEL), tok(A_GM), tok(D_MODEL), tok(512), tok(256), tok(256),
                  per_b(3 * D_MODEL), _full(b_ada.shape), _full(ng.shape), _full(wa.shape)],
        out_specs=[tok(D_MODEL), _full((D_MODEL, A_END)), _full((1, D_MODEL)), per_b(D_MODEL), per_b(D_MODEL)],
        compiler_params=_params(1),
    )(x, dx2, dz, dg, dqs, dkd, dvd, mod, b_ada, ng, wa)


def _adam_math(w, g, m, v):
    m_new = ADAM_B1 * m + (1.0 - ADAM_B1) * g
    v_new = ADAM_B2 * v + (1.0 - ADAM_B2) * (g * g)
    m_hat = m_new / (1.0 - ADAM_B1 ** ADAM_STEP)
    v_hat = v_new / (1.0 - ADAM_B2 ** ADAM_STEP)
    delta = -ADAM_LR * (m_hat / (jnp.sqrt(v_hat) + ADAM_EPS) + ADAM_WD * w)
    return delta, m_new, v_new


def _adam_call(name, w, g, m, v):
    rows, cols = w.shape
    tr = next((t for t in (256, 128) if rows % t == 0), rows)

    def body(w_ref, g_ref, m_ref, v_ref, d_ref, mo_ref, vo_ref):
        d, mn, vn = _adam_math(w_ref[...], g_ref[...], m_ref[...], v_ref[...])
        d_ref[...] = d
        mo_ref[...] = mn
        vo_ref[...] = vn

    spec = pl.BlockSpec((tr, cols), lambda i: (i, 0))
    return pl.pallas_call(
        body, name=name, grid=(rows // tr,),
        out_shape=[jax.ShapeDtypeStruct(w.shape, F32)] * 3,
        in_specs=[spec] * 4, out_specs=[spec] * 3,
        compiler_params=_params(1),
    )(w, g, m, v)


def _ada_bwd_call(act_all, dmod_cols, w, m, v):
    rows, cols = w.shape
    tr = 512

    def body(a_ref, dm_ref, w_ref, m_ref, v_ref, g_ref, d_ref, mo_ref, vo_ref):
        g = _dot_tn(a_ref[...].astype(BF16), dm_ref[...].astype(BF16))
        d, mn, vn = _adam_math(w_ref[...], g, m_ref[...], v_ref[...])
        g_ref[...] = g
        d_ref[...] = d
        mo_ref[...] = mn
        vo_ref[...] = vn

    spec = pl.BlockSpec((tr, cols), lambda i: (i, 0))
    nb = act_all.shape[0]
    return pl.pallas_call(
        body, name="ada_bwd", grid=(rows // tr,),
        out_shape=[jax.ShapeDtypeStruct(w.shape, F32)] * 4,
        in_specs=[pl.BlockSpec((nb, tr), lambda i: (0, i)), _full(dmod_cols.shape), spec, spec, spec],
        out_specs=[spec] * 4,
        compiler_params=_params(1),
    )(act_all, dmod_cols, w, m, v)


SMALL_ROW = {"norm_gain": (0, 1024), "final_gain": (1024, 2048), "q_norm_gain": (2048, 2432),
             "kv_norm_gain": (2432, 2688), "swa_sinks": (2688, 2696), "loss": (2816, 2944)}
SMALL_ORDER = ("b_ada", "norm_gain", "q_norm_gain", "kv_norm_gain", "swa_sinks", "final_gain")


def _small_call(parts_all, n_seq, params):
    k = len(params)

    def body(p_ref, *refs):
        ins, outs, loss_ref = refs[:3 * k], refs[3 * k:7 * k], refs[7 * k]
        row = p_ref[n_seq:n_seq + 1, :]
        for dv in range(1, 8):
            r0 = dv * ROWS_PER_DEVICE + n_seq
            row = row + p_ref[r0:r0 + 1, :]
        gb = None
        for dv in range(8):
            for r in range(n_seq):
                r0 = dv * ROWS_PER_DEVICE + r
                gb = p_ref[r0:r0 + 1, :] if gb is None else gb + p_ref[r0:r0 + 1, :]
        for j, name in enumerate(SMALL_ORDER):
            g = gb if name == "b_ada" else row[:, SMALL_ROW[name][0]:SMALL_ROW[name][1]]
            d, mn, vn = _adam_math(ins[3 * j][...], g, ins[3 * j + 1][...], ins[3 * j + 2][...])
            outs[4 * j][...] = g
            outs[4 * j + 1][...] = d
            outs[4 * j + 2][...] = mn
            outs[4 * j + 3][...] = vn
        loss_ref[...] = row[:, SMALL_ROW["loss"][0]:SMALL_ROW["loss"][1]]

    flat = [t for p in params for t in p]
    res = pl.pallas_call(
        body, name="small_update", grid=(1,),
        out_shape=[jax.ShapeDtypeStruct(p[0].shape, F32) for p in params for _ in range(4)]
        + [jax.ShapeDtypeStruct((1, HEAD_LANES), F32)],
        in_specs=[_full(parts_all.shape)] + [_full(t.shape) for t in flat],
        out_specs=[_full(p[0].shape) for p in params for _ in range(4)] + [_full((1, HEAD_LANES))],
        compiler_params=_params(1),
    )(parts_all, *flat)
    return [res[4 * j:4 * j + 4] for j in range(k)], res[4 * k]


def _rot(t):
    half = t.shape[-1] // 2
    return jnp.concatenate([-t[..., half:], t[..., :half]], axis=-1)


def _rot_t(g):
    half = g.shape[-1] // 2
    return jnp.concatenate([g[..., half:], -g[..., :half]], axis=-1)


def _columns(segments, lo, hi):
    out, at = [], 0
    for seg in segments:
        n = seg.shape[1]
        a, b = max(lo, at), min(hi, at + n)
        if a < b:
            out.append(seg[:, a - at:b - at])
        at += n
    return out


def _prepare_in(w_in_blocks):
    o = [0]
    for s in IN_SPLITS:
        o.append(o[-1] + s)
    part = lambda a, b: _columns(w_in_blocks, a, b)
    kr = jnp.concatenate(part(o[2], o[3]), axis=1)
    zero = jnp.zeros((kr.shape[0], 32), kr.dtype)
    return jnp.concatenate(part(0, o[2]) + [_rot(kr), zero, kr, zero] + part(o[3], o[8]), axis=1)


def _prepare_up(w_uq, w_ukv):
    uq = w_uq.reshape(Q_LORA, N_HEADS, MLA_NOPE + MLA_ROPE)
    zq = jnp.zeros((Q_LORA, N_HEADS, 32), w_uq.dtype)
    uq_full = jnp.concatenate([uq, zq], axis=-1).reshape(Q_LORA, 1024)
    uq_rot = jnp.concatenate([jnp.zeros((Q_LORA, N_HEADS, 64), w_uq.dtype), _rot(uq[..., MLA_NOPE:]), zq],
                             axis=-1).reshape(Q_LORA, 1024)
    wq2 = jnp.concatenate([uq_full, uq_rot], axis=1)
    ukv = w_ukv.reshape(KV_LORA, N_HEADS, 128)
    k_full = jnp.concatenate([ukv[..., :64], jnp.zeros((KV_LORA, N_HEADS, 64), w_ukv.dtype)], axis=-1).reshape(KV_LORA, 1024)
    wkv = jnp.concatenate([k_full, ukv[..., 64:].reshape(KV_LORA, 512)], axis=1)
    return wq2, wkv


def _restore_in(gwa):
    gkr = gwa[:, A_KR + 64:A_KR + 96] + _rot_t(gwa[:, A_KR:A_KR + 32])
    in_order = [gwa[:, :A_KR], gkr, gwa[:, A_GM:]]
    n = D_IN // 4
    return [jnp.concatenate(_columns(in_order, k * n, (k + 1) * n), axis=1) for k in range(4)]


def _restore_up(gwq2, gwkv):
    gf = gwq2[:, :1024].reshape(Q_LORA, N_HEADS, 128)
    gr = gwq2[:, 1024:].reshape(Q_LORA, N_HEADS, 128)
    g_uq = jnp.concatenate([gf[..., :64], gf[..., 64:96] + _rot_t(gr[..., 64:96])], axis=-1).reshape(Q_LORA, 768)
    gk = gwkv[:, :1024].reshape(KV_LORA, N_HEADS, 128)[..., :64]
    gv = gwkv[:, 1024:].reshape(KV_LORA, N_HEADS, 64)
    g_ukv = jnp.concatenate([gk, gv], axis=-1).reshape(KV_LORA, 1024)
    return g_uq, g_ukv


def _local_step(x, positions, target, mod_rows, b_ada, ng, qg, kvg, sinks, fg, w_in_b, later_shards):
    n_seq, seq, _ = x.shape
    n_tok = n_seq * seq
    x2d = x.reshape(n_tok, D_MODEL)
    t2d = target.reshape(n_tok, D_MODEL)
    pos_f = positions.astype(F32)
    pos_col = pos_f.reshape(n_tok, 1)
    pos_row = pos_f.reshape(n_tok // SWA_WINDOW, 1, SWA_WINDOW)
    mod3 = mod_rows.reshape(n_seq, 1, 3 * D_MODEL)
    inv = ROPE_THETA ** (-jnp.arange(0, MLA_ROPE, 2, dtype=F32) / MLA_ROPE)
    inv128 = jnp.tile(jnp.concatenate([inv, inv]), 4).reshape(1, HEAD_LANES)
    fg2 = fg.reshape(1, D_MODEL)

    wa = _prepare_in(w_in_b)
    zqkv, zkr, gates, qs, kd, vd, rope, f_uq, f_ukv, f_out = _pre_call(x2d, pos_col, mod3, b_ada, ng, inv128, wa,
                                                                       later_shards, seq)
    cols = lambda t, r: jnp.transpose(t.reshape(4, r, -1), (1, 0, 2)).reshape(r, -1)
    wq2, wkv = _prepare_up(cols(f_uq, Q_LORA), cols(f_ukv, KV_LORA))
    w_out_b = f_out.reshape(D_MODEL, D_MODEL)
    qf, kf, v = _up_call(zqkv, zkr, rope, qg, kvg, wq2, wkv, seq)
    o_mla, lse_mla = _mla_fwd_call(qf, kf, v, n_seq, seq)
    o_swa, lse_swa = _swa_fwd_call(qs, kd, vd, pos_col, pos_row, sinks, n_seq, seq)
    dx2, do, dg, g_out, g_fg, dgate, loss, delta_mla, delta_swa = _post_call(x2d, t2d, o_mla, o_swa, gates, mod3, b_ada, fg2, w_out_b, seq)
    dqf, dkf, dv = _mla_bwd_call(qf, kf, v, do, delta_mla, lse_mla, n_seq, seq)
    dqs, dkd, dvd, dsink, r_out = _swa_bwd_call(qs, kd, vd, do, delta_swa, lse_swa, pos_col, pos_row, sinks,
                                                g_out.reshape(4, 2, D_MODEL // 8, D_MODEL), n_seq, seq)
    dz, g_wq2, g_wkv, g_qg, g_kvg = _mid_bwd_call(dqf, dkf, dv, zqkv, rope, qg, kvg, wq2, wkv, seq)
    gx, g_wa, g_ng, dshift, dscale = _in_bwd_call(x2d, dx2, dz, dg, dqs, dkd, dvd, mod3, b_ada, ng, wa, seq)
    g_in = _restore_in(g_wa)
    g_uq, g_ukv = _restore_up(g_wq2, g_wkv)
    dmod = jnp.concatenate([dshift, dscale, dgate], axis=-1).reshape(n_seq, 3 * D_MODEL)
    small_row = jnp.concatenate([g_ng, g_fg, g_qg, g_kvg, jnp.pad(jnp.sum(dsink, axis=1).reshape(1, N_HEADS), ((0, 0), (0, 120))),
                                 loss, jnp.zeros((1, 128), F32)], axis=1)
    return gx.reshape(x.shape), (g_in, g_uq, g_ukv), r_out, small_row, dmod


def kernel(x, c, positions, w_ada, b_ada, norm_gain, w_in, q_norm_gain, kv_norm_gain, w_uq, w_ukv, swa_sinks, w_out, final_gain, loss_target, m_w_ada, m_b_ada, m_norm_gain, m_w_in, m_q_norm_gain, m_kv_norm_gain, m_w_uq, m_w_ukv, m_swa_sinks, m_w_out, m_final_gain, v_w_ada, v_b_ada, v_norm_gain, v_w_in, v_q_norm_gain, v_kv_norm_gain, v_w_uq, v_w_ukv, v_swa_sinks, v_w_out, v_final_gain):
    n_seq = x.shape[0]
    xi, yi, ci = lax.axis_index("x"), lax.axis_index("y"), lax.axis_index("c")
    dev = 4 * xi + 2 * yi + ci
    chip = 2 * xi + yi

    halves = lambda w: w.astype(BF16).reshape(2, w.shape[0] // 2, w.shape[1])
    c_blk = jnp.pad(c, ((0, ROWS_PER_DEVICE - n_seq), (0, 0)))
    act_all, pieces, f_in = _comm_fwd_call(c_blk, w_ada[0], [halves(w_in[0])])
    mine = lax.dynamic_slice_in_dim(pieces, dev * ROWS_PER_DEVICE, n_seq, axis=1)
    mod_rows = jnp.transpose(mine, (1, 0, 2)).reshape(n_seq, 3 * D_MODEL)
    w_in_blocks = [f_in[k].reshape(D_MODEL, -1) for k in range(4)]

    gx, (g_in_blocks, g_uq, g_ukv), r_out, small_row, dmod = _local_step(
        x, positions, loss_target, mod_rows, b_ada, norm_gain, q_norm_gain, kv_norm_gain, swa_sinks, final_gain,
        w_in_blocks, [halves(w_uq[0]), halves(w_ukv[0]), halves(w_out[0])])

    grads = [jnp.stack(g_in_blocks).reshape(4, 2, D_MODEL // 2, -1), _by_owner(g_uq, g_uq.shape[1] // 4),
             _by_owner(g_ukv, g_ukv.shape[1] // 4)]
    part = jnp.concatenate([dmod, small_row, jnp.zeros((ROWS_PER_DEVICE - n_seq - 1, 3 * D_MODEL), F32)], axis=0)
    r_in, r_uq, r_ukv, parts_all = _comm_bwd_call(grads, part)
    g_in_s, g_uq_s = r_in.reshape(w_in.shape[1:]), r_uq.reshape(w_uq.shape[1:])
    g_ukv_s, g_out_s = r_ukv.reshape(w_ukv.shape[1:]), r_out.reshape(w_out.shape[1:])

    tr = lambda a: jnp.swapaxes(a[0], 0, 1)
    back = lambda ts: [jnp.swapaxes(t, 0, 1) for t in ts]
    d_in, nm_in, nv_in = back(_adam_call("adam_w_in", tr(w_in), g_in_s.T, tr(m_w_in), tr(v_w_in)))
    d_uq, nm_uq, nv_uq = back(_adam_call("adam_w_uq", tr(w_uq), g_uq_s.T, tr(m_w_uq), tr(v_w_uq)))
    d_ukv, nm_ukv, nv_ukv = _adam_call("adam_w_ukv", w_ukv[0], g_ukv_s, m_w_ukv[0], v_w_ukv[0])
    d_out, nm_out, nv_out = _adam_call("adam_w_out", w_out[0], g_out_s, m_w_out[0], v_w_out[0])
    dmod_cols = lax.dynamic_slice_in_dim(parts_all, chip * 768, 768, axis=1)
    g_ada, d_ada, nm_ada, nv_ada = _ada_bwd_call(act_all, dmod_cols, w_ada[0], m_w_ada[0], v_w_ada[0])

    row = lambda t: t.reshape(1, -1)
    small = {"b_ada": (b_ada, m_b_ada, v_b_ada), "norm_gain": (norm_gain, m_norm_gain, v_norm_gain),
             "q_norm_gain": (q_norm_gain, m_q_norm_gain, v_q_norm_gain),
             "kv_norm_gain": (kv_norm_gain, m_kv_norm_gain, v_kv_norm_gain),
             "swa_sinks": (swa_sinks, m_swa_sinks, v_swa_sinks),
             "final_gain": (row(final_gain), row(m_final_gain), row(v_final_gain))}
    res, loss_row = _small_call(parts_all, n_seq, [small[name] for name in SMALL_ORDER])
    res = dict(zip(SMALL_ORDER, res))
    res["final_gain"] = [t.reshape(-1) for t in res["final_gain"]]
    e = lambda t: t[None]
    big = {"w_ada": (e(g_ada), e(d_ada), e(nm_ada), e(nv_ada)), "w_in": (e(g_in_s), e(d_in), e(nm_in), e(nv_in)),
           "w_uq": (e(g_uq_s), e(d_uq), e(nm_uq), e(nv_uq)), "w_ukv": (e(g_ukv_s), e(d_ukv), e(nm_ukv), e(nv_ukv)),
           "w_out": (e(g_out_s), e(d_out), e(nm_out), e(nv_out))}
    order = ("w_ada", "b_ada", "norm_gain", "w_in", "q_norm_gain", "kv_norm_gain", "w_uq", "w_ukv", "swa_sinks", "w_out",
             "final_gain")
    pick = lambda kind: [(big[n] if n in big else res[n])[kind] for n in order]
    return (loss_row[0, 0], gx, *pick(0), *pick(1), *pick(2), *pick(3))
```

```python
import jax
import jax.numpy as jnp
from jax import lax
from jax.experimental import pallas as pl
from jax.experimental.pallas import tpu as pltpu

F32 = jnp.float32
BF16 = jnp.bfloat16

D_MODEL = 1024
Q_LORA = 384
KV_LORA = 256
N_HEADS = 8
MLA_NOPE = 64
MLA_ROPE = 32
HEAD_LANES = 128
HALF = 64
SWA_WINDOW = 128
EPS = 1e-6
ROPE_THETA = 10000.0
MLA_SCALE = (MLA_NOPE + MLA_ROPE) ** -0.5
LOG2E = 1.4426950408889634
LN2 = 0.6931471805599453
SWA_SCALE = 64 ** -0.5
NEG = -1e30

ADAM_LR = 0.001
ADAM_B1 = 0.9
ADAM_B2 = 0.999
ADAM_EPS = 1e-08
ADAM_WD = 0.01
ADAM_STEP = 10

A_ZQ, A_ZKV, A_KR, A_GM, A_QS, A_KS, A_VS, A_GS, A_END = 0, 384, 640, 768, 1280, 1792, 1920, 2048, 2560
IN_SPLITS = (384, 256, 32, 512, 512, 128, 128, 512)
D_IN = sum(IN_SPLITS)

TOKEN_TILE = 512
ATT_TILE = 256
VMEM_LIMIT = 56 * 1024 * 1024


def _dot(a, b):
    return jnp.dot(a, b, preferred_element_type=F32)


def _dot_nt(a, b):
    return lax.dot_general(a, b, (((1,), (1,)), ((), ())), preferred_element_type=F32)


def _dot_tn(a, b):
    return lax.dot_general(a, b, (((0,), (0,)), ((), ())), preferred_element_type=F32)


def _params(n_grid):
    return pltpu.CompilerParams(dimension_semantics=("arbitrary",) * n_grid, vmem_limit_bytes=VMEM_LIMIT)


def _full(shape):
    nd = len(shape)
    return pl.BlockSpec(shape, lambda *_: (0,) * nd, pipeline_mode=pl.Buffered(1))


def _sigmoid(g):
    return 1.0 / (1.0 + jnp.exp(-g))


SUB_TILE = 256


def _sub_tiles(tm):
    sub = min(SUB_TILE, tm)
    return [slice(s * sub, (s + 1) * sub) for s in range(tm // sub)]


MESH = pl.DeviceIdType.MESH
ROWS_PER_DEVICE = 8
VMEM_SPEC = pl.BlockSpec(memory_space=pltpu.VMEM)
ANY_SPEC = pl.BlockSpec(memory_space=pl.ANY)


def _position():
    x, y, c = lax.axis_index("x"), lax.axis_index("y"), lax.axis_index("c")
    sibling = (x, y, 1 - c)
    others = [(1 - x, y, c), (x, 1 - y, c), (1 - x, 1 - y, c)]
    return (x, y, c), 4 * x + 2 * y + c, 2 * x + y, sibling, others


def _rows_of(dev):
    return pl.ds(pl.multiple_of(dev * ROWS_PER_DEVICE, ROWS_PER_DEVICE), ROWS_PER_DEVICE)


def _all_to_all_rows(block_ref, table_ref, dev, me, send_sems, recv_sems):
    x, y, c = me
    waits = []
    for k in range(1, 8):
        peer = (1 - x if k & 4 else x, 1 - y if k & 2 else y, 1 - c if k & 1 else c)
        pltpu.make_async_remote_copy(src_ref=block_ref, dst_ref=table_ref.at[_rows_of(dev)], send_sem=send_sems.at[k - 1],
                                     recv_sem=recv_sems.at[k - 1], device_id=peer, device_id_type=MESH).start()
        waits.append(pltpu.make_async_remote_copy(
            src_ref=block_ref, dst_ref=table_ref.at[_rows_of(jnp.bitwise_xor(dev, k))], send_sem=send_sems.at[k - 1],
            recv_sem=recv_sems.at[k - 1], device_id=peer, device_id_type=MESH))
    return waits


def _comm_fwd_call(c_blk, w_ada, shards):
    n = len(shards)

    def body(c_ref, wada_ref, *refs):
        w_refs, act_ref, pieces_ref, full_refs = refs[:n], refs[n], refs[n + 1], refs[n + 2:2 * n + 2]
        c_all_ref = refs[2 * n + 2]
        c_send, c_recv, p_send, p_recv, w_send, w_recv, f_send, f_recv, loc_sem = refs[2 * n + 3:]
        me, dev, chip, sibling, others = _position()
        core = me[2]
        chip_of = [2 * p[0] + p[1] for p in others]

        local = [pltpu.make_async_copy(w_refs[i], full_refs[i].at[chip], loc_sem.at[i]) for i in range(n)]
        for cp in local:
            cp.start()

        def over_ici(i, j, src_chip):
            return pltpu.make_async_remote_copy(
                src_ref=w_refs[i].at[core], dst_ref=full_refs[i].at[src_chip, core], send_sem=w_send.at[3 * i + j],
                recv_sem=w_recv.at[3 * i + j], device_id=others[j], device_id_type=MESH)

        def to_sibling(i, j, half):
            return pltpu.make_async_remote_copy(
                src_ref=full_refs[i].at[chip_of[j], half], dst_ref=full_refs[i].at[chip_of[j], half],
                send_sem=f_send.at[3 * i + j], recv_sem=f_recv.at[3 * i + j], device_id=sibling, device_id_type=MESH)

        c_all_ref[_rows_of(dev), :] = c_ref[...]
        c_waits = _all_to_all_rows(c_ref, c_all_ref, dev, me, c_send, c_recv)
        sent = [over_ici(i, j, chip) for i in range(n) for j in range(3)]
        for cp in sent:
            cp.start()

        for cp in c_waits:
            cp.wait()
        cv = c_all_ref[...]
        act = cv * _sigmoid(cv)
        act_ref[...] = act
        pieces_ref[chip] = _dot(act.astype(BF16), wada_ref[...].astype(BF16))
        piece = lambda j, src_chip: pltpu.make_async_remote_copy(
            src_ref=pieces_ref.at[chip], dst_ref=pieces_ref.at[src_chip], send_sem=p_send.at[j], recv_sem=p_recv.at[j],
            device_id=others[j], device_id_type=MESH)
        for j in range(3):
            piece(j, chip).start()

        for i in range(n):
            for j in range(3):
                over_ici(i, j, chip_of[j]).wait_recv()
                to_sibling(i, j, core).start()
        for j in range(3):
            piece(j, chip).wait_send()
            piece(j, chip_of[j]).wait_recv()
        for i in range(n):
            for j in range(3):
                to_sibling(i, j, 1 - core).wait_recv()
                to_sibling(i, j, core).wait_send()
        for cp in sent:
            cp.wait_send()
        for cp in local:
            cp.wait()

    rows = 8 * ROWS_PER_DEVICE
    dma = pltpu.SemaphoreType.DMA
    return pl.pallas_call(
        body, name="comm_fwd",
        out_shape=[jax.ShapeDtypeStruct((rows, D_MODEL), F32), jax.ShapeDtypeStruct((4, rows, w_ada.shape[1]), F32)]
        + [jax.ShapeDtypeStruct((4,) + s.shape, s.dtype) for s in shards],
        in_specs=[VMEM_SPEC, VMEM_SPEC] + [ANY_SPEC] * n,
        out_specs=[VMEM_SPEC, VMEM_SPEC] + [ANY_SPEC] * n,
        scratch_shapes=[pltpu.VMEM((rows, D_MODEL), F32), dma((7,)), dma((7,)), dma((3,)), dma((3,)),
                        dma((3 * n,)), dma((3 * n,)), dma((3 * n,)), dma((3 * n,)), dma((n,))],
        compiler_params=pltpu.CompilerParams(vmem_limit_bytes=VMEM_LIMIT),
    )(c_blk, w_ada, *shards)


def _comm_bwd_call(grads, part):
    n = len(grads)

    def body(part_ref, *refs):
        g_refs, f_refs, parts_ref = refs[:n], refs[n:2 * n], refs[2 * n]
        scratch = refs[2 * n + 1:]
        a_refs, b_refs, p_refs, r_refs = (scratch[k * n:(k + 1) * n] for k in range(4))
        s_send, s_recv, d_send, d_recv, e_send, e_recv, h_send, h_recv, loc_sem = scratch[4 * n:]
        me, dev, chip, sibling, others = _position()
        core = me[2]
        chip_of = [2 * p[0] + p[1] for p in others]

        parts_ref[_rows_of(dev), :] = part_ref[...]
        s_waits = _all_to_all_rows(part_ref, parts_ref, dev, me, s_send, s_recv)

        mine = [pltpu.make_async_copy(g_refs[i].at[:, core], a_refs[i], loc_sem.at[i]) for i in range(n)]
        swap = [pltpu.make_async_remote_copy(src_ref=g_refs[i].at[:, 1 - core], dst_ref=b_refs[i], send_sem=d_send.at[i],
                                             recv_sem=d_recv.at[i], device_id=sibling, device_id_type=MESH) for i in range(n)]
        order = sorted(range(n), key=lambda i: g_refs[i].shape[2] * g_refs[i].shape[3])
        for i in order:
            mine[i].start()
            swap[i].start()
        cross = [pltpu.make_async_remote_copy(src_ref=p_refs[i].at[chip_of[j]], dst_ref=r_refs[i].at[j],
                                              send_sem=e_send.at[3 * i + j], recv_sem=e_recv.at[3 * i + j],
                                              device_id=others[j], device_id_type=MESH) for i in range(n) for j in range(3)]
        for i in order:
            mine[i].wait()
            swap[i].wait()
            for k in range(4):
                s = a_refs[i][k] + b_refs[i][k]
                a_refs[i][k] = s
                p_refs[i][k] = s.astype(BF16)
            for j in range(3):
                cross[3 * i + j].start()
        share = {}
        for i in order:
            for j in range(3):
                cross[3 * i + j].wait()
            f_refs[i][core] = (a_refs[i][chip] + r_refs[i][0].astype(F32) + r_refs[i][1].astype(F32)
                               + r_refs[i][2].astype(F32))
            share[i] = pltpu.make_async_remote_copy(src_ref=f_refs[i].at[core], dst_ref=f_refs[i].at[core],
                                                    send_sem=h_send.at[i], recv_sem=h_recv.at[i], device_id=sibling,
                                                    device_id_type=MESH)
            share[i].start()
        for i in range(n):
            share[i].wait_send()
            pltpu.make_async_remote_copy(src_ref=f_refs[i].at[core], dst_ref=f_refs[i].at[1 - core], send_sem=h_send.at[i],
                                         recv_sem=h_recv.at[i], device_id=sibling, device_id_type=MESH).wait_recv()
        for cp in s_waits:
            cp.wait()

    rows = 8 * ROWS_PER_DEVICE
    dma = pltpu.SemaphoreType.DMA
    quarter = [(4,) + g.shape[2:] for g in grads]
    return pl.pallas_call(
        body, name="comm_bwd",
        out_shape=[jax.ShapeDtypeStruct((2,) + g.shape[2:], F32) for g in grads]
        + [jax.ShapeDtypeStruct((rows, part.shape[1]), F32)],
        in_specs=[VMEM_SPEC] + [ANY_SPEC] * n,
        out_specs=[VMEM_SPEC] * (n + 1),
        scratch_shapes=[pltpu.VMEM(q, F32) for q in quarter] + [pltpu.VMEM(q, F32) for q in quarter]
        + [pltpu.VMEM(q, BF16) for q in quarter] + [pltpu.VMEM((3,) + q[1:], BF16) for q in quarter]
        + [dma((7,)), dma((7,)), dma((n,)), dma((n,)), dma((3 * n,)), dma((3 * n,)), dma((n,)), dma((n,)), dma((n,))],
        compiler_params=pltpu.CompilerParams(vmem_limit_bytes=VMEM_LIMIT),
    )(part, *grads)


def _by_owner(g, n):
    return jnp.transpose(g.reshape(g.shape[0], 4, n), (1, 0, 2)).reshape(4, 2, g.shape[0] // 2, n)


def _reduce_operands(g):
    quarter = (4,) + g.shape[2:]
    dma = pltpu.SemaphoreType.DMA
    scratch = [pltpu.VMEM(quarter, F32), pltpu.VMEM(quarter, F32), pltpu.VMEM(quarter, BF16),
               pltpu.VMEM((3,) + quarter[1:], BF16), dma((5,)), dma((5,)), dma((2,))]
    return jax.ShapeDtypeStruct((2,) + g.shape[2:], F32), scratch


def _grad_reduce(step, n_steps, g_ref, f_ref, a_ref, b_ref, p_ref, r_ref, send, recv, loc_sem):
    me, _, chip, sibling, others = _position()
    core = me[2]
    chip_of = [2 * p[0] + p[1] for p in others]
    remote = lambda src, dst, k, to: pltpu.make_async_remote_copy(
        src_ref=src, dst_ref=dst, send_sem=send.at[k], recv_sem=recv.at[k], device_id=to, device_id_type=MESH)
    mine = pltpu.make_async_copy(g_ref.at[:, core], a_ref, loc_sem.at[0])
    swap = remote(g_ref.at[:, 1 - core], b_ref, 0, sibling)
    cross = [remote(p_ref.at[chip_of[j]], r_ref.at[j], 1 + j, others[j]) for j in range(3)]
    total_ref = b_ref.at[0]
    keep = pltpu.make_async_copy(total_ref, f_ref.at[core], loc_sem.at[1])
    share = lambda half: remote(total_ref, f_ref.at[half], 4, sibling)
    last = n_steps - 1
    at = [0, last // 5, 3 * last // 4, last]

    @pl.when(step == at[0])
    def _():
        mine.start()
        swap.start()

    @pl.when(step == at[1])
    def _():
        mine.wait()
        swap.wait()
        for k in range(4):
            s = a_ref[k] + b_ref[k]
            a_ref[k] = s
            p_ref[k] = s.astype(BF16)
        for cp in cross:
            cp.start()

    @pl.when(step == at[2])
    def _():
        for cp in cross:
            cp.wait()
        total_ref[...] = a_ref[chip] + r_ref[0].astype(F32) + r_ref[1].astype(F32) + r_ref[2].astype(F32)
        keep.start()
        share(core).start()

    @pl.when(step == at[3])
    def _():
        keep.wait()
        share(core).wait_send()
        share(1 - core).wait_recv()


def _twice(t):
    lo = _lane_lo()
    other = pltpu.roll(t, HALF, 1)
    return jnp.concatenate([jnp.where(lo, t, other), jnp.where(lo, other, t)], axis=1)


def _once(g):
    first, second = g[:, :HEAD_LANES], g[:, HEAD_LANES:]
    return jnp.where(_lane_lo(), first + pltpu.roll(first, HALF, 1), second + pltpu.roll(second, HALF, 1))


def _rope_tables(pos_ref, inv_row, rope_ref):
    quarter = pos_ref.shape[0] // 4
    lane = lax.broadcasted_iota(jnp.int32, (1, HEAD_LANES), 1)
    pos = [pos_ref[g * quarter:(g + 1) * quarter, :] for g in range(4)]
    ang = jnp.where(lane < 32, pos[0], jnp.where(lane < 64, pos[1], jnp.where(lane < 96, pos[2], pos[3]))) * inv_row
    cos, sin = jnp.cos(ang), jnp.sin(ang)
    rope_lanes = jnp.logical_and(lane >= HALF, lane < HALF + MLA_ROPE)
    for g in range(4):
        rows = slice(g * quarter, (g + 1) * quarter)
        shift = (HALF - 32 * g) % HEAD_LANES
        at = lambda t: t if shift == 0 else pltpu.roll(t, shift, 1)
        rope_ref[rows, :HEAD_LANES] = jnp.where(rope_lanes, at(cos), 1.0)
        rope_ref[rows, HEAD_LANES:] = jnp.where(rope_lanes, at(sin), 0.0)


def _gather_in_steps(step, n_steps, w_refs, full_refs, w_send, w_recv, f_send, f_recv, loc_sem):
    me, _, chip, sibling, others = _position()
    core = me[2]
    chip_of = [2 * p[0] + p[1] for p in others]
    n = len(w_refs)
    local = [pltpu.make_async_copy(w_refs[i], full_refs[i].at[chip], loc_sem.at[i]) for i in range(n)]

    def over_ici(i, j, src_chip):
        return pltpu.make_async_remote_copy(
            src_ref=w_refs[i].at[core], dst_ref=full_refs[i].at[src_chip, core], send_sem=w_send.at[3 * i + j],
            recv_sem=w_recv.at[3 * i + j], device_id=others[j], device_id_type=MESH)

    def to_sibling(i, j, half):
        return pltpu.make_async_remote_copy(
            src_ref=full_refs[i].at[chip_of[j], half], dst_ref=full_refs[i].at[chip_of[j], half],
            send_sem=f_send.at[3 * i + j], recv_sem=f_recv.at[3 * i + j], device_id=sibling, device_id_type=MESH)

    pairs = [(i, j) for i in range(n) for j in range(3)]

    @pl.when(step == 0)
    def _():
        for cp in local:
            cp.start()
        for i, j in pairs:
            over_ici(i, j, chip).start()

    @pl.when(step == 3 * n_steps // 4)
    def _():
        for i, j in pairs:
            over_ici(i, j, chip_of[j]).wait_recv()
            to_sibling(i, j, core).start()

    @pl.when(step == n_steps - 1)
    def _():
        for i, j in pairs:
            to_sibling(i, j, 1 - core).wait_recv()
            to_sibling(i, j, core).wait_send()
            over_ici(i, j, chip).wait_send()
        for cp in local:
            cp.wait()


def _pre_call(x, pos_col, mod, b_ada, ng, inv128, wa, shards, seq):
    n_tok = x.shape[0]
    tm = min(TOKEN_TILE, seq)
    per_seq = seq // tm
    n_steps = n_tok // tm
    n = len(shards)

    def body(x_ref, pos_ref, mod_ref, bada_ref, ng_ref, inv_ref, wa_ref, *refs):
        w_refs, refs = refs[:n], refs[n:]
        zqkv_ref, zkr_ref, gates_ref, qs_ref, kd_ref, vd_ref, rope_ref = refs[:7]
        full_refs, sems = refs[7:7 + n], refs[7 + n:]
        _gather_in_steps(pl.program_id(0), n_steps, w_refs, full_refs, *sems)
        _rope_tables(pos_ref, inv_ref[...], rope_ref)
        xv = x_ref[...]
        modv = mod_ref[0] + bada_ref[...]
        shift, scale = modv[:, :D_MODEL], modv[:, D_MODEL:2 * D_MODEL]
        r1 = lax.rsqrt(jnp.mean(xv * xv, axis=-1, keepdims=True) + EPS)
        h = ((xv * r1) * ng_ref[...]) * (1.0 + scale) + shift
        za = _dot(h.astype(BF16), wa_ref[...])
        zqkv_ref[...] = za[:, :A_KR]
        zkr_ref[...] = za[:, A_KR:A_GM]
        gates_ref[:, :512] = za[:, A_GM:A_QS]
        gates_ref[:, 512:] = za[:, A_GS:A_END]
        qs_ref[...] = (za[:, A_QS:A_KS] * (SWA_SCALE * LOG2E)).astype(BF16)
        kd_ref[...] = _twice(za[:, A_KS:A_VS]).astype(BF16)
        vd_ref[...] = _twice(za[:, A_VS:A_GS]).astype(BF16)

    tok = lambda w: pl.BlockSpec((tm, w), lambda i: (i, 0))
    outs = [(640, F32), (HEAD_LANES, F32), (1024, F32), (512, BF16), (256, BF16), (256, BF16), (2 * HEAD_LANES, F32)]
    dma = pltpu.SemaphoreType.DMA
    return pl.pallas_call(
        body, name="pre", grid=(n_steps,),
        out_shape=[jax.ShapeDtypeStruct((n_tok, w), dt) for w, dt in outs]
        + [jax.ShapeDtypeStruct((4,) + s.shape, s.dtype) for s in shards],
        in_specs=[tok(D_MODEL), tok(1), pl.BlockSpec((1, 1, 3 * D_MODEL), lambda i: (i // per_seq, 0, 0)),
                  _full(b_ada.shape), _full(ng.shape), _full(inv128.shape), _full(wa.shape)] + [ANY_SPEC] * n,
        out_specs=[tok(w) for w, _ in outs] + [ANY_SPEC] * n,
        scratch_shapes=[dma((3 * n,)), dma((3 * n,)), dma((3 * n,)), dma((3 * n,)), dma((n,))],
        compiler_params=_params(1),
    )(x, pos_col, mod, b_ada, ng, inv128, wa, *shards)


def _up_call(zqkv, zkr, rope, qg, kvg, wq2, wkv, seq):
    n_tok = zqkv.shape[0]
    tm = min(TOKEN_TILE, seq)

    def body(zqkv_ref, zkr_ref, rope_ref, qg_ref, kvg_ref, wq_ref, wkv_ref, qf_ref, kf_ref, v_ref):
        cos, sin = rope_ref[:, :HEAD_LANES], rope_ref[:, HEAD_LANES:]
        zq, zkv = zqkv_ref[:, A_ZQ:A_ZKV], zqkv_ref[:, A_ZKV:A_KR]
        rq = lax.rsqrt(jnp.mean(zq * zq, axis=-1, keepdims=True) + EPS)
        qn = ((zq * rq) * qg_ref[...]).astype(BF16)
        qr = _dot(qn, wq_ref[...])
        cf, sf = jnp.tile(cos, (1, N_HEADS)), jnp.tile(sin, (1, N_HEADS))
        qf_ref[...] = ((qr[:, :1024] * cf + qr[:, 1024:] * sf) * (MLA_SCALE * LOG2E)).astype(BF16)
        rkv = lax.rsqrt(jnp.mean(zkv * zkv, axis=-1, keepdims=True) + EPS)
        kvn = ((zkv * rkv) * kvg_ref[...]).astype(BF16)
        kv = _dot(kvn, wkv_ref[...])
        zkr = zkr_ref[...]
        kpe = jnp.where(_lane_lo(), 0.0, zkr * cos) + pltpu.roll(zkr, HALF, 1) * sin
        kf_ref[...] = (kv[:, :1024] + jnp.tile(kpe, (1, N_HEADS))).astype(BF16)
        v_ref[...] = kv[:, 1024:].astype(BF16)

    tok = lambda w: pl.BlockSpec((tm, w), lambda i: (i, 0))
    outs = [(1024, BF16), (1024, BF16), (512, BF16)]
    return pl.pallas_call(
        body, name="up", grid=(n_tok // tm,),
        out_shape=[jax.ShapeDtypeStruct((n_tok, w), dt) for w, dt in outs],
        in_specs=[tok(640), tok(HEAD_LANES), tok(2 * HEAD_LANES), _full(qg.shape), _full(kvg.shape), _full(wq2.shape),
                  _full(wkv.shape)],
        out_specs=[tok(w) for w, _ in outs],
        compiler_params=_params(1),
    )(zqkv, zkr, rope, qg, kvg, wq2, wkv)


def _lane_lo(width=HEAD_LANES):
    return lax.broadcasted_iota(jnp.int32, (1, width), 1) < HALF


def _eye(n=HEAD_LANES):
    r = lax.broadcasted_iota(jnp.int32, (n, n), 0)
    c = lax.broadcasted_iota(jnp.int32, (n, n), 1)
    return jnp.where(r == c, 1.0, 0.0).astype(BF16)


def _mla_fwd_call(qf, kf, v, n_seq, seq):
    tq = min(ATT_TILE, seq)
    nq = seq // tq

    ext = HALF + 16

    def body(q_ref, k_ref, v_ref, o_ref, lse_ref, vt_ref, acc_ref):
        i = pl.program_id(1)
        eye = _eye()

        @pl.when(i == 0)
        def _():
            for h in range(N_HEADS):
                vt_ref[h * ext + HALF:(h + 1) * ext, :] = jnp.ones((16, seq), BF16)
            for t in range(nq):
                for p in range(N_HEADS // 2):
                    pair = slice(p * HEAD_LANES, (p + 1) * HEAD_LANES)
                    v_t = _dot_nt(eye, v_ref[t * tq:(t + 1) * tq, pair]).astype(BF16)
                    for hh in range(2):
                        r0 = (2 * p + hh) * ext
                        vt_ref[r0:r0 + HALF, t * tq:(t + 1) * tq] = v_t[hh * HALF:(hh + 1) * HALF, :]

        q = q_ref[...]
        qcol = i * tq + lax.broadcasted_iota(jnp.int32, (1, tq), 1)
        heads = range(N_HEADS)
        lanes = [slice(h * HEAD_LANES, (h + 1) * HEAD_LANES) for h in heads]

        def make_step(masked, n_tiles):
            def step(kt0, carry):
                tiles = range(n_tiles)
                start = pl.multiple_of(kt0 * tq, tq)
                ks = [k_ref[pl.ds(pl.multiple_of((kt0 + t) * tq, tq), tq), :] for t in tiles]
                vt = vt_ref[:, pl.ds(start, n_tiles * tq)]
                last = n_tiles - 1
                if masked:
                    keep = ((kt0 + last) * tq + lax.broadcasted_iota(jnp.int32, (tq, 1), 0)) <= qcol

                def scores(h):
                    sts = [_dot_nt(ks[t][:, lanes[h]], q[:, lanes[h]]) for t in tiles]
                    if masked:
                        sts[last] = jnp.where(keep, sts[last], NEG)
                    return sts

                def softmax(h, sts):
                    m_old = carry[h]
                    m_new = m_old
                    for st in sts:
                        m_new = jnp.maximum(m_new, jnp.max(st, axis=0, keepdims=True))
                    pt = jnp.concatenate([jnp.exp2(st - m_new).astype(BF16) for st in sts], axis=0)
                    return m_new, jnp.exp2(m_old - m_new), pt

                def values(h, alpha, pt):
                    rows = slice(h * ext, (h + 1) * ext)
                    acc_ref[rows, :] = acc_ref[rows, :] * alpha + _dot(vt[rows, :], pt)

                sts, soft, out = {0: scores(0), 1: scores(1)}, {}, {}
                for h in range(N_HEADS + 1):
                    if h + 2 < N_HEADS:
                        sts[h + 2] = scores(h + 2)
                    if h < N_HEADS:
                        soft[h] = softmax(h, sts.pop(h))
                    if h >= 1:
                        m_new, alpha, pt = soft.pop(h - 1)
                        values(h - 1, alpha, pt)
                        out[h - 1] = m_new
                return tuple(out[h] for h in heads)
            return step

        acc_ref[...] = jnp.zeros_like(acc_ref)
        init = (jnp.full((1, tq), NEG, F32),) * N_HEADS
        count = i + 1
        carry = lax.fori_loop(0, (count + 1) // 2 - 1, lambda j, c: make_step(False, 2)(2 * j, c), init)
        carry = lax.cond(count % 2 == 0, lambda c: make_step(True, 2)(i - 1, c), lambda c: make_step(True, 1)(i, c), carry)
        dens = [acc_ref[h * ext + HALF:h * ext + HALF + 1, :] for h in heads]
        acc_t = jnp.concatenate([acc_ref[h * ext:h * ext + HALF, :] * (1.0 / dens[h]) for h in heads], axis=0)
        o_ref[...] = acc_t.T
        for h in heads:
            lse_ref[0, h // 4, h % 4:h % 4 + 1, :] = carry[h] + jnp.log2(dens[h])

    n_tok = qf.shape[0]
    return pl.pallas_call(
        body, name="mla_fwd", grid=(n_seq, nq),
        out_shape=[jax.ShapeDtypeStruct((n_tok, 512), F32), jax.ShapeDtypeStruct((n_seq, 2, 4, seq), F32)],
        in_specs=[pl.BlockSpec((tq, 1024), lambda b, i: (b * nq + i, 0)),
                  pl.BlockSpec((seq, 1024), lambda b, i: (b, 0)),
                  pl.BlockSpec((seq, 512), lambda b, i: (b, 0))],
        out_specs=[pl.BlockSpec((tq, 512), lambda b, i: (b * nq + i, 0)),
                   pl.BlockSpec((1, 2, 4, tq), lambda b, i: (b, 0, 0, i))],
        scratch_shapes=[pltpu.VMEM((N_HEADS * ext, seq), BF16), pltpu.VMEM((N_HEADS * ext, tq), F32)],
        compiler_params=_params(2),
    )(qf, kf, v)


def _mla_bwd_call(qf, kf, v, do, delta, lse, n_seq, seq):
    tq = min(ATT_TILE, seq)
    nq = seq // tq

    nh = 4
    heads = range(nh)
    lanes = [slice(h * HEAD_LANES, (h + 1) * HEAD_LANES) for h in heads]

    def body(q_ref, k_ref, v_ref, do_ref, dl_ref, lse_ref, dq_ref, dk_ref, dv_ref,
             kt_ref, dot_ref, dqt_ref, dvt_ref):
        eye = _eye()
        sub_lo = lax.broadcasted_iota(jnp.int32, (HEAD_LANES, 1), 0) < HALF

        for t in range(nq):
            r = slice(t * tq, (t + 1) * tq)
            kv = k_ref[r, :]
            for h in heads:
                kt_ref[lanes[h], r] = _dot_nt(eye, kv[:, lanes[h]]).astype(BF16)
            for p in range(nh // 2):
                dov = do_ref[r, lanes[p]]
                dt = _dot_nt(eye, dov)
                dot_ref[2 * p, :, r] = jnp.where(sub_lo, dt, 0.0).astype(BF16)
                dot_ref[2 * p + 1, :, r] = jnp.where(sub_lo, 0.0, dt).astype(BF16)
        dqt_ref[...] = jnp.zeros_like(dqt_ref)
        dvt_ref[...] = jnp.zeros_like(dvt_ref)

        def flush_dv(tile, which):
            rows = pl.ds(pl.multiple_of(tile * tq, tq), tq)
            for p in range(nh // 2):
                dv_ref[rows, lanes[p]] = dvt_ref[which, p * HEAD_LANES:(p + 1) * HEAD_LANES, :].T

        def k_step(kt, _):
            slot = kt % 2
            kr = pl.ds(pl.multiple_of(kt * tq, tq), tq)
            k = k_ref[kr, :]
            vv = v_ref[kr, :]
            k_t = kt_ref[:, kr]
            krow = kt * tq + lax.broadcasted_iota(jnp.int32, (tq, 1), 0)

            def make_step(masked, n_tiles):
                def q_step(qt0, carry):
                    tiles = range(n_tiles)
                    qrs = [pl.ds(pl.multiple_of((qt0 + t) * tq, tq), tq) for t in tiles]
                    if masked:
                        flush_dv(jnp.maximum(kt - 1, 0), 1 - slot)
                    qs = [q_ref[qr, :] for qr in qrs]
                    if masked:
                        keep = krow <= (qt0 * tq + lax.broadcasted_iota(jnp.int32, (1, tq), 1))

                    def scores(h):
                        do_ts = [dot_ref[h, :, qr] for qr in qrs]
                        sts = [_dot_nt(k[:, lanes[h]], qs[t][:, lanes[h]]) for t in tiles]
                        dpts = [_dot(vv[:, lanes[h // 2]], do_ts[t]) for t in tiles]
                        return do_ts, sts, dpts

                    def softmax(h, sts, dpts):
                        pts, dsts = [], []
                        for t in tiles:
                            pt = jnp.exp2(sts[t] - lse_ref[0, 0, h:h + 1, qrs[t]])
                            if masked and t == 0:
                                pt = jnp.where(keep, pt, 0.0)
                            dsts.append((pt * (dpts[t] - dl_ref[0, h:h + 1, qrs[t]])).astype(BF16))
                            pts.append(pt.astype(BF16))
                        return pts, dsts

                    def grads(h, do_ts, pts, dsts):
                        half = slice((h % 2) * HALF, (h % 2 + 1) * HALF)
                        dst_all = jnp.concatenate(dsts, axis=1)
                        pt_all = jnp.concatenate(pts, axis=1)
                        do_all = jnp.concatenate([do_ts[t][half, :] for t in tiles], axis=1)
                        q_all = jnp.concatenate([qs[t][:, lanes[h]] for t in tiles], axis=0)
                        dvt_ref[slot, h * HALF:(h + 1) * HALF, :] += _dot_nt(do_all, pt_all)
                        dk_ref[kr, lanes[h]] += _dot(dst_all, q_all)
                        for t in tiles:
                            dqt_ref[lanes[h], qrs[t]] += _dot(k_t[lanes[h], :], dsts[t])

                    first, second = {0: scores(0)}, {}
                    for h in range(nh + 1):
                        if h + 1 < nh:
                            first[h + 1] = scores(h + 1)
                        if h < nh:
                            do_ts, sts, dpts = first.pop(h)
                            second[h] = (do_ts,) + softmax(h, sts, dpts)
                        if h >= 1:
                            grads(h - 1, *second.pop(h - 1))
                    return carry
                return q_step

            dk_ref[kr, :] = jnp.zeros((tq, nh * HEAD_LANES), F32)
            dvt_ref[slot] = jnp.zeros(dvt_ref.shape[1:], F32)
            count = nq - kt
            lax.cond(count >= 2, lambda c: make_step(True, 2)(kt, c), lambda c: make_step(True, 1)(kt, c), 0)
            lax.fori_loop(1, count // 2, lambda j, c: make_step(False, 2)(kt + 2 * j, c), 0)
            lax.cond(jnp.logical_and(count % 2 == 1, count >= 3), lambda c: make_step(False, 1)(nq - 1, c), lambda c: c, 0)
            return 0

        lax.fori_loop(0, nq, k_step, 0)
        flush_dv(nq - 1, (nq - 1) % 2)
        for t in range(nq):
            r = slice(t * tq, (t + 1) * tq)
            for h in heads:
                dq_ref[r, lanes[h]] = dqt_ref[lanes[h], r].T

    n_tok = qf.shape[0]
    groups = N_HEADS // nh
    blk = lambda w: pl.BlockSpec((seq, w), lambda b, g: (b, g))
    return pl.pallas_call(
        body, name="mla_bwd", grid=(n_seq, groups),
        out_shape=[jax.ShapeDtypeStruct((n_tok, 1024), F32), jax.ShapeDtypeStruct((n_tok, 1024), F32),
                   jax.ShapeDtypeStruct((n_tok, 512), F32)],
        in_specs=[blk(512), blk(512), blk(256), blk(256), pl.BlockSpec((1, nh, seq), lambda b, g: (g, 0, b)),
                  pl.BlockSpec((1, 1, nh, seq), lambda b, g: (b, g, 0, 0))],
        out_specs=[blk(512), blk(512), blk(256)],
        scratch_shapes=[pltpu.VMEM((nh * HEAD_LANES, seq), BF16), pltpu.VMEM((nh, HEAD_LANES, seq), BF16),
                        pltpu.VMEM((nh * HEAD_LANES, seq), F32), pltpu.VMEM((2, nh * HALF, tq), F32)],
        compiler_params=_params(2),
    )(qf, kf, v, do, delta, lse)


SWA_BLOCKS = 4


def _swa_block(n, pos_col_ref, posq):
    w = SWA_WINDOW
    start = pl.multiple_of(jnp.maximum(n - 1, 0) * w, w)
    posk = pos_col_ref[pl.ds(start, 2 * w), :]
    rel = (n * w + lax.broadcasted_iota(jnp.int32, (1, w), 1)) - (start + lax.broadcasted_iota(jnp.int32, (2 * w, 1), 0))
    valid = jnp.logical_and(rel >= 0, rel < w)
    return start, jnp.where(valid, posq - posk, 1e30)


def _alibi(h):
    return LOG2E * 2.0 ** -(h + 1)


def _transpose_rows(eye, src_ref, dst_ref, seq, width):
    step = 2 * SWA_WINDOW
    for t in range(seq // step):
        for p in range(width // HEAD_LANES):
            lanes = slice(p * HEAD_LANES, (p + 1) * HEAD_LANES)
            dst_ref[lanes, t * step:(t + 1) * step] = _dot_nt(eye, src_ref[t * step:(t + 1) * step, lanes]).astype(BF16)


def _swa_fwd_call(qs, kd, vd, pos_col, pos_row, sinks, n_seq, seq):
    w = SWA_WINDOW
    qb = SWA_BLOCKS
    steps = seq // (qb * w)
    ext = HALF + 16

    def body(q_ref, k_ref, v_ref, pc_ref, pr_ref, sink_ref, o_ref, lse_ref, vt_ref):
        n = pl.program_id(1)
        lo = _lane_lo()
        hi = jnp.logical_not(lo)
        eye = _eye()

        @pl.when(n == 0)
        def _():
            step = 2 * w
            for kv in range(2):
                vt_ref[kv * ext + HALF:(kv + 1) * ext, :] = jnp.ones((16, seq), BF16)
                for t in range(seq // step):
                    v_t = _dot_nt(eye, v_ref[t * step:(t + 1) * step, kv * HEAD_LANES:(kv + 1) * HEAD_LANES])
                    vt_ref[kv * ext:kv * ext + HALF, t * step:(t + 1) * step] = v_t[:HALF, :].astype(BF16)

        heads = range(N_HEADS)
        blocks = range(qb)
        geo = [_swa_block(n * qb + bi, pc_ref, pr_ref[bi]) for bi in blocks]
        wins = [pl.ds(g[0], 2 * w) for g in geo]
        kwins = [k_ref[win, :] for win in wins]
        vts = [vt_ref[:, win] for win in wins]
        sts = []
        for bi in blocks:
            q = q_ref[bi * w:(bi + 1) * w, :]
            sts.append([])
            for j in range(N_HEADS // 2):
                qp = q[:, j * HEAD_LANES:(j + 1) * HEAD_LANES]
                both = jnp.concatenate([jnp.where(lo, qp, jnp.zeros_like(qp)), jnp.where(hi, qp, jnp.zeros_like(qp))], axis=0)
                st = _dot_nt(kwins[bi][:, (j // 2) * HEAD_LANES:(j // 2 + 1) * HEAD_LANES], both)
                sts[bi] += [st[:, :w], st[:, w:]]
        ps, ms = [], []
        for bi in blocks:
            ps.append([])
            ms.append([])
            for h in heads:
                s = sts[bi][h] - _alibi(h) * geo[bi][1]
                m = jnp.maximum(jnp.max(s, axis=0, keepdims=True), sink_ref[0, h] * LOG2E)
                ps[bi].append(jnp.exp2(s - m).astype(BF16))
                ms[bi].append(m)
        for bi in blocks:
            ots = []
            for h in heads:
                pv = _dot(vts[bi][(h // 4) * ext:(h // 4 + 1) * ext, :], ps[bi][h])
                l = pv[HALF:HALF + 1, :] + jnp.exp2(sink_ref[0, h] * LOG2E - ms[bi][h])
                ots.append(pv[:HALF, :] * (1.0 / l))
                lse_ref[0, h:h + 1, bi * w:(bi + 1) * w] = ms[bi][h] + jnp.log2(l)
            o_ref[bi * w:(bi + 1) * w, :] = jnp.concatenate(ots, axis=0).T

    n_tok = qs.shape[0]
    tok = lambda width: pl.BlockSpec((qb * w, width), lambda b, n: (b * steps + n, 0))
    whole = lambda width: pl.BlockSpec((seq, width), lambda b, n: (b, 0))
    return pl.pallas_call(
        body, name="swa_fwd", grid=(n_seq, steps),
        out_shape=[jax.ShapeDtypeStruct((n_tok, 512), F32), jax.ShapeDtypeStruct((n_seq, N_HEADS, seq), F32)],
        in_specs=[tok(512), whole(256), whole(256), whole(1), pl.BlockSpec((qb, 1, w), lambda b, n: (b * steps + n, 0, 0)),
                  pl.BlockSpec(memory_space=pltpu.SMEM)],
        out_specs=[tok(512), pl.BlockSpec((1, N_HEADS, qb * w), lambda b, n: (b, 0, n))],
        scratch_shapes=[pltpu.VMEM((2 * ext, seq), BF16)],
        compiler_params=_params(2),
    )(qs, kd, vd, pos_col, pos_row, sinks)


def _swa_bwd_call(qs, kd, vd, do, delta, lse, pos_col, pos_row, sinks, grads, n_seq, seq):
    w = SWA_WINDOW
    qb = SWA_BLOCKS
    steps = seq // (qb * w)
    ng = len(grads)
    reduced, reduce_scratch = zip(*[_reduce_operands(g) for g in grads])
    per = len(reduce_scratch[0])

    def body(q_ref, k_ref, v_ref, do_ref, dl_ref, lse_ref, pc_ref, pr_ref, sink_ref, *refs):
        g_refs, refs = refs[:ng], refs[ng:]
        dq_ref, dk_ref, dv_ref, dsink_ref = refs[:4]
        f_refs, kt_ref, reduce_refs = refs[4:4 + ng], refs[4 + ng], refs[5 + ng:]
        b, n = pl.program_id(0), pl.program_id(1)
        for i in range(ng):
            _grad_reduce(b * steps + n, n_seq * steps, g_refs[i], f_refs[i], *reduce_refs[per * i:per * (i + 1)])
        lo = _lane_lo()
        hi = jnp.logical_not(lo)
        sub_lo = lax.broadcasted_iota(jnp.int32, (HEAD_LANES, 1), 0) < HALF
        eye = _eye()

        @pl.when(n == 0)
        def _():
            dk_ref[...] = jnp.zeros_like(dk_ref)
            dv_ref[...] = jnp.zeros_like(dv_ref)
            _transpose_rows(eye, k_ref, kt_ref, seq, 2 * HEAD_LANES)

        @pl.when(jnp.logical_and(n == 0, b == 0))
        def _():
            dsink_ref[...] = jnp.zeros_like(dsink_ref)

        heads = range(N_HEADS)
        blocks = range(qb)
        kv_lanes = lambda h: slice((h // 4) * HEAD_LANES, (h // 4 + 1) * HEAD_LANES)
        geo = [_swa_block(n * qb + bi, pc_ref, pr_ref[bi]) for bi in blocks]
        wins = [pl.ds(g[0], 2 * w) for g in geo]
        kwins = [k_ref[win, :] for win in wins]
        vwins = [v_ref[win, :] for win in wins]

        do_ts, deltas, qms, doms = [], [], [], []
        for bi in blocks:
            rows = slice(bi * w, (bi + 1) * w)
            for lst in (do_ts, deltas, qms, doms):
                lst.append([])
            for j in range(N_HEADS // 2):
                pair = slice(j * HEAD_LANES, (j + 1) * HEAD_LANES)
                dop = do_ref[rows, pair]
                qp = q_ref[rows, pair]
                dt = _dot_nt(eye, dop)
                for hh in range(2):
                    half = lo if hh == 0 else hi
                    do_ts[bi].append(jnp.where(sub_lo, dt, 0.0).astype(BF16) if hh == 0
                                     else jnp.where(sub_lo, 0.0, dt).astype(BF16))
                    deltas[bi].append(dl_ref[2 * j + hh:2 * j + hh + 1, rows])
                    qms[bi].append(jnp.where(half, qp, jnp.zeros_like(qp)))
                    doms[bi].append(jnp.where(half, dop, jnp.zeros_like(dop)))
        sts, dpts = [], []
        for bi in blocks:
            sts.append([])
            dpts.append([])
            for j in range(N_HEADS // 2):
                a, b = 2 * j, 2 * j + 1
                st = _dot_nt(kwins[bi][:, kv_lanes(a)], jnp.concatenate([qms[bi][a], qms[bi][b]], axis=0))
                dpt = _dot(vwins[bi][:, kv_lanes(a)], jnp.concatenate([do_ts[bi][a], do_ts[bi][b]], axis=1))
                sts[bi] += [st[:, :w], st[:, w:]]
                dpts[bi] += [dpt[:, :w], dpt[:, w:]]
        pts, dsts = [], []
        for bi in blocks:
            pts.append([])
            dsts.append([])
            for h in heads:
                lse_h = lse_ref[0, h:h + 1, bi * w:(bi + 1) * w]
                pt = jnp.exp2(sts[bi][h] - _alibi(h) * geo[bi][1] - lse_h)
                dsts[bi].append((pt * (dpts[bi][h] - deltas[bi][h])).astype(BF16))
                pts[bi].append(pt.astype(BF16))
                dsink_ref[h:h + 1, :] += -jnp.exp2(sink_ref[0, h] * LOG2E - lse_h) * deltas[bi][h]
        for bi in blocks:
            for kv in range(2):
                group = range(4 * kv, 4 * kv + 4)
                dst_all = jnp.concatenate([dsts[bi][h] for h in group], axis=1)
                pt_all = jnp.concatenate([pts[bi][h] for h in group], axis=1)
                q_all = jnp.concatenate([qms[bi][h] for h in group], axis=0)
                do_all = jnp.concatenate([doms[bi][h] for h in group], axis=0)
                dk_ref[wins[bi], kv_lanes(4 * kv)] += _dot(dst_all, q_all)
                dv_ref[wins[bi], kv_lanes(4 * kv)] += _dot(pt_all, do_all)
        for bi in blocks:
            ktw = kt_ref[:, wins[bi]]
            for j in range(N_HEADS // 2):
                k_t = ktw[kv_lanes(2 * j), :]
                both = _dot(k_t, jnp.concatenate([dsts[bi][2 * j], dsts[bi][2 * j + 1]], axis=1))
                dq_t = jnp.where(sub_lo, both[:, :w], both[:, w:])
                dq_ref[bi * w:(bi + 1) * w, j * HEAD_LANES:(j + 1) * HEAD_LANES] = dq_t.T * SWA_SCALE

    n_tok = qs.shape[0]
    tok = lambda width: pl.BlockSpec((qb * w, width), lambda b, n: (b * steps + n, 0))
    whole = lambda width: pl.BlockSpec((seq, width), lambda b, n: (b, 0))
    return pl.pallas_call(
        body, name="swa_bwd", grid=(n_seq, steps),
        out_shape=[jax.ShapeDtypeStruct((n_tok, 512), F32), jax.ShapeDtypeStruct((n_tok, 256), F32),
                   jax.ShapeDtypeStruct((n_tok, 256), F32), jax.ShapeDtypeStruct((N_HEADS, HEAD_LANES), F32), *reduced],
        in_specs=[tok(512), whole(256), whole(256), pl.BlockSpec((qb * w, 512), lambda b, n: (b * steps + n, 1)),
                  pl.BlockSpec((N_HEADS, qb * w), lambda b, n: (0, b * steps + n)),
                  pl.BlockSpec((1, N_HEADS, qb * w), lambda b, n: (b, 0, n)),
                  whole(1), pl.BlockSpec((qb, 1, w), lambda b, n: (b * steps + n, 0, 0)),
                  pl.BlockSpec(memory_space=pltpu.SMEM)] + [ANY_SPEC] * ng,
        out_specs=[tok(512), whole(256), whole(256), _full((N_HEADS, HEAD_LANES))] + [ANY_SPEC] * ng,
        scratch_shapes=[pltpu.VMEM((2 * HEAD_LANES, seq), BF16)] + [s for group in reduce_scratch for s in group],
        compiler_params=_params(2),
    )(qs, kd, vd, do, delta, lse, pos_col, pos_row, sinks, *grads)


def _post_call(x, target, o_mla, o_swa, gates, mod, b_ada, fg, w_out, seq):
    n_tok = x.shape[0]
    tm = min(TOKEN_TILE, seq)
    per_seq = seq // tm
    n_seq = n_tok // seq

    def body(x_ref, t_ref, om_ref, os_ref, g_ref, mod_ref, bada_ref, fg_ref, w_ref,
             dx2_ref, do_ref, dg_ref, gw_ref, gfg_ref, dgate_ref, loss_ref, dmla_ref, dswa_ref):
        i = pl.program_id(0)

        @pl.when(i == 0)
        def _():
            gw_ref[...] = jnp.zeros_like(gw_ref)
            gfg_ref[...] = jnp.zeros_like(gfg_ref)
            loss_ref[...] = jnp.zeros_like(loss_ref)

        @pl.when(i % per_seq == 0)
        def _():
            dgate_ref[...] = jnp.zeros_like(dgate_ref)

        gate = mod_ref[0][:, 2 * D_MODEL:] + bada_ref[:, 2 * D_MODEL:]
        fgv = fg_ref[...]
        fgd = fgv * (1.0 / D_MODEL)
        subs = _sub_tiles(tm)
        gs = [g_ref[r, :] for r in subs]
        os_ = [jnp.concatenate([om_ref[r, :], os_ref[r, :]], axis=-1) for r in subs]
        sgs = [_sigmoid(g) for g in gs]
        sils = [g * sg for g, sg in zip(gs, sgs)]
        ypres = [(o * sil).astype(BF16) for o, sil in zip(os_, sils)]
        ys = [_dot(ypre, w_ref[...]) for ypre in ypres]
        dys, loss, gfg, dgate = [], 0.0, 0.0, 0.0
        for r, y in zip(subs, ys):
            x2 = x_ref[r, :] + gate * y
            r2 = lax.rsqrt(jnp.mean(x2 * x2, axis=-1, keepdims=True) + EPS)
            xn2 = x2 * r2
            err = xn2 * fgv - t_ref[r, :]
            loss = loss + jnp.sum(jnp.sum(err * err, axis=-1, keepdims=True), axis=0, keepdims=True)
            gfg = gfg + jnp.sum(err * xn2, axis=0, keepdims=True)
            dxn2 = err * fgd
            dx2 = r2 * (dxn2 - xn2 * jnp.mean(dxn2 * xn2, axis=-1, keepdims=True))
            dx2_ref[r, :] = dx2
            dgate = dgate + jnp.sum(dx2 * y, axis=0, keepdims=True)
            dys.append((dx2 * gate).astype(BF16))
        loss_ref[...] += jnp.broadcast_to(loss * (0.5 / D_MODEL), loss_ref.shape)
        gfg_ref[...] += gfg * (1.0 / D_MODEL)
        dgate_ref[0] += dgate
        gw_ref[...] += _dot_tn(jnp.concatenate(ypres, axis=0), jnp.concatenate(dys, axis=0))
        dypres = [_dot_nt(dy, w_ref[...]) for dy in dys]
        pick = jnp.where(jnp.right_shift(lax.broadcasted_iota(jnp.int32, (2 * N_HEADS, D_MODEL), 1), 6)
                         == lax.broadcasted_iota(jnp.int32, (2 * N_HEADS, D_MODEL), 0), 1.0, 0.0).astype(BF16)
        for r, dypre, o, g, sg, sil in zip(subs, dypres, os_, gs, sgs, sils):
            dov = (dypre * sil).astype(BF16)
            do_ref[r, :] = dov
            delta = _dot_nt(pick, (dov.astype(F32) * o).astype(BF16))
            for grp in range(2):
                dmla_ref[grp, :, r] = delta[4 * grp:4 * grp + 4, :]
            dswa_ref[:, r] = delta[N_HEADS:, :]
            dg_ref[r, :] = (dypre * o * (sg + sil * (1.0 - sg))).astype(BF16)

    tok = lambda w: pl.BlockSpec((tm, w), lambda i: (i, 0))
    per_b = pl.BlockSpec((1, 1, 3 * D_MODEL), lambda i: (i // per_seq, 0, 0))
    return pl.pallas_call(
        body, name="post", grid=(n_tok // tm,),
        out_shape=[jax.ShapeDtypeStruct((n_tok, D_MODEL), F32), jax.ShapeDtypeStruct((n_tok, D_MODEL), BF16),
                   jax.ShapeDtypeStruct((n_tok, D_MODEL), BF16), jax.ShapeDtypeStruct((D_MODEL, D_MODEL), F32),
                   jax.ShapeDtypeStruct((1, D_MODEL), F32), jax.ShapeDtypeStruct((n_seq, 1, D_MODEL), F32),
                   jax.ShapeDtypeStruct((1, HEAD_LANES), F32),
                   jax.ShapeDtypeStruct((2, N_HEADS // 2, n_tok), F32), jax.ShapeDtypeStruct((N_HEADS, n_tok), F32)],
        in_specs=[tok(D_MODEL), tok(D_MODEL), tok(512), tok(512), tok(D_MODEL), per_b, _full(b_ada.shape),
                  _full(fg.shape), _full(w_out.shape)],
        out_specs=[tok(D_MODEL), tok(D_MODEL), tok(D_MODEL), _full((D_MODEL, D_MODEL)), _full((1, D_MODEL)),
                   pl.BlockSpec((1, 1, D_MODEL), lambda i: (i // per_seq, 0, 0)), _full((1, HEAD_LANES)),
                   pl.BlockSpec((2, N_HEADS // 2, tm), lambda i: (0, 0, i)), pl.BlockSpec((N_HEADS, tm), lambda i: (0, i))],
        compiler_params=_params(1),
    )(x, target, o_mla, o_swa, gates, mod, b_ada, fg, w_out)


def _mid_bwd_call(dqf, dkf, dv, zqkv, rope, qg, kvg, wq2, wkv, seq):
    n_tok = dqf.shape[0]
    tm = min(TOKEN_TILE, seq)

    def body(dq_ref, dk_ref, dv_ref, z_ref, rope_ref, qg_ref, kvg_ref, wq_ref, wkv_ref,
             dz_ref, gwq_ref, gwkv_ref, gqg_ref, gkvg_ref):
        i = pl.program_id(0)

        @pl.when(i == 0)
        def _():
            gwq_ref[...] = jnp.zeros_like(gwq_ref)
            gwkv_ref[...] = jnp.zeros_like(gwkv_ref)
            gqg_ref[...] = jnp.zeros_like(gqg_ref)
            gkvg_ref[...] = jnp.zeros_like(gkvg_ref)

        cos, sin = rope_ref[:, :HEAD_LANES], rope_ref[:, HEAD_LANES:]
        cf, sf = jnp.tile(cos, (1, N_HEADS)), jnp.tile(sin, (1, N_HEADS))
        dq = dq_ref[...] * MLA_SCALE
        dqr = jnp.concatenate([dq * cf, dq * sf], axis=-1).astype(BF16)
        zq, zkv = z_ref[:, :Q_LORA], z_ref[:, Q_LORA:]
        qgv, kvgv = qg_ref[...], kvg_ref[...]

        rq = lax.rsqrt(jnp.mean(zq * zq, axis=-1, keepdims=True) + EPS)
        xq = zq * rq
        gwq_ref[...] += _dot_tn((xq * qgv).astype(BF16), dqr)
        dqn = _dot_nt(dqr, wq_ref[...])
        gqg_ref[...] += jnp.sum(dqn * xq, axis=0, keepdims=True)
        dxq = dqn * qgv
        dz_ref[:, :Q_LORA] = (rq * (dxq - xq * jnp.mean(dxq * xq, axis=-1, keepdims=True))).astype(BF16)

        dk = dk_ref[...] * LN2
        dkv = jnp.concatenate([dk, dv_ref[...]], axis=-1).astype(BF16)
        rkv = lax.rsqrt(jnp.mean(zkv * zkv, axis=-1, keepdims=True) + EPS)
        xkv = zkv * rkv
        gwkv_ref[...] += _dot_tn((xkv * kvgv).astype(BF16), dkv)
        dkvn = _dot_nt(dkv, wkv_ref[...])
        gkvg_ref[...] += jnp.sum(dkvn * xkv, axis=0, keepdims=True)
        dxkv = dkvn * kvgv
        dz_ref[:, Q_LORA:A_KR] = (rkv * (dxkv - xkv * jnp.mean(dxkv * xkv, axis=-1, keepdims=True))).astype(BF16)

        dkpe = dk[:, :HEAD_LANES]
        for h in range(1, N_HEADS):
            dkpe = dkpe + dk[:, h * HEAD_LANES:(h + 1) * HEAD_LANES]
        dz_ref[:, A_KR:] = (jnp.where(_lane_lo(), 0.0, dkpe * cos) + pltpu.roll(dkpe * sin, HALF, 1)).astype(BF16)

    tok = lambda w: pl.BlockSpec((tm, w), lambda i: (i, 0))
    return pl.pallas_call(
        body, name="mid_bwd", grid=(n_tok // tm,),
        out_shape=[jax.ShapeDtypeStruct((n_tok, A_GM), BF16),
                   jax.ShapeDtypeStruct(wq2.shape, F32), jax.ShapeDtypeStruct(wkv.shape, F32),
                   jax.ShapeDtypeStruct((1, Q_LORA), F32), jax.ShapeDtypeStruct((1, KV_LORA), F32)],
        in_specs=[tok(1024), tok(1024), tok(512), tok(640), tok(2 * HEAD_LANES), _full(qg.shape), _full(kvg.shape),
                  _full(wq2.shape), _full(wkv.shape)],
        out_specs=[tok(A_GM), _full(wq2.shape), _full(wkv.shape), _full((1, Q_LORA)), _full((1, KV_LORA))],
        compiler_params=_params(1),
    )(dqf, dkf, dv, zqkv, rope, qg, kvg, wq2, wkv)


def _in_bwd_call(x, dx2, dz, dg, dqs, dkd, dvd, mod, b_ada, ng, wa, seq):
    n_tok = x.shape[0]
    tm = min(TOKEN_TILE, seq)
    per_seq = seq // tm
    n_seq = n_tok // seq

    def body(x_ref, dx2_ref, dz_ref, dg_ref, dqs_ref, dkd_ref, dvd_ref, mod_ref, bada_ref, ng_ref,
             wa_ref, gx_ref, gwa_ref, gng_ref, dshift_ref, dscale_ref):
        i = pl.program_id(0)

        @pl.when(i == 0)
        def _():
            gwa_ref[...] = jnp.zeros_like(gwa_ref)
            gng_ref[...] = jnp.zeros_like(gng_ref)

        @pl.when(i % per_seq == 0)
        def _():
            dshift_ref[...] = jnp.zeros_like(dshift_ref)
            dscale_ref[...] = jnp.zeros_like(dscale_ref)

        xv = x_ref[...]
        modv = mod_ref[0] + bada_ref[...]
        shift, scale = modv[:, :D_MODEL], modv[:, D_MODEL:2 * D_MODEL]
        ngv = ng_ref[...]
        r1 = lax.rsqrt(jnp.mean(xv * xv, axis=-1, keepdims=True) + EPS)
        xn = xv * r1
        hb = ((xn * ngv) * (1.0 + scale) + shift).astype(BF16)

        dgv = dg_ref[...]
        pieces = [(A_ZQ, dz_ref[...]), (A_GM, dgv[:, :512]), (A_QS, dqs_ref[...].astype(BF16)),
                  (A_KS, jnp.concatenate([_once(dkd_ref[...]) * LN2, _once(dvd_ref[...])], axis=1).astype(BF16)),
                  (A_GS, dgv[:, 512:])]
        dh = None
        for off, piece in pieces:
            wd = piece.shape[1]
            gwa_ref[:, off:off + wd] += _dot_tn(hb, piece)
            term = _dot_nt(piece, wa_ref[:, off:off + wd])
            dh = term if dh is None else dh + term

        dshift_ref[0] += jnp.sum(dh, axis=0, keepdims=True)
        dscale_ref[0] += jnp.sum(dh * (xn * ngv), axis=0, keepdims=True)
        gng_ref[...] += jnp.sum(dh * xn * (1.0 + scale), axis=0, keepdims=True)
        dxn = dh * ngv * (1.0 + scale)
        gx_ref[...] = dx2_ref[...] + r1 * (dxn - xn * jnp.mean(dxn * xn, axis=-1, keepdims=True))

    tok = lambda w: pl.BlockSpec((tm, w), lambda i: (i, 0))
    per_b = lambda w: pl.BlockSpec((1, 1, w), lambda i: (i // per_seq, 0, 0))
    return pl.pallas_call(
        body, name="in_bwd", grid=(n_tok // tm,),
        out_shape=[jax.ShapeDtypeStruct((n_tok, D_MODEL), F32), jax.ShapeDtypeStruct((D_MODEL, A_END), F32),
                   jax.ShapeDtypeStruct((1, D_MODEL), F32),
                   jax.ShapeDtypeStruct((n_seq, 1, D_MODEL), F32), jax.ShapeDtypeStruct((n_seq, 1, D_MODEL), F32)],
        in_specs=[tok(D_MODEL), tok(D_MODEL), tok(A_GM), tok(D_MODEL), tok(512), tok(256), tok(256),
                  per_b(3 * D_MODEL), _full(b_ada.shape), _full(ng.shape), _full(wa.shape)],
        out_specs=[tok(D_MODEL), _full((D_MODEL, A_END)), _full((1, D_MODEL)), per_b(D_MODEL), per_b(D_MODEL)],
        compiler_params=_params(1),
    )(x, dx2, dz, dg, dqs, dkd, dvd, mod, b_ada, ng, wa)


def _adam_math(w, g, m, v):
    m_new = ADAM_B1 * m + (1.0 - ADAM_B1) * g
    v_new = ADAM_B2 * v + (1.0 - ADAM_B2) * (g * g)
    m_hat = m_new / (1.0 - ADAM_B1 ** ADAM_STEP)
    v_hat = v_new / (1.0 - ADAM_B2 ** ADAM_STEP)
    delta = -ADAM_LR * (m_hat / (jnp.sqrt(v_hat) + ADAM_EPS) + ADAM_WD * w)
    return delta, m_new, v_new


def _adam_call(name, w, g, m, v):
    rows, cols = w.shape
    tr = next((t for t in (256, 128) if rows % t == 0), rows)

    def body(w_ref, g_ref, m_ref, v_ref, d_ref, mo_ref, vo_ref):
        d, mn, vn = _adam_math(w_ref[...], g_ref[...], m_ref[...], v_ref[...])
        d_ref[...] = d
        mo_ref[...] = mn
        vo_ref[...] = vn

    spec = pl.BlockSpec((tr, cols), lambda i: (i, 0))
    return pl.pallas_call(
        body, name=name, grid=(rows // tr,),
        out_shape=[jax.ShapeDtypeStruct(w.shape, F32)] * 3,
        in_specs=[spec] * 4, out_specs=[spec] * 3,
        compiler_params=_params(1),
    )(w, g, m, v)


def _ada_bwd_call(act_all, dmod_cols, w, m, v):
    rows, cols = w.shape
    tr = 512

    def body(a_ref, dm_ref, w_ref, m_ref, v_ref, g_ref, d_ref, mo_ref, vo_ref):
        g = _dot_tn(a_ref[...].astype(BF16), dm_ref[...].astype(BF16))
        d, mn, vn = _adam_math(w_ref[...], g, m_ref[...], v_ref[...])
        g_ref[...] = g
        d_ref[...] = d
        mo_ref[...] = mn
        vo_ref[...] = vn

    spec = pl.BlockSpec((tr, cols), lambda i: (i, 0))
    nb = act_all.shape[0]
    return pl.pallas_call(
        body, name="ada_bwd", grid=(rows // tr,),
        out_shape=[jax.ShapeDtypeStruct(w.shape, F32)] * 4,
        in_specs=[pl.BlockSpec((nb, tr), lambda i: (0, i)), _full(dmod_cols.shape), spec, spec, spec],
        out_specs=[spec] * 4,
        compiler_params=_params(1),
    )(act_all, dmod_cols, w, m, v)


SMALL_ROW = {"norm_gain": (0, 1024), "final_gain": (1024, 2048), "q_norm_gain": (2048, 2432),
             "kv_norm_gain": (2432, 2688), "swa_sinks": (2688, 2696), "loss": (2816, 2944)}
SMALL_ORDER = ("b_ada", "norm_gain", "q_norm_gain", "kv_norm_gain", "swa_sinks", "final_gain")


def _small_call(parts_all, n_seq, params):
    k = len(params)

    def body(p_ref, *refs):
        ins, outs, loss_ref = refs[:3 * k], refs[3 * k:7 * k], refs[7 * k]
        row = p_ref[n_seq:n_seq + 1, :]
        for dv in range(1, 8):
            r0 = dv * ROWS_PER_DEVICE + n_seq
            row = row + p_ref[r0:r0 + 1, :]
        gb = None
        for dv in range(8):
            for r in range(n_seq):
                r0 = dv * ROWS_PER_DEVICE + r
                gb = p_ref[r0:r0 + 1, :] if gb is None else gb + p_ref[r0:r0 + 1, :]
        for j, name in enumerate(SMALL_ORDER):
            g = gb if name == "b_ada" else row[:, SMALL_ROW[name][0]:SMALL_ROW[name][1]]
            d, mn, vn = _adam_math(ins[3 * j][...], g, ins[3 * j + 1][...], ins[3 * j + 2][...])
            outs[4 * j][...] = g
            outs[4 * j + 1][...] = d
            outs[4 * j + 2][...] = mn
            outs[4 * j + 3][...] = vn
        loss_ref[...] = row[:, SMALL_ROW["loss"][0]:SMALL_ROW["loss"][1]]

    flat = [t for p in params for t in p]
    res = pl.pallas_call(
        body, name="small_update", grid=(1,),
        out_shape=[jax.ShapeDtypeStruct(p[0].shape, F32) for p in params for _ in range(4)]
        + [jax.ShapeDtypeStruct((1, HEAD_LANES), F32)],
        in_specs=[_full(parts_all.shape)] + [_full(t.shape) for t in flat],
        out_specs=[_full(p[0].shape) for p in params for _ in range(4)] + [_full((1, HEAD_LANES))],
        compiler_params=_params(1),
    )(parts_all, *flat)
    return [res[4 * j:4 * j + 4] for j in range(k)], res[4 * k]


def _rot(t):
    half = t.shape[-1] // 2
    return jnp.concatenate([-t[..., half:], t[..., :half]], axis=-1)


def _rot_t(g):
    half = g.shape[-1] // 2
    return jnp.concatenate([g[..., half:], -g[..., :half]], axis=-1)


def _columns(segments, lo, hi):
    out, at = [], 0
    for seg in segments:
        n = seg.shape[1]
        a, b = max(lo, at), min(hi, at + n)
        if a < b:
            out.append(seg[:, a - at:b - at])
        at += n
    return out


def _prepare_in(w_in_blocks):
    o = [0]
    for s in IN_SPLITS:
        o.append(o[-1] + s)
    part = lambda a, b: _columns(w_in_blocks, a, b)
    kr = jnp.concatenate(part(o[2], o[3]), axis=1)
    zero = jnp.zeros((kr.shape[0], 32), kr.dtype)
    return jnp.concatenate(part(0, o[2]) + [_rot(kr), zero, kr, zero] + part(o[3], o[8]), axis=1)


def _prepare_up(w_uq, w_ukv):
    uq = w_uq.reshape(Q_LORA, N_HEADS, MLA_NOPE + MLA_ROPE)
    zq = jnp.zeros((Q_LORA, N_HEADS, 32), w_uq.dtype)
    uq_full = jnp.concatenate([uq, zq], axis=-1).reshape(Q_LORA, 1024)
    uq_rot = jnp.concatenate([jnp.zeros((Q_LORA, N_HEADS, 64), w_uq.dtype), _rot(uq[..., MLA_NOPE:]), zq],
                             axis=-1).reshape(Q_LORA, 1024)
    wq2 = jnp.concatenate([uq_full, uq_rot], axis=1)
    ukv = w_ukv.reshape(KV_LORA, N_HEADS, 128)
    k_full = jnp.concatenate([ukv[..., :64], jnp.zeros((KV_LORA, N_HEADS, 64), w_ukv.dtype)], axis=-1).reshape(KV_LORA, 1024)
    wkv = jnp.concatenate([k_full, ukv[..., 64:].reshape(KV_LORA, 512)], axis=1)
    return wq2, wkv


def _restore_in(gwa):
    gkr = gwa[:, A_KR + 64:A_KR + 96] + _rot_t(gwa[:, A_KR:A_KR + 32])
    in_order = [gwa[:, :A_KR], gkr, gwa[:, A_GM:]]
    n = D_IN // 4
    return [jnp.concatenate(_columns(in_order, k * n, (k + 1) * n), axis=1) for k in range(4)]


def _restore_up(gwq2, gwkv):
    gf = gwq2[:, :1024].reshape(Q_LORA, N_HEADS, 128)
    gr = gwq2[:, 1024:].reshape(Q_LORA, N_HEADS, 128)
    g_uq = jnp.concatenate([gf[..., :64], gf[..., 64:96] + _rot_t(gr[..., 64:96])], axis=-1).reshape(Q_LORA, 768)
    gk = gwkv[:, :1024].reshape(KV_LORA, N_HEADS, 128)[..., :64]
    gv = gwkv[:, 1024:].reshape(KV_LORA, N_HEADS, 64)
    g_ukv = jnp.concatenate([gk, gv], axis=-1).reshape(KV_LORA, 1024)
    return g_uq, g_ukv


def _local_step(x, positions, target, mod_rows, b_ada, ng, qg, kvg, sinks, fg, w_in_b, later_shards):
    n_seq, seq, _ = x.shape
    n_tok = n_seq * seq
    x2d = x.reshape(n_tok, D_MODEL)
    t2d = target.reshape(n_tok, D_MODEL)
    pos_f = positions.astype(F32)
    pos_col = pos_f.reshape(n_tok, 1)
    pos_row = pos_f.reshape(n_tok // SWA_WINDOW, 1, SWA_WINDOW)
    mod3 = mod_rows.reshape(n_seq, 1, 3 * D_MODEL)
    inv = ROPE_THETA ** (-jnp.arange(0, MLA_ROPE, 2, dtype=F32) / MLA_ROPE)
    inv128 = jnp.tile(jnp.concatenate([inv, inv]), 4).reshape(1, HEAD_LANES)
    fg2 = fg.reshape(1, D_MODEL)

    wa = _prepare_in(w_in_b)
    zqkv, zkr, gates, qs, kd, vd, rope, f_uq, f_ukv, f_out = _pre_call(x2d, pos_col, mod3, b_ada, ng, inv128, wa,
                                                                       later_shards, seq)
    cols = lambda t, r: jnp.transpose(t.reshape(4, r, -1), (1, 0, 2)).reshape(r, -1)
    wq2, wkv = _prepare_up(cols(f_uq, Q_LORA), cols(f_ukv, KV_LORA))
    w_out_b = f_out.reshape(D_MODEL, D_MODEL)
    qf, kf, v = _up_call(zqkv, zkr, rope, qg, kvg, wq2, wkv, seq)
    o_mla, lse_mla = _mla_fwd_call(qf, kf, v, n_seq, seq)
    o_swa, lse_swa = _swa_fwd_call(qs, kd, vd, pos_col, pos_row, sinks, n_seq, seq)
    dx2, do, dg, g_out, g_fg, dgate, loss, delta_mla, delta_swa = _post_call(x2d, t2d, o_mla, o_swa, gates, mod3, b_ada, fg2, w_out_b, seq)
    dqf, dkf, dv = _mla_bwd_call(qf, kf, v, do, delta_mla, lse_mla, n_seq, seq)
    dz, g_wq2, g_wkv, g_qg, g_kvg = _mid_bwd_call(dqf, dkf, dv, zqkv, rope, qg, kvg, wq2, wkv, seq)
    g_uq, g_ukv = _restore_up(g_wq2, g_wkv)
    grads = [_by_owner(g_uq, g_uq.shape[1] // 4), _by_owner(g_ukv, g_ukv.shape[1] // 4),
             g_out.reshape(4, 2, D_MODEL // 8, D_MODEL)]
    dqs, dkd, dvd, dsink, r_uq, r_ukv, r_out = _swa_bwd_call(qs, kd, vd, do, delta_swa, lse_swa, pos_col, pos_row, sinks,
                                                             grads, n_seq, seq)
    gx, g_wa, g_ng, dshift, dscale = _in_bwd_call(x2d, dx2, dz, dg, dqs, dkd, dvd, mod3, b_ada, ng, wa, seq)
    g_in = _restore_in(g_wa)
    dmod = jnp.concatenate([dshift, dscale, dgate], axis=-1).reshape(n_seq, 3 * D_MODEL)
    small_row = jnp.concatenate([g_ng, g_fg, g_qg, g_kvg, jnp.pad(jnp.sum(dsink, axis=1).reshape(1, N_HEADS), ((0, 0), (0, 120))),
                                 loss, jnp.zeros((1, 128), F32)], axis=1)
    return gx.reshape(x.shape), g_in, (r_uq, r_ukv, r_out), small_row, dmod


def kernel(x, c, positions, w_ada, b_ada, norm_gain, w_in, q_norm_gain, kv_norm_gain, w_uq, w_ukv, swa_sinks, w_out, final_gain, loss_target, m_w_ada, m_b_ada, m_norm_gain, m_w_in, m_q_norm_gain, m_kv_norm_gain, m_w_uq, m_w_ukv, m_swa_sinks, m_w_out, m_final_gain, v_w_ada, v_b_ada, v_norm_gain, v_w_in, v_q_norm_gain, v_kv_norm_gain, v_w_uq, v_w_ukv, v_swa_sinks, v_w_out, v_final_gain):
    n_seq = x.shape[0]
    xi, yi, ci = lax.axis_index("x"), lax.axis_index("y"), lax.axis_index("c")
    dev = 4 * xi + 2 * yi + ci
    chip = 2 * xi + yi

    halves = lambda w: w.astype(BF16).reshape(2, w.shape[0] // 2, w.shape[1])
    c_blk = jnp.pad(c, ((0, ROWS_PER_DEVICE - n_seq), (0, 0)))
    act_all, pieces, f_in = _comm_fwd_call(c_blk, w_ada[0], [halves(w_in[0])])
    mine = lax.dynamic_slice_in_dim(pieces, dev * ROWS_PER_DEVICE, n_seq, axis=1)
    mod_rows = jnp.transpose(mine, (1, 0, 2)).reshape(n_seq, 3 * D_MODEL)
    w_in_blocks = [f_in[k].reshape(D_MODEL, -1) for k in range(4)]

    gx, g_in_blocks, (r_uq, r_ukv, r_out), small_row, dmod = _local_step(
        x, positions, loss_target, mod_rows, b_ada, norm_gain, q_norm_gain, kv_norm_gain, swa_sinks, final_gain,
        w_in_blocks, [halves(w_uq[0]), halves(w_ukv[0]), halves(w_out[0])])

    grads = [jnp.stack(g_in_blocks).reshape(4, 2, D_MODEL // 2, -1)]
    part = jnp.concatenate([dmod, small_row, jnp.zeros((ROWS_PER_DEVICE - n_seq - 1, 3 * D_MODEL), F32)], axis=0)
    r_in, parts_all = _comm_bwd_call(grads, part)
    g_in_s, g_uq_s = r_in.reshape(w_in.shape[1:]), r_uq.reshape(w_uq.shape[1:])
    g_ukv_s, g_out_s = r_ukv.reshape(w_ukv.shape[1:]), r_out.reshape(w_out.shape[1:])

    tr = lambda a: jnp.swapaxes(a[0], 0, 1)
    back = lambda ts: [jnp.swapaxes(t, 0, 1) for t in ts]
    d_in, nm_in, nv_in = back(_adam_call("adam_w_in", tr(w_in), g_in_s.T, tr(m_w_in), tr(v_w_in)))
    d_uq, nm_uq, nv_uq = back(_adam_call("adam_w_uq", tr(w_uq), g_uq_s.T, tr(m_w_uq), tr(v_w_uq)))
    d_ukv, nm_ukv, nv_ukv = _adam_call("adam_w_ukv", w_ukv[0], g_ukv_s, m_w_ukv[0], v_w_ukv[0])
    d_out, nm_out, nv_out = _adam_call("adam_w_out", w_out[0], g_out_s, m_w_out[0], v_w_out[0])
    dmod_cols = lax.dynamic_slice_in_dim(parts_all, chip * 768, 768, axis=1)
    g_ada, d_ada, nm_ada, nv_ada = _ada_bwd_call(act_all, dmod_cols, w_ada[0], m_w_ada[0], v_w_ada[0])

    row = lambda t: t.reshape(1, -1)
    small = {"b_ada": (b_ada, m_b_ada, v_b_ada), "norm_gain": (norm_gain, m_norm_gain, v_norm_gain),
             "q_norm_gain": (q_norm_gain, m_q_norm_gain, v_q_norm_gain),
             "kv_norm_gain": (kv_norm_gain, m_kv_norm_gain, v_kv_norm_gain),
             "swa_sinks": (swa_sinks, m_swa_sinks, v_swa_sinks),
             "final_gain": (row(final_gain), row(m_final_gain), row(v_final_gain))}
    res, loss_row = _small_call(parts_all, n_seq, [small[name] for name in SMALL_ORDER])
    res = dict(zip(SMALL_ORDER, res))
    res["final_gain"] = [t.reshape(-1) for t in res["final_gain"]]
    e = lambda t: t[None]
    big = {"w_ada": (e(g_ada), e(d_ada), e(nm_ada), e(nv_ada)), "w_in": (e(g_in_s), e(d_in), e(nm_in), e(nv_in)),
           "w_uq": (e(g_uq_s), e(d_uq), e(nm_uq), e(nv_uq)), "w_ukv": (e(g_ukv_s), e(d_ukv), e(nm_ukv), e(nv_ukv)),
           "w_out": (e(g_out_s), e(d_out), e(nm_out), e(nv_out))}
    order = ("w_ada", "b_ada", "norm_gain", "w_in", "q_norm_gain", "kv_norm_gain", "w_uq", "w_ukv", "swa_sinks", "w_out",
             "final_gain")
    pick = lambda kind: [(big[n] if n in big else res[n])[kind] for n in order]
    return (loss_row[0, 0], gx, *pick(0), *pick(1), *pick(2), *pick(3))
```

```python
import jax
import jax.numpy as jnp
from jax import lax
from jax.experimental import pallas as pl
from jax.experimental.pallas import tpu as pltpu

F32 = jnp.float32
BF16 = jnp.bfloat16

D_MODEL = 1024
Q_LORA = 384
KV_LORA = 256
N_HEADS = 8
MLA_NOPE = 64
MLA_ROPE = 32
HEAD_LANES = 128
HALF = 64
SWA_WINDOW = 128
EPS = 1e-6
ROPE_THETA = 10000.0
MLA_SCALE = (MLA_NOPE + MLA_ROPE) ** -0.5
LOG2E = 1.4426950408889634
LN2 = 0.6931471805599453
SWA_SCALE = 64 ** -0.5
NEG = -1e30

ADAM_LR = 0.001
ADAM_B1 = 0.9
ADAM_B2 = 0.999
ADAM_EPS = 1e-08
ADAM_WD = 0.01
ADAM_STEP = 10

A_ZQ, A_ZKV, A_KR, A_GM, A_QS, A_KS, A_VS, A_GS, A_END = 0, 384, 640, 768, 1280, 1792, 1920, 2048, 2560
IN_SPLITS = (384, 256, 32, 512, 512, 128, 128, 512)
D_IN = sum(IN_SPLITS)

TOKEN_TILE = 512
ATT_TILE = 256
VMEM_LIMIT = 56 * 1024 * 1024


def _dot(a, b):
    return jnp.dot(a, b, preferred_element_type=F32)


def _dot_nt(a, b):
    return lax.dot_general(a, b, (((1,), (1,)), ((), ())), preferred_element_type=F32)


def _dot_tn(a, b):
    return lax.dot_general(a, b, (((0,), (0,)), ((), ())), preferred_element_type=F32)


def _params(n_grid):
    return pltpu.CompilerParams(dimension_semantics=("arbitrary",) * n_grid, vmem_limit_bytes=VMEM_LIMIT)


def _full(shape):
    nd = len(shape)
    return pl.BlockSpec(shape, lambda *_: (0,) * nd, pipeline_mode=pl.Buffered(1))


def _sigmoid(g):
    return 1.0 / (1.0 + jnp.exp(-g))


SUB_TILE = 256


def _sub_tiles(tm):
    sub = min(SUB_TILE, tm)
    return [slice(s * sub, (s + 1) * sub) for s in range(tm // sub)]


MESH = pl.DeviceIdType.MESH
ROWS_PER_DEVICE = 8
VMEM_SPEC = pl.BlockSpec(memory_space=pltpu.VMEM)
ANY_SPEC = pl.BlockSpec(memory_space=pl.ANY)


def _position():
    x, y, c = lax.axis_index("x"), lax.axis_index("y"), lax.axis_index("c")
    sibling = (x, y, 1 - c)
    others = [(1 - x, y, c), (x, 1 - y, c), (1 - x, 1 - y, c)]
    return (x, y, c), 4 * x + 2 * y + c, 2 * x + y, sibling, others


def _rows_of(dev):
    return pl.ds(pl.multiple_of(dev * ROWS_PER_DEVICE, ROWS_PER_DEVICE), ROWS_PER_DEVICE)


def _all_to_all_rows(block_ref, table_ref, dev, me, send_sems, recv_sems):
    x, y, c = me
    waits = []
    for k in range(1, 8):
        peer = (1 - x if k & 4 else x, 1 - y if k & 2 else y, 1 - c if k & 1 else c)
        pltpu.make_async_remote_copy(src_ref=block_ref, dst_ref=table_ref.at[_rows_of(dev)], send_sem=send_sems.at[k - 1],
                                     recv_sem=recv_sems.at[k - 1], device_id=peer, device_id_type=MESH).start()
        waits.append(pltpu.make_async_remote_copy(
            src_ref=block_ref, dst_ref=table_ref.at[_rows_of(jnp.bitwise_xor(dev, k))], send_sem=send_sems.at[k - 1],
            recv_sem=recv_sems.at[k - 1], device_id=peer, device_id_type=MESH))
    return waits


def _comm_fwd_call(c_blk, w_ada, shards):
    n = len(shards)

    def body(c_ref, wada_ref, *refs):
        w_refs, act_ref, pieces_ref, full_refs = refs[:n], refs[n], refs[n + 1], refs[n + 2:2 * n + 2]
        c_all_ref = refs[2 * n + 2]
        c_send, c_recv, p_send, p_recv, w_send, w_recv, f_send, f_recv, loc_sem = refs[2 * n + 3:]
        me, dev, chip, sibling, others = _position()
        core = me[2]
        chip_of = [2 * p[0] + p[1] for p in others]

        local = [pltpu.make_async_copy(w_refs[i], full_refs[i].at[chip], loc_sem.at[i]) for i in range(n)]
        for cp in local:
            cp.start()

        def over_ici(i, j, src_chip):
            return pltpu.make_async_remote_copy(
                src_ref=w_refs[i].at[core], dst_ref=full_refs[i].at[src_chip, core], send_sem=w_send.at[3 * i + j],
                recv_sem=w_recv.at[3 * i + j], device_id=others[j], device_id_type=MESH)

        def to_sibling(i, j, half):
            return pltpu.make_async_remote_copy(
                src_ref=full_refs[i].at[chip_of[j], half], dst_ref=full_refs[i].at[chip_of[j], half],
                send_sem=f_send.at[3 * i + j], recv_sem=f_recv.at[3 * i + j], device_id=sibling, device_id_type=MESH)

        c_all_ref[_rows_of(dev), :] = c_ref[...]
        c_waits = _all_to_all_rows(c_ref, c_all_ref, dev, me, c_send, c_recv)
        sent = [over_ici(i, j, chip) for i in range(n) for j in range(3)]
        for cp in sent:
            cp.start()

        for cp in c_waits:
            cp.wait()
        cv = c_all_ref[...]
        act = cv * _sigmoid(cv)
        act_ref[...] = act
        pieces_ref[chip] = _dot(act.astype(BF16), wada_ref[...].astype(BF16))
        piece = lambda j, src_chip: pltpu.make_async_remote_copy(
            src_ref=pieces_ref.at[chip], dst_ref=pieces_ref.at[src_chip], send_sem=p_send.at[j], recv_sem=p_recv.at[j],
            device_id=others[j], device_id_type=MESH)
        for j in range(3):
            piece(j, chip).start()

        for i in range(n):
            for j in range(3):
                over_ici(i, j, chip_of[j]).wait_recv()
                to_sibling(i, j, core).start()
        for j in range(3):
            piece(j, chip).wait_send()
            piece(j, chip_of[j]).wait_recv()
        for i in range(n):
            for j in range(3):
                to_sibling(i, j, 1 - core).wait_recv()
                to_sibling(i, j, core).wait_send()
        for cp in sent:
            cp.wait_send()
        for cp in local:
            cp.wait()

    rows = 8 * ROWS_PER_DEVICE
    dma = pltpu.SemaphoreType.DMA
    return pl.pallas_call(
        body, name="comm_fwd",
        out_shape=[jax.ShapeDtypeStruct((rows, D_MODEL), F32), jax.ShapeDtypeStruct((4, rows, w_ada.shape[1]), F32)]
        + [jax.ShapeDtypeStruct((4,) + s.shape, s.dtype) for s in shards],
        in_specs=[VMEM_SPEC, VMEM_SPEC] + [ANY_SPEC] * n,
        out_specs=[VMEM_SPEC, VMEM_SPEC] + [ANY_SPEC] * n,
        scratch_shapes=[pltpu.VMEM((rows, D_MODEL), F32), dma((7,)), dma((7,)), dma((3,)), dma((3,)),
                        dma((3 * n,)), dma((3 * n,)), dma((3 * n,)), dma((3 * n,)), dma((n,))],
        compiler_params=pltpu.CompilerParams(vmem_limit_bytes=VMEM_LIMIT),
    )(c_blk, w_ada, *shards)


def _comm_bwd_call(grads, part):
    n = len(grads)

    def body(part_ref, *refs):
        g_refs, f_refs, parts_ref = refs[:n], refs[n:2 * n], refs[2 * n]
        scratch = refs[2 * n + 1:]
        a_refs, b_refs, p_refs, r_refs = (scratch[k * n:(k + 1) * n] for k in range(4))
        s_send, s_recv, d_send, d_recv, e_send, e_recv, h_send, h_recv, loc_sem = scratch[4 * n:]
        me, dev, chip, sibling, others = _position()
        core = me[2]
        chip_of = [2 * p[0] + p[1] for p in others]

        parts_ref[_rows_of(dev), :] = part_ref[...]
        s_waits = _all_to_all_rows(part_ref, parts_ref, dev, me, s_send, s_recv)

        mine = [pltpu.make_async_copy(g_refs[i].at[:, core], a_refs[i], loc_sem.at[i]) for i in range(n)]
        swap = [pltpu.make_async_remote_copy(src_ref=g_refs[i].at[:, 1 - core], dst_ref=b_refs[i], send_sem=d_send.at[i],
                                             recv_sem=d_recv.at[i], device_id=sibling, device_id_type=MESH) for i in range(n)]
        order = sorted(range(n), key=lambda i: g_refs[i].shape[2] * g_refs[i].shape[3])
        for i in order:
            mine[i].start()
            swap[i].start()
        cross = [pltpu.make_async_remote_copy(src_ref=p_refs[i].at[chip_of[j]], dst_ref=r_refs[i].at[j],
                                              send_sem=e_send.at[3 * i + j], recv_sem=e_recv.at[3 * i + j],
                                              device_id=others[j], device_id_type=MESH) for i in range(n) for j in range(3)]
        for i in order:
            mine[i].wait()
            swap[i].wait()
            for k in range(4):
                s = a_refs[i][k] + b_refs[i][k]
                a_refs[i][k] = s
                p_refs[i][k] = s.astype(BF16)
            for j in range(3):
                cross[3 * i + j].start()
        share = {}
        for i in order:
            for j in range(3):
                cross[3 * i + j].wait()
            f_refs[i][core] = (a_refs[i][chip] + r_refs[i][0].astype(F32) + r_refs[i][1].astype(F32)
                               + r_refs[i][2].astype(F32))
            share[i] = pltpu.make_async_remote_copy(src_ref=f_refs[i].at[core], dst_ref=f_refs[i].at[core],
                                                    send_sem=h_send.at[i], recv_sem=h_recv.at[i], device_id=sibling,
                                                    device_id_type=MESH)
            share[i].start()
        for i in range(n):
            share[i].wait_send()
            pltpu.make_async_remote_copy(src_ref=f_refs[i].at[core], dst_ref=f_refs[i].at[1 - core], send_sem=h_send.at[i],
                                         recv_sem=h_recv.at[i], device_id=sibling, device_id_type=MESH).wait_recv()
        for cp in s_waits:
            cp.wait()

    rows = 8 * ROWS_PER_DEVICE
    dma = pltpu.SemaphoreType.DMA
    quarter = [(4,) + g.shape[2:] for g in grads]
    return pl.pallas_call(
        body, name="comm_bwd",
        out_shape=[jax.ShapeDtypeStruct((2,) + g.shape[2:], F32) for g in grads]
        + [jax.ShapeDtypeStruct((rows, part.shape[1]), F32)],
        in_specs=[VMEM_SPEC] + [ANY_SPEC] * n,
        out_specs=[VMEM_SPEC] * (n + 1),
        scratch_shapes=[pltpu.VMEM(q, F32) for q in quarter] + [pltpu.VMEM(q, F32) for q in quarter]
        + [pltpu.VMEM(q, BF16) for q in quarter] + [pltpu.VMEM((3,) + q[1:], BF16) for q in quarter]
        + [dma((7,)), dma((7,)), dma((n,)), dma((n,)), dma((3 * n,)), dma((3 * n,)), dma((n,)), dma((n,)), dma((n,))],
        compiler_params=pltpu.CompilerParams(vmem_limit_bytes=VMEM_LIMIT),
    )(part, *grads)


def _by_owner(g, n):
    return jnp.transpose(g.reshape(g.shape[0], 4, n), (1, 0, 2)).reshape(4, 2, g.shape[0] // 2, n)


def _reduce_operands(g):
    quarter = (4,) + g.shape[2:]
    dma = pltpu.SemaphoreType.DMA
    scratch = [pltpu.VMEM(quarter, F32), pltpu.VMEM(quarter, F32), pltpu.VMEM(quarter, BF16),
               pltpu.VMEM((3,) + quarter[1:], BF16), dma((5,)), dma((5,)), dma((2,))]
    return jax.ShapeDtypeStruct((2,) + g.shape[2:], F32), scratch


def _grad_reduce(step, n_steps, g_ref, f_ref, a_ref, b_ref, p_ref, r_ref, send, recv, loc_sem):
    me, _, chip, sibling, others = _position()
    core = me[2]
    chip_of = [2 * p[0] + p[1] for p in others]
    remote = lambda src, dst, k, to: pltpu.make_async_remote_copy(
        src_ref=src, dst_ref=dst, send_sem=send.at[k], recv_sem=recv.at[k], device_id=to, device_id_type=MESH)
    mine = pltpu.make_async_copy(g_ref.at[:, core], a_ref, loc_sem.at[0])
    swap = remote(g_ref.at[:, 1 - core], b_ref, 0, sibling)
    cross = [remote(p_ref.at[chip_of[j]], r_ref.at[j], 1 + j, others[j]) for j in range(3)]
    total_ref = b_ref.at[0]
    keep = pltpu.make_async_copy(total_ref, f_ref.at[core], loc_sem.at[1])
    share = lambda half: remote(total_ref, f_ref.at[half], 4, sibling)
    at = [k * (n_steps - 1) // 3 for k in range(4)]

    @pl.when(step == at[0])
    def _():
        mine.start()
        swap.start()

    @pl.when(step == at[1])
    def _():
        mine.wait()
        swap.wait()
        for k in range(4):
            s = a_ref[k] + b_ref[k]
            a_ref[k] = s
            p_ref[k] = s.astype(BF16)
        for cp in cross:
            cp.start()

    @pl.when(step == at[2])
    def _():
        for cp in cross:
            cp.wait()
        total_ref[...] = a_ref[chip] + r_ref[0].astype(F32) + r_ref[1].astype(F32) + r_ref[2].astype(F32)
        keep.start()
        share(core).start()

    @pl.when(step == at[3])
    def _():
        keep.wait()
        share(core).wait_send()
        share(1 - core).wait_recv()


def _twice(t):
    lo = _lane_lo()
    other = pltpu.roll(t, HALF, 1)
    return jnp.concatenate([jnp.where(lo, t, other), jnp.where(lo, other, t)], axis=1)


def _once(g):
    first, second = g[:, :HEAD_LANES], g[:, HEAD_LANES:]
    return jnp.where(_lane_lo(), first + pltpu.roll(first, HALF, 1), second + pltpu.roll(second, HALF, 1))


def _rope_tables(pos_ref, inv_row, rope_ref):
    quarter = pos_ref.shape[0] // 4
    lane = lax.broadcasted_iota(jnp.int32, (1, HEAD_LANES), 1)
    pos = [pos_ref[g * quarter:(g + 1) * quarter, :] for g in range(4)]
    ang = jnp.where(lane < 32, pos[0], jnp.where(lane < 64, pos[1], jnp.where(lane < 96, pos[2], pos[3]))) * inv_row
    cos, sin = jnp.cos(ang), jnp.sin(ang)
    rope_lanes = jnp.logical_and(lane >= HALF, lane < HALF + MLA_ROPE)
    for g in range(4):
        rows = slice(g * quarter, (g + 1) * quarter)
        shift = (HALF - 32 * g) % HEAD_LANES
        at = lambda t: t if shift == 0 else pltpu.roll(t, shift, 1)
        rope_ref[rows, :HEAD_LANES] = jnp.where(rope_lanes, at(cos), 1.0)
        rope_ref[rows, HEAD_LANES:] = jnp.where(rope_lanes, at(sin), 0.0)


def _gather_in_steps(step, n_steps, w_refs, full_refs, w_send, w_recv, f_send, f_recv, loc_sem):
    me, _, chip, sibling, others = _position()
    core = me[2]
    chip_of = [2 * p[0] + p[1] for p in others]
    n = len(w_refs)
    local = [pltpu.make_async_copy(w_refs[i], full_refs[i].at[chip], loc_sem.at[i]) for i in range(n)]

    def over_ici(i, j, src_chip):
        return pltpu.make_async_remote_copy(
            src_ref=w_refs[i].at[core], dst_ref=full_refs[i].at[src_chip, core], send_sem=w_send.at[3 * i + j],
            recv_sem=w_recv.at[3 * i + j], device_id=others[j], device_id_type=MESH)

    def to_sibling(i, j, half):
        return pltpu.make_async_remote_copy(
            src_ref=full_refs[i].at[chip_of[j], half], dst_ref=full_refs[i].at[chip_of[j], half],
            send_sem=f_send.at[3 * i + j], recv_sem=f_recv.at[3 * i + j], device_id=sibling, device_id_type=MESH)

    pairs = [(i, j) for i in range(n) for j in range(3)]

    @pl.when(step == 0)
    def _():
        for cp in local:
            cp.start()
        for i, j in pairs:
            over_ici(i, j, chip).start()

    @pl.when(step == 3 * n_steps // 4)
    def _():
        for i, j in pairs:
            over_ici(i, j, chip_of[j]).wait_recv()
            to_sibling(i, j, core).start()

    @pl.when(step == n_steps - 1)
    def _():
        for i, j in pairs:
            to_sibling(i, j, 1 - core).wait_recv()
            to_sibling(i, j, core).wait_send()
            over_ici(i, j, chip).wait_send()
        for cp in local:
            cp.wait()


def _pre_call(x, pos_col, mod, b_ada, ng, inv128, wa, shards, seq):
    n_tok = x.shape[0]
    tm = min(TOKEN_TILE, seq)
    per_seq = seq // tm
    n_steps = n_tok // tm
    n = len(shards)

    def body(x_ref, pos_ref, mod_ref, bada_ref, ng_ref, inv_ref, wa_ref, *refs):
        w_refs, refs = refs[:n], refs[n:]
        zqkv_ref, zkr_ref, gates_ref, qs_ref, kd_ref, vd_ref, rope_ref = refs[:7]
        full_refs, sems = refs[7:7 + n], refs[7 + n:]
        _gather_in_steps(pl.program_id(0), n_steps, w_refs, full_refs, *sems)
        _rope_tables(pos_ref, inv_ref[...], rope_ref)
        xv = x_ref[...]
        modv = mod_ref[0] + bada_ref[...]
        shift, scale = modv[:, :D_MODEL], modv[:, D_MODEL:2 * D_MODEL]
        r1 = lax.rsqrt(jnp.mean(xv * xv, axis=-1, keepdims=True) + EPS)
        h = ((xv * r1) * ng_ref[...]) * (1.0 + scale) + shift
        za = _dot(h.astype(BF16), wa_ref[...])
        zqkv_ref[...] = za[:, :A_KR]
        zkr_ref[...] = za[:, A_KR:A_GM]
        gates_ref[:, :512] = za[:, A_GM:A_QS]
        gates_ref[:, 512:] = za[:, A_GS:A_END]
        qs_ref[...] = (za[:, A_QS:A_KS] * (SWA_SCALE * LOG2E)).astype(BF16)
        kd_ref[...] = _twice(za[:, A_KS:A_VS]).astype(BF16)
        vd_ref[...] = _twice(za[:, A_VS:A_GS]).astype(BF16)

    tok = lambda w: pl.BlockSpec((tm, w), lambda i: (i, 0))
    outs = [(640, F32), (HEAD_LANES, F32), (1024, F32), (512, BF16), (256, BF16), (256, BF16), (2 * HEAD_LANES, F32)]
    dma = pltpu.SemaphoreType.DMA
    return pl.pallas_call(
        body, name="pre", grid=(n_steps,),
        out_shape=[jax.ShapeDtypeStruct((n_tok, w), dt) for w, dt in outs]
        + [jax.ShapeDtypeStruct((4,) + s.shape, s.dtype) for s in shards],
        in_specs=[tok(D_MODEL), tok(1), pl.BlockSpec((1, 1, 3 * D_MODEL), lambda i: (i // per_seq, 0, 0)),
                  _full(b_ada.shape), _full(ng.shape), _full(inv128.shape), _full(wa.shape)] + [ANY_SPEC] * n,
        out_specs=[tok(w) for w, _ in outs] + [ANY_SPEC] * n,
        scratch_shapes=[dma((3 * n,)), dma((3 * n,)), dma((3 * n,)), dma((3 * n,)), dma((n,))],
        compiler_params=_params(1),
    )(x, pos_col, mod, b_ada, ng, inv128, wa, *shards)


def _up_call(zqkv, zkr, rope, qg, kvg, wq2, wkv, seq):
    n_tok = zqkv.shape[0]
    tm = min(TOKEN_TILE, seq)

    n_steps = n_tok // tm
    ring = 3

    def body(zqkv_hbm, zkr_hbm, rope_hbm, qg_ref, kvg_ref, wq_ref, wkv_ref, qf_ref, kf_ref, v_ref,
             zqkv_buf, zkr_buf, rope_buf, sems):
        i = pl.program_id(0)

        def fetch(step):
            static = isinstance(step, int)
            slot = step % ring if static else lax.rem(step, ring)
            rows = pl.ds(step * tm if static else pl.multiple_of(step * tm, tm), tm)
            pairs = [(zqkv_hbm, zqkv_buf), (zkr_hbm, zkr_buf), (rope_hbm, rope_buf)]
            return [pltpu.make_async_copy(src.at[rows], dst.at[slot], sems.at[k, slot]) for k, (src, dst) in enumerate(pairs)]

        @pl.when(i == 0)
        def _():
            for first in range(min(ring - 1, n_steps)):
                for cp in fetch(first):
                    cp.start()

        @pl.when(i + (ring - 1) < n_steps)
        def _():
            for cp in fetch(i + (ring - 1)):
                cp.start()

        for cp in fetch(i):
            cp.wait()
        slot = lax.rem(i, ring)
        zqkv_ref, zkr_ref, rope_ref = zqkv_buf.at[slot], zkr_buf.at[slot], rope_buf.at[slot]
        cos, sin = rope_ref[:, :HEAD_LANES], rope_ref[:, HEAD_LANES:]
        zq, zkv = zqkv_ref[:, A_ZQ:A_ZKV], zqkv_ref[:, A_ZKV:A_KR]
        rq = lax.rsqrt(jnp.mean(zq * zq, axis=-1, keepdims=True) + EPS)
        qn = ((zq * rq) * qg_ref[...]).astype(BF16)
        qr = _dot(qn, wq_ref[...])
        cf, sf = jnp.tile(cos, (1, N_HEADS)), jnp.tile(sin, (1, N_HEADS))
        qf_ref[...] = ((qr[:, :1024] * cf + qr[:, 1024:] * sf) * (MLA_SCALE * LOG2E)).astype(BF16)
        rkv = lax.rsqrt(jnp.mean(zkv * zkv, axis=-1, keepdims=True) + EPS)
        kvn = ((zkv * rkv) * kvg_ref[...]).astype(BF16)
        kv = _dot(kvn, wkv_ref[...])
        zkr = zkr_ref[...]
        kpe = jnp.where(_lane_lo(), 0.0, zkr * cos) + pltpu.roll(zkr, HALF, 1) * sin
        kf_ref[...] = (kv[:, :1024] + jnp.tile(kpe, (1, N_HEADS))).astype(BF16)
        v_ref[...] = kv[:, 1024:].astype(BF16)

    tok = lambda w: pl.BlockSpec((tm, w), lambda i: (i, 0))
    outs = [(1024, BF16), (1024, BF16), (512, BF16)]
    return pl.pallas_call(
        body, name="up", grid=(n_steps,),
        out_shape=[jax.ShapeDtypeStruct((n_tok, w), dt) for w, dt in outs],
        in_specs=[ANY_SPEC, ANY_SPEC, ANY_SPEC, _full(qg.shape), _full(kvg.shape), _full(wq2.shape), _full(wkv.shape)],
        out_specs=[tok(w) for w, _ in outs],
        scratch_shapes=[pltpu.VMEM((ring, tm, 640), F32), pltpu.VMEM((ring, tm, HEAD_LANES), F32),
                        pltpu.VMEM((ring, tm, 2 * HEAD_LANES), F32), pltpu.SemaphoreType.DMA((3, ring))],
        compiler_params=_params(1),
    )(zqkv, zkr, rope, qg, kvg, wq2, wkv)


def _lane_lo(width=HEAD_LANES):
    return lax.broadcasted_iota(jnp.int32, (1, width), 1) < HALF


def _eye(n=HEAD_LANES):
    r = lax.broadcasted_iota(jnp.int32, (n, n), 0)
    c = lax.broadcasted_iota(jnp.int32, (n, n), 1)
    return jnp.where(r == c, 1.0, 0.0).astype(BF16)


def _mla_fwd_call(qf, kf, v, n_seq, seq):
    tq = min(ATT_TILE, seq)
    nq = seq // tq

    ext = HALF + 16

    def body(q_ref, k_ref, v_ref, o_ref, lse_ref, vt_ref, acc_ref):
        i = pl.program_id(1)
        eye = _eye()

        @pl.when(i == 0)
        def _():
            for h in range(N_HEADS):
                vt_ref[h * ext + HALF:(h + 1) * ext, :] = jnp.ones((16, seq), BF16)
            for t in range(nq):
                for p in range(N_HEADS // 2):
                    pair = slice(p * HEAD_LANES, (p + 1) * HEAD_LANES)
                    v_t = _dot_nt(eye, v_ref[t * tq:(t + 1) * tq, pair]).astype(BF16)
                    for hh in range(2):
                        r0 = (2 * p + hh) * ext
                        vt_ref[r0:r0 + HALF, t * tq:(t + 1) * tq] = v_t[hh * HALF:(hh + 1) * HALF, :]

        q = q_ref[...]
        qcol = i * tq + lax.broadcasted_iota(jnp.int32, (1, tq), 1)
        heads = range(N_HEADS)
        lanes = [slice(h * HEAD_LANES, (h + 1) * HEAD_LANES) for h in heads]

        def make_step(masked, n_tiles):
            def step(kt0, carry):
                tiles = range(n_tiles)
                start = pl.multiple_of(kt0 * tq, tq)
                ks = [k_ref[pl.ds(pl.multiple_of((kt0 + t) * tq, tq), tq), :] for t in tiles]
                vt = vt_ref[:, pl.ds(start, n_tiles * tq)]
                last = n_tiles - 1
                if masked:
                    keep = ((kt0 + last) * tq + lax.broadcasted_iota(jnp.int32, (tq, 1), 0)) <= qcol

                def scores(h):
                    sts = [_dot_nt(ks[t][:, lanes[h]], q[:, lanes[h]]) for t in tiles]
                    if masked:
                        sts[last] = jnp.where(keep, sts[last], NEG)
                    return sts

                def softmax(h, sts):
                    m_old = carry[h]
                    m_new = m_old
                    for st in sts:
                        m_new = jnp.maximum(m_new, jnp.max(st, axis=0, keepdims=True))
                    pt = jnp.concatenate([jnp.exp2(st - m_new).astype(BF16) for st in sts], axis=0)
                    return m_new, jnp.exp2(m_old - m_new), pt

                def values(h, alpha, pt):
                    rows = slice(h * ext, (h + 1) * ext)
                    acc_ref[rows, :] = acc_ref[rows, :] * alpha + _dot(vt[rows, :], pt)

                sts, soft, out = {0: scores(0), 1: scores(1)}, {}, {}
                for h in range(N_HEADS + 1):
                    if h + 2 < N_HEADS:
                        sts[h + 2] = scores(h + 2)
                    if h < N_HEADS:
                        soft[h] = softmax(h, sts.pop(h))
                    if h >= 1:
                        m_new, alpha, pt = soft.pop(h - 1)
                        values(h - 1, alpha, pt)
                        out[h - 1] = m_new
                return tuple(out[h] for h in heads)
            return step

        acc_ref[...] = jnp.zeros_like(acc_ref)
        init = (jnp.full((1, tq), NEG, F32),) * N_HEADS
        count = i + 1
        carry = lax.fori_loop(0, (count + 1) // 2 - 1, lambda j, c: make_step(False, 2)(2 * j, c), init)
        carry = lax.cond(count % 2 == 0, lambda c: make_step(True, 2)(i - 1, c), lambda c: make_step(True, 1)(i, c), carry)
        dens = [acc_ref[h * ext + HALF:h * ext + HALF + 1, :] for h in heads]
        acc_t = jnp.concatenate([acc_ref[h * ext:h * ext + HALF, :] * (1.0 / dens[h]) for h in heads], axis=0)
        o_ref[...] = acc_t.T
        for h in heads:
            lse_ref[0, h // 4, h % 4:h % 4 + 1, :] = carry[h] + jnp.log2(dens[h])

    n_tok = qf.shape[0]
    return pl.pallas_call(
        body, name="mla_fwd", grid=(n_seq, nq),
        out_shape=[jax.ShapeDtypeStruct((n_tok, 512), F32), jax.ShapeDtypeStruct((n_seq, 2, 4, seq), F32)],
        in_specs=[pl.BlockSpec((tq, 1024), lambda b, i: (b * nq + i, 0)),
                  pl.BlockSpec((seq, 1024), lambda b, i: (b, 0)),
                  pl.BlockSpec((seq, 512), lambda b, i: (b, 0))],
        out_specs=[pl.BlockSpec((tq, 512), lambda b, i: (b * nq + i, 0)),
                   pl.BlockSpec((1, 2, 4, tq), lambda b, i: (b, 0, 0, i))],
        scratch_shapes=[pltpu.VMEM((N_HEADS * ext, seq), BF16), pltpu.VMEM((N_HEADS * ext, tq), F32)],
        compiler_params=_params(2),
    )(qf, kf, v)


def _mla_bwd_call(qf, kf, v, do, delta, lse, n_seq, seq):
    tq = min(ATT_TILE, seq)
    nq = seq // tq

    nh = 4
    heads = range(nh)
    lanes = [slice(h * HEAD_LANES, (h + 1) * HEAD_LANES) for h in heads]

    def body(q_ref, k_ref, v_ref, do_ref, dl_ref, lse_ref, dq_ref, dk_ref, dv_ref,
             kt_ref, dot_ref, dqt_ref, dvt_ref):
        eye = _eye()
        sub_lo = lax.broadcasted_iota(jnp.int32, (HEAD_LANES, 1), 0) < HALF

        for t in range(nq):
            r = slice(t * tq, (t + 1) * tq)
            kv = k_ref[r, :]
            for h in heads:
                kt_ref[lanes[h], r] = _dot_nt(eye, kv[:, lanes[h]]).astype(BF16)
            for p in range(nh // 2):
                dov = do_ref[r, lanes[p]]
                dt = _dot_nt(eye, dov)
                dot_ref[2 * p, :, r] = jnp.where(sub_lo, dt, 0.0).astype(BF16)
                dot_ref[2 * p + 1, :, r] = jnp.where(sub_lo, 0.0, dt).astype(BF16)
        dqt_ref[...] = jnp.zeros_like(dqt_ref)
        dvt_ref[...] = jnp.zeros_like(dvt_ref)

        def flush_dv(tile, which):
            rows = pl.ds(pl.multiple_of(tile * tq, tq), tq)
            for p in range(nh // 2):
                dv_ref[rows, lanes[p]] = dvt_ref[which, p * HEAD_LANES:(p + 1) * HEAD_LANES, :].T

        def k_step(kt, _):
            slot = kt % 2
            kr = pl.ds(pl.multiple_of(kt * tq, tq), tq)
            k = k_ref[kr, :]
            vv = v_ref[kr, :]
            k_t = kt_ref[:, kr]
            krow = kt * tq + lax.broadcasted_iota(jnp.int32, (tq, 1), 0)

            def make_step(masked, n_tiles):
                def q_step(qt0, carry):
                    tiles = range(n_tiles)
                    qrs = [pl.ds(pl.multiple_of((qt0 + t) * tq, tq), tq) for t in tiles]
                    if masked:
                        flush_dv(jnp.maximum(kt - 1, 0), 1 - slot)
                    qs = [q_ref[qr, :] for qr in qrs]
                    if masked:
                        keep = krow <= (qt0 * tq + lax.broadcasted_iota(jnp.int32, (1, tq), 1))

                    def scores(h):
                        do_ts = [dot_ref[h, :, qr] for qr in qrs]
                        sts = [_dot_nt(k[:, lanes[h]], qs[t][:, lanes[h]]) for t in tiles]
                        dpts = [_dot(vv[:, lanes[h // 2]], do_ts[t]) for t in tiles]
                        return do_ts, sts, dpts

                    def softmax(h, sts, dpts):
                        pts, dsts = [], []
                        for t in tiles:
                            pt = jnp.exp2(sts[t] - lse_ref[0, 0, h:h + 1, qrs[t]])
                            if masked and t == 0:
                                pt = jnp.where(keep, pt, 0.0)
                            dsts.append((pt * (dpts[t] - dl_ref[0, h:h + 1, qrs[t]])).astype(BF16))
                            pts.append(pt.astype(BF16))
                        return pts, dsts

                    def grads(h, do_ts, pts, dsts):
                        half = slice((h % 2) * HALF, (h % 2 + 1) * HALF)
                        dst_all = jnp.concatenate(dsts, axis=1)
                        pt_all = jnp.concatenate(pts, axis=1)
                        do_all = jnp.concatenate([do_ts[t][half, :] for t in tiles], axis=1)
                        q_all = jnp.concatenate([qs[t][:, lanes[h]] for t in tiles], axis=0)
                        dvt_ref[slot, h * HALF:(h + 1) * HALF, :] += _dot_nt(do_all, pt_all)
                        dk_ref[kr, lanes[h]] += _dot(dst_all, q_all)
                        for t in tiles:
                            dqt_ref[lanes[h], qrs[t]] += _dot(k_t[lanes[h], :], dsts[t])

                    first, second = {0: scores(0)}, {}
                    for h in range(nh + 1):
                        if h + 1 < nh:
                            first[h + 1] = scores(h + 1)
                        if h < nh:
                            do_ts, sts, dpts = first.pop(h)
                            second[h] = (do_ts,) + softmax(h, sts, dpts)
                        if h >= 1:
                            grads(h - 1, *second.pop(h - 1))
                    return carry
                return q_step

            dk_ref[kr, :] = jnp.zeros((tq, nh * HEAD_LANES), F32)
            dvt_ref[slot] = jnp.zeros(dvt_ref.shape[1:], F32)
            count = nq - kt
            lax.cond(count >= 2, lambda c: make_step(True, 2)(kt, c), lambda c: make_step(True, 1)(kt, c), 0)
            lax.fori_loop(1, count // 2, lambda j, c: make_step(False, 2)(kt + 2 * j, c), 0)
            lax.cond(jnp.logical_and(count % 2 == 1, count >= 3), lambda c: make_step(False, 1)(nq - 1, c), lambda c: c, 0)
            return 0

        lax.fori_loop(0, nq, k_step, 0)
        flush_dv(nq - 1, (nq - 1) % 2)
        for t in range(nq):
            r = slice(t * tq, (t + 1) * tq)
            for h in heads:
                dq_ref[r, lanes[h]] = dqt_ref[lanes[h], r].T

    n_tok = qf.shape[0]
    groups = N_HEADS // nh
    blk = lambda w: pl.BlockSpec((seq, w), lambda b, g: (b, g))
    return pl.pallas_call(
        body, name="mla_bwd", grid=(n_seq, groups),
        out_shape=[jax.ShapeDtypeStruct((n_tok, 1024), F32), jax.ShapeDtypeStruct((n_tok, 1024), F32),
                   jax.ShapeDtypeStruct((n_tok, 512), F32)],
        in_specs=[blk(512), blk(512), blk(256), blk(256), pl.BlockSpec((1, nh, seq), lambda b, g: (g, 0, b)),
                  pl.BlockSpec((1, 1, nh, seq), lambda b, g: (b, g, 0, 0))],
        out_specs=[blk(512), blk(512), blk(256)],
        scratch_shapes=[pltpu.VMEM((nh * HEAD_LANES, seq), BF16), pltpu.VMEM((nh, HEAD_LANES, seq), BF16),
                        pltpu.VMEM((nh * HEAD_LANES, seq), F32), pltpu.VMEM((2, nh * HALF, tq), F32)],
        compiler_params=_params(2),
    )(qf, kf, v, do, delta, lse)


SWA_BLOCKS = 4


def _swa_block(n, pos_col_ref, posq):
    w = SWA_WINDOW
    start = pl.multiple_of(jnp.maximum(n - 1, 0) * w, w)
    posk = pos_col_ref[pl.ds(start, 2 * w), :]
    rel = (n * w + lax.broadcasted_iota(jnp.int32, (1, w), 1)) - (start + lax.broadcasted_iota(jnp.int32, (2 * w, 1), 0))
    valid = jnp.logical_and(rel >= 0, rel < w)
    return start, jnp.where(valid, posq - posk, 1e30)


def _alibi(h):
    return LOG2E * 2.0 ** -(h + 1)


def _transpose_rows(eye, src_ref, dst_ref, seq, width):
    step = 2 * SWA_WINDOW
    for t in range(seq // step):
        for p in range(width // HEAD_LANES):
            lanes = slice(p * HEAD_LANES, (p + 1) * HEAD_LANES)
            dst_ref[lanes, t * step:(t + 1) * step] = _dot_nt(eye, src_ref[t * step:(t + 1) * step, lanes]).astype(BF16)


def _swa_fwd_call(qs, kd, vd, pos_col, pos_row, sinks, n_seq, seq):
    w = SWA_WINDOW
    qb = SWA_BLOCKS
    steps = seq // (qb * w)
    ext = HALF + 16

    def body(q_ref, k_ref, v_ref, pc_ref, pr_ref, sink_ref, o_ref, lse_ref, vt_ref):
        n = pl.program_id(1)
        lo = _lane_lo()
        hi = jnp.logical_not(lo)
        eye = _eye()

        @pl.when(n == 0)
        def _():
            step = 2 * w
            for kv in range(2):
                vt_ref[kv * ext + HALF:(kv + 1) * ext, :] = jnp.ones((16, seq), BF16)
                for t in range(seq // step):
                    v_t = _dot_nt(eye, v_ref[t * step:(t + 1) * step, kv * HEAD_LANES:(kv + 1) * HEAD_LANES])
                    vt_ref[kv * ext:kv * ext + HALF, t * step:(t + 1) * step] = v_t[:HALF, :].astype(BF16)

        heads = range(N_HEADS)
        blocks = range(qb)
        geo = [_swa_block(n * qb + bi, pc_ref, pr_ref[bi]) for bi in blocks]
        wins = [pl.ds(g[0], 2 * w) for g in geo]
        kwins = [k_ref[win, :] for win in wins]
        vts = [vt_ref[:, win] for win in wins]
        sts = []
        for bi in blocks:
            q = q_ref[bi * w:(bi + 1) * w, :]
            sts.append([])
            for j in range(N_HEADS // 2):
                qp = q[:, j * HEAD_LANES:(j + 1) * HEAD_LANES]
                both = jnp.concatenate([jnp.where(lo, qp, jnp.zeros_like(qp)), jnp.where(hi, qp, jnp.zeros_like(qp))], axis=0)
                st = _dot_nt(kwins[bi][:, (j // 2) * HEAD_LANES:(j // 2 + 1) * HEAD_LANES], both)
                sts[bi] += [st[:, :w], st[:, w:]]
        ps, ms = [], []
        for bi in blocks:
            ps.append([])
            ms.append([])
            for h in heads:
                s = sts[bi][h] - _alibi(h) * geo[bi][1]
                m = jnp.maximum(jnp.max(s, axis=0, keepdims=True), sink_ref[0, h] * LOG2E)
                ps[bi].append(jnp.exp2(s - m).astype(BF16))
                ms[bi].append(m)
        for bi in blocks:
            ots = []
            for h in heads:
                pv = _dot(vts[bi][(h // 4) * ext:(h // 4 + 1) * ext, :], ps[bi][h])
                l = pv[HALF:HALF + 1, :] + jnp.exp2(sink_ref[0, h] * LOG2E - ms[bi][h])
                ots.append(pv[:HALF, :] * (1.0 / l))
                lse_ref[0, h:h + 1, bi * w:(bi + 1) * w] = ms[bi][h] + jnp.log2(l)
            o_ref[bi * w:(bi + 1) * w, :] = jnp.concatenate(ots, axis=0).T

    n_tok = qs.shape[0]
    tok = lambda width: pl.BlockSpec((qb * w, width), lambda b, n: (b * steps + n, 0))
    whole = lambda width: pl.BlockSpec((seq, width), lambda b, n: (b, 0))
    return pl.pallas_call(
        body, name="swa_fwd", grid=(n_seq, steps),
        out_shape=[jax.ShapeDtypeStruct((n_tok, 512), F32), jax.ShapeDtypeStruct((n_seq, N_HEADS, seq), F32)],
        in_specs=[tok(512), whole(256), whole(256), whole(1), pl.BlockSpec((qb, 1, w), lambda b, n: (b * steps + n, 0, 0)),
                  pl.BlockSpec(memory_space=pltpu.SMEM)],
        out_specs=[tok(512), pl.BlockSpec((1, N_HEADS, qb * w), lambda b, n: (b, 0, n))],
        scratch_shapes=[pltpu.VMEM((2 * ext, seq), BF16)],
        compiler_params=_params(2),
    )(qs, kd, vd, pos_col, pos_row, sinks)


def _swa_bwd_call(qs, kd, vd, do, delta, lse, pos_col, pos_row, sinks, g_out, n_seq, seq):
    w = SWA_WINDOW
    qb = SWA_BLOCKS
    steps = seq // (qb * w)
    reduced, reduce_scratch = _reduce_operands(g_out)

    def body(q_ref, k_ref, v_ref, do_ref, dl_ref, lse_ref, pc_ref, pr_ref, sink_ref, g_ref, dq_ref, dk_ref, dv_ref,
             dsink_ref, f_ref, kt_ref, *reduce_refs):
        b, n = pl.program_id(0), pl.program_id(1)
        _grad_reduce(b * steps + n, n_seq * steps, g_ref, f_ref, *reduce_refs)
        lo = _lane_lo()
        hi = jnp.logical_not(lo)
        sub_lo = lax.broadcasted_iota(jnp.int32, (HEAD_LANES, 1), 0) < HALF
        eye = _eye()

        @pl.when(n == 0)
        def _():
            dk_ref[...] = jnp.zeros_like(dk_ref)
            dv_ref[...] = jnp.zeros_like(dv_ref)
            _transpose_rows(eye, k_ref, kt_ref, seq, 2 * HEAD_LANES)

        @pl.when(jnp.logical_and(n == 0, b == 0))
        def _():
            dsink_ref[...] = jnp.zeros_like(dsink_ref)

        heads = range(N_HEADS)
        blocks = range(qb)
        kv_lanes = lambda h: slice((h // 4) * HEAD_LANES, (h // 4 + 1) * HEAD_LANES)
        geo = [_swa_block(n * qb + bi, pc_ref, pr_ref[bi]) for bi in blocks]
        wins = [pl.ds(g[0], 2 * w) for g in geo]
        kwins = [k_ref[win, :] for win in wins]
        vwins = [v_ref[win, :] for win in wins]

        do_ts, deltas, qms, doms = [], [], [], []
        for bi in blocks:
            rows = slice(bi * w, (bi + 1) * w)
            for lst in (do_ts, deltas, qms, doms):
                lst.append([])
            for j in range(N_HEADS // 2):
                pair = slice(j * HEAD_LANES, (j + 1) * HEAD_LANES)
                dop = do_ref[rows, pair]
                qp = q_ref[rows, pair]
                dt = _dot_nt(eye, dop)
                for hh in range(2):
                    half = lo if hh == 0 else hi
                    do_ts[bi].append(jnp.where(sub_lo, dt, 0.0).astype(BF16) if hh == 0
                                     else jnp.where(sub_lo, 0.0, dt).astype(BF16))
                    deltas[bi].append(dl_ref[2 * j + hh:2 * j + hh + 1, rows])
                    qms[bi].append(jnp.where(half, qp, jnp.zeros_like(qp)))
                    doms[bi].append(jnp.where(half, dop, jnp.zeros_like(dop)))
        sts, dpts = [], []
        for bi in blocks:
            sts.append([])
            dpts.append([])
            for j in range(N_HEADS // 2):
                a, b = 2 * j, 2 * j + 1
                st = _dot_nt(kwins[bi][:, kv_lanes(a)], jnp.concatenate([qms[bi][a], qms[bi][b]], axis=0))
                dpt = _dot(vwins[bi][:, kv_lanes(a)], jnp.concatenate([do_ts[bi][a], do_ts[bi][b]], axis=1))
                sts[bi] += [st[:, :w], st[:, w:]]
                dpts[bi] += [dpt[:, :w], dpt[:, w:]]
        pts, dsts = [], []
        for bi in blocks:
            pts.append([])
            dsts.append([])
            for h in heads:
                lse_h = lse_ref[0, h:h + 1, bi * w:(bi + 1) * w]
                pt = jnp.exp2(sts[bi][h] - _alibi(h) * geo[bi][1] - lse_h)
                dsts[bi].append((pt * (dpts[bi][h] - deltas[bi][h])).astype(BF16))
                pts[bi].append(pt.astype(BF16))
                dsink_ref[h:h + 1, :] += -jnp.exp2(sink_ref[0, h] * LOG2E - lse_h) * deltas[bi][h]
        for bi in blocks:
            for kv in range(2):
                group = range(4 * kv, 4 * kv + 4)
                dst_all = jnp.concatenate([dsts[bi][h] for h in group], axis=1)
                pt_all = jnp.concatenate([pts[bi][h] for h in group], axis=1)
                q_all = jnp.concatenate([qms[bi][h] for h in group], axis=0)
                do_all = jnp.concatenate([doms[bi][h] for h in group], axis=0)
                dk_ref[wins[bi], kv_lanes(4 * kv)] += _dot(dst_all, q_all)
                dv_ref[wins[bi], kv_lanes(4 * kv)] += _dot(pt_all, do_all)
        for bi in blocks:
            ktw = kt_ref[:, wins[bi]]
            for j in range(N_HEADS // 2):
                k_t = ktw[kv_lanes(2 * j), :]
                both = _dot(k_t, jnp.concatenate([dsts[bi][2 * j], dsts[bi][2 * j + 1]], axis=1))
                dq_t = jnp.where(sub_lo, both[:, :w], both[:, w:])
                dq_ref[bi * w:(bi + 1) * w, j * HEAD_LANES:(j + 1) * HEAD_LANES] = dq_t.T * SWA_SCALE

    n_tok = qs.shape[0]
    tok = lambda width: pl.BlockSpec((qb * w, width), lambda b, n: (b * steps + n, 0))
    whole = lambda width: pl.BlockSpec((seq, width), lambda b, n: (b, 0))
    return pl.pallas_call(
        body, name="swa_bwd", grid=(n_seq, steps),
        out_shape=[jax.ShapeDtypeStruct((n_tok, 512), F32), jax.ShapeDtypeStruct((n_tok, 256), F32),
                   jax.ShapeDtypeStruct((n_tok, 256), F32), jax.ShapeDtypeStruct((N_HEADS, HEAD_LANES), F32), reduced],
        in_specs=[tok(512), whole(256), whole(256), pl.BlockSpec((qb * w, 512), lambda b, n: (b * steps + n, 1)),
                  pl.BlockSpec((N_HEADS, qb * w), lambda b, n: (0, b * steps + n)),
                  pl.BlockSpec((1, N_HEADS, qb * w), lambda b, n: (b, 0, n)),
                  whole(1), pl.BlockSpec((qb, 1, w), lambda b, n: (b * steps + n, 0, 0)),
                  pl.BlockSpec(memory_space=pltpu.SMEM), ANY_SPEC],
        out_specs=[tok(512), whole(256), whole(256), _full((N_HEADS, HEAD_LANES)), ANY_SPEC],
        scratch_shapes=[pltpu.VMEM((2 * HEAD_LANES, seq), BF16)] + reduce_scratch,
        compiler_params=_params(2),
    )(qs, kd, vd, do, delta, lse, pos_col, pos_row, sinks, g_out)


def _post_call(x, target, o_mla, o_swa, gates, mod, b_ada, fg, w_out, seq):
    n_tok = x.shape[0]
    tm = min(TOKEN_TILE, seq)
    per_seq = seq // tm
    n_seq = n_tok // seq

    def body(x_ref, t_ref, om_ref, os_ref, g_ref, mod_ref, bada_ref, fg_ref, w_ref,
             dx2_ref, do_ref, dg_ref, gw_ref, gfg_ref, dgate_ref, loss_ref, dmla_ref, dswa_ref):
        i = pl.program_id(0)

        @pl.when(i == 0)
        def _():
            gw_ref[...] = jnp.zeros_like(gw_ref)
            gfg_ref[...] = jnp.zeros_like(gfg_ref)
            loss_ref[...] = jnp.zeros_like(loss_ref)

        @pl.when(i % per_seq == 0)
        def _():
            dgate_ref[...] = jnp.zeros_like(dgate_ref)

        gate = mod_ref[0][:, 2 * D_MODEL:] + bada_ref[:, 2 * D_MODEL:]
        fgv = fg_ref[...]
        fgd = fgv * (1.0 / D_MODEL)
        subs = _sub_tiles(tm)
        gs = [g_ref[r, :] for r in subs]
        os_ = [jnp.concatenate([om_ref[r, :], os_ref[r, :]], axis=-1) for r in subs]
        sgs = [_sigmoid(g) for g in gs]
        sils = [g * sg for g, sg in zip(gs, sgs)]
        ypres = [(o * sil).astype(BF16) for o, sil in zip(os_, sils)]
        ys = [_dot(ypre, w_ref[...]) for ypre in ypres]
        dys, loss, gfg, dgate = [], 0.0, 0.0, 0.0
        for r, y in zip(subs, ys):
            x2 = x_ref[r, :] + gate * y
            r2 = lax.rsqrt(jnp.mean(x2 * x2, axis=-1, keepdims=True) + EPS)
            xn2 = x2 * r2
            err = xn2 * fgv - t_ref[r, :]
            loss = loss + jnp.sum(jnp.sum(err * err, axis=-1, keepdims=True), axis=0, keepdims=True)
            gfg = gfg + jnp.sum(err * xn2, axis=0, keepdims=True)
            dxn2 = err * fgd
            dx2 = r2 * (dxn2 - xn2 * jnp.mean(dxn2 * xn2, axis=-1, keepdims=True))
            dx2_ref[r, :] = dx2
            dgate = dgate + jnp.sum(dx2 * y, axis=0, keepdims=True)
            dys.append((dx2 * gate).astype(BF16))
        loss_ref[...] += jnp.broadcast_to(loss * (0.5 / D_MODEL), loss_ref.shape)
        gfg_ref[...] += gfg * (1.0 / D_MODEL)
        dgate_ref[0] += dgate
        gw_ref[...] += _dot_tn(jnp.concatenate(ypres, axis=0), jnp.concatenate(dys, axis=0))
        dypres = [_dot_nt(dy, w_ref[...]) for dy in dys]
        pick = jnp.where(jnp.right_shift(lax.broadcasted_iota(jnp.int32, (2 * N_HEADS, D_MODEL), 1), 6)
                         == lax.broadcasted_iota(jnp.int32, (2 * N_HEADS, D_MODEL), 0), 1.0, 0.0).astype(BF16)
        for r, dypre, o, g, sg, sil in zip(subs, dypres, os_, gs, sgs, sils):
            dov = (dypre * sil).astype(BF16)
            do_ref[r, :] = dov
            delta = _dot_nt(pick, (dov.astype(F32) * o).astype(BF16))
            for grp in range(2):
                dmla_ref[grp, :, r] = delta[4 * grp:4 * grp + 4, :]
            dswa_ref[:, r] = delta[N_HEADS:, :]
            dg_ref[r, :] = (dypre * o * (sg + sil * (1.0 - sg))).astype(BF16)

    tok = lambda w: pl.BlockSpec((tm, w), lambda i: (i, 0))
    per_b = pl.BlockSpec((1, 1, 3 * D_MODEL), lambda i: (i // per_seq, 0, 0))
    return pl.pallas_call(
        body, name="post", grid=(n_tok // tm,),
        out_shape=[jax.ShapeDtypeStruct((n_tok, D_MODEL), F32), jax.ShapeDtypeStruct((n_tok, D_MODEL), BF16),
                   jax.ShapeDtypeStruct((n_tok, D_MODEL), BF16), jax.ShapeDtypeStruct((D_MODEL, D_MODEL), F32),
                   jax.ShapeDtypeStruct((1, D_MODEL), F32), jax.ShapeDtypeStruct((n_seq, 1, D_MODEL), F32),
                   jax.ShapeDtypeStruct((1, HEAD_LANES), F32),
                   jax.ShapeDtypeStruct((2, N_HEADS // 2, n_tok), F32), jax.ShapeDtypeStruct((N_HEADS, n_tok), F32)],
        in_specs=[tok(D_MODEL), tok(D_MODEL), tok(512), tok(512), tok(D_MODEL), per_b, _full(b_ada.shape),
                  _full(fg.shape), _full(w_out.shape)],
        out_specs=[tok(D_MODEL), tok(D_MODEL), tok(D_MODEL), _full((D_MODEL, D_MODEL)), _full((1, D_MODEL)),
                   pl.BlockSpec((1, 1, D_MODEL), lambda i: (i // per_seq, 0, 0)), _full((1, HEAD_LANES)),
                   pl.BlockSpec((2, N_HEADS // 2, tm), lambda i: (0, 0, i)), pl.BlockSpec((N_HEADS, tm), lambda i: (0, i))],
        compiler_params=_params(1),
    )(x, target, o_mla, o_swa, gates, mod, b_ada, fg, w_out)


def _mid_bwd_call(dqf, dkf, dv, zqkv, rope, qg, kvg, wq2, wkv, seq):
    n_tok = dqf.shape[0]
    tm = min(TOKEN_TILE, seq)

    def body(dq_ref, dk_ref, dv_ref, z_ref, rope_ref, qg_ref, kvg_ref, wq_ref, wkv_ref,
             dz_ref, gwq_ref, gwkv_ref, gqg_ref, gkvg_ref):
        i = pl.program_id(0)

        @pl.when(i == 0)
        def _():
            gwq_ref[...] = jnp.zeros_like(gwq_ref)
            gwkv_ref[...] = jnp.zeros_like(gwkv_ref)
            gqg_ref[...] = jnp.zeros_like(gqg_ref)
            gkvg_ref[...] = jnp.zeros_like(gkvg_ref)

        cos, sin = rope_ref[:, :HEAD_LANES], rope_ref[:, HEAD_LANES:]
        cf, sf = jnp.tile(cos, (1, N_HEADS)), jnp.tile(sin, (1, N_HEADS))
        dq = dq_ref[...] * MLA_SCALE
        dqr = jnp.concatenate([dq * cf, dq * sf], axis=-1).astype(BF16)
        zq, zkv = z_ref[:, :Q_LORA], z_ref[:, Q_LORA:]
        qgv, kvgv = qg_ref[...], kvg_ref[...]

        rq = lax.rsqrt(jnp.mean(zq * zq, axis=-1, keepdims=True) + EPS)
        xq = zq * rq
        gwq_ref[...] += _dot_tn((xq * qgv).astype(BF16), dqr)
        dqn = _dot_nt(dqr, wq_ref[...])
        gqg_ref[...] += jnp.sum(dqn * xq, axis=0, keepdims=True)
        dxq = dqn * qgv
        dz_ref[:, :Q_LORA] = (rq * (dxq - xq * jnp.mean(dxq * xq, axis=-1, keepdims=True))).astype(BF16)

        dk = dk_ref[...] * LN2
        dkv = jnp.concatenate([dk, dv_ref[...]], axis=-1).astype(BF16)
        rkv = lax.rsqrt(jnp.mean(zkv * zkv, axis=-1, keepdims=True) + EPS)
        xkv = zkv * rkv
        gwkv_ref[...] += _dot_tn((xkv * kvgv).astype(BF16), dkv)
        dkvn = _dot_nt(dkv, wkv_ref[...])
        gkvg_ref[...] += jnp.sum(dkvn * xkv, axis=0, keepdims=True)
        dxkv = dkvn * kvgv
        dz_ref[:, Q_LORA:A_KR] = (rkv * (dxkv - xkv * jnp.mean(dxkv * xkv, axis=-1, keepdims=True))).astype(BF16)

        dkpe = dk[:, :HEAD_LANES]
        for h in range(1, N_HEADS):
            dkpe = dkpe + dk[:, h * HEAD_LANES:(h + 1) * HEAD_LANES]
        dz_ref[:, A_KR:] = (jnp.where(_lane_lo(), 0.0, dkpe * cos) + pltpu.roll(dkpe * sin, HALF, 1)).astype(BF16)

    tok = lambda w: pl.BlockSpec((tm, w), lambda i: (i, 0))
    return pl.pallas_call(
        body, name="mid_bwd", grid=(n_tok // tm,),
        out_shape=[jax.ShapeDtypeStruct((n_tok, A_GM), BF16),
                   jax.ShapeDtypeStruct(wq2.shape, F32), jax.ShapeDtypeStruct(wkv.shape, F32),
                   jax.ShapeDtypeStruct((1, Q_LORA), F32), jax.ShapeDtypeStruct((1, KV_LORA), F32)],
        in_specs=[tok(1024), tok(1024), tok(512), tok(640), tok(2 * HEAD_LANES), _full(qg.shape), _full(kvg.shape),
                  _full(wq2.shape), _full(wkv.shape)],
        out_specs=[tok(A_GM), _full(wq2.shape), _full(wkv.shape), _full((1, Q_LORA)), _full((1, KV_LORA))],
        compiler_params=_params(1),
    )(dqf, dkf, dv, zqkv, rope, qg, kvg, wq2, wkv)


def _in_bwd_call(x, dx2, dz, dg, dqs, dkd, dvd, mod, b_ada, ng, wa, seq):
    n_tok = x.shape[0]
    tm = min(TOKEN_TILE, seq)
    per_seq = seq // tm
    n_seq = n_tok // seq

    def body(x_ref, dx2_ref, dz_ref, dg_ref, dqs_ref, dkd_ref, dvd_ref, mod_ref, bada_ref, ng_ref,
             wa_ref, gx_ref, gwa_ref, gng_ref, dshift_ref, dscale_ref):
        i = pl.program_id(0)

        @pl.when(i == 0)
        def _():
            gwa_ref[...] = jnp.zeros_like(gwa_ref)
            gng_ref[...] = jnp.zeros_like(gng_ref)

        @pl.when(i % per_seq == 0)
        def _():
            dshift_ref[...] = jnp.zeros_like(dshift_ref)
            dscale_ref[...] = jnp.zeros_like(dscale_ref)

        xv = x_ref[...]
        modv = mod_ref[0] + bada_ref[...]
        shift, scale = modv[:, :D_MODEL], modv[:, D_MODEL:2 * D_MODEL]
        ngv = ng_ref[...]
        r1 = lax.rsqrt(jnp.mean(xv * xv, axis=-1, keepdims=True) + EPS)
        xn = xv * r1
        hb = ((xn * ngv) * (1.0 + scale) + shift).astype(BF16)

        dgv = dg_ref[...]
        pieces = [(A_ZQ, dz_ref[...]), (A_GM, dgv[:, :512]), (A_QS, dqs_ref[...].astype(BF16)),
                  (A_KS, jnp.concatenate([_once(dkd_ref[...]) * LN2, _once(dvd_ref[...])], axis=1).astype(BF16)),
                  (A_GS, dgv[:, 512:])]
        dh = None
        for off, piece in pieces:
            wd = piece.shape[1]
            gwa_ref[:, off:off + wd] += _dot_tn(hb, piece)
            term = _dot_nt(piece, wa_ref[:, off:off + wd])
            dh = term if dh is None else dh + term

        dshift_ref[0] += jnp.sum(dh, axis=0, keepdims=True)
        dscale_ref[0] += jnp.sum(dh * (xn * ngv), axis=0, keepdims=True)
        gng_ref[...] += jnp.sum(dh * xn * (1.0 + scale), axis=0, keepdims=True)
        dxn = dh * ngv * (1.0 + scale)
        gx_ref[...] = dx2_ref[...] + r1 * (dxn - xn * jnp.mean(dxn * xn, axis=-1, keepdims=True))

    tok = lambda w: pl.BlockSpec((tm, w), lambda i: (i, 0))
    per_b = lambda w: pl.BlockSpec((1, 1, w), lambda i: (i // per_seq, 0, 0))
    return pl.pallas_call(
        body, name="in_bwd", grid=(n_tok // tm,),
        out_shape=[jax.ShapeDtypeStruct((n_tok, D_MODEL), F32), jax.ShapeDtypeStruct((D_MODEL, A_END), F32),
                   jax.ShapeDtypeStruct((1, D_MODEL), F32),
                   jax.ShapeDtypeStruct((n_seq, 1, D_MODEL), F32), jax.ShapeDtypeStruct((n_seq, 1, D_MODEL), F32)],
        in_specs=[tok(D_MODEL), tok(D_MODEL), tok(A_GM), tok(D_MODEL), tok(512), tok(256), tok(256),
                  per_b(3 * D_MODEL), _full(b_ada.shape), _full(ng.shape), _full(wa.shape)],
        out_specs=[tok(D_MODEL), _full((D_MODEL, A_END)), _full((1, D_MODEL)), per_b(D_MODEL), per_b(D_MODEL)],
        compiler_params=_params(1),
    )(x, dx2, dz, dg, dqs, dkd, dvd, mod, b_ada, ng, wa)


def _adam_math(w, g, m, v):
    m_new = ADAM_B1 * m + (1.0 - ADAM_B1) * g
    v_new = ADAM_B2 * v + (1.0 - ADAM_B2) * (g * g)
    m_hat = m_new / (1.0 - ADAM_B1 ** ADAM_STEP)
    v_hat = v_new / (1.0 - ADAM_B2 ** ADAM_STEP)
    delta = -ADAM_LR * (m_hat / (jnp.sqrt(v_hat) + ADAM_EPS) + ADAM_WD * w)
    return delta, m_new, v_new


def _adam_call(name, w, g, m, v):
    rows, cols = w.shape
    tr = next((t for t in (256, 128) if rows % t == 0), rows)

    def body(w_ref, g_ref, m_ref, v_ref, d_ref, mo_ref, vo_ref):
        d, mn, vn = _adam_math(w_ref[...], g_ref[...], m_ref[...], v_ref[...])
        d_ref[...] = d
        mo_ref[...] = mn
        vo_ref[...] = vn

    spec = pl.BlockSpec((tr, cols), lambda i: (i, 0))
    return pl.pallas_call(
        body, name=name, grid=(rows // tr,),
        out_shape=[jax.ShapeDtypeStruct(w.shape, F32)] * 3,
        in_specs=[spec] * 4, out_specs=[spec] * 3,
        compiler_params=_params(1),
    )(w, g, m, v)


def _ada_bwd_call(act_all, dmod_cols, w, m, v):
    rows, cols = w.shape
    tr = 512

    def body(a_ref, dm_ref, w_ref, m_ref, v_ref, g_ref, d_ref, mo_ref, vo_ref):
        g = _dot_tn(a_ref[...].astype(BF16), dm_ref[...].astype(BF16))
        d, mn, vn = _adam_math(w_ref[...], g, m_ref[...], v_ref[...])
        g_ref[...] = g
        d_ref[...] = d
        mo_ref[...] = mn
        vo_ref[...] = vn

    spec = pl.BlockSpec((tr, cols), lambda i: (i, 0))
    nb = act_all.shape[0]
    return pl.pallas_call(
        body, name="ada_bwd", grid=(rows // tr,),
        out_shape=[jax.ShapeDtypeStruct(w.shape, F32)] * 4,
        in_specs=[pl.BlockSpec((nb, tr), lambda i: (0, i)), _full(dmod_cols.shape), spec, spec, spec],
        out_specs=[spec] * 4,
        compiler_params=_params(1),
    )(act_all, dmod_cols, w, m, v)


SMALL_ROW = {"norm_gain": (0, 1024), "final_gain": (1024, 2048), "q_norm_gain": (2048, 2432),
             "kv_norm_gain": (2432, 2688), "swa_sinks": (2688, 2696), "loss": (2816, 2944)}
SMALL_ORDER = ("b_ada", "norm_gain", "q_norm_gain", "kv_norm_gain", "swa_sinks", "final_gain")


def _small_call(parts_all, n_seq, params):
    k = len(params)

    def body(p_ref, *refs):
        ins, outs, loss_ref = refs[:3 * k], refs[3 * k:7 * k], refs[7 * k]
        row = p_ref[n_seq:n_seq + 1, :]
        for dv in range(1, 8):
            r0 = dv * ROWS_PER_DEVICE + n_seq
            row = row + p_ref[r0:r0 + 1, :]
        gb = None
        for dv in range(8):
            for r in range(n_seq):
                r0 = dv * ROWS_PER_DEVICE + r
                gb = p_ref[r0:r0 + 1, :] if gb is None else gb + p_ref[r0:r0 + 1, :]
        for j, name in enumerate(SMALL_ORDER):
            g = gb if name == "b_ada" else row[:, SMALL_ROW[name][0]:SMALL_ROW[name][1]]
            d, mn, vn = _adam_math(ins[3 * j][...], g, ins[3 * j + 1][...], ins[3 * j + 2][...])
            outs[4 * j][...] = g
            outs[4 * j + 1][...] = d
            outs[4 * j + 2][...] = mn
            outs[4 * j + 3][...] = vn
        loss_ref[...] = row[:, SMALL_ROW["loss"][0]:SMALL_ROW["loss"][1]]

    flat = [t for p in params for t in p]
    res = pl.pallas_call(
        body, name="small_update", grid=(1,),
        out_shape=[jax.ShapeDtypeStruct(p[0].shape, F32) for p in params for _ in range(4)]
        + [jax.ShapeDtypeStruct((1, HEAD_LANES), F32)],
        in_specs=[_full(parts_all.shape)] + [_full(t.shape) for t in flat],
        out_specs=[_full(p[0].shape) for p in params for _ in range(4)] + [_full((1, HEAD_LANES))],
        compiler_params=_params(1),
    )(parts_all, *flat)
    return [res[4 * j:4 * j + 4] for j in range(k)], res[4 * k]


def _rot(t):
    half = t.shape[-1] // 2
    return jnp.concatenate([-t[..., half:], t[..., :half]], axis=-1)


def _rot_t(g):
    half = g.shape[-1] // 2
    return jnp.concatenate([g[..., half:], -g[..., :half]], axis=-1)


def _columns(segments, lo, hi):
    out, at = [], 0
    for seg in segments:
        n = seg.shape[1]
        a, b = max(lo, at), min(hi, at + n)
        if a < b:
            out.append(seg[:, a - at:b - at])
        at += n
    return out


def _prepare_in(w_in_blocks):
    o = [0]
    for s in IN_SPLITS:
        o.append(o[-1] + s)
    part = lambda a, b: _columns(w_in_blocks, a, b)
    kr = jnp.concatenate(part(o[2], o[3]), axis=1)
    zero = jnp.zeros((kr.shape[0], 32), kr.dtype)
    return jnp.concatenate(part(0, o[2]) + [_rot(kr), zero, kr, zero] + part(o[3], o[8]), axis=1)


def _prepare_up(w_uq, w_ukv):
    uq = w_uq.reshape(Q_LORA, N_HEADS, MLA_NOPE + MLA_ROPE)
    zq = jnp.zeros((Q_LORA, N_HEADS, 32), w_uq.dtype)
    uq_full = jnp.concatenate([uq, zq], axis=-1).reshape(Q_LORA, 1024)
    uq_rot = jnp.concatenate([jnp.zeros((Q_LORA, N_HEADS, 64), w_uq.dtype), _rot(uq[..., MLA_NOPE:]), zq],
                             axis=-1).reshape(Q_LORA, 1024)
    wq2 = jnp.concatenate([uq_full, uq_rot], axis=1)
    ukv = w_ukv.reshape(KV_LORA, N_HEADS, 128)
    k_full = jnp.concatenate([ukv[..., :64], jnp.zeros((KV_LORA, N_HEADS, 64), w_ukv.dtype)], axis=-1).reshape(KV_LORA, 1024)
    wkv = jnp.concatenate([k_full, ukv[..., 64:].reshape(KV_LORA, 512)], axis=1)
    return wq2, wkv


def _restore_in(gwa):
    gkr = gwa[:, A_KR + 64:A_KR + 96] + _rot_t(gwa[:, A_KR:A_KR + 32])
    in_order = [gwa[:, :A_KR], gkr, gwa[:, A_GM:]]
    n = D_IN // 4
    return [jnp.concatenate(_columns(in_order, k * n, (k + 1) * n), axis=1) for k in range(4)]


def _restore_up(gwq2, gwkv):
    gf = gwq2[:, :1024].reshape(Q_LORA, N_HEADS, 128)
    gr = gwq2[:, 1024:].reshape(Q_LORA, N_HEADS, 128)
    g_uq = jnp.concatenate([gf[..., :64], gf[..., 64:96] + _rot_t(gr[..., 64:96])], axis=-1).reshape(Q_LORA, 768)
    gk = gwkv[:, :1024].reshape(KV_LORA, N_HEADS, 128)[..., :64]
    gv = gwkv[:, 1024:].reshape(KV_LORA, N_HEADS, 64)
    g_ukv = jnp.concatenate([gk, gv], axis=-1).reshape(KV_LORA, 1024)
    return g_uq, g_ukv


def _local_step(x, positions, target, mod_rows, b_ada, ng, qg, kvg, sinks, fg, w_in_b, later_shards):
    n_seq, seq, _ = x.shape
    n_tok = n_seq * seq
    x2d = x.reshape(n_tok, D_MODEL)
    t2d = target.reshape(n_tok, D_MODEL)
    pos_f = positions.astype(F32)
    pos_col = pos_f.reshape(n_tok, 1)
    pos_row = pos_f.reshape(n_tok // SWA_WINDOW, 1, SWA_WINDOW)
    mod3 = mod_rows.reshape(n_seq, 1, 3 * D_MODEL)
    inv = ROPE_THETA ** (-jnp.arange(0, MLA_ROPE, 2, dtype=F32) / MLA_ROPE)
    inv128 = jnp.tile(jnp.concatenate([inv, inv]), 4).reshape(1, HEAD_LANES)
    fg2 = fg.reshape(1, D_MODEL)

    wa = _prepare_in(w_in_b)
    zqkv, zkr, gates, qs, kd, vd, rope, f_uq, f_ukv, f_out = _pre_call(x2d, pos_col, mod3, b_ada, ng, inv128, wa,
                                                                       later_shards, seq)
    cols = lambda t, r: jnp.transpose(t.reshape(4, r, -1), (1, 0, 2)).reshape(r, -1)
    wq2, wkv = _prepare_up(cols(f_uq, Q_LORA), cols(f_ukv, KV_LORA))
    w_out_b = f_out.reshape(D_MODEL, D_MODEL)
    qf, kf, v = _up_call(zqkv, zkr, rope, qg, kvg, wq2, wkv, seq)
    o_mla, lse_mla = _mla_fwd_call(qf, kf, v, n_seq, seq)
    o_swa, lse_swa = _swa_fwd_call(qs, kd, vd, pos_col, pos_row, sinks, n_seq, seq)
    dx2, do, dg, g_out, g_fg, dgate, loss, delta_mla, delta_swa = _post_call(x2d, t2d, o_mla, o_swa, gates, mod3, b_ada, fg2, w_out_b, seq)
    dqf, dkf, dv = _mla_bwd_call(qf, kf, v, do, delta_mla, lse_mla, n_seq, seq)
    dqs, dkd, dvd, dsink, r_out = _swa_bwd_call(qs, kd, vd, do, delta_swa, lse_swa, pos_col, pos_row, sinks,
                                                g_out.reshape(4, 2, D_MODEL // 8, D_MODEL), n_seq, seq)
    dz, g_wq2, g_wkv, g_qg, g_kvg = _mid_bwd_call(dqf, dkf, dv, zqkv, rope, qg, kvg, wq2, wkv, seq)
    gx, g_wa, g_ng, dshift, dscale = _in_bwd_call(x2d, dx2, dz, dg, dqs, dkd, dvd, mod3, b_ada, ng, wa, seq)
    g_in = _restore_in(g_wa)
    g_uq, g_ukv = _restore_up(g_wq2, g_wkv)
    dmod = jnp.concatenate([dshift, dscale, dgate], axis=-1).reshape(n_seq, 3 * D_MODEL)
    small_row = jnp.concatenate([g_ng, g_fg, g_qg, g_kvg, jnp.pad(jnp.sum(dsink, axis=1).reshape(1, N_HEADS), ((0, 0), (0, 120))),
                                 loss, jnp.zeros((1, 128), F32)], axis=1)
    return gx.reshape(x.shape), (g_in, g_uq, g_ukv), r_out, small_row, dmod


def kernel(x, c, positions, w_ada, b_ada, norm_gain, w_in, q_norm_gain, kv_norm_gain, w_uq, w_ukv, swa_sinks, w_out, final_gain, loss_target, m_w_ada, m_b_ada, m_norm_gain, m_w_in, m_q_norm_gain, m_kv_norm_gain, m_w_uq, m_w_ukv, m_swa_sinks, m_w_out, m_final_gain, v_w_ada, v_b_ada, v_norm_gain, v_w_in, v_q_norm_gain, v_kv_norm_gain, v_w_uq, v_w_ukv, v_swa_sinks, v_w_out, v_final_gain):
    n_seq = x.shape[0]
    xi, yi, ci = lax.axis_index("x"), lax.axis_index("y"), lax.axis_index("c")
    dev = 4 * xi + 2 * yi + ci
    chip = 2 * xi + yi

    halves = lambda w: w.astype(BF16).reshape(2, w.shape[0] // 2, w.shape[1])
    c_blk = jnp.pad(c, ((0, ROWS_PER_DEVICE - n_seq), (0, 0)))
    act_all, pieces, f_in = _comm_fwd_call(c_blk, w_ada[0], [halves(w_in[0])])
    mine = lax.dynamic_slice_in_dim(pieces, dev * ROWS_PER_DEVICE, n_seq, axis=1)
    mod_rows = jnp.transpose(mine, (1, 0, 2)).reshape(n_seq, 3 * D_MODEL)
    w_in_blocks = [f_in[k].reshape(D_MODEL, -1) for k in range(4)]

    gx, (g_in_blocks, g_uq, g_ukv), r_out, small_row, dmod = _local_step(
        x, positions, loss_target, mod_rows, b_ada, norm_gain, q_norm_gain, kv_norm_gain, swa_sinks, final_gain,
        w_in_blocks, [halves(w_uq[0]), halves(w_ukv[0]), halves(w_out[0])])

    grads = [jnp.stack(g_in_blocks).reshape(4, 2, D_MODEL // 2, -1), _by_owner(g_uq, g_uq.shape[1] // 4),
             _by_owner(g_ukv, g_ukv.shape[1] // 4)]
    part = jnp.concatenate([dmod, small_row, jnp.zeros((ROWS_PER_DEVICE - n_seq - 1, 3 * D_MODEL), F32)], axis=0)
    r_in, r_uq, r_ukv, parts_all = _comm_bwd_call(grads, part)
    g_in_s, g_uq_s = r_in.reshape(w_in.shape[1:]), r_uq.reshape(w_uq.shape[1:])
    g_ukv_s, g_out_s = r_ukv.reshape(w_ukv.shape[1:]), r_out.reshape(w_out.shape[1:])

    tr = lambda a: jnp.swapaxes(a[0], 0, 1)
    back = lambda ts: [jnp.swapaxes(t, 0, 1) for t in ts]
    d_in, nm_in, nv_in = back(_adam_call("adam_w_in", tr(w_in), g_in_s.T, tr(m_w_in), tr(v_w_in)))
    d_uq, nm_uq, nv_uq = back(_adam_call("adam_w_uq", tr(w_uq), g_uq_s.T, tr(m_w_uq), tr(v_w_uq)))
    d_ukv, nm_ukv, nv_ukv = _adam_call("adam_w_ukv", w_ukv[0], g_ukv_s, m_w_ukv[0], v_w_ukv[0])
    d_out, nm_out, nv_out = _adam_call("adam_w_out", w_out[0], g_out_s, m_w_out[0], v_w_out[0])
    dmod_cols = lax.dynamic_slice_in_dim(parts_all, chip * 768, 768, axis=1)
    g_ada, d_ada, nm_ada, nv_ada = _ada_bwd_call(act_all, dmod_cols, w_ada[0], m_w_ada[0], v_w_ada[0])

    row = lambda t: t.reshape(1, -1)
    small = {"b_ada": (b_ada, m_b_ada, v_b_ada), "norm_gain": (norm_gain, m_norm_gain, v_norm_gain),
             "q_norm_gain": (q_norm_gain, m_q_norm_gain, v_q_norm_gain),
             "kv_norm_gain": (kv_norm_gain, m_kv_norm_gain, v_kv_norm_gain),
             "swa_sinks": (swa_sinks, m_swa_sinks, v_swa_sinks),
             "final_gain": (row(final_gain), row(m_final_gain), row(v_final_gain))}
    res, loss_row = _small_call(parts_all, n_seq, [small[name] for name in SMALL_ORDER])
    res = dict(zip(SMALL_ORDER, res))
    res["final_gain"] = [t.reshape(-1) for t in res["final_gain"]]
    e = lambda t: t[None]
    big = {"w_ada": (e(g_ada), e(d_ada), e(nm_ada), e(nv_ada)), "w_in": (e(g_in_s), e(d_in), e(nm_in), e(nv_in)),
           "w_uq": (e(g_uq_s), e(d_uq), e(nm_uq), e(nv_uq)), "w_ukv": (e(g_ukv_s), e(d_ukv), e(nm_ukv), e(nv_ukv)),
           "w_out": (e(g_out_s), e(d_out), e(nm_out), e(nv_out))}
    order = ("w_ada", "b_ada", "norm_gain", "w_in", "q_norm_gain", "kv_norm_gain", "w_uq", "w_ukv", "swa_sinks", "w_out",
             "final_gain")
    pick = lambda kind: [(big[n] if n in big else res[n])[kind] for n in order]
    return (loss_row[0, 0], gx, *pick(0), *pick(1), *pick(2), *pick(3))
```

```python
import jax
import jax.numpy as jnp
from jax import lax
from jax.experimental import pallas as pl
from jax.experimental.pallas import tpu as pltpu

F32 = jnp.float32
BF16 = jnp.bfloat16

D_MODEL = 1024
Q_LORA = 384
KV_LORA = 256
N_HEADS = 8
MLA_NOPE = 64
MLA_ROPE = 32
HEAD_LANES = 128
HALF = 64
SWA_WINDOW = 128
EPS = 1e-6
ROPE_THETA = 10000.0
MLA_SCALE = (MLA_NOPE + MLA_ROPE) ** -0.5
LOG2E = 1.4426950408889634
LN2 = 0.6931471805599453
SWA_SCALE = 64 ** -0.5
NEG = -1e30

ADAM_LR = 0.001
ADAM_B1 = 0.9
ADAM_B2 = 0.999
ADAM_EPS = 1e-08
ADAM_WD = 0.01
ADAM_STEP = 10

A_ZQ, A_ZKV, A_KR, A_GM, A_QS, A_KS, A_VS, A_GS, A_END = 0, 384, 640, 768, 1280, 1792, 1920, 2048, 2560
IN_SPLITS = (384, 256, 32, 512, 512, 128, 128, 512)
D_IN = sum(IN_SPLITS)

TOKEN_TILE = 512
ATT_TILE = 256
VMEM_LIMIT = 56 * 1024 * 1024


def _dot(a, b):
    return jnp.dot(a, b, preferred_element_type=F32)


def _dot_nt(a, b):
    return lax.dot_general(a, b, (((1,), (1,)), ((), ())), preferred_element_type=F32)


def _dot_tn(a, b):
    return lax.dot_general(a, b, (((0,), (0,)), ((), ())), preferred_element_type=F32)


def _params(n_grid):
    return pltpu.CompilerParams(dimension_semantics=("arbitrary",) * n_grid, vmem_limit_bytes=VMEM_LIMIT)


def _full(shape):
    nd = len(shape)
    return pl.BlockSpec(shape, lambda *_: (0,) * nd, pipeline_mode=pl.Buffered(1))


def _sigmoid(g):
    return 1.0 / (1.0 + jnp.exp(-g))


SUB_TILE = 256


def _sub_tiles(tm):
    sub = min(SUB_TILE, tm)
    return [slice(s * sub, (s + 1) * sub) for s in range(tm // sub)]


MESH = pl.DeviceIdType.MESH
ROWS_PER_DEVICE = 8
VMEM_SPEC = pl.BlockSpec(memory_space=pltpu.VMEM)
ANY_SPEC = pl.BlockSpec(memory_space=pl.ANY)


def _position():
    x, y, c = lax.axis_index("x"), lax.axis_index("y"), lax.axis_index("c")
    sibling = (x, y, 1 - c)
    others = [(1 - x, y, c), (x, 1 - y, c), (1 - x, 1 - y, c)]
    return (x, y, c), 4 * x + 2 * y + c, 2 * x + y, sibling, others


def _rows_of(dev):
    return pl.ds(pl.multiple_of(dev * ROWS_PER_DEVICE, ROWS_PER_DEVICE), ROWS_PER_DEVICE)


def _all_to_all_rows(block_ref, table_ref, dev, me, send_sems, recv_sems):
    x, y, c = me
    waits = []
    for k in range(1, 8):
        peer = (1 - x if k & 4 else x, 1 - y if k & 2 else y, 1 - c if k & 1 else c)
        pltpu.make_async_remote_copy(src_ref=block_ref, dst_ref=table_ref.at[_rows_of(dev)], send_sem=send_sems.at[k - 1],
                                     recv_sem=recv_sems.at[k - 1], device_id=peer, device_id_type=MESH).start()
        waits.append(pltpu.make_async_remote_copy(
            src_ref=block_ref, dst_ref=table_ref.at[_rows_of(jnp.bitwise_xor(dev, k))], send_sem=send_sems.at[k - 1],
            recv_sem=recv_sems.at[k - 1], device_id=peer, device_id_type=MESH))
    return waits


def _comm_fwd_call(c_blk, w_ada, shards):
    n = len(shards)

    def body(c_ref, wada_ref, *refs):
        w_refs, act_ref, pieces_ref, full_refs = refs[:n], refs[n], refs[n + 1], refs[n + 2:2 * n + 2]
        c_all_ref = refs[2 * n + 2]
        c_send, c_recv, p_send, p_recv, w_send, w_recv, f_send, f_recv, loc_sem = refs[2 * n + 3:]
        me, dev, chip, sibling, others = _position()
        core = me[2]
        chip_of = [2 * p[0] + p[1] for p in others]

        local = [pltpu.make_async_copy(w_refs[i], full_refs[i].at[chip], loc_sem.at[i]) for i in range(n)]

        def over_ici(i, j, src_chip):
            return pltpu.make_async_remote_copy(
                src_ref=w_refs[i].at[core], dst_ref=full_refs[i].at[src_chip, core], send_sem=w_send.at[3 * i + j],
                recv_sem=w_recv.at[3 * i + j], device_id=others[j], device_id_type=MESH)

        def to_sibling(i, j, half):
            return pltpu.make_async_remote_copy(
                src_ref=full_refs[i].at[chip_of[j], half], dst_ref=full_refs[i].at[chip_of[j], half],
                send_sem=f_send.at[3 * i + j], recv_sem=f_recv.at[3 * i + j], device_id=sibling, device_id_type=MESH)

        c_all_ref[_rows_of(dev), :] = c_ref[...]
        c_waits = _all_to_all_rows(c_ref, c_all_ref, dev, me, c_send, c_recv)
        sent = [over_ici(i, j, chip) for i in range(n) for j in range(3)]
        for cp in sent:
            cp.start()
        for cp in local:
            cp.start()

        for cp in c_waits:
            cp.wait()
        cv = c_all_ref[...]
        act = cv * _sigmoid(cv)
        act_ref[...] = act
        pieces_ref[chip] = _dot(act.astype(BF16), wada_ref[...].astype(BF16))
        piece = lambda j, src_chip: pltpu.make_async_remote_copy(
            src_ref=pieces_ref.at[chip], dst_ref=pieces_ref.at[src_chip], send_sem=p_send.at[j], recv_sem=p_recv.at[j],
            device_id=others[j], device_id_type=MESH)
        for j in range(3):
            piece(j, chip).start()

        for i in range(n):
            for j in range(3):
                over_ici(i, j, chip_of[j]).wait_recv()
                to_sibling(i, j, core).start()
        for j in range(3):
            piece(j, chip).wait_send()
            piece(j, chip_of[j]).wait_recv()
        for i in range(n):
            for j in range(3):
                to_sibling(i, j, 1 - core).wait_recv()
                to_sibling(i, j, core).wait_send()
        for cp in sent:
            cp.wait_send()
        for cp in local:
            cp.wait()

    rows = 8 * ROWS_PER_DEVICE
    dma = pltpu.SemaphoreType.DMA
    return pl.pallas_call(
        body, name="comm_fwd",
        out_shape=[jax.ShapeDtypeStruct((rows, D_MODEL), F32), jax.ShapeDtypeStruct((4, rows, w_ada.shape[1]), F32)]
        + [jax.ShapeDtypeStruct((4,) + s.shape, s.dtype) for s in shards],
        in_specs=[VMEM_SPEC, VMEM_SPEC] + [ANY_SPEC] * n,
        out_specs=[VMEM_SPEC, VMEM_SPEC] + [ANY_SPEC] * n,
        scratch_shapes=[pltpu.VMEM((rows, D_MODEL), F32), dma((7,)), dma((7,)), dma((3,)), dma((3,)),
                        dma((3 * n,)), dma((3 * n,)), dma((3 * n,)), dma((3 * n,)), dma((n,))],
        compiler_params=pltpu.CompilerParams(vmem_limit_bytes=VMEM_LIMIT),
    )(c_blk, w_ada, *shards)


def _comm_bwd_call(grads, part):
    n = len(grads)

    def body(part_ref, *refs):
        g_refs, f_refs, parts_ref = refs[:n], refs[n:2 * n], refs[2 * n]
        scratch = refs[2 * n + 1:]
        a_refs, b_refs, p_refs, r_refs = (scratch[k * n:(k + 1) * n] for k in range(4))
        s_send, s_recv, d_send, d_recv, e_send, e_recv, h_send, h_recv, loc_sem = scratch[4 * n:]
        me, dev, chip, sibling, others = _position()
        core = me[2]
        chip_of = [2 * p[0] + p[1] for p in others]

        parts_ref[_rows_of(dev), :] = part_ref[...]
        s_waits = _all_to_all_rows(part_ref, parts_ref, dev, me, s_send, s_recv)

        mine = [pltpu.make_async_copy(g_refs[i].at[:, core], a_refs[i], loc_sem.at[i]) for i in range(n)]
        swap = [pltpu.make_async_remote_copy(src_ref=g_refs[i].at[:, 1 - core], dst_ref=b_refs[i], send_sem=d_send.at[i],
                                             recv_sem=d_recv.at[i], device_id=sibling, device_id_type=MESH) for i in range(n)]
        order = sorted(range(n), key=lambda i: g_refs[i].shape[2] * g_refs[i].shape[3])
        for i in order:
            mine[i].start()
            swap[i].start()
        cross = [pltpu.make_async_remote_copy(src_ref=p_refs[i].at[chip_of[j]], dst_ref=r_refs[i].at[j],
                                              send_sem=e_send.at[3 * i + j], recv_sem=e_recv.at[3 * i + j],
                                              device_id=others[j], device_id_type=MESH) for i in range(n) for j in range(3)]
        for i in order:
            mine[i].wait()
            swap[i].wait()
            for k in range(4):
                s = a_refs[i][k] + b_refs[i][k]
                a_refs[i][k] = s
                p_refs[i][k] = s.astype(BF16)
            for j in range(3):
                cross[3 * i + j].start()
        share = {}
        for i in order:
            for j in range(3):
                cross[3 * i + j].wait()
            f_refs[i][core] = (a_refs[i][chip] + r_refs[i][0].astype(F32) + r_refs[i][1].astype(F32)
                               + r_refs[i][2].astype(F32))
            share[i] = pltpu.make_async_remote_copy(src_ref=f_refs[i].at[core], dst_ref=f_refs[i].at[core],
                                                    send_sem=h_send.at[i], recv_sem=h_recv.at[i], device_id=sibling,
                                                    device_id_type=MESH)
            share[i].start()
        for i in range(n):
            share[i].wait_send()
            pltpu.make_async_remote_copy(src_ref=f_refs[i].at[core], dst_ref=f_refs[i].at[1 - core], send_sem=h_send.at[i],
                                         recv_sem=h_recv.at[i], device_id=sibling, device_id_type=MESH).wait_recv()
        for cp in s_waits:
            cp.wait()

    rows = 8 * ROWS_PER_DEVICE
    dma = pltpu.SemaphoreType.DMA
    quarter = [(4,) + g.shape[2:] for g in grads]
    return pl.pallas_call(
        body, name="comm_bwd",
        out_shape=[jax.ShapeDtypeStruct((2,) + g.shape[2:], F32) for g in grads]
        + [jax.ShapeDtypeStruct((rows, part.shape[1]), F32)],
        in_specs=[VMEM_SPEC] + [ANY_SPEC] * n,
        out_specs=[VMEM_SPEC] * (n + 1),
        scratch_shapes=[pltpu.VMEM(q, F32) for q in quarter] + [pltpu.VMEM(q, F32) for q in quarter]
        + [pltpu.VMEM(q, BF16) for q in quarter] + [pltpu.VMEM((3,) + q[1:], BF16) for q in quarter]
        + [dma((7,)), dma((7,)), dma((n,)), dma((n,)), dma((3 * n,)), dma((3 * n,)), dma((n,)), dma((n,)), dma((n,))],
        compiler_params=pltpu.CompilerParams(vmem_limit_bytes=VMEM_LIMIT),
    )(part, *grads)


def _by_owner(g, n):
    return jnp.transpose(g.reshape(g.shape[0], 4, n), (1, 0, 2)).reshape(4, 2, g.shape[0] // 2, n)


def _reduce_operands(g):
    quarter = (4,) + g.shape[2:]
    dma = pltpu.SemaphoreType.DMA
    scratch = [pltpu.VMEM(quarter, F32), pltpu.VMEM(quarter, F32), pltpu.VMEM(quarter, BF16),
               pltpu.VMEM((3,) + quarter[1:], BF16), dma((5,)), dma((5,)), dma((2,))]
    return jax.ShapeDtypeStruct((2,) + g.shape[2:], F32), scratch


def _grad_reduce(step, n_steps, g_ref, f_ref, a_ref, b_ref, p_ref, r_ref, send, recv, loc_sem):
    me, _, chip, sibling, others = _position()
    core = me[2]
    chip_of = [2 * p[0] + p[1] for p in others]
    remote = lambda src, dst, k, to: pltpu.make_async_remote_copy(
        src_ref=src, dst_ref=dst, send_sem=send.at[k], recv_sem=recv.at[k], device_id=to, device_id_type=MESH)
    mine = pltpu.make_async_copy(g_ref.at[:, core], a_ref, loc_sem.at[0])
    swap = remote(g_ref.at[:, 1 - core], b_ref, 0, sibling)
    cross = [remote(p_ref.at[chip_of[j]], r_ref.at[j], 1 + j, others[j]) for j in range(3)]
    total_ref = b_ref.at[0]
    keep = pltpu.make_async_copy(total_ref, f_ref.at[core], loc_sem.at[1])
    share = lambda half: remote(total_ref, f_ref.at[half], 4, sibling)
    at = [k * (n_steps - 1) // 3 for k in range(4)]

    @pl.when(step == at[0])
    def _():
        mine.start()
        swap.start()

    @pl.when(step == at[1])
    def _():
        mine.wait()
        swap.wait()
        for k in range(4):
            s = a_ref[k] + b_ref[k]
            a_ref[k] = s
            p_ref[k] = s.astype(BF16)
        for cp in cross:
            cp.start()

    @pl.when(step == at[2])
    def _():
        for cp in cross:
            cp.wait()
        total_ref[...] = a_ref[chip] + r_ref[0].astype(F32) + r_ref[1].astype(F32) + r_ref[2].astype(F32)
        keep.start()
        share(core).start()

    @pl.when(step == at[3])
    def _():
        keep.wait()
        share(core).wait_send()
        share(1 - core).wait_recv()


def _twice(t):
    lo = _lane_lo()
    other = pltpu.roll(t, HALF, 1)
    return jnp.concatenate([jnp.where(lo, t, other), jnp.where(lo, other, t)], axis=1)


def _once(g):
    first, second = g[:, :HEAD_LANES], g[:, HEAD_LANES:]
    return jnp.where(_lane_lo(), first + pltpu.roll(first, HALF, 1), second + pltpu.roll(second, HALF, 1))


def _rope_tables(pos_ref, inv_row, rope_ref):
    quarter = pos_ref.shape[0] // 4
    lane = lax.broadcasted_iota(jnp.int32, (1, HEAD_LANES), 1)
    pos = [pos_ref[g * quarter:(g + 1) * quarter, :] for g in range(4)]
    ang = jnp.where(lane < 32, pos[0], jnp.where(lane < 64, pos[1], jnp.where(lane < 96, pos[2], pos[3]))) * inv_row
    cos, sin = jnp.cos(ang), jnp.sin(ang)
    rope_lanes = jnp.logical_and(lane >= HALF, lane < HALF + MLA_ROPE)
    for g in range(4):
        rows = slice(g * quarter, (g + 1) * quarter)
        shift = (HALF - 32 * g) % HEAD_LANES
        at = lambda t: t if shift == 0 else pltpu.roll(t, shift, 1)
        rope_ref[rows, :HEAD_LANES] = jnp.where(rope_lanes, at(cos), 1.0)
        rope_ref[rows, HEAD_LANES:] = jnp.where(rope_lanes, at(sin), 0.0)


def _gather_in_steps(step, n_steps, w_refs, full_refs, w_send, w_recv, f_send, f_recv, loc_sem):
    me, _, chip, sibling, others = _position()
    core = me[2]
    chip_of = [2 * p[0] + p[1] for p in others]
    n = len(w_refs)
    local = [pltpu.make_async_copy(w_refs[i], full_refs[i].at[chip], loc_sem.at[i]) for i in range(n)]

    def over_ici(i, j, src_chip):
        return pltpu.make_async_remote_copy(
            src_ref=w_refs[i].at[core], dst_ref=full_refs[i].at[src_chip, core], send_sem=w_send.at[3 * i + j],
            recv_sem=w_recv.at[3 * i + j], device_id=others[j], device_id_type=MESH)

    def to_sibling(i, j, half):
        return pltpu.make_async_remote_copy(
            src_ref=full_refs[i].at[chip_of[j], half], dst_ref=full_refs[i].at[chip_of[j], half],
            send_sem=f_send.at[3 * i + j], recv_sem=f_recv.at[3 * i + j], device_id=sibling, device_id_type=MESH)

    pairs = [(i, j) for i in range(n) for j in range(3)]

    @pl.when(step == 0)
    def _():
        for cp in local:
            cp.start()
        for i, j in pairs:
            over_ici(i, j, chip).start()

    @pl.when(step == 3 * n_steps // 4)
    def _():
        for i, j in pairs:
            over_ici(i, j, chip_of[j]).wait_recv()
            to_sibling(i, j, core).start()

    @pl.when(step == n_steps - 1)
    def _():
        for i, j in pairs:
            to_sibling(i, j, 1 - core).wait_recv()
            to_sibling(i, j, core).wait_send()
            over_ici(i, j, chip).wait_send()
        for cp in local:
            cp.wait()


def _pre_call(x, pos_col, mod, b_ada, ng, inv128, wa, shards, seq):
    n_tok = x.shape[0]
    tm = min(TOKEN_TILE, seq)
    per_seq = seq // tm
    n_steps = n_tok // tm
    n = len(shards)

    def body(x_ref, pos_ref, mod_ref, bada_ref, ng_ref, inv_ref, wa_ref, *refs):
        w_refs, refs = refs[:n], refs[n:]
        zqkv_ref, zkr_ref, gates_ref, qs_ref, kd_ref, vd_ref, rope_ref = refs[:7]
        full_refs, sems = refs[7:7 + n], refs[7 + n:]
        _gather_in_steps(pl.program_id(0), n_steps, w_refs, full_refs, *sems)
        _rope_tables(pos_ref, inv_ref[...], rope_ref)
        xv = x_ref[...]
        modv = mod_ref[0] + bada_ref[...]
        shift, scale = modv[:, :D_MODEL], modv[:, D_MODEL:2 * D_MODEL]
        r1 = lax.rsqrt(jnp.mean(xv * xv, axis=-1, keepdims=True) + EPS)
        h = ((xv * r1) * ng_ref[...]) * (1.0 + scale) + shift
        za = _dot(h.astype(BF16), wa_ref[...])
        zqkv_ref[...] = za[:, :A_KR]
        zkr_ref[...] = za[:, A_KR:A_GM]
        gates_ref[:, :512] = za[:, A_GM:A_QS]
        gates_ref[:, 512:] = za[:, A_GS:A_END]
        qs_ref[...] = (za[:, A_QS:A_KS] * (SWA_SCALE * LOG2E)).astype(BF16)
        kd_ref[...] = _twice(za[:, A_KS:A_VS]).astype(BF16)
        vd_ref[...] = _twice(za[:, A_VS:A_GS]).astype(BF16)

    tok = lambda w: pl.BlockSpec((tm, w), lambda i: (i, 0))
    outs = [(640, F32), (HEAD_LANES, F32), (1024, F32), (512, BF16), (256, BF16), (256, BF16), (2 * HEAD_LANES, F32)]
    dma = pltpu.SemaphoreType.DMA
    return pl.pallas_call(
        body, name="pre", grid=(n_steps,),
        out_shape=[jax.ShapeDtypeStruct((n_tok, w), dt) for w, dt in outs]
        + [jax.ShapeDtypeStruct((4,) + s.shape, s.dtype) for s in shards],
        in_specs=[tok(D_MODEL), tok(1), pl.BlockSpec((1, 1, 3 * D_MODEL), lambda i: (i // per_seq, 0, 0)),
                  _full(b_ada.shape), _full(ng.shape), _full(inv128.shape), _full(wa.shape)] + [ANY_SPEC] * n,
        out_specs=[tok(w) for w, _ in outs] + [ANY_SPEC] * n,
        scratch_shapes=[dma((3 * n,)), dma((3 * n,)), dma((3 * n,)), dma((3 * n,)), dma((n,))],
        compiler_params=_params(1),
    )(x, pos_col, mod, b_ada, ng, inv128, wa, *shards)


def _up_call(zqkv, zkr, rope, qg, kvg, wq2, wkv, seq):
    n_tok = zqkv.shape[0]
    tm = min(TOKEN_TILE, seq)

    n_steps = n_tok // tm
    ring = 3

    def body(zqkv_hbm, zkr_hbm, rope_hbm, qg_ref, kvg_ref, wq_ref, wkv_ref, qf_ref, kf_ref, v_ref,
             zqkv_buf, zkr_buf, rope_buf, sems):
        i = pl.program_id(0)

        def fetch(step):
            static = isinstance(step, int)
            slot = step % ring if static else lax.rem(step, ring)
            rows = pl.ds(step * tm if static else pl.multiple_of(step * tm, tm), tm)
            pairs = [(zqkv_hbm, zqkv_buf), (zkr_hbm, zkr_buf), (rope_hbm, rope_buf)]
            return [pltpu.make_async_copy(src.at[rows], dst.at[slot], sems.at[k, slot]) for k, (src, dst) in enumerate(pairs)]

        @pl.when(i == 0)
        def _():
            for first in range(min(ring - 1, n_steps)):
                for cp in fetch(first):
                    cp.start()

        @pl.when(i + (ring - 1) < n_steps)
        def _():
            for cp in fetch(i + (ring - 1)):
                cp.start()

        for cp in fetch(i):
            cp.wait()
        slot = lax.rem(i, ring)
        zqkv_ref, zkr_ref, rope_ref = zqkv_buf.at[slot], zkr_buf.at[slot], rope_buf.at[slot]
        cos, sin = rope_ref[:, :HEAD_LANES], rope_ref[:, HEAD_LANES:]
        zq, zkv = zqkv_ref[:, A_ZQ:A_ZKV], zqkv_ref[:, A_ZKV:A_KR]
        rq = lax.rsqrt(jnp.mean(zq * zq, axis=-1, keepdims=True) + EPS)
        qn = ((zq * rq) * qg_ref[...]).astype(BF16)
        qr = _dot(qn, wq_ref[...])
        cf, sf = jnp.tile(cos, (1, N_HEADS)), jnp.tile(sin, (1, N_HEADS))
        qf_ref[...] = ((qr[:, :1024] * cf + qr[:, 1024:] * sf) * (MLA_SCALE * LOG2E)).astype(BF16)
        rkv = lax.rsqrt(jnp.mean(zkv * zkv, axis=-1, keepdims=True) + EPS)
        kvn = ((zkv * rkv) * kvg_ref[...]).astype(BF16)
        kv = _dot(kvn, wkv_ref[...])
        zkr = zkr_ref[...]
        kpe = jnp.where(_lane_lo(), 0.0, zkr * cos) + pltpu.roll(zkr, HALF, 1) * sin
        kf_ref[...] = (kv[:, :1024] + jnp.tile(kpe, (1, N_HEADS))).astype(BF16)
        v_ref[...] = kv[:, 1024:].astype(BF16)

    tok = lambda w: pl.BlockSpec((tm, w), lambda i: (i, 0))
    outs = [(1024, BF16), (1024, BF16), (512, BF16)]
    return pl.pallas_call(
        body, name="up", grid=(n_steps,),
        out_shape=[jax.ShapeDtypeStruct((n_tok, w), dt) for w, dt in outs],
        in_specs=[ANY_SPEC, ANY_SPEC, ANY_SPEC, _full(qg.shape), _full(kvg.shape), _full(wq2.shape), _full(wkv.shape)],
        out_specs=[tok(w) for w, _ in outs],
        scratch_shapes=[pltpu.VMEM((ring, tm, 640), F32), pltpu.VMEM((ring, tm, HEAD_LANES), F32),
                        pltpu.VMEM((ring, tm, 2 * HEAD_LANES), F32), pltpu.SemaphoreType.DMA((3, ring))],
        compiler_params=_params(1),
    )(zqkv, zkr, rope, qg, kvg, wq2, wkv)


def _lane_lo(width=HEAD_LANES):
    return lax.broadcasted_iota(jnp.int32, (1, width), 1) < HALF


def _eye(n=HEAD_LANES):
    r = lax.broadcasted_iota(jnp.int32, (n, n), 0)
    c = lax.broadcasted_iota(jnp.int32, (n, n), 1)
    return jnp.where(r == c, 1.0, 0.0).astype(BF16)


def _mla_fwd_call(qf, kf, v, n_seq, seq):
    tq = min(ATT_TILE, seq)
    nq = seq // tq

    ext = HALF + 16

    def body(q_ref, k_ref, v_ref, o_ref, lse_ref, vt_ref, acc_ref):
        i = pl.program_id(1)
        eye = _eye()

        @pl.when(i == 0)
        def _():
            for h in range(N_HEADS):
                vt_ref[h * ext + HALF:(h + 1) * ext, :] = jnp.ones((16, seq), BF16)
            for t in range(nq):
                for p in range(N_HEADS // 2):
                    pair = slice(p * HEAD_LANES, (p + 1) * HEAD_LANES)
                    v_t = _dot_nt(eye, v_ref[t * tq:(t + 1) * tq, pair]).astype(BF16)
                    for hh in range(2):
                        r0 = (2 * p + hh) * ext
                        vt_ref[r0:r0 + HALF, t * tq:(t + 1) * tq] = v_t[hh * HALF:(hh + 1) * HALF, :]

        q = q_ref[...]
        qcol = i * tq + lax.broadcasted_iota(jnp.int32, (1, tq), 1)
        heads = range(N_HEADS)
        lanes = [slice(h * HEAD_LANES, (h + 1) * HEAD_LANES) for h in heads]

        def make_step(masked, n_tiles):
            def step(kt0, carry):
                tiles = range(n_tiles)
                start = pl.multiple_of(kt0 * tq, tq)
                ks = [k_ref[pl.ds(pl.multiple_of((kt0 + t) * tq, tq), tq), :] for t in tiles]
                vt = vt_ref[:, pl.ds(start, n_tiles * tq)]
                last = n_tiles - 1
                if masked:
                    keep = ((kt0 + last) * tq + lax.broadcasted_iota(jnp.int32, (tq, 1), 0)) <= qcol

                def scores(h):
                    sts = [_dot_nt(ks[t][:, lanes[h]], q[:, lanes[h]]) for t in tiles]
                    if masked:
                        sts[last] = jnp.where(keep, sts[last], NEG)
                    return sts

                def softmax(h, sts):
                    m_old = carry[h]
                    m_new = m_old
                    for st in sts:
                        m_new = jnp.maximum(m_new, jnp.max(st, axis=0, keepdims=True))
                    pt = jnp.concatenate([jnp.exp2(st - m_new).astype(BF16) for st in sts], axis=0)
                    return m_new, jnp.exp2(m_old - m_new), pt

                def values(h, alpha, pt):
                    rows = slice(h * ext, (h + 1) * ext)
                    acc_ref[rows, :] = acc_ref[rows, :] * alpha + _dot(vt[rows, :], pt)

                sts, soft, out = {0: scores(0), 1: scores(1)}, {}, {}
                for h in range(N_HEADS + 1):
                    if h + 2 < N_HEADS:
                        sts[h + 2] = scores(h + 2)
                    if h < N_HEADS:
                        soft[h] = softmax(h, sts.pop(h))
                    if h >= 1:
                        m_new, alpha, pt = soft.pop(h - 1)
                        values(h - 1, alpha, pt)
                        out[h - 1] = m_new
                return tuple(out[h] for h in heads)
            return step

        acc_ref[...] = jnp.zeros_like(acc_ref)
        init = (jnp.full((1, tq), NEG, F32),) * N_HEADS
        count = i + 1
        carry = lax.fori_loop(0, (count + 1) // 2 - 1, lambda j, c: make_step(False, 2)(2 * j, c), init)
        carry = lax.cond(count % 2 == 0, lambda c: make_step(True, 2)(i - 1, c), lambda c: make_step(True, 1)(i, c), carry)
        dens = [acc_ref[h * ext + HALF:h * ext + HALF + 1, :] for h in heads]
        acc_t = jnp.concatenate([acc_ref[h * ext:h * ext + HALF, :] * (1.0 / dens[h]) for h in heads], axis=0)
        o_ref[...] = acc_t.T
        for h in heads:
            lse_ref[0, h // 4, h % 4:h % 4 + 1, :] = carry[h] + jnp.log2(dens[h])

    n_tok = qf.shape[0]
    return pl.pallas_call(
        body, name="mla_fwd", grid=(n_seq, nq),
        out_shape=[jax.ShapeDtypeStruct((n_tok, 512), F32), jax.ShapeDtypeStruct((n_seq, 2, 4, seq), F32)],
        in_specs=[pl.BlockSpec((tq, 1024), lambda b, i: (b * nq + i, 0)),
                  pl.BlockSpec((seq, 1024), lambda b, i: (b, 0)),
                  pl.BlockSpec((seq, 512), lambda b, i: (b, 0))],
        out_specs=[pl.BlockSpec((tq, 512), lambda b, i: (b * nq + i, 0)),
                   pl.BlockSpec((1, 2, 4, tq), lambda b, i: (b, 0, 0, i))],
        scratch_shapes=[pltpu.VMEM((N_HEADS * ext, seq), BF16), pltpu.VMEM((N_HEADS * ext, tq), F32)],
        compiler_params=_params(2),
    )(qf, kf, v)


def _mla_bwd_call(qf, kf, v, do, delta, lse, n_seq, seq):
    tq = min(ATT_TILE, seq)
    nq = seq // tq

    nh = 4
    heads = range(nh)
    lanes = [slice(h * HEAD_LANES, (h + 1) * HEAD_LANES) for h in heads]

    def body(q_ref, k_ref, v_ref, do_ref, dl_ref, lse_ref, dq_ref, dk_ref, dv_ref,
             kt_ref, dot_ref, dqt_ref, dvt_ref):
        eye = _eye()
        sub_lo = lax.broadcasted_iota(jnp.int32, (HEAD_LANES, 1), 0) < HALF

        for t in range(nq):
            r = slice(t * tq, (t + 1) * tq)
            kv = k_ref[r, :]
            for h in heads:
                kt_ref[lanes[h], r] = _dot_nt(eye, kv[:, lanes[h]]).astype(BF16)
            for p in range(nh // 2):
                dov = do_ref[r, lanes[p]]
                dt = _dot_nt(eye, dov)
                dot_ref[2 * p, :, r] = jnp.where(sub_lo, dt, 0.0).astype(BF16)
                dot_ref[2 * p + 1, :, r] = jnp.where(sub_lo, 0.0, dt).astype(BF16)
        dqt_ref[...] = jnp.zeros_like(dqt_ref)
        dvt_ref[...] = jnp.zeros_like(dvt_ref)

        def flush_dv(tile, which):
            rows = pl.ds(pl.multiple_of(tile * tq, tq), tq)
            for p in range(nh // 2):
                dv_ref[rows, lanes[p]] = dvt_ref[which, p * HEAD_LANES:(p + 1) * HEAD_LANES, :].T

        def k_step(kt, _):
            slot = kt % 2
            kr = pl.ds(pl.multiple_of(kt * tq, tq), tq)
            k = k_ref[kr, :]
            vv = v_ref[kr, :]
            k_t = kt_ref[:, kr]
            krow = kt * tq + lax.broadcasted_iota(jnp.int32, (tq, 1), 0)

            def make_step(masked, n_tiles):
                def q_step(qt0, carry):
                    tiles = range(n_tiles)
                    qrs = [pl.ds(pl.multiple_of((qt0 + t) * tq, tq), tq) for t in tiles]
                    if masked:
                        flush_dv(jnp.maximum(kt - 1, 0), 1 - slot)
                    qs = [q_ref[qr, :] for qr in qrs]
                    if masked:
                        keep = krow <= (qt0 * tq + lax.broadcasted_iota(jnp.int32, (1, tq), 1))

                    def scores(h):
                        do_ts = [dot_ref[h, :, qr] for qr in qrs]
                        sts = [_dot_nt(k[:, lanes[h]], qs[t][:, lanes[h]]) for t in tiles]
                        dpts = [_dot(vv[:, lanes[h // 2]], do_ts[t]) for t in tiles]
                        return do_ts, sts, dpts

                    def softmax(h, sts, dpts):
                        pts, dsts = [], []
                        for t in tiles:
                            pt = jnp.exp2(sts[t] - lse_ref[0, 0, h:h + 1, qrs[t]])
                            if masked and t == 0:
                                pt = jnp.where(keep, pt, 0.0)
                            dsts.append((pt * (dpts[t] - dl_ref[0, h:h + 1, qrs[t]])).astype(BF16))
                            pts.append(pt.astype(BF16))
                        return pts, dsts

                    def grads(h, do_ts, pts, dsts):
                        half = slice((h % 2) * HALF, (h % 2 + 1) * HALF)
                        dst_all = jnp.concatenate(dsts, axis=1)
                        pt_all = jnp.concatenate(pts, axis=1)
                        do_all = jnp.concatenate([do_ts[t][half, :] for t in tiles], axis=1)
                        q_all = jnp.concatenate([qs[t][:, lanes[h]] for t in tiles], axis=0)
                        dvt_ref[slot, h * HALF:(h + 1) * HALF, :] += _dot_nt(do_all, pt_all)
                        dk_ref[kr, lanes[h]] += _dot(dst_all, q_all)
                        for t in tiles:
                            dqt_ref[lanes[h], qrs[t]] += _dot(k_t[lanes[h], :], dsts[t])

                    first, second = {0: scores(0)}, {}
                    for h in range(nh + 1):
                        if h + 1 < nh:
                            first[h + 1] = scores(h + 1)
                        if h < nh:
                            do_ts, sts, dpts = first.pop(h)
                            second[h] = (do_ts,) + softmax(h, sts, dpts)
                        if h >= 1:
                            grads(h - 1, *second.pop(h - 1))
                    return carry
                return q_step

            dk_ref[kr, :] = jnp.zeros((tq, nh * HEAD_LANES), F32)
            dvt_ref[slot] = jnp.zeros(dvt_ref.shape[1:], F32)
            count = nq - kt
            lax.cond(count >= 2, lambda c: make_step(True, 2)(kt, c), lambda c: make_step(True, 1)(kt, c), 0)
            lax.fori_loop(1, count // 2, lambda j, c: make_step(False, 2)(kt + 2 * j, c), 0)
            lax.cond(jnp.logical_and(count % 2 == 1, count >= 3), lambda c: make_step(False, 1)(nq - 1, c), lambda c: c, 0)
            return 0

        lax.fori_loop(0, nq, k_step, 0)
        flush_dv(nq - 1, (nq - 1) % 2)
        for t in range(nq):
            r = slice(t * tq, (t + 1) * tq)
            for h in heads:
                dq_ref[r, lanes[h]] = dqt_ref[lanes[h], r].T

    n_tok = qf.shape[0]
    groups = N_HEADS // nh
    blk = lambda w: pl.BlockSpec((seq, w), lambda b, g: (b, g))
    return pl.pallas_call(
        body, name="mla_bwd", grid=(n_seq, groups),
        out_shape=[jax.ShapeDtypeStruct((n_tok, 1024), F32), jax.ShapeDtypeStruct((n_tok, 1024), F32),
                   jax.ShapeDtypeStruct((n_tok, 512), F32)],
        in_specs=[blk(512), blk(512), blk(256), blk(256), pl.BlockSpec((1, nh, seq), lambda b, g: (g, 0, b)),
                  pl.BlockSpec((1, 1, nh, seq), lambda b, g: (b, g, 0, 0))],
        out_specs=[blk(512), blk(512), blk(256)],
        scratch_shapes=[pltpu.VMEM((nh * HEAD_LANES, seq), BF16), pltpu.VMEM((nh, HEAD_LANES, seq), BF16),
                        pltpu.VMEM((nh * HEAD_LANES, seq), F32), pltpu.VMEM((2, nh * HALF, tq), F32)],
        compiler_params=_params(2),
    )(qf, kf, v, do, delta, lse)


SWA_BLOCKS = 4


def _swa_block(n, pos_col_ref, posq):
    w = SWA_WINDOW
    start = pl.multiple_of(jnp.maximum(n - 1, 0) * w, w)
    posk = pos_col_ref[pl.ds(start, 2 * w), :]
    rel = (n * w + lax.broadcasted_iota(jnp.int32, (1, w), 1)) - (start + lax.broadcasted_iota(jnp.int32, (2 * w, 1), 0))
    valid = jnp.logical_and(rel >= 0, rel < w)
    return start, jnp.where(valid, posq - posk, 1e30)


def _alibi(h):
    return LOG2E * 2.0 ** -(h + 1)


def _transpose_rows(eye, src_ref, dst_ref, seq, width):
    step = 2 * SWA_WINDOW
    for t in range(seq // step):
        for p in range(width // HEAD_LANES):
            lanes = slice(p * HEAD_LANES, (p + 1) * HEAD_LANES)
            dst_ref[lanes, t * step:(t + 1) * step] = _dot_nt(eye, src_ref[t * step:(t + 1) * step, lanes]).astype(BF16)


def _swa_fwd_call(qs, kd, vd, pos_col, pos_row, sinks, n_seq, seq):
    w = SWA_WINDOW
    qb = SWA_BLOCKS
    steps = seq // (qb * w)
    ext = HALF + 16

    def body(q_ref, k_ref, v_ref, pc_ref, pr_ref, sink_ref, o_ref, lse_ref, vt_ref):
        n = pl.program_id(1)
        lo = _lane_lo()
        hi = jnp.logical_not(lo)
        eye = _eye()

        @pl.when(n == 0)
        def _():
            step = 2 * w
            for kv in range(2):
                vt_ref[kv * ext + HALF:(kv + 1) * ext, :] = jnp.ones((16, seq), BF16)
                for t in range(seq // step):
                    v_t = _dot_nt(eye, v_ref[t * step:(t + 1) * step, kv * HEAD_LANES:(kv + 1) * HEAD_LANES])
                    vt_ref[kv * ext:kv * ext + HALF, t * step:(t + 1) * step] = v_t[:HALF, :].astype(BF16)

        heads = range(N_HEADS)
        blocks = range(qb)
        geo = [_swa_block(n * qb + bi, pc_ref, pr_ref[bi]) for bi in blocks]
        wins = [pl.ds(g[0], 2 * w) for g in geo]
        kwins = [k_ref[win, :] for win in wins]
        vts = [vt_ref[:, win] for win in wins]
        sts = []
        for bi in blocks:
            q = q_ref[bi * w:(bi + 1) * w, :]
            sts.append([])
            for j in range(N_HEADS // 2):
                qp = q[:, j * HEAD_LANES:(j + 1) * HEAD_LANES]
                both = jnp.concatenate([jnp.where(lo, qp, jnp.zeros_like(qp)), jnp.where(hi, qp, jnp.zeros_like(qp))], axis=0)
                st = _dot_nt(kwins[bi][:, (j // 2) * HEAD_LANES:(j // 2 + 1) * HEAD_LANES], both)
                sts[bi] += [st[:, :w], st[:, w:]]
        ps, ms = [], []
        for bi in blocks:
            ps.append([])
            ms.append([])
            for h in heads:
                s = sts[bi][h] - _alibi(h) * geo[bi][1]
                m = jnp.maximum(jnp.max(s, axis=0, keepdims=True), sink_ref[0, h] * LOG2E)
                ps[bi].append(jnp.exp2(s - m).astype(BF16))
                ms[bi].append(m)
        for bi in blocks:
            ots = []
            for h in heads:
                pv = _dot(vts[bi][(h // 4) * ext:(h // 4 + 1) * ext, :], ps[bi][h])
                l = pv[HALF:HALF + 1, :] + jnp.exp2(sink_ref[0, h] * LOG2E - ms[bi][h])
                ots.append(pv[:HALF, :] * (1.0 / l))
                lse_ref[0, h:h + 1, bi * w:(bi + 1) * w] = ms[bi][h] + jnp.log2(l)
            o_ref[bi * w:(bi + 1) * w, :] = jnp.concatenate(ots, axis=0).T

    n_tok = qs.shape[0]
    tok = lambda width: pl.BlockSpec((qb * w, width), lambda b, n: (b * steps + n, 0))
    whole = lambda width: pl.BlockSpec((seq, width), lambda b, n: (b, 0))
    return pl.pallas_call(
        body, name="swa_fwd", grid=(n_seq, steps),
        out_shape=[jax.ShapeDtypeStruct((n_tok, 512), F32), jax.ShapeDtypeStruct((n_seq, N_HEADS, seq), F32)],
        in_specs=[tok(512), whole(256), whole(256), whole(1), pl.BlockSpec((qb, 1, w), lambda b, n: (b * steps + n, 0, 0)),
                  pl.BlockSpec(memory_space=pltpu.SMEM)],
        out_specs=[tok(512), pl.BlockSpec((1, N_HEADS, qb * w), lambda b, n: (b, 0, n))],
        scratch_shapes=[pltpu.VMEM((2 * ext, seq), BF16)],
        compiler_params=_params(2),
    )(qs, kd, vd, pos_col, pos_row, sinks)


def _swa_bwd_call(qs, kd, vd, do, delta, lse, pos_col, pos_row, sinks, g_out, n_seq, seq):
    w = SWA_WINDOW
    qb = SWA_BLOCKS
    steps = seq // (qb * w)
    reduced, reduce_scratch = _reduce_operands(g_out)

    def body(q_ref, k_ref, v_ref, do_ref, dl_ref, lse_ref, pc_ref, pr_ref, sink_ref, g_ref, dq_ref, dk_ref, dv_ref,
             dsink_ref, f_ref, kt_ref, *reduce_refs):
        b, n = pl.program_id(0), pl.program_id(1)
        _grad_reduce(b * steps + n, n_seq * steps, g_ref, f_ref, *reduce_refs)
        lo = _lane_lo()
        hi = jnp.logical_not(lo)
        sub_lo = lax.broadcasted_iota(jnp.int32, (HEAD_LANES, 1), 0) < HALF
        eye = _eye()

        @pl.when(n == 0)
        def _():
            dk_ref[...] = jnp.zeros_like(dk_ref)
            dv_ref[...] = jnp.zeros_like(dv_ref)
            _transpose_rows(eye, k_ref, kt_ref, seq, 2 * HEAD_LANES)

        @pl.when(jnp.logical_and(n == 0, b == 0))
        def _():
            dsink_ref[...] = jnp.zeros_like(dsink_ref)

        heads = range(N_HEADS)
        blocks = range(qb)
        kv_lanes = lambda h: slice((h // 4) * HEAD_LANES, (h // 4 + 1) * HEAD_LANES)
        geo = [_swa_block(n * qb + bi, pc_ref, pr_ref[bi]) for bi in blocks]
        wins = [pl.ds(g[0], 2 * w) for g in geo]
        kwins = [k_ref[win, :] for win in wins]
        vwins = [v_ref[win, :] for win in wins]

        do_ts, deltas, qms, doms = [], [], [], []
        for bi in blocks:
            rows = slice(bi * w, (bi + 1) * w)
            for lst in (do_ts, deltas, qms, doms):
                lst.append([])
            for j in range(N_HEADS // 2):
                pair = slice(j * HEAD_LANES, (j + 1) * HEAD_LANES)
                dop = do_ref[rows, pair]
                qp = q_ref[rows, pair]
                dt = _dot_nt(eye, dop)
                for hh in range(2):
                    half = lo if hh == 0 else hi
                    do_ts[bi].append(jnp.where(sub_lo, dt, 0.0).astype(BF16) if hh == 0
                                     else jnp.where(sub_lo, 0.0, dt).astype(BF16))
                    deltas[bi].append(dl_ref[2 * j + hh:2 * j + hh + 1, rows])
                    qms[bi].append(jnp.where(half, qp, jnp.zeros_like(qp)))
                    doms[bi].append(jnp.where(half, dop, jnp.zeros_like(dop)))
        sts, dpts = [], []
        for bi in blocks:
            sts.append([])
            dpts.append([])
            for j in range(N_HEADS // 2):
                a, b = 2 * j, 2 * j + 1
                st = _dot_nt(kwins[bi][:, kv_lanes(a)], jnp.concatenate([qms[bi][a], qms[bi][b]], axis=0))
                dpt = _dot(vwins[bi][:, kv_lanes(a)], jnp.concatenate([do_ts[bi][a], do_ts[bi][b]], axis=1))
                sts[bi] += [st[:, :w], st[:, w:]]
                dpts[bi] += [dpt[:, :w], dpt[:, w:]]
        pts, dsts = [], []
        for bi in blocks:
            pts.append([])
            dsts.append([])
            for h in heads:
                lse_h = lse_ref[0, h:h + 1, bi * w:(bi + 1) * w]
                pt = jnp.exp2(sts[bi][h] - _alibi(h) * geo[bi][1] - lse_h)
                dsts[bi].append((pt * (dpts[bi][h] - deltas[bi][h])).astype(BF16))
                pts[bi].append(pt.astype(BF16))
                dsink_ref[h:h + 1, :] += -jnp.exp2(sink_ref[0, h] * LOG2E - lse_h) * deltas[bi][h]
        for bi in blocks:
            for kv in range(2):
                group = range(4 * kv, 4 * kv + 4)
                dst_all = jnp.concatenate([dsts[bi][h] for h in group], axis=1)
                pt_all = jnp.concatenate([pts[bi][h] for h in group], axis=1)
                q_all = jnp.concatenate([qms[bi][h] for h in group], axis=0)
                do_all = jnp.concatenate([doms[bi][h] for h in group], axis=0)
                dk_ref[wins[bi], kv_lanes(4 * kv)] += _dot(dst_all, q_all)
                dv_ref[wins[bi], kv_lanes(4 * kv)] += _dot(pt_all, do_all)
        for bi in blocks:
            ktw = kt_ref[:, wins[bi]]
            for j in range(N_HEADS // 2):
                k_t = ktw[kv_lanes(2 * j), :]
                both = _dot(k_t, jnp.concatenate([dsts[bi][2 * j], dsts[bi][2 * j + 1]], axis=1))
                dq_t = jnp.where(sub_lo, both[:, :w], both[:, w:])
                dq_ref[bi * w:(bi + 1) * w, j * HEAD_LANES:(j + 1) * HEAD_LANES] = dq_t.T * SWA_SCALE

    n_tok = qs.shape[0]
    tok = lambda width: pl.BlockSpec((qb * w, width), lambda b, n: (b * steps + n, 0))
    whole = lambda width: pl.BlockSpec((seq, width), lambda b, n: (b, 0))
    return pl.pallas_call(
        body, name="swa_bwd", grid=(n_seq, steps),
        out_shape=[jax.ShapeDtypeStruct((n_tok, 512), F32), jax.ShapeDtypeStruct((n_tok, 256), F32),
                   jax.ShapeDtypeStruct((n_tok, 256), F32), jax.ShapeDtypeStruct((N_HEADS, HEAD_LANES), F32), reduced],
        in_specs=[tok(512), whole(256), whole(256), pl.BlockSpec((qb * w, 512), lambda b, n: (b * steps + n, 1)),
                  pl.BlockSpec((N_HEADS, qb * w), lambda b, n: (0, b * steps + n)),
                  pl.BlockSpec((1, N_HEADS, qb * w), lambda b, n: (b, 0, n)),
                  whole(1), pl.BlockSpec((qb, 1, w), lambda b, n: (b * steps + n, 0, 0)),
                  pl.BlockSpec(memory_space=pltpu.SMEM), ANY_SPEC],
        out_specs=[tok(512), whole(256), whole(256), _full((N_HEADS, HEAD_LANES)), ANY_SPEC],
        scratch_shapes=[pltpu.VMEM((2 * HEAD_LANES, seq), BF16)] + reduce_scratch,
        compiler_params=_params(2),
    )(qs, kd, vd, do, delta, lse, pos_col, pos_row, sinks, g_out)


def _post_call(x, target, o_mla, o_swa, gates, mod, b_ada, fg, w_out, seq):
    n_tok = x.shape[0]
    tm = min(TOKEN_TILE, seq)
    per_seq = seq // tm
    n_seq = n_tok // seq

    def body(x_ref, t_ref, om_ref, os_ref, g_ref, mod_ref, bada_ref, fg_ref, w_ref,
             dx2_ref, do_ref, dg_ref, gw_ref, gfg_ref, dgate_ref, loss_ref, dmla_ref, dswa_ref):
        i = pl.program_id(0)

        @pl.when(i == 0)
        def _():
            gw_ref[...] = jnp.zeros_like(gw_ref)
            gfg_ref[...] = jnp.zeros_like(gfg_ref)
            loss_ref[...] = jnp.zeros_like(loss_ref)

        @pl.when(i % per_seq == 0)
        def _():
            dgate_ref[...] = jnp.zeros_like(dgate_ref)

        gate = mod_ref[0][:, 2 * D_MODEL:] + bada_ref[:, 2 * D_MODEL:]
        fgv = fg_ref[...]
        fgd = fgv * (1.0 / D_MODEL)
        subs = _sub_tiles(tm)
        gs = [g_ref[r, :] for r in subs]
        os_ = [jnp.concatenate([om_ref[r, :], os_ref[r, :]], axis=-1) for r in subs]
        sgs = [_sigmoid(g) for g in gs]
        sils = [g * sg for g, sg in zip(gs, sgs)]
        ypres = [(o * sil).astype(BF16) for o, sil in zip(os_, sils)]
        ys = [_dot(ypre, w_ref[...]) for ypre in ypres]
        dys, loss, gfg, dgate = [], 0.0, 0.0, 0.0
        for r, y in zip(subs, ys):
            x2 = x_ref[r, :] + gate * y
            r2 = lax.rsqrt(jnp.mean(x2 * x2, axis=-1, keepdims=True) + EPS)
            xn2 = x2 * r2
            err = xn2 * fgv - t_ref[r, :]
            loss = loss + jnp.sum(jnp.sum(err * err, axis=-1, keepdims=True), axis=0, keepdims=True)
            gfg = gfg + jnp.sum(err * xn2, axis=0, keepdims=True)
            dxn2 = err * fgd
            dx2 = r2 * (dxn2 - xn2 * jnp.mean(dxn2 * xn2, axis=-1, keepdims=True))
            dx2_ref[r, :] = dx2
            dgate = dgate + jnp.sum(dx2 * y, axis=0, keepdims=True)
            dys.append((dx2 * gate).astype(BF16))
        loss_ref[...] += jnp.broadcast_to(loss * (0.5 / D_MODEL), loss_ref.shape)
        gfg_ref[...] += gfg * (1.0 / D_MODEL)
        dgate_ref[0] += dgate
        gw_ref[...] += _dot_tn(jnp.concatenate(ypres, axis=0), jnp.concatenate(dys, axis=0))
        dypres = [_dot_nt(dy, w_ref[...]) for dy in dys]
        pick = jnp.where(jnp.right_shift(lax.broadcasted_iota(jnp.int32, (2 * N_HEADS, D_MODEL), 1), 6)
                         == lax.broadcasted_iota(jnp.int32, (2 * N_HEADS, D_MODEL), 0), 1.0, 0.0).astype(BF16)
        for r, dypre, o, g, sg, sil in zip(subs, dypres, os_, gs, sgs, sils):
            dov = (dypre * sil).astype(BF16)
            do_ref[r, :] = dov
            delta = _dot_nt(pick, (dov.astype(F32) * o).astype(BF16))
            for grp in range(2):
                dmla_ref[grp, :, r] = delta[4 * grp:4 * grp + 4, :]
            dswa_ref[:, r] = delta[N_HEADS:, :]
            dg_ref[r, :] = (dypre * o * (sg + sil * (1.0 - sg))).astype(BF16)

    tok = lambda w: pl.BlockSpec((tm, w), lambda i: (i, 0))
    per_b = pl.BlockSpec((1, 1, 3 * D_MODEL), lambda i: (i // per_seq, 0, 0))
    return pl.pallas_call(
        body, name="post", grid=(n_tok // tm,),
        out_shape=[jax.ShapeDtypeStruct((n_tok, D_MODEL), F32), jax.ShapeDtypeStruct((n_tok, D_MODEL), BF16),
                   jax.ShapeDtypeStruct((n_tok, D_MODEL), BF16), jax.ShapeDtypeStruct((D_MODEL, D_MODEL), F32),
                   jax.ShapeDtypeStruct((1, D_MODEL), F32), jax.ShapeDtypeStruct((n_seq, 1, D_MODEL), F32),
                   jax.ShapeDtypeStruct((1, HEAD_LANES), F32),
                   jax.ShapeDtypeStruct((2, N_HEADS // 2, n_tok), F32), jax.ShapeDtypeStruct((N_HEADS, n_tok), F32)],
        in_specs=[tok(D_MODEL), tok(D_MODEL), tok(512), tok(512), tok(D_MODEL), per_b, _full(b_ada.shape),
                  _full(fg.shape), _full(w_out.shape)],
        out_specs=[tok(D_MODEL), tok(D_MODEL), tok(D_MODEL), _full((D_MODEL, D_MODEL)), _full((1, D_MODEL)),
                   pl.BlockSpec((1, 1, D_MODEL), lambda i: (i // per_seq, 0, 0)), _full((1, HEAD_LANES)),
                   pl.BlockSpec((2, N_HEADS // 2, tm), lambda i: (0, 0, i)), pl.BlockSpec((N_HEADS, tm), lambda i: (0, i))],
        compiler_params=_params(1),
    )(x, target, o_mla, o_swa, gates, mod, b_ada, fg, w_out)


def _mid_bwd_call(dqf, dkf, dv, zqkv, rope, qg, kvg, wq2, wkv, seq):
    n_tok = dqf.shape[0]
    tm = min(TOKEN_TILE, seq)

    def body(dq_ref, dk_ref, dv_ref, z_ref, rope_ref, qg_ref, kvg_ref, wq_ref, wkv_ref,
             dz_ref, gwq_ref, gwkv_ref, gqg_ref, gkvg_ref):
        i = pl.program_id(0)

        @pl.when(i == 0)
        def _():
            gwq_ref[...] = jnp.zeros_like(gwq_ref)
            gwkv_ref[...] = jnp.zeros_like(gwkv_ref)
            gqg_ref[...] = jnp.zeros_like(gqg_ref)
            gkvg_ref[...] = jnp.zeros_like(gkvg_ref)

        cos, sin = rope_ref[:, :HEAD_LANES], rope_ref[:, HEAD_LANES:]
        cf, sf = jnp.tile(cos, (1, N_HEADS)), jnp.tile(sin, (1, N_HEADS))
        dq = dq_ref[...] * MLA_SCALE
        dqr = jnp.concatenate([dq * cf, dq * sf], axis=-1).astype(BF16)
        zq, zkv = z_ref[:, :Q_LORA], z_ref[:, Q_LORA:]
        qgv, kvgv = qg_ref[...], kvg_ref[...]

        rq = lax.rsqrt(jnp.mean(zq * zq, axis=-1, keepdims=True) + EPS)
        xq = zq * rq
        gwq_ref[...] += _dot_tn((xq * qgv).astype(BF16), dqr)
        dqn = _dot_nt(dqr, wq_ref[...])
        gqg_ref[...] += jnp.sum(dqn * xq, axis=0, keepdims=True)
        dxq = dqn * qgv
        dz_ref[:, :Q_LORA] = (rq * (dxq - xq * jnp.mean(dxq * xq, axis=-1, keepdims=True))).astype(BF16)

        dk = dk_ref[...] * LN2
        dkv = jnp.concatenate([dk, dv_ref[...]], axis=-1).astype(BF16)
        rkv = lax.rsqrt(jnp.mean(zkv * zkv, axis=-1, keepdims=True) + EPS)
        xkv = zkv * rkv
        gwkv_ref[...] += _dot_tn((xkv * kvgv).astype(BF16), dkv)
        dkvn = _dot_nt(dkv, wkv_ref[...])
        gkvg_ref[...] += jnp.sum(dkvn * xkv, axis=0, keepdims=True)
        dxkv = dkvn * kvgv
        dz_ref[:, Q_LORA:A_KR] = (rkv * (dxkv - xkv * jnp.mean(dxkv * xkv, axis=-1, keepdims=True))).astype(BF16)

        dkpe = dk[:, :HEAD_LANES]
        for h in range(1, N_HEADS):
            dkpe = dkpe + dk[:, h * HEAD_LANES:(h + 1) * HEAD_LANES]
        dz_ref[:, A_KR:] = (jnp.where(_lane_lo(), 0.0, dkpe * cos) + pltpu.roll(dkpe * sin, HALF, 1)).astype(BF16)

    tok = lambda w: pl.BlockSpec((tm, w), lambda i: (i, 0))
    return pl.pallas_call(
        body, name="mid_bwd", grid=(n_tok // tm,),
        out_shape=[jax.ShapeDtypeStruct((n_tok, A_GM), BF16),
                   jax.ShapeDtypeStruct(wq2.shape, F32), jax.ShapeDtypeStruct(wkv.shape, F32),
                   jax.ShapeDtypeStruct((1, Q_LORA), F32), jax.ShapeDtypeStruct((1, KV_LORA), F32)],
        in_specs=[tok(1024), tok(1024), tok(512), tok(640), tok(2 * HEAD_LANES), _full(qg.shape), _full(kvg.shape),
                  _full(wq2.shape), _full(wkv.shape)],
        out_specs=[tok(A_GM), _full(wq2.shape), _full(wkv.shape), _full((1, Q_LORA)), _full((1, KV_LORA))],
        compiler_params=_params(1),
    )(dqf, dkf, dv, zqkv, rope, qg, kvg, wq2, wkv)


def _in_bwd_call(x, dx2, dz, dg, dqs, dkd, dvd, mod, b_ada, ng, wa, seq):
    n_tok = x.shape[0]
    tm = min(TOKEN_TILE, seq)
    per_seq = seq // tm
    n_seq = n_tok // seq

    def body(x_ref, dx2_ref, dz_ref, dg_ref, dqs_ref, dkd_ref, dvd_ref, mod_ref, bada_ref, ng_ref,
             wa_ref, gx_ref, gwa_ref, gng_ref, dshift_ref, dscale_ref):
        i = pl.program_id(0)

        @pl.when(i == 0)
        def _():
            gwa_ref[...] = jnp.zeros_like(gwa_ref)
            gng_ref[...] = jnp.zeros_like(gng_ref)

        @pl.when(i % per_seq == 0)
        def _():
            dshift_ref[...] = jnp.zeros_like(dshift_ref)
            dscale_ref[...] = jnp.zeros_like(dscale_ref)

        xv = x_ref[...]
        modv = mod_ref[0] + bada_ref[...]
        shift, scale = modv[:, :D_MODEL], modv[:, D_MODEL:2 * D_MODEL]
        ngv = ng_ref[...]
        r1 = lax.rsqrt(jnp.mean(xv * xv, axis=-1, keepdims=True) + EPS)
        xn = xv * r1
        hb = ((xn * ngv) * (1.0 + scale) + shift).astype(BF16)

        dgv = dg_ref[...]
        pieces = [(A_ZQ, dz_ref[...]), (A_GM, dgv[:, :512]), (A_QS, dqs_ref[...].astype(BF16)),
                  (A_KS, jnp.concatenate([_once(dkd_ref[...]) * LN2, _once(dvd_ref[...])], axis=1).astype(BF16)),
                  (A_GS, dgv[:, 512:])]
        dh = None
        for off, piece in pieces:
            wd = piece.shape[1]
            gwa_ref[:, off:off + wd] += _dot_tn(hb, piece)
            term = _dot_nt(piece, wa_ref[:, off:off + wd])
            dh = term if dh is None else dh + term

        dshift_ref[0] += jnp.sum(dh, axis=0, keepdims=True)
        dscale_ref[0] += jnp.sum(dh * (xn * ngv), axis=0, keepdims=True)
        gng_ref[...] += jnp.sum(dh * xn * (1.0 + scale), axis=0, keepdims=True)
        dxn = dh * ngv * (1.0 + scale)
        gx_ref[...] = dx2_ref[...] + r1 * (dxn - xn * jnp.mean(dxn * xn, axis=-1, keepdims=True))

    tok = lambda w: pl.BlockSpec((tm, w), lambda i: (i, 0))
    per_b = lambda w: pl.BlockSpec((1, 1, w), lambda i: (i // per_seq, 0, 0))
    return pl.pallas_call(
        body, name="in_bwd", grid=(n_tok // tm,),
        out_shape=[jax.ShapeDtypeStruct((n_tok, D_MODEL), F32), jax.ShapeDtypeStruct((D_MODEL, A_END), F32),
                   jax.ShapeDtypeStruct((1, D_MODEL), F32),
                   jax.ShapeDtypeStruct((n_seq, 1, D_MODEL), F32), jax.ShapeDtypeStruct((n_seq, 1, D_MODEL), F32)],
        in_specs=[tok(D_MODEL), tok(D_MODEL), tok(A_GM), tok(D_MODEL), tok(512), tok(256), tok(256),
                  per_b(3 * D_MODEL), _full(b_ada.shape), _full(ng.shape), _full(wa.shape)],
        out_specs=[tok(D_MODEL), _full((D_MODEL, A_END)), _full((1, D_MODEL)), per_b(D_MODEL), per_b(D_MODEL)],
        compiler_params=_params(1),
    )(x, dx2, dz, dg, dqs, dkd, dvd, mod, b_ada, ng, wa)


def _adam_math(w, g, m, v):
    m_new = ADAM_B1 * m + (1.0 - ADAM_B1) * g
    v_new = ADAM_B2 * v + (1.0 - ADAM_B2) * (g * g)
    m_hat = m_new / (1.0 - ADAM_B1 ** ADAM_STEP)
    v_hat = v_new / (1.0 - ADAM_B2 ** ADAM_STEP)
    delta = -ADAM_LR * (m_hat / (jnp.sqrt(v_hat) + ADAM_EPS) + ADAM_WD * w)
    return delta, m_new, v_new


def _adam_call(name, w, g, m, v):
    rows, cols = w.shape
    tr = next((t for t in (256, 128) if rows % t == 0), rows)

    def body(w_ref, g_ref, m_ref, v_ref, d_ref, mo_ref, vo_ref):
        d, mn, vn = _adam_math(w_ref[...], g_ref[...], m_ref[...], v_ref[...])
        d_ref[...] = d
        mo_ref[...] = mn
        vo_ref[...] = vn

    spec = pl.BlockSpec((tr, cols), lambda i: (i, 0))
    return pl.pallas_call(
        body, name=name, grid=(rows // tr,),
        out_shape=[jax.ShapeDtypeStruct(w.shape, F32)] * 3,
        in_specs=[spec] * 4, out_specs=[spec] * 3,
        compiler_params=_params(1),
    )(w, g, m, v)


def _ada_bwd_call(act_all, dmod_cols, w, m, v):
    rows, cols = w.shape
    tr = 512

    def body(a_ref, dm_ref, w_ref, m_ref, v_ref, g_ref, d_ref, mo_ref, vo_ref):
        g = _dot_tn(a_ref[...].astype(BF16), dm_ref[...].astype(BF16))
        d, mn, vn = _adam_math(w_ref[...], g, m_ref[...], v_ref[...])
        g_ref[...] = g
        d_ref[...] = d
        mo_ref[...] = mn
        vo_ref[...] = vn

    spec = pl.BlockSpec((tr, cols), lambda i: (i, 0))
    nb = act_all.shape[0]
    return pl.pallas_call(
        body, name="ada_bwd", grid=(rows // tr,),
        out_shape=[jax.ShapeDtypeStruct(w.shape, F32)] * 4,
        in_specs=[pl.BlockSpec((nb, tr), lambda i: (0, i)), _full(dmod_cols.shape), spec, spec, spec],
        out_specs=[spec] * 4,
        compiler_params=_params(1),
    )(act_all, dmod_cols, w, m, v)


SMALL_ROW = {"norm_gain": (0, 1024), "final_gain": (1024, 2048), "q_norm_gain": (2048, 2432),
             "kv_norm_gain": (2432, 2688), "swa_sinks": (2688, 2696), "loss": (2816, 2944)}
SMALL_ORDER = ("b_ada", "norm_gain", "q_norm_gain", "kv_norm_gain", "swa_sinks", "final_gain")


def _small_call(parts_all, n_seq, params):
    k = len(params)

    def body(p_ref, *refs):
        ins, outs, loss_ref = refs[:3 * k], refs[3 * k:7 * k], refs[7 * k]
        row = p_ref[n_seq:n_seq + 1, :]
        for dv in range(1, 8):
            r0 = dv * ROWS_PER_DEVICE + n_seq
            row = row + p_ref[r0:r0 + 1, :]
        gb = None
        for dv in range(8):
            for r in range(n_seq):
                r0 = dv * ROWS_PER_DEVICE + r
                gb = p_ref[r0:r0 + 1, :] if gb is None else gb + p_ref[r0:r0 + 1, :]
        for j, name in enumerate(SMALL_ORDER):
            g = gb if name == "b_ada" else row[:, SMALL_ROW[name][0]:SMALL_ROW[name][1]]
            d, mn, vn = _adam_math(ins[3 * j][...], g, ins[3 * j + 1][...], ins[3 * j + 2][...])
            outs[4 * j][...] = g
            outs[4 * j + 1][...] = d
            outs[4 * j + 2][...] = mn
            outs[4 * j + 3][...] = vn
        loss_ref[...] = row[:, SMALL_ROW["loss"][0]:SMALL_ROW["loss"][1]]

    flat = [t for p in params for t in p]
    res = pl.pallas_call(
        body, name="small_update", grid=(1,),
        out_shape=[jax.ShapeDtypeStruct(p[0].shape, F32) for p in params for _ in range(4)]
        + [jax.ShapeDtypeStruct((1, HEAD_LANES), F32)],
        in_specs=[_full(parts_all.shape)] + [_full(t.shape) for t in flat],
        out_specs=[_full(p[0].shape) for p in params for _ in range(4)] + [_full((1, HEAD_LANES))],
        compiler_params=_params(1),
    )(parts_all, *flat)
    return [res[4 * j:4 * j + 4] for j in range(k)], res[4 * k]


def _rot(t):
    half = t.shape[-1] // 2
    return jnp.concatenate([-t[..., half:], t[..., :half]], axis=-1)


def _rot_t(g):
    half = g.shape[-1] // 2
    return jnp.concatenate([g[..., half:], -g[..., :half]], axis=-1)


def _columns(segments, lo, hi):
    out, at = [], 0
    for seg in segments:
        n = seg.shape[1]
        a, b = max(lo, at), min(hi, at + n)
        if a < b:
            out.append(seg[:, a - at:b - at])
        at += n
    return out


def _prepare_in(w_in_blocks):
    o = [0]
    for s in IN_SPLITS:
        o.append(o[-1] + s)
    part = lambda a, b: _columns(w_in_blocks, a, b)
    kr = jnp.concatenate(part(o[2], o[3]), axis=1)
    zero = jnp.zeros((kr.shape[0], 32), kr.dtype)
    return jnp.concatenate(part(0, o[2]) + [_rot(kr), zero, kr, zero] + part(o[3], o[8]), axis=1)


def _prepare_up(w_uq, w_ukv):
    uq = w_uq.reshape(Q_LORA, N_HEADS, MLA_NOPE + MLA_ROPE)
    zq = jnp.zeros((Q_LORA, N_HEADS, 32), w_uq.dtype)
    uq_full = jnp.concatenate([uq, zq], axis=-1).reshape(Q_LORA, 1024)
    uq_rot = jnp.concatenate([jnp.zeros((Q_LORA, N_HEADS, 64), w_uq.dtype), _rot(uq[..., MLA_NOPE:]), zq],
                             axis=-1).reshape(Q_LORA, 1024)
    wq2 = jnp.concatenate([uq_full, uq_rot], axis=1)
    ukv = w_ukv.reshape(KV_LORA, N_HEADS, 128)
    k_full = jnp.concatenate([ukv[..., :64], jnp.zeros((KV_LORA, N_HEADS, 64), w_ukv.dtype)], axis=-1).reshape(KV_LORA, 1024)
    wkv = jnp.concatenate([k_full, ukv[..., 64:].reshape(KV_LORA, 512)], axis=1)
    return wq2, wkv


def _restore_in(gwa):
    gkr = gwa[:, A_KR + 64:A_KR + 96] + _rot_t(gwa[:, A_KR:A_KR + 32])
    in_order = [gwa[:, :A_KR], gkr, gwa[:, A_GM:]]
    n = D_IN // 4
    return [jnp.concatenate(_columns(in_order, k * n, (k + 1) * n), axis=1) for k in range(4)]


def _restore_up(gwq2, gwkv):
    gf = gwq2[:, :1024].reshape(Q_LORA, N_HEADS, 128)
    gr = gwq2[:, 1024:].reshape(Q_LORA, N_HEADS, 128)
    g_uq = jnp.concatenate([gf[..., :64], gf[..., 64:96] + _rot_t(gr[..., 64:96])], axis=-1).reshape(Q_LORA, 768)
    gk = gwkv[:, :1024].reshape(KV_LORA, N_HEADS, 128)[..., :64]
    gv = gwkv[:, 1024:].reshape(KV_LORA, N_HEADS, 64)
    g_ukv = jnp.concatenate([gk, gv], axis=-1).reshape(KV_LORA, 1024)
    return g_uq, g_ukv


def _local_step(x, positions, target, mod_rows, b_ada, ng, qg, kvg, sinks, fg, w_in_b, later_shards):
    n_seq, seq, _ = x.shape
    n_tok = n_seq * seq
    x2d = x.reshape(n_tok, D_MODEL)
    t2d = target.reshape(n_tok, D_MODEL)
    pos_f = positions.astype(F32)
    pos_col = pos_f.reshape(n_tok, 1)
    pos_row = pos_f.reshape(n_tok // SWA_WINDOW, 1, SWA_WINDOW)
    mod3 = mod_rows.reshape(n_seq, 1, 3 * D_MODEL)
    inv = ROPE_THETA ** (-jnp.arange(0, MLA_ROPE, 2, dtype=F32) / MLA_ROPE)
    inv128 = jnp.tile(jnp.concatenate([inv, inv]), 4).reshape(1, HEAD_LANES)
    fg2 = fg.reshape(1, D_MODEL)

    wa = _prepare_in(w_in_b)
    zqkv, zkr, gates, qs, kd, vd, rope, f_uq, f_ukv, f_out = _pre_call(x2d, pos_col, mod3, b_ada, ng, inv128, wa,
                                                                       later_shards, seq)
    cols = lambda t, r: jnp.transpose(t.reshape(4, r, -1), (1, 0, 2)).reshape(r, -1)
    wq2, wkv = _prepare_up(cols(f_uq, Q_LORA), cols(f_ukv, KV_LORA))
    w_out_b = f_out.reshape(D_MODEL, D_MODEL)
    qf, kf, v = _up_call(zqkv, zkr, rope, qg, kvg, wq2, wkv, seq)
    o_mla, lse_mla = _mla_fwd_call(qf, kf, v, n_seq, seq)
    o_swa, lse_swa = _swa_fwd_call(qs, kd, vd, pos_col, pos_row, sinks, n_seq, seq)
    dx2, do, dg, g_out, g_fg, dgate, loss, delta_mla, delta_swa = _post_call(x2d, t2d, o_mla, o_swa, gates, mod3, b_ada, fg2, w_out_b, seq)
    dqf, dkf, dv = _mla_bwd_call(qf, kf, v, do, delta_mla, lse_mla, n_seq, seq)
    dqs, dkd, dvd, dsink, r_out = _swa_bwd_call(qs, kd, vd, do, delta_swa, lse_swa, pos_col, pos_row, sinks,
                                                g_out.reshape(4, 2, D_MODEL // 8, D_MODEL), n_seq, seq)
    dz, g_wq2, g_wkv, g_qg, g_kvg = _mid_bwd_call(dqf, dkf, dv, zqkv, rope, qg, kvg, wq2, wkv, seq)
    gx, g_wa, g_ng, dshift, dscale = _in_bwd_call(x2d, dx2, dz, dg, dqs, dkd, dvd, mod3, b_ada, ng, wa, seq)
    g_in = _restore_in(g_wa)
    g_uq, g_ukv = _restore_up(g_wq2, g_wkv)
    dmod = jnp.concatenate([dshift, dscale, dgate], axis=-1).reshape(n_seq, 3 * D_MODEL)
    small_row = jnp.concatenate([g_ng, g_fg, g_qg, g_kvg, jnp.pad(jnp.sum(dsink, axis=1).reshape(1, N_HEADS), ((0, 0), (0, 120))),
                                 loss, jnp.zeros((1, 128), F32)], axis=1)
    return gx.reshape(x.shape), (g_in, g_uq, g_ukv), r_out, small_row, dmod


def kernel(x, c, positions, w_ada, b_ada, norm_gain, w_in, q_norm_gain, kv_norm_gain, w_uq, w_ukv, swa_sinks, w_out, final_gain, loss_target, m_w_ada, m_b_ada, m_norm_gain, m_w_in, m_q_norm_gain, m_kv_norm_gain, m_w_uq, m_w_ukv, m_swa_sinks, m_w_out, m_final_gain, v_w_ada, v_b_ada, v_norm_gain, v_w_in, v_q_norm_gain, v_kv_norm_gain, v_w_uq, v_w_ukv, v_swa_sinks, v_w_out, v_final_gain):
    n_seq = x.shape[0]
    xi, yi, ci = lax.axis_index("x"), lax.axis_index("y"), lax.axis_index("c")
    dev = 4 * xi + 2 * yi + ci
    chip = 2 * xi + yi

    halves = lambda w: w.astype(BF16).reshape(2, w.shape[0] // 2, w.shape[1])
    c_blk = jnp.pad(c, ((0, ROWS_PER_DEVICE - n_seq), (0, 0)))
    act_all, pieces, f_in = _comm_fwd_call(c_blk, w_ada[0], [halves(w_in[0])])
    mine = lax.dynamic_slice_in_dim(pieces, dev * ROWS_PER_DEVICE, n_seq, axis=1)
    mod_rows = jnp.transpose(mine, (1, 0, 2)).reshape(n_seq, 3 * D_MODEL)
    w_in_blocks = [f_in[k].reshape(D_MODEL, -1) for k in range(4)]

    gx, (g_in_blocks, g_uq, g_ukv), r_out, small_row, dmod = _local_step(
        x, positions, loss_target, mod_rows, b_ada, norm_gain, q_norm_gain, kv_norm_gain, swa_sinks, final_gain,
        w_in_blocks, [halves(w_uq[0]), halves(w_ukv[0]), halves(w_out[0])])

    grads = [jnp.stack(g_in_blocks).reshape(4, 2, D_MODEL // 2, -1), _by_owner(g_uq, g_uq.shape[1] // 4),
             _by_owner(g_ukv, g_ukv.shape[1] // 4)]
    part = jnp.concatenate([dmod, small_row, jnp.zeros((ROWS_PER_DEVICE - n_seq - 1, 3 * D_MODEL), F32)], axis=0)
    r_in, r_uq, r_ukv, parts_all = _comm_bwd_call(grads, part)
    g_in_s, g_uq_s = r_in.reshape(w_in.shape[1:]), r_uq.reshape(w_uq.shape[1:])
    g_ukv_s, g_out_s = r_ukv.reshape(w_ukv.shape[1:]), r_out.reshape(w_out.shape[1:])

    tr = lambda a: jnp.swapaxes(a[0], 0, 1)
    back = lambda ts: [jnp.swapaxes(t, 0, 1) for t in ts]
    d_in, nm_in, nv_in = back(_adam_call("adam_w_in", tr(w_in), g_in_s.T, tr(m_w_in), tr(v_w_in)))
    d_uq, nm_uq, nv_uq = back(_adam_call("adam_w_uq", tr(w_uq), g_uq_s.T, tr(m_w_uq), tr(v_w_uq)))
    d_ukv, nm_ukv, nv_ukv = _adam_call("adam_w_ukv", w_ukv[0], g_ukv_s, m_w_ukv[0], v_w_ukv[0])
    d_out, nm_out, nv_out = _adam_call("adam_w_out", w_out[0], g_out_s, m_w_out[0], v_w_out[0])
    dmod_cols = lax.dynamic_slice_in_dim(parts_all, chip * 768, 768, axis=1)
    g_ada, d_ada, nm_ada, nv_ada = _ada_bwd_call(act_all, dmod_cols, w_ada[0], m_w_ada[0], v_w_ada[0])

    row = lambda t: t.reshape(1, -1)
    small = {"b_ada": (b_ada, m_b_ada, v_b_ada), "norm_gain": (norm_gain, m_norm_gain, v_norm_gain),
             "q_norm_gain": (q_norm_gain, m_q_norm_gain, v_q_norm_gain),
             "kv_norm_gain": (kv_norm_gain, m_kv_norm_gain, v_kv_norm_gain),
             "swa_sinks": (swa_sinks, m_swa_sinks, v_swa_sinks),
             "final_gain": (row(final_gain), row(m_final_gain), row(v_final_gain))}
    res, loss_row = _small_call(parts_all, n_seq, [small[name] for name in SMALL_ORDER])
    res = dict(zip(SMALL_ORDER, res))
    res["final_gain"] = [t.reshape(-1) for t in res["final_gain"]]
    e = lambda t: t[None]
    big = {"w_ada": (e(g_ada), e(d_ada), e(nm_ada), e(nv_ada)), "w_in": (e(g_in_s), e(d_in), e(nm_in), e(nv_in)),
           "w_uq": (e(g_uq_s), e(d_uq), e(nm_uq), e(nv_uq)), "w_ukv": (e(g_ukv_s), e(d_ukv), e(nm_ukv), e(nv_ukv)),
           "w_out": (e(g_out_s), e(d_out), e(nm_out), e(nv_out))}
    order = ("w_ada", "b_ada", "norm_gain", "w_in", "q_norm_gain", "kv_norm_gain", "w_uq", "w_ukv", "swa_sinks", "w_out",
             "final_gain")
    pick = lambda kind: [(big[n] if n in big else res[n])[kind] for n in order]
    return (loss_row[0, 0], gx, *pick(0), *pick(1), *pick(2), *pick(3))
```

```python
import jax
import jax.numpy as jnp
from jax import lax
from jax.experimental import pallas as pl
from jax.experimental.pallas import tpu as pltpu

F32 = jnp.float32
BF16 = jnp.bfloat16

D_MODEL = 1024
Q_LORA = 384
KV_LORA = 256
N_HEADS = 8
MLA_NOPE = 64
MLA_ROPE = 32
HEAD_LANES = 128
HALF = 64
SWA_WINDOW = 128
EPS = 1e-6
ROPE_THETA = 10000.0
MLA_SCALE = (MLA_NOPE + MLA_ROPE) ** -0.5
LOG2E = 1.4426950408889634
LN2 = 0.6931471805599453
SWA_SCALE = 64 ** -0.5
NEG = -1e30

ADAM_LR = 0.001
ADAM_B1 = 0.9
ADAM_B2 = 0.999
ADAM_EPS = 1e-08
ADAM_WD = 0.01
ADAM_STEP = 10

A_ZQ, A_ZKV, A_KR, A_GM, A_QS, A_KS, A_VS, A_GS, A_END = 0, 384, 640, 768, 1280, 1792, 1920, 2048, 2560
IN_SPLITS = (384, 256, 32, 512, 512, 128, 128, 512)
D_IN = sum(IN_SPLITS)

TOKEN_TILE = 512
ATT_TILE = 256
VMEM_LIMIT = 56 * 1024 * 1024


def _dot(a, b):
    return jnp.dot(a, b, preferred_element_type=F32)


def _dot_nt(a, b):
    return lax.dot_general(a, b, (((1,), (1,)), ((), ())), preferred_element_type=F32)


def _dot_tn(a, b):
    return lax.dot_general(a, b, (((0,), (0,)), ((), ())), preferred_element_type=F32)


def _params(n_grid):
    return pltpu.CompilerParams(dimension_semantics=("arbitrary",) * n_grid, vmem_limit_bytes=VMEM_LIMIT)


def _full(shape):
    nd = len(shape)
    return pl.BlockSpec(shape, lambda *_: (0,) * nd, pipeline_mode=pl.Buffered(1))


def _sigmoid(g):
    return 1.0 / (1.0 + jnp.exp(-g))


SUB_TILE = 256


def _sub_tiles(tm):
    sub = min(SUB_TILE, tm)
    return [slice(s * sub, (s + 1) * sub) for s in range(tm // sub)]


MESH = pl.DeviceIdType.MESH
ROWS_PER_DEVICE = 8
VMEM_SPEC = pl.BlockSpec(memory_space=pltpu.VMEM)
ANY_SPEC = pl.BlockSpec(memory_space=pl.ANY)


def _position():
    x, y, c = lax.axis_index("x"), lax.axis_index("y"), lax.axis_index("c")
    sibling = (x, y, 1 - c)
    others = [(1 - x, y, c), (x, 1 - y, c), (1 - x, 1 - y, c)]
    return (x, y, c), 4 * x + 2 * y + c, 2 * x + y, sibling, others


def _rows_of(dev):
    return pl.ds(pl.multiple_of(dev * ROWS_PER_DEVICE, ROWS_PER_DEVICE), ROWS_PER_DEVICE)


def _all_to_all_rows(block_ref, table_ref, dev, me, send_sems, recv_sems):
    x, y, c = me
    waits = []
    for k in range(1, 8):
        peer = (1 - x if k & 4 else x, 1 - y if k & 2 else y, 1 - c if k & 1 else c)
        pltpu.make_async_remote_copy(src_ref=block_ref, dst_ref=table_ref.at[_rows_of(dev)], send_sem=send_sems.at[k - 1],
                                     recv_sem=recv_sems.at[k - 1], device_id=peer, device_id_type=MESH).start()
        waits.append(pltpu.make_async_remote_copy(
            src_ref=block_ref, dst_ref=table_ref.at[_rows_of(jnp.bitwise_xor(dev, k))], send_sem=send_sems.at[k - 1],
            recv_sem=recv_sems.at[k - 1], device_id=peer, device_id_type=MESH))
    return waits


def _comm_fwd_call(c_blk, w_ada, shards):
    n = len(shards)

    def body(c_ref, wada_ref, *refs):
        w_refs, act_ref, pieces_ref, full_refs = refs[:n], refs[n], refs[n + 1], refs[n + 2:2 * n + 2]
        c_all_ref = refs[2 * n + 2]
        c_send, c_recv, p_send, p_recv, w_send, w_recv, f_send, f_recv, loc_sem = refs[2 * n + 3:]
        me, dev, chip, sibling, others = _position()
        core = me[2]
        chip_of = [2 * p[0] + p[1] for p in others]

        local = [pltpu.make_async_copy(w_refs[i], full_refs[i].at[chip], loc_sem.at[i]) for i in range(n)]
        for cp in local:
            cp.start()

        def over_ici(i, j, src_chip):
            return pltpu.make_async_remote_copy(
                src_ref=w_refs[i].at[core], dst_ref=full_refs[i].at[src_chip, core], send_sem=w_send.at[3 * i + j],
                recv_sem=w_recv.at[3 * i + j], device_id=others[j], device_id_type=MESH)

        def to_sibling(i, j, half):
            return pltpu.make_async_remote_copy(
                src_ref=full_refs[i].at[chip_of[j], half], dst_ref=full_refs[i].at[chip_of[j], half],
                send_sem=f_send.at[3 * i + j], recv_sem=f_recv.at[3 * i + j], device_id=sibling, device_id_type=MESH)

        c_all_ref[_rows_of(dev), :] = c_ref[...]
        c_waits = _all_to_all_rows(c_ref, c_all_ref, dev, me, c_send, c_recv)
        sent = [over_ici(i, j, chip) for i in range(n) for j in range(3)]
        for cp in sent:
            cp.start()

        for cp in c_waits:
            cp.wait()
        cv = c_all_ref[...]
        act = cv * _sigmoid(cv)
        act_ref[...] = act
        pieces_ref[chip] = _dot(act.astype(BF16), wada_ref[...].astype(BF16))
        piece = lambda j, src_chip: pltpu.make_async_remote_copy(
            src_ref=pieces_ref.at[chip], dst_ref=pieces_ref.at[src_chip], send_sem=p_send.at[j], recv_sem=p_recv.at[j],
            device_id=others[j], device_id_type=MESH)
        for j in range(3):
            piece(j, chip).start()

        for i in range(n):
            for j in range(3):
                over_ici(i, j, chip_of[j]).wait_recv()
                to_sibling(i, j, core).start()
        for j in range(3):
            piece(j, chip).wait_send()
            piece(j, chip_of[j]).wait_recv()
        for i in range(n):
            for j in range(3):
                to_sibling(i, j, 1 - core).wait_recv()
                to_sibling(i, j, core).wait_send()
        for cp in sent:
            cp.wait_send()
        for cp in local:
            cp.wait()

    rows = 8 * ROWS_PER_DEVICE
    dma = pltpu.SemaphoreType.DMA
    return pl.pallas_call(
        body, name="comm_fwd",
        out_shape=[jax.ShapeDtypeStruct((rows, D_MODEL), F32), jax.ShapeDtypeStruct((4, rows, w_ada.shape[1]), F32)]
        + [jax.ShapeDtypeStruct((4,) + s.shape, s.dtype) for s in shards],
        in_specs=[VMEM_SPEC, VMEM_SPEC] + [ANY_SPEC] * n,
        out_specs=[VMEM_SPEC, VMEM_SPEC] + [ANY_SPEC] * n,
        scratch_shapes=[pltpu.VMEM((rows, D_MODEL), F32), dma((7,)), dma((7,)), dma((3,)), dma((3,)),
                        dma((3 * n,)), dma((3 * n,)), dma((3 * n,)), dma((3 * n,)), dma((n,))],
        compiler_params=pltpu.CompilerParams(vmem_limit_bytes=VMEM_LIMIT),
    )(c_blk, w_ada, *shards)


def _comm_bwd_call(grads, part):
    n = len(grads)

    def body(part_ref, *refs):
        g_refs, f_refs, parts_ref = refs[:n], refs[n:2 * n], refs[2 * n]
        scratch = refs[2 * n + 1:]
        a_refs, b_refs, p_refs, r_refs = (scratch[k * n:(k + 1) * n] for k in range(4))
        s_send, s_recv, d_send, d_recv, e_send, e_recv, h_send, h_recv, loc_sem = scratch[4 * n:]
        me, dev, chip, sibling, others = _position()
        core = me[2]
        chip_of = [2 * p[0] + p[1] for p in others]

        parts_ref[_rows_of(dev), :] = part_ref[...]
        s_waits = _all_to_all_rows(part_ref, parts_ref, dev, me, s_send, s_recv)

        mine = [pltpu.make_async_copy(g_refs[i].at[:, core], a_refs[i], loc_sem.at[i]) for i in range(n)]
        swap = [pltpu.make_async_remote_copy(src_ref=g_refs[i].at[:, 1 - core], dst_ref=b_refs[i], send_sem=d_send.at[i],
                                             recv_sem=d_recv.at[i], device_id=sibling, device_id_type=MESH) for i in range(n)]
        order = sorted(range(n), key=lambda i: g_refs[i].shape[2] * g_refs[i].shape[3])
        for i in order:
            mine[i].start()
            swap[i].start()
        cross = [pltpu.make_async_remote_copy(src_ref=p_refs[i].at[chip_of[j]], dst_ref=r_refs[i].at[j],
                                              send_sem=e_send.at[3 * i + j], recv_sem=e_recv.at[3 * i + j],
                                              device_id=others[j], device_id_type=MESH) for i in range(n) for j in range(3)]
        for i in order:
            mine[i].wait()
            swap[i].wait()
            for k in range(4):
                s = a_refs[i][k] + b_refs[i][k]
                a_refs[i][k] = s
                p_refs[i][k] = s.astype(BF16)
            for j in range(3):
                cross[3 * i + j].start()
        share = {}
        for i in order:
            for j in range(3):
                cross[3 * i + j].wait()
            f_refs[i][core] = (a_refs[i][chip] + r_refs[i][0].astype(F32) + r_refs[i][1].astype(F32)
                               + r_refs[i][2].astype(F32))
            share[i] = pltpu.make_async_remote_copy(src_ref=f_refs[i].at[core], dst_ref=f_refs[i].at[core],
                                                    send_sem=h_send.at[i], recv_sem=h_recv.at[i], device_id=sibling,
                                                    device_id_type=MESH)
            share[i].start()
        for i in range(n):
            share[i].wait_send()
            pltpu.make_async_remote_copy(src_ref=f_refs[i].at[core], dst_ref=f_refs[i].at[1 - core], send_sem=h_send.at[i],
                                         recv_sem=h_recv.at[i], device_id=sibling, device_id_type=MESH).wait_recv()
        for cp in s_waits:
            cp.wait()

    rows = 8 * ROWS_PER_DEVICE
    dma = pltpu.SemaphoreType.DMA
    quarter = [(4,) + g.shape[2:] for g in grads]
    return pl.pallas_call(
        body, name="comm_bwd",
        out_shape=[jax.ShapeDtypeStruct((2,) + g.shape[2:], F32) for g in grads]
        + [jax.ShapeDtypeStruct((rows, part.shape[1]), F32)],
        in_specs=[VMEM_SPEC] + [ANY_SPEC] * n,
        out_specs=[VMEM_SPEC] * (n + 1),
        scratch_shapes=[pltpu.VMEM(q, F32) for q in quarter] + [pltpu.VMEM(q, F32) for q in quarter]
        + [pltpu.VMEM(q, BF16) for q in quarter] + [pltpu.VMEM((3,) + q[1:], BF16) for q in quarter]
        + [dma((7,)), dma((7,)), dma((n,)), dma((n,)), dma((3 * n,)), dma((3 * n,)), dma((n,)), dma((n,)), dma((n,))],
        compiler_params=pltpu.CompilerParams(vmem_limit_bytes=VMEM_LIMIT),
    )(part, *grads)


def _by_owner(g, n):
    return jnp.transpose(g.reshape(g.shape[0], 4, n), (1, 0, 2)).reshape(4, 2, g.shape[0] // 2, n)


def _reduce_operands(g):
    quarter = (4,) + g.shape[2:]
    dma = pltpu.SemaphoreType.DMA
    scratch = [pltpu.VMEM(quarter, F32), pltpu.VMEM(quarter, F32), pltpu.VMEM(quarter, BF16),
               pltpu.VMEM((3,) + quarter[1:], BF16), dma((5,)), dma((5,)), dma((2,))]
    return jax.ShapeDtypeStruct((2,) + g.shape[2:], F32), scratch


def _grad_reduce(step, n_steps, g_ref, f_ref, a_ref, b_ref, p_ref, r_ref, send, recv, loc_sem):
    me, _, chip, sibling, others = _position()
    core = me[2]
    chip_of = [2 * p[0] + p[1] for p in others]
    remote = lambda src, dst, k, to: pltpu.make_async_remote_copy(
        src_ref=src, dst_ref=dst, send_sem=send.at[k], recv_sem=recv.at[k], device_id=to, device_id_type=MESH)
    mine = pltpu.make_async_copy(g_ref.at[:, core], a_ref, loc_sem.at[0])
    swap = remote(g_ref.at[:, 1 - core], b_ref, 0, sibling)
    cross = [remote(p_ref.at[chip_of[j]], r_ref.at[j], 1 + j, others[j]) for j in range(3)]
    total_ref = b_ref.at[0]
    keep = pltpu.make_async_copy(total_ref, f_ref.at[core], loc_sem.at[1])
    share = lambda half: remote(total_ref, f_ref.at[half], 4, sibling)
    at = [k * (n_steps - 1) // 3 for k in range(4)]

    @pl.when(step == at[0])
    def _():
        mine.start()
        swap.start()

    @pl.when(step == at[1])
    def _():
        mine.wait()
        swap.wait()
        for k in range(4):
            s = a_ref[k] + b_ref[k]
            a_ref[k] = s
            p_ref[k] = s.astype(BF16)
        for cp in cross:
            cp.start()

    @pl.when(step == at[2])
    def _():
        for cp in cross:
            cp.wait()
        total_ref[...] = a_ref[chip] + r_ref[0].astype(F32) + r_ref[1].astype(F32) + r_ref[2].astype(F32)
        keep.start()
        share(core).start()

    @pl.when(step == at[3])
    def _():
        keep.wait()
        share(core).wait_send()
        share(1 - core).wait_recv()


def _twice(t):
    lo = _lane_lo()
    other = pltpu.roll(t, HALF, 1)
    return jnp.concatenate([jnp.where(lo, t, other), jnp.where(lo, other, t)], axis=1)


def _once(g):
    first, second = g[:, :HEAD_LANES], g[:, HEAD_LANES:]
    return jnp.where(_lane_lo(), first + pltpu.roll(first, HALF, 1), second + pltpu.roll(second, HALF, 1))


def _rope_tables(pos_ref, inv_row, rope_ref):
    quarter = pos_ref.shape[0] // 4
    lane = lax.broadcasted_iota(jnp.int32, (1, HEAD_LANES), 1)
    pos = [pos_ref[g * quarter:(g + 1) * quarter, :] for g in range(4)]
    ang = jnp.where(lane < 32, pos[0], jnp.where(lane < 64, pos[1], jnp.where(lane < 96, pos[2], pos[3]))) * inv_row
    cos, sin = jnp.cos(ang), jnp.sin(ang)
    rope_lanes = jnp.logical_and(lane >= HALF, lane < HALF + MLA_ROPE)
    for g in range(4):
        rows = slice(g * quarter, (g + 1) * quarter)
        shift = (HALF - 32 * g) % HEAD_LANES
        at = lambda t: t if shift == 0 else pltpu.roll(t, shift, 1)
        rope_ref[rows, :HEAD_LANES] = jnp.where(rope_lanes, at(cos), 1.0)
        rope_ref[rows, HEAD_LANES:] = jnp.where(rope_lanes, at(sin), 0.0)


def _gather_in_steps(step, n_steps, w_refs, full_refs, w_send, w_recv, f_send, f_recv, loc_sem):
    me, _, chip, sibling, others = _position()
    core = me[2]
    chip_of = [2 * p[0] + p[1] for p in others]
    n = len(w_refs)
    local = [pltpu.make_async_copy(w_refs[i], full_refs[i].at[chip], loc_sem.at[i]) for i in range(n)]

    def over_ici(i, j, src_chip):
        return pltpu.make_async_remote_copy(
            src_ref=w_refs[i].at[core], dst_ref=full_refs[i].at[src_chip, core], send_sem=w_send.at[3 * i + j],
            recv_sem=w_recv.at[3 * i + j], device_id=others[j], device_id_type=MESH)

    def to_sibling(i, j, half):
        return pltpu.make_async_remote_copy(
            src_ref=full_refs[i].at[chip_of[j], half], dst_ref=full_refs[i].at[chip_of[j], half],
            send_sem=f_send.at[3 * i + j], recv_sem=f_recv.at[3 * i + j], device_id=sibling, device_id_type=MESH)

    pairs = [(i, j) for i in range(n) for j in range(3)]

    @pl.when(step == 0)
    def _():
        for cp in local:
            cp.start()
        for i, j in pairs:
            over_ici(i, j, chip).start()

    @pl.when(step == 3 * n_steps // 4)
    def _():
        for i, j in pairs:
            over_ici(i, j, chip_of[j]).wait_recv()
            to_sibling(i, j, core).start()

    @pl.when(step == n_steps - 1)
    def _():
        for i, j in pairs:
            to_sibling(i, j, 1 - core).wait_recv()
            to_sibling(i, j, core).wait_send()
            over_ici(i, j, chip).wait_send()
        for cp in local:
            cp.wait()


def _pre_call(x, pos_col, mod, b_ada, ng, inv128, wa, shards, seq):
    n_tok = x.shape[0]
    tm = min(TOKEN_TILE, seq)
    per_seq = seq // tm
    n_steps = n_tok // tm
    n = len(shards)

    def body(x_ref, pos_ref, mod_ref, bada_ref, ng_ref, inv_ref, wa_ref, *refs):
        w_refs, refs = refs[:n], refs[n:]
        zqkv_ref, zkr_ref, gates_ref, qs_ref, kd_ref, vd_ref, rope_ref = refs[:7]
        full_refs, sems = refs[7:7 + n], refs[7 + n:]
        _gather_in_steps(pl.program_id(0), n_steps, w_refs, full_refs, *sems)
        _rope_tables(pos_ref, inv_ref[...], rope_ref)
        xv = x_ref[...]
        modv = mod_ref[0] + bada_ref[...]
        shift, scale = modv[:, :D_MODEL], modv[:, D_MODEL:2 * D_MODEL]
        r1 = lax.rsqrt(jnp.mean(xv * xv, axis=-1, keepdims=True) + EPS)
        h = ((xv * r1) * ng_ref[...]) * (1.0 + scale) + shift
        za = _dot(h.astype(BF16), wa_ref[...])
        zqkv_ref[...] = za[:, :A_KR]
        zkr_ref[...] = za[:, A_KR:A_GM]
        gates_ref[:, :512] = za[:, A_GM:A_QS]
        gates_ref[:, 512:] = za[:, A_GS:A_END]
        qs_ref[...] = (za[:, A_QS:A_KS] * (SWA_SCALE * LOG2E)).astype(BF16)
        kd_ref[...] = _twice(za[:, A_KS:A_VS]).astype(BF16)
        vd_ref[...] = _twice(za[:, A_VS:A_GS]).astype(BF16)

    tok = lambda w: pl.BlockSpec((tm, w), lambda i: (i, 0))
    outs = [(640, F32), (HEAD_LANES, F32), (1024, F32), (512, BF16), (256, BF16), (256, BF16), (2 * HEAD_LANES, F32)]
    dma = pltpu.SemaphoreType.DMA
    return pl.pallas_call(
        body, name="pre", grid=(n_steps,),
        out_shape=[jax.ShapeDtypeStruct((n_tok, w), dt) for w, dt in outs]
        + [jax.ShapeDtypeStruct((4,) + s.shape, s.dtype) for s in shards],
        in_specs=[tok(D_MODEL), tok(1), pl.BlockSpec((1, 1, 3 * D_MODEL), lambda i: (i // per_seq, 0, 0)),
                  _full(b_ada.shape), _full(ng.shape), _full(inv128.shape), _full(wa.shape)] + [ANY_SPEC] * n,
        out_specs=[tok(w) for w, _ in outs] + [ANY_SPEC] * n,
        scratch_shapes=[dma((3 * n,)), dma((3 * n,)), dma((3 * n,)), dma((3 * n,)), dma((n,))],
        compiler_params=_params(1),
    )(x, pos_col, mod, b_ada, ng, inv128, wa, *shards)


def _up_call(zqkv, zkr, rope, qg, kvg, wq2, wkv, seq):
    n_tok = zqkv.shape[0]
    tm = min(TOKEN_TILE, seq)

    n_steps = n_tok // tm
    ring = 3

    def body(zqkv_hbm, zkr_hbm, rope_hbm, qg_ref, kvg_ref, wq_ref, wkv_ref, qf_ref, kf_ref, v_ref,
             zqkv_buf, zkr_buf, rope_buf, sems):
        i = pl.program_id(0)

        def fetch(step):
            static = isinstance(step, int)
            slot = step % ring if static else lax.rem(step, ring)
            rows = pl.ds(step * tm if static else pl.multiple_of(step * tm, tm), tm)
            pairs = [(zqkv_hbm, zqkv_buf), (zkr_hbm, zkr_buf), (rope_hbm, rope_buf)]
            return [pltpu.make_async_copy(src.at[rows], dst.at[slot], sems.at[k, slot]) for k, (src, dst) in enumerate(pairs)]

        @pl.when(i == 0)
        def _():
            for first in range(min(ring - 1, n_steps)):
                for cp in fetch(first):
                    cp.start()

        @pl.when(i + (ring - 1) < n_steps)
        def _():
            for cp in fetch(i + (ring - 1)):
                cp.start()

        for cp in fetch(i):
            cp.wait()
        slot = lax.rem(i, ring)
        zqkv_ref, zkr_ref, rope_ref = zqkv_buf.at[slot], zkr_buf.at[slot], rope_buf.at[slot]
        cos, sin = rope_ref[:, :HEAD_LANES], rope_ref[:, HEAD_LANES:]
        zq, zkv = zqkv_ref[:, A_ZQ:A_ZKV], zqkv_ref[:, A_ZKV:A_KR]
        rq = lax.rsqrt(jnp.mean(zq * zq, axis=-1, keepdims=True) + EPS)
        qn = ((zq * rq) * qg_ref[...]).astype(BF16)
        qr = _dot(qn, wq_ref[...])
        cf, sf = jnp.tile(cos, (1, N_HEADS)), jnp.tile(sin, (1, N_HEADS))
        qf_ref[...] = ((qr[:, :1024] * cf + qr[:, 1024:] * sf) * (MLA_SCALE * LOG2E)).astype(BF16)
        rkv = lax.rsqrt(jnp.mean(zkv * zkv, axis=-1, keepdims=True) + EPS)
        kvn = ((zkv * rkv) * kvg_ref[...]).astype(BF16)
        kv = _dot(kvn, wkv_ref[...])
        zkr = zkr_ref[...]
        kpe = jnp.where(_lane_lo(), 0.0, zkr * cos) + pltpu.roll(zkr, HALF, 1) * sin
        kf_ref[...] = (kv[:, :1024] + jnp.tile(kpe, (1, N_HEADS))).astype(BF16)
        v_ref[...] = kv[:, 1024:].astype(BF16)

    tok = lambda w: pl.BlockSpec((tm, w), lambda i: (i, 0))
    outs = [(1024, BF16), (1024, BF16), (512, BF16)]
    return pl.pallas_call(
        body, name="up", grid=(n_steps,),
        out_shape=[jax.ShapeDtypeStruct((n_tok, w), dt) for w, dt in outs],
        in_specs=[ANY_SPEC, ANY_SPEC, ANY_SPEC, _full(qg.shape), _full(kvg.shape), _full(wq2.shape), _full(wkv.shape)],
        out_specs=[tok(w) for w, _ in outs],
        scratch_shapes=[pltpu.VMEM((ring, tm, 640), F32), pltpu.VMEM((ring, tm, HEAD_LANES), F32),
                        pltpu.VMEM((ring, tm, 2 * HEAD_LANES), F32), pltpu.SemaphoreType.DMA((3, ring))],
        compiler_params=_params(1),
    )(zqkv, zkr, rope, qg, kvg, wq2, wkv)


def _lane_lo(width=HEAD_LANES):
    return lax.broadcasted_iota(jnp.int32, (1, width), 1) < HALF


def _eye(n=HEAD_LANES):
    r = lax.broadcasted_iota(jnp.int32, (n, n), 0)
    c = lax.broadcasted_iota(jnp.int32, (n, n), 1)
    return jnp.where(r == c, 1.0, 0.0).astype(BF16)


def _mla_fwd_call(qf, kf, v, n_seq, seq):
    tq = min(ATT_TILE, seq)
    nq = seq // tq

    ext = HALF + 16

    def body(q_ref, k_ref, v_ref, o_ref, lse_ref, vt_ref, acc_ref):
        i = pl.program_id(1)
        eye = _eye()

        @pl.when(i == 0)
        def _():
            for h in range(N_HEADS):
                vt_ref[h * ext + HALF:(h + 1) * ext, :] = jnp.ones((16, seq), BF16)
            for t in range(nq):
                for p in range(N_HEADS // 2):
                    pair = slice(p * HEAD_LANES, (p + 1) * HEAD_LANES)
                    v_t = _dot_nt(eye, v_ref[t * tq:(t + 1) * tq, pair]).astype(BF16)
                    for hh in range(2):
                        r0 = (2 * p + hh) * ext
                        vt_ref[r0:r0 + HALF, t * tq:(t + 1) * tq] = v_t[hh * HALF:(hh + 1) * HALF, :]

        q = q_ref[...]
        qcol = i * tq + lax.broadcasted_iota(jnp.int32, (1, tq), 1)
        heads = range(N_HEADS)
        lanes = [slice(h * HEAD_LANES, (h + 1) * HEAD_LANES) for h in heads]

        def make_step(masked, n_tiles):
            def step(kt0, carry):
                tiles = range(n_tiles)
                start = pl.multiple_of(kt0 * tq, tq)
                ks = [k_ref[pl.ds(pl.multiple_of((kt0 + t) * tq, tq), tq), :] for t in tiles]
                vt = vt_ref[:, pl.ds(start, n_tiles * tq)]
                last = n_tiles - 1
                if masked:
                    keep = ((kt0 + last) * tq + lax.broadcasted_iota(jnp.int32, (tq, 1), 0)) <= qcol

                def scores(h):
                    sts = [_dot_nt(ks[t][:, lanes[h]], q[:, lanes[h]]) for t in tiles]
                    if masked:
                        sts[last] = jnp.where(keep, sts[last], NEG)
                    return sts

                def softmax(h, sts):
                    m_old = carry[h]
                    m_new = m_old
                    for st in sts:
                        m_new = jnp.maximum(m_new, jnp.max(st, axis=0, keepdims=True))
                    pt = jnp.concatenate([jnp.exp2(st - m_new).astype(BF16) for st in sts], axis=0)
                    return m_new, jnp.exp2(m_old - m_new), pt

                def values(h, alpha, pt):
                    rows = slice(h * ext, (h + 1) * ext)
                    acc_ref[rows, :] = acc_ref[rows, :] * alpha + _dot(vt[rows, :], pt)

                sts, soft, out = {0: scores(0), 1: scores(1)}, {}, {}
                for h in range(N_HEADS + 1):
                    if h + 2 < N_HEADS:
                        sts[h + 2] = scores(h + 2)
                    if h < N_HEADS:
                        soft[h] = softmax(h, sts.pop(h))
                    if h >= 1:
                        m_new, alpha, pt = soft.pop(h - 1)
                        values(h - 1, alpha, pt)
                        out[h - 1] = m_new
                return tuple(out[h] for h in heads)
            return step

        acc_ref[...] = jnp.zeros_like(acc_ref)
        init = (jnp.full((1, tq), NEG, F32),) * N_HEADS
        count = i + 1
        carry = lax.fori_loop(0, (count + 1) // 2 - 1, lambda j, c: make_step(False, 2)(2 * j, c), init)
        carry = lax.cond(count % 2 == 0, lambda c: make_step(True, 2)(i - 1, c), lambda c: make_step(True, 1)(i, c), carry)
        dens = [acc_ref[h * ext + HALF:h * ext + HALF + 1, :] for h in heads]
        acc_t = jnp.concatenate([acc_ref[h * ext:h * ext + HALF, :] * (1.0 / dens[h]) for h in heads], axis=0)
        o_ref[...] = acc_t.T
        for h in heads:
            lse_ref[0, h // 4, h % 4:h % 4 + 1, :] = carry[h] + jnp.log2(dens[h])

    n_tok = qf.shape[0]
    return pl.pallas_call(
        body, name="mla_fwd", grid=(n_seq, nq),
        out_shape=[jax.ShapeDtypeStruct((n_tok, 512), F32), jax.ShapeDtypeStruct((n_seq, 2, 4, seq), F32)],
        in_specs=[pl.BlockSpec((tq, 1024), lambda b, i: (b * nq + i, 0)),
                  pl.BlockSpec((seq, 1024), lambda b, i: (b, 0)),
                  pl.BlockSpec((seq, 512), lambda b, i: (b, 0))],
        out_specs=[pl.BlockSpec((tq, 512), lambda b, i: (b * nq + i, 0)),
                   pl.BlockSpec((1, 2, 4, tq), lambda b, i: (b, 0, 0, i))],
        scratch_shapes=[pltpu.VMEM((N_HEADS * ext, seq), BF16), pltpu.VMEM((N_HEADS * ext, tq), F32)],
        compiler_params=_params(2),
    )(qf, kf, v)


def _mla_bwd_call(qf, kf, v, do, delta, lse, n_seq, seq):
    tq = min(ATT_TILE, seq)
    nq = seq // tq

    nh = 4
    heads = range(nh)
    lanes = [slice(h * HEAD_LANES, (h + 1) * HEAD_LANES) for h in heads]

    def body(q_ref, k_ref, v_ref, do_ref, dl_ref, lse_ref, dq_ref, dk_ref, dv_ref,
             kt_ref, dot_ref, dqt_ref, dvt_ref):
        eye = _eye()
        sub_lo = lax.broadcasted_iota(jnp.int32, (HEAD_LANES, 1), 0) < HALF

        for t in range(nq):
            r = slice(t * tq, (t + 1) * tq)
            kv = k_ref[r, :]
            for h in heads:
                kt_ref[lanes[h], r] = _dot_nt(eye, kv[:, lanes[h]]).astype(BF16)
            for p in range(nh // 2):
                dov = do_ref[r, lanes[p]]
                dt = _dot_nt(eye, dov)
                dot_ref[2 * p, :, r] = jnp.where(sub_lo, dt, 0.0).astype(BF16)
                dot_ref[2 * p + 1, :, r] = jnp.where(sub_lo, 0.0, dt).astype(BF16)
        dqt_ref[...] = jnp.zeros_like(dqt_ref)
        dvt_ref[...] = jnp.zeros_like(dvt_ref)

        def flush_dv(tile, which):
            rows = pl.ds(pl.multiple_of(tile * tq, tq), tq)
            for p in range(nh // 2):
                dv_ref[rows, lanes[p]] = dvt_ref[which, p * HEAD_LANES:(p + 1) * HEAD_LANES, :].T

        def k_step(kt, _):
            slot = kt % 2
            kr = pl.ds(pl.multiple_of(kt * tq, tq), tq)
            k = k_ref[kr, :]
            vv = v_ref[kr, :]
            k_t = kt_ref[:, kr]
            krow = kt * tq + lax.broadcasted_iota(jnp.int32, (tq, 1), 0)

            def make_step(masked, n_tiles):
                def q_step(qt0, carry):
                    tiles = range(n_tiles)
                    qrs = [pl.ds(pl.multiple_of((qt0 + t) * tq, tq), tq) for t in tiles]
                    if masked:
                        flush_dv(jnp.maximum(kt - 1, 0), 1 - slot)
                    qs = [q_ref[qr, :] for qr in qrs]
                    if masked:
                        keep = krow <= (qt0 * tq + lax.broadcasted_iota(jnp.int32, (1, tq), 1))

                    def scores(h):
                        do_ts = [dot_ref[h, :, qr] for qr in qrs]
                        sts = [_dot_nt(k[:, lanes[h]], qs[t][:, lanes[h]]) for t in tiles]
                        dpts = [_dot(vv[:, lanes[h // 2]], do_ts[t]) for t in tiles]
                        return do_ts, sts, dpts

                    def softmax(h, sts, dpts):
                        pts, dsts = [], []
                        for t in tiles:
                            pt = jnp.exp2(sts[t] - lse_ref[0, 0, h:h + 1, qrs[t]])
                            if masked and t == 0:
                                pt = jnp.where(keep, pt, 0.0)
                            dsts.append((pt * (dpts[t] - dl_ref[0, h:h + 1, qrs[t]])).astype(BF16))
                            pts.append(pt.astype(BF16))
                        return pts, dsts

                    def grads(h, do_ts, pts, dsts):
                        half = slice((h % 2) * HALF, (h % 2 + 1) * HALF)
                        dst_all = jnp.concatenate(dsts, axis=1)
                        pt_all = jnp.concatenate(pts, axis=1)
                        do_all = jnp.concatenate([do_ts[t][half, :] for t in tiles], axis=1)
                        q_all = jnp.concatenate([qs[t][:, lanes[h]] for t in tiles], axis=0)
                        dvt_ref[slot, h * HALF:(h + 1) * HALF, :] += _dot_nt(do_all, pt_all)
                        dk_ref[kr, lanes[h]] += _dot(dst_all, q_all)
                        for t in tiles:
                            dqt_ref[lanes[h], qrs[t]] += _dot(k_t[lanes[h], :], dsts[t])

                    first, second = {0: scores(0)}, {}
                    for h in range(nh + 1):
                        if h + 1 < nh:
                            first[h + 1] = scores(h + 1)
                        if h < nh:
                            do_ts, sts, dpts = first.pop(h)
                            second[h] = (do_ts,) + softmax(h, sts, dpts)
                        if h >= 1:
                            grads(h - 1, *second.pop(h - 1))
                    return carry
                return q_step

            dk_ref[kr, :] = jnp.zeros((tq, nh * HEAD_LANES), F32)
            dvt_ref[slot] = jnp.zeros(dvt_ref.shape[1:], F32)
            count = nq - kt
            lax.cond(count >= 2, lambda c: make_step(True, 2)(kt, c), lambda c: make_step(True, 1)(kt, c), 0)
            lax.fori_loop(1, count // 2, lambda j, c: make_step(False, 2)(kt + 2 * j, c), 0)
            lax.cond(jnp.logical_and(count % 2 == 1, count >= 3), lambda c: make_step(False, 1)(nq - 1, c), lambda c: c, 0)
            return 0

        lax.fori_loop(0, nq, k_step, 0)
        flush_dv(nq - 1, (nq - 1) % 2)
        for t in range(nq):
            r = slice(t * tq, (t + 1) * tq)
            for h in heads:
                dq_ref[r, lanes[h]] = dqt_ref[lanes[h], r].T

    n_tok = qf.shape[0]
    groups = N_HEADS // nh
    blk = lambda w: pl.BlockSpec((seq, w), lambda b, g: (b, g))
    return pl.pallas_call(
        body, name="mla_bwd", grid=(n_seq, groups),
        out_shape=[jax.ShapeDtypeStruct((n_tok, 1024), F32), jax.ShapeDtypeStruct((n_tok, 1024), F32),
                   jax.ShapeDtypeStruct((n_tok, 512), F32)],
        in_specs=[blk(512), blk(512), blk(256), blk(256), pl.BlockSpec((1, nh, seq), lambda b, g: (g, 0, b)),
                  pl.BlockSpec((1, 1, nh, seq), lambda b, g: (b, g, 0, 0))],
        out_specs=[blk(512), blk(512), blk(256)],
        scratch_shapes=[pltpu.VMEM((nh * HEAD_LANES, seq), BF16), pltpu.VMEM((nh, HEAD_LANES, seq), BF16),
                        pltpu.VMEM((nh * HEAD_LANES, seq), F32), pltpu.VMEM((2, nh * HALF, tq), F32)],
        compiler_params=_params(2),
    )(qf, kf, v, do, delta, lse)


SWA_BLOCKS = 4


def _swa_block(n, pos_col_ref, posq):
    w = SWA_WINDOW
    start = pl.multiple_of(jnp.maximum(n - 1, 0) * w, w)
    posk = pos_col_ref[pl.ds(start, 2 * w), :]
    rel = (n * w + lax.broadcasted_iota(jnp.int32, (1, w), 1)) - (start + lax.broadcasted_iota(jnp.int32, (2 * w, 1), 0))
    valid = jnp.logical_and(rel >= 0, rel < w)
    return start, jnp.where(valid, posq - posk, 1e30)


def _alibi(h):
    return LOG2E * 2.0 ** -(h + 1)


def _transpose_rows(eye, src_ref, dst_ref, seq, width):
    step = 2 * SWA_WINDOW
    for t in range(seq // step):
        for p in range(width // HEAD_LANES):
            lanes = slice(p * HEAD_LANES, (p + 1) * HEAD_LANES)
            dst_ref[lanes, t * step:(t + 1) * step] = _dot_nt(eye, src_ref[t * step:(t + 1) * step, lanes]).astype(BF16)


def _swa_fwd_call(qs, kd, vd, pos_col, pos_row, sinks, n_seq, seq):
    w = SWA_WINDOW
    qb = SWA_BLOCKS
    steps = seq // (qb * w)
    ext = HALF + 16

    def body(q_ref, k_ref, v_ref, pc_ref, pr_ref, sink_ref, o_ref, lse_ref, vt_ref):
        n = pl.program_id(1)
        lo = _lane_lo()
        hi = jnp.logical_not(lo)
        eye = _eye()

        @pl.when(n == 0)
        def _():
            step = 2 * w
            for kv in range(2):
                vt_ref[kv * ext + HALF:(kv + 1) * ext, :] = jnp.ones((16, seq), BF16)
                for t in range(seq // step):
                    v_t = _dot_nt(eye, v_ref[t * step:(t + 1) * step, kv * HEAD_LANES:(kv + 1) * HEAD_LANES])
                    vt_ref[kv * ext:kv * ext + HALF, t * step:(t + 1) * step] = v_t[:HALF, :].astype(BF16)

        heads = range(N_HEADS)
        blocks = range(qb)
        geo = [_swa_block(n * qb + bi, pc_ref, pr_ref[bi]) for bi in blocks]
        wins = [pl.ds(g[0], 2 * w) for g in geo]
        kwins = [k_ref[win, :] for win in wins]
        vts = [vt_ref[:, win] for win in wins]
        sts = []
        for bi in blocks:
            q = q_ref[bi * w:(bi + 1) * w, :]
            sts.append([])
            for j in range(N_HEADS // 2):
                qp = q[:, j * HEAD_LANES:(j + 1) * HEAD_LANES]
                both = jnp.concatenate([jnp.where(lo, qp, jnp.zeros_like(qp)), jnp.where(hi, qp, jnp.zeros_like(qp))], axis=0)
                st = _dot_nt(kwins[bi][:, (j // 2) * HEAD_LANES:(j // 2 + 1) * HEAD_LANES], both)
                sts[bi] += [st[:, :w], st[:, w:]]
        ps, ms = [], []
        for bi in blocks:
            ps.append([])
            ms.append([])
            for h in heads:
                s = sts[bi][h] - _alibi(h) * geo[bi][1]
                m = jnp.maximum(jnp.max(s, axis=0, keepdims=True), sink_ref[0, h] * LOG2E)
                ps[bi].append(jnp.exp2(s - m).astype(BF16))
                ms[bi].append(m)
        for bi in blocks:
            ots = []
            for h in heads:
                pv = _dot(vts[bi][(h // 4) * ext:(h // 4 + 1) * ext, :], ps[bi][h])
                l = pv[HALF:HALF + 1, :] + jnp.exp2(sink_ref[0, h] * LOG2E - ms[bi][h])
                ots.append(pv[:HALF, :] * (1.0 / l))
                lse_ref[0, h:h + 1, bi * w:(bi + 1) * w] = ms[bi][h] + jnp.log2(l)
            o_ref[bi * w:(bi + 1) * w, :] = jnp.concatenate(ots, axis=0).T

    n_tok = qs.shape[0]
    tok = lambda width: pl.BlockSpec((qb * w, width), lambda b, n: (b * steps + n, 0))
    whole = lambda width: pl.BlockSpec((seq, width), lambda b, n: (b, 0))
    return pl.pallas_call(
        body, name="swa_fwd", grid=(n_seq, steps),
        out_shape=[jax.ShapeDtypeStruct((n_tok, 512), F32), jax.ShapeDtypeStruct((n_seq, N_HEADS, seq), F32)],
        in_specs=[tok(512), whole(256), whole(256), whole(1), pl.BlockSpec((qb, 1, w), lambda b, n: (b * steps + n, 0, 0)),
                  pl.BlockSpec(memory_space=pltpu.SMEM)],
        out_specs=[tok(512), pl.BlockSpec((1, N_HEADS, qb * w), lambda b, n: (b, 0, n))],
        scratch_shapes=[pltpu.VMEM((2 * ext, seq), BF16)],
        compiler_params=_params(2),
    )(qs, kd, vd, pos_col, pos_row, sinks)


def _swa_bwd_call(qs, kd, vd, do, delta, lse, pos_col, pos_row, sinks, g_out, n_seq, seq):
    w = SWA_WINDOW
    qb = SWA_BLOCKS
    steps = seq // (qb * w)
    reduced, reduce_scratch = _reduce_operands(g_out)

    def body(q_ref, k_ref, v_ref, do_ref, dl_ref, lse_ref, pc_ref, pr_ref, sink_ref, g_ref, dq_ref, dk_ref, dv_ref,
             dsink_ref, f_ref, kt_ref, *reduce_refs):
        b, n = pl.program_id(0), pl.program_id(1)
        _grad_reduce(b * steps + n, n_seq * steps, g_ref, f_ref, *reduce_refs)
        lo = _lane_lo()
        hi = jnp.logical_not(lo)
        sub_lo = lax.broadcasted_iota(jnp.int32, (HEAD_LANES, 1), 0) < HALF
        eye = _eye()

        @pl.when(n == 0)
        def _():
            dk_ref[...] = jnp.zeros_like(dk_ref)
            dv_ref[...] = jnp.zeros_like(dv_ref)
            _transpose_rows(eye, k_ref, kt_ref, seq, 2 * HEAD_LANES)

        @pl.when(jnp.logical_and(n == 0, b == 0))
        def _():
            dsink_ref[...] = jnp.zeros_like(dsink_ref)

        heads = range(N_HEADS)
        blocks = range(qb)
        kv_lanes = lambda h: slice((h // 4) * HEAD_LANES, (h // 4 + 1) * HEAD_LANES)
        geo = [_swa_block(n * qb + bi, pc_ref, pr_ref[bi]) for bi in blocks]
        wins = [pl.ds(g[0], 2 * w) for g in geo]
        kwins = [k_ref[win, :] for win in wins]
        vwins = [v_ref[win, :] for win in wins]

        do_ts, deltas, qms, doms = [], [], [], []
        for bi in blocks:
            rows = slice(bi * w, (bi + 1) * w)
            for lst in (do_ts, deltas, qms, doms):
                lst.append([])
            for j in range(N_HEADS // 2):
                pair = slice(j * HEAD_LANES, (j + 1) * HEAD_LANES)
                dop = do_ref[rows, pair]
                qp = q_ref[rows, pair]
                dt = _dot_nt(eye, dop)
                for hh in range(2):
                    half = lo if hh == 0 else hi
                    do_ts[bi].append(jnp.where(sub_lo, dt, 0.0).astype(BF16) if hh == 0
                                     else jnp.where(sub_lo, 0.0, dt).astype(BF16))
                    deltas[bi].append(dl_ref[2 * j + hh:2 * j + hh + 1, rows])
                    qms[bi].append(jnp.where(half, qp, jnp.zeros_like(qp)))
                    doms[bi].append(jnp.where(half, dop, jnp.zeros_like(dop)))
        sts, dpts = [], []
        for bi in blocks:
            sts.append([])
            dpts.append([])
            for j in range(N_HEADS // 2):
                a, b = 2 * j, 2 * j + 1
                st = _dot_nt(kwins[bi][:, kv_lanes(a)], jnp.concatenate([qms[bi][a], qms[bi][b]], axis=0))
                dpt = _dot(vwins[bi][:, kv_lanes(a)], jnp.concatenate([do_ts[bi][a], do_ts[bi][b]], axis=1))
                sts[bi] += [st[:, :w], st[:, w:]]
                dpts[bi] += [dpt[:, :w], dpt[:, w:]]
        pts, dsts = [], []
        for bi in blocks:
            pts.append([])
            dsts.append([])
            for h in heads:
                lse_h = lse_ref[0, h:h + 1, bi * w:(bi + 1) * w]
                pt = jnp.exp2(sts[bi][h] - _alibi(h) * geo[bi][1] - lse_h)
                dsts[bi].append((pt * (dpts[bi][h] - deltas[bi][h])).astype(BF16))
                pts[bi].append(pt.astype(BF16))
                dsink_ref[h:h + 1, :] += -jnp.exp2(sink_ref[0, h] * LOG2E - lse_h) * deltas[bi][h]
        for bi in blocks:
            for kv in range(2):
                group = range(4 * kv, 4 * kv + 4)
                dst_all = jnp.concatenate([dsts[bi][h] for h in group], axis=1)
                pt_all = jnp.concatenate([pts[bi][h] for h in group], axis=1)
                q_all = jnp.concatenate([qms[bi][h] for h in group], axis=0)
                do_all = jnp.concatenate([doms[bi][h] for h in group], axis=0)
                dk_ref[wins[bi], kv_lanes(4 * kv)] += _dot(dst_all, q_all)
                dv_ref[wins[bi], kv_lanes(4 * kv)] += _dot(pt_all, do_all)
        for bi in blocks:
            ktw = kt_ref[:, wins[bi]]
            for j in range(N_HEADS // 2):
                k_t = ktw[kv_lanes(2 * j), :]
                both = _dot(k_t, jnp.concatenate([dsts[bi][2 * j], dsts[bi][2 * j + 1]], axis=1))
                dq_t = jnp.where(sub_lo, both[:, :w], both[:, w:])
                dq_ref[bi * w:(bi + 1) * w, j * HEAD_LANES:(j + 1) * HEAD_LANES] = dq_t.T * SWA_SCALE

    n_tok = qs.shape[0]
    tok = lambda width: pl.BlockSpec((qb * w, width), lambda b, n: (b * steps + n, 0))
    whole = lambda width: pl.BlockSpec((seq, width), lambda b, n: (b, 0))
    return pl.pallas_call(
        body, name="swa_bwd", grid=(n_seq, steps),
        out_shape=[jax.ShapeDtypeStruct((n_tok, 512), F32), jax.ShapeDtypeStruct((n_tok, 256), F32),
                   jax.ShapeDtypeStruct((n_tok, 256), F32), jax.ShapeDtypeStruct((N_HEADS, HEAD_LANES), F32), reduced],
        in_specs=[tok(512), whole(256), whole(256), pl.BlockSpec((qb * w, 512), lambda b, n: (b * steps + n, 1)),
                  pl.BlockSpec((N_HEADS, qb * w), lambda b, n: (0, b * steps + n)),
                  pl.BlockSpec((1, N_HEADS, qb * w), lambda b, n: (b, 0, n)),
                  whole(1), pl.BlockSpec((qb, 1, w), lambda b, n: (b * steps + n, 0, 0)),
                  pl.BlockSpec(memory_space=pltpu.SMEM), ANY_SPEC],
        out_specs=[tok(512), whole(256), whole(256), _full((N_HEADS, HEAD_LANES)), ANY_SPEC],
        scratch_shapes=[pltpu.VMEM((2 * HEAD_LANES, seq), BF16)] + reduce_scratch,
        compiler_params=_params(2),
    )(qs, kd, vd, do, delta, lse, pos_col, pos_row, sinks, g_out)


def _post_call(x, target, o_mla, o_swa, gates, mod, b_ada, fg, w_out, seq):
    n_tok = x.shape[0]
    tm = min(TOKEN_TILE, seq)
    per_seq = seq // tm
    n_seq = n_tok // seq

    def body(x_ref, t_ref, om_ref, os_ref, g_ref, mod_ref, bada_ref, fg_ref, w_ref,
             dx2_ref, do_ref, dg_ref, gw_ref, gfg_ref, dgate_ref, loss_ref, dmla_ref, dswa_ref):
        i = pl.program_id(0)

        @pl.when(i == 0)
        def _():
            gw_ref[...] = jnp.zeros_like(gw_ref)
            gfg_ref[...] = jnp.zeros_like(gfg_ref)
            loss_ref[...] = jnp.zeros_like(loss_ref)

        @pl.when(i % per_seq == 0)
        def _():
            dgate_ref[...] = jnp.zeros_like(dgate_ref)

        gate = mod_ref[0][:, 2 * D_MODEL:] + bada_ref[:, 2 * D_MODEL:]
        fgv = fg_ref[...]
        fgd = fgv * (1.0 / D_MODEL)
        subs = _sub_tiles(tm)
        gs = [g_ref[r, :] for r in subs]
        os_ = [jnp.concatenate([om_ref[r, :], os_ref[r, :]], axis=-1) for r in subs]
        sgs = [_sigmoid(g) for g in gs]
        sils = [g * sg for g, sg in zip(gs, sgs)]
        ypres = [(o * sil).astype(BF16) for o, sil in zip(os_, sils)]
        ys = [_dot(ypre, w_ref[...]) for ypre in ypres]
        dys, loss, gfg, dgate = [], 0.0, 0.0, 0.0
        for r, y in zip(subs, ys):
            x2 = x_ref[r, :] + gate * y
            r2 = lax.rsqrt(jnp.mean(x2 * x2, axis=-1, keepdims=True) + EPS)
            xn2 = x2 * r2
            err = xn2 * fgv - t_ref[r, :]
            loss = loss + jnp.sum(jnp.sum(err * err, axis=-1, keepdims=True), axis=0, keepdims=True)
            gfg = gfg + jnp.sum(err * xn2, axis=0, keepdims=True)
            dxn2 = err * fgd
            dx2 = r2 * (dxn2 - xn2 * jnp.mean(dxn2 * xn2, axis=-1, keepdims=True))
            dx2_ref[r, :] = dx2
            dgate = dgate + jnp.sum(dx2 * y, axis=0, keepdims=True)
            dys.append((dx2 * gate).astype(BF16))
        loss_ref[...] += jnp.broadcast_to(loss * (0.5 / D_MODEL), loss_ref.shape)
        gfg_ref[...] += gfg * (1.0 / D_MODEL)
        dgate_ref[0] += dgate
        gw_ref[...] += _dot_tn(jnp.concatenate(ypres, axis=0), jnp.concatenate(dys, axis=0))
        dypres = [_dot_nt(dy, w_ref[...]) for dy in dys]
        pick = jnp.where(jnp.right_shift(lax.broadcasted_iota(jnp.int32, (2 * N_HEADS, D_MODEL), 1), 6)
                         == lax.broadcasted_iota(jnp.int32, (2 * N_HEADS, D_MODEL), 0), 1.0, 0.0).astype(BF16)
        for r, dypre, o, g, sg, sil in zip(subs, dypres, os_, gs, sgs, sils):
            dov = (dypre * sil).astype(BF16)
            do_ref[r, :] = dov
            delta = _dot_nt(pick, (dov.astype(F32) * o).astype(BF16))
            for grp in range(2):
                dmla_ref[grp, :, r] = delta[4 * grp:4 * grp + 4, :]
            dswa_ref[:, r] = delta[N_HEADS:, :]
            dg_ref[r, :] = (dypre * o * (sg + sil * (1.0 - sg))).astype(BF16)

    tok = lambda w: pl.BlockSpec((tm, w), lambda i: (i, 0))
    per_b = pl.BlockSpec((1, 1, 3 * D_MODEL), lambda i: (i // per_seq, 0, 0))
    return pl.pallas_call(
        body, name="post", grid=(n_tok // tm,),
        out_shape=[jax.ShapeDtypeStruct((n_tok, D_MODEL), F32), jax.ShapeDtypeStruct((n_tok, D_MODEL), BF16),
                   jax.ShapeDtypeStruct((n_tok, D_MODEL), BF16), jax.ShapeDtypeStruct((D_MODEL, D_MODEL), F32),
                   jax.ShapeDtypeStruct((1, D_MODEL), F32), jax.ShapeDtypeStruct((n_seq, 1, D_MODEL), F32),
                   jax.ShapeDtypeStruct((1, HEAD_LANES), F32),
                   jax.ShapeDtypeStruct((2, N_HEADS // 2, n_tok), F32), jax.ShapeDtypeStruct((N_HEADS, n_tok), F32)],
        in_specs=[tok(D_MODEL), tok(D_MODEL), tok(512), tok(512), tok(D_MODEL), per_b, _full(b_ada.shape),
                  _full(fg.shape), _full(w_out.shape)],
        out_specs=[tok(D_MODEL), tok(D_MODEL), tok(D_MODEL), _full((D_MODEL, D_MODEL)), _full((1, D_MODEL)),
                   pl.BlockSpec((1, 1, D_MODEL), lambda i: (i // per_seq, 0, 0)), _full((1, HEAD_LANES)),
                   pl.BlockSpec((2, N_HEADS // 2, tm), lambda i: (0, 0, i)), pl.BlockSpec((N_HEADS, tm), lambda i: (0, i))],
        compiler_params=_params(1),
    )(x, target, o_mla, o_swa, gates, mod, b_ada, fg, w_out)


def _mid_bwd_call(dqf, dkf, dv, zqkv, rope, qg, kvg, wq2, wkv, seq):
    n_tok = dqf.shape[0]
    tm = min(TOKEN_TILE, seq)

    def body(dq_ref, dk_ref, dv_ref, z_ref, rope_ref, qg_ref, kvg_ref, wq_ref, wkv_ref,
             dz_ref, gwq_ref, gwkv_ref, gqg_ref, gkvg_ref):
        i = pl.program_id(0)

        @pl.when(i == 0)
        def _():
            gwq_ref[...] = jnp.zeros_like(gwq_ref)
            gwkv_ref[...] = jnp.zeros_like(gwkv_ref)
            gqg_ref[...] = jnp.zeros_like(gqg_ref)
            gkvg_ref[...] = jnp.zeros_like(gkvg_ref)

        cos, sin = rope_ref[:, :HEAD_LANES], rope_ref[:, HEAD_LANES:]
        cf, sf = jnp.tile(cos, (1, N_HEADS)), jnp.tile(sin, (1, N_HEADS))
        dq = dq_ref[...] * MLA_SCALE
        dqr = jnp.concatenate([dq * cf, dq * sf], axis=-1).astype(BF16)
        zq, zkv = z_ref[:, :Q_LORA], z_ref[:, Q_LORA:]
        qgv, kvgv = qg_ref[...], kvg_ref[...]

        rq = lax.rsqrt(jnp.mean(zq * zq, axis=-1, keepdims=True) + EPS)
        xq = zq * rq
        gwq_ref[...] += _dot_tn((xq * qgv).astype(BF16), dqr)
        dqn = _dot_nt(dqr, wq_ref[...])
        gqg_ref[...] += jnp.sum(dqn * xq, axis=0, keepdims=True)
        dxq = dqn * qgv
        dz_ref[:, :Q_LORA] = (rq * (dxq - xq * jnp.mean(dxq * xq, axis=-1, keepdims=True))).astype(BF16)

        dk = dk_ref[...] * LN2
        dkv = jnp.concatenate([dk, dv_ref[...]], axis=-1).astype(BF16)
        rkv = lax.rsqrt(jnp.mean(zkv * zkv, axis=-1, keepdims=True) + EPS)
        xkv = zkv * rkv
        gwkv_ref[...] += _dot_tn((xkv * kvgv).astype(BF16), dkv)
        dkvn = _dot_nt(dkv, wkv_ref[...])
        gkvg_ref[...] += jnp.sum(dkvn * xkv, axis=0, keepdims=True)
        dxkv = dkvn * kvgv
        dz_ref[:, Q_LORA:A_KR] = (rkv * (dxkv - xkv * jnp.mean(dxkv * xkv, axis=-1, keepdims=True))).astype(BF16)

        dkpe = dk[:, :HEAD_LANES]
        for h in range(1, N_HEADS):
            dkpe = dkpe + dk[:, h * HEAD_LANES:(h + 1) * HEAD_LANES]
        dz_ref[:, A_KR:] = (jnp.where(_lane_lo(), 0.0, dkpe * cos) + pltpu.roll(dkpe * sin, HALF, 1)).astype(BF16)

    tok = lambda w: pl.BlockSpec((tm, w), lambda i: (i, 0))
    return pl.pallas_call(
        body, name="mid_bwd", grid=(n_tok // tm,),
        out_shape=[jax.ShapeDtypeStruct((n_tok, A_GM), BF16),
                   jax.ShapeDtypeStruct(wq2.shape, F32), jax.ShapeDtypeStruct(wkv.shape, F32),
                   jax.ShapeDtypeStruct((1, Q_LORA), F32), jax.ShapeDtypeStruct((1, KV_LORA), F32)],
        in_specs=[tok(1024), tok(1024), tok(512), tok(640), tok(2 * HEAD_LANES), _full(qg.shape), _full(kvg.shape),
                  _full(wq2.shape), _full(wkv.shape)],
        out_specs=[tok(A_GM), _full(wq2.shape), _full(wkv.shape), _full((1, Q_LORA)), _full((1, KV_LORA))],
        compiler_params=_params(1),
    )(dqf, dkf, dv, zqkv, rope, qg, kvg, wq2, wkv)


def _in_bwd_call(x, dx2, dz, dg, dqs, dkd, dvd, mod, b_ada, ng, wa, seq):
    n_tok = x.shape[0]
    tm = min(TOKEN_TILE, seq)
    per_seq = seq // tm
    n_seq = n_tok // seq

    def body(x_ref, dx2_ref, dz_ref, dg_ref, dqs_ref, dkd_ref, dvd_ref, mod_ref, bada_ref, ng_ref,
             wa_ref, gx_ref, gwa_ref, gng_ref, dshift_ref, dscale_ref):
        i = pl.program_id(0)

        @pl.when(i == 0)
        def _():
            gwa_ref[...] = jnp.zeros_like(gwa_ref)
            gng_ref[...] = jnp.zeros_like(gng_ref)

        @pl.when(i % per_seq == 0)
        def _():
            dshift_ref[...] = jnp.zeros_like(dshift_ref)
            dscale_ref[...] = jnp.zeros_like(dscale_ref)

        xv = x_ref[...]
        modv = mod_ref[0] + bada_ref[...]
        shift, scale = modv[:, :D_MODEL], modv[:, D_MODEL:2 * D_MODEL]
        ngv = ng_ref[...]
        r1 = lax.rsqrt(jnp.mean(xv * xv, axis=-1, keepdims=True) + EPS)
        xn = xv * r1
        hb = ((xn * ngv) * (1.0 + scale) + shift).astype(BF16)

        dgv = dg_ref[...]
        pieces = [(A_ZQ, dz_ref[...]), (A_GM, dgv[:, :512]), (A_QS, dqs_ref[...].astype(BF16)),
                  (A_KS, jnp.concatenate([_once(dkd_ref[...]) * LN2, _once(dvd_ref[...])], axis=1).astype(BF16)),
                  (A_GS, dgv[:, 512:])]
        dh = None
        for off, piece in pieces:
            wd = piece.shape[1]
            gwa_ref[:, off:off + wd] += _dot_tn(hb, piece)
            term = _dot_nt(piece, wa_ref[:, off:off + wd])
            dh = term if dh is None else dh + term

        dshift_ref[0] += jnp.sum(dh, axis=0, keepdims=True)
        dscale_ref[0] += jnp.sum(dh * (xn * ngv), axis=0, keepdims=True)
        gng_ref[...] += jnp.sum(dh * xn * (1.0 + scale), axis=0, keepdims=True)
        dxn = dh * ngv * (1.0 + scale)
        gx_ref[...] = dx2_ref[...] + r1 * (dxn - xn * jnp.mean(dxn * xn, axis=-1, keepdims=True))

    tok = lambda w: pl.BlockSpec((tm, w), lambda i: (i, 0))
    per_b = lambda w: pl.BlockSpec((1, 1, w), lambda i: (i // per_seq, 0, 0))
    return pl.pallas_call(
        body, name="in_bwd", grid=(n_tok // tm,),
        out_shape=[jax.ShapeDtypeStruct((n_tok, D_MODEL), F32), jax.ShapeDtypeStruct((D_MODEL, A_END), F32),
                   jax.ShapeDtypeStruct((1, D_MODEL), F32),
                   jax.ShapeDtypeStruct((n_seq, 1, D_MODEL), F32), jax.ShapeDtypeStruct((n_seq, 1, D_MODEL), F32)],
        in_specs=[tok(D_MODEL), tok(D_MODEL), tok(A_GM), tok(D_MODEL), tok(512), tok(256), tok(256),
                  per_b(3 * D_MODEL), _full(b_ada.shape), _full(ng.shape), _full(wa.shape)],
        out_specs=[tok(D_MODEL), _full((D_MODEL, A_END)), _full((1, D_MODEL)), per_b(D_MODEL), per_b(D_MODEL)],
        compiler_params=_params(1),
    )(x, dx2, dz, dg, dqs, dkd, dvd, mod, b_ada, ng, wa)


def _adam_math(w, g, m, v):
    m_new = ADAM_B1 * m + (1.0 - ADAM_B1) * g
    v_new = ADAM_B2 * v + (1.0 - ADAM_B2) * (g * g)
    m_hat = m_new / (1.0 - ADAM_B1 ** ADAM_STEP)
    v_hat = v_new / (1.0 - ADAM_B2 ** ADAM_STEP)
    delta = -ADAM_LR * (m_hat / (jnp.sqrt(v_hat) + ADAM_EPS) + ADAM_WD * w)
    return delta, m_new, v_new


def _adam_call(name, w, g, m, v):
    rows, cols = w.shape
    tr = next((t for t in (256, 128) if rows % t == 0), rows)

    def body(w_ref, g_ref, m_ref, v_ref, d_ref, mo_ref, vo_ref):
        d, mn, vn = _adam_math(w_ref[...], g_ref[...], m_ref[...], v_ref[...])
        d_ref[...] = d
        mo_ref[...] = mn
        vo_ref[...] = vn

    spec = pl.BlockSpec((tr, cols), lambda i: (i, 0))
    return pl.pallas_call(
        body, name=name, grid=(rows // tr,),
        out_shape=[jax.ShapeDtypeStruct(w.shape, F32)] * 3,
        in_specs=[spec] * 4, out_specs=[spec] * 3,
        compiler_params=_params(1),
    )(w, g, m, v)


def _adam_group_call(name, groups):
    n = len(groups)

    def body(*refs):
        ins, outs = refs[:4 * n], refs[4 * n:]
        for k in range(n):
            w_ref, g_ref, m_ref, v_ref = ins[4 * k:4 * k + 4]
            d, mn, vn = _adam_math(w_ref[...], g_ref[...], m_ref[...], v_ref[...])
            outs[3 * k][...] = d
            outs[3 * k + 1][...] = mn
            outs[3 * k + 2][...] = vn

    flat = [t for group in groups for t in group]
    shapes = [group[0].shape for group in groups for _ in range(3)]
    res = pl.pallas_call(
        body, name=name, grid=(1,),
        out_shape=[jax.ShapeDtypeStruct(s, F32) for s in shapes],
        in_specs=[_full(t.shape) for t in flat], out_specs=[_full(s) for s in shapes],
        compiler_params=_params(1),
    )(*flat)
    return [res[3 * k:3 * k + 3] for k in range(n)]


def _ada_bwd_call(act_all, dmod_cols, w, m, v):
    rows, cols = w.shape
    tr = 512

    def body(a_ref, dm_ref, w_ref, m_ref, v_ref, g_ref, d_ref, mo_ref, vo_ref):
        g = _dot_tn(a_ref[...].astype(BF16), dm_ref[...].astype(BF16))
        d, mn, vn = _adam_math(w_ref[...], g, m_ref[...], v_ref[...])
        g_ref[...] = g
        d_ref[...] = d
        mo_ref[...] = mn
        vo_ref[...] = vn

    spec = pl.BlockSpec((tr, cols), lambda i: (i, 0))
    nb = act_all.shape[0]
    return pl.pallas_call(
        body, name="ada_bwd", grid=(rows // tr,),
        out_shape=[jax.ShapeDtypeStruct(w.shape, F32)] * 4,
        in_specs=[pl.BlockSpec((nb, tr), lambda i: (0, i)), _full(dmod_cols.shape), spec, spec, spec],
        out_specs=[spec] * 4,
        compiler_params=_params(1),
    )(act_all, dmod_cols, w, m, v)


SMALL_ROW = {"norm_gain": (0, 1024), "final_gain": (1024, 2048), "q_norm_gain": (2048, 2432),
             "kv_norm_gain": (2432, 2688), "swa_sinks": (2688, 2696), "loss": (2816, 2944)}
SMALL_ORDER = ("b_ada", "norm_gain", "q_norm_gain", "kv_norm_gain", "swa_sinks", "final_gain")


def _small_call(parts_all, n_seq, params):
    k = len(params)

    def body(p_ref, *refs):
        ins, outs, loss_ref = refs[:3 * k], refs[3 * k:7 * k], refs[7 * k]
        row = p_ref[n_seq:n_seq + 1, :]
        for dv in range(1, 8):
            r0 = dv * ROWS_PER_DEVICE + n_seq
            row = row + p_ref[r0:r0 + 1, :]
        gb = None
        for dv in range(8):
            for r in range(n_seq):
                r0 = dv * ROWS_PER_DEVICE + r
                gb = p_ref[r0:r0 + 1, :] if gb is None else gb + p_ref[r0:r0 + 1, :]
        for j, name in enumerate(SMALL_ORDER):
            g = gb if name == "b_ada" else row[:, SMALL_ROW[name][0]:SMALL_ROW[name][1]]
            d, mn, vn = _adam_math(ins[3 * j][...], g, ins[3 * j + 1][...], ins[3 * j + 2][...])
            outs[4 * j][...] = g
            outs[4 * j + 1][...] = d
            outs[4 * j + 2][...] = mn
            outs[4 * j + 3][...] = vn
        loss_ref[...] = row[:, SMALL_ROW["loss"][0]:SMALL_ROW["loss"][1]]

    flat = [t for p in params for t in p]
    res = pl.pallas_call(
        body, name="small_update", grid=(1,),
        out_shape=[jax.ShapeDtypeStruct(p[0].shape, F32) for p in params for _ in range(4)]
        + [jax.ShapeDtypeStruct((1, HEAD_LANES), F32)],
        in_specs=[_full(parts_all.shape)] + [_full(t.shape) for t in flat],
        out_specs=[_full(p[0].shape) for p in params for _ in range(4)] + [_full((1, HEAD_LANES))],
        compiler_params=_params(1),
    )(parts_all, *flat)
    return [res[4 * j:4 * j + 4] for j in range(k)], res[4 * k]


def _rot(t):
    half = t.shape[-1] // 2
    return jnp.concatenate([-t[..., half:], t[..., :half]], axis=-1)


def _rot_t(g):
    half = g.shape[-1] // 2
    return jnp.concatenate([g[..., half:], -g[..., :half]], axis=-1)


def _columns(segments, lo, hi):
    out, at = [], 0
    for seg in segments:
        n = seg.shape[1]
        a, b = max(lo, at), min(hi, at + n)
        if a < b:
            out.append(seg[:, a - at:b - at])
        at += n
    return out


def _prepare_in(w_in_blocks):
    o = [0]
    for s in IN_SPLITS:
        o.append(o[-1] + s)
    part = lambda a, b: _columns(w_in_blocks, a, b)
    kr = jnp.concatenate(part(o[2], o[3]), axis=1)
    zero = jnp.zeros((kr.shape[0], 32), kr.dtype)
    return jnp.concatenate(part(0, o[2]) + [_rot(kr), zero, kr, zero] + part(o[3], o[8]), axis=1)


def _prepare_up(w_uq, w_ukv):
    uq = w_uq.reshape(Q_LORA, N_HEADS, MLA_NOPE + MLA_ROPE)
    zq = jnp.zeros((Q_LORA, N_HEADS, 32), w_uq.dtype)
    uq_full = jnp.concatenate([uq, zq], axis=-1).reshape(Q_LORA, 1024)
    uq_rot = jnp.concatenate([jnp.zeros((Q_LORA, N_HEADS, 64), w_uq.dtype), _rot(uq[..., MLA_NOPE:]), zq],
                             axis=-1).reshape(Q_LORA, 1024)
    wq2 = jnp.concatenate([uq_full, uq_rot], axis=1)
    ukv = w_ukv.reshape(KV_LORA, N_HEADS, 128)
    k_full = jnp.concatenate([ukv[..., :64], jnp.zeros((KV_LORA, N_HEADS, 64), w_ukv.dtype)], axis=-1).reshape(KV_LORA, 1024)
    wkv = jnp.concatenate([k_full, ukv[..., 64:].reshape(KV_LORA, 512)], axis=1)
    return wq2, wkv


def _restore_in(gwa):
    gkr = gwa[:, A_KR + 64:A_KR + 96] + _rot_t(gwa[:, A_KR:A_KR + 32])
    in_order = [gwa[:, :A_KR], gkr, gwa[:, A_GM:]]
    n = D_IN // 4
    return [jnp.concatenate(_columns(in_order, k * n, (k + 1) * n), axis=1) for k in range(4)]


def _restore_up(gwq2, gwkv):
    gf = gwq2[:, :1024].reshape(Q_LORA, N_HEADS, 128)
    gr = gwq2[:, 1024:].reshape(Q_LORA, N_HEADS, 128)
    g_uq = jnp.concatenate([gf[..., :64], gf[..., 64:96] + _rot_t(gr[..., 64:96])], axis=-1).reshape(Q_LORA, 768)
    gk = gwkv[:, :1024].reshape(KV_LORA, N_HEADS, 128)[..., :64]
    gv = gwkv[:, 1024:].reshape(KV_LORA, N_HEADS, 64)
    g_ukv = jnp.concatenate([gk, gv], axis=-1).reshape(KV_LORA, 1024)
    return g_uq, g_ukv


def _local_step(x, positions, target, mod_rows, b_ada, ng, qg, kvg, sinks, fg, w_in_b, later_shards):
    n_seq, seq, _ = x.shape
    n_tok = n_seq * seq
    x2d = x.reshape(n_tok, D_MODEL)
    t2d = target.reshape(n_tok, D_MODEL)
    pos_f = positions.astype(F32)
    pos_col = pos_f.reshape(n_tok, 1)
    pos_row = pos_f.reshape(n_tok // SWA_WINDOW, 1, SWA_WINDOW)
    mod3 = mod_rows.reshape(n_seq, 1, 3 * D_MODEL)
    inv = ROPE_THETA ** (-jnp.arange(0, MLA_ROPE, 2, dtype=F32) / MLA_ROPE)
    inv128 = jnp.tile(jnp.concatenate([inv, inv]), 4).reshape(1, HEAD_LANES)
    fg2 = fg.reshape(1, D_MODEL)

    wa = _prepare_in(w_in_b)
    zqkv, zkr, gates, qs, kd, vd, rope, f_uq, f_ukv, f_out = _pre_call(x2d, pos_col, mod3, b_ada, ng, inv128, wa,
                                                                       later_shards, seq)
    cols = lambda t, r: jnp.transpose(t.reshape(4, r, -1), (1, 0, 2)).reshape(r, -1)
    wq2, wkv = _prepare_up(cols(f_uq, Q_LORA), cols(f_ukv, KV_LORA))
    w_out_b = f_out.reshape(D_MODEL, D_MODEL)
    qf, kf, v = _up_call(zqkv, zkr, rope, qg, kvg, wq2, wkv, seq)
    o_mla, lse_mla = _mla_fwd_call(qf, kf, v, n_seq, seq)
    o_swa, lse_swa = _swa_fwd_call(qs, kd, vd, pos_col, pos_row, sinks, n_seq, seq)
    dx2, do, dg, g_out, g_fg, dgate, loss, delta_mla, delta_swa = _post_call(x2d, t2d, o_mla, o_swa, gates, mod3, b_ada, fg2, w_out_b, seq)
    dqf, dkf, dv = _mla_bwd_call(qf, kf, v, do, delta_mla, lse_mla, n_seq, seq)
    dqs, dkd, dvd, dsink, r_out = _swa_bwd_call(qs, kd, vd, do, delta_swa, lse_swa, pos_col, pos_row, sinks,
                                                g_out.reshape(4, 2, D_MODEL // 8, D_MODEL), n_seq, seq)
    dz, g_wq2, g_wkv, g_qg, g_kvg = _mid_bwd_call(dqf, dkf, dv, zqkv, rope, qg, kvg, wq2, wkv, seq)
    gx, g_wa, g_ng, dshift, dscale = _in_bwd_call(x2d, dx2, dz, dg, dqs, dkd, dvd, mod3, b_ada, ng, wa, seq)
    g_in = _restore_in(g_wa)
    g_uq, g_ukv = _restore_up(g_wq2, g_wkv)
    dmod = jnp.concatenate([dshift, dscale, dgate], axis=-1).reshape(n_seq, 3 * D_MODEL)
    small_row = jnp.concatenate([g_ng, g_fg, g_qg, g_kvg, jnp.pad(jnp.sum(dsink, axis=1).reshape(1, N_HEADS), ((0, 0), (0, 120))),
                                 loss, jnp.zeros((1, 128), F32)], axis=1)
    return gx.reshape(x.shape), (g_in, g_uq, g_ukv), r_out, small_row, dmod


def kernel(x, c, positions, w_ada, b_ada, norm_gain, w_in, q_norm_gain, kv_norm_gain, w_uq, w_ukv, swa_sinks, w_out, final_gain, loss_target, m_w_ada, m_b_ada, m_norm_gain, m_w_in, m_q_norm_gain, m_kv_norm_gain, m_w_uq, m_w_ukv, m_swa_sinks, m_w_out, m_final_gain, v_w_ada, v_b_ada, v_norm_gain, v_w_in, v_q_norm_gain, v_kv_norm_gain, v_w_uq, v_w_ukv, v_swa_sinks, v_w_out, v_final_gain):
    n_seq = x.shape[0]
    xi, yi, ci = lax.axis_index("x"), lax.axis_index("y"), lax.axis_index("c")
    dev = 4 * xi + 2 * yi + ci
    chip = 2 * xi + yi

    halves = lambda w: w.astype(BF16).reshape(2, w.shape[0] // 2, w.shape[1])
    c_blk = jnp.pad(c, ((0, ROWS_PER_DEVICE - n_seq), (0, 0)))
    act_all, pieces, f_in = _comm_fwd_call(c_blk, w_ada[0], [halves(w_in[0])])
    mine = lax.dynamic_slice_in_dim(pieces, dev * ROWS_PER_DEVICE, n_seq, axis=1)
    mod_rows = jnp.transpose(mine, (1, 0, 2)).reshape(n_seq, 3 * D_MODEL)
    w_in_blocks = [f_in[k].reshape(D_MODEL, -1) for k in range(4)]

    gx, (g_in_blocks, g_uq, g_ukv), r_out, small_row, dmod = _local_step(
        x, positions, loss_target, mod_rows, b_ada, norm_gain, q_norm_gain, kv_norm_gain, swa_sinks, final_gain,
        w_in_blocks, [halves(w_uq[0]), halves(w_ukv[0]), halves(w_out[0])])

    grads = [jnp.stack(g_in_blocks).reshape(4, 2, D_MODEL // 2, -1), _by_owner(g_uq, g_uq.shape[1] // 4),
             _by_owner(g_ukv, g_ukv.shape[1] // 4)]
    part = jnp.concatenate([dmod, small_row, jnp.zeros((ROWS_PER_DEVICE - n_seq - 1, 3 * D_MODEL), F32)], axis=0)
    r_in, r_uq, r_ukv, parts_all = _comm_bwd_call(grads, part)
    g_in_s, g_uq_s = r_in.reshape(w_in.shape[1:]), r_uq.reshape(w_uq.shape[1:])
    g_ukv_s, g_out_s = r_ukv.reshape(w_ukv.shape[1:]), r_out.reshape(w_out.shape[1:])

    tr = lambda a: jnp.swapaxes(a[0], 0, 1)
    back = lambda ts: [jnp.swapaxes(t, 0, 1) for t in ts]
    d_in, nm_in, nv_in = back(_adam_call("adam_w_in", tr(w_in), g_in_s.T, tr(m_w_in), tr(v_w_in)))
    uq_t, (d_ukv, nm_ukv, nv_ukv), (d_out, nm_out, nv_out) = _adam_group_call(
        "adam_small", [(tr(w_uq), g_uq_s.T, tr(m_w_uq), tr(v_w_uq)), (w_ukv[0], g_ukv_s, m_w_ukv[0], v_w_ukv[0]),
                       (w_out[0], g_out_s, m_w_out[0], v_w_out[0])])
    d_uq, nm_uq, nv_uq = back(uq_t)
    dmod_cols = lax.dynamic_slice_in_dim(parts_all, chip * 768, 768, axis=1)
    g_ada, d_ada, nm_ada, nv_ada = _ada_bwd_call(act_all, dmod_cols, w_ada[0], m_w_ada[0], v_w_ada[0])

    row = lambda t: t.reshape(1, -1)
    small = {"b_ada": (b_ada, m_b_ada, v_b_ada), "norm_gain": (norm_gain, m_norm_gain, v_norm_gain),
             "q_norm_gain": (q_norm_gain, m_q_norm_gain, v_q_norm_gain),
             "kv_norm_gain": (kv_norm_gain, m_kv_norm_gain, v_kv_norm_gain),
             "swa_sinks": (swa_sinks, m_swa_sinks, v_swa_sinks),
             "final_gain": (row(final_gain), row(m_final_gain), row(v_final_gain))}
    res, loss_row = _small_call(parts_all, n_seq, [small[name] for name in SMALL_ORDER])
    res = dict(zip(SMALL_ORDER, res))
    res["final_gain"] = [t.reshape(-1) for t in res["final_gain"]]
    e = lambda t: t[None]
    big = {"w_ada": (e(g_ada), e(d_ada), e(nm_ada), e(nv_ada)), "w_in": (e(g_in_s), e(d_in), e(nm_in), e(nv_in)),
           "w_uq": (e(g_uq_s), e(d_uq), e(nm_uq), e(nv_uq)), "w_ukv": (e(g_ukv_s), e(d_ukv), e(nm_ukv), e(nv_ukv)),
           "w_out": (e(g_out_s), e(d_out), e(nm_out), e(nv_out))}
    order = ("w_ada", "b_ada", "norm_gain", "w_in", "q_norm_gain", "kv_norm_gain", "w_uq", "w_ukv", "swa_sinks", "w_out",
             "final_gain")
    pick = lambda kind: [(big[n] if n in big else res[n])[kind] for n in order]
    return (loss_row[0, 0], gx, *pick(0), *pick(1), *pick(2), *pick(3))
```

```python
import jax
import jax.numpy as jnp
from jax import lax
from jax.experimental import pallas as pl
from jax.experimental.pallas import tpu as pltpu

F32 = jnp.float32
BF16 = jnp.bfloat16

D_MODEL = 1024
Q_LORA = 384
KV_LORA = 256
N_HEADS = 8
MLA_NOPE = 64
MLA_ROPE = 32
HEAD_LANES = 128
HALF = 64
SWA_WINDOW = 128
EPS = 1e-6
ROPE_THETA = 10000.0
MLA_SCALE = (MLA_NOPE + MLA_ROPE) ** -0.5
LOG2E = 1.4426950408889634
LN2 = 0.6931471805599453
SWA_SCALE = 64 ** -0.5
NEG = -1e30

ADAM_LR = 0.001
ADAM_B1 = 0.9
ADAM_B2 = 0.999
ADAM_EPS = 1e-08
ADAM_WD = 0.01
ADAM_STEP = 10

A_ZQ, A_ZKV, A_KR, A_GM, A_QS, A_KS, A_VS, A_GS, A_END = 0, 384, 640, 768, 1280, 1792, 1920, 2048, 2560
IN_SPLITS = (384, 256, 32, 512, 512, 128, 128, 512)
D_IN = sum(IN_SPLITS)

TOKEN_TILE = 512
ATT_TILE = 256
VMEM_LIMIT = 56 * 1024 * 1024


def _dot(a, b):
    return jnp.dot(a, b, preferred_element_type=F32)


def _dot_nt(a, b):
    return lax.dot_general(a, b, (((1,), (1,)), ((), ())), preferred_element_type=F32)


def _dot_tn(a, b):
    return lax.dot_general(a, b, (((0,), (0,)), ((), ())), preferred_element_type=F32)


def _params(n_grid):
    return pltpu.CompilerParams(dimension_semantics=("arbitrary",) * n_grid, vmem_limit_bytes=VMEM_LIMIT)


def _full(shape):
    nd = len(shape)
    return pl.BlockSpec(shape, lambda *_: (0,) * nd, pipeline_mode=pl.Buffered(1))


def _sigmoid(g):
    return 1.0 / (1.0 + jnp.exp(-g))


SUB_TILE = 256


def _sub_tiles(tm):
    sub = min(SUB_TILE, tm)
    return [slice(s * sub, (s + 1) * sub) for s in range(tm // sub)]


MESH = pl.DeviceIdType.MESH
ROWS_PER_DEVICE = 8
VMEM_SPEC = pl.BlockSpec(memory_space=pltpu.VMEM)
ANY_SPEC = pl.BlockSpec(memory_space=pl.ANY)


def _position():
    x, y, c = lax.axis_index("x"), lax.axis_index("y"), lax.axis_index("c")
    sibling = (x, y, 1 - c)
    others = [(1 - x, y, c), (x, 1 - y, c), (1 - x, 1 - y, c)]
    return (x, y, c), 4 * x + 2 * y + c, 2 * x + y, sibling, others


def _rows_of(dev):
    return pl.ds(pl.multiple_of(dev * ROWS_PER_DEVICE, ROWS_PER_DEVICE), ROWS_PER_DEVICE)


def _all_to_all_rows(block_ref, table_ref, dev, me, send_sems, recv_sems):
    x, y, c = me
    waits = []
    for k in range(1, 8):
        peer = (1 - x if k & 4 else x, 1 - y if k & 2 else y, 1 - c if k & 1 else c)
        pltpu.make_async_remote_copy(src_ref=block_ref, dst_ref=table_ref.at[_rows_of(dev)], send_sem=send_sems.at[k - 1],
                                     recv_sem=recv_sems.at[k - 1], device_id=peer, device_id_type=MESH).start()
        waits.append(pltpu.make_async_remote_copy(
            src_ref=block_ref, dst_ref=table_ref.at[_rows_of(jnp.bitwise_xor(dev, k))], send_sem=send_sems.at[k - 1],
            recv_sem=recv_sems.at[k - 1], device_id=peer, device_id_type=MESH))
    return waits


def _comm_fwd_call(c_blk, w_ada, shards):
    n = len(shards)

    def body(c_ref, wada_ref, *refs):
        w_refs, act_ref, pieces_ref, full_refs = refs[:n], refs[n], refs[n + 1], refs[n + 2:2 * n + 2]
        c_all_ref = refs[2 * n + 2]
        c_send, c_recv, p_send, p_recv, w_send, w_recv, f_send, f_recv, loc_sem = refs[2 * n + 3:]
        me, dev, chip, sibling, others = _position()
        core = me[2]
        chip_of = [2 * p[0] + p[1] for p in others]

        local = [pltpu.make_async_copy(w_refs[i], full_refs[i].at[chip], loc_sem.at[i]) for i in range(n)]
        for cp in local:
            cp.start()

        def over_ici(i, j, src_chip):
            return pltpu.make_async_remote_copy(
                src_ref=w_refs[i].at[core], dst_ref=full_refs[i].at[src_chip, core], send_sem=w_send.at[3 * i + j],
                recv_sem=w_recv.at[3 * i + j], device_id=others[j], device_id_type=MESH)

        def to_sibling(i, j, half):
            return pltpu.make_async_remote_copy(
                src_ref=full_refs[i].at[chip_of[j], half], dst_ref=full_refs[i].at[chip_of[j], half],
                send_sem=f_send.at[3 * i + j], recv_sem=f_recv.at[3 * i + j], device_id=sibling, device_id_type=MESH)

        c_all_ref[_rows_of(dev), :] = c_ref[...]
        c_waits = _all_to_all_rows(c_ref, c_all_ref, dev, me, c_send, c_recv)
        sent = [over_ici(i, j, chip) for i in range(n) for j in range(3)]
        for cp in sent:
            cp.start()

        for cp in c_waits:
            cp.wait()
        cv = c_all_ref[...]
        act = cv * _sigmoid(cv)
        act_ref[...] = act
        pieces_ref[chip] = _dot(act.astype(BF16), wada_ref[...].astype(BF16))
        piece = lambda j, src_chip: pltpu.make_async_remote_copy(
            src_ref=pieces_ref.at[chip], dst_ref=pieces_ref.at[src_chip], send_sem=p_send.at[j], recv_sem=p_recv.at[j],
            device_id=others[j], device_id_type=MESH)
        for j in range(3):
            piece(j, chip).start()

        for i in range(n):
            for j in range(3):
                over_ici(i, j, chip_of[j]).wait_recv()
                to_sibling(i, j, core).start()
        for j in range(3):
            piece(j, chip).wait_send()
            piece(j, chip_of[j]).wait_recv()
        for i in range(n):
            for j in range(3):
                to_sibling(i, j, 1 - core).wait_recv()
                to_sibling(i, j, core).wait_send()
        for cp in sent:
            cp.wait_send()
        for cp in local:
            cp.wait()

    rows = 8 * ROWS_PER_DEVICE
    dma = pltpu.SemaphoreType.DMA
    return pl.pallas_call(
        body, name="comm_fwd",
        out_shape=[jax.ShapeDtypeStruct((rows, D_MODEL), F32), jax.ShapeDtypeStruct((4, rows, w_ada.shape[1]), F32)]
        + [jax.ShapeDtypeStruct((4,) + s.shape, s.dtype) for s in shards],
        in_specs=[VMEM_SPEC, VMEM_SPEC] + [ANY_SPEC] * n,
        out_specs=[VMEM_SPEC, VMEM_SPEC] + [ANY_SPEC] * n,
        scratch_shapes=[pltpu.VMEM((rows, D_MODEL), F32), dma((7,)), dma((7,)), dma((3,)), dma((3,)),
                        dma((3 * n,)), dma((3 * n,)), dma((3 * n,)), dma((3 * n,)), dma((n,))],
        compiler_params=pltpu.CompilerParams(vmem_limit_bytes=VMEM_LIMIT),
    )(c_blk, w_ada, *shards)


def _comm_bwd_call(grads, part):
    n = len(grads)

    def body(part_ref, *refs):
        g_refs, f_refs, parts_ref = refs[:n], refs[n:2 * n], refs[2 * n]
        scratch = refs[2 * n + 1:]
        a_refs, b_refs, p_refs, r_refs = (scratch[k * n:(k + 1) * n] for k in range(4))
        s_send, s_recv, d_send, d_recv, e_send, e_recv, h_send, h_recv, loc_sem = scratch[4 * n:]
        me, dev, chip, sibling, others = _position()
        core = me[2]
        chip_of = [2 * p[0] + p[1] for p in others]

        parts_ref[_rows_of(dev), :] = part_ref[...]
        s_waits = _all_to_all_rows(part_ref, parts_ref, dev, me, s_send, s_recv)

        mine = [pltpu.make_async_copy(g_refs[i].at[:, core], a_refs[i], loc_sem.at[i]) for i in range(n)]
        swap = [pltpu.make_async_remote_copy(src_ref=g_refs[i].at[:, 1 - core], dst_ref=b_refs[i], send_sem=d_send.at[i],
                                             recv_sem=d_recv.at[i], device_id=sibling, device_id_type=MESH) for i in range(n)]
        order = sorted(range(n), key=lambda i: g_refs[i].shape[2] * g_refs[i].shape[3])
        for i in order:
            mine[i].start()
            swap[i].start()
        cross = [pltpu.make_async_remote_copy(src_ref=p_refs[i].at[chip_of[j]], dst_ref=r_refs[i].at[j],
                                              send_sem=e_send.at[3 * i + j], recv_sem=e_recv.at[3 * i + j],
                                              device_id=others[j], device_id_type=MESH) for i in range(n) for j in range(3)]
        for i in order:
            mine[i].wait()
            swap[i].wait()
            for k in range(4):
                s = a_refs[i][k] + b_refs[i][k]
                a_refs[i][k] = s
                p_refs[i][k] = s.astype(BF16)
            for j in range(3):
                cross[3 * i + j].start()
        share = {}
        for i in order:
            for j in range(3):
                cross[3 * i + j].wait()
            f_refs[i][core] = (a_refs[i][chip] + r_refs[i][0].astype(F32) + r_refs[i][1].astype(F32)
                               + r_refs[i][2].astype(F32))
            share[i] = pltpu.make_async_remote_copy(src_ref=f_refs[i].at[core], dst_ref=f_refs[i].at[core],
                                                    send_sem=h_send.at[i], recv_sem=h_recv.at[i], device_id=sibling,
                                                    device_id_type=MESH)
            share[i].start()
        for i in range(n):
            share[i].wait_send()
            pltpu.make_async_remote_copy(src_ref=f_refs[i].at[core], dst_ref=f_refs[i].at[1 - core], send_sem=h_send.at[i],
                                         recv_sem=h_recv.at[i], device_id=sibling, device_id_type=MESH).wait_recv()
        for cp in s_waits:
            cp.wait()

    rows = 8 * ROWS_PER_DEVICE
    dma = pltpu.SemaphoreType.DMA
    quarter = [(4,) + g.shape[2:] for g in grads]
    return pl.pallas_call(
        body, name="comm_bwd",
        out_shape=[jax.ShapeDtypeStruct((2,) + g.shape[2:], F32) for g in grads]
        + [jax.ShapeDtypeStruct((rows, part.shape[1]), F32)],
        in_specs=[VMEM_SPEC] + [ANY_SPEC] * n,
        out_specs=[VMEM_SPEC] * (n + 1),
        scratch_shapes=[pltpu.VMEM(q, F32) for q in quarter] + [pltpu.VMEM(q, F32) for q in quarter]
        + [pltpu.VMEM(q, BF16) for q in quarter] + [pltpu.VMEM((3,) + q[1:], BF16) for q in quarter]
        + [dma((7,)), dma((7,)), dma((n,)), dma((n,)), dma((3 * n,)), dma((3 * n,)), dma((n,)), dma((n,)), dma((n,))],
        compiler_params=pltpu.CompilerParams(vmem_limit_bytes=VMEM_LIMIT),
    )(part, *grads)


def _by_owner(g, n):
    return jnp.transpose(g.reshape(g.shape[0], 4, n), (1, 0, 2)).reshape(4, 2, g.shape[0] // 2, n)


def _reduce_operands(g):
    quarter = (4,) + g.shape[2:]
    dma = pltpu.SemaphoreType.DMA
    scratch = [pltpu.VMEM(quarter, F32), pltpu.VMEM(quarter, F32), pltpu.VMEM(quarter, BF16),
               pltpu.VMEM((3,) + quarter[1:], BF16), dma((5,)), dma((5,)), dma((2,))]
    return jax.ShapeDtypeStruct((2,) + g.shape[2:], F32), scratch


def _grad_reduce(step, n_steps, g_ref, f_ref, a_ref, b_ref, p_ref, r_ref, send, recv, loc_sem):
    me, _, chip, sibling, others = _position()
    core = me[2]
    chip_of = [2 * p[0] + p[1] for p in others]
    remote = lambda src, dst, k, to: pltpu.make_async_remote_copy(
        src_ref=src, dst_ref=dst, send_sem=send.at[k], recv_sem=recv.at[k], device_id=to, device_id_type=MESH)
    mine = pltpu.make_async_copy(g_ref.at[:, core], a_ref, loc_sem.at[0])
    swap = remote(g_ref.at[:, 1 - core], b_ref, 0, sibling)
    cross = [remote(p_ref.at[chip_of[j]], r_ref.at[j], 1 + j, others[j]) for j in range(3)]
    total_ref = b_ref.at[0]
    keep = pltpu.make_async_copy(total_ref, f_ref.at[core], loc_sem.at[1])
    share = lambda half: remote(total_ref, f_ref.at[half], 4, sibling)
    at = [k * (n_steps - 1) // 3 for k in range(4)]

    @pl.when(step == at[0])
    def _():
        mine.start()
        swap.start()

    @pl.when(step == at[1])
    def _():
        mine.wait()
        swap.wait()
        for k in range(4):
            s = a_ref[k] + b_ref[k]
            a_ref[k] = s
            p_ref[k] = s.astype(BF16)
        for cp in cross:
            cp.start()

    @pl.when(step == at[2])
    def _():
        for cp in cross:
            cp.wait()
        total_ref[...] = a_ref[chip] + r_ref[0].astype(F32) + r_ref[1].astype(F32) + r_ref[2].astype(F32)
        keep.start()
        share(core).start()

    @pl.when(step == at[3])
    def _():
        keep.wait()
        share(core).wait_send()
        share(1 - core).wait_recv()


def _twice(t):
    lo = _lane_lo()
    other = pltpu.roll(t, HALF, 1)
    return jnp.concatenate([jnp.where(lo, t, other), jnp.where(lo, other, t)], axis=1)


def _once(g):
    first, second = g[:, :HEAD_LANES], g[:, HEAD_LANES:]
    return jnp.where(_lane_lo(), first + pltpu.roll(first, HALF, 1), second + pltpu.roll(second, HALF, 1))


def _rope_tables(pos_ref, inv_row, rope_ref):
    quarter = pos_ref.shape[0] // 4
    lane = lax.broadcasted_iota(jnp.int32, (1, HEAD_LANES), 1)
    pos = [pos_ref[g * quarter:(g + 1) * quarter, :] for g in range(4)]
    ang = jnp.where(lane < 32, pos[0], jnp.where(lane < 64, pos[1], jnp.where(lane < 96, pos[2], pos[3]))) * inv_row
    cos, sin = jnp.cos(ang), jnp.sin(ang)
    rope_lanes = jnp.logical_and(lane >= HALF, lane < HALF + MLA_ROPE)
    for g in range(4):
        rows = slice(g * quarter, (g + 1) * quarter)
        shift = (HALF - 32 * g) % HEAD_LANES
        at = lambda t: t if shift == 0 else pltpu.roll(t, shift, 1)
        rope_ref[rows, :HEAD_LANES] = jnp.where(rope_lanes, at(cos), 1.0)
        rope_ref[rows, HEAD_LANES:] = jnp.where(rope_lanes, at(sin), 0.0)


def _gather_in_steps(step, n_steps, w_refs, full_refs, w_send, w_recv, f_send, f_recv, loc_sem):
    me, _, chip, sibling, others = _position()
    core = me[2]
    chip_of = [2 * p[0] + p[1] for p in others]
    n = len(w_refs)
    local = [pltpu.make_async_copy(w_refs[i], full_refs[i].at[chip], loc_sem.at[i]) for i in range(n)]

    def over_ici(i, j, src_chip):
        return pltpu.make_async_remote_copy(
            src_ref=w_refs[i].at[core], dst_ref=full_refs[i].at[src_chip, core], send_sem=w_send.at[3 * i + j],
            recv_sem=w_recv.at[3 * i + j], device_id=others[j], device_id_type=MESH)

    def to_sibling(i, j, half):
        return pltpu.make_async_remote_copy(
            src_ref=full_refs[i].at[chip_of[j], half], dst_ref=full_refs[i].at[chip_of[j], half],
            send_sem=f_send.at[3 * i + j], recv_sem=f_recv.at[3 * i + j], device_id=sibling, device_id_type=MESH)

    pairs = [(i, j) for i in range(n) for j in range(3)]

    @pl.when(step == 0)
    def _():
        for cp in local:
            cp.start()
        for i, j in pairs:
            over_ici(i, j, chip).start()

    @pl.when(step == 3 * n_steps // 4)
    def _():
        for i, j in pairs:
            over_ici(i, j, chip_of[j]).wait_recv()
            to_sibling(i, j, core).start()

    @pl.when(step == n_steps - 1)
    def _():
        for i, j in pairs:
            to_sibling(i, j, 1 - core).wait_recv()
            to_sibling(i, j, core).wait_send()
            over_ici(i, j, chip).wait_send()
        for cp in local:
            cp.wait()


def _pre_call(x, pos_col, mod, b_ada, ng, inv128, wa, shards, seq):
    n_tok = x.shape[0]
    tm = min(TOKEN_TILE, seq)
    per_seq = seq // tm
    n_steps = n_tok // tm
    n = len(shards)

    def body(x_ref, pos_ref, mod_ref, bada_ref, ng_ref, inv_ref, wa_ref, *refs):
        w_refs, refs = refs[:n], refs[n:]
        zqkv_ref, zkr_ref, gates_ref, qs_ref, kd_ref, vd_ref, rope_ref = refs[:7]
        full_refs, sems = refs[7:7 + n], refs[7 + n:]
        _gather_in_steps(pl.program_id(0), n_steps, w_refs, full_refs, *sems)
        _rope_tables(pos_ref, inv_ref[...], rope_ref)
        xv = x_ref[...]
        modv = mod_ref[0] + bada_ref[...]
        shift, scale = modv[:, :D_MODEL], modv[:, D_MODEL:2 * D_MODEL]
        r1 = lax.rsqrt(jnp.mean(xv * xv, axis=-1, keepdims=True) + EPS)
        h = ((xv * r1) * ng_ref[...]) * (1.0 + scale) + shift
        za = _dot(h.astype(BF16), wa_ref[...])
        zqkv_ref[...] = za[:, :A_KR]
        zkr_ref[...] = za[:, A_KR:A_GM]
        gates_ref[:, :512] = za[:, A_GM:A_QS]
        gates_ref[:, 512:] = za[:, A_GS:A_END]
        qs_ref[...] = (za[:, A_QS:A_KS] * (SWA_SCALE * LOG2E)).astype(BF16)
        kd_ref[...] = _twice(za[:, A_KS:A_VS]).astype(BF16)
        vd_ref[...] = _twice(za[:, A_VS:A_GS]).astype(BF16)

    tok = lambda w: pl.BlockSpec((tm, w), lambda i: (i, 0))
    outs = [(640, F32), (HEAD_LANES, F32), (1024, F32), (512, BF16), (256, BF16), (256, BF16), (2 * HEAD_LANES, F32)]
    dma = pltpu.SemaphoreType.DMA
    return pl.pallas_call(
        body, name="pre", grid=(n_steps,),
        out_shape=[jax.ShapeDtypeStruct((n_tok, w), dt) for w, dt in outs]
        + [jax.ShapeDtypeStruct((4,) + s.shape, s.dtype) for s in shards],
        in_specs=[tok(D_MODEL), tok(1), pl.BlockSpec((1, 1, 3 * D_MODEL), lambda i: (i // per_seq, 0, 0)),
                  _full(b_ada.shape), _full(ng.shape), _full(inv128.shape), _full(wa.shape)] + [ANY_SPEC] * n,
        out_specs=[tok(w) for w, _ in outs] + [ANY_SPEC] * n,
        scratch_shapes=[dma((3 * n,)), dma((3 * n,)), dma((3 * n,)), dma((3 * n,)), dma((n,))],
        compiler_params=_params(1),
    )(x, pos_col, mod, b_ada, ng, inv128, wa, *shards)


def _up_call(zqkv, zkr, rope, qg, kvg, wq2, wkv, seq):
    n_tok = zqkv.shape[0]
    tm = min(TOKEN_TILE, seq)

    n_steps = n_tok // tm
    ring = 3

    def body(zqkv_hbm, zkr_hbm, rope_hbm, qg_ref, kvg_ref, wq_ref, wkv_ref, qf_ref, kf_ref, v_ref,
             zqkv_buf, zkr_buf, rope_buf, sems):
        i = pl.program_id(0)

        def fetch(step):
            static = isinstance(step, int)
            slot = step % ring if static else lax.rem(step, ring)
            rows = pl.ds(step * tm if static else pl.multiple_of(step * tm, tm), tm)
            pairs = [(zqkv_hbm, zqkv_buf), (zkr_hbm, zkr_buf), (rope_hbm, rope_buf)]
            return [pltpu.make_async_copy(src.at[rows], dst.at[slot], sems.at[k, slot]) for k, (src, dst) in enumerate(pairs)]

        @pl.when(i == 0)
        def _():
            for first in range(min(ring - 1, n_steps)):
                for cp in fetch(first):
                    cp.start()

        @pl.when(i + (ring - 1) < n_steps)
        def _():
            for cp in fetch(i + (ring - 1)):
                cp.start()

        for cp in fetch(i):
            cp.wait()
        slot = lax.rem(i, ring)
        zqkv_ref, zkr_ref, rope_ref = zqkv_buf.at[slot], zkr_buf.at[slot], rope_buf.at[slot]
        cos, sin = rope_ref[:, :HEAD_LANES], rope_ref[:, HEAD_LANES:]
        zq, zkv = zqkv_ref[:, A_ZQ:A_ZKV], zqkv_ref[:, A_ZKV:A_KR]
        rq = lax.rsqrt(jnp.mean(zq * zq, axis=-1, keepdims=True) + EPS)
        qn = ((zq * rq) * qg_ref[...]).astype(BF16)
        qr = _dot(qn, wq_ref[...])
        cf, sf = jnp.tile(cos, (1, N_HEADS)), jnp.tile(sin, (1, N_HEADS))
        qf_ref[...] = ((qr[:, :1024] * cf + qr[:, 1024:] * sf) * (MLA_SCALE * LOG2E)).astype(BF16)
        rkv = lax.rsqrt(jnp.mean(zkv * zkv, axis=-1, keepdims=True) + EPS)
        kvn = ((zkv * rkv) * kvg_ref[...]).astype(BF16)
        kv = _dot(kvn, wkv_ref[...])
        zkr = zkr_ref[...]
        kpe = jnp.where(_lane_lo(), 0.0, zkr * cos) + pltpu.roll(zkr, HALF, 1) * sin
        kf_ref[...] = (kv[:, :1024] + jnp.tile(kpe, (1, N_HEADS))).astype(BF16)
        v_ref[...] = kv[:, 1024:].astype(BF16)

    tok = lambda w: pl.BlockSpec((tm, w), lambda i: (i, 0))
    outs = [(1024, BF16), (1024, BF16), (512, BF16)]
    return pl.pallas_call(
        body, name="up", grid=(n_steps,),
        out_shape=[jax.ShapeDtypeStruct((n_tok, w), dt) for w, dt in outs],
        in_specs=[ANY_SPEC, ANY_SPEC, ANY_SPEC, _full(qg.shape), _full(kvg.shape), _full(wq2.shape), _full(wkv.shape)],
        out_specs=[tok(w) for w, _ in outs],
        scratch_shapes=[pltpu.VMEM((ring, tm, 640), F32), pltpu.VMEM((ring, tm, HEAD_LANES), F32),
                        pltpu.VMEM((ring, tm, 2 * HEAD_LANES), F32), pltpu.SemaphoreType.DMA((3, ring))],
        compiler_params=_params(1),
    )(zqkv, zkr, rope, qg, kvg, wq2, wkv)


def _lane_lo(width=HEAD_LANES):
    return lax.broadcasted_iota(jnp.int32, (1, width), 1) < HALF


def _eye(n=HEAD_LANES):
    r = lax.broadcasted_iota(jnp.int32, (n, n), 0)
    c = lax.broadcasted_iota(jnp.int32, (n, n), 1)
    return jnp.where(r == c, 1.0, 0.0).astype(BF16)


def _mla_fwd_call(qf, kf, v, n_seq, seq):
    tq = min(ATT_TILE, seq)
    nq = seq // tq

    ext = HALF + 16

    def body(q_ref, k_ref, v_ref, o_ref, lse_ref, vt_ref, acc_ref):
        i = pl.program_id(1)
        eye = _eye()

        @pl.when(i == 0)
        def _():
            for h in range(N_HEADS):
                vt_ref[h * ext + HALF:(h + 1) * ext, :] = jnp.ones((16, seq), BF16)
            for t in range(nq):
                for p in range(N_HEADS // 2):
                    pair = slice(p * HEAD_LANES, (p + 1) * HEAD_LANES)
                    v_t = _dot_nt(eye, v_ref[t * tq:(t + 1) * tq, pair]).astype(BF16)
                    for hh in range(2):
                        r0 = (2 * p + hh) * ext
                        vt_ref[r0:r0 + HALF, t * tq:(t + 1) * tq] = v_t[hh * HALF:(hh + 1) * HALF, :]

        q = q_ref[...]
        qcol = i * tq + lax.broadcasted_iota(jnp.int32, (1, tq), 1)
        heads = range(N_HEADS)
        lanes = [slice(h * HEAD_LANES, (h + 1) * HEAD_LANES) for h in heads]

        def make_step(masked, n_tiles):
            def step(kt0, carry):
                tiles = range(n_tiles)
                start = pl.multiple_of(kt0 * tq, tq)
                ks = [k_ref[pl.ds(pl.multiple_of((kt0 + t) * tq, tq), tq), :] for t in tiles]
                vt = vt_ref[:, pl.ds(start, n_tiles * tq)]
                last = n_tiles - 1
                if masked:
                    keep = ((kt0 + last) * tq + lax.broadcasted_iota(jnp.int32, (tq, 1), 0)) <= qcol

                def scores(h):
                    sts = [_dot_nt(ks[t][:, lanes[h]], q[:, lanes[h]]) for t in tiles]
                    if masked:
                        sts[last] = jnp.where(keep, sts[last], NEG)
                    return sts

                def softmax(h, sts):
                    m_old = carry[h]
                    m_new = m_old
                    for st in sts:
                        m_new = jnp.maximum(m_new, jnp.max(st, axis=0, keepdims=True))
                    pt = jnp.concatenate([jnp.exp2(st - m_new).astype(BF16) for st in sts], axis=0)
                    return m_new, jnp.exp2(m_old - m_new), pt

                def values(h, alpha, pt):
                    rows = slice(h * ext, (h + 1) * ext)
                    acc_ref[rows, :] = acc_ref[rows, :] * alpha + _dot(vt[rows, :], pt)

                sts, soft, out = {0: scores(0), 1: scores(1)}, {}, {}
                for h in range(N_HEADS + 1):
                    if h + 2 < N_HEADS:
                        sts[h + 2] = scores(h + 2)
                    if h < N_HEADS:
                        soft[h] = softmax(h, sts.pop(h))
                    if h >= 1:
                        m_new, alpha, pt = soft.pop(h - 1)
                        values(h - 1, alpha, pt)
                        out[h - 1] = m_new
                return tuple(out[h] for h in heads)
            return step

        acc_ref[...] = jnp.zeros_like(acc_ref)
        init = (jnp.full((1, tq), NEG, F32),) * N_HEADS
        count = i + 1
        carry = lax.fori_loop(0, (count + 1) // 2 - 1, lambda j, c: make_step(False, 2)(2 * j, c), init)
        carry = lax.cond(count % 2 == 0, lambda c: make_step(True, 2)(i - 1, c), lambda c: make_step(True, 1)(i, c), carry)
        dens = [acc_ref[h * ext + HALF:h * ext + HALF + 1, :] for h in heads]
        acc_t = jnp.concatenate([acc_ref[h * ext:h * ext + HALF, :] * (1.0 / dens[h]) for h in heads], axis=0)
        o_ref[...] = acc_t.T
        for h in heads:
            lse_ref[0, h // 4, h % 4:h % 4 + 1, :] = carry[h] + jnp.log2(dens[h])

    n_tok = qf.shape[0]
    return pl.pallas_call(
        body, name="mla_fwd", grid=(n_seq, nq),
        out_shape=[jax.ShapeDtypeStruct((n_tok, 512), F32), jax.ShapeDtypeStruct((n_seq, 2, 4, seq), F32)],
        in_specs=[pl.BlockSpec((tq, 1024), lambda b, i: (b * nq + i, 0)),
                  pl.BlockSpec((seq, 1024), lambda b, i: (b, 0)),
                  pl.BlockSpec((seq, 512), lambda b, i: (b, 0))],
        out_specs=[pl.BlockSpec((tq, 512), lambda b, i: (b * nq + i, 0)),
                   pl.BlockSpec((1, 2, 4, tq), lambda b, i: (b, 0, 0, i))],
        scratch_shapes=[pltpu.VMEM((N_HEADS * ext, seq), BF16), pltpu.VMEM((N_HEADS * ext, tq), F32)],
        compiler_params=_params(2),
    )(qf, kf, v)


def _mla_bwd_call(qf, kf, v, do, delta, lse, n_seq, seq):
    tq = min(ATT_TILE, seq)
    nq = seq // tq

    nh = 4
    heads = range(nh)
    lanes = [slice(h * HEAD_LANES, (h + 1) * HEAD_LANES) for h in heads]

    def body(q_ref, k_ref, v_ref, do_ref, dl_ref, lse_ref, dq_ref, dk_ref, dv_ref,
             kt_ref, dot_ref, dqt_ref, dvt_ref):
        eye = _eye()
        sub_lo = lax.broadcasted_iota(jnp.int32, (HEAD_LANES, 1), 0) < HALF

        for t in range(nq):
            r = slice(t * tq, (t + 1) * tq)
            kv = k_ref[r, :]
            for h in heads:
                kt_ref[lanes[h], r] = _dot_nt(eye, kv[:, lanes[h]]).astype(BF16)
            for p in range(nh // 2):
                dov = do_ref[r, lanes[p]]
                dt = _dot_nt(eye, dov)
                dot_ref[2 * p, :, r] = jnp.where(sub_lo, dt, 0.0).astype(BF16)
                dot_ref[2 * p + 1, :, r] = jnp.where(sub_lo, 0.0, dt).astype(BF16)
        dqt_ref[...] = jnp.zeros_like(dqt_ref)
        dvt_ref[...] = jnp.zeros_like(dvt_ref)

        def flush_dv(tile, which):
            rows = pl.ds(pl.multiple_of(tile * tq, tq), tq)
            for p in range(nh // 2):
                dv_ref[rows, lanes[p]] = dvt_ref[which, p * HEAD_LANES:(p + 1) * HEAD_LANES, :].T

        def k_step(kt, _):
            slot = kt % 2
            kr = pl.ds(pl.multiple_of(kt * tq, tq), tq)
            k = k_ref[kr, :]
            vv = v_ref[kr, :]
            k_t = kt_ref[:, kr]
            krow = kt * tq + lax.broadcasted_iota(jnp.int32, (tq, 1), 0)

            def make_step(masked, n_tiles):
                def q_step(qt0, carry):
                    tiles = range(n_tiles)
                    qrs = [pl.ds(pl.multiple_of((qt0 + t) * tq, tq), tq) for t in tiles]
                    if masked:
                        flush_dv(jnp.maximum(kt - 1, 0), 1 - slot)
                    qs = [q_ref[qr, :] for qr in qrs]
                    if masked:
                        keep = krow <= (qt0 * tq + lax.broadcasted_iota(jnp.int32, (1, tq), 1))

                    def scores(h):
                        do_ts = [dot_ref[h, :, qr] for qr in qrs]
                        sts = [_dot_nt(k[:, lanes[h]], qs[t][:, lanes[h]]) for t in tiles]
                        dpts = [_dot(vv[:, lanes[h // 2]], do_ts[t]) for t in tiles]
                        return do_ts, sts, dpts

                    def softmax(h, sts, dpts):
                        pts, dsts = [], []
                        for t in tiles:
                            pt = jnp.exp2(sts[t] - lse_ref[0, 0, h:h + 1, qrs[t]])
                            if masked and t == 0:
                                pt = jnp.where(keep, pt, 0.0)
                            dsts.append((pt * (dpts[t] - dl_ref[0, h:h + 1, qrs[t]])).astype(BF16))
                            pts.append(pt.astype(BF16))
                        return pts, dsts

                    def grads(h, do_ts, pts, dsts):
                        half = slice((h % 2) * HALF, (h % 2 + 1) * HALF)
                        dst_all = jnp.concatenate(dsts, axis=1)
                        pt_all = jnp.concatenate(pts, axis=1)
                        do_all = jnp.concatenate([do_ts[t][half, :] for t in tiles], axis=1)
                        q_all = jnp.concatenate([qs[t][:, lanes[h]] for t in tiles], axis=0)
                        dvt_ref[slot, h * HALF:(h + 1) * HALF, :] += _dot_nt(do_all, pt_all)
                        dk_ref[kr, lanes[h]] += _dot(dst_all, q_all)
                        for t in tiles:
                            dqt_ref[lanes[h], qrs[t]] += _dot(k_t[lanes[h], :], dsts[t])

                    first, second = {0: scores(0)}, {}
                    for h in range(nh + 1):
                        if h + 1 < nh:
                            first[h + 1] = scores(h + 1)
                        if h < nh:
                            do_ts, sts, dpts = first.pop(h)
                            second[h] = (do_ts,) + softmax(h, sts, dpts)
                        if h >= 1:
                            grads(h - 1, *second.pop(h - 1))
                    return carry
                return q_step

            dk_ref[kr, :] = jnp.zeros((tq, nh * HEAD_LANES), F32)
            dvt_ref[slot] = jnp.zeros(dvt_ref.shape[1:], F32)
            count = nq - kt
            lax.cond(count >= 2, lambda c: make_step(True, 2)(kt, c), lambda c: make_step(True, 1)(kt, c), 0)
            lax.fori_loop(1, count // 2, lambda j, c: make_step(False, 2)(kt + 2 * j, c), 0)
            lax.cond(jnp.logical_and(count % 2 == 1, count >= 3), lambda c: make_step(False, 1)(nq - 1, c), lambda c: c, 0)
            return 0

        lax.fori_loop(0, nq, k_step, 0)
        flush_dv(nq - 1, (nq - 1) % 2)
        for t in range(nq):
            r = slice(t * tq, (t + 1) * tq)
            for h in heads:
                dq_ref[r, lanes[h]] = dqt_ref[lanes[h], r].T

    n_tok = qf.shape[0]
    groups = N_HEADS // nh
    blk = lambda w: pl.BlockSpec((seq, w), lambda b, g: (b, g))
    return pl.pallas_call(
        body, name="mla_bwd", grid=(n_seq, groups),
        out_shape=[jax.ShapeDtypeStruct((n_tok, 1024), F32), jax.ShapeDtypeStruct((n_tok, 1024), F32),
                   jax.ShapeDtypeStruct((n_tok, 512), F32)],
        in_specs=[blk(512), blk(512), blk(256), blk(256), pl.BlockSpec((1, nh, seq), lambda b, g: (g, 0, b)),
                  pl.BlockSpec((1, 1, nh, seq), lambda b, g: (b, g, 0, 0))],
        out_specs=[blk(512), blk(512), blk(256)],
        scratch_shapes=[pltpu.VMEM((nh * HEAD_LANES, seq), BF16), pltpu.VMEM((nh, HEAD_LANES, seq), BF16),
                        pltpu.VMEM((nh * HEAD_LANES, seq), F32), pltpu.VMEM((2, nh * HALF, tq), F32)],
        compiler_params=_params(2),
    )(qf, kf, v, do, delta, lse)


SWA_BLOCKS = 4


def _swa_block(n, pos_col_ref, posq):
    w = SWA_WINDOW
    start = pl.multiple_of(jnp.maximum(n - 1, 0) * w, w)
    posk = pos_col_ref[pl.ds(start, 2 * w), :]
    rel = (n * w + lax.broadcasted_iota(jnp.int32, (1, w), 1)) - (start + lax.broadcasted_iota(jnp.int32, (2 * w, 1), 0))
    valid = jnp.logical_and(rel >= 0, rel < w)
    return start, jnp.where(valid, posq - posk, 1e30)


def _alibi(h):
    return LOG2E * 2.0 ** -(h + 1)


def _transpose_rows(eye, src_ref, dst_ref, seq, width):
    step = 2 * SWA_WINDOW
    for t in range(seq // step):
        for p in range(width // HEAD_LANES):
            lanes = slice(p * HEAD_LANES, (p + 1) * HEAD_LANES)
            dst_ref[lanes, t * step:(t + 1) * step] = _dot_nt(eye, src_ref[t * step:(t + 1) * step, lanes]).astype(BF16)


def _swa_fwd_call(qs, kd, vd, pos_col, pos_row, sinks, n_seq, seq):
    w = SWA_WINDOW
    qb = SWA_BLOCKS
    steps = seq // (qb * w)
    ext = HALF + 16

    def body(q_ref, k_ref, v_ref, pc_ref, pr_ref, sink_ref, o_ref, lse_ref, vt_ref):
        n = pl.program_id(1)
        lo = _lane_lo()
        hi = jnp.logical_not(lo)
        eye = _eye()

        @pl.when(n == 0)
        def _():
            step = 2 * w
            for kv in range(2):
                vt_ref[kv * ext + HALF:(kv + 1) * ext, :] = jnp.ones((16, seq), BF16)
                for t in range(seq // step):
                    v_t = _dot_nt(eye, v_ref[t * step:(t + 1) * step, kv * HEAD_LANES:(kv + 1) * HEAD_LANES])
                    vt_ref[kv * ext:kv * ext + HALF, t * step:(t + 1) * step] = v_t[:HALF, :].astype(BF16)

        heads = range(N_HEADS)
        blocks = range(qb)
        geo = [_swa_block(n * qb + bi, pc_ref, pr_ref[bi]) for bi in blocks]
        wins = [pl.ds(g[0], 2 * w) for g in geo]
        kwins = [k_ref[win, :] for win in wins]
        vts = [vt_ref[:, win] for win in wins]
        sts = []
        for bi in blocks:
            q = q_ref[bi * w:(bi + 1) * w, :]
            sts.append([])
            for j in range(N_HEADS // 2):
                qp = q[:, j * HEAD_LANES:(j + 1) * HEAD_LANES]
                both = jnp.concatenate([jnp.where(lo, qp, jnp.zeros_like(qp)), jnp.where(hi, qp, jnp.zeros_like(qp))], axis=0)
                st = _dot_nt(kwins[bi][:, (j // 2) * HEAD_LANES:(j // 2 + 1) * HEAD_LANES], both)
                sts[bi] += [st[:, :w], st[:, w:]]
        ps, ms = [], []
        for bi in blocks:
            ps.append([])
            ms.append([])
            for h in heads:
                s = sts[bi][h] - _alibi(h) * geo[bi][1]
                m = jnp.maximum(jnp.max(s, axis=0, keepdims=True), sink_ref[0, h] * LOG2E)
                ps[bi].append(jnp.exp2(s - m).astype(BF16))
                ms[bi].append(m)
        for bi in blocks:
            ots = []
            for h in heads:
                pv = _dot(vts[bi][(h // 4) * ext:(h // 4 + 1) * ext, :], ps[bi][h])
                l = pv[HALF:HALF + 1, :] + jnp.exp2(sink_ref[0, h] * LOG2E - ms[bi][h])
                ots.append(pv[:HALF, :] * (1.0 / l))
                lse_ref[0, h:h + 1, bi * w:(bi + 1) * w] = ms[bi][h] + jnp.log2(l)
            o_ref[bi * w:(bi + 1) * w, :] = jnp.concatenate(ots, axis=0).T

    n_tok = qs.shape[0]
    tok = lambda width: pl.BlockSpec((qb * w, width), lambda b, n: (b * steps + n, 0))
    whole = lambda width: pl.BlockSpec((seq, width), lambda b, n: (b, 0))
    return pl.pallas_call(
        body, name="swa_fwd", grid=(n_seq, steps),
        out_shape=[jax.ShapeDtypeStruct((n_tok, 512), F32), jax.ShapeDtypeStruct((n_seq, N_HEADS, seq), F32)],
        in_specs=[tok(512), whole(256), whole(256), whole(1), pl.BlockSpec((qb, 1, w), lambda b, n: (b * steps + n, 0, 0)),
                  pl.BlockSpec(memory_space=pltpu.SMEM)],
        out_specs=[tok(512), pl.BlockSpec((1, N_HEADS, qb * w), lambda b, n: (b, 0, n))],
        scratch_shapes=[pltpu.VMEM((2 * ext, seq), BF16)],
        compiler_params=_params(2),
    )(qs, kd, vd, pos_col, pos_row, sinks)


def _swa_bwd_call(qs, kd, vd, do, delta, lse, pos_col, pos_row, sinks, g_out, n_seq, seq):
    w = SWA_WINDOW
    qb = SWA_BLOCKS
    steps = seq // (qb * w)
    reduced, reduce_scratch = _reduce_operands(g_out)

    def body(q_ref, k_ref, v_ref, do_ref, dl_ref, lse_ref, pc_ref, pr_ref, sink_ref, g_ref, dq_ref, dk_ref, dv_ref,
             dsink_ref, f_ref, kt_ref, *reduce_refs):
        b, n = pl.program_id(0), pl.program_id(1)
        _grad_reduce(b * steps + n, n_seq * steps, g_ref, f_ref, *reduce_refs)
        lo = _lane_lo()
        hi = jnp.logical_not(lo)
        sub_lo = lax.broadcasted_iota(jnp.int32, (HEAD_LANES, 1), 0) < HALF
        eye = _eye()

        @pl.when(n == 0)
        def _():
            dk_ref[...] = jnp.zeros_like(dk_ref)
            dv_ref[...] = jnp.zeros_like(dv_ref)
            _transpose_rows(eye, k_ref, kt_ref, seq, 2 * HEAD_LANES)

        @pl.when(jnp.logical_and(n == 0, b == 0))
        def _():
            dsink_ref[...] = jnp.zeros_like(dsink_ref)

        heads = range(N_HEADS)
        blocks = range(qb)
        kv_lanes = lambda h: slice((h // 4) * HEAD_LANES, (h // 4 + 1) * HEAD_LANES)
        geo = [_swa_block(n * qb + bi, pc_ref, pr_ref[bi]) for bi in blocks]
        wins = [pl.ds(g[0], 2 * w) for g in geo]
        kwins = [k_ref[win, :] for win in wins]
        vwins = [v_ref[win, :] for win in wins]

        do_ts, deltas, qms, doms = [], [], [], []
        for bi in blocks:
            rows = slice(bi * w, (bi + 1) * w)
            for lst in (do_ts, deltas, qms, doms):
                lst.append([])
            for j in range(N_HEADS // 2):
                pair = slice(j * HEAD_LANES, (j + 1) * HEAD_LANES)
                dop = do_ref[rows, pair]
                qp = q_ref[rows, pair]
                dt = _dot_nt(eye, dop)
                for hh in range(2):
                    half = lo if hh == 0 else hi
                    do_ts[bi].append(jnp.where(sub_lo, dt, 0.0).astype(BF16) if hh == 0
                                     else jnp.where(sub_lo, 0.0, dt).astype(BF16))
                    deltas[bi].append(dl_ref[2 * j + hh:2 * j + hh + 1, rows])
                    qms[bi].append(jnp.where(half, qp, jnp.zeros_like(qp)))
                    doms[bi].append(jnp.where(half, dop, jnp.zeros_like(dop)))
        sts, dpts = [], []
        for bi in blocks:
            sts.append([])
            dpts.append([])
            for j in range(N_HEADS // 2):
                a, b = 2 * j, 2 * j + 1
                st = _dot_nt(kwins[bi][:, kv_lanes(a)], jnp.concatenate([qms[bi][a], qms[bi][b]], axis=0))
                dpt = _dot(vwins[bi][:, kv_lanes(a)], jnp.concatenate([do_ts[bi][a], do_ts[bi][b]], axis=1))
                sts[bi] += [st[:, :w], st[:, w:]]
                dpts[bi] += [dpt[:, :w], dpt[:, w:]]
        pts, dsts = [], []
        for bi in blocks:
            pts.append([])
            dsts.append([])
            for h in heads:
                lse_h = lse_ref[0, h:h + 1, bi * w:(bi + 1) * w]
                pt = jnp.exp2(sts[bi][h] - _alibi(h) * geo[bi][1] - lse_h)
                dsts[bi].append((pt * (dpts[bi][h] - deltas[bi][h])).astype(BF16))
                pts[bi].append(pt.astype(BF16))
                dsink_ref[h:h + 1, :] += -jnp.exp2(sink_ref[0, h] * LOG2E - lse_h) * deltas[bi][h]
        for bi in blocks:
            for kv in range(2):
                group = range(4 * kv, 4 * kv + 4)
                dst_all = jnp.concatenate([dsts[bi][h] for h in group], axis=1)
                pt_all = jnp.concatenate([pts[bi][h] for h in group], axis=1)
                q_all = jnp.concatenate([qms[bi][h] for h in group], axis=0)
                do_all = jnp.concatenate([doms[bi][h] for h in group], axis=0)
                dk_ref[wins[bi], kv_lanes(4 * kv)] += _dot(dst_all, q_all)
                dv_ref[wins[bi], kv_lanes(4 * kv)] += _dot(pt_all, do_all)
        for bi in blocks:
            ktw = kt_ref[:, wins[bi]]
            for j in range(N_HEADS // 2):
                k_t = ktw[kv_lanes(2 * j), :]
                both = _dot(k_t, jnp.concatenate([dsts[bi][2 * j], dsts[bi][2 * j + 1]], axis=1))
                dq_t = jnp.where(sub_lo, both[:, :w], both[:, w:])
                dq_ref[bi * w:(bi + 1) * w, j * HEAD_LANES:(j + 1) * HEAD_LANES] = dq_t.T * SWA_SCALE

    n_tok = qs.shape[0]
    tok = lambda width: pl.BlockSpec((qb * w, width), lambda b, n: (b * steps + n, 0))
    whole = lambda width: pl.BlockSpec((seq, width), lambda b, n: (b, 0))
    return pl.pallas_call(
        body, name="swa_bwd", grid=(n_seq, steps),
        out_shape=[jax.ShapeDtypeStruct((n_tok, 512), F32), jax.ShapeDtypeStruct((n_tok, 256), F32),
                   jax.ShapeDtypeStruct((n_tok, 256), F32), jax.ShapeDtypeStruct((N_HEADS, HEAD_LANES), F32), reduced],
        in_specs=[tok(512), whole(256), whole(256), pl.BlockSpec((qb * w, 512), lambda b, n: (b * steps + n, 1)),
                  pl.BlockSpec((N_HEADS, qb * w), lambda b, n: (0, b * steps + n)),
                  pl.BlockSpec((1, N_HEADS, qb * w), lambda b, n: (b, 0, n)),
                  whole(1), pl.BlockSpec((qb, 1, w), lambda b, n: (b * steps + n, 0, 0)),
                  pl.BlockSpec(memory_space=pltpu.SMEM), ANY_SPEC],
        out_specs=[tok(512), whole(256), whole(256), _full((N_HEADS, HEAD_LANES)), ANY_SPEC],
        scratch_shapes=[pltpu.VMEM((2 * HEAD_LANES, seq), BF16)] + reduce_scratch,
        compiler_params=_params(2),
    )(qs, kd, vd, do, delta, lse, pos_col, pos_row, sinks, g_out)


def _post_call(x, target, o_mla, o_swa, gates, mod, b_ada, fg, w_out, seq):
    n_tok = x.shape[0]
    tm = min(TOKEN_TILE, seq)
    per_seq = seq // tm
    n_seq = n_tok // seq

    def body(x_ref, t_ref, om_ref, os_ref, g_ref, mod_ref, bada_ref, fg_ref, w_ref,
             dx2_ref, do_ref, dg_ref, gw_ref, gfg_ref, dgate_ref, loss_ref, dmla_ref, dswa_ref):
        i = pl.program_id(0)

        @pl.when(i == 0)
        def _():
            gw_ref[...] = jnp.zeros_like(gw_ref)
            gfg_ref[...] = jnp.zeros_like(gfg_ref)
            loss_ref[...] = jnp.zeros_like(loss_ref)

        @pl.when(i % per_seq == 0)
        def _():
            dgate_ref[...] = jnp.zeros_like(dgate_ref)

        gate = mod_ref[0][:, 2 * D_MODEL:] + bada_ref[:, 2 * D_MODEL:]
        fgv = fg_ref[...]
        fgd = fgv * (1.0 / D_MODEL)
        subs = _sub_tiles(tm)
        gs = [g_ref[r, :] for r in subs]
        os_ = [jnp.concatenate([om_ref[r, :], os_ref[r, :]], axis=-1) for r in subs]
        sgs = [_sigmoid(g) for g in gs]
        sils = [g * sg for g, sg in zip(gs, sgs)]
        ypres = [(o * sil).astype(BF16) for o, sil in zip(os_, sils)]
        ys = [_dot(ypre, w_ref[...]) for ypre in ypres]
        dys, loss, gfg, dgate = [], 0.0, 0.0, 0.0
        for r, y in zip(subs, ys):
            x2 = x_ref[r, :] + gate * y
            r2 = lax.rsqrt(jnp.mean(x2 * x2, axis=-1, keepdims=True) + EPS)
            xn2 = x2 * r2
            err = xn2 * fgv - t_ref[r, :]
            loss = loss + jnp.sum(jnp.sum(err * err, axis=-1, keepdims=True), axis=0, keepdims=True)
            gfg = gfg + jnp.sum(err * xn2, axis=0, keepdims=True)
            dxn2 = err * fgd
            dx2 = r2 * (dxn2 - xn2 * jnp.mean(dxn2 * xn2, axis=-1, keepdims=True))
            dx2_ref[r, :] = dx2
            dgate = dgate + jnp.sum(dx2 * y, axis=0, keepdims=True)
            dys.append((dx2 * gate).astype(BF16))
        loss_ref[...] += jnp.broadcast_to(loss * (0.5 / D_MODEL), loss_ref.shape)
        gfg_ref[...] += gfg * (1.0 / D_MODEL)
        dgate_ref[0] += dgate
        gw_ref[...] += _dot_tn(jnp.concatenate(ypres, axis=0), jnp.concatenate(dys, axis=0))
        dypres = [_dot_nt(dy, w_ref[...]) for dy in dys]
        pick = jnp.where(jnp.right_shift(lax.broadcasted_iota(jnp.int32, (2 * N_HEADS, D_MODEL), 1), 6)
                         == lax.broadcasted_iota(jnp.int32, (2 * N_HEADS, D_MODEL), 0), 1.0, 0.0).astype(BF16)
        for r, dypre, o, g, sg, sil in zip(subs, dypres, os_, gs, sgs, sils):
            dov = (dypre * sil).astype(BF16)
            do_ref[r, :] = dov
            delta = _dot_nt(pick, (dov.astype(F32) * o).astype(BF16))
            for grp in range(2):
                dmla_ref[grp, :, r] = delta[4 * grp:4 * grp + 4, :]
            dswa_ref[:, r] = delta[N_HEADS:, :]
            dg_ref[r, :] = (dypre * o * (sg + sil * (1.0 - sg))).astype(BF16)

    tok = lambda w: pl.BlockSpec((tm, w), lambda i: (i, 0))
    per_b = pl.BlockSpec((1, 1, 3 * D_MODEL), lambda i: (i // per_seq, 0, 0))
    return pl.pallas_call(
        body, name="post", grid=(n_tok // tm,),
        out_shape=[jax.ShapeDtypeStruct((n_tok, D_MODEL), F32), jax.ShapeDtypeStruct((n_tok, D_MODEL), BF16),
                   jax.ShapeDtypeStruct((n_tok, D_MODEL), BF16), jax.ShapeDtypeStruct((D_MODEL, D_MODEL), F32),
                   jax.ShapeDtypeStruct((1, D_MODEL), F32), jax.ShapeDtypeStruct((n_seq, 1, D_MODEL), F32),
                   jax.ShapeDtypeStruct((1, HEAD_LANES), F32),
                   jax.ShapeDtypeStruct((2, N_HEADS // 2, n_tok), F32), jax.ShapeDtypeStruct((N_HEADS, n_tok), F32)],
        in_specs=[tok(D_MODEL), tok(D_MODEL), tok(512), tok(512), tok(D_MODEL), per_b, _full(b_ada.shape),
                  _full(fg.shape), _full(w_out.shape)],
        out_specs=[tok(D_MODEL), tok(D_MODEL), tok(D_MODEL), _full((D_MODEL, D_MODEL)), _full((1, D_MODEL)),
                   pl.BlockSpec((1, 1, D_MODEL), lambda i: (i // per_seq, 0, 0)), _full((1, HEAD_LANES)),
                   pl.BlockSpec((2, N_HEADS // 2, tm), lambda i: (0, 0, i)), pl.BlockSpec((N_HEADS, tm), lambda i: (0, i))],
        compiler_params=_params(1),
    )(x, target, o_mla, o_swa, gates, mod, b_ada, fg, w_out)


def _mid_bwd_call(dqf, dkf, dv, zqkv, rope, qg, kvg, wq2, wkv, seq):
    n_tok = dqf.shape[0]
    tm = min(TOKEN_TILE, seq)

    def body(dq_ref, dk_ref, dv_ref, z_ref, rope_ref, qg_ref, kvg_ref, wq_ref, wkv_ref,
             dz_ref, gwq_ref, gwkv_ref, gqg_ref, gkvg_ref):
        i = pl.program_id(0)

        @pl.when(i == 0)
        def _():
            gwq_ref[...] = jnp.zeros_like(gwq_ref)
            gwkv_ref[...] = jnp.zeros_like(gwkv_ref)
            gqg_ref[...] = jnp.zeros_like(gqg_ref)
            gkvg_ref[...] = jnp.zeros_like(gkvg_ref)

        cos, sin = rope_ref[:, :HEAD_LANES], rope_ref[:, HEAD_LANES:]
        cf, sf = jnp.tile(cos, (1, N_HEADS)), jnp.tile(sin, (1, N_HEADS))
        dq = dq_ref[...] * MLA_SCALE
        dqr = jnp.concatenate([dq * cf, dq * sf], axis=-1).astype(BF16)
        zq, zkv = z_ref[:, :Q_LORA], z_ref[:, Q_LORA:]
        qgv, kvgv = qg_ref[...], kvg_ref[...]

        rq = lax.rsqrt(jnp.mean(zq * zq, axis=-1, keepdims=True) + EPS)
        xq = zq * rq
        gwq_ref[...] += _dot_tn((xq * qgv).astype(BF16), dqr)
        dqn = _dot_nt(dqr, wq_ref[...])
        gqg_ref[...] += jnp.sum(dqn * xq, axis=0, keepdims=True)
        dxq = dqn * qgv
        dz_ref[:, :Q_LORA] = (rq * (dxq - xq * jnp.mean(dxq * xq, axis=-1, keepdims=True))).astype(BF16)

        dk = dk_ref[...] * LN2
        dkv = jnp.concatenate([dk, dv_ref[...]], axis=-1).astype(BF16)
        rkv = lax.rsqrt(jnp.mean(zkv * zkv, axis=-1, keepdims=True) + EPS)
        xkv = zkv * rkv
        gwkv_ref[...] += _dot_tn((xkv * kvgv).astype(BF16), dkv)
        dkvn = _dot_nt(dkv, wkv_ref[...])
        gkvg_ref[...] += jnp.sum(dkvn * xkv, axis=0, keepdims=True)
        dxkv = dkvn * kvgv
        dz_ref[:, Q_LORA:A_KR] = (rkv * (dxkv - xkv * jnp.mean(dxkv * xkv, axis=-1, keepdims=True))).astype(BF16)

        dkpe = dk[:, :HEAD_LANES]
        for h in range(1, N_HEADS):
            dkpe = dkpe + dk[:, h * HEAD_LANES:(h + 1) * HEAD_LANES]
        dz_ref[:, A_KR:] = (jnp.where(_lane_lo(), 0.0, dkpe * cos) + pltpu.roll(dkpe * sin, HALF, 1)).astype(BF16)

    tok = lambda w: pl.BlockSpec((tm, w), lambda i: (i, 0))
    return pl.pallas_call(
        body, name="mid_bwd", grid=(n_tok // tm,),
        out_shape=[jax.ShapeDtypeStruct((n_tok, A_GM), BF16),
                   jax.ShapeDtypeStruct(wq2.shape, F32), jax.ShapeDtypeStruct(wkv.shape, F32),
                   jax.ShapeDtypeStruct((1, Q_LORA), F32), jax.ShapeDtypeStruct((1, KV_LORA), F32)],
        in_specs=[tok(1024), tok(1024), tok(512), tok(640), tok(2 * HEAD_LANES), _full(qg.shape), _full(kvg.shape),
                  _full(wq2.shape), _full(wkv.shape)],
        out_specs=[tok(A_GM), _full(wq2.shape), _full(wkv.shape), _full((1, Q_LORA)), _full((1, KV_LORA))],
        compiler_params=_params(1),
    )(dqf, dkf, dv, zqkv, rope, qg, kvg, wq2, wkv)


def _in_bwd_call(x, dx2, dz, dg, dqs, dkd, dvd, mod, b_ada, ng, wa, seq):
    n_tok = x.shape[0]
    tm = min(TOKEN_TILE, seq)
    per_seq = seq // tm
    n_seq = n_tok // seq

    def body(x_ref, dx2_ref, dz_ref, dg_ref, dqs_ref, dkd_ref, dvd_ref, mod_ref, bada_ref, ng_ref,
             wa_ref, gx_ref, gwa_ref, gng_ref, dshift_ref, dscale_ref):
        i = pl.program_id(0)

        @pl.when(i == 0)
        def _():
            gwa_ref[...] = jnp.zeros_like(gwa_ref)
            gng_ref[...] = jnp.zeros_like(gng_ref)

        @pl.when(i % per_seq == 0)
        def _():
            dshift_ref[...] = jnp.zeros_like(dshift_ref)
            dscale_ref[...] = jnp.zeros_like(dscale_ref)

        xv = x_ref[...]
        modv = mod_ref[0] + bada_ref[...]
        shift, scale = modv[:, :D_MODEL], modv[:, D_MODEL:2 * D_MODEL]
        ngv = ng_ref[...]
        r1 = lax.rsqrt(jnp.mean(xv * xv, axis=-1, keepdims=True) + EPS)
        xn = xv * r1
        hb = ((xn * ngv) * (1.0 + scale) + shift).astype(BF16)

        dgv = dg_ref[...]
        pieces = [(A_ZQ, dz_ref[...]), (A_GM, dgv[:, :512]), (A_QS, dqs_ref[...].astype(BF16)),
                  (A_KS, jnp.concatenate([_once(dkd_ref[...]) * LN2, _once(dvd_ref[...])], axis=1).astype(BF16)),
                  (A_GS, dgv[:, 512:])]
        dh = None
        for off, piece in pieces:
            wd = piece.shape[1]
            gwa_ref[:, off:off + wd] += _dot_tn(hb, piece)
            term = _dot_nt(piece, wa_ref[:, off:off + wd])
            dh = term if dh is None else dh + term

        dshift_ref[0] += jnp.sum(dh, axis=0, keepdims=True)
        dscale_ref[0] += jnp.sum(dh * (xn * ngv), axis=0, keepdims=True)
        gng_ref[...] += jnp.sum(dh * xn * (1.0 + scale), axis=0, keepdims=True)
        dxn = dh * ngv * (1.0 + scale)
        gx_ref[...] = dx2_ref[...] + r1 * (dxn - xn * jnp.mean(dxn * xn, axis=-1, keepdims=True))

    tok = lambda w: pl.BlockSpec((tm, w), lambda i: (i, 0))
    per_b = lambda w: pl.BlockSpec((1, 1, w), lambda i: (i // per_seq, 0, 0))
    return pl.pallas_call(
        body, name="in_bwd", grid=(n_tok // tm,),
        out_shape=[jax.ShapeDtypeStruct((n_tok, D_MODEL), F32), jax.ShapeDtypeStruct((D_MODEL, A_END), F32),
                   jax.ShapeDtypeStruct((1, D_MODEL), F32),
                   jax.ShapeDtypeStruct((n_seq, 1, D_MODEL), F32), jax.ShapeDtypeStruct((n_seq, 1, D_MODEL), F32)],
        in_specs=[tok(D_MODEL), tok(D_MODEL), tok(A_GM), tok(D_MODEL), tok(512), tok(256), tok(256),
                  per_b(3 * D_MODEL), _full(b_ada.shape), _full(ng.shape), _full(wa.shape)],
        out_specs=[tok(D_MODEL), _full((D_MODEL, A_END)), _full((1, D_MODEL)), per_b(D_MODEL), per_b(D_MODEL)],
        compiler_params=_params(1),
    )(x, dx2, dz, dg, dqs, dkd, dvd, mod, b_ada, ng, wa)


def _adam_math(w, g, m, v):
    m_new = ADAM_B1 * m + (1.0 - ADAM_B1) * g
    v_new = ADAM_B2 * v + (1.0 - ADAM_B2) * (g * g)
    m_hat = m_new / (1.0 - ADAM_B1 ** ADAM_STEP)
    v_hat = v_new / (1.0 - ADAM_B2 ** ADAM_STEP)
    delta = -ADAM_LR * (m_hat / (jnp.sqrt(v_hat) + ADAM_EPS) + ADAM_WD * w)
    return delta, m_new, v_new


def _adam_group_call(name, groups):
    n = len(groups)

    def body(*refs):
        ins, outs = refs[:4 * n], refs[4 * n:]
        for k in range(n):
            w_ref, g_ref, m_ref, v_ref = ins[4 * k:4 * k + 4]
            d, mn, vn = _adam_math(w_ref[...], g_ref[...], m_ref[...], v_ref[...])
            outs[3 * k][...] = d
            outs[3 * k + 1][...] = mn
            outs[3 * k + 2][...] = vn

    flat = [t for group in groups for t in group]
    shapes = [group[0].shape for group in groups for _ in range(3)]
    res = pl.pallas_call(
        body, name=name, grid=(1,),
        out_shape=[jax.ShapeDtypeStruct(s, F32) for s in shapes],
        in_specs=[_full(t.shape) for t in flat], out_specs=[_full(s) for s in shapes],
        compiler_params=_params(1),
    )(*flat)
    return [res[3 * k:3 * k + 3] for k in range(n)]


def _ada_bwd_call(act_all, dmod_cols, w, m, v):
    rows, cols = w.shape
    tr = 512

    def body(a_ref, dm_ref, w_ref, m_ref, v_ref, g_ref, d_ref, mo_ref, vo_ref):
        g = _dot_tn(a_ref[...].astype(BF16), dm_ref[...].astype(BF16))
        d, mn, vn = _adam_math(w_ref[...], g, m_ref[...], v_ref[...])
        g_ref[...] = g
        d_ref[...] = d
        mo_ref[...] = mn
        vo_ref[...] = vn

    spec = pl.BlockSpec((tr, cols), lambda i: (i, 0))
    nb = act_all.shape[0]
    return pl.pallas_call(
        body, name="ada_bwd", grid=(rows // tr,),
        out_shape=[jax.ShapeDtypeStruct(w.shape, F32)] * 4,
        in_specs=[pl.BlockSpec((nb, tr), lambda i: (0, i)), _full(dmod_cols.shape), spec, spec, spec],
        out_specs=[spec] * 4,
        compiler_params=_params(1),
    )(act_all, dmod_cols, w, m, v)


SMALL_ROW = {"norm_gain": (0, 1024), "final_gain": (1024, 2048), "q_norm_gain": (2048, 2432),
             "kv_norm_gain": (2432, 2688), "swa_sinks": (2688, 2696), "loss": (2816, 2944)}
SMALL_ORDER = ("b_ada", "norm_gain", "q_norm_gain", "kv_norm_gain", "swa_sinks", "final_gain")


def _small_call(parts_all, n_seq, params):
    k = len(params)

    def body(p_ref, *refs):
        ins, outs, loss_ref = refs[:3 * k], refs[3 * k:7 * k], refs[7 * k]
        row = p_ref[n_seq:n_seq + 1, :]
        for dv in range(1, 8):
            r0 = dv * ROWS_PER_DEVICE + n_seq
            row = row + p_ref[r0:r0 + 1, :]
        gb = None
        for dv in range(8):
            for r in range(n_seq):
                r0 = dv * ROWS_PER_DEVICE + r
                gb = p_ref[r0:r0 + 1, :] if gb is None else gb + p_ref[r0:r0 + 1, :]
        for j, name in enumerate(SMALL_ORDER):
            g = gb if name == "b_ada" else row[:, SMALL_ROW[name][0]:SMALL_ROW[name][1]]
            d, mn, vn = _adam_math(ins[3 * j][...], g, ins[3 * j + 1][...], ins[3 * j + 2][...])
            outs[4 * j][...] = g
            outs[4 * j + 1][...] = d
            outs[4 * j + 2][...] = mn
            outs[4 * j + 3][...] = vn
        loss_ref[...] = row[:, SMALL_ROW["loss"][0]:SMALL_ROW["loss"][1]]

    flat = [t for p in params for t in p]
    res = pl.pallas_call(
        body, name="small_update", grid=(1,),
        out_shape=[jax.ShapeDtypeStruct(p[0].shape, F32) for p in params for _ in range(4)]
        + [jax.ShapeDtypeStruct((1, HEAD_LANES), F32)],
        in_specs=[_full(parts_all.shape)] + [_full(t.shape) for t in flat],
        out_specs=[_full(p[0].shape) for p in params for _ in range(4)] + [_full((1, HEAD_LANES))],
        compiler_params=_params(1),
    )(parts_all, *flat)
    return [res[4 * j:4 * j + 4] for j in range(k)], res[4 * k]


def _rot(t):
    half = t.shape[-1] // 2
    return jnp.concatenate([-t[..., half:], t[..., :half]], axis=-1)


def _rot_t(g):
    half = g.shape[-1] // 2
    return jnp.concatenate([g[..., half:], -g[..., :half]], axis=-1)


def _columns(segments, lo, hi):
    out, at = [], 0
    for seg in segments:
        n = seg.shape[1]
        a, b = max(lo, at), min(hi, at + n)
        if a < b:
            out.append(seg[:, a - at:b - at])
        at += n
    return out


def _prepare_in(w_in_blocks):
    o = [0]
    for s in IN_SPLITS:
        o.append(o[-1] + s)
    part = lambda a, b: _columns(w_in_blocks, a, b)
    kr = jnp.concatenate(part(o[2], o[3]), axis=1)
    zero = jnp.zeros((kr.shape[0], 32), kr.dtype)
    return jnp.concatenate(part(0, o[2]) + [_rot(kr), zero, kr, zero] + part(o[3], o[8]), axis=1)


def _prepare_up(w_uq, w_ukv):
    uq = w_uq.reshape(Q_LORA, N_HEADS, MLA_NOPE + MLA_ROPE)
    zq = jnp.zeros((Q_LORA, N_HEADS, 32), w_uq.dtype)
    uq_full = jnp.concatenate([uq, zq], axis=-1).reshape(Q_LORA, 1024)
    uq_rot = jnp.concatenate([jnp.zeros((Q_LORA, N_HEADS, 64), w_uq.dtype), _rot(uq[..., MLA_NOPE:]), zq],
                             axis=-1).reshape(Q_LORA, 1024)
    wq2 = jnp.concatenate([uq_full, uq_rot], axis=1)
    ukv = w_ukv.reshape(KV_LORA, N_HEADS, 128)
    k_full = jnp.concatenate([ukv[..., :64], jnp.zeros((KV_LORA, N_HEADS, 64), w_ukv.dtype)], axis=-1).reshape(KV_LORA, 1024)
    wkv = jnp.concatenate([k_full, ukv[..., 64:].reshape(KV_LORA, 512)], axis=1)
    return wq2, wkv


def _restore_in(gwa):
    gkr = gwa[:, A_KR + 64:A_KR + 96] + _rot_t(gwa[:, A_KR:A_KR + 32])
    in_order = [gwa[:, :A_KR], gkr, gwa[:, A_GM:]]
    n = D_IN // 4
    return [jnp.concatenate(_columns(in_order, k * n, (k + 1) * n), axis=1) for k in range(4)]


def _restore_up(gwq2, gwkv):
    gf = gwq2[:, :1024].reshape(Q_LORA, N_HEADS, 128)
    gr = gwq2[:, 1024:].reshape(Q_LORA, N_HEADS, 128)
    g_uq = jnp.concatenate([gf[..., :64], gf[..., 64:96] + _rot_t(gr[..., 64:96])], axis=-1).reshape(Q_LORA, 768)
    gk = gwkv[:, :1024].reshape(KV_LORA, N_HEADS, 128)[..., :64]
    gv = gwkv[:, 1024:].reshape(KV_LORA, N_HEADS, 64)
    g_ukv = jnp.concatenate([gk, gv], axis=-1).reshape(KV_LORA, 1024)
    return g_uq, g_ukv


def _local_step(x, positions, target, mod_rows, b_ada, ng, qg, kvg, sinks, fg, w_in_b, later_shards):
    n_seq, seq, _ = x.shape
    n_tok = n_seq * seq
    x2d = x.reshape(n_tok, D_MODEL)
    t2d = target.reshape(n_tok, D_MODEL)
    pos_f = positions.astype(F32)
    pos_col = pos_f.reshape(n_tok, 1)
    pos_row = pos_f.reshape(n_tok // SWA_WINDOW, 1, SWA_WINDOW)
    mod3 = mod_rows.reshape(n_seq, 1, 3 * D_MODEL)
    inv = ROPE_THETA ** (-jnp.arange(0, MLA_ROPE, 2, dtype=F32) / MLA_ROPE)
    inv128 = jnp.tile(jnp.concatenate([inv, inv]), 4).reshape(1, HEAD_LANES)
    fg2 = fg.reshape(1, D_MODEL)

    wa = _prepare_in(w_in_b)
    zqkv, zkr, gates, qs, kd, vd, rope, f_uq, f_ukv, f_out = _pre_call(x2d, pos_col, mod3, b_ada, ng, inv128, wa,
                                                                       later_shards, seq)
    cols = lambda t, r: jnp.transpose(t.reshape(4, r, -1), (1, 0, 2)).reshape(r, -1)
    wq2, wkv = _prepare_up(cols(f_uq, Q_LORA), cols(f_ukv, KV_LORA))
    w_out_b = f_out.reshape(D_MODEL, D_MODEL)
    qf, kf, v = _up_call(zqkv, zkr, rope, qg, kvg, wq2, wkv, seq)
    o_mla, lse_mla = _mla_fwd_call(qf, kf, v, n_seq, seq)
    o_swa, lse_swa = _swa_fwd_call(qs, kd, vd, pos_col, pos_row, sinks, n_seq, seq)
    dx2, do, dg, g_out, g_fg, dgate, loss, delta_mla, delta_swa = _post_call(x2d, t2d, o_mla, o_swa, gates, mod3, b_ada, fg2, w_out_b, seq)
    dqf, dkf, dv = _mla_bwd_call(qf, kf, v, do, delta_mla, lse_mla, n_seq, seq)
    dqs, dkd, dvd, dsink, r_out = _swa_bwd_call(qs, kd, vd, do, delta_swa, lse_swa, pos_col, pos_row, sinks,
                                                g_out.reshape(4, 2, D_MODEL // 8, D_MODEL), n_seq, seq)
    dz, g_wq2, g_wkv, g_qg, g_kvg = _mid_bwd_call(dqf, dkf, dv, zqkv, rope, qg, kvg, wq2, wkv, seq)
    gx, g_wa, g_ng, dshift, dscale = _in_bwd_call(x2d, dx2, dz, dg, dqs, dkd, dvd, mod3, b_ada, ng, wa, seq)
    g_in = _restore_in(g_wa)
    g_uq, g_ukv = _restore_up(g_wq2, g_wkv)
    dmod = jnp.concatenate([dshift, dscale, dgate], axis=-1).reshape(n_seq, 3 * D_MODEL)
    small_row = jnp.concatenate([g_ng, g_fg, g_qg, g_kvg, jnp.pad(jnp.sum(dsink, axis=1).reshape(1, N_HEADS), ((0, 0), (0, 120))),
                                 loss, jnp.zeros((1, 128), F32)], axis=1)
    return gx.reshape(x.shape), (g_in, g_uq, g_ukv), r_out, small_row, dmod


def kernel(x, c, positions, w_ada, b_ada, norm_gain, w_in, q_norm_gain, kv_norm_gain, w_uq, w_ukv, swa_sinks, w_out, final_gain, loss_target, m_w_ada, m_b_ada, m_norm_gain, m_w_in, m_q_norm_gain, m_kv_norm_gain, m_w_uq, m_w_ukv, m_swa_sinks, m_w_out, m_final_gain, v_w_ada, v_b_ada, v_norm_gain, v_w_in, v_q_norm_gain, v_kv_norm_gain, v_w_uq, v_w_ukv, v_swa_sinks, v_w_out, v_final_gain):
    n_seq = x.shape[0]
    xi, yi, ci = lax.axis_index("x"), lax.axis_index("y"), lax.axis_index("c")
    dev = 4 * xi + 2 * yi + ci
    chip = 2 * xi + yi

    halves = lambda w: w.astype(BF16).reshape(2, w.shape[0] // 2, w.shape[1])
    c_blk = jnp.pad(c, ((0, ROWS_PER_DEVICE - n_seq), (0, 0)))
    act_all, pieces, f_in = _comm_fwd_call(c_blk, w_ada[0], [halves(w_in[0])])
    mine = lax.dynamic_slice_in_dim(pieces, dev * ROWS_PER_DEVICE, n_seq, axis=1)
    mod_rows = jnp.transpose(mine, (1, 0, 2)).reshape(n_seq, 3 * D_MODEL)
    w_in_blocks = [f_in[k].reshape(D_MODEL, -1) for k in range(4)]

    gx, (g_in_blocks, g_uq, g_ukv), r_out, small_row, dmod = _local_step(
        x, positions, loss_target, mod_rows, b_ada, norm_gain, q_norm_gain, kv_norm_gain, swa_sinks, final_gain,
        w_in_blocks, [halves(w_uq[0]), halves(w_ukv[0]), halves(w_out[0])])

    grads = [jnp.stack(g_in_blocks).reshape(4, 2, D_MODEL // 2, -1), _by_owner(g_uq, g_uq.shape[1] // 4),
             _by_owner(g_ukv, g_ukv.shape[1] // 4)]
    part = jnp.concatenate([dmod, small_row, jnp.zeros((ROWS_PER_DEVICE - n_seq - 1, 3 * D_MODEL), F32)], axis=0)
    r_in, r_uq, r_ukv, parts_all = _comm_bwd_call(grads, part)
    g_in_s, g_uq_s = r_in.reshape(w_in.shape[1:]), r_uq.reshape(w_uq.shape[1:])
    g_ukv_s, g_out_s = r_ukv.reshape(w_ukv.shape[1:]), r_out.reshape(w_out.shape[1:])

    tr = lambda a: jnp.swapaxes(a[0], 0, 1)
    back = lambda ts: [jnp.swapaxes(t, 0, 1) for t in ts]
    in_t, uq_t, (d_ukv, nm_ukv, nv_ukv), (d_out, nm_out, nv_out) = _adam_group_call(
        "adam_weights", [(tr(w_in), g_in_s.T, tr(m_w_in), tr(v_w_in)), (tr(w_uq), g_uq_s.T, tr(m_w_uq), tr(v_w_uq)),
                         (w_ukv[0], g_ukv_s, m_w_ukv[0], v_w_ukv[0]), (w_out[0], g_out_s, m_w_out[0], v_w_out[0])])
    d_in, nm_in, nv_in = back(in_t)
    d_uq, nm_uq, nv_uq = back(uq_t)
    dmod_cols = lax.dynamic_slice_in_dim(parts_all, chip * 768, 768, axis=1)
    g_ada, d_ada, nm_ada, nv_ada = _ada_bwd_call(act_all, dmod_cols, w_ada[0], m_w_ada[0], v_w_ada[0])

    row = lambda t: t.reshape(1, -1)
    small = {"b_ada": (b_ada, m_b_ada, v_b_ada), "norm_gain": (norm_gain, m_norm_gain, v_norm_gain),
             "q_norm_gain": (q_norm_gain, m_q_norm_gain, v_q_norm_gain),
             "kv_norm_gain": (kv_norm_gain, m_kv_norm_gain, v_kv_norm_gain),
             "swa_sinks": (swa_sinks, m_swa_sinks, v_swa_sinks),
             "final_gain": (row(final_gain), row(m_final_gain), row(v_final_gain))}
    res, loss_row = _small_call(parts_all, n_seq, [small[name] for name in SMALL_ORDER])
    res = dict(zip(SMALL_ORDER, res))
    res["final_gain"] = [t.reshape(-1) for t in res["final_gain"]]
    e = lambda t: t[None]
    big = {"w_ada": (e(g_ada), e(d_ada), e(nm_ada), e(nv_ada)), "w_in": (e(g_in_s), e(d_in), e(nm_in), e(nv_in)),
           "w_uq": (e(g_uq_s), e(d_uq), e(nm_uq), e(nv_uq)), "w_ukv": (e(g_ukv_s), e(d_ukv), e(nm_ukv), e(nv_ukv)),
           "w_out": (e(g_out_s), e(d_out), e(nm_out), e(nv_out))}
    order = ("w_ada", "b_ada", "norm_gain", "w_in", "q_norm_gain", "kv_norm_gain", "w_uq", "w_ukv", "swa_sinks", "w_out",
             "final_gain")
    pick = lambda kind: [(big[n] if n in big else res[n])[kind] for n in order]
    return (loss_row[0, 0], gx, *pick(0), *pick(1), *pick(2), *pick(3))
```

```python
import jax
import jax.numpy as jnp
from jax import lax
from jax.experimental import pallas as pl
from jax.experimental.pallas import tpu as pltpu

F32 = jnp.float32
BF16 = jnp.bfloat16

D_MODEL = 1024
Q_LORA = 384
KV_LORA = 256
N_HEADS = 8
MLA_NOPE = 64
MLA_ROPE = 32
HEAD_LANES = 128
HALF = 64
SWA_WINDOW = 128
EPS = 1e-6
ROPE_THETA = 10000.0
MLA_SCALE = (MLA_NOPE + MLA_ROPE) ** -0.5
LOG2E = 1.4426950408889634
LN2 = 0.6931471805599453
SWA_SCALE = 64 ** -0.5
NEG = -1e30

ADAM_LR = 0.001
ADAM_B1 = 0.9
ADAM_B2 = 0.999
ADAM_EPS = 1e-08
ADAM_WD = 0.01
ADAM_STEP = 10

A_ZQ, A_ZKV, A_KR, A_GM, A_QS, A_KS, A_VS, A_GS, A_END = 0, 384, 640, 768, 1280, 1792, 1920, 2048, 2560
IN_SPLITS = (384, 256, 32, 512, 512, 128, 128, 512)
D_IN = sum(IN_SPLITS)

TOKEN_TILE = 512
ATT_TILE = 256
VMEM_LIMIT = 56 * 1024 * 1024


def _dot(a, b):
    return jnp.dot(a, b, preferred_element_type=F32)


def _dot_nt(a, b):
    return lax.dot_general(a, b, (((1,), (1,)), ((), ())), preferred_element_type=F32)


def _dot_tn(a, b):
    return lax.dot_general(a, b, (((0,), (0,)), ((), ())), preferred_element_type=F32)


def _params(n_grid):
    return pltpu.CompilerParams(dimension_semantics=("arbitrary",) * n_grid, vmem_limit_bytes=VMEM_LIMIT)


def _full(shape):
    nd = len(shape)
    return pl.BlockSpec(shape, lambda *_: (0,) * nd, pipeline_mode=pl.Buffered(1))


def _sigmoid(g):
    return 1.0 / (1.0 + jnp.exp(-g))


SUB_TILE = 256


def _sub_tiles(tm):
    sub = min(SUB_TILE, tm)
    return [slice(s * sub, (s + 1) * sub) for s in range(tm // sub)]


MESH = pl.DeviceIdType.MESH
ROWS_PER_DEVICE = 8
VMEM_SPEC = pl.BlockSpec(memory_space=pltpu.VMEM)
ANY_SPEC = pl.BlockSpec(memory_space=pl.ANY)


def _position():
    x, y, c = lax.axis_index("x"), lax.axis_index("y"), lax.axis_index("c")
    sibling = (x, y, 1 - c)
    others = [(1 - x, y, c), (x, 1 - y, c), (1 - x, 1 - y, c)]
    return (x, y, c), 4 * x + 2 * y + c, 2 * x + y, sibling, others


def _rows_of(dev):
    return pl.ds(pl.multiple_of(dev * ROWS_PER_DEVICE, ROWS_PER_DEVICE), ROWS_PER_DEVICE)


def _all_to_all_rows(block_ref, table_ref, dev, me, send_sems, recv_sems):
    x, y, c = me
    waits = []
    for k in range(1, 8):
        peer = (1 - x if k & 4 else x, 1 - y if k & 2 else y, 1 - c if k & 1 else c)
        pltpu.make_async_remote_copy(src_ref=block_ref, dst_ref=table_ref.at[_rows_of(dev)], send_sem=send_sems.at[k - 1],
                                     recv_sem=recv_sems.at[k - 1], device_id=peer, device_id_type=MESH).start()
        waits.append(pltpu.make_async_remote_copy(
            src_ref=block_ref, dst_ref=table_ref.at[_rows_of(jnp.bitwise_xor(dev, k))], send_sem=send_sems.at[k - 1],
            recv_sem=recv_sems.at[k - 1], device_id=peer, device_id_type=MESH))
    return waits


def _comm_fwd_call(c_blk, w_ada, shards):
    n = len(shards)

    def body(c_ref, wada_ref, *refs):
        w_refs, act_ref, pieces_ref, full_refs = refs[:n], refs[n], refs[n + 1], refs[n + 2:2 * n + 2]
        c_all_ref = refs[2 * n + 2]
        c_send, c_recv, p_send, p_recv, w_send, w_recv, f_send, f_recv, loc_sem = refs[2 * n + 3:]
        me, dev, chip, sibling, others = _position()
        core = me[2]
        chip_of = [2 * p[0] + p[1] for p in others]

        local = [pltpu.make_async_copy(w_refs[i], full_refs[i].at[chip], loc_sem.at[i]) for i in range(n)]
        for cp in local:
            cp.start()

        def over_ici(i, j, src_chip):
            return pltpu.make_async_remote_copy(
                src_ref=w_refs[i].at[core], dst_ref=full_refs[i].at[src_chip, core], send_sem=w_send.at[3 * i + j],
                recv_sem=w_recv.at[3 * i + j], device_id=others[j], device_id_type=MESH)

        def to_sibling(i, j, half):
            return pltpu.make_async_remote_copy(
                src_ref=full_refs[i].at[chip_of[j], half], dst_ref=full_refs[i].at[chip_of[j], half],
                send_sem=f_send.at[3 * i + j], recv_sem=f_recv.at[3 * i + j], device_id=sibling, device_id_type=MESH)

        c_all_ref[_rows_of(dev), :] = c_ref[...]
        c_waits = _all_to_all_rows(c_ref, c_all_ref, dev, me, c_send, c_recv)
        sent = [over_ici(i, j, chip) for i in range(n) for j in range(3)]
        for cp in sent:
            cp.start()

        for cp in c_waits:
            cp.wait()
        cv = c_all_ref[...]
        act = cv * _sigmoid(cv)
        act_ref[...] = act
        pieces_ref[chip] = _dot(act.astype(BF16), wada_ref[...].astype(BF16))
        piece = lambda j, src_chip: pltpu.make_async_remote_copy(
            src_ref=pieces_ref.at[chip], dst_ref=pieces_ref.at[src_chip], send_sem=p_send.at[j], recv_sem=p_recv.at[j],
            device_id=others[j], device_id_type=MESH)
        for j in range(3):
            piece(j, chip).start()

        for i in range(n):
            for j in range(3):
                over_ici(i, j, chip_of[j]).wait_recv()
                to_sibling(i, j, core).start()
        for j in range(3):
            piece(j, chip).wait_send()
            piece(j, chip_of[j]).wait_recv()
        for i in range(n):
            for j in range(3):
                to_sibling(i, j, 1 - core).wait_recv()
                to_sibling(i, j, core).wait_send()
        for cp in sent:
            cp.wait_send()
        for cp in local:
            cp.wait()

    rows = 8 * ROWS_PER_DEVICE
    dma = pltpu.SemaphoreType.DMA
    return pl.pallas_call(
        body, name="comm_fwd",
        out_shape=[jax.ShapeDtypeStruct((rows, D_MODEL), F32), jax.ShapeDtypeStruct((4, rows, w_ada.shape[1]), F32)]
        + [jax.ShapeDtypeStruct((4,) + s.shape, s.dtype) for s in shards],
        in_specs=[VMEM_SPEC, VMEM_SPEC] + [ANY_SPEC] * n,
        out_specs=[VMEM_SPEC, VMEM_SPEC] + [ANY_SPEC] * n,
        scratch_shapes=[pltpu.VMEM((rows, D_MODEL), F32), dma((7,)), dma((7,)), dma((3,)), dma((3,)),
                        dma((3 * n,)), dma((3 * n,)), dma((3 * n,)), dma((3 * n,)), dma((n,))],
        compiler_params=pltpu.CompilerParams(vmem_limit_bytes=VMEM_LIMIT),
    )(c_blk, w_ada, *shards)


def _comm_bwd_call(grads, part):
    n = len(grads)

    def body(part_ref, *refs):
        g_refs, f_refs, parts_ref = refs[:n], refs[n:2 * n], refs[2 * n]
        scratch = refs[2 * n + 1:]
        a_refs, b_refs, p_refs, r_refs = (scratch[k * n:(k + 1) * n] for k in range(4))
        s_send, s_recv, d_send, d_recv, e_send, e_recv, h_send, h_recv, loc_sem = scratch[4 * n:]
        me, dev, chip, sibling, others = _position()
        core = me[2]
        chip_of = [2 * p[0] + p[1] for p in others]

        parts_ref[_rows_of(dev), :] = part_ref[...]
        s_waits = _all_to_all_rows(part_ref, parts_ref, dev, me, s_send, s_recv)

        mine = [pltpu.make_async_copy(g_refs[i].at[:, core], a_refs[i], loc_sem.at[i]) for i in range(n)]
        swap = [pltpu.make_async_remote_copy(src_ref=g_refs[i].at[:, 1 - core], dst_ref=b_refs[i], send_sem=d_send.at[i],
                                             recv_sem=d_recv.at[i], device_id=sibling, device_id_type=MESH) for i in range(n)]
        order = sorted(range(n), key=lambda i: g_refs[i].shape[2] * g_refs[i].shape[3])
        for i in order:
            mine[i].start()
            swap[i].start()
        cross = [pltpu.make_async_remote_copy(src_ref=p_refs[i].at[chip_of[j]], dst_ref=r_refs[i].at[j],
                                              send_sem=e_send.at[3 * i + j], recv_sem=e_recv.at[3 * i + j],
                                              device_id=others[j], device_id_type=MESH) for i in range(n) for j in range(3)]
        for i in order:
            mine[i].wait()
            swap[i].wait()
            for k in range(4):
                s = a_refs[i][k] + b_refs[i][k]
                a_refs[i][k] = s
                p_refs[i][k] = s.astype(BF16)
            for j in range(3):
                cross[3 * i + j].start()
        share = {}
        for i in order:
            for j in range(3):
                cross[3 * i + j].wait()
            f_refs[i][core] = (a_refs[i][chip] + r_refs[i][0].astype(F32) + r_refs[i][1].astype(F32)
                               + r_refs[i][2].astype(F32))
            share[i] = pltpu.make_async_remote_copy(src_ref=f_refs[i].at[core], dst_ref=f_refs[i].at[core],
                                                    send_sem=h_send.at[i], recv_sem=h_recv.at[i], device_id=sibling,
                                                    device_id_type=MESH)
            share[i].start()
        for i in range(n):
            share[i].wait_send()
            pltpu.make_async_remote_copy(src_ref=f_refs[i].at[core], dst_ref=f_refs[i].at[1 - core], send_sem=h_send.at[i],
                                         recv_sem=h_recv.at[i], device_id=sibling, device_id_type=MESH).wait_recv()
        for cp in s_waits:
            cp.wait()

    rows = 8 * ROWS_PER_DEVICE
    dma = pltpu.SemaphoreType.DMA
    quarter = [(4,) + g.shape[2:] for g in grads]
    return pl.pallas_call(
        body, name="comm_bwd",
        out_shape=[jax.ShapeDtypeStruct((2,) + g.shape[2:], F32) for g in grads]
        + [jax.ShapeDtypeStruct((rows, part.shape[1]), F32)],
        in_specs=[VMEM_SPEC] + [ANY_SPEC] * n,
        out_specs=[VMEM_SPEC] * (n + 1),
        scratch_shapes=[pltpu.VMEM(q, F32) for q in quarter] + [pltpu.VMEM(q, F32) for q in quarter]
        + [pltpu.VMEM(q, BF16) for q in quarter] + [pltpu.VMEM((3,) + q[1:], BF16) for q in quarter]
        + [dma((7,)), dma((7,)), dma((n,)), dma((n,)), dma((3 * n,)), dma((3 * n,)), dma((n,)), dma((n,)), dma((n,))],
        compiler_params=pltpu.CompilerParams(vmem_limit_bytes=VMEM_LIMIT),
    )(part, *grads)


def _by_owner(g, n):
    return jnp.transpose(g.reshape(g.shape[0], 4, n), (1, 0, 2)).reshape(4, 2, g.shape[0] // 2, n)


def _reduce_operands(g):
    quarter = (4,) + g.shape[2:]
    dma = pltpu.SemaphoreType.DMA
    scratch = [pltpu.VMEM(quarter, F32), pltpu.VMEM(quarter, F32), pltpu.VMEM(quarter, BF16),
               pltpu.VMEM((3,) + quarter[1:], BF16), dma((5,)), dma((5,)), dma((2,))]
    return jax.ShapeDtypeStruct((2,) + g.shape[2:], F32), scratch


def _grad_reduce(step, n_steps, g_ref, f_ref, a_ref, b_ref, p_ref, r_ref, send, recv, loc_sem):
    me, _, chip, sibling, others = _position()
    core = me[2]
    chip_of = [2 * p[0] + p[1] for p in others]
    remote = lambda src, dst, k, to: pltpu.make_async_remote_copy(
        src_ref=src, dst_ref=dst, send_sem=send.at[k], recv_sem=recv.at[k], device_id=to, device_id_type=MESH)
    mine = pltpu.make_async_copy(g_ref.at[:, core], a_ref, loc_sem.at[0])
    swap = remote(g_ref.at[:, 1 - core], b_ref, 0, sibling)
    cross = [remote(p_ref.at[chip_of[j]], r_ref.at[j], 1 + j, others[j]) for j in range(3)]
    total_ref = b_ref.at[0]
    keep = pltpu.make_async_copy(total_ref, f_ref.at[core], loc_sem.at[1])
    share = lambda half: remote(total_ref, f_ref.at[half], 4, sibling)
    at = [k * (n_steps - 1) // 3 for k in range(4)]

    @pl.when(step == at[0])
    def _():
        mine.start()
        swap.start()

    @pl.when(step == at[1])
    def _():
        mine.wait()
        swap.wait()
        for k in range(4):
            s = a_ref[k] + b_ref[k]
            a_ref[k] = s
            p_ref[k] = s.astype(BF16)
        for cp in cross:
            cp.start()

    @pl.when(step == at[2])
    def _():
        for cp in cross:
            cp.wait()
        total_ref[...] = a_ref[chip] + r_ref[0].astype(F32) + r_ref[1].astype(F32) + r_ref[2].astype(F32)
        keep.start()
        share(core).start()

    @pl.when(step == at[3])
    def _():
        keep.wait()
        share(core).wait_send()
        share(1 - core).wait_recv()


def _twice(t):
    lo = _lane_lo()
    other = pltpu.roll(t, HALF, 1)
    return jnp.concatenate([jnp.where(lo, t, other), jnp.where(lo, other, t)], axis=1)


def _once(g):
    first, second = g[:, :HEAD_LANES], g[:, HEAD_LANES:]
    return jnp.where(_lane_lo(), first + pltpu.roll(first, HALF, 1), second + pltpu.roll(second, HALF, 1))


def _rope_tables(pos_ref, inv_row, rope_ref):
    quarter = pos_ref.shape[0] // 4
    lane = lax.broadcasted_iota(jnp.int32, (1, HEAD_LANES), 1)
    pos = [pos_ref[g * quarter:(g + 1) * quarter, :] for g in range(4)]
    ang = jnp.where(lane < 32, pos[0], jnp.where(lane < 64, pos[1], jnp.where(lane < 96, pos[2], pos[3]))) * inv_row
    cos, sin = jnp.cos(ang), jnp.sin(ang)
    rope_lanes = jnp.logical_and(lane >= HALF, lane < HALF + MLA_ROPE)
    for g in range(4):
        rows = slice(g * quarter, (g + 1) * quarter)
        shift = (HALF - 32 * g) % HEAD_LANES
        at = lambda t: t if shift == 0 else pltpu.roll(t, shift, 1)
        rope_ref[rows, :HEAD_LANES] = jnp.where(rope_lanes, at(cos), 1.0)
        rope_ref[rows, HEAD_LANES:] = jnp.where(rope_lanes, at(sin), 0.0)


def _gather_in_steps(step, n_steps, w_refs, full_refs, w_send, w_recv, f_send, f_recv, loc_sem):
    me, _, chip, sibling, others = _position()
    core = me[2]
    chip_of = [2 * p[0] + p[1] for p in others]
    n = len(w_refs)
    local = [pltpu.make_async_copy(w_refs[i], full_refs[i].at[chip], loc_sem.at[i]) for i in range(n)]

    def over_ici(i, j, src_chip):
        return pltpu.make_async_remote_copy(
            src_ref=w_refs[i].at[core], dst_ref=full_refs[i].at[src_chip, core], send_sem=w_send.at[3 * i + j],
            recv_sem=w_recv.at[3 * i + j], device_id=others[j], device_id_type=MESH)

    def to_sibling(i, j, half):
        return pltpu.make_async_remote_copy(
            src_ref=full_refs[i].at[chip_of[j], half], dst_ref=full_refs[i].at[chip_of[j], half],
            send_sem=f_send.at[3 * i + j], recv_sem=f_recv.at[3 * i + j], device_id=sibling, device_id_type=MESH)

    pairs = [(i, j) for i in range(n) for j in range(3)]

    @pl.when(step == 0)
    def _():
        for cp in local:
            cp.start()
        for i, j in pairs:
            over_ici(i, j, chip).start()

    @pl.when(step == 3 * n_steps // 4)
    def _():
        for i, j in pairs:
            over_ici(i, j, chip_of[j]).wait_recv()
            to_sibling(i, j, core).start()

    @pl.when(step == n_steps - 1)
    def _():
        for i, j in pairs:
            to_sibling(i, j, 1 - core).wait_recv()
            to_sibling(i, j, core).wait_send()
            over_ici(i, j, chip).wait_send()
        for cp in local:
            cp.wait()


def _pre_call(x, pos_col, mod, b_ada, ng, inv128, wa, shards, seq):
    n_tok = x.shape[0]
    tm = min(TOKEN_TILE, seq)
    per_seq = seq // tm
    n_steps = n_tok // tm
    n = len(shards)

    def body(x_ref, pos_ref, mod_ref, bada_ref, ng_ref, inv_ref, wa_ref, *refs):
        w_refs, refs = refs[:n], refs[n:]
        zqkv_ref, zkr_ref, gates_ref, qs_ref, kd_ref, vd_ref, rope_ref = refs[:7]
        full_refs, sems = refs[7:7 + n], refs[7 + n:]
        _gather_in_steps(pl.program_id(0), n_steps, w_refs, full_refs, *sems)
        _rope_tables(pos_ref, inv_ref[...], rope_ref)
        xv = x_ref[...]
        modv = mod_ref[0] + bada_ref[...]
        shift, scale = modv[:, :D_MODEL], modv[:, D_MODEL:2 * D_MODEL]
        r1 = lax.rsqrt(jnp.mean(xv * xv, axis=-1, keepdims=True) + EPS)
        h = ((xv * r1) * ng_ref[...]) * (1.0 + scale) + shift
        za = _dot(h.astype(BF16), wa_ref[...])
        zqkv_ref[...] = za[:, :A_KR]
        zkr_ref[...] = za[:, A_KR:A_GM]
        gates_ref[:, :512] = za[:, A_GM:A_QS]
        gates_ref[:, 512:] = za[:, A_GS:A_END]
        qs_ref[...] = (za[:, A_QS:A_KS] * (SWA_SCALE * LOG2E)).astype(BF16)
        kd_ref[...] = _twice(za[:, A_KS:A_VS]).astype(BF16)
        vd_ref[...] = _twice(za[:, A_VS:A_GS]).astype(BF16)

    tok = lambda w: pl.BlockSpec((tm, w), lambda i: (i, 0))
    outs = [(640, F32), (HEAD_LANES, F32), (1024, F32), (512, BF16), (256, BF16), (256, BF16), (2 * HEAD_LANES, F32)]
    dma = pltpu.SemaphoreType.DMA
    return pl.pallas_call(
        body, name="pre", grid=(n_steps,),
        out_shape=[jax.ShapeDtypeStruct((n_tok, w), dt) for w, dt in outs]
        + [jax.ShapeDtypeStruct((4,) + s.shape, s.dtype) for s in shards],
        in_specs=[tok(D_MODEL), tok(1), pl.BlockSpec((1, 1, 3 * D_MODEL), lambda i: (i // per_seq, 0, 0)),
                  _full(b_ada.shape), _full(ng.shape), _full(inv128.shape), _full(wa.shape)] + [ANY_SPEC] * n,
        out_specs=[tok(w) for w, _ in outs] + [ANY_SPEC] * n,
        scratch_shapes=[dma((3 * n,)), dma((3 * n,)), dma((3 * n,)), dma((3 * n,)), dma((n,))],
        compiler_params=_params(1),
    )(x, pos_col, mod, b_ada, ng, inv128, wa, *shards)


def _up_call(zqkv, zkr, rope, qg, kvg, wq2, wkv, seq):
    n_tok = zqkv.shape[0]
    tm = min(TOKEN_TILE, seq)

    n_steps = n_tok // tm
    ring = 3

    def body(zqkv_hbm, zkr_hbm, rope_hbm, qg_ref, kvg_ref, wq_ref, wkv_ref, qf_ref, kf_ref, v_ref,
             zqkv_buf, zkr_buf, rope_buf, sems):
        i = pl.program_id(0)

        def fetch(step):
            static = isinstance(step, int)
            slot = step % ring if static else lax.rem(step, ring)
            rows = pl.ds(step * tm if static else pl.multiple_of(step * tm, tm), tm)
            pairs = [(zqkv_hbm, zqkv_buf), (zkr_hbm, zkr_buf), (rope_hbm, rope_buf)]
            return [pltpu.make_async_copy(src.at[rows], dst.at[slot], sems.at[k, slot]) for k, (src, dst) in enumerate(pairs)]

        @pl.when(i == 0)
        def _():
            for first in range(min(ring - 1, n_steps)):
                for cp in fetch(first):
                    cp.start()

        @pl.when(i + (ring - 1) < n_steps)
        def _():
            for cp in fetch(i + (ring - 1)):
                cp.start()

        for cp in fetch(i):
            cp.wait()
        slot = lax.rem(i, ring)
        zqkv_ref, zkr_ref, rope_ref = zqkv_buf.at[slot], zkr_buf.at[slot], rope_buf.at[slot]
        cos, sin = rope_ref[:, :HEAD_LANES], rope_ref[:, HEAD_LANES:]
        zq, zkv = zqkv_ref[:, A_ZQ:A_ZKV], zqkv_ref[:, A_ZKV:A_KR]
        rq = lax.rsqrt(jnp.mean(zq * zq, axis=-1, keepdims=True) + EPS)
        qn = ((zq * rq) * qg_ref[...]).astype(BF16)
        qr = _dot(qn, wq_ref[...])
        cf, sf = jnp.tile(cos, (1, N_HEADS)), jnp.tile(sin, (1, N_HEADS))
        qf_ref[...] = ((qr[:, :1024] * cf + qr[:, 1024:] * sf) * (MLA_SCALE * LOG2E)).astype(BF16)
        rkv = lax.rsqrt(jnp.mean(zkv * zkv, axis=-1, keepdims=True) + EPS)
        kvn = ((zkv * rkv) * kvg_ref[...]).astype(BF16)
        kv = _dot(kvn, wkv_ref[...])
        zkr = zkr_ref[...]
        kpe = jnp.where(_lane_lo(), 0.0, zkr * cos) + pltpu.roll(zkr, HALF, 1) * sin
        kf_ref[...] = (kv[:, :1024] + jnp.tile(kpe, (1, N_HEADS))).astype(BF16)
        v_ref[...] = kv[:, 1024:].astype(BF16)

    tok = lambda w: pl.BlockSpec((tm, w), lambda i: (i, 0))
    outs = [(1024, BF16), (1024, BF16), (512, BF16)]
    return pl.pallas_call(
        body, name="up", grid=(n_steps,),
        out_shape=[jax.ShapeDtypeStruct((n_tok, w), dt) for w, dt in outs],
        in_specs=[ANY_SPEC, ANY_SPEC, ANY_SPEC, _full(qg.shape), _full(kvg.shape), _full(wq2.shape), _full(wkv.shape)],
        out_specs=[tok(w) for w, _ in outs],
        scratch_shapes=[pltpu.VMEM((ring, tm, 640), F32), pltpu.VMEM((ring, tm, HEAD_LANES), F32),
                        pltpu.VMEM((ring, tm, 2 * HEAD_LANES), F32), pltpu.SemaphoreType.DMA((3, ring))],
        compiler_params=_params(1),
    )(zqkv, zkr, rope, qg, kvg, wq2, wkv)


def _lane_lo(width=HEAD_LANES):
    return lax.broadcasted_iota(jnp.int32, (1, width), 1) < HALF


def _eye(n=HEAD_LANES):
    r = lax.broadcasted_iota(jnp.int32, (n, n), 0)
    c = lax.broadcasted_iota(jnp.int32, (n, n), 1)
    return jnp.where(r == c, 1.0, 0.0).astype(BF16)


def _mla_fwd_call(qf, kf, v, n_seq, seq):
    tq = min(ATT_TILE, seq)
    nq = seq // tq

    ext = HALF + 16

    def body(q_ref, k_ref, v_ref, o_ref, lse_ref, vt_ref, acc_ref):
        i = pl.program_id(1)
        eye = _eye()

        @pl.when(i == 0)
        def _():
            for h in range(N_HEADS):
                vt_ref[h * ext + HALF:(h + 1) * ext, :] = jnp.ones((16, seq), BF16)
            for t in range(nq):
                for p in range(N_HEADS // 2):
                    pair = slice(p * HEAD_LANES, (p + 1) * HEAD_LANES)
                    v_t = _dot_nt(eye, v_ref[t * tq:(t + 1) * tq, pair]).astype(BF16)
                    for hh in range(2):
                        r0 = (2 * p + hh) * ext
                        vt_ref[r0:r0 + HALF, t * tq:(t + 1) * tq] = v_t[hh * HALF:(hh + 1) * HALF, :]

        q = q_ref[...]
        qcol = i * tq + lax.broadcasted_iota(jnp.int32, (1, tq), 1)
        heads = range(N_HEADS)
        lanes = [slice(h * HEAD_LANES, (h + 1) * HEAD_LANES) for h in heads]

        def make_step(masked, n_tiles):
            def step(kt0, carry):
                tiles = range(n_tiles)
                start = pl.multiple_of(kt0 * tq, tq)
                ks = [k_ref[pl.ds(pl.multiple_of((kt0 + t) * tq, tq), tq), :] for t in tiles]
                vt = vt_ref[:, pl.ds(start, n_tiles * tq)]
                last = n_tiles - 1
                if masked:
                    keep = ((kt0 + last) * tq + lax.broadcasted_iota(jnp.int32, (tq, 1), 0)) <= qcol

                def scores(h):
                    sts = [_dot_nt(ks[t][:, lanes[h]], q[:, lanes[h]]) for t in tiles]
                    if masked:
                        sts[last] = jnp.where(keep, sts[last], NEG)
                    return sts

                def softmax(h, sts):
                    m_old = carry[h]
                    m_new = m_old
                    for st in sts:
                        m_new = jnp.maximum(m_new, jnp.max(st, axis=0, keepdims=True))
                    pt = jnp.concatenate([jnp.exp2(st - m_new).astype(BF16) for st in sts], axis=0)
                    return m_new, jnp.exp2(m_old - m_new), pt

                def values(h, alpha, pt):
                    rows = slice(h * ext, (h + 1) * ext)
                    acc_ref[rows, :] = acc_ref[rows, :] * alpha + _dot(vt[rows, :], pt)

                sts, soft, out = {0: scores(0), 1: scores(1)}, {}, {}
                for h in range(N_HEADS + 1):
                    if h + 2 < N_HEADS:
                        sts[h + 2] = scores(h + 2)
                    if h < N_HEADS:
                        soft[h] = softmax(h, sts.pop(h))
                    if h >= 1:
                        m_new, alpha, pt = soft.pop(h - 1)
                        values(h - 1, alpha, pt)
                        out[h - 1] = m_new
                return tuple(out[h] for h in heads)
            return step

        acc_ref[...] = jnp.zeros_like(acc_ref)
        init = (jnp.full((1, tq), NEG, F32),) * N_HEADS
        count = i + 1
        carry = lax.fori_loop(0, (count + 1) // 2 - 1, lambda j, c: make_step(False, 2)(2 * j, c), init)
        carry = lax.cond(count % 2 == 0, lambda c: make_step(True, 2)(i - 1, c), lambda c: make_step(True, 1)(i, c), carry)
        dens = [acc_ref[h * ext + HALF:h * ext + HALF + 1, :] for h in heads]
        acc_t = jnp.concatenate([acc_ref[h * ext:h * ext + HALF, :] * (1.0 / dens[h]) for h in heads], axis=0)
        o_ref[...] = acc_t.T
        for h in heads:
            lse_ref[0, h // 4, h % 4:h % 4 + 1, :] = carry[h] + jnp.log2(dens[h])

    n_tok = qf.shape[0]
    return pl.pallas_call(
        body, name="mla_fwd", grid=(n_seq, nq),
        out_shape=[jax.ShapeDtypeStruct((n_tok, 512), F32), jax.ShapeDtypeStruct((n_seq, 2, 4, seq), F32)],
        in_specs=[pl.BlockSpec((tq, 1024), lambda b, i: (b * nq + i, 0)),
                  pl.BlockSpec((seq, 1024), lambda b, i: (b, 0)),
                  pl.BlockSpec((seq, 512), lambda b, i: (b, 0))],
        out_specs=[pl.BlockSpec((tq, 512), lambda b, i: (b * nq + i, 0)),
                   pl.BlockSpec((1, 2, 4, tq), lambda b, i: (b, 0, 0, i))],
        scratch_shapes=[pltpu.VMEM((N_HEADS * ext, seq), BF16), pltpu.VMEM((N_HEADS * ext, tq), F32)],
        compiler_params=_params(2),
    )(qf, kf, v)


def _mla_bwd_call(qf, kf, v, do, delta, lse, n_seq, seq):
    tq = min(ATT_TILE, seq)
    nq = seq // tq

    nh = 4
    heads = range(nh)
    lanes = [slice(h * HEAD_LANES, (h + 1) * HEAD_LANES) for h in heads]

    def body(q_ref, k_ref, v_ref, do_ref, dl_ref, lse_ref, dq_ref, dk_ref, dv_ref,
             kt_ref, dot_ref, dqt_ref, dvt_ref):
        eye = _eye()
        sub_lo = lax.broadcasted_iota(jnp.int32, (HEAD_LANES, 1), 0) < HALF

        for t in range(nq):
            r = slice(t * tq, (t + 1) * tq)
            kv = k_ref[r, :]
            for h in heads:
                kt_ref[lanes[h], r] = _dot_nt(eye, kv[:, lanes[h]]).astype(BF16)
            for p in range(nh // 2):
                dov = do_ref[r, lanes[p]]
                dt = _dot_nt(eye, dov)
                dot_ref[2 * p, :, r] = jnp.where(sub_lo, dt, 0.0).astype(BF16)
                dot_ref[2 * p + 1, :, r] = jnp.where(sub_lo, 0.0, dt).astype(BF16)
        dqt_ref[...] = jnp.zeros_like(dqt_ref)
        dvt_ref[...] = jnp.zeros_like(dvt_ref)

        def flush_dv(tile, which):
            rows = pl.ds(pl.multiple_of(tile * tq, tq), tq)
            for p in range(nh // 2):
                dv_ref[rows, lanes[p]] = dvt_ref[which, p * HEAD_LANES:(p + 1) * HEAD_LANES, :].T

        def k_step(kt, _):
            slot = kt % 2
            kr = pl.ds(pl.multiple_of(kt * tq, tq), tq)
            k = k_ref[kr, :]
            vv = v_ref[kr, :]
            k_t = kt_ref[:, kr]
            krow = kt * tq + lax.broadcasted_iota(jnp.int32, (tq, 1), 0)

            def make_step(masked, n_tiles):
                def q_step(qt0, carry):
                    tiles = range(n_tiles)
                    qrs = [pl.ds(pl.multiple_of((qt0 + t) * tq, tq), tq) for t in tiles]
                    if masked:
                        flush_dv(jnp.maximum(kt - 1, 0), 1 - slot)
                    qs = [q_ref[qr, :] for qr in qrs]
                    if masked:
                        keep = krow <= (qt0 * tq + lax.broadcasted_iota(jnp.int32, (1, tq), 1))

                    def scores(h):
                        do_ts = [dot_ref[h, :, qr] for qr in qrs]
                        sts = [_dot_nt(k[:, lanes[h]], qs[t][:, lanes[h]]) for t in tiles]
                        dpts = [_dot(vv[:, lanes[h // 2]], do_ts[t]) for t in tiles]
                        return do_ts, sts, dpts

                    def softmax(h, sts, dpts):
                        pts, dsts = [], []
                        for t in tiles:
                            pt = jnp.exp2(sts[t] - lse_ref[0, 0, h:h + 1, qrs[t]])
                            if masked and t == 0:
                                pt = jnp.where(keep, pt, 0.0)
                            dsts.append((pt * (dpts[t] - dl_ref[0, h:h + 1, qrs[t]])).astype(BF16))
                            pts.append(pt.astype(BF16))
                        return pts, dsts

                    def grads(h, do_ts, pts, dsts):
                        half = slice((h % 2) * HALF, (h % 2 + 1) * HALF)
                        dst_all = jnp.concatenate(dsts, axis=1)
                        pt_all = jnp.concatenate(pts, axis=1)
                        do_all = jnp.concatenate([do_ts[t][half, :] for t in tiles], axis=1)
                        q_all = jnp.concatenate([qs[t][:, lanes[h]] for t in tiles], axis=0)
                        dvt_ref[slot, h * HALF:(h + 1) * HALF, :] += _dot_nt(do_all, pt_all)
                        dk_ref[kr, lanes[h]] += _dot(dst_all, q_all)
                        for t in tiles:
                            dqt_ref[lanes[h], qrs[t]] += _dot(k_t[lanes[h], :], dsts[t])

                    first, second = {0: scores(0)}, {}
                    for h in range(nh + 1):
                        if h + 1 < nh:
                            first[h + 1] = scores(h + 1)
                        if h < nh:
                            do_ts, sts, dpts = first.pop(h)
                            second[h] = (do_ts,) + softmax(h, sts, dpts)
                        if h >= 1:
                            grads(h - 1, *second.pop(h - 1))
                    return carry
                return q_step

            dk_ref[kr, :] = jnp.zeros((tq, nh * HEAD_LANES), F32)
            dvt_ref[slot] = jnp.zeros(dvt_ref.shape[1:], F32)
            count = nq - kt
            lax.cond(count >= 2, lambda c: make_step(True, 2)(kt, c), lambda c: make_step(True, 1)(kt, c), 0)
            lax.fori_loop(1, count // 2, lambda j, c: make_step(False, 2)(kt + 2 * j, c), 0)
            lax.cond(jnp.logical_and(count % 2 == 1, count >= 3), lambda c: make_step(False, 1)(nq - 1, c), lambda c: c, 0)
            return 0

        lax.fori_loop(0, nq, k_step, 0)
        flush_dv(nq - 1, (nq - 1) % 2)
        for t in range(nq):
            r = slice(t * tq, (t + 1) * tq)
            for h in heads:
                dq_ref[r, lanes[h]] = dqt_ref[lanes[h], r].T

    n_tok = qf.shape[0]
    groups = N_HEADS // nh
    blk = lambda w: pl.BlockSpec((seq, w), lambda b, g: (b, g))
    return pl.pallas_call(
        body, name="mla_bwd", grid=(n_seq, groups),
        out_shape=[jax.ShapeDtypeStruct((n_tok, 1024), F32), jax.ShapeDtypeStruct((n_tok, 1024), F32),
                   jax.ShapeDtypeStruct((n_tok, 512), F32)],
        in_specs=[blk(512), blk(512), blk(256), blk(256), pl.BlockSpec((1, nh, seq), lambda b, g: (g, 0, b)),
                  pl.BlockSpec((1, 1, nh, seq), lambda b, g: (b, g, 0, 0))],
        out_specs=[blk(512), blk(512), blk(256)],
        scratch_shapes=[pltpu.VMEM((nh * HEAD_LANES, seq), BF16), pltpu.VMEM((nh, HEAD_LANES, seq), BF16),
                        pltpu.VMEM((nh * HEAD_LANES, seq), F32), pltpu.VMEM((2, nh * HALF, tq), F32)],
        compiler_params=_params(2),
    )(qf, kf, v, do, delta, lse)


SWA_BLOCKS = 4


def _swa_block(n, pos_col_ref, posq):
    w = SWA_WINDOW
    start = pl.multiple_of(jnp.maximum(n - 1, 0) * w, w)
    posk = pos_col_ref[pl.ds(start, 2 * w), :]
    rel = (n * w + lax.broadcasted_iota(jnp.int32, (1, w), 1)) - (start + lax.broadcasted_iota(jnp.int32, (2 * w, 1), 0))
    valid = jnp.logical_and(rel >= 0, rel < w)
    return start, jnp.where(valid, posq - posk, 1e30)


def _alibi(h):
    return LOG2E * 2.0 ** -(h + 1)


def _transpose_rows(eye, src_ref, dst_ref, seq, width):
    step = 2 * SWA_WINDOW
    for t in range(seq // step):
        for p in range(width // HEAD_LANES):
            lanes = slice(p * HEAD_LANES, (p + 1) * HEAD_LANES)
            dst_ref[lanes, t * step:(t + 1) * step] = _dot_nt(eye, src_ref[t * step:(t + 1) * step, lanes]).astype(BF16)


def _swa_fwd_call(qs, kd, vd, pos_col, pos_row, sinks, n_seq, seq):
    w = SWA_WINDOW
    qb = SWA_BLOCKS
    steps = seq // (qb * w)
    ext = HALF + 16

    def body(q_ref, k_ref, v_ref, pc_ref, pr_ref, sink_ref, o_ref, lse_ref, vt_ref):
        n = pl.program_id(1)
        lo = _lane_lo()
        hi = jnp.logical_not(lo)
        eye = _eye()

        @pl.when(n == 0)
        def _():
            step = 2 * w
            for kv in range(2):
                vt_ref[kv * ext + HALF:(kv + 1) * ext, :] = jnp.ones((16, seq), BF16)
                for t in range(seq // step):
                    v_t = _dot_nt(eye, v_ref[t * step:(t + 1) * step, kv * HEAD_LANES:(kv + 1) * HEAD_LANES])
                    vt_ref[kv * ext:kv * ext + HALF, t * step:(t + 1) * step] = v_t[:HALF, :].astype(BF16)

        heads = range(N_HEADS)
        blocks = range(qb)
        geo = [_swa_block(n * qb + bi, pc_ref, pr_ref[bi]) for bi in blocks]
        wins = [pl.ds(g[0], 2 * w) for g in geo]
        kwins = [k_ref[win, :] for win in wins]
        vts = [vt_ref[:, win] for win in wins]
        sts = []
        for bi in blocks:
            q = q_ref[bi * w:(bi + 1) * w, :]
            sts.append([])
            for j in range(N_HEADS // 2):
                qp = q[:, j * HEAD_LANES:(j + 1) * HEAD_LANES]
                both = jnp.concatenate([jnp.where(lo, qp, jnp.zeros_like(qp)), jnp.where(hi, qp, jnp.zeros_like(qp))], axis=0)
                st = _dot_nt(kwins[bi][:, (j // 2) * HEAD_LANES:(j // 2 + 1) * HEAD_LANES], both)
                sts[bi] += [st[:, :w], st[:, w:]]
        ps, ms = [], []
        for bi in blocks:
            ps.append([])
            ms.append([])
            for h in heads:
                s = sts[bi][h] - _alibi(h) * geo[bi][1]
                m = jnp.maximum(jnp.max(s, axis=0, keepdims=True), sink_ref[0, h] * LOG2E)
                ps[bi].append(jnp.exp2(s - m).astype(BF16))
                ms[bi].append(m)
        for bi in blocks:
            ots = []
            for h in heads:
                pv = _dot(vts[bi][(h // 4) * ext:(h // 4 + 1) * ext, :], ps[bi][h])
                l = pv[HALF:HALF + 1, :] + jnp.exp2(sink_ref[0, h] * LOG2E - ms[bi][h])
                ots.append(pv[:HALF, :] * (1.0 / l))
                lse_ref[0, h:h + 1, bi * w:(bi + 1) * w] = ms[bi][h] + jnp.log2(l)
            o_ref[bi * w:(bi + 1) * w, :] = jnp.concatenate(ots, axis=0).T

    n_tok = qs.shape[0]
    tok = lambda width: pl.BlockSpec((qb * w, width), lambda b, n: (b * steps + n, 0))
    whole = lambda width: pl.BlockSpec((seq, width), lambda b, n: (b, 0))
    return pl.pallas_call(
        body, name="swa_fwd", grid=(n_seq, steps),
        out_shape=[jax.ShapeDtypeStruct((n_tok, 512), F32), jax.ShapeDtypeStruct((n_seq, N_HEADS, seq), F32)],
        in_specs=[tok(512), whole(256), whole(256), whole(1), pl.BlockSpec((qb, 1, w), lambda b, n: (b * steps + n, 0, 0)),
                  pl.BlockSpec(memory_space=pltpu.SMEM)],
        out_specs=[tok(512), pl.BlockSpec((1, N_HEADS, qb * w), lambda b, n: (b, 0, n))],
        scratch_shapes=[pltpu.VMEM((2 * ext, seq), BF16)],
        compiler_params=_params(2),
    )(qs, kd, vd, pos_col, pos_row, sinks)


def _swa_bwd_call(qs, kd, vd, do, delta, lse, pos_col, pos_row, sinks, g_out, n_seq, seq):
    w = SWA_WINDOW
    qb = SWA_BLOCKS
    steps = seq // (qb * w)
    reduced, reduce_scratch = _reduce_operands(g_out)

    def body(q_ref, k_ref, v_ref, do_ref, dl_ref, lse_ref, pc_ref, pr_ref, sink_ref, g_ref, dq_ref, dk_ref, dv_ref,
             dsink_ref, f_ref, kt_ref, *reduce_refs):
        b, n = pl.program_id(0), pl.program_id(1)
        _grad_reduce(b * steps + n, n_seq * steps, g_ref, f_ref, *reduce_refs)
        lo = _lane_lo()
        hi = jnp.logical_not(lo)
        sub_lo = lax.broadcasted_iota(jnp.int32, (HEAD_LANES, 1), 0) < HALF
        eye = _eye()

        @pl.when(n == 0)
        def _():
            dk_ref[...] = jnp.zeros_like(dk_ref)
            dv_ref[...] = jnp.zeros_like(dv_ref)
            _transpose_rows(eye, k_ref, kt_ref, seq, 2 * HEAD_LANES)

        @pl.when(jnp.logical_and(n == 0, b == 0))
        def _():
            dsink_ref[...] = jnp.zeros_like(dsink_ref)

        heads = range(N_HEADS)
        blocks = range(qb)
        kv_lanes = lambda h: slice((h // 4) * HEAD_LANES, (h // 4 + 1) * HEAD_LANES)
        geo = [_swa_block(n * qb + bi, pc_ref, pr_ref[bi]) for bi in blocks]
        wins = [pl.ds(g[0], 2 * w) for g in geo]
        kwins = [k_ref[win, :] for win in wins]
        vwins = [v_ref[win, :] for win in wins]

        do_ts, deltas, qms, doms = [], [], [], []
        for bi in blocks:
            rows = slice(bi * w, (bi + 1) * w)
            for lst in (do_ts, deltas, qms, doms):
                lst.append([])
            for j in range(N_HEADS // 2):
                pair = slice(j * HEAD_LANES, (j + 1) * HEAD_LANES)
                dop = do_ref[rows, pair]
                qp = q_ref[rows, pair]
                dt = _dot_nt(eye, dop)
                for hh in range(2):
                    half = lo if hh == 0 else hi
                    do_ts[bi].append(jnp.where(sub_lo, dt, 0.0).astype(BF16) if hh == 0
                                     else jnp.where(sub_lo, 0.0, dt).astype(BF16))
                    deltas[bi].append(dl_ref[2 * j + hh:2 * j + hh + 1, rows])
                    qms[bi].append(jnp.where(half, qp, jnp.zeros_like(qp)))
                    doms[bi].append(jnp.where(half, dop, jnp.zeros_like(dop)))
        sts, dpts = [], []
        for bi in blocks:
            sts.append([])
            dpts.append([])
            for j in range(N_HEADS // 2):
                a, b = 2 * j, 2 * j + 1
                st = _dot_nt(kwins[bi][:, kv_lanes(a)], jnp.concatenate([qms[bi][a], qms[bi][b]], axis=0))
                dpt = _dot(vwins[bi][:, kv_lanes(a)], jnp.concatenate([do_ts[bi][a], do_ts[bi][b]], axis=1))
                sts[bi] += [st[:, :w], st[:, w:]]
                dpts[bi] += [dpt[:, :w], dpt[:, w:]]
        pts, dsts = [], []
        for bi in blocks:
            pts.append([])
            dsts.append([])
            for h in heads:
                lse_h = lse_ref[0, h:h + 1, bi * w:(bi + 1) * w]
                pt = jnp.exp2(sts[bi][h] - _alibi(h) * geo[bi][1] - lse_h)
                dsts[bi].append((pt * (dpts[bi][h] - deltas[bi][h])).astype(BF16))
                pts[bi].append(pt.astype(BF16))
                dsink_ref[h:h + 1, :] += -jnp.exp2(sink_ref[0, h] * LOG2E - lse_h) * deltas[bi][h]
        for bi in blocks:
            for kv in range(2):
                group = range(4 * kv, 4 * kv + 4)
                dst_all = jnp.concatenate([dsts[bi][h] for h in group], axis=1)
                pt_all = jnp.concatenate([pts[bi][h] for h in group], axis=1)
                q_all = jnp.concatenate([qms[bi][h] for h in group], axis=0)
                do_all = jnp.concatenate([doms[bi][h] for h in group], axis=0)
                dk_ref[wins[bi], kv_lanes(4 * kv)] += _dot(dst_all, q_all)
                dv_ref[wins[bi], kv_lanes(4 * kv)] += _dot(pt_all, do_all)
        for bi in blocks:
            ktw = kt_ref[:, wins[bi]]
            for j in range(N_HEADS // 2):
                k_t = ktw[kv_lanes(2 * j), :]
                both = _dot(k_t, jnp.concatenate([dsts[bi][2 * j], dsts[bi][2 * j + 1]], axis=1))
                dq_t = jnp.where(sub_lo, both[:, :w], both[:, w:])
                dq_ref[bi * w:(bi + 1) * w, j * HEAD_LANES:(j + 1) * HEAD_LANES] = dq_t.T * SWA_SCALE

    n_tok = qs.shape[0]
    tok = lambda width: pl.BlockSpec((qb * w, width), lambda b, n: (b * steps + n, 0))
    whole = lambda width: pl.BlockSpec((seq, width), lambda b, n: (b, 0))
    return pl.pallas_call(
        body, name="swa_bwd", grid=(n_seq, steps),
        out_shape=[jax.ShapeDtypeStruct((n_tok, 512), F32), jax.ShapeDtypeStruct((n_tok, 256), F32),
                   jax.ShapeDtypeStruct((n_tok, 256), F32), jax.ShapeDtypeStruct((N_HEADS, HEAD_LANES), F32), reduced],
        in_specs=[tok(512), whole(256), whole(256), pl.BlockSpec((qb * w, 512), lambda b, n: (b * steps + n, 1)),
                  pl.BlockSpec((N_HEADS, qb * w), lambda b, n: (0, b * steps + n)),
                  pl.BlockSpec((1, N_HEADS, qb * w), lambda b, n: (b, 0, n)),
                  whole(1), pl.BlockSpec((qb, 1, w), lambda b, n: (b * steps + n, 0, 0)),
                  pl.BlockSpec(memory_space=pltpu.SMEM), ANY_SPEC],
        out_specs=[tok(512), whole(256), whole(256), _full((N_HEADS, HEAD_LANES)), ANY_SPEC],
        scratch_shapes=[pltpu.VMEM((2 * HEAD_LANES, seq), BF16)] + reduce_scratch,
        compiler_params=_params(2),
    )(qs, kd, vd, do, delta, lse, pos_col, pos_row, sinks, g_out)


def _post_call(x, target, o_mla, o_swa, gates, mod, b_ada, fg, w_out, seq):
    n_tok = x.shape[0]
    tm = min(TOKEN_TILE, seq)
    per_seq = seq // tm
    n_seq = n_tok // seq

    def body(x_ref, t_ref, om_ref, os_ref, g_ref, mod_ref, bada_ref, fg_ref, w_ref,
             dx2_ref, do_ref, dg_ref, gw_ref, gfg_ref, dgate_ref, loss_ref, dmla_ref, dswa_ref):
        i = pl.program_id(0)

        @pl.when(i == 0)
        def _():
            gw_ref[...] = jnp.zeros_like(gw_ref)
            gfg_ref[...] = jnp.zeros_like(gfg_ref)
            loss_ref[...] = jnp.zeros_like(loss_ref)

        @pl.when(i % per_seq == 0)
        def _():
            dgate_ref[...] = jnp.zeros_like(dgate_ref)

        gate = mod_ref[0][:, 2 * D_MODEL:] + bada_ref[:, 2 * D_MODEL:]
        fgv = fg_ref[...]
        fgd = fgv * (1.0 / D_MODEL)
        subs = _sub_tiles(tm)
        gs = [g_ref[r, :] for r in subs]
        os_ = [jnp.concatenate([om_ref[r, :], os_ref[r, :]], axis=-1) for r in subs]
        sgs = [_sigmoid(g) for g in gs]
        sils = [g * sg for g, sg in zip(gs, sgs)]
        ypres = [(o * sil).astype(BF16) for o, sil in zip(os_, sils)]
        ys = [_dot(ypre, w_ref[...]) for ypre in ypres]
        dys, loss, gfg, dgate = [], 0.0, 0.0, 0.0
        for r, y in zip(subs, ys):
            x2 = x_ref[r, :] + gate * y
            r2 = lax.rsqrt(jnp.mean(x2 * x2, axis=-1, keepdims=True) + EPS)
            xn2 = x2 * r2
            err = xn2 * fgv - t_ref[r, :]
            loss = loss + jnp.sum(jnp.sum(err * err, axis=-1, keepdims=True), axis=0, keepdims=True)
            gfg = gfg + jnp.sum(err * xn2, axis=0, keepdims=True)
            dxn2 = err * fgd
            dx2 = r2 * (dxn2 - xn2 * jnp.mean(dxn2 * xn2, axis=-1, keepdims=True))
            dx2_ref[r, :] = dx2
            dgate = dgate + jnp.sum(dx2 * y, axis=0, keepdims=True)
            dys.append((dx2 * gate).astype(BF16))
        loss_ref[...] += jnp.broadcast_to(loss * (0.5 / D_MODEL), loss_ref.shape)
        gfg_ref[...] += gfg * (1.0 / D_MODEL)
        dgate_ref[0] += dgate
        gw_ref[...] += _dot_tn(jnp.concatenate(ypres, axis=0), jnp.concatenate(dys, axis=0))
        dypres = [_dot_nt(dy, w_ref[...]) for dy in dys]
        pick = jnp.where(jnp.right_shift(lax.broadcasted_iota(jnp.int32, (2 * N_HEADS, D_MODEL), 1), 6)
                         == lax.broadcasted_iota(jnp.int32, (2 * N_HEADS, D_MODEL), 0), 1.0, 0.0).astype(BF16)
        for r, dypre, o, g, sg, sil in zip(subs, dypres, os_, gs, sgs, sils):
            dov = (dypre * sil).astype(BF16)
            do_ref[r, :] = dov
            delta = _dot_nt(pick, (dov.astype(F32) * o).astype(BF16))
            for grp in range(2):
                dmla_ref[grp, :, r] = delta[4 * grp:4 * grp + 4, :]
            dswa_ref[:, r] = delta[N_HEADS:, :]
            dg_ref[r, :] = (dypre * o * (sg + sil * (1.0 - sg))).astype(BF16)

    tok = lambda w: pl.BlockSpec((tm, w), lambda i: (i, 0))
    per_b = pl.BlockSpec((1, 1, 3 * D_MODEL), lambda i: (i // per_seq, 0, 0))
    return pl.pallas_call(
        body, name="post", grid=(n_tok // tm,),
        out_shape=[jax.ShapeDtypeStruct((n_tok, D_MODEL), F32), jax.ShapeDtypeStruct((n_tok, D_MODEL), BF16),
                   jax.ShapeDtypeStruct((n_tok, D_MODEL), BF16), jax.ShapeDtypeStruct((D_MODEL, D_MODEL), F32),
                   jax.ShapeDtypeStruct((1, D_MODEL), F32), jax.ShapeDtypeStruct((n_seq, 1, D_MODEL), F32),
                   jax.ShapeDtypeStruct((1, HEAD_LANES), F32),
                   jax.ShapeDtypeStruct((2, N_HEADS // 2, n_tok), F32), jax.ShapeDtypeStruct((N_HEADS, n_tok), F32)],
        in_specs=[tok(D_MODEL), tok(D_MODEL), tok(512), tok(512), tok(D_MODEL), per_b, _full(b_ada.shape),
                  _full(fg.shape), _full(w_out.shape)],
        out_specs=[tok(D_MODEL), tok(D_MODEL), tok(D_MODEL), _full((D_MODEL, D_MODEL)), _full((1, D_MODEL)),
                   pl.BlockSpec((1, 1, D_MODEL), lambda i: (i // per_seq, 0, 0)), _full((1, HEAD_LANES)),
                   pl.BlockSpec((2, N_HEADS // 2, tm), lambda i: (0, 0, i)), pl.BlockSpec((N_HEADS, tm), lambda i: (0, i))],
        compiler_params=_params(1),
    )(x, target, o_mla, o_swa, gates, mod, b_ada, fg, w_out)


def _mid_bwd_call(dqf, dkf, dv, zqkv, rope, qg, kvg, wq2, wkv, seq):
    n_tok = dqf.shape[0]
    tm = min(TOKEN_TILE, seq)

    def body(dq_ref, dk_ref, dv_ref, z_ref, rope_ref, qg_ref, kvg_ref, wq_ref, wkv_ref,
             dz_ref, gwq_ref, gwkv_ref, gqg_ref, gkvg_ref):
        i = pl.program_id(0)

        @pl.when(i == 0)
        def _():
            gwq_ref[...] = jnp.zeros_like(gwq_ref)
            gwkv_ref[...] = jnp.zeros_like(gwkv_ref)
            gqg_ref[...] = jnp.zeros_like(gqg_ref)
            gkvg_ref[...] = jnp.zeros_like(gkvg_ref)

        cos, sin = rope_ref[:, :HEAD_LANES], rope_ref[:, HEAD_LANES:]
        cf, sf = jnp.tile(cos, (1, N_HEADS)), jnp.tile(sin, (1, N_HEADS))
        dq = dq_ref[...] * MLA_SCALE
        dqr = jnp.concatenate([dq * cf, dq * sf], axis=-1).astype(BF16)
        zq, zkv = z_ref[:, :Q_LORA], z_ref[:, Q_LORA:]
        qgv, kvgv = qg_ref[...], kvg_ref[...]

        rq = lax.rsqrt(jnp.mean(zq * zq, axis=-1, keepdims=True) + EPS)
        xq = zq * rq
        gwq_ref[...] += _dot_tn((xq * qgv).astype(BF16), dqr)
        dqn = _dot_nt(dqr, wq_ref[...])
        gqg_ref[...] += jnp.sum(dqn * xq, axis=0, keepdims=True)
        dxq = dqn * qgv
        dz_ref[:, :Q_LORA] = (rq * (dxq - xq * jnp.mean(dxq * xq, axis=-1, keepdims=True))).astype(BF16)

        dk = dk_ref[...] * LN2
        dkv = jnp.concatenate([dk, dv_ref[...]], axis=-1).astype(BF16)
        rkv = lax.rsqrt(jnp.mean(zkv * zkv, axis=-1, keepdims=True) + EPS)
        xkv = zkv * rkv
        gwkv_ref[...] += _dot_tn((xkv * kvgv).astype(BF16), dkv)
        dkvn = _dot_nt(dkv, wkv_ref[...])
        gkvg_ref[...] += jnp.sum(dkvn * xkv, axis=0, keepdims=True)
        dxkv = dkvn * kvgv
        dz_ref[:, Q_LORA:A_KR] = (rkv * (dxkv - xkv * jnp.mean(dxkv * xkv, axis=-1, keepdims=True))).astype(BF16)

        dkpe = dk[:, :HEAD_LANES]
        for h in range(1, N_HEADS):
            dkpe = dkpe + dk[:, h * HEAD_LANES:(h + 1) * HEAD_LANES]
        dz_ref[:, A_KR:] = (jnp.where(_lane_lo(), 0.0, dkpe * cos) + pltpu.roll(dkpe * sin, HALF, 1)).astype(BF16)

    tok = lambda w: pl.BlockSpec((tm, w), lambda i: (i, 0))
    return pl.pallas_call(
        body, name="mid_bwd", grid=(n_tok // tm,),
        out_shape=[jax.ShapeDtypeStruct((n_tok, A_GM), BF16),
                   jax.ShapeDtypeStruct(wq2.shape, F32), jax.ShapeDtypeStruct(wkv.shape, F32),
                   jax.ShapeDtypeStruct((1, Q_LORA), F32), jax.ShapeDtypeStruct((1, KV_LORA), F32)],
        in_specs=[tok(1024), tok(1024), tok(512), tok(640), tok(2 * HEAD_LANES), _full(qg.shape), _full(kvg.shape),
                  _full(wq2.shape), _full(wkv.shape)],
        out_specs=[tok(A_GM), _full(wq2.shape), _full(wkv.shape), _full((1, Q_LORA)), _full((1, KV_LORA))],
        compiler_params=_params(1),
    )(dqf, dkf, dv, zqkv, rope, qg, kvg, wq2, wkv)


def _in_bwd_call(x, dx2, dz, dg, dqs, dkd, dvd, mod, b_ada, ng, wa, seq):
    n_tok = x.shape[0]
    tm = min(TOKEN_TILE, seq)
    per_seq = seq // tm
    n_seq = n_tok // seq

    def body(x_ref, dx2_ref, dz_ref, dg_ref, dqs_ref, dkd_ref, dvd_ref, mod_ref, bada_ref, ng_ref,
             wa_ref, gx_ref, gwa_ref, gng_ref, dshift_ref, dscale_ref):
        i = pl.program_id(0)

        @pl.when(i == 0)
        def _():
            gwa_ref[...] = jnp.zeros_like(gwa_ref)
            gng_ref[...] = jnp.zeros_like(gng_ref)

        @pl.when(i % per_seq == 0)
        def _():
            dshift_ref[...] = jnp.zeros_like(dshift_ref)
            dscale_ref[...] = jnp.zeros_like(dscale_ref)

        xv = x_ref[...]
        modv = mod_ref[0] + bada_ref[...]
        shift, scale = modv[:, :D_MODEL], modv[:, D_MODEL:2 * D_MODEL]
        ngv = ng_ref[...]
        r1 = lax.rsqrt(jnp.mean(xv * xv, axis=-1, keepdims=True) + EPS)
        xn = xv * r1
        hb = ((xn * ngv) * (1.0 + scale) + shift).astype(BF16)

        dgv = dg_ref[...]
        pieces = [(A_ZQ, dz_ref[...]), (A_GM, dgv[:, :512]), (A_QS, dqs_ref[...].astype(BF16)),
                  (A_KS, jnp.concatenate([_once(dkd_ref[...]) * LN2, _once(dvd_ref[...])], axis=1).astype(BF16)),
                  (A_GS, dgv[:, 512:])]
        dh = None
        for off, piece in pieces:
            wd = piece.shape[1]
            gwa_ref[:, off:off + wd] += _dot_tn(hb, piece)
            term = _dot_nt(piece, wa_ref[:, off:off + wd])
            dh = term if dh is None else dh + term

        dshift_ref[0] += jnp.sum(dh, axis=0, keepdims=True)
        dscale_ref[0] += jnp.sum(dh * (xn * ngv), axis=0, keepdims=True)
        gng_ref[...] += jnp.sum(dh * xn * (1.0 + scale), axis=0, keepdims=True)
        dxn = dh * ngv * (1.0 + scale)
        gx_ref[...] = dx2_ref[...] + r1 * (dxn - xn * jnp.mean(dxn * xn, axis=-1, keepdims=True))

    tok = lambda w: pl.BlockSpec((tm, w), lambda i: (i, 0))
    per_b = lambda w: pl.BlockSpec((1, 1, w), lambda i: (i // per_seq, 0, 0))
    return pl.pallas_call(
        body, name="in_bwd", grid=(n_tok // tm,),
        out_shape=[jax.ShapeDtypeStruct((n_tok, D_MODEL), F32), jax.ShapeDtypeStruct((D_MODEL, A_END), F32),
                   jax.ShapeDtypeStruct((1, D_MODEL), F32),
                   jax.ShapeDtypeStruct((n_seq, 1, D_MODEL), F32), jax.ShapeDtypeStruct((n_seq, 1, D_MODEL), F32)],
        in_specs=[tok(D_MODEL), tok(D_MODEL), tok(A_GM), tok(D_MODEL), tok(512), tok(256), tok(256),
                  per_b(3 * D_MODEL), _full(b_ada.shape), _full(ng.shape), _full(wa.shape)],
        out_specs=[tok(D_MODEL), _full((D_MODEL, A_END)), _full((1, D_MODEL)), per_b(D_MODEL), per_b(D_MODEL)],
        compiler_params=_params(1),
    )(x, dx2, dz, dg, dqs, dkd, dvd, mod, b_ada, ng, wa)


def _adam_math(w, g, m, v):
    m_new = ADAM_B1 * m + (1.0 - ADAM_B1) * g
    v_new = ADAM_B2 * v + (1.0 - ADAM_B2) * (g * g)
    m_hat = m_new / (1.0 - ADAM_B1 ** ADAM_STEP)
    v_hat = v_new / (1.0 - ADAM_B2 ** ADAM_STEP)
    delta = -ADAM_LR * (m_hat / (jnp.sqrt(v_hat) + ADAM_EPS) + ADAM_WD * w)
    return delta, m_new, v_new


def _adam_group_call(name, groups):
    n = len(groups)

    def body(*refs):
        ins, outs = refs[:4 * n], refs[4 * n:]
        for k in range(n):
            w_ref, g_ref, m_ref, v_ref = ins[4 * k:4 * k + 4]
            d, mn, vn = _adam_math(w_ref[...], g_ref[...], m_ref[...], v_ref[...])
            outs[3 * k][...] = d
            outs[3 * k + 1][...] = mn
            outs[3 * k + 2][...] = vn

    flat = [t for group in groups for t in group]
    shapes = [group[0].shape for group in groups for _ in range(3)]
    res = pl.pallas_call(
        body, name=name, grid=(1,),
        out_shape=[jax.ShapeDtypeStruct(s, F32) for s in shapes],
        in_specs=[_full(t.shape) for t in flat], out_specs=[_full(s) for s in shapes],
        compiler_params=_params(1),
    )(*flat)
    return [res[3 * k:3 * k + 3] for k in range(n)]


def _ada_bwd_call(act_all, dmod_cols, w, m, v):
    rows, cols = w.shape
    tr = rows

    def body(a_ref, dm_ref, w_ref, m_ref, v_ref, g_ref, d_ref, mo_ref, vo_ref):
        g = _dot_tn(a_ref[...].astype(BF16), dm_ref[...].astype(BF16))
        d, mn, vn = _adam_math(w_ref[...], g, m_ref[...], v_ref[...])
        g_ref[...] = g
        d_ref[...] = d
        mo_ref[...] = mn
        vo_ref[...] = vn

    spec = pl.BlockSpec((tr, cols), lambda i: (i, 0))
    nb = act_all.shape[0]
    return pl.pallas_call(
        body, name="ada_bwd", grid=(rows // tr,),
        out_shape=[jax.ShapeDtypeStruct(w.shape, F32)] * 4,
        in_specs=[pl.BlockSpec((nb, tr), lambda i: (0, i)), _full(dmod_cols.shape), spec, spec, spec],
        out_specs=[spec] * 4,
        compiler_params=_params(1),
    )(act_all, dmod_cols, w, m, v)


SMALL_ROW = {"norm_gain": (0, 1024), "final_gain": (1024, 2048), "q_norm_gain": (2048, 2432),
             "kv_norm_gain": (2432, 2688), "swa_sinks": (2688, 2696), "loss": (2816, 2944)}
SMALL_ORDER = ("b_ada", "norm_gain", "q_norm_gain", "kv_norm_gain", "swa_sinks", "final_gain")


def _small_call(parts_all, n_seq, params):
    k = len(params)

    def body(p_ref, *refs):
        ins, outs, loss_ref = refs[:3 * k], refs[3 * k:7 * k], refs[7 * k]
        row = p_ref[n_seq:n_seq + 1, :]
        for dv in range(1, 8):
            r0 = dv * ROWS_PER_DEVICE + n_seq
            row = row + p_ref[r0:r0 + 1, :]
        gb = None
        for dv in range(8):
            for r in range(n_seq):
                r0 = dv * ROWS_PER_DEVICE + r
                gb = p_ref[r0:r0 + 1, :] if gb is None else gb + p_ref[r0:r0 + 1, :]
        for j, name in enumerate(SMALL_ORDER):
            g = gb if name == "b_ada" else row[:, SMALL_ROW[name][0]:SMALL_ROW[name][1]]
            d, mn, vn = _adam_math(ins[3 * j][...], g, ins[3 * j + 1][...], ins[3 * j + 2][...])
            outs[4 * j][...] = g
            outs[4 * j + 1][...] = d
            outs[4 * j + 2][...] = mn
            outs[4 * j + 3][...] = vn
        loss_ref[...] = row[:, SMALL_ROW["loss"][0]:SMALL_ROW["loss"][1]]

    flat = [t for p in params for t in p]
    res = pl.pallas_call(
        body, name="small_update", grid=(1,),
        out_shape=[jax.ShapeDtypeStruct(p[0].shape, F32) for p in params for _ in range(4)]
        + [jax.ShapeDtypeStruct((1, HEAD_LANES), F32)],
        in_specs=[_full(parts_all.shape)] + [_full(t.shape) for t in flat],
        out_specs=[_full(p[0].shape) for p in params for _ in range(4)] + [_full((1, HEAD_LANES))],
        compiler_params=_params(1),
    )(parts_all, *flat)
    return [res[4 * j:4 * j + 4] for j in range(k)], res[4 * k]


def _rot(t):
    half = t.shape[-1] // 2
    return jnp.concatenate([-t[..., half:], t[..., :half]], axis=-1)


def _rot_t(g):
    half = g.shape[-1] // 2
    return jnp.concatenate([g[..., half:], -g[..., :half]], axis=-1)


def _columns(segments, lo, hi):
    out, at = [], 0
    for seg in segments:
        n = seg.shape[1]
        a, b = max(lo, at), min(hi, at + n)
        if a < b:
            out.append(seg[:, a - at:b - at])
        at += n
    return out


def _prepare_in(w_in_blocks):
    o = [0]
    for s in IN_SPLITS:
        o.append(o[-1] + s)
    part = lambda a, b: _columns(w_in_blocks, a, b)
    kr = jnp.concatenate(part(o[2], o[3]), axis=1)
    zero = jnp.zeros((kr.shape[0], 32), kr.dtype)
    return jnp.concatenate(part(0, o[2]) + [_rot(kr), zero, kr, zero] + part(o[3], o[8]), axis=1)


def _prepare_up(w_uq, w_ukv):
    uq = w_uq.reshape(Q_LORA, N_HEADS, MLA_NOPE + MLA_ROPE)
    zq = jnp.zeros((Q_LORA, N_HEADS, 32), w_uq.dtype)
    uq_full = jnp.concatenate([uq, zq], axis=-1).reshape(Q_LORA, 1024)
    uq_rot = jnp.concatenate([jnp.zeros((Q_LORA, N_HEADS, 64), w_uq.dtype), _rot(uq[..., MLA_NOPE:]), zq],
                             axis=-1).reshape(Q_LORA, 1024)
    wq2 = jnp.concatenate([uq_full, uq_rot], axis=1)
    ukv = w_ukv.reshape(KV_LORA, N_HEADS, 128)
    k_full = jnp.concatenate([ukv[..., :64], jnp.zeros((KV_LORA, N_HEADS, 64), w_ukv.dtype)], axis=-1).reshape(KV_LORA, 1024)
    wkv = jnp.concatenate([k_full, ukv[..., 64:].reshape(KV_LORA, 512)], axis=1)
    return wq2, wkv


def _restore_in(gwa):
    gkr = gwa[:, A_KR + 64:A_KR + 96] + _rot_t(gwa[:, A_KR:A_KR + 32])
    in_order = [gwa[:, :A_KR], gkr, gwa[:, A_GM:]]
    n = D_IN // 4
    return [jnp.concatenate(_columns(in_order, k * n, (k + 1) * n), axis=1) for k in range(4)]


def _restore_up(gwq2, gwkv):
    gf = gwq2[:, :1024].reshape(Q_LORA, N_HEADS, 128)
    gr = gwq2[:, 1024:].reshape(Q_LORA, N_HEADS, 128)
    g_uq = jnp.concatenate([gf[..., :64], gf[..., 64:96] + _rot_t(gr[..., 64:96])], axis=-1).reshape(Q_LORA, 768)
    gk = gwkv[:, :1024].reshape(KV_LORA, N_HEADS, 128)[..., :64]
    gv = gwkv[:, 1024:].reshape(KV_LORA, N_HEADS, 64)
    g_ukv = jnp.concatenate([gk, gv], axis=-1).reshape(KV_LORA, 1024)
    return g_uq, g_ukv


def _local_step(x, positions, target, mod_rows, b_ada, ng, qg, kvg, sinks, fg, w_in_b, later_shards):
    n_seq, seq, _ = x.shape
    n_tok = n_seq * seq
    x2d = x.reshape(n_tok, D_MODEL)
    t2d = target.reshape(n_tok, D_MODEL)
    pos_f = positions.astype(F32)
    pos_col = pos_f.reshape(n_tok, 1)
    pos_row = pos_f.reshape(n_tok // SWA_WINDOW, 1, SWA_WINDOW)
    mod3 = mod_rows.reshape(n_seq, 1, 3 * D_MODEL)
    inv = ROPE_THETA ** (-jnp.arange(0, MLA_ROPE, 2, dtype=F32) / MLA_ROPE)
    inv128 = jnp.tile(jnp.concatenate([inv, inv]), 4).reshape(1, HEAD_LANES)
    fg2 = fg.reshape(1, D_MODEL)

    wa = _prepare_in(w_in_b)
    zqkv, zkr, gates, qs, kd, vd, rope, f_uq, f_ukv, f_out = _pre_call(x2d, pos_col, mod3, b_ada, ng, inv128, wa,
                                                                       later_shards, seq)
    cols = lambda t, r: jnp.transpose(t.reshape(4, r, -1), (1, 0, 2)).reshape(r, -1)
    wq2, wkv = _prepare_up(cols(f_uq, Q_LORA), cols(f_ukv, KV_LORA))
    w_out_b = f_out.reshape(D_MODEL, D_MODEL)
    qf, kf, v = _up_call(zqkv, zkr, rope, qg, kvg, wq2, wkv, seq)
    o_mla, lse_mla = _mla_fwd_call(qf, kf, v, n_seq, seq)
    o_swa, lse_swa = _swa_fwd_call(qs, kd, vd, pos_col, pos_row, sinks, n_seq, seq)
    dx2, do, dg, g_out, g_fg, dgate, loss, delta_mla, delta_swa = _post_call(x2d, t2d, o_mla, o_swa, gates, mod3, b_ada, fg2, w_out_b, seq)
    dqf, dkf, dv = _mla_bwd_call(qf, kf, v, do, delta_mla, lse_mla, n_seq, seq)
    dqs, dkd, dvd, dsink, r_out = _swa_bwd_call(qs, kd, vd, do, delta_swa, lse_swa, pos_col, pos_row, sinks,
                                                g_out.reshape(4, 2, D_MODEL // 8, D_MODEL), n_seq, seq)
    dz, g_wq2, g_wkv, g_qg, g_kvg = _mid_bwd_call(dqf, dkf, dv, zqkv, rope, qg, kvg, wq2, wkv, seq)
    gx, g_wa, g_ng, dshift, dscale = _in_bwd_call(x2d, dx2, dz, dg, dqs, dkd, dvd, mod3, b_ada, ng, wa, seq)
    g_in = _restore_in(g_wa)
    g_uq, g_ukv = _restore_up(g_wq2, g_wkv)
    dmod = jnp.concatenate([dshift, dscale, dgate], axis=-1).reshape(n_seq, 3 * D_MODEL)
    small_row = jnp.concatenate([g_ng, g_fg, g_qg, g_kvg, jnp.pad(jnp.sum(dsink, axis=1).reshape(1, N_HEADS), ((0, 0), (0, 120))),
                                 loss, jnp.zeros((1, 128), F32)], axis=1)
    return gx.reshape(x.shape), (g_in, g_uq, g_ukv), r_out, small_row, dmod


def kernel(x, c, positions, w_ada, b_ada, norm_gain, w_in, q_norm_gain, kv_norm_gain, w_uq, w_ukv, swa_sinks, w_out, final_gain, loss_target, m_w_ada, m_b_ada, m_norm_gain, m_w_in, m_q_norm_gain, m_kv_norm_gain, m_w_uq, m_w_ukv, m_swa_sinks, m_w_out, m_final_gain, v_w_ada, v_b_ada, v_norm_gain, v_w_in, v_q_norm_gain, v_kv_norm_gain, v_w_uq, v_w_ukv, v_swa_sinks, v_w_out, v_final_gain):
    n_seq = x.shape[0]
    xi, yi, ci = lax.axis_index("x"), lax.axis_index("y"), lax.axis_index("c")
    dev = 4 * xi + 2 * yi + ci
    chip = 2 * xi + yi

    halves = lambda w: w.astype(BF16).reshape(2, w.shape[0] // 2, w.shape[1])
    c_blk = jnp.pad(c, ((0, ROWS_PER_DEVICE - n_seq), (0, 0)))
    act_all, pieces, f_in = _comm_fwd_call(c_blk, w_ada[0], [halves(w_in[0])])
    mine = lax.dynamic_slice_in_dim(pieces, dev * ROWS_PER_DEVICE, n_seq, axis=1)
    mod_rows = jnp.transpose(mine, (1, 0, 2)).reshape(n_seq, 3 * D_MODEL)
    w_in_blocks = [f_in[k].reshape(D_MODEL, -1) for k in range(4)]

    gx, (g_in_blocks, g_uq, g_ukv), r_out, small_row, dmod = _local_step(
        x, positions, loss_target, mod_rows, b_ada, norm_gain, q_norm_gain, kv_norm_gain, swa_sinks, final_gain,
        w_in_blocks, [halves(w_uq[0]), halves(w_ukv[0]), halves(w_out[0])])

    grads = [jnp.stack(g_in_blocks).reshape(4, 2, D_MODEL // 2, -1), _by_owner(g_uq, g_uq.shape[1] // 4),
             _by_owner(g_ukv, g_ukv.shape[1] // 4)]
    part = jnp.concatenate([dmod, small_row, jnp.zeros((ROWS_PER_DEVICE - n_seq - 1, 3 * D_MODEL), F32)], axis=0)
    r_in, r_uq, r_ukv, parts_all = _comm_bwd_call(grads, part)
    g_in_s, g_uq_s = r_in.reshape(w_in.shape[1:]), r_uq.reshape(w_uq.shape[1:])
    g_ukv_s, g_out_s = r_ukv.reshape(w_ukv.shape[1:]), r_out.reshape(w_out.shape[1:])

    tr = lambda a: jnp.swapaxes(a[0], 0, 1)
    back = lambda ts: [jnp.swapaxes(t, 0, 1) for t in ts]
    in_t, uq_t, (d_ukv, nm_ukv, nv_ukv), (d_out, nm_out, nv_out) = _adam_group_call(
        "adam_weights", [(tr(w_in), g_in_s.T, tr(m_w_in), tr(v_w_in)), (tr(w_uq), g_uq_s.T, tr(m_w_uq), tr(v_w_uq)),
                         (w_ukv[0], g_ukv_s, m_w_ukv[0], v_w_ukv[0]), (w_out[0], g_out_s, m_w_out[0], v_w_out[0])])
    d_in, nm_in, nv_in = back(in_t)
    d_uq, nm_uq, nv_uq = back(uq_t)
    dmod_cols = lax.dynamic_slice_in_dim(parts_all, chip * 768, 768, axis=1)
    g_ada, d_ada, nm_ada, nv_ada = _ada_bwd_call(act_all, dmod_cols, w_ada[0], m_w_ada[0], v_w_ada[0])

    row = lambda t: t.reshape(1, -1)
    small = {"b_ada": (b_ada, m_b_ada, v_b_ada), "norm_gain": (norm_gain, m_norm_gain, v_norm_gain),
             "q_norm_gain": (q_norm_gain, m_q_norm_gain, v_q_norm_gain),
             "kv_norm_gain": (kv_norm_gain, m_kv_norm_gain, v_kv_norm_gain),
             "swa_sinks": (swa_sinks, m_swa_sinks, v_swa_sinks),
             "final_gain": (row(final_gain), row(m_final_gain), row(v_final_gain))}
    res, loss_row = _small_call(parts_all, n_seq, [small[name] for name in SMALL_ORDER])
    res = dict(zip(SMALL_ORDER, res))
    res["final_gain"] = [t.reshape(-1) for t in res["final_gain"]]
    e = lambda t: t[None]
    big = {"w_ada": (e(g_ada), e(d_ada), e(nm_ada), e(nv_ada)), "w_in": (e(g_in_s), e(d_in), e(nm_in), e(nv_in)),
           "w_uq": (e(g_uq_s), e(d_uq), e(nm_uq), e(nv_uq)), "w_ukv": (e(g_ukv_s), e(d_ukv), e(nm_ukv), e(nv_ukv)),
           "w_out": (e(g_out_s), e(d_out), e(nm_out), e(nv_out))}
    order = ("w_ada", "b_ada", "norm_gain", "w_in", "q_norm_gain", "kv_norm_gain", "w_uq", "w_ukv", "swa_sinks", "w_out",
             "final_gain")
    pick = lambda kind: [(big[n] if n in big else res[n])[kind] for n in order]
    return (loss_row[0, 0], gx, *pick(0), *pick(1), *pick(2), *pick(3))
```
